```python
import jax, jax.numpy as jnp
from jax import lax
import numpy as np

D_MODEL = 1024
BATCH = 8
SEQ = 8192
DEPTH = 1

HEAD_DIM = 64
ROT_DIM = HEAD_DIM // 4
ROPE_THETA = 500000.0
BLOCK = 128
N_MEM = 256
D_MIX = D_MODEL
C_HEADS = 4
C_W = C_HEADS * HEAD_DIM
A_Q_HEADS = (D_MIX - C_W) // (2 * HEAD_DIM)
A_KV_HEADS = 2
A_GROUP = A_Q_HEADS // A_KV_HEADS
A_W = A_Q_HEADS * HEAD_DIM
A_KV_W = A_KV_HEADS * HEAD_DIM
A_WINDOW = 128
B_HEADS = A_Q_HEADS
B_W = B_HEADS * HEAD_DIM
B_CONFIGS = ((128, 1), (512, 4), (2048, 16))
RMS_EPS = 1e-6
IN_WIDTHS = (A_W, A_KV_W, A_KV_W, A_W,
             B_W, B_W, B_W, B_W,
             C_W, C_W)
D_IN = sum(IN_WIDTHS)
IN_SPLITS = tuple(int(s) for s in np.cumsum(IN_WIDTHS)[:-1])

kernel_name = "hybrid_swa_sink_dilated_memxattn_layer"


def rms_norm(x, g):
    xf = x.astype(jnp.float32)
    y = xf * lax.rsqrt(jnp.mean(xf * xf, axis=-1, keepdims=True) + RMS_EPS)
    return (y * g.astype(jnp.float32)).astype(x.dtype)


def rope_tables(seq):
    inv_freq = ROPE_THETA ** (-jnp.arange(0, ROT_DIM, 2, dtype=jnp.float32) / ROT_DIM)
    ang = jnp.arange(seq, dtype=jnp.float32)[:, None] * inv_freq[None, :]
    return jnp.cos(ang)[:, None, :], jnp.sin(ang)[:, None, :]


def apply_partial_rope(t, cos, sin):
    tf = t.astype(jnp.float32)
    half = ROT_DIM // 2
    r1, r2, rest = tf[..., :half], tf[..., half:ROT_DIM], tf[..., ROT_DIM:]
    out = jnp.concatenate([r1 * cos - r2 * sin, r2 * cos + r1 * sin, rest], axis=-1)
    return out.astype(t.dtype)


def banded_attention(q, k, v, max_dist, sink):
    N, L, KVH, G, Dh = q.shape
    nb = -(-L // BLOCK)
    Lp = nb * BLOCK
    pad = Lp - L
    n_prev = -(-max_dist // BLOCK)
    W = (n_prev + 1) * BLOCK
    qp = jnp.pad(q, ((0, 0), (0, pad), (0, 0), (0, 0), (0, 0)))
    kp = jnp.pad(k, ((0, 0), (n_prev * BLOCK, pad), (0, 0), (0, 0)))
    vp = jnp.pad(v, ((0, 0), (n_prev * BLOCK, pad), (0, 0), (0, 0)))

    def windows(t):
        return jnp.concatenate(
            [t[:, j * BLOCK:(j + nb) * BLOCK].reshape(N, nb, BLOCK, KVH, Dh)
             for j in range(n_prev + 1)], axis=2)

    kw, vw = windows(kp), windows(vp)
    qb = qp.reshape(N, nb, BLOCK, KVH, G, Dh)
    s = jnp.einsum('nbqhgd,nbkhd->nbhgqk', qb, kw,
                   preferred_element_type=jnp.float32)
    qi = jnp.arange(BLOCK)[:, None]
    kj = jnp.arange(W)[None, :]
    dist = qi + n_prev * BLOCK - kj
    kpos = jnp.arange(nb)[:, None, None] * BLOCK + kj[None] - n_prev * BLOCK
    mask = (dist >= 0)[None] & (dist <= max_dist)[None] & (kpos >= 0)
    s = jnp.where(mask[:, None, None], s, -jnp.inf)
    m = jnp.max(s, axis=-1)
    if sink is not None:
        sink_b = sink.astype(jnp.float32)[None, None, :, :, None]
        m = jnp.maximum(m, sink_b)
    p = jnp.exp(s - m[..., None])
    denom = jnp.sum(p, axis=-1)
    if sink is not None:
        denom = denom + jnp.exp(sink_b - m)
    o = jnp.einsum('nbhgqk,nbkhd->nbqhgd', p, vw.astype(jnp.float32))
    o = o / jnp.moveaxis(denom, -1, 2)[..., None]
    lse = jnp.moveaxis(m + jnp.log(denom), -1, 2)
    o = o.reshape(N, Lp, KVH, G, Dh)[:, :L]
    lse = lse.reshape(N, Lp, KVH, G)[:, :L]
    return o.astype(q.dtype), lse


def to_strided(t, d):
    B, S = t.shape[:2]
    rest = t.shape[2:]
    t = t.reshape((B, S // d, d) + rest)
    t = jnp.moveaxis(t, 2, 1)
    return t.reshape((B * d, S // d) + rest)


def from_strided(t, B, d):
    L = t.shape[1]
    rest = t.shape[2:]
    t = t.reshape((B, d, L) + rest)
    t = jnp.moveaxis(t, 1, 2)
    return t.reshape((B, d * L) + rest)


def _fwd_setup_inputs(seed: int = 0) -> dict:
    key = jax.random.key(seed)
    ks = jax.random.split(key, 10)
    f32 = jnp.float32
    x = jax.random.normal(ks[0], (BATCH, SEQ, D_MODEL), f32)
    mem = jax.random.normal(ks[1], (BATCH, N_MEM, D_MODEL), f32)
    pre_norm = 1.0 + 0.02 * jax.random.normal(ks[2], (DEPTH, D_MODEL), f32)
    w_in = jax.random.normal(ks[3], (DEPTH, D_MODEL, D_IN), f32) * D_MODEL ** -0.5
    sink_a = 0.5 * jax.random.normal(ks[4], (DEPTH, A_Q_HEADS), f32)
    mem_norm = 1.0 + 0.02 * jax.random.normal(ks[5], (DEPTH, D_MODEL), f32)
    w_mem_kv = jax.random.normal(ks[6], (DEPTH, D_MODEL, 2 * C_W), f32) * D_MODEL ** -0.5
    w_out = jax.random.normal(ks[7], (DEPTH, D_MIX, D_MODEL), f32) * D_MIX ** -0.5
    post_norm = 1.0 + 0.02 * jax.random.normal(ks[8], (DEPTH, D_MODEL), f32)
    return {"x": x, "mem": mem, "pre_norm": pre_norm, "w_in": w_in,
            "sink_a": sink_a, "mem_norm": mem_norm, "w_mem_kv": w_mem_kv,
            "w_out": w_out, "post_norm": post_norm}


def _fwd_reference(x, mem, pre_norm, w_in, sink_a, mem_norm, w_mem_kv, w_out, post_norm):
    B, S, _ = x.shape
    scale = HEAD_DIM ** -0.5
    cos, sin = rope_tables(S)
    h = x
    for l in range(DEPTH):
        u = rms_norm(h, pre_norm[l])
        proj = jnp.einsum('bsd,de->bse', u, w_in[l])
        qa, ka, va, ga, qb, kb, vb, gb, qc, gc = jnp.split(proj, IN_SPLITS, axis=-1)

        qa = apply_partial_rope(qa.reshape(B, S, A_Q_HEADS, HEAD_DIM), cos, sin)
        qa = qa.reshape(B, S, A_KV_HEADS, A_GROUP, HEAD_DIM)
        ka = apply_partial_rope(ka.reshape(B, S, A_KV_HEADS, HEAD_DIM), cos, sin)
        va = va.reshape(B, S, A_KV_HEADS, HEAD_DIM)
        sink = sink_a[l].reshape(A_KV_HEADS, A_GROUP)
        oa, _ = banded_attention(qa * scale, ka, va, A_WINDOW - 1, sink)
        oa = oa.reshape(B, S, A_W)

        qb = apply_partial_rope(qb.reshape(B, S, B_HEADS, HEAD_DIM), cos, sin)
        kb = apply_partial_rope(kb.reshape(B, S, B_HEADS, HEAD_DIM), cos, sin)
        vb = vb.reshape(B, S, B_HEADS, HEAD_DIM)
        outs, lses = [], []
        for (win, dil) in B_CONFIGS:
            qs = to_strided(qb * scale, dil)[:, :, :, None, :]
            o, lse = banded_attention(qs, to_strided(kb, dil), to_strided(vb, dil),
                                      win // dil, None)
            outs.append(from_strided(o[:, :, :, 0], B, dil).astype(jnp.float32))
            lses.append(from_strided(lse[..., 0], B, dil))
        wts = jax.nn.softmax(jnp.stack(lses, axis=0), axis=0)
        ob = jnp.sum(wts[..., None] * jnp.stack(outs, axis=0), axis=0)
        ob = ob.astype(x.dtype).reshape(B, S, B_W)

        mkv = jnp.einsum('bmd,de->bme', rms_norm(mem, mem_norm[l]), w_mem_kv[l])
        mk, mv = jnp.split(mkv, 2, axis=-1)
        mk = mk.reshape(B, N_MEM, C_HEADS, HEAD_DIM)
        mv = mv.reshape(B, N_MEM, C_HEADS, HEAD_DIM)
        qc = qc.reshape(B, S, C_HEADS, HEAD_DIM)
        sc = jnp.einsum('bshd,bmhd->bhsm', qc * scale, mk,
                        preferred_element_type=jnp.float32)
        pc = jax.nn.softmax(sc, axis=-1)
        oc = jnp.einsum('bhsm,bmhd->bshd', pc, mv.astype(jnp.float32))
        oc = oc.astype(x.dtype).reshape(B, S, C_W)

        y = jnp.concatenate([oa * jax.nn.silu(ga), ob * jax.nn.silu(gb),
                             oc * jax.nn.silu(gc)], axis=-1)
        y = jnp.einsum('bse,ed->bsd', y, w_out[l])
        h = h + rms_norm(y, post_norm[l])
    return h


import jax as _jax
import jax.numpy as _jnp

TWIN_FORMAT = 'train_step'
FWD_PARAMS = ['x', 'mem', 'pre_norm', 'w_in', 'sink_a', 'mem_norm', 'w_mem_kv', 'w_out', 'post_norm']
TWIN_WEIGHTS = ['pre_norm', 'w_in', 'sink_a', 'mem_norm', 'w_mem_kv', 'w_out', 'post_norm']
TWIN_DIFF_INPUT = 'x'
TWIN_INPUTS = ['x', 'mem', 'pre_norm', 'w_in', 'sink_a', 'mem_norm', 'w_mem_kv', 'w_out', 'post_norm', 'loss_target', 'm_pre_norm', 'm_w_in', 'm_sink_a', 'm_mem_norm', 'm_w_mem_kv', 'm_w_out', 'm_post_norm', 'v_pre_norm', 'v_w_in', 'v_sink_a', 'v_mem_norm', 'v_w_mem_kv', 'v_w_out', 'v_post_norm']
TWIN_OUTPUTS = ['loss', 'grad_x', 'grad_pre_norm', 'grad_w_in', 'grad_sink_a', 'grad_mem_norm', 'grad_w_mem_kv', 'grad_w_out', 'grad_post_norm', 'delta_pre_norm', 'delta_w_in', 'delta_sink_a', 'delta_mem_norm', 'delta_w_mem_kv', 'delta_w_out', 'delta_post_norm', 'new_m_pre_norm', 'new_m_w_in', 'new_m_sink_a', 'new_m_mem_norm', 'new_m_w_mem_kv', 'new_m_w_out', 'new_m_post_norm', 'new_v_pre_norm', 'new_v_w_in', 'new_v_sink_a', 'new_v_mem_norm', 'new_v_w_mem_kv', 'new_v_w_out', 'new_v_post_norm']
TWIN_LEAF_KINDS = {'loss': 'loss', 'grad_x': 'grad_x', 'grad_pre_norm': 'grad_w', 'grad_w_in': 'grad_w', 'grad_sink_a': 'grad_w', 'grad_mem_norm': 'grad_w', 'grad_w_mem_kv': 'grad_w', 'grad_w_out': 'grad_w', 'grad_post_norm': 'grad_w', 'delta_pre_norm': 'delta_w', 'delta_w_in': 'delta_w', 'delta_sink_a': 'delta_w', 'delta_mem_norm': 'delta_w', 'delta_w_mem_kv': 'delta_w', 'delta_w_out': 'delta_w', 'delta_post_norm': 'delta_w', 'new_m_pre_norm': 'new_m', 'new_m_w_in': 'new_m', 'new_m_sink_a': 'new_m', 'new_m_mem_norm': 'new_m', 'new_m_w_mem_kv': 'new_m', 'new_m_w_out': 'new_m', 'new_m_post_norm': 'new_m', 'new_v_pre_norm': 'new_v', 'new_v_w_in': 'new_v', 'new_v_sink_a': 'new_v', 'new_v_mem_norm': 'new_v', 'new_v_w_mem_kv': 'new_v', 'new_v_w_out': 'new_v', 'new_v_post_norm': 'new_v'}


def _forward(args):
    return _fwd_reference(*[args[k] for k in FWD_PARAMS])


def _output_shape():
    def fwd():
        inp = _fwd_setup_inputs(0)
        return _fwd_reference(*[inp[k] for k in FWD_PARAMS])
    out = _jax.eval_shape(fwd)
    return out.shape, out.dtype

N_MICROBATCH = 1
ADAM_LR = 0.001
ADAM_B1 = 0.9
ADAM_B2 = 0.999
ADAM_EPS = 1e-08
ADAM_WD = 0.01
ADAM_STEP = 10
PER_EXAMPLE_BATCH_AXIS = {'x': 0, 'mem': 0, 'loss_target': 0}
SHARED_INPUTS = []
_WEIGHT_DTYPES = {'pre_norm': _jnp.float32, 'w_in': _jnp.float32, 'sink_a': _jnp.float32, 'mem_norm': _jnp.float32, 'w_mem_kv': _jnp.float32, 'w_out': _jnp.float32, 'post_norm': _jnp.float32}
MOMENT_SCALE = {'pre_norm': 6.721057e-01, 'w_in': 4.040869e-01, 'sink_a': 5.639471e-02, 'mem_norm': 2.796478e-01, 'w_mem_kv': 4.039070e-01, 'w_out': 3.788500e-01, 'post_norm': 6.406205e+01}


def _to_microbatches(a, axis):
    t = _jnp.moveaxis(a, axis, 0)
    t = t.reshape((N_MICROBATCH, t.shape[0] // N_MICROBATCH) + t.shape[1:])
    return _jnp.moveaxis(t, 1, axis + 1)


def setup_inputs(seed: int = 0) -> dict:
    inp = _fwd_setup_inputs(seed)
    key = _jax.random.fold_in(_jax.random.key(seed), 7919)
    shape, _ = _output_shape()
    out = dict(inp)
    out["loss_target"] = _jax.random.normal(_jax.random.fold_in(key, 0), shape, _jnp.float32)
    for i, name in enumerate(TWIN_WEIGHTS):
        w = inp[name].astype(_jnp.float32)
        if MOMENT_SCALE is None:
            s = _jnp.sqrt(_jnp.mean(_jnp.square(w)) + 1e-30)
        else:
            s = MOMENT_SCALE[name]
        km, kv = _jax.random.split(_jax.random.fold_in(key, i + 1))
        out[name] = w
        out["m_" + name] = s * _jax.random.normal(km, w.shape, _jnp.float32)
        out["v_" + name] = (s * s) * _jax.random.uniform(kv, w.shape, _jnp.float32, 0.5, 1.5)
    if N_MICROBATCH > 1:
        for name, axis in PER_EXAMPLE_BATCH_AXIS.items():
            out[name] = _to_microbatches(out[name], axis)
    return {'x': out['x'], 'mem': out['mem'], 'pre_norm': out['pre_norm'], 'w_in': out['w_in'], 'sink_a': out['sink_a'], 'mem_norm': out['mem_norm'], 'w_mem_kv': out['w_mem_kv'], 'w_out': out['w_out'], 'post_norm': out['post_norm'], 'loss_target': out['loss_target'], 'm_pre_norm': out['m_pre_norm'], 'm_w_in': out['m_w_in'], 'm_sink_a': out['m_sink_a'], 'm_mem_norm': out['m_mem_norm'], 'm_w_mem_kv': out['m_w_mem_kv'], 'm_w_out': out['m_w_out'], 'm_post_norm': out['m_post_norm'], 'v_pre_norm': out['v_pre_norm'], 'v_w_in': out['v_w_in'], 'v_sink_a': out['v_sink_a'], 'v_mem_norm': out['v_mem_norm'], 'v_w_mem_kv': out['v_w_mem_kv'], 'v_w_out': out['v_w_out'], 'v_post_norm': out['v_post_norm']}


def _loss(weights, diff, rest, loss_target):
    with _jax.named_scope("forward"):
        args = {**rest, TWIN_DIFF_INPUT: diff, **{k: w.astype(_WEIGHT_DTYPES[k]) for k, w in weights.items()}}
        y = _forward(args)
    with _jax.named_scope("loss_head"):
        err = _jnp.square(y.astype(_jnp.float32) - loss_target)
        return 0.5 * _jnp.sum(_jnp.mean(err, axis=-1)) if err.ndim else 0.5 * err


def _adamw(w, g, m, v):
    m = ADAM_B1 * m + (1.0 - ADAM_B1) * g
    v = ADAM_B2 * v + (1.0 - ADAM_B2) * _jnp.square(g)
    m_hat = m / (1.0 - ADAM_B1 ** ADAM_STEP)
    v_hat = v / (1.0 - ADAM_B2 ** ADAM_STEP)
    delta = -ADAM_LR * (m_hat / (_jnp.sqrt(v_hat) + ADAM_EPS) + ADAM_WD * w)
    return delta, m, v


def reference(x, mem, pre_norm, w_in, sink_a, mem_norm, w_mem_kv, w_out, post_norm, loss_target, m_pre_norm, m_w_in, m_sink_a, m_mem_norm, m_w_mem_kv, m_w_out, m_post_norm, v_pre_norm, v_w_in, v_sink_a, v_mem_norm, v_w_mem_kv, v_w_out, v_post_norm):
    given = dict(x=x, mem=mem, pre_norm=pre_norm, w_in=w_in, sink_a=sink_a, mem_norm=mem_norm, w_mem_kv=w_mem_kv, w_out=w_out, post_norm=post_norm, loss_target=loss_target, m_pre_norm=m_pre_norm, m_w_in=m_w_in, m_sink_a=m_sink_a, m_mem_norm=m_mem_norm, m_w_mem_kv=m_w_mem_kv, m_w_out=m_w_out, m_post_norm=m_post_norm, v_pre_norm=v_pre_norm, v_w_in=v_w_in, v_sink_a=v_sink_a, v_mem_norm=v_mem_norm, v_w_mem_kv=v_w_mem_kv, v_w_out=v_w_out, v_post_norm=v_post_norm)
    weights = {n: given[n] for n in TWIN_WEIGHTS}
    shared = {n: given[n] for n in SHARED_INPUTS}
    per_example = {n: given[n] for n in ['x', 'mem']}
    grad_fn = _jax.value_and_grad(_loss, argnums=(0, 1))

    def one_microbatch(ex, loss_target):
        ex = dict(ex)
        diff = ex.pop(TWIN_DIFF_INPUT)
        return grad_fn(weights, diff, {**shared, **ex}, loss_target)

    if N_MICROBATCH == 1:
        loss, (grad_w, grad_x) = one_microbatch(per_example, given["loss_target"])
    else:
        def body(carry, xs):
            loss_sum, grad_sum = carry
            l_k, (gw_k, gx_k) = one_microbatch(xs[0], xs[1])
            with _jax.named_scope("update"):
                return (loss_sum + l_k, _jax.tree.map(_jnp.add, grad_sum, gw_k)), gx_k

        init = (_jnp.zeros((), _jnp.float32), _jax.tree.map(_jnp.zeros_like, weights))
        (loss, grad_w), grad_x = _jax.lax.scan(body, init, (per_example, given["loss_target"]))
    with _jax.named_scope("update"):
        delta_w, new_m, new_v = {}, {}, {}
        for n in TWIN_WEIGHTS:
            delta_w[n], new_m[n], new_v[n] = _adamw(weights[n], grad_w[n], given["m_" + n], given["v_" + n])
    return (loss, grad_x, *[grad_w[n] for n in TWIN_WEIGHTS], *[delta_w[n] for n in TWIN_WEIGHTS],
            *[new_m[n] for n in TWIN_WEIGHTS], *[new_v[n] for n in TWIN_WEIGHTS])
```

```python
import functools

import numpy as np
import jax
import jax.numpy as jnp
from jax import lax
from jax.experimental import pallas as pl
from jax.experimental.pallas import tpu as pltpu

F32 = jnp.float32
BF16 = jnp.bfloat16

D_MODEL = 1024
HEAD_DIM = 64
LANES = 128
BLOCK = 128
A_W, A_KV_W, B_W, C_W = 384, 128, 384, 256
N_MEM = 256
D_IN = 3072
N_CHIPS = 4
SHARD_IN = D_IN // N_CHIPS
B_CONFIGS = ((128, 1), (512, 4), (2048, 16))
A_WINDOW = 128
RMS_EPS = 1e-6
ROPE_THETA = 500000.0
SCALE = HEAD_DIM ** -0.5
NEG = -1e30
ADAM_LR, ADAM_B1, ADAM_B2, ADAM_EPS, ADAM_WD, ADAM_STEP = 0.001, 0.9, 0.999, 1e-08, 0.01, 10

NT = (((1,), (1,)), ((), ()))
TN = (((0,), (0,)), ((), ()))
MESH = pl.DeviceIdType.MESH

_PROJ_LAYOUT = (
    [("qa", 128 * i, True, True) for i in range(3)] + [("ka", 0, True, False), ("va", 0, False, False)]
    + [("ga", 128 * i, False, False) for i in range(3)]
    + [("qb", 128 * i, True, True) for i in range(3)] + [("kb", 128 * i, True, False) for i in range(3)]
    + [("vb", 128 * i, False, False) for i in range(3)] + [("gb", 128 * i, False, False) for i in range(3)]
    + [("qc", 128 * i, False, True) for i in range(2)] + [("gc", 128 * i, False, False) for i in range(2)]
)
_PROJ_NAMES = ("qa", "ka", "va", "ga", "qb", "kb", "vb", "gb", "qc", "gc")
_PROJ_WIDTH = dict(qa=A_W, ka=A_KV_W, va=A_KV_W, ga=A_W, qb=B_W, kb=B_W, vb=B_W, gb=B_W, qc=C_W, gc=C_W)
_PROJ_DTYPE = dict(qa=BF16, ka=BF16, va=BF16, ga=F32, qb=BF16, kb=BF16, vb=BF16, gb=F32, qc=BF16, gc=F32)


def _dot(a, b):
    return jnp.dot(a, b, preferred_element_type=F32)


def _dot_nt(a, b):
    return lax.dot_general(a, b, NT, preferred_element_type=F32)


def _dot_tn(a, b):
    return lax.dot_general(a, b, TN, preferred_element_type=F32)


def _half_masks(rows):
    lane = lax.broadcasted_iota(jnp.int32, (rows, LANES), 1)
    return lane < HEAD_DIM, lane >= HEAD_DIM


def _rope(t, c, sm, sp):
    return t * c + pltpu.roll(t, LANES - 8, 1) * sm + pltpu.roll(t, 8, 1) * sp


def _rope_tables(seq):
    inv_freq = ROPE_THETA ** (-jnp.arange(0, 16, 2, dtype=F32) / 16)
    ang = jnp.arange(seq, dtype=F32)[:, None] * inv_freq[None, :]
    cos, sin = jnp.cos(ang), jnp.sin(ang)
    z8, z48, o48 = jnp.zeros((seq, 8), F32), jnp.zeros((seq, 48), F32), jnp.ones((seq, 48), F32)
    c = jnp.concatenate([cos, cos, o48], axis=1)
    sm = jnp.concatenate([-sin, z8, z48], axis=1)
    sp = jnp.concatenate([z8, sin, z48], axis=1)
    return tuple(jnp.concatenate([t, t], axis=1) for t in (c, sm, sp))


def _split3(x):
    a = x.astype(BF16)
    r = x - a.astype(F32)
    b = r.astype(BF16)
    c = (r - b.astype(F32)).astype(BF16)
    return a, b, c


def _row_stats(sel, x, terms):
    parts = _split3(x)[:terms]
    out = _dot_nt(sel, parts[0])
    for p in parts[1:]:
        out = out + _dot_nt(sel, p)
    return out


def _stat_selectors():
    row = lax.broadcasted_iota(jnp.int32, (8, LANES), 0)
    lane = lax.broadcasted_iota(jnp.int32, (8, LANES), 1)
    head_sum = ((row < 2) & (lane // HEAD_DIM == row)).astype(BF16)
    head_pick = ((row < 2) & (lane == row * HEAD_DIM)).astype(BF16)
    return head_sum, head_pick


def _gather_weights(w_in_s, w_out_s, w_mkv_s):
    shards = (w_in_s, w_out_s, w_mkv_s)
    n = len(shards)

    def body(*refs):
        srcs, outs = refs[:n], refs[n:2 * n]
        send_sems, recv_sems, local_sems = refs[2 * n:]
        x, y, c = lax.axis_index("x"), lax.axis_index("y"), lax.axis_index("c")
        my_chip = 2 * x + y
        sibling = (x, y, 1 - c)
        chips = [(1 - x, y), (x, 1 - y), (1 - x, 1 - y)]

        def half(t, chip, which):
            rows = shards[t].shape[0] // 2
            return outs[t].at[chip, pl.ds(which * rows, rows)]

        def src_half(t, which):
            rows = shards[t].shape[0] // 2
            return srcs[t].at[pl.ds(which * rows, rows)]

        def copy(k, src, dst, to):
            return pltpu.make_async_remote_copy(src_ref=src, dst_ref=dst, send_sem=send_sems.at[k],
                                                recv_sem=recv_sems.at[k], device_id=to, device_id_type=MESH)

        own = [pltpu.make_async_copy(srcs[t], outs[t].at[my_chip], local_sems.at[t]) for t in range(n)]
        for cp in own:
            cp.start()
        first = []
        for j, (cx, cy) in enumerate(chips):
            for t in range(n):
                first.append(copy(n * j + t, src_half(t, c), half(t, my_chip, c), (cx, cy, c)))
        for cp in first:
            cp.start()
        passed = []
        for j, (cx, cy) in enumerate(chips):
            chip = 2 * cx + cy
            for t in range(n):
                k = n * j + t
                copy(k, src_half(t, c), half(t, chip, c), (cx, cy, c)).wait_recv()
                fwd = copy(n * 3 + k, half(t, chip, c), half(t, chip, c), sibling)
                fwd.start()
                passed.append(fwd)
        for j, (cx, cy) in enumerate(chips):
            chip = 2 * cx + cy
            for t in range(n):
                k = n * 3 + n * j + t
                copy(k, half(t, chip, 1 - c), half(t, chip, 1 - c), sibling).wait_recv()
        for cp in first + passed:
            cp.wait_send()
        for cp in own:
            cp.wait()

    any_spec = pl.BlockSpec(memory_space=pl.ANY)
    return pl.pallas_call(
        body, name="gather_weights",
        out_shape=[jax.ShapeDtypeStruct((N_CHIPS,) + s.shape, s.dtype) for s in shards],
        in_specs=[any_spec] * n, out_specs=[any_spec] * n,
        scratch_shapes=[pltpu.SemaphoreType.DMA((6 * n,)), pltpu.SemaphoreType.DMA((6 * n,)),
                        pltpu.SemaphoreType.DMA((n,))],
    )(*shards)


def _mem_kv(mem, mem_norm, w_mkv):
    def body(mem_ref, g_ref, w_ref, mk_ref, mv_ref):
        m = mem_ref[...]
        r = lax.rsqrt(jnp.mean(m * m, axis=-1, keepdims=True) + RMS_EPS)
        mn = (m * r * g_ref[...]).astype(BF16)
        kv = _dot(mn, w_ref[...])
        mk_ref[...] = kv[:, :C_W].astype(BF16)
        mv_ref[...] = kv[:, C_W:].astype(BF16)

    return pl.pallas_call(
        body, name="mem_kv",
        out_shape=[jax.ShapeDtypeStruct((N_MEM, C_W), BF16)] * 2,
    )(mem, mem_norm, w_mkv)


def _mem_kv_bwd(mem, mem_norm, w_mkv, dmk, dmv):
    def body(mem_ref, g_ref, w_ref, dmk_ref, dmv_ref, gw_ref, gn_ref):
        m = mem_ref[...]
        r = lax.rsqrt(jnp.mean(m * m, axis=-1, keepdims=True) + RMS_EPS)
        mhat = m * r
        mn = (mhat * g_ref[...]).astype(BF16)
        dkv = jnp.concatenate([dmk_ref[...], dmv_ref[...]], axis=1).astype(BF16)
        gw_ref[...] = _dot_tn(mn, dkv)
        dmn = _dot_nt(dkv, w_ref[...])
        gn_ref[...] = jnp.sum(dmn * mhat, axis=0, keepdims=True)

    return pl.pallas_call(
        body, name="mem_kv_bwd",
        out_shape=[jax.ShapeDtypeStruct((D_MODEL, 2 * C_W), F32), jax.ShapeDtypeStruct((1, D_MODEL), F32)],
    )(mem, mem_norm, w_mkv, dmk, dmv)


def _pre_proj(x, pre_norm, w_in_g, rope):
    seq = x.shape[0]
    tm = min(512, seq)

    def body(x_ref, g_ref, w_ref, c_ref, sm_ref, sp_ref, *outs):
        out = dict(zip(_PROJ_NAMES + ("ut",), outs))
        xv = x_ref[...]
        r = lax.rsqrt(jnp.mean(xv * xv, axis=-1, keepdims=True) + RMS_EPS)
        u = xv * r * g_ref[...]
        ub = u.astype(BF16)
        out["ut"][...] = u.T.astype(BF16)
        c, sm, sp = c_ref[...], sm_ref[...], sp_ref[...]
        for j in range(N_CHIPS):
            pj = _dot(ub, w_ref[j])
            for b in range(SHARD_IN // LANES):
                name, off, roped, scaled = _PROJ_LAYOUT[(SHARD_IN // LANES) * j + b]
                piece = pj[:, LANES * b:LANES * (b + 1)]
                if roped:
                    piece = _rope(piece, c, sm, sp)
                if scaled:
                    piece = piece * SCALE
                out[name][:, off:off + LANES] = piece.astype(out[name].dtype)

    row = lambda w: pl.BlockSpec((tm, w), lambda i: (i, 0))
    full = lambda a: pl.BlockSpec(a.shape, lambda i: (0,) * a.ndim)
    out_shape = [jax.ShapeDtypeStruct((seq, _PROJ_WIDTH[n]), _PROJ_DTYPE[n]) for n in _PROJ_NAMES]
    out_shape.append(jax.ShapeDtypeStruct((D_MODEL, seq), BF16))
    res = pl.pallas_call(
        body, name="pre_proj", grid=(seq // tm,),
        in_specs=[row(D_MODEL), full(pre_norm), full(w_in_g), row(LANES), row(LANES), row(LANES)],
        out_specs=[row(_PROJ_WIDTH[n]) for n in _PROJ_NAMES] + [pl.BlockSpec((D_MODEL, tm), lambda i: (0, i))],
        out_shape=out_shape,
    )(x, pre_norm, w_in_g, *rope)
    return dict(zip(_PROJ_NAMES + ("ut",), res))


def _band_bias(max_dist, transposed):
    i = np.arange(BLOCK)[:, None]
    j = np.arange(BLOCK)[None, :]
    if transposed:
        same = i <= j
        other = (j + BLOCK - i) <= max_dist
        vis = np.concatenate([same, other], axis=1)
    else:
        prev = (i + BLOCK - j) <= max_dist
        same = j <= i
        vis = np.concatenate([prev, same], axis=1)
    return jnp.asarray(np.where(vis, 0.0, NEG).astype(np.float32))


def _kv_place(h, gqa):
    return (0, h // 3) if gqa else (h // 2, h % 2)


def _band_fwd(q, k, v, sink, *, dil, max_dist, name):
    seq, wq = q.shape
    wk = k.shape[1]
    gqa = wk != wq
    length = seq // dil
    tq = min(512, length)
    ns, nt = tq // BLOCK, length // tq
    npair = wq // LANES
    bias = _band_bias(max_dist, transposed=False)
    has_sink = sink is not None

    def body(*refs):
        if has_sink:
            sink_ref, refs = refs[0], refs[1:]
        q_ref, k_ref, kp_ref, v_ref, vp_ref, bias_ref, o_ref, lse_ref, kbuf, vbuf = refs[:10]
        i = pl.program_id(1)
        kbuf[0:BLOCK] = kp_ref[...]
        kbuf[BLOCK:] = k_ref[...]
        vbuf[0:BLOCK] = vp_ref[...]
        vbuf[BLOCK:] = v_ref[...]
        if gqa:
            kroll, vroll = refs[10:12]
            kroll[...] = pltpu.roll(kbuf[...], HEAD_DIM, 1)
            vroll[...] = pltpu.roll(vbuf[...], HEAD_DIM, 1)
        half = _half_masks(BLOCK)
        col_prev = (lax.broadcasted_iota(jnp.int32, (1, 2 * BLOCK), 1) < BLOCK).astype(F32)

        def sub(a, carry):
            r0 = pl.multiple_of(a * BLOCK, BLOCK)
            pen = jnp.where((i == 0) & (a == 0), NEG, 0.0)
            b = bias_ref[...] + pen * col_prev
            for p in range(npair):
                lanes = slice(p * LANES, (p + 1) * LANES)
                qp = q_ref[pl.ds(r0, BLOCK), lanes]
                o_h, l_h = [], []
                for e in range(2):
                    h = 2 * p + e
                    pk, ek = _kv_place(h, gqa)
                    klanes = slice(pk * LANES, (pk + 1) * LANES)
                    kw = (kbuf if ek == e else kroll)[pl.ds(r0, 2 * BLOCK), klanes]
                    vw = (vbuf if ek == e else vroll)[pl.ds(r0, 2 * BLOCK), klanes]
                    qm = jnp.where(half[e], qp, jnp.zeros_like(qp))
                    s = _dot_nt(qm, kw) + b
                    m = jnp.max(s, axis=1, keepdims=True)
                    if has_sink:
                        m = jnp.maximum(m, sink_ref[h])
                    pe = jnp.exp(s - m)
                    l = jnp.sum(pe, axis=1, keepdims=True)
                    if has_sink:
                        l = l + jnp.exp(sink_ref[h] - m)
                    pv = _dot(pe.astype(BF16), vw)
                    o_h.append(pv * (1.0 / l))
                    l_h.append(jnp.broadcast_to(m + jnp.log(l), (BLOCK, LANES)))
                o_ref[pl.ds(r0, BLOCK), lanes] = jnp.where(half[0], o_h[0], o_h[1])
                lse_ref[pl.ds(r0, BLOCK), lanes] = jnp.where(half[0], l_h[0], l_h[1])
            return carry

        lax.fori_loop(0, ns, sub, 0)

    main = lambda w: pl.BlockSpec((tq, w), lambda r, i: (i, r))
    prev = lambda w: pl.BlockSpec((BLOCK, w), lambda r, i: (jnp.maximum(i * ns - 1, 0), r))
    in_specs = [main(wq), main(wk), prev(wk), main(wk), prev(wk), pl.BlockSpec(bias.shape, lambda r, i: (0, 0))]
    args = [q.reshape(length, dil * wq), k.reshape(length, dil * wk), k.reshape(length, dil * wk),
            v.reshape(length, dil * wk), v.reshape(length, dil * wk), bias]
    if has_sink:
        in_specs = [pl.BlockSpec(memory_space=pltpu.SMEM)] + in_specs
        args = [sink] + args
    scratch = [pltpu.VMEM((tq + BLOCK, wk), BF16)] * 2
    if gqa:
        scratch = scratch + [pltpu.VMEM((tq + BLOCK, wk), BF16)] * 2
    o, lse = pl.pallas_call(
        body, name=name, grid=(dil, nt), in_specs=in_specs,
        out_specs=[main(wq), main(wq)],
        out_shape=[jax.ShapeDtypeStruct((length, dil * wq), F32)] * 2,
        scratch_shapes=scratch,
    )(*args)
    return o.reshape(seq, wq), lse.reshape(seq, wq)


def _band_bwd(q, k, v, do, o, lse, *, dil, max_dist, name):
    seq, wq = q.shape
    wk = k.shape[1]
    gqa = wk != wq
    length = seq // dil
    tq = min(512, length)
    ns, nt = tq // BLOCK, length // tq
    npair = wq // LANES
    nblocks = length // BLOCK
    bias = _band_bias(max_dist, transposed=True)

    def body(q_ref, qn_ref, do_ref, don_ref, o_ref, on_ref, lse_ref, lsen_ref, k_ref, v_ref, bias_ref,
             dq_ref, dk_ref, dv_ref, qbuf, dobuf, stat_l, stat_d, dqacc, *rolled):
        i = pl.program_id(1)
        qbuf[0:tq] = q_ref[...]
        qbuf[tq:] = qn_ref[...]
        dobuf[0:tq] = do_ref[...]
        dobuf[tq:] = don_ref[...]
        if gqa:
            kroll, vroll = rolled
            kroll[...] = pltpu.roll(k_ref[...], HEAD_DIM, 1)
            vroll[...] = pltpu.roll(v_ref[...], HEAD_DIM, 1)
        head_sum, head_pick = _stat_selectors()
        for a in range(ns + 1):
            rows = slice(a * BLOCK, (a + 1) * BLOCK) if a < ns else slice(0, BLOCK)
            for p in range(npair):
                lanes = slice(p * LANES, (p + 1) * LANES)
                src_o, src_l, src_do = (o_ref, lse_ref, do_ref) if a < ns else (on_ref, lsen_ref, don_ref)
                prod = src_do[rows, lanes].astype(F32) * src_o[rows, lanes]
                stat_d[a, p] = _row_stats(head_sum, prod, 3)
                stat_l[a, p] = _row_stats(head_pick, src_l[rows, lanes], 3)

        @pl.when(i == 0)
        def _():
            dqacc[0:BLOCK] = jnp.zeros((BLOCK, wq), F32)

        @pl.when(i > 0)
        def _():
            dqacc[0:BLOCK] = dqacc[tq:tq + BLOCK]

        dqacc[BLOCK:] = jnp.zeros((tq, wq), F32)
        half = _half_masks(BLOCK)
        half2 = _half_masks(2 * BLOCK)
        col_next = (lax.broadcasted_iota(jnp.int32, (1, 2 * BLOCK), 1) >= BLOCK).astype(F32)

        def sub(b, carry):
            r0 = pl.multiple_of(b * BLOCK, BLOCK)
            pen = jnp.where((i == nt - 1) & (b == ns - 1), NEG, 0.0)
            bt = bias_ref[...] + pen * col_next
            acc = {}
            for p in range(npair):
                lanes = slice(p * LANES, (p + 1) * LANES)
                qw = qbuf[pl.ds(r0, 2 * BLOCK), lanes]
                dow = dobuf[pl.ds(r0, 2 * BLOCK), lanes]
                for e in range(2):
                    h = 2 * p + e
                    pk, ek = _kv_place(h, gqa)
                    klanes = slice(pk * LANES, (pk + 1) * LANES)
                    kb = (k_ref if ek == e else kroll)[pl.ds(r0, BLOCK), klanes]
                    vb = (v_ref if ek == e else vroll)[pl.ds(r0, BLOCK), klanes]
                    qm = jnp.where(half2[e], qw, jnp.zeros_like(qw))
                    dom = jnp.where(half2[e], dow, jnp.zeros_like(dow))
                    lrow = jnp.concatenate([stat_l[b, p, e:e + 1, :], stat_l[b + 1, p, e:e + 1, :]], axis=1)
                    drow = jnp.concatenate([stat_d[b, p, e:e + 1, :], stat_d[b + 1, p, e:e + 1, :]], axis=1)
                    st = _dot_nt(kb, qm) + bt
                    pt = jnp.exp(st - lrow)
                    dv_c = _dot(pt.astype(BF16), dom)
                    dpt = _dot_nt(vb, dom)
                    dsb = (pt * (dpt - drow)).astype(BF16)
                    dk_c = _dot(dsb, qm)
                    kbm = jnp.where(half[e], kb, jnp.zeros_like(kb))
                    dqacc[pl.ds(r0, 2 * BLOCK), lanes] += _dot_tn(dsb, kbm)
                    key = (pk, ek == e)
                    if key in acc:
                        acc[key] = (acc[key][0] + dk_c, acc[key][1] + dv_c)
                    else:
                        acc[key] = (dk_c, dv_c)
                if not gqa:
                    dk_ref[pl.ds(r0, BLOCK), lanes] = acc[(p, True)][0]
                    dv_ref[pl.ds(r0, BLOCK), lanes] = acc[(p, True)][1]
            if gqa:
                dk_al, dv_al = acc[(0, True)]
                dk_mis, dv_mis = acc[(0, False)]
                dk_ref[pl.ds(r0, BLOCK), :] = dk_al + pltpu.roll(dk_mis, HEAD_DIM, 1)
                dv_ref[pl.ds(r0, BLOCK), :] = dv_al + pltpu.roll(dv_mis, HEAD_DIM, 1)
            return carry

        lax.fori_loop(0, ns, sub, 0)
        dq_ref[...] = dqacc[0:tq]

    main = lambda w: pl.BlockSpec((tq, w), lambda r, i: (i, r))
    nxt = lambda w: pl.BlockSpec((BLOCK, w), lambda r, i: (jnp.minimum((i + 1) * ns, nblocks - 1), r))
    view = lambda a: a.reshape(length, dil * a.shape[1])
    q2, k2, v2, do2, o2, lse2 = (view(a) for a in (q, k, v, do, o, lse))
    scratch = [pltpu.VMEM((tq + BLOCK, wq), BF16), pltpu.VMEM((tq + BLOCK, wq), BF16),
               pltpu.VMEM((ns + 1, npair, 8, LANES), F32), pltpu.VMEM((ns + 1, npair, 8, LANES), F32),
               pltpu.VMEM((tq + BLOCK, wq), F32)]
    if gqa:
        scratch = scratch + [pltpu.VMEM((tq, wk), BF16)] * 2
    dq, dk, dv = pl.pallas_call(
        body, name=name, grid=(dil, nt),
        in_specs=[main(wq), nxt(wq), main(wq), nxt(wq), main(wq), nxt(wq), main(wq), nxt(wq), main(wk), main(wk),
                  pl.BlockSpec(bias.shape, lambda r, i: (0, 0))],
        out_specs=[main(wq), main(wk), main(wk)],
        out_shape=[jax.ShapeDtypeStruct((length, dil * wq), F32), jax.ShapeDtypeStruct((length, dil * wk), F32),
                   jax.ShapeDtypeStruct((length, dil * wk), F32)],
        scratch_shapes=scratch,
        compiler_params=pltpu.CompilerParams(dimension_semantics=("arbitrary", "arbitrary")),
    )(q2, q2, do2, do2, o2, o2, lse2, lse2, k2, v2, bias)
    return dq.reshape(seq, wq), dk.reshape(seq, wk), dv.reshape(seq, wk)


def _mem_attn_fwd(q, mk, mv):
    seq = q.shape[0]
    tq = min(512, seq)
    ns = tq // BLOCK

    def body(q_ref, mk_ref, mv_ref, o_ref, lse_ref):
        half = _half_masks(BLOCK)

        def sub(a, carry):
            r0 = pl.multiple_of(a * BLOCK, BLOCK)
            for p in range(C_W // LANES):
                lanes = slice(p * LANES, (p + 1) * LANES)
                qp = q_ref[pl.ds(r0, BLOCK), lanes]
                o_h, l_h = [], []
                for e in range(2):
                    qm = jnp.where(half[e], qp, jnp.zeros_like(qp))
                    s = _dot_nt(qm, mk_ref[:, lanes])
                    m = jnp.max(s, axis=1, keepdims=True)
                    pe = jnp.exp(s - m)
                    l = jnp.sum(pe, axis=1, keepdims=True)
                    o_h.append(_dot(pe.astype(BF16), mv_ref[:, lanes]) * (1.0 / l))
                    l_h.append(jnp.broadcast_to(m + jnp.log(l), (BLOCK, LANES)))
                o_ref[pl.ds(r0, BLOCK), lanes] = jnp.where(half[0], o_h[0], o_h[1])
                lse_ref[pl.ds(r0, BLOCK), lanes] = jnp.where(half[0], l_h[0], l_h[1])
            return carry

        lax.fori_loop(0, ns, sub, 0)

    row = pl.BlockSpec((tq, C_W), lambda i: (i, 0))
    full = pl.BlockSpec((N_MEM, C_W), lambda i: (0, 0))
    return pl.pallas_call(
        body, name="mem_attn_fwd", grid=(seq // tq,), in_specs=[row, full, full], out_specs=[row, row],
        out_shape=[jax.ShapeDtypeStruct((seq, C_W), F32)] * 2,
    )(q, mk, mv)


def _mem_attn_bwd(q, mk, mv, do, o, lse):
    seq = q.shape[0]
    tq = min(512, seq)
    ns = tq // BLOCK
    npair = C_W // LANES

    def body(q_ref, mk_ref, mv_ref, do_ref, o_ref, lse_ref, dq_ref, dmk_ref, dmv_ref, stat_l, stat_d):
        @pl.when(pl.program_id(0) == 0)
        def _():
            dmk_ref[...] = jnp.zeros_like(dmk_ref)
            dmv_ref[...] = jnp.zeros_like(dmv_ref)

        head_sum, head_pick = _stat_selectors()
        for a in range(ns):
            rows = slice(a * BLOCK, (a + 1) * BLOCK)
            for p in range(npair):
                lanes = slice(p * LANES, (p + 1) * LANES)
                prod = do_ref[rows, lanes].astype(F32) * o_ref[rows, lanes]
                stat_d[a, p] = _row_stats(head_sum, prod, 3)
                stat_l[a, p] = _row_stats(head_pick, lse_ref[rows, lanes], 3)
        half = _half_masks(BLOCK)
        halfk = _half_masks(N_MEM)

        def sub(a, carry):
            r0 = pl.multiple_of(a * BLOCK, BLOCK)
            for p in range(npair):
                lanes = slice(p * LANES, (p + 1) * LANES)
                qp = q_ref[pl.ds(r0, BLOCK), lanes]
                dop = do_ref[pl.ds(r0, BLOCK), lanes]
                kb, vb = mk_ref[:, lanes], mv_ref[:, lanes]
                dq_pair = None
                for e in range(2):
                    qm = jnp.where(half[e], qp, jnp.zeros_like(qp))
                    dom = jnp.where(half[e], dop, jnp.zeros_like(dop))
                    st = _dot_nt(kb, qm)
                    pt = jnp.exp(st - stat_l[a, p, e:e + 1, :])
                    dmv_ref[:, lanes] += _dot(pt.astype(BF16), dom)
                    dpt = _dot_nt(vb, dom)
                    dsb = (pt * (dpt - stat_d[a, p, e:e + 1, :])).astype(BF16)
                    dmk_ref[:, lanes] += _dot(dsb, qm)
                    kbm = jnp.where(halfk[e], kb, jnp.zeros_like(kb))
                    dq_c = _dot_tn(dsb, kbm)
                    dq_pair = dq_c if dq_pair is None else dq_pair + dq_c
                dq_ref[pl.ds(r0, BLOCK), lanes] = dq_pair
            return carry

        lax.fori_loop(0, ns, sub, 0)

    row = pl.BlockSpec((tq, C_W), lambda i: (i, 0))
    full = pl.BlockSpec((N_MEM, C_W), lambda i: (0, 0))
    return pl.pallas_call(
        body, name="mem_attn_bwd", grid=(seq // tq,),
        in_specs=[row, full, full, row, row, row], out_specs=[row, full, full],
        out_shape=[jax.ShapeDtypeStruct((seq, C_W), F32), jax.ShapeDtypeStruct((N_MEM, C_W), F32),
                   jax.ShapeDtypeStruct((N_MEM, C_W), F32)],
        scratch_shapes=[pltpu.VMEM((ns, npair, 8, LANES), F32)] * 2,
        compiler_params=pltpu.CompilerParams(dimension_semantics=("arbitrary",)),
    )(q, mk, mv, do, o, lse)


def _silu_and_grad(g):
    s = 1.0 / (1.0 + jnp.exp(-g))
    return g * s, s * (1.0 + g * (1.0 - s))


def _post(x, target, post_norm, w_out, sink_lanes, oa, lse_a, ga, ob_list, lseb_list, gb, oc, gc):
    seq = x.shape[0]
    tm = min(256, seq)
    inv_d = 1.0 / D_MODEL

    def body(x_ref, t_ref, gp_ref, w_ref, sink_ref, oa_ref, lsea_ref, ga_ref, ob0, ob1, ob2, lb0, lb1, lb2, gb_ref,
             oc_ref, gc_ref,
             g_ref, ob_ref, lseb_ref, doa_ref, dga_ref, dob_ref, dgb_ref, doc_ref, dgc_ref,
             gw_ref, gpost_ref, gsink_ref, loss_ref, ycat):
        @pl.when(pl.program_id(0) == 0)
        def _():
            gw_ref[...] = jnp.zeros_like(gw_ref)
            gpost_ref[...] = jnp.zeros_like(gpost_ref)
            gsink_ref[...] = jnp.zeros_like(gsink_ref)
            loss_ref[...] = jnp.zeros_like(loss_ref)

        l0, l1, l2 = lb0[...], lb1[...], lb2[...]
        mx = jnp.maximum(jnp.maximum(l0, l1), l2)
        w0, w1, w2 = jnp.exp(l0 - mx), jnp.exp(l1 - mx), jnp.exp(l2 - mx)
        z = w0 + w1 + w2
        ob = (w0 * ob0[...] + w1 * ob1[...] + w2 * ob2[...]) / z
        ob_ref[...] = ob
        lseb_ref[...] = mx + jnp.log(z)
        oa, oc = oa_ref[...], oc_ref[...]
        sa, dsa = _silu_and_grad(ga_ref[...])
        sb, dsb = _silu_and_grad(gb_ref[...])
        sc, dsc = _silu_and_grad(gc_ref[...])
        ycat[:, 0:A_W] = (oa * sa).astype(BF16)
        ycat[:, A_W:A_W + B_W] = (ob * sb).astype(BF16)
        ycat[:, A_W + B_W:] = (oc * sc).astype(BF16)
        yc = ycat[...]
        y2 = _dot(yc, w_ref[...])
        r = lax.rsqrt(jnp.mean(y2 * y2, axis=-1, keepdims=True) + RMS_EPS)
        zhat = y2 * r
        gp = gp_ref[...]
        err = x_ref[...] + zhat * gp - t_ref[...]
        loss_ref[...] += jnp.sum(err * err) * (0.5 * inv_d)
        g = err * inv_d
        g_ref[...] = g
        gpost_ref[...] += jnp.sum(g * zhat, axis=0, keepdims=True)
        a = g * gp
        dy2 = (r * (a - zhat * jnp.mean(a * zhat, axis=-1, keepdims=True))).astype(BF16)
        gw_ref[...] += _dot_tn(yc, dy2)
        dycat = _dot_nt(dy2, w_ref[...])
        dya, dyb, dyc = dycat[:, 0:A_W], dycat[:, A_W:A_W + B_W], dycat[:, A_W + B_W:]
        doa = dya * sa
        doa_ref[...] = doa.astype(BF16)
        dga_ref[...] = dya * oa * dsa
        dob_ref[...] = (dyb * sb).astype(BF16)
        dgb_ref[...] = dyb * ob * dsb
        doc_ref[...] = (dyc * sc).astype(BF16)
        dgc_ref[...] = dyc * oc * dsc
        gsink_ref[...] += jnp.sum(jnp.exp(sink_ref[...] - lsea_ref[...]) * (doa * oa), axis=0, keepdims=True)

    row = lambda w: pl.BlockSpec((tm, w), lambda i: (i, 0))
    full = lambda shape: pl.BlockSpec(shape, lambda i: (0,) * len(shape))
    ins = [x, target, post_norm, w_out, sink_lanes, oa, lse_a, ga, *ob_list, *lseb_list, gb, oc, gc]
    in_specs = [row(D_MODEL), row(D_MODEL), full((1, D_MODEL)), full((D_MODEL, D_MODEL)), full((1, A_W)),
                row(A_W), row(A_W), row(A_W)] + [row(B_W)] * 7 + [row(C_W), row(C_W)]
    out_shape = [jax.ShapeDtypeStruct((seq, D_MODEL), F32),
                 jax.ShapeDtypeStruct((seq, B_W), F32), jax.ShapeDtypeStruct((seq, B_W), F32),
                 jax.ShapeDtypeStruct((seq, A_W), BF16), jax.ShapeDtypeStruct((seq, A_W), F32),
                 jax.ShapeDtypeStruct((seq, B_W), BF16), jax.ShapeDtypeStruct((seq, B_W), F32),
                 jax.ShapeDtypeStruct((seq, C_W), BF16), jax.ShapeDtypeStruct((seq, C_W), F32),
                 jax.ShapeDtypeStruct((D_MODEL, D_MODEL), F32), jax.ShapeDtypeStruct((1, D_MODEL), F32),
                 jax.ShapeDtypeStruct((1, A_W), F32), jax.ShapeDtypeStruct((1, LANES), F32)]
    out_specs = [row(D_MODEL), row(B_W), row(B_W), row(A_W), row(A_W), row(B_W), row(B_W), row(C_W), row(C_W),
                 full((D_MODEL, D_MODEL)), full((1, D_MODEL)), full((1, A_W)), full((1, LANES))]
    names = ("g", "ob", "lse_b", "doa", "dga", "dob", "dgb", "doc", "dgc", "gw_out", "gpost", "gsink", "loss")
    res = pl.pallas_call(
        body, name="post", grid=(seq // tm,), in_specs=in_specs, out_specs=out_specs, out_shape=out_shape,
        scratch_shapes=[pltpu.VMEM((tm, D_MODEL), BF16)],
        compiler_params=pltpu.CompilerParams(dimension_semantics=("arbitrary",)),
    )(*ins)
    return dict(zip(names, res))


def _dx(x, g, pre_norm, w_in_g, rope, grads):
    seq = x.shape[0]
    tm = min(256, seq)
    counts = [len(grads[n]) for n in _PROJ_NAMES]
    flat = [a for n in _PROJ_NAMES for a in grads[n]]

    def body(x_ref, g_ref, gp_ref, w_ref, c_ref, sm_ref, sp_ref, *refs):
        n_in = len(flat)
        ins, (dproj_ref, gx_ref, gpre_ref) = refs[:n_in], refs[n_in:]

        @pl.when(pl.program_id(0) == 0)
        def _():
            gpre_ref[...] = jnp.zeros_like(gpre_ref)

        by_name, pos = {}, 0
        for n, cnt in zip(_PROJ_NAMES, counts):
            by_name[n] = ins[pos:pos + cnt]
            pos += cnt
        c, sm, sp = c_ref[...], -sm_ref[...], -sp_ref[...]
        for blk, (name, off, roped, scaled) in enumerate(_PROJ_LAYOUT):
            piece = by_name[name][0][:, off:off + LANES]
            for extra in by_name[name][1:]:
                piece = piece + extra[:, off:off + LANES]
            if roped:
                piece = _rope(piece, c, sm, sp)
            if scaled:
                piece = piece * SCALE
            dproj_ref[:, blk * LANES:(blk + 1) * LANES] = piece.astype(BF16)
        du = None
        for j in range(N_CHIPS):
            part = _dot_nt(dproj_ref[:, j * SHARD_IN:(j + 1) * SHARD_IN], w_ref[j])
            du = part if du is None else du + part
        xv = x_ref[...]
        r = lax.rsqrt(jnp.mean(xv * xv, axis=-1, keepdims=True) + RMS_EPS)
        xhat = xv * r
        gpre_ref[...] += jnp.sum(du * xhat, axis=0, keepdims=True)
        a = du * gp_ref[...]
        gx_ref[...] = g_ref[...] + r * (a - xhat * jnp.mean(a * xhat, axis=-1, keepdims=True))

    row = lambda w: pl.BlockSpec((tm, w), lambda i: (i, 0))
    full = lambda a: pl.BlockSpec(a.shape, lambda i: (0,) * a.ndim)
    in_specs = ([row(D_MODEL), row(D_MODEL), full(pre_norm), full(w_in_g), row(LANES), row(LANES), row(LANES)]
                + [row(a.shape[1]) for a in flat])
    return pl.pallas_call(
        body, name="dx", grid=(seq // tm,), in_specs=in_specs,
        out_specs=[row(D_IN), row(D_MODEL), pl.BlockSpec((1, D_MODEL), lambda i: (0, 0))],
        out_shape=[jax.ShapeDtypeStruct((seq, D_IN), BF16), jax.ShapeDtypeStruct((seq, D_MODEL), F32),
                   jax.ShapeDtypeStruct((1, D_MODEL), F32)],
        compiler_params=pltpu.CompilerParams(dimension_semantics=("arbitrary",)),
    )(x, g, pre_norm, w_in_g, *rope, *flat)


def _grad_w_in(ut, dproj):
    seq = ut.shape[1]
    tk = min(1024, seq)

    def body(ut_ref, dp_ref, out_ref):
        @pl.when(pl.program_id(1) == 0)
        def _():
            out_ref[...] = jnp.zeros_like(out_ref)

        out_ref[...] += _dot(ut_ref[...], dp_ref[...])

    return pl.pallas_call(
        body, name="grad_w_in", grid=(N_CHIPS, seq // tk),
        in_specs=[pl.BlockSpec((D_MODEL, tk), lambda j, i: (0, i)), pl.BlockSpec((tk, SHARD_IN), lambda j, i: (i, j))],
        out_specs=pl.BlockSpec((None, D_MODEL, SHARD_IN), lambda j, i: (j, 0, 0)),
        out_shape=jax.ShapeDtypeStruct((N_CHIPS, D_MODEL, SHARD_IN), F32),
        compiler_params=pltpu.CompilerParams(dimension_semantics=("arbitrary", "arbitrary")),
    )(ut, dproj)


def _pair_exchange(grads):
    n = len(grads)

    def body(*refs):
        srcs, outs = refs[:n], refs[n:2 * n]
        send_sems, recv_sems = refs[2 * n:]
        x, y, c = lax.axis_index("x"), lax.axis_index("y"), lax.axis_index("c")
        copies = []
        for t in range(n):
            rows = grads[t].shape[1] // 2
            copies.append(pltpu.make_async_remote_copy(
                src_ref=srcs[t].at[:, pl.ds((1 - c) * rows, rows)], dst_ref=outs[t],
                send_sem=send_sems.at[t], recv_sem=recv_sems.at[t], device_id=(x, y, 1 - c), device_id_type=MESH))
        for cp in copies:
            cp.start()
        for cp in copies:
            cp.wait()

    any_spec = pl.BlockSpec(memory_space=pl.ANY)
    return pl.pallas_call(
        body, name="pair_exchange",
        out_shape=[jax.ShapeDtypeStruct((g.shape[0], g.shape[1] // 2, g.shape[2]), g.dtype) for g in grads],
        in_specs=[any_spec] * n, out_specs=[any_spec] * n,
        scratch_shapes=[pltpu.SemaphoreType.DMA((n,)), pltpu.SemaphoreType.DMA((n,))],
    )(*grads)


def _pair_add(core, own, got):
    nchip, rows2, width = own.shape
    rows = rows2 // 2
    tr = min(128, rows)
    nb = rows // tr

    def body(core_ref, own_ref, got_ref, out_ref):
        out_ref[...] = own_ref[...] + got_ref[...]

    grid_spec = pltpu.PrefetchScalarGridSpec(
        num_scalar_prefetch=1, grid=(nchip, nb),
        in_specs=[pl.BlockSpec((None, tr, width), lambda k, i, core_ref: (k, core_ref[0] * nb + i, 0)),
                  pl.BlockSpec((None, tr, width), lambda k, i, core_ref: (k, i, 0))],
        out_specs=pl.BlockSpec((None, tr, width), lambda k, i, core_ref: (k, i, 0)))
    return pl.pallas_call(
        body, name=f"pair_add_{width}", grid_spec=grid_spec,
        out_shape=jax.ShapeDtypeStruct((nchip, rows, width), own.dtype),
    )(core, own, got)


def _chip_exchange(parts, small):
    n = len(parts)

    def body(*refs):
        srcs, small_ref = refs[:n], refs[n]
        outs, small_out = refs[n + 1:2 * n + 1], refs[2 * n + 1]
        send_sems, recv_sems, local_sems = refs[2 * n + 2:]
        x, y, c = lax.axis_index("x"), lax.axis_index("y"), lax.axis_index("c")
        my_chip = 2 * x + y
        me = 4 * x + 2 * y + c
        chips = [(1 - x, y), (x, 1 - y), (1 - x, 1 - y)]
        local = [pltpu.make_async_copy(srcs[t].at[my_chip], outs[t].at[my_chip], local_sems.at[t]) for t in range(n)]
        local.append(pltpu.make_async_copy(small_ref, small_out.at[me], local_sems.at[n]))
        for cp in local:
            cp.start()
        sent = []
        for j, (cx, cy) in enumerate(chips):
            for t in range(n):
                k = n * j + t
                sent.append(pltpu.make_async_remote_copy(
                    src_ref=srcs[t].at[2 * cx + cy], dst_ref=outs[t].at[my_chip], send_sem=send_sems.at[k],
                    recv_sem=recv_sems.at[k], device_id=(cx, cy, c), device_id_type=MESH))
        peers = [(x, y, 1 - c)] + [(cx, cy, cc) for (cx, cy) in chips for cc in (c, 1 - c)]
        for j, peer in enumerate(peers):
            k = 3 * n + j
            sent.append(pltpu.make_async_remote_copy(
                src_ref=small_ref, dst_ref=small_out.at[me], send_sem=send_sems.at[k], recv_sem=recv_sems.at[k],
                device_id=peer, device_id_type=MESH))
        for cp in sent:
            cp.start()
        for cp in sent:
            cp.wait()
        for cp in local:
            cp.wait()

    any_spec = pl.BlockSpec(memory_space=pl.ANY)
    nsem = 3 * n + 7
    return pl.pallas_call(
        body, name="chip_exchange",
        out_shape=[jax.ShapeDtypeStruct(p.shape, p.dtype) for p in parts]
        + [jax.ShapeDtypeStruct((8,) + small.shape, small.dtype)],
        in_specs=[any_spec] * (n + 1), out_specs=[any_spec] * (n + 1),
        scratch_shapes=[pltpu.SemaphoreType.DMA((nsem,)), pltpu.SemaphoreType.DMA((nsem,)),
                        pltpu.SemaphoreType.DMA((n + 1,))],
    )(*parts, small)


def _slot_sum(slots, name):
    ns, rows, width = slots.shape
    tr = min(128, rows)

    def body(in_ref, out_ref):
        acc = in_ref[0]
        for s in range(1, ns):
            acc = acc + in_ref[s]
        out_ref[...] = acc

    return pl.pallas_call(
        body, name=name, grid=(rows // tr,),
        in_specs=[pl.BlockSpec((ns, tr, width), lambda i: (0, i, 0))],
        out_specs=pl.BlockSpec((tr, width), lambda i: (i, 0)),
        out_shape=jax.ShapeDtypeStruct((rows, width), slots.dtype),
    )(slots)


def _pair_gather(halves):
    n = len(halves)

    def body(*refs):
        srcs, outs = refs[:n], refs[n:2 * n]
        send_sems, recv_sems, local_sems = refs[2 * n:]
        x, y, c = lax.axis_index("x"), lax.axis_index("y"), lax.axis_index("c")
        copies, local = [], []
        for t in range(n):
            rows = halves[t].shape[0]
            mine = outs[t].at[pl.ds(c * rows, rows)]
            local.append(pltpu.make_async_copy(srcs[t], mine, local_sems.at[t]))
            copies.append(pltpu.make_async_remote_copy(
                src_ref=srcs[t], dst_ref=mine, send_sem=send_sems.at[t], recv_sem=recv_sems.at[t],
                device_id=(x, y, 1 - c), device_id_type=MESH))
        for cp in local + copies:
            cp.start()
        for cp in copies:
            cp.wait()
        for cp in local:
            cp.wait()

    any_spec = pl.BlockSpec(memory_space=pl.ANY)
    return pl.pallas_call(
        body, name="pair_gather",
        out_shape=[jax.ShapeDtypeStruct((2 * h.shape[0], h.shape[1]), h.dtype) for h in halves],
        in_specs=[any_spec] * n, out_specs=[any_spec] * n,
        scratch_shapes=[pltpu.SemaphoreType.DMA((n,)), pltpu.SemaphoreType.DMA((n,)), pltpu.SemaphoreType.DMA((n,))],
    )(*halves)


def _adamw(w, g, m, v, name):
    rows, width = w.shape
    tr = min(256, rows)
    c1 = 1.0 / (1.0 - ADAM_B1 ** ADAM_STEP)
    c2 = 1.0 / (1.0 - ADAM_B2 ** ADAM_STEP)

    def body(w_ref, g_ref, m_ref, v_ref, d_ref, nm_ref, nv_ref):
        gv = g_ref[...]
        nm = ADAM_B1 * m_ref[...] + (1.0 - ADAM_B1) * gv
        nv = ADAM_B2 * v_ref[...] + (1.0 - ADAM_B2) * (gv * gv)
        nm_ref[...] = nm
        nv_ref[...] = nv
        d_ref[...] = -ADAM_LR * ((nm * c1) / (jnp.sqrt(nv * c2) + ADAM_EPS) + ADAM_WD * w_ref[...])

    spec = pl.BlockSpec((tr, width), lambda i: (i, 0))
    return pl.pallas_call(
        body, name=name, grid=(rows // tr,), in_specs=[spec] * 4, out_specs=[spec] * 3,
        out_shape=[jax.ShapeDtypeStruct(w.shape, F32)] * 3,
    )(w, g, m, v)


def _local_step(x, mem, target, pre_norm, sink_a, mem_norm, post_norm, w_in_g, w_out, w_mkv):
    seq = x.shape[0]
    rope = _rope_tables(seq)
    mk, mv = _mem_kv(mem, mem_norm, w_mkv)
    pr = _pre_proj(x, pre_norm, w_in_g, rope)
    sink = sink_a.reshape(-1)
    oa, lse_a = _band_fwd(pr["qa"], pr["ka"], pr["va"], sink, dil=1, max_dist=A_WINDOW - 1, name="swa_fwd")
    ob_list, lseb_list = [], []
    for win, dil in B_CONFIGS:
        o_i, l_i = _band_fwd(pr["qb"], pr["kb"], pr["vb"], None, dil=dil, max_dist=win // dil, name=f"dil{dil}_fwd")
        ob_list.append(o_i)
        lseb_list.append(l_i)
    oc, lse_c = _mem_attn_fwd(pr["qc"], mk, mv)
    sink_lanes = jnp.repeat(sink, HEAD_DIM).reshape(1, A_W)
    po = _post(x, target, post_norm, w_out, sink_lanes, oa, lse_a, pr["ga"], ob_list, lseb_list, pr["gb"], oc, pr["gc"])
    dqc, dmk, dmv = _mem_attn_bwd(pr["qc"], mk, mv, po["doc"], oc, lse_c)
    dqa, dka, dva = _band_bwd(pr["qa"], pr["ka"], pr["va"], po["doa"], oa, lse_a, dil=1, max_dist=A_WINDOW - 1,
                              name="swa_bwd")
    dqb, dkb, dvb = [], [], []
    for win, dil in B_CONFIGS:
        dq_i, dk_i, dv_i = _band_bwd(pr["qb"], pr["kb"], pr["vb"], po["dob"], po["ob"], po["lse_b"], dil=dil,
                                     max_dist=win // dil, name=f"dil{dil}_bwd")
        dqb.append(dq_i)
        dkb.append(dk_i)
        dvb.append(dv_i)
    grads = dict(qa=[dqa], ka=[dka], va=[dva], ga=[po["dga"]], qb=dqb, kb=dkb, vb=dvb, gb=[po["dgb"]],
                 qc=[dqc], gc=[po["dgc"]])
    dproj, grad_x, gpre = _dx(x, po["g"], pre_norm, w_in_g, rope, grads)
    gw_in = _grad_w_in(pr["ut"], dproj)
    gw_mkv, gmem = _mem_kv_bwd(mem, mem_norm, w_mkv, dmk, dmv)
    gsink = -jnp.sum(po["gsink"].reshape(A_W // HEAD_DIM, HEAD_DIM), axis=1)
    return dict(loss=po["loss"][0, 0], grad_x=grad_x, gw_in=gw_in, gw_out=po["gw_out"], gw_mkv=gw_mkv,
                gpre=gpre, gpost=po["gpost"], gmem=gmem, gsink=gsink)


def kernel(x, mem, pre_norm, w_in, sink_a, mem_norm, w_mem_kv, w_out, post_norm, loss_target, m_pre_norm, m_w_in, m_sink_a, m_mem_norm, m_w_mem_kv, m_w_out, m_post_norm, v_pre_norm, v_w_in, v_sink_a, v_mem_norm, v_w_mem_kv, v_w_out, v_post_norm):
    w_in_g, w_out_g, w_mkv_g = _gather_weights(w_in[0].astype(BF16), w_out[0].astype(BF16), w_mem_kv[0].astype(BF16))
    loc = _local_step(x[0], mem[0], loss_target[0], pre_norm, sink_a, mem_norm, post_norm,
                      w_in_g, w_out_g.reshape(D_MODEL, D_MODEL), w_mkv_g.reshape(D_MODEL, 2 * C_W))
    loss = lax.psum(loc["loss"], ("x", "y", "c"))

    big = [loc["gw_in"], loc["gw_out"].reshape(N_CHIPS, D_MODEL // N_CHIPS, D_MODEL),
           loc["gw_mkv"].reshape(N_CHIPS, D_MODEL // N_CHIPS, 2 * C_W)]
    small = jnp.concatenate([loc["gpre"], loc["gpost"], loc["gmem"],
                             jnp.pad(loc["gsink"], (0, D_MODEL - loc["gsink"].shape[0])).reshape(1, D_MODEL),
                             jnp.zeros((4, D_MODEL), F32)], axis=0)
    core = lax.axis_index("c").astype(jnp.int32).reshape(1)
    got = _pair_exchange(big)
    parts = [_pair_add(core, own, g) for own, g in zip(big, got)]
    *slots, small_slots = _chip_exchange(parts, small)
    halves = [_slot_sum(s, name=f"chip_sum_{s.shape[2]}") for s in slots]
    g_in, g_out, g_mkv = _pair_gather(halves)
    small_sum = _slot_sum(small_slots, name="device_sum")
    g_pre, g_post, g_mem = small_sum[0:1], small_sum[1:2], small_sum[2:3]
    g_sink = small_sum[3:4, :sink_a.shape[1]]

    d_in, nm_in, nv_in = _adamw(w_in[0], g_in, m_w_in[0], v_w_in[0], "adamw_in")
    d_out, nm_out, nv_out = _adamw(w_out[0], g_out, m_w_out[0], v_w_out[0], "adamw_out")
    d_mkv, nm_mkv, nv_mkv = _adamw(w_mem_kv[0], g_mkv, m_w_mem_kv[0], v_w_mem_kv[0], "adamw_mkv")
    pad6 = lambda a: jnp.pad(a, ((0, 0), (0, D_MODEL - a.shape[1])))
    stack = lambda a, b, c_, d_: jnp.concatenate([a, b, c_, pad6(d_), jnp.zeros((4, D_MODEL), F32)], axis=0)
    d_s, nm_s, nv_s = _adamw(stack(pre_norm, post_norm, mem_norm, sink_a), small_sum,
                             stack(m_pre_norm, m_post_norm, m_mem_norm, m_sink_a),
                             stack(v_pre_norm, v_post_norm, v_mem_norm, v_sink_a), "adamw_small")
    ns_ = sink_a.shape[1]
    unpack = lambda a: (a[0:1], a[3:4, :ns_], a[2:3], a[1:2])
    d_pre, d_sink, d_mem, d_post = unpack(d_s)
    nm_pre, nm_sink, nm_mem, nm_post = unpack(nm_s)
    nv_pre, nv_sink, nv_mem, nv_post = unpack(nv_s)
    lead = lambda a: a[None]
    return (loss, lead(loc["grad_x"]),
            g_pre, lead(g_in), g_sink, g_mem, lead(g_mkv), lead(g_out), g_post,
            d_pre, lead(d_in), d_sink, d_mem, lead(d_mkv), lead(d_out), d_post,
            nm_pre, lead(nm_in), nm_sink, nm_mem, lead(nm_mkv), lead(nm_out), nm_post,
            nv_pre, lead(nv_in), nv_sink, nv_mem, lead(nv_mkv), lead(nv_out), nv_post)
```

```python
import numpy as np
import jax
import jax.numpy as jnp
from jax import lax
from jax.experimental import pallas as pl
from jax.experimental.pallas import tpu as pltpu

F32 = jnp.float32
BF16 = jnp.bfloat16

D_MODEL = 1024
HEAD_DIM = 64
LANES = 128
BLOCK = 128
A_W, A_KV_W, B_W, C_W = 384, 128, 384, 256
N_MEM = 256
D_IN = 3072
N_CHIPS = 4
SHARD_IN = D_IN // N_CHIPS
B_CONFIGS = ((128, 1), (512, 4), (2048, 16))
B_DILS = tuple(d for _, d in B_CONFIGS)
A_WINDOW = 128
RMS_EPS = 1e-6
ROPE_THETA = 500000.0
SCALE = HEAD_DIM ** -0.5
NEG = -1e30
ADAM_LR, ADAM_B1, ADAM_B2, ADAM_EPS, ADAM_WD, ADAM_STEP = 0.001, 0.9, 0.999, 1e-08, 0.01, 10

NT = (((1,), (1,)), ((), ()))
TN = (((0,), (0,)), ((), ()))
MESH = pl.DeviceIdType.MESH

_PROJ_LAYOUT = (
    [("qa", 128 * i, True, True) for i in range(3)] + [("ka", 0, True, False), ("va", 0, False, False)]
    + [("ga", 128 * i, False, False) for i in range(3)]
    + [("qb", 128 * i, True, True) for i in range(3)] + [("kb", 128 * i, True, False) for i in range(3)]
    + [("vb", 128 * i, False, False) for i in range(3)] + [("gb", 128 * i, False, False) for i in range(3)]
    + [("qc", 128 * i, False, True) for i in range(2)] + [("gc", 128 * i, False, False) for i in range(2)]
)
_PROJ_WIDTH = dict(qa=A_W, ka=A_KV_W, va=A_KV_W, ga=A_W, qb=B_W, kb=B_W, vb=B_W, gb=B_W, qc=C_W, gc=C_W)
_NATURAL = ("qa", "ka", "va", "ga", "gb", "qc", "gc")
_DILATED = ("qb", "kb", "vb")


def _dot(a, b):
    return jnp.dot(a, b, preferred_element_type=F32)


def _dot_nt(a, b):
    return lax.dot_general(a, b, NT, preferred_element_type=F32)


def _dot_tn(a, b):
    return lax.dot_general(a, b, TN, preferred_element_type=F32)


def _half_masks(rows):
    lane = lax.broadcasted_iota(jnp.int32, (rows, LANES), 1)
    return lane < HEAD_DIM, lane >= HEAD_DIM


def _rope(t, c, sm, sp):
    return t * c + pltpu.roll(t, LANES - 8, 1) * sm + pltpu.roll(t, 8, 1) * sp


def _rope_tables(seq):
    dim = jnp.arange(LANES) % HEAD_DIM
    inv_freq = ROPE_THETA ** (-jnp.arange(0, 16, 2, dtype=F32) / 16)
    ang = jnp.arange(seq, dtype=F32)[:, None] * inv_freq[dim % 8][None, :]
    cos, sin = jnp.cos(ang), jnp.sin(ang)
    c = jnp.where(dim < 16, cos, 1.0)
    sm = jnp.where(dim < 8, -sin, 0.0)
    sp = jnp.where((dim >= 8) & (dim < 16), sin, 0.0)
    return c, sm, sp


def _split3(x):
    a = x.astype(BF16)
    r = x - a.astype(F32)
    b = r.astype(BF16)
    c = (r - b.astype(F32)).astype(BF16)
    return a, b, c


def _rows_to_lanes(x):
    row = lax.broadcasted_iota(jnp.int32, (8, LANES), 0)
    lane = lax.broadcasted_iota(jnp.int32, (8, LANES), 1)
    eye = (row == lane).astype(BF16)
    a, b, c = _split3(x)
    return _dot_nt(eye, a) + _dot_nt(eye, b) + _dot_nt(eye, c)


def _head_sum_matrix(width):
    k = lax.broadcasted_iota(jnp.int32, (width, LANES), 0)
    h = lax.broadcasted_iota(jnp.int32, (width, LANES), 1)
    return (k // HEAD_DIM == h).astype(BF16)


def _head_expand_matrix(width):
    h = lax.broadcasted_iota(jnp.int32, (LANES, width), 0)
    k = lax.broadcasted_iota(jnp.int32, (LANES, width), 1)
    return (k // HEAD_DIM == h).astype(BF16)


def _dot_split(x, mat, terms):
    parts = _split3(x)[:terms]
    out = _dot(parts[0], mat)
    for p in parts[1:]:
        out = out + _dot(p, mat)
    return out


def _per_head(cols):
    rows = cols[0].shape[0]
    lane = lax.broadcasted_iota(jnp.int32, (rows, LANES), 1)
    out = jnp.zeros((rows, LANES), F32)
    for h, col in enumerate(cols):
        out = jnp.where(lane == h, col, out)
    return out


def _lane_blocks(width):
    return [slice(p * LANES, (p + 1) * LANES) for p in range(width // LANES)]


def _stage(rows, width):
    return pltpu.VMEM((width // LANES, rows, LANES), F32)


def _stage_write(buf, value):
    for p, lanes in enumerate(_lane_blocks(value.shape[1])):
        buf[p] = value[:, lanes]


def _stage_read(buf):
    return jnp.concatenate([buf[p] for p in range(buf.shape[0])], axis=1) if buf.shape[0] > 1 else buf[0]


def _to_residues(buf, out_ref, dil):
    rows = buf.shape[1] // dil
    for r in range(dil):
        for p in range(buf.shape[0]):
            plane = buf.at[p]
            out_ref[r, :, p * LANES:(p + 1) * LANES] = plane[pl.ds(r, rows, stride=dil), :].astype(out_ref.dtype)


def _from_residues(in_ref, buf, dil):
    rows = buf.shape[1] // dil
    for r in range(dil):
        for p in range(buf.shape[0]):
            plane = buf.at[p]
            plane[pl.ds(r, rows, stride=dil), :] = in_ref[r, :, p * LANES:(p + 1) * LANES].astype(F32)


def _residue_spec(dil, tm, width):
    return pl.BlockSpec((dil, tm // dil, width), lambda i: (0, i, 0))


def _gather_weights(w_in_s, w_out_s, w_mkv_s):
    shards = (w_in_s, w_out_s, w_mkv_s)
    n = len(shards)

    def body(*refs):
        srcs, outs = refs[:n], refs[n:2 * n]
        send_sems, recv_sems, local_sems = refs[2 * n:]
        x, y, c = lax.axis_index("x"), lax.axis_index("y"), lax.axis_index("c")
        my_chip = 2 * x + y
        sibling = (x, y, 1 - c)
        chips = [(1 - x, y), (x, 1 - y), (1 - x, 1 - y)]

        def half(t, chip, which):
            rows = shards[t].shape[0] // 2
            return outs[t].at[chip, pl.ds(which * rows, rows)]

        def src_half(t, which):
            rows = shards[t].shape[0] // 2
            return srcs[t].at[pl.ds(which * rows, rows)]

        def copy(k, src, dst, to):
            return pltpu.make_async_remote_copy(src_ref=src, dst_ref=dst, send_sem=send_sems.at[k],
                                                recv_sem=recv_sems.at[k], device_id=to, device_id_type=MESH)

        own = [pltpu.make_async_copy(srcs[t], outs[t].at[my_chip], local_sems.at[t]) for t in range(n)]
        for cp in own:
            cp.start()
        first = []
        for j, (cx, cy) in enumerate(chips):
            for t in range(n):
                first.append(copy(n * j + t, src_half(t, c), half(t, my_chip, c), (cx, cy, c)))
        for cp in first:
            cp.start()
        passed = []
        for j, (cx, cy) in enumerate(chips):
            chip = 2 * cx + cy
            for t in range(n):
                k = n * j + t
                copy(k, src_half(t, c), half(t, chip, c), (cx, cy, c)).wait_recv()
                fwd = copy(n * 3 + k, half(t, chip, c), half(t, chip, c), sibling)
                fwd.start()
                passed.append(fwd)
        for j, (cx, cy) in enumerate(chips):
            chip = 2 * cx + cy
            for t in range(n):
                k = n * 3 + n * j + t
                copy(k, half(t, chip, 1 - c), half(t, chip, 1 - c), sibling).wait_recv()
        for cp in first + passed:
            cp.wait_send()
        for cp in own:
            cp.wait()

    any_spec = pl.BlockSpec(memory_space=pl.ANY)
    return pl.pallas_call(
        body, name="gather_weights",
        out_shape=[jax.ShapeDtypeStruct((N_CHIPS,) + s.shape, s.dtype) for s in shards],
        in_specs=[any_spec] * n, out_specs=[any_spec] * n,
        scratch_shapes=[pltpu.SemaphoreType.DMA((6 * n,)), pltpu.SemaphoreType.DMA((6 * n,)),
                        pltpu.SemaphoreType.DMA((n,))],
    )(*shards)


def _mem_kv(mem, mem_norm, w_mkv):
    def body(mem_ref, g_ref, w_ref, mk_ref, mv_ref):
        m = mem_ref[...]
        r = lax.rsqrt(jnp.mean(m * m, axis=-1, keepdims=True) + RMS_EPS)
        mn = (m * r * g_ref[...]).astype(BF16)
        kv = _dot(mn, w_ref[...])
        mk_ref[...] = kv[:, :C_W].astype(BF16)
        mv_ref[...] = kv[:, C_W:].astype(BF16)

    return pl.pallas_call(
        body, name="mem_kv",
        out_shape=[jax.ShapeDtypeStruct((N_MEM, C_W), BF16)] * 2,
    )(mem, mem_norm, w_mkv)


def _mem_kv_bwd(mem, mem_norm, w_mkv, dmk, dmv):
    def body(mem_ref, g_ref, w_ref, dmk_ref, dmv_ref, gw_ref, gn_ref):
        m = mem_ref[...]
        r = lax.rsqrt(jnp.mean(m * m, axis=-1, keepdims=True) + RMS_EPS)
        mhat = m * r
        mn = (mhat * g_ref[...]).astype(BF16)
        dkv = jnp.concatenate([dmk_ref[...], dmv_ref[...]], axis=1).astype(BF16)
        gw_ref[...] = _dot_tn(mn, dkv)
        dmn = _dot_nt(dkv, w_ref[...])
        gn_ref[...] = jnp.sum(dmn * mhat, axis=0, keepdims=True)

    return pl.pallas_call(
        body, name="mem_kv_bwd",
        out_shape=[jax.ShapeDtypeStruct((D_MODEL, 2 * C_W), F32), jax.ShapeDtypeStruct((1, D_MODEL), F32)],
    )(mem, mem_norm, w_mkv, dmk, dmv)


def _pre_proj(x, pre_norm, w_in_g, rope):
    seq = x.shape[0]
    tm = min(512, seq)
    n_nat, n_dil = len(_NATURAL), len(_DILATED) * len(B_DILS)

    def body(x_ref, g_ref, w_ref, c_ref, sm_ref, sp_ref, *refs):
        nat = dict(zip(_NATURAL, refs[:n_nat]))
        res = {n: refs[n_nat + len(B_DILS) * k:n_nat + len(B_DILS) * (k + 1)] for k, n in enumerate(_DILATED)}
        ut = refs[n_nat + n_dil]
        bufs = dict(zip(_DILATED, refs[n_nat + n_dil + 1:]))
        xv = x_ref[...]
        r = lax.rsqrt(jnp.mean(xv * xv, axis=-1, keepdims=True) + RMS_EPS)
        u = xv * r * g_ref[...]
        ub = u.astype(BF16)
        ut[...] = u.T.astype(BF16)
        c, sm, sp = c_ref[...], sm_ref[...], sp_ref[...]
        for j in range(N_CHIPS):
            pj = _dot(ub, w_ref[j])
            for b in range(SHARD_IN // LANES):
                name, off, roped, scaled = _PROJ_LAYOUT[(SHARD_IN // LANES) * j + b]
                piece = pj[:, LANES * b:LANES * (b + 1)]
                if roped:
                    piece = _rope(piece, c, sm, sp)
                if scaled:
                    piece = piece * SCALE
                if name in bufs:
                    bufs[name][off // LANES] = piece
                else:
                    nat[name][:, off:off + LANES] = piece.astype(BF16)
        for name in _DILATED:
            for ref, dil in zip(res[name], B_DILS):
                _to_residues(bufs[name], ref, dil)

    row = lambda w: pl.BlockSpec((tm, w), lambda i: (i, 0))
    full = lambda a: pl.BlockSpec(a.shape, lambda i: (0,) * a.ndim)
    out_shape = [jax.ShapeDtypeStruct((seq, _PROJ_WIDTH[n]), BF16) for n in _NATURAL]
    out_specs = [row(_PROJ_WIDTH[n]) for n in _NATURAL]
    for n in _DILATED:
        for dil in B_DILS:
            out_shape.append(jax.ShapeDtypeStruct((dil, seq // dil, B_W), BF16))
            out_specs.append(_residue_spec(dil, tm, B_W))
    out_shape.append(jax.ShapeDtypeStruct((D_MODEL, seq), BF16))
    out_specs.append(pl.BlockSpec((D_MODEL, tm), lambda i: (0, i)))
    res = pl.pallas_call(
        body, name="pre_proj", grid=(seq // tm,),
        in_specs=[row(D_MODEL), full(pre_norm), full(w_in_g), row(LANES), row(LANES), row(LANES)],
        out_specs=out_specs, out_shape=out_shape,
        scratch_shapes=[_stage(tm, B_W)] * len(_DILATED),
    )(x, pre_norm, w_in_g, *rope)
    out = dict(zip(_NATURAL, res[:n_nat]))
    for k, n in enumerate(_DILATED):
        out[n] = res[n_nat + len(B_DILS) * k:n_nat + len(B_DILS) * (k + 1)]
    out["ut"] = res[n_nat + n_dil]
    return out


def _band_bias(max_dist, transposed):
    i = np.arange(BLOCK)[:, None]
    j = np.arange(BLOCK)[None, :]
    if transposed:
        same = i <= j
        other = (j + BLOCK - i) <= max_dist
        vis = np.concatenate([same, other], axis=1)
    else:
        prev = (i + BLOCK - j) <= max_dist
        same = j <= i
        vis = np.concatenate([prev, same], axis=1)
    return jnp.asarray(np.where(vis, 0.0, NEG).astype(np.float32))


def _kv_place(h, gqa):
    return (0, h // 3) if gqa else (h // 2, h % 2)


def _band_fwd(q, k, v, sink, *, max_dist, name):
    dil, length, wq = q.shape
    wk = k.shape[2]
    gqa = wk != wq
    tq = min(512, length)
    ns, nt = tq // BLOCK, length // tq
    npair = wq // LANES
    bias = _band_bias(max_dist, transposed=False)
    has_sink = sink is not None

    def body(*refs):
        if has_sink:
            sink_ref, refs = refs[0], refs[1:]
        q_ref, k_ref, kp_ref, v_ref, vp_ref, bias_ref, o_ref, lse_ref, kbuf, vbuf = refs[:10]
        i = pl.program_id(1)
        kbuf[0:BLOCK] = kp_ref[...]
        kbuf[BLOCK:] = k_ref[...]
        vbuf[0:BLOCK] = vp_ref[...]
        vbuf[BLOCK:] = v_ref[...]
        if gqa:
            kroll, vroll = refs[10:12]
            kroll[...] = pltpu.roll(kbuf[...], HEAD_DIM, 1)
            vroll[...] = pltpu.roll(vbuf[...], HEAD_DIM, 1)
        half = _half_masks(BLOCK)
        col_prev = (lax.broadcasted_iota(jnp.int32, (1, 2 * BLOCK), 1) < BLOCK).astype(F32)

        def sub(a, carry):
            r0 = pl.multiple_of(a * BLOCK, BLOCK)
            pen = jnp.where((i == 0) & (a == 0), NEG, 0.0)
            b = bias_ref[...] + pen * col_prev
            lse_cols = []
            for p in range(npair):
                lanes = slice(p * LANES, (p + 1) * LANES)
                qp = q_ref[pl.ds(r0, BLOCK), lanes]
                o_h = []
                for e in range(2):
                    h = 2 * p + e
                    pk, ek = _kv_place(h, gqa)
                    klanes = slice(pk * LANES, (pk + 1) * LANES)
                    kw = (kbuf if ek == e else kroll)[pl.ds(r0, 2 * BLOCK), klanes]
                    vw = (vbuf if ek == e else vroll)[pl.ds(r0, 2 * BLOCK), klanes]
                    qm = jnp.where(half[e], qp, jnp.zeros_like(qp))
                    s = _dot_nt(qm, kw) + b
                    m = jnp.max(s, axis=1, keepdims=True)
                    if has_sink:
                        m = jnp.maximum(m, sink_ref[h])
                    pe = jnp.exp(s - m)
                    l = jnp.sum(pe, axis=1, keepdims=True)
                    if has_sink:
                        l = l + jnp.exp(sink_ref[h] - m)
                    pv = _dot(pe.astype(BF16), vw)
                    o_h.append(pv * (1.0 / l))
                    lse_cols.append(m + jnp.log(l))
                o_ref[pl.ds(r0, BLOCK), lanes] = jnp.where(half[0], o_h[0], o_h[1]).astype(BF16)
            lse_ref[pl.ds(r0, BLOCK), :] = _per_head(lse_cols)
            return carry

        lax.fori_loop(0, ns, sub, 0)

    main = lambda w: pl.BlockSpec((None, tq, w), lambda r, i: (r, i, 0))
    prev = lambda w: pl.BlockSpec((None, BLOCK, w), lambda r, i: (r, jnp.maximum(i * ns - 1, 0), 0))
    in_specs = [main(wq), main(wk), prev(wk), main(wk), prev(wk), pl.BlockSpec(bias.shape, lambda r, i: (0, 0))]
    args = [q, k, k, v, v, bias]
    if has_sink:
        in_specs = [pl.BlockSpec(memory_space=pltpu.SMEM)] + in_specs
        args = [sink] + args
    scratch = [pltpu.VMEM((tq + BLOCK, wk), BF16)] * (4 if gqa else 2)
    return pl.pallas_call(
        body, name=name, grid=(dil, nt), in_specs=in_specs,
        out_specs=[main(wq), main(LANES)],
        out_shape=[jax.ShapeDtypeStruct((dil, length, wq), BF16), jax.ShapeDtypeStruct((dil, length, LANES), F32)],
        scratch_shapes=scratch,
    )(*args)


def _band_bwd(q, k, v, do, lse, delta, *, max_dist, name):
    dil, length, wq = q.shape
    wk = k.shape[2]
    gqa = wk != wq
    tq = min(512, length)
    ns, nt = tq // BLOCK, length // tq
    npair = wq // LANES
    nblocks = length // BLOCK
    bias = _band_bias(max_dist, transposed=True)

    def body(q_ref, qn_ref, do_ref, don_ref, lse_ref, lsen_ref, dl_ref, dln_ref, k_ref, v_ref, bias_ref,
             dq_ref, dk_ref, dv_ref, qbuf, dobuf, stat_l, stat_d, dqacc, *rolled):
        i = pl.program_id(1)
        qbuf[0:tq] = q_ref[...]
        qbuf[tq:] = qn_ref[...]
        dobuf[0:tq] = do_ref[...]
        dobuf[tq:] = don_ref[...]
        if gqa:
            kroll, vroll = rolled
            kroll[...] = pltpu.roll(k_ref[...], HEAD_DIM, 1)
            vroll[...] = pltpu.roll(v_ref[...], HEAD_DIM, 1)
        for a in range(ns):
            rows = slice(a * BLOCK, (a + 1) * BLOCK)
            stat_l[a] = _rows_to_lanes(lse_ref[rows, :])
            stat_d[a] = _rows_to_lanes(dl_ref[rows, :])
        stat_l[ns] = _rows_to_lanes(lsen_ref[...])
        stat_d[ns] = _rows_to_lanes(dln_ref[...])

        @pl.when(i == 0)
        def _():
            dqacc[0:BLOCK] = jnp.zeros((BLOCK, wq), F32)

        @pl.when(i > 0)
        def _():
            dqacc[0:BLOCK] = dqacc[tq:tq + BLOCK]

        dqacc[BLOCK:] = jnp.zeros((tq, wq), F32)
        half = _half_masks(BLOCK)
        half2 = _half_masks(2 * BLOCK)
        col_next = (lax.broadcasted_iota(jnp.int32, (1, 2 * BLOCK), 1) >= BLOCK).astype(F32)

        def sub(b, carry):
            r0 = pl.multiple_of(b * BLOCK, BLOCK)
            pen = jnp.where((i == nt - 1) & (b == ns - 1), NEG, 0.0)
            bt = bias_ref[...] + pen * col_next
            acc = {}
            for p in range(npair):
                lanes = slice(p * LANES, (p + 1) * LANES)
                qw = qbuf[pl.ds(r0, 2 * BLOCK), lanes]
                dow = dobuf[pl.ds(r0, 2 * BLOCK), lanes]
                for e in range(2):
                    h = 2 * p + e
                    pk, ek = _kv_place(h, gqa)
                    klanes = slice(pk * LANES, (pk + 1) * LANES)
                    kb = (k_ref if ek == e else kroll)[pl.ds(r0, BLOCK), klanes]
                    vb = (v_ref if ek == e else vroll)[pl.ds(r0, BLOCK), klanes]
                    qm = jnp.where(half2[e], qw, jnp.zeros_like(qw))
                    dom = jnp.where(half2[e], dow, jnp.zeros_like(dow))
                    lrow = jnp.concatenate([stat_l[b, h:h + 1, :], stat_l[b + 1, h:h + 1, :]], axis=1)
                    drow = jnp.concatenate([stat_d[b, h:h + 1, :], stat_d[b + 1, h:h + 1, :]], axis=1)
                    st = _dot_nt(kb, qm) + bt
                    pt = jnp.exp(st - lrow)
                    dv_c = _dot(pt.astype(BF16), dom)
                    dpt = _dot_nt(vb, dom)
                    dsb = (pt * (dpt - drow)).astype(BF16)
                    dk_c = _dot(dsb, qm)
                    kbm = jnp.where(half[e], kb, jnp.zeros_like(kb))
                    dqacc[pl.ds(r0, 2 * BLOCK), lanes] += _dot_tn(dsb, kbm)
                    key = (pk, ek == e)
                    if key in acc:
                        acc[key] = (acc[key][0] + dk_c, acc[key][1] + dv_c)
                    else:
                        acc[key] = (dk_c, dv_c)
                if not gqa:
                    dk_ref[pl.ds(r0, BLOCK), lanes] = acc[(p, True)][0].astype(BF16)
                    dv_ref[pl.ds(r0, BLOCK), lanes] = acc[(p, True)][1].astype(BF16)
            if gqa:
                dk_al, dv_al = acc[(0, True)]
                dk_mis, dv_mis = acc[(0, False)]
                dk_ref[pl.ds(r0, BLOCK), :] = (dk_al + pltpu.roll(dk_mis, HEAD_DIM, 1)).astype(BF16)
                dv_ref[pl.ds(r0, BLOCK), :] = (dv_al + pltpu.roll(dv_mis, HEAD_DIM, 1)).astype(BF16)
            return carry

        lax.fori_loop(0, ns, sub, 0)
        dq_ref[...] = dqacc[0:tq].astype(BF16)

    main = lambda w: pl.BlockSpec((None, tq, w), lambda r, i: (r, i, 0))
    nxt = lambda w: pl.BlockSpec((None, BLOCK, w), lambda r, i: (r, jnp.minimum((i + 1) * ns, nblocks - 1), 0))
    scratch = [pltpu.VMEM((tq + BLOCK, wq), BF16), pltpu.VMEM((tq + BLOCK, wq), BF16),
               pltpu.VMEM((ns + 1, 8, LANES), F32), pltpu.VMEM((ns + 1, 8, LANES), F32),
               pltpu.VMEM((tq + BLOCK, wq), F32)]
    if gqa:
        scratch = scratch + [pltpu.VMEM((tq, wk), BF16)] * 2
    return pl.pallas_call(
        body, name=name, grid=(dil, nt),
        in_specs=[main(wq), nxt(wq), main(wq), nxt(wq), main(LANES), nxt(LANES), main(LANES), nxt(LANES),
                  main(wk), main(wk), pl.BlockSpec(bias.shape, lambda r, i: (0, 0))],
        out_specs=[main(wq), main(wk), main(wk)],
        out_shape=[jax.ShapeDtypeStruct((dil, length, wq), BF16), jax.ShapeDtypeStruct((dil, length, wk), BF16),
                   jax.ShapeDtypeStruct((dil, length, wk), BF16)],
        scratch_shapes=scratch,
        compiler_params=pltpu.CompilerParams(dimension_semantics=("arbitrary", "arbitrary")),
    )(q, q, do, do, lse, lse, delta, delta, k, v, bias)


def _mem_attn_fwd(q, mk, mv):
    seq = q.shape[0]
    tq = min(512, seq)
    ns = tq // BLOCK

    def body(q_ref, mk_ref, mv_ref, o_ref, lse_ref):
        half = _half_masks(BLOCK)

        def sub(a, carry):
            r0 = pl.multiple_of(a * BLOCK, BLOCK)
            lse_cols = []
            for p in range(C_W // LANES):
                lanes = slice(p * LANES, (p + 1) * LANES)
                qp = q_ref[pl.ds(r0, BLOCK), lanes]
                o_h = []
                for e in range(2):
                    qm = jnp.where(half[e], qp, jnp.zeros_like(qp))
                    s = _dot_nt(qm, mk_ref[:, lanes])
                    m = jnp.max(s, axis=1, keepdims=True)
                    pe = jnp.exp(s - m)
                    l = jnp.sum(pe, axis=1, keepdims=True)
                    o_h.append(_dot(pe.astype(BF16), mv_ref[:, lanes]) * (1.0 / l))
                    lse_cols.append(m + jnp.log(l))
                o_ref[pl.ds(r0, BLOCK), lanes] = jnp.where(half[0], o_h[0], o_h[1]).astype(BF16)
            lse_ref[pl.ds(r0, BLOCK), :] = _per_head(lse_cols)
            return carry

        lax.fori_loop(0, ns, sub, 0)

    row = lambda w: pl.BlockSpec((tq, w), lambda i: (i, 0))
    full = pl.BlockSpec((N_MEM, C_W), lambda i: (0, 0))
    return pl.pallas_call(
        body, name="mem_attn_fwd", grid=(seq // tq,), in_specs=[row(C_W), full, full],
        out_specs=[row(C_W), row(LANES)],
        out_shape=[jax.ShapeDtypeStruct((seq, C_W), BF16), jax.ShapeDtypeStruct((seq, LANES), F32)],
    )(q, mk, mv)


def _mem_attn_bwd(q, mk, mv, do, lse, delta):
    seq = q.shape[0]
    tq = min(512, seq)
    ns = tq // BLOCK
    npair = C_W // LANES

    def body(q_ref, mk_ref, mv_ref, do_ref, lse_ref, dl_ref, dq_ref, dmk_ref, dmv_ref, stat_l, stat_d):
        @pl.when(pl.program_id(0) == 0)
        def _():
            dmk_ref[...] = jnp.zeros_like(dmk_ref)
            dmv_ref[...] = jnp.zeros_like(dmv_ref)

        for a in range(ns):
            rows = slice(a * BLOCK, (a + 1) * BLOCK)
            stat_l[a] = _rows_to_lanes(lse_ref[rows, :])
            stat_d[a] = _rows_to_lanes(dl_ref[rows, :])
        half = _half_masks(BLOCK)
        halfk = _half_masks(N_MEM)

        def sub(a, carry):
            r0 = pl.multiple_of(a * BLOCK, BLOCK)
            for p in range(npair):
                lanes = slice(p * LANES, (p + 1) * LANES)
                qp = q_ref[pl.ds(r0, BLOCK), lanes]
                dop = do_ref[pl.ds(r0, BLOCK), lanes]
                kb, vb = mk_ref[:, lanes], mv_ref[:, lanes]
                dq_pair = None
                for e in range(2):
                    h = 2 * p + e
                    qm = jnp.where(half[e], qp, jnp.zeros_like(qp))
                    dom = jnp.where(half[e], dop, jnp.zeros_like(dop))
                    st = _dot_nt(kb, qm)
                    pt = jnp.exp(st - stat_l[a, h:h + 1, :])
                    dmv_ref[:, lanes] += _dot(pt.astype(BF16), dom)
                    dpt = _dot_nt(vb, dom)
                    dsb = (pt * (dpt - stat_d[a, h:h + 1, :])).astype(BF16)
                    dmk_ref[:, lanes] += _dot(dsb, qm)
                    kbm = jnp.where(halfk[e], kb, jnp.zeros_like(kb))
                    dq_c = _dot_tn(dsb, kbm)
                    dq_pair = dq_c if dq_pair is None else dq_pair + dq_c
                dq_ref[pl.ds(r0, BLOCK), lanes] = dq_pair.astype(BF16)
            return carry

        lax.fori_loop(0, ns, sub, 0)

    row = lambda w: pl.BlockSpec((tq, w), lambda i: (i, 0))
    full = pl.BlockSpec((N_MEM, C_W), lambda i: (0, 0))
    return pl.pallas_call(
        body, name="mem_attn_bwd", grid=(seq // tq,),
        in_specs=[row(C_W), full, full, row(C_W), row(LANES), row(LANES)], out_specs=[row(C_W), full, full],
        out_shape=[jax.ShapeDtypeStruct((seq, C_W), BF16), jax.ShapeDtypeStruct((N_MEM, C_W), F32),
                   jax.ShapeDtypeStruct((N_MEM, C_W), F32)],
        scratch_shapes=[pltpu.VMEM((ns, 8, LANES), F32)] * 2,
        compiler_params=pltpu.CompilerParams(dimension_semantics=("arbitrary",)),
    )(q, mk, mv, do, lse, delta)


def _silu_and_grad(g):
    s = 1.0 / (1.0 + jnp.exp(-g))
    return g * s, s * (1.0 + g * (1.0 - s))


def _post(x, target, post_norm, w_out, sink_row, oa, lse_a, ga, ob_list, lseb_list, gb, oc, gc):
    seq = x.shape[0]
    tm = min(256, seq)
    inv_d = 1.0 / D_MODEL
    nd = len(B_DILS)

    def body(*refs):
        (x_ref, t_ref, gp_ref, w_ref, sink_ref, oa_ref, lsea_ref, ga_ref), refs = refs[:8], refs[8:]
        ob_refs, lb_refs, (gb_ref, oc_ref, gc_ref), refs = refs[:nd], refs[nd:2 * nd], refs[2 * nd:2 * nd + 3], refs[2 * nd + 3:]
        (g_ref, doa_ref, dla_ref, dga_ref), refs = refs[:4], refs[4:]
        dob_refs, lsec_refs, dlb_refs, refs = refs[:nd], refs[nd:2 * nd], refs[2 * nd:3 * nd], refs[3 * nd:]
        (dgb_ref, doc_ref, dlc_ref, dgc_ref, gw_ref, gpost_ref, gsink_ref, loss_ref), refs = refs[:8], refs[8:]
        ycat, obufs, lbufs, st_do, st_l, st_d = refs[0], refs[1:nd], refs[nd:2 * nd - 1], refs[2 * nd - 1], refs[2 * nd], refs[2 * nd + 1]

        @pl.when(pl.program_id(0) == 0)
        def _():
            gw_ref[...] = jnp.zeros_like(gw_ref)
            gpost_ref[...] = jnp.zeros_like(gpost_ref)
            gsink_ref[...] = jnp.zeros_like(gsink_ref)
            loss_ref[...] = jnp.zeros_like(loss_ref)

        o_i, l_i = [ob_refs[0][0].astype(F32)], [lb_refs[0][0]]
        for k in range(1, nd):
            _from_residues(ob_refs[k], obufs[k - 1], B_DILS[k])
            _from_residues(lb_refs[k], lbufs[k - 1], B_DILS[k])
            o_i.append(_stage_read(obufs[k - 1]))
            l_i.append(_stage_read(lbufs[k - 1]))
        mx = l_i[0]
        for l in l_i[1:]:
            mx = jnp.maximum(mx, l)
        w_i = [jnp.exp(l - mx) for l in l_i]
        z = w_i[0]
        for w in w_i[1:]:
            z = z + w
        lse_b = mx + jnp.log(z)
        expand = _head_expand_matrix(B_W)
        inv_z = 1.0 / z
        ob = None
        for w, o in zip(w_i, o_i):
            term = _dot_split(w * inv_z, expand, 2) * o
            ob = term if ob is None else ob + term
        oa, oc = oa_ref[...].astype(F32), oc_ref[...].astype(F32)
        sa, dsa = _silu_and_grad(ga_ref[...].astype(F32))
        sb, dsb = _silu_and_grad(gb_ref[...].astype(F32))
        sc, dsc = _silu_and_grad(gc_ref[...].astype(F32))
        ycat[:, 0:A_W] = (oa * sa).astype(BF16)
        ycat[:, A_W:A_W + B_W] = (ob * sb).astype(BF16)
        ycat[:, A_W + B_W:] = (oc * sc).astype(BF16)
        yc = ycat[...]
        y2 = _dot(yc, w_ref[...])
        r = lax.rsqrt(jnp.mean(y2 * y2, axis=-1, keepdims=True) + RMS_EPS)
        zhat = y2 * r
        gp = gp_ref[...]
        err = x_ref[...] + zhat * gp - t_ref[...]
        loss_ref[...] += jnp.sum(err * err) * (0.5 * inv_d)
        g = err * inv_d
        g_ref[...] = g
        gpost_ref[...] += jnp.sum(g * zhat, axis=0, keepdims=True)
        a = g * gp
        dy2 = (r * (a - zhat * jnp.mean(a * zhat, axis=-1, keepdims=True))).astype(BF16)
        gw_ref[...] += _dot_tn(yc, dy2)
        dycat = _dot_nt(dy2, w_ref[...])
        dya, dyb, dyc = dycat[:, 0:A_W], dycat[:, A_W:A_W + B_W], dycat[:, A_W + B_W:]
        doa, dob, doc = dya * sa, dyb * sb, dyc * sc
        doa_ref[...] = doa.astype(BF16)
        doc_ref[...] = doc.astype(BF16)
        dga_ref[...] = (dya * oa * dsa).astype(BF16)
        dgb_ref[...] = (dyb * ob * dsb).astype(BF16)
        dgc_ref[...] = (dyc * oc * dsc).astype(BF16)
        dl_a = _dot_split(doa * oa, _head_sum_matrix(A_W), 3)
        dla_ref[...] = dl_a
        dlc_ref[...] = _dot_split(doc * oc, _head_sum_matrix(C_W), 3)
        gsink_ref[...] += jnp.sum(jnp.exp(sink_ref[...] - lsea_ref[...]) * dl_a, axis=0, keepdims=True)
        _stage_write(st_do, dob)
        _stage_write(st_l, lse_b)
        _stage_write(st_d, _dot_split(dob * ob, _head_sum_matrix(B_W), 3))
        for k, dil in enumerate(B_DILS):
            _to_residues(st_do, dob_refs[k], dil)
            _to_residues(st_l, lsec_refs[k], dil)
            _to_residues(st_d, dlb_refs[k], dil)

    row = lambda w: pl.BlockSpec((tm, w), lambda i: (i, 0))
    full = lambda shape: pl.BlockSpec(shape, lambda i: (0,) * len(shape))
    res_specs = lambda w: [_residue_spec(d, tm, w) for d in B_DILS]
    res_shapes = lambda w, dt: [jax.ShapeDtypeStruct((d, seq // d, w), dt) for d in B_DILS]
    ins = [x, target, post_norm, w_out, sink_row, oa, lse_a, ga, *ob_list, *lseb_list, gb, oc, gc]
    in_specs = ([row(D_MODEL), row(D_MODEL), full((1, D_MODEL)), full((D_MODEL, D_MODEL)), full((1, LANES)),
                 row(A_W), row(LANES), row(A_W)] + res_specs(B_W) + res_specs(LANES) + [row(B_W), row(C_W), row(C_W)])
    out_shape = ([jax.ShapeDtypeStruct((seq, D_MODEL), F32), jax.ShapeDtypeStruct((seq, A_W), BF16),
                  jax.ShapeDtypeStruct((seq, LANES), F32), jax.ShapeDtypeStruct((seq, A_W), BF16)]
                 + res_shapes(B_W, BF16) + res_shapes(LANES, F32) + res_shapes(LANES, F32)
                 + [jax.ShapeDtypeStruct((seq, B_W), BF16), jax.ShapeDtypeStruct((seq, C_W), BF16),
                    jax.ShapeDtypeStruct((seq, LANES), F32), jax.ShapeDtypeStruct((seq, C_W), BF16),
                    jax.ShapeDtypeStruct((D_MODEL, D_MODEL), F32), jax.ShapeDtypeStruct((1, D_MODEL), F32),
                    jax.ShapeDtypeStruct((1, LANES), F32), jax.ShapeDtypeStruct((1, LANES), F32)])
    out_specs = ([row(D_MODEL), row(A_W), row(LANES), row(A_W)] + res_specs(B_W) + res_specs(LANES) + res_specs(LANES)
                 + [row(B_W), row(C_W), row(LANES), row(C_W),
                    full((D_MODEL, D_MODEL)), full((1, D_MODEL)), full((1, LANES)), full((1, LANES))])
    scratch = ([pltpu.VMEM((tm, D_MODEL), BF16)] + [_stage(tm, B_W)] * (nd - 1) + [_stage(tm, LANES)] * (nd - 1)
               + [_stage(tm, B_W), _stage(tm, LANES), _stage(tm, LANES)])
    res = pl.pallas_call(
        body, name="post", grid=(seq // tm,), in_specs=in_specs, out_specs=out_specs, out_shape=out_shape,
        scratch_shapes=scratch,
        compiler_params=pltpu.CompilerParams(dimension_semantics=("arbitrary",)),
    )(*ins)
    out = dict(g=res[0], doa=res[1], dl_a=res[2], dga=res[3], dob=res[4:4 + nd], lse_b=res[4 + nd:4 + 2 * nd],
               dl_b=res[4 + 2 * nd:4 + 3 * nd])
    rest = res[4 + 3 * nd:]
    out.update(dgb=rest[0], doc=rest[1], dl_c=rest[2], dgc=rest[3], gw_out=rest[4], gpost=rest[5], gsink=rest[6],
               loss=rest[7])
    return out


def _dx(x, g, pre_norm, w_in_g, rope, nat, res):
    seq = x.shape[0]
    tm = min(256, seq)
    nd = len(B_DILS)
    nat_list = [nat[n] for n in _NATURAL]
    res_list = [a for n in _DILATED for a in res[n]]

    def body(x_ref, g_ref, gp_ref, w_ref, c_ref, sm_ref, sp_ref, *refs):
        nat_refs = dict(zip(_NATURAL, refs[:len(_NATURAL)]))
        refs = refs[len(_NATURAL):]
        res_refs = {n: refs[nd * k:nd * (k + 1)] for k, n in enumerate(_DILATED)}
        refs = refs[nd * len(_DILATED):]
        dproj_ref, gx_ref, gpre_ref = refs[:3]
        bufs = {n: refs[3 + (nd - 1) * k:3 + (nd - 1) * (k + 1)] for k, n in enumerate(_DILATED)}

        @pl.when(pl.program_id(0) == 0)
        def _():
            gpre_ref[...] = jnp.zeros_like(gpre_ref)

        for n in _DILATED:
            for k in range(1, nd):
                _from_residues(res_refs[n][k], bufs[n][k - 1], B_DILS[k])
        c, sm, sp = c_ref[...], -sm_ref[...], -sp_ref[...]
        for blk, (name, off, roped, scaled) in enumerate(_PROJ_LAYOUT):
            lanes = slice(off, off + LANES)
            if name in nat_refs:
                piece = nat_refs[name][:, lanes].astype(F32)
            else:
                piece = res_refs[name][0][0, :, lanes].astype(F32)
                for buf in bufs[name]:
                    piece = piece + buf[off // LANES]
            if roped:
                piece = _rope(piece, c, sm, sp)
            if scaled:
                piece = piece * SCALE
            dproj_ref[:, blk * LANES:(blk + 1) * LANES] = piece.astype(BF16)
        du = None
        for j in range(N_CHIPS):
            part = _dot_nt(dproj_ref[:, j * SHARD_IN:(j + 1) * SHARD_IN], w_ref[j])
            du = part if du is None else du + part
        xv = x_ref[...]
        r = lax.rsqrt(jnp.mean(xv * xv, axis=-1, keepdims=True) + RMS_EPS)
        xhat = xv * r
        gpre_ref[...] += jnp.sum(du * xhat, axis=0, keepdims=True)
        a = du * gp_ref[...]
        gx_ref[...] = g_ref[...] + r * (a - xhat * jnp.mean(a * xhat, axis=-1, keepdims=True))

    row = lambda w: pl.BlockSpec((tm, w), lambda i: (i, 0))
    full = lambda a: pl.BlockSpec(a.shape, lambda i: (0,) * a.ndim)
    in_specs = ([row(D_MODEL), row(D_MODEL), full(pre_norm), full(w_in_g), row(LANES), row(LANES), row(LANES)]
                + [row(a.shape[1]) for a in nat_list]
                + [_residue_spec(d, tm, B_W) for _ in _DILATED for d in B_DILS])
    return pl.pallas_call(
        body, name="dx", grid=(seq // tm,), in_specs=in_specs,
        out_specs=[row(D_IN), row(D_MODEL), pl.BlockSpec((1, D_MODEL), lambda i: (0, 0))],
        out_shape=[jax.ShapeDtypeStruct((seq, D_IN), BF16), jax.ShapeDtypeStruct((seq, D_MODEL), F32),
                   jax.ShapeDtypeStruct((1, D_MODEL), F32)],
        scratch_shapes=[_stage(tm, B_W)] * ((nd - 1) * len(_DILATED)),
        compiler_params=pltpu.CompilerParams(dimension_semantics=("arbitrary",)),
    )(x, g, pre_norm, w_in_g, *rope, *nat_list, *res_list)


def _grad_w_in(ut, dproj):
    seq = ut.shape[1]
    tk = min(1024, seq)

    def body(ut_ref, dp_ref, out_ref):
        @pl.when(pl.program_id(1) == 0)
        def _():
            out_ref[...] = jnp.zeros_like(out_ref)

        out_ref[...] += _dot(ut_ref[...], dp_ref[...])

    return pl.pallas_call(
        body, name="grad_w_in", grid=(N_CHIPS, seq // tk),
        in_specs=[pl.BlockSpec((D_MODEL, tk), lambda j, i: (0, i)), pl.BlockSpec((tk, SHARD_IN), lambda j, i: (i, j))],
        out_specs=pl.BlockSpec((None, D_MODEL, SHARD_IN), lambda j, i: (j, 0, 0)),
        out_shape=jax.ShapeDtypeStruct((N_CHIPS, D_MODEL, SHARD_IN), F32),
        compiler_params=pltpu.CompilerParams(dimension_semantics=("arbitrary", "arbitrary")),
    )(ut, dproj)


def _pair_exchange(grads):
    n = len(grads)

    def body(*refs):
        srcs, outs = refs[:n], refs[n:2 * n]
        send_sems, recv_sems = refs[2 * n:]
        x, y, c = lax.axis_index("x"), lax.axis_index("y"), lax.axis_index("c")
        copies = []
        for t in range(n):
            rows = grads[t].shape[1] // 2
            copies.append(pltpu.make_async_remote_copy(
                src_ref=srcs[t].at[:, pl.ds((1 - c) * rows, rows)], dst_ref=outs[t],
                send_sem=send_sems.at[t], recv_sem=recv_sems.at[t], device_id=(x, y, 1 - c), device_id_type=MESH))
        for cp in copies:
            cp.start()
        for cp in copies:
            cp.wait()

    any_spec = pl.BlockSpec(memory_space=pl.ANY)
    return pl.pallas_call(
        body, name="pair_exchange",
        out_shape=[jax.ShapeDtypeStruct((g.shape[0], g.shape[1] // 2, g.shape[2]), g.dtype) for g in grads],
        in_specs=[any_spec] * n, out_specs=[any_spec] * n,
        scratch_shapes=[pltpu.SemaphoreType.DMA((n,)), pltpu.SemaphoreType.DMA((n,))],
    )(*grads)


def _pair_add(core, own, got):
    nchip, rows2, width = own.shape
    rows = rows2 // 2
    tr = min(128, rows)
    nb = rows // tr

    def body(core_ref, own_ref, got_ref, out_ref):
        out_ref[...] = own_ref[...] + got_ref[...]

    grid_spec = pltpu.PrefetchScalarGridSpec(
        num_scalar_prefetch=1, grid=(nchip, nb),
        in_specs=[pl.BlockSpec((None, tr, width), lambda k, i, core_ref: (k, core_ref[0] * nb + i, 0)),
                  pl.BlockSpec((None, tr, width), lambda k, i, core_ref: (k, i, 0))],
        out_specs=pl.BlockSpec((None, tr, width), lambda k, i, core_ref: (k, i, 0)))
    return pl.pallas_call(
        body, name=f"pair_add_{width}", grid_spec=grid_spec,
        out_shape=jax.ShapeDtypeStruct((nchip, rows, width), own.dtype),
    )(core, own, got)


def _chip_exchange(parts, small):
    n = len(parts)

    def body(*refs):
        srcs, small_ref = refs[:n], refs[n]
        outs, small_out = refs[n + 1:2 * n + 1], refs[2 * n + 1]
        send_sems, recv_sems, local_sems = refs[2 * n + 2:]
        x, y, c = lax.axis_index("x"), lax.axis_index("y"), lax.axis_index("c")
        my_chip = 2 * x + y
        me = 4 * x + 2 * y + c
        chips = [(1 - x, y), (x, 1 - y), (1 - x, 1 - y)]
        local = [pltpu.make_async_copy(srcs[t].at[my_chip], outs[t].at[my_chip], local_sems.at[t]) for t in range(n)]
        local.append(pltpu.make_async_copy(small_ref, small_out.at[me], local_sems.at[n]))
        for cp in local:
            cp.start()
        sent = []
        for j, (cx, cy) in enumerate(chips):
            for t in range(n):
                k = n * j + t
                sent.append(pltpu.make_async_remote_copy(
                    src_ref=srcs[t].at[2 * cx + cy], dst_ref=outs[t].at[my_chip], send_sem=send_sems.at[k],
                    recv_sem=recv_sems.at[k], device_id=(cx, cy, c), device_id_type=MESH))
        peers = [(x, y, 1 - c)] + [(cx, cy, cc) for (cx, cy) in chips for cc in (c, 1 - c)]
        for j, peer in enumerate(peers):
            k = 3 * n + j
            sent.append(pltpu.make_async_remote_copy(
                src_ref=small_ref, dst_ref=small_out.at[me], send_sem=send_sems.at[k], recv_sem=recv_sems.at[k],
                device_id=peer, device_id_type=MESH))
        for cp in sent:
            cp.start()
        for cp in sent:
            cp.wait()
        for cp in local:
            cp.wait()

    any_spec = pl.BlockSpec(memory_space=pl.ANY)
    nsem = 3 * n + 7
    return pl.pallas_call(
        body, name="chip_exchange",
        out_shape=[jax.ShapeDtypeStruct(p.shape, p.dtype) for p in parts]
        + [jax.ShapeDtypeStruct((8,) + small.shape, small.dtype)],
        in_specs=[any_spec] * (n + 1), out_specs=[any_spec] * (n + 1),
        scratch_shapes=[pltpu.SemaphoreType.DMA((nsem,)), pltpu.SemaphoreType.DMA((nsem,)),
                        pltpu.SemaphoreType.DMA((n + 1,))],
    )(*parts, small)


def _slot_sum(slots, name):
    ns, rows, width = slots.shape
    tr = min(128, rows)

    def body(in_ref, out_ref):
        acc = in_ref[0]
        for s in range(1, ns):
            acc = acc + in_ref[s]
        out_ref[...] = acc

    return pl.pallas_call(
        body, name=name, grid=(rows // tr,),
        in_specs=[pl.BlockSpec((ns, tr, width), lambda i: (0, i, 0))],
        out_specs=pl.BlockSpec((tr, width), lambda i: (i, 0)),
        out_shape=jax.ShapeDtypeStruct((rows, width), slots.dtype),
    )(slots)


def _pair_gather(halves):
    n = len(halves)

    def body(*refs):
        srcs, outs = refs[:n], refs[n:2 * n]
        send_sems, recv_sems, local_sems = refs[2 * n:]
        x, y, c = lax.axis_index("x"), lax.axis_index("y"), lax.axis_index("c")
        copies, local = [], []
        for t in range(n):
            rows = halves[t].shape[0]
            mine = outs[t].at[pl.ds(c * rows, rows)]
            local.append(pltpu.make_async_copy(srcs[t], mine, local_sems.at[t]))
            copies.append(pltpu.make_async_remote_copy(
                src_ref=srcs[t], dst_ref=mine, send_sem=send_sems.at[t], recv_sem=recv_sems.at[t],
                device_id=(x, y, 1 - c), device_id_type=MESH))
        for cp in local + copies:
            cp.start()
        for cp in copies:
            cp.wait()
        for cp in local:
            cp.wait()

    any_spec = pl.BlockSpec(memory_space=pl.ANY)
    return pl.pallas_call(
        body, name="pair_gather",
        out_shape=[jax.ShapeDtypeStruct((2 * h.shape[0], h.shape[1]), h.dtype) for h in halves],
        in_specs=[any_spec] * n, out_specs=[any_spec] * n,
        scratch_shapes=[pltpu.SemaphoreType.DMA((n,)), pltpu.SemaphoreType.DMA((n,)), pltpu.SemaphoreType.DMA((n,))],
    )(*halves)


def _adamw(w, g, m, v, name):
    rows, width = w.shape
    tr = min(256, rows)
    c1 = 1.0 / (1.0 - ADAM_B1 ** ADAM_STEP)
    c2 = 1.0 / (1.0 - ADAM_B2 ** ADAM_STEP)

    def body(w_ref, g_ref, m_ref, v_ref, d_ref, nm_ref, nv_ref):
        gv = g_ref[...]
        nm = ADAM_B1 * m_ref[...] + (1.0 - ADAM_B1) * gv
        nv = ADAM_B2 * v_ref[...] + (1.0 - ADAM_B2) * (gv * gv)
        nm_ref[...] = nm
        nv_ref[...] = nv
        d_ref[...] = -ADAM_LR * ((nm * c1) / (jnp.sqrt(nv * c2) + ADAM_EPS) + ADAM_WD * w_ref[...])

    spec = pl.BlockSpec((tr, width), lambda i: (i, 0))
    return pl.pallas_call(
        body, name=name, grid=(rows // tr,), in_specs=[spec] * 4, out_specs=[spec] * 3,
        out_shape=[jax.ShapeDtypeStruct(w.shape, F32)] * 3,
    )(w, g, m, v)


def _local_step(x, mem, target, pre_norm, sink_a, mem_norm, post_norm, w_in_g, w_out, w_mkv):
    seq = x.shape[0]
    rope = _rope_tables(seq)
    mk, mv = _mem_kv(mem, mem_norm, w_mkv)
    pr = _pre_proj(x, pre_norm, w_in_g, rope)
    sink = sink_a.reshape(-1)
    qa, ka, va = pr["qa"][None], pr["ka"][None], pr["va"][None]
    oa, lse_a = _band_fwd(qa, ka, va, sink, max_dist=A_WINDOW - 1, name="swa_fwd")
    ob_list, lseb_list = [], []
    for k, (win, dil) in enumerate(B_CONFIGS):
        o_i, l_i = _band_fwd(pr["qb"][k], pr["kb"][k], pr["vb"][k], None, max_dist=win // dil, name=f"dil{dil}_fwd")
        ob_list.append(o_i)
        lseb_list.append(l_i)
    oc, lse_c = _mem_attn_fwd(pr["qc"], mk, mv)
    sink_row = jnp.pad(sink, (0, LANES - sink.shape[0])).reshape(1, LANES)
    po = _post(x, target, post_norm, w_out, sink_row, oa[0], lse_a[0], pr["ga"], ob_list, lseb_list, pr["gb"], oc,
               pr["gc"])
    dqc, dmk, dmv = _mem_attn_bwd(pr["qc"], mk, mv, po["doc"], lse_c, po["dl_c"])
    dqa, dka, dva = _band_bwd(qa, ka, va, po["doa"][None], lse_a, po["dl_a"][None], max_dist=A_WINDOW - 1,
                              name="swa_bwd")
    res = dict(qb=[], kb=[], vb=[])
    for k, (win, dil) in enumerate(B_CONFIGS):
        dq_i, dk_i, dv_i = _band_bwd(pr["qb"][k], pr["kb"][k], pr["vb"][k], po["dob"][k], po["lse_b"][k],
                                     po["dl_b"][k], max_dist=win // dil, name=f"dil{dil}_bwd")
        res["qb"].append(dq_i)
        res["kb"].append(dk_i)
        res["vb"].append(dv_i)
    nat = dict(qa=dqa[0], ka=dka[0], va=dva[0], ga=po["dga"], gb=po["dgb"], qc=dqc, gc=po["dgc"])
    dproj, grad_x, gpre = _dx(x, po["g"], pre_norm, w_in_g, rope, nat, res)
    gw_in = _grad_w_in(pr["ut"], dproj)
    gw_mkv, gmem = _mem_kv_bwd(mem, mem_norm, w_mkv, dmk, dmv)
    gsink = -po["gsink"][0, :sink.shape[0]]
    return dict(loss=po["loss"][0, 0], grad_x=grad_x, gw_in=gw_in, gw_out=po["gw_out"], gw_mkv=gw_mkv,
                gpre=gpre, gpost=po["gpost"], gmem=gmem, gsink=gsink)


def kernel(x, mem, pre_norm, w_in, sink_a, mem_norm, w_mem_kv, w_out, post_norm, loss_target, m_pre_norm, m_w_in, m_sink_a, m_mem_norm, m_w_mem_kv, m_w_out, m_post_norm, v_pre_norm, v_w_in, v_sink_a, v_mem_norm, v_w_mem_kv, v_w_out, v_post_norm):
    w_in_g, w_out_g, w_mkv_g = _gather_weights(w_in[0].astype(BF16), w_out[0].astype(BF16), w_mem_kv[0].astype(BF16))
    loc = _local_step(x[0], mem[0], loss_target[0], pre_norm, sink_a, mem_norm, post_norm,
                      w_in_g, w_out_g.reshape(D_MODEL, D_MODEL), w_mkv_g.reshape(D_MODEL, 2 * C_W))
    loss = lax.psum(loc["loss"], ("x", "y", "c"))

    big = [loc["gw_in"], loc["gw_out"].reshape(N_CHIPS, D_MODEL // N_CHIPS, D_MODEL),
           loc["gw_mkv"].reshape(N_CHIPS, D_MODEL // N_CHIPS, 2 * C_W)]
    small = jnp.concatenate([loc["gpre"], loc["gpost"], loc["gmem"],
                             jnp.pad(loc["gsink"], (0, D_MODEL - loc["gsink"].shape[0])).reshape(1, D_MODEL),
                             jnp.zeros((4, D_MODEL), F32)], axis=0)
    core = lax.axis_index("c").astype(jnp.int32).reshape(1)
    got = _pair_exchange(big)
    parts = [_pair_add(core, own, g) for own, g in zip(big, got)]
    *slots, small_slots = _chip_exchange(parts, small)
    halves = [_slot_sum(s, name=f"chip_sum_{s.shape[2]}") for s in slots]
    g_in, g_out, g_mkv = _pair_gather(halves)
    small_sum = _slot_sum(small_slots, name="device_sum")
    g_pre, g_post, g_mem = small_sum[0:1], small_sum[1:2], small_sum[2:3]
    g_sink = small_sum[3:4, :sink_a.shape[1]]

    d_in, nm_in, nv_in = _adamw(w_in[0], g_in, m_w_in[0], v_w_in[0], "adamw_in")
    d_out, nm_out, nv_out = _adamw(w_out[0], g_out, m_w_out[0], v_w_out[0], "adamw_out")
    d_mkv, nm_mkv, nv_mkv = _adamw(w_mem_kv[0], g_mkv, m_w_mem_kv[0], v_w_mem_kv[0], "adamw_mkv")
    pad6 = lambda a: jnp.pad(a, ((0, 0), (0, D_MODEL - a.shape[1])))
    stack = lambda a, b, c_, d_: jnp.concatenate([a, b, c_, pad6(d_), jnp.zeros((4, D_MODEL), F32)], axis=0)
    d_s, nm_s, nv_s = _adamw(stack(pre_norm, post_norm, mem_norm, sink_a), small_sum,
                             stack(m_pre_norm, m_post_norm, m_mem_norm, m_sink_a),
                             stack(v_pre_norm, v_post_norm, v_mem_norm, v_sink_a), "adamw_small")
    ns_ = sink_a.shape[1]
    unpack = lambda a: (a[0:1], a[3:4, :ns_], a[2:3], a[1:2])
    d_pre, d_sink, d_mem, d_post = unpack(d_s)
    nm_pre, nm_sink, nm_mem, nm_post = unpack(nm_s)
    nv_pre, nv_sink, nv_mem, nv_post = unpack(nv_s)
    lead = lambda a: a[None]
    return (loss, lead(loc["grad_x"]),
            g_pre, lead(g_in), g_sink, g_mem, lead(g_mkv), lead(g_out), g_post,
            d_pre, lead(d_in), d_sink, d_mem, lead(d_mkv), lead(d_out), d_post,
            nm_pre, lead(nm_in), nm_sink, nm_mem, lead(nm_mkv), lead(nm_out), nm_post,
            nv_pre, lead(nv_in), nv_sink, nv_mem, lead(nv_mkv), lead(nv_out), nv_post)
```

```python
import numpy as np
import jax
import jax.numpy as jnp
from jax import lax
from jax.experimental import pallas as pl
from jax.experimental.pallas import tpu as pltpu

F32 = jnp.float32
BF16 = jnp.bfloat16

D_MODEL = 1024
HEAD_DIM = 64
LANES = 128
BLOCK = 128
A_W, A_KV_W, B_W, C_W = 384, 128, 384, 256
N_MEM = 256
D_IN = 3072
N_CHIPS = 4
SHARD_IN = D_IN // N_CHIPS
B_CONFIGS = ((128, 1), (512, 4), (2048, 16))
B_DILS = tuple(d for _, d in B_CONFIGS)
A_WINDOW = 128
RMS_EPS = 1e-6
ROPE_THETA = 500000.0
SCALE = HEAD_DIM ** -0.5
NEG = -1e30
ADAM_LR, ADAM_B1, ADAM_B2, ADAM_EPS, ADAM_WD, ADAM_STEP = 0.001, 0.9, 0.999, 1e-08, 0.01, 10

NT = (((1,), (1,)), ((), ()))
TN = (((0,), (0,)), ((), ()))
MESH = pl.DeviceIdType.MESH

_PROJ_LAYOUT = (
    [("qa", 128 * i, True, True) for i in range(3)] + [("ka", 0, True, False), ("va", 0, False, False)]
    + [("ga", 128 * i, False, False) for i in range(3)]
    + [("qb", 128 * i, True, True) for i in range(3)] + [("kb", 128 * i, True, False) for i in range(3)]
    + [("vb", 128 * i, False, False) for i in range(3)] + [("gb", 128 * i, False, False) for i in range(3)]
    + [("qc", 128 * i, False, True) for i in range(2)] + [("gc", 128 * i, False, False) for i in range(2)]
)
_PROJ_WIDTH = dict(qa=A_W, ka=A_KV_W, va=A_KV_W, ga=A_W, qb=B_W, kb=B_W, vb=B_W, gb=B_W, qc=C_W, gc=C_W)
_NATURAL = ("qa", "ka", "va", "ga", "gb", "qc", "gc")
_DILATED = ("qb", "kb", "vb")


def _dot(a, b):
    return jnp.dot(a, b, preferred_element_type=F32)


def _dot_nt(a, b):
    return lax.dot_general(a, b, NT, preferred_element_type=F32)


def _dot_tn(a, b):
    return lax.dot_general(a, b, TN, preferred_element_type=F32)


def _half_masks(rows):
    lane = lax.broadcasted_iota(jnp.int32, (rows, LANES), 1)
    return lane < HEAD_DIM, lane >= HEAD_DIM


def _rope(t, c, sm, sp):
    return t * c + pltpu.roll(t, LANES - 8, 1) * sm + pltpu.roll(t, 8, 1) * sp


def _rope_tables(seq):
    dim = jnp.arange(LANES) % HEAD_DIM
    inv_freq = ROPE_THETA ** (-jnp.arange(0, 16, 2, dtype=F32) / 16)
    ang = jnp.arange(seq, dtype=F32)[:, None] * inv_freq[dim % 8][None, :]
    cos, sin = jnp.cos(ang), jnp.sin(ang)
    c = jnp.where(dim < 16, cos, 1.0)
    sm = jnp.where(dim < 8, -sin, 0.0)
    sp = jnp.where((dim >= 8) & (dim < 16), sin, 0.0)
    return c, sm, sp


def _split3(x):
    a = x.astype(BF16)
    r = x - a.astype(F32)
    b = r.astype(BF16)
    c = (r - b.astype(F32)).astype(BF16)
    return a, b, c


def _rows_to_lanes(x):
    row = lax.broadcasted_iota(jnp.int32, (8, LANES), 0)
    lane = lax.broadcasted_iota(jnp.int32, (8, LANES), 1)
    eye = (row == lane).astype(BF16)
    a, b, c = _split3(x)
    return _dot_nt(eye, a) + _dot_nt(eye, b) + _dot_nt(eye, c)


def _head_sum_matrix(width):
    k = lax.broadcasted_iota(jnp.int32, (width, LANES), 0)
    h = lax.broadcasted_iota(jnp.int32, (width, LANES), 1)
    return (k // HEAD_DIM == h).astype(BF16)


def _head_expand_matrix(width):
    h = lax.broadcasted_iota(jnp.int32, (LANES, width), 0)
    k = lax.broadcasted_iota(jnp.int32, (LANES, width), 1)
    return (k // HEAD_DIM == h).astype(BF16)


def _dot_split(x, mat, terms):
    parts = _split3(x)[:terms]
    out = _dot(parts[0], mat)
    for p in parts[1:]:
        out = out + _dot(p, mat)
    return out


def _per_head(cols):
    rows = cols[0].shape[0]
    lane = lax.broadcasted_iota(jnp.int32, (rows, LANES), 1)
    out = jnp.zeros((rows, LANES), F32)
    for h, col in enumerate(cols):
        out = jnp.where(lane == h, col, out)
    return out


def _lane_blocks(width):
    return [slice(p * LANES, (p + 1) * LANES) for p in range(width // LANES)]


def _stage(rows, width):
    return pltpu.VMEM((width // LANES, rows, LANES), F32)


def _stage_write(buf, value):
    for p, lanes in enumerate(_lane_blocks(value.shape[1])):
        buf[p] = value[:, lanes]


def _stage_read(buf):
    return jnp.concatenate([buf[p] for p in range(buf.shape[0])], axis=1) if buf.shape[0] > 1 else buf[0]


def _to_residues(buf, out_ref, dil):
    rows = buf.shape[1] // dil
    for r in range(dil):
        for p in range(buf.shape[0]):
            plane = buf.at[p]
            out_ref[r, :, p * LANES:(p + 1) * LANES] = plane[pl.ds(r, rows, stride=dil), :].astype(out_ref.dtype)


def _from_residues(in_ref, buf, dil):
    rows = buf.shape[1] // dil
    for r in range(dil):
        for p in range(buf.shape[0]):
            plane = buf.at[p]
            plane[pl.ds(r, rows, stride=dil), :] = in_ref[r, :, p * LANES:(p + 1) * LANES].astype(F32)


def _residue_spec(dil, tm, width):
    return pl.BlockSpec((dil, tm // dil, width), lambda i: (0, i, 0))


def _gather_weights(w_in_s, w_out_s, w_mkv_s):
    shards = tuple(s.reshape(2, s.shape[0] // 2, s.shape[1]) for s in (w_in_s, w_out_s, w_mkv_s))
    n = len(shards)

    def body(*refs):
        srcs, outs = refs[:n], refs[n:2 * n]
        send_sems, recv_sems, local_sems = refs[2 * n:]
        x, y, c = lax.axis_index("x"), lax.axis_index("y"), lax.axis_index("c")
        my_chip = 2 * x + y
        sibling = (x, y, 1 - c)
        chips = [(1 - x, y), (x, 1 - y), (1 - x, 1 - y)]

        def half(t, chip, which):
            return outs[t].at[chip, which]

        def src_half(t, which):
            return srcs[t].at[which]

        def copy(k, src, dst, to):
            return pltpu.make_async_remote_copy(src_ref=src, dst_ref=dst, send_sem=send_sems.at[k],
                                                recv_sem=recv_sems.at[k], device_id=to, device_id_type=MESH)

        own = [pltpu.make_async_copy(srcs[t], outs[t].at[my_chip], local_sems.at[t]) for t in range(n)]
        for cp in own:
            cp.start()
        first = []
        for j, (cx, cy) in enumerate(chips):
            for t in range(n):
                first.append(copy(n * j + t, src_half(t, c), half(t, my_chip, c), (cx, cy, c)))
        for cp in first:
            cp.start()
        passed = []
        for j, (cx, cy) in enumerate(chips):
            chip = 2 * cx + cy
            for t in range(n):
                k = n * j + t
                copy(k, src_half(t, c), half(t, chip, c), (cx, cy, c)).wait_recv()
                fwd = copy(n * 3 + k, half(t, chip, c), half(t, chip, c), sibling)
                fwd.start()
                passed.append(fwd)
        for j, (cx, cy) in enumerate(chips):
            chip = 2 * cx + cy
            for t in range(n):
                k = n * 3 + n * j + t
                copy(k, half(t, chip, 1 - c), half(t, chip, 1 - c), sibling).wait_recv()
        for cp in first + passed:
            cp.wait_send()
        for cp in own:
            cp.wait()

    any_spec = pl.BlockSpec(memory_space=pl.ANY)
    return pl.pallas_call(
        body, name="gather_weights",
        out_shape=[jax.ShapeDtypeStruct((N_CHIPS,) + s.shape, s.dtype) for s in shards],
        in_specs=[any_spec] * n, out_specs=[any_spec] * n,
        scratch_shapes=[pltpu.SemaphoreType.DMA((6 * n,)), pltpu.SemaphoreType.DMA((6 * n,)),
                        pltpu.SemaphoreType.DMA((n,))],
    )(*shards)


def _mem_kv(mem, mem_norm, w_mkv):
    def body(mem_ref, g_ref, w_ref, mk_ref, mv_ref):
        m = mem_ref[...]
        r = lax.rsqrt(jnp.mean(m * m, axis=-1, keepdims=True) + RMS_EPS)
        mn = (m * r * g_ref[...]).astype(BF16)
        kv = _dot(mn, w_ref[...])
        mk_ref[...] = kv[:, :C_W].astype(BF16)
        mv_ref[...] = kv[:, C_W:].astype(BF16)

    return pl.pallas_call(
        body, name="mem_kv",
        out_shape=[jax.ShapeDtypeStruct((N_MEM, C_W), BF16)] * 2,
    )(mem, mem_norm, w_mkv)


def _mem_kv_bwd(mem, mem_norm, w_mkv, dmk, dmv):
    def body(mem_ref, g_ref, w_ref, dmk_ref, dmv_ref, gw_ref, gn_ref):
        m = mem_ref[...]
        r = lax.rsqrt(jnp.mean(m * m, axis=-1, keepdims=True) + RMS_EPS)
        mhat = m * r
        mn = (mhat * g_ref[...]).astype(BF16)
        dkv = jnp.concatenate([dmk_ref[...], dmv_ref[...]], axis=1).astype(BF16)
        gw_ref[...] = _dot_tn(mn, dkv)
        dmn = _dot_nt(dkv, w_ref[...])
        gn_ref[...] = jnp.sum(dmn * mhat, axis=0, keepdims=True)

    return pl.pallas_call(
        body, name="mem_kv_bwd",
        out_shape=[jax.ShapeDtypeStruct((D_MODEL, 2 * C_W), F32), jax.ShapeDtypeStruct((1, D_MODEL), F32)],
    )(mem, mem_norm, w_mkv, dmk, dmv)


def _pre_proj(x, pre_norm, w_in_g, rope):
    seq = x.shape[0]
    tm = min(512, seq)
    n_nat, n_dil = len(_NATURAL), len(_DILATED) * len(B_DILS)

    def body(x_ref, g_ref, w_ref, c_ref, sm_ref, sp_ref, *refs):
        nat = dict(zip(_NATURAL, refs[:n_nat]))
        res = {n: refs[n_nat + len(B_DILS) * k:n_nat + len(B_DILS) * (k + 1)] for k, n in enumerate(_DILATED)}
        ut = refs[n_nat + n_dil]
        bufs = dict(zip(_DILATED, refs[n_nat + n_dil + 1:]))
        xv = x_ref[...]
        r = lax.rsqrt(jnp.mean(xv * xv, axis=-1, keepdims=True) + RMS_EPS)
        u = xv * r * g_ref[...]
        ub = u.astype(BF16)
        ut[...] = u.T.astype(BF16)
        c, sm, sp = c_ref[...], sm_ref[...], sp_ref[...]
        for j in range(N_CHIPS):
            pj = _dot(ub, w_ref[j])
            for b in range(SHARD_IN // LANES):
                name, off, roped, scaled = _PROJ_LAYOUT[(SHARD_IN // LANES) * j + b]
                piece = pj[:, LANES * b:LANES * (b + 1)]
                if roped:
                    piece = _rope(piece, c, sm, sp)
                if scaled:
                    piece = piece * SCALE
                if name in bufs:
                    bufs[name][off // LANES] = piece
                else:
                    nat[name][:, off:off + LANES] = piece.astype(BF16)
        for name in _DILATED:
            for ref, dil in zip(res[name], B_DILS):
                _to_residues(bufs[name], ref, dil)

    row = lambda w: pl.BlockSpec((tm, w), lambda i: (i, 0))
    full = lambda a: pl.BlockSpec(a.shape, lambda i: (0,) * a.ndim)
    out_shape = [jax.ShapeDtypeStruct((seq, _PROJ_WIDTH[n]), BF16) for n in _NATURAL]
    out_specs = [row(_PROJ_WIDTH[n]) for n in _NATURAL]
    for n in _DILATED:
        for dil in B_DILS:
            out_shape.append(jax.ShapeDtypeStruct((dil, seq // dil, B_W), BF16))
            out_specs.append(_residue_spec(dil, tm, B_W))
    out_shape.append(jax.ShapeDtypeStruct((D_MODEL, seq), BF16))
    out_specs.append(pl.BlockSpec((D_MODEL, tm), lambda i: (0, i)))
    res = pl.pallas_call(
        body, name="pre_proj", grid=(seq // tm,),
        in_specs=[row(D_MODEL), full(pre_norm), full(w_in_g), row(LANES), row(LANES), row(LANES)],
        out_specs=out_specs, out_shape=out_shape,
        scratch_shapes=[_stage(tm, B_W)] * len(_DILATED),
    )(x, pre_norm, w_in_g, *rope)
    out = dict(zip(_NATURAL, res[:n_nat]))
    for k, n in enumerate(_DILATED):
        out[n] = res[n_nat + len(B_DILS) * k:n_nat + len(B_DILS) * (k + 1)]
    out["ut"] = res[n_nat + n_dil]
    return out


def _band_bias(max_dist, transposed):
    i = np.arange(BLOCK)[:, None]
    j = np.arange(BLOCK)[None, :]
    if transposed:
        same = i <= j
        other = (j + BLOCK - i) <= max_dist
        vis = np.concatenate([same, other], axis=1)
    else:
        prev = (i + BLOCK - j) <= max_dist
        same = j <= i
        vis = np.concatenate([prev, same], axis=1)
    return jnp.asarray(np.where(vis, 0.0, NEG).astype(np.float32))


def _kv_place(h, gqa):
    return (0, h // 3) if gqa else (h // 2, h % 2)


def _band_fwd(q, k, v, sink, *, max_dist, name):
    dil, length, wq = q.shape
    wk = k.shape[2]
    gqa = wk != wq
    tq = min(512, length)
    ns, nt = tq // BLOCK, length // tq
    npair = wq // LANES
    bias = _band_bias(max_dist, transposed=False)
    has_sink = sink is not None

    def body(*refs):
        if has_sink:
            sink_ref, refs = refs[0], refs[1:]
        q_ref, k_ref, kp_ref, v_ref, vp_ref, bias_ref, o_ref, lse_ref, kbuf, vbuf = refs[:10]
        i = pl.program_id(1)
        kbuf[0:BLOCK] = kp_ref[...]
        kbuf[BLOCK:] = k_ref[...]
        vbuf[0:BLOCK] = vp_ref[...]
        vbuf[BLOCK:] = v_ref[...]
        if gqa:
            kroll, vroll = refs[10:12]
            kroll[...] = pltpu.roll(kbuf[...], HEAD_DIM, 1)
            vroll[...] = pltpu.roll(vbuf[...], HEAD_DIM, 1)
        half = _half_masks(BLOCK)
        col_prev = (lax.broadcasted_iota(jnp.int32, (1, 2 * BLOCK), 1) < BLOCK).astype(F32)

        def sub(a, carry):
            r0 = pl.multiple_of(a * BLOCK, BLOCK)
            pen = jnp.where((i == 0) & (a == 0), NEG, 0.0)
            b = bias_ref[...] + pen * col_prev
            lse_cols = []
            for p in range(npair):
                lanes = slice(p * LANES, (p + 1) * LANES)
                qp = q_ref[pl.ds(r0, BLOCK), lanes]
                o_h = []
                for e in range(2):
                    h = 2 * p + e
                    pk, ek = _kv_place(h, gqa)
                    klanes = slice(pk * LANES, (pk + 1) * LANES)
                    kw = (kbuf if ek == e else kroll)[pl.ds(r0, 2 * BLOCK), klanes]
                    vw = (vbuf if ek == e else vroll)[pl.ds(r0, 2 * BLOCK), klanes]
                    qm = jnp.where(half[e], qp, jnp.zeros_like(qp))
                    s = _dot_nt(qm, kw) + b
                    m = jnp.max(s, axis=1, keepdims=True)
                    if has_sink:
                        m = jnp.maximum(m, sink_ref[h])
                    pe = jnp.exp(s - m)
                    l = jnp.sum(pe, axis=1, keepdims=True)
                    if has_sink:
                        l = l + jnp.exp(sink_ref[h] - m)
                    pv = _dot(pe.astype(BF16), vw)
                    o_h.append(pv * (1.0 / l))
                    lse_cols.append(m + jnp.log(l))
                o_ref[pl.ds(r0, BLOCK), lanes] = jnp.where(half[0], o_h[0], o_h[1]).astype(BF16)
            lse_ref[pl.ds(r0, BLOCK), :] = _per_head(lse_cols)
            return carry

        lax.fori_loop(0, ns, sub, 0)

    main = lambda w: pl.BlockSpec((None, tq, w), lambda r, i: (r, i, 0))
    prev = lambda w: pl.BlockSpec((None, BLOCK, w), lambda r, i: (r, jnp.maximum(i * ns - 1, 0), 0))
    in_specs = [main(wq), main(wk), prev(wk), main(wk), prev(wk), pl.BlockSpec(bias.shape, lambda r, i: (0, 0))]
    args = [q, k, k, v, v, bias]
    if has_sink:
        in_specs = [pl.BlockSpec(memory_space=pltpu.SMEM)] + in_specs
        args = [sink] + args
    scratch = [pltpu.VMEM((tq + BLOCK, wk), BF16)] * (4 if gqa else 2)
    return pl.pallas_call(
        body, name=name, grid=(dil, nt), in_specs=in_specs,
        out_specs=[main(wq), main(LANES)],
        out_shape=[jax.ShapeDtypeStruct((dil, length, wq), BF16), jax.ShapeDtypeStruct((dil, length, LANES), F32)],
        scratch_shapes=scratch,
    )(*args)


def _band_bwd(q, k, v, do, lse, delta, *, max_dist, name):
    dil, length, wq = q.shape
    wk = k.shape[2]
    gqa = wk != wq
    tq = min(512, length)
    ns, nt = tq // BLOCK, length // tq
    npair = wq // LANES
    nblocks = length // BLOCK
    bias = _band_bias(max_dist, transposed=True)

    def body(q_ref, qn_ref, do_ref, don_ref, lse_ref, lsen_ref, dl_ref, dln_ref, k_ref, v_ref, bias_ref,
             dq_ref, dk_ref, dv_ref, qbuf, dobuf, stat_l, stat_d, dqacc, *rolled):
        i = pl.program_id(1)
        qbuf[0:tq] = q_ref[...]
        qbuf[tq:] = qn_ref[...]
        dobuf[0:tq] = do_ref[...]
        dobuf[tq:] = don_ref[...]
        if gqa:
            kroll, vroll = rolled
            kroll[...] = pltpu.roll(k_ref[...], HEAD_DIM, 1)
            vroll[...] = pltpu.roll(v_ref[...], HEAD_DIM, 1)
        for a in range(ns):
            rows = slice(a * BLOCK, (a + 1) * BLOCK)
            stat_l[a] = _rows_to_lanes(lse_ref[rows, :])
            stat_d[a] = _rows_to_lanes(dl_ref[rows, :])
        stat_l[ns] = _rows_to_lanes(lsen_ref[...])
        stat_d[ns] = _rows_to_lanes(dln_ref[...])

        @pl.when(i == 0)
        def _():
            dqacc[0:BLOCK] = jnp.zeros((BLOCK, wq), F32)

        @pl.when(i > 0)
        def _():
            dqacc[0:BLOCK] = dqacc[tq:tq + BLOCK]

        dqacc[BLOCK:] = jnp.zeros((tq, wq), F32)
        half = _half_masks(BLOCK)
        half2 = _half_masks(2 * BLOCK)
        col_next = (lax.broadcasted_iota(jnp.int32, (1, 2 * BLOCK), 1) >= BLOCK).astype(F32)

        def sub(b, carry):
            r0 = pl.multiple_of(b * BLOCK, BLOCK)
            pen = jnp.where((i == nt - 1) & (b == ns - 1), NEG, 0.0)
            bt = bias_ref[...] + pen * col_next
            acc = {}
            for p in range(npair):
                lanes = slice(p * LANES, (p + 1) * LANES)
                qw = qbuf[pl.ds(r0, 2 * BLOCK), lanes]
                dow = dobuf[pl.ds(r0, 2 * BLOCK), lanes]
                for e in range(2):
                    h = 2 * p + e
                    pk, ek = _kv_place(h, gqa)
                    klanes = slice(pk * LANES, (pk + 1) * LANES)
                    kb = (k_ref if ek == e else kroll)[pl.ds(r0, BLOCK), klanes]
                    vb = (v_ref if ek == e else vroll)[pl.ds(r0, BLOCK), klanes]
                    qm = jnp.where(half2[e], qw, jnp.zeros_like(qw))
                    dom = jnp.where(half2[e], dow, jnp.zeros_like(dow))
                    lrow = jnp.concatenate([stat_l[b, h:h + 1, :], stat_l[b + 1, h:h + 1, :]], axis=1)
                    drow = jnp.concatenate([stat_d[b, h:h + 1, :], stat_d[b + 1, h:h + 1, :]], axis=1)
                    st = _dot_nt(kb, qm) + bt
                    pt = jnp.exp(st - lrow)
                    dv_c = _dot(pt.astype(BF16), dom)
                    dpt = _dot_nt(vb, dom)
                    dsb = (pt * (dpt - drow)).astype(BF16)
                    dk_c = _dot(dsb, qm)
                    kbm = jnp.where(half[e], kb, jnp.zeros_like(kb))
                    dqacc[pl.ds(r0, 2 * BLOCK), lanes] += _dot_tn(dsb, kbm)
                    key = (pk, ek == e)
                    if key in acc:
                        acc[key] = (acc[key][0] + dk_c, acc[key][1] + dv_c)
                    else:
                        acc[key] = (dk_c, dv_c)
                if not gqa:
                    dk_ref[pl.ds(r0, BLOCK), lanes] = acc[(p, True)][0].astype(BF16)
                    dv_ref[pl.ds(r0, BLOCK), lanes] = acc[(p, True)][1].astype(BF16)
            if gqa:
                dk_al, dv_al = acc[(0, True)]
                dk_mis, dv_mis = acc[(0, False)]
                dk_ref[pl.ds(r0, BLOCK), :] = (dk_al + pltpu.roll(dk_mis, HEAD_DIM, 1)).astype(BF16)
                dv_ref[pl.ds(r0, BLOCK), :] = (dv_al + pltpu.roll(dv_mis, HEAD_DIM, 1)).astype(BF16)
            return carry

        lax.fori_loop(0, ns, sub, 0)
        dq_ref[...] = dqacc[0:tq].astype(BF16)

    main = lambda w: pl.BlockSpec((None, tq, w), lambda r, i: (r, i, 0))
    nxt = lambda w: pl.BlockSpec((None, BLOCK, w), lambda r, i: (r, jnp.minimum((i + 1) * ns, nblocks - 1), 0))
    scratch = [pltpu.VMEM((tq + BLOCK, wq), BF16), pltpu.VMEM((tq + BLOCK, wq), BF16),
               pltpu.VMEM((ns + 1, 8, LANES), F32), pltpu.VMEM((ns + 1, 8, LANES), F32),
               pltpu.VMEM((tq + BLOCK, wq), F32)]
    if gqa:
        scratch = scratch + [pltpu.VMEM((tq, wk), BF16)] * 2
    return pl.pallas_call(
        body, name=name, grid=(dil, nt),
        in_specs=[main(wq), nxt(wq), main(wq), nxt(wq), main(LANES), nxt(LANES), main(LANES), nxt(LANES),
                  main(wk), main(wk), pl.BlockSpec(bias.shape, lambda r, i: (0, 0))],
        out_specs=[main(wq), main(wk), main(wk)],
        out_shape=[jax.ShapeDtypeStruct((dil, length, wq), BF16), jax.ShapeDtypeStruct((dil, length, wk), BF16),
                   jax.ShapeDtypeStruct((dil, length, wk), BF16)],
        scratch_shapes=scratch,
        compiler_params=pltpu.CompilerParams(dimension_semantics=("arbitrary", "arbitrary")),
    )(q, q, do, do, lse, lse, delta, delta, k, v, bias)


def _mem_attn_fwd(q, mk, mv):
    seq = q.shape[0]
    tq = min(512, seq)
    ns = tq // BLOCK

    def body(q_ref, mk_ref, mv_ref, o_ref, lse_ref):
        half = _half_masks(BLOCK)

        def sub(a, carry):
            r0 = pl.multiple_of(a * BLOCK, BLOCK)
            lse_cols = []
            for p in range(C_W // LANES):
                lanes = slice(p * LANES, (p + 1) * LANES)
                qp = q_ref[pl.ds(r0, BLOCK), lanes]
                o_h = []
                for e in range(2):
                    qm = jnp.where(half[e], qp, jnp.zeros_like(qp))
                    s = _dot_nt(qm, mk_ref[:, lanes])
                    m = jnp.max(s, axis=1, keepdims=True)
                    pe = jnp.exp(s - m)
                    l = jnp.sum(pe, axis=1, keepdims=True)
                    o_h.append(_dot(pe.astype(BF16), mv_ref[:, lanes]) * (1.0 / l))
                    lse_cols.append(m + jnp.log(l))
                o_ref[pl.ds(r0, BLOCK), lanes] = jnp.where(half[0], o_h[0], o_h[1]).astype(BF16)
            lse_ref[pl.ds(r0, BLOCK), :] = _per_head(lse_cols)
            return carry

        lax.fori_loop(0, ns, sub, 0)

    row = lambda w: pl.BlockSpec((tq, w), lambda i: (i, 0))
    full = pl.BlockSpec((N_MEM, C_W), lambda i: (0, 0))
    return pl.pallas_call(
        body, name="mem_attn_fwd", grid=(seq // tq,), in_specs=[row(C_W), full, full],
        out_specs=[row(C_W), row(LANES)],
        out_shape=[jax.ShapeDtypeStruct((seq, C_W), BF16), jax.ShapeDtypeStruct((seq, LANES), F32)],
    )(q, mk, mv)


def _mem_attn_bwd(q, mk, mv, do, lse, delta):
    seq = q.shape[0]
    tq = min(512, seq)
    ns = tq // BLOCK
    npair = C_W // LANES

    def body(q_ref, mk_ref, mv_ref, do_ref, lse_ref, dl_ref, dq_ref, dmk_ref, dmv_ref, stat_l, stat_d):
        @pl.when(pl.program_id(0) == 0)
        def _():
            dmk_ref[...] = jnp.zeros_like(dmk_ref)
            dmv_ref[...] = jnp.zeros_like(dmv_ref)

        for a in range(ns):
            rows = slice(a * BLOCK, (a + 1) * BLOCK)
            stat_l[a] = _rows_to_lanes(lse_ref[rows, :])
            stat_d[a] = _rows_to_lanes(dl_ref[rows, :])
        half = _half_masks(BLOCK)
        halfk = _half_masks(N_MEM)

        def sub(a, carry):
            r0 = pl.multiple_of(a * BLOCK, BLOCK)
            for p in range(npair):
                lanes = slice(p * LANES, (p + 1) * LANES)
                qp = q_ref[pl.ds(r0, BLOCK), lanes]
                dop = do_ref[pl.ds(r0, BLOCK), lanes]
                kb, vb = mk_ref[:, lanes], mv_ref[:, lanes]
                dq_pair = None
                for e in range(2):
                    h = 2 * p + e
                    qm = jnp.where(half[e], qp, jnp.zeros_like(qp))
                    dom = jnp.where(half[e], dop, jnp.zeros_like(dop))
                    st = _dot_nt(kb, qm)
                    pt = jnp.exp(st - stat_l[a, h:h + 1, :])
                    dmv_ref[:, lanes] += _dot(pt.astype(BF16), dom)
                    dpt = _dot_nt(vb, dom)
                    dsb = (pt * (dpt - stat_d[a, h:h + 1, :])).astype(BF16)
                    dmk_ref[:, lanes] += _dot(dsb, qm)
                    kbm = jnp.where(halfk[e], kb, jnp.zeros_like(kb))
                    dq_c = _dot_tn(dsb, kbm)
                    dq_pair = dq_c if dq_pair is None else dq_pair + dq_c
                dq_ref[pl.ds(r0, BLOCK), lanes] = dq_pair.astype(BF16)
            return carry

        lax.fori_loop(0, ns, sub, 0)

    row = lambda w: pl.BlockSpec((tq, w), lambda i: (i, 0))
    full = pl.BlockSpec((N_MEM, C_W), lambda i: (0, 0))
    return pl.pallas_call(
        body, name="mem_attn_bwd", grid=(seq // tq,),
        in_specs=[row(C_W), full, full, row(C_W), row(LANES), row(LANES)], out_specs=[row(C_W), full, full],
        out_shape=[jax.ShapeDtypeStruct((seq, C_W), BF16), jax.ShapeDtypeStruct((N_MEM, C_W), F32),
                   jax.ShapeDtypeStruct((N_MEM, C_W), F32)],
        scratch_shapes=[pltpu.VMEM((ns, 8, LANES), F32)] * 2,
        compiler_params=pltpu.CompilerParams(dimension_semantics=("arbitrary",)),
    )(q, mk, mv, do, lse, delta)


def _silu_and_grad(g):
    s = 1.0 / (1.0 + jnp.exp(-g))
    return g * s, s * (1.0 + g * (1.0 - s))


def _post(x, target, post_norm, w_out, sink_row, oa, lse_a, ga, ob_list, lseb_list, gb, oc, gc):
    seq = x.shape[0]
    tm = min(256, seq)
    inv_d = 1.0 / D_MODEL
    nd = len(B_DILS)

    def body(*refs):
        (x_ref, t_ref, gp_ref, w_ref, sink_ref, oa_ref, lsea_ref, ga_ref), refs = refs[:8], refs[8:]
        ob_refs, lb_refs, (gb_ref, oc_ref, gc_ref), refs = refs[:nd], refs[nd:2 * nd], refs[2 * nd:2 * nd + 3], refs[2 * nd + 3:]
        (g_ref, doa_ref, dla_ref, dga_ref), refs = refs[:4], refs[4:]
        dob_refs, lsec_refs, dlb_refs, refs = refs[:nd], refs[nd:2 * nd], refs[2 * nd:3 * nd], refs[3 * nd:]
        (dgb_ref, doc_ref, dlc_ref, dgc_ref, gw_ref, gpost_ref, gsink_ref, loss_ref), refs = refs[:8], refs[8:]
        ycat, obufs, lbufs, st_do, st_l, st_d = refs[0], refs[1:nd], refs[nd:2 * nd - 1], refs[2 * nd - 1], refs[2 * nd], refs[2 * nd + 1]

        @pl.when(pl.program_id(0) == 0)
        def _():
            gw_ref[...] = jnp.zeros_like(gw_ref)
            gpost_ref[...] = jnp.zeros_like(gpost_ref)
            gsink_ref[...] = jnp.zeros_like(gsink_ref)
            loss_ref[...] = jnp.zeros_like(loss_ref)

        o_i, l_i = [ob_refs[0][0].astype(F32)], [lb_refs[0][0]]
        for k in range(1, nd):
            _from_residues(ob_refs[k], obufs[k - 1], B_DILS[k])
            _from_residues(lb_refs[k], lbufs[k - 1], B_DILS[k])
            o_i.append(_stage_read(obufs[k - 1]))
            l_i.append(_stage_read(lbufs[k - 1]))
        mx = l_i[0]
        for l in l_i[1:]:
            mx = jnp.maximum(mx, l)
        w_i = [jnp.exp(l - mx) for l in l_i]
        z = w_i[0]
        for w in w_i[1:]:
            z = z + w
        lse_b = mx + jnp.log(z)
        expand = _head_expand_matrix(B_W)
        inv_z = 1.0 / z
        ob = None
        for w, o in zip(w_i, o_i):
            term = _dot_split(w * inv_z, expand, 2) * o
            ob = term if ob is None else ob + term
        oa, oc = oa_ref[...].astype(F32), oc_ref[...].astype(F32)
        sa, dsa = _silu_and_grad(ga_ref[...].astype(F32))
        sb, dsb = _silu_and_grad(gb_ref[...].astype(F32))
        sc, dsc = _silu_and_grad(gc_ref[...].astype(F32))
        ycat[:, 0:A_W] = (oa * sa).astype(BF16)
        ycat[:, A_W:A_W + B_W] = (ob * sb).astype(BF16)
        ycat[:, A_W + B_W:] = (oc * sc).astype(BF16)
        yc = ycat[...]
        y2 = _dot(yc, w_ref[...])
        r = lax.rsqrt(jnp.mean(y2 * y2, axis=-1, keepdims=True) + RMS_EPS)
        zhat = y2 * r
        gp = gp_ref[...]
        err = x_ref[...] + zhat * gp - t_ref[...]
        loss_ref[...] += jnp.sum(err * err) * (0.5 * inv_d)
        g = err * inv_d
        g_ref[...] = g
        gpost_ref[...] += jnp.sum(g * zhat, axis=0, keepdims=True)
        a = g * gp
        dy2 = (r * (a - zhat * jnp.mean(a * zhat, axis=-1, keepdims=True))).astype(BF16)
        gw_ref[...] += _dot_tn(yc, dy2)
        dycat = _dot_nt(dy2, w_ref[...])
        dya, dyb, dyc = dycat[:, 0:A_W], dycat[:, A_W:A_W + B_W], dycat[:, A_W + B_W:]
        doa, dob, doc = dya * sa, dyb * sb, dyc * sc
        doa_ref[...] = doa.astype(BF16)
        doc_ref[...] = doc.astype(BF16)
        dga_ref[...] = (dya * oa * dsa).astype(BF16)
        dgb_ref[...] = (dyb * ob * dsb).astype(BF16)
        dgc_ref[...] = (dyc * oc * dsc).astype(BF16)
        dl_a = _dot_split(doa * oa, _head_sum_matrix(A_W), 3)
        dla_ref[...] = dl_a
        dlc_ref[...] = _dot_split(doc * oc, _head_sum_matrix(C_W), 3)
        gsink_ref[...] += jnp.sum(jnp.exp(sink_ref[...] - lsea_ref[...]) * dl_a, axis=0, keepdims=True)
        _stage_write(st_do, dob)
        _stage_write(st_l, lse_b)
        _stage_write(st_d, _dot_split(dob * ob, _head_sum_matrix(B_W), 3))
        for k, dil in enumerate(B_DILS):
            _to_residues(st_do, dob_refs[k], dil)
            _to_residues(st_l, lsec_refs[k], dil)
            _to_residues(st_d, dlb_refs[k], dil)

    row = lambda w: pl.BlockSpec((tm, w), lambda i: (i, 0))
    full = lambda shape: pl.BlockSpec(shape, lambda i: (0,) * len(shape))
    res_specs = lambda w: [_residue_spec(d, tm, w) for d in B_DILS]
    res_shapes = lambda w, dt: [jax.ShapeDtypeStruct((d, seq // d, w), dt) for d in B_DILS]
    ins = [x, target, post_norm, w_out, sink_row, oa, lse_a, ga, *ob_list, *lseb_list, gb, oc, gc]
    in_specs = ([row(D_MODEL), row(D_MODEL), full((1, D_MODEL)), full((D_MODEL, D_MODEL)), full((1, LANES)),
                 row(A_W), row(LANES), row(A_W)] + res_specs(B_W) + res_specs(LANES) + [row(B_W), row(C_W), row(C_W)])
    out_shape = ([jax.ShapeDtypeStruct((seq, D_MODEL), F32), jax.ShapeDtypeStruct((seq, A_W), BF16),
                  jax.ShapeDtypeStruct((seq, LANES), F32), jax.ShapeDtypeStruct((seq, A_W), BF16)]
                 + res_shapes(B_W, BF16) + res_shapes(LANES, F32) + res_shapes(LANES, F32)
                 + [jax.ShapeDtypeStruct((seq, B_W), BF16), jax.ShapeDtypeStruct((seq, C_W), BF16),
                    jax.ShapeDtypeStruct((seq, LANES), F32), jax.ShapeDtypeStruct((seq, C_W), BF16),
                    jax.ShapeDtypeStruct((D_MODEL, D_MODEL), F32), jax.ShapeDtypeStruct((1, D_MODEL), F32),
                    jax.ShapeDtypeStruct((1, LANES), F32), jax.ShapeDtypeStruct((1, LANES), F32)])
    out_specs = ([row(D_MODEL), row(A_W), row(LANES), row(A_W)] + res_specs(B_W) + res_specs(LANES) + res_specs(LANES)
                 + [row(B_W), row(C_W), row(LANES), row(C_W),
                    full((D_MODEL, D_MODEL)), full((1, D_MODEL)), full((1, LANES)), full((1, LANES))])
    scratch = ([pltpu.VMEM((tm, D_MODEL), BF16)] + [_stage(tm, B_W)] * (nd - 1) + [_stage(tm, LANES)] * (nd - 1)
               + [_stage(tm, B_W), _stage(tm, LANES), _stage(tm, LANES)])
    res = pl.pallas_call(
        body, name="post", grid=(seq // tm,), in_specs=in_specs, out_specs=out_specs, out_shape=out_shape,
        scratch_shapes=scratch,
        compiler_params=pltpu.CompilerParams(dimension_semantics=("arbitrary",)),
    )(*ins)
    out = dict(g=res[0], doa=res[1], dl_a=res[2], dga=res[3], dob=res[4:4 + nd], lse_b=res[4 + nd:4 + 2 * nd],
               dl_b=res[4 + 2 * nd:4 + 3 * nd])
    rest = res[4 + 3 * nd:]
    out.update(dgb=rest[0], doc=rest[1], dl_c=rest[2], dgc=rest[3], gw_out=rest[4], gpost=rest[5], gsink=rest[6],
               loss=rest[7])
    return out


def _dx(x, g, pre_norm, w_in_g, rope, nat, res):
    seq = x.shape[0]
    tm = min(256, seq)
    nd = len(B_DILS)
    nat_list = [nat[n] for n in _NATURAL]
    res_list = [a for n in _DILATED for a in res[n]]

    def body(x_ref, g_ref, gp_ref, w_ref, c_ref, sm_ref, sp_ref, *refs):
        nat_refs = dict(zip(_NATURAL, refs[:len(_NATURAL)]))
        refs = refs[len(_NATURAL):]
        res_refs = {n: refs[nd * k:nd * (k + 1)] for k, n in enumerate(_DILATED)}
        refs = refs[nd * len(_DILATED):]
        dproj_ref, gx_ref, gpre_ref = refs[:3]
        bufs = {n: refs[3 + (nd - 1) * k:3 + (nd - 1) * (k + 1)] for k, n in enumerate(_DILATED)}

        @pl.when(pl.program_id(0) == 0)
        def _():
            gpre_ref[...] = jnp.zeros_like(gpre_ref)

        for n in _DILATED:
            for k in range(1, nd):
                _from_residues(res_refs[n][k], bufs[n][k - 1], B_DILS[k])
        c, sm, sp = c_ref[...], -sm_ref[...], -sp_ref[...]
        for blk, (name, off, roped, scaled) in enumerate(_PROJ_LAYOUT):
            lanes = slice(off, off + LANES)
            if name in nat_refs:
                piece = nat_refs[name][:, lanes].astype(F32)
            else:
                piece = res_refs[name][0][0, :, lanes].astype(F32)
                for buf in bufs[name]:
                    piece = piece + buf[off // LANES]
            if roped:
                piece = _rope(piece, c, sm, sp)
            if scaled:
                piece = piece * SCALE
            dproj_ref[:, blk * LANES:(blk + 1) * LANES] = piece.astype(BF16)
        du = None
        for j in range(N_CHIPS):
            part = _dot_nt(dproj_ref[:, j * SHARD_IN:(j + 1) * SHARD_IN], w_ref[j])
            du = part if du is None else du + part
        xv = x_ref[...]
        r = lax.rsqrt(jnp.mean(xv * xv, axis=-1, keepdims=True) + RMS_EPS)
        xhat = xv * r
        gpre_ref[...] += jnp.sum(du * xhat, axis=0, keepdims=True)
        a = du * gp_ref[...]
        gx_ref[...] = g_ref[...] + r * (a - xhat * jnp.mean(a * xhat, axis=-1, keepdims=True))

    row = lambda w: pl.BlockSpec((tm, w), lambda i: (i, 0))
    full = lambda a: pl.BlockSpec(a.shape, lambda i: (0,) * a.ndim)
    in_specs = ([row(D_MODEL), row(D_MODEL), full(pre_norm), full(w_in_g), row(LANES), row(LANES), row(LANES)]
                + [row(a.shape[1]) for a in nat_list]
                + [_residue_spec(d, tm, B_W) for _ in _DILATED for d in B_DILS])
    return pl.pallas_call(
        body, name="dx", grid=(seq // tm,), in_specs=in_specs,
        out_specs=[row(D_IN), row(D_MODEL), pl.BlockSpec((1, D_MODEL), lambda i: (0, 0))],
        out_shape=[jax.ShapeDtypeStruct((seq, D_IN), BF16), jax.ShapeDtypeStruct((seq, D_MODEL), F32),
                   jax.ShapeDtypeStruct((1, D_MODEL), F32)],
        scratch_shapes=[_stage(tm, B_W)] * ((nd - 1) * len(_DILATED)),
        compiler_params=pltpu.CompilerParams(dimension_semantics=("arbitrary",)),
    )(x, g, pre_norm, w_in_g, *rope, *nat_list, *res_list)


def _grad_w_in(ut, dproj):
    seq = ut.shape[1]
    tk = min(1024, seq)

    def body(ut_ref, dp_ref, out_ref):
        @pl.when(pl.program_id(1) == 0)
        def _():
            out_ref[...] = jnp.zeros_like(out_ref)

        out_ref[...] += _dot(ut_ref[...], dp_ref[...])

    return pl.pallas_call(
        body, name="grad_w_in", grid=(N_CHIPS, seq // tk),
        in_specs=[pl.BlockSpec((D_MODEL, tk), lambda j, i: (0, i)), pl.BlockSpec((tk, SHARD_IN), lambda j, i: (i, j))],
        out_specs=pl.BlockSpec((None, D_MODEL, SHARD_IN), lambda j, i: (j, 0, 0)),
        out_shape=jax.ShapeDtypeStruct((N_CHIPS, D_MODEL, SHARD_IN), F32),
        compiler_params=pltpu.CompilerParams(dimension_semantics=("arbitrary", "arbitrary")),
    )(ut, dproj)


def _pair_exchange(grads):
    n = len(grads)

    def body(*refs):
        srcs, outs = refs[:n], refs[n:2 * n]
        send_sems, recv_sems = refs[2 * n:]
        x, y, c = lax.axis_index("x"), lax.axis_index("y"), lax.axis_index("c")
        copies = []
        for t in range(n):
            rows = grads[t].shape[1] // 2
            copies.append(pltpu.make_async_remote_copy(
                src_ref=srcs[t].at[:, pl.ds((1 - c) * rows, rows)], dst_ref=outs[t],
                send_sem=send_sems.at[t], recv_sem=recv_sems.at[t], device_id=(x, y, 1 - c), device_id_type=MESH))
        for cp in copies:
            cp.start()
        for cp in copies:
            cp.wait()

    any_spec = pl.BlockSpec(memory_space=pl.ANY)
    return pl.pallas_call(
        body, name="pair_exchange",
        out_shape=[jax.ShapeDtypeStruct((g.shape[0], g.shape[1] // 2, g.shape[2]), g.dtype) for g in grads],
        in_specs=[any_spec] * n, out_specs=[any_spec] * n,
        scratch_shapes=[pltpu.SemaphoreType.DMA((n,)), pltpu.SemaphoreType.DMA((n,))],
    )(*grads)


def _pair_add(core, own, got):
    nchip, rows2, width = own.shape
    rows = rows2 // 2
    tr = min(128, rows)
    nb = rows // tr

    def body(core_ref, own_ref, got_ref, out_ref):
        out_ref[...] = (own_ref[...] + got_ref[...]).astype(BF16)

    grid_spec = pltpu.PrefetchScalarGridSpec(
        num_scalar_prefetch=1, grid=(nchip, nb),
        in_specs=[pl.BlockSpec((None, tr, width), lambda k, i, core_ref: (k, core_ref[0] * nb + i, 0)),
                  pl.BlockSpec((None, tr, width), lambda k, i, core_ref: (k, i, 0))],
        out_specs=pl.BlockSpec((None, tr, width), lambda k, i, core_ref: (k, i, 0)))
    return pl.pallas_call(
        body, name=f"pair_add_{width}", grid_spec=grid_spec,
        out_shape=jax.ShapeDtypeStruct((nchip, rows, width), BF16),
    )(core, own, got)


def _chip_exchange(parts, small):
    n = len(parts)

    def body(*refs):
        srcs, small_ref = refs[:n], refs[n]
        outs, small_out = refs[n + 1:2 * n + 1], refs[2 * n + 1]
        send_sems, recv_sems, local_sems = refs[2 * n + 2:]
        x, y, c = lax.axis_index("x"), lax.axis_index("y"), lax.axis_index("c")
        my_chip = 2 * x + y
        me = 4 * x + 2 * y + c
        chips = [(1 - x, y), (x, 1 - y), (1 - x, 1 - y)]
        local = [pltpu.make_async_copy(srcs[t].at[my_chip], outs[t].at[my_chip], local_sems.at[t]) for t in range(n)]
        local.append(pltpu.make_async_copy(small_ref, small_out.at[me], local_sems.at[n]))
        for cp in local:
            cp.start()
        sent = []
        for j, (cx, cy) in enumerate(chips):
            for t in range(n):
                k = n * j + t
                sent.append(pltpu.make_async_remote_copy(
                    src_ref=srcs[t].at[2 * cx + cy], dst_ref=outs[t].at[my_chip], send_sem=send_sems.at[k],
                    recv_sem=recv_sems.at[k], device_id=(cx, cy, c), device_id_type=MESH))
        peers = [(x, y, 1 - c)] + [(cx, cy, cc) for (cx, cy) in chips for cc in (c, 1 - c)]
        for j, peer in enumerate(peers):
            k = 3 * n + j
            sent.append(pltpu.make_async_remote_copy(
                src_ref=small_ref, dst_ref=small_out.at[me], send_sem=send_sems.at[k], recv_sem=recv_sems.at[k],
                device_id=peer, device_id_type=MESH))
        for cp in sent:
            cp.start()
        for cp in sent:
            cp.wait()
        for cp in local:
            cp.wait()

    any_spec = pl.BlockSpec(memory_space=pl.ANY)
    nsem = 3 * n + 7
    return pl.pallas_call(
        body, name="chip_exchange",
        out_shape=[jax.ShapeDtypeStruct(p.shape, p.dtype) for p in parts]
        + [jax.ShapeDtypeStruct((8,) + small.shape, small.dtype)],
        in_specs=[any_spec] * (n + 1), out_specs=[any_spec] * (n + 1),
        scratch_shapes=[pltpu.SemaphoreType.DMA((nsem,)), pltpu.SemaphoreType.DMA((nsem,)),
                        pltpu.SemaphoreType.DMA((n + 1,))],
    )(*parts, small)


def _slot_sum(slots, name):
    ns, rows, width = slots.shape
    tr = min(128, rows)

    def body(in_ref, out_ref):
        acc = in_ref[0].astype(F32)
        for s in range(1, ns):
            acc = acc + in_ref[s].astype(F32)
        out_ref[...] = acc

    return pl.pallas_call(
        body, name=name, grid=(rows // tr,),
        in_specs=[pl.BlockSpec((ns, tr, width), lambda i: (0, i, 0))],
        out_specs=pl.BlockSpec((tr, width), lambda i: (i, 0)),
        out_shape=jax.ShapeDtypeStruct((rows, width), F32),
    )(slots)


def _pair_gather(halves):
    n = len(halves)

    def body(*refs):
        srcs, outs = refs[:n], refs[n:2 * n]
        send_sems, recv_sems, local_sems = refs[2 * n:]
        x, y, c = lax.axis_index("x"), lax.axis_index("y"), lax.axis_index("c")
        copies, local = [], []
        for t in range(n):
            mine = outs[t].at[c]
            local.append(pltpu.make_async_copy(srcs[t], mine, local_sems.at[t]))
            copies.append(pltpu.make_async_remote_copy(
                src_ref=srcs[t], dst_ref=mine, send_sem=send_sems.at[t], recv_sem=recv_sems.at[t],
                device_id=(x, y, 1 - c), device_id_type=MESH))
        for cp in local + copies:
            cp.start()
        for cp in copies:
            cp.wait()
        for cp in local:
            cp.wait()

    any_spec = pl.BlockSpec(memory_space=pl.ANY)
    res = pl.pallas_call(
        body, name="pair_gather",
        out_shape=[jax.ShapeDtypeStruct((2,) + h.shape, h.dtype) for h in halves],
        in_specs=[any_spec] * n, out_specs=[any_spec] * n,
        scratch_shapes=[pltpu.SemaphoreType.DMA((n,)), pltpu.SemaphoreType.DMA((n,)), pltpu.SemaphoreType.DMA((n,))],
    )(*halves)
    return [r.reshape(2 * h.shape[0], h.shape[1]) for r, h in zip(res, halves)]


def _adamw(w, g, m, v, name):
    rows, width = w.shape
    tr = min(256, rows)
    c1 = 1.0 / (1.0 - ADAM_B1 ** ADAM_STEP)
    c2 = 1.0 / (1.0 - ADAM_B2 ** ADAM_STEP)

    def body(w_ref, g_ref, m_ref, v_ref, d_ref, nm_ref, nv_ref):
        gv = g_ref[...]
        nm = ADAM_B1 * m_ref[...] + (1.0 - ADAM_B1) * gv
        nv = ADAM_B2 * v_ref[...] + (1.0 - ADAM_B2) * (gv * gv)
        nm_ref[...] = nm
        nv_ref[...] = nv
        d_ref[...] = -ADAM_LR * ((nm * c1) / (jnp.sqrt(nv * c2) + ADAM_EPS) + ADAM_WD * w_ref[...])

    spec = pl.BlockSpec((tr, width), lambda i: (i, 0))
    return pl.pallas_call(
        body, name=name, grid=(rows // tr,), in_specs=[spec] * 4, out_specs=[spec] * 3,
        out_shape=[jax.ShapeDtypeStruct(w.shape, F32)] * 3,
    )(w, g, m, v)


def _local_step(x, mem, target, pre_norm, sink_a, mem_norm, post_norm, w_in_g, w_out, w_mkv):
    seq = x.shape[0]
    rope = _rope_tables(seq)
    mk, mv = _mem_kv(mem, mem_norm, w_mkv)
    pr = _pre_proj(x, pre_norm, w_in_g, rope)
    sink = sink_a.reshape(-1)
    qa, ka, va = pr["qa"][None], pr["ka"][None], pr["va"][None]
    oa, lse_a = _band_fwd(qa, ka, va, sink, max_dist=A_WINDOW - 1, name="swa_fwd")
    ob_list, lseb_list = [], []
    for k, (win, dil) in enumerate(B_CONFIGS):
        o_i, l_i = _band_fwd(pr["qb"][k], pr["kb"][k], pr["vb"][k], None, max_dist=win // dil, name=f"dil{dil}_fwd")
        ob_list.append(o_i)
        lseb_list.append(l_i)
    oc, lse_c = _mem_attn_fwd(pr["qc"], mk, mv)
    sink_row = jnp.pad(sink, (0, LANES - sink.shape[0])).reshape(1, LANES)
    po = _post(x, target, post_norm, w_out, sink_row, oa[0], lse_a[0], pr["ga"], ob_list, lseb_list, pr["gb"], oc,
               pr["gc"])
    dqc, dmk, dmv = _mem_attn_bwd(pr["qc"], mk, mv, po["doc"], lse_c, po["dl_c"])
    dqa, dka, dva = _band_bwd(qa, ka, va, po["doa"][None], lse_a, po["dl_a"][None], max_dist=A_WINDOW - 1,
                              name="swa_bwd")
    res = dict(qb=[], kb=[], vb=[])
    for k, (win, dil) in enumerate(B_CONFIGS):
        dq_i, dk_i, dv_i = _band_bwd(pr["qb"][k], pr["kb"][k], pr["vb"][k], po["dob"][k], po["lse_b"][k],
                                     po["dl_b"][k], max_dist=win // dil, name=f"dil{dil}_bwd")
        res["qb"].append(dq_i)
        res["kb"].append(dk_i)
        res["vb"].append(dv_i)
    nat = dict(qa=dqa[0], ka=dka[0], va=dva[0], ga=po["dga"], gb=po["dgb"], qc=dqc, gc=po["dgc"])
    dproj, grad_x, gpre = _dx(x, po["g"], pre_norm, w_in_g, rope, nat, res)
    gw_in = _grad_w_in(pr["ut"], dproj)
    gw_mkv, gmem = _mem_kv_bwd(mem, mem_norm, w_mkv, dmk, dmv)
    gsink = -po["gsink"][0, :sink.shape[0]]
    return dict(loss=po["loss"][0, 0], grad_x=grad_x, gw_in=gw_in, gw_out=po["gw_out"], gw_mkv=gw_mkv,
                gpre=gpre, gpost=po["gpost"], gmem=gmem, gsink=gsink)


def kernel(x, mem, pre_norm, w_in, sink_a, mem_norm, w_mem_kv, w_out, post_norm, loss_target, m_pre_norm, m_w_in, m_sink_a, m_mem_norm, m_w_mem_kv, m_w_out, m_post_norm, v_pre_norm, v_w_in, v_sink_a, v_mem_norm, v_w_mem_kv, v_w_out, v_post_norm):
    w_in_g, w_out_g, w_mkv_g = _gather_weights(w_in[0].astype(BF16), w_out[0].astype(BF16), w_mem_kv[0].astype(BF16))
    loc = _local_step(x[0], mem[0], loss_target[0], pre_norm, sink_a, mem_norm, post_norm,
                      w_in_g.reshape(N_CHIPS, D_MODEL, SHARD_IN), w_out_g.reshape(D_MODEL, D_MODEL),
                      w_mkv_g.reshape(D_MODEL, 2 * C_W))
    loss = lax.psum(loc["loss"], ("x", "y", "c"))

    big = [loc["gw_in"], loc["gw_out"].reshape(N_CHIPS, D_MODEL // N_CHIPS, D_MODEL),
           loc["gw_mkv"].reshape(N_CHIPS, D_MODEL // N_CHIPS, 2 * C_W)]
    small = jnp.concatenate([loc["gpre"], loc["gpost"], loc["gmem"],
                             jnp.pad(loc["gsink"], (0, D_MODEL - loc["gsink"].shape[0])).reshape(1, D_MODEL),
                             jnp.zeros((4, D_MODEL), F32)], axis=0)
    core = lax.axis_index("c").astype(jnp.int32).reshape(1)
    got = _pair_exchange(big)
    parts = [_pair_add(core, own, g) for own, g in zip(big, got)]
    *slots, small_slots = _chip_exchange(parts, small)
    halves = [_slot_sum(s, name=f"chip_sum_{s.shape[2]}") for s in slots]
    g_in, g_out, g_mkv = _pair_gather(halves)
    small_sum = _slot_sum(small_slots, name="device_sum")
    g_pre, g_post, g_mem = small_sum[0:1], small_sum[1:2], small_sum[2:3]
    g_sink = small_sum[3:4, :sink_a.shape[1]]

    d_in, nm_in, nv_in = _adamw(w_in[0], g_in, m_w_in[0], v_w_in[0], "adamw_in")
    d_out, nm_out, nv_out = _adamw(w_out[0], g_out, m_w_out[0], v_w_out[0], "adamw_out")
    d_mkv, nm_mkv, nv_mkv = _adamw(w_mem_kv[0], g_mkv, m_w_mem_kv[0], v_w_mem_kv[0], "adamw_mkv")
    pad6 = lambda a: jnp.pad(a, ((0, 0), (0, D_MODEL - a.shape[1])))
    stack = lambda a, b, c_, d_: jnp.concatenate([a, b, c_, pad6(d_), jnp.zeros((4, D_MODEL), F32)], axis=0)
    d_s, nm_s, nv_s = _adamw(stack(pre_norm, post_norm, mem_norm, sink_a), small_sum,
                             stack(m_pre_norm, m_post_norm, m_mem_norm, m_sink_a),
                             stack(v_pre_norm, v_post_norm, v_mem_norm, v_sink_a), "adamw_small")
    ns_ = sink_a.shape[1]
    unpack = lambda a: (a[0:1], a[3:4, :ns_], a[2:3], a[1:2])
    d_pre, d_sink, d_mem, d_post = unpack(d_s)
    nm_pre, nm_sink, nm_mem, nm_post = unpack(nm_s)
    nv_pre, nv_sink, nv_mem, nv_post = unpack(nv_s)
    lead = lambda a: a[None]
    return (loss, lead(loc["grad_x"]),
            g_pre, lead(g_in), g_sink, g_mem, lead(g_mkv), lead(g_out), g_post,
            d_pre, lead(d_in), d_sink, d_mem, lead(d_mkv), lead(d_out), d_post,
            nm_pre, lead(nm_in), nm_sink, nm_mem, lead(nm_mkv), lead(nm_out), nm_post,
            nv_pre, lead(nv_in), nv_sink, nv_mem, lead(nv_mkv), lead(nv_out), nv_post)
```

```python
import numpy as np
import jax
import jax.numpy as jnp
from jax import lax
from jax.experimental import pallas as pl
from jax.experimental.pallas import tpu as pltpu

F32 = jnp.float32
BF16 = jnp.bfloat16

D_MODEL = 1024
HEAD_DIM = 64
LANES = 128
BLOCK = 128
A_W, A_KV_W, B_W, C_W = 384, 128, 384, 256
N_MEM = 256
D_IN = 3072
N_CHIPS = 4
SHARD_IN = D_IN // N_CHIPS
B_CONFIGS = ((128, 1), (512, 4), (2048, 16))
B_DILS = tuple(d for _, d in B_CONFIGS)
A_WINDOW = 128
RMS_EPS = 1e-6
ROPE_THETA = 500000.0
SCALE = HEAD_DIM ** -0.5
NEG = -1e30
ADAM_LR, ADAM_B1, ADAM_B2, ADAM_EPS, ADAM_WD, ADAM_STEP = 0.001, 0.9, 0.999, 1e-08, 0.01, 10

NT = (((1,), (1,)), ((), ()))
TN = (((0,), (0,)), ((), ()))
MESH = pl.DeviceIdType.MESH

_PROJ_LAYOUT = (
    [("qa", 128 * i, True, True) for i in range(3)] + [("ka", 0, True, False), ("va", 0, False, False)]
    + [("ga", 128 * i, False, False) for i in range(3)]
    + [("qb", 128 * i, True, True) for i in range(3)] + [("kb", 128 * i, True, False) for i in range(3)]
    + [("vb", 128 * i, False, False) for i in range(3)] + [("gb", 128 * i, False, False) for i in range(3)]
    + [("qc", 128 * i, False, True) for i in range(2)] + [("gc", 128 * i, False, False) for i in range(2)]
)
_PROJ_WIDTH = dict(qa=A_W, ka=A_KV_W, va=A_KV_W, ga=A_W, qb=B_W, kb=B_W, vb=B_W, gb=B_W, qc=C_W, gc=C_W)
_NATURAL = ("qa", "ka", "va", "ga", "gb", "qc", "gc")
_DILATED = ("qb", "kb", "vb")


def _dot(a, b):
    return jnp.dot(a, b, preferred_element_type=F32)


def _dot_nt(a, b):
    return lax.dot_general(a, b, NT, preferred_element_type=F32)


def _dot_tn(a, b):
    return lax.dot_general(a, b, TN, preferred_element_type=F32)


def _half_masks(rows):
    lane = lax.broadcasted_iota(jnp.int32, (rows, LANES), 1)
    return lane < HEAD_DIM, lane >= HEAD_DIM


def _rope(t, c, sm, sp):
    return t * c + pltpu.roll(t, LANES - 8, 1) * sm + pltpu.roll(t, 8, 1) * sp


def _rope_tables(seq):
    dim = jnp.arange(LANES) % HEAD_DIM
    inv_freq = ROPE_THETA ** (-jnp.arange(0, 16, 2, dtype=F32) / 16)
    ang = jnp.arange(seq, dtype=F32)[:, None] * inv_freq[dim % 8][None, :]
    cos, sin = jnp.cos(ang), jnp.sin(ang)
    c = jnp.where(dim < 16, cos, 1.0)
    sm = jnp.where(dim < 8, -sin, 0.0)
    sp = jnp.where((dim >= 8) & (dim < 16), sin, 0.0)
    return c, sm, sp


def _split3(x):
    a = x.astype(BF16)
    r = x - a.astype(F32)
    b = r.astype(BF16)
    c = (r - b.astype(F32)).astype(BF16)
    return a, b, c


def _rows_to_lanes(x):
    row = lax.broadcasted_iota(jnp.int32, (8, LANES), 0)
    lane = lax.broadcasted_iota(jnp.int32, (8, LANES), 1)
    eye = (row == lane).astype(BF16)
    a, b, c = _split3(x)
    return _dot_nt(eye, a) + _dot_nt(eye, b) + _dot_nt(eye, c)


def _head_sum_matrix(width):
    k = lax.broadcasted_iota(jnp.int32, (width, LANES), 0)
    h = lax.broadcasted_iota(jnp.int32, (width, LANES), 1)
    return (k // HEAD_DIM == h).astype(BF16)


def _head_expand_matrix(width):
    h = lax.broadcasted_iota(jnp.int32, (LANES, width), 0)
    k = lax.broadcasted_iota(jnp.int32, (LANES, width), 1)
    return (k // HEAD_DIM == h).astype(BF16)


def _dot_split(x, mat, terms):
    parts = _split3(x)[:terms]
    out = _dot(parts[0], mat)
    for p in parts[1:]:
        out = out + _dot(p, mat)
    return out


def _per_head(cols, fill=0.0):
    rows = cols[0].shape[0]
    lane = lax.broadcasted_iota(jnp.int32, (rows, LANES), 1)
    out = jnp.full((rows, LANES), fill, F32)
    for h, col in enumerate(cols):
        out = jnp.where(lane == h, col, out)
    return out


def _lane_blocks(width):
    return [slice(p * LANES, (p + 1) * LANES) for p in range(width // LANES)]


def _stage(rows, width):
    return pltpu.VMEM((width // LANES, rows, LANES), F32)


def _stage_write(buf, value):
    for p, lanes in enumerate(_lane_blocks(value.shape[1])):
        buf[p] = value[:, lanes]


def _stage_read(buf):
    return jnp.concatenate([buf[p] for p in range(buf.shape[0])], axis=1) if buf.shape[0] > 1 else buf[0]


def _to_residues(buf, out_ref, dil):
    rows = buf.shape[1] // dil
    for r in range(dil):
        for p in range(buf.shape[0]):
            plane = buf.at[p]
            out_ref[r, :, p * LANES:(p + 1) * LANES] = plane[pl.ds(r, rows, stride=dil), :].astype(out_ref.dtype)


def _from_residues(in_ref, buf, dil):
    rows = buf.shape[1] // dil
    for r in range(dil):
        for p in range(buf.shape[0]):
            plane = buf.at[p]
            plane[pl.ds(r, rows, stride=dil), :] = in_ref[r, :, p * LANES:(p + 1) * LANES].astype(F32)


def _residue_spec(dil, tm, width):
    return pl.BlockSpec((dil, tm // dil, width), lambda i: (0, i, 0))


def _gather_weights(w_in_s, w_out_s, w_mkv_s):
    shards = tuple(s.reshape(2, s.shape[0] // 2, s.shape[1]) for s in (w_in_s, w_out_s, w_mkv_s))
    n = len(shards)

    def body(*refs):
        srcs, outs = refs[:n], refs[n:2 * n]
        send_sems, recv_sems, local_sems = refs[2 * n:]
        x, y, c = lax.axis_index("x"), lax.axis_index("y"), lax.axis_index("c")
        my_chip = 2 * x + y
        sibling = (x, y, 1 - c)
        chips = [(1 - x, y), (x, 1 - y), (1 - x, 1 - y)]

        def half(t, chip, which):
            return outs[t].at[chip, which]

        def src_half(t, which):
            return srcs[t].at[which]

        def copy(k, src, dst, to):
            return pltpu.make_async_remote_copy(src_ref=src, dst_ref=dst, send_sem=send_sems.at[k],
                                                recv_sem=recv_sems.at[k], device_id=to, device_id_type=MESH)

        own = [pltpu.make_async_copy(srcs[t], outs[t].at[my_chip], local_sems.at[t]) for t in range(n)]
        for cp in own:
            cp.start()
        first = []
        for j, (cx, cy) in enumerate(chips):
            for t in range(n):
                first.append(copy(n * j + t, src_half(t, c), half(t, my_chip, c), (cx, cy, c)))
        for cp in first:
            cp.start()
        passed = []
        for j, (cx, cy) in enumerate(chips):
            chip = 2 * cx + cy
            for t in range(n):
                k = n * j + t
                copy(k, src_half(t, c), half(t, chip, c), (cx, cy, c)).wait_recv()
                fwd = copy(n * 3 + k, half(t, chip, c), half(t, chip, c), sibling)
                fwd.start()
                passed.append(fwd)
        for j, (cx, cy) in enumerate(chips):
            chip = 2 * cx + cy
            for t in range(n):
                k = n * 3 + n * j + t
                copy(k, half(t, chip, 1 - c), half(t, chip, 1 - c), sibling).wait_recv()
        for cp in first + passed:
            cp.wait_send()
        for cp in own:
            cp.wait()

    any_spec = pl.BlockSpec(memory_space=pl.ANY)
    return pl.pallas_call(
        body, name="gather_weights",
        out_shape=[jax.ShapeDtypeStruct((N_CHIPS,) + s.shape, s.dtype) for s in shards],
        in_specs=[any_spec] * n, out_specs=[any_spec] * n,
        scratch_shapes=[pltpu.SemaphoreType.DMA((6 * n,)), pltpu.SemaphoreType.DMA((6 * n,)),
                        pltpu.SemaphoreType.DMA((n,))],
    )(*shards)


def _mem_kv(mem, mem_norm, w_mkv):
    def body(mem_ref, g_ref, w_ref, mk_ref, mv_ref):
        m = mem_ref[...]
        r = lax.rsqrt(jnp.mean(m * m, axis=-1, keepdims=True) + RMS_EPS)
        mn = (m * r * g_ref[...]).astype(BF16)
        kv = _dot(mn, w_ref[...])
        mk_ref[...] = kv[:, :C_W].astype(BF16)
        mv_ref[...] = kv[:, C_W:].astype(BF16)

    return pl.pallas_call(
        body, name="mem_kv",
        out_shape=[jax.ShapeDtypeStruct((N_MEM, C_W), BF16)] * 2,
    )(mem, mem_norm, w_mkv)


def _mem_kv_bwd(mem, mem_norm, w_mkv, dmk, dmv):
    def body(mem_ref, g_ref, w_ref, dmk_ref, dmv_ref, gw_ref, gn_ref):
        m = mem_ref[...]
        r = lax.rsqrt(jnp.mean(m * m, axis=-1, keepdims=True) + RMS_EPS)
        mhat = m * r
        mn = (mhat * g_ref[...]).astype(BF16)
        dkv = jnp.concatenate([dmk_ref[...], dmv_ref[...]], axis=1).astype(BF16)
        gw_ref[...] = _dot_tn(mn, dkv)
        dmn = _dot_nt(dkv, w_ref[...])
        gn_ref[...] = jnp.sum(dmn * mhat, axis=0, keepdims=True)

    return pl.pallas_call(
        body, name="mem_kv_bwd",
        out_shape=[jax.ShapeDtypeStruct((D_MODEL, 2 * C_W), F32), jax.ShapeDtypeStruct((1, D_MODEL), F32)],
    )(mem, mem_norm, w_mkv, dmk, dmv)


def _pre_proj(x, pre_norm, w_in_g, rope):
    seq = x.shape[0]
    tm = min(512, seq)
    n_nat, n_dil = len(_NATURAL), len(_DILATED) * len(B_DILS)

    def body(x_ref, g_ref, w_ref, c_ref, sm_ref, sp_ref, *refs):
        nat = dict(zip(_NATURAL, refs[:n_nat]))
        res = {n: refs[n_nat + len(B_DILS) * k:n_nat + len(B_DILS) * (k + 1)] for k, n in enumerate(_DILATED)}
        ut = refs[n_nat + n_dil]
        bufs = dict(zip(_DILATED, refs[n_nat + n_dil + 1:]))
        xv = x_ref[...]
        r = lax.rsqrt(jnp.mean(xv * xv, axis=-1, keepdims=True) + RMS_EPS)
        u = xv * r * g_ref[...]
        ub = u.astype(BF16)
        ut[...] = u.T.astype(BF16)
        c, sm, sp = c_ref[...], sm_ref[...], sp_ref[...]
        for j in range(N_CHIPS):
            pj = _dot(ub, w_ref[j])
            for b in range(SHARD_IN // LANES):
                name, off, roped, scaled = _PROJ_LAYOUT[(SHARD_IN // LANES) * j + b]
                piece = pj[:, LANES * b:LANES * (b + 1)]
                if roped:
                    piece = _rope(piece, c, sm, sp)
                if scaled:
                    piece = piece * SCALE
                if name in bufs:
                    bufs[name][off // LANES] = piece
                else:
                    nat[name][:, off:off + LANES] = piece.astype(BF16)
        for name in _DILATED:
            for ref, dil in zip(res[name], B_DILS):
                _to_residues(bufs[name], ref, dil)

    row = lambda w: pl.BlockSpec((tm, w), lambda i: (i, 0))
    full = lambda a: pl.BlockSpec(a.shape, lambda i: (0,) * a.ndim)
    out_shape = [jax.ShapeDtypeStruct((seq, _PROJ_WIDTH[n]), BF16) for n in _NATURAL]
    out_specs = [row(_PROJ_WIDTH[n]) for n in _NATURAL]
    for n in _DILATED:
        for dil in B_DILS:
            out_shape.append(jax.ShapeDtypeStruct((dil, seq // dil, B_W), BF16))
            out_specs.append(_residue_spec(dil, tm, B_W))
    out_shape.append(jax.ShapeDtypeStruct((D_MODEL, seq), BF16))
    out_specs.append(pl.BlockSpec((D_MODEL, tm), lambda i: (0, i)))
    res = pl.pallas_call(
        body, name="pre_proj", grid=(seq // tm,),
        in_specs=[row(D_MODEL), full(pre_norm), full(w_in_g), row(LANES), row(LANES), row(LANES)],
        out_specs=out_specs, out_shape=out_shape,
        scratch_shapes=[_stage(tm, B_W)] * len(_DILATED),
    )(x, pre_norm, w_in_g, *rope)
    out = dict(zip(_NATURAL, res[:n_nat]))
    for k, n in enumerate(_DILATED):
        out[n] = res[n_nat + len(B_DILS) * k:n_nat + len(B_DILS) * (k + 1)]
    out["ut"] = res[n_nat + n_dil]
    return out


def _band_bias(max_dist, transposed):
    i = np.arange(BLOCK)[:, None]
    j = np.arange(BLOCK)[None, :]
    if transposed:
        same = i <= j
        other = (j + BLOCK - i) <= max_dist
        vis = np.concatenate([same, other], axis=1)
    else:
        prev = (i + BLOCK - j) <= max_dist
        same = j <= i
        vis = np.concatenate([prev, same], axis=1)
    return jnp.asarray(np.where(vis, 0.0, NEG).astype(np.float32))


def _kv_place(h, gqa):
    return (0, h // 3) if gqa else (h // 2, h % 2)


def _band_fwd(q, k, v, sink, *, max_dist, name):
    dil, length, wq = q.shape
    wk = k.shape[2]
    gqa = wk != wq
    tq = min(512, length)
    ns, nt = tq // BLOCK, length // tq
    npair = wq // LANES
    bias = _band_bias(max_dist, transposed=False)
    has_sink = sink is not None

    def body(*refs):
        if has_sink:
            sink_ref, refs = refs[0], refs[1:]
        q_ref, k_ref, kp_ref, v_ref, vp_ref, bias_ref, o_ref, lse_ref, kbuf, vbuf = refs[:10]
        i = pl.program_id(1)
        kbuf[0:BLOCK] = kp_ref[...]
        kbuf[BLOCK:] = k_ref[...]
        vbuf[0:BLOCK] = vp_ref[...]
        vbuf[BLOCK:] = v_ref[...]
        if gqa:
            kroll, vroll = refs[10:12]
            kroll[...] = pltpu.roll(kbuf[...], HEAD_DIM, 1)
            vroll[...] = pltpu.roll(vbuf[...], HEAD_DIM, 1)
        half = _half_masks(BLOCK)
        col_prev = (lax.broadcasted_iota(jnp.int32, (1, 2 * BLOCK), 1) < BLOCK).astype(F32)

        def sub(a, carry):
            r0 = pl.multiple_of(a * BLOCK, BLOCK)
            pen = jnp.where((i == 0) & (a == 0), NEG, 0.0)
            b = bias_ref[...] + pen * col_prev
            m_cols, l_cols = [], []
            for p in range(npair):
                lanes = slice(p * LANES, (p + 1) * LANES)
                qp = q_ref[pl.ds(r0, BLOCK), lanes]
                o_h = []
                for e in range(2):
                    h = 2 * p + e
                    pk, ek = _kv_place(h, gqa)
                    klanes = slice(pk * LANES, (pk + 1) * LANES)
                    kw = (kbuf if ek == e else kroll)[pl.ds(r0, 2 * BLOCK), klanes]
                    vw = (vbuf if ek == e else vroll)[pl.ds(r0, 2 * BLOCK), klanes]
                    qm = jnp.where(half[e], qp, jnp.zeros_like(qp))
                    s = _dot_nt(qm, kw) + b
                    m = jnp.max(s, axis=1, keepdims=True)
                    if has_sink:
                        m = jnp.maximum(m, sink_ref[h])
                    pe = jnp.exp(s - m)
                    l = jnp.sum(pe, axis=1, keepdims=True)
                    if has_sink:
                        l = l + jnp.exp(sink_ref[h] - m)
                    pv = _dot(pe.astype(BF16), vw)
                    o_h.append(pv * (1.0 / l))
                    m_cols.append(m)
                    l_cols.append(l)
                o_ref[pl.ds(r0, BLOCK), lanes] = jnp.where(half[0], o_h[0], o_h[1]).astype(BF16)
            lse_ref[pl.ds(r0, BLOCK), :] = _per_head(m_cols) + jnp.log(_per_head(l_cols, 1.0))
            return carry

        lax.fori_loop(0, ns, sub, 0, unroll=True)

    main = lambda w: pl.BlockSpec((None, tq, w), lambda r, i: (r, i, 0))
    prev = lambda w: pl.BlockSpec((None, BLOCK, w), lambda r, i: (r, jnp.maximum(i * ns - 1, 0), 0))
    in_specs = [main(wq), main(wk), prev(wk), main(wk), prev(wk), pl.BlockSpec(bias.shape, lambda r, i: (0, 0))]
    args = [q, k, k, v, v, bias]
    if has_sink:
        in_specs = [pl.BlockSpec(memory_space=pltpu.SMEM)] + in_specs
        args = [sink] + args
    scratch = [pltpu.VMEM((tq + BLOCK, wk), BF16)] * (4 if gqa else 2)
    return pl.pallas_call(
        body, name=name, grid=(dil, nt), in_specs=in_specs,
        out_specs=[main(wq), main(LANES)],
        out_shape=[jax.ShapeDtypeStruct((dil, length, wq), BF16), jax.ShapeDtypeStruct((dil, length, LANES), F32)],
        scratch_shapes=scratch,
    )(*args)


def _band_bwd(q, k, v, do, lse, delta, *, max_dist, name):
    dil, length, wq = q.shape
    wk = k.shape[2]
    gqa = wk != wq
    tq = min(512, length)
    ns, nt = tq // BLOCK, length // tq
    npair = wq // LANES
    nblocks = length // BLOCK
    bias = _band_bias(max_dist, transposed=True)

    def body(q_ref, qn_ref, do_ref, don_ref, lse_ref, lsen_ref, dl_ref, dln_ref, k_ref, v_ref, bias_ref,
             dq_ref, dk_ref, dv_ref, qbuf, dobuf, stat_l, stat_d, dqacc, *rolled):
        i = pl.program_id(1)
        qbuf[0:tq] = q_ref[...]
        qbuf[tq:] = qn_ref[...]
        dobuf[0:tq] = do_ref[...]
        dobuf[tq:] = don_ref[...]
        if gqa:
            kroll, vroll = rolled
            kroll[...] = pltpu.roll(k_ref[...], HEAD_DIM, 1)
            vroll[...] = pltpu.roll(v_ref[...], HEAD_DIM, 1)
        for a in range(ns):
            rows = slice(a * BLOCK, (a + 1) * BLOCK)
            stat_l[a] = _rows_to_lanes(lse_ref[rows, :])
            stat_d[a] = _rows_to_lanes(dl_ref[rows, :])
        stat_l[ns] = _rows_to_lanes(lsen_ref[...])
        stat_d[ns] = _rows_to_lanes(dln_ref[...])

        @pl.when(i == 0)
        def _():
            dqacc[0:BLOCK] = jnp.zeros((BLOCK, wq), F32)

        @pl.when(i > 0)
        def _():
            dqacc[0:BLOCK] = dqacc[tq:tq + BLOCK]

        dqacc[BLOCK:] = jnp.zeros((tq, wq), F32)
        half = _half_masks(BLOCK)
        half2 = _half_masks(2 * BLOCK)
        col_next = (lax.broadcasted_iota(jnp.int32, (1, 2 * BLOCK), 1) >= BLOCK).astype(F32)

        def sub(b, carry):
            r0 = pl.multiple_of(b * BLOCK, BLOCK)
            pen = jnp.where((i == nt - 1) & (b == ns - 1), NEG, 0.0)
            bt = bias_ref[...] + pen * col_next
            acc = {}
            for p in range(npair):
                lanes = slice(p * LANES, (p + 1) * LANES)
                qw = qbuf[pl.ds(r0, 2 * BLOCK), lanes]
                dow = dobuf[pl.ds(r0, 2 * BLOCK), lanes]
                for e in range(2):
                    h = 2 * p + e
                    pk, ek = _kv_place(h, gqa)
                    klanes = slice(pk * LANES, (pk + 1) * LANES)
                    kb = (k_ref if ek == e else kroll)[pl.ds(r0, BLOCK), klanes]
                    vb = (v_ref if ek == e else vroll)[pl.ds(r0, BLOCK), klanes]
                    qm = jnp.where(half2[e], qw, jnp.zeros_like(qw))
                    dom = jnp.where(half2[e], dow, jnp.zeros_like(dow))
                    lrow = jnp.concatenate([stat_l[b, h:h + 1, :], stat_l[b + 1, h:h + 1, :]], axis=1)
                    drow = jnp.concatenate([stat_d[b, h:h + 1, :], stat_d[b + 1, h:h + 1, :]], axis=1)
                    st = _dot_nt(kb, qm) + bt
                    pt = jnp.exp(st - lrow)
                    dv_c = _dot(pt.astype(BF16), dom)
                    dpt = _dot_nt(vb, dom)
                    dsb = (pt * (dpt - drow)).astype(BF16)
                    dk_c = _dot(dsb, qm)
                    kbm = jnp.where(half[e], kb, jnp.zeros_like(kb))
                    dqacc[pl.ds(r0, 2 * BLOCK), lanes] += _dot_tn(dsb, kbm)
                    key = (pk, ek == e)
                    if key in acc:
                        acc[key] = (acc[key][0] + dk_c, acc[key][1] + dv_c)
                    else:
                        acc[key] = (dk_c, dv_c)
                if not gqa:
                    dk_ref[pl.ds(r0, BLOCK), lanes] = acc[(p, True)][0].astype(BF16)
                    dv_ref[pl.ds(r0, BLOCK), lanes] = acc[(p, True)][1].astype(BF16)
            if gqa:
                dk_al, dv_al = acc[(0, True)]
                dk_mis, dv_mis = acc[(0, False)]
                dk_ref[pl.ds(r0, BLOCK), :] = (dk_al + pltpu.roll(dk_mis, HEAD_DIM, 1)).astype(BF16)
                dv_ref[pl.ds(r0, BLOCK), :] = (dv_al + pltpu.roll(dv_mis, HEAD_DIM, 1)).astype(BF16)
            return carry

        lax.fori_loop(0, ns, sub, 0, unroll=True)
        dq_ref[...] = dqacc[0:tq].astype(BF16)

    main = lambda w: pl.BlockSpec((None, tq, w), lambda r, i: (r, i, 0))
    nxt = lambda w: pl.BlockSpec((None, BLOCK, w), lambda r, i: (r, jnp.minimum((i + 1) * ns, nblocks - 1), 0))
    scratch = [pltpu.VMEM((tq + BLOCK, wq), BF16), pltpu.VMEM((tq + BLOCK, wq), BF16),
               pltpu.VMEM((ns + 1, 8, LANES), F32), pltpu.VMEM((ns + 1, 8, LANES), F32),
               pltpu.VMEM((tq + BLOCK, wq), F32)]
    if gqa:
        scratch = scratch + [pltpu.VMEM((tq, wk), BF16)] * 2
    return pl.pallas_call(
        body, name=name, grid=(dil, nt),
        in_specs=[main(wq), nxt(wq), main(wq), nxt(wq), main(LANES), nxt(LANES), main(LANES), nxt(LANES),
                  main(wk), main(wk), pl.BlockSpec(bias.shape, lambda r, i: (0, 0))],
        out_specs=[main(wq), main(wk), main(wk)],
        out_shape=[jax.ShapeDtypeStruct((dil, length, wq), BF16), jax.ShapeDtypeStruct((dil, length, wk), BF16),
                   jax.ShapeDtypeStruct((dil, length, wk), BF16)],
        scratch_shapes=scratch,
        compiler_params=pltpu.CompilerParams(dimension_semantics=("arbitrary", "arbitrary")),
    )(q, q, do, do, lse, lse, delta, delta, k, v, bias)


def _mem_attn_fwd(q, mk, mv):
    seq = q.shape[0]
    tq = min(512, seq)
    ns = tq // BLOCK

    def body(q_ref, mk_ref, mv_ref, o_ref, lse_ref):
        half = _half_masks(BLOCK)

        def sub(a, carry):
            r0 = pl.multiple_of(a * BLOCK, BLOCK)
            m_cols, l_cols = [], []
            for p in range(C_W // LANES):
                lanes = slice(p * LANES, (p + 1) * LANES)
                qp = q_ref[pl.ds(r0, BLOCK), lanes]
                o_h = []
                for e in range(2):
                    qm = jnp.where(half[e], qp, jnp.zeros_like(qp))
                    s = _dot_nt(qm, mk_ref[:, lanes])
                    m = jnp.max(s, axis=1, keepdims=True)
                    pe = jnp.exp(s - m)
                    l = jnp.sum(pe, axis=1, keepdims=True)
                    o_h.append(_dot(pe.astype(BF16), mv_ref[:, lanes]) * (1.0 / l))
                    m_cols.append(m)
                    l_cols.append(l)
                o_ref[pl.ds(r0, BLOCK), lanes] = jnp.where(half[0], o_h[0], o_h[1]).astype(BF16)
            lse_ref[pl.ds(r0, BLOCK), :] = _per_head(m_cols) + jnp.log(_per_head(l_cols, 1.0))
            return carry

        lax.fori_loop(0, ns, sub, 0, unroll=True)

    row = lambda w: pl.BlockSpec((tq, w), lambda i: (i, 0))
    full = pl.BlockSpec((N_MEM, C_W), lambda i: (0, 0))
    return pl.pallas_call(
        body, name="mem_attn_fwd", grid=(seq // tq,), in_specs=[row(C_W), full, full],
        out_specs=[row(C_W), row(LANES)],
        out_shape=[jax.ShapeDtypeStruct((seq, C_W), BF16), jax.ShapeDtypeStruct((seq, LANES), F32)],
    )(q, mk, mv)


def _mem_attn_bwd(q, mk, mv, do, lse, delta):
    seq = q.shape[0]
    tq = min(512, seq)
    ns = tq // BLOCK
    npair = C_W // LANES

    def body(q_ref, mk_ref, mv_ref, do_ref, lse_ref, dl_ref, dq_ref, dmk_ref, dmv_ref, stat_l, stat_d):
        @pl.when(pl.program_id(0) == 0)
        def _():
            dmk_ref[...] = jnp.zeros_like(dmk_ref)
            dmv_ref[...] = jnp.zeros_like(dmv_ref)

        for a in range(ns):
            rows = slice(a * BLOCK, (a + 1) * BLOCK)
            stat_l[a] = _rows_to_lanes(lse_ref[rows, :])
            stat_d[a] = _rows_to_lanes(dl_ref[rows, :])
        half = _half_masks(BLOCK)
        halfk = _half_masks(N_MEM)

        def sub(a, carry):
            r0 = pl.multiple_of(a * BLOCK, BLOCK)
            for p in range(npair):
                lanes = slice(p * LANES, (p + 1) * LANES)
                qp = q_ref[pl.ds(r0, BLOCK), lanes]
                dop = do_ref[pl.ds(r0, BLOCK), lanes]
                kb, vb = mk_ref[:, lanes], mv_ref[:, lanes]
                dq_pair = None
                for e in range(2):
                    h = 2 * p + e
                    qm = jnp.where(half[e], qp, jnp.zeros_like(qp))
                    dom = jnp.where(half[e], dop, jnp.zeros_like(dop))
                    st = _dot_nt(kb, qm)
                    pt = jnp.exp(st - stat_l[a, h:h + 1, :])
                    dmv_ref[:, lanes] += _dot(pt.astype(BF16), dom)
                    dpt = _dot_nt(vb, dom)
                    dsb = (pt * (dpt - stat_d[a, h:h + 1, :])).astype(BF16)
                    dmk_ref[:, lanes] += _dot(dsb, qm)
                    kbm = jnp.where(halfk[e], kb, jnp.zeros_like(kb))
                    dq_c = _dot_tn(dsb, kbm)
                    dq_pair = dq_c if dq_pair is None else dq_pair + dq_c
                dq_ref[pl.ds(r0, BLOCK), lanes] = dq_pair.astype(BF16)
            return carry

        lax.fori_loop(0, ns, sub, 0, unroll=True)

    row = lambda w: pl.BlockSpec((tq, w), lambda i: (i, 0))
    full = pl.BlockSpec((N_MEM, C_W), lambda i: (0, 0))
    return pl.pallas_call(
        body, name="mem_attn_bwd", grid=(seq // tq,),
        in_specs=[row(C_W), full, full, row(C_W), row(LANES), row(LANES)], out_specs=[row(C_W), full, full],
        out_shape=[jax.ShapeDtypeStruct((seq, C_W), BF16), jax.ShapeDtypeStruct((N_MEM, C_W), F32),
                   jax.ShapeDtypeStruct((N_MEM, C_W), F32)],
        scratch_shapes=[pltpu.VMEM((ns, 8, LANES), F32)] * 2,
        compiler_params=pltpu.CompilerParams(dimension_semantics=("arbitrary",)),
    )(q, mk, mv, do, lse, delta)


def _silu_and_grad(g):
    s = 1.0 / (1.0 + jnp.exp(-g))
    return g * s, s * (1.0 + g * (1.0 - s))


def _post(x, target, post_norm, w_out, sink_row, oa, lse_a, ga, ob_list, lseb_list, gb, oc, gc):
    seq = x.shape[0]
    tm = min(256, seq)
    inv_d = 1.0 / D_MODEL
    nd = len(B_DILS)

    def body(*refs):
        (x_ref, t_ref, gp_ref, w_ref, sink_ref, oa_ref, lsea_ref, ga_ref), refs = refs[:8], refs[8:]
        ob_refs, lb_refs, (gb_ref, oc_ref, gc_ref), refs = refs[:nd], refs[nd:2 * nd], refs[2 * nd:2 * nd + 3], refs[2 * nd + 3:]
        (g_ref, doa_ref, dla_ref, dga_ref), refs = refs[:4], refs[4:]
        dob_refs, lsec_refs, dlb_refs, refs = refs[:nd], refs[nd:2 * nd], refs[2 * nd:3 * nd], refs[3 * nd:]
        (dgb_ref, doc_ref, dlc_ref, dgc_ref, gw_ref, gpost_ref, gsink_ref, loss_ref), refs = refs[:8], refs[8:]
        ycat, obufs, lbufs, st_do, st_l, st_d = refs[0], refs[1:nd], refs[nd:2 * nd - 1], refs[2 * nd - 1], refs[2 * nd], refs[2 * nd + 1]

        @pl.when(pl.program_id(0) == 0)
        def _():
            gw_ref[...] = jnp.zeros_like(gw_ref)
            gpost_ref[...] = jnp.zeros_like(gpost_ref)
            gsink_ref[...] = jnp.zeros_like(gsink_ref)
            loss_ref[...] = jnp.zeros_like(loss_ref)

        o_i, l_i = [ob_refs[0][0].astype(F32)], [lb_refs[0][0]]
        for k in range(1, nd):
            _from_residues(ob_refs[k], obufs[k - 1], B_DILS[k])
            _from_residues(lb_refs[k], lbufs[k - 1], B_DILS[k])
            o_i.append(_stage_read(obufs[k - 1]))
            l_i.append(_stage_read(lbufs[k - 1]))
        mx = l_i[0]
        for l in l_i[1:]:
            mx = jnp.maximum(mx, l)
        w_i = [jnp.exp(l - mx) for l in l_i]
        z = w_i[0]
        for w in w_i[1:]:
            z = z + w
        lse_b = mx + jnp.log(z)
        expand = _head_expand_matrix(B_W)
        inv_z = 1.0 / z
        ob = None
        for w, o in zip(w_i, o_i):
            term = _dot_split(w * inv_z, expand, 2) * o
            ob = term if ob is None else ob + term
        oa, oc = oa_ref[...].astype(F32), oc_ref[...].astype(F32)
        sa, dsa = _silu_and_grad(ga_ref[...].astype(F32))
        sb, dsb = _silu_and_grad(gb_ref[...].astype(F32))
        sc, dsc = _silu_and_grad(gc_ref[...].astype(F32))
        ycat[:, 0:A_W] = (oa * sa).astype(BF16)
        ycat[:, A_W:A_W + B_W] = (ob * sb).astype(BF16)
        ycat[:, A_W + B_W:] = (oc * sc).astype(BF16)
        yc = ycat[...]
        y2 = _dot(yc, w_ref[...])
        r = lax.rsqrt(jnp.mean(y2 * y2, axis=-1, keepdims=True) + RMS_EPS)
        zhat = y2 * r
        gp = gp_ref[...]
        err = x_ref[...] + zhat * gp - t_ref[...]
        loss_ref[...] += jnp.sum(err * err) * (0.5 * inv_d)
        g = err * inv_d
        g_ref[...] = g
        gpost_ref[...] += jnp.sum(g * zhat, axis=0, keepdims=True)
        a = g * gp
        dy2 = (r * (a - zhat * jnp.mean(a * zhat, axis=-1, keepdims=True))).astype(BF16)
        gw_ref[...] += _dot_tn(yc, dy2)
        dycat = _dot_nt(dy2, w_ref[...])
        dya, dyb, dyc = dycat[:, 0:A_W], dycat[:, A_W:A_W + B_W], dycat[:, A_W + B_W:]
        doa, dob, doc = dya * sa, dyb * sb, dyc * sc
        doa_ref[...] = doa.astype(BF16)
        doc_ref[...] = doc.astype(BF16)
        dga_ref[...] = (dya * oa * dsa).astype(BF16)
        dgb_ref[...] = (dyb * ob * dsb).astype(BF16)
        dgc_ref[...] = (dyc * oc * dsc).astype(BF16)
        dl_a = _dot_split(doa * oa, _head_sum_matrix(A_W), 3)
        dla_ref[...] = dl_a
        dlc_ref[...] = _dot_split(doc * oc, _head_sum_matrix(C_W), 3)
        gsink_ref[...] += jnp.sum(jnp.exp(sink_ref[...] - lsea_ref[...]) * dl_a, axis=0, keepdims=True)
        _stage_write(st_do, dob)
        _stage_write(st_l, lse_b)
        _stage_write(st_d, _dot_split(dob * ob, _head_sum_matrix(B_W), 3))
        for k, dil in enumerate(B_DILS):
            _to_residues(st_do, dob_refs[k], dil)
            _to_residues(st_l, lsec_refs[k], dil)
            _to_residues(st_d, dlb_refs[k], dil)

    row = lambda w: pl.BlockSpec((tm, w), lambda i: (i, 0))
    full = lambda shape: pl.BlockSpec(shape, lambda i: (0,) * len(shape))
    res_specs = lambda w: [_residue_spec(d, tm, w) for d in B_DILS]
    res_shapes = lambda w, dt: [jax.ShapeDtypeStruct((d, seq // d, w), dt) for d in B_DILS]
    ins = [x, target, post_norm, w_out, sink_row, oa, lse_a, ga, *ob_list, *lseb_list, gb, oc, gc]
    in_specs = ([row(D_MODEL), row(D_MODEL), full((1, D_MODEL)), full((D_MODEL, D_MODEL)), full((1, LANES)),
                 row(A_W), row(LANES), row(A_W)] + res_specs(B_W) + res_specs(LANES) + [row(B_W), row(C_W), row(C_W)])
    out_shape = ([jax.ShapeDtypeStruct((seq, D_MODEL), F32), jax.ShapeDtypeStruct((seq, A_W), BF16),
                  jax.ShapeDtypeStruct((seq, LANES), F32), jax.ShapeDtypeStruct((seq, A_W), BF16)]
                 + res_shapes(B_W, BF16) + res_shapes(LANES, F32) + res_shapes(LANES, F32)
                 + [jax.ShapeDtypeStruct((seq, B_W), BF16), jax.ShapeDtypeStruct((seq, C_W), BF16),
                    jax.ShapeDtypeStruct((seq, LANES), F32), jax.ShapeDtypeStruct((seq, C_W), BF16),
                    jax.ShapeDtypeStruct((D_MODEL, D_MODEL), F32), jax.ShapeDtypeStruct((1, D_MODEL), F32),
                    jax.ShapeDtypeStruct((1, LANES), F32), jax.ShapeDtypeStruct((1, LANES), F32)])
    out_specs = ([row(D_MODEL), row(A_W), row(LANES), row(A_W)] + res_specs(B_W) + res_specs(LANES) + res_specs(LANES)
                 + [row(B_W), row(C_W), row(LANES), row(C_W),
                    full((D_MODEL, D_MODEL)), full((1, D_MODEL)), full((1, LANES)), full((1, LANES))])
    scratch = ([pltpu.VMEM((tm, D_MODEL), BF16)] + [_stage(tm, B_W)] * (nd - 1) + [_stage(tm, LANES)] * (nd - 1)
               + [_stage(tm, B_W), _stage(tm, LANES), _stage(tm, LANES)])
    res = pl.pallas_call(
        body, name="post", grid=(seq // tm,), in_specs=in_specs, out_specs=out_specs, out_shape=out_shape,
        scratch_shapes=scratch,
        compiler_params=pltpu.CompilerParams(dimension_semantics=("arbitrary",)),
    )(*ins)
    out = dict(g=res[0], doa=res[1], dl_a=res[2], dga=res[3], dob=res[4:4 + nd], lse_b=res[4 + nd:4 + 2 * nd],
               dl_b=res[4 + 2 * nd:4 + 3 * nd])
    rest = res[4 + 3 * nd:]
    out.update(dgb=rest[0], doc=rest[1], dl_c=rest[2], dgc=rest[3], gw_out=rest[4], gpost=rest[5], gsink=rest[6],
               loss=rest[7])
    return out


def _dx(x, g, pre_norm, w_in_g, rope, nat, res):
    seq = x.shape[0]
    tm = min(256, seq)
    nd = len(B_DILS)
    nat_list = [nat[n] for n in _NATURAL]
    res_list = [a for n in _DILATED for a in res[n]]

    def body(x_ref, g_ref, gp_ref, w_ref, c_ref, sm_ref, sp_ref, *refs):
        nat_refs = dict(zip(_NATURAL, refs[:len(_NATURAL)]))
        refs = refs[len(_NATURAL):]
        res_refs = {n: refs[nd * k:nd * (k + 1)] for k, n in enumerate(_DILATED)}
        refs = refs[nd * len(_DILATED):]
        dproj_ref, gx_ref, gpre_ref = refs[:3]
        bufs = {n: refs[3 + (nd - 1) * k:3 + (nd - 1) * (k + 1)] for k, n in enumerate(_DILATED)}

        @pl.when(pl.program_id(0) == 0)
        def _():
            gpre_ref[...] = jnp.zeros_like(gpre_ref)

        for n in _DILATED:
            for k in range(1, nd):
                _from_residues(res_refs[n][k], bufs[n][k - 1], B_DILS[k])
        c, sm, sp = c_ref[...], -sm_ref[...], -sp_ref[...]
        for blk, (name, off, roped, scaled) in enumerate(_PROJ_LAYOUT):
            lanes = slice(off, off + LANES)
            if name in nat_refs:
                piece = nat_refs[name][:, lanes].astype(F32)
            else:
                piece = res_refs[name][0][0, :, lanes].astype(F32)
                for buf in bufs[name]:
                    piece = piece + buf[off // LANES]
            if roped:
                piece = _rope(piece, c, sm, sp)
            if scaled:
                piece = piece * SCALE
            dproj_ref[:, blk * LANES:(blk + 1) * LANES] = piece.astype(BF16)
        du = None
        for j in range(N_CHIPS):
            part = _dot_nt(dproj_ref[:, j * SHARD_IN:(j + 1) * SHARD_IN], w_ref[j])
            du = part if du is None else du + part
        xv = x_ref[...]
        r = lax.rsqrt(jnp.mean(xv * xv, axis=-1, keepdims=True) + RMS_EPS)
        xhat = xv * r
        gpre_ref[...] += jnp.sum(du * xhat, axis=0, keepdims=True)
        a = du * gp_ref[...]
        gx_ref[...] = g_ref[...] + r * (a - xhat * jnp.mean(a * xhat, axis=-1, keepdims=True))

    row = lambda w: pl.BlockSpec((tm, w), lambda i: (i, 0))
    full = lambda a: pl.BlockSpec(a.shape, lambda i: (0,) * a.ndim)
    in_specs = ([row(D_MODEL), row(D_MODEL), full(pre_norm), full(w_in_g), row(LANES), row(LANES), row(LANES)]
                + [row(a.shape[1]) for a in nat_list]
                + [_residue_spec(d, tm, B_W) for _ in _DILATED for d in B_DILS])
    return pl.pallas_call(
        body, name="dx", grid=(seq // tm,), in_specs=in_specs,
        out_specs=[row(D_IN), row(D_MODEL), pl.BlockSpec((1, D_MODEL), lambda i: (0, 0))],
        out_shape=[jax.ShapeDtypeStruct((seq, D_IN), BF16), jax.ShapeDtypeStruct((seq, D_MODEL), F32),
                   jax.ShapeDtypeStruct((1, D_MODEL), F32)],
        scratch_shapes=[_stage(tm, B_W)] * ((nd - 1) * len(_DILATED)),
        compiler_params=pltpu.CompilerParams(dimension_semantics=("arbitrary",)),
    )(x, g, pre_norm, w_in_g, *rope, *nat_list, *res_list)


def _grad_w_in(ut, dproj):
    seq = ut.shape[1]
    tk = min(1024, seq)

    def body(ut_ref, dp_ref, out_ref):
        @pl.when(pl.program_id(1) == 0)
        def _():
            out_ref[...] = jnp.zeros_like(out_ref)

        out_ref[...] += _dot(ut_ref[...], dp_ref[...])

    return pl.pallas_call(
        body, name="grad_w_in", grid=(N_CHIPS, seq // tk),
        in_specs=[pl.BlockSpec((D_MODEL, tk), lambda j, i: (0, i)), pl.BlockSpec((tk, SHARD_IN), lambda j, i: (i, j))],
        out_specs=pl.BlockSpec((None, D_MODEL, SHARD_IN), lambda j, i: (j, 0, 0)),
        out_shape=jax.ShapeDtypeStruct((N_CHIPS, D_MODEL, SHARD_IN), F32),
        compiler_params=pltpu.CompilerParams(dimension_semantics=("arbitrary", "arbitrary")),
    )(ut, dproj)


def _pair_exchange(grads):
    n = len(grads)

    def body(*refs):
        srcs, outs = refs[:n], refs[n:2 * n]
        send_sems, recv_sems = refs[2 * n:]
        x, y, c = lax.axis_index("x"), lax.axis_index("y"), lax.axis_index("c")
        copies = []
        for t in range(n):
            rows = grads[t].shape[1] // 2
            copies.append(pltpu.make_async_remote_copy(
                src_ref=srcs[t].at[:, pl.ds((1 - c) * rows, rows)], dst_ref=outs[t],
                send_sem=send_sems.at[t], recv_sem=recv_sems.at[t], device_id=(x, y, 1 - c), device_id_type=MESH))
        for cp in copies:
            cp.start()
        for cp in copies:
            cp.wait()

    any_spec = pl.BlockSpec(memory_space=pl.ANY)
    return pl.pallas_call(
        body, name="pair_exchange",
        out_shape=[jax.ShapeDtypeStruct((g.shape[0], g.shape[1] // 2, g.shape[2]), g.dtype) for g in grads],
        in_specs=[any_spec] * n, out_specs=[any_spec] * n,
        scratch_shapes=[pltpu.SemaphoreType.DMA((n,)), pltpu.SemaphoreType.DMA((n,))],
    )(*grads)


def _pair_add(core, own, got):
    nchip, rows2, width = own.shape
    rows = rows2 // 2
    tr = min(128, rows)
    nb = rows // tr

    def body(core_ref, own_ref, got_ref, out_ref):
        out_ref[...] = (own_ref[...] + got_ref[...]).astype(BF16)

    grid_spec = pltpu.PrefetchScalarGridSpec(
        num_scalar_prefetch=1, grid=(nchip, nb),
        in_specs=[pl.BlockSpec((None, tr, width), lambda k, i, core_ref: (k, core_ref[0] * nb + i, 0)),
                  pl.BlockSpec((None, tr, width), lambda k, i, core_ref: (k, i, 0))],
        out_specs=pl.BlockSpec((None, tr, width), lambda k, i, core_ref: (k, i, 0)))
    return pl.pallas_call(
        body, name=f"pair_add_{width}", grid_spec=grid_spec,
        out_shape=jax.ShapeDtypeStruct((nchip, rows, width), BF16),
    )(core, own, got)


def _chip_exchange(parts, small):
    n = len(parts)

    def body(*refs):
        srcs, small_ref = refs[:n], refs[n]
        outs, small_out = refs[n + 1:2 * n + 1], refs[2 * n + 1]
        send_sems, recv_sems, local_sems = refs[2 * n + 2:]
        x, y, c = lax.axis_index("x"), lax.axis_index("y"), lax.axis_index("c")
        my_chip = 2 * x + y
        me = 4 * x + 2 * y + c
        chips = [(1 - x, y), (x, 1 - y), (1 - x, 1 - y)]
        local = [pltpu.make_async_copy(srcs[t].at[my_chip], outs[t].at[my_chip], local_sems.at[t]) for t in range(n)]
        local.append(pltpu.make_async_copy(small_ref, small_out.at[me], local_sems.at[n]))
        for cp in local:
            cp.start()
        sent = []
        for j, (cx, cy) in enumerate(chips):
            for t in range(n):
                k = n * j + t
                sent.append(pltpu.make_async_remote_copy(
                    src_ref=srcs[t].at[2 * cx + cy], dst_ref=outs[t].at[my_chip], send_sem=send_sems.at[k],
                    recv_sem=recv_sems.at[k], device_id=(cx, cy, c), device_id_type=MESH))
        peers = [(x, y, 1 - c)] + [(cx, cy, cc) for (cx, cy) in chips for cc in (c, 1 - c)]
        for j, peer in enumerate(peers):
            k = 3 * n + j
            sent.append(pltpu.make_async_remote_copy(
                src_ref=small_ref, dst_ref=small_out.at[me], send_sem=send_sems.at[k], recv_sem=recv_sems.at[k],
                device_id=peer, device_id_type=MESH))
        for cp in sent:
            cp.start()
        for cp in sent:
            cp.wait()
        for cp in local:
            cp.wait()

    any_spec = pl.BlockSpec(memory_space=pl.ANY)
    nsem = 3 * n + 7
    return pl.pallas_call(
        body, name="chip_exchange",
        out_shape=[jax.ShapeDtypeStruct(p.shape, p.dtype) for p in parts]
        + [jax.ShapeDtypeStruct((8,) + small.shape, small.dtype)],
        in_specs=[any_spec] * (n + 1), out_specs=[any_spec] * (n + 1),
        scratch_shapes=[pltpu.SemaphoreType.DMA((nsem,)), pltpu.SemaphoreType.DMA((nsem,)),
                        pltpu.SemaphoreType.DMA((n + 1,))],
    )(*parts, small)


def _slot_sum(slots, name):
    ns, rows, width = slots.shape
    tr = min(128, rows)

    def body(in_ref, out_ref):
        acc = in_ref[0].astype(F32)
        for s in range(1, ns):
            acc = acc + in_ref[s].astype(F32)
        out_ref[...] = acc

    return pl.pallas_call(
        body, name=name, grid=(rows // tr,),
        in_specs=[pl.BlockSpec((ns, tr, width), lambda i: (0, i, 0))],
        out_specs=pl.BlockSpec((tr, width), lambda i: (i, 0)),
        out_shape=jax.ShapeDtypeStruct((rows, width), F32),
    )(slots)


def _pair_gather(halves):
    n = len(halves)

    def body(*refs):
        srcs, outs = refs[:n], refs[n:2 * n]
        send_sems, recv_sems, local_sems = refs[2 * n:]
        x, y, c = lax.axis_index("x"), lax.axis_index("y"), lax.axis_index("c")
        copies, local = [], []
        for t in range(n):
            mine = outs[t].at[c]
            local.append(pltpu.make_async_copy(srcs[t], mine, local_sems.at[t]))
            copies.append(pltpu.make_async_remote_copy(
                src_ref=srcs[t], dst_ref=mine, send_sem=send_sems.at[t], recv_sem=recv_sems.at[t],
                device_id=(x, y, 1 - c), device_id_type=MESH))
        for cp in local + copies:
            cp.start()
        for cp in copies:
            cp.wait()
        for cp in local:
            cp.wait()

    any_spec = pl.BlockSpec(memory_space=pl.ANY)
    res = pl.pallas_call(
        body, name="pair_gather",
        out_shape=[jax.ShapeDtypeStruct((2,) + h.shape, h.dtype) for h in halves],
        in_specs=[any_spec] * n, out_specs=[any_spec] * n,
        scratch_shapes=[pltpu.SemaphoreType.DMA((n,)), pltpu.SemaphoreType.DMA((n,)), pltpu.SemaphoreType.DMA((n,))],
    )(*halves)
    return [r.reshape(2 * h.shape[0], h.shape[1]) for r, h in zip(res, halves)]


def _adamw(w, g, m, v, name):
    rows, width = w.shape
    tr = min(256, rows)
    c1 = 1.0 / (1.0 - ADAM_B1 ** ADAM_STEP)
    c2 = 1.0 / (1.0 - ADAM_B2 ** ADAM_STEP)

    def body(w_ref, g_ref, m_ref, v_ref, d_ref, nm_ref, nv_ref):
        gv = g_ref[...]
        nm = ADAM_B1 * m_ref[...] + (1.0 - ADAM_B1) * gv
        nv = ADAM_B2 * v_ref[...] + (1.0 - ADAM_B2) * (gv * gv)
        nm_ref[...] = nm
        nv_ref[...] = nv
        d_ref[...] = -ADAM_LR * ((nm * c1) / (jnp.sqrt(nv * c2) + ADAM_EPS) + ADAM_WD * w_ref[...])

    spec = pl.BlockSpec((tr, width), lambda i: (i, 0))
    return pl.pallas_call(
        body, name=name, grid=(rows // tr,), in_specs=[spec] * 4, out_specs=[spec] * 3,
        out_shape=[jax.ShapeDtypeStruct(w.shape, F32)] * 3,
    )(w, g, m, v)


def _local_step(x, mem, target, pre_norm, sink_a, mem_norm, post_norm, w_in_g, w_out, w_mkv):
    seq = x.shape[0]
    rope = _rope_tables(seq)
    mk, mv = _mem_kv(mem, mem_norm, w_mkv)
    pr = _pre_proj(x, pre_norm, w_in_g, rope)
    sink = sink_a.reshape(-1)
    qa, ka, va = pr["qa"][None], pr["ka"][None], pr["va"][None]
    oa, lse_a = _band_fwd(qa, ka, va, sink, max_dist=A_WINDOW - 1, name="swa_fwd")
    ob_list, lseb_list = [], []
    for k, (win, dil) in enumerate(B_CONFIGS):
        o_i, l_i = _band_fwd(pr["qb"][k], pr["kb"][k], pr["vb"][k], None, max_dist=win // dil, name=f"dil{dil}_fwd")
        ob_list.append(o_i)
        lseb_list.append(l_i)
    oc, lse_c = _mem_attn_fwd(pr["qc"], mk, mv)
    sink_row = jnp.pad(sink, (0, LANES - sink.shape[0])).reshape(1, LANES)
    po = _post(x, target, post_norm, w_out, sink_row, oa[0], lse_a[0], pr["ga"], ob_list, lseb_list, pr["gb"], oc,
               pr["gc"])
    dqc, dmk, dmv = _mem_attn_bwd(pr["qc"], mk, mv, po["doc"], lse_c, po["dl_c"])
    dqa, dka, dva = _band_bwd(qa, ka, va, po["doa"][None], lse_a, po["dl_a"][None], max_dist=A_WINDOW - 1,
                              name="swa_bwd")
    res = dict(qb=[], kb=[], vb=[])
    for k, (win, dil) in enumerate(B_CONFIGS):
        dq_i, dk_i, dv_i = _band_bwd(pr["qb"][k], pr["kb"][k], pr["vb"][k], po["dob"][k], po["lse_b"][k],
                                     po["dl_b"][k], max_dist=win // dil, name=f"dil{dil}_bwd")
        res["qb"].append(dq_i)
        res["kb"].append(dk_i)
        res["vb"].append(dv_i)
    nat = dict(qa=dqa[0], ka=dka[0], va=dva[0], ga=po["dga"], gb=po["dgb"], qc=dqc, gc=po["dgc"])
    dproj, grad_x, gpre = _dx(x, po["g"], pre_norm, w_in_g, rope, nat, res)
    gw_in = _grad_w_in(pr["ut"], dproj)
    gw_mkv, gmem = _mem_kv_bwd(mem, mem_norm, w_mkv, dmk, dmv)
    gsink = -po["gsink"][0, :sink.shape[0]]
    return dict(loss=po["loss"][0, 0], grad_x=grad_x, gw_in=gw_in, gw_out=po["gw_out"], gw_mkv=gw_mkv,
                gpre=gpre, gpost=po["gpost"], gmem=gmem, gsink=gsink)


def kernel(x, mem, pre_norm, w_in, sink_a, mem_norm, w_mem_kv, w_out, post_norm, loss_target, m_pre_norm, m_w_in, m_sink_a, m_mem_norm, m_w_mem_kv, m_w_out, m_post_norm, v_pre_norm, v_w_in, v_sink_a, v_mem_norm, v_w_mem_kv, v_w_out, v_post_norm):
    w_in_g, w_out_g, w_mkv_g = _gather_weights(w_in[0].astype(BF16), w_out[0].astype(BF16), w_mem_kv[0].astype(BF16))
    loc = _local_step(x[0], mem[0], loss_target[0], pre_norm, sink_a, mem_norm, post_norm,
                      w_in_g.reshape(N_CHIPS, D_MODEL, SHARD_IN), w_out_g.reshape(D_MODEL, D_MODEL),
                      w_mkv_g.reshape(D_MODEL, 2 * C_W))
    loss = lax.psum(loc["loss"], ("x", "y", "c"))

    big = [loc["gw_in"], loc["gw_out"].reshape(N_CHIPS, D_MODEL // N_CHIPS, D_MODEL),
           loc["gw_mkv"].reshape(N_CHIPS, D_MODEL // N_CHIPS, 2 * C_W)]
    small = jnp.concatenate([loc["gpre"], loc["gpost"], loc["gmem"],
                             jnp.pad(loc["gsink"], (0, D_MODEL - loc["gsink"].shape[0])).reshape(1, D_MODEL),
                             jnp.zeros((4, D_MODEL), F32)], axis=0)
    core = lax.axis_index("c").astype(jnp.int32).reshape(1)
    got = _pair_exchange(big)
    parts = [_pair_add(core, own, g) for own, g in zip(big, got)]
    *slots, small_slots = _chip_exchange(parts, small)
    halves = [_slot_sum(s, name=f"chip_sum_{s.shape[2]}") for s in slots]
    g_in, g_out, g_mkv = _pair_gather(halves)
    small_sum = _slot_sum(small_slots, name="device_sum")
    g_pre, g_post, g_mem = small_sum[0:1], small_sum[1:2], small_sum[2:3]
    g_sink = small_sum[3:4, :sink_a.shape[1]]

    d_in, nm_in, nv_in = _adamw(w_in[0], g_in, m_w_in[0], v_w_in[0], "adamw_in")
    d_out, nm_out, nv_out = _adamw(w_out[0], g_out, m_w_out[0], v_w_out[0], "adamw_out")
    d_mkv, nm_mkv, nv_mkv = _adamw(w_mem_kv[0], g_mkv, m_w_mem_kv[0], v_w_mem_kv[0], "adamw_mkv")
    pad6 = lambda a: jnp.pad(a, ((0, 0), (0, D_MODEL - a.shape[1])))
    stack = lambda a, b, c_, d_: jnp.concatenate([a, b, c_, pad6(d_), jnp.zeros((4, D_MODEL), F32)], axis=0)
    d_s, nm_s, nv_s = _adamw(stack(pre_norm, post_norm, mem_norm, sink_a), small_sum,
                             stack(m_pre_norm, m_post_norm, m_mem_norm, m_sink_a),
                             stack(v_pre_norm, v_post_norm, v_mem_norm, v_sink_a), "adamw_small")
    ns_ = sink_a.shape[1]
    unpack = lambda a: (a[0:1], a[3:4, :ns_], a[2:3], a[1:2])
    d_pre, d_sink, d_mem, d_post = unpack(d_s)
    nm_pre, nm_sink, nm_mem, nm_post = unpack(nm_s)
    nv_pre, nv_sink, nv_mem, nv_post = unpack(nv_s)
    lead = lambda a: a[None]
    return (loss, lead(loc["grad_x"]),
            g_pre, lead(g_in), g_sink, g_mem, lead(g_mkv), lead(g_out), g_post,
            d_pre, lead(d_in), d_sink, d_mem, lead(d_mkv), lead(d_out), d_post,
            nm_pre, lead(nm_in), nm_sink, nm_mem, lead(nm_mkv), lead(nm_out), nm_post,
            nv_pre, lead(nv_in), nv_sink, nv_mem, lead(nv_mkv), lead(nv_out), nv_post)
```

```python
import numpy as np
import jax
import jax.numpy as jnp
from jax import lax
from jax.experimental import pallas as pl
from jax.experimental.pallas import tpu as pltpu

F32 = jnp.float32
BF16 = jnp.bfloat16

D_MODEL = 1024
HEAD_DIM = 64
LANES = 128
BLOCK = 128
A_W, A_KV_W, B_W, C_W = 384, 128, 384, 256
N_MEM = 256
D_IN = 3072
N_CHIPS = 4
SHARD_IN = D_IN // N_CHIPS
B_CONFIGS = ((128, 1), (512, 4), (2048, 16))
B_DILS = tuple(d for _, d in B_CONFIGS)
A_WINDOW = 128
RMS_EPS = 1e-6
ROPE_THETA = 500000.0
SCALE = HEAD_DIM ** -0.5
NEG = -1e30
ADAM_LR, ADAM_B1, ADAM_B2, ADAM_EPS, ADAM_WD, ADAM_STEP = 0.001, 0.9, 0.999, 1e-08, 0.01, 10

NT = (((1,), (1,)), ((), ()))
TN = (((0,), (0,)), ((), ()))
MESH = pl.DeviceIdType.MESH

_PROJ_LAYOUT = (
    [("qa", 128 * i, True, True) for i in range(3)] + [("ka", 0, True, False), ("va", 0, False, False)]
    + [("ga", 128 * i, False, False) for i in range(3)]
    + [("qb", 128 * i, True, True) for i in range(3)] + [("kb", 128 * i, True, False) for i in range(3)]
    + [("vb", 128 * i, False, False) for i in range(3)] + [("gb", 128 * i, False, False) for i in range(3)]
    + [("qc", 128 * i, False, True) for i in range(2)] + [("gc", 128 * i, False, False) for i in range(2)]
)
_PROJ_WIDTH = dict(qa=A_W, ka=A_KV_W, va=A_KV_W, ga=A_W, qb=B_W, kb=B_W, vb=B_W, gb=B_W, qc=C_W, gc=C_W)
_NATURAL = ("qa", "ka", "va", "ga", "gb", "qc", "gc")
_DILATED = ("qb", "kb", "vb")


def _dot(a, b):
    return jnp.dot(a, b, preferred_element_type=F32)


def _dot_nt(a, b):
    return lax.dot_general(a, b, NT, preferred_element_type=F32)


def _dot_tn(a, b):
    return lax.dot_general(a, b, TN, preferred_element_type=F32)


def _half_masks(rows):
    lane = lax.broadcasted_iota(jnp.int32, (rows, LANES), 1)
    return lane < HEAD_DIM, lane >= HEAD_DIM


def _rope(t, c, sm, sp):
    return t * c + pltpu.roll(t, LANES - 8, 1) * sm + pltpu.roll(t, 8, 1) * sp


def _rope_tables(seq):
    dim = jnp.arange(LANES) % HEAD_DIM
    inv_freq = ROPE_THETA ** (-jnp.arange(0, 16, 2, dtype=F32) / 16)
    ang = jnp.arange(seq, dtype=F32)[:, None] * inv_freq[dim % 8][None, :]
    cos, sin = jnp.cos(ang), jnp.sin(ang)
    c = jnp.where(dim < 16, cos, 1.0)
    sm = jnp.where(dim < 8, -sin, 0.0)
    sp = jnp.where((dim >= 8) & (dim < 16), sin, 0.0)
    return c, sm, sp


def _split3(x):
    a = x.astype(BF16)
    r = x - a.astype(F32)
    b = r.astype(BF16)
    c = (r - b.astype(F32)).astype(BF16)
    return a, b, c


def _rows_to_lanes(x):
    row = lax.broadcasted_iota(jnp.int32, (8, LANES), 0)
    lane = lax.broadcasted_iota(jnp.int32, (8, LANES), 1)
    eye = (row == lane).astype(BF16)
    a, b, c = _split3(x)
    return _dot_nt(eye, a) + _dot_nt(eye, b) + _dot_nt(eye, c)


def _head_sum_matrix(width):
    k = lax.broadcasted_iota(jnp.int32, (width, LANES), 0)
    h = lax.broadcasted_iota(jnp.int32, (width, LANES), 1)
    return (k // HEAD_DIM == h).astype(BF16)


def _head_expand_matrix(width):
    h = lax.broadcasted_iota(jnp.int32, (LANES, width), 0)
    k = lax.broadcasted_iota(jnp.int32, (LANES, width), 1)
    return (k // HEAD_DIM == h).astype(BF16)


def _dot_split(x, mat, terms):
    parts = _split3(x)[:terms]
    out = _dot(parts[0], mat)
    for p in parts[1:]:
        out = out + _dot(p, mat)
    return out


def _per_head(cols, fill=0.0):
    rows = cols[0].shape[0]
    lane = lax.broadcasted_iota(jnp.int32, (rows, LANES), 1)
    out = jnp.full((rows, LANES), fill, F32)
    for h, col in enumerate(cols):
        out = jnp.where(lane == h, col, out)
    return out


def _lane_blocks(width):
    return [slice(p * LANES, (p + 1) * LANES) for p in range(width // LANES)]


def _stage(rows, width):
    return pltpu.VMEM((width // LANES, rows, LANES), F32)


def _stage_write(buf, value):
    for p, lanes in enumerate(_lane_blocks(value.shape[1])):
        buf[p] = value[:, lanes]


def _stage_read(buf):
    return jnp.concatenate([buf[p] for p in range(buf.shape[0])], axis=1) if buf.shape[0] > 1 else buf[0]


def _to_residues(buf, out_ref, dil):
    rows = buf.shape[1] // dil
    for r in range(dil):
        for p in range(buf.shape[0]):
            plane = buf.at[p]
            out_ref[r, :, p * LANES:(p + 1) * LANES] = plane[pl.ds(r, rows, stride=dil), :].astype(out_ref.dtype)


def _from_residues(in_ref, buf, dil):
    rows = buf.shape[1] // dil
    for r in range(dil):
        for p in range(buf.shape[0]):
            plane = buf.at[p]
            plane[pl.ds(r, rows, stride=dil), :] = in_ref[r, :, p * LANES:(p + 1) * LANES].astype(F32)


def _residue_spec(dil, tm, width):
    return pl.BlockSpec((dil, tm // dil, width), lambda i: (0, i, 0))


def _gather_weights(w_in_s, w_out_s, w_mkv_s):
    shards = tuple(s.reshape(2, s.shape[0] // 2, s.shape[1]) for s in (w_in_s, w_out_s, w_mkv_s))
    n = len(shards)

    def body(*refs):
        srcs, outs = refs[:n], refs[2 * n:3 * n]
        send_sems, recv_sems = refs[3 * n:]
        x, y, c = lax.axis_index("x"), lax.axis_index("y"), lax.axis_index("c")
        my_chip = 2 * x + y
        sibling = (x, y, 1 - c)
        chips = [(1 - x, y), (x, 1 - y), (1 - x, 1 - y)]

        def half(t, chip, which):
            return outs[t].at[chip, which]

        def src_half(t, which):
            return srcs[t].at[which]

        def copy(k, src, dst, to):
            return pltpu.make_async_remote_copy(src_ref=src, dst_ref=dst, send_sem=send_sems.at[k],
                                                recv_sem=recv_sems.at[k], device_id=to, device_id_type=MESH)

        first = []
        for j, (cx, cy) in enumerate(chips):
            for t in range(n):
                first.append(copy(n * j + t, src_half(t, c), half(t, my_chip, c), (cx, cy, c)))
        for cp in first:
            cp.start()
        passed = []
        for j, (cx, cy) in enumerate(chips):
            chip = 2 * cx + cy
            for t in range(n):
                k = n * j + t
                copy(k, src_half(t, c), half(t, chip, c), (cx, cy, c)).wait_recv()
                fwd = copy(n * 3 + k, half(t, chip, c), half(t, chip, c), sibling)
                fwd.start()
                passed.append(fwd)
        for j, (cx, cy) in enumerate(chips):
            chip = 2 * cx + cy
            for t in range(n):
                k = n * 3 + n * j + t
                copy(k, half(t, chip, 1 - c), half(t, chip, 1 - c), sibling).wait_recv()
        for cp in first + passed:
            cp.wait_send()

    my_chip = 2 * lax.axis_index("x") + lax.axis_index("y")
    landing = [lax.dynamic_update_slice(jnp.zeros((N_CHIPS,) + s.shape, s.dtype), s[None], (my_chip, 0, 0, 0))
               for s in shards]
    any_spec = pl.BlockSpec(memory_space=pl.ANY)
    return pl.pallas_call(
        body, name="gather_weights",
        out_shape=[jax.ShapeDtypeStruct((N_CHIPS,) + s.shape, s.dtype) for s in shards],
        in_specs=[any_spec] * (2 * n), out_specs=[any_spec] * n,
        input_output_aliases={n + t: t for t in range(n)},
        scratch_shapes=[pltpu.SemaphoreType.DMA((6 * n,)), pltpu.SemaphoreType.DMA((6 * n,))],
    )(*shards, *landing)


def _mem_kv(mem, mem_norm, w_mkv):
    def body(mem_ref, g_ref, w_ref, mk_ref, mv_ref):
        m = mem_ref[...]
        r = lax.rsqrt(jnp.mean(m * m, axis=-1, keepdims=True) + RMS_EPS)
        mn = (m * r * g_ref[...]).astype(BF16)
        kv = _dot(mn, w_ref[...])
        mk_ref[...] = kv[:, :C_W].astype(BF16)
        mv_ref[...] = kv[:, C_W:].astype(BF16)

    return pl.pallas_call(
        body, name="mem_kv",
        out_shape=[jax.ShapeDtypeStruct((N_MEM, C_W), BF16)] * 2,
    )(mem, mem_norm, w_mkv)


def _mem_kv_bwd(mem, mem_norm, w_mkv, dmk, dmv):
    def body(mem_ref, g_ref, w_ref, dmk_ref, dmv_ref, gw_ref, gn_ref):
        m = mem_ref[...]
        r = lax.rsqrt(jnp.mean(m * m, axis=-1, keepdims=True) + RMS_EPS)
        mhat = m * r
        mn = (mhat * g_ref[...]).astype(BF16)
        dkv = jnp.concatenate([dmk_ref[...], dmv_ref[...]], axis=1).astype(BF16)
        gw_ref[...] = _dot_tn(mn, dkv)
        dmn = _dot_nt(dkv, w_ref[...])
        gn_ref[...] = jnp.sum(dmn * mhat, axis=0, keepdims=True)

    return pl.pallas_call(
        body, name="mem_kv_bwd",
        out_shape=[jax.ShapeDtypeStruct((D_MODEL, 2 * C_W), F32), jax.ShapeDtypeStruct((1, D_MODEL), F32)],
    )(mem, mem_norm, w_mkv, dmk, dmv)


def _pre_proj(x, pre_norm, w_in_g, rope):
    seq = x.shape[0]
    tm = min(512, seq)
    n_nat, n_dil = len(_NATURAL), len(_DILATED) * len(B_DILS)

    def body(x_ref, g_ref, w_ref, c_ref, sm_ref, sp_ref, *refs):
        nat = dict(zip(_NATURAL, refs[:n_nat]))
        res = {n: refs[n_nat + len(B_DILS) * k:n_nat + len(B_DILS) * (k + 1)] for k, n in enumerate(_DILATED)}
        ut = refs[n_nat + n_dil]
        bufs = dict(zip(_DILATED, refs[n_nat + n_dil + 1:]))
        xv = x_ref[...]
        r = lax.rsqrt(jnp.mean(xv * xv, axis=-1, keepdims=True) + RMS_EPS)
        u = xv * r * g_ref[...]
        ub = u.astype(BF16)
        ut[...] = u.T.astype(BF16)
        c, sm, sp = c_ref[...], sm_ref[...], sp_ref[...]
        for j in range(N_CHIPS):
            pj = _dot(ub, w_ref[j])
            for b in range(SHARD_IN // LANES):
                name, off, roped, scaled = _PROJ_LAYOUT[(SHARD_IN // LANES) * j + b]
                piece = pj[:, LANES * b:LANES * (b + 1)]
                if roped:
                    piece = _rope(piece, c, sm, sp)
                if scaled:
                    piece = piece * SCALE
                if name in bufs:
                    bufs[name][off // LANES] = piece
                else:
                    nat[name][:, off:off + LANES] = piece.astype(BF16)
        for name in _DILATED:
            for ref, dil in zip(res[name], B_DILS):
                _to_residues(bufs[name], ref, dil)

    row = lambda w: pl.BlockSpec((tm, w), lambda i: (i, 0))
    full = lambda a: pl.BlockSpec(a.shape, lambda i: (0,) * a.ndim)
    out_shape = [jax.ShapeDtypeStruct((seq, _PROJ_WIDTH[n]), BF16) for n in _NATURAL]
    out_specs = [row(_PROJ_WIDTH[n]) for n in _NATURAL]
    for n in _DILATED:
        for dil in B_DILS:
            out_shape.append(jax.ShapeDtypeStruct((dil, seq // dil, B_W), BF16))
            out_specs.append(_residue_spec(dil, tm, B_W))
    out_shape.append(jax.ShapeDtypeStruct((D_MODEL, seq), BF16))
    out_specs.append(pl.BlockSpec((D_MODEL, tm), lambda i: (0, i)))
    res = pl.pallas_call(
        body, name="pre_proj", grid=(seq // tm,),
        in_specs=[row(D_MODEL), full(pre_norm), full(w_in_g), row(LANES), row(LANES), row(LANES)],
        out_specs=out_specs, out_shape=out_shape,
        scratch_shapes=[_stage(tm, B_W)] * len(_DILATED),
    )(x, pre_norm, w_in_g, *rope)
    out = dict(zip(_NATURAL, res[:n_nat]))
    for k, n in enumerate(_DILATED):
        out[n] = res[n_nat + len(B_DILS) * k:n_nat + len(B_DILS) * (k + 1)]
    out["ut"] = res[n_nat + n_dil]
    return out


def _band_bias(max_dist, transposed):
    i = np.arange(BLOCK)[:, None]
    j = np.arange(BLOCK)[None, :]
    if transposed:
        same = i <= j
        other = (j + BLOCK - i) <= max_dist
        vis = np.concatenate([same, other], axis=1)
    else:
        prev = (i + BLOCK - j) <= max_dist
        same = j <= i
        vis = np.concatenate([prev, same], axis=1)
    return jnp.asarray(np.where(vis, 0.0, NEG).astype(np.float32))


def _kv_place(h, gqa):
    return (0, h // 3) if gqa else (h // 2, h % 2)


def _band_fwd(q, k, v, sink, *, max_dist, name):
    dil, length, wq = q.shape
    wk = k.shape[2]
    gqa = wk != wq
    tq = min(512, length)
    ns, nt = tq // BLOCK, length // tq
    npair = wq // LANES
    bias = _band_bias(max_dist, transposed=False)
    has_sink = sink is not None

    def body(*refs):
        if has_sink:
            sink_ref, refs = refs[0], refs[1:]
        q_ref, k_ref, kp_ref, v_ref, vp_ref, bias_ref, o_ref, lse_ref, kbuf, vbuf = refs[:10]
        i = pl.program_id(1)
        kbuf[0:BLOCK] = kp_ref[...]
        kbuf[BLOCK:] = k_ref[...]
        vbuf[0:BLOCK] = vp_ref[...]
        vbuf[BLOCK:] = v_ref[...]
        if gqa:
            kroll, vroll = refs[10:12]
            kroll[...] = pltpu.roll(kbuf[...], HEAD_DIM, 1)
            vroll[...] = pltpu.roll(vbuf[...], HEAD_DIM, 1)
        half = _half_masks(BLOCK)
        col_prev = (lax.broadcasted_iota(jnp.int32, (1, 2 * BLOCK), 1) < BLOCK).astype(F32)

        def sub(a, carry):
            r0 = pl.multiple_of(a * BLOCK, BLOCK)
            pen = jnp.where((i == 0) & (a == 0), NEG, 0.0)
            b = bias_ref[...] + pen * col_prev
            m_cols, l_cols = [], []
            for p in range(npair):
                lanes = slice(p * LANES, (p + 1) * LANES)
                qp = q_ref[pl.ds(r0, BLOCK), lanes]
                o_h = []
                for e in range(2):
                    h = 2 * p + e
                    pk, ek = _kv_place(h, gqa)
                    klanes = slice(pk * LANES, (pk + 1) * LANES)
                    kw = (kbuf if ek == e else kroll)[pl.ds(r0, 2 * BLOCK), klanes]
                    vw = (vbuf if ek == e else vroll)[pl.ds(r0, 2 * BLOCK), klanes]
                    qm = jnp.where(half[e], qp, jnp.zeros_like(qp))
                    s = _dot_nt(qm, kw) + b
                    m = jnp.max(s, axis=1, keepdims=True)
                    if has_sink:
                        m = jnp.maximum(m, sink_ref[h])
                    pe = jnp.exp(s - m)
                    l = jnp.sum(pe, axis=1, keepdims=True)
                    if has_sink:
                        l = l + jnp.exp(sink_ref[h] - m)
                    pv = _dot(pe.astype(BF16), vw)
                    o_h.append(pv * (1.0 / l))
                    m_cols.append(m)
                    l_cols.append(l)
                o_ref[pl.ds(r0, BLOCK), lanes] = jnp.where(half[0], o_h[0], o_h[1]).astype(BF16)
            lse_ref[pl.ds(r0, BLOCK), :] = _per_head(m_cols) + jnp.log(_per_head(l_cols, 1.0))
            return carry

        lax.fori_loop(0, ns, sub, 0, unroll=True)

    main = lambda w: pl.BlockSpec((None, tq, w), lambda r, i: (r, i, 0))
    prev = lambda w: pl.BlockSpec((None, BLOCK, w), lambda r, i: (r, jnp.maximum(i * ns - 1, 0), 0))
    in_specs = [main(wq), main(wk), prev(wk), main(wk), prev(wk), pl.BlockSpec(bias.shape, lambda r, i: (0, 0))]
    args = [q, k, k, v, v, bias]
    if has_sink:
        in_specs = [pl.BlockSpec(memory_space=pltpu.SMEM)] + in_specs
        args = [sink] + args
    scratch = [pltpu.VMEM((tq + BLOCK, wk), BF16)] * (4 if gqa else 2)
    return pl.pallas_call(
        body, name=name, grid=(dil, nt), in_specs=in_specs,
        out_specs=[main(wq), main(LANES)],
        out_shape=[jax.ShapeDtypeStruct((dil, length, wq), BF16), jax.ShapeDtypeStruct((dil, length, LANES), F32)],
        scratch_shapes=scratch,
    )(*args)


def _band_bwd(q, k, v, do, lse, delta, *, max_dist, name):
    dil, length, wq = q.shape
    wk = k.shape[2]
    gqa = wk != wq
    tq = min(512, length)
    ns, nt = tq // BLOCK, length // tq
    npair = wq // LANES
    nblocks = length // BLOCK
    bias = _band_bias(max_dist, transposed=True)

    def body(q_ref, qn_ref, do_ref, don_ref, lse_ref, lsen_ref, dl_ref, dln_ref, k_ref, v_ref, bias_ref,
             dq_ref, dk_ref, dv_ref, qbuf, dobuf, stat_l, stat_d, dqacc, *rolled):
        i = pl.program_id(1)
        qbuf[0:tq] = q_ref[...]
        qbuf[tq:] = qn_ref[...]
        dobuf[0:tq] = do_ref[...]
        dobuf[tq:] = don_ref[...]
        if gqa:
            kroll, vroll = rolled
            kroll[...] = pltpu.roll(k_ref[...], HEAD_DIM, 1)
            vroll[...] = pltpu.roll(v_ref[...], HEAD_DIM, 1)
        for a in range(ns):
            rows = slice(a * BLOCK, (a + 1) * BLOCK)
            stat_l[a] = _rows_to_lanes(lse_ref[rows, :])
            stat_d[a] = _rows_to_lanes(dl_ref[rows, :])
        stat_l[ns] = _rows_to_lanes(lsen_ref[...])
        stat_d[ns] = _rows_to_lanes(dln_ref[...])

        @pl.when(i == 0)
        def _():
            dqacc[0:BLOCK] = jnp.zeros((BLOCK, wq), F32)

        @pl.when(i > 0)
        def _():
            dqacc[0:BLOCK] = dqacc[tq:tq + BLOCK]

        dqacc[BLOCK:] = jnp.zeros((tq, wq), F32)
        half = _half_masks(BLOCK)
        half2 = _half_masks(2 * BLOCK)
        col_next = (lax.broadcasted_iota(jnp.int32, (1, 2 * BLOCK), 1) >= BLOCK).astype(F32)

        def sub(b, carry):
            r0 = pl.multiple_of(b * BLOCK, BLOCK)
            pen = jnp.where((i == nt - 1) & (b == ns - 1), NEG, 0.0)
            bt = bias_ref[...] + pen * col_next
            acc = {}
            for p in range(npair):
                lanes = slice(p * LANES, (p + 1) * LANES)
                qw = qbuf[pl.ds(r0, 2 * BLOCK), lanes]
                dow = dobuf[pl.ds(r0, 2 * BLOCK), lanes]
                for e in range(2):
                    h = 2 * p + e
                    pk, ek = _kv_place(h, gqa)
                    klanes = slice(pk * LANES, (pk + 1) * LANES)
                    kb = (k_ref if ek == e else kroll)[pl.ds(r0, BLOCK), klanes]
                    vb = (v_ref if ek == e else vroll)[pl.ds(r0, BLOCK), klanes]
                    qm = jnp.where(half2[e], qw, jnp.zeros_like(qw))
                    dom = jnp.where(half2[e], dow, jnp.zeros_like(dow))
                    lrow = jnp.concatenate([stat_l[b, h:h + 1, :], stat_l[b + 1, h:h + 1, :]], axis=1)
                    drow = jnp.concatenate([stat_d[b, h:h + 1, :], stat_d[b + 1, h:h + 1, :]], axis=1)
                    st = _dot_nt(kb, qm) + bt
                    pt = jnp.exp(st - lrow)
                    dv_c = _dot(pt.astype(BF16), dom)
                    dpt = _dot_nt(vb, dom)
                    dsb = (pt * (dpt - drow)).astype(BF16)
                    dk_c = _dot(dsb, qm)
                    kbm = jnp.where(half[e], kb, jnp.zeros_like(kb))
                    dqacc[pl.ds(r0, 2 * BLOCK), lanes] += _dot_tn(dsb, kbm)
                    key = (pk, ek == e)
                    if key in acc:
                        acc[key] = (acc[key][0] + dk_c, acc[key][1] + dv_c)
                    else:
                        acc[key] = (dk_c, dv_c)
                if not gqa:
                    dk_ref[pl.ds(r0, BLOCK), lanes] = acc[(p, True)][0].astype(BF16)
                    dv_ref[pl.ds(r0, BLOCK), lanes] = acc[(p, True)][1].astype(BF16)
            if gqa:
                dk_al, dv_al = acc[(0, True)]
                dk_mis, dv_mis = acc[(0, False)]
                dk_ref[pl.ds(r0, BLOCK), :] = (dk_al + pltpu.roll(dk_mis, HEAD_DIM, 1)).astype(BF16)
                dv_ref[pl.ds(r0, BLOCK), :] = (dv_al + pltpu.roll(dv_mis, HEAD_DIM, 1)).astype(BF16)
            return carry

        lax.fori_loop(0, ns, sub, 0, unroll=True)
        dq_ref[...] = dqacc[0:tq].astype(BF16)

    main = lambda w: pl.BlockSpec((None, tq, w), lambda r, i: (r, i, 0))
    nxt = lambda w: pl.BlockSpec((None, BLOCK, w), lambda r, i: (r, jnp.minimum((i + 1) * ns, nblocks - 1), 0))
    scratch = [pltpu.VMEM((tq + BLOCK, wq), BF16), pltpu.VMEM((tq + BLOCK, wq), BF16),
               pltpu.VMEM((ns + 1, 8, LANES), F32), pltpu.VMEM((ns + 1, 8, LANES), F32),
               pltpu.VMEM((tq + BLOCK, wq), F32)]
    if gqa:
        scratch = scratch + [pltpu.VMEM((tq, wk), BF16)] * 2
    return pl.pallas_call(
        body, name=name, grid=(dil, nt),
        in_specs=[main(wq), nxt(wq), main(wq), nxt(wq), main(LANES), nxt(LANES), main(LANES), nxt(LANES),
                  main(wk), main(wk), pl.BlockSpec(bias.shape, lambda r, i: (0, 0))],
        out_specs=[main(wq), main(wk), main(wk)],
        out_shape=[jax.ShapeDtypeStruct((dil, length, wq), BF16), jax.ShapeDtypeStruct((dil, length, wk), BF16),
                   jax.ShapeDtypeStruct((dil, length, wk), BF16)],
        scratch_shapes=scratch,
        compiler_params=pltpu.CompilerParams(dimension_semantics=("arbitrary", "arbitrary")),
    )(q, q, do, do, lse, lse, delta, delta, k, v, bias)


def _mem_attn_fwd(q, mk, mv):
    seq = q.shape[0]
    tq = min(512, seq)
    ns = tq // BLOCK

    def body(q_ref, mk_ref, mv_ref, o_ref, lse_ref):
        half = _half_masks(BLOCK)

        def sub(a, carry):
            r0 = pl.multiple_of(a * BLOCK, BLOCK)
            m_cols, l_cols = [], []
            for p in range(C_W // LANES):
                lanes = slice(p * LANES, (p + 1) * LANES)
                qp = q_ref[pl.ds(r0, BLOCK), lanes]
                o_h = []
                for e in range(2):
                    qm = jnp.where(half[e], qp, jnp.zeros_like(qp))
                    s = _dot_nt(qm, mk_ref[:, lanes])
                    m = jnp.max(s, axis=1, keepdims=True)
                    pe = jnp.exp(s - m)
                    l = jnp.sum(pe, axis=1, keepdims=True)
                    o_h.append(_dot(pe.astype(BF16), mv_ref[:, lanes]) * (1.0 / l))
                    m_cols.append(m)
                    l_cols.append(l)
                o_ref[pl.ds(r0, BLOCK), lanes] = jnp.where(half[0], o_h[0], o_h[1]).astype(BF16)
            lse_ref[pl.ds(r0, BLOCK), :] = _per_head(m_cols) + jnp.log(_per_head(l_cols, 1.0))
            return carry

        lax.fori_loop(0, ns, sub, 0, unroll=True)

    row = lambda w: pl.BlockSpec((tq, w), lambda i: (i, 0))
    full = pl.BlockSpec((N_MEM, C_W), lambda i: (0, 0))
    return pl.pallas_call(
        body, name="mem_attn_fwd", grid=(seq // tq,), in_specs=[row(C_W), full, full],
        out_specs=[row(C_W), row(LANES)],
        out_shape=[jax.ShapeDtypeStruct((seq, C_W), BF16), jax.ShapeDtypeStruct((seq, LANES), F32)],
    )(q, mk, mv)


def _mem_attn_bwd(q, mk, mv, do, lse, delta):
    seq = q.shape[0]
    tq = min(512, seq)
    ns = tq // BLOCK
    npair = C_W // LANES

    def body(q_ref, mk_ref, mv_ref, do_ref, lse_ref, dl_ref, dq_ref, dmk_ref, dmv_ref, stat_l, stat_d):
        @pl.when(pl.program_id(0) == 0)
        def _():
            dmk_ref[...] = jnp.zeros_like(dmk_ref)
            dmv_ref[...] = jnp.zeros_like(dmv_ref)

        for a in range(ns):
            rows = slice(a * BLOCK, (a + 1) * BLOCK)
            stat_l[a] = _rows_to_lanes(lse_ref[rows, :])
            stat_d[a] = _rows_to_lanes(dl_ref[rows, :])
        half = _half_masks(BLOCK)
        halfk = _half_masks(N_MEM)

        def sub(a, carry):
            r0 = pl.multiple_of(a * BLOCK, BLOCK)
            for p in range(npair):
                lanes = slice(p * LANES, (p + 1) * LANES)
                qp = q_ref[pl.ds(r0, BLOCK), lanes]
                dop = do_ref[pl.ds(r0, BLOCK), lanes]
                kb, vb = mk_ref[:, lanes], mv_ref[:, lanes]
                dq_pair = None
                for e in range(2):
                    h = 2 * p + e
                    qm = jnp.where(half[e], qp, jnp.zeros_like(qp))
                    dom = jnp.where(half[e], dop, jnp.zeros_like(dop))
                    st = _dot_nt(kb, qm)
                    pt = jnp.exp(st - stat_l[a, h:h + 1, :])
                    dmv_ref[:, lanes] += _dot(pt.astype(BF16), dom)
                    dpt = _dot_nt(vb, dom)
                    dsb = (pt * (dpt - stat_d[a, h:h + 1, :])).astype(BF16)
                    dmk_ref[:, lanes] += _dot(dsb, qm)
                    kbm = jnp.where(halfk[e], kb, jnp.zeros_like(kb))
                    dq_c = _dot_tn(dsb, kbm)
                    dq_pair = dq_c if dq_pair is None else dq_pair + dq_c
                dq_ref[pl.ds(r0, BLOCK), lanes] = dq_pair.astype(BF16)
            return carry

        lax.fori_loop(0, ns, sub, 0, unroll=True)

    row = lambda w: pl.BlockSpec((tq, w), lambda i: (i, 0))
    full = pl.BlockSpec((N_MEM, C_W), lambda i: (0, 0))
    return pl.pallas_call(
        body, name="mem_attn_bwd", grid=(seq // tq,),
        in_specs=[row(C_W), full, full, row(C_W), row(LANES), row(LANES)], out_specs=[row(C_W), full, full],
        out_shape=[jax.ShapeDtypeStruct((seq, C_W), BF16), jax.ShapeDtypeStruct((N_MEM, C_W), F32),
                   jax.ShapeDtypeStruct((N_MEM, C_W), F32)],
        scratch_shapes=[pltpu.VMEM((ns, 8, LANES), F32)] * 2,
        compiler_params=pltpu.CompilerParams(dimension_semantics=("arbitrary",)),
    )(q, mk, mv, do, lse, delta)


def _silu_and_grad(g):
    s = 1.0 / (1.0 + jnp.exp(-g))
    return g * s, s * (1.0 + g * (1.0 - s))


def _post(x, target, post_norm, w_out, sink_row, oa, lse_a, ga, ob_list, lseb_list, gb, oc, gc):
    seq = x.shape[0]
    tm = min(256, seq)
    inv_d = 1.0 / D_MODEL
    nd = len(B_DILS)

    def body(*refs):
        (x_ref, t_ref, gp_ref, w_ref, sink_ref, oa_ref, lsea_ref, ga_ref), refs = refs[:8], refs[8:]
        ob_refs, lb_refs, (gb_ref, oc_ref, gc_ref), refs = refs[:nd], refs[nd:2 * nd], refs[2 * nd:2 * nd + 3], refs[2 * nd + 3:]
        (g_ref, doa_ref, dla_ref, dga_ref), refs = refs[:4], refs[4:]
        dob_refs, lsec_refs, dlb_refs, refs = refs[:nd], refs[nd:2 * nd], refs[2 * nd:3 * nd], refs[3 * nd:]
        (dgb_ref, doc_ref, dlc_ref, dgc_ref, gw_ref, gpost_ref, gsink_ref, loss_ref), refs = refs[:8], refs[8:]
        ycat, obufs, lbufs, st_do, st_l, st_d = refs[0], refs[1:nd], refs[nd:2 * nd - 1], refs[2 * nd - 1], refs[2 * nd], refs[2 * nd + 1]

        @pl.when(pl.program_id(0) == 0)
        def _():
            gw_ref[...] = jnp.zeros_like(gw_ref)
            gpost_ref[...] = jnp.zeros_like(gpost_ref)
            gsink_ref[...] = jnp.zeros_like(gsink_ref)
            loss_ref[...] = jnp.zeros_like(loss_ref)

        o_i, l_i = [ob_refs[0][0].astype(F32)], [lb_refs[0][0]]
        for k in range(1, nd):
            _from_residues(ob_refs[k], obufs[k - 1], B_DILS[k])
            _from_residues(lb_refs[k], lbufs[k - 1], B_DILS[k])
            o_i.append(_stage_read(obufs[k - 1]))
            l_i.append(_stage_read(lbufs[k - 1]))
        mx = l_i[0]
        for l in l_i[1:]:
            mx = jnp.maximum(mx, l)
        w_i = [jnp.exp(l - mx) for l in l_i]
        z = w_i[0]
        for w in w_i[1:]:
            z = z + w
        lse_b = mx + jnp.log(z)
        expand = _head_expand_matrix(B_W)
        inv_z = 1.0 / z
        ob = None
        for w, o in zip(w_i, o_i):
            term = _dot_split(w * inv_z, expand, 2) * o
            ob = term if ob is None else ob + term
        oa, oc = oa_ref[...].astype(F32), oc_ref[...].astype(F32)
        sa, dsa = _silu_and_grad(ga_ref[...].astype(F32))
        sb, dsb = _silu_and_grad(gb_ref[...].astype(F32))
        sc, dsc = _silu_and_grad(gc_ref[...].astype(F32))
        ycat[:, 0:A_W] = (oa * sa).astype(BF16)
        ycat[:, A_W:A_W + B_W] = (ob * sb).astype(BF16)
        ycat[:, A_W + B_W:] = (oc * sc).astype(BF16)
        yc = ycat[...]
        y2 = _dot(yc, w_ref[...])
        r = lax.rsqrt(jnp.mean(y2 * y2, axis=-1, keepdims=True) + RMS_EPS)
        zhat = y2 * r
        gp = gp_ref[...]
        err = x_ref[...] + zhat * gp - t_ref[...]
        loss_ref[...] += jnp.sum(err * err) * (0.5 * inv_d)
        g = err * inv_d
        g_ref[...] = g
        gpost_ref[...] += jnp.sum(g * zhat, axis=0, keepdims=True)
        a = g * gp
        dy2 = (r * (a - zhat * jnp.mean(a * zhat, axis=-1, keepdims=True))).astype(BF16)
        gw_ref[...] += _dot_tn(yc, dy2)
        dycat = _dot_nt(dy2, w_ref[...])
        dya, dyb, dyc = dycat[:, 0:A_W], dycat[:, A_W:A_W + B_W], dycat[:, A_W + B_W:]
        doa, dob, doc = dya * sa, dyb * sb, dyc * sc
        doa_ref[...] = doa.astype(BF16)
        doc_ref[...] = doc.astype(BF16)
        dga_ref[...] = (dya * oa * dsa).astype(BF16)
        dgb_ref[...] = (dyb * ob * dsb).astype(BF16)
        dgc_ref[...] = (dyc * oc * dsc).astype(BF16)
        dl_a = _dot_split(doa * oa, _head_sum_matrix(A_W), 3)
        dla_ref[...] = dl_a
        dlc_ref[...] = _dot_split(doc * oc, _head_sum_matrix(C_W), 3)
        gsink_ref[...] += jnp.sum(jnp.exp(sink_ref[...] - lsea_ref[...]) * dl_a, axis=0, keepdims=True)
        _stage_write(st_do, dob)
        _stage_write(st_l, lse_b)
        _stage_write(st_d, _dot_split(dob * ob, _head_sum_matrix(B_W), 3))
        for k, dil in enumerate(B_DILS):
            _to_residues(st_do, dob_refs[k], dil)
            _to_residues(st_l, lsec_refs[k], dil)
            _to_residues(st_d, dlb_refs[k], dil)

    row = lambda w: pl.BlockSpec((tm, w), lambda i: (i, 0))
    full = lambda shape: pl.BlockSpec(shape, lambda i: (0,) * len(shape))
    res_specs = lambda w: [_residue_spec(d, tm, w) for d in B_DILS]
    res_shapes = lambda w, dt: [jax.ShapeDtypeStruct((d, seq // d, w), dt) for d in B_DILS]
    ins = [x, target, post_norm, w_out, sink_row, oa, lse_a, ga, *ob_list, *lseb_list, gb, oc, gc]
    in_specs = ([row(D_MODEL), row(D_MODEL), full((1, D_MODEL)), full((D_MODEL, D_MODEL)), full((1, LANES)),
                 row(A_W), row(LANES), row(A_W)] + res_specs(B_W) + res_specs(LANES) + [row(B_W), row(C_W), row(C_W)])
    out_shape = ([jax.ShapeDtypeStruct((seq, D_MODEL), F32), jax.ShapeDtypeStruct((seq, A_W), BF16),
                  jax.ShapeDtypeStruct((seq, LANES), F32), jax.ShapeDtypeStruct((seq, A_W), BF16)]
                 + res_shapes(B_W, BF16) + res_shapes(LANES, F32) + res_shapes(LANES, F32)
                 + [jax.ShapeDtypeStruct((seq, B_W), BF16), jax.ShapeDtypeStruct((seq, C_W), BF16),
                    jax.ShapeDtypeStruct((seq, LANES), F32), jax.ShapeDtypeStruct((seq, C_W), BF16),
                    jax.ShapeDtypeStruct((D_MODEL, D_MODEL), F32), jax.ShapeDtypeStruct((1, D_MODEL), F32),
                    jax.ShapeDtypeStruct((1, LANES), F32), jax.ShapeDtypeStruct((1, LANES), F32)])
    out_specs = ([row(D_MODEL), row(A_W), row(LANES), row(A_W)] + res_specs(B_W) + res_specs(LANES) + res_specs(LANES)
                 + [row(B_W), row(C_W), row(LANES), row(C_W),
                    full((D_MODEL, D_MODEL)), full((1, D_MODEL)), full((1, LANES)), full((1, LANES))])
    scratch = ([pltpu.VMEM((tm, D_MODEL), BF16)] + [_stage(tm, B_W)] * (nd - 1) + [_stage(tm, LANES)] * (nd - 1)
               + [_stage(tm, B_W), _stage(tm, LANES), _stage(tm, LANES)])
    res = pl.pallas_call(
        body, name="post", grid=(seq // tm,), in_specs=in_specs, out_specs=out_specs, out_shape=out_shape,
        scratch_shapes=scratch,
        compiler_params=pltpu.CompilerParams(dimension_semantics=("arbitrary",)),
    )(*ins)
    out = dict(g=res[0], doa=res[1], dl_a=res[2], dga=res[3], dob=res[4:4 + nd], lse_b=res[4 + nd:4 + 2 * nd],
               dl_b=res[4 + 2 * nd:4 + 3 * nd])
    rest = res[4 + 3 * nd:]
    out.update(dgb=rest[0], doc=rest[1], dl_c=rest[2], dgc=rest[3], gw_out=rest[4], gpost=rest[5], gsink=rest[6],
               loss=rest[7])
    return out


def _dx(x, g, pre_norm, w_in_g, rope, nat, res):
    seq = x.shape[0]
    tm = min(256, seq)
    nd = len(B_DILS)
    nat_list = [nat[n] for n in _NATURAL]
    res_list = [a for n in _DILATED for a in res[n]]

    def body(x_ref, g_ref, gp_ref, w_ref, c_ref, sm_ref, sp_ref, *refs):
        nat_refs = dict(zip(_NATURAL, refs[:len(_NATURAL)]))
        refs = refs[len(_NATURAL):]
        res_refs = {n: refs[nd * k:nd * (k + 1)] for k, n in enumerate(_DILATED)}
        refs = refs[nd * len(_DILATED):]
        dproj_ref, gx_ref, gpre_ref = refs[:3]
        bufs = {n: refs[3 + (nd - 1) * k:3 + (nd - 1) * (k + 1)] for k, n in enumerate(_DILATED)}

        @pl.when(pl.program_id(0) == 0)
        def _():
            gpre_ref[...] = jnp.zeros_like(gpre_ref)

        for n in _DILATED:
            for k in range(1, nd):
                _from_residues(res_refs[n][k], bufs[n][k - 1], B_DILS[k])
        c, sm, sp = c_ref[...], -sm_ref[...], -sp_ref[...]
        for blk, (name, off, roped, scaled) in enumerate(_PROJ_LAYOUT):
            lanes = slice(off, off + LANES)
            if name in nat_refs:
                piece = nat_refs[name][:, lanes].astype(F32)
            else:
                piece = res_refs[name][0][0, :, lanes].astype(F32)
                for buf in bufs[name]:
                    piece = piece + buf[off // LANES]
            if roped:
                piece = _rope(piece, c, sm, sp)
            if scaled:
                piece = piece * SCALE
            dproj_ref[:, blk * LANES:(blk + 1) * LANES] = piece.astype(BF16)
        du = None
        for j in range(N_CHIPS):
            part = _dot_nt(dproj_ref[:, j * SHARD_IN:(j + 1) * SHARD_IN], w_ref[j])
            du = part if du is None else du + part
        xv = x_ref[...]
        r = lax.rsqrt(jnp.mean(xv * xv, axis=-1, keepdims=True) + RMS_EPS)
        xhat = xv * r
        gpre_ref[...] += jnp.sum(du * xhat, axis=0, keepdims=True)
        a = du * gp_ref[...]
        gx_ref[...] = g_ref[...] + r * (a - xhat * jnp.mean(a * xhat, axis=-1, keepdims=True))

    row = lambda w: pl.BlockSpec((tm, w), lambda i: (i, 0))
    full = lambda a: pl.BlockSpec(a.shape, lambda i: (0,) * a.ndim)
    in_specs = ([row(D_MODEL), row(D_MODEL), full(pre_norm), full(w_in_g), row(LANES), row(LANES), row(LANES)]
                + [row(a.shape[1]) for a in nat_list]
                + [_residue_spec(d, tm, B_W) for _ in _DILATED for d in B_DILS])
    return pl.pallas_call(
        body, name="dx", grid=(seq // tm,), in_specs=in_specs,
        out_specs=[row(D_IN), row(D_MODEL), pl.BlockSpec((1, D_MODEL), lambda i: (0, 0))],
        out_shape=[jax.ShapeDtypeStruct((seq, D_IN), BF16), jax.ShapeDtypeStruct((seq, D_MODEL), F32),
                   jax.ShapeDtypeStruct((1, D_MODEL), F32)],
        scratch_shapes=[_stage(tm, B_W)] * ((nd - 1) * len(_DILATED)),
        compiler_params=pltpu.CompilerParams(dimension_semantics=("arbitrary",)),
    )(x, g, pre_norm, w_in_g, *rope, *nat_list, *res_list)


def _grad_w_in(ut, dproj):
    seq = ut.shape[1]
    tk = min(1024, seq)

    def body(ut_ref, dp_ref, out_ref):
        @pl.when(pl.program_id(1) == 0)
        def _():
            out_ref[...] = jnp.zeros_like(out_ref)

        out_ref[...] += _dot(ut_ref[...], dp_ref[...])

    return pl.pallas_call(
        body, name="grad_w_in", grid=(N_CHIPS, seq // tk),
        in_specs=[pl.BlockSpec((D_MODEL, tk), lambda j, i: (0, i)), pl.BlockSpec((tk, SHARD_IN), lambda j, i: (i, j))],
        out_specs=pl.BlockSpec((None, D_MODEL, SHARD_IN), lambda j, i: (j, 0, 0)),
        out_shape=jax.ShapeDtypeStruct((N_CHIPS, D_MODEL, SHARD_IN), F32),
        compiler_params=pltpu.CompilerParams(dimension_semantics=("arbitrary", "arbitrary")),
    )(ut, dproj)


def _pair_exchange(grads):
    n = len(grads)

    def body(*refs):
        srcs, outs = refs[:n], refs[n:2 * n]
        send_sems, recv_sems = refs[2 * n:]
        x, y, c = lax.axis_index("x"), lax.axis_index("y"), lax.axis_index("c")
        copies = []
        for t in range(n):
            rows = grads[t].shape[1] // 2
            copies.append(pltpu.make_async_remote_copy(
                src_ref=srcs[t].at[:, pl.ds((1 - c) * rows, rows)], dst_ref=outs[t],
                send_sem=send_sems.at[t], recv_sem=recv_sems.at[t], device_id=(x, y, 1 - c), device_id_type=MESH))
        for cp in copies:
            cp.start()
        for cp in copies:
            cp.wait()

    any_spec = pl.BlockSpec(memory_space=pl.ANY)
    return pl.pallas_call(
        body, name="pair_exchange",
        out_shape=[jax.ShapeDtypeStruct((g.shape[0], g.shape[1] // 2, g.shape[2]), g.dtype) for g in grads],
        in_specs=[any_spec] * n, out_specs=[any_spec] * n,
        scratch_shapes=[pltpu.SemaphoreType.DMA((n,)), pltpu.SemaphoreType.DMA((n,))],
    )(*grads)


def _pair_add(core, own, got):
    nchip, rows2, width = own.shape
    rows = rows2 // 2
    tr = min(128, rows)
    nb = rows // tr

    def body(core_ref, own_ref, got_ref, out_ref):
        out_ref[...] = (own_ref[...] + got_ref[...]).astype(BF16)

    grid_spec = pltpu.PrefetchScalarGridSpec(
        num_scalar_prefetch=1, grid=(nchip, nb),
        in_specs=[pl.BlockSpec((None, tr, width), lambda k, i, core_ref: (k, core_ref[0] * nb + i, 0)),
                  pl.BlockSpec((None, tr, width), lambda k, i, core_ref: (k, i, 0))],
        out_specs=pl.BlockSpec((None, tr, width), lambda k, i, core_ref: (k, i, 0)))
    return pl.pallas_call(
        body, name=f"pair_add_{width}", grid_spec=grid_spec,
        out_shape=jax.ShapeDtypeStruct((nchip, rows, width), BF16),
    )(core, own, got)


def _chip_exchange(parts, small):
    n = len(parts)

    def body(*refs):
        srcs, small_ref = refs[:n], refs[n]
        outs, small_out = refs[n + 1:2 * n + 1], refs[2 * n + 1]
        send_sems, recv_sems, local_sems = refs[2 * n + 2:]
        x, y, c = lax.axis_index("x"), lax.axis_index("y"), lax.axis_index("c")
        my_chip = 2 * x + y
        me = 4 * x + 2 * y + c
        chips = [(1 - x, y), (x, 1 - y), (1 - x, 1 - y)]
        local = [pltpu.make_async_copy(srcs[t].at[my_chip], outs[t].at[my_chip], local_sems.at[t]) for t in range(n)]
        local.append(pltpu.make_async_copy(small_ref, small_out.at[me], local_sems.at[n]))
        for cp in local:
            cp.start()
        sent = []
        for j, (cx, cy) in enumerate(chips):
            for t in range(n):
                k = n * j + t
                sent.append(pltpu.make_async_remote_copy(
                    src_ref=srcs[t].at[2 * cx + cy], dst_ref=outs[t].at[my_chip], send_sem=send_sems.at[k],
                    recv_sem=recv_sems.at[k], device_id=(cx, cy, c), device_id_type=MESH))
        peers = [(x, y, 1 - c)] + [(cx, cy, cc) for (cx, cy) in chips for cc in (c, 1 - c)]
        for j, peer in enumerate(peers):
            k = 3 * n + j
            sent.append(pltpu.make_async_remote_copy(
                src_ref=small_ref, dst_ref=small_out.at[me], send_sem=send_sems.at[k], recv_sem=recv_sems.at[k],
                device_id=peer, device_id_type=MESH))
        for cp in sent:
            cp.start()
        for cp in sent:
            cp.wait()
        for cp in local:
            cp.wait()

    any_spec = pl.BlockSpec(memory_space=pl.ANY)
    nsem = 3 * n + 7
    return pl.pallas_call(
        body, name="chip_exchange",
        out_shape=[jax.ShapeDtypeStruct(p.shape, p.dtype) for p in parts]
        + [jax.ShapeDtypeStruct((8,) + small.shape, small.dtype)],
        in_specs=[any_spec] * (n + 1), out_specs=[any_spec] * (n + 1),
        scratch_shapes=[pltpu.SemaphoreType.DMA((nsem,)), pltpu.SemaphoreType.DMA((nsem,)),
                        pltpu.SemaphoreType.DMA((n + 1,))],
    )(*parts, small)


def _slot_sum(slots, name, core=None):
    ns, rows, width = slots.shape
    tr = min(128, rows)

    def body(*refs):
        in_ref, out_ref = refs[-2:]
        acc = in_ref[0].astype(F32)
        for s in range(1, ns):
            acc = acc + in_ref[s].astype(F32)
        out_ref[...] = acc

    if core is None:
        return pl.pallas_call(
            body, name=name, grid=(rows // tr,),
            in_specs=[pl.BlockSpec((ns, tr, width), lambda i: (0, i, 0))],
            out_specs=pl.BlockSpec((tr, width), lambda i: (i, 0)),
            out_shape=jax.ShapeDtypeStruct((rows, width), F32),
        )(slots)
    grid_spec = pltpu.PrefetchScalarGridSpec(
        num_scalar_prefetch=1, grid=(rows // tr,),
        in_specs=[pl.BlockSpec((ns, tr, width), lambda i, core_ref: (0, i, 0))],
        out_specs=pl.BlockSpec((None, tr, width), lambda i, core_ref: (core_ref[0], i, 0)))
    return pl.pallas_call(
        body, name=name, grid_spec=grid_spec, out_shape=jax.ShapeDtypeStruct((2, rows, width), F32),
    )(core, slots)


def _pair_gather(bufs):
    n = len(bufs)

    def body(*refs):
        outs = refs[n:2 * n]
        send_sems, recv_sems = refs[2 * n:]
        x, y, c = lax.axis_index("x"), lax.axis_index("y"), lax.axis_index("c")
        copies = [pltpu.make_async_remote_copy(
            src_ref=outs[t].at[c], dst_ref=outs[t].at[c], send_sem=send_sems.at[t], recv_sem=recv_sems.at[t],
            device_id=(x, y, 1 - c), device_id_type=MESH) for t in range(n)]
        for cp in copies:
            cp.start()
        for cp in copies:
            cp.wait()

    any_spec = pl.BlockSpec(memory_space=pl.ANY)
    res = pl.pallas_call(
        body, name="pair_gather",
        out_shape=[jax.ShapeDtypeStruct(b.shape, b.dtype) for b in bufs],
        in_specs=[any_spec] * n, out_specs=[any_spec] * n,
        input_output_aliases={t: t for t in range(n)},
        scratch_shapes=[pltpu.SemaphoreType.DMA((n,)), pltpu.SemaphoreType.DMA((n,))],
    )(*bufs)
    return [r.reshape(2 * b.shape[1], b.shape[2]) for r, b in zip(res, bufs)]


def _adamw(w, g, m, v, name):
    rows, width = w.shape
    tr = min(256, rows)
    c1 = 1.0 / (1.0 - ADAM_B1 ** ADAM_STEP)
    c2 = 1.0 / (1.0 - ADAM_B2 ** ADAM_STEP)

    def body(w_ref, g_ref, m_ref, v_ref, d_ref, nm_ref, nv_ref):
        gv = g_ref[...]
        nm = ADAM_B1 * m_ref[...] + (1.0 - ADAM_B1) * gv
        nv = ADAM_B2 * v_ref[...] + (1.0 - ADAM_B2) * (gv * gv)
        nm_ref[...] = nm
        nv_ref[...] = nv
        d_ref[...] = -ADAM_LR * ((nm * c1) / (jnp.sqrt(nv * c2) + ADAM_EPS) + ADAM_WD * w_ref[...])

    spec = pl.BlockSpec((tr, width), lambda i: (i, 0))
    return pl.pallas_call(
        body, name=name, grid=(rows // tr,), in_specs=[spec] * 4, out_specs=[spec] * 3,
        out_shape=[jax.ShapeDtypeStruct(w.shape, F32)] * 3,
    )(w, g, m, v)


def _local_step(x, mem, target, pre_norm, sink_a, mem_norm, post_norm, w_in_g, w_out, w_mkv):
    seq = x.shape[0]
    rope = _rope_tables(seq)
    mk, mv = _mem_kv(mem, mem_norm, w_mkv)
    pr = _pre_proj(x, pre_norm, w_in_g, rope)
    sink = sink_a.reshape(-1)
    qa, ka, va = pr["qa"][None], pr["ka"][None], pr["va"][None]
    oa, lse_a = _band_fwd(qa, ka, va, sink, max_dist=A_WINDOW - 1, name="swa_fwd")
    ob_list, lseb_list = [], []
    for k, (win, dil) in enumerate(B_CONFIGS):
        o_i, l_i = _band_fwd(pr["qb"][k], pr["kb"][k], pr["vb"][k], None, max_dist=win // dil, name=f"dil{dil}_fwd")
        ob_list.append(o_i)
        lseb_list.append(l_i)
    oc, lse_c = _mem_attn_fwd(pr["qc"], mk, mv)
    sink_row = jnp.pad(sink, (0, LANES - sink.shape[0])).reshape(1, LANES)
    po = _post(x, target, post_norm, w_out, sink_row, oa[0], lse_a[0], pr["ga"], ob_list, lseb_list, pr["gb"], oc,
               pr["gc"])
    dqc, dmk, dmv = _mem_attn_bwd(pr["qc"], mk, mv, po["doc"], lse_c, po["dl_c"])
    dqa, dka, dva = _band_bwd(qa, ka, va, po["doa"][None], lse_a, po["dl_a"][None], max_dist=A_WINDOW - 1,
                              name="swa_bwd")
    res = dict(qb=[], kb=[], vb=[])
    for k, (win, dil) in enumerate(B_CONFIGS):
        dq_i, dk_i, dv_i = _band_bwd(pr["qb"][k], pr["kb"][k], pr["vb"][k], po["dob"][k], po["lse_b"][k],
                                     po["dl_b"][k], max_dist=win // dil, name=f"dil{dil}_bwd")
        res["qb"].append(dq_i)
        res["kb"].append(dk_i)
        res["vb"].append(dv_i)
    nat = dict(qa=dqa[0], ka=dka[0], va=dva[0], ga=po["dga"], gb=po["dgb"], qc=dqc, gc=po["dgc"])
    dproj, grad_x, gpre = _dx(x, po["g"], pre_norm, w_in_g, rope, nat, res)
    gw_in = _grad_w_in(pr["ut"], dproj)
    gw_mkv, gmem = _mem_kv_bwd(mem, mem_norm, w_mkv, dmk, dmv)
    gsink = -po["gsink"][0, :sink.shape[0]]
    return dict(loss=po["loss"][0, 0], grad_x=grad_x, gw_in=gw_in, gw_out=po["gw_out"], gw_mkv=gw_mkv,
                gpre=gpre, gpost=po["gpost"], gmem=gmem, gsink=gsink)


def kernel(x, mem, pre_norm, w_in, sink_a, mem_norm, w_mem_kv, w_out, post_norm, loss_target, m_pre_norm, m_w_in, m_sink_a, m_mem_norm, m_w_mem_kv, m_w_out, m_post_norm, v_pre_norm, v_w_in, v_sink_a, v_mem_norm, v_w_mem_kv, v_w_out, v_post_norm):
    w_in_g, w_out_g, w_mkv_g = _gather_weights(w_in[0].astype(BF16), w_out[0].astype(BF16), w_mem_kv[0].astype(BF16))
    loc = _local_step(x[0], mem[0], loss_target[0], pre_norm, sink_a, mem_norm, post_norm,
                      w_in_g.reshape(N_CHIPS, D_MODEL, SHARD_IN), w_out_g.reshape(D_MODEL, D_MODEL),
                      w_mkv_g.reshape(D_MODEL, 2 * C_W))
    loss = lax.psum(loc["loss"], ("x", "y", "c"))

    big = [loc["gw_in"], loc["gw_out"].reshape(N_CHIPS, D_MODEL // N_CHIPS, D_MODEL),
           loc["gw_mkv"].reshape(N_CHIPS, D_MODEL // N_CHIPS, 2 * C_W)]
    small = jnp.concatenate([loc["gpre"], loc["gpost"], loc["gmem"],
                             jnp.pad(loc["gsink"], (0, D_MODEL - loc["gsink"].shape[0])).reshape(1, D_MODEL),
                             jnp.zeros((4, D_MODEL), F32)], axis=0)
    core = lax.axis_index("c").astype(jnp.int32).reshape(1)
    got = _pair_exchange(big)
    parts = [_pair_add(core, own, g) for own, g in zip(big, got)]
    *slots, small_slots = _chip_exchange(parts, small)
    halves = [_slot_sum(s, name=f"chip_sum_{s.shape[2]}", core=core) for s in slots]
    g_in, g_out, g_mkv = _pair_gather(halves)
    small_sum = _slot_sum(small_slots, name="device_sum")
    g_pre, g_post, g_mem = small_sum[0:1], small_sum[1:2], small_sum[2:3]
    g_sink = small_sum[3:4, :sink_a.shape[1]]

    d_in, nm_in, nv_in = _adamw(w_in[0], g_in, m_w_in[0], v_w_in[0], "adamw_in")
    d_out, nm_out, nv_out = _adamw(w_out[0], g_out, m_w_out[0], v_w_out[0], "adamw_out")
    d_mkv, nm_mkv, nv_mkv = _adamw(w_mem_kv[0], g_mkv, m_w_mem_kv[0], v_w_mem_kv[0], "adamw_mkv")
    pad6 = lambda a: jnp.pad(a, ((0, 0), (0, D_MODEL - a.shape[1])))
    stack = lambda a, b, c_, d_: jnp.concatenate([a, b, c_, pad6(d_), jnp.zeros((4, D_MODEL), F32)], axis=0)
    d_s, nm_s, nv_s = _adamw(stack(pre_norm, post_norm, mem_norm, sink_a), small_sum,
                             stack(m_pre_norm, m_post_norm, m_mem_norm, m_sink_a),
                             stack(v_pre_norm, v_post_norm, v_mem_norm, v_sink_a), "adamw_small")
    ns_ = sink_a.shape[1]
    unpack = lambda a: (a[0:1], a[3:4, :ns_], a[2:3], a[1:2])
    d_pre, d_sink, d_mem, d_post = unpack(d_s)
    nm_pre, nm_sink, nm_mem, nm_post = unpack(nm_s)
    nv_pre, nv_sink, nv_mem, nv_post = unpack(nv_s)
    lead = lambda a: a[None]
    return (loss, lead(loc["grad_x"]),
            g_pre, lead(g_in), g_sink, g_mem, lead(g_mkv), lead(g_out), g_post,
            d_pre, lead(d_in), d_sink, d_mem, lead(d_mkv), lead(d_out), d_post,
            nm_pre, lead(nm_in), nm_sink, nm_mem, lead(nm_mkv), lead(nm_out), nm_post,
            nv_pre, lead(nv_in), nv_sink, nv_mem, lead(nv_mkv), lead(nv_out), nv_post)
```

```python
import numpy as np
import jax
import jax.numpy as jnp
from jax import lax
from jax.experimental import pallas as pl
from jax.experimental.pallas import tpu as pltpu

F32 = jnp.float32
BF16 = jnp.bfloat16

D_MODEL = 1024
HEAD_DIM = 64
LANES = 128
BLOCK = 128
A_W, A_KV_W, B_W, C_W = 384, 128, 384, 256
N_MEM = 256
D_IN = 3072
N_CHIPS = 4
SHARD_IN = D_IN // N_CHIPS
B_CONFIGS = ((128, 1), (512, 4), (2048, 16))
B_DILS = tuple(d for _, d in B_CONFIGS)
A_WINDOW = 128
RMS_EPS = 1e-6
ROPE_THETA = 500000.0
SCALE = HEAD_DIM ** -0.5
NEG = -1e30
ADAM_LR, ADAM_B1, ADAM_B2, ADAM_EPS, ADAM_WD, ADAM_STEP = 0.001, 0.9, 0.999, 1e-08, 0.01, 10

NT = (((1,), (1,)), ((), ()))
TN = (((0,), (0,)), ((), ()))
MESH = pl.DeviceIdType.MESH

_PROJ_LAYOUT = (
    [("qa", 128 * i, True, True) for i in range(3)] + [("ka", 0, True, False), ("va", 0, False, False)]
    + [("ga", 128 * i, False, False) for i in range(3)]
    + [("qb", 128 * i, True, True) for i in range(3)] + [("kb", 128 * i, True, False) for i in range(3)]
    + [("vb", 128 * i, False, False) for i in range(3)] + [("gb", 128 * i, False, False) for i in range(3)]
    + [("qc", 128 * i, False, True) for i in range(2)] + [("gc", 128 * i, False, False) for i in range(2)]
)
_PROJ_WIDTH = dict(qa=A_W, ka=A_KV_W, va=A_KV_W, ga=A_W, qb=B_W, kb=B_W, vb=B_W, gb=B_W, qc=C_W, gc=C_W)
_NATURAL = ("qa", "ka", "va", "ga", "gb", "qc", "gc")
_DILATED = ("qb", "kb", "vb")


def _dot(a, b):
    return jnp.dot(a, b, preferred_element_type=F32)


def _dot_nt(a, b):
    return lax.dot_general(a, b, NT, preferred_element_type=F32)


def _dot_tn(a, b):
    return lax.dot_general(a, b, TN, preferred_element_type=F32)


def _half_masks(rows):
    lane = lax.broadcasted_iota(jnp.int32, (rows, LANES), 1)
    return lane < HEAD_DIM, lane >= HEAD_DIM


def _rope(t, c, sm, sp):
    return t * c + pltpu.roll(t, LANES - 8, 1) * sm + pltpu.roll(t, 8, 1) * sp


def _rope_tables(seq):
    dim = jnp.arange(LANES) % HEAD_DIM
    inv_freq = ROPE_THETA ** (-jnp.arange(0, 16, 2, dtype=F32) / 16)
    ang = jnp.arange(seq, dtype=F32)[:, None] * inv_freq[dim % 8][None, :]
    cos, sin = jnp.cos(ang), jnp.sin(ang)
    c = jnp.where(dim < 16, cos, 1.0)
    sm = jnp.where(dim < 8, -sin, 0.0)
    sp = jnp.where((dim >= 8) & (dim < 16), sin, 0.0)
    return c, sm, sp


def _split3(x):
    a = x.astype(BF16)
    r = x - a.astype(F32)
    b = r.astype(BF16)
    c = (r - b.astype(F32)).astype(BF16)
    return a, b, c


def _rows_to_lanes(x):
    row = lax.broadcasted_iota(jnp.int32, (8, LANES), 0)
    lane = lax.broadcasted_iota(jnp.int32, (8, LANES), 1)
    eye = (row == lane).astype(BF16)
    a, b, c = _split3(x)
    return _dot_nt(eye, a) + _dot_nt(eye, b) + _dot_nt(eye, c)


def _head_sum_matrix(width):
    k = lax.broadcasted_iota(jnp.int32, (width, LANES), 0)
    h = lax.broadcasted_iota(jnp.int32, (width, LANES), 1)
    return (k // HEAD_DIM == h).astype(BF16)


def _head_expand_matrix(width):
    h = lax.broadcasted_iota(jnp.int32, (LANES, width), 0)
    k = lax.broadcasted_iota(jnp.int32, (LANES, width), 1)
    return (k // HEAD_DIM == h).astype(BF16)


def _dot_split(x, mat, terms):
    parts = _split3(x)[:terms]
    out = _dot(parts[0], mat)
    for p in parts[1:]:
        out = out + _dot(p, mat)
    return out


def _per_head(cols, fill=0.0):
    rows = cols[0].shape[0]
    lane = lax.broadcasted_iota(jnp.int32, (rows, LANES), 1)
    out = jnp.full((rows, LANES), fill, F32)
    for h, col in enumerate(cols):
        out = jnp.where(lane == h, col, out)
    return out


def _lane_blocks(width):
    return [slice(p * LANES, (p + 1) * LANES) for p in range(width // LANES)]


def _stage(rows, width):
    return pltpu.VMEM((width // LANES, rows, LANES), F32)


def _stage_write(buf, value):
    for p, lanes in enumerate(_lane_blocks(value.shape[1])):
        buf[p] = value[:, lanes]


def _stage_read(buf):
    return jnp.concatenate([buf[p] for p in range(buf.shape[0])], axis=1) if buf.shape[0] > 1 else buf[0]


def _to_residues(buf, out_ref, dil):
    rows = buf.shape[1] // dil
    for r in range(dil):
        for p in range(buf.shape[0]):
            plane = buf.at[p]
            out_ref[r, :, p * LANES:(p + 1) * LANES] = plane[pl.ds(r, rows, stride=dil), :].astype(out_ref.dtype)


def _from_residues(in_ref, buf, dil):
    rows = buf.shape[1] // dil
    for r in range(dil):
        for p in range(buf.shape[0]):
            plane = buf.at[p]
            plane[pl.ds(r, rows, stride=dil), :] = in_ref[r, :, p * LANES:(p + 1) * LANES].astype(F32)


def _residue_spec(dil, tm, width):
    return pl.BlockSpec((dil, tm // dil, width), lambda i: (0, i, 0))


def _gather_weights(w_in_s, w_out_s, w_mkv_s):
    shards = tuple(s.reshape(2, s.shape[0] // 2, s.shape[1]) for s in (w_in_s, w_out_s, w_mkv_s))
    n = len(shards)

    def body(*refs):
        srcs, outs = refs[:n], refs[2 * n:3 * n]
        send_sems, recv_sems = refs[3 * n:]
        x, y, c = lax.axis_index("x"), lax.axis_index("y"), lax.axis_index("c")
        my_chip = 2 * x + y
        sibling = (x, y, 1 - c)
        chips = [(1 - x, y), (x, 1 - y), (1 - x, 1 - y)]

        def half(t, chip, which):
            return outs[t].at[chip, which]

        def src_half(t, which):
            return srcs[t].at[which]

        def copy(k, src, dst, to):
            return pltpu.make_async_remote_copy(src_ref=src, dst_ref=dst, send_sem=send_sems.at[k],
                                                recv_sem=recv_sems.at[k], device_id=to, device_id_type=MESH)

        first = []
        for j, (cx, cy) in enumerate(chips):
            for t in range(n):
                first.append(copy(n * j + t, src_half(t, c), half(t, my_chip, c), (cx, cy, c)))
        for cp in first:
            cp.start()
        passed = []
        for j, (cx, cy) in enumerate(chips):
            chip = 2 * cx + cy
            for t in range(n):
                k = n * j + t
                copy(k, src_half(t, c), half(t, chip, c), (cx, cy, c)).wait_recv()
                fwd = copy(n * 3 + k, half(t, chip, c), half(t, chip, c), sibling)
                fwd.start()
                passed.append(fwd)
        for j, (cx, cy) in enumerate(chips):
            chip = 2 * cx + cy
            for t in range(n):
                k = n * 3 + n * j + t
                copy(k, half(t, chip, 1 - c), half(t, chip, 1 - c), sibling).wait_recv()
        for cp in first + passed:
            cp.wait_send()

    my_chip = 2 * lax.axis_index("x") + lax.axis_index("y")
    landing = [lax.dynamic_update_slice(jnp.zeros((N_CHIPS,) + s.shape, s.dtype), s[None], (my_chip, 0, 0, 0))
               for s in shards]
    any_spec = pl.BlockSpec(memory_space=pl.ANY)
    return pl.pallas_call(
        body, name="gather_weights",
        out_shape=[jax.ShapeDtypeStruct((N_CHIPS,) + s.shape, s.dtype) for s in shards],
        in_specs=[any_spec] * (2 * n), out_specs=[any_spec] * n,
        input_output_aliases={n + t: t for t in range(n)},
        scratch_shapes=[pltpu.SemaphoreType.DMA((6 * n,)), pltpu.SemaphoreType.DMA((6 * n,))],
    )(*shards, *landing)


def _mem_kv(mem, mem_norm, w_mkv):
    def body(mem_ref, g_ref, w_ref, mk_ref, mv_ref):
        m = mem_ref[...]
        r = lax.rsqrt(jnp.mean(m * m, axis=-1, keepdims=True) + RMS_EPS)
        mn = (m * r * g_ref[...]).astype(BF16)
        kv = _dot(mn, w_ref[...])
        mk_ref[...] = kv[:, :C_W].astype(BF16)
        mv_ref[...] = kv[:, C_W:].astype(BF16)

    return pl.pallas_call(
        body, name="mem_kv",
        out_shape=[jax.ShapeDtypeStruct((N_MEM, C_W), BF16)] * 2,
    )(mem, mem_norm, w_mkv)


def _mem_kv_bwd(mem, mem_norm, w_mkv, dmk, dmv):
    def body(mem_ref, g_ref, w_ref, dmk_ref, dmv_ref, gw_ref, gn_ref):
        m = mem_ref[...]
        r = lax.rsqrt(jnp.mean(m * m, axis=-1, keepdims=True) + RMS_EPS)
        mhat = m * r
        mn = (mhat * g_ref[...]).astype(BF16)
        dkv = jnp.concatenate([dmk_ref[...], dmv_ref[...]], axis=1).astype(BF16)
        gw_ref[...] = _dot_tn(mn, dkv)
        dmn = _dot_nt(dkv, w_ref[...])
        gn_ref[...] = jnp.sum(dmn * mhat, axis=0, keepdims=True)

    return pl.pallas_call(
        body, name="mem_kv_bwd",
        out_shape=[jax.ShapeDtypeStruct((D_MODEL, 2 * C_W), F32), jax.ShapeDtypeStruct((1, D_MODEL), F32)],
    )(mem, mem_norm, w_mkv, dmk, dmv)


def _pre_proj(x, pre_norm, w_in_g, rope):
    seq = x.shape[0]
    tm = min(512, seq)
    n_nat, n_dil = len(_NATURAL), len(_DILATED) * len(B_DILS)

    def body(x_ref, g_ref, w_ref, c_ref, sm_ref, sp_ref, *refs):
        nat = dict(zip(_NATURAL, refs[:n_nat]))
        res = {n: refs[n_nat + len(B_DILS) * k:n_nat + len(B_DILS) * (k + 1)] for k, n in enumerate(_DILATED)}
        ut = refs[n_nat + n_dil]
        bufs = dict(zip(_DILATED, refs[n_nat + n_dil + 1:]))
        xv = x_ref[...]
        r = lax.rsqrt(jnp.mean(xv * xv, axis=-1, keepdims=True) + RMS_EPS)
        u = xv * r * g_ref[...]
        ub = u.astype(BF16)
        ut[...] = u.T.astype(BF16)
        c, sm, sp = c_ref[...], sm_ref[...], sp_ref[...]
        for j in range(N_CHIPS):
            pj = _dot(ub, w_ref[j])
            for b in range(SHARD_IN // LANES):
                name, off, roped, scaled = _PROJ_LAYOUT[(SHARD_IN // LANES) * j + b]
                piece = pj[:, LANES * b:LANES * (b + 1)]
                if roped:
                    piece = _rope(piece, c, sm, sp)
                if scaled:
                    piece = piece * SCALE
                if name in bufs:
                    bufs[name][off // LANES] = piece
                else:
                    nat[name][:, off:off + LANES] = piece.astype(BF16)
        for name in _DILATED:
            for ref, dil in zip(res[name], B_DILS):
                _to_residues(bufs[name], ref, dil)

    row = lambda w: pl.BlockSpec((tm, w), lambda i: (i, 0))
    full = lambda a: pl.BlockSpec(a.shape, lambda i: (0,) * a.ndim)
    out_shape = [jax.ShapeDtypeStruct((seq, _PROJ_WIDTH[n]), BF16) for n in _NATURAL]
    out_specs = [row(_PROJ_WIDTH[n]) for n in _NATURAL]
    for n in _DILATED:
        for dil in B_DILS:
            out_shape.append(jax.ShapeDtypeStruct((dil, seq // dil, B_W), BF16))
            out_specs.append(_residue_spec(dil, tm, B_W))
    out_shape.append(jax.ShapeDtypeStruct((D_MODEL, seq), BF16))
    out_specs.append(pl.BlockSpec((D_MODEL, tm), lambda i: (0, i)))
    res = pl.pallas_call(
        body, name="pre_proj", grid=(seq // tm,),
        in_specs=[row(D_MODEL), full(pre_norm), full(w_in_g), row(LANES), row(LANES), row(LANES)],
        out_specs=out_specs, out_shape=out_shape,
        scratch_shapes=[_stage(tm, B_W)] * len(_DILATED),
    )(x, pre_norm, w_in_g, *rope)
    out = dict(zip(_NATURAL, res[:n_nat]))
    for k, n in enumerate(_DILATED):
        out[n] = res[n_nat + len(B_DILS) * k:n_nat + len(B_DILS) * (k + 1)]
    out["ut"] = res[n_nat + n_dil]
    return out


def _band_bias(max_dist, transposed):
    i = np.arange(BLOCK)[:, None]
    j = np.arange(BLOCK)[None, :]
    if transposed:
        same = i <= j
        other = (j + BLOCK - i) <= max_dist
        vis = np.concatenate([same, other], axis=1)
    else:
        prev = (i + BLOCK - j) <= max_dist
        same = j <= i
        vis = np.concatenate([prev, same], axis=1)
    return jnp.asarray(np.where(vis, 0.0, NEG).astype(np.float32))


def _kv_place(h, gqa):
    return (0, h // 3) if gqa else (h // 2, h % 2)


def _band_fwd(q, k, v, sink, *, max_dist, name):
    dil, length, wq = q.shape
    wk = k.shape[2]
    gqa = wk != wq
    tq = min(512, length)
    ns, nt = tq // BLOCK, length // tq
    npair = wq // LANES
    bias = _band_bias(max_dist, transposed=False)
    has_sink = sink is not None

    def body(*refs):
        if has_sink:
            sink_ref, refs = refs[0], refs[1:]
        q_ref, k_ref, kp_ref, v_ref, vp_ref, bias_ref, o_ref, lse_ref, kbuf, vbuf = refs[:10]
        i = pl.program_id(1)
        kbuf[0:BLOCK] = kp_ref[...]
        kbuf[BLOCK:] = k_ref[...]
        vbuf[0:BLOCK] = vp_ref[...]
        vbuf[BLOCK:] = v_ref[...]
        if gqa:
            kroll, vroll = refs[10:12]
            kroll[...] = pltpu.roll(kbuf[...], HEAD_DIM, 1)
            vroll[...] = pltpu.roll(vbuf[...], HEAD_DIM, 1)
        half = _half_masks(BLOCK)
        col_prev = (lax.broadcasted_iota(jnp.int32, (1, 2 * BLOCK), 1) < BLOCK).astype(F32)

        def sub(a, carry):
            r0 = pl.multiple_of(a * BLOCK, BLOCK)
            pen = jnp.where((i == 0) & (a == 0), NEG, 0.0)
            b = bias_ref[...] + pen * col_prev
            scores = []
            for p in range(npair):
                qp = q_ref[pl.ds(r0, BLOCK), p * LANES:(p + 1) * LANES]
                for e in range(2):
                    pk, ek = _kv_place(2 * p + e, gqa)
                    kw = (kbuf if ek == e else kroll)[pl.ds(r0, 2 * BLOCK), pk * LANES:(pk + 1) * LANES]
                    scores.append(_dot_nt(jnp.where(half[e], qp, jnp.zeros_like(qp)), kw))
            m_cols, l_cols, probs = [], [], []
            for h, s in enumerate(scores):
                s = s + b
                m = jnp.max(s, axis=1, keepdims=True)
                if has_sink:
                    m = jnp.maximum(m, sink_ref[h])
                pe = jnp.exp(s - m)
                l = jnp.sum(pe, axis=1, keepdims=True)
                if has_sink:
                    l = l + jnp.exp(sink_ref[h] - m)
                probs.append(pe.astype(BF16))
                m_cols.append(m)
                l_cols.append(l)
            for p in range(npair):
                o_h = []
                for e in range(2):
                    h = 2 * p + e
                    pk, ek = _kv_place(h, gqa)
                    vw = (vbuf if ek == e else vroll)[pl.ds(r0, 2 * BLOCK), pk * LANES:(pk + 1) * LANES]
                    o_h.append(_dot(probs[h], vw) * (1.0 / l_cols[h]))
                o_ref[pl.ds(r0, BLOCK), p * LANES:(p + 1) * LANES] = jnp.where(half[0], o_h[0], o_h[1]).astype(BF16)
            lse_ref[pl.ds(r0, BLOCK), :] = _per_head(m_cols) + jnp.log(_per_head(l_cols, 1.0))
            return carry

        lax.fori_loop(0, ns, sub, 0, unroll=True)

    main = lambda w: pl.BlockSpec((None, tq, w), lambda r, i: (r, i, 0))
    prev = lambda w: pl.BlockSpec((None, BLOCK, w), lambda r, i: (r, jnp.maximum(i * ns - 1, 0), 0))
    in_specs = [main(wq), main(wk), prev(wk), main(wk), prev(wk), pl.BlockSpec(bias.shape, lambda r, i: (0, 0))]
    args = [q, k, k, v, v, bias]
    if has_sink:
        in_specs = [pl.BlockSpec(memory_space=pltpu.SMEM)] + in_specs
        args = [sink] + args
    scratch = [pltpu.VMEM((tq + BLOCK, wk), BF16)] * (4 if gqa else 2)
    return pl.pallas_call(
        body, name=name, grid=(dil, nt), in_specs=in_specs,
        out_specs=[main(wq), main(LANES)],
        out_shape=[jax.ShapeDtypeStruct((dil, length, wq), BF16), jax.ShapeDtypeStruct((dil, length, LANES), F32)],
        scratch_shapes=scratch,
    )(*args)


def _band_bwd(q, k, v, do, lse, delta, *, max_dist, name):
    dil, length, wq = q.shape
    wk = k.shape[2]
    gqa = wk != wq
    tq = min(512, length)
    ns, nt = tq // BLOCK, length // tq
    npair = wq // LANES
    nblocks = length // BLOCK
    bias = _band_bias(max_dist, transposed=True)

    def body(q_ref, qn_ref, do_ref, don_ref, lse_ref, lsen_ref, dl_ref, dln_ref, k_ref, v_ref, bias_ref,
             dq_ref, dk_ref, dv_ref, qbuf, dobuf, stat_l, stat_d, dqt, kt, *rolled):
        i = pl.program_id(1)
        qbuf[0:tq] = q_ref[...]
        qbuf[tq:] = qn_ref[...]
        dobuf[0:tq] = do_ref[...]
        dobuf[tq:] = don_ref[...]
        for pk in range(wk // LANES):
            kt[pk] = k_ref[:, pk * LANES:(pk + 1) * LANES].astype(F32).T.astype(BF16)
        if gqa:
            kroll, vroll, ktroll = rolled
            kroll[...] = pltpu.roll(k_ref[...], HEAD_DIM, 1)
            vroll[...] = pltpu.roll(v_ref[...], HEAD_DIM, 1)
            ktroll[0] = kroll[...].astype(F32).T.astype(BF16)
        for a in range(ns):
            rows = slice(a * BLOCK, (a + 1) * BLOCK)
            stat_l[a] = _rows_to_lanes(lse_ref[rows, :])
            stat_d[a] = _rows_to_lanes(dl_ref[rows, :])
        stat_l[ns] = _rows_to_lanes(lsen_ref[...])
        stat_d[ns] = _rows_to_lanes(dln_ref[...])

        @pl.when(i == 0)
        def _():
            dqt[:, :, 0:BLOCK] = jnp.zeros((npair, LANES, BLOCK), F32)

        @pl.when(i > 0)
        def _():
            dqt[:, :, 0:BLOCK] = dqt[:, :, tq:tq + BLOCK]

        dqt[:, :, BLOCK:] = jnp.zeros((npair, LANES, tq), F32)
        half2 = _half_masks(2 * BLOCK)
        row = lax.broadcasted_iota(jnp.int32, (LANES, BLOCK), 0)
        row_half = (row < HEAD_DIM, row >= HEAD_DIM)
        col_next = (lax.broadcasted_iota(jnp.int32, (1, 2 * BLOCK), 1) >= BLOCK).astype(F32)

        for b in range(ns):
            rows = slice(b * BLOCK, (b + 1) * BLOCK)
            window = slice(b * BLOCK, (b + 2) * BLOCK)
            bt = bias_ref[...]
            if b == ns - 1:
                bt = bt + jnp.where(i == nt - 1, NEG, 0.0) * col_next
            acc = {}
            items = []
            for p in range(npair):
                lanes = slice(p * LANES, (p + 1) * LANES)
                qw = qbuf[window, lanes]
                dow = dobuf[window, lanes]
                for e in range(2):
                    h = 2 * p + e
                    pk, ek = _kv_place(h, gqa)
                    klanes = slice(pk * LANES, (pk + 1) * LANES)
                    kb = (k_ref if ek == e else kroll)[rows, klanes]
                    vb = (v_ref if ek == e else vroll)[rows, klanes]
                    qm = jnp.where(half2[e], qw, jnp.zeros_like(qw))
                    dom = jnp.where(half2[e], dow, jnp.zeros_like(dow))
                    items.append(dict(p=p, e=e, h=h, pk=pk, ek=ek, qm=qm, dom=dom,
                                      st=_dot_nt(kb, qm), dpt=_dot_nt(vb, dom)))
            for it in items:
                h = it["h"]
                lrow = jnp.concatenate([stat_l[b, h:h + 1, :], stat_l[b + 1, h:h + 1, :]], axis=1)
                drow = jnp.concatenate([stat_d[b, h:h + 1, :], stat_d[b + 1, h:h + 1, :]], axis=1)
                pt = jnp.exp(it["st"] + bt - lrow)
                it["ptb"] = pt.astype(BF16)
                it["dsb"] = (pt * (it["dpt"] - drow)).astype(BF16)
            for it in items:
                p, e, pk, ek = it["p"], it["e"], it["pk"], it["ek"]
                dv_c = _dot(it["ptb"], it["dom"])
                dk_c = _dot(it["dsb"], it["qm"])
                kbt = (kt if ek == e else ktroll)[pk, :, rows]
                kbtm = jnp.where(row_half[e], kbt, jnp.zeros_like(kbt))
                dqt[p, :, window] += _dot(kbtm, it["dsb"])
                key = (pk, ek == e)
                if key in acc:
                    acc[key] = (acc[key][0] + dk_c, acc[key][1] + dv_c)
                else:
                    acc[key] = (dk_c, dv_c)
            if not gqa:
                for p in range(npair):
                    lanes = slice(p * LANES, (p + 1) * LANES)
                    dk_ref[rows, lanes] = acc[(p, True)][0].astype(BF16)
                    dv_ref[rows, lanes] = acc[(p, True)][1].astype(BF16)
            if gqa:
                dk_al, dv_al = acc[(0, True)]
                dk_mis, dv_mis = acc[(0, False)]
                dk_ref[rows, :] = (dk_al + pltpu.roll(dk_mis, HEAD_DIM, 1)).astype(BF16)
                dv_ref[rows, :] = (dv_al + pltpu.roll(dv_mis, HEAD_DIM, 1)).astype(BF16)

        for p in range(npair):
            dq_ref[:, p * LANES:(p + 1) * LANES] = dqt[p, :, 0:tq].T.astype(BF16)

    main = lambda w: pl.BlockSpec((None, tq, w), lambda r, i: (r, i, 0))
    nxt = lambda w: pl.BlockSpec((None, BLOCK, w), lambda r, i: (r, jnp.minimum((i + 1) * ns, nblocks - 1), 0))
    scratch = [pltpu.VMEM((tq + BLOCK, wq), BF16), pltpu.VMEM((tq + BLOCK, wq), BF16),
               pltpu.VMEM((ns + 1, 8, LANES), F32), pltpu.VMEM((ns + 1, 8, LANES), F32),
               pltpu.VMEM((npair, LANES, tq + BLOCK), F32), pltpu.VMEM((wk // LANES, LANES, tq), BF16)]
    if gqa:
        scratch = scratch + [pltpu.VMEM((tq, wk), BF16)] * 2 + [pltpu.VMEM((1, LANES, tq), BF16)]
    return pl.pallas_call(
        body, name=name, grid=(dil, nt),
        in_specs=[main(wq), nxt(wq), main(wq), nxt(wq), main(LANES), nxt(LANES), main(LANES), nxt(LANES),
                  main(wk), main(wk), pl.BlockSpec(bias.shape, lambda r, i: (0, 0))],
        out_specs=[main(wq), main(wk), main(wk)],
        out_shape=[jax.ShapeDtypeStruct((dil, length, wq), BF16), jax.ShapeDtypeStruct((dil, length, wk), BF16),
                   jax.ShapeDtypeStruct((dil, length, wk), BF16)],
        scratch_shapes=scratch,
        compiler_params=pltpu.CompilerParams(dimension_semantics=("arbitrary", "arbitrary")),
    )(q, q, do, do, lse, lse, delta, delta, k, v, bias)


def _mem_attn_fwd(q, mk, mv):
    seq = q.shape[0]
    tq = min(512, seq)
    ns = tq // BLOCK

    def body(q_ref, mk_ref, mv_ref, o_ref, lse_ref):
        half = _half_masks(BLOCK)

        def sub(a, carry):
            r0 = pl.multiple_of(a * BLOCK, BLOCK)
            scores = []
            for p in range(C_W // LANES):
                lanes = slice(p * LANES, (p + 1) * LANES)
                qp = q_ref[pl.ds(r0, BLOCK), lanes]
                for e in range(2):
                    scores.append(_dot_nt(jnp.where(half[e], qp, jnp.zeros_like(qp)), mk_ref[:, lanes]))
            m_cols, l_cols, probs = [], [], []
            for s in scores:
                m = jnp.max(s, axis=1, keepdims=True)
                pe = jnp.exp(s - m)
                probs.append(pe.astype(BF16))
                m_cols.append(m)
                l_cols.append(jnp.sum(pe, axis=1, keepdims=True))
            for p in range(C_W // LANES):
                lanes = slice(p * LANES, (p + 1) * LANES)
                o_h = [_dot(probs[2 * p + e], mv_ref[:, lanes]) * (1.0 / l_cols[2 * p + e]) for e in range(2)]
                o_ref[pl.ds(r0, BLOCK), lanes] = jnp.where(half[0], o_h[0], o_h[1]).astype(BF16)
            lse_ref[pl.ds(r0, BLOCK), :] = _per_head(m_cols) + jnp.log(_per_head(l_cols, 1.0))
            return carry

        lax.fori_loop(0, ns, sub, 0, unroll=True)

    row = lambda w: pl.BlockSpec((tq, w), lambda i: (i, 0))
    full = pl.BlockSpec((N_MEM, C_W), lambda i: (0, 0))
    return pl.pallas_call(
        body, name="mem_attn_fwd", grid=(seq // tq,), in_specs=[row(C_W), full, full],
        out_specs=[row(C_W), row(LANES)],
        out_shape=[jax.ShapeDtypeStruct((seq, C_W), BF16), jax.ShapeDtypeStruct((seq, LANES), F32)],
    )(q, mk, mv)


def _mem_attn_bwd(q, mk, mv, do, lse, delta):
    seq = q.shape[0]
    tq = min(512, seq)
    ns = tq // BLOCK
    npair = C_W // LANES

    def body(q_ref, mk_ref, mv_ref, do_ref, lse_ref, dl_ref, dq_ref, dmk_ref, dmv_ref, stat_l, stat_d, mkt, dqt):
        @pl.when(pl.program_id(0) == 0)
        def _():
            dmk_ref[...] = jnp.zeros_like(dmk_ref)
            dmv_ref[...] = jnp.zeros_like(dmv_ref)
            for p in range(npair):
                mkt[p] = mk_ref[:, p * LANES:(p + 1) * LANES].astype(F32).T.astype(BF16)

        for a in range(ns):
            rows = slice(a * BLOCK, (a + 1) * BLOCK)
            stat_l[a] = _rows_to_lanes(lse_ref[rows, :])
            stat_d[a] = _rows_to_lanes(dl_ref[rows, :])
        half = _half_masks(BLOCK)
        row = lax.broadcasted_iota(jnp.int32, (LANES, N_MEM), 0)
        row_half = (row < HEAD_DIM, row >= HEAD_DIM)

        for a in range(ns):
            rows = slice(a * BLOCK, (a + 1) * BLOCK)
            items = []
            for p in range(npair):
                lanes = slice(p * LANES, (p + 1) * LANES)
                qp = q_ref[rows, lanes]
                dop = do_ref[rows, lanes]
                for e in range(2):
                    qm = jnp.where(half[e], qp, jnp.zeros_like(qp))
                    dom = jnp.where(half[e], dop, jnp.zeros_like(dop))
                    items.append(dict(p=p, e=e, qm=qm, dom=dom, st=_dot_nt(mk_ref[:, lanes], qm),
                                      dpt=_dot_nt(mv_ref[:, lanes], dom)))
            for it in items:
                h = 2 * it["p"] + it["e"]
                pt = jnp.exp(it["st"] - stat_l[a, h:h + 1, :])
                it["ptb"] = pt.astype(BF16)
                it["dsb"] = (pt * (it["dpt"] - stat_d[a, h:h + 1, :])).astype(BF16)
            for p in range(npair):
                lanes = slice(p * LANES, (p + 1) * LANES)
                pair = [it for it in items if it["p"] == p]
                dmv_ref[:, lanes] += _dot(pair[0]["ptb"], pair[0]["dom"]) + _dot(pair[1]["ptb"], pair[1]["dom"])
                dmk_ref[:, lanes] += _dot(pair[0]["dsb"], pair[0]["qm"]) + _dot(pair[1]["dsb"], pair[1]["qm"])
                kbt = mkt[p]
                dqt[p, :, rows] = (_dot(jnp.where(row_half[0], kbt, jnp.zeros_like(kbt)), pair[0]["dsb"])
                                   + _dot(jnp.where(row_half[1], kbt, jnp.zeros_like(kbt)), pair[1]["dsb"]))
        for p in range(npair):
            dq_ref[:, p * LANES:(p + 1) * LANES] = dqt[p].T.astype(BF16)

    row = lambda w: pl.BlockSpec((tq, w), lambda i: (i, 0))
    full = pl.BlockSpec((N_MEM, C_W), lambda i: (0, 0))
    return pl.pallas_call(
        body, name="mem_attn_bwd", grid=(seq // tq,),
        in_specs=[row(C_W), full, full, row(C_W), row(LANES), row(LANES)], out_specs=[row(C_W), full, full],
        out_shape=[jax.ShapeDtypeStruct((seq, C_W), BF16), jax.ShapeDtypeStruct((N_MEM, C_W), F32),
                   jax.ShapeDtypeStruct((N_MEM, C_W), F32)],
        scratch_shapes=[pltpu.VMEM((ns, 8, LANES), F32)] * 2
        + [pltpu.VMEM((npair, LANES, N_MEM), BF16), pltpu.VMEM((npair, LANES, tq), F32)],
        compiler_params=pltpu.CompilerParams(dimension_semantics=("arbitrary",)),
    )(q, mk, mv, do, lse, delta)


def _silu_and_grad(g):
    s = 1.0 / (1.0 + jnp.exp(-g))
    return g * s, s * (1.0 + g * (1.0 - s))


def _post(x, target, post_norm, w_out, sink_row, oa, lse_a, ga, ob_list, lseb_list, gb, oc, gc):
    seq = x.shape[0]
    tm = min(256, seq)
    inv_d = 1.0 / D_MODEL
    nd = len(B_DILS)

    def body(*refs):
        (x_ref, t_ref, gp_ref, w_ref, sink_ref, oa_ref, lsea_ref, ga_ref), refs = refs[:8], refs[8:]
        ob_refs, lb_refs, (gb_ref, oc_ref, gc_ref), refs = refs[:nd], refs[nd:2 * nd], refs[2 * nd:2 * nd + 3], refs[2 * nd + 3:]
        (g_ref, doa_ref, dla_ref, dga_ref), refs = refs[:4], refs[4:]
        dob_refs, lsec_refs, dlb_refs, refs = refs[:nd], refs[nd:2 * nd], refs[2 * nd:3 * nd], refs[3 * nd:]
        (dgb_ref, doc_ref, dlc_ref, dgc_ref, gw_ref, gpost_ref, gsink_ref, loss_ref), refs = refs[:8], refs[8:]
        ycat, obufs, lbufs, st_do, st_l, st_d = refs[0], refs[1:nd], refs[nd:2 * nd - 1], refs[2 * nd - 1], refs[2 * nd], refs[2 * nd + 1]

        @pl.when(pl.program_id(0) == 0)
        def _():
            gw_ref[...] = jnp.zeros_like(gw_ref)
            gpost_ref[...] = jnp.zeros_like(gpost_ref)
            gsink_ref[...] = jnp.zeros_like(gsink_ref)
            loss_ref[...] = jnp.zeros_like(loss_ref)

        o_i, l_i = [ob_refs[0][0].astype(F32)], [lb_refs[0][0]]
        for k in range(1, nd):
            _from_residues(ob_refs[k], obufs[k - 1], B_DILS[k])
            _from_residues(lb_refs[k], lbufs[k - 1], B_DILS[k])
            o_i.append(_stage_read(obufs[k - 1]))
            l_i.append(_stage_read(lbufs[k - 1]))
        mx = l_i[0]
        for l in l_i[1:]:
            mx = jnp.maximum(mx, l)
        w_i = [jnp.exp(l - mx) for l in l_i]
        z = w_i[0]
        for w in w_i[1:]:
            z = z + w
        lse_b = mx + jnp.log(z)
        expand = _head_expand_matrix(B_W)
        inv_z = 1.0 / z
        ob = None
        for w, o in zip(w_i, o_i):
            term = _dot_split(w * inv_z, expand, 2) * o
            ob = term if ob is None else ob + term
        oa, oc = oa_ref[...].astype(F32), oc_ref[...].astype(F32)
        sa, dsa = _silu_and_grad(ga_ref[...].astype(F32))
        sb, dsb = _silu_and_grad(gb_ref[...].astype(F32))
        sc, dsc = _silu_and_grad(gc_ref[...].astype(F32))
        ycat[:, 0:A_W] = (oa * sa).astype(BF16)
        ycat[:, A_W:A_W + B_W] = (ob * sb).astype(BF16)
        ycat[:, A_W + B_W:] = (oc * sc).astype(BF16)
        yc = ycat[...]
        y2 = _dot(yc, w_ref[...])
        r = lax.rsqrt(jnp.mean(y2 * y2, axis=-1, keepdims=True) + RMS_EPS)
        zhat = y2 * r
        gp = gp_ref[...]
        err = x_ref[...] + zhat * gp - t_ref[...]
        loss_ref[...] += jnp.sum(err * err) * (0.5 * inv_d)
        g = err * inv_d
        g_ref[...] = g
        gpost_ref[...] += jnp.sum(g * zhat, axis=0, keepdims=True)
        a = g * gp
        dy2 = (r * (a - zhat * jnp.mean(a * zhat, axis=-1, keepdims=True))).astype(BF16)
        gw_ref[...] += _dot_tn(yc, dy2)
        dycat = _dot_nt(dy2, w_ref[...])
        dya, dyb, dyc = dycat[:, 0:A_W], dycat[:, A_W:A_W + B_W], dycat[:, A_W + B_W:]
        doa, dob, doc = dya * sa, dyb * sb, dyc * sc
        doa_ref[...] = doa.astype(BF16)
        doc_ref[...] = doc.astype(BF16)
        dga_ref[...] = (dya * oa * dsa).astype(BF16)
        dgb_ref[...] = (dyb * ob * dsb).astype(BF16)
        dgc_ref[...] = (dyc * oc * dsc).astype(BF16)
        dl_a = _dot_split(doa * oa, _head_sum_matrix(A_W), 3)
        dla_ref[...] = dl_a
        dlc_ref[...] = _dot_split(doc * oc, _head_sum_matrix(C_W), 3)
        gsink_ref[...] += jnp.sum(jnp.exp(sink_ref[...] - lsea_ref[...]) * dl_a, axis=0, keepdims=True)
        _stage_write(st_do, dob)
        _stage_write(st_l, lse_b)
        _stage_write(st_d, _dot_split(dob * ob, _head_sum_matrix(B_W), 3))
        for k, dil in enumerate(B_DILS):
            _to_residues(st_do, dob_refs[k], dil)
            _to_residues(st_l, lsec_refs[k], dil)
            _to_residues(st_d, dlb_refs[k], dil)

    row = lambda w: pl.BlockSpec((tm, w), lambda i: (i, 0))
    full = lambda shape: pl.BlockSpec(shape, lambda i: (0,) * len(shape))
    res_specs = lambda w: [_residue_spec(d, tm, w) for d in B_DILS]
    res_shapes = lambda w, dt: [jax.ShapeDtypeStruct((d, seq // d, w), dt) for d in B_DILS]
    ins = [x, target, post_norm, w_out, sink_row, oa, lse_a, ga, *ob_list, *lseb_list, gb, oc, gc]
    in_specs = ([row(D_MODEL), row(D_MODEL), full((1, D_MODEL)), full((D_MODEL, D_MODEL)), full((1, LANES)),
                 row(A_W), row(LANES), row(A_W)] + res_specs(B_W) + res_specs(LANES) + [row(B_W), row(C_W), row(C_W)])
    out_shape = ([jax.ShapeDtypeStruct((seq, D_MODEL), F32), jax.ShapeDtypeStruct((seq, A_W), BF16),
                  jax.ShapeDtypeStruct((seq, LANES), F32), jax.ShapeDtypeStruct((seq, A_W), BF16)]
                 + res_shapes(B_W, BF16) + res_shapes(LANES, F32) + res_shapes(LANES, F32)
                 + [jax.ShapeDtypeStruct((seq, B_W), BF16), jax.ShapeDtypeStruct((seq, C_W), BF16),
                    jax.ShapeDtypeStruct((seq, LANES), F32), jax.ShapeDtypeStruct((seq, C_W), BF16),
                    jax.ShapeDtypeStruct((D_MODEL, D_MODEL), F32), jax.ShapeDtypeStruct((1, D_MODEL), F32),
                    jax.ShapeDtypeStruct((1, LANES), F32), jax.ShapeDtypeStruct((1, LANES), F32)])
    out_specs = ([row(D_MODEL), row(A_W), row(LANES), row(A_W)] + res_specs(B_W) + res_specs(LANES) + res_specs(LANES)
                 + [row(B_W), row(C_W), row(LANES), row(C_W),
                    full((D_MODEL, D_MODEL)), full((1, D_MODEL)), full((1, LANES)), full((1, LANES))])
    scratch = ([pltpu.VMEM((tm, D_MODEL), BF16)] + [_stage(tm, B_W)] * (nd - 1) + [_stage(tm, LANES)] * (nd - 1)
               + [_stage(tm, B_W), _stage(tm, LANES), _stage(tm, LANES)])
    res = pl.pallas_call(
        body, name="post", grid=(seq // tm,), in_specs=in_specs, out_specs=out_specs, out_shape=out_shape,
        scratch_shapes=scratch,
        compiler_params=pltpu.CompilerParams(dimension_semantics=("arbitrary",)),
    )(*ins)
    out = dict(g=res[0], doa=res[1], dl_a=res[2], dga=res[3], dob=res[4:4 + nd], lse_b=res[4 + nd:4 + 2 * nd],
               dl_b=res[4 + 2 * nd:4 + 3 * nd])
    rest = res[4 + 3 * nd:]
    out.update(dgb=rest[0], doc=rest[1], dl_c=rest[2], dgc=rest[3], gw_out=rest[4], gpost=rest[5], gsink=rest[6],
               loss=rest[7])
    return out


def _dx(x, g, pre_norm, w_in_g, rope, nat, res):
    seq = x.shape[0]
    tm = min(256, seq)
    nd = len(B_DILS)
    nat_list = [nat[n] for n in _NATURAL]
    res_list = [a for n in _DILATED for a in res[n]]

    def body(x_ref, g_ref, gp_ref, w_ref, c_ref, sm_ref, sp_ref, *refs):
        nat_refs = dict(zip(_NATURAL, refs[:len(_NATURAL)]))
        refs = refs[len(_NATURAL):]
        res_refs = {n: refs[nd * k:nd * (k + 1)] for k, n in enumerate(_DILATED)}
        refs = refs[nd * len(_DILATED):]
        dproj_ref, gx_ref, gpre_ref = refs[:3]
        bufs = {n: refs[3 + (nd - 1) * k:3 + (nd - 1) * (k + 1)] for k, n in enumerate(_DILATED)}

        @pl.when(pl.program_id(0) == 0)
        def _():
            gpre_ref[...] = jnp.zeros_like(gpre_ref)

        for n in _DILATED:
            for k in range(1, nd):
                _from_residues(res_refs[n][k], bufs[n][k - 1], B_DILS[k])
        c, sm, sp = c_ref[...], -sm_ref[...], -sp_ref[...]
        for blk, (name, off, roped, scaled) in enumerate(_PROJ_LAYOUT):
            lanes = slice(off, off + LANES)
            if name in nat_refs:
                piece = nat_refs[name][:, lanes].astype(F32)
            else:
                piece = res_refs[name][0][0, :, lanes].astype(F32)
                for buf in bufs[name]:
                    piece = piece + buf[off // LANES]
            if roped:
                piece = _rope(piece, c, sm, sp)
            if scaled:
                piece = piece * SCALE
            dproj_ref[:, blk * LANES:(blk + 1) * LANES] = piece.astype(BF16)
        du = None
        for j in range(N_CHIPS):
            part = _dot_nt(dproj_ref[:, j * SHARD_IN:(j + 1) * SHARD_IN], w_ref[j])
            du = part if du is None else du + part
        xv = x_ref[...]
        r = lax.rsqrt(jnp.mean(xv * xv, axis=-1, keepdims=True) + RMS_EPS)
        xhat = xv * r
        gpre_ref[...] += jnp.sum(du * xhat, axis=0, keepdims=True)
        a = du * gp_ref[...]
        gx_ref[...] = g_ref[...] + r * (a - xhat * jnp.mean(a * xhat, axis=-1, keepdims=True))

    row = lambda w: pl.BlockSpec((tm, w), lambda i: (i, 0))
    full = lambda a: pl.BlockSpec(a.shape, lambda i: (0,) * a.ndim)
    in_specs = ([row(D_MODEL), row(D_MODEL), full(pre_norm), full(w_in_g), row(LANES), row(LANES), row(LANES)]
                + [row(a.shape[1]) for a in nat_list]
                + [_residue_spec(d, tm, B_W) for _ in _DILATED for d in B_DILS])
    return pl.pallas_call(
        body, name="dx", grid=(seq // tm,), in_specs=in_specs,
        out_specs=[row(D_IN), row(D_MODEL), pl.BlockSpec((1, D_MODEL), lambda i: (0, 0))],
        out_shape=[jax.ShapeDtypeStruct((seq, D_IN), BF16), jax.ShapeDtypeStruct((seq, D_MODEL), F32),
                   jax.ShapeDtypeStruct((1, D_MODEL), F32)],
        scratch_shapes=[_stage(tm, B_W)] * ((nd - 1) * len(_DILATED)),
        compiler_params=pltpu.CompilerParams(dimension_semantics=("arbitrary",)),
    )(x, g, pre_norm, w_in_g, *rope, *nat_list, *res_list)


def _grad_w_in(ut, dproj):
    seq = ut.shape[1]
    tk = min(1024, seq)

    def body(ut_ref, dp_ref, out_ref):
        @pl.when(pl.program_id(1) == 0)
        def _():
            out_ref[...] = jnp.zeros_like(out_ref)

        out_ref[...] += _dot(ut_ref[...], dp_ref[...])

    return pl.pallas_call(
        body, name="grad_w_in", grid=(N_CHIPS, seq // tk),
        in_specs=[pl.BlockSpec((D_MODEL, tk), lambda j, i: (0, i)), pl.BlockSpec((tk, SHARD_IN), lambda j, i: (i, j))],
        out_specs=pl.BlockSpec((None, D_MODEL, SHARD_IN), lambda j, i: (j, 0, 0)),
        out_shape=jax.ShapeDtypeStruct((N_CHIPS, D_MODEL, SHARD_IN), F32),
        compiler_params=pltpu.CompilerParams(dimension_semantics=("arbitrary", "arbitrary")),
    )(ut, dproj)


def _pair_exchange(grads):
    n = len(grads)

    def body(*refs):
        srcs, outs = refs[:n], refs[n:2 * n]
        send_sems, recv_sems = refs[2 * n:]
        x, y, c = lax.axis_index("x"), lax.axis_index("y"), lax.axis_index("c")
        copies = []
        for t in range(n):
            rows = grads[t].shape[1] // 2
            copies.append(pltpu.make_async_remote_copy(
                src_ref=srcs[t].at[:, pl.ds((1 - c) * rows, rows)], dst_ref=outs[t],
                send_sem=send_sems.at[t], recv_sem=recv_sems.at[t], device_id=(x, y, 1 - c), device_id_type=MESH))
        for cp in copies:
            cp.start()
        for cp in copies:
            cp.wait()

    any_spec = pl.BlockSpec(memory_space=pl.ANY)
    return pl.pallas_call(
        body, name="pair_exchange",
        out_shape=[jax.ShapeDtypeStruct((g.shape[0], g.shape[1] // 2, g.shape[2]), g.dtype) for g in grads],
        in_specs=[any_spec] * n, out_specs=[any_spec] * n,
        scratch_shapes=[pltpu.SemaphoreType.DMA((n,)), pltpu.SemaphoreType.DMA((n,))],
    )(*grads)


def _pair_add(core, own, got):
    nchip, rows2, width = own.shape
    rows = rows2 // 2
    tr = min(128, rows)
    nb = rows // tr

    def body(core_ref, own_ref, got_ref, out_ref):
        out_ref[...] = (own_ref[...] + got_ref[...]).astype(BF16)

    grid_spec = pltpu.PrefetchScalarGridSpec(
        num_scalar_prefetch=1, grid=(nchip, nb),
        in_specs=[pl.BlockSpec((None, tr, width), lambda k, i, core_ref: (k, core_ref[0] * nb + i, 0)),
                  pl.BlockSpec((None, tr, width), lambda k, i, core_ref: (k, i, 0))],
        out_specs=pl.BlockSpec((None, tr, width), lambda k, i, core_ref: (k, i, 0)))
    return pl.pallas_call(
        body, name=f"pair_add_{width}", grid_spec=grid_spec,
        out_shape=jax.ShapeDtypeStruct((nchip, rows, width), BF16),
    )(core, own, got)


def _chip_exchange(parts, small):
    n = len(parts)

    def body(*refs):
        srcs, small_ref = refs[:n], refs[n]
        outs, small_out = refs[n + 1:2 * n + 1], refs[2 * n + 1]
        send_sems, recv_sems, local_sems = refs[2 * n + 2:]
        x, y, c = lax.axis_index("x"), lax.axis_index("y"), lax.axis_index("c")
        my_chip = 2 * x + y
        me = 4 * x + 2 * y + c
        chips = [(1 - x, y), (x, 1 - y), (1 - x, 1 - y)]
        local = [pltpu.make_async_copy(srcs[t].at[my_chip], outs[t].at[my_chip], local_sems.at[t]) for t in range(n)]
        local.append(pltpu.make_async_copy(small_ref, small_out.at[me], local_sems.at[n]))
        for cp in local:
            cp.start()
        sent = []
        for j, (cx, cy) in enumerate(chips):
            for t in range(n):
                k = n * j + t
                sent.append(pltpu.make_async_remote_copy(
                    src_ref=srcs[t].at[2 * cx + cy], dst_ref=outs[t].at[my_chip], send_sem=send_sems.at[k],
                    recv_sem=recv_sems.at[k], device_id=(cx, cy, c), device_id_type=MESH))
        peers = [(x, y, 1 - c)] + [(cx, cy, cc) for (cx, cy) in chips for cc in (c, 1 - c)]
        for j, peer in enumerate(peers):
            k = 3 * n + j
            sent.append(pltpu.make_async_remote_copy(
                src_ref=small_ref, dst_ref=small_out.at[me], send_sem=send_sems.at[k], recv_sem=recv_sems.at[k],
                device_id=peer, device_id_type=MESH))
        for cp in sent:
            cp.start()
        for cp in sent:
            cp.wait()
        for cp in local:
            cp.wait()

    any_spec = pl.BlockSpec(memory_space=pl.ANY)
    nsem = 3 * n + 7
    return pl.pallas_call(
        body, name="chip_exchange",
        out_shape=[jax.ShapeDtypeStruct(p.shape, p.dtype) for p in parts]
        + [jax.ShapeDtypeStruct((8,) + small.shape, small.dtype)],
        in_specs=[any_spec] * (n + 1), out_specs=[any_spec] * (n + 1),
        scratch_shapes=[pltpu.SemaphoreType.DMA((nsem,)), pltpu.SemaphoreType.DMA((nsem,)),
                        pltpu.SemaphoreType.DMA((n + 1,))],
    )(*parts, small)


def _slot_sum(slots, name, core=None):
    ns, rows, width = slots.shape
    tr = min(128, rows)

    def body(*refs):
        in_ref, out_ref = refs[-2:]
        acc = in_ref[0].astype(F32)
        for s in range(1, ns):
            acc = acc + in_ref[s].astype(F32)
        out_ref[...] = acc

    if core is None:
        return pl.pallas_call(
            body, name=name, grid=(rows // tr,),
            in_specs=[pl.BlockSpec((ns, tr, width), lambda i: (0, i, 0))],
            out_specs=pl.BlockSpec((tr, width), lambda i: (i, 0)),
            out_shape=jax.ShapeDtypeStruct((rows, width), F32),
        )(slots)
    grid_spec = pltpu.PrefetchScalarGridSpec(
        num_scalar_prefetch=1, grid=(rows // tr,),
        in_specs=[pl.BlockSpec((ns, tr, width), lambda i, core_ref: (0, i, 0))],
        out_specs=pl.BlockSpec((None, tr, width), lambda i, core_ref: (core_ref[0], i, 0)))
    return pl.pallas_call(
        body, name=name, grid_spec=grid_spec, out_shape=jax.ShapeDtypeStruct((2, rows, width), F32),
    )(core, slots)


def _pair_gather(bufs):
    n = len(bufs)

    def body(*refs):
        outs = refs[n:2 * n]
        send_sems, recv_sems = refs[2 * n:]
        x, y, c = lax.axis_index("x"), lax.axis_index("y"), lax.axis_index("c")
        copies = [pltpu.make_async_remote_copy(
            src_ref=outs[t].at[c], dst_ref=outs[t].at[c], send_sem=send_sems.at[t], recv_sem=recv_sems.at[t],
            device_id=(x, y, 1 - c), device_id_type=MESH) for t in range(n)]
        for cp in copies:
            cp.start()
        for cp in copies:
            cp.wait()

    any_spec = pl.BlockSpec(memory_space=pl.ANY)
    res = pl.pallas_call(
        body, name="pair_gather",
        out_shape=[jax.ShapeDtypeStruct(b.shape, b.dtype) for b in bufs],
        in_specs=[any_spec] * n, out_specs=[any_spec] * n,
        input_output_aliases={t: t for t in range(n)},
        scratch_shapes=[pltpu.SemaphoreType.DMA((n,)), pltpu.SemaphoreType.DMA((n,))],
    )(*bufs)
    return [r.reshape(2 * b.shape[1], b.shape[2]) for r, b in zip(res, bufs)]


def _adamw(w, g, m, v, name):
    rows, width = w.shape
    tr = min(256, rows)
    c1 = 1.0 / (1.0 - ADAM_B1 ** ADAM_STEP)
    c2 = 1.0 / (1.0 - ADAM_B2 ** ADAM_STEP)

    def body(w_ref, g_ref, m_ref, v_ref, d_ref, nm_ref, nv_ref):
        gv = g_ref[...]
        nm = ADAM_B1 * m_ref[...] + (1.0 - ADAM_B1) * gv
        nv = ADAM_B2 * v_ref[...] + (1.0 - ADAM_B2) * (gv * gv)
        nm_ref[...] = nm
        nv_ref[...] = nv
        d_ref[...] = -ADAM_LR * ((nm * c1) / (jnp.sqrt(nv * c2) + ADAM_EPS) + ADAM_WD * w_ref[...])

    spec = pl.BlockSpec((tr, width), lambda i: (i, 0))
    return pl.pallas_call(
        body, name=name, grid=(rows // tr,), in_specs=[spec] * 4, out_specs=[spec] * 3,
        out_shape=[jax.ShapeDtypeStruct(w.shape, F32)] * 3,
    )(w, g, m, v)


def _local_step(x, mem, target, pre_norm, sink_a, mem_norm, post_norm, w_in_g, w_out, w_mkv):
    seq = x.shape[0]
    rope = _rope_tables(seq)
    mk, mv = _mem_kv(mem, mem_norm, w_mkv)
    pr = _pre_proj(x, pre_norm, w_in_g, rope)
    sink = sink_a.reshape(-1)
    qa, ka, va = pr["qa"][None], pr["ka"][None], pr["va"][None]
    oa, lse_a = _band_fwd(qa, ka, va, sink, max_dist=A_WINDOW - 1, name="swa_fwd")
    ob_list, lseb_list = [], []
    for k, (win, dil) in enumerate(B_CONFIGS):
        o_i, l_i = _band_fwd(pr["qb"][k], pr["kb"][k], pr["vb"][k], None, max_dist=win // dil, name=f"dil{dil}_fwd")
        ob_list.append(o_i)
        lseb_list.append(l_i)
    oc, lse_c = _mem_attn_fwd(pr["qc"], mk, mv)
    sink_row = jnp.pad(sink, (0, LANES - sink.shape[0])).reshape(1, LANES)
    po = _post(x, target, post_norm, w_out, sink_row, oa[0], lse_a[0], pr["ga"], ob_list, lseb_list, pr["gb"], oc,
               pr["gc"])
    dqc, dmk, dmv = _mem_attn_bwd(pr["qc"], mk, mv, po["doc"], lse_c, po["dl_c"])
    dqa, dka, dva = _band_bwd(qa, ka, va, po["doa"][None], lse_a, po["dl_a"][None], max_dist=A_WINDOW - 1,
                              name="swa_bwd")
    res = dict(qb=[], kb=[], vb=[])
    for k, (win, dil) in enumerate(B_CONFIGS):
        dq_i, dk_i, dv_i = _band_bwd(pr["qb"][k], pr["kb"][k], pr["vb"][k], po["dob"][k], po["lse_b"][k],
                                     po["dl_b"][k], max_dist=win // dil, name=f"dil{dil}_bwd")
        res["qb"].append(dq_i)
        res["kb"].append(dk_i)
        res["vb"].append(dv_i)
    nat = dict(qa=dqa[0], ka=dka[0], va=dva[0], ga=po["dga"], gb=po["dgb"], qc=dqc, gc=po["dgc"])
    dproj, grad_x, gpre = _dx(x, po["g"], pre_norm, w_in_g, rope, nat, res)
    gw_in = _grad_w_in(pr["ut"], dproj)
    gw_mkv, gmem = _mem_kv_bwd(mem, mem_norm, w_mkv, dmk, dmv)
    gsink = -po["gsink"][0, :sink.shape[0]]
    return dict(loss=po["loss"][0, 0], grad_x=grad_x, gw_in=gw_in, gw_out=po["gw_out"], gw_mkv=gw_mkv,
                gpre=gpre, gpost=po["gpost"], gmem=gmem, gsink=gsink)


def kernel(x, mem, pre_norm, w_in, sink_a, mem_norm, w_mem_kv, w_out, post_norm, loss_target, m_pre_norm, m_w_in, m_sink_a, m_mem_norm, m_w_mem_kv, m_w_out, m_post_norm, v_pre_norm, v_w_in, v_sink_a, v_mem_norm, v_w_mem_kv, v_w_out, v_post_norm):
    w_in_g, w_out_g, w_mkv_g = _gather_weights(w_in[0].astype(BF16), w_out[0].astype(BF16), w_mem_kv[0].astype(BF16))
    loc = _local_step(x[0], mem[0], loss_target[0], pre_norm, sink_a, mem_norm, post_norm,
                      w_in_g.reshape(N_CHIPS, D_MODEL, SHARD_IN), w_out_g.reshape(D_MODEL, D_MODEL),
                      w_mkv_g.reshape(D_MODEL, 2 * C_W))
    loss = lax.psum(loc["loss"], ("x", "y", "c"))

    big = [loc["gw_in"], loc["gw_out"].reshape(N_CHIPS, D_MODEL // N_CHIPS, D_MODEL),
           loc["gw_mkv"].reshape(N_CHIPS, D_MODEL // N_CHIPS, 2 * C_W)]
    small = jnp.concatenate([loc["gpre"], loc["gpost"], loc["gmem"],
                             jnp.pad(loc["gsink"], (0, D_MODEL - loc["gsink"].shape[0])).reshape(1, D_MODEL),
                             jnp.zeros((4, D_MODEL), F32)], axis=0)
    core = lax.axis_index("c").astype(jnp.int32).reshape(1)
    got = _pair_exchange(big)
    parts = [_pair_add(core, own, g) for own, g in zip(big, got)]
    *slots, small_slots = _chip_exchange(parts, small)
    halves = [_slot_sum(s, name=f"chip_sum_{s.shape[2]}", core=core) for s in slots]
    g_in, g_out, g_mkv = _pair_gather(halves)
    small_sum = _slot_sum(small_slots, name="device_sum")
    g_pre, g_post, g_mem = small_sum[0:1], small_sum[1:2], small_sum[2:3]
    g_sink = small_sum[3:4, :sink_a.shape[1]]

    d_in, nm_in, nv_in = _adamw(w_in[0], g_in, m_w_in[0], v_w_in[0], "adamw_in")
    d_out, nm_out, nv_out = _adamw(w_out[0], g_out, m_w_out[0], v_w_out[0], "adamw_out")
    d_mkv, nm_mkv, nv_mkv = _adamw(w_mem_kv[0], g_mkv, m_w_mem_kv[0], v_w_mem_kv[0], "adamw_mkv")
    pad6 = lambda a: jnp.pad(a, ((0, 0), (0, D_MODEL - a.shape[1])))
    stack = lambda a, b, c_, d_: jnp.concatenate([a, b, c_, pad6(d_), jnp.zeros((4, D_MODEL), F32)], axis=0)
    d_s, nm_s, nv_s = _adamw(stack(pre_norm, post_norm, mem_norm, sink_a), small_sum,
                             stack(m_pre_norm, m_post_norm, m_mem_norm, m_sink_a),
                             stack(v_pre_norm, v_post_norm, v_mem_norm, v_sink_a), "adamw_small")
    ns_ = sink_a.shape[1]
    unpack = lambda a: (a[0:1], a[3:4, :ns_], a[2:3], a[1:2])
    d_pre, d_sink, d_mem, d_post = unpack(d_s)
    nm_pre, nm_sink, nm_mem, nm_post = unpack(nm_s)
    nv_pre, nv_sink, nv_mem, nv_post = unpack(nv_s)
    lead = lambda a: a[None]
    return (loss, lead(loc["grad_x"]),
            g_pre, lead(g_in), g_sink, g_mem, lead(g_mkv), lead(g_out), g_post,
            d_pre, lead(d_in), d_sink, d_mem, lead(d_mkv), lead(d_out), d_post,
            nm_pre, lead(nm_in), nm_sink, nm_mem, lead(nm_mkv), lead(nm_out), nm_post,
            nv_pre, lead(nv_in), nv_sink, nv_mem, lead(nv_mkv), lead(nv_out), nv_post)
```

```python
import numpy as np
import jax
import jax.numpy as jnp
from jax import lax
from jax.experimental import pallas as pl
from jax.experimental.pallas import tpu as pltpu

F32 = jnp.float32
BF16 = jnp.bfloat16

D_MODEL = 1024
HEAD_DIM = 64
LANES = 128
BLOCK = 128
A_W, A_KV_W, B_W, C_W = 384, 128, 384, 256
N_MEM = 256
D_IN = 3072
N_CHIPS = 4
SHARD_IN = D_IN // N_CHIPS
B_CONFIGS = ((128, 1), (512, 4), (2048, 16))
B_DILS = tuple(d for _, d in B_CONFIGS)
A_WINDOW = 128
RMS_EPS = 1e-6
ROPE_THETA = 500000.0
SCALE = HEAD_DIM ** -0.5
NEG = -1e30
ADAM_LR, ADAM_B1, ADAM_B2, ADAM_EPS, ADAM_WD, ADAM_STEP = 0.001, 0.9, 0.999, 1e-08, 0.01, 10

NT = (((1,), (1,)), ((), ()))
TN = (((0,), (0,)), ((), ()))
MESH = pl.DeviceIdType.MESH

_PROJ_LAYOUT = (
    [("qa", 128 * i, True, True) for i in range(3)] + [("ka", 0, True, False), ("va", 0, False, False)]
    + [("ga", 128 * i, False, False) for i in range(3)]
    + [("qb", 128 * i, True, True) for i in range(3)] + [("kb", 128 * i, True, False) for i in range(3)]
    + [("vb", 128 * i, False, False) for i in range(3)] + [("gb", 128 * i, False, False) for i in range(3)]
    + [("qc", 128 * i, False, True) for i in range(2)] + [("gc", 128 * i, False, False) for i in range(2)]
)
_PROJ_WIDTH = dict(qa=A_W, ka=A_KV_W, va=A_KV_W, ga=A_W, qb=B_W, kb=B_W, vb=B_W, gb=B_W, qc=C_W, gc=C_W)
_NATURAL = ("qa", "ka", "va", "ga", "gb", "qc", "gc")
_DILATED = ("qb", "kb", "vb")


def _dot(a, b):
    return jnp.dot(a, b, preferred_element_type=F32)


def _dot_nt(a, b):
    return lax.dot_general(a, b, NT, preferred_element_type=F32)


def _dot_tn(a, b):
    return lax.dot_general(a, b, TN, preferred_element_type=F32)


def _half_masks(rows):
    lane = lax.broadcasted_iota(jnp.int32, (rows, LANES), 1)
    return lane < HEAD_DIM, lane >= HEAD_DIM


def _rope(t, c, sm, sp):
    return t * c + pltpu.roll(t, LANES - 8, 1) * sm + pltpu.roll(t, 8, 1) * sp


def _rope_tables(seq, tm):
    dim = jnp.arange(LANES) % HEAD_DIM
    inv_freq = ROPE_THETA ** (-jnp.arange(0, 16, 2, dtype=F32) / 16)
    freq = jnp.where(dim < 16, inv_freq[dim % 8], 0.0)[None, :]
    local = jnp.arange(tm, dtype=F32)[:, None] * freq
    base = (jnp.arange(seq // tm, dtype=F32) * tm)[:, None] * freq
    both = lambda a: jnp.concatenate([jnp.cos(a), jnp.sin(a)], axis=1)
    return both(local), jnp.repeat(both(base), 8, axis=0)


def _rope_coeffs(local_ref, base_ref):
    cl, sl = local_ref[:, :LANES], local_ref[:, LANES:]
    cb, sb = base_ref[0:1, :LANES], base_ref[0:1, LANES:]
    cos = cb * cl - sb * sl
    sin = sb * cl + cb * sl
    dim = lax.broadcasted_iota(jnp.int32, (1, LANES), 1) % HEAD_DIM
    return cos, jnp.where(dim < 8, -sin, 0.0), jnp.where((dim >= 8) & (dim < 16), sin, 0.0)


def _split3(x):
    a = x.astype(BF16)
    r = x - a.astype(F32)
    b = r.astype(BF16)
    c = (r - b.astype(F32)).astype(BF16)
    return a, b, c


def _rows_to_lanes(x):
    row = lax.broadcasted_iota(jnp.int32, (8, LANES), 0)
    lane = lax.broadcasted_iota(jnp.int32, (8, LANES), 1)
    eye = (row == lane).astype(BF16)
    a, b, c = _split3(x)
    return _dot_nt(eye, a) + _dot_nt(eye, b) + _dot_nt(eye, c)


def _head_sum_matrix(width):
    k = lax.broadcasted_iota(jnp.int32, (width, LANES), 0)
    h = lax.broadcasted_iota(jnp.int32, (width, LANES), 1)
    return (k // HEAD_DIM == h).astype(BF16)


def _head_expand_matrix(width):
    h = lax.broadcasted_iota(jnp.int32, (LANES, width), 0)
    k = lax.broadcasted_iota(jnp.int32, (LANES, width), 1)
    return (k // HEAD_DIM == h).astype(BF16)


def _dot_split(x, mat, terms):
    parts = _split3(x)[:terms]
    out = _dot(parts[0], mat)
    for p in parts[1:]:
        out = out + _dot(p, mat)
    return out


def _per_head(cols, fill=0.0):
    rows = cols[0].shape[0]
    lane = lax.broadcasted_iota(jnp.int32, (rows, LANES), 1)
    out = jnp.full((rows, LANES), fill, F32)
    for h, col in enumerate(cols):
        out = jnp.where(lane == h, col, out)
    return out


def _lane_blocks(width):
    return [slice(p * LANES, (p + 1) * LANES) for p in range(width // LANES)]


def _stage(rows, width):
    return pltpu.VMEM((width // LANES, rows, LANES), F32)


def _stage_write(buf, value):
    for p, lanes in enumerate(_lane_blocks(value.shape[1])):
        buf[p] = value[:, lanes]


def _stage_read(buf):
    return jnp.concatenate([buf[p] for p in range(buf.shape[0])], axis=1) if buf.shape[0] > 1 else buf[0]


def _to_residues(buf, out_ref, dil):
    rows = buf.shape[1] // dil
    for r in range(dil):
        for p in range(buf.shape[0]):
            plane = buf.at[p]
            out_ref[r, :, p * LANES:(p + 1) * LANES] = plane[pl.ds(r, rows, stride=dil), :].astype(out_ref.dtype)


def _from_residues(in_ref, buf, dil):
    rows = buf.shape[1] // dil
    for r in range(dil):
        for p in range(buf.shape[0]):
            plane = buf.at[p]
            plane[pl.ds(r, rows, stride=dil), :] = in_ref[r, :, p * LANES:(p + 1) * LANES].astype(F32)


def _residue_spec(dil, tm, width):
    return pl.BlockSpec((dil, tm // dil, width), lambda i: (0, i, 0))


def _gather_weights(w_in_s, w_out_s, w_mkv_s):
    shards = tuple(s.reshape(2, s.shape[0] // 2, s.shape[1]) for s in (w_in_s, w_out_s, w_mkv_s))
    n = len(shards)

    def body(*refs):
        srcs, outs = refs[:n], refs[2 * n:3 * n]
        send_sems, recv_sems = refs[3 * n:]
        x, y, c = lax.axis_index("x"), lax.axis_index("y"), lax.axis_index("c")
        my_chip = 2 * x + y
        sibling = (x, y, 1 - c)
        chips = [(1 - x, y), (x, 1 - y), (1 - x, 1 - y)]

        def half(t, chip, which):
            return outs[t].at[chip, which]

        def src_half(t, which):
            return srcs[t].at[which]

        def copy(k, src, dst, to):
            return pltpu.make_async_remote_copy(src_ref=src, dst_ref=dst, send_sem=send_sems.at[k],
                                                recv_sem=recv_sems.at[k], device_id=to, device_id_type=MESH)

        first = []
        for j, (cx, cy) in enumerate(chips):
            for t in range(n):
                first.append(copy(n * j + t, src_half(t, c), half(t, my_chip, c), (cx, cy, c)))
        for cp in first:
            cp.start()
        passed = []
        for j, (cx, cy) in enumerate(chips):
            chip = 2 * cx + cy
            for t in range(n):
                k = n * j + t
                copy(k, src_half(t, c), half(t, chip, c), (cx, cy, c)).wait_recv()
                fwd = copy(n * 3 + k, half(t, chip, c), half(t, chip, c), sibling)
                fwd.start()
                passed.append(fwd)
        for j, (cx, cy) in enumerate(chips):
            chip = 2 * cx + cy
            for t in range(n):
                k = n * 3 + n * j + t
                copy(k, half(t, chip, 1 - c), half(t, chip, 1 - c), sibling).wait_recv()
        for cp in first + passed:
            cp.wait_send()

    my_chip = 2 * lax.axis_index("x") + lax.axis_index("y")
    landing = [lax.dynamic_update_slice(jnp.zeros((N_CHIPS,) + s.shape, s.dtype), s[None], (my_chip, 0, 0, 0))
               for s in shards]
    any_spec = pl.BlockSpec(memory_space=pl.ANY)
    return pl.pallas_call(
        body, name="gather_weights",
        out_shape=[jax.ShapeDtypeStruct((N_CHIPS,) + s.shape, s.dtype) for s in shards],
        in_specs=[any_spec] * (2 * n), out_specs=[any_spec] * n,
        input_output_aliases={n + t: t for t in range(n)},
        scratch_shapes=[pltpu.SemaphoreType.DMA((6 * n,)), pltpu.SemaphoreType.DMA((6 * n,))],
    )(*shards, *landing)


def _mem_kv(mem, mem_norm, w_mkv):
    def body(mem_ref, g_ref, w_ref, mk_ref, mv_ref):
        m = mem_ref[...]
        r = lax.rsqrt(jnp.mean(m * m, axis=-1, keepdims=True) + RMS_EPS)
        mn = (m * r * g_ref[...]).astype(BF16)
        kv = _dot(mn, w_ref[...])
        mk_ref[...] = kv[:, :C_W].astype(BF16)
        mv_ref[...] = kv[:, C_W:].astype(BF16)

    return pl.pallas_call(
        body, name="mem_kv",
        out_shape=[jax.ShapeDtypeStruct((N_MEM, C_W), BF16)] * 2,
    )(mem, mem_norm, w_mkv)


def _mem_kv_bwd(mem, mem_norm, w_mkv, dmk, dmv):
    def body(mem_ref, g_ref, w_ref, dmk_ref, dmv_ref, gw_ref, gn_ref):
        m = mem_ref[...]
        r = lax.rsqrt(jnp.mean(m * m, axis=-1, keepdims=True) + RMS_EPS)
        mhat = m * r
        mn = (mhat * g_ref[...]).astype(BF16)
        dkv = jnp.concatenate([dmk_ref[...], dmv_ref[...]], axis=1).astype(BF16)
        gw_ref[...] = _dot_tn(mn, dkv)
        dmn = _dot_nt(dkv, w_ref[...])
        gn_ref[...] = jnp.sum(dmn * mhat, axis=0, keepdims=True)

    return pl.pallas_call(
        body, name="mem_kv_bwd",
        out_shape=[jax.ShapeDtypeStruct((D_MODEL, 2 * C_W), F32), jax.ShapeDtypeStruct((1, D_MODEL), F32)],
    )(mem, mem_norm, w_mkv, dmk, dmv)


def _pre_proj(x, pre_norm, w_in_g):
    seq = x.shape[0]
    tm = min(512, seq)
    n_nat, n_dil = len(_NATURAL), len(_DILATED) * len(B_DILS)
    rope = _rope_tables(seq, tm)

    def body(x_ref, g_ref, w_ref, rl_ref, rb_ref, *refs):
        nat = dict(zip(_NATURAL, refs[:n_nat]))
        res = {n: refs[n_nat + len(B_DILS) * k:n_nat + len(B_DILS) * (k + 1)] for k, n in enumerate(_DILATED)}
        ut = refs[n_nat + n_dil]
        bufs = dict(zip(_DILATED, refs[n_nat + n_dil + 1:]))
        xv = x_ref[...]
        r = lax.rsqrt(jnp.mean(xv * xv, axis=-1, keepdims=True) + RMS_EPS)
        u = xv * r * g_ref[...]
        ub = u.astype(BF16)
        ut[...] = u.T.astype(BF16)
        c, sm, sp = _rope_coeffs(rl_ref, rb_ref)
        for j in range(N_CHIPS):
            pj = _dot(ub, w_ref[j])
            for b in range(SHARD_IN // LANES):
                name, off, roped, scaled = _PROJ_LAYOUT[(SHARD_IN // LANES) * j + b]
                piece = pj[:, LANES * b:LANES * (b + 1)]
                if roped:
                    piece = _rope(piece, c, sm, sp)
                if scaled:
                    piece = piece * SCALE
                if name in bufs:
                    bufs[name][off // LANES] = piece
                else:
                    nat[name][:, off:off + LANES] = piece.astype(BF16)
        for name in _DILATED:
            for ref, dil in zip(res[name], B_DILS):
                _to_residues(bufs[name], ref, dil)

    row = lambda w: pl.BlockSpec((tm, w), lambda i: (i, 0))
    full = lambda a: pl.BlockSpec(a.shape, lambda i: (0,) * a.ndim)
    out_shape = [jax.ShapeDtypeStruct((seq, _PROJ_WIDTH[n]), BF16) for n in _NATURAL]
    out_specs = [row(_PROJ_WIDTH[n]) for n in _NATURAL]
    for n in _DILATED:
        for dil in B_DILS:
            out_shape.append(jax.ShapeDtypeStruct((dil, seq // dil, B_W), BF16))
            out_specs.append(_residue_spec(dil, tm, B_W))
    out_shape.append(jax.ShapeDtypeStruct((D_MODEL, seq), BF16))
    out_specs.append(pl.BlockSpec((D_MODEL, tm), lambda i: (0, i)))
    res = pl.pallas_call(
        body, name="pre_proj", grid=(seq // tm,),
        in_specs=[row(D_MODEL), full(pre_norm), full(w_in_g), full(rope[0]), pl.BlockSpec((8, 2 * LANES), lambda i: (i, 0))],
        out_specs=out_specs, out_shape=out_shape,
        scratch_shapes=[_stage(tm, B_W)] * len(_DILATED),
    )(x, pre_norm, w_in_g, *rope)
    out = dict(zip(_NATURAL, res[:n_nat]))
    for k, n in enumerate(_DILATED):
        out[n] = res[n_nat + len(B_DILS) * k:n_nat + len(B_DILS) * (k + 1)]
    out["ut"] = res[n_nat + n_dil]
    return out


def _band_bias(max_dist, transposed):
    i = np.arange(BLOCK)[:, None]
    j = np.arange(BLOCK)[None, :]
    if transposed:
        same = i <= j
        other = (j + BLOCK - i) <= max_dist
        vis = np.concatenate([same, other], axis=1)
    else:
        prev = (i + BLOCK - j) <= max_dist
        same = j <= i
        vis = np.concatenate([prev, same], axis=1)
    return jnp.asarray(np.where(vis, 0.0, NEG).astype(np.float32))


def _kv_place(h, gqa):
    return (0, h // 3) if gqa else (h // 2, h % 2)


def _band_fwd(q, k, v, sink, *, max_dist, name):
    dil, length, wq = q.shape
    wk = k.shape[2]
    gqa = wk != wq
    tq = min(512, length)
    ns, nt = tq // BLOCK, length // tq
    npair = wq // LANES
    bias = _band_bias(max_dist, transposed=False)
    has_sink = sink is not None

    def body(*refs):
        if has_sink:
            sink_ref, refs = refs[0], refs[1:]
        q_ref, k_ref, kp_ref, v_ref, vp_ref, bias_ref, o_ref, lse_ref, kbuf, vbuf = refs[:10]
        i = pl.program_id(1)
        kbuf[0:BLOCK] = kp_ref[...]
        kbuf[BLOCK:] = k_ref[...]
        vbuf[0:BLOCK] = vp_ref[...]
        vbuf[BLOCK:] = v_ref[...]
        if gqa:
            kroll, vroll = refs[10:12]
            kroll[...] = pltpu.roll(kbuf[...], HEAD_DIM, 1)
            vroll[...] = pltpu.roll(vbuf[...], HEAD_DIM, 1)
        half = _half_masks(BLOCK)
        col_prev = (lax.broadcasted_iota(jnp.int32, (1, 2 * BLOCK), 1) < BLOCK).astype(F32)

        def sub(a, carry):
            r0 = pl.multiple_of(a * BLOCK, BLOCK)
            pen = jnp.where((i == 0) & (a == 0), NEG, 0.0)
            b = bias_ref[...] + pen * col_prev
            scores = []
            for p in range(npair):
                qp = q_ref[pl.ds(r0, BLOCK), p * LANES:(p + 1) * LANES]
                for e in range(2):
                    pk, ek = _kv_place(2 * p + e, gqa)
                    kw = (kbuf if ek == e else kroll)[pl.ds(r0, 2 * BLOCK), pk * LANES:(pk + 1) * LANES]
                    scores.append(_dot_nt(jnp.where(half[e], qp, jnp.zeros_like(qp)), kw))
            m_cols, l_cols, probs = [], [], []
            for h, s in enumerate(scores):
                s = s + b
                m = jnp.max(s, axis=1, keepdims=True)
                if has_sink:
                    m = jnp.maximum(m, sink_ref[h])
                pe = jnp.exp(s - m)
                l = jnp.sum(pe, axis=1, keepdims=True)
                if has_sink:
                    l = l + jnp.exp(sink_ref[h] - m)
                probs.append(pe.astype(BF16))
                m_cols.append(m)
                l_cols.append(l)
            for p in range(npair):
                o_h = []
                for e in range(2):
                    h = 2 * p + e
                    pk, ek = _kv_place(h, gqa)
                    vw = (vbuf if ek == e else vroll)[pl.ds(r0, 2 * BLOCK), pk * LANES:(pk + 1) * LANES]
                    o_h.append(_dot(probs[h], vw) * (1.0 / l_cols[h]))
                o_ref[pl.ds(r0, BLOCK), p * LANES:(p + 1) * LANES] = jnp.where(half[0], o_h[0], o_h[1]).astype(BF16)
            lse_ref[pl.ds(r0, BLOCK), :] = _per_head(m_cols) + jnp.log(_per_head(l_cols, 1.0))
            return carry

        lax.fori_loop(0, ns, sub, 0, unroll=True)

    main = lambda w: pl.BlockSpec((None, tq, w), lambda r, i: (r, i, 0))
    prev = lambda w: pl.BlockSpec((None, BLOCK, w), lambda r, i: (r, jnp.maximum(i * ns - 1, 0), 0))
    in_specs = [main(wq), main(wk), prev(wk), main(wk), prev(wk), pl.BlockSpec(bias.shape, lambda r, i: (0, 0))]
    args = [q, k, k, v, v, bias]
    if has_sink:
        in_specs = [pl.BlockSpec(memory_space=pltpu.SMEM)] + in_specs
        args = [sink] + args
    scratch = [pltpu.VMEM((tq + BLOCK, wk), BF16)] * (4 if gqa else 2)
    return pl.pallas_call(
        body, name=name, grid=(dil, nt), in_specs=in_specs,
        out_specs=[main(wq), main(LANES)],
        out_shape=[jax.ShapeDtypeStruct((dil, length, wq), BF16), jax.ShapeDtypeStruct((dil, length, LANES), F32)],
        scratch_shapes=scratch,
    )(*args)


def _band_bwd(q, k, v, do, lse, delta, *, max_dist, name):
    dil, length, wq = q.shape
    wk = k.shape[2]
    gqa = wk != wq
    tq = min(512, length)
    ns, nt = tq // BLOCK, length // tq
    npair = wq // LANES
    nblocks = length // BLOCK
    bias = _band_bias(max_dist, transposed=True)

    def body(q_ref, qn_ref, do_ref, don_ref, lse_ref, lsen_ref, dl_ref, dln_ref, k_ref, v_ref, bias_ref,
             dq_ref, dk_ref, dv_ref, qbuf, dobuf, stat_l, stat_d, dqt, kt, *rolled):
        i = pl.program_id(1)
        qbuf[0:tq] = q_ref[...]
        qbuf[tq:] = qn_ref[...]
        dobuf[0:tq] = do_ref[...]
        dobuf[tq:] = don_ref[...]
        for pk in range(wk // LANES):
            kt[pk] = k_ref[:, pk * LANES:(pk + 1) * LANES].astype(F32).T.astype(BF16)
        if gqa:
            kroll, vroll, ktroll = rolled
            kroll[...] = pltpu.roll(k_ref[...], HEAD_DIM, 1)
            vroll[...] = pltpu.roll(v_ref[...], HEAD_DIM, 1)
            ktroll[0] = kroll[...].astype(F32).T.astype(BF16)
        for a in range(ns):
            rows = slice(a * BLOCK, (a + 1) * BLOCK)
            stat_l[a] = _rows_to_lanes(lse_ref[rows, :])
            stat_d[a] = _rows_to_lanes(dl_ref[rows, :])
        stat_l[ns] = _rows_to_lanes(lsen_ref[...])
        stat_d[ns] = _rows_to_lanes(dln_ref[...])

        @pl.when(i == 0)
        def _():
            dqt[:, :, 0:BLOCK] = jnp.zeros((npair, LANES, BLOCK), F32)

        @pl.when(i > 0)
        def _():
            dqt[:, :, 0:BLOCK] = dqt[:, :, tq:tq + BLOCK]

        dqt[:, :, BLOCK:] = jnp.zeros((npair, LANES, tq), F32)
        half2 = _half_masks(2 * BLOCK)
        row = lax.broadcasted_iota(jnp.int32, (LANES, BLOCK), 0)
        row_half = (row < HEAD_DIM, row >= HEAD_DIM)
        col_next = (lax.broadcasted_iota(jnp.int32, (1, 2 * BLOCK), 1) >= BLOCK).astype(F32)

        for b in range(ns):
            rows = slice(b * BLOCK, (b + 1) * BLOCK)
            window = slice(b * BLOCK, (b + 2) * BLOCK)
            bt = bias_ref[...]
            if b == ns - 1:
                bt = bt + jnp.where(i == nt - 1, NEG, 0.0) * col_next
            acc = {}
            items = []
            for p in range(npair):
                lanes = slice(p * LANES, (p + 1) * LANES)
                qw = qbuf[window, lanes]
                dow = dobuf[window, lanes]
                for e in range(2):
                    h = 2 * p + e
                    pk, ek = _kv_place(h, gqa)
                    klanes = slice(pk * LANES, (pk + 1) * LANES)
                    kb = (k_ref if ek == e else kroll)[rows, klanes]
                    vb = (v_ref if ek == e else vroll)[rows, klanes]
                    qm = jnp.where(half2[e], qw, jnp.zeros_like(qw))
                    dom = jnp.where(half2[e], dow, jnp.zeros_like(dow))
                    items.append(dict(p=p, e=e, h=h, pk=pk, ek=ek, qm=qm, dom=dom,
                                      st=_dot_nt(kb, qm), dpt=_dot_nt(vb, dom)))
            for it in items:
                h = it["h"]
                lrow = jnp.concatenate([stat_l[b, h:h + 1, :], stat_l[b + 1, h:h + 1, :]], axis=1)
                drow = jnp.concatenate([stat_d[b, h:h + 1, :], stat_d[b + 1, h:h + 1, :]], axis=1)
                pt = jnp.exp(it["st"] + bt - lrow)
                it["ptb"] = pt.astype(BF16)
                it["dsb"] = (pt * (it["dpt"] - drow)).astype(BF16)
            for it in items:
                p, e, pk, ek = it["p"], it["e"], it["pk"], it["ek"]
                dv_c = _dot(it["ptb"], it["dom"])
                dk_c = _dot(it["dsb"], it["qm"])
                kbt = (kt if ek == e else ktroll)[pk, :, rows]
                kbtm = jnp.where(row_half[e], kbt, jnp.zeros_like(kbt))
                dqt[p, :, window] += _dot(kbtm, it["dsb"])
                key = (pk, ek == e)
                if key in acc:
                    acc[key] = (acc[key][0] + dk_c, acc[key][1] + dv_c)
                else:
                    acc[key] = (dk_c, dv_c)
            if not gqa:
                for p in range(npair):
                    lanes = slice(p * LANES, (p + 1) * LANES)
                    dk_ref[rows, lanes] = acc[(p, True)][0].astype(BF16)
                    dv_ref[rows, lanes] = acc[(p, True)][1].astype(BF16)
            if gqa:
                dk_al, dv_al = acc[(0, True)]
                dk_mis, dv_mis = acc[(0, False)]
                dk_ref[rows, :] = (dk_al + pltpu.roll(dk_mis, HEAD_DIM, 1)).astype(BF16)
                dv_ref[rows, :] = (dv_al + pltpu.roll(dv_mis, HEAD_DIM, 1)).astype(BF16)

        for p in range(npair):
            dq_ref[:, p * LANES:(p + 1) * LANES] = dqt[p, :, 0:tq].T.astype(BF16)

    main = lambda w: pl.BlockSpec((None, tq, w), lambda r, i: (r, i, 0))
    nxt = lambda w: pl.BlockSpec((None, BLOCK, w), lambda r, i: (r, jnp.minimum((i + 1) * ns, nblocks - 1), 0))
    scratch = [pltpu.VMEM((tq + BLOCK, wq), BF16), pltpu.VMEM((tq + BLOCK, wq), BF16),
               pltpu.VMEM((ns + 1, 8, LANES), F32), pltpu.VMEM((ns + 1, 8, LANES), F32),
               pltpu.VMEM((npair, LANES, tq + BLOCK), F32), pltpu.VMEM((wk // LANES, LANES, tq), BF16)]
    if gqa:
        scratch = scratch + [pltpu.VMEM((tq, wk), BF16)] * 2 + [pltpu.VMEM((1, LANES, tq), BF16)]
    return pl.pallas_call(
        body, name=name, grid=(dil, nt),
        in_specs=[main(wq), nxt(wq), main(wq), nxt(wq), main(LANES), nxt(LANES), main(LANES), nxt(LANES),
                  main(wk), main(wk), pl.BlockSpec(bias.shape, lambda r, i: (0, 0))],
        out_specs=[main(wq), main(wk), main(wk)],
        out_shape=[jax.ShapeDtypeStruct((dil, length, wq), BF16), jax.ShapeDtypeStruct((dil, length, wk), BF16),
                   jax.ShapeDtypeStruct((dil, length, wk), BF16)],
        scratch_shapes=scratch,
        compiler_params=pltpu.CompilerParams(dimension_semantics=("arbitrary", "arbitrary")),
    )(q, q, do, do, lse, lse, delta, delta, k, v, bias)


def _mem_attn_fwd(q, mk, mv):
    seq = q.shape[0]
    tq = min(512, seq)
    ns = tq // BLOCK

    def body(q_ref, mk_ref, mv_ref, o_ref, lse_ref):
        half = _half_masks(BLOCK)

        def sub(a, carry):
            r0 = pl.multiple_of(a * BLOCK, BLOCK)
            scores = []
            for p in range(C_W // LANES):
                lanes = slice(p * LANES, (p + 1) * LANES)
                qp = q_ref[pl.ds(r0, BLOCK), lanes]
                for e in range(2):
                    scores.append(_dot_nt(jnp.where(half[e], qp, jnp.zeros_like(qp)), mk_ref[:, lanes]))
            m_cols, l_cols, probs = [], [], []
            for s in scores:
                m = jnp.max(s, axis=1, keepdims=True)
                pe = jnp.exp(s - m)
                probs.append(pe.astype(BF16))
                m_cols.append(m)
                l_cols.append(jnp.sum(pe, axis=1, keepdims=True))
            for p in range(C_W // LANES):
                lanes = slice(p * LANES, (p + 1) * LANES)
                o_h = [_dot(probs[2 * p + e], mv_ref[:, lanes]) * (1.0 / l_cols[2 * p + e]) for e in range(2)]
                o_ref[pl.ds(r0, BLOCK), lanes] = jnp.where(half[0], o_h[0], o_h[1]).astype(BF16)
            lse_ref[pl.ds(r0, BLOCK), :] = _per_head(m_cols) + jnp.log(_per_head(l_cols, 1.0))
            return carry

        lax.fori_loop(0, ns, sub, 0, unroll=True)

    row = lambda w: pl.BlockSpec((tq, w), lambda i: (i, 0))
    full = pl.BlockSpec((N_MEM, C_W), lambda i: (0, 0))
    return pl.pallas_call(
        body, name="mem_attn_fwd", grid=(seq // tq,), in_specs=[row(C_W), full, full],
        out_specs=[row(C_W), row(LANES)],
        out_shape=[jax.ShapeDtypeStruct((seq, C_W), BF16), jax.ShapeDtypeStruct((seq, LANES), F32)],
    )(q, mk, mv)


def _mem_attn_bwd(q, mk, mv, do, lse, delta):
    seq = q.shape[0]
    tq = min(512, seq)
    ns = tq // BLOCK
    npair = C_W // LANES

    def body(q_ref, mk_ref, mv_ref, do_ref, lse_ref, dl_ref, dq_ref, dmk_ref, dmv_ref, stat_l, stat_d, mkt, dqt):
        @pl.when(pl.program_id(0) == 0)
        def _():
            dmk_ref[...] = jnp.zeros_like(dmk_ref)
            dmv_ref[...] = jnp.zeros_like(dmv_ref)
            for p in range(npair):
                mkt[p] = mk_ref[:, p * LANES:(p + 1) * LANES].astype(F32).T.astype(BF16)

        for a in range(ns):
            rows = slice(a * BLOCK, (a + 1) * BLOCK)
            stat_l[a] = _rows_to_lanes(lse_ref[rows, :])
            stat_d[a] = _rows_to_lanes(dl_ref[rows, :])
        half = _half_masks(BLOCK)
        row = lax.broadcasted_iota(jnp.int32, (LANES, N_MEM), 0)
        row_half = (row < HEAD_DIM, row >= HEAD_DIM)

        for a in range(ns):
            rows = slice(a * BLOCK, (a + 1) * BLOCK)
            items = []
            for p in range(npair):
                lanes = slice(p * LANES, (p + 1) * LANES)
                qp = q_ref[rows, lanes]
                dop = do_ref[rows, lanes]
                for e in range(2):
                    qm = jnp.where(half[e], qp, jnp.zeros_like(qp))
                    dom = jnp.where(half[e], dop, jnp.zeros_like(dop))
                    items.append(dict(p=p, e=e, qm=qm, dom=dom, st=_dot_nt(mk_ref[:, lanes], qm),
                                      dpt=_dot_nt(mv_ref[:, lanes], dom)))
            for it in items:
                h = 2 * it["p"] + it["e"]
                pt = jnp.exp(it["st"] - stat_l[a, h:h + 1, :])
                it["ptb"] = pt.astype(BF16)
                it["dsb"] = (pt * (it["dpt"] - stat_d[a, h:h + 1, :])).astype(BF16)
            for p in range(npair):
                lanes = slice(p * LANES, (p + 1) * LANES)
                pair = [it for it in items if it["p"] == p]
                dmv_ref[:, lanes] += _dot(pair[0]["ptb"], pair[0]["dom"]) + _dot(pair[1]["ptb"], pair[1]["dom"])
                dmk_ref[:, lanes] += _dot(pair[0]["dsb"], pair[0]["qm"]) + _dot(pair[1]["dsb"], pair[1]["qm"])
                kbt = mkt[p]
                dqt[p, :, rows] = (_dot(jnp.where(row_half[0], kbt, jnp.zeros_like(kbt)), pair[0]["dsb"])
                                   + _dot(jnp.where(row_half[1], kbt, jnp.zeros_like(kbt)), pair[1]["dsb"]))
        for p in range(npair):
            dq_ref[:, p * LANES:(p + 1) * LANES] = dqt[p].T.astype(BF16)

    row = lambda w: pl.BlockSpec((tq, w), lambda i: (i, 0))
    full = pl.BlockSpec((N_MEM, C_W), lambda i: (0, 0))
    return pl.pallas_call(
        body, name="mem_attn_bwd", grid=(seq // tq,),
        in_specs=[row(C_W), full, full, row(C_W), row(LANES), row(LANES)], out_specs=[row(C_W), full, full],
        out_shape=[jax.ShapeDtypeStruct((seq, C_W), BF16), jax.ShapeDtypeStruct((N_MEM, C_W), F32),
                   jax.ShapeDtypeStruct((N_MEM, C_W), F32)],
        scratch_shapes=[pltpu.VMEM((ns, 8, LANES), F32)] * 2
        + [pltpu.VMEM((npair, LANES, N_MEM), BF16), pltpu.VMEM((npair, LANES, tq), F32)],
        compiler_params=pltpu.CompilerParams(dimension_semantics=("arbitrary",)),
    )(q, mk, mv, do, lse, delta)


def _silu_and_grad(g):
    s = 1.0 / (1.0 + jnp.exp(-g))
    return g * s, s * (1.0 + g * (1.0 - s))


def _post(x, target, post_norm, w_out, sink_row, oa, lse_a, ga, ob_list, lseb_list, gb, oc, gc):
    seq = x.shape[0]
    tm = min(256, seq)
    inv_d = 1.0 / D_MODEL
    nd = len(B_DILS)

    def body(*refs):
        (x_ref, t_ref, gp_ref, w_ref, sink_ref, oa_ref, lsea_ref, ga_ref), refs = refs[:8], refs[8:]
        ob_refs, lb_refs, (gb_ref, oc_ref, gc_ref), refs = refs[:nd], refs[nd:2 * nd], refs[2 * nd:2 * nd + 3], refs[2 * nd + 3:]
        (g_ref, doa_ref, dla_ref, dga_ref), refs = refs[:4], refs[4:]
        dob_refs, lsec_refs, dlb_refs, refs = refs[:nd], refs[nd:2 * nd], refs[2 * nd:3 * nd], refs[3 * nd:]
        (dgb_ref, doc_ref, dlc_ref, dgc_ref, gw_ref, gpost_ref, gsink_ref, loss_ref), refs = refs[:8], refs[8:]
        ycat, obufs, lbufs, st_do, st_l, st_d = refs[0], refs[1:nd], refs[nd:2 * nd - 1], refs[2 * nd - 1], refs[2 * nd], refs[2 * nd + 1]
        obsc = refs[2 * nd + 2]

        @pl.when(pl.program_id(0) == 0)
        def _():
            gw_ref[...] = jnp.zeros_like(gw_ref)
            gpost_ref[...] = jnp.zeros_like(gpost_ref)
            gsink_ref[...] = jnp.zeros_like(gsink_ref)
            loss_ref[...] = jnp.zeros_like(loss_ref)

        o_i, l_i = [ob_refs[0][0].astype(F32)], [lb_refs[0][0]]
        for k in range(1, nd):
            _from_residues(ob_refs[k], obufs[k - 1], B_DILS[k])
            _from_residues(lb_refs[k], lbufs[k - 1], B_DILS[k])
            o_i.append(_stage_read(obufs[k - 1]))
            l_i.append(_stage_read(lbufs[k - 1]))
        mx = l_i[0]
        for l in l_i[1:]:
            mx = jnp.maximum(mx, l)
        w_i = [jnp.exp(l - mx) for l in l_i]
        z = w_i[0]
        for w in w_i[1:]:
            z = z + w
        _stage_write(st_l, mx + jnp.log(z))
        expand = _head_expand_matrix(B_W)
        inv_z = 1.0 / z
        ob = None
        for w, o in zip(w_i, o_i):
            term = _dot_split(w * inv_z, expand, 2) * o
            ob = term if ob is None else ob + term
        obsc[...] = ob
        ycat[:, 0:A_W] = (oa_ref[...].astype(F32) * _silu_and_grad(ga_ref[...].astype(F32))[0]).astype(BF16)
        ycat[:, A_W:A_W + B_W] = (ob * _silu_and_grad(gb_ref[...].astype(F32))[0]).astype(BF16)
        ycat[:, A_W + B_W:] = (oc_ref[...].astype(F32) * _silu_and_grad(gc_ref[...].astype(F32))[0]).astype(BF16)
        y2 = _dot(ycat[...], w_ref[...])
        r = lax.rsqrt(jnp.mean(y2 * y2, axis=-1, keepdims=True) + RMS_EPS)
        zhat = y2 * r
        gp = gp_ref[...]
        err = x_ref[...] + zhat * gp - t_ref[...]
        loss_ref[...] += jnp.sum(err * err) * (0.5 * inv_d)
        g = err * inv_d
        g_ref[...] = g
        gpost_ref[...] += jnp.sum(g * zhat, axis=0, keepdims=True)
        a = g * gp
        dy2 = (r * (a - zhat * jnp.mean(a * zhat, axis=-1, keepdims=True))).astype(BF16)
        gw_ref[...] += _dot_tn(ycat[...], dy2)
        dycat = _dot_nt(dy2, w_ref[...])
        oa = oa_ref[...].astype(F32)
        sa, dsa = _silu_and_grad(ga_ref[...].astype(F32))
        dya = dycat[:, 0:A_W]
        doa = dya * sa
        doa_ref[...] = doa.astype(BF16)
        dga_ref[...] = (dya * oa * dsa).astype(BF16)
        dl_a = _dot_split(doa * oa, _head_sum_matrix(A_W), 2)
        dla_ref[...] = dl_a
        gsink_ref[...] += jnp.sum(jnp.exp(sink_ref[...] - lsea_ref[...]) * dl_a, axis=0, keepdims=True)
        oc = oc_ref[...].astype(F32)
        sc, dsc = _silu_and_grad(gc_ref[...].astype(F32))
        dyc = dycat[:, A_W + B_W:]
        doc = dyc * sc
        doc_ref[...] = doc.astype(BF16)
        dgc_ref[...] = (dyc * oc * dsc).astype(BF16)
        dlc_ref[...] = _dot_split(doc * oc, _head_sum_matrix(C_W), 2)
        ob = obsc[...]
        sb, dsb = _silu_and_grad(gb_ref[...].astype(F32))
        dyb = dycat[:, A_W:A_W + B_W]
        dob = dyb * sb
        dgb_ref[...] = (dyb * ob * dsb).astype(BF16)
        _stage_write(st_do, dob)
        _stage_write(st_d, _dot_split(dob * ob, _head_sum_matrix(B_W), 2))
        for k, dil in enumerate(B_DILS):
            _to_residues(st_do, dob_refs[k], dil)
            _to_residues(st_l, lsec_refs[k], dil)
            _to_residues(st_d, dlb_refs[k], dil)

    row = lambda w: pl.BlockSpec((tm, w), lambda i: (i, 0))
    full = lambda shape: pl.BlockSpec(shape, lambda i: (0,) * len(shape))
    res_specs = lambda w: [_residue_spec(d, tm, w) for d in B_DILS]
    res_shapes = lambda w, dt: [jax.ShapeDtypeStruct((d, seq // d, w), dt) for d in B_DILS]
    ins = [x, target, post_norm, w_out, sink_row, oa, lse_a, ga, *ob_list, *lseb_list, gb, oc, gc]
    in_specs = ([row(D_MODEL), row(D_MODEL), full((1, D_MODEL)), full((D_MODEL, D_MODEL)), full((1, LANES)),
                 row(A_W), row(LANES), row(A_W)] + res_specs(B_W) + res_specs(LANES) + [row(B_W), row(C_W), row(C_W)])
    out_shape = ([jax.ShapeDtypeStruct((seq, D_MODEL), F32), jax.ShapeDtypeStruct((seq, A_W), BF16),
                  jax.ShapeDtypeStruct((seq, LANES), F32), jax.ShapeDtypeStruct((seq, A_W), BF16)]
                 + res_shapes(B_W, BF16) + res_shapes(LANES, F32) + res_shapes(LANES, F32)
                 + [jax.ShapeDtypeStruct((seq, B_W), BF16), jax.ShapeDtypeStruct((seq, C_W), BF16),
                    jax.ShapeDtypeStruct((seq, LANES), F32), jax.ShapeDtypeStruct((seq, C_W), BF16),
                    jax.ShapeDtypeStruct((D_MODEL, D_MODEL), F32), jax.ShapeDtypeStruct((1, D_MODEL), F32),
                    jax.ShapeDtypeStruct((1, LANES), F32), jax.ShapeDtypeStruct((1, LANES), F32)])
    out_specs = ([row(D_MODEL), row(A_W), row(LANES), row(A_W)] + res_specs(B_W) + res_specs(LANES) + res_specs(LANES)
                 + [row(B_W), row(C_W), row(LANES), row(C_W),
                    full((D_MODEL, D_MODEL)), full((1, D_MODEL)), full((1, LANES)), full((1, LANES))])
    scratch = ([pltpu.VMEM((tm, D_MODEL), BF16)] + [_stage(tm, B_W)] * (nd - 1) + [_stage(tm, LANES)] * (nd - 1)
               + [_stage(tm, B_W), _stage(tm, LANES), _stage(tm, LANES), pltpu.VMEM((tm, B_W), F32)])
    res = pl.pallas_call(
        body, name="post", grid=(seq // tm,), in_specs=in_specs, out_specs=out_specs, out_shape=out_shape,
        scratch_shapes=scratch,
        compiler_params=pltpu.CompilerParams(dimension_semantics=("arbitrary",)),
    )(*ins)
    out = dict(g=res[0], doa=res[1], dl_a=res[2], dga=res[3], dob=res[4:4 + nd], lse_b=res[4 + nd:4 + 2 * nd],
               dl_b=res[4 + 2 * nd:4 + 3 * nd])
    rest = res[4 + 3 * nd:]
    out.update(dgb=rest[0], doc=rest[1], dl_c=rest[2], dgc=rest[3], gw_out=rest[4], gpost=rest[5], gsink=rest[6],
               loss=rest[7])
    return out


def _dx(x, g, pre_norm, w_in_g, nat, res):
    seq = x.shape[0]
    tm = min(256, seq)
    nd = len(B_DILS)
    nat_list = [nat[n] for n in _NATURAL]
    res_list = [a for n in _DILATED for a in res[n]]
    rope = _rope_tables(seq, tm)

    def body(x_ref, g_ref, gp_ref, w_ref, rl_ref, rb_ref, *refs):
        nat_refs = dict(zip(_NATURAL, refs[:len(_NATURAL)]))
        refs = refs[len(_NATURAL):]
        res_refs = {n: refs[nd * k:nd * (k + 1)] for k, n in enumerate(_DILATED)}
        refs = refs[nd * len(_DILATED):]
        dproj_ref, gx_ref, gpre_ref = refs[:3]
        bufs = {n: refs[3 + (nd - 1) * k:3 + (nd - 1) * (k + 1)] for k, n in enumerate(_DILATED)}

        @pl.when(pl.program_id(0) == 0)
        def _():
            gpre_ref[...] = jnp.zeros_like(gpre_ref)

        for n in _DILATED:
            for k in range(1, nd):
                _from_residues(res_refs[n][k], bufs[n][k - 1], B_DILS[k])
        c, sm, sp = _rope_coeffs(rl_ref, rb_ref)
        sm, sp = -sm, -sp
        for blk, (name, off, roped, scaled) in enumerate(_PROJ_LAYOUT):
            lanes = slice(off, off + LANES)
            if name in nat_refs:
                piece = nat_refs[name][:, lanes].astype(F32)
            else:
                piece = res_refs[name][0][0, :, lanes].astype(F32)
                for buf in bufs[name]:
                    piece = piece + buf[off // LANES]
            if roped:
                piece = _rope(piece, c, sm, sp)
            if scaled:
                piece = piece * SCALE
            dproj_ref[:, blk * LANES:(blk + 1) * LANES] = piece.astype(BF16)
        du = None
        for j in range(N_CHIPS):
            part = _dot_nt(dproj_ref[:, j * SHARD_IN:(j + 1) * SHARD_IN], w_ref[j])
            du = part if du is None else du + part
        xv = x_ref[...]
        r = lax.rsqrt(jnp.mean(xv * xv, axis=-1, keepdims=True) + RMS_EPS)
        xhat = xv * r
        gpre_ref[...] += jnp.sum(du * xhat, axis=0, keepdims=True)
        a = du * gp_ref[...]
        gx_ref[...] = g_ref[...] + r * (a - xhat * jnp.mean(a * xhat, axis=-1, keepdims=True))

    row = lambda w: pl.BlockSpec((tm, w), lambda i: (i, 0))
    full = lambda a: pl.BlockSpec(a.shape, lambda i: (0,) * a.ndim)
    in_specs = ([row(D_MODEL), row(D_MODEL), full(pre_norm), full(w_in_g), full(rope[0]),
                 pl.BlockSpec((8, 2 * LANES), lambda i: (i, 0))]
                + [row(a.shape[1]) for a in nat_list]
                + [_residue_spec(d, tm, B_W) for _ in _DILATED for d in B_DILS])
    return pl.pallas_call(
        body, name="dx", grid=(seq // tm,), in_specs=in_specs,
        out_specs=[row(D_IN), row(D_MODEL), pl.BlockSpec((1, D_MODEL), lambda i: (0, 0))],
        out_shape=[jax.ShapeDtypeStruct((seq, D_IN), BF16), jax.ShapeDtypeStruct((seq, D_MODEL), F32),
                   jax.ShapeDtypeStruct((1, D_MODEL), F32)],
        scratch_shapes=[_stage(tm, B_W)] * ((nd - 1) * len(_DILATED)),
        compiler_params=pltpu.CompilerParams(dimension_semantics=("arbitrary",)),
    )(x, g, pre_norm, w_in_g, *rope, *nat_list, *res_list)


def _grad_w_in(ut, dproj):
    seq = ut.shape[1]
    tk = min(1024, seq)

    def body(ut_ref, dp_ref, out_ref):
        @pl.when(pl.program_id(1) == 0)
        def _():
            out_ref[...] = jnp.zeros_like(out_ref)

        out_ref[...] += _dot(ut_ref[...], dp_ref[...])

    return pl.pallas_call(
        body, name="grad_w_in", grid=(N_CHIPS, seq // tk),
        in_specs=[pl.BlockSpec((D_MODEL, tk), lambda j, i: (0, i)), pl.BlockSpec((tk, SHARD_IN), lambda j, i: (i, j))],
        out_specs=pl.BlockSpec((None, D_MODEL, SHARD_IN), lambda j, i: (j, 0, 0)),
        out_shape=jax.ShapeDtypeStruct((N_CHIPS, D_MODEL, SHARD_IN), F32),
        compiler_params=pltpu.CompilerParams(dimension_semantics=("arbitrary", "arbitrary")),
    )(ut, dproj)


def _pair_exchange(grads):
    n = len(grads)

    def body(*refs):
        srcs, outs = refs[:n], refs[n:2 * n]
        send_sems, recv_sems = refs[2 * n:]
        x, y, c = lax.axis_index("x"), lax.axis_index("y"), lax.axis_index("c")
        copies = []
        for t in range(n):
            rows = grads[t].shape[1] // 2
            copies.append(pltpu.make_async_remote_copy(
                src_ref=srcs[t].at[:, pl.ds((1 - c) * rows, rows)], dst_ref=outs[t],
                send_sem=send_sems.at[t], recv_sem=recv_sems.at[t], device_id=(x, y, 1 - c), device_id_type=MESH))
        for cp in copies:
            cp.start()
        for cp in copies:
            cp.wait()

    any_spec = pl.BlockSpec(memory_space=pl.ANY)
    return pl.pallas_call(
        body, name="pair_exchange",
        out_shape=[jax.ShapeDtypeStruct((g.shape[0], g.shape[1] // 2, g.shape[2]), g.dtype) for g in grads],
        in_specs=[any_spec] * n, out_specs=[any_spec] * n,
        scratch_shapes=[pltpu.SemaphoreType.DMA((n,)), pltpu.SemaphoreType.DMA((n,))],
    )(*grads)


def _pair_add(core, own, got):
    nchip, rows2, width = own.shape
    rows = rows2 // 2
    tr = min(512, rows)
    nb = rows // tr

    def body(core_ref, own_ref, got_ref, out_ref):
        out_ref[...] = (own_ref[...] + got_ref[...]).astype(BF16)

    grid_spec = pltpu.PrefetchScalarGridSpec(
        num_scalar_prefetch=1, grid=(nchip, nb),
        in_specs=[pl.BlockSpec((None, tr, width), lambda k, i, core_ref: (k, core_ref[0] * nb + i, 0)),
                  pl.BlockSpec((None, tr, width), lambda k, i, core_ref: (k, i, 0))],
        out_specs=pl.BlockSpec((None, tr, width), lambda k, i, core_ref: (k, i, 0)))
    return pl.pallas_call(
        body, name=f"pair_add_{width}", grid_spec=grid_spec,
        out_shape=jax.ShapeDtypeStruct((nchip, rows, width), BF16),
    )(core, own, got)


def _chip_exchange(parts, small):
    n = len(parts)

    def body(*refs):
        srcs, small_ref = refs[:n], refs[n]
        outs, small_out = refs[n + 1:2 * n + 1], refs[2 * n + 1]
        send_sems, recv_sems, local_sems = refs[2 * n + 2:]
        x, y, c = lax.axis_index("x"), lax.axis_index("y"), lax.axis_index("c")
        my_chip = 2 * x + y
        me = 4 * x + 2 * y + c
        chips = [(1 - x, y), (x, 1 - y), (1 - x, 1 - y)]
        local = [pltpu.make_async_copy(srcs[t].at[my_chip], outs[t].at[my_chip], local_sems.at[t]) for t in range(n)]
        local.append(pltpu.make_async_copy(small_ref, small_out.at[me], local_sems.at[n]))
        for cp in local:
            cp.start()
        sent = []
        for j, (cx, cy) in enumerate(chips):
            for t in range(n):
                k = n * j + t
                sent.append(pltpu.make_async_remote_copy(
                    src_ref=srcs[t].at[2 * cx + cy], dst_ref=outs[t].at[my_chip], send_sem=send_sems.at[k],
                    recv_sem=recv_sems.at[k], device_id=(cx, cy, c), device_id_type=MESH))
        peers = [(x, y, 1 - c)] + [(cx, cy, cc) for (cx, cy) in chips for cc in (c, 1 - c)]
        for j, peer in enumerate(peers):
            k = 3 * n + j
            sent.append(pltpu.make_async_remote_copy(
                src_ref=small_ref, dst_ref=small_out.at[me], send_sem=send_sems.at[k], recv_sem=recv_sems.at[k],
                device_id=peer, device_id_type=MESH))
        for cp in sent:
            cp.start()
        for cp in sent:
            cp.wait()
        for cp in local:
            cp.wait()

    any_spec = pl.BlockSpec(memory_space=pl.ANY)
    nsem = 3 * n + 7
    return pl.pallas_call(
        body, name="chip_exchange",
        out_shape=[jax.ShapeDtypeStruct(p.shape, p.dtype) for p in parts]
        + [jax.ShapeDtypeStruct((8,) + small.shape, small.dtype)],
        in_specs=[any_spec] * (n + 1), out_specs=[any_spec] * (n + 1),
        scratch_shapes=[pltpu.SemaphoreType.DMA((nsem,)), pltpu.SemaphoreType.DMA((nsem,)),
                        pltpu.SemaphoreType.DMA((n + 1,))],
    )(*parts, small)


def _slot_sum(slots, name, core=None):
    ns, rows, width = slots.shape
    tr = min(512, rows)

    def body(*refs):
        in_ref, out_ref = refs[-2:]
        acc = in_ref[0].astype(F32)
        for s in range(1, ns):
            acc = acc + in_ref[s].astype(F32)
        out_ref[...] = acc

    if core is None:
        return pl.pallas_call(
            body, name=name, grid=(rows // tr,),
            in_specs=[pl.BlockSpec((ns, tr, width), lambda i: (0, i, 0))],
            out_specs=pl.BlockSpec((tr, width), lambda i: (i, 0)),
            out_shape=jax.ShapeDtypeStruct((rows, width), F32),
        )(slots)
    grid_spec = pltpu.PrefetchScalarGridSpec(
        num_scalar_prefetch=1, grid=(rows // tr,),
        in_specs=[pl.BlockSpec((ns, tr, width), lambda i, core_ref: (0, i, 0))],
        out_specs=pl.BlockSpec((None, tr, width), lambda i, core_ref: (core_ref[0], i, 0)))
    return pl.pallas_call(
        body, name=name, grid_spec=grid_spec, out_shape=jax.ShapeDtypeStruct((2, rows, width), F32),
    )(core, slots)


def _pair_gather(bufs):
    n = len(bufs)

    def body(*refs):
        outs = refs[n:2 * n]
        send_sems, recv_sems = refs[2 * n:]
        x, y, c = lax.axis_index("x"), lax.axis_index("y"), lax.axis_index("c")
        copies = [pltpu.make_async_remote_copy(
            src_ref=outs[t].at[c], dst_ref=outs[t].at[c], send_sem=send_sems.at[t], recv_sem=recv_sems.at[t],
            device_id=(x, y, 1 - c), device_id_type=MESH) for t in range(n)]
        for cp in copies:
            cp.start()
        for cp in copies:
            cp.wait()

    any_spec = pl.BlockSpec(memory_space=pl.ANY)
    res = pl.pallas_call(
        body, name="pair_gather",
        out_shape=[jax.ShapeDtypeStruct(b.shape, b.dtype) for b in bufs],
        in_specs=[any_spec] * n, out_specs=[any_spec] * n,
        input_output_aliases={t: t for t in range(n)},
        scratch_shapes=[pltpu.SemaphoreType.DMA((n,)), pltpu.SemaphoreType.DMA((n,))],
    )(*bufs)
    return [r.reshape(2 * b.shape[1], b.shape[2]) for r, b in zip(res, bufs)]


def _adamw(w, g, m, v, name):
    rows, width = w.shape
    tr = min(256, rows)
    c1 = 1.0 / (1.0 - ADAM_B1 ** ADAM_STEP)
    c2 = 1.0 / (1.0 - ADAM_B2 ** ADAM_STEP)

    def body(w_ref, g_ref, m_ref, v_ref, d_ref, nm_ref, nv_ref):
        gv = g_ref[...]
        nm = ADAM_B1 * m_ref[...] + (1.0 - ADAM_B1) * gv
        nv = ADAM_B2 * v_ref[...] + (1.0 - ADAM_B2) * (gv * gv)
        nm_ref[...] = nm
        nv_ref[...] = nv
        d_ref[...] = -ADAM_LR * ((nm * c1) / (jnp.sqrt(nv * c2) + ADAM_EPS) + ADAM_WD * w_ref[...])

    spec = pl.BlockSpec((tr, width), lambda i: (i, 0))
    return pl.pallas_call(
        body, name=name, grid=(rows // tr,), in_specs=[spec] * 4, out_specs=[spec] * 3,
        out_shape=[jax.ShapeDtypeStruct(w.shape, F32)] * 3,
    )(w, g, m, v)


def _local_step(x, mem, target, pre_norm, sink_a, mem_norm, post_norm, w_in_g, w_out, w_mkv):
    mk, mv = _mem_kv(mem, mem_norm, w_mkv)
    pr = _pre_proj(x, pre_norm, w_in_g)
    sink = sink_a.reshape(-1)
    qa, ka, va = pr["qa"][None], pr["ka"][None], pr["va"][None]
    oa, lse_a = _band_fwd(qa, ka, va, sink, max_dist=A_WINDOW - 1, name="swa_fwd")
    ob_list, lseb_list = [], []
    for k, (win, dil) in enumerate(B_CONFIGS):
        o_i, l_i = _band_fwd(pr["qb"][k], pr["kb"][k], pr["vb"][k], None, max_dist=win // dil, name=f"dil{dil}_fwd")
        ob_list.append(o_i)
        lseb_list.append(l_i)
    oc, lse_c = _mem_attn_fwd(pr["qc"], mk, mv)
    sink_row = jnp.pad(sink, (0, LANES - sink.shape[0])).reshape(1, LANES)
    po = _post(x, target, post_norm, w_out, sink_row, oa[0], lse_a[0], pr["ga"], ob_list, lseb_list, pr["gb"], oc,
               pr["gc"])
    dqc, dmk, dmv = _mem_attn_bwd(pr["qc"], mk, mv, po["doc"], lse_c, po["dl_c"])
    dqa, dka, dva = _band_bwd(qa, ka, va, po["doa"][None], lse_a, po["dl_a"][None], max_dist=A_WINDOW - 1,
                              name="swa_bwd")
    res = dict(qb=[], kb=[], vb=[])
    for k, (win, dil) in enumerate(B_CONFIGS):
        dq_i, dk_i, dv_i = _band_bwd(pr["qb"][k], pr["kb"][k], pr["vb"][k], po["dob"][k], po["lse_b"][k],
                                     po["dl_b"][k], max_dist=win // dil, name=f"dil{dil}_bwd")
        res["qb"].append(dq_i)
        res["kb"].append(dk_i)
        res["vb"].append(dv_i)
    nat = dict(qa=dqa[0], ka=dka[0], va=dva[0], ga=po["dga"], gb=po["dgb"], qc=dqc, gc=po["dgc"])
    dproj, grad_x, gpre = _dx(x, po["g"], pre_norm, w_in_g, nat, res)
    gw_in = _grad_w_in(pr["ut"], dproj)
    gw_mkv, gmem = _mem_kv_bwd(mem, mem_norm, w_mkv, dmk, dmv)
    gsink = -po["gsink"][0, :sink.shape[0]]
    return dict(loss=po["loss"], grad_x=grad_x, gw_in=gw_in, gw_out=po["gw_out"], gw_mkv=gw_mkv,
                gpre=gpre, gpost=po["gpost"], gmem=gmem, gsink=gsink)


def kernel(x, mem, pre_norm, w_in, sink_a, mem_norm, w_mem_kv, w_out, post_norm, loss_target, m_pre_norm, m_w_in, m_sink_a, m_mem_norm, m_w_mem_kv, m_w_out, m_post_norm, v_pre_norm, v_w_in, v_sink_a, v_mem_norm, v_w_mem_kv, v_w_out, v_post_norm):
    w_in_g, w_out_g, w_mkv_g = _gather_weights(w_in[0].astype(BF16), w_out[0].astype(BF16), w_mem_kv[0].astype(BF16))
    loc = _local_step(x[0], mem[0], loss_target[0], pre_norm, sink_a, mem_norm, post_norm,
                      w_in_g.reshape(N_CHIPS, D_MODEL, SHARD_IN), w_out_g.reshape(D_MODEL, D_MODEL),
                      w_mkv_g.reshape(D_MODEL, 2 * C_W))
    big = [loc["gw_in"], loc["gw_out"].reshape(N_CHIPS, D_MODEL // N_CHIPS, D_MODEL),
           loc["gw_mkv"].reshape(N_CHIPS, D_MODEL // N_CHIPS, 2 * C_W)]
    widen = lambda a: jnp.pad(a.reshape(1, -1), ((0, 0), (0, D_MODEL - a.size)))
    small = jnp.concatenate([loc["gpre"], loc["gpost"], loc["gmem"], widen(loc["gsink"]), widen(loc["loss"]),
                             jnp.zeros((3, D_MODEL), F32)], axis=0)
    core = lax.axis_index("c").astype(jnp.int32).reshape(1)
    got = _pair_exchange(big)
    parts = [_pair_add(core, own, g) for own, g in zip(big, got)]
    *slots, small_slots = _chip_exchange(parts, small)
    halves = [_slot_sum(s, name=f"chip_sum_{s.shape[2]}", core=core) for s in slots]
    g_in, g_out, g_mkv = _pair_gather(halves)
    small_sum = _slot_sum(small_slots, name="device_sum")
    g_pre, g_post, g_mem = small_sum[0:1], small_sum[1:2], small_sum[2:3]
    g_sink = small_sum[3:4, :sink_a.shape[1]]
    loss = small_sum[4, 0]

    d_in, nm_in, nv_in = _adamw(w_in[0], g_in, m_w_in[0], v_w_in[0], "adamw_in")
    d_out, nm_out, nv_out = _adamw(w_out[0], g_out, m_w_out[0], v_w_out[0], "adamw_out")
    d_mkv, nm_mkv, nv_mkv = _adamw(w_mem_kv[0], g_mkv, m_w_mem_kv[0], v_w_mem_kv[0], "adamw_mkv")
    pad6 = lambda a: jnp.pad(a, ((0, 0), (0, D_MODEL - a.shape[1])))
    stack = lambda a, b, c_, d_: jnp.concatenate([a, b, c_, pad6(d_), jnp.zeros((4, D_MODEL), F32)], axis=0)
    d_s, nm_s, nv_s = _adamw(stack(pre_norm, post_norm, mem_norm, sink_a), small_sum,
                             stack(m_pre_norm, m_post_norm, m_mem_norm, m_sink_a),
                             stack(v_pre_norm, v_post_norm, v_mem_norm, v_sink_a), "adamw_small")
    ns_ = sink_a.shape[1]
    unpack = lambda a: (a[0:1], a[3:4, :ns_], a[2:3], a[1:2])
    d_pre, d_sink, d_mem, d_post = unpack(d_s)
    nm_pre, nm_sink, nm_mem, nm_post = unpack(nm_s)
    nv_pre, nv_sink, nv_mem, nv_post = unpack(nv_s)
    lead = lambda a: a[None]
    return (loss, lead(loc["grad_x"]),
            g_pre, lead(g_in), g_sink, g_mem, lead(g_mkv), lead(g_out), g_post,
            d_pre, lead(d_in), d_sink, d_mem, lead(d_mkv), lead(d_out), d_post,
            nm_pre, lead(nm_in), nm_sink, nm_mem, lead(nm_mkv), lead(nm_out), nm_post,
            nv_pre, lead(nv_in), nv_sink, nv_mem, lead(nv_mkv), lead(nv_out), nv_post)
```

```python
import numpy as np
import jax
import jax.numpy as jnp
from jax import lax
from jax.experimental import pallas as pl
from jax.experimental.pallas import tpu as pltpu

F32 = jnp.float32
BF16 = jnp.bfloat16

D_MODEL = 1024
HEAD_DIM = 64
LANES = 128
BLOCK = 128
A_W, A_KV_W, B_W, C_W = 384, 128, 384, 256
N_MEM = 256
D_IN = 3072
N_CHIPS = 4
SHARD_IN = D_IN // N_CHIPS
B_CONFIGS = ((128, 1), (512, 4), (2048, 16))
B_DILS = tuple(d for _, d in B_CONFIGS)
A_WINDOW = 128
RMS_EPS = 1e-6
ROPE_THETA = 500000.0
SCALE = HEAD_DIM ** -0.5
NEG = -1e30
ADAM_LR, ADAM_B1, ADAM_B2, ADAM_EPS, ADAM_WD, ADAM_STEP = 0.001, 0.9, 0.999, 1e-08, 0.01, 10

NT = (((1,), (1,)), ((), ()))
TN = (((0,), (0,)), ((), ()))
MESH = pl.DeviceIdType.MESH

_PROJ_LAYOUT = (
    [("qa", 128 * i, True, True) for i in range(3)] + [("ka", 0, True, False), ("va", 0, False, False)]
    + [("ga", 128 * i, False, False) for i in range(3)]
    + [("qb", 128 * i, True, True) for i in range(3)] + [("kb", 128 * i, True, False) for i in range(3)]
    + [("vb", 128 * i, False, False) for i in range(3)] + [("gb", 128 * i, False, False) for i in range(3)]
    + [("qc", 128 * i, False, True) for i in range(2)] + [("gc", 128 * i, False, False) for i in range(2)]
)
_PROJ_WIDTH = dict(qa=A_W, ka=A_KV_W, va=A_KV_W, ga=A_W, qb=B_W, kb=B_W, vb=B_W, gb=B_W, qc=C_W, gc=C_W)
_NATURAL = ("qa", "ka", "va", "ga", "gb", "qc", "gc")
_DILATED = ("qb", "kb", "vb")


def _dot(a, b):
    return jnp.dot(a, b, preferred_element_type=F32)


def _dot_nt(a, b):
    return lax.dot_general(a, b, NT, preferred_element_type=F32)


def _dot_tn(a, b):
    return lax.dot_general(a, b, TN, preferred_element_type=F32)


def _half_masks(rows):
    lane = lax.broadcasted_iota(jnp.int32, (rows, LANES), 1)
    return lane < HEAD_DIM, lane >= HEAD_DIM


def _rope(t, c, sm, sp):
    return t * c + pltpu.roll(t, LANES - 8, 1) * sm + pltpu.roll(t, 8, 1) * sp


def _rope_tables(seq, tm):
    dim = jnp.arange(LANES) % HEAD_DIM
    inv_freq = ROPE_THETA ** (-jnp.arange(0, 16, 2, dtype=F32) / 16)
    freq = jnp.where(dim < 16, inv_freq[dim % 8], 0.0)[None, :]
    local = jnp.arange(tm, dtype=F32)[:, None] * freq
    base = (jnp.arange(seq // tm, dtype=F32) * tm)[:, None] * freq
    both = lambda a: jnp.concatenate([jnp.cos(a), jnp.sin(a)], axis=1)
    return both(local), jnp.repeat(both(base), 8, axis=0)


def _rope_coeffs(local_ref, base_ref):
    cl, sl = local_ref[:, :LANES], local_ref[:, LANES:]
    cb, sb = base_ref[0:1, :LANES], base_ref[0:1, LANES:]
    cos = cb * cl - sb * sl
    sin = sb * cl + cb * sl
    dim = lax.broadcasted_iota(jnp.int32, (1, LANES), 1) % HEAD_DIM
    return cos, jnp.where(dim < 8, -sin, 0.0), jnp.where((dim >= 8) & (dim < 16), sin, 0.0)


def _split3(x):
    a = x.astype(BF16)
    r = x - a.astype(F32)
    b = r.astype(BF16)
    c = (r - b.astype(F32)).astype(BF16)
    return a, b, c


def _rows_to_lanes(x):
    row = lax.broadcasted_iota(jnp.int32, (8, LANES), 0)
    lane = lax.broadcasted_iota(jnp.int32, (8, LANES), 1)
    eye = (row == lane).astype(BF16)
    a, b, c = _split3(x)
    return _dot_nt(eye, a) + _dot_nt(eye, b) + _dot_nt(eye, c)


def _head_sum_matrix(width):
    k = lax.broadcasted_iota(jnp.int32, (width, LANES), 0)
    h = lax.broadcasted_iota(jnp.int32, (width, LANES), 1)
    return (k // HEAD_DIM == h).astype(BF16)


def _head_expand_matrix(width):
    h = lax.broadcasted_iota(jnp.int32, (LANES, width), 0)
    k = lax.broadcasted_iota(jnp.int32, (LANES, width), 1)
    return (k // HEAD_DIM == h).astype(BF16)


def _dot_split(x, mat, terms):
    parts = _split3(x)[:terms]
    out = _dot(parts[0], mat)
    for p in parts[1:]:
        out = out + _dot(p, mat)
    return out


def _per_head(cols, fill=0.0):
    rows = cols[0].shape[0]
    lane = lax.broadcasted_iota(jnp.int32, (rows, LANES), 1)
    out = jnp.full((rows, LANES), fill, F32)
    for h, col in enumerate(cols):
        out = jnp.where(lane == h, col, out)
    return out


def _lane_blocks(width):
    return [slice(p * LANES, (p + 1) * LANES) for p in range(width // LANES)]


def _stage(rows, width):
    return pltpu.VMEM((width // LANES, rows, LANES), F32)


def _stage_write(buf, value):
    for p, lanes in enumerate(_lane_blocks(value.shape[1])):
        buf[p] = value[:, lanes]


def _stage_read(buf):
    return jnp.concatenate([buf[p] for p in range(buf.shape[0])], axis=1) if buf.shape[0] > 1 else buf[0]


def _to_residues(buf, out_ref, dil):
    rows = buf.shape[1] // dil
    for r in range(dil):
        for p in range(buf.shape[0]):
            plane = buf.at[p]
            out_ref[r, :, p * LANES:(p + 1) * LANES] = plane[pl.ds(r, rows, stride=dil), :].astype(out_ref.dtype)


def _from_residues(in_ref, buf, dil):
    rows = buf.shape[1] // dil
    for r in range(dil):
        for p in range(buf.shape[0]):
            plane = buf.at[p]
            plane[pl.ds(r, rows, stride=dil), :] = in_ref[r, :, p * LANES:(p + 1) * LANES].astype(F32)


def _residue_spec(dil, tm, width):
    return pl.BlockSpec((dil, tm // dil, width), lambda i: (0, i, 0))


def _gather_weights(w_in_s, w_out_s, w_mkv_s):
    shards = tuple(s.reshape(2, s.shape[0] // 2, s.shape[1]) for s in (w_in_s, w_out_s, w_mkv_s))
    n = len(shards)

    def body(*refs):
        srcs, outs = refs[:n], refs[2 * n:3 * n]
        send_sems, recv_sems = refs[3 * n:]
        x, y, c = lax.axis_index("x"), lax.axis_index("y"), lax.axis_index("c")
        my_chip = 2 * x + y
        sibling = (x, y, 1 - c)
        chips = [(1 - x, y), (x, 1 - y), (1 - x, 1 - y)]

        def half(t, chip, which):
            return outs[t].at[chip, which]

        def src_half(t, which):
            return srcs[t].at[which]

        def copy(k, src, dst, to):
            return pltpu.make_async_remote_copy(src_ref=src, dst_ref=dst, send_sem=send_sems.at[k],
                                                recv_sem=recv_sems.at[k], device_id=to, device_id_type=MESH)

        first = []
        for j, (cx, cy) in enumerate(chips):
            for t in range(n):
                first.append(copy(n * j + t, src_half(t, c), half(t, my_chip, c), (cx, cy, c)))
        for cp in first:
            cp.start()
        passed = []
        for j, (cx, cy) in enumerate(chips):
            chip = 2 * cx + cy
            for t in range(n):
                k = n * j + t
                copy(k, src_half(t, c), half(t, chip, c), (cx, cy, c)).wait_recv()
                fwd = copy(n * 3 + k, half(t, chip, c), half(t, chip, c), sibling)
                fwd.start()
                passed.append(fwd)
        for j, (cx, cy) in enumerate(chips):
            chip = 2 * cx + cy
            for t in range(n):
                k = n * 3 + n * j + t
                copy(k, half(t, chip, 1 - c), half(t, chip, 1 - c), sibling).wait_recv()
        for cp in first + passed:
            cp.wait_send()

    my_chip = 2 * lax.axis_index("x") + lax.axis_index("y")
    landing = [lax.dynamic_update_slice(jnp.zeros((N_CHIPS,) + s.shape, s.dtype), s[None], (my_chip, 0, 0, 0))
               for s in shards]
    any_spec = pl.BlockSpec(memory_space=pl.ANY)
    return pl.pallas_call(
        body, name="gather_weights",
        out_shape=[jax.ShapeDtypeStruct((N_CHIPS,) + s.shape, s.dtype) for s in shards],
        in_specs=[any_spec] * (2 * n), out_specs=[any_spec] * n,
        input_output_aliases={n + t: t for t in range(n)},
        scratch_shapes=[pltpu.SemaphoreType.DMA((6 * n,)), pltpu.SemaphoreType.DMA((6 * n,))],
    )(*shards, *landing)


def _mem_kv(mem, mem_norm, w_mkv):
    def body(mem_ref, g_ref, w_ref, mk_ref, mv_ref):
        m = mem_ref[...]
        r = lax.rsqrt(jnp.mean(m * m, axis=-1, keepdims=True) + RMS_EPS)
        mn = (m * r * g_ref[...]).astype(BF16)
        kv = _dot(mn, w_ref[...])
        mk_ref[...] = kv[:, :C_W].astype(BF16)
        mv_ref[...] = kv[:, C_W:].astype(BF16)

    return pl.pallas_call(
        body, name="mem_kv",
        out_shape=[jax.ShapeDtypeStruct((N_MEM, C_W), BF16)] * 2,
    )(mem, mem_norm, w_mkv)


def _mem_kv_bwd(mem, mem_norm, w_mkv, dmk, dmv):
    def body(mem_ref, g_ref, w_ref, dmk_ref, dmv_ref, gw_ref, gn_ref):
        m = mem_ref[...]
        r = lax.rsqrt(jnp.mean(m * m, axis=-1, keepdims=True) + RMS_EPS)
        mhat = m * r
        mn = (mhat * g_ref[...]).astype(BF16)
        dkv = jnp.concatenate([dmk_ref[...], dmv_ref[...]], axis=1).astype(BF16)
        gw_ref[...] = _dot_tn(mn, dkv)
        dmn = _dot_nt(dkv, w_ref[...])
        gn_ref[...] = jnp.sum(dmn * mhat, axis=0, keepdims=True)

    return pl.pallas_call(
        body, name="mem_kv_bwd",
        out_shape=[jax.ShapeDtypeStruct((D_MODEL, 2 * C_W), F32), jax.ShapeDtypeStruct((1, D_MODEL), F32)],
    )(mem, mem_norm, w_mkv, dmk, dmv)


def _pre_proj(x, pre_norm, w_in_g):
    seq = x.shape[0]
    tm = min(512, seq)
    n_nat, n_dil = len(_NATURAL), len(_DILATED) * len(B_DILS)
    rope = _rope_tables(seq, tm)

    def body(x_ref, g_ref, w_ref, rl_ref, rb_ref, *refs):
        nat = dict(zip(_NATURAL, refs[:n_nat]))
        res = {n: refs[n_nat + len(B_DILS) * k:n_nat + len(B_DILS) * (k + 1)] for k, n in enumerate(_DILATED)}
        ut = refs[n_nat + n_dil]
        bufs = dict(zip(_DILATED, refs[n_nat + n_dil + 1:]))
        xv = x_ref[...]
        r = lax.rsqrt(jnp.mean(xv * xv, axis=-1, keepdims=True) + RMS_EPS)
        u = xv * r * g_ref[...]
        ub = u.astype(BF16)
        ut[...] = u.T.astype(BF16)
        c, sm, sp = _rope_coeffs(rl_ref, rb_ref)
        for j in range(N_CHIPS):
            pj = _dot(ub, w_ref[j])
            for b in range(SHARD_IN // LANES):
                name, off, roped, scaled = _PROJ_LAYOUT[(SHARD_IN // LANES) * j + b]
                piece = pj[:, LANES * b:LANES * (b + 1)]
                if roped:
                    piece = _rope(piece, c, sm, sp)
                if scaled:
                    piece = piece * SCALE
                if name in bufs:
                    bufs[name][off // LANES] = piece
                else:
                    nat[name][:, off:off + LANES] = piece.astype(BF16)
        for name in _DILATED:
            for ref, dil in zip(res[name], B_DILS):
                _to_residues(bufs[name], ref, dil)

    row = lambda w: pl.BlockSpec((tm, w), lambda i: (i, 0))
    full = lambda a: pl.BlockSpec(a.shape, lambda i: (0,) * a.ndim)
    out_shape = [jax.ShapeDtypeStruct((seq, _PROJ_WIDTH[n]), BF16) for n in _NATURAL]
    out_specs = [row(_PROJ_WIDTH[n]) for n in _NATURAL]
    for n in _DILATED:
        for dil in B_DILS:
            out_shape.append(jax.ShapeDtypeStruct((dil, seq // dil, B_W), BF16))
            out_specs.append(_residue_spec(dil, tm, B_W))
    out_shape.append(jax.ShapeDtypeStruct((D_MODEL, seq), BF16))
    out_specs.append(pl.BlockSpec((D_MODEL, tm), lambda i: (0, i)))
    res = pl.pallas_call(
        body, name="pre_proj", grid=(seq // tm,),
        in_specs=[row(D_MODEL), full(pre_norm), full(w_in_g), full(rope[0]), pl.BlockSpec((8, 2 * LANES), lambda i: (i, 0))],
        out_specs=out_specs, out_shape=out_shape,
        scratch_shapes=[_stage(tm, B_W)] * len(_DILATED),
    )(x, pre_norm, w_in_g, *rope)
    out = dict(zip(_NATURAL, res[:n_nat]))
    for k, n in enumerate(_DILATED):
        out[n] = res[n_nat + len(B_DILS) * k:n_nat + len(B_DILS) * (k + 1)]
    out["ut"] = res[n_nat + n_dil]
    return out


def _band_bias(max_dist, transposed):
    i = np.arange(BLOCK)[:, None]
    j = np.arange(BLOCK)[None, :]
    if transposed:
        same = i <= j
        other = (j + BLOCK - i) <= max_dist
        vis = np.concatenate([same, other], axis=1)
    else:
        prev = (i + BLOCK - j) <= max_dist
        same = j <= i
        vis = np.concatenate([prev, same], axis=1)
    return jnp.asarray(np.where(vis, 0.0, NEG).astype(np.float32))


def _kv_place(h, gqa):
    return (0, h // 3) if gqa else (h // 2, h % 2)


def _band_fwd(q, k, v, sink, *, max_dist, name):
    dil, length, wq = q.shape
    wk = k.shape[2]
    gqa = wk != wq
    tq = min(512, length)
    ns, nt = tq // BLOCK, length // tq
    npair = wq // LANES
    bias = _band_bias(max_dist, transposed=False)
    has_sink = sink is not None

    def body(*refs):
        if has_sink:
            sink_ref, refs = refs[0], refs[1:]
        q_ref, k_ref, kp_ref, v_ref, vp_ref, bias_ref, o_ref, lse_ref, kbuf, vbuf = refs[:10]
        i = pl.program_id(1)
        kbuf[0:BLOCK] = kp_ref[...]
        kbuf[BLOCK:] = k_ref[...]
        vbuf[0:BLOCK] = vp_ref[...]
        vbuf[BLOCK:] = v_ref[...]
        if gqa:
            kroll, vroll = refs[10:12]
            kroll[...] = pltpu.roll(kbuf[...], HEAD_DIM, 1)
            vroll[...] = pltpu.roll(vbuf[...], HEAD_DIM, 1)
        half = _half_masks(BLOCK)
        col_prev = (lax.broadcasted_iota(jnp.int32, (1, 2 * BLOCK), 1) < BLOCK).astype(F32)

        def sub(a, carry):
            r0 = pl.multiple_of(a * BLOCK, BLOCK)
            pen = jnp.where((i == 0) & (a == 0), NEG, 0.0)
            b = bias_ref[...] + pen * col_prev
            scores = []
            for p in range(npair):
                qp = q_ref[pl.ds(r0, BLOCK), p * LANES:(p + 1) * LANES]
                for e in range(2):
                    pk, ek = _kv_place(2 * p + e, gqa)
                    kw = (kbuf if ek == e else kroll)[pl.ds(r0, 2 * BLOCK), pk * LANES:(pk + 1) * LANES]
                    scores.append(_dot_nt(jnp.where(half[e], qp, jnp.zeros_like(qp)), kw))
            m_cols, l_cols, probs = [], [], []
            for h, s in enumerate(scores):
                s = s + b
                m = jnp.max(s, axis=1, keepdims=True)
                if has_sink:
                    m = jnp.maximum(m, sink_ref[h])
                pe = jnp.exp(s - m)
                l = jnp.sum(pe, axis=1, keepdims=True)
                if has_sink:
                    l = l + jnp.exp(sink_ref[h] - m)
                probs.append(pe.astype(BF16))
                m_cols.append(m)
                l_cols.append(l)
            for p in range(npair):
                o_h = []
                for e in range(2):
                    h = 2 * p + e
                    pk, ek = _kv_place(h, gqa)
                    vw = (vbuf if ek == e else vroll)[pl.ds(r0, 2 * BLOCK), pk * LANES:(pk + 1) * LANES]
                    o_h.append(_dot(probs[h], vw) * (1.0 / l_cols[h]))
                o_ref[pl.ds(r0, BLOCK), p * LANES:(p + 1) * LANES] = jnp.where(half[0], o_h[0], o_h[1]).astype(BF16)
            lse_ref[pl.ds(r0, BLOCK), :] = _per_head(m_cols) + jnp.log(_per_head(l_cols, 1.0))
            return carry

        lax.fori_loop(0, ns, sub, 0, unroll=True)

    main = lambda w: pl.BlockSpec((None, tq, w), lambda r, i: (r, i, 0))
    prev = lambda w: pl.BlockSpec((None, BLOCK, w), lambda r, i: (r, jnp.maximum(i * ns - 1, 0), 0))
    in_specs = [main(wq), main(wk), prev(wk), main(wk), prev(wk), pl.BlockSpec(bias.shape, lambda r, i: (0, 0))]
    args = [q, k, k, v, v, bias]
    if has_sink:
        in_specs = [pl.BlockSpec(memory_space=pltpu.SMEM)] + in_specs
        args = [sink] + args
    scratch = [pltpu.VMEM((tq + BLOCK, wk), BF16)] * (4 if gqa else 2)
    return pl.pallas_call(
        body, name=name, grid=(dil, nt), in_specs=in_specs,
        out_specs=[main(wq), main(LANES)],
        out_shape=[jax.ShapeDtypeStruct((dil, length, wq), BF16), jax.ShapeDtypeStruct((dil, length, LANES), F32)],
        scratch_shapes=scratch,
    )(*args)


def _band_bwd(q, k, v, do, lse, delta, *, max_dist, name):
    dil, length, wq = q.shape
    wk = k.shape[2]
    gqa = wk != wq
    tq = min(512, length)
    ns, nt = tq // BLOCK, length // tq
    npair = wq // LANES
    nblocks = length // BLOCK
    bias = _band_bias(max_dist, transposed=True)

    def body(q_ref, qn_ref, do_ref, don_ref, lse_ref, lsen_ref, dl_ref, dln_ref, k_ref, v_ref, bias_ref,
             dq_ref, dk_ref, dv_ref, qbuf, dobuf, stat_l, stat_d, dqt, kt, *rolled):
        i = pl.program_id(1)
        qbuf[0:tq] = q_ref[...]
        qbuf[tq:] = qn_ref[...]
        dobuf[0:tq] = do_ref[...]
        dobuf[tq:] = don_ref[...]
        for pk in range(wk // LANES):
            kt[pk] = k_ref[:, pk * LANES:(pk + 1) * LANES].astype(F32).T.astype(BF16)
        if gqa:
            kroll, vroll, ktroll = rolled
            kroll[...] = pltpu.roll(k_ref[...], HEAD_DIM, 1)
            vroll[...] = pltpu.roll(v_ref[...], HEAD_DIM, 1)
            ktroll[0] = kroll[...].astype(F32).T.astype(BF16)
        for a in range(ns):
            rows = slice(a * BLOCK, (a + 1) * BLOCK)
            stat_l[a] = _rows_to_lanes(lse_ref[rows, :])
            stat_d[a] = _rows_to_lanes(dl_ref[rows, :])
        stat_l[ns] = _rows_to_lanes(lsen_ref[...])
        stat_d[ns] = _rows_to_lanes(dln_ref[...])

        @pl.when(i == 0)
        def _():
            dqt[:, :, 0:BLOCK] = jnp.zeros((npair, LANES, BLOCK), F32)

        @pl.when(i > 0)
        def _():
            dqt[:, :, 0:BLOCK] = dqt[:, :, tq:tq + BLOCK]

        dqt[:, :, BLOCK:] = jnp.zeros((npair, LANES, tq), F32)
        half2 = _half_masks(2 * BLOCK)
        row = lax.broadcasted_iota(jnp.int32, (LANES, BLOCK), 0)
        row_half = (row < HEAD_DIM, row >= HEAD_DIM)
        col_next = (lax.broadcasted_iota(jnp.int32, (1, 2 * BLOCK), 1) >= BLOCK).astype(F32)

        for b in range(ns):
            rows = slice(b * BLOCK, (b + 1) * BLOCK)
            window = slice(b * BLOCK, (b + 2) * BLOCK)
            bt = bias_ref[...]
            if b == ns - 1:
                bt = bt + jnp.where(i == nt - 1, NEG, 0.0) * col_next
            acc = {}
            items = []
            for p in range(npair):
                lanes = slice(p * LANES, (p + 1) * LANES)
                qw = qbuf[window, lanes]
                dow = dobuf[window, lanes]
                for e in range(2):
                    h = 2 * p + e
                    pk, ek = _kv_place(h, gqa)
                    klanes = slice(pk * LANES, (pk + 1) * LANES)
                    kb = (k_ref if ek == e else kroll)[rows, klanes]
                    vb = (v_ref if ek == e else vroll)[rows, klanes]
                    qm = jnp.where(half2[e], qw, jnp.zeros_like(qw))
                    dom = jnp.where(half2[e], dow, jnp.zeros_like(dow))
                    items.append(dict(p=p, e=e, h=h, pk=pk, ek=ek, qm=qm, dom=dom,
                                      st=_dot_nt(kb, qm), dpt=_dot_nt(vb, dom)))
            for it in items:
                h = it["h"]
                lrow = jnp.concatenate([stat_l[b, h:h + 1, :], stat_l[b + 1, h:h + 1, :]], axis=1)
                drow = jnp.concatenate([stat_d[b, h:h + 1, :], stat_d[b + 1, h:h + 1, :]], axis=1)
                pt = jnp.exp(it["st"] + bt - lrow)
                it["ptb"] = pt.astype(BF16)
                it["dsb"] = (pt * (it["dpt"] - drow)).astype(BF16)
            for it in items:
                p, e, pk, ek = it["p"], it["e"], it["pk"], it["ek"]
                dv_c = _dot(it["ptb"], it["dom"])
                dk_c = _dot(it["dsb"], it["qm"])
                kbt = (kt if ek == e else ktroll)[pk, :, rows]
                kbtm = jnp.where(row_half[e], kbt, jnp.zeros_like(kbt))
                dqt[p, :, window] += _dot(kbtm, it["dsb"])
                key = (pk, ek == e)
                if key in acc:
                    acc[key] = (acc[key][0] + dk_c, acc[key][1] + dv_c)
                else:
                    acc[key] = (dk_c, dv_c)
            if not gqa:
                for p in range(npair):
                    lanes = slice(p * LANES, (p + 1) * LANES)
                    dk_ref[rows, lanes] = acc[(p, True)][0].astype(BF16)
                    dv_ref[rows, lanes] = acc[(p, True)][1].astype(BF16)
            if gqa:
                dk_al, dv_al = acc[(0, True)]
                dk_mis, dv_mis = acc[(0, False)]
                dk_ref[rows, :] = (dk_al + pltpu.roll(dk_mis, HEAD_DIM, 1)).astype(BF16)
                dv_ref[rows, :] = (dv_al + pltpu.roll(dv_mis, HEAD_DIM, 1)).astype(BF16)

        for p in range(npair):
            dq_ref[:, p * LANES:(p + 1) * LANES] = dqt[p, :, 0:tq].T.astype(BF16)

    main = lambda w: pl.BlockSpec((None, tq, w), lambda r, i: (r, i, 0))
    nxt = lambda w: pl.BlockSpec((None, BLOCK, w), lambda r, i: (r, jnp.minimum((i + 1) * ns, nblocks - 1), 0))
    scratch = [pltpu.VMEM((tq + BLOCK, wq), BF16), pltpu.VMEM((tq + BLOCK, wq), BF16),
               pltpu.VMEM((ns + 1, 8, LANES), F32), pltpu.VMEM((ns + 1, 8, LANES), F32),
               pltpu.VMEM((npair, LANES, tq + BLOCK), F32), pltpu.VMEM((wk // LANES, LANES, tq), BF16)]
    if gqa:
        scratch = scratch + [pltpu.VMEM((tq, wk), BF16)] * 2 + [pltpu.VMEM((1, LANES, tq), BF16)]
    return pl.pallas_call(
        body, name=name, grid=(dil, nt),
        in_specs=[main(wq), nxt(wq), main(wq), nxt(wq), main(LANES), nxt(LANES), main(LANES), nxt(LANES),
                  main(wk), main(wk), pl.BlockSpec(bias.shape, lambda r, i: (0, 0))],
        out_specs=[main(wq), main(wk), main(wk)],
        out_shape=[jax.ShapeDtypeStruct((dil, length, wq), BF16), jax.ShapeDtypeStruct((dil, length, wk), BF16),
                   jax.ShapeDtypeStruct((dil, length, wk), BF16)],
        scratch_shapes=scratch,
        compiler_params=pltpu.CompilerParams(dimension_semantics=("arbitrary", "arbitrary")),
    )(q, q, do, do, lse, lse, delta, delta, k, v, bias)


def _mem_attn_fwd(q, mk, mv):
    seq = q.shape[0]
    tq = min(512, seq)
    ns = tq // BLOCK

    def body(q_ref, mk_ref, mv_ref, o_ref, lse_ref):
        half = _half_masks(BLOCK)

        def sub(a, carry):
            r0 = pl.multiple_of(a * BLOCK, BLOCK)
            scores = []
            for p in range(C_W // LANES):
                lanes = slice(p * LANES, (p + 1) * LANES)
                qp = q_ref[pl.ds(r0, BLOCK), lanes]
                for e in range(2):
                    scores.append(_dot_nt(jnp.where(half[e], qp, jnp.zeros_like(qp)), mk_ref[:, lanes]))
            m_cols, l_cols, probs = [], [], []
            for s in scores:
                m = jnp.max(s, axis=1, keepdims=True)
                pe = jnp.exp(s - m)
                probs.append(pe.astype(BF16))
                m_cols.append(m)
                l_cols.append(jnp.sum(pe, axis=1, keepdims=True))
            for p in range(C_W // LANES):
                lanes = slice(p * LANES, (p + 1) * LANES)
                o_h = [_dot(probs[2 * p + e], mv_ref[:, lanes]) * (1.0 / l_cols[2 * p + e]) for e in range(2)]
                o_ref[pl.ds(r0, BLOCK), lanes] = jnp.where(half[0], o_h[0], o_h[1]).astype(BF16)
            lse_ref[pl.ds(r0, BLOCK), :] = _per_head(m_cols) + jnp.log(_per_head(l_cols, 1.0))
            return carry

        lax.fori_loop(0, ns, sub, 0, unroll=True)

    row = lambda w: pl.BlockSpec((tq, w), lambda i: (i, 0))
    full = pl.BlockSpec((N_MEM, C_W), lambda i: (0, 0))
    return pl.pallas_call(
        body, name="mem_attn_fwd", grid=(seq // tq,), in_specs=[row(C_W), full, full],
        out_specs=[row(C_W), row(LANES)],
        out_shape=[jax.ShapeDtypeStruct((seq, C_W), BF16), jax.ShapeDtypeStruct((seq, LANES), F32)],
    )(q, mk, mv)


def _mem_attn_bwd(q, mk, mv, do, lse, delta):
    seq = q.shape[0]
    tq = min(512, seq)
    ns = tq // BLOCK
    npair = C_W // LANES

    def body(q_ref, mk_ref, mv_ref, do_ref, lse_ref, dl_ref, dq_ref, dmk_ref, dmv_ref, stat_l, stat_d, mkt, dqt):
        @pl.when(pl.program_id(0) == 0)
        def _():
            dmk_ref[...] = jnp.zeros_like(dmk_ref)
            dmv_ref[...] = jnp.zeros_like(dmv_ref)
            for p in range(npair):
                mkt[p] = mk_ref[:, p * LANES:(p + 1) * LANES].astype(F32).T.astype(BF16)

        for a in range(ns):
            rows = slice(a * BLOCK, (a + 1) * BLOCK)
            stat_l[a] = _rows_to_lanes(lse_ref[rows, :])
            stat_d[a] = _rows_to_lanes(dl_ref[rows, :])
        half = _half_masks(BLOCK)
        row = lax.broadcasted_iota(jnp.int32, (LANES, N_MEM), 0)
        row_half = (row < HEAD_DIM, row >= HEAD_DIM)

        for a in range(ns):
            rows = slice(a * BLOCK, (a + 1) * BLOCK)
            items = []
            for p in range(npair):
                lanes = slice(p * LANES, (p + 1) * LANES)
                qp = q_ref[rows, lanes]
                dop = do_ref[rows, lanes]
                for e in range(2):
                    qm = jnp.where(half[e], qp, jnp.zeros_like(qp))
                    dom = jnp.where(half[e], dop, jnp.zeros_like(dop))
                    items.append(dict(p=p, e=e, qm=qm, dom=dom, st=_dot_nt(mk_ref[:, lanes], qm),
                                      dpt=_dot_nt(mv_ref[:, lanes], dom)))
            for it in items:
                h = 2 * it["p"] + it["e"]
                pt = jnp.exp(it["st"] - stat_l[a, h:h + 1, :])
                it["ptb"] = pt.astype(BF16)
                it["dsb"] = (pt * (it["dpt"] - stat_d[a, h:h + 1, :])).astype(BF16)
            for p in range(npair):
                lanes = slice(p * LANES, (p + 1) * LANES)
                pair = [it for it in items if it["p"] == p]
                dmv_ref[:, lanes] += _dot(pair[0]["ptb"], pair[0]["dom"]) + _dot(pair[1]["ptb"], pair[1]["dom"])
                dmk_ref[:, lanes] += _dot(pair[0]["dsb"], pair[0]["qm"]) + _dot(pair[1]["dsb"], pair[1]["qm"])
                kbt = mkt[p]
                dqt[p, :, rows] = (_dot(jnp.where(row_half[0], kbt, jnp.zeros_like(kbt)), pair[0]["dsb"])
                                   + _dot(jnp.where(row_half[1], kbt, jnp.zeros_like(kbt)), pair[1]["dsb"]))
        for p in range(npair):
            dq_ref[:, p * LANES:(p + 1) * LANES] = dqt[p].T.astype(BF16)

    row = lambda w: pl.BlockSpec((tq, w), lambda i: (i, 0))
    full = pl.BlockSpec((N_MEM, C_W), lambda i: (0, 0))
    return pl.pallas_call(
        body, name="mem_attn_bwd", grid=(seq // tq,),
        in_specs=[row(C_W), full, full, row(C_W), row(LANES), row(LANES)], out_specs=[row(C_W), full, full],
        out_shape=[jax.ShapeDtypeStruct((seq, C_W), BF16), jax.ShapeDtypeStruct((N_MEM, C_W), F32),
                   jax.ShapeDtypeStruct((N_MEM, C_W), F32)],
        scratch_shapes=[pltpu.VMEM((ns, 8, LANES), F32)] * 2
        + [pltpu.VMEM((npair, LANES, N_MEM), BF16), pltpu.VMEM((npair, LANES, tq), F32)],
        compiler_params=pltpu.CompilerParams(dimension_semantics=("arbitrary",)),
    )(q, mk, mv, do, lse, delta)


def _silu_and_grad(g):
    s = 1.0 / (1.0 + jnp.exp(-g))
    return g * s, s * (1.0 + g * (1.0 - s))


def _post(x, target, post_norm, w_out, sink_row, oa, lse_a, ga, ob_list, lseb_list, gb, oc, gc):
    seq = x.shape[0]
    tm = min(256, seq)
    inv_d = 1.0 / D_MODEL
    nd = len(B_DILS)

    def body(*refs):
        (x_ref, t_ref, gp_ref, w_ref, sink_ref, oa_ref, lsea_ref, ga_ref), refs = refs[:8], refs[8:]
        ob_refs, lb_refs, (gb_ref, oc_ref, gc_ref), refs = refs[:nd], refs[nd:2 * nd], refs[2 * nd:2 * nd + 3], refs[2 * nd + 3:]
        (g_ref, doa_ref, dla_ref, dga_ref), refs = refs[:4], refs[4:]
        dob_refs, lsec_refs, dlb_refs, refs = refs[:nd], refs[nd:2 * nd], refs[2 * nd:3 * nd], refs[3 * nd:]
        (dgb_ref, doc_ref, dlc_ref, dgc_ref, gw_ref, gpost_ref, gsink_ref, loss_ref), refs = refs[:8], refs[8:]
        ycat, obufs, lbufs, st_do, st_l, st_d = refs[0], refs[1:nd], refs[nd:2 * nd - 1], refs[2 * nd - 1], refs[2 * nd], refs[2 * nd + 1]

        @pl.when(pl.program_id(0) == 0)
        def _():
            gw_ref[...] = jnp.zeros_like(gw_ref)
            gpost_ref[...] = jnp.zeros_like(gpost_ref)
            gsink_ref[...] = jnp.zeros_like(gsink_ref)
            loss_ref[...] = jnp.zeros_like(loss_ref)

        o_i, l_i = [ob_refs[0][0].astype(F32)], [lb_refs[0][0]]
        for k in range(1, nd):
            _from_residues(ob_refs[k], obufs[k - 1], B_DILS[k])
            _from_residues(lb_refs[k], lbufs[k - 1], B_DILS[k])
            o_i.append(_stage_read(obufs[k - 1]))
            l_i.append(_stage_read(lbufs[k - 1]))
        mx = l_i[0]
        for l in l_i[1:]:
            mx = jnp.maximum(mx, l)
        w_i = [jnp.exp(l - mx) for l in l_i]
        z = w_i[0]
        for w in w_i[1:]:
            z = z + w
        _stage_write(st_l, mx + jnp.log(z))
        expand = _head_expand_matrix(B_W)
        inv_z = 1.0 / z
        ob = None
        for w, o in zip(w_i, o_i):
            term = _dot_split(w * inv_z, expand, 2) * o
            ob = term if ob is None else ob + term
        oa, oc = oa_ref[...].astype(F32), oc_ref[...].astype(F32)
        sa, dsa = _silu_and_grad(ga_ref[...].astype(F32))
        sb, dsb = _silu_and_grad(gb_ref[...].astype(F32))
        sc, dsc = _silu_and_grad(gc_ref[...].astype(F32))
        ycat[:, 0:A_W] = (oa * sa).astype(BF16)
        ycat[:, A_W:A_W + B_W] = (ob * sb).astype(BF16)
        ycat[:, A_W + B_W:] = (oc * sc).astype(BF16)
        y2 = _dot(ycat[...], w_ref[...])
        r = lax.rsqrt(jnp.mean(y2 * y2, axis=-1, keepdims=True) + RMS_EPS)
        zhat = y2 * r
        gp = gp_ref[...]
        err = x_ref[...] + zhat * gp - t_ref[...]
        loss_ref[...] += jnp.sum(err * err) * (0.5 * inv_d)
        g = err * inv_d
        g_ref[...] = g
        gpost_ref[...] += jnp.sum(g * zhat, axis=0, keepdims=True)
        a = g * gp
        dy2 = (r * (a - zhat * jnp.mean(a * zhat, axis=-1, keepdims=True))).astype(BF16)
        gw_ref[...] += _dot_tn(ycat[...], dy2)
        dycat = _dot_nt(dy2, w_ref[...])
        dya, dyb, dyc = dycat[:, 0:A_W], dycat[:, A_W:A_W + B_W], dycat[:, A_W + B_W:]
        doa, dob, doc = dya * sa, dyb * sb, dyc * sc
        doa_ref[...] = doa.astype(BF16)
        doc_ref[...] = doc.astype(BF16)
        dga_ref[...] = (dya * oa * dsa).astype(BF16)
        dgb_ref[...] = (dyb * ob * dsb).astype(BF16)
        dgc_ref[...] = (dyc * oc * dsc).astype(BF16)
        dl_a = _dot_split(doa * oa, _head_sum_matrix(A_W), 2)
        dla_ref[...] = dl_a
        dlc_ref[...] = _dot_split(doc * oc, _head_sum_matrix(C_W), 2)
        gsink_ref[...] += jnp.sum(jnp.exp(sink_ref[...] - lsea_ref[...]) * dl_a, axis=0, keepdims=True)
        _stage_write(st_do, dob)
        _stage_write(st_d, _dot_split(dob * ob, _head_sum_matrix(B_W), 2))
        for k, dil in enumerate(B_DILS):
            _to_residues(st_do, dob_refs[k], dil)
            _to_residues(st_l, lsec_refs[k], dil)
            _to_residues(st_d, dlb_refs[k], dil)

    row = lambda w: pl.BlockSpec((tm, w), lambda i: (i, 0))
    full = lambda shape: pl.BlockSpec(shape, lambda i: (0,) * len(shape))
    res_specs = lambda w: [_residue_spec(d, tm, w) for d in B_DILS]
    res_shapes = lambda w, dt: [jax.ShapeDtypeStruct((d, seq // d, w), dt) for d in B_DILS]
    ins = [x, target, post_norm, w_out, sink_row, oa, lse_a, ga, *ob_list, *lseb_list, gb, oc, gc]
    in_specs = ([row(D_MODEL), row(D_MODEL), full((1, D_MODEL)), full((D_MODEL, D_MODEL)), full((1, LANES)),
                 row(A_W), row(LANES), row(A_W)] + res_specs(B_W) + res_specs(LANES) + [row(B_W), row(C_W), row(C_W)])
    out_shape = ([jax.ShapeDtypeStruct((seq, D_MODEL), F32), jax.ShapeDtypeStruct((seq, A_W), BF16),
                  jax.ShapeDtypeStruct((seq, LANES), F32), jax.ShapeDtypeStruct((seq, A_W), BF16)]
                 + res_shapes(B_W, BF16) + res_shapes(LANES, F32) + res_shapes(LANES, F32)
                 + [jax.ShapeDtypeStruct((seq, B_W), BF16), jax.ShapeDtypeStruct((seq, C_W), BF16),
                    jax.ShapeDtypeStruct((seq, LANES), F32), jax.ShapeDtypeStruct((seq, C_W), BF16),
                    jax.ShapeDtypeStruct((D_MODEL, D_MODEL), F32), jax.ShapeDtypeStruct((1, D_MODEL), F32),
                    jax.ShapeDtypeStruct((1, LANES), F32), jax.ShapeDtypeStruct((1, LANES), F32)])
    out_specs = ([row(D_MODEL), row(A_W), row(LANES), row(A_W)] + res_specs(B_W) + res_specs(LANES) + res_specs(LANES)
                 + [row(B_W), row(C_W), row(LANES), row(C_W),
                    full((D_MODEL, D_MODEL)), full((1, D_MODEL)), full((1, LANES)), full((1, LANES))])
    scratch = ([pltpu.VMEM((tm, D_MODEL), BF16)] + [_stage(tm, B_W)] * (nd - 1) + [_stage(tm, LANES)] * (nd - 1)
               + [_stage(tm, B_W), _stage(tm, LANES), _stage(tm, LANES)])
    res = pl.pallas_call(
        body, name="post", grid=(seq // tm,), in_specs=in_specs, out_specs=out_specs, out_shape=out_shape,
        scratch_shapes=scratch,
        compiler_params=pltpu.CompilerParams(dimension_semantics=("arbitrary",)),
    )(*ins)
    out = dict(g=res[0], doa=res[1], dl_a=res[2], dga=res[3], dob=res[4:4 + nd], lse_b=res[4 + nd:4 + 2 * nd],
               dl_b=res[4 + 2 * nd:4 + 3 * nd])
    rest = res[4 + 3 * nd:]
    out.update(dgb=rest[0], doc=rest[1], dl_c=rest[2], dgc=rest[3], gw_out=rest[4], gpost=rest[5], gsink=rest[6],
               loss=rest[7])
    return out


def _assemble_dproj(seq, nat, res):
    tm = min(256, seq)
    nd = len(B_DILS)
    nat_list = [nat[n] for n in _NATURAL]
    res_list = [a for n in _DILATED for a in res[n]]
    rope = _rope_tables(seq, tm)

    def body(rl_ref, rb_ref, *refs):
        nat_refs = dict(zip(_NATURAL, refs[:len(_NATURAL)]))
        refs = refs[len(_NATURAL):]
        res_refs = {n: refs[nd * k:nd * (k + 1)] for k, n in enumerate(_DILATED)}
        refs = refs[nd * len(_DILATED):]
        dproj_ref = refs[0]
        bufs = {n: refs[1 + (nd - 1) * k:1 + (nd - 1) * (k + 1)] for k, n in enumerate(_DILATED)}

        for n in _DILATED:
            for k in range(1, nd):
                _from_residues(res_refs[n][k], bufs[n][k - 1], B_DILS[k])
        c, sm, sp = _rope_coeffs(rl_ref, rb_ref)
        sm, sp = -sm, -sp
        for blk, (name, off, roped, scaled) in enumerate(_PROJ_LAYOUT):
            lanes = slice(off, off + LANES)
            if name in nat_refs:
                piece = nat_refs[name][:, lanes].astype(F32)
            else:
                piece = res_refs[name][0][0, :, lanes].astype(F32)
                for buf in bufs[name]:
                    piece = piece + buf[off // LANES]
            if roped:
                piece = _rope(piece, c, sm, sp)
            if scaled:
                piece = piece * SCALE
            dproj_ref[:, blk * LANES:(blk + 1) * LANES] = piece.astype(BF16)

    row = lambda w: pl.BlockSpec((tm, w), lambda i: (i, 0))
    in_specs = ([pl.BlockSpec(rope[0].shape, lambda i: (0, 0)), pl.BlockSpec((8, 2 * LANES), lambda i: (i, 0))]
                + [row(a.shape[1]) for a in nat_list]
                + [_residue_spec(d, tm, B_W) for _ in _DILATED for d in B_DILS])
    return pl.pallas_call(
        body, name="assemble_dproj", grid=(seq // tm,), in_specs=in_specs, out_specs=row(D_IN),
        out_shape=jax.ShapeDtypeStruct((seq, D_IN), BF16),
        scratch_shapes=[_stage(tm, B_W)] * ((nd - 1) * len(_DILATED)),
    )(*rope, *nat_list, *res_list)


def _input_grad(x, g, pre_norm, w_in_g, dproj, gx_prev, part, nparts, host, name):
    seq = x.shape[0]
    tm = min(256, seq)
    steps = seq // tm // nparts
    first_block = part * steps
    n_host_in = len(host["ins"]) if host else 0
    n_host_out = len(host["outs"]) if host else 0

    def body(*refs):
        x_ref, g_ref, gp_ref, w_ref, dp_ref = refs[:5]
        refs = refs[5 + (gx_prev is not None):]
        host_in, refs = refs[:n_host_in], refs[n_host_in:]
        gx_ref, gpre_ref = refs[:2]
        host_out, sems = refs[2:2 + n_host_out], refs[2 + n_host_out:]
        step = pl.program_id(0)

        @pl.when(step == 0)
        def _():
            gpre_ref[...] = jnp.zeros_like(gpre_ref)
            if host:
                for cp in host["build"](host_in, host_out, *sems):
                    cp.start()

        du = None
        for j in range(N_CHIPS):
            term = _dot_nt(dp_ref[:, j * SHARD_IN:(j + 1) * SHARD_IN], w_ref[j])
            du = term if du is None else du + term
        xv = x_ref[...]
        r = lax.rsqrt(jnp.mean(xv * xv, axis=-1, keepdims=True) + RMS_EPS)
        xhat = xv * r
        gpre_ref[...] += jnp.sum(du * xhat, axis=0, keepdims=True)
        a = du * gp_ref[...]
        gx_ref[...] = g_ref[...] + r * (a - xhat * jnp.mean(a * xhat, axis=-1, keepdims=True))

        if host:
            @pl.when(step == steps - 1)
            def _():
                for cp in host["build"](host_in, host_out, *sems):
                    cp.wait()

    row = lambda w: pl.BlockSpec((tm, w), lambda i: (first_block + i, 0))
    full = lambda a: pl.BlockSpec(a.shape, lambda i: (0,) * a.ndim)
    any_spec = pl.BlockSpec(memory_space=pl.ANY)
    ins = [x, g, pre_norm, w_in_g, dproj]
    in_specs = [row(D_MODEL), row(D_MODEL), full(pre_norm), full(w_in_g), row(D_IN)]
    aliases = {}
    if gx_prev is not None:
        aliases[len(ins)] = 0
        ins.append(gx_prev)
        in_specs.append(any_spec)
    out_shape = [jax.ShapeDtypeStruct((seq, D_MODEL), F32), jax.ShapeDtypeStruct((1, D_MODEL), F32)]
    out_specs = [row(D_MODEL), pl.BlockSpec((1, D_MODEL), lambda i: (0, 0))]
    scratch = []
    if host:
        ins += list(host["ins"])
        in_specs += [any_spec] * n_host_in
        out_shape += list(host["outs"])
        out_specs += [any_spec] * n_host_out
        scratch = list(host["sems"])
    return pl.pallas_call(
        body, name=name, grid=(steps,), in_specs=in_specs, out_specs=out_specs, out_shape=out_shape,
        input_output_aliases=aliases, scratch_shapes=scratch,
        compiler_params=pltpu.CompilerParams(dimension_semantics=("arbitrary",)),
    )(*ins)


def _grad_w_in(ut, dproj):
    seq = ut.shape[1]
    tk = min(1024, seq)

    def body(ut_ref, dp_ref, out_ref):
        @pl.when(pl.program_id(1) == 0)
        def _():
            out_ref[...] = jnp.zeros_like(out_ref)

        out_ref[...] += _dot(ut_ref[...], dp_ref[...])

    return pl.pallas_call(
        body, name="grad_w_in", grid=(N_CHIPS, seq // tk),
        in_specs=[pl.BlockSpec((D_MODEL, tk), lambda j, i: (0, i)), pl.BlockSpec((tk, SHARD_IN), lambda j, i: (i, j))],
        out_specs=pl.BlockSpec((None, D_MODEL, SHARD_IN), lambda j, i: (j, 0, 0)),
        out_shape=jax.ShapeDtypeStruct((N_CHIPS, D_MODEL, SHARD_IN), F32),
        compiler_params=pltpu.CompilerParams(dimension_semantics=("arbitrary", "arbitrary")),
    )(ut, dproj)


def _pair_exchange(grads):
    n = len(grads)

    def build(srcs, outs, send_sems, recv_sems):
        x, y, c = lax.axis_index("x"), lax.axis_index("y"), lax.axis_index("c")
        copies = []
        for t in range(n):
            rows = grads[t].shape[1] // 2
            copies.append(pltpu.make_async_remote_copy(
                src_ref=srcs[t].at[:, pl.ds((1 - c) * rows, rows)], dst_ref=outs[t],
                send_sem=send_sems.at[t], recv_sem=recv_sems.at[t], device_id=(x, y, 1 - c), device_id_type=MESH))
        return copies

    return dict(ins=list(grads), build=build,
                outs=[jax.ShapeDtypeStruct((g.shape[0], g.shape[1] // 2, g.shape[2]), g.dtype) for g in grads],
                sems=[pltpu.SemaphoreType.DMA((n,)), pltpu.SemaphoreType.DMA((n,))])


def _pair_add(core, own, got):
    nchip, rows2, width = own.shape
    rows = rows2 // 2
    tr = min(512, rows)
    nb = rows // tr

    def body(core_ref, own_ref, got_ref, out_ref):
        out_ref[...] = (own_ref[...] + got_ref[...]).astype(BF16)

    grid_spec = pltpu.PrefetchScalarGridSpec(
        num_scalar_prefetch=1, grid=(nchip, nb),
        in_specs=[pl.BlockSpec((None, tr, width), lambda k, i, core_ref: (k, core_ref[0] * nb + i, 0)),
                  pl.BlockSpec((None, tr, width), lambda k, i, core_ref: (k, i, 0))],
        out_specs=pl.BlockSpec((None, tr, width), lambda k, i, core_ref: (k, i, 0)))
    return pl.pallas_call(
        body, name=f"pair_add_{width}", grid_spec=grid_spec,
        out_shape=jax.ShapeDtypeStruct((nchip, rows, width), BF16),
    )(core, own, got)


def _chip_exchange(parts):
    n = len(parts)

    def build(srcs, outs, send_sems, recv_sems, local_sems):
        x, y, c = lax.axis_index("x"), lax.axis_index("y"), lax.axis_index("c")
        my_chip = 2 * x + y
        chips = [(1 - x, y), (x, 1 - y), (1 - x, 1 - y)]
        copies = [pltpu.make_async_copy(srcs[t].at[my_chip], outs[t].at[my_chip], local_sems.at[t]) for t in range(n)]
        for j, (cx, cy) in enumerate(chips):
            for t in range(n):
                k = n * j + t
                copies.append(pltpu.make_async_remote_copy(
                    src_ref=srcs[t].at[2 * cx + cy], dst_ref=outs[t].at[my_chip], send_sem=send_sems.at[k],
                    recv_sem=recv_sems.at[k], device_id=(cx, cy, c), device_id_type=MESH))
        return copies

    return dict(ins=list(parts), build=build, outs=[jax.ShapeDtypeStruct(p.shape, p.dtype) for p in parts],
                sems=[pltpu.SemaphoreType.DMA((3 * n,)), pltpu.SemaphoreType.DMA((3 * n,)),
                      pltpu.SemaphoreType.DMA((n,))])


def _slot_sum(slots, name, core=None):
    ns, rows, width = slots.shape
    tr = min(512, rows)

    def body(*refs):
        in_ref, out_ref = refs[-2:]
        acc = in_ref[0].astype(F32)
        for s in range(1, ns):
            acc = acc + in_ref[s].astype(F32)
        out_ref[...] = acc

    if core is None:
        return pl.pallas_call(
            body, name=name, grid=(rows // tr,),
            in_specs=[pl.BlockSpec((ns, tr, width), lambda i: (0, i, 0))],
            out_specs=pl.BlockSpec((tr, width), lambda i: (i, 0)),
            out_shape=jax.ShapeDtypeStruct((rows, width), F32),
        )(slots)
    grid_spec = pltpu.PrefetchScalarGridSpec(
        num_scalar_prefetch=1, grid=(rows // tr,),
        in_specs=[pl.BlockSpec((ns, tr, width), lambda i, core_ref: (0, i, 0))],
        out_specs=pl.BlockSpec((None, tr, width), lambda i, core_ref: (core_ref[0], i, 0)))
    return pl.pallas_call(
        body, name=name, grid_spec=grid_spec, out_shape=jax.ShapeDtypeStruct((2, rows, width), F32),
    )(core, slots)


def _pair_gather(bufs, small):
    n = len(bufs)

    def body(*refs):
        small_ref, outs, small_out = refs[n], refs[n + 1:2 * n + 1], refs[2 * n + 1]
        send_sems, recv_sems, local_sem = refs[2 * n + 2:]
        x, y, c = lax.axis_index("x"), lax.axis_index("y"), lax.axis_index("c")
        me = 4 * x + 2 * y + c
        chips = [(1 - x, y), (x, 1 - y), (1 - x, 1 - y)]
        mine = pltpu.make_async_copy(small_ref, small_out.at[me], local_sem)
        mine.start()
        copies = [pltpu.make_async_remote_copy(
            src_ref=outs[t].at[c], dst_ref=outs[t].at[c], send_sem=send_sems.at[t], recv_sem=recv_sems.at[t],
            device_id=(x, y, 1 - c), device_id_type=MESH) for t in range(n)]
        peers = [(x, y, 1 - c)] + [(cx, cy, cc) for (cx, cy) in chips for cc in (c, 1 - c)]
        for j, peer in enumerate(peers):
            copies.append(pltpu.make_async_remote_copy(
                src_ref=small_ref, dst_ref=small_out.at[me], send_sem=send_sems.at[n + j],
                recv_sem=recv_sems.at[n + j], device_id=peer, device_id_type=MESH))
        for cp in copies:
            cp.start()
        for cp in copies:
            cp.wait()
        mine.wait()

    any_spec = pl.BlockSpec(memory_space=pl.ANY)
    res = pl.pallas_call(
        body, name="pair_gather",
        out_shape=[jax.ShapeDtypeStruct(b.shape, b.dtype) for b in bufs]
        + [jax.ShapeDtypeStruct((8,) + small.shape, small.dtype)],
        in_specs=[any_spec] * (n + 1), out_specs=[any_spec] * (n + 1),
        input_output_aliases={t: t for t in range(n)},
        scratch_shapes=[pltpu.SemaphoreType.DMA((n + 7,)), pltpu.SemaphoreType.DMA((n + 7,)),
                        pltpu.SemaphoreType.DMA],
    )(*bufs, small)
    return [r.reshape(2 * b.shape[1], b.shape[2]) for r, b in zip(res[:n], bufs)], res[n]


def _adamw(w, g, m, v, name):
    rows, width = w.shape
    tr = min(256, rows)
    c1 = 1.0 / (1.0 - ADAM_B1 ** ADAM_STEP)
    c2 = 1.0 / (1.0 - ADAM_B2 ** ADAM_STEP)

    def body(w_ref, g_ref, m_ref, v_ref, d_ref, nm_ref, nv_ref):
        gv = g_ref[...]
        nm = ADAM_B1 * m_ref[...] + (1.0 - ADAM_B1) * gv
        nv = ADAM_B2 * v_ref[...] + (1.0 - ADAM_B2) * (gv * gv)
        nm_ref[...] = nm
        nv_ref[...] = nv
        d_ref[...] = -ADAM_LR * ((nm * c1) / (jnp.sqrt(nv * c2) + ADAM_EPS) + ADAM_WD * w_ref[...])

    spec = pl.BlockSpec((tr, width), lambda i: (i, 0))
    return pl.pallas_call(
        body, name=name, grid=(rows // tr,), in_specs=[spec] * 4, out_specs=[spec] * 3,
        out_shape=[jax.ShapeDtypeStruct(w.shape, F32)] * 3,
    )(w, g, m, v)


def _local_step(x, mem, target, pre_norm, sink_a, mem_norm, post_norm, w_in_g, w_out, w_mkv):
    mk, mv = _mem_kv(mem, mem_norm, w_mkv)
    pr = _pre_proj(x, pre_norm, w_in_g)
    sink = sink_a.reshape(-1)
    qa, ka, va = pr["qa"][None], pr["ka"][None], pr["va"][None]
    oa, lse_a = _band_fwd(qa, ka, va, sink, max_dist=A_WINDOW - 1, name="swa_fwd")
    ob_list, lseb_list = [], []
    for k, (win, dil) in enumerate(B_CONFIGS):
        o_i, l_i = _band_fwd(pr["qb"][k], pr["kb"][k], pr["vb"][k], None, max_dist=win // dil, name=f"dil{dil}_fwd")
        ob_list.append(o_i)
        lseb_list.append(l_i)
    oc, lse_c = _mem_attn_fwd(pr["qc"], mk, mv)
    sink_row = jnp.pad(sink, (0, LANES - sink.shape[0])).reshape(1, LANES)
    po = _post(x, target, post_norm, w_out, sink_row, oa[0], lse_a[0], pr["ga"], ob_list, lseb_list, pr["gb"], oc,
               pr["gc"])
    dqc, dmk, dmv = _mem_attn_bwd(pr["qc"], mk, mv, po["doc"], lse_c, po["dl_c"])
    dqa, dka, dva = _band_bwd(qa, ka, va, po["doa"][None], lse_a, po["dl_a"][None], max_dist=A_WINDOW - 1,
                              name="swa_bwd")
    res = dict(qb=[], kb=[], vb=[])
    for k, (win, dil) in enumerate(B_CONFIGS):
        dq_i, dk_i, dv_i = _band_bwd(pr["qb"][k], pr["kb"][k], pr["vb"][k], po["dob"][k], po["lse_b"][k],
                                     po["dl_b"][k], max_dist=win // dil, name=f"dil{dil}_bwd")
        res["qb"].append(dq_i)
        res["kb"].append(dk_i)
        res["vb"].append(dv_i)
    nat = dict(qa=dqa[0], ka=dka[0], va=dva[0], ga=po["dga"], gb=po["dgb"], qc=dqc, gc=po["dgc"])
    dproj = _assemble_dproj(x.shape[0], nat, res)
    gw_in = _grad_w_in(pr["ut"], dproj)
    gw_mkv, gmem = _mem_kv_bwd(mem, mem_norm, w_mkv, dmk, dmv)
    gsink = -po["gsink"][0, :sink.shape[0]]
    return dict(loss=po["loss"], g=po["g"], dproj=dproj, gw_in=gw_in, gw_out=po["gw_out"], gw_mkv=gw_mkv,
                gpost=po["gpost"], gmem=gmem, gsink=gsink)


def kernel(x, mem, pre_norm, w_in, sink_a, mem_norm, w_mem_kv, w_out, post_norm, loss_target, m_pre_norm, m_w_in, m_sink_a, m_mem_norm, m_w_mem_kv, m_w_out, m_post_norm, v_pre_norm, v_w_in, v_sink_a, v_mem_norm, v_w_mem_kv, v_w_out, v_post_norm):
    w_in_g, w_out_g, w_mkv_g = _gather_weights(w_in[0].astype(BF16), w_out[0].astype(BF16), w_mem_kv[0].astype(BF16))
    loc = _local_step(x[0], mem[0], loss_target[0], pre_norm, sink_a, mem_norm, post_norm,
                      w_in_g.reshape(N_CHIPS, D_MODEL, SHARD_IN), w_out_g.reshape(D_MODEL, D_MODEL),
                      w_mkv_g.reshape(D_MODEL, 2 * C_W))
    big = [loc["gw_in"], loc["gw_out"].reshape(N_CHIPS, D_MODEL // N_CHIPS, D_MODEL),
           loc["gw_mkv"].reshape(N_CHIPS, D_MODEL // N_CHIPS, 2 * C_W)]
    core = lax.axis_index("c").astype(jnp.int32).reshape(1)
    w_in_full = w_in_g.reshape(N_CHIPS, D_MODEL, SHARD_IN)
    gx_a, gpre_a, *got = _input_grad(x[0], loc["g"], pre_norm, w_in_full, loc["dproj"], None, 0, 2,
                                     _pair_exchange(big), "input_grad_a")
    parts = [_pair_add(core, own, g) for own, g in zip(big, got)]
    grad_x, gpre_b, *slots = _input_grad(x[0], loc["g"], pre_norm, w_in_full, loc["dproj"], gx_a, 1, 2,
                                         _chip_exchange(parts), "input_grad_b")
    halves = [_slot_sum(s, name=f"chip_sum_{s.shape[2]}", core=core) for s in slots]
    widen = lambda a: jnp.pad(a.reshape(1, -1), ((0, 0), (0, D_MODEL - a.size)))
    small = jnp.concatenate([gpre_a, loc["gpost"], loc["gmem"], widen(loc["gsink"]), widen(loc["loss"]), gpre_b,
                             jnp.zeros((2, D_MODEL), F32)], axis=0)
    (g_in, g_out, g_mkv), small_slots = _pair_gather(halves, small)
    small_sum = _slot_sum(small_slots, name="device_sum")
    g_pre, g_post, g_mem = small_sum[0:1] + small_sum[5:6], small_sum[1:2], small_sum[2:3]
    g_sink = small_sum[3:4, :sink_a.shape[1]]
    loss = small_sum[4, 0]

    d_in, nm_in, nv_in = _adamw(w_in[0], g_in, m_w_in[0], v_w_in[0], "adamw_in")
    d_out, nm_out, nv_out = _adamw(w_out[0], g_out, m_w_out[0], v_w_out[0], "adamw_out")
    d_mkv, nm_mkv, nv_mkv = _adamw(w_mem_kv[0], g_mkv, m_w_mem_kv[0], v_w_mem_kv[0], "adamw_mkv")
    pad6 = lambda a: jnp.pad(a, ((0, 0), (0, D_MODEL - a.shape[1])))
    stack = lambda a, b, c_, d_: jnp.concatenate([a, b, c_, pad6(d_), jnp.zeros((4, D_MODEL), F32)], axis=0)
    d_s, nm_s, nv_s = _adamw(stack(pre_norm, post_norm, mem_norm, sink_a),
                             jnp.concatenate([g_pre, small_sum[1:]], axis=0),
                             stack(m_pre_norm, m_post_norm, m_mem_norm, m_sink_a),
                             stack(v_pre_norm, v_post_norm, v_mem_norm, v_sink_a), "adamw_small")
    ns_ = sink_a.shape[1]
    unpack = lambda a: (a[0:1], a[3:4, :ns_], a[2:3], a[1:2])
    d_pre, d_sink, d_mem, d_post = unpack(d_s)
    nm_pre, nm_sink, nm_mem, nm_post = unpack(nm_s)
    nv_pre, nv_sink, nv_mem, nv_post = unpack(nv_s)
    lead = lambda a: a[None]
    return (loss, lead(grad_x),
            g_pre, lead(g_in), g_sink, g_mem, lead(g_mkv), lead(g_out), g_post,
            d_pre, lead(d_in), d_sink, d_mem, lead(d_mkv), lead(d_out), d_post,
            nm_pre, lead(nm_in), nm_sink, nm_mem, lead(nm_mkv), lead(nm_out), nm_post,
            nv_pre, lead(nv_in), nv_sink, nv_mem, lead(nv_mkv), lead(nv_out), nv_post)
```

```python
import numpy as np
import jax
import jax.numpy as jnp
from jax import lax
from jax.experimental import pallas as pl
from jax.experimental.pallas import tpu as pltpu

F32 = jnp.float32
BF16 = jnp.bfloat16

D_MODEL = 1024
HEAD_DIM = 64
LANES = 128
BLOCK = 128
A_W, A_KV_W, B_W, C_W = 384, 128, 384, 256
N_MEM = 256
D_IN = 3072
N_CHIPS = 4
SHARD_IN = D_IN // N_CHIPS
B_CONFIGS = ((128, 1), (512, 4), (2048, 16))
B_DILS = tuple(d for _, d in B_CONFIGS)
A_WINDOW = 128
RMS_EPS = 1e-6
ROPE_THETA = 500000.0
SCALE = HEAD_DIM ** -0.5
NEG = -1e30
ADAM_LR, ADAM_B1, ADAM_B2, ADAM_EPS, ADAM_WD, ADAM_STEP = 0.001, 0.9, 0.999, 1e-08, 0.01, 10

NT = (((1,), (1,)), ((), ()))
TN = (((0,), (0,)), ((), ()))
MESH = pl.DeviceIdType.MESH

_PROJ_LAYOUT = (
    [("qa", 128 * i, True, True) for i in range(3)] + [("ka", 0, True, False), ("va", 0, False, False)]
    + [("ga", 128 * i, False, False) for i in range(3)]
    + [("qb", 128 * i, True, True) for i in range(3)] + [("kb", 128 * i, True, False) for i in range(3)]
    + [("vb", 128 * i, False, False) for i in range(3)] + [("gb", 128 * i, False, False) for i in range(3)]
    + [("qc", 128 * i, False, True) for i in range(2)] + [("gc", 128 * i, False, False) for i in range(2)]
)
_PROJ_WIDTH = dict(qa=A_W, ka=A_KV_W, va=A_KV_W, ga=A_W, qb=B_W, kb=B_W, vb=B_W, gb=B_W, qc=C_W, gc=C_W)
_NATURAL = ("qa", "ka", "va", "ga", "gb", "qc", "gc")
_DILATED = ("qb", "kb", "vb")


def _dot(a, b):
    return jnp.dot(a, b, preferred_element_type=F32)


def _dot_nt(a, b):
    return lax.dot_general(a, b, NT, preferred_element_type=F32)


def _dot_tn(a, b):
    return lax.dot_general(a, b, TN, preferred_element_type=F32)


def _half_masks(rows):
    lane = lax.broadcasted_iota(jnp.int32, (rows, LANES), 1)
    return lane < HEAD_DIM, lane >= HEAD_DIM


def _rope(t, c, sm, sp):
    return t * c + pltpu.roll(t, LANES - 8, 1) * sm + pltpu.roll(t, 8, 1) * sp


def _rope_tables(seq, tm):
    dim = jnp.arange(LANES) % HEAD_DIM
    inv_freq = ROPE_THETA ** (-jnp.arange(0, 16, 2, dtype=F32) / 16)
    freq = jnp.where(dim < 16, inv_freq[dim % 8], 0.0)[None, :]
    local = jnp.arange(tm, dtype=F32)[:, None] * freq
    base = (jnp.arange(seq // tm, dtype=F32) * tm)[:, None] * freq
    both = lambda a: jnp.concatenate([jnp.cos(a), jnp.sin(a)], axis=1)
    return both(local), jnp.repeat(both(base), 8, axis=0)


def _rope_coeffs(local_ref, base_ref):
    cl, sl = local_ref[:, :LANES], local_ref[:, LANES:]
    cb, sb = base_ref[0:1, :LANES], base_ref[0:1, LANES:]
    cos = cb * cl - sb * sl
    sin = sb * cl + cb * sl
    dim = lax.broadcasted_iota(jnp.int32, (1, LANES), 1) % HEAD_DIM
    return cos, jnp.where(dim < 8, -sin, 0.0), jnp.where((dim >= 8) & (dim < 16), sin, 0.0)


def _split3(x):
    a = x.astype(BF16)
    r = x - a.astype(F32)
    b = r.astype(BF16)
    c = (r - b.astype(F32)).astype(BF16)
    return a, b, c


def _rows_to_lanes(x):
    row = lax.broadcasted_iota(jnp.int32, (8, LANES), 0)
    lane = lax.broadcasted_iota(jnp.int32, (8, LANES), 1)
    eye = (row == lane).astype(BF16)
    a, b, c = _split3(x)
    return _dot_nt(eye, a) + _dot_nt(eye, b) + _dot_nt(eye, c)


def _head_sum_matrix(width):
    k = lax.broadcasted_iota(jnp.int32, (width, LANES), 0)
    h = lax.broadcasted_iota(jnp.int32, (width, LANES), 1)
    return (k // HEAD_DIM == h).astype(BF16)


def _head_expand_matrix(width):
    h = lax.broadcasted_iota(jnp.int32, (LANES, width), 0)
    k = lax.broadcasted_iota(jnp.int32, (LANES, width), 1)
    return (k // HEAD_DIM == h).astype(BF16)


def _dot_split(x, mat, terms):
    parts = _split3(x)[:terms]
    out = _dot(parts[0], mat)
    for p in parts[1:]:
        out = out + _dot(p, mat)
    return out


def _per_head(cols, fill=0.0):
    rows = cols[0].shape[0]
    lane = lax.broadcasted_iota(jnp.int32, (rows, LANES), 1)
    out = jnp.full((rows, LANES), fill, F32)
    for h, col in enumerate(cols):
        out = jnp.where(lane == h, col, out)
    return out


def _lane_blocks(width):
    return [slice(p * LANES, (p + 1) * LANES) for p in range(width // LANES)]


def _stage(rows, width):
    return pltpu.VMEM((width // LANES, rows, LANES), F32)


def _stage_write(buf, value):
    for p, lanes in enumerate(_lane_blocks(value.shape[1])):
        buf[p] = value[:, lanes]


def _stage_read(buf):
    return jnp.concatenate([buf[p] for p in range(buf.shape[0])], axis=1) if buf.shape[0] > 1 else buf[0]


def _to_residues(buf, out_ref, dil):
    rows = buf.shape[1] // dil
    for r in range(dil):
        for p in range(buf.shape[0]):
            plane = buf.at[p]
            out_ref[r, :, p * LANES:(p + 1) * LANES] = plane[pl.ds(r, rows, stride=dil), :].astype(out_ref.dtype)


def _from_residues(in_ref, buf, dil):
    rows = buf.shape[1] // dil
    for r in range(dil):
        for p in range(buf.shape[0]):
            plane = buf.at[p]
            plane[pl.ds(r, rows, stride=dil), :] = in_ref[r, :, p * LANES:(p + 1) * LANES].astype(F32)


def _residue_spec(dil, tm, width):
    return pl.BlockSpec((dil, tm // dil, width), lambda i: (0, i, 0))


def _gather_weights(w_in_s, w_out_s, w_mkv_s):
    shards = tuple(s.reshape(2, s.shape[0] // 2, s.shape[1]) for s in (w_in_s, w_out_s, w_mkv_s))
    n = len(shards)

    def body(*refs):
        srcs, outs = refs[:n], refs[2 * n:3 * n]
        send_sems, recv_sems = refs[3 * n:]
        x, y, c = lax.axis_index("x"), lax.axis_index("y"), lax.axis_index("c")
        my_chip = 2 * x + y
        sibling = (x, y, 1 - c)
        chips = [(1 - x, y), (x, 1 - y), (1 - x, 1 - y)]

        def half(t, chip, which):
            return outs[t].at[chip, which]

        def src_half(t, which):
            return srcs[t].at[which]

        def copy(k, src, dst, to):
            return pltpu.make_async_remote_copy(src_ref=src, dst_ref=dst, send_sem=send_sems.at[k],
                                                recv_sem=recv_sems.at[k], device_id=to, device_id_type=MESH)

        first = []
        for j, (cx, cy) in enumerate(chips):
            for t in range(n):
                first.append(copy(n * j + t, src_half(t, c), half(t, my_chip, c), (cx, cy, c)))
        for cp in first:
            cp.start()
        passed = []
        for j, (cx, cy) in enumerate(chips):
            chip = 2 * cx + cy
            for t in range(n):
                k = n * j + t
                copy(k, src_half(t, c), half(t, chip, c), (cx, cy, c)).wait_recv()
                fwd = copy(n * 3 + k, half(t, chip, c), half(t, chip, c), sibling)
                fwd.start()
                passed.append(fwd)
        for j, (cx, cy) in enumerate(chips):
            chip = 2 * cx + cy
            for t in range(n):
                k = n * 3 + n * j + t
                copy(k, half(t, chip, 1 - c), half(t, chip, 1 - c), sibling).wait_recv()
        for cp in first + passed:
            cp.wait_send()

    my_chip = 2 * lax.axis_index("x") + lax.axis_index("y")
    landing = [lax.dynamic_update_slice(jnp.zeros((N_CHIPS,) + s.shape, s.dtype), s[None], (my_chip, 0, 0, 0))
               for s in shards]
    any_spec = pl.BlockSpec(memory_space=pl.ANY)
    return pl.pallas_call(
        body, name="gather_weights",
        out_shape=[jax.ShapeDtypeStruct((N_CHIPS,) + s.shape, s.dtype) for s in shards],
        in_specs=[any_spec] * (2 * n), out_specs=[any_spec] * n,
        input_output_aliases={n + t: t for t in range(n)},
        scratch_shapes=[pltpu.SemaphoreType.DMA((6 * n,)), pltpu.SemaphoreType.DMA((6 * n,))],
    )(*shards, *landing)


def _mem_kv(mem, mem_norm, w_mkv):
    def body(mem_ref, g_ref, w_ref, mk_ref, mv_ref):
        m = mem_ref[...]
        r = lax.rsqrt(jnp.mean(m * m, axis=-1, keepdims=True) + RMS_EPS)
        mn = (m * r * g_ref[...]).astype(BF16)
        kv = _dot(mn, w_ref[...])
        mk_ref[...] = kv[:, :C_W].astype(BF16)
        mv_ref[...] = kv[:, C_W:].astype(BF16)

    return pl.pallas_call(
        body, name="mem_kv",
        out_shape=[jax.ShapeDtypeStruct((N_MEM, C_W), BF16)] * 2,
    )(mem, mem_norm, w_mkv)


def _mem_kv_bwd(mem, mem_norm, w_mkv, dmk, dmv):
    def body(mem_ref, g_ref, w_ref, dmk_ref, dmv_ref, gw_ref, gn_ref):
        m = mem_ref[...]
        r = lax.rsqrt(jnp.mean(m * m, axis=-1, keepdims=True) + RMS_EPS)
        mhat = m * r
        mn = (mhat * g_ref[...]).astype(BF16)
        dkv = jnp.concatenate([dmk_ref[...], dmv_ref[...]], axis=1).astype(BF16)
        gw_ref[...] = _dot_tn(mn, dkv)
        dmn = _dot_nt(dkv, w_ref[...])
        gn_ref[...] = jnp.sum(dmn * mhat, axis=0, keepdims=True)

    return pl.pallas_call(
        body, name="mem_kv_bwd",
        out_shape=[jax.ShapeDtypeStruct((D_MODEL, 2 * C_W), F32), jax.ShapeDtypeStruct((1, D_MODEL), F32)],
    )(mem, mem_norm, w_mkv, dmk, dmv)


def _pre_proj(x, pre_norm, w_in_g):
    seq = x.shape[0]
    tm = min(512, seq)
    n_nat, n_dil = len(_NATURAL), len(_DILATED) * len(B_DILS)
    rope = _rope_tables(seq, tm)

    def body(x_ref, g_ref, w_ref, rl_ref, rb_ref, *refs):
        nat = dict(zip(_NATURAL, refs[:n_nat]))
        res = {n: refs[n_nat + len(B_DILS) * k:n_nat + len(B_DILS) * (k + 1)] for k, n in enumerate(_DILATED)}
        ut = refs[n_nat + n_dil]
        bufs = dict(zip(_DILATED, refs[n_nat + n_dil + 1:]))
        xv = x_ref[...]
        r = lax.rsqrt(jnp.mean(xv * xv, axis=-1, keepdims=True) + RMS_EPS)
        u = xv * r * g_ref[...]
        ub = u.astype(BF16)
        ut[...] = u.T.astype(BF16)
        c, sm, sp = _rope_coeffs(rl_ref, rb_ref)
        for j in range(N_CHIPS):
            pj = _dot(ub, w_ref[j])
            for b in range(SHARD_IN // LANES):
                name, off, roped, scaled = _PROJ_LAYOUT[(SHARD_IN // LANES) * j + b]
                piece = pj[:, LANES * b:LANES * (b + 1)]
                if roped:
                    piece = _rope(piece, c, sm, sp)
                if scaled:
                    piece = piece * SCALE
                if name in bufs:
                    bufs[name][off // LANES] = piece
                else:
                    nat[name][:, off:off + LANES] = piece.astype(BF16)
        for name in _DILATED:
            for ref, dil in zip(res[name], B_DILS):
                _to_residues(bufs[name], ref, dil)

    row = lambda w: pl.BlockSpec((tm, w), lambda i: (i, 0))
    full = lambda a: pl.BlockSpec(a.shape, lambda i: (0,) * a.ndim)
    out_shape = [jax.ShapeDtypeStruct((seq, _PROJ_WIDTH[n]), BF16) for n in _NATURAL]
    out_specs = [row(_PROJ_WIDTH[n]) for n in _NATURAL]
    for n in _DILATED:
        for dil in B_DILS:
            out_shape.append(jax.ShapeDtypeStruct((dil, seq // dil, B_W), BF16))
            out_specs.append(_residue_spec(dil, tm, B_W))
    out_shape.append(jax.ShapeDtypeStruct((D_MODEL, seq), BF16))
    out_specs.append(pl.BlockSpec((D_MODEL, tm), lambda i: (0, i)))
    res = pl.pallas_call(
        body, name="pre_proj", grid=(seq // tm,),
        in_specs=[row(D_MODEL), full(pre_norm), full(w_in_g), full(rope[0]), pl.BlockSpec((8, 2 * LANES), lambda i: (i, 0))],
        out_specs=out_specs, out_shape=out_shape,
        scratch_shapes=[_stage(tm, B_W)] * len(_DILATED),
    )(x, pre_norm, w_in_g, *rope)
    out = dict(zip(_NATURAL, res[:n_nat]))
    for k, n in enumerate(_DILATED):
        out[n] = res[n_nat + len(B_DILS) * k:n_nat + len(B_DILS) * (k + 1)]
    out["ut"] = res[n_nat + n_dil]
    return out


def _band_bias(max_dist, transposed):
    i = np.arange(BLOCK)[:, None]
    j = np.arange(BLOCK)[None, :]
    if transposed:
        same = i <= j
        other = (j + BLOCK - i) <= max_dist
        vis = np.concatenate([same, other], axis=1)
    else:
        prev = (i + BLOCK - j) <= max_dist
        same = j <= i
        vis = np.concatenate([prev, same], axis=1)
    return jnp.asarray(np.where(vis, 0.0, NEG).astype(np.float32))


def _kv_place(h, gqa):
    return (0, h // 3) if gqa else (h // 2, h % 2)


def _band_fwd(q, k, v, sink, *, max_dist, name):
    dil, length, wq = q.shape
    wk = k.shape[2]
    gqa = wk != wq
    tq = min(512, length)
    ns, nt = tq // BLOCK, length // tq
    npair = wq // LANES
    bias = _band_bias(max_dist, transposed=False)
    has_sink = sink is not None

    def body(*refs):
        if has_sink:
            sink_ref, refs = refs[0], refs[1:]
        q_ref, k_ref, kp_ref, v_ref, vp_ref, bias_ref, o_ref, lse_ref, kbuf, vbuf = refs[:10]
        i = pl.program_id(1)
        kbuf[0:BLOCK] = kp_ref[...]
        kbuf[BLOCK:] = k_ref[...]
        vbuf[0:BLOCK] = vp_ref[...]
        vbuf[BLOCK:] = v_ref[...]
        if gqa:
            kroll, vroll = refs[10:12]
            kroll[...] = pltpu.roll(kbuf[...], HEAD_DIM, 1)
            vroll[...] = pltpu.roll(vbuf[...], HEAD_DIM, 1)
        half = _half_masks(BLOCK)
        col_prev = (lax.broadcasted_iota(jnp.int32, (1, 2 * BLOCK), 1) < BLOCK).astype(F32)

        def sub(a, carry):
            r0 = pl.multiple_of(a * BLOCK, BLOCK)
            pen = jnp.where((i == 0) & (a == 0), NEG, 0.0)
            b = bias_ref[...] + pen * col_prev
            scores = []
            for p in range(npair):
                qp = q_ref[pl.ds(r0, BLOCK), p * LANES:(p + 1) * LANES]
                for e in range(2):
                    pk, ek = _kv_place(2 * p + e, gqa)
                    kw = (kbuf if ek == e else kroll)[pl.ds(r0, 2 * BLOCK), pk * LANES:(pk + 1) * LANES]
                    scores.append(_dot_nt(jnp.where(half[e], qp, jnp.zeros_like(qp)), kw))
            m_cols, l_cols, probs = [], [], []
            for h, s in enumerate(scores):
                s = s + b
                m = jnp.max(s, axis=1, keepdims=True)
                if has_sink:
                    m = jnp.maximum(m, sink_ref[h])
                pe = jnp.exp(s - m)
                l = jnp.sum(pe, axis=1, keepdims=True)
                if has_sink:
                    l = l + jnp.exp(sink_ref[h] - m)
                probs.append(pe.astype(BF16))
                m_cols.append(m)
                l_cols.append(l)
            for p in range(npair):
                o_h = []
                for e in range(2):
                    h = 2 * p + e
                    pk, ek = _kv_place(h, gqa)
                    vw = (vbuf if ek == e else vroll)[pl.ds(r0, 2 * BLOCK), pk * LANES:(pk + 1) * LANES]
                    o_h.append(_dot(probs[h], vw) * (1.0 / l_cols[h]))
                o_ref[pl.ds(r0, BLOCK), p * LANES:(p + 1) * LANES] = jnp.where(half[0], o_h[0], o_h[1]).astype(BF16)
            lse_ref[pl.ds(r0, BLOCK), :] = _per_head(m_cols) + jnp.log(_per_head(l_cols, 1.0))
            return carry

        lax.fori_loop(0, ns, sub, 0, unroll=True)

    main = lambda w: pl.BlockSpec((None, tq, w), lambda r, i: (r, i, 0))
    prev = lambda w: pl.BlockSpec((None, BLOCK, w), lambda r, i: (r, jnp.maximum(i * ns - 1, 0), 0))
    in_specs = [main(wq), main(wk), prev(wk), main(wk), prev(wk), pl.BlockSpec(bias.shape, lambda r, i: (0, 0))]
    args = [q, k, k, v, v, bias]
    if has_sink:
        in_specs = [pl.BlockSpec(memory_space=pltpu.SMEM)] + in_specs
        args = [sink] + args
    scratch = [pltpu.VMEM((tq + BLOCK, wk), BF16)] * (4 if gqa else 2)
    return pl.pallas_call(
        body, name=name, grid=(dil, nt), in_specs=in_specs,
        out_specs=[main(wq), main(LANES)],
        out_shape=[jax.ShapeDtypeStruct((dil, length, wq), BF16), jax.ShapeDtypeStruct((dil, length, LANES), F32)],
        scratch_shapes=scratch,
    )(*args)


def _band_bwd(q, k, v, do, lse, delta, *, max_dist, name):
    dil, length, wq = q.shape
    wk = k.shape[2]
    gqa = wk != wq
    tq = min(512, length)
    ns, nt = tq // BLOCK, length // tq
    npair = wq // LANES
    nblocks = length // BLOCK
    bias = _band_bias(max_dist, transposed=True)

    def body(q_ref, qn_ref, do_ref, don_ref, lse_ref, lsen_ref, dl_ref, dln_ref, k_ref, v_ref, bias_ref,
             dq_ref, dk_ref, dv_ref, qbuf, dobuf, stat_l, stat_d, dqt, kt, *rolled):
        i = pl.program_id(1)
        qbuf[0:tq] = q_ref[...]
        qbuf[tq:] = qn_ref[...]
        dobuf[0:tq] = do_ref[...]
        dobuf[tq:] = don_ref[...]
        for pk in range(wk // LANES):
            kt[pk] = k_ref[:, pk * LANES:(pk + 1) * LANES].astype(F32).T.astype(BF16)
        if gqa:
            kroll, vroll, ktroll = rolled
            kroll[...] = pltpu.roll(k_ref[...], HEAD_DIM, 1)
            vroll[...] = pltpu.roll(v_ref[...], HEAD_DIM, 1)
            ktroll[0] = kroll[...].astype(F32).T.astype(BF16)
        for a in range(ns):
            rows = slice(a * BLOCK, (a + 1) * BLOCK)
            stat_l[a] = _rows_to_lanes(lse_ref[rows, :])
            stat_d[a] = _rows_to_lanes(dl_ref[rows, :])
        stat_l[ns] = _rows_to_lanes(lsen_ref[...])
        stat_d[ns] = _rows_to_lanes(dln_ref[...])

        @pl.when(i == 0)
        def _():
            dqt[:, :, 0:BLOCK] = jnp.zeros((npair, LANES, BLOCK), F32)

        @pl.when(i > 0)
        def _():
            dqt[:, :, 0:BLOCK] = dqt[:, :, tq:tq + BLOCK]

        dqt[:, :, BLOCK:] = jnp.zeros((npair, LANES, tq), F32)
        half2 = _half_masks(2 * BLOCK)
        row = lax.broadcasted_iota(jnp.int32, (LANES, BLOCK), 0)
        row_half = (row < HEAD_DIM, row >= HEAD_DIM)
        col_next = (lax.broadcasted_iota(jnp.int32, (1, 2 * BLOCK), 1) >= BLOCK).astype(F32)

        for b in range(ns):
            rows = slice(b * BLOCK, (b + 1) * BLOCK)
            window = slice(b * BLOCK, (b + 2) * BLOCK)
            bt = bias_ref[...]
            if b == ns - 1:
                bt = bt + jnp.where(i == nt - 1, NEG, 0.0) * col_next
            acc = {}
            items = []
            for p in range(npair):
                lanes = slice(p * LANES, (p + 1) * LANES)
                qw = qbuf[window, lanes]
                dow = dobuf[window, lanes]
                for e in range(2):
                    h = 2 * p + e
                    pk, ek = _kv_place(h, gqa)
                    klanes = slice(pk * LANES, (pk + 1) * LANES)
                    kb = (k_ref if ek == e else kroll)[rows, klanes]
                    vb = (v_ref if ek == e else vroll)[rows, klanes]
                    qm = jnp.where(half2[e], qw, jnp.zeros_like(qw))
                    dom = jnp.where(half2[e], dow, jnp.zeros_like(dow))
                    items.append(dict(p=p, e=e, h=h, pk=pk, ek=ek, qm=qm, dom=dom,
                                      st=_dot_nt(kb, qm), dpt=_dot_nt(vb, dom)))
            for it in items:
                h = it["h"]
                lrow = jnp.concatenate([stat_l[b, h:h + 1, :], stat_l[b + 1, h:h + 1, :]], axis=1)
                drow = jnp.concatenate([stat_d[b, h:h + 1, :], stat_d[b + 1, h:h + 1, :]], axis=1)
                pt = jnp.exp(it["st"] + bt - lrow)
                it["ptb"] = pt.astype(BF16)
                it["dsb"] = (pt * (it["dpt"] - drow)).astype(BF16)
            for it in items:
                p, e, pk, ek = it["p"], it["e"], it["pk"], it["ek"]
                dv_c = _dot(it["ptb"], it["dom"])
                dk_c = _dot(it["dsb"], it["qm"])
                kbt = (kt if ek == e else ktroll)[pk, :, rows]
                kbtm = jnp.where(row_half[e], kbt, jnp.zeros_like(kbt))
                dqt[p, :, window] += _dot(kbtm, it["dsb"])
                key = (pk, ek == e)
                if key in acc:
                    acc[key] = (acc[key][0] + dk_c, acc[key][1] + dv_c)
                else:
                    acc[key] = (dk_c, dv_c)
            if not gqa:
                for p in range(npair):
                    lanes = slice(p * LANES, (p + 1) * LANES)
                    dk_ref[rows, lanes] = acc[(p, True)][0].astype(BF16)
                    dv_ref[rows, lanes] = acc[(p, True)][1].astype(BF16)
            if gqa:
                dk_al, dv_al = acc[(0, True)]
                dk_mis, dv_mis = acc[(0, False)]
                dk_ref[rows, :] = (dk_al + pltpu.roll(dk_mis, HEAD_DIM, 1)).astype(BF16)
                dv_ref[rows, :] = (dv_al + pltpu.roll(dv_mis, HEAD_DIM, 1)).astype(BF16)

        for p in range(npair):
            dq_ref[:, p * LANES:(p + 1) * LANES] = dqt[p, :, 0:tq].T.astype(BF16)

    main = lambda w: pl.BlockSpec((None, tq, w), lambda r, i: (r, i, 0))
    nxt = lambda w: pl.BlockSpec((None, BLOCK, w), lambda r, i: (r, jnp.minimum((i + 1) * ns, nblocks - 1), 0))
    scratch = [pltpu.VMEM((tq + BLOCK, wq), BF16), pltpu.VMEM((tq + BLOCK, wq), BF16),
               pltpu.VMEM((ns + 1, 8, LANES), F32), pltpu.VMEM((ns + 1, 8, LANES), F32),
               pltpu.VMEM((npair, LANES, tq + BLOCK), F32), pltpu.VMEM((wk // LANES, LANES, tq), BF16)]
    if gqa:
        scratch = scratch + [pltpu.VMEM((tq, wk), BF16)] * 2 + [pltpu.VMEM((1, LANES, tq), BF16)]
    return pl.pallas_call(
        body, name=name, grid=(dil, nt),
        in_specs=[main(wq), nxt(wq), main(wq), nxt(wq), main(LANES), nxt(LANES), main(LANES), nxt(LANES),
                  main(wk), main(wk), pl.BlockSpec(bias.shape, lambda r, i: (0, 0))],
        out_specs=[main(wq), main(wk), main(wk)],
        out_shape=[jax.ShapeDtypeStruct((dil, length, wq), BF16), jax.ShapeDtypeStruct((dil, length, wk), BF16),
                   jax.ShapeDtypeStruct((dil, length, wk), BF16)],
        scratch_shapes=scratch,
        compiler_params=pltpu.CompilerParams(dimension_semantics=("arbitrary", "arbitrary")),
    )(q, q, do, do, lse, lse, delta, delta, k, v, bias)


def _mem_attn_fwd(q, mk, mv):
    seq = q.shape[0]
    tq = min(512, seq)
    ns = tq // BLOCK

    def body(q_ref, mk_ref, mv_ref, o_ref, lse_ref):
        half = _half_masks(BLOCK)

        def sub(a, carry):
            r0 = pl.multiple_of(a * BLOCK, BLOCK)
            scores = []
            for p in range(C_W // LANES):
                lanes = slice(p * LANES, (p + 1) * LANES)
                qp = q_ref[pl.ds(r0, BLOCK), lanes]
                for e in range(2):
                    scores.append(_dot_nt(jnp.where(half[e], qp, jnp.zeros_like(qp)), mk_ref[:, lanes]))
            m_cols, l_cols, probs = [], [], []
            for s in scores:
                m = jnp.max(s, axis=1, keepdims=True)
                pe = jnp.exp(s - m)
                probs.append(pe.astype(BF16))
                m_cols.append(m)
                l_cols.append(jnp.sum(pe, axis=1, keepdims=True))
            for p in range(C_W // LANES):
                lanes = slice(p * LANES, (p + 1) * LANES)
                o_h = [_dot(probs[2 * p + e], mv_ref[:, lanes]) * (1.0 / l_cols[2 * p + e]) for e in range(2)]
                o_ref[pl.ds(r0, BLOCK), lanes] = jnp.where(half[0], o_h[0], o_h[1]).astype(BF16)
            lse_ref[pl.ds(r0, BLOCK), :] = _per_head(m_cols) + jnp.log(_per_head(l_cols, 1.0))
            return carry

        lax.fori_loop(0, ns, sub, 0, unroll=True)

    row = lambda w: pl.BlockSpec((tq, w), lambda i: (i, 0))
    full = pl.BlockSpec((N_MEM, C_W), lambda i: (0, 0))
    return pl.pallas_call(
        body, name="mem_attn_fwd", grid=(seq // tq,), in_specs=[row(C_W), full, full],
        out_specs=[row(C_W), row(LANES)],
        out_shape=[jax.ShapeDtypeStruct((seq, C_W), BF16), jax.ShapeDtypeStruct((seq, LANES), F32)],
    )(q, mk, mv)


def _mem_attn_bwd(q, mk, mv, do, lse, delta):
    seq = q.shape[0]
    tq = min(512, seq)
    ns = tq // BLOCK
    npair = C_W // LANES

    def body(q_ref, mk_ref, mv_ref, do_ref, lse_ref, dl_ref, dq_ref, dmk_ref, dmv_ref, stat_l, stat_d, mkt, dqt):
        @pl.when(pl.program_id(0) == 0)
        def _():
            dmk_ref[...] = jnp.zeros_like(dmk_ref)
            dmv_ref[...] = jnp.zeros_like(dmv_ref)
            for p in range(npair):
                mkt[p] = mk_ref[:, p * LANES:(p + 1) * LANES].astype(F32).T.astype(BF16)

        for a in range(ns):
            rows = slice(a * BLOCK, (a + 1) * BLOCK)
            stat_l[a] = _rows_to_lanes(lse_ref[rows, :])
            stat_d[a] = _rows_to_lanes(dl_ref[rows, :])
        half = _half_masks(BLOCK)
        row = lax.broadcasted_iota(jnp.int32, (LANES, N_MEM), 0)
        row_half = (row < HEAD_DIM, row >= HEAD_DIM)

        for a in range(ns):
            rows = slice(a * BLOCK, (a + 1) * BLOCK)
            items = []
            for p in range(npair):
                lanes = slice(p * LANES, (p + 1) * LANES)
                qp = q_ref[rows, lanes]
                dop = do_ref[rows, lanes]
                for e in range(2):
                    qm = jnp.where(half[e], qp, jnp.zeros_like(qp))
                    dom = jnp.where(half[e], dop, jnp.zeros_like(dop))
                    items.append(dict(p=p, e=e, qm=qm, dom=dom, st=_dot_nt(mk_ref[:, lanes], qm),
                                      dpt=_dot_nt(mv_ref[:, lanes], dom)))
            for it in items:
                h = 2 * it["p"] + it["e"]
                pt = jnp.exp(it["st"] - stat_l[a, h:h + 1, :])
                it["ptb"] = pt.astype(BF16)
                it["dsb"] = (pt * (it["dpt"] - stat_d[a, h:h + 1, :])).astype(BF16)
            for p in range(npair):
                lanes = slice(p * LANES, (p + 1) * LANES)
                pair = [it for it in items if it["p"] == p]
                dmv_ref[:, lanes] += _dot(pair[0]["ptb"], pair[0]["dom"]) + _dot(pair[1]["ptb"], pair[1]["dom"])
                dmk_ref[:, lanes] += _dot(pair[0]["dsb"], pair[0]["qm"]) + _dot(pair[1]["dsb"], pair[1]["qm"])
                kbt = mkt[p]
                dqt[p, :, rows] = (_dot(jnp.where(row_half[0], kbt, jnp.zeros_like(kbt)), pair[0]["dsb"])
                                   + _dot(jnp.where(row_half[1], kbt, jnp.zeros_like(kbt)), pair[1]["dsb"]))
        for p in range(npair):
            dq_ref[:, p * LANES:(p + 1) * LANES] = dqt[p].T.astype(BF16)

    row = lambda w: pl.BlockSpec((tq, w), lambda i: (i, 0))
    full = pl.BlockSpec((N_MEM, C_W), lambda i: (0, 0))
    return pl.pallas_call(
        body, name="mem_attn_bwd", grid=(seq // tq,),
        in_specs=[row(C_W), full, full, row(C_W), row(LANES), row(LANES)], out_specs=[row(C_W), full, full],
        out_shape=[jax.ShapeDtypeStruct((seq, C_W), BF16), jax.ShapeDtypeStruct((N_MEM, C_W), F32),
                   jax.ShapeDtypeStruct((N_MEM, C_W), F32)],
        scratch_shapes=[pltpu.VMEM((ns, 8, LANES), F32)] * 2
        + [pltpu.VMEM((npair, LANES, N_MEM), BF16), pltpu.VMEM((npair, LANES, tq), F32)],
        compiler_params=pltpu.CompilerParams(dimension_semantics=("arbitrary",)),
    )(q, mk, mv, do, lse, delta)


def _silu_and_grad(g):
    s = 1.0 / (1.0 + jnp.exp(-g))
    return g * s, s * (1.0 + g * (1.0 - s))


def _post(x, target, post_norm, w_out, sink_row, oa, lse_a, ga, ob_list, lseb_list, gb, oc, gc):
    seq = x.shape[0]
    tm = min(256, seq)
    inv_d = 1.0 / D_MODEL
    nd = len(B_DILS)

    def body(*refs):
        (x_ref, t_ref, gp_ref, w_ref, sink_ref, oa_ref, lsea_ref, ga_ref), refs = refs[:8], refs[8:]
        ob_refs, lb_refs, (gb_ref, oc_ref, gc_ref), refs = refs[:nd], refs[nd:2 * nd], refs[2 * nd:2 * nd + 3], refs[2 * nd + 3:]
        (g_ref, doa_ref, dla_ref, dga_ref), refs = refs[:4], refs[4:]
        dob_refs, lsec_refs, dlb_refs, refs = refs[:nd], refs[nd:2 * nd], refs[2 * nd:3 * nd], refs[3 * nd:]
        (dgb_ref, doc_ref, dlc_ref, dgc_ref, gw_ref, gpost_ref, gsink_ref, loss_ref), refs = refs[:8], refs[8:]
        ycat, obufs, lbufs, st_do, st_l, st_d = refs[0], refs[1:nd], refs[nd:2 * nd - 1], refs[2 * nd - 1], refs[2 * nd], refs[2 * nd + 1]

        @pl.when(pl.program_id(0) == 0)
        def _():
            gw_ref[...] = jnp.zeros_like(gw_ref)
            gpost_ref[...] = jnp.zeros_like(gpost_ref)
            gsink_ref[...] = jnp.zeros_like(gsink_ref)
            loss_ref[...] = jnp.zeros_like(loss_ref)

        o_i, l_i = [ob_refs[0][0].astype(F32)], [lb_refs[0][0]]
        for k in range(1, nd):
            _from_residues(ob_refs[k], obufs[k - 1], B_DILS[k])
            _from_residues(lb_refs[k], lbufs[k - 1], B_DILS[k])
            o_i.append(_stage_read(obufs[k - 1]))
            l_i.append(_stage_read(lbufs[k - 1]))
        mx = l_i[0]
        for l in l_i[1:]:
            mx = jnp.maximum(mx, l)
        w_i = [jnp.exp(l - mx) for l in l_i]
        z = w_i[0]
        for w in w_i[1:]:
            z = z + w
        _stage_write(st_l, mx + jnp.log(z))
        expand = _head_expand_matrix(B_W)
        inv_z = 1.0 / z
        ob = None
        for w, o in zip(w_i, o_i):
            term = _dot_split(w * inv_z, expand, 2) * o
            ob = term if ob is None else ob + term
        oa, oc = oa_ref[...].astype(F32), oc_ref[...].astype(F32)
        sa, dsa = _silu_and_grad(ga_ref[...].astype(F32))
        sb, dsb = _silu_and_grad(gb_ref[...].astype(F32))
        sc, dsc = _silu_and_grad(gc_ref[...].astype(F32))
        ycat[:, 0:A_W] = (oa * sa).astype(BF16)
        ycat[:, A_W:A_W + B_W] = (ob * sb).astype(BF16)
        ycat[:, A_W + B_W:] = (oc * sc).astype(BF16)
        y2 = _dot(ycat[...], w_ref[...])
        r = lax.rsqrt(jnp.mean(y2 * y2, axis=-1, keepdims=True) + RMS_EPS)
        zhat = y2 * r
        gp = gp_ref[...]
        err = x_ref[...] + zhat * gp - t_ref[...]
        loss_ref[...] += jnp.sum(err * err) * (0.5 * inv_d)
        g = err * inv_d
        g_ref[...] = g
        gpost_ref[...] += jnp.sum(g * zhat, axis=0, keepdims=True)
        a = g * gp
        dy2 = (r * (a - zhat * jnp.mean(a * zhat, axis=-1, keepdims=True))).astype(BF16)
        gw_ref[...] += _dot_tn(ycat[...], dy2)
        dycat = _dot_nt(dy2, w_ref[...])
        dya, dyb, dyc = dycat[:, 0:A_W], dycat[:, A_W:A_W + B_W], dycat[:, A_W + B_W:]
        doa, dob, doc = dya * sa, dyb * sb, dyc * sc
        doa_ref[...] = doa.astype(BF16)
        doc_ref[...] = doc.astype(BF16)
        dga_ref[...] = (dya * oa * dsa).astype(BF16)
        dgb_ref[...] = (dyb * ob * dsb).astype(BF16)
        dgc_ref[...] = (dyc * oc * dsc).astype(BF16)
        dl_a = _dot_split(doa * oa, _head_sum_matrix(A_W), 2)
        dla_ref[...] = dl_a
        dlc_ref[...] = _dot_split(doc * oc, _head_sum_matrix(C_W), 2)
        gsink_ref[...] += jnp.sum(jnp.exp(sink_ref[...] - lsea_ref[...]) * dl_a, axis=0, keepdims=True)
        _stage_write(st_do, dob)
        _stage_write(st_d, _dot_split(dob * ob, _head_sum_matrix(B_W), 2))
        for k, dil in enumerate(B_DILS):
            _to_residues(st_do, dob_refs[k], dil)
            _to_residues(st_l, lsec_refs[k], dil)
            _to_residues(st_d, dlb_refs[k], dil)

    row = lambda w: pl.BlockSpec((tm, w), lambda i: (i, 0))
    full = lambda shape: pl.BlockSpec(shape, lambda i: (0,) * len(shape))
    res_specs = lambda w: [_residue_spec(d, tm, w) for d in B_DILS]
    res_shapes = lambda w, dt: [jax.ShapeDtypeStruct((d, seq // d, w), dt) for d in B_DILS]
    ins = [x, target, post_norm, w_out, sink_row, oa, lse_a, ga, *ob_list, *lseb_list, gb, oc, gc]
    in_specs = ([row(D_MODEL), row(D_MODEL), full((1, D_MODEL)), full((D_MODEL, D_MODEL)), full((1, LANES)),
                 row(A_W), row(LANES), row(A_W)] + res_specs(B_W) + res_specs(LANES) + [row(B_W), row(C_W), row(C_W)])
    out_shape = ([jax.ShapeDtypeStruct((seq, D_MODEL), F32), jax.ShapeDtypeStruct((seq, A_W), BF16),
                  jax.ShapeDtypeStruct((seq, LANES), F32), jax.ShapeDtypeStruct((seq, A_W), BF16)]
                 + res_shapes(B_W, BF16) + res_shapes(LANES, F32) + res_shapes(LANES, F32)
                 + [jax.ShapeDtypeStruct((seq, B_W), BF16), jax.ShapeDtypeStruct((seq, C_W), BF16),
                    jax.ShapeDtypeStruct((seq, LANES), F32), jax.ShapeDtypeStruct((seq, C_W), BF16),
                    jax.ShapeDtypeStruct((D_MODEL, D_MODEL), F32), jax.ShapeDtypeStruct((1, D_MODEL), F32),
                    jax.ShapeDtypeStruct((1, LANES), F32), jax.ShapeDtypeStruct((1, LANES), F32)])
    out_specs = ([row(D_MODEL), row(A_W), row(LANES), row(A_W)] + res_specs(B_W) + res_specs(LANES) + res_specs(LANES)
                 + [row(B_W), row(C_W), row(LANES), row(C_W),
                    full((D_MODEL, D_MODEL)), full((1, D_MODEL)), full((1, LANES)), full((1, LANES))])
    scratch = ([pltpu.VMEM((tm, D_MODEL), BF16)] + [_stage(tm, B_W)] * (nd - 1) + [_stage(tm, LANES)] * (nd - 1)
               + [_stage(tm, B_W), _stage(tm, LANES), _stage(tm, LANES)])
    res = pl.pallas_call(
        body, name="post", grid=(seq // tm,), in_specs=in_specs, out_specs=out_specs, out_shape=out_shape,
        scratch_shapes=scratch,
        compiler_params=pltpu.CompilerParams(dimension_semantics=("arbitrary",)),
    )(*ins)
    out = dict(g=res[0], doa=res[1], dl_a=res[2], dga=res[3], dob=res[4:4 + nd], lse_b=res[4 + nd:4 + 2 * nd],
               dl_b=res[4 + 2 * nd:4 + 3 * nd])
    rest = res[4 + 3 * nd:]
    out.update(dgb=rest[0], doc=rest[1], dl_c=rest[2], dgc=rest[3], gw_out=rest[4], gpost=rest[5], gsink=rest[6],
               loss=rest[7])
    return out


def _grad_w_in(ut, nat, res):
    seq = ut.shape[1]
    tm = min(512, seq)
    nd = len(B_DILS)
    nat_list = [nat[n] for n in _NATURAL]
    res_list = [a for n in _DILATED for a in res[n]]
    rope = _rope_tables(seq, tm)

    def body(rl_ref, rb_ref, ut_ref, *refs):
        nat_refs = dict(zip(_NATURAL, refs[:len(_NATURAL)]))
        refs = refs[len(_NATURAL):]
        res_refs = {n: refs[nd * k:nd * (k + 1)] for k, n in enumerate(_DILATED)}
        refs = refs[nd * len(_DILATED):]
        dproj_ref, gw_ref = refs[:2]
        bufs = {n: refs[2 + (nd - 1) * k:2 + (nd - 1) * (k + 1)] for k, n in enumerate(_DILATED)}

        @pl.when(pl.program_id(0) == 0)
        def _():
            gw_ref[...] = jnp.zeros_like(gw_ref)

        for n in _DILATED:
            for k in range(1, nd):
                _from_residues(res_refs[n][k], bufs[n][k - 1], B_DILS[k])
        c, sm, sp = _rope_coeffs(rl_ref, rb_ref)
        sm, sp = -sm, -sp
        for blk, (name, off, roped, scaled) in enumerate(_PROJ_LAYOUT):
            lanes = slice(off, off + LANES)
            if name in nat_refs:
                piece = nat_refs[name][:, lanes].astype(F32)
            else:
                piece = res_refs[name][0][0, :, lanes].astype(F32)
                for buf in bufs[name]:
                    piece = piece + buf[off // LANES]
            if roped:
                piece = _rope(piece, c, sm, sp)
            if scaled:
                piece = piece * SCALE
            dproj_ref[:, blk * LANES:(blk + 1) * LANES] = piece.astype(BF16)
        for j in range(N_CHIPS):
            gw_ref[j] += _dot(ut_ref[...], dproj_ref[:, j * SHARD_IN:(j + 1) * SHARD_IN])

    row = lambda w: pl.BlockSpec((tm, w), lambda i: (i, 0))
    in_specs = ([pl.BlockSpec(rope[0].shape, lambda i: (0, 0)), pl.BlockSpec((8, 2 * LANES), lambda i: (i, 0)),
                 pl.BlockSpec((D_MODEL, tm), lambda i: (0, i))]
                + [row(a.shape[1]) for a in nat_list]
                + [_residue_spec(d, tm, B_W) for _ in _DILATED for d in B_DILS])
    return pl.pallas_call(
        body, name="grad_w_in", grid=(seq // tm,), in_specs=in_specs,
        out_specs=[row(D_IN), pl.BlockSpec((N_CHIPS, D_MODEL, SHARD_IN), lambda i: (0, 0, 0))],
        out_shape=[jax.ShapeDtypeStruct((seq, D_IN), BF16), jax.ShapeDtypeStruct((N_CHIPS, D_MODEL, SHARD_IN), F32)],
        scratch_shapes=[_stage(tm, B_W)] * ((nd - 1) * len(_DILATED)),
        compiler_params=pltpu.CompilerParams(dimension_semantics=("arbitrary",)),
    )(*rope, ut, *nat_list, *res_list)


def _input_grad(x, g, pre_norm, w_in_g, dproj, gx_prev, span, host, name):
    seq = x.shape[0]
    tm = seq // 16
    first_block, steps = span
    n_host_in = len(host["ins"]) if host else 0
    n_host_out = len(host["outs"]) if host else 0

    def body(*refs):
        x_ref, g_ref, gp_ref, w_ref, dp_ref = refs[:5]
        refs = refs[5 + (gx_prev is not None):]
        host_in, refs = refs[:n_host_in], refs[n_host_in:]
        gx_ref, gpre_ref = refs[:2]
        host_out, sems = refs[2:2 + n_host_out], refs[2 + n_host_out:]
        step = pl.program_id(0)

        @pl.when(step == 0)
        def _():
            gpre_ref[...] = jnp.zeros_like(gpre_ref)
            if host:
                for cp in host["build"](host_in, host_out, *sems):
                    cp.start()

        du = None
        for j in range(N_CHIPS):
            term = _dot_nt(dp_ref[:, j * SHARD_IN:(j + 1) * SHARD_IN], w_ref[j])
            du = term if du is None else du + term
        xv = x_ref[...]
        r = lax.rsqrt(jnp.mean(xv * xv, axis=-1, keepdims=True) + RMS_EPS)
        xhat = xv * r
        gpre_ref[...] += jnp.sum(du * xhat, axis=0, keepdims=True)
        a = du * gp_ref[...]
        gx_ref[...] = g_ref[...] + r * (a - xhat * jnp.mean(a * xhat, axis=-1, keepdims=True))

        if host:
            @pl.when(step == steps - 1)
            def _():
                for cp in host["build"](host_in, host_out, *sems):
                    cp.wait()

    row = lambda w: pl.BlockSpec((tm, w), lambda i: (first_block + i, 0))
    full = lambda a: pl.BlockSpec(a.shape, lambda i: (0,) * a.ndim)
    any_spec = pl.BlockSpec(memory_space=pl.ANY)
    ins = [x, g, pre_norm, w_in_g, dproj]
    in_specs = [row(D_MODEL), row(D_MODEL), full(pre_norm), full(w_in_g), row(D_IN)]
    aliases = {}
    if gx_prev is not None:
        aliases[len(ins)] = 0
        ins.append(gx_prev)
        in_specs.append(any_spec)
    out_shape = [jax.ShapeDtypeStruct((seq, D_MODEL), F32), jax.ShapeDtypeStruct((1, D_MODEL), F32)]
    out_specs = [row(D_MODEL), pl.BlockSpec((1, D_MODEL), lambda i: (0, 0))]
    scratch = []
    if host:
        ins += list(host["ins"])
        in_specs += [any_spec] * n_host_in
        out_shape += list(host["outs"])
        out_specs += [any_spec] * n_host_out
        scratch = list(host["sems"])
    return pl.pallas_call(
        body, name=name, grid=(steps,), in_specs=in_specs, out_specs=out_specs, out_shape=out_shape,
        input_output_aliases=aliases, scratch_shapes=scratch,
        compiler_params=pltpu.CompilerParams(dimension_semantics=("arbitrary",)),
    )(*ins)


def _pair_exchange(grads):
    n = len(grads)

    def build(srcs, outs, send_sems, recv_sems):
        x, y, c = lax.axis_index("x"), lax.axis_index("y"), lax.axis_index("c")
        copies = []
        for t in range(n):
            rows = grads[t].shape[1] // 2
            copies.append(pltpu.make_async_remote_copy(
                src_ref=srcs[t].at[:, pl.ds((1 - c) * rows, rows)], dst_ref=outs[t],
                send_sem=send_sems.at[t], recv_sem=recv_sems.at[t], device_id=(x, y, 1 - c), device_id_type=MESH))
        return copies

    return dict(ins=list(grads), build=build,
                outs=[jax.ShapeDtypeStruct((g.shape[0], g.shape[1] // 2, g.shape[2]), g.dtype) for g in grads],
                sems=[pltpu.SemaphoreType.DMA((n,)), pltpu.SemaphoreType.DMA((n,))])


def _pair_add(core, own, got):
    nchip, rows2, width = own.shape
    rows = rows2 // 2
    tr = min(512, rows)
    nb = rows // tr

    def body(core_ref, own_ref, got_ref, out_ref):
        out_ref[...] = (own_ref[...] + got_ref[...]).astype(BF16)

    grid_spec = pltpu.PrefetchScalarGridSpec(
        num_scalar_prefetch=1, grid=(nchip, nb),
        in_specs=[pl.BlockSpec((None, tr, width), lambda k, i, core_ref: (k, core_ref[0] * nb + i, 0)),
                  pl.BlockSpec((None, tr, width), lambda k, i, core_ref: (k, i, 0))],
        out_specs=pl.BlockSpec((None, tr, width), lambda k, i, core_ref: (k, i, 0)))
    return pl.pallas_call(
        body, name=f"pair_add_{width}", grid_spec=grid_spec,
        out_shape=jax.ShapeDtypeStruct((nchip, rows, width), BF16),
    )(core, own, got)


def _chip_exchange(parts):
    n = len(parts)

    def build(srcs, outs, send_sems, recv_sems, local_sems):
        x, y, c = lax.axis_index("x"), lax.axis_index("y"), lax.axis_index("c")
        my_chip = 2 * x + y
        chips = [(1 - x, y), (x, 1 - y), (1 - x, 1 - y)]
        copies = [pltpu.make_async_copy(srcs[t].at[my_chip], outs[t].at[my_chip], local_sems.at[t]) for t in range(n)]
        for j, (cx, cy) in enumerate(chips):
            for t in range(n):
                k = n * j + t
                copies.append(pltpu.make_async_remote_copy(
                    src_ref=srcs[t].at[2 * cx + cy], dst_ref=outs[t].at[my_chip], send_sem=send_sems.at[k],
                    recv_sem=recv_sems.at[k], device_id=(cx, cy, c), device_id_type=MESH))
        return copies

    return dict(ins=list(parts), build=build, outs=[jax.ShapeDtypeStruct(p.shape, p.dtype) for p in parts],
                sems=[pltpu.SemaphoreType.DMA((3 * n,)), pltpu.SemaphoreType.DMA((3 * n,)),
                      pltpu.SemaphoreType.DMA((n,))])


def _slot_sum(slots, name, core=None):
    ns, rows, width = slots.shape
    tr = min(512, rows)

    def body(*refs):
        in_ref, out_ref = refs[-2:]
        acc = in_ref[0].astype(F32)
        for s in range(1, ns):
            acc = acc + in_ref[s].astype(F32)
        out_ref[...] = acc

    if core is None:
        return pl.pallas_call(
            body, name=name, grid=(rows // tr,),
            in_specs=[pl.BlockSpec((ns, tr, width), lambda i: (0, i, 0))],
            out_specs=pl.BlockSpec((tr, width), lambda i: (i, 0)),
            out_shape=jax.ShapeDtypeStruct((rows, width), F32),
        )(slots)
    grid_spec = pltpu.PrefetchScalarGridSpec(
        num_scalar_prefetch=1, grid=(rows // tr,),
        in_specs=[pl.BlockSpec((ns, tr, width), lambda i, core_ref: (0, i, 0))],
        out_specs=pl.BlockSpec((None, tr, width), lambda i, core_ref: (core_ref[0], i, 0)))
    return pl.pallas_call(
        body, name=name, grid_spec=grid_spec, out_shape=jax.ShapeDtypeStruct((2, rows, width), F32),
    )(core, slots)


def _pair_gather(bufs, small):
    n = len(bufs)

    def body(*refs):
        small_ref, outs, small_out = refs[n], refs[n + 1:2 * n + 1], refs[2 * n + 1]
        send_sems, recv_sems, local_sem = refs[2 * n + 2:]
        x, y, c = lax.axis_index("x"), lax.axis_index("y"), lax.axis_index("c")
        me = 4 * x + 2 * y + c
        chips = [(1 - x, y), (x, 1 - y), (1 - x, 1 - y)]
        mine = pltpu.make_async_copy(small_ref, small_out.at[me], local_sem)
        mine.start()
        copies = [pltpu.make_async_remote_copy(
            src_ref=outs[t].at[c], dst_ref=outs[t].at[c], send_sem=send_sems.at[t], recv_sem=recv_sems.at[t],
            device_id=(x, y, 1 - c), device_id_type=MESH) for t in range(n)]
        peers = [(x, y, 1 - c)] + [(cx, cy, cc) for (cx, cy) in chips for cc in (c, 1 - c)]
        for j, peer in enumerate(peers):
            copies.append(pltpu.make_async_remote_copy(
                src_ref=small_ref, dst_ref=small_out.at[me], send_sem=send_sems.at[n + j],
                recv_sem=recv_sems.at[n + j], device_id=peer, device_id_type=MESH))
        for cp in copies:
            cp.start()
        for cp in copies:
            cp.wait()
        mine.wait()

    any_spec = pl.BlockSpec(memory_space=pl.ANY)
    res = pl.pallas_call(
        body, name="pair_gather",
        out_shape=[jax.ShapeDtypeStruct(b.shape, b.dtype) for b in bufs]
        + [jax.ShapeDtypeStruct((8,) + small.shape, small.dtype)],
        in_specs=[any_spec] * (n + 1), out_specs=[any_spec] * (n + 1),
        input_output_aliases={t: t for t in range(n)},
        scratch_shapes=[pltpu.SemaphoreType.DMA((n + 7,)), pltpu.SemaphoreType.DMA((n + 7,)),
                        pltpu.SemaphoreType.DMA],
    )(*bufs, small)
    return [r.reshape(2 * b.shape[1], b.shape[2]) for r, b in zip(res[:n], bufs)], res[n]


def _adamw(w, g, m, v, name):
    rows, width = w.shape
    tr = min(256, rows)
    c1 = 1.0 / (1.0 - ADAM_B1 ** ADAM_STEP)
    c2 = 1.0 / (1.0 - ADAM_B2 ** ADAM_STEP)

    def body(w_ref, g_ref, m_ref, v_ref, d_ref, nm_ref, nv_ref):
        gv = g_ref[...]
        nm = ADAM_B1 * m_ref[...] + (1.0 - ADAM_B1) * gv
        nv = ADAM_B2 * v_ref[...] + (1.0 - ADAM_B2) * (gv * gv)
        nm_ref[...] = nm
        nv_ref[...] = nv
        d_ref[...] = -ADAM_LR * ((nm * c1) / (jnp.sqrt(nv * c2) + ADAM_EPS) + ADAM_WD * w_ref[...])

    spec = pl.BlockSpec((tr, width), lambda i: (i, 0))
    return pl.pallas_call(
        body, name=name, grid=(rows // tr,), in_specs=[spec] * 4, out_specs=[spec] * 3,
        out_shape=[jax.ShapeDtypeStruct(w.shape, F32)] * 3,
    )(w, g, m, v)


def _local_step(x, mem, target, pre_norm, sink_a, mem_norm, post_norm, w_in_g, w_out, w_mkv):
    mk, mv = _mem_kv(mem, mem_norm, w_mkv)
    pr = _pre_proj(x, pre_norm, w_in_g)
    sink = sink_a.reshape(-1)
    qa, ka, va = pr["qa"][None], pr["ka"][None], pr["va"][None]
    oa, lse_a = _band_fwd(qa, ka, va, sink, max_dist=A_WINDOW - 1, name="swa_fwd")
    ob_list, lseb_list = [], []
    for k, (win, dil) in enumerate(B_CONFIGS):
        o_i, l_i = _band_fwd(pr["qb"][k], pr["kb"][k], pr["vb"][k], None, max_dist=win // dil, name=f"dil{dil}_fwd")
        ob_list.append(o_i)
        lseb_list.append(l_i)
    oc, lse_c = _mem_attn_fwd(pr["qc"], mk, mv)
    sink_row = jnp.pad(sink, (0, LANES - sink.shape[0])).reshape(1, LANES)
    po = _post(x, target, post_norm, w_out, sink_row, oa[0], lse_a[0], pr["ga"], ob_list, lseb_list, pr["gb"], oc,
               pr["gc"])
    dqc, dmk, dmv = _mem_attn_bwd(pr["qc"], mk, mv, po["doc"], lse_c, po["dl_c"])
    dqa, dka, dva = _band_bwd(qa, ka, va, po["doa"][None], lse_a, po["dl_a"][None], max_dist=A_WINDOW - 1,
                              name="swa_bwd")
    res = dict(qb=[], kb=[], vb=[])
    for k, (win, dil) in enumerate(B_CONFIGS):
        dq_i, dk_i, dv_i = _band_bwd(pr["qb"][k], pr["kb"][k], pr["vb"][k], po["dob"][k], po["lse_b"][k],
                                     po["dl_b"][k], max_dist=win // dil, name=f"dil{dil}_bwd")
        res["qb"].append(dq_i)
        res["kb"].append(dk_i)
        res["vb"].append(dv_i)
    nat = dict(qa=dqa[0], ka=dka[0], va=dva[0], ga=po["dga"], gb=po["dgb"], qc=dqc, gc=po["dgc"])
    dproj, gw_in = _grad_w_in(pr["ut"], nat, res)
    gw_mkv, gmem = _mem_kv_bwd(mem, mem_norm, w_mkv, dmk, dmv)
    gsink = -po["gsink"][0, :sink.shape[0]]
    return dict(loss=po["loss"], g=po["g"], dproj=dproj, gw_in=gw_in, gw_out=po["gw_out"], gw_mkv=gw_mkv,
                gpost=po["gpost"], gmem=gmem, gsink=gsink)


def kernel(x, mem, pre_norm, w_in, sink_a, mem_norm, w_mem_kv, w_out, post_norm, loss_target, m_pre_norm, m_w_in, m_sink_a, m_mem_norm, m_w_mem_kv, m_w_out, m_post_norm, v_pre_norm, v_w_in, v_sink_a, v_mem_norm, v_w_mem_kv, v_w_out, v_post_norm):
    w_in_g, w_out_g, w_mkv_g = _gather_weights(w_in[0].astype(BF16), w_out[0].astype(BF16), w_mem_kv[0].astype(BF16))
    loc = _local_step(x[0], mem[0], loss_target[0], pre_norm, sink_a, mem_norm, post_norm,
                      w_in_g.reshape(N_CHIPS, D_MODEL, SHARD_IN), w_out_g.reshape(D_MODEL, D_MODEL),
                      w_mkv_g.reshape(D_MODEL, 2 * C_W))
    big = [loc["gw_in"], loc["gw_out"].reshape(N_CHIPS, D_MODEL // N_CHIPS, D_MODEL),
           loc["gw_mkv"].reshape(N_CHIPS, D_MODEL // N_CHIPS, 2 * C_W)]
    core = lax.axis_index("c").astype(jnp.int32).reshape(1)
    w_in_full = w_in_g.reshape(N_CHIPS, D_MODEL, SHARD_IN)
    step_in = (x[0], loc["g"], pre_norm, w_in_full, loc["dproj"])
    gx_a, gpre_a, *got = _input_grad(*step_in, None, (0, 4), _pair_exchange(big), "input_grad_a")
    parts = [_pair_add(core, own, g) for own, g in zip(big, got)]
    gx_b, gpre_b, *slots = _input_grad(*step_in, gx_a, (4, 8), _chip_exchange(parts), "input_grad_b")
    grad_x, gpre_c = _input_grad(*step_in, gx_b, (12, 4), None, "input_grad_c")
    halves = [_slot_sum(s, name=f"chip_sum_{s.shape[2]}", core=core) for s in slots]
    widen = lambda a: jnp.pad(a.reshape(1, -1), ((0, 0), (0, D_MODEL - a.size)))
    small = jnp.concatenate([gpre_a, loc["gpost"], loc["gmem"], widen(loc["gsink"]), widen(loc["loss"]), gpre_b,
                             gpre_c, jnp.zeros((1, D_MODEL), F32)], axis=0)
    (g_in, g_out, g_mkv), small_slots = _pair_gather(halves, small)
    small_sum = _slot_sum(small_slots, name="device_sum")
    g_pre, g_post, g_mem = small_sum[0:1] + small_sum[5:6] + small_sum[6:7], small_sum[1:2], small_sum[2:3]
    g_sink = small_sum[3:4, :sink_a.shape[1]]
    loss = small_sum[4, 0]

    d_in, nm_in, nv_in = _adamw(w_in[0], g_in, m_w_in[0], v_w_in[0], "adamw_in")
    d_out, nm_out, nv_out = _adamw(w_out[0], g_out, m_w_out[0], v_w_out[0], "adamw_out")
    d_mkv, nm_mkv, nv_mkv = _adamw(w_mem_kv[0], g_mkv, m_w_mem_kv[0], v_w_mem_kv[0], "adamw_mkv")
    pad6 = lambda a: jnp.pad(a, ((0, 0), (0, D_MODEL - a.shape[1])))
    stack = lambda a, b, c_, d_: jnp.concatenate([a, b, c_, pad6(d_), jnp.zeros((4, D_MODEL), F32)], axis=0)
    d_s, nm_s, nv_s = _adamw(stack(pre_norm, post_norm, mem_norm, sink_a),
                             jnp.concatenate([g_pre, small_sum[1:]], axis=0),
                             stack(m_pre_norm, m_post_norm, m_mem_norm, m_sink_a),
                             stack(v_pre_norm, v_post_norm, v_mem_norm, v_sink_a), "adamw_small")
    ns_ = sink_a.shape[1]
    unpack = lambda a: (a[0:1], a[3:4, :ns_], a[2:3], a[1:2])
    d_pre, d_sink, d_mem, d_post = unpack(d_s)
    nm_pre, nm_sink, nm_mem, nm_post = unpack(nm_s)
    nv_pre, nv_sink, nv_mem, nv_post = unpack(nv_s)
    lead = lambda a: a[None]
    return (loss, lead(grad_x),
            g_pre, lead(g_in), g_sink, g_mem, lead(g_mkv), lead(g_out), g_post,
            d_pre, lead(d_in), d_sink, d_mem, lead(d_mkv), lead(d_out), d_post,
            nm_pre, lead(nm_in), nm_sink, nm_mem, lead(nm_mkv), lead(nm_out), nm_post,
            nv_pre, lead(nv_in), nv_sink, nv_mem, lead(nv_mkv), lead(nv_out), nv_post)
```

```python
import numpy as np
import jax
import jax.numpy as jnp
from jax import lax
from jax.experimental import pallas as pl
from jax.experimental.pallas import tpu as pltpu

F32 = jnp.float32
BF16 = jnp.bfloat16

D_MODEL = 1024
HEAD_DIM = 64
LANES = 128
BLOCK = 128
A_W, A_KV_W, B_W, C_W = 384, 128, 384, 256
N_MEM = 256
D_IN = 3072
N_CHIPS = 4
SHARD_IN = D_IN // N_CHIPS
B_CONFIGS = ((128, 1), (512, 4), (2048, 16))
B_DILS = tuple(d for _, d in B_CONFIGS)
A_WINDOW = 128
RMS_EPS = 1e-6
ROPE_THETA = 500000.0
SCALE = HEAD_DIM ** -0.5
NEG = -1e30
ADAM_LR, ADAM_B1, ADAM_B2, ADAM_EPS, ADAM_WD, ADAM_STEP = 0.001, 0.9, 0.999, 1e-08, 0.01, 10

NT = (((1,), (1,)), ((), ()))
TN = (((0,), (0,)), ((), ()))
MESH = pl.DeviceIdType.MESH

_PROJ_LAYOUT = (
    [("qa", 128 * i, True, True) for i in range(3)] + [("ka", 0, True, False), ("va", 0, False, False)]
    + [("ga", 128 * i, False, False) for i in range(3)]
    + [("qb", 128 * i, True, True) for i in range(3)] + [("kb", 128 * i, True, False) for i in range(3)]
    + [("vb", 128 * i, False, False) for i in range(3)] + [("gb", 128 * i, False, False) for i in range(3)]
    + [("qc", 128 * i, False, True) for i in range(2)] + [("gc", 128 * i, False, False) for i in range(2)]
)
_PROJ_WIDTH = dict(qa=A_W, ka=A_KV_W, va=A_KV_W, ga=A_W, qb=B_W, kb=B_W, vb=B_W, gb=B_W, qc=C_W, gc=C_W)
_NATURAL = ("qa", "ka", "va", "ga", "gb", "qc", "gc")
_DILATED = ("qb", "kb", "vb")


def _dot(a, b):
    return jnp.dot(a, b, preferred_element_type=F32)


def _dot_nt(a, b):
    return lax.dot_general(a, b, NT, preferred_element_type=F32)


def _dot_tn(a, b):
    return lax.dot_general(a, b, TN, preferred_element_type=F32)


def _half_masks(rows):
    lane = lax.broadcasted_iota(jnp.int32, (rows, LANES), 1)
    return lane < HEAD_DIM, lane >= HEAD_DIM


def _rope(t, c, sm, sp):
    return t * c + pltpu.roll(t, LANES - 8, 1) * sm + pltpu.roll(t, 8, 1) * sp


def _rope_tables(seq, tm):
    dim = jnp.arange(LANES) % HEAD_DIM
    inv_freq = ROPE_THETA ** (-jnp.arange(0, 16, 2, dtype=F32) / 16)
    freq = jnp.where(dim < 16, inv_freq[dim % 8], 0.0)[None, :]
    local = jnp.arange(tm, dtype=F32)[:, None] * freq
    base = (jnp.arange(seq // tm, dtype=F32) * tm)[:, None] * freq
    both = lambda a: jnp.concatenate([jnp.cos(a), jnp.sin(a)], axis=1)
    return both(local), jnp.repeat(both(base), 8, axis=0)


def _rope_coeffs(local_ref, base_ref):
    cl, sl = local_ref[:, :LANES], local_ref[:, LANES:]
    cb, sb = base_ref[0:1, :LANES], base_ref[0:1, LANES:]
    cos = cb * cl - sb * sl
    sin = sb * cl + cb * sl
    dim = lax.broadcasted_iota(jnp.int32, (1, LANES), 1) % HEAD_DIM
    return cos, jnp.where(dim < 8, -sin, 0.0), jnp.where((dim >= 8) & (dim < 16), sin, 0.0)


def _split3(x):
    a = x.astype(BF16)
    r = x - a.astype(F32)
    b = r.astype(BF16)
    c = (r - b.astype(F32)).astype(BF16)
    return a, b, c


def _rows_to_lanes(x):
    row = lax.broadcasted_iota(jnp.int32, (8, LANES), 0)
    lane = lax.broadcasted_iota(jnp.int32, (8, LANES), 1)
    eye = (row == lane).astype(BF16)
    a, b, c = _split3(x)
    return _dot_nt(eye, a) + _dot_nt(eye, b) + _dot_nt(eye, c)


def _head_sum_matrix(width):
    k = lax.broadcasted_iota(jnp.int32, (width, LANES), 0)
    h = lax.broadcasted_iota(jnp.int32, (width, LANES), 1)
    return (k // HEAD_DIM == h).astype(BF16)


def _head_expand_matrix(width):
    h = lax.broadcasted_iota(jnp.int32, (LANES, width), 0)
    k = lax.broadcasted_iota(jnp.int32, (LANES, width), 1)
    return (k // HEAD_DIM == h).astype(BF16)


def _dot_split(x, mat, terms):
    parts = _split3(x)[:terms]
    out = _dot(parts[0], mat)
    for p in parts[1:]:
        out = out + _dot(p, mat)
    return out


def _per_head(cols, fill=0.0):
    rows = cols[0].shape[0]
    lane = lax.broadcasted_iota(jnp.int32, (rows, LANES), 1)
    out = jnp.full((rows, LANES), fill, F32)
    for h, col in enumerate(cols):
        out = jnp.where(lane == h, col, out)
    return out


def _lane_blocks(width):
    return [slice(p * LANES, (p + 1) * LANES) for p in range(width // LANES)]


def _stage(rows, width):
    return pltpu.VMEM((width // LANES, rows, LANES), F32)


def _stage_write(buf, value):
    for p, lanes in enumerate(_lane_blocks(value.shape[1])):
        buf[p] = value[:, lanes]


def _stage_read(buf):
    return jnp.concatenate([buf[p] for p in range(buf.shape[0])], axis=1) if buf.shape[0] > 1 else buf[0]


def _to_residues(buf, out_ref, dil):
    rows = buf.shape[1] // dil
    for r in range(dil):
        for p in range(buf.shape[0]):
            plane = buf.at[p]
            out_ref[r, :, p * LANES:(p + 1) * LANES] = plane[pl.ds(r, rows, stride=dil), :].astype(out_ref.dtype)


def _from_residues(in_ref, buf, dil):
    rows = buf.shape[1] // dil
    for r in range(dil):
        for p in range(buf.shape[0]):
            plane = buf.at[p]
            plane[pl.ds(r, rows, stride=dil), :] = in_ref[r, :, p * LANES:(p + 1) * LANES].astype(F32)


def _residue_spec(dil, tm, width):
    return pl.BlockSpec((dil, tm // dil, width), lambda i: (0, i, 0))


def _gather_weights(w_in_s, w_out_s, w_mkv_s):
    shards = tuple(s.reshape(2, s.shape[0] // 2, s.shape[1]) for s in (w_in_s, w_out_s, w_mkv_s))
    n = len(shards)

    def body(*refs):
        srcs, outs = refs[:n], refs[2 * n:3 * n]
        send_sems, recv_sems = refs[3 * n:]
        x, y, c = lax.axis_index("x"), lax.axis_index("y"), lax.axis_index("c")
        my_chip = 2 * x + y
        sibling = (x, y, 1 - c)
        chips = [(1 - x, y), (x, 1 - y), (1 - x, 1 - y)]

        def half(t, chip, which):
            return outs[t].at[chip, which]

        def src_half(t, which):
            return srcs[t].at[which]

        def copy(k, src, dst, to):
            return pltpu.make_async_remote_copy(src_ref=src, dst_ref=dst, send_sem=send_sems.at[k],
                                                recv_sem=recv_sems.at[k], device_id=to, device_id_type=MESH)

        first = []
        for j, (cx, cy) in enumerate(chips):
            for t in range(n):
                first.append(copy(n * j + t, src_half(t, c), half(t, my_chip, c), (cx, cy, c)))
        for cp in first:
            cp.start()
        passed = []
        for j, (cx, cy) in enumerate(chips):
            chip = 2 * cx + cy
            for t in range(n):
                k = n * j + t
                copy(k, src_half(t, c), half(t, chip, c), (cx, cy, c)).wait_recv()
                fwd = copy(n * 3 + k, half(t, chip, c), half(t, chip, c), sibling)
                fwd.start()
                passed.append(fwd)
        for j, (cx, cy) in enumerate(chips):
            chip = 2 * cx + cy
            for t in range(n):
                k = n * 3 + n * j + t
                copy(k, half(t, chip, 1 - c), half(t, chip, 1 - c), sibling).wait_recv()
        for cp in first + passed:
            cp.wait_send()

    my_chip = 2 * lax.axis_index("x") + lax.axis_index("y")
    landing = [lax.dynamic_update_slice(jnp.zeros((N_CHIPS,) + s.shape, s.dtype), s[None], (my_chip, 0, 0, 0))
               for s in shards]
    any_spec = pl.BlockSpec(memory_space=pl.ANY)
    return pl.pallas_call(
        body, name="gather_weights",
        out_shape=[jax.ShapeDtypeStruct((N_CHIPS,) + s.shape, s.dtype) for s in shards],
        in_specs=[any_spec] * (2 * n), out_specs=[any_spec] * n,
        input_output_aliases={n + t: t for t in range(n)},
        scratch_shapes=[pltpu.SemaphoreType.DMA((6 * n,)), pltpu.SemaphoreType.DMA((6 * n,))],
    )(*shards, *landing)


def _mem_kv(mem, mem_norm, w_mkv):
    def body(mem_ref, g_ref, w_ref, mk_ref, mv_ref):
        m = mem_ref[...]
        r = lax.rsqrt(jnp.mean(m * m, axis=-1, keepdims=True) + RMS_EPS)
        mn = (m * r * g_ref[...]).astype(BF16)
        kv = _dot(mn, w_ref[...])
        mk_ref[...] = kv[:, :C_W].astype(BF16)
        mv_ref[...] = kv[:, C_W:].astype(BF16)

    return pl.pallas_call(
        body, name="mem_kv",
        out_shape=[jax.ShapeDtypeStruct((N_MEM, C_W), BF16)] * 2,
    )(mem, mem_norm, w_mkv)


def _mem_kv_bwd(mem, mem_norm, w_mkv, dmk, dmv):
    def body(mem_ref, g_ref, w_ref, dmk_ref, dmv_ref, gw_ref, gn_ref):
        m = mem_ref[...]
        r = lax.rsqrt(jnp.mean(m * m, axis=-1, keepdims=True) + RMS_EPS)
        mhat = m * r
        mn = (mhat * g_ref[...]).astype(BF16)
        dkv = jnp.concatenate([dmk_ref[...], dmv_ref[...]], axis=1).astype(BF16)
        gw_ref[...] = _dot_tn(mn, dkv)
        dmn = _dot_nt(dkv, w_ref[...])
        gn_ref[...] = jnp.sum(dmn * mhat, axis=0, keepdims=True)

    return pl.pallas_call(
        body, name="mem_kv_bwd",
        out_shape=[jax.ShapeDtypeStruct((D_MODEL, 2 * C_W), F32), jax.ShapeDtypeStruct((1, D_MODEL), F32)],
    )(mem, mem_norm, w_mkv, dmk, dmv)


def _pre_proj(x, pre_norm, w_in_g):
    seq = x.shape[0]
    tm = min(512, seq)
    n_nat, n_dil = len(_NATURAL), len(_DILATED) * len(B_DILS)
    rope = _rope_tables(seq, tm)

    def body(x_ref, g_ref, w_ref, rl_ref, rb_ref, *refs):
        nat = dict(zip(_NATURAL, refs[:n_nat]))
        res = {n: refs[n_nat + len(B_DILS) * k:n_nat + len(B_DILS) * (k + 1)] for k, n in enumerate(_DILATED)}
        ut = refs[n_nat + n_dil]
        bufs = dict(zip(_DILATED, refs[n_nat + n_dil + 1:]))
        xv = x_ref[...]
        r = lax.rsqrt(jnp.mean(xv * xv, axis=-1, keepdims=True) + RMS_EPS)
        u = xv * r * g_ref[...]
        ub = u.astype(BF16)
        ut[...] = u.T.astype(BF16)
        c, sm, sp = _rope_coeffs(rl_ref, rb_ref)
        for j in range(N_CHIPS):
            pj = _dot(ub, w_ref[j])
            for b in range(SHARD_IN // LANES):
                name, off, roped, scaled = _PROJ_LAYOUT[(SHARD_IN // LANES) * j + b]
                piece = pj[:, LANES * b:LANES * (b + 1)]
                if roped:
                    piece = _rope(piece, c, sm, sp)
                if scaled:
                    piece = piece * SCALE
                if name in bufs:
                    bufs[name][off // LANES] = piece
                else:
                    nat[name][:, off:off + LANES] = piece.astype(BF16)
        for name in _DILATED:
            for ref, dil in zip(res[name], B_DILS):
                _to_residues(bufs[name], ref, dil)

    row = lambda w: pl.BlockSpec((tm, w), lambda i: (i, 0))
    full = lambda a: pl.BlockSpec(a.shape, lambda i: (0,) * a.ndim)
    out_shape = [jax.ShapeDtypeStruct((seq, _PROJ_WIDTH[n]), BF16) for n in _NATURAL]
    out_specs = [row(_PROJ_WIDTH[n]) for n in _NATURAL]
    for n in _DILATED:
        for dil in B_DILS:
            out_shape.append(jax.ShapeDtypeStruct((dil, seq // dil, B_W), BF16))
            out_specs.append(_residue_spec(dil, tm, B_W))
    out_shape.append(jax.ShapeDtypeStruct((D_MODEL, seq), BF16))
    out_specs.append(pl.BlockSpec((D_MODEL, tm), lambda i: (0, i)))
    res = pl.pallas_call(
        body, name="pre_proj", grid=(seq // tm,),
        in_specs=[row(D_MODEL), full(pre_norm), full(w_in_g), full(rope[0]), pl.BlockSpec((8, 2 * LANES), lambda i: (i, 0))],
        out_specs=out_specs, out_shape=out_shape,
        scratch_shapes=[_stage(tm, B_W)] * len(_DILATED),
    )(x, pre_norm, w_in_g, *rope)
    out = dict(zip(_NATURAL, res[:n_nat]))
    for k, n in enumerate(_DILATED):
        out[n] = res[n_nat + len(B_DILS) * k:n_nat + len(B_DILS) * (k + 1)]
    out["ut"] = res[n_nat + n_dil]
    return out


def _band_bias(max_dist, transposed):
    i = np.arange(BLOCK)[:, None]
    j = np.arange(BLOCK)[None, :]
    if transposed:
        same = i <= j
        other = (j + BLOCK - i) <= max_dist
        vis = np.concatenate([same, other], axis=1)
    else:
        prev = (i + BLOCK - j) <= max_dist
        same = j <= i
        vis = np.concatenate([prev, same], axis=1)
    return jnp.asarray(np.where(vis, 0.0, NEG).astype(np.float32))


def _kv_place(h, gqa):
    return (0, h // 3) if gqa else (h // 2, h % 2)


def _band_fwd(q, k, v, sink, *, max_dist, name):
    dil, length, wq = q.shape
    wk = k.shape[2]
    gqa = wk != wq
    tq = min(512, length)
    ns, nt = tq // BLOCK, length // tq
    npair = wq // LANES
    bias = _band_bias(max_dist, transposed=False)
    has_sink = sink is not None

    def body(*refs):
        if has_sink:
            sink_ref, refs = refs[0], refs[1:]
        q_ref, k_ref, kp_ref, v_ref, vp_ref, bias_ref, o_ref, lse_ref, kbuf, vbuf = refs[:10]
        i = pl.program_id(1)
        kbuf[0:BLOCK] = kp_ref[...]
        kbuf[BLOCK:] = k_ref[...]
        vbuf[0:BLOCK] = vp_ref[...]
        vbuf[BLOCK:] = v_ref[...]
        if gqa:
            kroll, vroll = refs[10:12]
            kroll[...] = pltpu.roll(kbuf[...], HEAD_DIM, 1)
            vroll[...] = pltpu.roll(vbuf[...], HEAD_DIM, 1)
        half = _half_masks(BLOCK)
        col_prev = (lax.broadcasted_iota(jnp.int32, (1, 2 * BLOCK), 1) < BLOCK).astype(F32)

        def sub(a, carry):
            r0 = pl.multiple_of(a * BLOCK, BLOCK)
            pen = jnp.where((i == 0) & (a == 0), NEG, 0.0)
            b = bias_ref[...] + pen * col_prev
            scores = []
            for p in range(npair):
                qp = q_ref[pl.ds(r0, BLOCK), p * LANES:(p + 1) * LANES]
                for e in range(2):
                    pk, ek = _kv_place(2 * p + e, gqa)
                    kw = (kbuf if ek == e else kroll)[pl.ds(r0, 2 * BLOCK), pk * LANES:(pk + 1) * LANES]
                    scores.append(_dot_nt(jnp.where(half[e], qp, jnp.zeros_like(qp)), kw))
            m_cols, l_cols, probs = [], [], []
            for h, s in enumerate(scores):
                s = s + b
                m = jnp.max(s, axis=1, keepdims=True)
                if has_sink:
                    m = jnp.maximum(m, sink_ref[h])
                pe = jnp.exp(s - m)
                l = jnp.sum(pe, axis=1, keepdims=True)
                if has_sink:
                    l = l + jnp.exp(sink_ref[h] - m)
                probs.append(pe.astype(BF16))
                m_cols.append(m)
                l_cols.append(l)
            for p in range(npair):
                o_h = []
                for e in range(2):
                    h = 2 * p + e
                    pk, ek = _kv_place(h, gqa)
                    vw = (vbuf if ek == e else vroll)[pl.ds(r0, 2 * BLOCK), pk * LANES:(pk + 1) * LANES]
                    o_h.append(_dot(probs[h], vw) * (1.0 / l_cols[h]))
                o_ref[pl.ds(r0, BLOCK), p * LANES:(p + 1) * LANES] = jnp.where(half[0], o_h[0], o_h[1]).astype(BF16)
            lse_ref[pl.ds(r0, BLOCK), :] = _per_head(m_cols) + jnp.log(_per_head(l_cols, 1.0))
            return carry

        lax.fori_loop(0, ns, sub, 0, unroll=True)

    main = lambda w: pl.BlockSpec((None, tq, w), lambda r, i: (r, i, 0))
    prev = lambda w: pl.BlockSpec((None, BLOCK, w), lambda r, i: (r, jnp.maximum(i * ns - 1, 0), 0))
    in_specs = [main(wq), main(wk), prev(wk), main(wk), prev(wk), pl.BlockSpec(bias.shape, lambda r, i: (0, 0))]
    args = [q, k, k, v, v, bias]
    if has_sink:
        in_specs = [pl.BlockSpec(memory_space=pltpu.SMEM)] + in_specs
        args = [sink] + args
    scratch = [pltpu.VMEM((tq + BLOCK, wk), BF16)] * (4 if gqa else 2)
    return pl.pallas_call(
        body, name=name, grid=(dil, nt), in_specs=in_specs,
        out_specs=[main(wq), main(LANES)],
        out_shape=[jax.ShapeDtypeStruct((dil, length, wq), BF16), jax.ShapeDtypeStruct((dil, length, LANES), F32)],
        scratch_shapes=scratch,
    )(*args)


def _band_bwd(q, k, v, do, lse, delta, *, max_dist, name):
    dil, length, wq = q.shape
    wk = k.shape[2]
    gqa = wk != wq
    tq = min(512, length)
    ns, nt = tq // BLOCK, length // tq
    npair = wq // LANES
    nblocks = length // BLOCK
    bias = _band_bias(max_dist, transposed=True)

    def body(q_ref, qn_ref, do_ref, don_ref, lse_ref, lsen_ref, dl_ref, dln_ref, k_ref, v_ref, bias_ref,
             dq_ref, dk_ref, dv_ref, qbuf, dobuf, stat_l, stat_d, dqt, kt, *rolled):
        i = pl.program_id(1)
        qbuf[0:tq] = q_ref[...]
        qbuf[tq:] = qn_ref[...]
        dobuf[0:tq] = do_ref[...]
        dobuf[tq:] = don_ref[...]
        for pk in range(wk // LANES):
            kt[pk] = k_ref[:, pk * LANES:(pk + 1) * LANES].astype(F32).T.astype(BF16)
        if gqa:
            kroll, vroll, ktroll = rolled
            kroll[...] = pltpu.roll(k_ref[...], HEAD_DIM, 1)
            vroll[...] = pltpu.roll(v_ref[...], HEAD_DIM, 1)
            ktroll[0] = kroll[...].astype(F32).T.astype(BF16)
        for a in range(ns):
            rows = slice(a * BLOCK, (a + 1) * BLOCK)
            stat_l[a] = _rows_to_lanes(lse_ref[rows, :])
            stat_d[a] = _rows_to_lanes(dl_ref[rows, :])
        stat_l[ns] = _rows_to_lanes(lsen_ref[...])
        stat_d[ns] = _rows_to_lanes(dln_ref[...])

        @pl.when(i == 0)
        def _():
            dqt[:, :, 0:BLOCK] = jnp.zeros((npair, LANES, BLOCK), F32)

        @pl.when(i > 0)
        def _():
            dqt[:, :, 0:BLOCK] = dqt[:, :, tq:tq + BLOCK]

        dqt[:, :, BLOCK:] = jnp.zeros((npair, LANES, tq), F32)
        half2 = _half_masks(2 * BLOCK)
        row = lax.broadcasted_iota(jnp.int32, (LANES, BLOCK), 0)
        row_half = (row < HEAD_DIM, row >= HEAD_DIM)
        col_next = (lax.broadcasted_iota(jnp.int32, (1, 2 * BLOCK), 1) >= BLOCK).astype(F32)

        for b in range(ns):
            rows = slice(b * BLOCK, (b + 1) * BLOCK)
            window = slice(b * BLOCK, (b + 2) * BLOCK)
            bt = bias_ref[...]
            if b == ns - 1:
                bt = bt + jnp.where(i == nt - 1, NEG, 0.0) * col_next
            acc = {}
            items = []
            for p in range(npair):
                lanes = slice(p * LANES, (p + 1) * LANES)
                qw = qbuf[window, lanes]
                dow = dobuf[window, lanes]
                for e in range(2):
                    h = 2 * p + e
                    pk, ek = _kv_place(h, gqa)
                    klanes = slice(pk * LANES, (pk + 1) * LANES)
                    kb = (k_ref if ek == e else kroll)[rows, klanes]
                    vb = (v_ref if ek == e else vroll)[rows, klanes]
                    qm = jnp.where(half2[e], qw, jnp.zeros_like(qw))
                    dom = jnp.where(half2[e], dow, jnp.zeros_like(dow))
                    items.append(dict(p=p, e=e, h=h, pk=pk, ek=ek, qm=qm, dom=dom,
                                      st=_dot_nt(kb, qm), dpt=_dot_nt(vb, dom)))
            for it in items:
                h = it["h"]
                lrow = jnp.concatenate([stat_l[b, h:h + 1, :], stat_l[b + 1, h:h + 1, :]], axis=1)
                drow = jnp.concatenate([stat_d[b, h:h + 1, :], stat_d[b + 1, h:h + 1, :]], axis=1)
                pt = jnp.exp(it["st"] + bt - lrow)
                it["ptb"] = pt.astype(BF16)
                it["dsb"] = (pt * (it["dpt"] - drow)).astype(BF16)
            for it in items:
                p, e, pk, ek = it["p"], it["e"], it["pk"], it["ek"]
                dv_c = _dot(it["ptb"], it["dom"])
                dk_c = _dot(it["dsb"], it["qm"])
                kbt = (kt if ek == e else ktroll)[pk, :, rows]
                kbtm = jnp.where(row_half[e], kbt, jnp.zeros_like(kbt))
                dqt[p, :, window] += _dot(kbtm, it["dsb"])
                key = (pk, ek == e)
                if key in acc:
                    acc[key] = (acc[key][0] + dk_c, acc[key][1] + dv_c)
                else:
                    acc[key] = (dk_c, dv_c)
            if not gqa:
                for p in range(npair):
                    lanes = slice(p * LANES, (p + 1) * LANES)
                    dk_ref[rows, lanes] = acc[(p, True)][0].astype(BF16)
                    dv_ref[rows, lanes] = acc[(p, True)][1].astype(BF16)
            if gqa:
                dk_al, dv_al = acc[(0, True)]
                dk_mis, dv_mis = acc[(0, False)]
                dk_ref[rows, :] = (dk_al + pltpu.roll(dk_mis, HEAD_DIM, 1)).astype(BF16)
                dv_ref[rows, :] = (dv_al + pltpu.roll(dv_mis, HEAD_DIM, 1)).astype(BF16)

        for p in range(npair):
            dq_ref[:, p * LANES:(p + 1) * LANES] = dqt[p, :, 0:tq].T.astype(BF16)

    main = lambda w: pl.BlockSpec((None, tq, w), lambda r, i: (r, i, 0))
    nxt = lambda w: pl.BlockSpec((None, BLOCK, w), lambda r, i: (r, jnp.minimum((i + 1) * ns, nblocks - 1), 0))
    scratch = [pltpu.VMEM((tq + BLOCK, wq), BF16), pltpu.VMEM((tq + BLOCK, wq), BF16),
               pltpu.VMEM((ns + 1, 8, LANES), F32), pltpu.VMEM((ns + 1, 8, LANES), F32),
               pltpu.VMEM((npair, LANES, tq + BLOCK), F32), pltpu.VMEM((wk // LANES, LANES, tq), BF16)]
    if gqa:
        scratch = scratch + [pltpu.VMEM((tq, wk), BF16)] * 2 + [pltpu.VMEM((1, LANES, tq), BF16)]
    return pl.pallas_call(
        body, name=name, grid=(dil, nt),
        in_specs=[main(wq), nxt(wq), main(wq), nxt(wq), main(LANES), nxt(LANES), main(LANES), nxt(LANES),
                  main(wk), main(wk), pl.BlockSpec(bias.shape, lambda r, i: (0, 0))],
        out_specs=[main(wq), main(wk), main(wk)],
        out_shape=[jax.ShapeDtypeStruct((dil, length, wq), BF16), jax.ShapeDtypeStruct((dil, length, wk), BF16),
                   jax.ShapeDtypeStruct((dil, length, wk), BF16)],
        scratch_shapes=scratch,
        compiler_params=pltpu.CompilerParams(dimension_semantics=("arbitrary", "arbitrary")),
    )(q, q, do, do, lse, lse, delta, delta, k, v, bias)


def _mem_attn_fwd(q, mk, mv):
    seq = q.shape[0]
    tq = min(512, seq)
    ns = tq // BLOCK

    def body(q_ref, mk_ref, mv_ref, o_ref, lse_ref):
        half = _half_masks(BLOCK)

        def sub(a, carry):
            r0 = pl.multiple_of(a * BLOCK, BLOCK)
            scores = []
            for p in range(C_W // LANES):
                lanes = slice(p * LANES, (p + 1) * LANES)
                qp = q_ref[pl.ds(r0, BLOCK), lanes]
                for e in range(2):
                    scores.append(_dot_nt(jnp.where(half[e], qp, jnp.zeros_like(qp)), mk_ref[:, lanes]))
            m_cols, l_cols, probs = [], [], []
            for s in scores:
                m = jnp.max(s, axis=1, keepdims=True)
                pe = jnp.exp(s - m)
                probs.append(pe.astype(BF16))
                m_cols.append(m)
                l_cols.append(jnp.sum(pe, axis=1, keepdims=True))
            for p in range(C_W // LANES):
                lanes = slice(p * LANES, (p + 1) * LANES)
                o_h = [_dot(probs[2 * p + e], mv_ref[:, lanes]) * (1.0 / l_cols[2 * p + e]) for e in range(2)]
                o_ref[pl.ds(r0, BLOCK), lanes] = jnp.where(half[0], o_h[0], o_h[1]).astype(BF16)
            lse_ref[pl.ds(r0, BLOCK), :] = _per_head(m_cols) + jnp.log(_per_head(l_cols, 1.0))
            return carry

        lax.fori_loop(0, ns, sub, 0, unroll=True)

    row = lambda w: pl.BlockSpec((tq, w), lambda i: (i, 0))
    full = pl.BlockSpec((N_MEM, C_W), lambda i: (0, 0))
    return pl.pallas_call(
        body, name="mem_attn_fwd", grid=(seq // tq,), in_specs=[row(C_W), full, full],
        out_specs=[row(C_W), row(LANES)],
        out_shape=[jax.ShapeDtypeStruct((seq, C_W), BF16), jax.ShapeDtypeStruct((seq, LANES), F32)],
    )(q, mk, mv)


def _mem_attn_bwd(q, mk, mv, do, lse, delta):
    seq = q.shape[0]
    tq = min(512, seq)
    ns = tq // BLOCK
    npair = C_W // LANES

    def body(q_ref, mk_ref, mv_ref, do_ref, lse_ref, dl_ref, dq_ref, dmk_ref, dmv_ref, stat_l, stat_d, mkt, dqt):
        @pl.when(pl.program_id(0) == 0)
        def _():
            dmk_ref[...] = jnp.zeros_like(dmk_ref)
            dmv_ref[...] = jnp.zeros_like(dmv_ref)
            for p in range(npair):
                mkt[p] = mk_ref[:, p * LANES:(p + 1) * LANES].astype(F32).T.astype(BF16)

        for a in range(ns):
            rows = slice(a * BLOCK, (a + 1) * BLOCK)
            stat_l[a] = _rows_to_lanes(lse_ref[rows, :])
            stat_d[a] = _rows_to_lanes(dl_ref[rows, :])
        half = _half_masks(BLOCK)
        row = lax.broadcasted_iota(jnp.int32, (LANES, N_MEM), 0)
        row_half = (row < HEAD_DIM, row >= HEAD_DIM)

        for a in range(ns):
            rows = slice(a * BLOCK, (a + 1) * BLOCK)
            items = []
            for p in range(npair):
                lanes = slice(p * LANES, (p + 1) * LANES)
                qp = q_ref[rows, lanes]
                dop = do_ref[rows, lanes]
                for e in range(2):
                    qm = jnp.where(half[e], qp, jnp.zeros_like(qp))
                    dom = jnp.where(half[e], dop, jnp.zeros_like(dop))
                    items.append(dict(p=p, e=e, qm=qm, dom=dom, st=_dot_nt(mk_ref[:, lanes], qm),
                                      dpt=_dot_nt(mv_ref[:, lanes], dom)))
            for it in items:
                h = 2 * it["p"] + it["e"]
                pt = jnp.exp(it["st"] - stat_l[a, h:h + 1, :])
                it["ptb"] = pt.astype(BF16)
                it["dsb"] = (pt * (it["dpt"] - stat_d[a, h:h + 1, :])).astype(BF16)
            for p in range(npair):
                lanes = slice(p * LANES, (p + 1) * LANES)
                pair = [it for it in items if it["p"] == p]
                dmv_ref[:, lanes] += _dot(pair[0]["ptb"], pair[0]["dom"]) + _dot(pair[1]["ptb"], pair[1]["dom"])
                dmk_ref[:, lanes] += _dot(pair[0]["dsb"], pair[0]["qm"]) + _dot(pair[1]["dsb"], pair[1]["qm"])
                kbt = mkt[p]
                dqt[p, :, rows] = (_dot(jnp.where(row_half[0], kbt, jnp.zeros_like(kbt)), pair[0]["dsb"])
                                   + _dot(jnp.where(row_half[1], kbt, jnp.zeros_like(kbt)), pair[1]["dsb"]))
        for p in range(npair):
            dq_ref[:, p * LANES:(p + 1) * LANES] = dqt[p].T.astype(BF16)

    row = lambda w: pl.BlockSpec((tq, w), lambda i: (i, 0))
    full = pl.BlockSpec((N_MEM, C_W), lambda i: (0, 0))
    return pl.pallas_call(
        body, name="mem_attn_bwd", grid=(seq // tq,),
        in_specs=[row(C_W), full, full, row(C_W), row(LANES), row(LANES)], out_specs=[row(C_W), full, full],
        out_shape=[jax.ShapeDtypeStruct((seq, C_W), BF16), jax.ShapeDtypeStruct((N_MEM, C_W), F32),
                   jax.ShapeDtypeStruct((N_MEM, C_W), F32)],
        scratch_shapes=[pltpu.VMEM((ns, 8, LANES), F32)] * 2
        + [pltpu.VMEM((npair, LANES, N_MEM), BF16), pltpu.VMEM((npair, LANES, tq), F32)],
        compiler_params=pltpu.CompilerParams(dimension_semantics=("arbitrary",)),
    )(q, mk, mv, do, lse, delta)


def _silu_and_grad(g):
    s = 1.0 / (1.0 + jnp.exp(-g))
    return g * s, s * (1.0 + g * (1.0 - s))


def _post(x, target, post_norm, w_out, sink_row, oa, lse_a, ga, ob_list, lseb_list, gb, oc, gc):
    seq = x.shape[0]
    tm = min(512, seq)
    inv_d = 1.0 / D_MODEL
    nd = len(B_DILS)

    def body(*refs):
        (x_ref, t_ref, gp_ref, w_ref, sink_ref, oa_ref, lsea_ref, ga_ref), refs = refs[:8], refs[8:]
        ob_refs, lb_refs, (gb_ref, oc_ref, gc_ref), refs = refs[:nd], refs[nd:2 * nd], refs[2 * nd:2 * nd + 3], refs[2 * nd + 3:]
        (g_ref, doa_ref, dla_ref, dga_ref), refs = refs[:4], refs[4:]
        dob_refs, lsec_refs, dlb_refs, refs = refs[:nd], refs[nd:2 * nd], refs[2 * nd:3 * nd], refs[3 * nd:]
        (dgb_ref, doc_ref, dlc_ref, dgc_ref, gw_ref, gpost_ref, gsink_ref, loss_ref), refs = refs[:8], refs[8:]
        ycat, obufs, lbufs, st_do, st_l, st_d = refs[0], refs[1:nd], refs[nd:2 * nd - 1], refs[2 * nd - 1], refs[2 * nd], refs[2 * nd + 1]

        @pl.when(pl.program_id(0) == 0)
        def _():
            gw_ref[...] = jnp.zeros_like(gw_ref)
            gpost_ref[...] = jnp.zeros_like(gpost_ref)
            gsink_ref[...] = jnp.zeros_like(gsink_ref)
            loss_ref[...] = jnp.zeros_like(loss_ref)

        o_i, l_i = [ob_refs[0][0].astype(F32)], [lb_refs[0][0]]
        for k in range(1, nd):
            _from_residues(ob_refs[k], obufs[k - 1], B_DILS[k])
            _from_residues(lb_refs[k], lbufs[k - 1], B_DILS[k])
            o_i.append(_stage_read(obufs[k - 1]))
            l_i.append(_stage_read(lbufs[k - 1]))
        mx = l_i[0]
        for l in l_i[1:]:
            mx = jnp.maximum(mx, l)
        w_i = [jnp.exp(l - mx) for l in l_i]
        z = w_i[0]
        for w in w_i[1:]:
            z = z + w
        _stage_write(st_l, mx + jnp.log(z))
        expand = _head_expand_matrix(B_W)
        inv_z = 1.0 / z
        ob = None
        for w, o in zip(w_i, o_i):
            term = _dot_split(w * inv_z, expand, 2) * o
            ob = term if ob is None else ob + term
        oa, oc = oa_ref[...].astype(F32), oc_ref[...].astype(F32)
        sa, dsa = _silu_and_grad(ga_ref[...].astype(F32))
        sb, dsb = _silu_and_grad(gb_ref[...].astype(F32))
        sc, dsc = _silu_and_grad(gc_ref[...].astype(F32))
        ycat[:, 0:A_W] = (oa * sa).astype(BF16)
        ycat[:, A_W:A_W + B_W] = (ob * sb).astype(BF16)
        ycat[:, A_W + B_W:] = (oc * sc).astype(BF16)
        y2 = _dot(ycat[...], w_ref[...])
        r = lax.rsqrt(jnp.mean(y2 * y2, axis=-1, keepdims=True) + RMS_EPS)
        zhat = y2 * r
        gp = gp_ref[...]
        err = x_ref[...] + zhat * gp - t_ref[...]
        loss_ref[...] += jnp.sum(err * err) * (0.5 * inv_d)
        g = err * inv_d
        g_ref[...] = g
        gpost_ref[...] += jnp.sum(g * zhat, axis=0, keepdims=True)
        a = g * gp
        dy2 = (r * (a - zhat * jnp.mean(a * zhat, axis=-1, keepdims=True))).astype(BF16)
        gw_ref[...] += _dot_tn(ycat[...], dy2)
        dycat = _dot_nt(dy2, w_ref[...])
        dya, dyb, dyc = dycat[:, 0:A_W], dycat[:, A_W:A_W + B_W], dycat[:, A_W + B_W:]
        doa, dob, doc = dya * sa, dyb * sb, dyc * sc
        doa_ref[...] = doa.astype(BF16)
        doc_ref[...] = doc.astype(BF16)
        dga_ref[...] = (dya * oa * dsa).astype(BF16)
        dgb_ref[...] = (dyb * ob * dsb).astype(BF16)
        dgc_ref[...] = (dyc * oc * dsc).astype(BF16)
        dl_a = _dot_split(doa * oa, _head_sum_matrix(A_W), 2)
        dla_ref[...] = dl_a
        dlc_ref[...] = _dot_split(doc * oc, _head_sum_matrix(C_W), 2)
        gsink_ref[...] += jnp.sum(jnp.exp(sink_ref[...] - lsea_ref[...]) * dl_a, axis=0, keepdims=True)
        _stage_write(st_do, dob)
        _stage_write(st_d, _dot_split(dob * ob, _head_sum_matrix(B_W), 2))
        for k, dil in enumerate(B_DILS):
            _to_residues(st_do, dob_refs[k], dil)
            _to_residues(st_l, lsec_refs[k], dil)
            _to_residues(st_d, dlb_refs[k], dil)

    row = lambda w: pl.BlockSpec((tm, w), lambda i: (i, 0))
    full = lambda shape: pl.BlockSpec(shape, lambda i: (0,) * len(shape))
    res_specs = lambda w: [_residue_spec(d, tm, w) for d in B_DILS]
    res_shapes = lambda w, dt: [jax.ShapeDtypeStruct((d, seq // d, w), dt) for d in B_DILS]
    ins = [x, target, post_norm, w_out, sink_row, oa, lse_a, ga, *ob_list, *lseb_list, gb, oc, gc]
    in_specs = ([row(D_MODEL), row(D_MODEL), full((1, D_MODEL)), full((D_MODEL, D_MODEL)), full((1, LANES)),
                 row(A_W), row(LANES), row(A_W)] + res_specs(B_W) + res_specs(LANES) + [row(B_W), row(C_W), row(C_W)])
    out_shape = ([jax.ShapeDtypeStruct((seq, D_MODEL), F32), jax.ShapeDtypeStruct((seq, A_W), BF16),
                  jax.ShapeDtypeStruct((seq, LANES), F32), jax.ShapeDtypeStruct((seq, A_W), BF16)]
                 + res_shapes(B_W, BF16) + res_shapes(LANES, F32) + res_shapes(LANES, F32)
                 + [jax.ShapeDtypeStruct((seq, B_W), BF16), jax.ShapeDtypeStruct((seq, C_W), BF16),
                    jax.ShapeDtypeStruct((seq, LANES), F32), jax.ShapeDtypeStruct((seq, C_W), BF16),
                    jax.ShapeDtypeStruct((D_MODEL, D_MODEL), F32), jax.ShapeDtypeStruct((1, D_MODEL), F32),
                    jax.ShapeDtypeStruct((1, LANES), F32), jax.ShapeDtypeStruct((1, LANES), F32)])
    out_specs = ([row(D_MODEL), row(A_W), row(LANES), row(A_W)] + res_specs(B_W) + res_specs(LANES) + res_specs(LANES)
                 + [row(B_W), row(C_W), row(LANES), row(C_W),
                    full((D_MODEL, D_MODEL)), full((1, D_MODEL)), full((1, LANES)), full((1, LANES))])
    scratch = ([pltpu.VMEM((tm, D_MODEL), BF16)] + [_stage(tm, B_W)] * (nd - 1) + [_stage(tm, LANES)] * (nd - 1)
               + [_stage(tm, B_W), _stage(tm, LANES), _stage(tm, LANES)])
    res = pl.pallas_call(
        body, name="post", grid=(seq // tm,), in_specs=in_specs, out_specs=out_specs, out_shape=out_shape,
        scratch_shapes=scratch,
        compiler_params=pltpu.CompilerParams(dimension_semantics=("arbitrary",)),
    )(*ins)
    out = dict(g=res[0], doa=res[1], dl_a=res[2], dga=res[3], dob=res[4:4 + nd], lse_b=res[4 + nd:4 + 2 * nd],
               dl_b=res[4 + 2 * nd:4 + 3 * nd])
    rest = res[4 + 3 * nd:]
    out.update(dgb=rest[0], doc=rest[1], dl_c=rest[2], dgc=rest[3], gw_out=rest[4], gpost=rest[5], gsink=rest[6],
               loss=rest[7])
    return out


def _grad_w_in(ut, nat, res):
    seq = ut.shape[1]
    tm = min(512, seq)
    nd = len(B_DILS)
    nat_list = [nat[n] for n in _NATURAL]
    res_list = [a for n in _DILATED for a in res[n]]
    rope = _rope_tables(seq, tm)

    def body(rl_ref, rb_ref, ut_ref, *refs):
        nat_refs = dict(zip(_NATURAL, refs[:len(_NATURAL)]))
        refs = refs[len(_NATURAL):]
        res_refs = {n: refs[nd * k:nd * (k + 1)] for k, n in enumerate(_DILATED)}
        refs = refs[nd * len(_DILATED):]
        dproj_ref, gw_ref = refs[:2]
        bufs = {n: refs[2 + (nd - 1) * k:2 + (nd - 1) * (k + 1)] for k, n in enumerate(_DILATED)}

        @pl.when(pl.program_id(0) == 0)
        def _():
            gw_ref[...] = jnp.zeros_like(gw_ref)

        for n in _DILATED:
            for k in range(1, nd):
                _from_residues(res_refs[n][k], bufs[n][k - 1], B_DILS[k])
        c, sm, sp = _rope_coeffs(rl_ref, rb_ref)
        sm, sp = -sm, -sp
        for blk, (name, off, roped, scaled) in enumerate(_PROJ_LAYOUT):
            lanes = slice(off, off + LANES)
            if name in nat_refs:
                piece = nat_refs[name][:, lanes].astype(F32)
            else:
                piece = res_refs[name][0][0, :, lanes].astype(F32)
                for buf in bufs[name]:
                    piece = piece + buf[off // LANES]
            if roped:
                piece = _rope(piece, c, sm, sp)
            if scaled:
                piece = piece * SCALE
            dproj_ref[:, blk * LANES:(blk + 1) * LANES] = piece.astype(BF16)
        for j in range(N_CHIPS):
            gw_ref[j] += _dot(ut_ref[...], dproj_ref[:, j * SHARD_IN:(j + 1) * SHARD_IN])

    row = lambda w: pl.BlockSpec((tm, w), lambda i: (i, 0))
    in_specs = ([pl.BlockSpec(rope[0].shape, lambda i: (0, 0)), pl.BlockSpec((8, 2 * LANES), lambda i: (i, 0)),
                 pl.BlockSpec((D_MODEL, tm), lambda i: (0, i))]
                + [row(a.shape[1]) for a in nat_list]
                + [_residue_spec(d, tm, B_W) for _ in _DILATED for d in B_DILS])
    return pl.pallas_call(
        body, name="grad_w_in", grid=(seq // tm,), in_specs=in_specs,
        out_specs=[row(D_IN), pl.BlockSpec((N_CHIPS, D_MODEL, SHARD_IN), lambda i: (0, 0, 0))],
        out_shape=[jax.ShapeDtypeStruct((seq, D_IN), BF16), jax.ShapeDtypeStruct((N_CHIPS, D_MODEL, SHARD_IN), F32)],
        scratch_shapes=[_stage(tm, B_W)] * ((nd - 1) * len(_DILATED)),
        compiler_params=pltpu.CompilerParams(dimension_semantics=("arbitrary",)),
    )(*rope, ut, *nat_list, *res_list)


def _input_grad(x, g, pre_norm, w_in_g, dproj, gx_prev, span, host, name):
    seq = x.shape[0]
    tm = seq // 16
    first_block, steps = span
    n_host_in = len(host["ins"]) if host else 0
    n_host_out = len(host["outs"]) if host else 0

    def body(*refs):
        x_ref, g_ref, gp_ref, w_ref, dp_ref = refs[:5]
        refs = refs[5 + (gx_prev is not None):]
        host_in, refs = refs[:n_host_in], refs[n_host_in:]
        gx_ref, gpre_ref = refs[:2]
        host_out, sems = refs[2:2 + n_host_out], refs[2 + n_host_out:]
        step = pl.program_id(0)

        @pl.when(step == 0)
        def _():
            gpre_ref[...] = jnp.zeros_like(gpre_ref)
            if host:
                for cp in host["build"](host_in, host_out, *sems):
                    cp.start()

        du = None
        for j in range(N_CHIPS):
            term = _dot_nt(dp_ref[:, j * SHARD_IN:(j + 1) * SHARD_IN], w_ref[j])
            du = term if du is None else du + term
        xv = x_ref[...]
        r = lax.rsqrt(jnp.mean(xv * xv, axis=-1, keepdims=True) + RMS_EPS)
        xhat = xv * r
        gpre_ref[...] += jnp.sum(du * xhat, axis=0, keepdims=True)
        a = du * gp_ref[...]
        gx_ref[...] = g_ref[...] + r * (a - xhat * jnp.mean(a * xhat, axis=-1, keepdims=True))

        if host:
            @pl.when(step == steps - 1)
            def _():
                for cp in host["build"](host_in, host_out, *sems):
                    cp.wait()

    row = lambda w: pl.BlockSpec((tm, w), lambda i: (first_block + i, 0))
    full = lambda a: pl.BlockSpec(a.shape, lambda i: (0,) * a.ndim)
    any_spec = pl.BlockSpec(memory_space=pl.ANY)
    ins = [x, g, pre_norm, w_in_g, dproj]
    in_specs = [row(D_MODEL), row(D_MODEL), full(pre_norm), full(w_in_g), row(D_IN)]
    aliases = {}
    if gx_prev is not None:
        aliases[len(ins)] = 0
        ins.append(gx_prev)
        in_specs.append(any_spec)
    out_shape = [jax.ShapeDtypeStruct((seq, D_MODEL), F32), jax.ShapeDtypeStruct((1, D_MODEL), F32)]
    out_specs = [row(D_MODEL), pl.BlockSpec((1, D_MODEL), lambda i: (0, 0))]
    scratch = []
    if host:
        ins += list(host["ins"])
        in_specs += [any_spec] * n_host_in
        out_shape += list(host["outs"])
        out_specs += [any_spec] * n_host_out
        scratch = list(host["sems"])
    return pl.pallas_call(
        body, name=name, grid=(steps,), in_specs=in_specs, out_specs=out_specs, out_shape=out_shape,
        input_output_aliases=aliases, scratch_shapes=scratch,
        compiler_params=pltpu.CompilerParams(dimension_semantics=("arbitrary",)),
    )(*ins)


def _pair_exchange(grads):
    n = len(grads)

    def build(srcs, outs, send_sems, recv_sems):
        x, y, c = lax.axis_index("x"), lax.axis_index("y"), lax.axis_index("c")
        copies = []
        for t in range(n):
            rows = grads[t].shape[1] // 2
            copies.append(pltpu.make_async_remote_copy(
                src_ref=srcs[t].at[:, pl.ds((1 - c) * rows, rows)], dst_ref=outs[t],
                send_sem=send_sems.at[t], recv_sem=recv_sems.at[t], device_id=(x, y, 1 - c), device_id_type=MESH))
        return copies

    return dict(ins=list(grads), build=build,
                outs=[jax.ShapeDtypeStruct((g.shape[0], g.shape[1] // 2, g.shape[2]), g.dtype) for g in grads],
                sems=[pltpu.SemaphoreType.DMA((n,)), pltpu.SemaphoreType.DMA((n,))])


def _pair_add(core, own, got):
    nchip, rows2, width = own.shape
    rows = rows2 // 2
    tr = min(512, rows)
    nb = rows // tr

    def body(core_ref, own_ref, got_ref, out_ref):
        out_ref[...] = (own_ref[...] + got_ref[...]).astype(BF16)

    grid_spec = pltpu.PrefetchScalarGridSpec(
        num_scalar_prefetch=1, grid=(nchip, nb),
        in_specs=[pl.BlockSpec((None, tr, width), lambda k, i, core_ref: (k, core_ref[0] * nb + i, 0)),
                  pl.BlockSpec((None, tr, width), lambda k, i, core_ref: (k, i, 0))],
        out_specs=pl.BlockSpec((None, tr, width), lambda k, i, core_ref: (k, i, 0)))
    return pl.pallas_call(
        body, name=f"pair_add_{width}", grid_spec=grid_spec,
        out_shape=jax.ShapeDtypeStruct((nchip, rows, width), BF16),
    )(core, own, got)


def _chip_exchange(parts):
    n = len(parts)

    def build(srcs, outs, send_sems, recv_sems, local_sems):
        x, y, c = lax.axis_index("x"), lax.axis_index("y"), lax.axis_index("c")
        my_chip = 2 * x + y
        chips = [(1 - x, y), (x, 1 - y), (1 - x, 1 - y)]
        copies = [pltpu.make_async_copy(srcs[t].at[my_chip], outs[t].at[my_chip], local_sems.at[t]) for t in range(n)]
        for j, (cx, cy) in enumerate(chips):
            for t in range(n):
                k = n * j + t
                copies.append(pltpu.make_async_remote_copy(
                    src_ref=srcs[t].at[2 * cx + cy], dst_ref=outs[t].at[my_chip], send_sem=send_sems.at[k],
                    recv_sem=recv_sems.at[k], device_id=(cx, cy, c), device_id_type=MESH))
        return copies

    return dict(ins=list(parts), build=build, outs=[jax.ShapeDtypeStruct(p.shape, p.dtype) for p in parts],
                sems=[pltpu.SemaphoreType.DMA((3 * n,)), pltpu.SemaphoreType.DMA((3 * n,)),
                      pltpu.SemaphoreType.DMA((n,))])


def _slot_sum(slots, name, core=None):
    ns, rows, width = slots.shape
    tr = min(512, rows)

    def body(*refs):
        in_ref, out_ref = refs[-2:]
        acc = in_ref[0].astype(F32)
        for s in range(1, ns):
            acc = acc + in_ref[s].astype(F32)
        out_ref[...] = acc

    if core is None:
        return pl.pallas_call(
            body, name=name, grid=(rows // tr,),
            in_specs=[pl.BlockSpec((ns, tr, width), lambda i: (0, i, 0))],
            out_specs=pl.BlockSpec((tr, width), lambda i: (i, 0)),
            out_shape=jax.ShapeDtypeStruct((rows, width), F32),
        )(slots)
    grid_spec = pltpu.PrefetchScalarGridSpec(
        num_scalar_prefetch=1, grid=(rows // tr,),
        in_specs=[pl.BlockSpec((ns, tr, width), lambda i, core_ref: (0, i, 0))],
        out_specs=pl.BlockSpec((None, tr, width), lambda i, core_ref: (core_ref[0], i, 0)))
    return pl.pallas_call(
        body, name=name, grid_spec=grid_spec, out_shape=jax.ShapeDtypeStruct((2, rows, width), F32),
    )(core, slots)


def _pair_gather(bufs, small):
    n = len(bufs)

    def body(*refs):
        small_ref, outs, small_out = refs[n], refs[n + 1:2 * n + 1], refs[2 * n + 1]
        send_sems, recv_sems, local_sem = refs[2 * n + 2:]
        x, y, c = lax.axis_index("x"), lax.axis_index("y"), lax.axis_index("c")
        me = 4 * x + 2 * y + c
        chips = [(1 - x, y), (x, 1 - y), (1 - x, 1 - y)]
        mine = pltpu.make_async_copy(small_ref, small_out.at[me], local_sem)
        mine.start()
        copies = [pltpu.make_async_remote_copy(
            src_ref=outs[t].at[c], dst_ref=outs[t].at[c], send_sem=send_sems.at[t], recv_sem=recv_sems.at[t],
            device_id=(x, y, 1 - c), device_id_type=MESH) for t in range(n)]
        peers = [(x, y, 1 - c)] + [(cx, cy, cc) for (cx, cy) in chips for cc in (c, 1 - c)]
        for j, peer in enumerate(peers):
            copies.append(pltpu.make_async_remote_copy(
                src_ref=small_ref, dst_ref=small_out.at[me], send_sem=send_sems.at[n + j],
                recv_sem=recv_sems.at[n + j], device_id=peer, device_id_type=MESH))
        for cp in copies:
            cp.start()
        for cp in copies:
            cp.wait()
        mine.wait()

    any_spec = pl.BlockSpec(memory_space=pl.ANY)
    res = pl.pallas_call(
        body, name="pair_gather",
        out_shape=[jax.ShapeDtypeStruct(b.shape, b.dtype) for b in bufs]
        + [jax.ShapeDtypeStruct((8,) + small.shape, small.dtype)],
        in_specs=[any_spec] * (n + 1), out_specs=[any_spec] * (n + 1),
        input_output_aliases={t: t for t in range(n)},
        scratch_shapes=[pltpu.SemaphoreType.DMA((n + 7,)), pltpu.SemaphoreType.DMA((n + 7,)),
                        pltpu.SemaphoreType.DMA],
    )(*bufs, small)
    return [r.reshape(2 * b.shape[1], b.shape[2]) for r, b in zip(res[:n], bufs)], res[n]


def _adamw(w, g, m, v, name):
    rows, width = w.shape
    tr = min(256, rows)
    c1 = 1.0 / (1.0 - ADAM_B1 ** ADAM_STEP)
    c2 = 1.0 / (1.0 - ADAM_B2 ** ADAM_STEP)

    def body(w_ref, g_ref, m_ref, v_ref, d_ref, nm_ref, nv_ref):
        gv = g_ref[...]
        nm = ADAM_B1 * m_ref[...] + (1.0 - ADAM_B1) * gv
        nv = ADAM_B2 * v_ref[...] + (1.0 - ADAM_B2) * (gv * gv)
        nm_ref[...] = nm
        nv_ref[...] = nv
        d_ref[...] = -ADAM_LR * ((nm * c1) / (jnp.sqrt(nv * c2) + ADAM_EPS) + ADAM_WD * w_ref[...])

    spec = pl.BlockSpec((tr, width), lambda i: (i, 0))
    return pl.pallas_call(
        body, name=name, grid=(rows // tr,), in_specs=[spec] * 4, out_specs=[spec] * 3,
        out_shape=[jax.ShapeDtypeStruct(w.shape, F32)] * 3,
    )(w, g, m, v)


def _local_step(x, mem, target, pre_norm, sink_a, mem_norm, post_norm, w_in_g, w_out, w_mkv):
    mk, mv = _mem_kv(mem, mem_norm, w_mkv)
    pr = _pre_proj(x, pre_norm, w_in_g)
    sink = sink_a.reshape(-1)
    qa, ka, va = pr["qa"][None], pr["ka"][None], pr["va"][None]
    oa, lse_a = _band_fwd(qa, ka, va, sink, max_dist=A_WINDOW - 1, name="swa_fwd")
    ob_list, lseb_list = [], []
    for k, (win, dil) in enumerate(B_CONFIGS):
        o_i, l_i = _band_fwd(pr["qb"][k], pr["kb"][k], pr["vb"][k], None, max_dist=win // dil, name=f"dil{dil}_fwd")
        ob_list.append(o_i)
        lseb_list.append(l_i)
    oc, lse_c = _mem_attn_fwd(pr["qc"], mk, mv)
    sink_row = jnp.pad(sink, (0, LANES - sink.shape[0])).reshape(1, LANES)
    po = _post(x, target, post_norm, w_out, sink_row, oa[0], lse_a[0], pr["ga"], ob_list, lseb_list, pr["gb"], oc,
               pr["gc"])
    dqc, dmk, dmv = _mem_attn_bwd(pr["qc"], mk, mv, po["doc"], lse_c, po["dl_c"])
    dqa, dka, dva = _band_bwd(qa, ka, va, po["doa"][None], lse_a, po["dl_a"][None], max_dist=A_WINDOW - 1,
                              name="swa_bwd")
    res = dict(qb=[], kb=[], vb=[])
    for k, (win, dil) in enumerate(B_CONFIGS):
        dq_i, dk_i, dv_i = _band_bwd(pr["qb"][k], pr["kb"][k], pr["vb"][k], po["dob"][k], po["lse_b"][k],
                                     po["dl_b"][k], max_dist=win // dil, name=f"dil{dil}_bwd")
        res["qb"].append(dq_i)
        res["kb"].append(dk_i)
        res["vb"].append(dv_i)
    nat = dict(qa=dqa[0], ka=dka[0], va=dva[0], ga=po["dga"], gb=po["dgb"], qc=dqc, gc=po["dgc"])
    dproj, gw_in = _grad_w_in(pr["ut"], nat, res)
    gw_mkv, gmem = _mem_kv_bwd(mem, mem_norm, w_mkv, dmk, dmv)
    gsink = -po["gsink"][0, :sink.shape[0]]
    return dict(loss=po["loss"], g=po["g"], dproj=dproj, gw_in=gw_in, gw_out=po["gw_out"], gw_mkv=gw_mkv,
                gpost=po["gpost"], gmem=gmem, gsink=gsink)


def kernel(x, mem, pre_norm, w_in, sink_a, mem_norm, w_mem_kv, w_out, post_norm, loss_target, m_pre_norm, m_w_in, m_sink_a, m_mem_norm, m_w_mem_kv, m_w_out, m_post_norm, v_pre_norm, v_w_in, v_sink_a, v_mem_norm, v_w_mem_kv, v_w_out, v_post_norm):
    w_in_g, w_out_g, w_mkv_g = _gather_weights(w_in[0].astype(BF16), w_out[0].astype(BF16), w_mem_kv[0].astype(BF16))
    loc = _local_step(x[0], mem[0], loss_target[0], pre_norm, sink_a, mem_norm, post_norm,
                      w_in_g.reshape(N_CHIPS, D_MODEL, SHARD_IN), w_out_g.reshape(D_MODEL, D_MODEL),
                      w_mkv_g.reshape(D_MODEL, 2 * C_W))
    big = [loc["gw_in"], loc["gw_out"].reshape(N_CHIPS, D_MODEL // N_CHIPS, D_MODEL),
           loc["gw_mkv"].reshape(N_CHIPS, D_MODEL // N_CHIPS, 2 * C_W)]
    core = lax.axis_index("c").astype(jnp.int32).reshape(1)
    w_in_full = w_in_g.reshape(N_CHIPS, D_MODEL, SHARD_IN)
    step_in = (x[0], loc["g"], pre_norm, w_in_full, loc["dproj"])
    gx_a, gpre_a, *got = _input_grad(*step_in, None, (0, 3), _pair_exchange(big), "input_grad_a")
    parts = [_pair_add(core, own, g) for own, g in zip(big, got)]
    gx_b, gpre_b, *slots = _input_grad(*step_in, gx_a, (3, 10), _chip_exchange(parts), "input_grad_b")
    grad_x, gpre_c = _input_grad(*step_in, gx_b, (13, 3), None, "input_grad_c")
    halves = [_slot_sum(s, name=f"chip_sum_{s.shape[2]}", core=core) for s in slots]
    widen = lambda a: jnp.pad(a.reshape(1, -1), ((0, 0), (0, D_MODEL - a.size)))
    small = jnp.concatenate([gpre_a, loc["gpost"], loc["gmem"], widen(loc["gsink"]), widen(loc["loss"]), gpre_b,
                             gpre_c, jnp.zeros((1, D_MODEL), F32)], axis=0)
    (g_in, g_out, g_mkv), small_slots = _pair_gather(halves, small)
    small_sum = _slot_sum(small_slots, name="device_sum")
    g_pre, g_post, g_mem = small_sum[0:1] + small_sum[5:6] + small_sum[6:7], small_sum[1:2], small_sum[2:3]
    g_sink = small_sum[3:4, :sink_a.shape[1]]
    loss = small_sum[4, 0]

    d_in, nm_in, nv_in = _adamw(w_in[0], g_in, m_w_in[0], v_w_in[0], "adamw_in")
    d_out, nm_out, nv_out = _adamw(w_out[0], g_out, m_w_out[0], v_w_out[0], "adamw_out")
    d_mkv, nm_mkv, nv_mkv = _adamw(w_mem_kv[0], g_mkv, m_w_mem_kv[0], v_w_mem_kv[0], "adamw_mkv")
    pad6 = lambda a: jnp.pad(a, ((0, 0), (0, D_MODEL - a.shape[1])))
    stack = lambda a, b, c_, d_: jnp.concatenate([a, b, c_, pad6(d_), jnp.zeros((4, D_MODEL), F32)], axis=0)
    d_s, nm_s, nv_s = _adamw(stack(pre_norm, post_norm, mem_norm, sink_a),
                             jnp.concatenate([g_pre, small_sum[1:]], axis=0),
                             stack(m_pre_norm, m_post_norm, m_mem_norm, m_sink_a),
                             stack(v_pre_norm, v_post_norm, v_mem_norm, v_sink_a), "adamw_small")
    ns_ = sink_a.shape[1]
    unpack = lambda a: (a[0:1], a[3:4, :ns_], a[2:3], a[1:2])
    d_pre, d_sink, d_mem, d_post = unpack(d_s)
    nm_pre, nm_sink, nm_mem, nm_post = unpack(nm_s)
    nv_pre, nv_sink, nv_mem, nv_post = unpack(nv_s)
    lead = lambda a: a[None]
    return (loss, lead(grad_x),
            g_pre, lead(g_in), g_sink, g_mem, lead(g_mkv), lead(g_out), g_post,
            d_pre, lead(d_in), d_sink, d_mem, lead(d_mkv), lead(d_out), d_post,
            nm_pre, lead(nm_in), nm_sink, nm_mem, lead(nm_mkv), lead(nm_out), nm_post,
            nv_pre, lead(nv_in), nv_sink, nv_mem, lead(nv_mkv), lead(nv_out), nv_post)
```

```python
import numpy as np
import jax
import jax.numpy as jnp
from jax import lax
from jax.experimental import pallas as pl
from jax.experimental.pallas import tpu as pltpu

F32 = jnp.float32
BF16 = jnp.bfloat16

D_MODEL = 1024
HEAD_DIM = 64
LANES = 128
BLOCK = 128
A_W, A_KV_W, B_W, C_W = 384, 128, 384, 256
N_MEM = 256
D_IN = 3072
N_CHIPS = 4
SHARD_IN = D_IN // N_CHIPS
B_CONFIGS = ((128, 1), (512, 4), (2048, 16))
B_DILS = tuple(d for _, d in B_CONFIGS)
A_WINDOW = 128
RMS_EPS = 1e-6
ROPE_THETA = 500000.0
SCALE = HEAD_DIM ** -0.5
NEG = -1e30
ADAM_LR, ADAM_B1, ADAM_B2, ADAM_EPS, ADAM_WD, ADAM_STEP = 0.001, 0.9, 0.999, 1e-08, 0.01, 10

NT = (((1,), (1,)), ((), ()))
TN = (((0,), (0,)), ((), ()))
MESH = pl.DeviceIdType.MESH

_PROJ_LAYOUT = (
    [("qa", 128 * i, True, True) for i in range(3)] + [("ka", 0, True, False), ("va", 0, False, False)]
    + [("ga", 128 * i, False, False) for i in range(3)]
    + [("qb", 128 * i, True, True) for i in range(3)] + [("kb", 128 * i, True, False) for i in range(3)]
    + [("vb", 128 * i, False, False) for i in range(3)] + [("gb", 128 * i, False, False) for i in range(3)]
    + [("qc", 128 * i, False, True) for i in range(2)] + [("gc", 128 * i, False, False) for i in range(2)]
)
_PROJ_WIDTH = dict(qa=A_W, ka=A_KV_W, va=A_KV_W, ga=A_W, qb=B_W, kb=B_W, vb=B_W, gb=B_W, qc=C_W, gc=C_W)
_NATURAL = ("qa", "ka", "va", "ga", "gb", "qc", "gc")
_DILATED = ("qb", "kb", "vb")


def _dot(a, b):
    return jnp.dot(a, b, preferred_element_type=F32)


def _dot_nt(a, b):
    return lax.dot_general(a, b, NT, preferred_element_type=F32)


def _dot_tn(a, b):
    return lax.dot_general(a, b, TN, preferred_element_type=F32)


def _half_masks(rows):
    lane = lax.broadcasted_iota(jnp.int32, (rows, LANES), 1)
    return lane < HEAD_DIM, lane >= HEAD_DIM


def _rope(t, c, sm, sp):
    return t * c + pltpu.roll(t, LANES - 8, 1) * sm + pltpu.roll(t, 8, 1) * sp


def _rope_tables(seq, tm):
    dim = jnp.arange(LANES) % HEAD_DIM
    inv_freq = ROPE_THETA ** (-jnp.arange(0, 16, 2, dtype=F32) / 16)
    freq = jnp.where(dim < 16, inv_freq[dim % 8], 0.0)[None, :]
    local = jnp.arange(tm, dtype=F32)[:, None] * freq
    base = (jnp.arange(seq // tm, dtype=F32) * tm)[:, None] * freq
    both = lambda a: jnp.concatenate([jnp.cos(a), jnp.sin(a)], axis=1)
    return both(local), jnp.repeat(both(base), 8, axis=0)


def _rope_coeffs(local_ref, base_ref):
    cl, sl = local_ref[:, :LANES], local_ref[:, LANES:]
    cb, sb = base_ref[0:1, :LANES], base_ref[0:1, LANES:]
    cos = cb * cl - sb * sl
    sin = sb * cl + cb * sl
    dim = lax.broadcasted_iota(jnp.int32, (1, LANES), 1) % HEAD_DIM
    return cos, jnp.where(dim < 8, -sin, 0.0), jnp.where((dim >= 8) & (dim < 16), sin, 0.0)


def _split3(x):
    a = x.astype(BF16)
    r = x - a.astype(F32)
    b = r.astype(BF16)
    c = (r - b.astype(F32)).astype(BF16)
    return a, b, c


def _rows_to_lanes(x):
    row = lax.broadcasted_iota(jnp.int32, (8, LANES), 0)
    lane = lax.broadcasted_iota(jnp.int32, (8, LANES), 1)
    eye = (row == lane).astype(BF16)
    a, b, c = _split3(x)
    return _dot_nt(eye, a) + _dot_nt(eye, b) + _dot_nt(eye, c)


def _head_sum_matrix(width):
    k = lax.broadcasted_iota(jnp.int32, (width, LANES), 0)
    h = lax.broadcasted_iota(jnp.int32, (width, LANES), 1)
    return (k // HEAD_DIM == h).astype(BF16)


def _head_expand_matrix(width):
    h = lax.broadcasted_iota(jnp.int32, (LANES, width), 0)
    k = lax.broadcasted_iota(jnp.int32, (LANES, width), 1)
    return (k // HEAD_DIM == h).astype(BF16)


def _dot_split(x, mat, terms):
    parts = _split3(x)[:terms]
    out = _dot(parts[0], mat)
    for p in parts[1:]:
        out = out + _dot(p, mat)
    return out


def _per_head(cols, fill=0.0):
    rows = cols[0].shape[0]
    lane = lax.broadcasted_iota(jnp.int32, (rows, LANES), 1)
    out = jnp.full((rows, LANES), fill, F32)
    for h, col in enumerate(cols):
        out = jnp.where(lane == h, col, out)
    return out


def _lane_blocks(width):
    return [slice(p * LANES, (p + 1) * LANES) for p in range(width // LANES)]


def _stage(rows, width):
    return pltpu.VMEM((width // LANES, rows, LANES), F32)


def _stage_write(buf, value):
    for p, lanes in enumerate(_lane_blocks(value.shape[1])):
        buf[p] = value[:, lanes]


def _stage_read(buf):
    return jnp.concatenate([buf[p] for p in range(buf.shape[0])], axis=1) if buf.shape[0] > 1 else buf[0]


def _to_residues(buf, out_ref, dil):
    rows = buf.shape[1] // dil
    for r in range(dil):
        for p in range(buf.shape[0]):
            plane = buf.at[p]
            out_ref[r, :, p * LANES:(p + 1) * LANES] = plane[pl.ds(r, rows, stride=dil), :].astype(out_ref.dtype)


def _from_residues(in_ref, buf, dil):
    rows = buf.shape[1] // dil
    for r in range(dil):
        for p in range(buf.shape[0]):
            plane = buf.at[p]
            plane[pl.ds(r, rows, stride=dil), :] = in_ref[r, :, p * LANES:(p + 1) * LANES].astype(F32)


def _residue_spec(dil, tm, width):
    return pl.BlockSpec((dil, tm // dil, width), lambda i: (0, i, 0))


def _gather_weights(w_in_s, w_out_s, w_mkv_s):
    shards = tuple(s.reshape(2, s.shape[0] // 2, s.shape[1]) for s in (w_in_s, w_out_s, w_mkv_s))
    n = len(shards)

    def body(*refs):
        srcs, outs = refs[:n], refs[2 * n:3 * n]
        send_sems, recv_sems = refs[3 * n:]
        x, y, c = lax.axis_index("x"), lax.axis_index("y"), lax.axis_index("c")
        my_chip = 2 * x + y
        sibling = (x, y, 1 - c)
        chips = [(1 - x, y), (x, 1 - y), (1 - x, 1 - y)]

        def half(t, chip, which):
            return outs[t].at[chip, which]

        def src_half(t, which):
            return srcs[t].at[which]

        def copy(k, src, dst, to):
            return pltpu.make_async_remote_copy(src_ref=src, dst_ref=dst, send_sem=send_sems.at[k],
                                                recv_sem=recv_sems.at[k], device_id=to, device_id_type=MESH)

        first = []
        for j, (cx, cy) in enumerate(chips):
            for t in range(n):
                first.append(copy(n * j + t, src_half(t, c), half(t, my_chip, c), (cx, cy, c)))
        for cp in first:
            cp.start()
        passed = []
        for j, (cx, cy) in enumerate(chips):
            chip = 2 * cx + cy
            for t in range(n):
                k = n * j + t
                copy(k, src_half(t, c), half(t, chip, c), (cx, cy, c)).wait_recv()
                fwd = copy(n * 3 + k, half(t, chip, c), half(t, chip, c), sibling)
                fwd.start()
                passed.append(fwd)
        for j, (cx, cy) in enumerate(chips):
            chip = 2 * cx + cy
            for t in range(n):
                k = n * 3 + n * j + t
                copy(k, half(t, chip, 1 - c), half(t, chip, 1 - c), sibling).wait_recv()
        for cp in first + passed:
            cp.wait_send()

    my_chip = 2 * lax.axis_index("x") + lax.axis_index("y")
    landing = [lax.dynamic_update_slice(jnp.zeros((N_CHIPS,) + s.shape, s.dtype), s[None], (my_chip, 0, 0, 0))
               for s in shards]
    any_spec = pl.BlockSpec(memory_space=pl.ANY)
    return pl.pallas_call(
        body, name="gather_weights",
        out_shape=[jax.ShapeDtypeStruct((N_CHIPS,) + s.shape, s.dtype) for s in shards],
        in_specs=[any_spec] * (2 * n), out_specs=[any_spec] * n,
        input_output_aliases={n + t: t for t in range(n)},
        scratch_shapes=[pltpu.SemaphoreType.DMA((6 * n,)), pltpu.SemaphoreType.DMA((6 * n,))],
    )(*shards, *landing)


def _mem_kv(mem, mem_norm, w_mkv):
    def body(mem_ref, g_ref, w_ref, mk_ref, mv_ref):
        m = mem_ref[...]
        r = lax.rsqrt(jnp.mean(m * m, axis=-1, keepdims=True) + RMS_EPS)
        mn = (m * r * g_ref[...]).astype(BF16)
        kv = _dot(mn, w_ref[...])
        mk_ref[...] = kv[:, :C_W].astype(BF16)
        mv_ref[...] = kv[:, C_W:].astype(BF16)

    return pl.pallas_call(
        body, name="mem_kv",
        out_shape=[jax.ShapeDtypeStruct((N_MEM, C_W), BF16)] * 2,
    )(mem, mem_norm, w_mkv)


def _mem_kv_bwd(mem, mem_norm, w_mkv, dmk, dmv):
    def body(mem_ref, g_ref, w_ref, dmk_ref, dmv_ref, gw_ref, gn_ref):
        m = mem_ref[...]
        r = lax.rsqrt(jnp.mean(m * m, axis=-1, keepdims=True) + RMS_EPS)
        mhat = m * r
        mn = (mhat * g_ref[...]).astype(BF16)
        dkv = jnp.concatenate([dmk_ref[...], dmv_ref[...]], axis=1).astype(BF16)
        gw_ref[...] = _dot_tn(mn, dkv)
        dmn = _dot_nt(dkv, w_ref[...])
        gn_ref[...] = jnp.sum(dmn * mhat, axis=0, keepdims=True)

    return pl.pallas_call(
        body, name="mem_kv_bwd",
        out_shape=[jax.ShapeDtypeStruct((D_MODEL, 2 * C_W), F32), jax.ShapeDtypeStruct((1, D_MODEL), F32)],
    )(mem, mem_norm, w_mkv, dmk, dmv)


def _pre_proj(x, pre_norm, w_in_g):
    seq = x.shape[0]
    tm = min(512, seq)
    n_nat, n_dil = len(_NATURAL), len(_DILATED) * len(B_DILS)
    rope = _rope_tables(seq, tm)

    def body(x_ref, g_ref, w_ref, rl_ref, rb_ref, *refs):
        nat = dict(zip(_NATURAL, refs[:n_nat]))
        res = {n: refs[n_nat + len(B_DILS) * k:n_nat + len(B_DILS) * (k + 1)] for k, n in enumerate(_DILATED)}
        ut = refs[n_nat + n_dil]
        bufs = dict(zip(_DILATED, refs[n_nat + n_dil + 1:]))
        xv = x_ref[...]
        r = lax.rsqrt(jnp.mean(xv * xv, axis=-1, keepdims=True) + RMS_EPS)
        u = xv * r * g_ref[...]
        ub = u.astype(BF16)
        ut[...] = u.T.astype(BF16)
        c, sm, sp = _rope_coeffs(rl_ref, rb_ref)
        for j in range(N_CHIPS):
            pj = _dot(ub, w_ref[j])
            for b in range(SHARD_IN // LANES):
                name, off, roped, scaled = _PROJ_LAYOUT[(SHARD_IN // LANES) * j + b]
                piece = pj[:, LANES * b:LANES * (b + 1)]
                if roped:
                    piece = _rope(piece, c, sm, sp)
                if scaled:
                    piece = piece * SCALE
                if name in bufs:
                    bufs[name][off // LANES] = piece
                else:
                    nat[name][:, off:off + LANES] = piece.astype(BF16)
        for name in _DILATED:
            for ref, dil in zip(res[name], B_DILS):
                _to_residues(bufs[name], ref, dil)

    row = lambda w: pl.BlockSpec((tm, w), lambda i: (i, 0))
    full = lambda a: pl.BlockSpec(a.shape, lambda i: (0,) * a.ndim)
    out_shape = [jax.ShapeDtypeStruct((seq, _PROJ_WIDTH[n]), BF16) for n in _NATURAL]
    out_specs = [row(_PROJ_WIDTH[n]) for n in _NATURAL]
    for n in _DILATED:
        for dil in B_DILS:
            out_shape.append(jax.ShapeDtypeStruct((dil, seq // dil, B_W), BF16))
            out_specs.append(_residue_spec(dil, tm, B_W))
    out_shape.append(jax.ShapeDtypeStruct((D_MODEL, seq), BF16))
    out_specs.append(pl.BlockSpec((D_MODEL, tm), lambda i: (0, i)))
    res = pl.pallas_call(
        body, name="pre_proj", grid=(seq // tm,),
        in_specs=[row(D_MODEL), full(pre_norm), full(w_in_g), full(rope[0]), pl.BlockSpec((8, 2 * LANES), lambda i: (i, 0))],
        out_specs=out_specs, out_shape=out_shape,
        scratch_shapes=[_stage(tm, B_W)] * len(_DILATED),
    )(x, pre_norm, w_in_g, *rope)
    out = dict(zip(_NATURAL, res[:n_nat]))
    for k, n in enumerate(_DILATED):
        out[n] = res[n_nat + len(B_DILS) * k:n_nat + len(B_DILS) * (k + 1)]
    out["ut"] = res[n_nat + n_dil]
    return out


def _band_bias(max_dist, transposed):
    i = np.arange(BLOCK)[:, None]
    j = np.arange(BLOCK)[None, :]
    if transposed:
        same = i <= j
        other = (j + BLOCK - i) <= max_dist
        vis = np.concatenate([same, other], axis=1)
    else:
        prev = (i + BLOCK - j) <= max_dist
        same = j <= i
        vis = np.concatenate([prev, same], axis=1)
    return jnp.asarray(np.where(vis, 0.0, NEG).astype(np.float32))


def _kv_place(h, gqa):
    return (0, h // 3) if gqa else (h // 2, h % 2)


def _band_fwd(q, k, v, sink, *, max_dist, name):
    dil, length, wq = q.shape
    wk = k.shape[2]
    gqa = wk != wq
    tq = min(512, length)
    ns, nt = tq // BLOCK, length // tq
    npair = wq // LANES
    bias = _band_bias(max_dist, transposed=False)
    has_sink = sink is not None

    def body(*refs):
        if has_sink:
            sink_ref, refs = refs[0], refs[1:]
        q_ref, k_ref, kp_ref, v_ref, vp_ref, bias_ref, o_ref, lse_ref, kbuf, vbuf = refs[:10]
        i = pl.program_id(1)
        kbuf[0:BLOCK] = kp_ref[...]
        kbuf[BLOCK:] = k_ref[...]
        vbuf[0:BLOCK] = vp_ref[...]
        vbuf[BLOCK:] = v_ref[...]
        if gqa:
            kroll, vroll = refs[10:12]
            kroll[...] = pltpu.roll(kbuf[...], HEAD_DIM, 1)
            vroll[...] = pltpu.roll(vbuf[...], HEAD_DIM, 1)
        half = _half_masks(BLOCK)
        col_prev = (lax.broadcasted_iota(jnp.int32, (1, 2 * BLOCK), 1) < BLOCK).astype(F32)

        def sub(a, carry):
            r0 = pl.multiple_of(a * BLOCK, BLOCK)
            pen = jnp.where((i == 0) & (a == 0), NEG, 0.0)
            b = bias_ref[...] + pen * col_prev
            scores = []
            for p in range(npair):
                qp = q_ref[pl.ds(r0, BLOCK), p * LANES:(p + 1) * LANES]
                for e in range(2):
                    pk, ek = _kv_place(2 * p + e, gqa)
                    kw = (kbuf if ek == e else kroll)[pl.ds(r0, 2 * BLOCK), pk * LANES:(pk + 1) * LANES]
                    scores.append(_dot_nt(jnp.where(half[e], qp, jnp.zeros_like(qp)), kw))
            m_cols, l_cols, probs = [], [], []
            for h, s in enumerate(scores):
                s = s + b
                m = jnp.max(s, axis=1, keepdims=True)
                if has_sink:
                    m = jnp.maximum(m, sink_ref[h])
                pe = jnp.exp(s - m)
                l = jnp.sum(pe, axis=1, keepdims=True)
                if has_sink:
                    l = l + jnp.exp(sink_ref[h] - m)
                probs.append(pe.astype(BF16))
                m_cols.append(m)
                l_cols.append(l)
            for p in range(npair):
                o_h = []
                for e in range(2):
                    h = 2 * p + e
                    pk, ek = _kv_place(h, gqa)
                    vw = (vbuf if ek == e else vroll)[pl.ds(r0, 2 * BLOCK), pk * LANES:(pk + 1) * LANES]
                    o_h.append(_dot(probs[h], vw) * (1.0 / l_cols[h]))
                o_ref[pl.ds(r0, BLOCK), p * LANES:(p + 1) * LANES] = jnp.where(half[0], o_h[0], o_h[1]).astype(BF16)
            lse_ref[pl.ds(r0, BLOCK), :] = _per_head(m_cols) + jnp.log(_per_head(l_cols, 1.0))
            return carry

        lax.fori_loop(0, ns, sub, 0, unroll=True)

    main = lambda w: pl.BlockSpec((None, tq, w), lambda r, i: (r, i, 0))
    prev = lambda w: pl.BlockSpec((None, BLOCK, w), lambda r, i: (r, jnp.maximum(i * ns - 1, 0), 0))
    in_specs = [main(wq), main(wk), prev(wk), main(wk), prev(wk), pl.BlockSpec(bias.shape, lambda r, i: (0, 0))]
    args = [q, k, k, v, v, bias]
    if has_sink:
        in_specs = [pl.BlockSpec(memory_space=pltpu.SMEM)] + in_specs
        args = [sink] + args
    scratch = [pltpu.VMEM((tq + BLOCK, wk), BF16)] * (4 if gqa else 2)
    return pl.pallas_call(
        body, name=name, grid=(dil, nt), in_specs=in_specs,
        out_specs=[main(wq), main(LANES)],
        out_shape=[jax.ShapeDtypeStruct((dil, length, wq), BF16), jax.ShapeDtypeStruct((dil, length, LANES), F32)],
        scratch_shapes=scratch,
    )(*args)


def _band_bwd(q, k, v, do, lse, delta, *, max_dist, name):
    dil, length, wq = q.shape
    wk = k.shape[2]
    gqa = wk != wq
    tq = min(512, length)
    ns, nt = tq // BLOCK, length // tq
    npair = wq // LANES
    nblocks = length // BLOCK
    bias = _band_bias(max_dist, transposed=True)

    def body(q_ref, qn_ref, do_ref, don_ref, lse_ref, lsen_ref, dl_ref, dln_ref, k_ref, v_ref, bias_ref,
             dq_ref, dk_ref, dv_ref, qbuf, dobuf, stat_l, stat_d, dqt, kt, *rolled):
        i = pl.program_id(1)
        qbuf[0:tq] = q_ref[...]
        qbuf[tq:] = qn_ref[...]
        dobuf[0:tq] = do_ref[...]
        dobuf[tq:] = don_ref[...]
        for pk in range(wk // LANES):
            kt[pk] = k_ref[:, pk * LANES:(pk + 1) * LANES].astype(F32).T.astype(BF16)
        if gqa:
            kroll, vroll, ktroll = rolled
            kroll[...] = pltpu.roll(k_ref[...], HEAD_DIM, 1)
            vroll[...] = pltpu.roll(v_ref[...], HEAD_DIM, 1)
            ktroll[0] = kroll[...].astype(F32).T.astype(BF16)
        for a in range(ns):
            rows = slice(a * BLOCK, (a + 1) * BLOCK)
            stat_l[a] = _rows_to_lanes(lse_ref[rows, :])
            stat_d[a] = _rows_to_lanes(dl_ref[rows, :])
        stat_l[ns] = _rows_to_lanes(lsen_ref[...])
        stat_d[ns] = _rows_to_lanes(dln_ref[...])

        @pl.when(i == 0)
        def _():
            dqt[:, :, 0:BLOCK] = jnp.zeros((npair, LANES, BLOCK), F32)

        @pl.when(i > 0)
        def _():
            dqt[:, :, 0:BLOCK] = dqt[:, :, tq:tq + BLOCK]

        dqt[:, :, BLOCK:] = jnp.zeros((npair, LANES, tq), F32)
        half2 = _half_masks(2 * BLOCK)
        row = lax.broadcasted_iota(jnp.int32, (LANES, BLOCK), 0)
        row_half = (row < HEAD_DIM, row >= HEAD_DIM)
        col_next = (lax.broadcasted_iota(jnp.int32, (1, 2 * BLOCK), 1) >= BLOCK).astype(F32)

        for b in range(ns):
            rows = slice(b * BLOCK, (b + 1) * BLOCK)
            window = slice(b * BLOCK, (b + 2) * BLOCK)
            bt = bias_ref[...]
            if b == ns - 1:
                bt = bt + jnp.where(i == nt - 1, NEG, 0.0) * col_next
            acc = {}
            items = []
            for p in range(npair):
                lanes = slice(p * LANES, (p + 1) * LANES)
                qw = qbuf[window, lanes]
                dow = dobuf[window, lanes]
                for e in range(2):
                    h = 2 * p + e
                    pk, ek = _kv_place(h, gqa)
                    klanes = slice(pk * LANES, (pk + 1) * LANES)
                    kb = (k_ref if ek == e else kroll)[rows, klanes]
                    vb = (v_ref if ek == e else vroll)[rows, klanes]
                    qm = jnp.where(half2[e], qw, jnp.zeros_like(qw))
                    dom = jnp.where(half2[e], dow, jnp.zeros_like(dow))
                    items.append(dict(p=p, e=e, h=h, pk=pk, ek=ek, qm=qm, dom=dom,
                                      st=_dot_nt(kb, qm), dpt=_dot_nt(vb, dom)))
            for it in items:
                h = it["h"]
                lrow = jnp.concatenate([stat_l[b, h:h + 1, :], stat_l[b + 1, h:h + 1, :]], axis=1)
                drow = jnp.concatenate([stat_d[b, h:h + 1, :], stat_d[b + 1, h:h + 1, :]], axis=1)
                pt = jnp.exp(it["st"] + bt - lrow)
                it["ptb"] = pt.astype(BF16)
                it["dsb"] = (pt * (it["dpt"] - drow)).astype(BF16)
            for it in items:
                p, e, pk, ek = it["p"], it["e"], it["pk"], it["ek"]
                dv_c = _dot(it["ptb"], it["dom"])
                dk_c = _dot(it["dsb"], it["qm"])
                kbt = (kt if ek == e else ktroll)[pk, :, rows]
                kbtm = jnp.where(row_half[e], kbt, jnp.zeros_like(kbt))
                dqt[p, :, window] += _dot(kbtm, it["dsb"])
                key = (pk, ek == e)
                if key in acc:
                    acc[key] = (acc[key][0] + dk_c, acc[key][1] + dv_c)
                else:
                    acc[key] = (dk_c, dv_c)
            if not gqa:
                for p in range(npair):
                    lanes = slice(p * LANES, (p + 1) * LANES)
                    dk_ref[rows, lanes] = acc[(p, True)][0].astype(BF16)
                    dv_ref[rows, lanes] = acc[(p, True)][1].astype(BF16)
            if gqa:
                dk_al, dv_al = acc[(0, True)]
                dk_mis, dv_mis = acc[(0, False)]
                dk_ref[rows, :] = (dk_al + pltpu.roll(dk_mis, HEAD_DIM, 1)).astype(BF16)
                dv_ref[rows, :] = (dv_al + pltpu.roll(dv_mis, HEAD_DIM, 1)).astype(BF16)

        for p in range(npair):
            dq_ref[:, p * LANES:(p + 1) * LANES] = dqt[p, :, 0:tq].T.astype(BF16)

    main = lambda w: pl.BlockSpec((None, tq, w), lambda r, i: (r, i, 0))
    nxt = lambda w: pl.BlockSpec((None, BLOCK, w), lambda r, i: (r, jnp.minimum((i + 1) * ns, nblocks - 1), 0))
    scratch = [pltpu.VMEM((tq + BLOCK, wq), BF16), pltpu.VMEM((tq + BLOCK, wq), BF16),
               pltpu.VMEM((ns + 1, 8, LANES), F32), pltpu.VMEM((ns + 1, 8, LANES), F32),
               pltpu.VMEM((npair, LANES, tq + BLOCK), F32), pltpu.VMEM((wk // LANES, LANES, tq), BF16)]
    if gqa:
        scratch = scratch + [pltpu.VMEM((tq, wk), BF16)] * 2 + [pltpu.VMEM((1, LANES, tq), BF16)]
    return pl.pallas_call(
        body, name=name, grid=(dil, nt),
        in_specs=[main(wq), nxt(wq), main(wq), nxt(wq), main(LANES), nxt(LANES), main(LANES), nxt(LANES),
                  main(wk), main(wk), pl.BlockSpec(bias.shape, lambda r, i: (0, 0))],
        out_specs=[main(wq), main(wk), main(wk)],
        out_shape=[jax.ShapeDtypeStruct((dil, length, wq), BF16), jax.ShapeDtypeStruct((dil, length, wk), BF16),
                   jax.ShapeDtypeStruct((dil, length, wk), BF16)],
        scratch_shapes=scratch,
        compiler_params=pltpu.CompilerParams(dimension_semantics=("arbitrary", "arbitrary")),
    )(q, q, do, do, lse, lse, delta, delta, k, v, bias)


def _mem_attn_fwd(q, mk, mv):
    seq = q.shape[0]
    tq = min(512, seq)
    ns = tq // BLOCK

    def body(q_ref, mk_ref, mv_ref, o_ref, lse_ref):
        half = _half_masks(BLOCK)

        def sub(a, carry):
            r0 = pl.multiple_of(a * BLOCK, BLOCK)
            scores = []
            for p in range(C_W // LANES):
                lanes = slice(p * LANES, (p + 1) * LANES)
                qp = q_ref[pl.ds(r0, BLOCK), lanes]
                for e in range(2):
                    scores.append(_dot_nt(jnp.where(half[e], qp, jnp.zeros_like(qp)), mk_ref[:, lanes]))
            m_cols, l_cols, probs = [], [], []
            for s in scores:
                m = jnp.max(s, axis=1, keepdims=True)
                pe = jnp.exp(s - m)
                probs.append(pe.astype(BF16))
                m_cols.append(m)
                l_cols.append(jnp.sum(pe, axis=1, keepdims=True))
            for p in range(C_W // LANES):
                lanes = slice(p * LANES, (p + 1) * LANES)
                o_h = [_dot(probs[2 * p + e], mv_ref[:, lanes]) * (1.0 / l_cols[2 * p + e]) for e in range(2)]
                o_ref[pl.ds(r0, BLOCK), lanes] = jnp.where(half[0], o_h[0], o_h[1]).astype(BF16)
            lse_ref[pl.ds(r0, BLOCK), :] = _per_head(m_cols) + jnp.log(_per_head(l_cols, 1.0))
            return carry

        lax.fori_loop(0, ns, sub, 0, unroll=True)

    row = lambda w: pl.BlockSpec((tq, w), lambda i: (i, 0))
    full = pl.BlockSpec((N_MEM, C_W), lambda i: (0, 0))
    return pl.pallas_call(
        body, name="mem_attn_fwd", grid=(seq // tq,), in_specs=[row(C_W), full, full],
        out_specs=[row(C_W), row(LANES)],
        out_shape=[jax.ShapeDtypeStruct((seq, C_W), BF16), jax.ShapeDtypeStruct((seq, LANES), F32)],
    )(q, mk, mv)


def _mem_attn_bwd(q, mk, mv, do, lse, delta):
    seq = q.shape[0]
    tq = min(512, seq)
    ns = tq // BLOCK
    npair = C_W // LANES

    def body(q_ref, mk_ref, mv_ref, do_ref, lse_ref, dl_ref, dq_ref, dmk_ref, dmv_ref, stat_l, stat_d, mkt, dqt):
        @pl.when(pl.program_id(0) == 0)
        def _():
            dmk_ref[...] = jnp.zeros_like(dmk_ref)
            dmv_ref[...] = jnp.zeros_like(dmv_ref)
            for p in range(npair):
                mkt[p] = mk_ref[:, p * LANES:(p + 1) * LANES].astype(F32).T.astype(BF16)

        for a in range(ns):
            rows = slice(a * BLOCK, (a + 1) * BLOCK)
            stat_l[a] = _rows_to_lanes(lse_ref[rows, :])
            stat_d[a] = _rows_to_lanes(dl_ref[rows, :])
        half = _half_masks(BLOCK)
        row = lax.broadcasted_iota(jnp.int32, (LANES, N_MEM), 0)
        row_half = (row < HEAD_DIM, row >= HEAD_DIM)

        for a in range(ns):
            rows = slice(a * BLOCK, (a + 1) * BLOCK)
            items = []
            for p in range(npair):
                lanes = slice(p * LANES, (p + 1) * LANES)
                qp = q_ref[rows, lanes]
                dop = do_ref[rows, lanes]
                for e in range(2):
                    qm = jnp.where(half[e], qp, jnp.zeros_like(qp))
                    dom = jnp.where(half[e], dop, jnp.zeros_like(dop))
                    items.append(dict(p=p, e=e, qm=qm, dom=dom, st=_dot_nt(mk_ref[:, lanes], qm),
                                      dpt=_dot_nt(mv_ref[:, lanes], dom)))
            for it in items:
                h = 2 * it["p"] + it["e"]
                pt = jnp.exp(it["st"] - stat_l[a, h:h + 1, :])
                it["ptb"] = pt.astype(BF16)
                it["dsb"] = (pt * (it["dpt"] - stat_d[a, h:h + 1, :])).astype(BF16)
            for p in range(npair):
                lanes = slice(p * LANES, (p + 1) * LANES)
                pair = [it for it in items if it["p"] == p]
                dmv_ref[:, lanes] += _dot(pair[0]["ptb"], pair[0]["dom"]) + _dot(pair[1]["ptb"], pair[1]["dom"])
                dmk_ref[:, lanes] += _dot(pair[0]["dsb"], pair[0]["qm"]) + _dot(pair[1]["dsb"], pair[1]["qm"])
                kbt = mkt[p]
                dqt[p, :, rows] = (_dot(jnp.where(row_half[0], kbt, jnp.zeros_like(kbt)), pair[0]["dsb"])
                                   + _dot(jnp.where(row_half[1], kbt, jnp.zeros_like(kbt)), pair[1]["dsb"]))
        for p in range(npair):
            dq_ref[:, p * LANES:(p + 1) * LANES] = dqt[p].T.astype(BF16)

    row = lambda w: pl.BlockSpec((tq, w), lambda i: (i, 0))
    full = pl.BlockSpec((N_MEM, C_W), lambda i: (0, 0))
    return pl.pallas_call(
        body, name="mem_attn_bwd", grid=(seq // tq,),
        in_specs=[row(C_W), full, full, row(C_W), row(LANES), row(LANES)], out_specs=[row(C_W), full, full],
        out_shape=[jax.ShapeDtypeStruct((seq, C_W), BF16), jax.ShapeDtypeStruct((N_MEM, C_W), F32),
                   jax.ShapeDtypeStruct((N_MEM, C_W), F32)],
        scratch_shapes=[pltpu.VMEM((ns, 8, LANES), F32)] * 2
        + [pltpu.VMEM((npair, LANES, N_MEM), BF16), pltpu.VMEM((npair, LANES, tq), F32)],
        compiler_params=pltpu.CompilerParams(dimension_semantics=("arbitrary",)),
    )(q, mk, mv, do, lse, delta)


def _silu_and_grad(g):
    s = 1.0 / (1.0 + jnp.exp(-g))
    return g * s, s * (1.0 + g * (1.0 - s))


def _post(x, target, post_norm, w_out, sink_row, oa, lse_a, ga, ob_list, lseb_list, gb, oc, gc):
    seq = x.shape[0]
    tm = min(512, seq)
    inv_d = 1.0 / D_MODEL
    nd = len(B_DILS)

    def body(*refs):
        (x_ref, t_ref, gp_ref, w_ref, wt_ref, sink_ref, oa_ref, lsea_ref, ga_ref), refs = refs[:9], refs[9:]
        ob_refs, lb_refs, (gb_ref, oc_ref, gc_ref), refs = refs[:nd], refs[nd:2 * nd], refs[2 * nd:2 * nd + 3], refs[2 * nd + 3:]
        (g_ref, doa_ref, dla_ref, dga_ref), refs = refs[:4], refs[4:]
        dob_refs, lsec_refs, dlb_refs, refs = refs[:nd], refs[nd:2 * nd], refs[2 * nd:3 * nd], refs[3 * nd:]
        (dgb_ref, doc_ref, dlc_ref, dgc_ref, gw_ref, gpost_ref, gsink_ref, loss_ref), refs = refs[:8], refs[8:]
        ycat, obufs, lbufs, st_do, st_l, st_d = refs[0], refs[1:nd], refs[nd:2 * nd - 1], refs[2 * nd - 1], refs[2 * nd], refs[2 * nd + 1]

        @pl.when(pl.program_id(0) == 0)
        def _():
            gw_ref[...] = jnp.zeros_like(gw_ref)
            gpost_ref[...] = jnp.zeros_like(gpost_ref)
            gsink_ref[...] = jnp.zeros_like(gsink_ref)
            loss_ref[...] = jnp.zeros_like(loss_ref)

        o_i, l_i = [ob_refs[0][0].astype(F32)], [lb_refs[0][0]]
        for k in range(1, nd):
            _from_residues(ob_refs[k], obufs[k - 1], B_DILS[k])
            _from_residues(lb_refs[k], lbufs[k - 1], B_DILS[k])
            o_i.append(_stage_read(obufs[k - 1]))
            l_i.append(_stage_read(lbufs[k - 1]))
        mx = l_i[0]
        for l in l_i[1:]:
            mx = jnp.maximum(mx, l)
        w_i = [jnp.exp(l - mx) for l in l_i]
        z = w_i[0]
        for w in w_i[1:]:
            z = z + w
        _stage_write(st_l, mx + jnp.log(z))
        expand = _head_expand_matrix(B_W)
        inv_z = 1.0 / z
        ob = None
        for w, o in zip(w_i, o_i):
            term = _dot_split(w * inv_z, expand, 2) * o
            ob = term if ob is None else ob + term
        oa, oc = oa_ref[...].astype(F32), oc_ref[...].astype(F32)
        sa, dsa = _silu_and_grad(ga_ref[...].astype(F32))
        sb, dsb = _silu_and_grad(gb_ref[...].astype(F32))
        sc, dsc = _silu_and_grad(gc_ref[...].astype(F32))
        ycat[:, 0:A_W] = (oa * sa).astype(BF16)
        ycat[:, A_W:A_W + B_W] = (ob * sb).astype(BF16)
        ycat[:, A_W + B_W:] = (oc * sc).astype(BF16)
        y2 = _dot(ycat[...], w_ref[...])
        r = lax.rsqrt(jnp.mean(y2 * y2, axis=-1, keepdims=True) + RMS_EPS)
        zhat = y2 * r
        gp = gp_ref[...]
        err = x_ref[...] + zhat * gp - t_ref[...]
        loss_ref[...] += jnp.sum(err * err) * (0.5 * inv_d)
        g = err * inv_d
        g_ref[...] = g
        gpost_ref[...] += jnp.sum(g * zhat, axis=0, keepdims=True)
        a = g * gp
        dy2 = (r * (a - zhat * jnp.mean(a * zhat, axis=-1, keepdims=True))).astype(BF16)
        gw_ref[...] += _dot_tn(ycat[...], dy2)
        dycat = _dot(dy2, wt_ref[...])
        dya, dyb, dyc = dycat[:, 0:A_W], dycat[:, A_W:A_W + B_W], dycat[:, A_W + B_W:]
        doa, dob, doc = dya * sa, dyb * sb, dyc * sc
        doa_ref[...] = doa.astype(BF16)
        doc_ref[...] = doc.astype(BF16)
        dga_ref[...] = (dya * oa * dsa).astype(BF16)
        dgb_ref[...] = (dyb * ob * dsb).astype(BF16)
        dgc_ref[...] = (dyc * oc * dsc).astype(BF16)
        dl_a = _dot_split(doa * oa, _head_sum_matrix(A_W), 2)
        dla_ref[...] = dl_a
        dlc_ref[...] = _dot_split(doc * oc, _head_sum_matrix(C_W), 2)
        gsink_ref[...] += jnp.sum(jnp.exp(sink_ref[...] - lsea_ref[...]) * dl_a, axis=0, keepdims=True)
        _stage_write(st_do, dob)
        _stage_write(st_d, _dot_split(dob * ob, _head_sum_matrix(B_W), 2))
        for k, dil in enumerate(B_DILS):
            _to_residues(st_do, dob_refs[k], dil)
            _to_residues(st_l, lsec_refs[k], dil)
            _to_residues(st_d, dlb_refs[k], dil)

    row = lambda w: pl.BlockSpec((tm, w), lambda i: (i, 0))
    full = lambda shape: pl.BlockSpec(shape, lambda i: (0,) * len(shape))
    res_specs = lambda w: [_residue_spec(d, tm, w) for d in B_DILS]
    res_shapes = lambda w, dt: [jax.ShapeDtypeStruct((d, seq // d, w), dt) for d in B_DILS]
    ins = [x, target, post_norm, w_out, w_out.T, sink_row, oa, lse_a, ga, *ob_list, *lseb_list, gb, oc, gc]
    in_specs = ([row(D_MODEL), row(D_MODEL), full((1, D_MODEL)), full((D_MODEL, D_MODEL)), full((D_MODEL, D_MODEL)),
                 full((1, LANES)),
                 row(A_W), row(LANES), row(A_W)] + res_specs(B_W) + res_specs(LANES) + [row(B_W), row(C_W), row(C_W)])
    out_shape = ([jax.ShapeDtypeStruct((seq, D_MODEL), F32), jax.ShapeDtypeStruct((seq, A_W), BF16),
                  jax.ShapeDtypeStruct((seq, LANES), F32), jax.ShapeDtypeStruct((seq, A_W), BF16)]
                 + res_shapes(B_W, BF16) + res_shapes(LANES, F32) + res_shapes(LANES, F32)
                 + [jax.ShapeDtypeStruct((seq, B_W), BF16), jax.ShapeDtypeStruct((seq, C_W), BF16),
                    jax.ShapeDtypeStruct((seq, LANES), F32), jax.ShapeDtypeStruct((seq, C_W), BF16),
                    jax.ShapeDtypeStruct((D_MODEL, D_MODEL), F32), jax.ShapeDtypeStruct((1, D_MODEL), F32),
                    jax.ShapeDtypeStruct((1, LANES), F32), jax.ShapeDtypeStruct((1, LANES), F32)])
    out_specs = ([row(D_MODEL), row(A_W), row(LANES), row(A_W)] + res_specs(B_W) + res_specs(LANES) + res_specs(LANES)
                 + [row(B_W), row(C_W), row(LANES), row(C_W),
                    full((D_MODEL, D_MODEL)), full((1, D_MODEL)), full((1, LANES)), full((1, LANES))])
    scratch = ([pltpu.VMEM((tm, D_MODEL), BF16)] + [_stage(tm, B_W)] * (nd - 1) + [_stage(tm, LANES)] * (nd - 1)
               + [_stage(tm, B_W), _stage(tm, LANES), _stage(tm, LANES)])
    res = pl.pallas_call(
        body, name="post", grid=(seq // tm,), in_specs=in_specs, out_specs=out_specs, out_shape=out_shape,
        scratch_shapes=scratch,
        compiler_params=pltpu.CompilerParams(dimension_semantics=("arbitrary",)),
    )(*ins)
    out = dict(g=res[0], doa=res[1], dl_a=res[2], dga=res[3], dob=res[4:4 + nd], lse_b=res[4 + nd:4 + 2 * nd],
               dl_b=res[4 + 2 * nd:4 + 3 * nd])
    rest = res[4 + 3 * nd:]
    out.update(dgb=rest[0], doc=rest[1], dl_c=rest[2], dgc=rest[3], gw_out=rest[4], gpost=rest[5], gsink=rest[6],
               loss=rest[7])
    return out


def _grad_w_in(ut, nat, res):
    seq = ut.shape[1]
    tm = min(512, seq)
    nd = len(B_DILS)
    nat_list = [nat[n] for n in _NATURAL]
    res_list = [a for n in _DILATED for a in res[n]]
    rope = _rope_tables(seq, tm)

    def body(rl_ref, rb_ref, ut_ref, *refs):
        nat_refs = dict(zip(_NATURAL, refs[:len(_NATURAL)]))
        refs = refs[len(_NATURAL):]
        res_refs = {n: refs[nd * k:nd * (k + 1)] for k, n in enumerate(_DILATED)}
        refs = refs[nd * len(_DILATED):]
        dproj_ref, gw_ref = refs[:2]
        bufs = {n: refs[2 + (nd - 1) * k:2 + (nd - 1) * (k + 1)] for k, n in enumerate(_DILATED)}

        @pl.when(pl.program_id(0) == 0)
        def _():
            gw_ref[...] = jnp.zeros_like(gw_ref)

        for n in _DILATED:
            for k in range(1, nd):
                _from_residues(res_refs[n][k], bufs[n][k - 1], B_DILS[k])
        c, sm, sp = _rope_coeffs(rl_ref, rb_ref)
        sm, sp = -sm, -sp
        for blk, (name, off, roped, scaled) in enumerate(_PROJ_LAYOUT):
            lanes = slice(off, off + LANES)
            if name in nat_refs:
                piece = nat_refs[name][:, lanes].astype(F32)
            else:
                piece = res_refs[name][0][0, :, lanes].astype(F32)
                for buf in bufs[name]:
                    piece = piece + buf[off // LANES]
            if roped:
                piece = _rope(piece, c, sm, sp)
            if scaled:
                piece = piece * SCALE
            dproj_ref[:, blk * LANES:(blk + 1) * LANES] = piece.astype(BF16)
        for j in range(N_CHIPS):
            gw_ref[j] += _dot(ut_ref[...], dproj_ref[:, j * SHARD_IN:(j + 1) * SHARD_IN])

    row = lambda w: pl.BlockSpec((tm, w), lambda i: (i, 0))
    in_specs = ([pl.BlockSpec(rope[0].shape, lambda i: (0, 0)), pl.BlockSpec((8, 2 * LANES), lambda i: (i, 0)),
                 pl.BlockSpec((D_MODEL, tm), lambda i: (0, i))]
                + [row(a.shape[1]) for a in nat_list]
                + [_residue_spec(d, tm, B_W) for _ in _DILATED for d in B_DILS])
    return pl.pallas_call(
        body, name="grad_w_in", grid=(seq // tm,), in_specs=in_specs,
        out_specs=[row(D_IN), pl.BlockSpec((N_CHIPS, D_MODEL, SHARD_IN), lambda i: (0, 0, 0))],
        out_shape=[jax.ShapeDtypeStruct((seq, D_IN), BF16), jax.ShapeDtypeStruct((N_CHIPS, D_MODEL, SHARD_IN), F32)],
        scratch_shapes=[_stage(tm, B_W)] * ((nd - 1) * len(_DILATED)),
        compiler_params=pltpu.CompilerParams(dimension_semantics=("arbitrary",)),
    )(*rope, ut, *nat_list, *res_list)


def _input_grad(x, g, pre_norm, w_in_g, dproj, gx_prev, span, host, name):
    seq = x.shape[0]
    tm = seq // 16
    first_block, steps = span
    n_host_in = len(host["ins"]) if host else 0
    n_host_out = len(host["outs"]) if host else 0

    def body(*refs):
        x_ref, g_ref, gp_ref, w_ref, dp_ref = refs[:5]
        refs = refs[5 + (gx_prev is not None):]
        host_in, refs = refs[:n_host_in], refs[n_host_in:]
        gx_ref, gpre_ref = refs[:2]
        host_out, sems = refs[2:2 + n_host_out], refs[2 + n_host_out:]
        step = pl.program_id(0)

        @pl.when(step == 0)
        def _():
            gpre_ref[...] = jnp.zeros_like(gpre_ref)
            if host:
                for cp in host["build"](host_in, host_out, *sems):
                    cp.start()

        du = None
        for j in range(N_CHIPS):
            term = _dot(dp_ref[:, j * SHARD_IN:(j + 1) * SHARD_IN], w_ref[j])
            du = term if du is None else du + term
        xv = x_ref[...]
        r = lax.rsqrt(jnp.mean(xv * xv, axis=-1, keepdims=True) + RMS_EPS)
        xhat = xv * r
        gpre_ref[...] += jnp.sum(du * xhat, axis=0, keepdims=True)
        a = du * gp_ref[...]
        gx_ref[...] = g_ref[...] + r * (a - xhat * jnp.mean(a * xhat, axis=-1, keepdims=True))

        if host:
            @pl.when(step == steps - 1)
            def _():
                for cp in host["build"](host_in, host_out, *sems):
                    cp.wait()

    row = lambda w: pl.BlockSpec((tm, w), lambda i: (first_block + i, 0))
    full = lambda a: pl.BlockSpec(a.shape, lambda i: (0,) * a.ndim)
    any_spec = pl.BlockSpec(memory_space=pl.ANY)
    ins = [x, g, pre_norm, w_in_g, dproj]
    in_specs = [row(D_MODEL), row(D_MODEL), full(pre_norm), full(w_in_g), row(D_IN)]
    aliases = {}
    if gx_prev is not None:
        aliases[len(ins)] = 0
        ins.append(gx_prev)
        in_specs.append(any_spec)
    out_shape = [jax.ShapeDtypeStruct((seq, D_MODEL), F32), jax.ShapeDtypeStruct((1, D_MODEL), F32)]
    out_specs = [row(D_MODEL), pl.BlockSpec((1, D_MODEL), lambda i: (0, 0))]
    scratch = []
    if host:
        ins += list(host["ins"])
        in_specs += [any_spec] * n_host_in
        out_shape += list(host["outs"])
        out_specs += [any_spec] * n_host_out
        scratch = list(host["sems"])
    return pl.pallas_call(
        body, name=name, grid=(steps,), in_specs=in_specs, out_specs=out_specs, out_shape=out_shape,
        input_output_aliases=aliases, scratch_shapes=scratch,
        compiler_params=pltpu.CompilerParams(dimension_semantics=("arbitrary",)),
    )(*ins)


def _pair_exchange(grads):
    n = len(grads)

    def build(srcs, outs, send_sems, recv_sems):
        x, y, c = lax.axis_index("x"), lax.axis_index("y"), lax.axis_index("c")
        copies = []
        for t in range(n):
            rows = grads[t].shape[1] // 2
            copies.append(pltpu.make_async_remote_copy(
                src_ref=srcs[t].at[:, pl.ds((1 - c) * rows, rows)], dst_ref=outs[t],
                send_sem=send_sems.at[t], recv_sem=recv_sems.at[t], device_id=(x, y, 1 - c), device_id_type=MESH))
        return copies

    return dict(ins=list(grads), build=build,
                outs=[jax.ShapeDtypeStruct((g.shape[0], g.shape[1] // 2, g.shape[2]), g.dtype) for g in grads],
                sems=[pltpu.SemaphoreType.DMA((n,)), pltpu.SemaphoreType.DMA((n,))])


def _pair_add(core, own, got):
    nchip, rows2, width = own.shape
    rows = rows2 // 2
    tr = min(512, rows)
    nb = rows // tr

    def body(core_ref, own_ref, got_ref, out_ref):
        out_ref[...] = (own_ref[...] + got_ref[...]).astype(BF16)

    grid_spec = pltpu.PrefetchScalarGridSpec(
        num_scalar_prefetch=1, grid=(nchip, nb),
        in_specs=[pl.BlockSpec((None, tr, width), lambda k, i, core_ref: (k, core_ref[0] * nb + i, 0)),
                  pl.BlockSpec((None, tr, width), lambda k, i, core_ref: (k, i, 0))],
        out_specs=pl.BlockSpec((None, tr, width), lambda k, i, core_ref: (k, i, 0)))
    return pl.pallas_call(
        body, name=f"pair_add_{width}", grid_spec=grid_spec,
        out_shape=jax.ShapeDtypeStruct((nchip, rows, width), BF16),
    )(core, own, got)


def _chip_exchange(parts):
    n = len(parts)

    def build(srcs, outs, send_sems, recv_sems, local_sems):
        x, y, c = lax.axis_index("x"), lax.axis_index("y"), lax.axis_index("c")
        my_chip = 2 * x + y
        chips = [(1 - x, y), (x, 1 - y), (1 - x, 1 - y)]
        copies = [pltpu.make_async_copy(srcs[t].at[my_chip], outs[t].at[my_chip], local_sems.at[t]) for t in range(n)]
        for j, (cx, cy) in enumerate(chips):
            for t in range(n):
                k = n * j + t
                copies.append(pltpu.make_async_remote_copy(
                    src_ref=srcs[t].at[2 * cx + cy], dst_ref=outs[t].at[my_chip], send_sem=send_sems.at[k],
                    recv_sem=recv_sems.at[k], device_id=(cx, cy, c), device_id_type=MESH))
        return copies

    return dict(ins=list(parts), build=build, outs=[jax.ShapeDtypeStruct(p.shape, p.dtype) for p in parts],
                sems=[pltpu.SemaphoreType.DMA((3 * n,)), pltpu.SemaphoreType.DMA((3 * n,)),
                      pltpu.SemaphoreType.DMA((n,))])


def _slot_sum(slots, name, core=None):
    ns, rows, width = slots.shape
    tr = min(512, rows)

    def body(*refs):
        in_ref, out_ref = refs[-2:]
        acc = in_ref[0].astype(F32)
        for s in range(1, ns):
            acc = acc + in_ref[s].astype(F32)
        out_ref[...] = acc

    if core is None:
        return pl.pallas_call(
            body, name=name, grid=(rows // tr,),
            in_specs=[pl.BlockSpec((ns, tr, width), lambda i: (0, i, 0))],
            out_specs=pl.BlockSpec((tr, width), lambda i: (i, 0)),
            out_shape=jax.ShapeDtypeStruct((rows, width), F32),
        )(slots)
    grid_spec = pltpu.PrefetchScalarGridSpec(
        num_scalar_prefetch=1, grid=(rows // tr,),
        in_specs=[pl.BlockSpec((ns, tr, width), lambda i, core_ref: (0, i, 0))],
        out_specs=pl.BlockSpec((None, tr, width), lambda i, core_ref: (core_ref[0], i, 0)))
    return pl.pallas_call(
        body, name=name, grid_spec=grid_spec, out_shape=jax.ShapeDtypeStruct((2, rows, width), F32),
    )(core, slots)


def _pair_gather(bufs, small):
    n = len(bufs)

    def body(*refs):
        small_ref, outs, small_out = refs[n], refs[n + 1:2 * n + 1], refs[2 * n + 1]
        send_sems, recv_sems, local_sem = refs[2 * n + 2:]
        x, y, c = lax.axis_index("x"), lax.axis_index("y"), lax.axis_index("c")
        me = 4 * x + 2 * y + c
        chips = [(1 - x, y), (x, 1 - y), (1 - x, 1 - y)]
        mine = pltpu.make_async_copy(small_ref, small_out.at[me], local_sem)
        mine.start()
        copies = [pltpu.make_async_remote_copy(
            src_ref=outs[t].at[c], dst_ref=outs[t].at[c], send_sem=send_sems.at[t], recv_sem=recv_sems.at[t],
            device_id=(x, y, 1 - c), device_id_type=MESH) for t in range(n)]
        peers = [(x, y, 1 - c)] + [(cx, cy, cc) for (cx, cy) in chips for cc in (c, 1 - c)]
        for j, peer in enumerate(peers):
            copies.append(pltpu.make_async_remote_copy(
                src_ref=small_ref, dst_ref=small_out.at[me], send_sem=send_sems.at[n + j],
                recv_sem=recv_sems.at[n + j], device_id=peer, device_id_type=MESH))
        for cp in copies:
            cp.start()
        for cp in copies:
            cp.wait()
        mine.wait()

    any_spec = pl.BlockSpec(memory_space=pl.ANY)
    res = pl.pallas_call(
        body, name="pair_gather",
        out_shape=[jax.ShapeDtypeStruct(b.shape, b.dtype) for b in bufs]
        + [jax.ShapeDtypeStruct((8,) + small.shape, small.dtype)],
        in_specs=[any_spec] * (n + 1), out_specs=[any_spec] * (n + 1),
        input_output_aliases={t: t for t in range(n)},
        scratch_shapes=[pltpu.SemaphoreType.DMA((n + 7,)), pltpu.SemaphoreType.DMA((n + 7,)),
                        pltpu.SemaphoreType.DMA],
    )(*bufs, small)
    return [r.reshape(2 * b.shape[1], b.shape[2]) for r, b in zip(res[:n], bufs)], res[n]


def _adamw(w, g, m, v, name):
    rows, width = w.shape
    tr = min(256, rows)
    c1 = 1.0 / (1.0 - ADAM_B1 ** ADAM_STEP)
    c2 = 1.0 / (1.0 - ADAM_B2 ** ADAM_STEP)

    def body(w_ref, g_ref, m_ref, v_ref, d_ref, nm_ref, nv_ref):
        gv = g_ref[...]
        nm = ADAM_B1 * m_ref[...] + (1.0 - ADAM_B1) * gv
        nv = ADAM_B2 * v_ref[...] + (1.0 - ADAM_B2) * (gv * gv)
        nm_ref[...] = nm
        nv_ref[...] = nv
        d_ref[...] = -ADAM_LR * ((nm * c1) / (jnp.sqrt(nv * c2) + ADAM_EPS) + ADAM_WD * w_ref[...])

    spec = pl.BlockSpec((tr, width), lambda i: (i, 0))
    return pl.pallas_call(
        body, name=name, grid=(rows // tr,), in_specs=[spec] * 4, out_specs=[spec] * 3,
        out_shape=[jax.ShapeDtypeStruct(w.shape, F32)] * 3,
    )(w, g, m, v)


def _local_step(x, mem, target, pre_norm, sink_a, mem_norm, post_norm, w_in_g, w_out, w_mkv):
    mk, mv = _mem_kv(mem, mem_norm, w_mkv)
    pr = _pre_proj(x, pre_norm, w_in_g)
    sink = sink_a.reshape(-1)
    qa, ka, va = pr["qa"][None], pr["ka"][None], pr["va"][None]
    oa, lse_a = _band_fwd(qa, ka, va, sink, max_dist=A_WINDOW - 1, name="swa_fwd")
    ob_list, lseb_list = [], []
    for k, (win, dil) in enumerate(B_CONFIGS):
        o_i, l_i = _band_fwd(pr["qb"][k], pr["kb"][k], pr["vb"][k], None, max_dist=win // dil, name=f"dil{dil}_fwd")
        ob_list.append(o_i)
        lseb_list.append(l_i)
    oc, lse_c = _mem_attn_fwd(pr["qc"], mk, mv)
    sink_row = jnp.pad(sink, (0, LANES - sink.shape[0])).reshape(1, LANES)
    po = _post(x, target, post_norm, w_out, sink_row, oa[0], lse_a[0], pr["ga"], ob_list, lseb_list, pr["gb"], oc,
               pr["gc"])
    dqc, dmk, dmv = _mem_attn_bwd(pr["qc"], mk, mv, po["doc"], lse_c, po["dl_c"])
    dqa, dka, dva = _band_bwd(qa, ka, va, po["doa"][None], lse_a, po["dl_a"][None], max_dist=A_WINDOW - 1,
                              name="swa_bwd")
    res = dict(qb=[], kb=[], vb=[])
    for k, (win, dil) in enumerate(B_CONFIGS):
        dq_i, dk_i, dv_i = _band_bwd(pr["qb"][k], pr["kb"][k], pr["vb"][k], po["dob"][k], po["lse_b"][k],
                                     po["dl_b"][k], max_dist=win // dil, name=f"dil{dil}_bwd")
        res["qb"].append(dq_i)
        res["kb"].append(dk_i)
        res["vb"].append(dv_i)
    nat = dict(qa=dqa[0], ka=dka[0], va=dva[0], ga=po["dga"], gb=po["dgb"], qc=dqc, gc=po["dgc"])
    dproj, gw_in = _grad_w_in(pr["ut"], nat, res)
    gw_mkv, gmem = _mem_kv_bwd(mem, mem_norm, w_mkv, dmk, dmv)
    gsink = -po["gsink"][0, :sink.shape[0]]
    return dict(loss=po["loss"], g=po["g"], dproj=dproj, gw_in=gw_in, gw_out=po["gw_out"], gw_mkv=gw_mkv,
                gpost=po["gpost"], gmem=gmem, gsink=gsink)


def kernel(x, mem, pre_norm, w_in, sink_a, mem_norm, w_mem_kv, w_out, post_norm, loss_target, m_pre_norm, m_w_in, m_sink_a, m_mem_norm, m_w_mem_kv, m_w_out, m_post_norm, v_pre_norm, v_w_in, v_sink_a, v_mem_norm, v_w_mem_kv, v_w_out, v_post_norm):
    w_in_g, w_out_g, w_mkv_g = _gather_weights(w_in[0].astype(BF16), w_out[0].astype(BF16), w_mem_kv[0].astype(BF16))
    loc = _local_step(x[0], mem[0], loss_target[0], pre_norm, sink_a, mem_norm, post_norm,
                      w_in_g.reshape(N_CHIPS, D_MODEL, SHARD_IN), w_out_g.reshape(D_MODEL, D_MODEL),
                      w_mkv_g.reshape(D_MODEL, 2 * C_W))
    big = [loc["gw_in"], loc["gw_out"].reshape(N_CHIPS, D_MODEL // N_CHIPS, D_MODEL),
           loc["gw_mkv"].reshape(N_CHIPS, D_MODEL // N_CHIPS, 2 * C_W)]
    core = lax.axis_index("c").astype(jnp.int32).reshape(1)
    w_in_t = jnp.swapaxes(w_in_g.reshape(N_CHIPS, D_MODEL, SHARD_IN), 1, 2)
    step_in = (x[0], loc["g"], pre_norm, w_in_t, loc["dproj"])
    gx_a, gpre_a, *got = _input_grad(*step_in, None, (0, 3), _pair_exchange(big), "input_grad_a")
    parts = [_pair_add(core, own, g) for own, g in zip(big, got)]
    gx_b, gpre_b, *slots = _input_grad(*step_in, gx_a, (3, 10), _chip_exchange(parts), "input_grad_b")
    grad_x, gpre_c = _input_grad(*step_in, gx_b, (13, 3), None, "input_grad_c")
    halves = [_slot_sum(s, name=f"chip_sum_{s.shape[2]}", core=core) for s in slots]
    widen = lambda a: jnp.pad(a.reshape(1, -1), ((0, 0), (0, D_MODEL - a.size)))
    small = jnp.concatenate([gpre_a, loc["gpost"], loc["gmem"], widen(loc["gsink"]), widen(loc["loss"]), gpre_b,
                             gpre_c, jnp.zeros((1, D_MODEL), F32)], axis=0)
    (g_in, g_out, g_mkv), small_slots = _pair_gather(halves, small)
    small_sum = _slot_sum(small_slots, name="device_sum")
    g_pre, g_post, g_mem = small_sum[0:1] + small_sum[5:6] + small_sum[6:7], small_sum[1:2], small_sum[2:3]
    g_sink = small_sum[3:4, :sink_a.shape[1]]
    loss = small_sum[4, 0]

    d_in, nm_in, nv_in = _adamw(w_in[0], g_in, m_w_in[0], v_w_in[0], "adamw_in")
    d_out, nm_out, nv_out = _adamw(w_out[0], g_out, m_w_out[0], v_w_out[0], "adamw_out")
    d_mkv, nm_mkv, nv_mkv = _adamw(w_mem_kv[0], g_mkv, m_w_mem_kv[0], v_w_mem_kv[0], "adamw_mkv")
    pad6 = lambda a: jnp.pad(a, ((0, 0), (0, D_MODEL - a.shape[1])))
    stack = lambda a, b, c_, d_: jnp.concatenate([a, b, c_, pad6(d_), jnp.zeros((4, D_MODEL), F32)], axis=0)
    d_s, nm_s, nv_s = _adamw(stack(pre_norm, post_norm, mem_norm, sink_a),
                             jnp.concatenate([g_pre, small_sum[1:]], axis=0),
                             stack(m_pre_norm, m_post_norm, m_mem_norm, m_sink_a),
                             stack(v_pre_norm, v_post_norm, v_mem_norm, v_sink_a), "adamw_small")
    ns_ = sink_a.shape[1]
    unpack = lambda a: (a[0:1], a[3:4, :ns_], a[2:3], a[1:2])
    d_pre, d_sink, d_mem, d_post = unpack(d_s)
    nm_pre, nm_sink, nm_mem, nm_post = unpack(nm_s)
    nv_pre, nv_sink, nv_mem, nv_post = unpack(nv_s)
    lead = lambda a: a[None]
    return (loss, lead(grad_x),
            g_pre, lead(g_in), g_sink, g_mem, lead(g_mkv), lead(g_out), g_post,
            d_pre, lead(d_in), d_sink, d_mem, lead(d_mkv), lead(d_out), d_post,
            nm_pre, lead(nm_in), nm_sink, nm_mem, lead(nm_mkv), lead(nm_out), nm_post,
            nv_pre, lead(nv_in), nv_sink, nv_mem, lead(nv_mkv), lead(nv_out), nv_post)
```

```python
import numpy as np
import jax
import jax.numpy as jnp
from jax import lax
from jax.experimental import pallas as pl
from jax.experimental.pallas import tpu as pltpu

F32 = jnp.float32
BF16 = jnp.bfloat16

D_MODEL = 1024
HEAD_DIM = 64
LANES = 128
BLOCK = 128
A_W, A_KV_W, B_W, C_W = 384, 128, 384, 256
N_MEM = 256
D_IN = 3072
N_CHIPS = 4
SHARD_IN = D_IN // N_CHIPS
B_CONFIGS = ((128, 1), (512, 4), (2048, 16))
B_DILS = tuple(d for _, d in B_CONFIGS)
A_WINDOW = 128
RMS_EPS = 1e-6
ROPE_THETA = 500000.0
SCALE = HEAD_DIM ** -0.5
NEG = -1e30
ADAM_LR, ADAM_B1, ADAM_B2, ADAM_EPS, ADAM_WD, ADAM_STEP = 0.001, 0.9, 0.999, 1e-08, 0.01, 10

NT = (((1,), (1,)), ((), ()))
TN = (((0,), (0,)), ((), ()))
MESH = pl.DeviceIdType.MESH

_PROJ_LAYOUT = (
    [("qa", 128 * i, True, True) for i in range(3)] + [("ka", 0, True, False), ("va", 0, False, False)]
    + [("ga", 128 * i, False, False) for i in range(3)]
    + [("qb", 128 * i, True, True) for i in range(3)] + [("kb", 128 * i, True, False) for i in range(3)]
    + [("vb", 128 * i, False, False) for i in range(3)] + [("gb", 128 * i, False, False) for i in range(3)]
    + [("qc", 128 * i, False, True) for i in range(2)] + [("gc", 128 * i, False, False) for i in range(2)]
)
_PROJ_WIDTH = dict(qa=A_W, ka=A_KV_W, va=A_KV_W, ga=A_W, qb=B_W, kb=B_W, vb=B_W, gb=B_W, qc=C_W, gc=C_W)
_NATURAL = ("qa", "ka", "va", "ga", "gb", "qc", "gc")
_DILATED = ("qb", "kb", "vb")


def _dot(a, b):
    return jnp.dot(a, b, preferred_element_type=F32)


def _dot_nt(a, b):
    return lax.dot_general(a, b, NT, preferred_element_type=F32)


def _dot_tn(a, b):
    return lax.dot_general(a, b, TN, preferred_element_type=F32)


def _half_masks(rows):
    lane = lax.broadcasted_iota(jnp.int32, (rows, LANES), 1)
    return lane < HEAD_DIM, lane >= HEAD_DIM


def _rope(t, c, sm, sp):
    return t * c + pltpu.roll(t, LANES - 8, 1) * sm + pltpu.roll(t, 8, 1) * sp


def _rope_tables(seq, tm):
    dim = jnp.arange(LANES) % HEAD_DIM
    inv_freq = ROPE_THETA ** (-jnp.arange(0, 16, 2, dtype=F32) / 16)
    freq = jnp.where(dim < 16, inv_freq[dim % 8], 0.0)[None, :]
    local = jnp.arange(tm, dtype=F32)[:, None] * freq
    base = (jnp.arange(seq // tm, dtype=F32) * tm)[:, None] * freq
    both = lambda a: jnp.concatenate([jnp.cos(a), jnp.sin(a)], axis=1)
    return both(local), jnp.repeat(both(base), 8, axis=0)


def _rope_coeffs(local_ref, base_ref):
    cl, sl = local_ref[:, :LANES], local_ref[:, LANES:]
    cb, sb = base_ref[0:1, :LANES], base_ref[0:1, LANES:]
    cos = cb * cl - sb * sl
    sin = sb * cl + cb * sl
    dim = lax.broadcasted_iota(jnp.int32, (1, LANES), 1) % HEAD_DIM
    return cos, jnp.where(dim < 8, -sin, 0.0), jnp.where((dim >= 8) & (dim < 16), sin, 0.0)


def _split3(x):
    a = x.astype(BF16)
    r = x - a.astype(F32)
    b = r.astype(BF16)
    c = (r - b.astype(F32)).astype(BF16)
    return a, b, c


def _rows_to_lanes(x):
    row = lax.broadcasted_iota(jnp.int32, (8, LANES), 0)
    lane = lax.broadcasted_iota(jnp.int32, (8, LANES), 1)
    eye = (row == lane).astype(BF16)
    a, b, c = _split3(x)
    return _dot_nt(eye, a) + _dot_nt(eye, b) + _dot_nt(eye, c)


def _head_sum_matrix(width):
    k = lax.broadcasted_iota(jnp.int32, (width, LANES), 0)
    h = lax.broadcasted_iota(jnp.int32, (width, LANES), 1)
    return (k // HEAD_DIM == h).astype(BF16)


def _head_expand_matrix(width):
    h = lax.broadcasted_iota(jnp.int32, (LANES, width), 0)
    k = lax.broadcasted_iota(jnp.int32, (LANES, width), 1)
    return (k // HEAD_DIM == h).astype(BF16)


def _dot_split(x, mat, terms):
    parts = _split3(x)[:terms]
    out = _dot(parts[0], mat)
    for p in parts[1:]:
        out = out + _dot(p, mat)
    return out


def _per_head(cols, fill=0.0):
    rows = cols[0].shape[0]
    lane = lax.broadcasted_iota(jnp.int32, (rows, LANES), 1)
    out = jnp.full((rows, LANES), fill, F32)
    for h, col in enumerate(cols):
        out = jnp.where(lane == h, col, out)
    return out


def _lane_blocks(width):
    return [slice(p * LANES, (p + 1) * LANES) for p in range(width // LANES)]


def _stage(rows, width):
    return pltpu.VMEM((width // LANES, rows, LANES), F32)


def _stage_write(buf, value):
    for p, lanes in enumerate(_lane_blocks(value.shape[1])):
        buf[p] = value[:, lanes]


def _stage_read(buf):
    return jnp.concatenate([buf[p] for p in range(buf.shape[0])], axis=1) if buf.shape[0] > 1 else buf[0]


def _to_residues(buf, out_ref, dil):
    rows = buf.shape[1] // dil
    for r in range(dil):
        for p in range(buf.shape[0]):
            plane = buf.at[p]
            out_ref[r, :, p * LANES:(p + 1) * LANES] = plane[pl.ds(r, rows, stride=dil), :].astype(out_ref.dtype)


def _from_residues(in_ref, buf, dil):
    rows = buf.shape[1] // dil
    for r in range(dil):
        for p in range(buf.shape[0]):
            plane = buf.at[p]
            plane[pl.ds(r, rows, stride=dil), :] = in_ref[r, :, p * LANES:(p + 1) * LANES].astype(F32)


def _residue_spec(dil, tm, width):
    return pl.BlockSpec((dil, tm // dil, width), lambda i: (0, i, 0))


def _gather_weights(w_in_s, w_out_s, w_mkv_s):
    shards = tuple(s.reshape(2, s.shape[0] // 2, s.shape[1]) for s in (w_in_s, w_out_s, w_mkv_s))
    n = len(shards)

    def body(*refs):
        srcs, outs = refs[:n], refs[2 * n:3 * n]
        send_sems, recv_sems = refs[3 * n:]
        x, y, c = lax.axis_index("x"), lax.axis_index("y"), lax.axis_index("c")
        my_chip = 2 * x + y
        sibling = (x, y, 1 - c)
        chips = [(1 - x, y), (x, 1 - y), (1 - x, 1 - y)]

        def half(t, chip, which):
            return outs[t].at[chip, which]

        def src_half(t, which):
            return srcs[t].at[which]

        def copy(k, src, dst, to):
            return pltpu.make_async_remote_copy(src_ref=src, dst_ref=dst, send_sem=send_sems.at[k],
                                                recv_sem=recv_sems.at[k], device_id=to, device_id_type=MESH)

        first = []
        for j, (cx, cy) in enumerate(chips):
            for t in range(n):
                first.append(copy(n * j + t, src_half(t, c), half(t, my_chip, c), (cx, cy, c)))
        for cp in first:
            cp.start()
        passed = []
        for j, (cx, cy) in enumerate(chips):
            chip = 2 * cx + cy
            for t in range(n):
                k = n * j + t
                copy(k, src_half(t, c), half(t, chip, c), (cx, cy, c)).wait_recv()
                fwd = copy(n * 3 + k, half(t, chip, c), half(t, chip, c), sibling)
                fwd.start()
                passed.append(fwd)
        for j, (cx, cy) in enumerate(chips):
            chip = 2 * cx + cy
            for t in range(n):
                k = n * 3 + n * j + t
                copy(k, half(t, chip, 1 - c), half(t, chip, 1 - c), sibling).wait_recv()
        for cp in first + passed:
            cp.wait_send()

    my_chip = 2 * lax.axis_index("x") + lax.axis_index("y")
    landing = [lax.dynamic_update_slice(jnp.zeros((N_CHIPS,) + s.shape, s.dtype), s[None], (my_chip, 0, 0, 0))
               for s in shards]
    any_spec = pl.BlockSpec(memory_space=pl.ANY)
    return pl.pallas_call(
        body, name="gather_weights",
        out_shape=[jax.ShapeDtypeStruct((N_CHIPS,) + s.shape, s.dtype) for s in shards],
        in_specs=[any_spec] * (2 * n), out_specs=[any_spec] * n,
        input_output_aliases={n + t: t for t in range(n)},
        scratch_shapes=[pltpu.SemaphoreType.DMA((6 * n,)), pltpu.SemaphoreType.DMA((6 * n,))],
    )(*shards, *landing)


def _mem_kv(mem, mem_norm, w_mkv):
    def body(mem_ref, g_ref, w_ref, mk_ref, mv_ref):
        m = mem_ref[...]
        r = lax.rsqrt(jnp.mean(m * m, axis=-1, keepdims=True) + RMS_EPS)
        mn = (m * r * g_ref[...]).astype(BF16)
        kv = _dot(mn, w_ref[...])
        mk_ref[...] = kv[:, :C_W].astype(BF16)
        mv_ref[...] = kv[:, C_W:].astype(BF16)

    return pl.pallas_call(
        body, name="mem_kv",
        out_shape=[jax.ShapeDtypeStruct((N_MEM, C_W), BF16)] * 2,
    )(mem, mem_norm, w_mkv)


def _mem_kv_bwd(mem, mem_norm, w_mkv, dmk, dmv):
    def body(mem_ref, g_ref, w_ref, dmk_ref, dmv_ref, gw_ref, gn_ref):
        m = mem_ref[...]
        r = lax.rsqrt(jnp.mean(m * m, axis=-1, keepdims=True) + RMS_EPS)
        mhat = m * r
        mn = (mhat * g_ref[...]).astype(BF16)
        dkv = jnp.concatenate([dmk_ref[...], dmv_ref[...]], axis=1).astype(BF16)
        gw_ref[...] = _dot_tn(mn, dkv)
        dmn = _dot_nt(dkv, w_ref[...])
        gn_ref[...] = jnp.sum(dmn * mhat, axis=0, keepdims=True)

    return pl.pallas_call(
        body, name="mem_kv_bwd",
        out_shape=[jax.ShapeDtypeStruct((D_MODEL, 2 * C_W), F32), jax.ShapeDtypeStruct((1, D_MODEL), F32)],
    )(mem, mem_norm, w_mkv, dmk, dmv)


def _pre_proj(x, pre_norm, w_in_g):
    seq = x.shape[0]
    tm = min(1024, seq)
    n_nat, n_dil = len(_NATURAL), len(_DILATED) * len(B_DILS)
    rope = _rope_tables(seq, tm)

    def body(x_ref, g_ref, w_ref, rl_ref, rb_ref, *refs):
        nat = dict(zip(_NATURAL, refs[:n_nat]))
        res = {n: refs[n_nat + len(B_DILS) * k:n_nat + len(B_DILS) * (k + 1)] for k, n in enumerate(_DILATED)}
        ut = refs[n_nat + n_dil]
        bufs = dict(zip(_DILATED, refs[n_nat + n_dil + 1:]))
        xv = x_ref[...]
        r = lax.rsqrt(jnp.mean(xv * xv, axis=-1, keepdims=True) + RMS_EPS)
        u = xv * r * g_ref[...]
        ub = u.astype(BF16)
        ut[...] = u.T.astype(BF16)
        c, sm, sp = _rope_coeffs(rl_ref, rb_ref)
        for j in range(N_CHIPS):
            pj = _dot(ub, w_ref[j])
            for b in range(SHARD_IN // LANES):
                name, off, roped, scaled = _PROJ_LAYOUT[(SHARD_IN // LANES) * j + b]
                piece = pj[:, LANES * b:LANES * (b + 1)]
                if roped:
                    piece = _rope(piece, c, sm, sp)
                if scaled:
                    piece = piece * SCALE
                if name in bufs:
                    bufs[name][off // LANES] = piece
                else:
                    nat[name][:, off:off + LANES] = piece.astype(BF16)
        for name in _DILATED:
            for ref, dil in zip(res[name], B_DILS):
                _to_residues(bufs[name], ref, dil)

    row = lambda w: pl.BlockSpec((tm, w), lambda i: (i, 0))
    full = lambda a: pl.BlockSpec(a.shape, lambda i: (0,) * a.ndim)
    out_shape = [jax.ShapeDtypeStruct((seq, _PROJ_WIDTH[n]), BF16) for n in _NATURAL]
    out_specs = [row(_PROJ_WIDTH[n]) for n in _NATURAL]
    for n in _DILATED:
        for dil in B_DILS:
            out_shape.append(jax.ShapeDtypeStruct((dil, seq // dil, B_W), BF16))
            out_specs.append(_residue_spec(dil, tm, B_W))
    out_shape.append(jax.ShapeDtypeStruct((D_MODEL, seq), BF16))
    out_specs.append(pl.BlockSpec((D_MODEL, tm), lambda i: (0, i)))
    res = pl.pallas_call(
        body, name="pre_proj", grid=(seq // tm,),
        in_specs=[row(D_MODEL), full(pre_norm),
                  pl.BlockSpec(w_in_g.shape, lambda i: (0, 0, 0), pipeline_mode=pl.Buffered(1)),
                  full(rope[0]), pl.BlockSpec((8, 2 * LANES), lambda i: (i, 0))],
        out_specs=out_specs, out_shape=out_shape,
        scratch_shapes=[_stage(tm, B_W)] * len(_DILATED),
    )(x, pre_norm, w_in_g, *rope)
    out = dict(zip(_NATURAL, res[:n_nat]))
    for k, n in enumerate(_DILATED):
        out[n] = res[n_nat + len(B_DILS) * k:n_nat + len(B_DILS) * (k + 1)]
    out["ut"] = res[n_nat + n_dil]
    return out


def _band_bias(max_dist, transposed):
    i = np.arange(BLOCK)[:, None]
    j = np.arange(BLOCK)[None, :]
    if transposed:
        same = i <= j
        other = (j + BLOCK - i) <= max_dist
        vis = np.concatenate([same, other], axis=1)
    else:
        prev = (i + BLOCK - j) <= max_dist
        same = j <= i
        vis = np.concatenate([prev, same], axis=1)
    return jnp.asarray(np.where(vis, 0.0, NEG).astype(np.float32))


def _kv_place(h, gqa):
    return (0, h // 3) if gqa else (h // 2, h % 2)


def _band_fwd(q, k, v, sink, *, max_dist, name):
    dil, length, wq = q.shape
    wk = k.shape[2]
    gqa = wk != wq
    tq = min(512, length)
    ns, nt = tq // BLOCK, length // tq
    npair = wq // LANES
    bias = _band_bias(max_dist, transposed=False)
    has_sink = sink is not None

    def body(*refs):
        if has_sink:
            sink_ref, refs = refs[0], refs[1:]
        q_ref, k_ref, kp_ref, v_ref, vp_ref, bias_ref, o_ref, lse_ref, kbuf, vbuf = refs[:10]
        i = pl.program_id(1)
        kbuf[0:BLOCK] = kp_ref[...]
        kbuf[BLOCK:] = k_ref[...]
        vbuf[0:BLOCK] = vp_ref[...]
        vbuf[BLOCK:] = v_ref[...]
        if gqa:
            kroll, vroll = refs[10:12]
            kroll[...] = pltpu.roll(kbuf[...], HEAD_DIM, 1)
            vroll[...] = pltpu.roll(vbuf[...], HEAD_DIM, 1)
        half = _half_masks(BLOCK)
        col_prev = (lax.broadcasted_iota(jnp.int32, (1, 2 * BLOCK), 1) < BLOCK).astype(F32)

        def sub(a, carry):
            r0 = pl.multiple_of(a * BLOCK, BLOCK)
            pen = jnp.where((i == 0) & (a == 0), NEG, 0.0)
            b = bias_ref[...] + pen * col_prev
            scores = []
            for p in range(npair):
                qp = q_ref[pl.ds(r0, BLOCK), p * LANES:(p + 1) * LANES]
                for e in range(2):
                    pk, ek = _kv_place(2 * p + e, gqa)
                    kw = (kbuf if ek == e else kroll)[pl.ds(r0, 2 * BLOCK), pk * LANES:(pk + 1) * LANES]
                    scores.append(_dot_nt(jnp.where(half[e], qp, jnp.zeros_like(qp)), kw))
            m_cols, l_cols, probs = [], [], []
            for h, s in enumerate(scores):
                s = s + b
                m = jnp.max(s, axis=1, keepdims=True)
                if has_sink:
                    m = jnp.maximum(m, sink_ref[h])
                pe = jnp.exp(s - m)
                l = jnp.sum(pe, axis=1, keepdims=True)
                if has_sink:
                    l = l + jnp.exp(sink_ref[h] - m)
                probs.append(pe.astype(BF16))
                m_cols.append(m)
                l_cols.append(l)
            for p in range(npair):
                o_h = []
                for e in range(2):
                    h = 2 * p + e
                    pk, ek = _kv_place(h, gqa)
                    vw = (vbuf if ek == e else vroll)[pl.ds(r0, 2 * BLOCK), pk * LANES:(pk + 1) * LANES]
                    o_h.append(_dot(probs[h], vw) * (1.0 / l_cols[h]))
                o_ref[pl.ds(r0, BLOCK), p * LANES:(p + 1) * LANES] = jnp.where(half[0], o_h[0], o_h[1]).astype(BF16)
            lse_ref[pl.ds(r0, BLOCK), :] = _per_head(m_cols) + jnp.log(_per_head(l_cols, 1.0))
            return carry

        lax.fori_loop(0, ns, sub, 0, unroll=True)

    main = lambda w: pl.BlockSpec((None, tq, w), lambda r, i: (r, i, 0))
    prev = lambda w: pl.BlockSpec((None, BLOCK, w), lambda r, i: (r, jnp.maximum(i * ns - 1, 0), 0))
    in_specs = [main(wq), main(wk), prev(wk), main(wk), prev(wk), pl.BlockSpec(bias.shape, lambda r, i: (0, 0))]
    args = [q, k, k, v, v, bias]
    if has_sink:
        in_specs = [pl.BlockSpec(memory_space=pltpu.SMEM)] + in_specs
        args = [sink] + args
    scratch = [pltpu.VMEM((tq + BLOCK, wk), BF16)] * (4 if gqa else 2)
    return pl.pallas_call(
        body, name=name, grid=(dil, nt), in_specs=in_specs,
        out_specs=[main(wq), main(LANES)],
        out_shape=[jax.ShapeDtypeStruct((dil, length, wq), BF16), jax.ShapeDtypeStruct((dil, length, LANES), F32)],
        scratch_shapes=scratch,
    )(*args)


def _band_bwd(q, k, v, do, lse, delta, *, max_dist, name):
    dil, length, wq = q.shape
    wk = k.shape[2]
    gqa = wk != wq
    tq = min(512, length)
    ns, nt = tq // BLOCK, length // tq
    npair = wq // LANES
    nblocks = length // BLOCK
    bias = _band_bias(max_dist, transposed=True)

    def body(q_ref, qn_ref, do_ref, don_ref, lse_ref, lsen_ref, dl_ref, dln_ref, k_ref, v_ref, bias_ref,
             dq_ref, dk_ref, dv_ref, qbuf, dobuf, stat_l, stat_d, dqt, kt, *rolled):
        i = pl.program_id(1)
        qbuf[0:tq] = q_ref[...]
        qbuf[tq:] = qn_ref[...]
        dobuf[0:tq] = do_ref[...]
        dobuf[tq:] = don_ref[...]
        for pk in range(wk // LANES):
            kt[pk] = k_ref[:, pk * LANES:(pk + 1) * LANES].astype(F32).T.astype(BF16)
        if gqa:
            kroll, vroll, ktroll = rolled
            kroll[...] = pltpu.roll(k_ref[...], HEAD_DIM, 1)
            vroll[...] = pltpu.roll(v_ref[...], HEAD_DIM, 1)
            ktroll[0] = kroll[...].astype(F32).T.astype(BF16)
        for a in range(ns):
            rows = slice(a * BLOCK, (a + 1) * BLOCK)
            stat_l[a] = _rows_to_lanes(lse_ref[rows, :])
            stat_d[a] = _rows_to_lanes(dl_ref[rows, :])
        stat_l[ns] = _rows_to_lanes(lsen_ref[...])
        stat_d[ns] = _rows_to_lanes(dln_ref[...])

        @pl.when(i == 0)
        def _():
            dqt[:, :, 0:BLOCK] = jnp.zeros((npair, LANES, BLOCK), F32)

        @pl.when(i > 0)
        def _():
            dqt[:, :, 0:BLOCK] = dqt[:, :, tq:tq + BLOCK]

        dqt[:, :, BLOCK:] = jnp.zeros((npair, LANES, tq), F32)
        half2 = _half_masks(2 * BLOCK)
        row = lax.broadcasted_iota(jnp.int32, (LANES, BLOCK), 0)
        row_half = (row < HEAD_DIM, row >= HEAD_DIM)
        col_next = (lax.broadcasted_iota(jnp.int32, (1, 2 * BLOCK), 1) >= BLOCK).astype(F32)

        for b in range(ns):
            rows = slice(b * BLOCK, (b + 1) * BLOCK)
            window = slice(b * BLOCK, (b + 2) * BLOCK)
            bt = bias_ref[...]
            if b == ns - 1:
                bt = bt + jnp.where(i == nt - 1, NEG, 0.0) * col_next
            acc = {}
            items = []
            for p in range(npair):
                lanes = slice(p * LANES, (p + 1) * LANES)
                qw = qbuf[window, lanes]
                dow = dobuf[window, lanes]
                for e in range(2):
                    h = 2 * p + e
                    pk, ek = _kv_place(h, gqa)
                    klanes = slice(pk * LANES, (pk + 1) * LANES)
                    kb = (k_ref if ek == e else kroll)[rows, klanes]
                    vb = (v_ref if ek == e else vroll)[rows, klanes]
                    qm = jnp.where(half2[e], qw, jnp.zeros_like(qw))
                    dom = jnp.where(half2[e], dow, jnp.zeros_like(dow))
                    items.append(dict(p=p, e=e, h=h, pk=pk, ek=ek, qm=qm, dom=dom,
                                      st=_dot_nt(kb, qm), dpt=_dot_nt(vb, dom)))
            for it in items:
                h = it["h"]
                lrow = jnp.concatenate([stat_l[b, h:h + 1, :], stat_l[b + 1, h:h + 1, :]], axis=1)
                drow = jnp.concatenate([stat_d[b, h:h + 1, :], stat_d[b + 1, h:h + 1, :]], axis=1)
                pt = jnp.exp(it["st"] + bt - lrow)
                it["ptb"] = pt.astype(BF16)
                it["dsb"] = (pt * (it["dpt"] - drow)).astype(BF16)
            for it in items:
                p, e, pk, ek = it["p"], it["e"], it["pk"], it["ek"]
                dv_c = _dot(it["ptb"], it["dom"])
                dk_c = _dot(it["dsb"], it["qm"])
                kbt = (kt if ek == e else ktroll)[pk, :, rows]
                kbtm = jnp.where(row_half[e], kbt, jnp.zeros_like(kbt))
                dqt[p, :, window] += _dot(kbtm, it["dsb"])
                key = (pk, ek == e)
                if key in acc:
                    acc[key] = (acc[key][0] + dk_c, acc[key][1] + dv_c)
                else:
                    acc[key] = (dk_c, dv_c)
            if not gqa:
                for p in range(npair):
                    lanes = slice(p * LANES, (p + 1) * LANES)
                    dk_ref[rows, lanes] = acc[(p, True)][0].astype(BF16)
                    dv_ref[rows, lanes] = acc[(p, True)][1].astype(BF16)
            if gqa:
                dk_al, dv_al = acc[(0, True)]
                dk_mis, dv_mis = acc[(0, False)]
                dk_ref[rows, :] = (dk_al + pltpu.roll(dk_mis, HEAD_DIM, 1)).astype(BF16)
                dv_ref[rows, :] = (dv_al + pltpu.roll(dv_mis, HEAD_DIM, 1)).astype(BF16)

        for p in range(npair):
            dq_ref[:, p * LANES:(p + 1) * LANES] = dqt[p, :, 0:tq].T.astype(BF16)

    main = lambda w: pl.BlockSpec((None, tq, w), lambda r, i: (r, i, 0))
    nxt = lambda w: pl.BlockSpec((None, BLOCK, w), lambda r, i: (r, jnp.minimum((i + 1) * ns, nblocks - 1), 0))
    scratch = [pltpu.VMEM((tq + BLOCK, wq), BF16), pltpu.VMEM((tq + BLOCK, wq), BF16),
               pltpu.VMEM((ns + 1, 8, LANES), F32), pltpu.VMEM((ns + 1, 8, LANES), F32),
               pltpu.VMEM((npair, LANES, tq + BLOCK), F32), pltpu.VMEM((wk // LANES, LANES, tq), BF16)]
    if gqa:
        scratch = scratch + [pltpu.VMEM((tq, wk), BF16)] * 2 + [pltpu.VMEM((1, LANES, tq), BF16)]
    return pl.pallas_call(
        body, name=name, grid=(dil, nt),
        in_specs=[main(wq), nxt(wq), main(wq), nxt(wq), main(LANES), nxt(LANES), main(LANES), nxt(LANES),
                  main(wk), main(wk), pl.BlockSpec(bias.shape, lambda r, i: (0, 0))],
        out_specs=[main(wq), main(wk), main(wk)],
        out_shape=[jax.ShapeDtypeStruct((dil, length, wq), BF16), jax.ShapeDtypeStruct((dil, length, wk), BF16),
                   jax.ShapeDtypeStruct((dil, length, wk), BF16)],
        scratch_shapes=scratch,
        compiler_params=pltpu.CompilerParams(dimension_semantics=("arbitrary", "arbitrary")),
    )(q, q, do, do, lse, lse, delta, delta, k, v, bias)


def _mem_attn_fwd(q, mk, mv):
    seq = q.shape[0]
    tq = min(512, seq)
    ns = tq // BLOCK

    def body(q_ref, mk_ref, mv_ref, o_ref, lse_ref):
        half = _half_masks(BLOCK)

        def sub(a, carry):
            r0 = pl.multiple_of(a * BLOCK, BLOCK)
            scores = []
            for p in range(C_W // LANES):
                lanes = slice(p * LANES, (p + 1) * LANES)
                qp = q_ref[pl.ds(r0, BLOCK), lanes]
                for e in range(2):
                    scores.append(_dot_nt(jnp.where(half[e], qp, jnp.zeros_like(qp)), mk_ref[:, lanes]))
            m_cols, l_cols, probs = [], [], []
            for s in scores:
                m = jnp.max(s, axis=1, keepdims=True)
                pe = jnp.exp(s - m)
                probs.append(pe.astype(BF16))
                m_cols.append(m)
                l_cols.append(jnp.sum(pe, axis=1, keepdims=True))
            for p in range(C_W // LANES):
                lanes = slice(p * LANES, (p + 1) * LANES)
                o_h = [_dot(probs[2 * p + e], mv_ref[:, lanes]) * (1.0 / l_cols[2 * p + e]) for e in range(2)]
                o_ref[pl.ds(r0, BLOCK), lanes] = jnp.where(half[0], o_h[0], o_h[1]).astype(BF16)
            lse_ref[pl.ds(r0, BLOCK), :] = _per_head(m_cols) + jnp.log(_per_head(l_cols, 1.0))
            return carry

        lax.fori_loop(0, ns, sub, 0, unroll=True)

    row = lambda w: pl.BlockSpec((tq, w), lambda i: (i, 0))
    full = pl.BlockSpec((N_MEM, C_W), lambda i: (0, 0))
    return pl.pallas_call(
        body, name="mem_attn_fwd", grid=(seq // tq,), in_specs=[row(C_W), full, full],
        out_specs=[row(C_W), row(LANES)],
        out_shape=[jax.ShapeDtypeStruct((seq, C_W), BF16), jax.ShapeDtypeStruct((seq, LANES), F32)],
    )(q, mk, mv)


def _mem_attn_bwd(q, mk, mv, do, lse, delta):
    seq = q.shape[0]
    tq = min(512, seq)
    ns = tq // BLOCK
    npair = C_W // LANES

    def body(q_ref, mk_ref, mv_ref, do_ref, lse_ref, dl_ref, dq_ref, dmk_ref, dmv_ref, stat_l, stat_d, mkt, dqt):
        @pl.when(pl.program_id(0) == 0)
        def _():
            dmk_ref[...] = jnp.zeros_like(dmk_ref)
            dmv_ref[...] = jnp.zeros_like(dmv_ref)
            for p in range(npair):
                mkt[p] = mk_ref[:, p * LANES:(p + 1) * LANES].astype(F32).T.astype(BF16)

        for a in range(ns):
            rows = slice(a * BLOCK, (a + 1) * BLOCK)
            stat_l[a] = _rows_to_lanes(lse_ref[rows, :])
            stat_d[a] = _rows_to_lanes(dl_ref[rows, :])
        half = _half_masks(BLOCK)
        row = lax.broadcasted_iota(jnp.int32, (LANES, N_MEM), 0)
        row_half = (row < HEAD_DIM, row >= HEAD_DIM)

        for a in range(ns):
            rows = slice(a * BLOCK, (a + 1) * BLOCK)
            items = []
            for p in range(npair):
                lanes = slice(p * LANES, (p + 1) * LANES)
                qp = q_ref[rows, lanes]
                dop = do_ref[rows, lanes]
                for e in range(2):
                    qm = jnp.where(half[e], qp, jnp.zeros_like(qp))
                    dom = jnp.where(half[e], dop, jnp.zeros_like(dop))
                    items.append(dict(p=p, e=e, qm=qm, dom=dom, st=_dot_nt(mk_ref[:, lanes], qm),
                                      dpt=_dot_nt(mv_ref[:, lanes], dom)))
            for it in items:
                h = 2 * it["p"] + it["e"]
                pt = jnp.exp(it["st"] - stat_l[a, h:h + 1, :])
                it["ptb"] = pt.astype(BF16)
                it["dsb"] = (pt * (it["dpt"] - stat_d[a, h:h + 1, :])).astype(BF16)
            for p in range(npair):
                lanes = slice(p * LANES, (p + 1) * LANES)
                pair = [it for it in items if it["p"] == p]
                dmv_ref[:, lanes] += _dot(pair[0]["ptb"], pair[0]["dom"]) + _dot(pair[1]["ptb"], pair[1]["dom"])
                dmk_ref[:, lanes] += _dot(pair[0]["dsb"], pair[0]["qm"]) + _dot(pair[1]["dsb"], pair[1]["qm"])
                kbt = mkt[p]
                dqt[p, :, rows] = (_dot(jnp.where(row_half[0], kbt, jnp.zeros_like(kbt)), pair[0]["dsb"])
                                   + _dot(jnp.where(row_half[1], kbt, jnp.zeros_like(kbt)), pair[1]["dsb"]))
        for p in range(npair):
            dq_ref[:, p * LANES:(p + 1) * LANES] = dqt[p].T.astype(BF16)

    row = lambda w: pl.BlockSpec((tq, w), lambda i: (i, 0))
    full = pl.BlockSpec((N_MEM, C_W), lambda i: (0, 0))
    return pl.pallas_call(
        body, name="mem_attn_bwd", grid=(seq // tq,),
        in_specs=[row(C_W), full, full, row(C_W), row(LANES), row(LANES)], out_specs=[row(C_W), full, full],
        out_shape=[jax.ShapeDtypeStruct((seq, C_W), BF16), jax.ShapeDtypeStruct((N_MEM, C_W), F32),
                   jax.ShapeDtypeStruct((N_MEM, C_W), F32)],
        scratch_shapes=[pltpu.VMEM((ns, 8, LANES), F32)] * 2
        + [pltpu.VMEM((npair, LANES, N_MEM), BF16), pltpu.VMEM((npair, LANES, tq), F32)],
        compiler_params=pltpu.CompilerParams(dimension_semantics=("arbitrary",)),
    )(q, mk, mv, do, lse, delta)


def _silu_and_grad(g):
    s = 1.0 / (1.0 + jnp.exp(-g))
    return g * s, s * (1.0 + g * (1.0 - s))


def _post(x, target, post_norm, w_out, sink_row, oa, lse_a, ga, ob_list, lseb_list, gb, oc, gc):
    seq = x.shape[0]
    tm = min(512, seq)
    inv_d = 1.0 / D_MODEL
    nd = len(B_DILS)

    def body(*refs):
        (x_ref, t_ref, gp_ref, w_ref, sink_ref, oa_ref, lsea_ref, ga_ref), refs = refs[:8], refs[8:]
        ob_refs, lb_refs, (gb_ref, oc_ref, gc_ref), refs = refs[:nd], refs[nd:2 * nd], refs[2 * nd:2 * nd + 3], refs[2 * nd + 3:]
        (g_ref, doa_ref, dla_ref, dga_ref), refs = refs[:4], refs[4:]
        dob_refs, lsec_refs, dlb_refs, refs = refs[:nd], refs[nd:2 * nd], refs[2 * nd:3 * nd], refs[3 * nd:]
        (dgb_ref, doc_ref, dlc_ref, dgc_ref, gw_ref, gpost_ref, gsink_ref, loss_ref), refs = refs[:8], refs[8:]
        ycat, obufs, lbufs, st_do, st_l, st_d = refs[0], refs[1:nd], refs[nd:2 * nd - 1], refs[2 * nd - 1], refs[2 * nd], refs[2 * nd + 1]

        @pl.when(pl.program_id(0) == 0)
        def _():
            gw_ref[...] = jnp.zeros_like(gw_ref)
            gpost_ref[...] = jnp.zeros_like(gpost_ref)
            gsink_ref[...] = jnp.zeros_like(gsink_ref)
            loss_ref[...] = jnp.zeros_like(loss_ref)

        o_i, l_i = [ob_refs[0][0].astype(F32)], [lb_refs[0][0]]
        for k in range(1, nd):
            _from_residues(ob_refs[k], obufs[k - 1], B_DILS[k])
            _from_residues(lb_refs[k], lbufs[k - 1], B_DILS[k])
            o_i.append(_stage_read(obufs[k - 1]))
            l_i.append(_stage_read(lbufs[k - 1]))
        mx = l_i[0]
        for l in l_i[1:]:
            mx = jnp.maximum(mx, l)
        w_i = [jnp.exp(l - mx) for l in l_i]
        z = w_i[0]
        for w in w_i[1:]:
            z = z + w
        _stage_write(st_l, mx + jnp.log(z))
        expand = _head_expand_matrix(B_W)
        inv_z = 1.0 / z
        ob = None
        for w, o in zip(w_i, o_i):
            term = _dot_split(w * inv_z, expand, 2) * o
            ob = term if ob is None else ob + term
        oa, oc = oa_ref[...].astype(F32), oc_ref[...].astype(F32)
        sa, dsa = _silu_and_grad(ga_ref[...].astype(F32))
        sb, dsb = _silu_and_grad(gb_ref[...].astype(F32))
        sc, dsc = _silu_and_grad(gc_ref[...].astype(F32))
        ycat[:, 0:A_W] = (oa * sa).astype(BF16)
        ycat[:, A_W:A_W + B_W] = (ob * sb).astype(BF16)
        ycat[:, A_W + B_W:] = (oc * sc).astype(BF16)
        y2 = _dot(ycat[...], w_ref[...])
        r = lax.rsqrt(jnp.mean(y2 * y2, axis=-1, keepdims=True) + RMS_EPS)
        zhat = y2 * r
        gp = gp_ref[...]
        err = x_ref[...] + zhat * gp - t_ref[...]
        loss_ref[...] += jnp.sum(err * err) * (0.5 * inv_d)
        g = err * inv_d
        g_ref[...] = g
        gpost_ref[...] += jnp.sum(g * zhat, axis=0, keepdims=True)
        a = g * gp
        dy2 = (r * (a - zhat * jnp.mean(a * zhat, axis=-1, keepdims=True))).astype(BF16)
        gw_ref[...] += _dot_tn(ycat[...], dy2)
        dycat = _dot_nt(dy2, w_ref[...])
        dya, dyb, dyc = dycat[:, 0:A_W], dycat[:, A_W:A_W + B_W], dycat[:, A_W + B_W:]
        doa, dob, doc = dya * sa, dyb * sb, dyc * sc
        doa_ref[...] = doa.astype(BF16)
        doc_ref[...] = doc.astype(BF16)
        dga_ref[...] = (dya * oa * dsa).astype(BF16)
        dgb_ref[...] = (dyb * ob * dsb).astype(BF16)
        dgc_ref[...] = (dyc * oc * dsc).astype(BF16)
        dl_a = _dot_split(doa * oa, _head_sum_matrix(A_W), 2)
        dla_ref[...] = dl_a
        dlc_ref[...] = _dot_split(doc * oc, _head_sum_matrix(C_W), 2)
        gsink_ref[...] += jnp.sum(jnp.exp(sink_ref[...] - lsea_ref[...]) * dl_a, axis=0, keepdims=True)
        _stage_write(st_do, dob)
        _stage_write(st_d, _dot_split(dob * ob, _head_sum_matrix(B_W), 2))
        for k, dil in enumerate(B_DILS):
            _to_residues(st_do, dob_refs[k], dil)
            _to_residues(st_l, lsec_refs[k], dil)
            _to_residues(st_d, dlb_refs[k], dil)

    row = lambda w: pl.BlockSpec((tm, w), lambda i: (i, 0))
    full = lambda shape: pl.BlockSpec(shape, lambda i: (0,) * len(shape))
    res_specs = lambda w: [_residue_spec(d, tm, w) for d in B_DILS]
    res_shapes = lambda w, dt: [jax.ShapeDtypeStruct((d, seq // d, w), dt) for d in B_DILS]
    ins = [x, target, post_norm, w_out, sink_row, oa, lse_a, ga, *ob_list, *lseb_list, gb, oc, gc]
    in_specs = ([row(D_MODEL), row(D_MODEL), full((1, D_MODEL)), full((D_MODEL, D_MODEL)), full((1, LANES)),
                 row(A_W), row(LANES), row(A_W)] + res_specs(B_W) + res_specs(LANES) + [row(B_W), row(C_W), row(C_W)])
    out_shape = ([jax.ShapeDtypeStruct((seq, D_MODEL), F32), jax.ShapeDtypeStruct((seq, A_W), BF16),
                  jax.ShapeDtypeStruct((seq, LANES), F32), jax.ShapeDtypeStruct((seq, A_W), BF16)]
                 + res_shapes(B_W, BF16) + res_shapes(LANES, F32) + res_shapes(LANES, F32)
                 + [jax.ShapeDtypeStruct((seq, B_W), BF16), jax.ShapeDtypeStruct((seq, C_W), BF16),
                    jax.ShapeDtypeStruct((seq, LANES), F32), jax.ShapeDtypeStruct((seq, C_W), BF16),
                    jax.ShapeDtypeStruct((D_MODEL, D_MODEL), F32), jax.ShapeDtypeStruct((1, D_MODEL), F32),
                    jax.ShapeDtypeStruct((1, LANES), F32), jax.ShapeDtypeStruct((1, LANES), F32)])
    out_specs = ([row(D_MODEL), row(A_W), row(LANES), row(A_W)] + res_specs(B_W) + res_specs(LANES) + res_specs(LANES)
                 + [row(B_W), row(C_W), row(LANES), row(C_W),
                    full((D_MODEL, D_MODEL)), full((1, D_MODEL)), full((1, LANES)), full((1, LANES))])
    scratch = ([pltpu.VMEM((tm, D_MODEL), BF16)] + [_stage(tm, B_W)] * (nd - 1) + [_stage(tm, LANES)] * (nd - 1)
               + [_stage(tm, B_W), _stage(tm, LANES), _stage(tm, LANES)])
    res = pl.pallas_call(
        body, name="post", grid=(seq // tm,), in_specs=in_specs, out_specs=out_specs, out_shape=out_shape,
        scratch_shapes=scratch,
        compiler_params=pltpu.CompilerParams(dimension_semantics=("arbitrary",)),
    )(*ins)
    out = dict(g=res[0], doa=res[1], dl_a=res[2], dga=res[3], dob=res[4:4 + nd], lse_b=res[4 + nd:4 + 2 * nd],
               dl_b=res[4 + 2 * nd:4 + 3 * nd])
    rest = res[4 + 3 * nd:]
    out.update(dgb=rest[0], doc=rest[1], dl_c=rest[2], dgc=rest[3], gw_out=rest[4], gpost=rest[5], gsink=rest[6],
               loss=rest[7])
    return out


def _grad_w_in(ut, nat, res):
    seq = ut.shape[1]
    tm = min(512, seq)
    nd = len(B_DILS)
    nat_list = [nat[n] for n in _NATURAL]
    res_list = [a for n in _DILATED for a in res[n]]
    rope = _rope_tables(seq, tm)

    def body(rl_ref, rb_ref, ut_ref, *refs):
        nat_refs = dict(zip(_NATURAL, refs[:len(_NATURAL)]))
        refs = refs[len(_NATURAL):]
        res_refs = {n: refs[nd * k:nd * (k + 1)] for k, n in enumerate(_DILATED)}
        refs = refs[nd * len(_DILATED):]
        dproj_ref, gw_ref = refs[:2]
        bufs = {n: refs[2 + (nd - 1) * k:2 + (nd - 1) * (k + 1)] for k, n in enumerate(_DILATED)}

        @pl.when(pl.program_id(0) == 0)
        def _():
            gw_ref[...] = jnp.zeros_like(gw_ref)

        for n in _DILATED:
            for k in range(1, nd):
                _from_residues(res_refs[n][k], bufs[n][k - 1], B_DILS[k])
        c, sm, sp = _rope_coeffs(rl_ref, rb_ref)
        sm, sp = -sm, -sp
        for blk, (name, off, roped, scaled) in enumerate(_PROJ_LAYOUT):
            lanes = slice(off, off + LANES)
            if name in nat_refs:
                piece = nat_refs[name][:, lanes].astype(F32)
            else:
                piece = res_refs[name][0][0, :, lanes].astype(F32)
                for buf in bufs[name]:
                    piece = piece + buf[off // LANES]
            if roped:
                piece = _rope(piece, c, sm, sp)
            if scaled:
                piece = piece * SCALE
            dproj_ref[:, blk * LANES:(blk + 1) * LANES] = piece.astype(BF16)
        for j in range(N_CHIPS):
            gw_ref[j] += _dot(ut_ref[...], dproj_ref[:, j * SHARD_IN:(j + 1) * SHARD_IN])

    row = lambda w: pl.BlockSpec((tm, w), lambda i: (i, 0))
    in_specs = ([pl.BlockSpec(rope[0].shape, lambda i: (0, 0)), pl.BlockSpec((8, 2 * LANES), lambda i: (i, 0)),
                 pl.BlockSpec((D_MODEL, tm), lambda i: (0, i))]
                + [row(a.shape[1]) for a in nat_list]
                + [_residue_spec(d, tm, B_W) for _ in _DILATED for d in B_DILS])
    return pl.pallas_call(
        body, name="grad_w_in", grid=(seq // tm,), in_specs=in_specs,
        out_specs=[row(D_IN), pl.BlockSpec((N_CHIPS, D_MODEL, SHARD_IN), lambda i: (0, 0, 0))],
        out_shape=[jax.ShapeDtypeStruct((seq, D_IN), BF16), jax.ShapeDtypeStruct((N_CHIPS, D_MODEL, SHARD_IN), F32)],
        scratch_shapes=[_stage(tm, B_W)] * ((nd - 1) * len(_DILATED)),
        compiler_params=pltpu.CompilerParams(dimension_semantics=("arbitrary",)),
    )(*rope, ut, *nat_list, *res_list)


def _input_grad(x, g, pre_norm, w_in_g, dproj, gx_prev, span, host, name):
    seq = x.shape[0]
    tm = seq // 8
    first_block, steps = span
    n_host_in = len(host["ins"]) if host else 0
    n_host_out = len(host["outs"]) if host else 0

    def body(*refs):
        x_ref, g_ref, gp_ref, w_ref, dp_ref = refs[:5]
        refs = refs[5 + (gx_prev is not None):]
        host_in, refs = refs[:n_host_in], refs[n_host_in:]
        gx_ref, gpre_ref = refs[:2]
        host_out, sems = refs[2:2 + n_host_out], refs[2 + n_host_out:]
        step = pl.program_id(0)

        @pl.when(step == 0)
        def _():
            gpre_ref[...] = jnp.zeros_like(gpre_ref)
            if host:
                for cp in host["build"](host_in, host_out, *sems):
                    cp.start()

        du = None
        for j in range(N_CHIPS):
            term = _dot_nt(dp_ref[:, j * SHARD_IN:(j + 1) * SHARD_IN], w_ref[j])
            du = term if du is None else du + term
        xv = x_ref[...]
        r = lax.rsqrt(jnp.mean(xv * xv, axis=-1, keepdims=True) + RMS_EPS)
        xhat = xv * r
        gpre_ref[...] += jnp.sum(du * xhat, axis=0, keepdims=True)
        a = du * gp_ref[...]
        gx_ref[...] = g_ref[...] + r * (a - xhat * jnp.mean(a * xhat, axis=-1, keepdims=True))

        if host:
            @pl.when(step == steps - 1)
            def _():
                for cp in host["build"](host_in, host_out, *sems):
                    cp.wait()

    row = lambda w: pl.BlockSpec((tm, w), lambda i: (first_block + i, 0))
    full = lambda a: pl.BlockSpec(a.shape, lambda i: (0,) * a.ndim)
    any_spec = pl.BlockSpec(memory_space=pl.ANY)
    ins = [x, g, pre_norm, w_in_g, dproj]
    weights = pl.BlockSpec(w_in_g.shape, lambda i: (0, 0, 0), pipeline_mode=pl.Buffered(1))
    in_specs = [row(D_MODEL), row(D_MODEL), full(pre_norm), weights, row(D_IN)]
    aliases = {}
    if gx_prev is not None:
        aliases[len(ins)] = 0
        ins.append(gx_prev)
        in_specs.append(any_spec)
    out_shape = [jax.ShapeDtypeStruct((seq, D_MODEL), F32), jax.ShapeDtypeStruct((1, D_MODEL), F32)]
    out_specs = [row(D_MODEL), pl.BlockSpec((1, D_MODEL), lambda i: (0, 0))]
    scratch = []
    if host:
        ins += list(host["ins"])
        in_specs += [any_spec] * n_host_in
        out_shape += list(host["outs"])
        out_specs += [any_spec] * n_host_out
        scratch = list(host["sems"])
    return pl.pallas_call(
        body, name=name, grid=(steps,), in_specs=in_specs, out_specs=out_specs, out_shape=out_shape,
        input_output_aliases=aliases, scratch_shapes=scratch,
        compiler_params=pltpu.CompilerParams(dimension_semantics=("arbitrary",)),
    )(*ins)


def _pair_exchange(grads):
    n = len(grads)

    def build(srcs, outs, send_sems, recv_sems):
        x, y, c = lax.axis_index("x"), lax.axis_index("y"), lax.axis_index("c")
        copies = []
        for t in range(n):
            rows = grads[t].shape[1] // 2
            copies.append(pltpu.make_async_remote_copy(
                src_ref=srcs[t].at[:, pl.ds((1 - c) * rows, rows)], dst_ref=outs[t],
                send_sem=send_sems.at[t], recv_sem=recv_sems.at[t], device_id=(x, y, 1 - c), device_id_type=MESH))
        return copies

    return dict(ins=list(grads), build=build,
                outs=[jax.ShapeDtypeStruct((g.shape[0], g.shape[1] // 2, g.shape[2]), g.dtype) for g in grads],
                sems=[pltpu.SemaphoreType.DMA((n,)), pltpu.SemaphoreType.DMA((n,))])


def _pair_add(core, own, got):
    nchip, rows2, width = own.shape
    rows = rows2 // 2
    tr = min(512, rows)
    nb = rows // tr

    def body(core_ref, own_ref, got_ref, out_ref):
        out_ref[...] = (own_ref[...] + got_ref[...]).astype(BF16)

    grid_spec = pltpu.PrefetchScalarGridSpec(
        num_scalar_prefetch=1, grid=(nchip, nb),
        in_specs=[pl.BlockSpec((None, tr, width), lambda k, i, core_ref: (k, core_ref[0] * nb + i, 0)),
                  pl.BlockSpec((None, tr, width), lambda k, i, core_ref: (k, i, 0))],
        out_specs=pl.BlockSpec((None, tr, width), lambda k, i, core_ref: (k, i, 0)))
    return pl.pallas_call(
        body, name=f"pair_add_{width}", grid_spec=grid_spec,
        out_shape=jax.ShapeDtypeStruct((nchip, rows, width), BF16),
    )(core, own, got)


def _chip_exchange(parts):
    n = len(parts)

    def build(srcs, outs, send_sems, recv_sems, local_sems):
        x, y, c = lax.axis_index("x"), lax.axis_index("y"), lax.axis_index("c")
        my_chip = 2 * x + y
        chips = [(1 - x, y), (x, 1 - y), (1 - x, 1 - y)]
        copies = [pltpu.make_async_copy(srcs[t].at[my_chip], outs[t].at[my_chip], local_sems.at[t]) for t in range(n)]
        for j, (cx, cy) in enumerate(chips):
            for t in range(n):
                k = n * j + t
                copies.append(pltpu.make_async_remote_copy(
                    src_ref=srcs[t].at[2 * cx + cy], dst_ref=outs[t].at[my_chip], send_sem=send_sems.at[k],
                    recv_sem=recv_sems.at[k], device_id=(cx, cy, c), device_id_type=MESH))
        return copies

    return dict(ins=list(parts), build=build, outs=[jax.ShapeDtypeStruct(p.shape, p.dtype) for p in parts],
                sems=[pltpu.SemaphoreType.DMA((3 * n,)), pltpu.SemaphoreType.DMA((3 * n,)),
                      pltpu.SemaphoreType.DMA((n,))])


def _slot_sum(slots, name, core=None):
    ns, rows, width = slots.shape
    tr = min(512, rows)

    def body(*refs):
        in_ref, out_ref = refs[-2:]
        acc = in_ref[0].astype(F32)
        for s in range(1, ns):
            acc = acc + in_ref[s].astype(F32)
        out_ref[...] = acc

    if core is None:
        return pl.pallas_call(
            body, name=name, grid=(rows // tr,),
            in_specs=[pl.BlockSpec((ns, tr, width), lambda i: (0, i, 0))],
            out_specs=pl.BlockSpec((tr, width), lambda i: (i, 0)),
            out_shape=jax.ShapeDtypeStruct((rows, width), F32),
        )(slots)
    grid_spec = pltpu.PrefetchScalarGridSpec(
        num_scalar_prefetch=1, grid=(rows // tr,),
        in_specs=[pl.BlockSpec((ns, tr, width), lambda i, core_ref: (0, i, 0))],
        out_specs=pl.BlockSpec((None, tr, width), lambda i, core_ref: (core_ref[0], i, 0)))
    return pl.pallas_call(
        body, name=name, grid_spec=grid_spec, out_shape=jax.ShapeDtypeStruct((2, rows, width), F32),
    )(core, slots)


def _pair_gather(bufs, small):
    n = len(bufs)

    def body(*refs):
        small_ref, outs, small_out = refs[n], refs[n + 1:2 * n + 1], refs[2 * n + 1]
        send_sems, recv_sems, local_sem = refs[2 * n + 2:]
        x, y, c = lax.axis_index("x"), lax.axis_index("y"), lax.axis_index("c")
        me = 4 * x + 2 * y + c
        chips = [(1 - x, y), (x, 1 - y), (1 - x, 1 - y)]
        mine = pltpu.make_async_copy(small_ref, small_out.at[me], local_sem)
        mine.start()
        copies = [pltpu.make_async_remote_copy(
            src_ref=outs[t].at[c], dst_ref=outs[t].at[c], send_sem=send_sems.at[t], recv_sem=recv_sems.at[t],
            device_id=(x, y, 1 - c), device_id_type=MESH) for t in range(n)]
        peers = [(x, y, 1 - c)] + [(cx, cy, cc) for (cx, cy) in chips for cc in (c, 1 - c)]
        for j, peer in enumerate(peers):
            copies.append(pltpu.make_async_remote_copy(
                src_ref=small_ref, dst_ref=small_out.at[me], send_sem=send_sems.at[n + j],
                recv_sem=recv_sems.at[n + j], device_id=peer, device_id_type=MESH))
        for cp in copies:
            cp.start()
        for cp in copies:
            cp.wait()
        mine.wait()

    any_spec = pl.BlockSpec(memory_space=pl.ANY)
    res = pl.pallas_call(
        body, name="pair_gather",
        out_shape=[jax.ShapeDtypeStruct(b.shape, b.dtype) for b in bufs]
        + [jax.ShapeDtypeStruct((8,) + small.shape, small.dtype)],
        in_specs=[any_spec] * (n + 1), out_specs=[any_spec] * (n + 1),
        input_output_aliases={t: t for t in range(n)},
        scratch_shapes=[pltpu.SemaphoreType.DMA((n + 7,)), pltpu.SemaphoreType.DMA((n + 7,)),
                        pltpu.SemaphoreType.DMA],
    )(*bufs, small)
    return [r.reshape(2 * b.shape[1], b.shape[2]) for r, b in zip(res[:n], bufs)], res[n]


def _adamw(w, g, m, v, name):
    rows, width = w.shape
    tr = min(256, rows)
    c1 = 1.0 / (1.0 - ADAM_B1 ** ADAM_STEP)
    c2 = 1.0 / (1.0 - ADAM_B2 ** ADAM_STEP)

    def body(w_ref, g_ref, m_ref, v_ref, d_ref, nm_ref, nv_ref):
        gv = g_ref[...]
        nm = ADAM_B1 * m_ref[...] + (1.0 - ADAM_B1) * gv
        nv = ADAM_B2 * v_ref[...] + (1.0 - ADAM_B2) * (gv * gv)
        nm_ref[...] = nm
        nv_ref[...] = nv
        d_ref[...] = -ADAM_LR * ((nm * c1) / (jnp.sqrt(nv * c2) + ADAM_EPS) + ADAM_WD * w_ref[...])

    spec = pl.BlockSpec((tr, width), lambda i: (i, 0))
    return pl.pallas_call(
        body, name=name, grid=(rows // tr,), in_specs=[spec] * 4, out_specs=[spec] * 3,
        out_shape=[jax.ShapeDtypeStruct(w.shape, F32)] * 3,
    )(w, g, m, v)


def _local_step(x, mem, target, pre_norm, sink_a, mem_norm, post_norm, w_in_g, w_out, w_mkv):
    mk, mv = _mem_kv(mem, mem_norm, w_mkv)
    pr = _pre_proj(x, pre_norm, w_in_g)
    sink = sink_a.reshape(-1)
    qa, ka, va = pr["qa"][None], pr["ka"][None], pr["va"][None]
    oa, lse_a = _band_fwd(qa, ka, va, sink, max_dist=A_WINDOW - 1, name="swa_fwd")
    ob_list, lseb_list = [], []
    for k, (win, dil) in enumerate(B_CONFIGS):
        o_i, l_i = _band_fwd(pr["qb"][k], pr["kb"][k], pr["vb"][k], None, max_dist=win // dil, name=f"dil{dil}_fwd")
        ob_list.append(o_i)
        lseb_list.append(l_i)
    oc, lse_c = _mem_attn_fwd(pr["qc"], mk, mv)
    sink_row = jnp.pad(sink, (0, LANES - sink.shape[0])).reshape(1, LANES)
    po = _post(x, target, post_norm, w_out, sink_row, oa[0], lse_a[0], pr["ga"], ob_list, lseb_list, pr["gb"], oc,
               pr["gc"])
    dqc, dmk, dmv = _mem_attn_bwd(pr["qc"], mk, mv, po["doc"], lse_c, po["dl_c"])
    dqa, dka, dva = _band_bwd(qa, ka, va, po["doa"][None], lse_a, po["dl_a"][None], max_dist=A_WINDOW - 1,
                              name="swa_bwd")
    res = dict(qb=[], kb=[], vb=[])
    for k, (win, dil) in enumerate(B_CONFIGS):
        dq_i, dk_i, dv_i = _band_bwd(pr["qb"][k], pr["kb"][k], pr["vb"][k], po["dob"][k], po["lse_b"][k],
                                     po["dl_b"][k], max_dist=win // dil, name=f"dil{dil}_bwd")
        res["qb"].append(dq_i)
        res["kb"].append(dk_i)
        res["vb"].append(dv_i)
    nat = dict(qa=dqa[0], ka=dka[0], va=dva[0], ga=po["dga"], gb=po["dgb"], qc=dqc, gc=po["dgc"])
    dproj, gw_in = _grad_w_in(pr["ut"], nat, res)
    gw_mkv, gmem = _mem_kv_bwd(mem, mem_norm, w_mkv, dmk, dmv)
    gsink = -po["gsink"][0, :sink.shape[0]]
    return dict(loss=po["loss"], g=po["g"], dproj=dproj, gw_in=gw_in, gw_out=po["gw_out"], gw_mkv=gw_mkv,
                gpost=po["gpost"], gmem=gmem, gsink=gsink)


def kernel(x, mem, pre_norm, w_in, sink_a, mem_norm, w_mem_kv, w_out, post_norm, loss_target, m_pre_norm, m_w_in, m_sink_a, m_mem_norm, m_w_mem_kv, m_w_out, m_post_norm, v_pre_norm, v_w_in, v_sink_a, v_mem_norm, v_w_mem_kv, v_w_out, v_post_norm):
    w_in_g, w_out_g, w_mkv_g = _gather_weights(w_in[0].astype(BF16), w_out[0].astype(BF16), w_mem_kv[0].astype(BF16))
    loc = _local_step(x[0], mem[0], loss_target[0], pre_norm, sink_a, mem_norm, post_norm,
                      w_in_g.reshape(N_CHIPS, D_MODEL, SHARD_IN), w_out_g.reshape(D_MODEL, D_MODEL),
                      w_mkv_g.reshape(D_MODEL, 2 * C_W))
    big = [loc["gw_in"], loc["gw_out"].reshape(N_CHIPS, D_MODEL // N_CHIPS, D_MODEL),
           loc["gw_mkv"].reshape(N_CHIPS, D_MODEL // N_CHIPS, 2 * C_W)]
    core = lax.axis_index("c").astype(jnp.int32).reshape(1)
    w_in_full = w_in_g.reshape(N_CHIPS, D_MODEL, SHARD_IN)
    step_in = (x[0], loc["g"], pre_norm, w_in_full, loc["dproj"])
    gx_a, gpre_a, *got = _input_grad(*step_in, None, (0, 1), _pair_exchange(big), "input_grad_a")
    parts = [_pair_add(core, own, g) for own, g in zip(big, got)]
    gx_b, gpre_b, *slots = _input_grad(*step_in, gx_a, (1, 6), _chip_exchange(parts), "input_grad_b")
    grad_x, gpre_c = _input_grad(*step_in, gx_b, (7, 1), None, "input_grad_c")
    halves = [_slot_sum(s, name=f"chip_sum_{s.shape[2]}", core=core) for s in slots]
    widen = lambda a: jnp.pad(a.reshape(1, -1), ((0, 0), (0, D_MODEL - a.size)))
    small = jnp.concatenate([gpre_a, loc["gpost"], loc["gmem"], widen(loc["gsink"]), widen(loc["loss"]), gpre_b,
                             gpre_c, jnp.zeros((1, D_MODEL), F32)], axis=0)
    (g_in, g_out, g_mkv), small_slots = _pair_gather(halves, small)
    small_sum = _slot_sum(small_slots, name="device_sum")
    g_pre, g_post, g_mem = small_sum[0:1] + small_sum[5:6] + small_sum[6:7], small_sum[1:2], small_sum[2:3]
    g_sink = small_sum[3:4, :sink_a.shape[1]]
    loss = small_sum[4, 0]

    d_in, nm_in, nv_in = _adamw(w_in[0], g_in, m_w_in[0], v_w_in[0], "adamw_in")
    d_out, nm_out, nv_out = _adamw(w_out[0], g_out, m_w_out[0], v_w_out[0], "adamw_out")
    d_mkv, nm_mkv, nv_mkv = _adamw(w_mem_kv[0], g_mkv, m_w_mem_kv[0], v_w_mem_kv[0], "adamw_mkv")
    pad6 = lambda a: jnp.pad(a, ((0, 0), (0, D_MODEL - a.shape[1])))
    stack = lambda a, b, c_, d_: jnp.concatenate([a, b, c_, pad6(d_), jnp.zeros((4, D_MODEL), F32)], axis=0)
    d_s, nm_s, nv_s = _adamw(stack(pre_norm, post_norm, mem_norm, sink_a),
                             jnp.concatenate([g_pre, small_sum[1:]], axis=0),
                             stack(m_pre_norm, m_post_norm, m_mem_norm, m_sink_a),
                             stack(v_pre_norm, v_post_norm, v_mem_norm, v_sink_a), "adamw_small")
    ns_ = sink_a.shape[1]
    unpack = lambda a: (a[0:1], a[3:4, :ns_], a[2:3], a[1:2])
    d_pre, d_sink, d_mem, d_post = unpack(d_s)
    nm_pre, nm_sink, nm_mem, nm_post = unpack(nm_s)
    nv_pre, nv_sink, nv_mem, nv_post = unpack(nv_s)
    lead = lambda a: a[None]
    return (loss, lead(grad_x),
            g_pre, lead(g_in), g_sink, g_mem, lead(g_mkv), lead(g_out), g_post,
            d_pre, lead(d_in), d_sink, d_mem, lead(d_mkv), lead(d_out), d_post,
            nm_pre, lead(nm_in), nm_sink, nm_mem, lead(nm_mkv), lead(nm_out), nm_post,
            nv_pre, lead(nv_in), nv_sink, nv_mem, lead(nv_mkv), lead(nv_out), nv_post)
```

```python
import numpy as np
import jax
import jax.numpy as jnp
from jax import lax
from jax.experimental import pallas as pl
from jax.experimental.pallas import tpu as pltpu

F32 = jnp.float32
BF16 = jnp.bfloat16

D_MODEL = 1024
HEAD_DIM = 64
LANES = 128
BLOCK = 128
A_W, A_KV_W, B_W, C_W = 384, 128, 384, 256
N_MEM = 256
D_IN = 3072
N_CHIPS = 4
SHARD_IN = D_IN // N_CHIPS
B_CONFIGS = ((128, 1), (512, 4), (2048, 16))
B_DILS = tuple(d for _, d in B_CONFIGS)
A_WINDOW = 128
RMS_EPS = 1e-6
ROPE_THETA = 500000.0
SCALE = HEAD_DIM ** -0.5
NEG = -1e30
ADAM_LR, ADAM_B1, ADAM_B2, ADAM_EPS, ADAM_WD, ADAM_STEP = 0.001, 0.9, 0.999, 1e-08, 0.01, 10

NT = (((1,), (1,)), ((), ()))
TN = (((0,), (0,)), ((), ()))
MESH = pl.DeviceIdType.MESH

_PROJ_LAYOUT = (
    [("qa", 128 * i, True, True) for i in range(3)] + [("ka", 0, True, False), ("va", 0, False, False)]
    + [("ga", 128 * i, False, False) for i in range(3)]
    + [("qb", 128 * i, True, True) for i in range(3)] + [("kb", 128 * i, True, False) for i in range(3)]
    + [("vb", 128 * i, False, False) for i in range(3)] + [("gb", 128 * i, False, False) for i in range(3)]
    + [("qc", 128 * i, False, True) for i in range(2)] + [("gc", 128 * i, False, False) for i in range(2)]
)
_PROJ_WIDTH = dict(qa=A_W, ka=A_KV_W, va=A_KV_W, ga=A_W, qb=B_W, kb=B_W, vb=B_W, gb=B_W, qc=C_W, gc=C_W)
_NATURAL = ("qa", "ka", "va", "ga", "gb", "qc", "gc")
_DILATED = ("qb", "kb", "vb")


def _dot(a, b):
    return jnp.dot(a, b, preferred_element_type=F32)


def _dot_nt(a, b):
    return lax.dot_general(a, b, NT, preferred_element_type=F32)


def _dot_tn(a, b):
    return lax.dot_general(a, b, TN, preferred_element_type=F32)


def _half_masks(rows):
    lane = lax.broadcasted_iota(jnp.int32, (rows, LANES), 1)
    return lane < HEAD_DIM, lane >= HEAD_DIM


def _rope(t, c, sm, sp):
    return t * c + pltpu.roll(t, LANES - 8, 1) * sm + pltpu.roll(t, 8, 1) * sp


def _rope_tables(seq, tm):
    dim = jnp.arange(LANES) % HEAD_DIM
    inv_freq = ROPE_THETA ** (-jnp.arange(0, 16, 2, dtype=F32) / 16)
    freq = jnp.where(dim < 16, inv_freq[dim % 8], 0.0)[None, :]
    local = jnp.arange(tm, dtype=F32)[:, None] * freq
    base = (jnp.arange(seq // tm, dtype=F32) * tm)[:, None] * freq
    both = lambda a: jnp.concatenate([jnp.cos(a), jnp.sin(a)], axis=1)
    return both(local), jnp.repeat(both(base), 8, axis=0)


def _rope_coeffs(local_ref, base_ref):
    cl, sl = local_ref[:, :LANES], local_ref[:, LANES:]
    cb, sb = base_ref[0:1, :LANES], base_ref[0:1, LANES:]
    cos = cb * cl - sb * sl
    sin = sb * cl + cb * sl
    dim = lax.broadcasted_iota(jnp.int32, (1, LANES), 1) % HEAD_DIM
    return cos, jnp.where(dim < 8, -sin, 0.0), jnp.where((dim >= 8) & (dim < 16), sin, 0.0)


def _split3(x):
    a = x.astype(BF16)
    r = x - a.astype(F32)
    b = r.astype(BF16)
    c = (r - b.astype(F32)).astype(BF16)
    return a, b, c


def _rows_to_lanes(x):
    row = lax.broadcasted_iota(jnp.int32, (8, LANES), 0)
    lane = lax.broadcasted_iota(jnp.int32, (8, LANES), 1)
    eye = (row == lane).astype(BF16)
    a, b, c = _split3(x)
    return _dot_nt(eye, a) + _dot_nt(eye, b) + _dot_nt(eye, c)


def _head_sum_matrix(width):
    k = lax.broadcasted_iota(jnp.int32, (width, LANES), 0)
    h = lax.broadcasted_iota(jnp.int32, (width, LANES), 1)
    return (k // HEAD_DIM == h).astype(BF16)


def _head_expand_matrix(width):
    h = lax.broadcasted_iota(jnp.int32, (LANES, width), 0)
    k = lax.broadcasted_iota(jnp.int32, (LANES, width), 1)
    return (k // HEAD_DIM == h).astype(BF16)


def _dot_split(x, mat, terms):
    parts = _split3(x)[:terms]
    out = _dot(parts[0], mat)
    for p in parts[1:]:
        out = out + _dot(p, mat)
    return out


def _per_head(cols, fill=0.0):
    rows = cols[0].shape[0]
    lane = lax.broadcasted_iota(jnp.int32, (rows, LANES), 1)
    out = jnp.full((rows, LANES), fill, F32)
    for h, col in enumerate(cols):
        out = jnp.where(lane == h, col, out)
    return out


def _lane_blocks(width):
    return [slice(p * LANES, (p + 1) * LANES) for p in range(width // LANES)]


def _stage(rows, width):
    return pltpu.VMEM((width // LANES, rows, LANES), F32)


def _stage_write(buf, value):
    for p, lanes in enumerate(_lane_blocks(value.shape[1])):
        buf[p] = value[:, lanes]


def _stage_read(buf):
    return jnp.concatenate([buf[p] for p in range(buf.shape[0])], axis=1) if buf.shape[0] > 1 else buf[0]


def _to_residues(buf, out_ref, dil):
    rows = buf.shape[1] // dil
    for r in range(dil):
        for p in range(buf.shape[0]):
            plane = buf.at[p]
            out_ref[r, :, p * LANES:(p + 1) * LANES] = plane[pl.ds(r, rows, stride=dil), :].astype(out_ref.dtype)


def _from_residues(in_ref, buf, dil):
    rows = buf.shape[1] // dil
    for r in range(dil):
        for p in range(buf.shape[0]):
            plane = buf.at[p]
            plane[pl.ds(r, rows, stride=dil), :] = in_ref[r, :, p * LANES:(p + 1) * LANES].astype(F32)


def _residue_spec(dil, tm, width):
    return pl.BlockSpec((dil, tm // dil, width), lambda i: (0, i, 0))


def _gather_exchange(shards_2d):
    shards = tuple(s.reshape(2, s.shape[0] // 2, s.shape[1]) for s in shards_2d)
    n = len(shards)

    def copies(in_refs, out_refs, send_sems, recv_sems):
        srcs, outs = in_refs[:n], out_refs
        x, y, c = lax.axis_index("x"), lax.axis_index("y"), lax.axis_index("c")
        my_chip = 2 * x + y
        sibling = (x, y, 1 - c)
        chips = [(1 - x, y), (x, 1 - y), (1 - x, 1 - y)]

        def copy(k, src, dst, to):
            return pltpu.make_async_remote_copy(src_ref=src, dst_ref=dst, send_sem=send_sems.at[k],
                                                recv_sem=recv_sems.at[k], device_id=to, device_id_type=MESH)

        first, arrive, passed, sibling_arrive = [], [], [], []
        for j, (cx, cy) in enumerate(chips):
            chip = 2 * cx + cy
            for t in range(n):
                k = n * j + t
                first.append(copy(k, srcs[t].at[c], outs[t].at[my_chip, c], (cx, cy, c)))
                arrive.append(copy(k, srcs[t].at[c], outs[t].at[chip, c], (cx, cy, c)))
                passed.append(copy(n * 3 + k, outs[t].at[chip, c], outs[t].at[chip, c], sibling))
                sibling_arrive.append(copy(n * 3 + k, outs[t].at[chip, 1 - c], outs[t].at[chip, 1 - c], sibling))
        return first, arrive, passed, sibling_arrive

    def start(*refs):
        for cp in copies(*refs)[0]:
            cp.start()

    def finish(*refs):
        first, arrive, passed, sibling_arrive = copies(*refs)
        for got, fwd in zip(arrive, passed):
            got.wait_recv()
            fwd.start()
        for cp in sibling_arrive:
            cp.wait_recv()
        for cp in first + passed:
            cp.wait_send()

    my_chip = 2 * lax.axis_index("x") + lax.axis_index("y")
    landing = [lax.dynamic_update_slice(jnp.zeros((N_CHIPS,) + s.shape, s.dtype), s[None], (my_chip, 0, 0, 0))
               for s in shards]
    return dict(ins=list(shards) + landing, start=start, finish=finish, aliases={n + t: t for t in range(n)},
                outs=[jax.ShapeDtypeStruct((N_CHIPS,) + s.shape, s.dtype) for s in shards],
                sems=[pltpu.SemaphoreType.DMA((6 * n,)), pltpu.SemaphoreType.DMA((6 * n,))])


def _run_exchange(ex, name):
    n_in, n_out = len(ex["ins"]), len(ex["outs"])

    def body(*refs):
        in_refs, out_refs, sems = refs[:n_in], refs[n_in:n_in + n_out], refs[n_in + n_out:]
        ex["start"](in_refs, out_refs, *sems)
        ex["finish"](in_refs, out_refs, *sems)

    any_spec = pl.BlockSpec(memory_space=pl.ANY)
    return pl.pallas_call(
        body, name=name, out_shape=ex["outs"], in_specs=[any_spec] * n_in, out_specs=[any_spec] * n_out,
        input_output_aliases=ex.get("aliases", {}), scratch_shapes=ex["sems"],
    )(*ex["ins"])


def _mem_kv(mem, mem_norm, w_mkv):
    def body(mem_ref, g_ref, w_ref, mk_ref, mv_ref):
        m = mem_ref[...]
        r = lax.rsqrt(jnp.mean(m * m, axis=-1, keepdims=True) + RMS_EPS)
        mn = (m * r * g_ref[...]).astype(BF16)
        kv = _dot(mn, w_ref[...])
        mk_ref[...] = kv[:, :C_W].astype(BF16)
        mv_ref[...] = kv[:, C_W:].astype(BF16)

    return pl.pallas_call(
        body, name="mem_kv",
        out_shape=[jax.ShapeDtypeStruct((N_MEM, C_W), BF16)] * 2,
    )(mem, mem_norm, w_mkv)


def _mem_kv_bwd(mem, mem_norm, w_mkv, dmk, dmv):
    def body(mem_ref, g_ref, w_ref, dmk_ref, dmv_ref, gw_ref, gn_ref):
        m = mem_ref[...]
        r = lax.rsqrt(jnp.mean(m * m, axis=-1, keepdims=True) + RMS_EPS)
        mhat = m * r
        mn = (mhat * g_ref[...]).astype(BF16)
        dkv = jnp.concatenate([dmk_ref[...], dmv_ref[...]], axis=1).astype(BF16)
        gw_ref[...] = _dot_tn(mn, dkv)
        dmn = _dot_nt(dkv, w_ref[...])
        gn_ref[...] = jnp.sum(dmn * mhat, axis=0, keepdims=True)

    return pl.pallas_call(
        body, name="mem_kv_bwd",
        out_shape=[jax.ShapeDtypeStruct((D_MODEL, 2 * C_W), F32), jax.ShapeDtypeStruct((1, D_MODEL), F32)],
    )(mem, mem_norm, w_mkv, dmk, dmv)


def _pre_proj(x, pre_norm, w_in_g, host=None):
    seq = x.shape[0]
    tm = min(512, seq)
    n_nat, n_dil = len(_NATURAL), len(_DILATED) * len(B_DILS)
    rope = _rope_tables(seq, tm)

    n_host_in = len(host["ins"]) if host else 0
    n_host_out = len(host["outs"]) if host else 0
    n_own_out = n_nat + n_dil + 1

    def body(x_ref, g_ref, w_ref, rl_ref, rb_ref, *refs):
        host_in, refs = refs[:n_host_in], refs[n_host_in:]
        nat = dict(zip(_NATURAL, refs[:n_nat]))
        res = {n: refs[n_nat + len(B_DILS) * k:n_nat + len(B_DILS) * (k + 1)] for k, n in enumerate(_DILATED)}
        ut = refs[n_nat + n_dil]
        host_out = refs[n_own_out:n_own_out + n_host_out]
        bufs = dict(zip(_DILATED, refs[n_own_out + n_host_out:]))
        sems = refs[n_own_out + n_host_out + len(_DILATED):]
        if host:
            @pl.when(pl.program_id(0) == 0)
            def _():
                host["start"](host_in, host_out, *sems)

        xv = x_ref[...]
        r = lax.rsqrt(jnp.mean(xv * xv, axis=-1, keepdims=True) + RMS_EPS)
        u = xv * r * g_ref[...]
        ub = u.astype(BF16)
        ut[...] = u.T.astype(BF16)
        c, sm, sp = _rope_coeffs(rl_ref, rb_ref)
        for j in range(N_CHIPS):
            pj = _dot(ub, w_ref[j])
            for b in range(SHARD_IN // LANES):
                name, off, roped, scaled = _PROJ_LAYOUT[(SHARD_IN // LANES) * j + b]
                piece = pj[:, LANES * b:LANES * (b + 1)]
                if roped:
                    piece = _rope(piece, c, sm, sp)
                if scaled:
                    piece = piece * SCALE
                if name in bufs:
                    bufs[name][off // LANES] = piece
                else:
                    nat[name][:, off:off + LANES] = piece.astype(BF16)
        for name in _DILATED:
            for ref, dil in zip(res[name], B_DILS):
                _to_residues(bufs[name], ref, dil)
        if host:
            @pl.when(pl.program_id(0) == seq // tm - 1)
            def _():
                host["finish"](host_in, host_out, *sems)

    row = lambda w: pl.BlockSpec((tm, w), lambda i: (i, 0))
    full = lambda a: pl.BlockSpec(a.shape, lambda i: (0,) * a.ndim)
    any_spec = pl.BlockSpec(memory_space=pl.ANY)
    out_shape = [jax.ShapeDtypeStruct((seq, _PROJ_WIDTH[n]), BF16) for n in _NATURAL]
    out_specs = [row(_PROJ_WIDTH[n]) for n in _NATURAL]
    for n in _DILATED:
        for dil in B_DILS:
            out_shape.append(jax.ShapeDtypeStruct((dil, seq // dil, B_W), BF16))
            out_specs.append(_residue_spec(dil, tm, B_W))
    out_shape.append(jax.ShapeDtypeStruct((D_MODEL, seq), BF16))
    out_specs.append(pl.BlockSpec((D_MODEL, tm), lambda i: (0, i)))
    ins = [x, pre_norm, w_in_g, *rope]
    in_specs = [row(D_MODEL), full(pre_norm), full(w_in_g), full(rope[0]), pl.BlockSpec((8, 2 * LANES), lambda i: (i, 0))]
    scratch = [_stage(tm, B_W)] * len(_DILATED)
    aliases = {}
    if host:
        aliases = {len(ins) + k: n_own_out + v for k, v in host.get("aliases", {}).items()}
        ins += list(host["ins"])
        in_specs += [any_spec] * n_host_in
        out_shape += list(host["outs"])
        out_specs += [any_spec] * n_host_out
        scratch += list(host["sems"])
    res = pl.pallas_call(
        body, name="pre_proj", grid=(seq // tm,), in_specs=in_specs, out_specs=out_specs, out_shape=out_shape,
        input_output_aliases=aliases, scratch_shapes=scratch,
        compiler_params=pltpu.CompilerParams(dimension_semantics=("arbitrary",)),
    )(*ins)
    out = dict(zip(_NATURAL, res[:n_nat]))
    for k, n in enumerate(_DILATED):
        out[n] = res[n_nat + len(B_DILS) * k:n_nat + len(B_DILS) * (k + 1)]
    out["ut"] = res[n_nat + n_dil]
    out["hosted"] = res[n_own_out:]
    return out


def _band_bias(max_dist, transposed):
    i = np.arange(BLOCK)[:, None]
    j = np.arange(BLOCK)[None, :]
    if transposed:
        same = i <= j
        other = (j + BLOCK - i) <= max_dist
        vis = np.concatenate([same, other], axis=1)
    else:
        prev = (i + BLOCK - j) <= max_dist
        same = j <= i
        vis = np.concatenate([prev, same], axis=1)
    return jnp.asarray(np.where(vis, 0.0, NEG).astype(np.float32))


def _kv_place(h, gqa):
    return (0, h // 3) if gqa else (h // 2, h % 2)


def _band_fwd(q, k, v, sink, *, max_dist, name):
    dil, length, wq = q.shape
    wk = k.shape[2]
    gqa = wk != wq
    tq = min(512, length)
    ns, nt = tq // BLOCK, length // tq
    npair = wq // LANES
    bias = _band_bias(max_dist, transposed=False)
    has_sink = sink is not None

    def body(*refs):
        if has_sink:
            sink_ref, refs = refs[0], refs[1:]
        q_ref, k_ref, kp_ref, v_ref, vp_ref, bias_ref, o_ref, lse_ref, kbuf, vbuf = refs[:10]
        i = pl.program_id(1)
        kbuf[0:BLOCK] = kp_ref[...]
        kbuf[BLOCK:] = k_ref[...]
        vbuf[0:BLOCK] = vp_ref[...]
        vbuf[BLOCK:] = v_ref[...]
        if gqa:
            kroll, vroll = refs[10:12]
            kroll[...] = pltpu.roll(kbuf[...], HEAD_DIM, 1)
            vroll[...] = pltpu.roll(vbuf[...], HEAD_DIM, 1)
        half = _half_masks(BLOCK)
        col_prev = (lax.broadcasted_iota(jnp.int32, (1, 2 * BLOCK), 1) < BLOCK).astype(F32)

        def sub(a, carry):
            r0 = pl.multiple_of(a * BLOCK, BLOCK)
            pen = jnp.where((i == 0) & (a == 0), NEG, 0.0)
            b = bias_ref[...] + pen * col_prev
            scores = []
            for p in range(npair):
                qp = q_ref[pl.ds(r0, BLOCK), p * LANES:(p + 1) * LANES]
                for e in range(2):
                    pk, ek = _kv_place(2 * p + e, gqa)
                    kw = (kbuf if ek == e else kroll)[pl.ds(r0, 2 * BLOCK), pk * LANES:(pk + 1) * LANES]
                    scores.append(_dot_nt(jnp.where(half[e], qp, jnp.zeros_like(qp)), kw))
            m_cols, l_cols, probs = [], [], []
            for h, s in enumerate(scores):
                s = s + b
                m = jnp.max(s, axis=1, keepdims=True)
                if has_sink:
                    m = jnp.maximum(m, sink_ref[h])
                pe = jnp.exp(s - m)
                l = jnp.sum(pe, axis=1, keepdims=True)
                if has_sink:
                    l = l + jnp.exp(sink_ref[h] - m)
                probs.append(pe.astype(BF16))
                m_cols.append(m)
                l_cols.append(l)
            for p in range(npair):
                o_h = []
                for e in range(2):
                    h = 2 * p + e
                    pk, ek = _kv_place(h, gqa)
                    vw = (vbuf if ek == e else vroll)[pl.ds(r0, 2 * BLOCK), pk * LANES:(pk + 1) * LANES]
                    o_h.append(_dot(probs[h], vw) * (1.0 / l_cols[h]))
                o_ref[pl.ds(r0, BLOCK), p * LANES:(p + 1) * LANES] = jnp.where(half[0], o_h[0], o_h[1]).astype(BF16)
            lse_ref[pl.ds(r0, BLOCK), :] = _per_head(m_cols) + jnp.log(_per_head(l_cols, 1.0))
            return carry

        lax.fori_loop(0, ns, sub, 0, unroll=True)

    main = lambda w: pl.BlockSpec((None, tq, w), lambda r, i: (r, i, 0))
    prev = lambda w: pl.BlockSpec((None, BLOCK, w), lambda r, i: (r, jnp.maximum(i * ns - 1, 0), 0))
    in_specs = [main(wq), main(wk), prev(wk), main(wk), prev(wk), pl.BlockSpec(bias.shape, lambda r, i: (0, 0))]
    args = [q, k, k, v, v, bias]
    if has_sink:
        in_specs = [pl.BlockSpec(memory_space=pltpu.SMEM)] + in_specs
        args = [sink] + args
    scratch = [pltpu.VMEM((tq + BLOCK, wk), BF16)] * (4 if gqa else 2)
    return pl.pallas_call(
        body, name=name, grid=(dil, nt), in_specs=in_specs,
        out_specs=[main(wq), main(LANES)],
        out_shape=[jax.ShapeDtypeStruct((dil, length, wq), BF16), jax.ShapeDtypeStruct((dil, length, LANES), F32)],
        scratch_shapes=scratch,
    )(*args)


def _band_bwd(q, k, v, do, lse, delta, *, max_dist, name):
    dil, length, wq = q.shape
    wk = k.shape[2]
    gqa = wk != wq
    tq = min(512, length)
    ns, nt = tq // BLOCK, length // tq
    npair = wq // LANES
    nblocks = length // BLOCK
    bias = _band_bias(max_dist, transposed=True)

    def body(q_ref, qn_ref, do_ref, don_ref, lse_ref, lsen_ref, dl_ref, dln_ref, k_ref, v_ref, bias_ref,
             dq_ref, dk_ref, dv_ref, qbuf, dobuf, stat_l, stat_d, dqt, kt, *rolled):
        i = pl.program_id(1)
        qbuf[0:tq] = q_ref[...]
        qbuf[tq:] = qn_ref[...]
        dobuf[0:tq] = do_ref[...]
        dobuf[tq:] = don_ref[...]
        for pk in range(wk // LANES):
            kt[pk] = k_ref[:, pk * LANES:(pk + 1) * LANES].astype(F32).T.astype(BF16)
        if gqa:
            kroll, vroll, ktroll = rolled
            kroll[...] = pltpu.roll(k_ref[...], HEAD_DIM, 1)
            vroll[...] = pltpu.roll(v_ref[...], HEAD_DIM, 1)
            ktroll[0] = kroll[...].astype(F32).T.astype(BF16)
        for a in range(ns):
            rows = slice(a * BLOCK, (a + 1) * BLOCK)
            stat_l[a] = _rows_to_lanes(lse_ref[rows, :])
            stat_d[a] = _rows_to_lanes(dl_ref[rows, :])
        stat_l[ns] = _rows_to_lanes(lsen_ref[...])
        stat_d[ns] = _rows_to_lanes(dln_ref[...])

        @pl.when(i == 0)
        def _():
            dqt[:, :, 0:BLOCK] = jnp.zeros((npair, LANES, BLOCK), F32)

        @pl.when(i > 0)
        def _():
            dqt[:, :, 0:BLOCK] = dqt[:, :, tq:tq + BLOCK]

        dqt[:, :, BLOCK:] = jnp.zeros((npair, LANES, tq), F32)
        half2 = _half_masks(2 * BLOCK)
        row = lax.broadcasted_iota(jnp.int32, (LANES, BLOCK), 0)
        row_half = (row < HEAD_DIM, row >= HEAD_DIM)
        col_next = (lax.broadcasted_iota(jnp.int32, (1, 2 * BLOCK), 1) >= BLOCK).astype(F32)

        for b in range(ns):
            rows = slice(b * BLOCK, (b + 1) * BLOCK)
            window = slice(b * BLOCK, (b + 2) * BLOCK)
            bt = bias_ref[...]
            if b == ns - 1:
                bt = bt + jnp.where(i == nt - 1, NEG, 0.0) * col_next
            acc = {}
            items = []
            for p in range(npair):
                lanes = slice(p * LANES, (p + 1) * LANES)
                qw = qbuf[window, lanes]
                dow = dobuf[window, lanes]
                for e in range(2):
                    h = 2 * p + e
                    pk, ek = _kv_place(h, gqa)
                    klanes = slice(pk * LANES, (pk + 1) * LANES)
                    kb = (k_ref if ek == e else kroll)[rows, klanes]
                    vb = (v_ref if ek == e else vroll)[rows, klanes]
                    qm = jnp.where(half2[e], qw, jnp.zeros_like(qw))
                    dom = jnp.where(half2[e], dow, jnp.zeros_like(dow))
                    items.append(dict(p=p, e=e, h=h, pk=pk, ek=ek, qm=qm, dom=dom,
                                      st=_dot_nt(kb, qm), dpt=_dot_nt(vb, dom)))
            for it in items:
                h = it["h"]
                lrow = jnp.concatenate([stat_l[b, h:h + 1, :], stat_l[b + 1, h:h + 1, :]], axis=1)
                drow = jnp.concatenate([stat_d[b, h:h + 1, :], stat_d[b + 1, h:h + 1, :]], axis=1)
                pt = jnp.exp(it["st"] + bt - lrow)
                it["ptb"] = pt.astype(BF16)
                it["dsb"] = (pt * (it["dpt"] - drow)).astype(BF16)
            for it in items:
                p, e, pk, ek = it["p"], it["e"], it["pk"], it["ek"]
                dv_c = _dot(it["ptb"], it["dom"])
                dk_c = _dot(it["dsb"], it["qm"])
                kbt = (kt if ek == e else ktroll)[pk, :, rows]
                kbtm = jnp.where(row_half[e], kbt, jnp.zeros_like(kbt))
                dqt[p, :, window] += _dot(kbtm, it["dsb"])
                key = (pk, ek == e)
                if key in acc:
                    acc[key] = (acc[key][0] + dk_c, acc[key][1] + dv_c)
                else:
                    acc[key] = (dk_c, dv_c)
            if not gqa:
                for p in range(npair):
                    lanes = slice(p * LANES, (p + 1) * LANES)
                    dk_ref[rows, lanes] = acc[(p, True)][0].astype(BF16)
                    dv_ref[rows, lanes] = acc[(p, True)][1].astype(BF16)
            if gqa:
                dk_al, dv_al = acc[(0, True)]
                dk_mis, dv_mis = acc[(0, False)]
                dk_ref[rows, :] = (dk_al + pltpu.roll(dk_mis, HEAD_DIM, 1)).astype(BF16)
                dv_ref[rows, :] = (dv_al + pltpu.roll(dv_mis, HEAD_DIM, 1)).astype(BF16)

        for p in range(npair):
            dq_ref[:, p * LANES:(p + 1) * LANES] = dqt[p, :, 0:tq].T.astype(BF16)

    main = lambda w: pl.BlockSpec((None, tq, w), lambda r, i: (r, i, 0))
    nxt = lambda w: pl.BlockSpec((None, BLOCK, w), lambda r, i: (r, jnp.minimum((i + 1) * ns, nblocks - 1), 0))
    scratch = [pltpu.VMEM((tq + BLOCK, wq), BF16), pltpu.VMEM((tq + BLOCK, wq), BF16),
               pltpu.VMEM((ns + 1, 8, LANES), F32), pltpu.VMEM((ns + 1, 8, LANES), F32),
               pltpu.VMEM((npair, LANES, tq + BLOCK), F32), pltpu.VMEM((wk // LANES, LANES, tq), BF16)]
    if gqa:
        scratch = scratch + [pltpu.VMEM((tq, wk), BF16)] * 2 + [pltpu.VMEM((1, LANES, tq), BF16)]
    return pl.pallas_call(
        body, name=name, grid=(dil, nt),
        in_specs=[main(wq), nxt(wq), main(wq), nxt(wq), main(LANES), nxt(LANES), main(LANES), nxt(LANES),
                  main(wk), main(wk), pl.BlockSpec(bias.shape, lambda r, i: (0, 0))],
        out_specs=[main(wq), main(wk), main(wk)],
        out_shape=[jax.ShapeDtypeStruct((dil, length, wq), BF16), jax.ShapeDtypeStruct((dil, length, wk), BF16),
                   jax.ShapeDtypeStruct((dil, length, wk), BF16)],
        scratch_shapes=scratch,
        compiler_params=pltpu.CompilerParams(dimension_semantics=("arbitrary", "arbitrary")),
    )(q, q, do, do, lse, lse, delta, delta, k, v, bias)


def _mem_attn_fwd(q, mk, mv):
    seq = q.shape[0]
    tq = min(512, seq)
    ns = tq // BLOCK

    def body(q_ref, mk_ref, mv_ref, o_ref, lse_ref):
        half = _half_masks(BLOCK)

        def sub(a, carry):
            r0 = pl.multiple_of(a * BLOCK, BLOCK)
            scores = []
            for p in range(C_W // LANES):
                lanes = slice(p * LANES, (p + 1) * LANES)
                qp = q_ref[pl.ds(r0, BLOCK), lanes]
                for e in range(2):
                    scores.append(_dot_nt(jnp.where(half[e], qp, jnp.zeros_like(qp)), mk_ref[:, lanes]))
            m_cols, l_cols, probs = [], [], []
            for s in scores:
                m = jnp.max(s, axis=1, keepdims=True)
                pe = jnp.exp(s - m)
                probs.append(pe.astype(BF16))
                m_cols.append(m)
                l_cols.append(jnp.sum(pe, axis=1, keepdims=True))
            for p in range(C_W // LANES):
                lanes = slice(p * LANES, (p + 1) * LANES)
                o_h = [_dot(probs[2 * p + e], mv_ref[:, lanes]) * (1.0 / l_cols[2 * p + e]) for e in range(2)]
                o_ref[pl.ds(r0, BLOCK), lanes] = jnp.where(half[0], o_h[0], o_h[1]).astype(BF16)
            lse_ref[pl.ds(r0, BLOCK), :] = _per_head(m_cols) + jnp.log(_per_head(l_cols, 1.0))
            return carry

        lax.fori_loop(0, ns, sub, 0, unroll=True)

    row = lambda w: pl.BlockSpec((tq, w), lambda i: (i, 0))
    full = pl.BlockSpec((N_MEM, C_W), lambda i: (0, 0))
    return pl.pallas_call(
        body, name="mem_attn_fwd", grid=(seq // tq,), in_specs=[row(C_W), full, full],
        out_specs=[row(C_W), row(LANES)],
        out_shape=[jax.ShapeDtypeStruct((seq, C_W), BF16), jax.ShapeDtypeStruct((seq, LANES), F32)],
    )(q, mk, mv)


def _mem_attn_bwd(q, mk, mv, do, lse, delta):
    seq = q.shape[0]
    tq = min(512, seq)
    ns = tq // BLOCK
    npair = C_W // LANES

    def body(q_ref, mk_ref, mv_ref, do_ref, lse_ref, dl_ref, dq_ref, dmk_ref, dmv_ref, stat_l, stat_d, mkt, dqt):
        @pl.when(pl.program_id(0) == 0)
        def _():
            dmk_ref[...] = jnp.zeros_like(dmk_ref)
            dmv_ref[...] = jnp.zeros_like(dmv_ref)
            for p in range(npair):
                mkt[p] = mk_ref[:, p * LANES:(p + 1) * LANES].astype(F32).T.astype(BF16)

        for a in range(ns):
            rows = slice(a * BLOCK, (a + 1) * BLOCK)
            stat_l[a] = _rows_to_lanes(lse_ref[rows, :])
            stat_d[a] = _rows_to_lanes(dl_ref[rows, :])
        half = _half_masks(BLOCK)
        row = lax.broadcasted_iota(jnp.int32, (LANES, N_MEM), 0)
        row_half = (row < HEAD_DIM, row >= HEAD_DIM)

        for a in range(ns):
            rows = slice(a * BLOCK, (a + 1) * BLOCK)
            items = []
            for p in range(npair):
                lanes = slice(p * LANES, (p + 1) * LANES)
                qp = q_ref[rows, lanes]
                dop = do_ref[rows, lanes]
                for e in range(2):
                    qm = jnp.where(half[e], qp, jnp.zeros_like(qp))
                    dom = jnp.where(half[e], dop, jnp.zeros_like(dop))
                    items.append(dict(p=p, e=e, qm=qm, dom=dom, st=_dot_nt(mk_ref[:, lanes], qm),
                                      dpt=_dot_nt(mv_ref[:, lanes], dom)))
            for it in items:
                h = 2 * it["p"] + it["e"]
                pt = jnp.exp(it["st"] - stat_l[a, h:h + 1, :])
                it["ptb"] = pt.astype(BF16)
                it["dsb"] = (pt * (it["dpt"] - stat_d[a, h:h + 1, :])).astype(BF16)
            for p in range(npair):
                lanes = slice(p * LANES, (p + 1) * LANES)
                pair = [it for it in items if it["p"] == p]
                dmv_ref[:, lanes] += _dot(pair[0]["ptb"], pair[0]["dom"]) + _dot(pair[1]["ptb"], pair[1]["dom"])
                dmk_ref[:, lanes] += _dot(pair[0]["dsb"], pair[0]["qm"]) + _dot(pair[1]["dsb"], pair[1]["qm"])
                kbt = mkt[p]
                dqt[p, :, rows] = (_dot(jnp.where(row_half[0], kbt, jnp.zeros_like(kbt)), pair[0]["dsb"])
                                   + _dot(jnp.where(row_half[1], kbt, jnp.zeros_like(kbt)), pair[1]["dsb"]))
        for p in range(npair):
            dq_ref[:, p * LANES:(p + 1) * LANES] = dqt[p].T.astype(BF16)

    row = lambda w: pl.BlockSpec((tq, w), lambda i: (i, 0))
    full = pl.BlockSpec((N_MEM, C_W), lambda i: (0, 0))
    return pl.pallas_call(
        body, name="mem_attn_bwd", grid=(seq // tq,),
        in_specs=[row(C_W), full, full, row(C_W), row(LANES), row(LANES)], out_specs=[row(C_W), full, full],
        out_shape=[jax.ShapeDtypeStruct((seq, C_W), BF16), jax.ShapeDtypeStruct((N_MEM, C_W), F32),
                   jax.ShapeDtypeStruct((N_MEM, C_W), F32)],
        scratch_shapes=[pltpu.VMEM((ns, 8, LANES), F32)] * 2
        + [pltpu.VMEM((npair, LANES, N_MEM), BF16), pltpu.VMEM((npair, LANES, tq), F32)],
        compiler_params=pltpu.CompilerParams(dimension_semantics=("arbitrary",)),
    )(q, mk, mv, do, lse, delta)


def _silu_and_grad(g):
    s = 1.0 / (1.0 + jnp.exp(-g))
    return g * s, s * (1.0 + g * (1.0 - s))


def _post(x, target, post_norm, w_out, sink_row, oa, lse_a, ga, ob_list, lseb_list, gb, oc, gc):
    seq = x.shape[0]
    tm = min(512, seq)
    inv_d = 1.0 / D_MODEL
    nd = len(B_DILS)

    def body(*refs):
        (x_ref, t_ref, gp_ref, w_ref, sink_ref, oa_ref, lsea_ref, ga_ref), refs = refs[:8], refs[8:]
        ob_refs, lb_refs, (gb_ref, oc_ref, gc_ref), refs = refs[:nd], refs[nd:2 * nd], refs[2 * nd:2 * nd + 3], refs[2 * nd + 3:]
        (g_ref, doa_ref, dla_ref, dga_ref), refs = refs[:4], refs[4:]
        dob_refs, lsec_refs, dlb_refs, refs = refs[:nd], refs[nd:2 * nd], refs[2 * nd:3 * nd], refs[3 * nd:]
        (dgb_ref, doc_ref, dlc_ref, dgc_ref, gw_ref, gpost_ref, gsink_ref, loss_ref), refs = refs[:8], refs[8:]
        ycat, obufs, lbufs, st_do, st_l, st_d = refs[0], refs[1:nd], refs[nd:2 * nd - 1], refs[2 * nd - 1], refs[2 * nd], refs[2 * nd + 1]

        @pl.when(pl.program_id(0) == 0)
        def _():
            gw_ref[...] = jnp.zeros_like(gw_ref)
            gpost_ref[...] = jnp.zeros_like(gpost_ref)
            gsink_ref[...] = jnp.zeros_like(gsink_ref)
            loss_ref[...] = jnp.zeros_like(loss_ref)

        o_i, l_i = [ob_refs[0][0].astype(F32)], [lb_refs[0][0]]
        for k in range(1, nd):
            _from_residues(ob_refs[k], obufs[k - 1], B_DILS[k])
            _from_residues(lb_refs[k], lbufs[k - 1], B_DILS[k])
            o_i.append(_stage_read(obufs[k - 1]))
            l_i.append(_stage_read(lbufs[k - 1]))
        mx = l_i[0]
        for l in l_i[1:]:
            mx = jnp.maximum(mx, l)
        w_i = [jnp.exp(l - mx) for l in l_i]
        z = w_i[0]
        for w in w_i[1:]:
            z = z + w
        _stage_write(st_l, mx + jnp.log(z))
        expand = _head_expand_matrix(B_W)
        inv_z = 1.0 / z
        ob = None
        for w, o in zip(w_i, o_i):
            term = _dot_split(w * inv_z, expand, 2) * o
            ob = term if ob is None else ob + term
        oa, oc = oa_ref[...].astype(F32), oc_ref[...].astype(F32)
        sa, dsa = _silu_and_grad(ga_ref[...].astype(F32))
        sb, dsb = _silu_and_grad(gb_ref[...].astype(F32))
        sc, dsc = _silu_and_grad(gc_ref[...].astype(F32))
        ycat[:, 0:A_W] = (oa * sa).astype(BF16)
        ycat[:, A_W:A_W + B_W] = (ob * sb).astype(BF16)
        ycat[:, A_W + B_W:] = (oc * sc).astype(BF16)
        y2 = _dot(ycat[...], w_ref[...])
        r = lax.rsqrt(jnp.mean(y2 * y2, axis=-1, keepdims=True) + RMS_EPS)
        zhat = y2 * r
        gp = gp_ref[...]
        err = x_ref[...] + zhat * gp - t_ref[...]
        loss_ref[...] += jnp.sum(err * err) * (0.5 * inv_d)
        g = err * inv_d
        g_ref[...] = g
        gpost_ref[...] += jnp.sum(g * zhat, axis=0, keepdims=True)
        a = g * gp
        dy2 = (r * (a - zhat * jnp.mean(a * zhat, axis=-1, keepdims=True))).astype(BF16)
        gw_ref[...] += _dot_tn(ycat[...], dy2)
        dycat = _dot_nt(dy2, w_ref[...])
        dya, dyb, dyc = dycat[:, 0:A_W], dycat[:, A_W:A_W + B_W], dycat[:, A_W + B_W:]
        doa, dob, doc = dya * sa, dyb * sb, dyc * sc
        doa_ref[...] = doa.astype(BF16)
        doc_ref[...] = doc.astype(BF16)
        dga_ref[...] = (dya * oa * dsa).astype(BF16)
        dgb_ref[...] = (dyb * ob * dsb).astype(BF16)
        dgc_ref[...] = (dyc * oc * dsc).astype(BF16)
        dl_a = _dot_split(doa * oa, _head_sum_matrix(A_W), 2)
        dla_ref[...] = dl_a
        dlc_ref[...] = _dot_split(doc * oc, _head_sum_matrix(C_W), 2)
        gsink_ref[...] += jnp.sum(jnp.exp(sink_ref[...] - lsea_ref[...]) * dl_a, axis=0, keepdims=True)
        _stage_write(st_do, dob)
        _stage_write(st_d, _dot_split(dob * ob, _head_sum_matrix(B_W), 2))
        for k, dil in enumerate(B_DILS):
            _to_residues(st_do, dob_refs[k], dil)
            _to_residues(st_l, lsec_refs[k], dil)
            _to_residues(st_d, dlb_refs[k], dil)

    row = lambda w: pl.BlockSpec((tm, w), lambda i: (i, 0))
    full = lambda shape: pl.BlockSpec(shape, lambda i: (0,) * len(shape))
    res_specs = lambda w: [_residue_spec(d, tm, w) for d in B_DILS]
    res_shapes = lambda w, dt: [jax.ShapeDtypeStruct((d, seq // d, w), dt) for d in B_DILS]
    ins = [x, target, post_norm, w_out, sink_row, oa, lse_a, ga, *ob_list, *lseb_list, gb, oc, gc]
    in_specs = ([row(D_MODEL), row(D_MODEL), full((1, D_MODEL)), full((D_MODEL, D_MODEL)), full((1, LANES)),
                 row(A_W), row(LANES), row(A_W)] + res_specs(B_W) + res_specs(LANES) + [row(B_W), row(C_W), row(C_W)])
    out_shape = ([jax.ShapeDtypeStruct((seq, D_MODEL), F32), jax.ShapeDtypeStruct((seq, A_W), BF16),
                  jax.ShapeDtypeStruct((seq, LANES), F32), jax.ShapeDtypeStruct((seq, A_W), BF16)]
                 + res_shapes(B_W, BF16) + res_shapes(LANES, F32) + res_shapes(LANES, F32)
                 + [jax.ShapeDtypeStruct((seq, B_W), BF16), jax.ShapeDtypeStruct((seq, C_W), BF16),
                    jax.ShapeDtypeStruct((seq, LANES), F32), jax.ShapeDtypeStruct((seq, C_W), BF16),
                    jax.ShapeDtypeStruct((D_MODEL, D_MODEL), F32), jax.ShapeDtypeStruct((1, D_MODEL), F32),
                    jax.ShapeDtypeStruct((1, LANES), F32), jax.ShapeDtypeStruct((1, LANES), F32)])
    out_specs = ([row(D_MODEL), row(A_W), row(LANES), row(A_W)] + res_specs(B_W) + res_specs(LANES) + res_specs(LANES)
                 + [row(B_W), row(C_W), row(LANES), row(C_W),
                    full((D_MODEL, D_MODEL)), full((1, D_MODEL)), full((1, LANES)), full((1, LANES))])
    scratch = ([pltpu.VMEM((tm, D_MODEL), BF16)] + [_stage(tm, B_W)] * (nd - 1) + [_stage(tm, LANES)] * (nd - 1)
               + [_stage(tm, B_W), _stage(tm, LANES), _stage(tm, LANES)])
    res = pl.pallas_call(
        body, name="post", grid=(seq // tm,), in_specs=in_specs, out_specs=out_specs, out_shape=out_shape,
        scratch_shapes=scratch,
        compiler_params=pltpu.CompilerParams(dimension_semantics=("arbitrary",)),
    )(*ins)
    out = dict(g=res[0], doa=res[1], dl_a=res[2], dga=res[3], dob=res[4:4 + nd], lse_b=res[4 + nd:4 + 2 * nd],
               dl_b=res[4 + 2 * nd:4 + 3 * nd])
    rest = res[4 + 3 * nd:]
    out.update(dgb=rest[0], doc=rest[1], dl_c=rest[2], dgc=rest[3], gw_out=rest[4], gpost=rest[5], gsink=rest[6],
               loss=rest[7])
    return out


def _grad_w_in(ut, nat, res):
    seq = ut.shape[1]
    tm = min(512, seq)
    nd = len(B_DILS)
    nat_list = [nat[n] for n in _NATURAL]
    res_list = [a for n in _DILATED for a in res[n]]
    rope = _rope_tables(seq, tm)

    def body(rl_ref, rb_ref, ut_ref, *refs):
        nat_refs = dict(zip(_NATURAL, refs[:len(_NATURAL)]))
        refs = refs[len(_NATURAL):]
        res_refs = {n: refs[nd * k:nd * (k + 1)] for k, n in enumerate(_DILATED)}
        refs = refs[nd * len(_DILATED):]
        dproj_ref, gw_ref = refs[:2]
        bufs = {n: refs[2 + (nd - 1) * k:2 + (nd - 1) * (k + 1)] for k, n in enumerate(_DILATED)}

        @pl.when(pl.program_id(0) == 0)
        def _():
            gw_ref[...] = jnp.zeros_like(gw_ref)

        for n in _DILATED:
            for k in range(1, nd):
                _from_residues(res_refs[n][k], bufs[n][k - 1], B_DILS[k])
        c, sm, sp = _rope_coeffs(rl_ref, rb_ref)
        sm, sp = -sm, -sp
        for blk, (name, off, roped, scaled) in enumerate(_PROJ_LAYOUT):
            lanes = slice(off, off + LANES)
            if name in nat_refs:
                piece = nat_refs[name][:, lanes].astype(F32)
            else:
                piece = res_refs[name][0][0, :, lanes].astype(F32)
                for buf in bufs[name]:
                    piece = piece + buf[off // LANES]
            if roped:
                piece = _rope(piece, c, sm, sp)
            if scaled:
                piece = piece * SCALE
            dproj_ref[:, blk * LANES:(blk + 1) * LANES] = piece.astype(BF16)
        for j in range(N_CHIPS):
            gw_ref[j] += _dot(ut_ref[...], dproj_ref[:, j * SHARD_IN:(j + 1) * SHARD_IN])

    row = lambda w: pl.BlockSpec((tm, w), lambda i: (i, 0))
    in_specs = ([pl.BlockSpec(rope[0].shape, lambda i: (0, 0)), pl.BlockSpec((8, 2 * LANES), lambda i: (i, 0)),
                 pl.BlockSpec((D_MODEL, tm), lambda i: (0, i))]
                + [row(a.shape[1]) for a in nat_list]
                + [_residue_spec(d, tm, B_W) for _ in _DILATED for d in B_DILS])
    return pl.pallas_call(
        body, name="grad_w_in", grid=(seq // tm,), in_specs=in_specs,
        out_specs=[row(D_IN), pl.BlockSpec((N_CHIPS, D_MODEL, SHARD_IN), lambda i: (0, 0, 0))],
        out_shape=[jax.ShapeDtypeStruct((seq, D_IN), BF16), jax.ShapeDtypeStruct((N_CHIPS, D_MODEL, SHARD_IN), F32)],
        scratch_shapes=[_stage(tm, B_W)] * ((nd - 1) * len(_DILATED)),
        compiler_params=pltpu.CompilerParams(dimension_semantics=("arbitrary",)),
    )(*rope, ut, *nat_list, *res_list)


def _input_grad(x, g, pre_norm, w_in_g, dproj, gx_prev, span, host, name):
    seq = x.shape[0]
    tm = seq // 16
    first_block, steps = span
    n_host_in = len(host["ins"]) if host else 0
    n_host_out = len(host["outs"]) if host else 0

    def body(*refs):
        x_ref, g_ref, gp_ref, w_ref, dp_ref = refs[:5]
        refs = refs[5 + (gx_prev is not None):]
        host_in, refs = refs[:n_host_in], refs[n_host_in:]
        gx_ref, gpre_ref = refs[:2]
        host_out, sems = refs[2:2 + n_host_out], refs[2 + n_host_out:]
        step = pl.program_id(0)

        @pl.when(step == 0)
        def _():
            gpre_ref[...] = jnp.zeros_like(gpre_ref)
            if host:
                host["start"](host_in, host_out, *sems)

        du = None
        for j in range(N_CHIPS):
            term = _dot_nt(dp_ref[:, j * SHARD_IN:(j + 1) * SHARD_IN], w_ref[j])
            du = term if du is None else du + term
        xv = x_ref[...]
        r = lax.rsqrt(jnp.mean(xv * xv, axis=-1, keepdims=True) + RMS_EPS)
        xhat = xv * r
        gpre_ref[...] += jnp.sum(du * xhat, axis=0, keepdims=True)
        a = du * gp_ref[...]
        gx_ref[...] = g_ref[...] + r * (a - xhat * jnp.mean(a * xhat, axis=-1, keepdims=True))

        if host:
            @pl.when(step == steps - 1)
            def _():
                host["finish"](host_in, host_out, *sems)

    row = lambda w: pl.BlockSpec((tm, w), lambda i: (first_block + i, 0))
    full = lambda a: pl.BlockSpec(a.shape, lambda i: (0,) * a.ndim)
    any_spec = pl.BlockSpec(memory_space=pl.ANY)
    ins = [x, g, pre_norm, w_in_g, dproj]
    in_specs = [row(D_MODEL), row(D_MODEL), full(pre_norm), full(w_in_g), row(D_IN)]
    aliases = {}
    if gx_prev is not None:
        aliases[len(ins)] = 0
        ins.append(gx_prev)
        in_specs.append(any_spec)
    out_shape = [jax.ShapeDtypeStruct((seq, D_MODEL), F32), jax.ShapeDtypeStruct((1, D_MODEL), F32)]
    out_specs = [row(D_MODEL), pl.BlockSpec((1, D_MODEL), lambda i: (0, 0))]
    scratch = []
    if host:
        ins += list(host["ins"])
        in_specs += [any_spec] * n_host_in
        out_shape += list(host["outs"])
        out_specs += [any_spec] * n_host_out
        scratch = list(host["sems"])
    return pl.pallas_call(
        body, name=name, grid=(steps,), in_specs=in_specs, out_specs=out_specs, out_shape=out_shape,
        input_output_aliases=aliases, scratch_shapes=scratch,
        compiler_params=pltpu.CompilerParams(dimension_semantics=("arbitrary",)),
    )(*ins)


def _start_finish(build):
    def start(*refs):
        for cp in build(*refs):
            cp.start()

    def finish(*refs):
        for cp in build(*refs):
            cp.wait()

    return dict(start=start, finish=finish)


def _pair_exchange(grads):
    n = len(grads)

    def build(srcs, outs, send_sems, recv_sems):
        x, y, c = lax.axis_index("x"), lax.axis_index("y"), lax.axis_index("c")
        copies = []
        for t in range(n):
            rows = grads[t].shape[1] // 2
            copies.append(pltpu.make_async_remote_copy(
                src_ref=srcs[t].at[:, pl.ds((1 - c) * rows, rows)], dst_ref=outs[t],
                send_sem=send_sems.at[t], recv_sem=recv_sems.at[t], device_id=(x, y, 1 - c), device_id_type=MESH))
        return copies

    return dict(ins=list(grads), **_start_finish(build),
                outs=[jax.ShapeDtypeStruct((g.shape[0], g.shape[1] // 2, g.shape[2]), g.dtype) for g in grads],
                sems=[pltpu.SemaphoreType.DMA((n,)), pltpu.SemaphoreType.DMA((n,))])


def _pair_add(core, own, got):
    nchip, rows2, width = own.shape
    rows = rows2 // 2
    tr = min(512, rows)
    nb = rows // tr

    def body(core_ref, own_ref, got_ref, out_ref):
        out_ref[...] = (own_ref[...] + got_ref[...]).astype(BF16)

    grid_spec = pltpu.PrefetchScalarGridSpec(
        num_scalar_prefetch=1, grid=(nchip, nb),
        in_specs=[pl.BlockSpec((None, tr, width), lambda k, i, core_ref: (k, core_ref[0] * nb + i, 0)),
                  pl.BlockSpec((None, tr, width), lambda k, i, core_ref: (k, i, 0))],
        out_specs=pl.BlockSpec((None, tr, width), lambda k, i, core_ref: (k, i, 0)))
    return pl.pallas_call(
        body, name=f"pair_add_{width}", grid_spec=grid_spec,
        out_shape=jax.ShapeDtypeStruct((nchip, rows, width), BF16),
    )(core, own, got)


def _chip_exchange(parts):
    n = len(parts)

    def build(srcs, outs, send_sems, recv_sems, local_sems):
        x, y, c = lax.axis_index("x"), lax.axis_index("y"), lax.axis_index("c")
        my_chip = 2 * x + y
        chips = [(1 - x, y), (x, 1 - y), (1 - x, 1 - y)]
        copies = [pltpu.make_async_copy(srcs[t].at[my_chip], outs[t].at[my_chip], local_sems.at[t]) for t in range(n)]
        for j, (cx, cy) in enumerate(chips):
            for t in range(n):
                k = n * j + t
                copies.append(pltpu.make_async_remote_copy(
                    src_ref=srcs[t].at[2 * cx + cy], dst_ref=outs[t].at[my_chip], send_sem=send_sems.at[k],
                    recv_sem=recv_sems.at[k], device_id=(cx, cy, c), device_id_type=MESH))
        return copies

    return dict(ins=list(parts), **_start_finish(build), outs=[jax.ShapeDtypeStruct(p.shape, p.dtype) for p in parts],
                sems=[pltpu.SemaphoreType.DMA((3 * n,)), pltpu.SemaphoreType.DMA((3 * n,)),
                      pltpu.SemaphoreType.DMA((n,))])


def _slot_sum(slots, name, core=None):
    ns, rows, width = slots.shape
    tr = min(512, rows)

    def body(*refs):
        in_ref, out_ref = refs[-2:]
        acc = in_ref[0].astype(F32)
        for s in range(1, ns):
            acc = acc + in_ref[s].astype(F32)
        out_ref[...] = acc

    if core is None:
        return pl.pallas_call(
            body, name=name, grid=(rows // tr,),
            in_specs=[pl.BlockSpec((ns, tr, width), lambda i: (0, i, 0))],
            out_specs=pl.BlockSpec((tr, width), lambda i: (i, 0)),
            out_shape=jax.ShapeDtypeStruct((rows, width), F32),
        )(slots)
    grid_spec = pltpu.PrefetchScalarGridSpec(
        num_scalar_prefetch=1, grid=(rows // tr,),
        in_specs=[pl.BlockSpec((ns, tr, width), lambda i, core_ref: (0, i, 0))],
        out_specs=pl.BlockSpec((None, tr, width), lambda i, core_ref: (core_ref[0], i, 0)))
    return pl.pallas_call(
        body, name=name, grid_spec=grid_spec, out_shape=jax.ShapeDtypeStruct((2, rows, width), F32),
    )(core, slots)


def _pair_gather(bufs, small):
    n = len(bufs)

    def body(*refs):
        small_ref, outs, small_out = refs[n], refs[n + 1:2 * n + 1], refs[2 * n + 1]
        send_sems, recv_sems, local_sem = refs[2 * n + 2:]
        x, y, c = lax.axis_index("x"), lax.axis_index("y"), lax.axis_index("c")
        me = 4 * x + 2 * y + c
        chips = [(1 - x, y), (x, 1 - y), (1 - x, 1 - y)]
        mine = pltpu.make_async_copy(small_ref, small_out.at[me], local_sem)
        mine.start()
        copies = [pltpu.make_async_remote_copy(
            src_ref=outs[t].at[c], dst_ref=outs[t].at[c], send_sem=send_sems.at[t], recv_sem=recv_sems.at[t],
            device_id=(x, y, 1 - c), device_id_type=MESH) for t in range(n)]
        peers = [(x, y, 1 - c)] + [(cx, cy, cc) for (cx, cy) in chips for cc in (c, 1 - c)]
        for j, peer in enumerate(peers):
            copies.append(pltpu.make_async_remote_copy(
                src_ref=small_ref, dst_ref=small_out.at[me], send_sem=send_sems.at[n + j],
                recv_sem=recv_sems.at[n + j], device_id=peer, device_id_type=MESH))
        for cp in copies:
            cp.start()
        for cp in copies:
            cp.wait()
        mine.wait()

    any_spec = pl.BlockSpec(memory_space=pl.ANY)
    res = pl.pallas_call(
        body, name="pair_gather",
        out_shape=[jax.ShapeDtypeStruct(b.shape, b.dtype) for b in bufs]
        + [jax.ShapeDtypeStruct((8,) + small.shape, small.dtype)],
        in_specs=[any_spec] * (n + 1), out_specs=[any_spec] * (n + 1),
        input_output_aliases={t: t for t in range(n)},
        scratch_shapes=[pltpu.SemaphoreType.DMA((n + 7,)), pltpu.SemaphoreType.DMA((n + 7,)),
                        pltpu.SemaphoreType.DMA],
    )(*bufs, small)
    return [r.reshape(2 * b.shape[1], b.shape[2]) for r, b in zip(res[:n], bufs)], res[n]


def _adamw(w, g, m, v, name):
    rows, width = w.shape
    tr = min(256, rows)
    c1 = 1.0 / (1.0 - ADAM_B1 ** ADAM_STEP)
    c2 = 1.0 / (1.0 - ADAM_B2 ** ADAM_STEP)

    def body(w_ref, g_ref, m_ref, v_ref, d_ref, nm_ref, nv_ref):
        gv = g_ref[...]
        nm = ADAM_B1 * m_ref[...] + (1.0 - ADAM_B1) * gv
        nv = ADAM_B2 * v_ref[...] + (1.0 - ADAM_B2) * (gv * gv)
        nm_ref[...] = nm
        nv_ref[...] = nv
        d_ref[...] = -ADAM_LR * ((nm * c1) / (jnp.sqrt(nv * c2) + ADAM_EPS) + ADAM_WD * w_ref[...])

    spec = pl.BlockSpec((tr, width), lambda i: (i, 0))
    return pl.pallas_call(
        body, name=name, grid=(rows // tr,), in_specs=[spec] * 4, out_specs=[spec] * 3,
        out_shape=[jax.ShapeDtypeStruct(w.shape, F32)] * 3,
    )(w, g, m, v)


def _local_step(x, mem, target, pre_norm, sink_a, mem_norm, post_norm, w_in_g, w_out, w_mkv, late_gather=None):
    pr = _pre_proj(x, pre_norm, w_in_g, late_gather)
    if late_gather is not None:
        w_out, w_mkv = (g.reshape(D_MODEL, g.shape[-1]) for g in pr["hosted"])
    mk, mv = _mem_kv(mem, mem_norm, w_mkv)
    sink = sink_a.reshape(-1)
    qa, ka, va = pr["qa"][None], pr["ka"][None], pr["va"][None]
    oa, lse_a = _band_fwd(qa, ka, va, sink, max_dist=A_WINDOW - 1, name="swa_fwd")
    ob_list, lseb_list = [], []
    for k, (win, dil) in enumerate(B_CONFIGS):
        o_i, l_i = _band_fwd(pr["qb"][k], pr["kb"][k], pr["vb"][k], None, max_dist=win // dil, name=f"dil{dil}_fwd")
        ob_list.append(o_i)
        lseb_list.append(l_i)
    oc, lse_c = _mem_attn_fwd(pr["qc"], mk, mv)
    sink_row = jnp.pad(sink, (0, LANES - sink.shape[0])).reshape(1, LANES)
    po = _post(x, target, post_norm, w_out, sink_row, oa[0], lse_a[0], pr["ga"], ob_list, lseb_list, pr["gb"], oc,
               pr["gc"])
    dqc, dmk, dmv = _mem_attn_bwd(pr["qc"], mk, mv, po["doc"], lse_c, po["dl_c"])
    dqa, dka, dva = _band_bwd(qa, ka, va, po["doa"][None], lse_a, po["dl_a"][None], max_dist=A_WINDOW - 1,
                              name="swa_bwd")
    res = dict(qb=[], kb=[], vb=[])
    for k, (win, dil) in enumerate(B_CONFIGS):
        dq_i, dk_i, dv_i = _band_bwd(pr["qb"][k], pr["kb"][k], pr["vb"][k], po["dob"][k], po["lse_b"][k],
                                     po["dl_b"][k], max_dist=win // dil, name=f"dil{dil}_bwd")
        res["qb"].append(dq_i)
        res["kb"].append(dk_i)
        res["vb"].append(dv_i)
    nat = dict(qa=dqa[0], ka=dka[0], va=dva[0], ga=po["dga"], gb=po["dgb"], qc=dqc, gc=po["dgc"])
    dproj, gw_in = _grad_w_in(pr["ut"], nat, res)
    gw_mkv, gmem = _mem_kv_bwd(mem, mem_norm, w_mkv, dmk, dmv)
    gsink = -po["gsink"][0, :sink.shape[0]]
    return dict(loss=po["loss"], g=po["g"], dproj=dproj, gw_in=gw_in, gw_out=po["gw_out"], gw_mkv=gw_mkv,
                gpost=po["gpost"], gmem=gmem, gsink=gsink)


def kernel(x, mem, pre_norm, w_in, sink_a, mem_norm, w_mem_kv, w_out, post_norm, loss_target, m_pre_norm, m_w_in, m_sink_a, m_mem_norm, m_w_mem_kv, m_w_out, m_post_norm, v_pre_norm, v_w_in, v_sink_a, v_mem_norm, v_w_mem_kv, v_w_out, v_post_norm):
    (w_in_g,) = _run_exchange(_gather_exchange([w_in[0].astype(BF16)]), "gather_w_in")
    late_gather = _gather_exchange([w_out[0].astype(BF16), w_mem_kv[0].astype(BF16)])
    loc = _local_step(x[0], mem[0], loss_target[0], pre_norm, sink_a, mem_norm, post_norm,
                      w_in_g.reshape(N_CHIPS, D_MODEL, SHARD_IN), None, None, late_gather)
    big = [loc["gw_in"], loc["gw_out"].reshape(N_CHIPS, D_MODEL // N_CHIPS, D_MODEL),
           loc["gw_mkv"].reshape(N_CHIPS, D_MODEL // N_CHIPS, 2 * C_W)]
    core = lax.axis_index("c").astype(jnp.int32).reshape(1)
    w_in_full = w_in_g.reshape(N_CHIPS, D_MODEL, SHARD_IN)
    step_in = (x[0], loc["g"], pre_norm, w_in_full, loc["dproj"])
    gx_a, gpre_a, *got = _input_grad(*step_in, None, (0, 3), _pair_exchange(big), "input_grad_a")
    parts = [_pair_add(core, own, g) for own, g in zip(big, got)]
    gx_b, gpre_b, *slots = _input_grad(*step_in, gx_a, (3, 10), _chip_exchange(parts), "input_grad_b")
    grad_x, gpre_c = _input_grad(*step_in, gx_b, (13, 3), None, "input_grad_c")
    halves = [_slot_sum(s, name=f"chip_sum_{s.shape[2]}", core=core) for s in slots]
    widen = lambda a: jnp.pad(a.reshape(1, -1), ((0, 0), (0, D_MODEL - a.size)))
    small = jnp.concatenate([gpre_a, loc["gpost"], loc["gmem"], widen(loc["gsink"]), widen(loc["loss"]), gpre_b,
                             gpre_c, jnp.zeros((1, D_MODEL), F32)], axis=0)
    (g_in, g_out, g_mkv), small_slots = _pair_gather(halves, small)
    small_sum = _slot_sum(small_slots, name="device_sum")
    g_pre, g_post, g_mem = small_sum[0:1] + small_sum[5:6] + small_sum[6:7], small_sum[1:2], small_sum[2:3]
    g_sink = small_sum[3:4, :sink_a.shape[1]]
    loss = small_sum[4, 0]

    d_in, nm_in, nv_in = _adamw(w_in[0], g_in, m_w_in[0], v_w_in[0], "adamw_in")
    d_out, nm_out, nv_out = _adamw(w_out[0], g_out, m_w_out[0], v_w_out[0], "adamw_out")
    d_mkv, nm_mkv, nv_mkv = _adamw(w_mem_kv[0], g_mkv, m_w_mem_kv[0], v_w_mem_kv[0], "adamw_mkv")
    pad6 = lambda a: jnp.pad(a, ((0, 0), (0, D_MODEL - a.shape[1])))
    stack = lambda a, b, c_, d_: jnp.concatenate([a, b, c_, pad6(d_), jnp.zeros((4, D_MODEL), F32)], axis=0)
    d_s, nm_s, nv_s = _adamw(stack(pre_norm, post_norm, mem_norm, sink_a),
                             jnp.concatenate([g_pre, small_sum[1:]], axis=0),
                             stack(m_pre_norm, m_post_norm, m_mem_norm, m_sink_a),
                             stack(v_pre_norm, v_post_norm, v_mem_norm, v_sink_a), "adamw_small")
    ns_ = sink_a.shape[1]
    unpack = lambda a: (a[0:1], a[3:4, :ns_], a[2:3], a[1:2])
    d_pre, d_sink, d_mem, d_post = unpack(d_s)
    nm_pre, nm_sink, nm_mem, nm_post = unpack(nm_s)
    nv_pre, nv_sink, nv_mem, nv_post = unpack(nv_s)
    lead = lambda a: a[None]
    return (loss, lead(grad_x),
            g_pre, lead(g_in), g_sink, g_mem, lead(g_mkv), lead(g_out), g_post,
            d_pre, lead(d_in), d_sink, d_mem, lead(d_mkv), lead(d_out), d_post,
            nm_pre, lead(nm_in), nm_sink, nm_mem, lead(nm_mkv), lead(nm_out), nm_post,
            nv_pre, lead(nv_in), nv_sink, nv_mem, lead(nv_mkv), lead(nv_out), nv_post)
```

```python
import numpy as np
import jax
import jax.numpy as jnp
from jax import lax
from jax.experimental import pallas as pl
from jax.experimental.pallas import tpu as pltpu

F32 = jnp.float32
BF16 = jnp.bfloat16

D_MODEL = 1024
HEAD_DIM = 64
LANES = 128
BLOCK = 128
A_W, A_KV_W, B_W, C_W = 384, 128, 384, 256
N_MEM = 256
D_IN = 3072
N_CHIPS = 4
SHARD_IN = D_IN // N_CHIPS
B_CONFIGS = ((128, 1), (512, 4), (2048, 16))
B_DILS = tuple(d for _, d in B_CONFIGS)
A_WINDOW = 128
RMS_EPS = 1e-6
ROPE_THETA = 500000.0
SCALE = HEAD_DIM ** -0.5
NEG = -1e30
ADAM_LR, ADAM_B1, ADAM_B2, ADAM_EPS, ADAM_WD, ADAM_STEP = 0.001, 0.9, 0.999, 1e-08, 0.01, 10

NT = (((1,), (1,)), ((), ()))
TN = (((0,), (0,)), ((), ()))
MESH = pl.DeviceIdType.MESH

_PROJ_LAYOUT = (
    [("qa", 128 * i, True, True) for i in range(3)] + [("ka", 0, True, False), ("va", 0, False, False)]
    + [("ga", 128 * i, False, False) for i in range(3)]
    + [("qb", 128 * i, True, True) for i in range(3)] + [("kb", 128 * i, True, False) for i in range(3)]
    + [("vb", 128 * i, False, False) for i in range(3)] + [("gb", 128 * i, False, False) for i in range(3)]
    + [("qc", 128 * i, False, True) for i in range(2)] + [("gc", 128 * i, False, False) for i in range(2)]
)
_PROJ_WIDTH = dict(qa=A_W, ka=A_KV_W, va=A_KV_W, ga=A_W, qb=B_W, kb=B_W, vb=B_W, gb=B_W, qc=C_W, gc=C_W)
_NATURAL = ("qa", "ka", "va", "ga", "gb", "qc", "gc")
_DILATED = ("qb", "kb", "vb")


def _dot(a, b):
    return jnp.dot(a, b, preferred_element_type=F32)


def _dot_nt(a, b):
    return lax.dot_general(a, b, NT, preferred_element_type=F32)


def _dot_tn(a, b):
    return lax.dot_general(a, b, TN, preferred_element_type=F32)


def _half_masks(rows):
    lane = lax.broadcasted_iota(jnp.int32, (rows, LANES), 1)
    return lane < HEAD_DIM, lane >= HEAD_DIM


def _rope(t, c, sm, sp):
    return t * c + pltpu.roll(t, LANES - 8, 1) * sm + pltpu.roll(t, 8, 1) * sp


def _rope_tables(seq, tm):
    dim = jnp.arange(LANES) % HEAD_DIM
    inv_freq = ROPE_THETA ** (-jnp.arange(0, 16, 2, dtype=F32) / 16)
    freq = jnp.where(dim < 16, inv_freq[dim % 8], 0.0)[None, :]
    local = jnp.arange(tm, dtype=F32)[:, None] * freq
    base = (jnp.arange(seq // tm, dtype=F32) * tm)[:, None] * freq
    both = lambda a: jnp.concatenate([jnp.cos(a), jnp.sin(a)], axis=1)
    return both(local), jnp.repeat(both(base), 8, axis=0)


def _rope_coeffs(local_ref, base_ref):
    cl, sl = local_ref[:, :LANES], local_ref[:, LANES:]
    cb, sb = base_ref[0:1, :LANES], base_ref[0:1, LANES:]
    cos = cb * cl - sb * sl
    sin = sb * cl + cb * sl
    dim = lax.broadcasted_iota(jnp.int32, (1, LANES), 1) % HEAD_DIM
    return cos, jnp.where(dim < 8, -sin, 0.0), jnp.where((dim >= 8) & (dim < 16), sin, 0.0)


def _split3(x):
    a = x.astype(BF16)
    r = x - a.astype(F32)
    b = r.astype(BF16)
    c = (r - b.astype(F32)).astype(BF16)
    return a, b, c


def _rows_to_lanes(x):
    row = lax.broadcasted_iota(jnp.int32, (8, LANES), 0)
    lane = lax.broadcasted_iota(jnp.int32, (8, LANES), 1)
    eye = (row == lane).astype(BF16)
    a, b, c = _split3(x)
    return _dot_nt(eye, a) + _dot_nt(eye, b) + _dot_nt(eye, c)


def _head_sum_matrix(width):
    k = lax.broadcasted_iota(jnp.int32, (width, LANES), 0)
    h = lax.broadcasted_iota(jnp.int32, (width, LANES), 1)
    return (k // HEAD_DIM == h).astype(BF16)


def _head_expand_matrix(width):
    h = lax.broadcasted_iota(jnp.int32, (LANES, width), 0)
    k = lax.broadcasted_iota(jnp.int32, (LANES, width), 1)
    return (k // HEAD_DIM == h).astype(BF16)


def _dot_split(x, mat, terms):
    parts = _split3(x)[:terms]
    out = _dot(parts[0], mat)
    for p in parts[1:]:
        out = out + _dot(p, mat)
    return out


def _per_head(cols, fill=0.0):
    rows = cols[0].shape[0]
    lane = lax.broadcasted_iota(jnp.int32, (rows, LANES), 1)
    out = jnp.full((rows, LANES), fill, F32)
    for h, col in enumerate(cols):
        out = jnp.where(lane == h, col, out)
    return out


def _lane_blocks(width):
    return [slice(p * LANES, (p + 1) * LANES) for p in range(width // LANES)]


def _stage(rows, width):
    return pltpu.VMEM((width // LANES, rows, LANES), F32)


def _stage_write(buf, value):
    for p, lanes in enumerate(_lane_blocks(value.shape[1])):
        buf[p] = value[:, lanes]


def _stage_read(buf):
    return jnp.concatenate([buf[p] for p in range(buf.shape[0])], axis=1) if buf.shape[0] > 1 else buf[0]


def _to_residues(buf, out_ref, dil):
    rows = buf.shape[1] // dil
    for r in range(dil):
        for p in range(buf.shape[0]):
            plane = buf.at[p]
            out_ref[r, :, p * LANES:(p + 1) * LANES] = plane[pl.ds(r, rows, stride=dil), :].astype(out_ref.dtype)


def _from_residues(in_ref, buf, dil):
    rows = buf.shape[1] // dil
    for r in range(dil):
        for p in range(buf.shape[0]):
            plane = buf.at[p]
            plane[pl.ds(r, rows, stride=dil), :] = in_ref[r, :, p * LANES:(p + 1) * LANES].astype(F32)


def _residue_spec(dil, tm, width):
    return pl.BlockSpec((dil, tm // dil, width), lambda i: (0, i, 0))


def _gather_exchange(shards_2d):
    shards = tuple(s.reshape(2, s.shape[0] // 2, s.shape[1]) for s in shards_2d)
    n = len(shards)

    def copies(in_refs, out_refs, send_sems, recv_sems):
        srcs, outs = in_refs[:n], out_refs
        x, y, c = lax.axis_index("x"), lax.axis_index("y"), lax.axis_index("c")
        my_chip = 2 * x + y
        sibling = (x, y, 1 - c)
        chips = [(1 - x, y), (x, 1 - y), (1 - x, 1 - y)]

        def copy(k, src, dst, to):
            return pltpu.make_async_remote_copy(src_ref=src, dst_ref=dst, send_sem=send_sems.at[k],
                                                recv_sem=recv_sems.at[k], device_id=to, device_id_type=MESH)

        first, arrive, passed, sibling_arrive = [], [], [], []
        for j, (cx, cy) in enumerate(chips):
            chip = 2 * cx + cy
            for t in range(n):
                k = n * j + t
                first.append(copy(k, srcs[t].at[c], outs[t].at[my_chip, c], (cx, cy, c)))
                arrive.append(copy(k, srcs[t].at[c], outs[t].at[chip, c], (cx, cy, c)))
                passed.append(copy(n * 3 + k, outs[t].at[chip, c], outs[t].at[chip, c], sibling))
                sibling_arrive.append(copy(n * 3 + k, outs[t].at[chip, 1 - c], outs[t].at[chip, 1 - c], sibling))
        return first, arrive, passed, sibling_arrive

    def start(*refs):
        for cp in copies(*refs)[0]:
            cp.start()

    def finish(*refs):
        first, arrive, passed, sibling_arrive = copies(*refs)
        for got, fwd in zip(arrive, passed):
            got.wait_recv()
            fwd.start()
        for cp in sibling_arrive:
            cp.wait_recv()
        for cp in first + passed:
            cp.wait_send()

    my_chip = 2 * lax.axis_index("x") + lax.axis_index("y")
    landing = [lax.dynamic_update_slice(jnp.zeros((N_CHIPS,) + s.shape, s.dtype), s[None], (my_chip, 0, 0, 0))
               for s in shards]
    return dict(ins=list(shards) + landing, start=start, finish=finish, aliases={n + t: t for t in range(n)},
                outs=[jax.ShapeDtypeStruct((N_CHIPS,) + s.shape, s.dtype) for s in shards],
                sems=[pltpu.SemaphoreType.DMA((6 * n,)), pltpu.SemaphoreType.DMA((6 * n,))])


def _run_exchange(ex, name):
    n_in, n_out = len(ex["ins"]), len(ex["outs"])

    def body(*refs):
        in_refs, out_refs, sems = refs[:n_in], refs[n_in:n_in + n_out], refs[n_in + n_out:]
        ex["start"](in_refs, out_refs, *sems)
        ex["finish"](in_refs, out_refs, *sems)

    any_spec = pl.BlockSpec(memory_space=pl.ANY)
    return pl.pallas_call(
        body, name=name, out_shape=ex["outs"], in_specs=[any_spec] * n_in, out_specs=[any_spec] * n_out,
        input_output_aliases=ex.get("aliases", {}), scratch_shapes=ex["sems"],
    )(*ex["ins"])


def _mem_kv(mem, mem_norm, w_mkv):
    def body(mem_ref, g_ref, w_ref, mk_ref, mv_ref):
        m = mem_ref[...]
        r = lax.rsqrt(jnp.mean(m * m, axis=-1, keepdims=True) + RMS_EPS)
        mn = (m * r * g_ref[...]).astype(BF16)
        kv = _dot(mn, w_ref[...])
        mk_ref[...] = kv[:, :C_W].astype(BF16)
        mv_ref[...] = kv[:, C_W:].astype(BF16)

    return pl.pallas_call(
        body, name="mem_kv",
        out_shape=[jax.ShapeDtypeStruct((N_MEM, C_W), BF16)] * 2,
    )(mem, mem_norm, w_mkv)


def _mem_kv_bwd(mem, mem_norm, w_mkv, dmk, dmv):
    def body(mem_ref, g_ref, w_ref, dmk_ref, dmv_ref, gw_ref, gn_ref):
        m = mem_ref[...]
        r = lax.rsqrt(jnp.mean(m * m, axis=-1, keepdims=True) + RMS_EPS)
        mhat = m * r
        mn = (mhat * g_ref[...]).astype(BF16)
        dkv = jnp.concatenate([dmk_ref[...], dmv_ref[...]], axis=1).astype(BF16)
        gw_ref[...] = _dot_tn(mn, dkv)
        dmn = _dot_nt(dkv, w_ref[...])
        gn_ref[...] = jnp.sum(dmn * mhat, axis=0, keepdims=True)

    return pl.pallas_call(
        body, name="mem_kv_bwd",
        out_shape=[jax.ShapeDtypeStruct((D_MODEL, 2 * C_W), F32), jax.ShapeDtypeStruct((1, D_MODEL), F32)],
    )(mem, mem_norm, w_mkv, dmk, dmv)


def _pre_norm(x, pre_norm, host=None):
    seq = x.shape[0]
    tm = min(512, seq)
    n_host_in = len(host["ins"]) if host else 0
    n_host_out = len(host["outs"]) if host else 0

    def body(x_ref, g_ref, *refs):
        host_in, (u_ref, ut_ref), refs = refs[:n_host_in], refs[n_host_in:n_host_in + 2], refs[n_host_in + 2:]
        host_out, sems = refs[:n_host_out], refs[n_host_out:]
        if host:
            @pl.when(pl.program_id(0) == 0)
            def _():
                host["start"](host_in, host_out, *sems)

        xv = x_ref[...]
        r = lax.rsqrt(jnp.mean(xv * xv, axis=-1, keepdims=True) + RMS_EPS)
        u = xv * r * g_ref[...]
        u_ref[...] = u.astype(BF16)
        ut_ref[...] = u.T.astype(BF16)
        if host:
            @pl.when(pl.program_id(0) == seq // tm - 1)
            def _():
                host["finish"](host_in, host_out, *sems)

    any_spec = pl.BlockSpec(memory_space=pl.ANY)
    ins = [x, pre_norm]
    in_specs = [pl.BlockSpec((tm, D_MODEL), lambda i: (i, 0)), pl.BlockSpec(pre_norm.shape, lambda i: (0, 0))]
    out_shape = [jax.ShapeDtypeStruct((seq, D_MODEL), BF16), jax.ShapeDtypeStruct((D_MODEL, seq), BF16)]
    out_specs = [pl.BlockSpec((tm, D_MODEL), lambda i: (i, 0)), pl.BlockSpec((D_MODEL, tm), lambda i: (0, i))]
    aliases, scratch = {}, []
    if host:
        aliases = {len(ins) + k: 2 + v for k, v in host.get("aliases", {}).items()}
        ins += list(host["ins"])
        in_specs += [any_spec] * n_host_in
        out_shape += list(host["outs"])
        out_specs += [any_spec] * n_host_out
        scratch = list(host["sems"])
    res = pl.pallas_call(
        body, name="pre_norm", grid=(seq // tm,), in_specs=in_specs, out_specs=out_specs, out_shape=out_shape,
        input_output_aliases=aliases, scratch_shapes=scratch,
        compiler_params=pltpu.CompilerParams(dimension_semantics=("arbitrary",)),
    )(*ins)
    return res[0], res[1], res[2:]


def _pre_proj(u, w_in_g, host=None):
    seq = u.shape[0]
    tm = min(512, seq)
    n_nat, n_dil = len(_NATURAL), len(_DILATED) * len(B_DILS)
    rope = _rope_tables(seq, tm)

    n_host_in = len(host["ins"]) if host else 0
    n_host_out = len(host["outs"]) if host else 0
    n_own_out = n_nat + n_dil

    def body(u_ref, w_ref, rl_ref, rb_ref, *refs):
        host_in, refs = refs[:n_host_in], refs[n_host_in:]
        nat = dict(zip(_NATURAL, refs[:n_nat]))
        res = {n: refs[n_nat + len(B_DILS) * k:n_nat + len(B_DILS) * (k + 1)] for k, n in enumerate(_DILATED)}
        host_out = refs[n_own_out:n_own_out + n_host_out]
        bufs = dict(zip(_DILATED, refs[n_own_out + n_host_out:]))
        sems = refs[n_own_out + n_host_out + len(_DILATED):]
        if host:
            @pl.when(pl.program_id(0) == 0)
            def _():
                host["start"](host_in, host_out, *sems)

        ub = u_ref[...]
        c, sm, sp = _rope_coeffs(rl_ref, rb_ref)
        for j in range(N_CHIPS):
            pj = _dot(ub, w_ref[j])
            for b in range(SHARD_IN // LANES):
                name, off, roped, scaled = _PROJ_LAYOUT[(SHARD_IN // LANES) * j + b]
                piece = pj[:, LANES * b:LANES * (b + 1)]
                if roped:
                    piece = _rope(piece, c, sm, sp)
                if scaled:
                    piece = piece * SCALE
                if name in bufs:
                    bufs[name][off // LANES] = piece
                else:
                    nat[name][:, off:off + LANES] = piece.astype(BF16)
        for name in _DILATED:
            for ref, dil in zip(res[name], B_DILS):
                _to_residues(bufs[name], ref, dil)
        if host:
            @pl.when(pl.program_id(0) == seq // tm - 1)
            def _():
                host["finish"](host_in, host_out, *sems)

    row = lambda w: pl.BlockSpec((tm, w), lambda i: (i, 0))
    full = lambda a: pl.BlockSpec(a.shape, lambda i: (0,) * a.ndim)
    any_spec = pl.BlockSpec(memory_space=pl.ANY)
    out_shape = [jax.ShapeDtypeStruct((seq, _PROJ_WIDTH[n]), BF16) for n in _NATURAL]
    out_specs = [row(_PROJ_WIDTH[n]) for n in _NATURAL]
    for n in _DILATED:
        for dil in B_DILS:
            out_shape.append(jax.ShapeDtypeStruct((dil, seq // dil, B_W), BF16))
            out_specs.append(_residue_spec(dil, tm, B_W))
    ins = [u, w_in_g, *rope]
    in_specs = [row(D_MODEL), full(w_in_g), full(rope[0]), pl.BlockSpec((8, 2 * LANES), lambda i: (i, 0))]
    scratch = [_stage(tm, B_W)] * len(_DILATED)
    aliases = {}
    if host:
        aliases = {len(ins) + k: n_own_out + v for k, v in host.get("aliases", {}).items()}
        ins += list(host["ins"])
        in_specs += [any_spec] * n_host_in
        out_shape += list(host["outs"])
        out_specs += [any_spec] * n_host_out
        scratch += list(host["sems"])
    res = pl.pallas_call(
        body, name="pre_proj", grid=(seq // tm,), in_specs=in_specs, out_specs=out_specs, out_shape=out_shape,
        input_output_aliases=aliases, scratch_shapes=scratch,
        compiler_params=pltpu.CompilerParams(dimension_semantics=("arbitrary",)),
    )(*ins)
    out = dict(zip(_NATURAL, res[:n_nat]))
    for k, n in enumerate(_DILATED):
        out[n] = res[n_nat + len(B_DILS) * k:n_nat + len(B_DILS) * (k + 1)]
    out["hosted"] = res[n_own_out:]
    return out


def _band_bias(max_dist, transposed):
    i = np.arange(BLOCK)[:, None]
    j = np.arange(BLOCK)[None, :]
    if transposed:
        same = i <= j
        other = (j + BLOCK - i) <= max_dist
        vis = np.concatenate([same, other], axis=1)
    else:
        prev = (i + BLOCK - j) <= max_dist
        same = j <= i
        vis = np.concatenate([prev, same], axis=1)
    return jnp.asarray(np.where(vis, 0.0, NEG).astype(np.float32))


def _kv_place(h, gqa):
    return (0, h // 3) if gqa else (h // 2, h % 2)


def _band_fwd(q, k, v, sink, *, max_dist, name):
    dil, length, wq = q.shape
    wk = k.shape[2]
    gqa = wk != wq
    tq = min(512, length)
    ns, nt = tq // BLOCK, length // tq
    npair = wq // LANES
    bias = _band_bias(max_dist, transposed=False)
    has_sink = sink is not None

    def body(*refs):
        if has_sink:
            sink_ref, refs = refs[0], refs[1:]
        q_ref, k_ref, kp_ref, v_ref, vp_ref, bias_ref, o_ref, lse_ref, kbuf, vbuf = refs[:10]
        i = pl.program_id(1)
        kbuf[0:BLOCK] = kp_ref[...]
        kbuf[BLOCK:] = k_ref[...]
        vbuf[0:BLOCK] = vp_ref[...]
        vbuf[BLOCK:] = v_ref[...]
        if gqa:
            kroll, vroll = refs[10:12]
            kroll[...] = pltpu.roll(kbuf[...], HEAD_DIM, 1)
            vroll[...] = pltpu.roll(vbuf[...], HEAD_DIM, 1)
        half = _half_masks(BLOCK)
        col_prev = (lax.broadcasted_iota(jnp.int32, (1, 2 * BLOCK), 1) < BLOCK).astype(F32)

        def sub(a, carry):
            r0 = pl.multiple_of(a * BLOCK, BLOCK)
            pen = jnp.where((i == 0) & (a == 0), NEG, 0.0)
            b = bias_ref[...] + pen * col_prev
            scores = []
            for p in range(npair):
                qp = q_ref[pl.ds(r0, BLOCK), p * LANES:(p + 1) * LANES]
                for e in range(2):
                    pk, ek = _kv_place(2 * p + e, gqa)
                    kw = (kbuf if ek == e else kroll)[pl.ds(r0, 2 * BLOCK), pk * LANES:(pk + 1) * LANES]
                    scores.append(_dot_nt(jnp.where(half[e], qp, jnp.zeros_like(qp)), kw))
            m_cols, l_cols, probs = [], [], []
            for h, s in enumerate(scores):
                s = s + b
                m = jnp.max(s, axis=1, keepdims=True)
                if has_sink:
                    m = jnp.maximum(m, sink_ref[h])
                pe = jnp.exp(s - m)
                l = jnp.sum(pe, axis=1, keepdims=True)
                if has_sink:
                    l = l + jnp.exp(sink_ref[h] - m)
                probs.append(pe.astype(BF16))
                m_cols.append(m)
                l_cols.append(l)
            for p in range(npair):
                o_h = []
                for e in range(2):
                    h = 2 * p + e
                    pk, ek = _kv_place(h, gqa)
                    vw = (vbuf if ek == e else vroll)[pl.ds(r0, 2 * BLOCK), pk * LANES:(pk + 1) * LANES]
                    o_h.append(_dot(probs[h], vw) * (1.0 / l_cols[h]))
                o_ref[pl.ds(r0, BLOCK), p * LANES:(p + 1) * LANES] = jnp.where(half[0], o_h[0], o_h[1]).astype(BF16)
            lse_ref[pl.ds(r0, BLOCK), :] = _per_head(m_cols) + jnp.log(_per_head(l_cols, 1.0))
            return carry

        lax.fori_loop(0, ns, sub, 0, unroll=True)

    main = lambda w: pl.BlockSpec((None, tq, w), lambda r, i: (r, i, 0))
    prev = lambda w: pl.BlockSpec((None, BLOCK, w), lambda r, i: (r, jnp.maximum(i * ns - 1, 0), 0))
    in_specs = [main(wq), main(wk), prev(wk), main(wk), prev(wk), pl.BlockSpec(bias.shape, lambda r, i: (0, 0))]
    args = [q, k, k, v, v, bias]
    if has_sink:
        in_specs = [pl.BlockSpec(memory_space=pltpu.SMEM)] + in_specs
        args = [sink] + args
    scratch = [pltpu.VMEM((tq + BLOCK, wk), BF16)] * (4 if gqa else 2)
    return pl.pallas_call(
        body, name=name, grid=(dil, nt), in_specs=in_specs,
        out_specs=[main(wq), main(LANES)],
        out_shape=[jax.ShapeDtypeStruct((dil, length, wq), BF16), jax.ShapeDtypeStruct((dil, length, LANES), F32)],
        scratch_shapes=scratch,
    )(*args)


def _band_bwd(q, k, v, do, lse, delta, *, max_dist, name):
    dil, length, wq = q.shape
    wk = k.shape[2]
    gqa = wk != wq
    tq = min(512, length)
    ns, nt = tq // BLOCK, length // tq
    npair = wq // LANES
    nblocks = length // BLOCK
    bias = _band_bias(max_dist, transposed=True)

    def body(q_ref, qn_ref, do_ref, don_ref, lse_ref, lsen_ref, dl_ref, dln_ref, k_ref, v_ref, bias_ref,
             dq_ref, dk_ref, dv_ref, qbuf, dobuf, stat_l, stat_d, dqt, kt, *rolled):
        i = pl.program_id(1)
        qbuf[0:tq] = q_ref[...]
        qbuf[tq:] = qn_ref[...]
        dobuf[0:tq] = do_ref[...]
        dobuf[tq:] = don_ref[...]
        for pk in range(wk // LANES):
            kt[pk] = k_ref[:, pk * LANES:(pk + 1) * LANES].astype(F32).T.astype(BF16)
        if gqa:
            kroll, vroll, ktroll = rolled
            kroll[...] = pltpu.roll(k_ref[...], HEAD_DIM, 1)
            vroll[...] = pltpu.roll(v_ref[...], HEAD_DIM, 1)
            ktroll[0] = kroll[...].astype(F32).T.astype(BF16)
        for a in range(ns):
            rows = slice(a * BLOCK, (a + 1) * BLOCK)
            stat_l[a] = _rows_to_lanes(lse_ref[rows, :])
            stat_d[a] = _rows_to_lanes(dl_ref[rows, :])
        stat_l[ns] = _rows_to_lanes(lsen_ref[...])
        stat_d[ns] = _rows_to_lanes(dln_ref[...])

        @pl.when(i == 0)
        def _():
            dqt[:, :, 0:BLOCK] = jnp.zeros((npair, LANES, BLOCK), F32)

        @pl.when(i > 0)
        def _():
            dqt[:, :, 0:BLOCK] = dqt[:, :, tq:tq + BLOCK]

        dqt[:, :, BLOCK:] = jnp.zeros((npair, LANES, tq), F32)
        half2 = _half_masks(2 * BLOCK)
        row = lax.broadcasted_iota(jnp.int32, (LANES, BLOCK), 0)
        row_half = (row < HEAD_DIM, row >= HEAD_DIM)
        col_next = (lax.broadcasted_iota(jnp.int32, (1, 2 * BLOCK), 1) >= BLOCK).astype(F32)

        for b in range(ns):
            rows = slice(b * BLOCK, (b + 1) * BLOCK)
            window = slice(b * BLOCK, (b + 2) * BLOCK)
            bt = bias_ref[...]
            if b == ns - 1:
                bt = bt + jnp.where(i == nt - 1, NEG, 0.0) * col_next
            acc = {}
            items = []
            for p in range(npair):
                lanes = slice(p * LANES, (p + 1) * LANES)
                qw = qbuf[window, lanes]
                dow = dobuf[window, lanes]
                for e in range(2):
                    h = 2 * p + e
                    pk, ek = _kv_place(h, gqa)
                    klanes = slice(pk * LANES, (pk + 1) * LANES)
                    kb = (k_ref if ek == e else kroll)[rows, klanes]
                    vb = (v_ref if ek == e else vroll)[rows, klanes]
                    qm = jnp.where(half2[e], qw, jnp.zeros_like(qw))
                    dom = jnp.where(half2[e], dow, jnp.zeros_like(dow))
                    items.append(dict(p=p, e=e, h=h, pk=pk, ek=ek, qm=qm, dom=dom,
                                      st=_dot_nt(kb, qm), dpt=_dot_nt(vb, dom)))
            for it in items:
                h = it["h"]
                lrow = jnp.concatenate([stat_l[b, h:h + 1, :], stat_l[b + 1, h:h + 1, :]], axis=1)
                drow = jnp.concatenate([stat_d[b, h:h + 1, :], stat_d[b + 1, h:h + 1, :]], axis=1)
                pt = jnp.exp(it["st"] + bt - lrow)
                it["ptb"] = pt.astype(BF16)
                it["dsb"] = (pt * (it["dpt"] - drow)).astype(BF16)
            for it in items:
                p, e, pk, ek = it["p"], it["e"], it["pk"], it["ek"]
                dv_c = _dot(it["ptb"], it["dom"])
                dk_c = _dot(it["dsb"], it["qm"])
                kbt = (kt if ek == e else ktroll)[pk, :, rows]
                kbtm = jnp.where(row_half[e], kbt, jnp.zeros_like(kbt))
                dqt[p, :, window] += _dot(kbtm, it["dsb"])
                key = (pk, ek == e)
                if key in acc:
                    acc[key] = (acc[key][0] + dk_c, acc[key][1] + dv_c)
                else:
                    acc[key] = (dk_c, dv_c)
            if not gqa:
                for p in range(npair):
                    lanes = slice(p * LANES, (p + 1) * LANES)
                    dk_ref[rows, lanes] = acc[(p, True)][0].astype(BF16)
                    dv_ref[rows, lanes] = acc[(p, True)][1].astype(BF16)
            if gqa:
                dk_al, dv_al = acc[(0, True)]
                dk_mis, dv_mis = acc[(0, False)]
                dk_ref[rows, :] = (dk_al + pltpu.roll(dk_mis, HEAD_DIM, 1)).astype(BF16)
                dv_ref[rows, :] = (dv_al + pltpu.roll(dv_mis, HEAD_DIM, 1)).astype(BF16)

        for p in range(npair):
            dq_ref[:, p * LANES:(p + 1) * LANES] = dqt[p, :, 0:tq].T.astype(BF16)

    main = lambda w: pl.BlockSpec((None, tq, w), lambda r, i: (r, i, 0))
    nxt = lambda w: pl.BlockSpec((None, BLOCK, w), lambda r, i: (r, jnp.minimum((i + 1) * ns, nblocks - 1), 0))
    scratch = [pltpu.VMEM((tq + BLOCK, wq), BF16), pltpu.VMEM((tq + BLOCK, wq), BF16),
               pltpu.VMEM((ns + 1, 8, LANES), F32), pltpu.VMEM((ns + 1, 8, LANES), F32),
               pltpu.VMEM((npair, LANES, tq + BLOCK), F32), pltpu.VMEM((wk // LANES, LANES, tq), BF16)]
    if gqa:
        scratch = scratch + [pltpu.VMEM((tq, wk), BF16)] * 2 + [pltpu.VMEM((1, LANES, tq), BF16)]
    return pl.pallas_call(
        body, name=name, grid=(dil, nt),
        in_specs=[main(wq), nxt(wq), main(wq), nxt(wq), main(LANES), nxt(LANES), main(LANES), nxt(LANES),
                  main(wk), main(wk), pl.BlockSpec(bias.shape, lambda r, i: (0, 0))],
        out_specs=[main(wq), main(wk), main(wk)],
        out_shape=[jax.ShapeDtypeStruct((dil, length, wq), BF16), jax.ShapeDtypeStruct((dil, length, wk), BF16),
                   jax.ShapeDtypeStruct((dil, length, wk), BF16)],
        scratch_shapes=scratch,
        compiler_params=pltpu.CompilerParams(dimension_semantics=("arbitrary", "arbitrary")),
    )(q, q, do, do, lse, lse, delta, delta, k, v, bias)


def _mem_attn_fwd(q, mk, mv):
    seq = q.shape[0]
    tq = min(512, seq)
    ns = tq // BLOCK

    def body(q_ref, mk_ref, mv_ref, o_ref, lse_ref):
        half = _half_masks(BLOCK)

        def sub(a, carry):
            r0 = pl.multiple_of(a * BLOCK, BLOCK)
            scores = []
            for p in range(C_W // LANES):
                lanes = slice(p * LANES, (p + 1) * LANES)
                qp = q_ref[pl.ds(r0, BLOCK), lanes]
                for e in range(2):
                    scores.append(_dot_nt(jnp.where(half[e], qp, jnp.zeros_like(qp)), mk_ref[:, lanes]))
            m_cols, l_cols, probs = [], [], []
            for s in scores:
                m = jnp.max(s, axis=1, keepdims=True)
                pe = jnp.exp(s - m)
                probs.append(pe.astype(BF16))
                m_cols.append(m)
                l_cols.append(jnp.sum(pe, axis=1, keepdims=True))
            for p in range(C_W // LANES):
                lanes = slice(p * LANES, (p + 1) * LANES)
                o_h = [_dot(probs[2 * p + e], mv_ref[:, lanes]) * (1.0 / l_cols[2 * p + e]) for e in range(2)]
                o_ref[pl.ds(r0, BLOCK), lanes] = jnp.where(half[0], o_h[0], o_h[1]).astype(BF16)
            lse_ref[pl.ds(r0, BLOCK), :] = _per_head(m_cols) + jnp.log(_per_head(l_cols, 1.0))
            return carry

        lax.fori_loop(0, ns, sub, 0, unroll=True)

    row = lambda w: pl.BlockSpec((tq, w), lambda i: (i, 0))
    full = pl.BlockSpec((N_MEM, C_W), lambda i: (0, 0))
    return pl.pallas_call(
        body, name="mem_attn_fwd", grid=(seq // tq,), in_specs=[row(C_W), full, full],
        out_specs=[row(C_W), row(LANES)],
        out_shape=[jax.ShapeDtypeStruct((seq, C_W), BF16), jax.ShapeDtypeStruct((seq, LANES), F32)],
    )(q, mk, mv)


def _mem_attn_bwd(q, mk, mv, do, lse, delta):
    seq = q.shape[0]
    tq = min(512, seq)
    ns = tq // BLOCK
    npair = C_W // LANES

    def body(q_ref, mk_ref, mv_ref, do_ref, lse_ref, dl_ref, dq_ref, dmk_ref, dmv_ref, stat_l, stat_d, mkt, dqt):
        @pl.when(pl.program_id(0) == 0)
        def _():
            dmk_ref[...] = jnp.zeros_like(dmk_ref)
            dmv_ref[...] = jnp.zeros_like(dmv_ref)
            for p in range(npair):
                mkt[p] = mk_ref[:, p * LANES:(p + 1) * LANES].astype(F32).T.astype(BF16)

        for a in range(ns):
            rows = slice(a * BLOCK, (a + 1) * BLOCK)
            stat_l[a] = _rows_to_lanes(lse_ref[rows, :])
            stat_d[a] = _rows_to_lanes(dl_ref[rows, :])
        half = _half_masks(BLOCK)
        row = lax.broadcasted_iota(jnp.int32, (LANES, N_MEM), 0)
        row_half = (row < HEAD_DIM, row >= HEAD_DIM)

        for a in range(ns):
            rows = slice(a * BLOCK, (a + 1) * BLOCK)
            items = []
            for p in range(npair):
                lanes = slice(p * LANES, (p + 1) * LANES)
                qp = q_ref[rows, lanes]
                dop = do_ref[rows, lanes]
                for e in range(2):
                    qm = jnp.where(half[e], qp, jnp.zeros_like(qp))
                    dom = jnp.where(half[e], dop, jnp.zeros_like(dop))
                    items.append(dict(p=p, e=e, qm=qm, dom=dom, st=_dot_nt(mk_ref[:, lanes], qm),
                                      dpt=_dot_nt(mv_ref[:, lanes], dom)))
            for it in items:
                h = 2 * it["p"] + it["e"]
                pt = jnp.exp(it["st"] - stat_l[a, h:h + 1, :])
                it["ptb"] = pt.astype(BF16)
                it["dsb"] = (pt * (it["dpt"] - stat_d[a, h:h + 1, :])).astype(BF16)
            for p in range(npair):
                lanes = slice(p * LANES, (p + 1) * LANES)
                pair = [it for it in items if it["p"] == p]
                dmv_ref[:, lanes] += _dot(pair[0]["ptb"], pair[0]["dom"]) + _dot(pair[1]["ptb"], pair[1]["dom"])
                dmk_ref[:, lanes] += _dot(pair[0]["dsb"], pair[0]["qm"]) + _dot(pair[1]["dsb"], pair[1]["qm"])
                kbt = mkt[p]
                dqt[p, :, rows] = (_dot(jnp.where(row_half[0], kbt, jnp.zeros_like(kbt)), pair[0]["dsb"])
                                   + _dot(jnp.where(row_half[1], kbt, jnp.zeros_like(kbt)), pair[1]["dsb"]))
        for p in range(npair):
            dq_ref[:, p * LANES:(p + 1) * LANES] = dqt[p].T.astype(BF16)

    row = lambda w: pl.BlockSpec((tq, w), lambda i: (i, 0))
    full = pl.BlockSpec((N_MEM, C_W), lambda i: (0, 0))
    return pl.pallas_call(
        body, name="mem_attn_bwd", grid=(seq // tq,),
        in_specs=[row(C_W), full, full, row(C_W), row(LANES), row(LANES)], out_specs=[row(C_W), full, full],
        out_shape=[jax.ShapeDtypeStruct((seq, C_W), BF16), jax.ShapeDtypeStruct((N_MEM, C_W), F32),
                   jax.ShapeDtypeStruct((N_MEM, C_W), F32)],
        scratch_shapes=[pltpu.VMEM((ns, 8, LANES), F32)] * 2
        + [pltpu.VMEM((npair, LANES, N_MEM), BF16), pltpu.VMEM((npair, LANES, tq), F32)],
        compiler_params=pltpu.CompilerParams(dimension_semantics=("arbitrary",)),
    )(q, mk, mv, do, lse, delta)


def _silu_and_grad(g):
    s = 1.0 / (1.0 + jnp.exp(-g))
    return g * s, s * (1.0 + g * (1.0 - s))


def _post(x, target, post_norm, w_out, sink_row, oa, lse_a, ga, ob_list, lseb_list, gb, oc, gc):
    seq = x.shape[0]
    tm = min(512, seq)
    inv_d = 1.0 / D_MODEL
    nd = len(B_DILS)

    def body(*refs):
        (x_ref, t_ref, gp_ref, w_ref, sink_ref, oa_ref, lsea_ref, ga_ref), refs = refs[:8], refs[8:]
        ob_refs, lb_refs, (gb_ref, oc_ref, gc_ref), refs = refs[:nd], refs[nd:2 * nd], refs[2 * nd:2 * nd + 3], refs[2 * nd + 3:]
        (g_ref, doa_ref, dla_ref, dga_ref), refs = refs[:4], refs[4:]
        dob_refs, lsec_refs, dlb_refs, refs = refs[:nd], refs[nd:2 * nd], refs[2 * nd:3 * nd], refs[3 * nd:]
        (dgb_ref, doc_ref, dlc_ref, dgc_ref, gw_ref, gpost_ref, gsink_ref, loss_ref), refs = refs[:8], refs[8:]
        ycat, obufs, lbufs, st_do, st_l, st_d = refs[0], refs[1:nd], refs[nd:2 * nd - 1], refs[2 * nd - 1], refs[2 * nd], refs[2 * nd + 1]

        @pl.when(pl.program_id(0) == 0)
        def _():
            gw_ref[...] = jnp.zeros_like(gw_ref)
            gpost_ref[...] = jnp.zeros_like(gpost_ref)
            gsink_ref[...] = jnp.zeros_like(gsink_ref)
            loss_ref[...] = jnp.zeros_like(loss_ref)

        o_i, l_i = [ob_refs[0][0].astype(F32)], [lb_refs[0][0]]
        for k in range(1, nd):
            _from_residues(ob_refs[k], obufs[k - 1], B_DILS[k])
            _from_residues(lb_refs[k], lbufs[k - 1], B_DILS[k])
            o_i.append(_stage_read(obufs[k - 1]))
            l_i.append(_stage_read(lbufs[k - 1]))
        mx = l_i[0]
        for l in l_i[1:]:
            mx = jnp.maximum(mx, l)
        w_i = [jnp.exp(l - mx) for l in l_i]
        z = w_i[0]
        for w in w_i[1:]:
            z = z + w
        _stage_write(st_l, mx + jnp.log(z))
        expand = _head_expand_matrix(B_W)
        inv_z = 1.0 / z
        ob = None
        for w, o in zip(w_i, o_i):
            term = _dot_split(w * inv_z, expand, 2) * o
            ob = term if ob is None else ob + term
        oa, oc = oa_ref[...].astype(F32), oc_ref[...].astype(F32)
        sa, dsa = _silu_and_grad(ga_ref[...].astype(F32))
        sb, dsb = _silu_and_grad(gb_ref[...].astype(F32))
        sc, dsc = _silu_and_grad(gc_ref[...].astype(F32))
        ycat[:, 0:A_W] = (oa * sa).astype(BF16)
        ycat[:, A_W:A_W + B_W] = (ob * sb).astype(BF16)
        ycat[:, A_W + B_W:] = (oc * sc).astype(BF16)
        y2 = _dot(ycat[...], w_ref[...])
        r = lax.rsqrt(jnp.mean(y2 * y2, axis=-1, keepdims=True) + RMS_EPS)
        zhat = y2 * r
        gp = gp_ref[...]
        err = x_ref[...] + zhat * gp - t_ref[...]
        loss_ref[...] += jnp.sum(err * err) * (0.5 * inv_d)
        g = err * inv_d
        g_ref[...] = g
        gpost_ref[...] += jnp.sum(g * zhat, axis=0, keepdims=True)
        a = g * gp
        dy2 = (r * (a - zhat * jnp.mean(a * zhat, axis=-1, keepdims=True))).astype(BF16)
        gw_ref[...] += _dot_tn(ycat[...], dy2)
        dycat = _dot_nt(dy2, w_ref[...])
        dya, dyb, dyc = dycat[:, 0:A_W], dycat[:, A_W:A_W + B_W], dycat[:, A_W + B_W:]
        doa, dob, doc = dya * sa, dyb * sb, dyc * sc
        doa_ref[...] = doa.astype(BF16)
        doc_ref[...] = doc.astype(BF16)
        dga_ref[...] = (dya * oa * dsa).astype(BF16)
        dgb_ref[...] = (dyb * ob * dsb).astype(BF16)
        dgc_ref[...] = (dyc * oc * dsc).astype(BF16)
        dl_a = _dot_split(doa * oa, _head_sum_matrix(A_W), 2)
        dla_ref[...] = dl_a
        dlc_ref[...] = _dot_split(doc * oc, _head_sum_matrix(C_W), 2)
        gsink_ref[...] += jnp.sum(jnp.exp(sink_ref[...] - lsea_ref[...]) * dl_a, axis=0, keepdims=True)
        _stage_write(st_do, dob)
        _stage_write(st_d, _dot_split(dob * ob, _head_sum_matrix(B_W), 2))
        for k, dil in enumerate(B_DILS):
            _to_residues(st_do, dob_refs[k], dil)
            _to_residues(st_l, lsec_refs[k], dil)
            _to_residues(st_d, dlb_refs[k], dil)

    row = lambda w: pl.BlockSpec((tm, w), lambda i: (i, 0))
    full = lambda shape: pl.BlockSpec(shape, lambda i: (0,) * len(shape))
    res_specs = lambda w: [_residue_spec(d, tm, w) for d in B_DILS]
    res_shapes = lambda w, dt: [jax.ShapeDtypeStruct((d, seq // d, w), dt) for d in B_DILS]
    ins = [x, target, post_norm, w_out, sink_row, oa, lse_a, ga, *ob_list, *lseb_list, gb, oc, gc]
    in_specs = ([row(D_MODEL), row(D_MODEL), full((1, D_MODEL)), full((D_MODEL, D_MODEL)), full((1, LANES)),
                 row(A_W), row(LANES), row(A_W)] + res_specs(B_W) + res_specs(LANES) + [row(B_W), row(C_W), row(C_W)])
    out_shape = ([jax.ShapeDtypeStruct((seq, D_MODEL), F32), jax.ShapeDtypeStruct((seq, A_W), BF16),
                  jax.ShapeDtypeStruct((seq, LANES), F32), jax.ShapeDtypeStruct((seq, A_W), BF16)]
                 + res_shapes(B_W, BF16) + res_shapes(LANES, F32) + res_shapes(LANES, F32)
                 + [jax.ShapeDtypeStruct((seq, B_W), BF16), jax.ShapeDtypeStruct((seq, C_W), BF16),
                    jax.ShapeDtypeStruct((seq, LANES), F32), jax.ShapeDtypeStruct((seq, C_W), BF16),
                    jax.ShapeDtypeStruct((D_MODEL, D_MODEL), F32), jax.ShapeDtypeStruct((1, D_MODEL), F32),
                    jax.ShapeDtypeStruct((1, LANES), F32), jax.ShapeDtypeStruct((1, LANES), F32)])
    out_specs = ([row(D_MODEL), row(A_W), row(LANES), row(A_W)] + res_specs(B_W) + res_specs(LANES) + res_specs(LANES)
                 + [row(B_W), row(C_W), row(LANES), row(C_W),
                    full((D_MODEL, D_MODEL)), full((1, D_MODEL)), full((1, LANES)), full((1, LANES))])
    scratch = ([pltpu.VMEM((tm, D_MODEL), BF16)] + [_stage(tm, B_W)] * (nd - 1) + [_stage(tm, LANES)] * (nd - 1)
               + [_stage(tm, B_W), _stage(tm, LANES), _stage(tm, LANES)])
    res = pl.pallas_call(
        body, name="post", grid=(seq // tm,), in_specs=in_specs, out_specs=out_specs, out_shape=out_shape,
        scratch_shapes=scratch,
        compiler_params=pltpu.CompilerParams(dimension_semantics=("arbitrary",)),
    )(*ins)
    out = dict(g=res[0], doa=res[1], dl_a=res[2], dga=res[3], dob=res[4:4 + nd], lse_b=res[4 + nd:4 + 2 * nd],
               dl_b=res[4 + 2 * nd:4 + 3 * nd])
    rest = res[4 + 3 * nd:]
    out.update(dgb=rest[0], doc=rest[1], dl_c=rest[2], dgc=rest[3], gw_out=rest[4], gpost=rest[5], gsink=rest[6],
               loss=rest[7])
    return out


def _grad_w_in(ut, nat, res):
    seq = ut.shape[1]
    tm = min(512, seq)
    nd = len(B_DILS)
    nat_list = [nat[n] for n in _NATURAL]
    res_list = [a for n in _DILATED for a in res[n]]
    rope = _rope_tables(seq, tm)

    def body(rl_ref, rb_ref, ut_ref, *refs):
        nat_refs = dict(zip(_NATURAL, refs[:len(_NATURAL)]))
        refs = refs[len(_NATURAL):]
        res_refs = {n: refs[nd * k:nd * (k + 1)] for k, n in enumerate(_DILATED)}
        refs = refs[nd * len(_DILATED):]
        dproj_ref, gw_ref = refs[:2]
        bufs = {n: refs[2 + (nd - 1) * k:2 + (nd - 1) * (k + 1)] for k, n in enumerate(_DILATED)}

        @pl.when(pl.program_id(0) == 0)
        def _():
            gw_ref[...] = jnp.zeros_like(gw_ref)

        for n in _DILATED:
            for k in range(1, nd):
                _from_residues(res_refs[n][k], bufs[n][k - 1], B_DILS[k])
        c, sm, sp = _rope_coeffs(rl_ref, rb_ref)
        sm, sp = -sm, -sp
        for blk, (name, off, roped, scaled) in enumerate(_PROJ_LAYOUT):
            lanes = slice(off, off + LANES)
            if name in nat_refs:
                piece = nat_refs[name][:, lanes].astype(F32)
            else:
                piece = res_refs[name][0][0, :, lanes].astype(F32)
                for buf in bufs[name]:
                    piece = piece + buf[off // LANES]
            if roped:
                piece = _rope(piece, c, sm, sp)
            if scaled:
                piece = piece * SCALE
            dproj_ref[:, blk * LANES:(blk + 1) * LANES] = piece.astype(BF16)
        for j in range(N_CHIPS):
            gw_ref[j] += _dot(ut_ref[...], dproj_ref[:, j * SHARD_IN:(j + 1) * SHARD_IN])

    row = lambda w: pl.BlockSpec((tm, w), lambda i: (i, 0))
    in_specs = ([pl.BlockSpec(rope[0].shape, lambda i: (0, 0)), pl.BlockSpec((8, 2 * LANES), lambda i: (i, 0)),
                 pl.BlockSpec((D_MODEL, tm), lambda i: (0, i))]
                + [row(a.shape[1]) for a in nat_list]
                + [_residue_spec(d, tm, B_W) for _ in _DILATED for d in B_DILS])
    return pl.pallas_call(
        body, name="grad_w_in", grid=(seq // tm,), in_specs=in_specs,
        out_specs=[row(D_IN), pl.BlockSpec((N_CHIPS, D_MODEL, SHARD_IN), lambda i: (0, 0, 0))],
        out_shape=[jax.ShapeDtypeStruct((seq, D_IN), BF16), jax.ShapeDtypeStruct((N_CHIPS, D_MODEL, SHARD_IN), F32)],
        scratch_shapes=[_stage(tm, B_W)] * ((nd - 1) * len(_DILATED)),
        compiler_params=pltpu.CompilerParams(dimension_semantics=("arbitrary",)),
    )(*rope, ut, *nat_list, *res_list)


def _input_grad(x, g, pre_norm, w_in_g, dproj, gx_prev, span, host, name):
    seq = x.shape[0]
    tm = seq // 16
    first_block, steps = span
    n_host_in = len(host["ins"]) if host else 0
    n_host_out = len(host["outs"]) if host else 0

    def body(*refs):
        x_ref, g_ref, gp_ref, w_ref, dp_ref = refs[:5]
        refs = refs[5 + (gx_prev is not None):]
        host_in, refs = refs[:n_host_in], refs[n_host_in:]
        gx_ref, gpre_ref = refs[:2]
        host_out, sems = refs[2:2 + n_host_out], refs[2 + n_host_out:]
        step = pl.program_id(0)

        @pl.when(step == 0)
        def _():
            gpre_ref[...] = jnp.zeros_like(gpre_ref)
            if host:
                host["start"](host_in, host_out, *sems)

        du = None
        for j in range(N_CHIPS):
            term = _dot_nt(dp_ref[:, j * SHARD_IN:(j + 1) * SHARD_IN], w_ref[j])
            du = term if du is None else du + term
        xv = x_ref[...]
        r = lax.rsqrt(jnp.mean(xv * xv, axis=-1, keepdims=True) + RMS_EPS)
        xhat = xv * r
        gpre_ref[...] += jnp.sum(du * xhat, axis=0, keepdims=True)
        a = du * gp_ref[...]
        gx_ref[...] = g_ref[...] + r * (a - xhat * jnp.mean(a * xhat, axis=-1, keepdims=True))

        if host:
            @pl.when(step == steps - 1)
            def _():
                host["finish"](host_in, host_out, *sems)

    row = lambda w: pl.BlockSpec((tm, w), lambda i: (first_block + i, 0))
    full = lambda a: pl.BlockSpec(a.shape, lambda i: (0,) * a.ndim)
    any_spec = pl.BlockSpec(memory_space=pl.ANY)
    ins = [x, g, pre_norm, w_in_g, dproj]
    in_specs = [row(D_MODEL), row(D_MODEL), full(pre_norm), full(w_in_g), row(D_IN)]
    aliases = {}
    if gx_prev is not None:
        aliases[len(ins)] = 0
        ins.append(gx_prev)
        in_specs.append(any_spec)
    out_shape = [jax.ShapeDtypeStruct((seq, D_MODEL), F32), jax.ShapeDtypeStruct((1, D_MODEL), F32)]
    out_specs = [row(D_MODEL), pl.BlockSpec((1, D_MODEL), lambda i: (0, 0))]
    scratch = []
    if host:
        ins += list(host["ins"])
        in_specs += [any_spec] * n_host_in
        out_shape += list(host["outs"])
        out_specs += [any_spec] * n_host_out
        scratch = list(host["sems"])
    return pl.pallas_call(
        body, name=name, grid=(steps,), in_specs=in_specs, out_specs=out_specs, out_shape=out_shape,
        input_output_aliases=aliases, scratch_shapes=scratch,
        compiler_params=pltpu.CompilerParams(dimension_semantics=("arbitrary",)),
    )(*ins)


def _start_finish(build):
    def start(*refs):
        for cp in build(*refs):
            cp.start()

    def finish(*refs):
        for cp in build(*refs):
            cp.wait()

    return dict(start=start, finish=finish)


def _pair_exchange(grads):
    n = len(grads)

    def build(srcs, outs, send_sems, recv_sems):
        x, y, c = lax.axis_index("x"), lax.axis_index("y"), lax.axis_index("c")
        copies = []
        for t in range(n):
            rows = grads[t].shape[1] // 2
            copies.append(pltpu.make_async_remote_copy(
                src_ref=srcs[t].at[:, pl.ds((1 - c) * rows, rows)], dst_ref=outs[t],
                send_sem=send_sems.at[t], recv_sem=recv_sems.at[t], device_id=(x, y, 1 - c), device_id_type=MESH))
        return copies

    return dict(ins=list(grads), **_start_finish(build),
                outs=[jax.ShapeDtypeStruct((g.shape[0], g.shape[1] // 2, g.shape[2]), g.dtype) for g in grads],
                sems=[pltpu.SemaphoreType.DMA((n,)), pltpu.SemaphoreType.DMA((n,))])


def _pair_add(core, own, got):
    nchip, rows2, width = own.shape
    rows = rows2 // 2
    tr = min(512, rows)
    nb = rows // tr

    def body(core_ref, own_ref, got_ref, out_ref):
        out_ref[...] = (own_ref[...] + got_ref[...]).astype(BF16)

    grid_spec = pltpu.PrefetchScalarGridSpec(
        num_scalar_prefetch=1, grid=(nchip, nb),
        in_specs=[pl.BlockSpec((None, tr, width), lambda k, i, core_ref: (k, core_ref[0] * nb + i, 0)),
                  pl.BlockSpec((None, tr, width), lambda k, i, core_ref: (k, i, 0))],
        out_specs=pl.BlockSpec((None, tr, width), lambda k, i, core_ref: (k, i, 0)))
    return pl.pallas_call(
        body, name=f"pair_add_{width}", grid_spec=grid_spec,
        out_shape=jax.ShapeDtypeStruct((nchip, rows, width), BF16),
    )(core, own, got)


def _chip_exchange(parts):
    n = len(parts)

    def build(srcs, outs, send_sems, recv_sems, local_sems):
        x, y, c = lax.axis_index("x"), lax.axis_index("y"), lax.axis_index("c")
        my_chip = 2 * x + y
        chips = [(1 - x, y), (x, 1 - y), (1 - x, 1 - y)]
        copies = [pltpu.make_async_copy(srcs[t].at[my_chip], outs[t].at[my_chip], local_sems.at[t]) for t in range(n)]
        for j, (cx, cy) in enumerate(chips):
            for t in range(n):
                k = n * j + t
                copies.append(pltpu.make_async_remote_copy(
                    src_ref=srcs[t].at[2 * cx + cy], dst_ref=outs[t].at[my_chip], send_sem=send_sems.at[k],
                    recv_sem=recv_sems.at[k], device_id=(cx, cy, c), device_id_type=MESH))
        return copies

    return dict(ins=list(parts), **_start_finish(build), outs=[jax.ShapeDtypeStruct(p.shape, p.dtype) for p in parts],
                sems=[pltpu.SemaphoreType.DMA((3 * n,)), pltpu.SemaphoreType.DMA((3 * n,)),
                      pltpu.SemaphoreType.DMA((n,))])


def _slot_sum(slots, name, core=None):
    ns, rows, width = slots.shape
    tr = min(512, rows)

    def body(*refs):
        in_ref, out_ref = refs[-2:]
        acc = in_ref[0].astype(F32)
        for s in range(1, ns):
            acc = acc + in_ref[s].astype(F32)
        out_ref[...] = acc

    if core is None:
        return pl.pallas_call(
            body, name=name, grid=(rows // tr,),
            in_specs=[pl.BlockSpec((ns, tr, width), lambda i: (0, i, 0))],
            out_specs=pl.BlockSpec((tr, width), lambda i: (i, 0)),
            out_shape=jax.ShapeDtypeStruct((rows, width), F32),
        )(slots)
    grid_spec = pltpu.PrefetchScalarGridSpec(
        num_scalar_prefetch=1, grid=(rows // tr,),
        in_specs=[pl.BlockSpec((ns, tr, width), lambda i, core_ref: (0, i, 0))],
        out_specs=pl.BlockSpec((None, tr, width), lambda i, core_ref: (core_ref[0], i, 0)))
    return pl.pallas_call(
        body, name=name, grid_spec=grid_spec, out_shape=jax.ShapeDtypeStruct((2, rows, width), F32),
    )(core, slots)


def _pair_gather(bufs, small):
    n = len(bufs)

    def body(*refs):
        small_ref, outs, small_out = refs[n], refs[n + 1:2 * n + 1], refs[2 * n + 1]
        send_sems, recv_sems, local_sem = refs[2 * n + 2:]
        x, y, c = lax.axis_index("x"), lax.axis_index("y"), lax.axis_index("c")
        me = 4 * x + 2 * y + c
        chips = [(1 - x, y), (x, 1 - y), (1 - x, 1 - y)]
        mine = pltpu.make_async_copy(small_ref, small_out.at[me], local_sem)
        mine.start()
        copies = [pltpu.make_async_remote_copy(
            src_ref=outs[t].at[c], dst_ref=outs[t].at[c], send_sem=send_sems.at[t], recv_sem=recv_sems.at[t],
            device_id=(x, y, 1 - c), device_id_type=MESH) for t in range(n)]
        peers = [(x, y, 1 - c)] + [(cx, cy, cc) for (cx, cy) in chips for cc in (c, 1 - c)]
        for j, peer in enumerate(peers):
            copies.append(pltpu.make_async_remote_copy(
                src_ref=small_ref, dst_ref=small_out.at[me], send_sem=send_sems.at[n + j],
                recv_sem=recv_sems.at[n + j], device_id=peer, device_id_type=MESH))
        for cp in copies:
            cp.start()
        for cp in copies:
            cp.wait()
        mine.wait()

    any_spec = pl.BlockSpec(memory_space=pl.ANY)
    res = pl.pallas_call(
        body, name="pair_gather",
        out_shape=[jax.ShapeDtypeStruct(b.shape, b.dtype) for b in bufs]
        + [jax.ShapeDtypeStruct((8,) + small.shape, small.dtype)],
        in_specs=[any_spec] * (n + 1), out_specs=[any_spec] * (n + 1),
        input_output_aliases={t: t for t in range(n)},
        scratch_shapes=[pltpu.SemaphoreType.DMA((n + 7,)), pltpu.SemaphoreType.DMA((n + 7,)),
                        pltpu.SemaphoreType.DMA],
    )(*bufs, small)
    return [r.reshape(2 * b.shape[1], b.shape[2]) for r, b in zip(res[:n], bufs)], res[n]


def _adamw(w, g, m, v, name):
    rows, width = w.shape
    tr = min(256, rows)
    c1 = 1.0 / (1.0 - ADAM_B1 ** ADAM_STEP)
    c2 = 1.0 / (1.0 - ADAM_B2 ** ADAM_STEP)

    def body(w_ref, g_ref, m_ref, v_ref, d_ref, nm_ref, nv_ref):
        gv = g_ref[...]
        nm = ADAM_B1 * m_ref[...] + (1.0 - ADAM_B1) * gv
        nv = ADAM_B2 * v_ref[...] + (1.0 - ADAM_B2) * (gv * gv)
        nm_ref[...] = nm
        nv_ref[...] = nv
        d_ref[...] = -ADAM_LR * ((nm * c1) / (jnp.sqrt(nv * c2) + ADAM_EPS) + ADAM_WD * w_ref[...])

    spec = pl.BlockSpec((tr, width), lambda i: (i, 0))
    return pl.pallas_call(
        body, name=name, grid=(rows // tr,), in_specs=[spec] * 4, out_specs=[spec] * 3,
        out_shape=[jax.ShapeDtypeStruct(w.shape, F32)] * 3,
    )(w, g, m, v)


def _local_step(x, mem, target, pre_norm, sink_a, mem_norm, post_norm, w_in_g, w_out, w_mkv, gathers=None):
    first_gather, late_gather = gathers if gathers else (None, None)
    u, ut, hosted = _pre_norm(x, pre_norm, first_gather)
    if gathers:
        w_in_g = hosted[0].reshape(N_CHIPS, D_MODEL, SHARD_IN)
    pr = _pre_proj(u, w_in_g, late_gather)
    pr["ut"] = ut
    if gathers:
        w_out, w_mkv = (g.reshape(D_MODEL, g.shape[-1]) for g in pr["hosted"])
    mk, mv = _mem_kv(mem, mem_norm, w_mkv)
    sink = sink_a.reshape(-1)
    qa, ka, va = pr["qa"][None], pr["ka"][None], pr["va"][None]
    oa, lse_a = _band_fwd(qa, ka, va, sink, max_dist=A_WINDOW - 1, name="swa_fwd")
    ob_list, lseb_list = [], []
    for k, (win, dil) in enumerate(B_CONFIGS):
        o_i, l_i = _band_fwd(pr["qb"][k], pr["kb"][k], pr["vb"][k], None, max_dist=win // dil, name=f"dil{dil}_fwd")
        ob_list.append(o_i)
        lseb_list.append(l_i)
    oc, lse_c = _mem_attn_fwd(pr["qc"], mk, mv)
    sink_row = jnp.pad(sink, (0, LANES - sink.shape[0])).reshape(1, LANES)
    po = _post(x, target, post_norm, w_out, sink_row, oa[0], lse_a[0], pr["ga"], ob_list, lseb_list, pr["gb"], oc,
               pr["gc"])
    dqc, dmk, dmv = _mem_attn_bwd(pr["qc"], mk, mv, po["doc"], lse_c, po["dl_c"])
    dqa, dka, dva = _band_bwd(qa, ka, va, po["doa"][None], lse_a, po["dl_a"][None], max_dist=A_WINDOW - 1,
                              name="swa_bwd")
    res = dict(qb=[], kb=[], vb=[])
    for k, (win, dil) in enumerate(B_CONFIGS):
        dq_i, dk_i, dv_i = _band_bwd(pr["qb"][k], pr["kb"][k], pr["vb"][k], po["dob"][k], po["lse_b"][k],
                                     po["dl_b"][k], max_dist=win // dil, name=f"dil{dil}_bwd")
        res["qb"].append(dq_i)
        res["kb"].append(dk_i)
        res["vb"].append(dv_i)
    nat = dict(qa=dqa[0], ka=dka[0], va=dva[0], ga=po["dga"], gb=po["dgb"], qc=dqc, gc=po["dgc"])
    dproj, gw_in = _grad_w_in(pr["ut"], nat, res)
    gw_mkv, gmem = _mem_kv_bwd(mem, mem_norm, w_mkv, dmk, dmv)
    gsink = -po["gsink"][0, :sink.shape[0]]
    return dict(loss=po["loss"], g=po["g"], dproj=dproj, gw_in=gw_in, gw_out=po["gw_out"], gw_mkv=gw_mkv,
                gpost=po["gpost"], gmem=gmem, gsink=gsink, w_in_g=w_in_g)


def kernel(x, mem, pre_norm, w_in, sink_a, mem_norm, w_mem_kv, w_out, post_norm, loss_target, m_pre_norm, m_w_in, m_sink_a, m_mem_norm, m_w_mem_kv, m_w_out, m_post_norm, v_pre_norm, v_w_in, v_sink_a, v_mem_norm, v_w_mem_kv, v_w_out, v_post_norm):
    gathers = (_gather_exchange([w_in[0].astype(BF16)]),
               _gather_exchange([w_out[0].astype(BF16), w_mem_kv[0].astype(BF16)]))
    loc = _local_step(x[0], mem[0], loss_target[0], pre_norm, sink_a, mem_norm, post_norm, None, None, None, gathers)
    big = [loc["gw_in"], loc["gw_out"].reshape(N_CHIPS, D_MODEL // N_CHIPS, D_MODEL),
           loc["gw_mkv"].reshape(N_CHIPS, D_MODEL // N_CHIPS, 2 * C_W)]
    core = lax.axis_index("c").astype(jnp.int32).reshape(1)
    w_in_full = loc["w_in_g"]
    step_in = (x[0], loc["g"], pre_norm, w_in_full, loc["dproj"])
    gx_a, gpre_a, *got = _input_grad(*step_in, None, (0, 3), _pair_exchange(big), "input_grad_a")
    parts = [_pair_add(core, own, g) for own, g in zip(big, got)]
    gx_b, gpre_b, *slots = _input_grad(*step_in, gx_a, (3, 10), _chip_exchange(parts), "input_grad_b")
    grad_x, gpre_c = _input_grad(*step_in, gx_b, (13, 3), None, "input_grad_c")
    halves = [_slot_sum(s, name=f"chip_sum_{s.shape[2]}", core=core) for s in slots]
    widen = lambda a: jnp.pad(a.reshape(1, -1), ((0, 0), (0, D_MODEL - a.size)))
    small = jnp.concatenate([gpre_a, loc["gpost"], loc["gmem"], widen(loc["gsink"]), widen(loc["loss"]), gpre_b,
                             gpre_c, jnp.zeros((1, D_MODEL), F32)], axis=0)
    (g_in, g_out, g_mkv), small_slots = _pair_gather(halves, small)
    small_sum = _slot_sum(small_slots, name="device_sum")
    g_pre, g_post, g_mem = small_sum[0:1] + small_sum[5:6] + small_sum[6:7], small_sum[1:2], small_sum[2:3]
    g_sink = small_sum[3:4, :sink_a.shape[1]]
    loss = small_sum[4, 0]

    d_in, nm_in, nv_in = _adamw(w_in[0], g_in, m_w_in[0], v_w_in[0], "adamw_in")
    d_out, nm_out, nv_out = _adamw(w_out[0], g_out, m_w_out[0], v_w_out[0], "adamw_out")
    d_mkv, nm_mkv, nv_mkv = _adamw(w_mem_kv[0], g_mkv, m_w_mem_kv[0], v_w_mem_kv[0], "adamw_mkv")
    pad6 = lambda a: jnp.pad(a, ((0, 0), (0, D_MODEL - a.shape[1])))
    stack = lambda a, b, c_, d_: jnp.concatenate([a, b, c_, pad6(d_), jnp.zeros((4, D_MODEL), F32)], axis=0)
    d_s, nm_s, nv_s = _adamw(stack(pre_norm, post_norm, mem_norm, sink_a),
                             jnp.concatenate([g_pre, small_sum[1:]], axis=0),
                             stack(m_pre_norm, m_post_norm, m_mem_norm, m_sink_a),
                             stack(v_pre_norm, v_post_norm, v_mem_norm, v_sink_a), "adamw_small")
    ns_ = sink_a.shape[1]
    unpack = lambda a: (a[0:1], a[3:4, :ns_], a[2:3], a[1:2])
    d_pre, d_sink, d_mem, d_post = unpack(d_s)
    nm_pre, nm_sink, nm_mem, nm_post = unpack(nm_s)
    nv_pre, nv_sink, nv_mem, nv_post = unpack(nv_s)
    lead = lambda a: a[None]
    return (loss, lead(grad_x),
            g_pre, lead(g_in), g_sink, g_mem, lead(g_mkv), lead(g_out), g_post,
            d_pre, lead(d_in), d_sink, d_mem, lead(d_mkv), lead(d_out), d_post,
            nm_pre, lead(nm_in), nm_sink, nm_mem, lead(nm_mkv), lead(nm_out), nm_post,
            nv_pre, lead(nv_in), nv_sink, nv_mem, lead(nv_mkv), lead(nv_out), nv_post)
```

```python
import numpy as np
import jax
import jax.numpy as jnp
from jax import lax
from jax.experimental import pallas as pl
from jax.experimental.pallas import tpu as pltpu

F32 = jnp.float32
BF16 = jnp.bfloat16

D_MODEL = 1024
HEAD_DIM = 64
LANES = 128
BLOCK = 128
A_W, A_KV_W, B_W, C_W = 384, 128, 384, 256
N_MEM = 256
D_IN = 3072
N_CHIPS = 4
SHARD_IN = D_IN // N_CHIPS
B_CONFIGS = ((128, 1), (512, 4), (2048, 16))
B_DILS = tuple(d for _, d in B_CONFIGS)
A_WINDOW = 128
RMS_EPS = 1e-6
ROPE_THETA = 500000.0
SCALE = HEAD_DIM ** -0.5
NEG = -1e30
ADAM_LR, ADAM_B1, ADAM_B2, ADAM_EPS, ADAM_WD, ADAM_STEP = 0.001, 0.9, 0.999, 1e-08, 0.01, 10

NT = (((1,), (1,)), ((), ()))
TN = (((0,), (0,)), ((), ()))
MESH = pl.DeviceIdType.MESH

_PROJ_LAYOUT = (
    [("qa", 128 * i, True, True) for i in range(3)] + [("ka", 0, True, False), ("va", 0, False, False)]
    + [("ga", 128 * i, False, False) for i in range(3)]
    + [("qb", 128 * i, True, True) for i in range(3)] + [("kb", 128 * i, True, False) for i in range(3)]
    + [("vb", 128 * i, False, False) for i in range(3)] + [("gb", 128 * i, False, False) for i in range(3)]
    + [("qc", 128 * i, False, True) for i in range(2)] + [("gc", 128 * i, False, False) for i in range(2)]
)
_PROJ_WIDTH = dict(qa=A_W, ka=A_KV_W, va=A_KV_W, ga=A_W, qb=B_W, kb=B_W, vb=B_W, gb=B_W, qc=C_W, gc=C_W)
_NATURAL = ("qa", "ka", "va", "ga", "gb", "qc", "gc")
_DILATED = ("qb", "kb", "vb")


def _dot(a, b):
    return jnp.dot(a, b, preferred_element_type=F32)


def _dot_nt(a, b):
    return lax.dot_general(a, b, NT, preferred_element_type=F32)


def _dot_tn(a, b):
    return lax.dot_general(a, b, TN, preferred_element_type=F32)


def _half_masks(rows):
    lane = lax.broadcasted_iota(jnp.int32, (rows, LANES), 1)
    return lane < HEAD_DIM, lane >= HEAD_DIM


def _rope(t, c, sm, sp):
    return t * c + pltpu.roll(t, LANES - 8, 1) * sm + pltpu.roll(t, 8, 1) * sp


def _rope_tables(seq, tm):
    dim = jnp.arange(LANES) % HEAD_DIM
    inv_freq = ROPE_THETA ** (-jnp.arange(0, 16, 2, dtype=F32) / 16)
    freq = jnp.where(dim < 16, inv_freq[dim % 8], 0.0)[None, :]
    local = jnp.arange(tm, dtype=F32)[:, None] * freq
    base = (jnp.arange(seq // tm, dtype=F32) * tm)[:, None] * freq
    both = lambda a: jnp.concatenate([jnp.cos(a), jnp.sin(a)], axis=1)
    return both(local), jnp.repeat(both(base), 8, axis=0)


def _rope_coeffs(local_ref, base_ref):
    cl, sl = local_ref[:, :LANES], local_ref[:, LANES:]
    cb, sb = base_ref[0:1, :LANES], base_ref[0:1, LANES:]
    cos = cb * cl - sb * sl
    sin = sb * cl + cb * sl
    dim = lax.broadcasted_iota(jnp.int32, (1, LANES), 1) % HEAD_DIM
    return cos, jnp.where(dim < 8, -sin, 0.0), jnp.where((dim >= 8) & (dim < 16), sin, 0.0)


def _split3(x):
    a = x.astype(BF16)
    r = x - a.astype(F32)
    b = r.astype(BF16)
    c = (r - b.astype(F32)).astype(BF16)
    return a, b, c


def _rows_to_lanes(x):
    row = lax.broadcasted_iota(jnp.int32, (8, LANES), 0)
    lane = lax.broadcasted_iota(jnp.int32, (8, LANES), 1)
    eye = (row == lane).astype(BF16)
    a, b, c = _split3(x)
    return _dot_nt(eye, a) + _dot_nt(eye, b) + _dot_nt(eye, c)


def _head_sum_matrix(width):
    k = lax.broadcasted_iota(jnp.int32, (width, LANES), 0)
    h = lax.broadcasted_iota(jnp.int32, (width, LANES), 1)
    return (k // HEAD_DIM == h).astype(BF16)


def _head_expand_matrix(width):
    h = lax.broadcasted_iota(jnp.int32, (LANES, width), 0)
    k = lax.broadcasted_iota(jnp.int32, (LANES, width), 1)
    return (k // HEAD_DIM == h).astype(BF16)


def _dot_split(x, mat, terms):
    parts = _split3(x)[:terms]
    out = _dot(parts[0], mat)
    for p in parts[1:]:
        out = out + _dot(p, mat)
    return out


def _per_head(cols, fill=0.0):
    rows = cols[0].shape[0]
    lane = lax.broadcasted_iota(jnp.int32, (rows, LANES), 1)
    out = jnp.full((rows, LANES), fill, F32)
    for h, col in enumerate(cols):
        out = jnp.where(lane == h, col, out)
    return out


def _lane_blocks(width):
    return [slice(p * LANES, (p + 1) * LANES) for p in range(width // LANES)]


def _stage(rows, width):
    return pltpu.VMEM((width // LANES, rows, LANES), F32)


def _stage_write(buf, value):
    for p, lanes in enumerate(_lane_blocks(value.shape[1])):
        buf[p] = value[:, lanes]


def _stage_read(buf):
    return jnp.concatenate([buf[p] for p in range(buf.shape[0])], axis=1) if buf.shape[0] > 1 else buf[0]


def _to_residues(buf, out_ref, dil):
    rows = buf.shape[1] // dil
    for r in range(dil):
        for p in range(buf.shape[0]):
            plane = buf.at[p]
            out_ref[r, :, p * LANES:(p + 1) * LANES] = plane[pl.ds(r, rows, stride=dil), :].astype(out_ref.dtype)


def _from_residues(in_ref, buf, dil):
    rows = buf.shape[1] // dil
    for r in range(dil):
        for p in range(buf.shape[0]):
            plane = buf.at[p]
            plane[pl.ds(r, rows, stride=dil), :] = in_ref[r, :, p * LANES:(p + 1) * LANES].astype(F32)


def _residue_spec(dil, tm, width):
    return pl.BlockSpec((dil, tm // dil, width), lambda i: (0, i, 0))


def _gather_exchange(shards_2d):
    shards = tuple(s.reshape(2, s.shape[0] // 2, s.shape[1]) for s in shards_2d)
    n = len(shards)

    def copies(in_refs, out_refs, send_sems, recv_sems):
        srcs, outs = in_refs[:n], out_refs
        x, y, c = lax.axis_index("x"), lax.axis_index("y"), lax.axis_index("c")
        my_chip = 2 * x + y
        sibling = (x, y, 1 - c)
        chips = [(1 - x, y), (x, 1 - y), (1 - x, 1 - y)]

        def copy(k, src, dst, to):
            return pltpu.make_async_remote_copy(src_ref=src, dst_ref=dst, send_sem=send_sems.at[k],
                                                recv_sem=recv_sems.at[k], device_id=to, device_id_type=MESH)

        first, arrive, passed, sibling_arrive = [], [], [], []
        for j, (cx, cy) in enumerate(chips):
            chip = 2 * cx + cy
            for t in range(n):
                k = n * j + t
                first.append(copy(k, srcs[t].at[c], outs[t].at[my_chip, c], (cx, cy, c)))
                arrive.append(copy(k, srcs[t].at[c], outs[t].at[chip, c], (cx, cy, c)))
                passed.append(copy(n * 3 + k, outs[t].at[chip, c], outs[t].at[chip, c], sibling))
                sibling_arrive.append(copy(n * 3 + k, outs[t].at[chip, 1 - c], outs[t].at[chip, 1 - c], sibling))
        return first, arrive, passed, sibling_arrive

    def start(*refs):
        for cp in copies(*refs)[0]:
            cp.start()

    def finish(*refs):
        first, arrive, passed, sibling_arrive = copies(*refs)
        for got, fwd in zip(arrive, passed):
            got.wait_recv()
            fwd.start()
        for cp in sibling_arrive:
            cp.wait_recv()
        for cp in first + passed:
            cp.wait_send()

    my_chip = 2 * lax.axis_index("x") + lax.axis_index("y")
    landing = [lax.dynamic_update_slice(jnp.zeros((N_CHIPS,) + s.shape, s.dtype), s[None], (my_chip, 0, 0, 0))
               for s in shards]
    return dict(ins=list(shards) + landing, start=start, finish=finish, aliases={n + t: t for t in range(n)},
                outs=[jax.ShapeDtypeStruct((N_CHIPS,) + s.shape, s.dtype) for s in shards],
                sems=[pltpu.SemaphoreType.DMA((6 * n,)), pltpu.SemaphoreType.DMA((6 * n,))])


def _run_exchange(ex, name):
    n_in, n_out = len(ex["ins"]), len(ex["outs"])

    def body(*refs):
        in_refs, out_refs, sems = refs[:n_in], refs[n_in:n_in + n_out], refs[n_in + n_out:]
        ex["start"](in_refs, out_refs, *sems)
        ex["finish"](in_refs, out_refs, *sems)

    any_spec = pl.BlockSpec(memory_space=pl.ANY)
    return pl.pallas_call(
        body, name=name, out_shape=ex["outs"], in_specs=[any_spec] * n_in, out_specs=[any_spec] * n_out,
        input_output_aliases=ex.get("aliases", {}), scratch_shapes=ex["sems"],
    )(*ex["ins"])


def _mem_kv(mem, mem_norm, w_mkv):
    def body(mem_ref, g_ref, w_ref, mk_ref, mv_ref):
        m = mem_ref[...]
        r = lax.rsqrt(jnp.mean(m * m, axis=-1, keepdims=True) + RMS_EPS)
        mn = (m * r * g_ref[...]).astype(BF16)
        kv = _dot(mn, w_ref[...])
        mk_ref[...] = kv[:, :C_W].astype(BF16)
        mv_ref[...] = kv[:, C_W:].astype(BF16)

    return pl.pallas_call(
        body, name="mem_kv",
        out_shape=[jax.ShapeDtypeStruct((N_MEM, C_W), BF16)] * 2,
    )(mem, mem_norm, w_mkv)


def _mem_kv_bwd(mem, mem_norm, w_mkv, dmk, dmv):
    def body(mem_ref, g_ref, w_ref, dmk_ref, dmv_ref, gw_ref, gn_ref):
        m = mem_ref[...]
        r = lax.rsqrt(jnp.mean(m * m, axis=-1, keepdims=True) + RMS_EPS)
        mhat = m * r
        mn = (mhat * g_ref[...]).astype(BF16)
        dkv = jnp.concatenate([dmk_ref[...], dmv_ref[...]], axis=1).astype(BF16)
        gw_ref[...] = _dot_tn(mn, dkv)
        dmn = _dot_nt(dkv, w_ref[...])
        gn_ref[...] = jnp.sum(dmn * mhat, axis=0, keepdims=True)

    return pl.pallas_call(
        body, name="mem_kv_bwd",
        out_shape=[jax.ShapeDtypeStruct((D_MODEL, 2 * C_W), F32), jax.ShapeDtypeStruct((1, D_MODEL), F32)],
    )(mem, mem_norm, w_mkv, dmk, dmv)


def _pre_norm(x, pre_norm, host=None):
    seq = x.shape[0]
    tm = min(512, seq)
    n_host_in = len(host["ins"]) if host else 0
    n_host_out = len(host["outs"]) if host else 0

    def body(x_ref, g_ref, *refs):
        host_in, (u_ref, ut_ref), refs = refs[:n_host_in], refs[n_host_in:n_host_in + 2], refs[n_host_in + 2:]
        host_out, sems = refs[:n_host_out], refs[n_host_out:]
        if host:
            @pl.when(pl.program_id(0) == 0)
            def _():
                host["start"](host_in, host_out, *sems)

        xv = x_ref[...]
        r = lax.rsqrt(jnp.mean(xv * xv, axis=-1, keepdims=True) + RMS_EPS)
        u = xv * r * g_ref[...]
        u_ref[...] = u.astype(BF16)
        ut_ref[...] = u.T.astype(BF16)
        if host:
            @pl.when(pl.program_id(0) == seq // tm - 1)
            def _():
                host["finish"](host_in, host_out, *sems)

    any_spec = pl.BlockSpec(memory_space=pl.ANY)
    ins = [x, pre_norm]
    in_specs = [pl.BlockSpec((tm, D_MODEL), lambda i: (i, 0)), pl.BlockSpec(pre_norm.shape, lambda i: (0, 0))]
    out_shape = [jax.ShapeDtypeStruct((seq, D_MODEL), BF16), jax.ShapeDtypeStruct((D_MODEL, seq), BF16)]
    out_specs = [pl.BlockSpec((tm, D_MODEL), lambda i: (i, 0)), pl.BlockSpec((D_MODEL, tm), lambda i: (0, i))]
    aliases, scratch = {}, []
    if host:
        aliases = {len(ins) + k: 2 + v for k, v in host.get("aliases", {}).items()}
        ins += list(host["ins"])
        in_specs += [any_spec] * n_host_in
        out_shape += list(host["outs"])
        out_specs += [any_spec] * n_host_out
        scratch = list(host["sems"])
    res = pl.pallas_call(
        body, name="pre_norm", grid=(seq // tm,), in_specs=in_specs, out_specs=out_specs, out_shape=out_shape,
        input_output_aliases=aliases, scratch_shapes=scratch,
        compiler_params=pltpu.CompilerParams(dimension_semantics=("arbitrary",)),
    )(*ins)
    return res[0], res[1], res[2:]


def _pre_proj(u, w_in_g, host=None):
    seq = u.shape[0]
    tm = min(512, seq)
    n_nat, n_dil = len(_NATURAL), len(_DILATED) * len(B_DILS)
    rope = _rope_tables(seq, tm)

    n_host_in = len(host["ins"]) if host else 0
    n_host_out = len(host["outs"]) if host else 0
    n_own_out = n_nat + n_dil

    def body(u_ref, w_ref, rl_ref, rb_ref, *refs):
        host_in, refs = refs[:n_host_in], refs[n_host_in:]
        nat = dict(zip(_NATURAL, refs[:n_nat]))
        res = {n: refs[n_nat + len(B_DILS) * k:n_nat + len(B_DILS) * (k + 1)] for k, n in enumerate(_DILATED)}
        host_out = refs[n_own_out:n_own_out + n_host_out]
        bufs = dict(zip(_DILATED, refs[n_own_out + n_host_out:]))
        sems = refs[n_own_out + n_host_out + len(_DILATED):]
        if host:
            @pl.when(pl.program_id(0) == 0)
            def _():
                host["start"](host_in, host_out, *sems)

        ub = u_ref[...]
        c, sm, sp = _rope_coeffs(rl_ref, rb_ref)
        for j in range(N_CHIPS):
            pj = _dot(ub, w_ref[j])
            for b in range(SHARD_IN // LANES):
                name, off, roped, scaled = _PROJ_LAYOUT[(SHARD_IN // LANES) * j + b]
                piece = pj[:, LANES * b:LANES * (b + 1)]
                if roped:
                    piece = _rope(piece, c, sm, sp)
                if scaled:
                    piece = piece * SCALE
                if name in bufs:
                    bufs[name][off // LANES] = piece
                else:
                    nat[name][:, off:off + LANES] = piece.astype(BF16)
        for name in _DILATED:
            for ref, dil in zip(res[name], B_DILS):
                _to_residues(bufs[name], ref, dil)
        if host:
            @pl.when(pl.program_id(0) == seq // tm - 1)
            def _():
                host["finish"](host_in, host_out, *sems)

    row = lambda w: pl.BlockSpec((tm, w), lambda i: (i, 0))
    full = lambda a: pl.BlockSpec(a.shape, lambda i: (0,) * a.ndim)
    any_spec = pl.BlockSpec(memory_space=pl.ANY)
    out_shape = [jax.ShapeDtypeStruct((seq, _PROJ_WIDTH[n]), BF16) for n in _NATURAL]
    out_specs = [row(_PROJ_WIDTH[n]) for n in _NATURAL]
    for n in _DILATED:
        for dil in B_DILS:
            out_shape.append(jax.ShapeDtypeStruct((dil, seq // dil, B_W), BF16))
            out_specs.append(_residue_spec(dil, tm, B_W))
    ins = [u, w_in_g, *rope]
    in_specs = [row(D_MODEL), full(w_in_g), full(rope[0]), pl.BlockSpec((8, 2 * LANES), lambda i: (i, 0))]
    scratch = [_stage(tm, B_W)] * len(_DILATED)
    aliases = {}
    if host:
        aliases = {len(ins) + k: n_own_out + v for k, v in host.get("aliases", {}).items()}
        ins += list(host["ins"])
        in_specs += [any_spec] * n_host_in
        out_shape += list(host["outs"])
        out_specs += [any_spec] * n_host_out
        scratch += list(host["sems"])
    res = pl.pallas_call(
        body, name="pre_proj", grid=(seq // tm,), in_specs=in_specs, out_specs=out_specs, out_shape=out_shape,
        input_output_aliases=aliases, scratch_shapes=scratch,
        compiler_params=pltpu.CompilerParams(dimension_semantics=("arbitrary",)),
    )(*ins)
    out = dict(zip(_NATURAL, res[:n_nat]))
    for k, n in enumerate(_DILATED):
        out[n] = res[n_nat + len(B_DILS) * k:n_nat + len(B_DILS) * (k + 1)]
    out["hosted"] = res[n_own_out:]
    return out


def _band_bias(max_dist, transposed):
    i = np.arange(BLOCK)[:, None]
    j = np.arange(BLOCK)[None, :]
    if transposed:
        same = i <= j
        other = (j + BLOCK - i) <= max_dist
        vis = np.concatenate([same, other], axis=1)
    else:
        prev = (i + BLOCK - j) <= max_dist
        same = j <= i
        vis = np.concatenate([prev, same], axis=1)
    return jnp.asarray(np.where(vis, 0.0, NEG).astype(np.float32))


def _kv_place(h, gqa):
    return (0, h // 3) if gqa else (h // 2, h % 2)


def _band_fwd(q, k, v, sink, *, max_dist, name):
    dil, length, wq = q.shape
    wk = k.shape[2]
    gqa = wk != wq
    tq = min(512, length)
    ns, nt = tq // BLOCK, length // tq
    npair = wq // LANES
    bias = _band_bias(max_dist, transposed=False)
    has_sink = sink is not None

    def body(*refs):
        if has_sink:
            sink_ref, refs = refs[0], refs[1:]
        q_ref, k_ref, kp_ref, v_ref, vp_ref, bias_ref, o_ref, lse_ref, kbuf, vbuf = refs[:10]
        i = pl.program_id(1)
        kbuf[0:BLOCK] = kp_ref[...]
        kbuf[BLOCK:] = k_ref[...]
        vbuf[0:BLOCK] = vp_ref[...]
        vbuf[BLOCK:] = v_ref[...]
        if gqa:
            kroll, vroll = refs[10:12]
            kroll[...] = pltpu.roll(kbuf[...], HEAD_DIM, 1)
            vroll[...] = pltpu.roll(vbuf[...], HEAD_DIM, 1)
        half = _half_masks(BLOCK)
        col_prev = (lax.broadcasted_iota(jnp.int32, (1, 2 * BLOCK), 1) < BLOCK).astype(F32)

        def sub(a, carry):
            r0 = pl.multiple_of(a * BLOCK, BLOCK)
            pen = jnp.where((i == 0) & (a == 0), NEG, 0.0)
            b = bias_ref[...] + pen * col_prev
            scores = []
            for p in range(npair):
                qp = q_ref[pl.ds(r0, BLOCK), p * LANES:(p + 1) * LANES]
                for e in range(2):
                    pk, ek = _kv_place(2 * p + e, gqa)
                    kw = (kbuf if ek == e else kroll)[pl.ds(r0, 2 * BLOCK), pk * LANES:(pk + 1) * LANES]
                    scores.append(_dot_nt(jnp.where(half[e], qp, jnp.zeros_like(qp)), kw))
            m_cols, l_cols, probs = [], [], []
            for h, s in enumerate(scores):
                s = s + b
                m = jnp.max(s, axis=1, keepdims=True)
                if has_sink:
                    m = jnp.maximum(m, sink_ref[h])
                pe = jnp.exp(s - m)
                l = jnp.sum(pe, axis=1, keepdims=True)
                if has_sink:
                    l = l + jnp.exp(sink_ref[h] - m)
                probs.append(pe.astype(BF16))
                m_cols.append(m)
                l_cols.append(l)
            for p in range(npair):
                o_h = []
                for e in range(2):
                    h = 2 * p + e
                    pk, ek = _kv_place(h, gqa)
                    vw = (vbuf if ek == e else vroll)[pl.ds(r0, 2 * BLOCK), pk * LANES:(pk + 1) * LANES]
                    o_h.append(_dot(probs[h], vw) * (1.0 / l_cols[h]))
                o_ref[pl.ds(r0, BLOCK), p * LANES:(p + 1) * LANES] = jnp.where(half[0], o_h[0], o_h[1]).astype(BF16)
            lse_ref[pl.ds(r0, BLOCK), :] = _per_head(m_cols) + jnp.log(_per_head(l_cols, 1.0))
            return carry

        lax.fori_loop(0, ns, sub, 0, unroll=True)

    main = lambda w: pl.BlockSpec((None, tq, w), lambda r, i: (r, i, 0))
    prev = lambda w: pl.BlockSpec((None, BLOCK, w), lambda r, i: (r, jnp.maximum(i * ns - 1, 0), 0))
    in_specs = [main(wq), main(wk), prev(wk), main(wk), prev(wk), pl.BlockSpec(bias.shape, lambda r, i: (0, 0))]
    args = [q, k, k, v, v, bias]
    if has_sink:
        in_specs = [pl.BlockSpec(memory_space=pltpu.SMEM)] + in_specs
        args = [sink] + args
    scratch = [pltpu.VMEM((tq + BLOCK, wk), BF16)] * (4 if gqa else 2)
    return pl.pallas_call(
        body, name=name, grid=(dil, nt), in_specs=in_specs,
        out_specs=[main(wq), main(LANES)],
        out_shape=[jax.ShapeDtypeStruct((dil, length, wq), BF16), jax.ShapeDtypeStruct((dil, length, LANES), F32)],
        scratch_shapes=scratch,
    )(*args)


def _band_bwd(q, k, v, do, lse, delta, *, max_dist, name):
    dil, length, wq = q.shape
    wk = k.shape[2]
    gqa = wk != wq
    tq = min(512, length)
    ns, nt = tq // BLOCK, length // tq
    npair = wq // LANES
    nblocks = length // BLOCK
    bias = _band_bias(max_dist, transposed=True)

    def body(q_ref, qn_ref, do_ref, don_ref, lse_ref, lsen_ref, dl_ref, dln_ref, k_ref, v_ref, bias_ref,
             dq_ref, dk_ref, dv_ref, qbuf, dobuf, stat_l, stat_d, dqt, kt, *rolled):
        i = pl.program_id(1)
        qbuf[0:tq] = q_ref[...]
        qbuf[tq:] = qn_ref[...]
        dobuf[0:tq] = do_ref[...]
        dobuf[tq:] = don_ref[...]
        for pk in range(wk // LANES):
            kt[pk] = k_ref[:, pk * LANES:(pk + 1) * LANES].astype(F32).T.astype(BF16)
        if gqa:
            kroll, vroll, ktroll = rolled
            kroll[...] = pltpu.roll(k_ref[...], HEAD_DIM, 1)
            vroll[...] = pltpu.roll(v_ref[...], HEAD_DIM, 1)
            ktroll[0] = kroll[...].astype(F32).T.astype(BF16)
        for a in range(ns):
            rows = slice(a * BLOCK, (a + 1) * BLOCK)
            stat_l[a] = _rows_to_lanes(lse_ref[rows, :])
            stat_d[a] = _rows_to_lanes(dl_ref[rows, :])
        stat_l[ns] = _rows_to_lanes(lsen_ref[...])
        stat_d[ns] = _rows_to_lanes(dln_ref[...])

        @pl.when(i == 0)
        def _():
            dqt[:, :, 0:BLOCK] = jnp.zeros((npair, LANES, BLOCK), F32)

        @pl.when(i > 0)
        def _():
            dqt[:, :, 0:BLOCK] = dqt[:, :, tq:tq + BLOCK]

        dqt[:, :, BLOCK:] = jnp.zeros((npair, LANES, tq), F32)
        half2 = _half_masks(2 * BLOCK)
        row = lax.broadcasted_iota(jnp.int32, (LANES, BLOCK), 0)
        row_half = (row < HEAD_DIM, row >= HEAD_DIM)
        col_next = (lax.broadcasted_iota(jnp.int32, (1, 2 * BLOCK), 1) >= BLOCK).astype(F32)

        for b in range(ns):
            rows = slice(b * BLOCK, (b + 1) * BLOCK)
            window = slice(b * BLOCK, (b + 2) * BLOCK)
            bt = bias_ref[...]
            if b == ns - 1:
                bt = bt + jnp.where(i == nt - 1, NEG, 0.0) * col_next
            acc = {}
            items = []
            for p in range(npair):
                lanes = slice(p * LANES, (p + 1) * LANES)
                qw = qbuf[window, lanes]
                dow = dobuf[window, lanes]
                for e in range(2):
                    h = 2 * p + e
                    pk, ek = _kv_place(h, gqa)
                    klanes = slice(pk * LANES, (pk + 1) * LANES)
                    kb = (k_ref if ek == e else kroll)[rows, klanes]
                    vb = (v_ref if ek == e else vroll)[rows, klanes]
                    qm = jnp.where(half2[e], qw, jnp.zeros_like(qw))
                    dom = jnp.where(half2[e], dow, jnp.zeros_like(dow))
                    items.append(dict(p=p, e=e, h=h, pk=pk, ek=ek, qm=qm, dom=dom,
                                      st=_dot_nt(kb, qm), dpt=_dot_nt(vb, dom)))
            for it in items:
                h = it["h"]
                lrow = jnp.concatenate([stat_l[b, h:h + 1, :], stat_l[b + 1, h:h + 1, :]], axis=1)
                drow = jnp.concatenate([stat_d[b, h:h + 1, :], stat_d[b + 1, h:h + 1, :]], axis=1)
                pt = jnp.exp(it["st"] + bt - lrow)
                it["ptb"] = pt.astype(BF16)
                it["dsb"] = (pt * (it["dpt"] - drow)).astype(BF16)
            for it in items:
                p, e, pk, ek = it["p"], it["e"], it["pk"], it["ek"]
                dv_c = _dot(it["ptb"], it["dom"])
                dk_c = _dot(it["dsb"], it["qm"])
                kbt = (kt if ek == e else ktroll)[pk, :, rows]
                kbtm = jnp.where(row_half[e], kbt, jnp.zeros_like(kbt))
                dqt[p, :, window] += _dot(kbtm, it["dsb"])
                key = (pk, ek == e)
                if key in acc:
                    acc[key] = (acc[key][0] + dk_c, acc[key][1] + dv_c)
                else:
                    acc[key] = (dk_c, dv_c)
            if not gqa:
                for p in range(npair):
                    lanes = slice(p * LANES, (p + 1) * LANES)
                    dk_ref[rows, lanes] = acc[(p, True)][0].astype(BF16)
                    dv_ref[rows, lanes] = acc[(p, True)][1].astype(BF16)
            if gqa:
                dk_al, dv_al = acc[(0, True)]
                dk_mis, dv_mis = acc[(0, False)]
                dk_ref[rows, :] = (dk_al + pltpu.roll(dk_mis, HEAD_DIM, 1)).astype(BF16)
                dv_ref[rows, :] = (dv_al + pltpu.roll(dv_mis, HEAD_DIM, 1)).astype(BF16)

        for p in range(npair):
            dq_ref[:, p * LANES:(p + 1) * LANES] = dqt[p, :, 0:tq].T.astype(BF16)

    main = lambda w: pl.BlockSpec((None, tq, w), lambda r, i: (r, i, 0))
    nxt = lambda w: pl.BlockSpec((None, BLOCK, w), lambda r, i: (r, jnp.minimum((i + 1) * ns, nblocks - 1), 0))
    scratch = [pltpu.VMEM((tq + BLOCK, wq), BF16), pltpu.VMEM((tq + BLOCK, wq), BF16),
               pltpu.VMEM((ns + 1, 8, LANES), F32), pltpu.VMEM((ns + 1, 8, LANES), F32),
               pltpu.VMEM((npair, LANES, tq + BLOCK), F32), pltpu.VMEM((wk // LANES, LANES, tq), BF16)]
    if gqa:
        scratch = scratch + [pltpu.VMEM((tq, wk), BF16)] * 2 + [pltpu.VMEM((1, LANES, tq), BF16)]
    return pl.pallas_call(
        body, name=name, grid=(dil, nt),
        in_specs=[main(wq), nxt(wq), main(wq), nxt(wq), main(LANES), nxt(LANES), main(LANES), nxt(LANES),
                  main(wk), main(wk), pl.BlockSpec(bias.shape, lambda r, i: (0, 0))],
        out_specs=[main(wq), main(wk), main(wk)],
        out_shape=[jax.ShapeDtypeStruct((dil, length, wq), BF16), jax.ShapeDtypeStruct((dil, length, wk), BF16),
                   jax.ShapeDtypeStruct((dil, length, wk), BF16)],
        scratch_shapes=scratch,
        compiler_params=pltpu.CompilerParams(dimension_semantics=("arbitrary", "arbitrary")),
    )(q, q, do, do, lse, lse, delta, delta, k, v, bias)


def _mem_attn_fwd(q, mk, mv):
    seq = q.shape[0]
    tq = min(512, seq)
    ns = tq // BLOCK

    def body(q_ref, mk_ref, mv_ref, o_ref, lse_ref):
        half = _half_masks(BLOCK)

        def sub(a, carry):
            r0 = pl.multiple_of(a * BLOCK, BLOCK)
            scores = []
            for p in range(C_W // LANES):
                lanes = slice(p * LANES, (p + 1) * LANES)
                qp = q_ref[pl.ds(r0, BLOCK), lanes]
                for e in range(2):
                    scores.append(_dot_nt(jnp.where(half[e], qp, jnp.zeros_like(qp)), mk_ref[:, lanes]))
            m_cols, l_cols, probs = [], [], []
            for s in scores:
                m = jnp.max(s, axis=1, keepdims=True)
                pe = jnp.exp(s - m)
                probs.append(pe.astype(BF16))
                m_cols.append(m)
                l_cols.append(jnp.sum(pe, axis=1, keepdims=True))
            for p in range(C_W // LANES):
                lanes = slice(p * LANES, (p + 1) * LANES)
                o_h = [_dot(probs[2 * p + e], mv_ref[:, lanes]) * (1.0 / l_cols[2 * p + e]) for e in range(2)]
                o_ref[pl.ds(r0, BLOCK), lanes] = jnp.where(half[0], o_h[0], o_h[1]).astype(BF16)
            lse_ref[pl.ds(r0, BLOCK), :] = _per_head(m_cols) + jnp.log(_per_head(l_cols, 1.0))
            return carry

        lax.fori_loop(0, ns, sub, 0, unroll=True)

    row = lambda w: pl.BlockSpec((tq, w), lambda i: (i, 0))
    full = pl.BlockSpec((N_MEM, C_W), lambda i: (0, 0))
    return pl.pallas_call(
        body, name="mem_attn_fwd", grid=(seq // tq,), in_specs=[row(C_W), full, full],
        out_specs=[row(C_W), row(LANES)],
        out_shape=[jax.ShapeDtypeStruct((seq, C_W), BF16), jax.ShapeDtypeStruct((seq, LANES), F32)],
    )(q, mk, mv)


def _mem_attn_bwd(q, mk, mv, do, lse, delta):
    seq = q.shape[0]
    tq = min(512, seq)
    ns = tq // BLOCK
    npair = C_W // LANES

    def body(q_ref, mk_ref, mv_ref, do_ref, lse_ref, dl_ref, dq_ref, dmk_ref, dmv_ref, stat_l, stat_d, mkt, dqt):
        @pl.when(pl.program_id(0) == 0)
        def _():
            dmk_ref[...] = jnp.zeros_like(dmk_ref)
            dmv_ref[...] = jnp.zeros_like(dmv_ref)
            for p in range(npair):
                mkt[p] = mk_ref[:, p * LANES:(p + 1) * LANES].astype(F32).T.astype(BF16)

        for a in range(ns):
            rows = slice(a * BLOCK, (a + 1) * BLOCK)
            stat_l[a] = _rows_to_lanes(lse_ref[rows, :])
            stat_d[a] = _rows_to_lanes(dl_ref[rows, :])
        half = _half_masks(BLOCK)
        row = lax.broadcasted_iota(jnp.int32, (LANES, N_MEM), 0)
        row_half = (row < HEAD_DIM, row >= HEAD_DIM)

        for a in range(ns):
            rows = slice(a * BLOCK, (a + 1) * BLOCK)
            items = []
            for p in range(npair):
                lanes = slice(p * LANES, (p + 1) * LANES)
                qp = q_ref[rows, lanes]
                dop = do_ref[rows, lanes]
                for e in range(2):
                    qm = jnp.where(half[e], qp, jnp.zeros_like(qp))
                    dom = jnp.where(half[e], dop, jnp.zeros_like(dop))
                    items.append(dict(p=p, e=e, qm=qm, dom=dom, st=_dot_nt(mk_ref[:, lanes], qm),
                                      dpt=_dot_nt(mv_ref[:, lanes], dom)))
            for it in items:
                h = 2 * it["p"] + it["e"]
                pt = jnp.exp(it["st"] - stat_l[a, h:h + 1, :])
                it["ptb"] = pt.astype(BF16)
                it["dsb"] = (pt * (it["dpt"] - stat_d[a, h:h + 1, :])).astype(BF16)
            for p in range(npair):
                lanes = slice(p * LANES, (p + 1) * LANES)
                pair = [it for it in items if it["p"] == p]
                dmv_ref[:, lanes] += _dot(pair[0]["ptb"], pair[0]["dom"]) + _dot(pair[1]["ptb"], pair[1]["dom"])
                dmk_ref[:, lanes] += _dot(pair[0]["dsb"], pair[0]["qm"]) + _dot(pair[1]["dsb"], pair[1]["qm"])
                kbt = mkt[p]
                dqt[p, :, rows] = (_dot(jnp.where(row_half[0], kbt, jnp.zeros_like(kbt)), pair[0]["dsb"])
                                   + _dot(jnp.where(row_half[1], kbt, jnp.zeros_like(kbt)), pair[1]["dsb"]))
        for p in range(npair):
            dq_ref[:, p * LANES:(p + 1) * LANES] = dqt[p].T.astype(BF16)

    row = lambda w: pl.BlockSpec((tq, w), lambda i: (i, 0))
    full = pl.BlockSpec((N_MEM, C_W), lambda i: (0, 0))
    return pl.pallas_call(
        body, name="mem_attn_bwd", grid=(seq // tq,),
        in_specs=[row(C_W), full, full, row(C_W), row(LANES), row(LANES)], out_specs=[row(C_W), full, full],
        out_shape=[jax.ShapeDtypeStruct((seq, C_W), BF16), jax.ShapeDtypeStruct((N_MEM, C_W), F32),
                   jax.ShapeDtypeStruct((N_MEM, C_W), F32)],
        scratch_shapes=[pltpu.VMEM((ns, 8, LANES), F32)] * 2
        + [pltpu.VMEM((npair, LANES, N_MEM), BF16), pltpu.VMEM((npair, LANES, tq), F32)],
        compiler_params=pltpu.CompilerParams(dimension_semantics=("arbitrary",)),
    )(q, mk, mv, do, lse, delta)


def _silu_and_grad(g):
    s = 1.0 / (1.0 + jnp.exp(-g))
    return g * s, s * (1.0 + g * (1.0 - s))


def _post(x, target, post_norm, w_out, sink_row, oa, lse_a, ga, ob_list, lseb_list, gb, oc, gc):
    seq = x.shape[0]
    tm = min(512, seq)
    inv_d = 1.0 / D_MODEL
    nd = len(B_DILS)

    def body(*refs):
        (x_ref, t_ref, gp_ref, w_ref, sink_ref, oa_ref, lsea_ref, ga_ref), refs = refs[:8], refs[8:]
        ob_refs, lb_refs, (gb_ref, oc_ref, gc_ref), refs = refs[:nd], refs[nd:2 * nd], refs[2 * nd:2 * nd + 3], refs[2 * nd + 3:]
        (g_ref, doa_ref, dla_ref, dga_ref), refs = refs[:4], refs[4:]
        dob_refs, lsec_refs, dlb_refs, refs = refs[:nd], refs[nd:2 * nd], refs[2 * nd:3 * nd], refs[3 * nd:]
        (dgb_ref, doc_ref, dlc_ref, dgc_ref, gw_ref, gpost_ref, gsink_ref, loss_ref), refs = refs[:8], refs[8:]
        ycat, obufs, lbufs, st_do, st_l, st_d = refs[0], refs[1:nd], refs[nd:2 * nd - 1], refs[2 * nd - 1], refs[2 * nd], refs[2 * nd + 1]

        @pl.when(pl.program_id(0) == 0)
        def _():
            gw_ref[...] = jnp.zeros_like(gw_ref)
            gpost_ref[...] = jnp.zeros_like(gpost_ref)
            gsink_ref[...] = jnp.zeros_like(gsink_ref)
            loss_ref[...] = jnp.zeros_like(loss_ref)

        o_i, l_i = [ob_refs[0][0].astype(F32)], [lb_refs[0][0]]
        for k in range(1, nd):
            _from_residues(ob_refs[k], obufs[k - 1], B_DILS[k])
            _from_residues(lb_refs[k], lbufs[k - 1], B_DILS[k])
            o_i.append(_stage_read(obufs[k - 1]))
            l_i.append(_stage_read(lbufs[k - 1]))
        mx = l_i[0]
        for l in l_i[1:]:
            mx = jnp.maximum(mx, l)
        w_i = [jnp.exp(l - mx) for l in l_i]
        z = w_i[0]
        for w in w_i[1:]:
            z = z + w
        _stage_write(st_l, mx + jnp.log(z))
        expand = _head_expand_matrix(B_W)
        inv_z = 1.0 / z
        ob = None
        for w, o in zip(w_i, o_i):
            term = _dot_split(w * inv_z, expand, 2) * o
            ob = term if ob is None else ob + term
        oa, oc = oa_ref[...].astype(F32), oc_ref[...].astype(F32)
        sa, dsa = _silu_and_grad(ga_ref[...].astype(F32))
        sb, dsb = _silu_and_grad(gb_ref[...].astype(F32))
        sc, dsc = _silu_and_grad(gc_ref[...].astype(F32))
        ycat[:, 0:A_W] = (oa * sa).astype(BF16)
        ycat[:, A_W:A_W + B_W] = (ob * sb).astype(BF16)
        ycat[:, A_W + B_W:] = (oc * sc).astype(BF16)
        y2 = _dot(ycat[...], w_ref[...])
        r = lax.rsqrt(jnp.mean(y2 * y2, axis=-1, keepdims=True) + RMS_EPS)
        zhat = y2 * r
        gp = gp_ref[...]
        err = x_ref[...] + zhat * gp - t_ref[...]
        loss_ref[...] += jnp.sum(err * err) * (0.5 * inv_d)
        g = err * inv_d
        g_ref[...] = g
        gpost_ref[...] += jnp.sum(g * zhat, axis=0, keepdims=True)
        a = g * gp
        dy2 = (r * (a - zhat * jnp.mean(a * zhat, axis=-1, keepdims=True))).astype(BF16)
        gw_ref[...] += _dot_tn(ycat[...], dy2)
        dycat = _dot_nt(dy2, w_ref[...])
        dya, dyb, dyc = dycat[:, 0:A_W], dycat[:, A_W:A_W + B_W], dycat[:, A_W + B_W:]
        doa, dob, doc = dya * sa, dyb * sb, dyc * sc
        doa_ref[...] = doa.astype(BF16)
        doc_ref[...] = doc.astype(BF16)
        dga_ref[...] = (dya * oa * dsa).astype(BF16)
        dgb_ref[...] = (dyb * ob * dsb).astype(BF16)
        dgc_ref[...] = (dyc * oc * dsc).astype(BF16)
        dl_a = _dot_split(doa * oa, _head_sum_matrix(A_W), 2)
        dla_ref[...] = dl_a
        dlc_ref[...] = _dot_split(doc * oc, _head_sum_matrix(C_W), 2)
        gsink_ref[...] += jnp.sum(jnp.exp(sink_ref[...] - lsea_ref[...]) * dl_a, axis=0, keepdims=True)
        _stage_write(st_do, dob)
        _stage_write(st_d, _dot_split(dob * ob, _head_sum_matrix(B_W), 2))
        for k, dil in enumerate(B_DILS):
            _to_residues(st_do, dob_refs[k], dil)
            _to_residues(st_l, lsec_refs[k], dil)
            _to_residues(st_d, dlb_refs[k], dil)

    row = lambda w: pl.BlockSpec((tm, w), lambda i: (i, 0))
    full = lambda shape: pl.BlockSpec(shape, lambda i: (0,) * len(shape))
    res_specs = lambda w: [_residue_spec(d, tm, w) for d in B_DILS]
    res_shapes = lambda w, dt: [jax.ShapeDtypeStruct((d, seq // d, w), dt) for d in B_DILS]
    ins = [x, target, post_norm, w_out, sink_row, oa, lse_a, ga, *ob_list, *lseb_list, gb, oc, gc]
    in_specs = ([row(D_MODEL), row(D_MODEL), full((1, D_MODEL)), full((D_MODEL, D_MODEL)), full((1, LANES)),
                 row(A_W), row(LANES), row(A_W)] + res_specs(B_W) + res_specs(LANES) + [row(B_W), row(C_W), row(C_W)])
    out_shape = ([jax.ShapeDtypeStruct((seq, D_MODEL), F32), jax.ShapeDtypeStruct((seq, A_W), BF16),
                  jax.ShapeDtypeStruct((seq, LANES), F32), jax.ShapeDtypeStruct((seq, A_W), BF16)]
                 + res_shapes(B_W, BF16) + res_shapes(LANES, F32) + res_shapes(LANES, F32)
                 + [jax.ShapeDtypeStruct((seq, B_W), BF16), jax.ShapeDtypeStruct((seq, C_W), BF16),
                    jax.ShapeDtypeStruct((seq, LANES), F32), jax.ShapeDtypeStruct((seq, C_W), BF16),
                    jax.ShapeDtypeStruct((D_MODEL, D_MODEL), F32), jax.ShapeDtypeStruct((1, D_MODEL), F32),
                    jax.ShapeDtypeStruct((1, LANES), F32), jax.ShapeDtypeStruct((1, LANES), F32)])
    out_specs = ([row(D_MODEL), row(A_W), row(LANES), row(A_W)] + res_specs(B_W) + res_specs(LANES) + res_specs(LANES)
                 + [row(B_W), row(C_W), row(LANES), row(C_W),
                    full((D_MODEL, D_MODEL)), full((1, D_MODEL)), full((1, LANES)), full((1, LANES))])
    scratch = ([pltpu.VMEM((tm, D_MODEL), BF16)] + [_stage(tm, B_W)] * (nd - 1) + [_stage(tm, LANES)] * (nd - 1)
               + [_stage(tm, B_W), _stage(tm, LANES), _stage(tm, LANES)])
    res = pl.pallas_call(
        body, name="post", grid=(seq // tm,), in_specs=in_specs, out_specs=out_specs, out_shape=out_shape,
        scratch_shapes=scratch,
        compiler_params=pltpu.CompilerParams(dimension_semantics=("arbitrary",)),
    )(*ins)
    out = dict(g=res[0], doa=res[1], dl_a=res[2], dga=res[3], dob=res[4:4 + nd], lse_b=res[4 + nd:4 + 2 * nd],
               dl_b=res[4 + 2 * nd:4 + 3 * nd])
    rest = res[4 + 3 * nd:]
    out.update(dgb=rest[0], doc=rest[1], dl_c=rest[2], dgc=rest[3], gw_out=rest[4], gpost=rest[5], gsink=rest[6],
               loss=rest[7])
    return out


def _grad_w_in(ut, nat, res):
    seq = ut.shape[1]
    tm = min(512, seq)
    nd = len(B_DILS)
    nat_list = [nat[n] for n in _NATURAL]
    res_list = [a for n in _DILATED for a in res[n]]
    rope = _rope_tables(seq, tm)

    def body(rl_ref, rb_ref, ut_ref, *refs):
        nat_refs = dict(zip(_NATURAL, refs[:len(_NATURAL)]))
        refs = refs[len(_NATURAL):]
        res_refs = {n: refs[nd * k:nd * (k + 1)] for k, n in enumerate(_DILATED)}
        refs = refs[nd * len(_DILATED):]
        dproj_ref, gw_ref = refs[:2]
        bufs = {n: refs[2 + (nd - 1) * k:2 + (nd - 1) * (k + 1)] for k, n in enumerate(_DILATED)}

        @pl.when(pl.program_id(0) == 0)
        def _():
            gw_ref[...] = jnp.zeros_like(gw_ref)

        for n in _DILATED:
            for k in range(1, nd):
                _from_residues(res_refs[n][k], bufs[n][k - 1], B_DILS[k])
        c, sm, sp = _rope_coeffs(rl_ref, rb_ref)
        sm, sp = -sm, -sp
        for blk, (name, off, roped, scaled) in enumerate(_PROJ_LAYOUT):
            lanes = slice(off, off + LANES)
            if name in nat_refs:
                piece = nat_refs[name][:, lanes].astype(F32)
            else:
                piece = res_refs[name][0][0, :, lanes].astype(F32)
                for buf in bufs[name]:
                    piece = piece + buf[off // LANES]
            if roped:
                piece = _rope(piece, c, sm, sp)
            if scaled:
                piece = piece * SCALE
            dproj_ref[:, blk * LANES:(blk + 1) * LANES] = piece.astype(BF16)
        for j in range(N_CHIPS):
            gw_ref[j] += _dot(ut_ref[...], dproj_ref[:, j * SHARD_IN:(j + 1) * SHARD_IN])

    row = lambda w: pl.BlockSpec((tm, w), lambda i: (i, 0))
    in_specs = ([pl.BlockSpec(rope[0].shape, lambda i: (0, 0)), pl.BlockSpec((8, 2 * LANES), lambda i: (i, 0)),
                 pl.BlockSpec((D_MODEL, tm), lambda i: (0, i))]
                + [row(a.shape[1]) for a in nat_list]
                + [_residue_spec(d, tm, B_W) for _ in _DILATED for d in B_DILS])
    return pl.pallas_call(
        body, name="grad_w_in", grid=(seq // tm,), in_specs=in_specs,
        out_specs=[row(D_IN), pl.BlockSpec((N_CHIPS, D_MODEL, SHARD_IN), lambda i: (0, 0, 0))],
        out_shape=[jax.ShapeDtypeStruct((seq, D_IN), BF16), jax.ShapeDtypeStruct((N_CHIPS, D_MODEL, SHARD_IN), F32)],
        scratch_shapes=[_stage(tm, B_W)] * ((nd - 1) * len(_DILATED)),
        compiler_params=pltpu.CompilerParams(dimension_semantics=("arbitrary",)),
    )(*rope, ut, *nat_list, *res_list)


def _input_grad(x, g, pre_norm, w_in_g, dproj, gx_prev, span, after, name):
    seq = x.shape[0]
    tm = seq // 16
    first_block, steps = span

    def body(*refs):
        x_ref, g_ref, gp_ref, w_ref, dp_ref = refs[:5]
        gx_ref, gpre_ref = refs[-2:]

        @pl.when(pl.program_id(0) == 0)
        def _():
            gpre_ref[...] = jnp.zeros_like(gpre_ref)

        du = None
        for j in range(N_CHIPS):
            term = _dot_nt(dp_ref[:, j * SHARD_IN:(j + 1) * SHARD_IN], w_ref[j])
            du = term if du is None else du + term
        xv = x_ref[...]
        r = lax.rsqrt(jnp.mean(xv * xv, axis=-1, keepdims=True) + RMS_EPS)
        xhat = xv * r
        gpre_ref[...] += jnp.sum(du * xhat, axis=0, keepdims=True)
        a = du * gp_ref[...]
        gx_ref[...] = g_ref[...] + r * (a - xhat * jnp.mean(a * xhat, axis=-1, keepdims=True))

    row = lambda w: pl.BlockSpec((tm, w), lambda i: (first_block + i, 0))
    full = lambda a: pl.BlockSpec(a.shape, lambda i: (0,) * a.ndim)
    any_spec = pl.BlockSpec(memory_space=pl.ANY)
    ins = [x, g, pre_norm, w_in_g, dproj]
    in_specs = [row(D_MODEL), row(D_MODEL), full(pre_norm), full(w_in_g), row(D_IN)]
    aliases = {}
    if gx_prev is not None:
        aliases[len(ins)] = 0
        ins.append(gx_prev)
        in_specs.append(any_spec)
    if after is not None:
        ins.append(after)
        in_specs.append(any_spec)
    return pl.pallas_call(
        body, name=name, grid=(steps,), in_specs=in_specs,
        out_specs=[row(D_MODEL), pl.BlockSpec((1, D_MODEL), lambda i: (0, 0))],
        out_shape=[jax.ShapeDtypeStruct((seq, D_MODEL), F32), jax.ShapeDtypeStruct((1, D_MODEL), F32)],
        input_output_aliases=aliases,
        compiler_params=pltpu.CompilerParams(dimension_semantics=("arbitrary",)),
    )(*ins)


def _exchange_start(ex, name):
    n_in, n_out, n_sem = len(ex["ins"]), len(ex["outs"]), len(ex["sems"])

    def body(*refs):
        in_refs, land_refs, sems = refs[:n_in], refs[n_in:n_in + n_out], refs[n_in + n_out:n_in + n_out + n_sem]
        ex["start"](in_refs, land_refs, *sems)
        token = refs[-1]
        token[...] = jnp.zeros_like(token)

    hbm = pl.BlockSpec(memory_space=pltpu.HBM)
    sem = pl.BlockSpec(memory_space=pltpu.SEMAPHORE)
    ins = [pltpu.with_memory_space_constraint(a, pltpu.HBM) for a in ex["ins"]]
    landing = [pltpu.with_memory_space_constraint(lax.empty(o.shape, o.dtype), pltpu.HBM) for o in ex["outs"]]
    res = pl.pallas_call(
        body, name=name,
        out_shape=list(ex["sems"]) + [pltpu.HBM(a.shape, a.dtype) for a in ex["ins"]]
        + [pltpu.HBM(o.shape, o.dtype) for o in ex["outs"]] + [jax.ShapeDtypeStruct((8, LANES), F32)],
        in_specs=[hbm] * (n_in + n_out),
        out_specs=[sem] * n_sem + [hbm] * (n_in + n_out) + [pl.BlockSpec(memory_space=pltpu.VMEM)],
        input_output_aliases={k: n_sem + k for k in range(n_in + n_out)},
        compiler_params=pltpu.CompilerParams(has_side_effects=pltpu.SideEffectType.DATAFLOW_SIDE_EFFECTING),
    )(*ins, *landing)
    return res[:-1], res[-1]


def _exchange_wait(ex, handles, after, name):
    n_in, n_out, n_sem = len(ex["ins"]), len(ex["outs"]), len(ex["sems"])
    sems, thru = handles[:n_sem], handles[n_sem:]

    def body(*refs):
        in_refs, land_refs = refs[:n_in], refs[n_in:n_in + n_out]
        sem_refs = refs[n_in + n_out:n_in + n_out + n_sem]
        ex["finish"](in_refs, land_refs, *sem_refs)

    hbm = pl.BlockSpec(memory_space=pltpu.HBM)
    sem = pl.BlockSpec(memory_space=pltpu.SEMAPHORE)
    res = pl.pallas_call(
        body, name=name,
        out_shape=[pltpu.HBM(a.shape, a.dtype) for a in thru],
        in_specs=[hbm] * (n_in + n_out) + [sem] * n_sem + [pl.BlockSpec(memory_space=pl.ANY)],
        out_specs=[hbm] * (n_in + n_out),
        input_output_aliases={k: k for k in range(n_in + n_out)},
        compiler_params=pltpu.CompilerParams(has_side_effects=pltpu.SideEffectType.DATAFLOW_SIDE_EFFECTING),
    )(*thru, *sems, after)
    return res[n_in:]


def _start_finish(build):
    def start(*refs):
        for cp in build(*refs):
            cp.start()

    def finish(*refs):
        for cp in build(*refs):
            cp.wait()

    return dict(start=start, finish=finish)


def _pair_exchange(grads):
    n = len(grads)

    def build(srcs, outs, send_sems, recv_sems):
        x, y, c = lax.axis_index("x"), lax.axis_index("y"), lax.axis_index("c")
        copies = []
        for t in range(n):
            rows = grads[t].shape[1] // 2
            copies.append(pltpu.make_async_remote_copy(
                src_ref=srcs[t].at[:, pl.ds((1 - c) * rows, rows)], dst_ref=outs[t],
                send_sem=send_sems.at[t], recv_sem=recv_sems.at[t], device_id=(x, y, 1 - c), device_id_type=MESH))
        return copies

    return dict(ins=list(grads), **_start_finish(build),
                outs=[jax.ShapeDtypeStruct((g.shape[0], g.shape[1] // 2, g.shape[2]), g.dtype) for g in grads],
                sems=[pltpu.SemaphoreType.DMA((n,)), pltpu.SemaphoreType.DMA((n,))])


def _pair_add(core, own, got):
    nchip, rows2, width = own.shape
    rows = rows2 // 2
    tr = min(512, rows)
    nb = rows // tr

    def body(core_ref, own_ref, got_ref, out_ref):
        out_ref[...] = (own_ref[...] + got_ref[...]).astype(BF16)

    grid_spec = pltpu.PrefetchScalarGridSpec(
        num_scalar_prefetch=1, grid=(nchip, nb),
        in_specs=[pl.BlockSpec((None, tr, width), lambda k, i, core_ref: (k, core_ref[0] * nb + i, 0)),
                  pl.BlockSpec((None, tr, width), lambda k, i, core_ref: (k, i, 0))],
        out_specs=pl.BlockSpec((None, tr, width), lambda k, i, core_ref: (k, i, 0)))
    return pl.pallas_call(
        body, name=f"pair_add_{width}", grid_spec=grid_spec,
        out_shape=jax.ShapeDtypeStruct((nchip, rows, width), BF16),
    )(core, own, got)


def _chip_exchange(parts):
    n = len(parts)

    def build(srcs, outs, send_sems, recv_sems, local_sems):
        x, y, c = lax.axis_index("x"), lax.axis_index("y"), lax.axis_index("c")
        my_chip = 2 * x + y
        chips = [(1 - x, y), (x, 1 - y), (1 - x, 1 - y)]
        copies = [pltpu.make_async_copy(srcs[t].at[my_chip], outs[t].at[my_chip], local_sems.at[t]) for t in range(n)]
        for j, (cx, cy) in enumerate(chips):
            for t in range(n):
                k = n * j + t
                copies.append(pltpu.make_async_remote_copy(
                    src_ref=srcs[t].at[2 * cx + cy], dst_ref=outs[t].at[my_chip], send_sem=send_sems.at[k],
                    recv_sem=recv_sems.at[k], device_id=(cx, cy, c), device_id_type=MESH))
        return copies

    return dict(ins=list(parts), **_start_finish(build), outs=[jax.ShapeDtypeStruct(p.shape, p.dtype) for p in parts],
                sems=[pltpu.SemaphoreType.DMA((3 * n,)), pltpu.SemaphoreType.DMA((3 * n,)),
                      pltpu.SemaphoreType.DMA((n,))])


def _slot_sum(slots, name, core=None):
    ns, rows, width = slots.shape
    tr = min(512, rows)

    def body(*refs):
        in_ref, out_ref = refs[-2:]
        acc = in_ref[0].astype(F32)
        for s in range(1, ns):
            acc = acc + in_ref[s].astype(F32)
        out_ref[...] = acc

    if core is None:
        return pl.pallas_call(
            body, name=name, grid=(rows // tr,),
            in_specs=[pl.BlockSpec((ns, tr, width), lambda i: (0, i, 0))],
            out_specs=pl.BlockSpec((tr, width), lambda i: (i, 0)),
            out_shape=jax.ShapeDtypeStruct((rows, width), F32),
        )(slots)
    grid_spec = pltpu.PrefetchScalarGridSpec(
        num_scalar_prefetch=1, grid=(rows // tr,),
        in_specs=[pl.BlockSpec((ns, tr, width), lambda i, core_ref: (0, i, 0))],
        out_specs=pl.BlockSpec((None, tr, width), lambda i, core_ref: (core_ref[0], i, 0)))
    return pl.pallas_call(
        body, name=name, grid_spec=grid_spec, out_shape=jax.ShapeDtypeStruct((2, rows, width), F32),
    )(core, slots)


def _pair_gather(bufs, small):
    n = len(bufs)

    def body(*refs):
        small_ref, outs, small_out = refs[n], refs[n + 1:2 * n + 1], refs[2 * n + 1]
        send_sems, recv_sems, local_sem = refs[2 * n + 2:]
        x, y, c = lax.axis_index("x"), lax.axis_index("y"), lax.axis_index("c")
        me = 4 * x + 2 * y + c
        chips = [(1 - x, y), (x, 1 - y), (1 - x, 1 - y)]
        mine = pltpu.make_async_copy(small_ref, small_out.at[me], local_sem)
        mine.start()
        copies = [pltpu.make_async_remote_copy(
            src_ref=outs[t].at[c], dst_ref=outs[t].at[c], send_sem=send_sems.at[t], recv_sem=recv_sems.at[t],
            device_id=(x, y, 1 - c), device_id_type=MESH) for t in range(n)]
        peers = [(x, y, 1 - c)] + [(cx, cy, cc) for (cx, cy) in chips for cc in (c, 1 - c)]
        for j, peer in enumerate(peers):
            copies.append(pltpu.make_async_remote_copy(
                src_ref=small_ref, dst_ref=small_out.at[me], send_sem=send_sems.at[n + j],
                recv_sem=recv_sems.at[n + j], device_id=peer, device_id_type=MESH))
        for cp in copies:
            cp.start()
        for cp in copies:
            cp.wait()
        mine.wait()

    any_spec = pl.BlockSpec(memory_space=pl.ANY)
    res = pl.pallas_call(
        body, name="pair_gather",
        out_shape=[jax.ShapeDtypeStruct(b.shape, b.dtype) for b in bufs]
        + [jax.ShapeDtypeStruct((8,) + small.shape, small.dtype)],
        in_specs=[any_spec] * (n + 1), out_specs=[any_spec] * (n + 1),
        input_output_aliases={t: t for t in range(n)},
        scratch_shapes=[pltpu.SemaphoreType.DMA((n + 7,)), pltpu.SemaphoreType.DMA((n + 7,)),
                        pltpu.SemaphoreType.DMA],
    )(*bufs, small)
    return [r.reshape(2 * b.shape[1], b.shape[2]) for r, b in zip(res[:n], bufs)], res[n]


def _adamw(w, g, m, v, name):
    rows, width = w.shape
    tr = min(256, rows)
    c1 = 1.0 / (1.0 - ADAM_B1 ** ADAM_STEP)
    c2 = 1.0 / (1.0 - ADAM_B2 ** ADAM_STEP)

    def body(w_ref, g_ref, m_ref, v_ref, d_ref, nm_ref, nv_ref):
        gv = g_ref[...]
        nm = ADAM_B1 * m_ref[...] + (1.0 - ADAM_B1) * gv
        nv = ADAM_B2 * v_ref[...] + (1.0 - ADAM_B2) * (gv * gv)
        nm_ref[...] = nm
        nv_ref[...] = nv
        d_ref[...] = -ADAM_LR * ((nm * c1) / (jnp.sqrt(nv * c2) + ADAM_EPS) + ADAM_WD * w_ref[...])

    spec = pl.BlockSpec((tr, width), lambda i: (i, 0))
    return pl.pallas_call(
        body, name=name, grid=(rows // tr,), in_specs=[spec] * 4, out_specs=[spec] * 3,
        out_shape=[jax.ShapeDtypeStruct(w.shape, F32)] * 3,
    )(w, g, m, v)


def _local_step(x, mem, target, pre_norm, sink_a, mem_norm, post_norm, w_in_g, w_out, w_mkv, gathers=None):
    first_gather, late_gather = gathers if gathers else (None, None)
    u, ut, hosted = _pre_norm(x, pre_norm, first_gather)
    if gathers:
        w_in_g = hosted[0].reshape(N_CHIPS, D_MODEL, SHARD_IN)
    pr = _pre_proj(u, w_in_g, late_gather)
    pr["ut"] = ut
    if gathers:
        w_out, w_mkv = (g.reshape(D_MODEL, g.shape[-1]) for g in pr["hosted"])
    mk, mv = _mem_kv(mem, mem_norm, w_mkv)
    sink = sink_a.reshape(-1)
    qa, ka, va = pr["qa"][None], pr["ka"][None], pr["va"][None]
    oa, lse_a = _band_fwd(qa, ka, va, sink, max_dist=A_WINDOW - 1, name="swa_fwd")
    ob_list, lseb_list = [], []
    for k, (win, dil) in enumerate(B_CONFIGS):
        o_i, l_i = _band_fwd(pr["qb"][k], pr["kb"][k], pr["vb"][k], None, max_dist=win // dil, name=f"dil{dil}_fwd")
        ob_list.append(o_i)
        lseb_list.append(l_i)
    oc, lse_c = _mem_attn_fwd(pr["qc"], mk, mv)
    sink_row = jnp.pad(sink, (0, LANES - sink.shape[0])).reshape(1, LANES)
    po = _post(x, target, post_norm, w_out, sink_row, oa[0], lse_a[0], pr["ga"], ob_list, lseb_list, pr["gb"], oc,
               pr["gc"])
    dqc, dmk, dmv = _mem_attn_bwd(pr["qc"], mk, mv, po["doc"], lse_c, po["dl_c"])
    dqa, dka, dva = _band_bwd(qa, ka, va, po["doa"][None], lse_a, po["dl_a"][None], max_dist=A_WINDOW - 1,
                              name="swa_bwd")
    res = dict(qb=[], kb=[], vb=[])
    for k, (win, dil) in enumerate(B_CONFIGS):
        dq_i, dk_i, dv_i = _band_bwd(pr["qb"][k], pr["kb"][k], pr["vb"][k], po["dob"][k], po["lse_b"][k],
                                     po["dl_b"][k], max_dist=win // dil, name=f"dil{dil}_bwd")
        res["qb"].append(dq_i)
        res["kb"].append(dk_i)
        res["vb"].append(dv_i)
    nat = dict(qa=dqa[0], ka=dka[0], va=dva[0], ga=po["dga"], gb=po["dgb"], qc=dqc, gc=po["dgc"])
    dproj, gw_in = _grad_w_in(pr["ut"], nat, res)
    gw_mkv, gmem = _mem_kv_bwd(mem, mem_norm, w_mkv, dmk, dmv)
    gsink = -po["gsink"][0, :sink.shape[0]]
    return dict(loss=po["loss"], g=po["g"], dproj=dproj, gw_in=gw_in, gw_out=po["gw_out"], gw_mkv=gw_mkv,
                gpost=po["gpost"], gmem=gmem, gsink=gsink, w_in_g=w_in_g)


def kernel(x, mem, pre_norm, w_in, sink_a, mem_norm, w_mem_kv, w_out, post_norm, loss_target, m_pre_norm, m_w_in, m_sink_a, m_mem_norm, m_w_mem_kv, m_w_out, m_post_norm, v_pre_norm, v_w_in, v_sink_a, v_mem_norm, v_w_mem_kv, v_w_out, v_post_norm):
    gathers = (_gather_exchange([w_in[0].astype(BF16)]),
               _gather_exchange([w_out[0].astype(BF16), w_mem_kv[0].astype(BF16)]))
    loc = _local_step(x[0], mem[0], loss_target[0], pre_norm, sink_a, mem_norm, post_norm, None, None, None, gathers)
    big = [loc["gw_in"], loc["gw_out"].reshape(N_CHIPS, D_MODEL // N_CHIPS, D_MODEL),
           loc["gw_mkv"].reshape(N_CHIPS, D_MODEL // N_CHIPS, 2 * C_W)]
    core = lax.axis_index("c").astype(jnp.int32).reshape(1)
    w_in_full = loc["w_in_g"]
    step_in = (x[0], loc["g"], pre_norm, w_in_full, loc["dproj"])
    pair_ex = _pair_exchange(big)
    pair_handles, token = _exchange_start(pair_ex, "pair_exchange_start")
    gx_a, gpre_a = _input_grad(*step_in, None, (0, 3), token, "input_grad_a")
    got = _exchange_wait(pair_ex, pair_handles, gpre_a, "pair_exchange_wait")
    parts = [_pair_add(core, own, g) for own, g in zip(big, got)]
    chip_ex = _chip_exchange(parts)
    chip_handles, token = _exchange_start(chip_ex, "chip_exchange_start")
    grad_x, gpre_b = _input_grad(*step_in, gx_a, (3, 13), token, "input_grad_b")
    slots = _exchange_wait(chip_ex, chip_handles, gpre_b, "chip_exchange_wait")
    halves = [_slot_sum(s, name=f"chip_sum_{s.shape[2]}", core=core) for s in slots]
    widen = lambda a: jnp.pad(a.reshape(1, -1), ((0, 0), (0, D_MODEL - a.size)))
    small = jnp.concatenate([gpre_a, loc["gpost"], loc["gmem"], widen(loc["gsink"]), widen(loc["loss"]), gpre_b,
                             jnp.zeros((2, D_MODEL), F32)], axis=0)
    (g_in, g_out, g_mkv), small_slots = _pair_gather(halves, small)
    small_sum = _slot_sum(small_slots, name="device_sum")
    g_pre, g_post, g_mem = small_sum[0:1] + small_sum[5:6], small_sum[1:2], small_sum[2:3]
    g_sink = small_sum[3:4, :sink_a.shape[1]]
    loss = small_sum[4, 0]

    d_in, nm_in, nv_in = _adamw(w_in[0], g_in, m_w_in[0], v_w_in[0], "adamw_in")
    d_out, nm_out, nv_out = _adamw(w_out[0], g_out, m_w_out[0], v_w_out[0], "adamw_out")
    d_mkv, nm_mkv, nv_mkv = _adamw(w_mem_kv[0], g_mkv, m_w_mem_kv[0], v_w_mem_kv[0], "adamw_mkv")
    pad6 = lambda a: jnp.pad(a, ((0, 0), (0, D_MODEL - a.shape[1])))
    stack = lambda a, b, c_, d_: jnp.concatenate([a, b, c_, pad6(d_), jnp.zeros((4, D_MODEL), F32)], axis=0)
    d_s, nm_s, nv_s = _adamw(stack(pre_norm, post_norm, mem_norm, sink_a),
                             jnp.concatenate([g_pre, small_sum[1:]], axis=0),
                             stack(m_pre_norm, m_post_norm, m_mem_norm, m_sink_a),
                             stack(v_pre_norm, v_post_norm, v_mem_norm, v_sink_a), "adamw_small")
    ns_ = sink_a.shape[1]
    unpack = lambda a: (a[0:1], a[3:4, :ns_], a[2:3], a[1:2])
    d_pre, d_sink, d_mem, d_post = unpack(d_s)
    nm_pre, nm_sink, nm_mem, nm_post = unpack(nm_s)
    nv_pre, nv_sink, nv_mem, nv_post = unpack(nv_s)
    lead = lambda a: a[None]
    return (loss, lead(grad_x),
            g_pre, lead(g_in), g_sink, g_mem, lead(g_mkv), lead(g_out), g_post,
            d_pre, lead(d_in), d_sink, d_mem, lead(d_mkv), lead(d_out), d_post,
            nm_pre, lead(nm_in), nm_sink, nm_mem, lead(nm_mkv), lead(nm_out), nm_post,
            nv_pre, lead(nv_in), nv_sink, nv_mem, lead(nv_mkv), lead(nv_out), nv_post)
```

```python
import numpy as np
import jax
import jax.numpy as jnp
from jax import lax
from jax.experimental import pallas as pl
from jax.experimental.pallas import tpu as pltpu

F32 = jnp.float32
BF16 = jnp.bfloat16

D_MODEL = 1024
HEAD_DIM = 64
LANES = 128
BLOCK = 128
A_W, A_KV_W, B_W, C_W = 384, 128, 384, 256
N_MEM = 256
D_IN = 3072
N_CHIPS = 4
SHARD_IN = D_IN // N_CHIPS
B_CONFIGS = ((128, 1), (512, 4), (2048, 16))
B_DILS = tuple(d for _, d in B_CONFIGS)
A_WINDOW = 128
RMS_EPS = 1e-6
ROPE_THETA = 500000.0
SCALE = HEAD_DIM ** -0.5
NEG = -1e30
ADAM_LR, ADAM_B1, ADAM_B2, ADAM_EPS, ADAM_WD, ADAM_STEP = 0.001, 0.9, 0.999, 1e-08, 0.01, 10

NT = (((1,), (1,)), ((), ()))
TN = (((0,), (0,)), ((), ()))
MESH = pl.DeviceIdType.MESH

_PROJ_LAYOUT = (
    [("qa", 128 * i, True, True) for i in range(3)] + [("ka", 0, True, False), ("va", 0, False, False)]
    + [("ga", 128 * i, False, False) for i in range(3)]
    + [("qb", 128 * i, True, True) for i in range(3)] + [("kb", 128 * i, True, False) for i in range(3)]
    + [("vb", 128 * i, False, False) for i in range(3)] + [("gb", 128 * i, False, False) for i in range(3)]
    + [("qc", 128 * i, False, True) for i in range(2)] + [("gc", 128 * i, False, False) for i in range(2)]
)
_PROJ_WIDTH = dict(qa=A_W, ka=A_KV_W, va=A_KV_W, ga=A_W, qb=B_W, kb=B_W, vb=B_W, gb=B_W, qc=C_W, gc=C_W)
_NATURAL = ("qa", "ka", "va", "ga", "gb", "qc", "gc")
_DILATED = ("qb", "kb", "vb")


def _dot(a, b):
    return jnp.dot(a, b, preferred_element_type=F32)


def _dot_nt(a, b):
    return lax.dot_general(a, b, NT, preferred_element_type=F32)


def _dot_tn(a, b):
    return lax.dot_general(a, b, TN, preferred_element_type=F32)


def _half_masks(rows):
    lane = lax.broadcasted_iota(jnp.int32, (rows, LANES), 1)
    return lane < HEAD_DIM, lane >= HEAD_DIM


def _rope(t, c, sm, sp):
    return t * c + pltpu.roll(t, LANES - 8, 1) * sm + pltpu.roll(t, 8, 1) * sp


def _rope_tables(seq, tm):
    dim = jnp.arange(LANES) % HEAD_DIM
    inv_freq = ROPE_THETA ** (-jnp.arange(0, 16, 2, dtype=F32) / 16)
    freq = jnp.where(dim < 16, inv_freq[dim % 8], 0.0)[None, :]
    local = jnp.arange(tm, dtype=F32)[:, None] * freq
    base = (jnp.arange(seq // tm, dtype=F32) * tm)[:, None] * freq
    both = lambda a: jnp.concatenate([jnp.cos(a), jnp.sin(a)], axis=1)
    return both(local), jnp.repeat(both(base), 8, axis=0)


def _rope_coeffs(local_ref, base_ref):
    cl, sl = local_ref[:, :LANES], local_ref[:, LANES:]
    cb, sb = base_ref[0:1, :LANES], base_ref[0:1, LANES:]
    cos = cb * cl - sb * sl
    sin = sb * cl + cb * sl
    dim = lax.broadcasted_iota(jnp.int32, (1, LANES), 1) % HEAD_DIM
    return cos, jnp.where(dim < 8, -sin, 0.0), jnp.where((dim >= 8) & (dim < 16), sin, 0.0)


def _split3(x):
    a = x.astype(BF16)
    r = x - a.astype(F32)
    b = r.astype(BF16)
    c = (r - b.astype(F32)).astype(BF16)
    return a, b, c


def _rows_to_lanes(x):
    row = lax.broadcasted_iota(jnp.int32, (8, LANES), 0)
    lane = lax.broadcasted_iota(jnp.int32, (8, LANES), 1)
    eye = (row == lane).astype(BF16)
    a, b, c = _split3(x)
    return _dot_nt(eye, a) + _dot_nt(eye, b) + _dot_nt(eye, c)


def _head_sum_matrix(width):
    k = lax.broadcasted_iota(jnp.int32, (width, LANES), 0)
    h = lax.broadcasted_iota(jnp.int32, (width, LANES), 1)
    return (k // HEAD_DIM == h).astype(BF16)


def _head_expand_matrix(width):
    h = lax.broadcasted_iota(jnp.int32, (LANES, width), 0)
    k = lax.broadcasted_iota(jnp.int32, (LANES, width), 1)
    return (k // HEAD_DIM == h).astype(BF16)


def _dot_split(x, mat, terms):
    parts = _split3(x)[:terms]
    out = _dot(parts[0], mat)
    for p in parts[1:]:
        out = out + _dot(p, mat)
    return out


def _per_head(cols, fill=0.0):
    rows = cols[0].shape[0]
    lane = lax.broadcasted_iota(jnp.int32, (rows, LANES), 1)
    out = jnp.full((rows, LANES), fill, F32)
    for h, col in enumerate(cols):
        out = jnp.where(lane == h, col, out)
    return out


def _lane_blocks(width):
    return [slice(p * LANES, (p + 1) * LANES) for p in range(width // LANES)]


def _stage(rows, width):
    return pltpu.VMEM((width // LANES, rows, LANES), F32)


def _stage_write(buf, value):
    for p, lanes in enumerate(_lane_blocks(value.shape[1])):
        buf[p] = value[:, lanes]


def _stage_read(buf):
    return jnp.concatenate([buf[p] for p in range(buf.shape[0])], axis=1) if buf.shape[0] > 1 else buf[0]


def _to_residues(buf, out_ref, dil):
    rows = buf.shape[1] // dil
    for r in range(dil):
        for p in range(buf.shape[0]):
            plane = buf.at[p]
            out_ref[r, :, p * LANES:(p + 1) * LANES] = plane[pl.ds(r, rows, stride=dil), :].astype(out_ref.dtype)


def _from_residues(in_ref, buf, dil):
    rows = buf.shape[1] // dil
    for r in range(dil):
        for p in range(buf.shape[0]):
            plane = buf.at[p]
            plane[pl.ds(r, rows, stride=dil), :] = in_ref[r, :, p * LANES:(p + 1) * LANES].astype(F32)


def _residue_spec(dil, tm, width):
    return pl.BlockSpec((dil, tm // dil, width), lambda i: (0, i, 0))


def _gather_exchange(shards_2d):
    shards = tuple(s.reshape(2, s.shape[0] // 2, s.shape[1]) for s in shards_2d)
    n = len(shards)

    def copies(in_refs, out_refs, send_sems, recv_sems):
        srcs, outs = in_refs[:n], out_refs
        x, y, c = lax.axis_index("x"), lax.axis_index("y"), lax.axis_index("c")
        my_chip = 2 * x + y
        sibling = (x, y, 1 - c)
        chips = [(1 - x, y), (x, 1 - y), (1 - x, 1 - y)]

        def copy(k, src, dst, to):
            return pltpu.make_async_remote_copy(src_ref=src, dst_ref=dst, send_sem=send_sems.at[k],
                                                recv_sem=recv_sems.at[k], device_id=to, device_id_type=MESH)

        first, arrive, passed, sibling_arrive = [], [], [], []
        for j, (cx, cy) in enumerate(chips):
            chip = 2 * cx + cy
            for t in range(n):
                k = n * j + t
                first.append(copy(k, srcs[t].at[c], outs[t].at[my_chip, c], (cx, cy, c)))
                arrive.append(copy(k, srcs[t].at[c], outs[t].at[chip, c], (cx, cy, c)))
                passed.append(copy(n * 3 + k, outs[t].at[chip, c], outs[t].at[chip, c], sibling))
                sibling_arrive.append(copy(n * 3 + k, outs[t].at[chip, 1 - c], outs[t].at[chip, 1 - c], sibling))
        return first, arrive, passed, sibling_arrive

    def start(*refs):
        for cp in copies(*refs)[0]:
            cp.start()

    def finish(*refs):
        first, arrive, passed, sibling_arrive = copies(*refs)
        for got, fwd in zip(arrive, passed):
            got.wait_recv()
            fwd.start()
        for cp in sibling_arrive:
            cp.wait_recv()
        for cp in first + passed:
            cp.wait_send()

    my_chip = 2 * lax.axis_index("x") + lax.axis_index("y")
    landing = [lax.dynamic_update_slice(jnp.zeros((N_CHIPS,) + s.shape, s.dtype), s[None], (my_chip, 0, 0, 0))
               for s in shards]
    return dict(ins=list(shards) + landing, start=start, finish=finish, aliases={n + t: t for t in range(n)},
                outs=[jax.ShapeDtypeStruct((N_CHIPS,) + s.shape, s.dtype) for s in shards],
                sems=[pltpu.SemaphoreType.DMA((6 * n,)), pltpu.SemaphoreType.DMA((6 * n,))])


def _run_exchange(ex, name):
    n_in, n_out = len(ex["ins"]), len(ex["outs"])

    def body(*refs):
        in_refs, out_refs, sems = refs[:n_in], refs[n_in:n_in + n_out], refs[n_in + n_out:]
        ex["start"](in_refs, out_refs, *sems)
        ex["finish"](in_refs, out_refs, *sems)

    any_spec = pl.BlockSpec(memory_space=pl.ANY)
    return pl.pallas_call(
        body, name=name, out_shape=ex["outs"], in_specs=[any_spec] * n_in, out_specs=[any_spec] * n_out,
        input_output_aliases=ex.get("aliases", {}), scratch_shapes=ex["sems"],
    )(*ex["ins"])


def _mem_kv(mem, mem_norm, w_mkv):
    def body(mem_ref, g_ref, w_ref, mk_ref, mv_ref):
        m = mem_ref[...]
        r = lax.rsqrt(jnp.mean(m * m, axis=-1, keepdims=True) + RMS_EPS)
        mn = (m * r * g_ref[...]).astype(BF16)
        kv = _dot(mn, w_ref[...])
        mk_ref[...] = kv[:, :C_W].astype(BF16)
        mv_ref[...] = kv[:, C_W:].astype(BF16)

    return pl.pallas_call(
        body, name="mem_kv",
        out_shape=[jax.ShapeDtypeStruct((N_MEM, C_W), BF16)] * 2,
    )(mem, mem_norm, w_mkv)


def _mem_kv_bwd(mem, mem_norm, w_mkv, dmk, dmv):
    def body(mem_ref, g_ref, w_ref, dmk_ref, dmv_ref, gw_ref, gn_ref):
        m = mem_ref[...]
        r = lax.rsqrt(jnp.mean(m * m, axis=-1, keepdims=True) + RMS_EPS)
        mhat = m * r
        mn = (mhat * g_ref[...]).astype(BF16)
        dkv = jnp.concatenate([dmk_ref[...], dmv_ref[...]], axis=1).astype(BF16)
        gw_ref[...] = _dot_tn(mn, dkv)
        dmn = _dot_nt(dkv, w_ref[...])
        gn_ref[...] = jnp.sum(dmn * mhat, axis=0, keepdims=True)

    return pl.pallas_call(
        body, name="mem_kv_bwd",
        out_shape=[jax.ShapeDtypeStruct((D_MODEL, 2 * C_W), F32), jax.ShapeDtypeStruct((1, D_MODEL), F32)],
    )(mem, mem_norm, w_mkv, dmk, dmv)


def _pre_norm(x, pre_norm, host=None):
    seq = x.shape[0]
    tm = min(512, seq)
    n_host_in = len(host["ins"]) if host else 0
    n_host_out = len(host["outs"]) if host else 0

    def body(x_ref, g_ref, *refs):
        host_in, (u_ref, ut_ref), refs = refs[:n_host_in], refs[n_host_in:n_host_in + 2], refs[n_host_in + 2:]
        host_out, sems = refs[:n_host_out], refs[n_host_out:]
        if host:
            @pl.when(pl.program_id(0) == 0)
            def _():
                host["start"](host_in, host_out, *sems)

        xv = x_ref[...]
        r = lax.rsqrt(jnp.mean(xv * xv, axis=-1, keepdims=True) + RMS_EPS)
        u = xv * r * g_ref[...]
        u_ref[...] = u.astype(BF16)
        ut_ref[...] = u.T.astype(BF16)
        if host:
            @pl.when(pl.program_id(0) == seq // tm - 1)
            def _():
                host["finish"](host_in, host_out, *sems)

    any_spec = pl.BlockSpec(memory_space=pl.ANY)
    ins = [x, pre_norm]
    in_specs = [pl.BlockSpec((tm, D_MODEL), lambda i: (i, 0)), pl.BlockSpec(pre_norm.shape, lambda i: (0, 0))]
    out_shape = [jax.ShapeDtypeStruct((seq, D_MODEL), BF16), jax.ShapeDtypeStruct((D_MODEL, seq), BF16)]
    out_specs = [pl.BlockSpec((tm, D_MODEL), lambda i: (i, 0)), pl.BlockSpec((D_MODEL, tm), lambda i: (0, i))]
    aliases, scratch = {}, []
    if host:
        aliases = {len(ins) + k: 2 + v for k, v in host.get("aliases", {}).items()}
        ins += list(host["ins"])
        in_specs += [any_spec] * n_host_in
        out_shape += list(host["outs"])
        out_specs += [any_spec] * n_host_out
        scratch = list(host["sems"])
    res = pl.pallas_call(
        body, name="pre_norm", grid=(seq // tm,), in_specs=in_specs, out_specs=out_specs, out_shape=out_shape,
        input_output_aliases=aliases, scratch_shapes=scratch,
        compiler_params=pltpu.CompilerParams(dimension_semantics=("arbitrary",)),
    )(*ins)
    return res[0], res[1], res[2:]


def _pre_proj(u, w_in_g, host=None):
    seq = u.shape[0]
    tm = min(512, seq)
    n_nat, n_dil = len(_NATURAL), len(_DILATED) * len(B_DILS)
    rope = _rope_tables(seq, tm)

    n_host_in = len(host["ins"]) if host else 0
    n_host_out = len(host["outs"]) if host else 0
    n_own_out = n_nat + n_dil

    def body(u_ref, w_ref, rl_ref, rb_ref, *refs):
        host_in, refs = refs[:n_host_in], refs[n_host_in:]
        nat = dict(zip(_NATURAL, refs[:n_nat]))
        res = {n: refs[n_nat + len(B_DILS) * k:n_nat + len(B_DILS) * (k + 1)] for k, n in enumerate(_DILATED)}
        host_out = refs[n_own_out:n_own_out + n_host_out]
        bufs = dict(zip(_DILATED, refs[n_own_out + n_host_out:]))
        sems = refs[n_own_out + n_host_out + len(_DILATED):]
        if host:
            @pl.when(pl.program_id(0) == 0)
            def _():
                host["start"](host_in, host_out, *sems)

        ub = u_ref[...]
        c, sm, sp = _rope_coeffs(rl_ref, rb_ref)
        for j in range(N_CHIPS):
            pj = _dot(ub, w_ref[j])
            for b in range(SHARD_IN // LANES):
                name, off, roped, scaled = _PROJ_LAYOUT[(SHARD_IN // LANES) * j + b]
                piece = pj[:, LANES * b:LANES * (b + 1)]
                if roped:
                    piece = _rope(piece, c, sm, sp)
                if scaled:
                    piece = piece * SCALE
                if name in bufs:
                    bufs[name][off // LANES] = piece
                else:
                    nat[name][:, off:off + LANES] = piece.astype(BF16)
        for name in _DILATED:
            for ref, dil in zip(res[name], B_DILS):
                _to_residues(bufs[name], ref, dil)
        if host:
            @pl.when(pl.program_id(0) == seq // tm - 1)
            def _():
                host["finish"](host_in, host_out, *sems)

    row = lambda w: pl.BlockSpec((tm, w), lambda i: (i, 0))
    full = lambda a: pl.BlockSpec(a.shape, lambda i: (0,) * a.ndim)
    any_spec = pl.BlockSpec(memory_space=pl.ANY)
    out_shape = [jax.ShapeDtypeStruct((seq, _PROJ_WIDTH[n]), BF16) for n in _NATURAL]
    out_specs = [row(_PROJ_WIDTH[n]) for n in _NATURAL]
    for n in _DILATED:
        for dil in B_DILS:
            out_shape.append(jax.ShapeDtypeStruct((dil, seq // dil, B_W), BF16))
            out_specs.append(_residue_spec(dil, tm, B_W))
    ins = [u, w_in_g, *rope]
    in_specs = [row(D_MODEL), full(w_in_g), full(rope[0]), pl.BlockSpec((8, 2 * LANES), lambda i: (i, 0))]
    scratch = [_stage(tm, B_W)] * len(_DILATED)
    aliases = {}
    if host:
        aliases = {len(ins) + k: n_own_out + v for k, v in host.get("aliases", {}).items()}
        ins += list(host["ins"])
        in_specs += [any_spec] * n_host_in
        out_shape += list(host["outs"])
        out_specs += [any_spec] * n_host_out
        scratch += list(host["sems"])
    res = pl.pallas_call(
        body, name="pre_proj", grid=(seq // tm,), in_specs=in_specs, out_specs=out_specs, out_shape=out_shape,
        input_output_aliases=aliases, scratch_shapes=scratch,
        compiler_params=pltpu.CompilerParams(dimension_semantics=("arbitrary",)),
    )(*ins)
    out = dict(zip(_NATURAL, res[:n_nat]))
    for k, n in enumerate(_DILATED):
        out[n] = res[n_nat + len(B_DILS) * k:n_nat + len(B_DILS) * (k + 1)]
    out["hosted"] = res[n_own_out:]
    return out


def _band_bias(max_dist, transposed):
    i = np.arange(BLOCK)[:, None]
    j = np.arange(BLOCK)[None, :]
    if transposed:
        same = i <= j
        other = (j + BLOCK - i) <= max_dist
        vis = np.concatenate([same, other], axis=1)
    else:
        prev = (i + BLOCK - j) <= max_dist
        same = j <= i
        vis = np.concatenate([prev, same], axis=1)
    return jnp.asarray(np.where(vis, 0.0, NEG).astype(np.float32))


def _kv_place(h, gqa):
    return (0, h // 3) if gqa else (h // 2, h % 2)


def _band_fwd(q, k, v, sink, *, max_dist, name):
    dil, length, wq = q.shape
    wk = k.shape[2]
    gqa = wk != wq
    tq = min(512, length)
    ns, nt = tq // BLOCK, length // tq
    npair = wq // LANES
    bias = _band_bias(max_dist, transposed=False)
    has_sink = sink is not None

    def body(*refs):
        if has_sink:
            sink_ref, refs = refs[0], refs[1:]
        q_ref, k_ref, kp_ref, v_ref, vp_ref, bias_ref, o_ref, lse_ref, kbuf, vbuf = refs[:10]
        i = pl.program_id(1)
        kbuf[0:BLOCK] = kp_ref[...]
        kbuf[BLOCK:] = k_ref[...]
        vbuf[0:BLOCK] = vp_ref[...]
        vbuf[BLOCK:] = v_ref[...]
        if gqa:
            kroll, vroll = refs[10:12]
            kroll[...] = pltpu.roll(kbuf[...], HEAD_DIM, 1)
            vroll[...] = pltpu.roll(vbuf[...], HEAD_DIM, 1)
        half = _half_masks(BLOCK)
        col_prev = (lax.broadcasted_iota(jnp.int32, (1, 2 * BLOCK), 1) < BLOCK).astype(F32)

        def sub(a, carry):
            r0 = pl.multiple_of(a * BLOCK, BLOCK)
            pen = jnp.where((i == 0) & (a == 0), NEG, 0.0)
            b = bias_ref[...] + pen * col_prev
            scores = []
            for p in range(npair):
                qp = q_ref[pl.ds(r0, BLOCK), p * LANES:(p + 1) * LANES]
                for e in range(2):
                    pk, ek = _kv_place(2 * p + e, gqa)
                    kw = (kbuf if ek == e else kroll)[pl.ds(r0, 2 * BLOCK), pk * LANES:(pk + 1) * LANES]
                    scores.append(_dot_nt(jnp.where(half[e], qp, jnp.zeros_like(qp)), kw))
            m_cols, l_cols, probs = [], [], []
            for h, s in enumerate(scores):
                s = s + b
                m = jnp.max(s, axis=1, keepdims=True)
                if has_sink:
                    m = jnp.maximum(m, sink_ref[h])
                pe = jnp.exp(s - m)
                l = jnp.sum(pe, axis=1, keepdims=True)
                if has_sink:
                    l = l + jnp.exp(sink_ref[h] - m)
                probs.append(pe.astype(BF16))
                m_cols.append(m)
                l_cols.append(l)
            for p in range(npair):
                o_h = []
                for e in range(2):
                    h = 2 * p + e
                    pk, ek = _kv_place(h, gqa)
                    vw = (vbuf if ek == e else vroll)[pl.ds(r0, 2 * BLOCK), pk * LANES:(pk + 1) * LANES]
                    o_h.append(_dot(probs[h], vw) * (1.0 / l_cols[h]))
                o_ref[pl.ds(r0, BLOCK), p * LANES:(p + 1) * LANES] = jnp.where(half[0], o_h[0], o_h[1]).astype(BF16)
            lse_ref[pl.ds(r0, BLOCK), :] = _per_head(m_cols) + jnp.log(_per_head(l_cols, 1.0))
            return carry

        lax.fori_loop(0, ns, sub, 0, unroll=True)

    main = lambda w: pl.BlockSpec((None, tq, w), lambda r, i: (r, i, 0))
    prev = lambda w: pl.BlockSpec((None, BLOCK, w), lambda r, i: (r, jnp.maximum(i * ns - 1, 0), 0))
    in_specs = [main(wq), main(wk), prev(wk), main(wk), prev(wk), pl.BlockSpec(bias.shape, lambda r, i: (0, 0))]
    args = [q, k, k, v, v, bias]
    if has_sink:
        in_specs = [pl.BlockSpec(memory_space=pltpu.SMEM)] + in_specs
        args = [sink] + args
    scratch = [pltpu.VMEM((tq + BLOCK, wk), BF16)] * (4 if gqa else 2)
    return pl.pallas_call(
        body, name=name, grid=(dil, nt), in_specs=in_specs,
        out_specs=[main(wq), main(LANES)],
        out_shape=[jax.ShapeDtypeStruct((dil, length, wq), BF16), jax.ShapeDtypeStruct((dil, length, LANES), F32)],
        scratch_shapes=scratch,
    )(*args)


def _band_bwd(q, k, v, do, lse, delta, *, max_dist, name):
    dil, length, wq = q.shape
    wk = k.shape[2]
    gqa = wk != wq
    tq = min(512, length)
    ns, nt = tq // BLOCK, length // tq
    npair = wq // LANES
    nblocks = length // BLOCK
    bias = _band_bias(max_dist, transposed=True)

    def body(q_ref, qn_ref, do_ref, don_ref, lse_ref, lsen_ref, dl_ref, dln_ref, k_ref, v_ref, bias_ref,
             dq_ref, dk_ref, dv_ref, qbuf, dobuf, stat_l, stat_d, dqt, kt, *rolled):
        i = pl.program_id(1)
        qbuf[0:tq] = q_ref[...]
        qbuf[tq:] = qn_ref[...]
        dobuf[0:tq] = do_ref[...]
        dobuf[tq:] = don_ref[...]
        for pk in range(wk // LANES):
            kt[pk] = k_ref[:, pk * LANES:(pk + 1) * LANES].astype(F32).T.astype(BF16)
        if gqa:
            kroll, vroll, ktroll = rolled
            kroll[...] = pltpu.roll(k_ref[...], HEAD_DIM, 1)
            vroll[...] = pltpu.roll(v_ref[...], HEAD_DIM, 1)
            ktroll[0] = kroll[...].astype(F32).T.astype(BF16)
        for a in range(ns):
            rows = slice(a * BLOCK, (a + 1) * BLOCK)
            stat_l[a] = _rows_to_lanes(lse_ref[rows, :])
            stat_d[a] = _rows_to_lanes(dl_ref[rows, :])
        stat_l[ns] = _rows_to_lanes(lsen_ref[...])
        stat_d[ns] = _rows_to_lanes(dln_ref[...])

        @pl.when(i == 0)
        def _():
            dqt[:, :, 0:BLOCK] = jnp.zeros((npair, LANES, BLOCK), F32)

        @pl.when(i > 0)
        def _():
            dqt[:, :, 0:BLOCK] = dqt[:, :, tq:tq + BLOCK]

        dqt[:, :, BLOCK:] = jnp.zeros((npair, LANES, tq), F32)
        half2 = _half_masks(2 * BLOCK)
        row = lax.broadcasted_iota(jnp.int32, (LANES, BLOCK), 0)
        row_half = (row < HEAD_DIM, row >= HEAD_DIM)
        col_next = (lax.broadcasted_iota(jnp.int32, (1, 2 * BLOCK), 1) >= BLOCK).astype(F32)

        for b in range(ns):
            rows = slice(b * BLOCK, (b + 1) * BLOCK)
            window = slice(b * BLOCK, (b + 2) * BLOCK)
            bt = bias_ref[...]
            if b == ns - 1:
                bt = bt + jnp.where(i == nt - 1, NEG, 0.0) * col_next
            acc = {}
            items = []
            for p in range(npair):
                lanes = slice(p * LANES, (p + 1) * LANES)
                qw = qbuf[window, lanes]
                dow = dobuf[window, lanes]
                for e in range(2):
                    h = 2 * p + e
                    pk, ek = _kv_place(h, gqa)
                    klanes = slice(pk * LANES, (pk + 1) * LANES)
                    kb = (k_ref if ek == e else kroll)[rows, klanes]
                    vb = (v_ref if ek == e else vroll)[rows, klanes]
                    qm = jnp.where(half2[e], qw, jnp.zeros_like(qw))
                    dom = jnp.where(half2[e], dow, jnp.zeros_like(dow))
                    items.append(dict(p=p, e=e, h=h, pk=pk, ek=ek, qm=qm, dom=dom,
                                      st=_dot_nt(kb, qm), dpt=_dot_nt(vb, dom)))
            for it in items:
                h = it["h"]
                lrow = jnp.concatenate([stat_l[b, h:h + 1, :], stat_l[b + 1, h:h + 1, :]], axis=1)
                drow = jnp.concatenate([stat_d[b, h:h + 1, :], stat_d[b + 1, h:h + 1, :]], axis=1)
                pt = jnp.exp(it["st"] + bt - lrow)
                it["ptb"] = pt.astype(BF16)
                it["dsb"] = (pt * (it["dpt"] - drow)).astype(BF16)
            for it in items:
                p, e, pk, ek = it["p"], it["e"], it["pk"], it["ek"]
                dv_c = _dot(it["ptb"], it["dom"])
                dk_c = _dot(it["dsb"], it["qm"])
                kbt = (kt if ek == e else ktroll)[pk, :, rows]
                kbtm = jnp.where(row_half[e], kbt, jnp.zeros_like(kbt))
                dqt[p, :, window] += _dot(kbtm, it["dsb"])
                key = (pk, ek == e)
                if key in acc:
                    acc[key] = (acc[key][0] + dk_c, acc[key][1] + dv_c)
                else:
                    acc[key] = (dk_c, dv_c)
            if not gqa:
                for p in range(npair):
                    lanes = slice(p * LANES, (p + 1) * LANES)
                    dk_ref[rows, lanes] = acc[(p, True)][0].astype(BF16)
                    dv_ref[rows, lanes] = acc[(p, True)][1].astype(BF16)
            if gqa:
                dk_al, dv_al = acc[(0, True)]
                dk_mis, dv_mis = acc[(0, False)]
                dk_ref[rows, :] = (dk_al + pltpu.roll(dk_mis, HEAD_DIM, 1)).astype(BF16)
                dv_ref[rows, :] = (dv_al + pltpu.roll(dv_mis, HEAD_DIM, 1)).astype(BF16)

        for p in range(npair):
            dq_ref[:, p * LANES:(p + 1) * LANES] = dqt[p, :, 0:tq].T.astype(BF16)

    main = lambda w: pl.BlockSpec((None, tq, w), lambda r, i: (r, i, 0))
    nxt = lambda w: pl.BlockSpec((None, BLOCK, w), lambda r, i: (r, jnp.minimum((i + 1) * ns, nblocks - 1), 0))
    scratch = [pltpu.VMEM((tq + BLOCK, wq), BF16), pltpu.VMEM((tq + BLOCK, wq), BF16),
               pltpu.VMEM((ns + 1, 8, LANES), F32), pltpu.VMEM((ns + 1, 8, LANES), F32),
               pltpu.VMEM((npair, LANES, tq + BLOCK), F32), pltpu.VMEM((wk // LANES, LANES, tq), BF16)]
    if gqa:
        scratch = scratch + [pltpu.VMEM((tq, wk), BF16)] * 2 + [pltpu.VMEM((1, LANES, tq), BF16)]
    return pl.pallas_call(
        body, name=name, grid=(dil, nt),
        in_specs=[main(wq), nxt(wq), main(wq), nxt(wq), main(LANES), nxt(LANES), main(LANES), nxt(LANES),
                  main(wk), main(wk), pl.BlockSpec(bias.shape, lambda r, i: (0, 0))],
        out_specs=[main(wq), main(wk), main(wk)],
        out_shape=[jax.ShapeDtypeStruct((dil, length, wq), BF16), jax.ShapeDtypeStruct((dil, length, wk), BF16),
                   jax.ShapeDtypeStruct((dil, length, wk), BF16)],
        scratch_shapes=scratch,
        compiler_params=pltpu.CompilerParams(dimension_semantics=("arbitrary", "arbitrary")),
    )(q, q, do, do, lse, lse, delta, delta, k, v, bias)


def _mem_attn_fwd(q, mk, mv):
    seq = q.shape[0]
    tq = min(512, seq)
    ns = tq // BLOCK

    def body(q_ref, mk_ref, mv_ref, o_ref, lse_ref):
        half = _half_masks(BLOCK)

        def sub(a, carry):
            r0 = pl.multiple_of(a * BLOCK, BLOCK)
            scores = []
            for p in range(C_W // LANES):
                lanes = slice(p * LANES, (p + 1) * LANES)
                qp = q_ref[pl.ds(r0, BLOCK), lanes]
                for e in range(2):
                    scores.append(_dot_nt(jnp.where(half[e], qp, jnp.zeros_like(qp)), mk_ref[:, lanes]))
            m_cols, l_cols, probs = [], [], []
            for s in scores:
                m = jnp.max(s, axis=1, keepdims=True)
                pe = jnp.exp(s - m)
                probs.append(pe.astype(BF16))
                m_cols.append(m)
                l_cols.append(jnp.sum(pe, axis=1, keepdims=True))
            for p in range(C_W // LANES):
                lanes = slice(p * LANES, (p + 1) * LANES)
                o_h = [_dot(probs[2 * p + e], mv_ref[:, lanes]) * (1.0 / l_cols[2 * p + e]) for e in range(2)]
                o_ref[pl.ds(r0, BLOCK), lanes] = jnp.where(half[0], o_h[0], o_h[1]).astype(BF16)
            lse_ref[pl.ds(r0, BLOCK), :] = _per_head(m_cols) + jnp.log(_per_head(l_cols, 1.0))
            return carry

        lax.fori_loop(0, ns, sub, 0, unroll=True)

    row = lambda w: pl.BlockSpec((tq, w), lambda i: (i, 0))
    full = pl.BlockSpec((N_MEM, C_W), lambda i: (0, 0))
    return pl.pallas_call(
        body, name="mem_attn_fwd", grid=(seq // tq,), in_specs=[row(C_W), full, full],
        out_specs=[row(C_W), row(LANES)],
        out_shape=[jax.ShapeDtypeStruct((seq, C_W), BF16), jax.ShapeDtypeStruct((seq, LANES), F32)],
    )(q, mk, mv)


def _mem_attn_bwd(q, mk, mv, do, lse, delta):
    seq = q.shape[0]
    tq = min(512, seq)
    ns = tq // BLOCK
    npair = C_W // LANES

    def body(q_ref, mk_ref, mv_ref, do_ref, lse_ref, dl_ref, dq_ref, dmk_ref, dmv_ref, stat_l, stat_d, mkt, dqt):
        @pl.when(pl.program_id(0) == 0)
        def _():
            dmk_ref[...] = jnp.zeros_like(dmk_ref)
            dmv_ref[...] = jnp.zeros_like(dmv_ref)
            for p in range(npair):
                mkt[p] = mk_ref[:, p * LANES:(p + 1) * LANES].astype(F32).T.astype(BF16)

        for a in range(ns):
            rows = slice(a * BLOCK, (a + 1) * BLOCK)
            stat_l[a] = _rows_to_lanes(lse_ref[rows, :])
            stat_d[a] = _rows_to_lanes(dl_ref[rows, :])
        half = _half_masks(BLOCK)
        row = lax.broadcasted_iota(jnp.int32, (LANES, N_MEM), 0)
        row_half = (row < HEAD_DIM, row >= HEAD_DIM)

        for a in range(ns):
            rows = slice(a * BLOCK, (a + 1) * BLOCK)
            items = []
            for p in range(npair):
                lanes = slice(p * LANES, (p + 1) * LANES)
                qp = q_ref[rows, lanes]
                dop = do_ref[rows, lanes]
                for e in range(2):
                    qm = jnp.where(half[e], qp, jnp.zeros_like(qp))
                    dom = jnp.where(half[e], dop, jnp.zeros_like(dop))
                    items.append(dict(p=p, e=e, qm=qm, dom=dom, st=_dot_nt(mk_ref[:, lanes], qm),
                                      dpt=_dot_nt(mv_ref[:, lanes], dom)))
            for it in items:
                h = 2 * it["p"] + it["e"]
                pt = jnp.exp(it["st"] - stat_l[a, h:h + 1, :])
                it["ptb"] = pt.astype(BF16)
                it["dsb"] = (pt * (it["dpt"] - stat_d[a, h:h + 1, :])).astype(BF16)
            for p in range(npair):
                lanes = slice(p * LANES, (p + 1) * LANES)
                pair = [it for it in items if it["p"] == p]
                dmv_ref[:, lanes] += _dot(pair[0]["ptb"], pair[0]["dom"]) + _dot(pair[1]["ptb"], pair[1]["dom"])
                dmk_ref[:, lanes] += _dot(pair[0]["dsb"], pair[0]["qm"]) + _dot(pair[1]["dsb"], pair[1]["qm"])
                kbt = mkt[p]
                dqt[p, :, rows] = (_dot(jnp.where(row_half[0], kbt, jnp.zeros_like(kbt)), pair[0]["dsb"])
                                   + _dot(jnp.where(row_half[1], kbt, jnp.zeros_like(kbt)), pair[1]["dsb"]))
        for p in range(npair):
            dq_ref[:, p * LANES:(p + 1) * LANES] = dqt[p].T.astype(BF16)

    row = lambda w: pl.BlockSpec((tq, w), lambda i: (i, 0))
    full = pl.BlockSpec((N_MEM, C_W), lambda i: (0, 0))
    return pl.pallas_call(
        body, name="mem_attn_bwd", grid=(seq // tq,),
        in_specs=[row(C_W), full, full, row(C_W), row(LANES), row(LANES)], out_specs=[row(C_W), full, full],
        out_shape=[jax.ShapeDtypeStruct((seq, C_W), BF16), jax.ShapeDtypeStruct((N_MEM, C_W), F32),
                   jax.ShapeDtypeStruct((N_MEM, C_W), F32)],
        scratch_shapes=[pltpu.VMEM((ns, 8, LANES), F32)] * 2
        + [pltpu.VMEM((npair, LANES, N_MEM), BF16), pltpu.VMEM((npair, LANES, tq), F32)],
        compiler_params=pltpu.CompilerParams(dimension_semantics=("arbitrary",)),
    )(q, mk, mv, do, lse, delta)


def _silu_and_grad(g):
    s = 1.0 / (1.0 + jnp.exp(-g))
    return g * s, s * (1.0 + g * (1.0 - s))


def _post(x, target, post_norm, w_out, sink_row, oa, lse_a, ga, ob_list, lseb_list, gb, oc, gc):
    seq = x.shape[0]
    tm = min(512, seq)
    inv_d = 1.0 / D_MODEL
    nd = len(B_DILS)

    def body(*refs):
        (x_ref, t_ref, gp_ref, w_ref, sink_ref, oa_ref, lsea_ref, ga_ref), refs = refs[:8], refs[8:]
        ob_refs, lb_refs, (gb_ref, oc_ref, gc_ref), refs = refs[:nd], refs[nd:2 * nd], refs[2 * nd:2 * nd + 3], refs[2 * nd + 3:]
        (g_ref, doa_ref, dla_ref, dga_ref), refs = refs[:4], refs[4:]
        dob_refs, lsec_refs, dlb_refs, refs = refs[:nd], refs[nd:2 * nd], refs[2 * nd:3 * nd], refs[3 * nd:]
        (dgb_ref, doc_ref, dlc_ref, dgc_ref, gw_ref, gpost_ref, gsink_ref, loss_ref), refs = refs[:8], refs[8:]
        ycat, obufs, lbufs, st_do, st_l, st_d = refs[0], refs[1:nd], refs[nd:2 * nd - 1], refs[2 * nd - 1], refs[2 * nd], refs[2 * nd + 1]

        @pl.when(pl.program_id(0) == 0)
        def _():
            gw_ref[...] = jnp.zeros_like(gw_ref)
            gpost_ref[...] = jnp.zeros_like(gpost_ref)
            gsink_ref[...] = jnp.zeros_like(gsink_ref)
            loss_ref[...] = jnp.zeros_like(loss_ref)

        o_i, l_i = [ob_refs[0][0].astype(F32)], [lb_refs[0][0]]
        for k in range(1, nd):
            _from_residues(ob_refs[k], obufs[k - 1], B_DILS[k])
            _from_residues(lb_refs[k], lbufs[k - 1], B_DILS[k])
            o_i.append(_stage_read(obufs[k - 1]))
            l_i.append(_stage_read(lbufs[k - 1]))
        mx = l_i[0]
        for l in l_i[1:]:
            mx = jnp.maximum(mx, l)
        w_i = [jnp.exp(l - mx) for l in l_i]
        z = w_i[0]
        for w in w_i[1:]:
            z = z + w
        _stage_write(st_l, mx + jnp.log(z))
        expand = _head_expand_matrix(B_W)
        inv_z = 1.0 / z
        ob = None
        for w, o in zip(w_i, o_i):
            term = _dot_split(w * inv_z, expand, 2) * o
            ob = term if ob is None else ob + term
        oa, oc = oa_ref[...].astype(F32), oc_ref[...].astype(F32)
        sa, dsa = _silu_and_grad(ga_ref[...].astype(F32))
        sb, dsb = _silu_and_grad(gb_ref[...].astype(F32))
        sc, dsc = _silu_and_grad(gc_ref[...].astype(F32))
        ycat[:, 0:A_W] = (oa * sa).astype(BF16)
        ycat[:, A_W:A_W + B_W] = (ob * sb).astype(BF16)
        ycat[:, A_W + B_W:] = (oc * sc).astype(BF16)
        y2 = _dot(ycat[...], w_ref[...])
        r = lax.rsqrt(jnp.mean(y2 * y2, axis=-1, keepdims=True) + RMS_EPS)
        zhat = y2 * r
        gp = gp_ref[...]
        err = x_ref[...] + zhat * gp - t_ref[...]
        loss_ref[...] += jnp.sum(err * err) * (0.5 * inv_d)
        g = err * inv_d
        g_ref[...] = g
        gpost_ref[...] += jnp.sum(g * zhat, axis=0, keepdims=True)
        a = g * gp
        dy2 = (r * (a - zhat * jnp.mean(a * zhat, axis=-1, keepdims=True))).astype(BF16)
        gw_ref[...] += _dot_tn(ycat[...], dy2)
        dycat = _dot_nt(dy2, w_ref[...])
        dya, dyb, dyc = dycat[:, 0:A_W], dycat[:, A_W:A_W + B_W], dycat[:, A_W + B_W:]
        doa, dob, doc = dya * sa, dyb * sb, dyc * sc
        doa_ref[...] = doa.astype(BF16)
        doc_ref[...] = doc.astype(BF16)
        dga_ref[...] = (dya * oa * dsa).astype(BF16)
        dgb_ref[...] = (dyb * ob * dsb).astype(BF16)
        dgc_ref[...] = (dyc * oc * dsc).astype(BF16)
        dl_a = _dot_split(doa * oa, _head_sum_matrix(A_W), 2)
        dla_ref[...] = dl_a
        dlc_ref[...] = _dot_split(doc * oc, _head_sum_matrix(C_W), 2)
        gsink_ref[...] += jnp.sum(jnp.exp(sink_ref[...] - lsea_ref[...]) * dl_a, axis=0, keepdims=True)
        _stage_write(st_do, dob)
        _stage_write(st_d, _dot_split(dob * ob, _head_sum_matrix(B_W), 2))
        for k, dil in enumerate(B_DILS):
            _to_residues(st_do, dob_refs[k], dil)
            _to_residues(st_l, lsec_refs[k], dil)
            _to_residues(st_d, dlb_refs[k], dil)

    row = lambda w: pl.BlockSpec((tm, w), lambda i: (i, 0))
    full = lambda shape: pl.BlockSpec(shape, lambda i: (0,) * len(shape))
    res_specs = lambda w: [_residue_spec(d, tm, w) for d in B_DILS]
    res_shapes = lambda w, dt: [jax.ShapeDtypeStruct((d, seq // d, w), dt) for d in B_DILS]
    ins = [x, target, post_norm, w_out, sink_row, oa, lse_a, ga, *ob_list, *lseb_list, gb, oc, gc]
    in_specs = ([row(D_MODEL), row(D_MODEL), full((1, D_MODEL)), full((D_MODEL, D_MODEL)), full((1, LANES)),
                 row(A_W), row(LANES), row(A_W)] + res_specs(B_W) + res_specs(LANES) + [row(B_W), row(C_W), row(C_W)])
    out_shape = ([jax.ShapeDtypeStruct((seq, D_MODEL), F32), jax.ShapeDtypeStruct((seq, A_W), BF16),
                  jax.ShapeDtypeStruct((seq, LANES), F32), jax.ShapeDtypeStruct((seq, A_W), BF16)]
                 + res_shapes(B_W, BF16) + res_shapes(LANES, F32) + res_shapes(LANES, F32)
                 + [jax.ShapeDtypeStruct((seq, B_W), BF16), jax.ShapeDtypeStruct((seq, C_W), BF16),
                    jax.ShapeDtypeStruct((seq, LANES), F32), jax.ShapeDtypeStruct((seq, C_W), BF16),
                    jax.ShapeDtypeStruct((D_MODEL, D_MODEL), F32), jax.ShapeDtypeStruct((1, D_MODEL), F32),
                    jax.ShapeDtypeStruct((1, LANES), F32), jax.ShapeDtypeStruct((1, LANES), F32)])
    out_specs = ([row(D_MODEL), row(A_W), row(LANES), row(A_W)] + res_specs(B_W) + res_specs(LANES) + res_specs(LANES)
                 + [row(B_W), row(C_W), row(LANES), row(C_W),
                    full((D_MODEL, D_MODEL)), full((1, D_MODEL)), full((1, LANES)), full((1, LANES))])
    scratch = ([pltpu.VMEM((tm, D_MODEL), BF16)] + [_stage(tm, B_W)] * (nd - 1) + [_stage(tm, LANES)] * (nd - 1)
               + [_stage(tm, B_W), _stage(tm, LANES), _stage(tm, LANES)])
    res = pl.pallas_call(
        body, name="post", grid=(seq // tm,), in_specs=in_specs, out_specs=out_specs, out_shape=out_shape,
        scratch_shapes=scratch,
        compiler_params=pltpu.CompilerParams(dimension_semantics=("arbitrary",)),
    )(*ins)
    out = dict(g=res[0], doa=res[1], dl_a=res[2], dga=res[3], dob=res[4:4 + nd], lse_b=res[4 + nd:4 + 2 * nd],
               dl_b=res[4 + 2 * nd:4 + 3 * nd])
    rest = res[4 + 3 * nd:]
    out.update(dgb=rest[0], doc=rest[1], dl_c=rest[2], dgc=rest[3], gw_out=rest[4], gpost=rest[5], gsink=rest[6],
               loss=rest[7])
    return out


def _grad_w_in(ut, nat, res):
    seq = ut.shape[1]
    tm = min(512, seq)
    nd = len(B_DILS)
    nat_list = [nat[n] for n in _NATURAL]
    res_list = [a for n in _DILATED for a in res[n]]
    rope = _rope_tables(seq, tm)

    def body(rl_ref, rb_ref, ut_ref, *refs):
        nat_refs = dict(zip(_NATURAL, refs[:len(_NATURAL)]))
        refs = refs[len(_NATURAL):]
        res_refs = {n: refs[nd * k:nd * (k + 1)] for k, n in enumerate(_DILATED)}
        refs = refs[nd * len(_DILATED):]
        dproj_ref, gw_ref = refs[:2]
        bufs = {n: refs[2 + (nd - 1) * k:2 + (nd - 1) * (k + 1)] for k, n in enumerate(_DILATED)}

        @pl.when(pl.program_id(0) == 0)
        def _():
            gw_ref[...] = jnp.zeros_like(gw_ref)

        for n in _DILATED:
            for k in range(1, nd):
                _from_residues(res_refs[n][k], bufs[n][k - 1], B_DILS[k])
        c, sm, sp = _rope_coeffs(rl_ref, rb_ref)
        sm, sp = -sm, -sp
        for blk, (name, off, roped, scaled) in enumerate(_PROJ_LAYOUT):
            lanes = slice(off, off + LANES)
            if name in nat_refs:
                piece = nat_refs[name][:, lanes].astype(F32)
            else:
                piece = res_refs[name][0][0, :, lanes].astype(F32)
                for buf in bufs[name]:
                    piece = piece + buf[off // LANES]
            if roped:
                piece = _rope(piece, c, sm, sp)
            if scaled:
                piece = piece * SCALE
            dproj_ref[:, blk * LANES:(blk + 1) * LANES] = piece.astype(BF16)
        for j in range(N_CHIPS):
            gw_ref[j] += _dot(ut_ref[...], dproj_ref[:, j * SHARD_IN:(j + 1) * SHARD_IN])

    row = lambda w: pl.BlockSpec((tm, w), lambda i: (i, 0))
    in_specs = ([pl.BlockSpec(rope[0].shape, lambda i: (0, 0)), pl.BlockSpec((8, 2 * LANES), lambda i: (i, 0)),
                 pl.BlockSpec((D_MODEL, tm), lambda i: (0, i))]
                + [row(a.shape[1]) for a in nat_list]
                + [_residue_spec(d, tm, B_W) for _ in _DILATED for d in B_DILS])
    return pl.pallas_call(
        body, name="grad_w_in", grid=(seq // tm,), in_specs=in_specs,
        out_specs=[row(D_IN), pl.BlockSpec((N_CHIPS, D_MODEL, SHARD_IN), lambda i: (0, 0, 0))],
        out_shape=[jax.ShapeDtypeStruct((seq, D_IN), BF16), jax.ShapeDtypeStruct((N_CHIPS, D_MODEL, SHARD_IN), F32)],
        scratch_shapes=[_stage(tm, B_W)] * ((nd - 1) * len(_DILATED)),
        compiler_params=pltpu.CompilerParams(dimension_semantics=("arbitrary",)),
    )(*rope, ut, *nat_list, *res_list)


def _input_grad(x, g, pre_norm, w_in_g, dproj, gx_prev, span, after, name):
    seq = x.shape[0]
    tm = seq // 16
    first_block, steps = span

    def body(*refs):
        x_ref, g_ref, gp_ref, w_ref, dp_ref = refs[:5]
        gx_ref, gpre_ref = refs[-2:]

        @pl.when(pl.program_id(0) == 0)
        def _():
            gpre_ref[...] = jnp.zeros_like(gpre_ref)

        du = None
        for j in range(N_CHIPS):
            term = _dot_nt(dp_ref[:, j * SHARD_IN:(j + 1) * SHARD_IN], w_ref[j])
            du = term if du is None else du + term
        xv = x_ref[...]
        r = lax.rsqrt(jnp.mean(xv * xv, axis=-1, keepdims=True) + RMS_EPS)
        xhat = xv * r
        gpre_ref[...] += jnp.sum(du * xhat, axis=0, keepdims=True)
        a = du * gp_ref[...]
        gx_ref[...] = g_ref[...] + r * (a - xhat * jnp.mean(a * xhat, axis=-1, keepdims=True))

    row = lambda w: pl.BlockSpec((tm, w), lambda i: (first_block + i, 0))
    full = lambda a: pl.BlockSpec(a.shape, lambda i: (0,) * a.ndim)
    any_spec = pl.BlockSpec(memory_space=pl.ANY)
    ins = [x, g, pre_norm, w_in_g, dproj]
    in_specs = [row(D_MODEL), row(D_MODEL), full(pre_norm), full(w_in_g), row(D_IN)]
    aliases = {}
    if gx_prev is not None:
        aliases[len(ins)] = 0
        ins.append(gx_prev)
        in_specs.append(any_spec)
    if after is not None:
        ins.append(after)
        in_specs.append(any_spec)
    return pl.pallas_call(
        body, name=name, grid=(steps,), in_specs=in_specs,
        out_specs=[row(D_MODEL), pl.BlockSpec((1, D_MODEL), lambda i: (0, 0))],
        out_shape=[jax.ShapeDtypeStruct((seq, D_MODEL), F32), jax.ShapeDtypeStruct((1, D_MODEL), F32)],
        input_output_aliases=aliases,
        compiler_params=pltpu.CompilerParams(dimension_semantics=("arbitrary",)),
    )(*ins)


def _exchange_start(ex, name):
    n_in, n_out, n_sem = len(ex["ins"]), len(ex["outs"]), len(ex["sems"])

    def body(*refs):
        in_refs, land_refs, sems = refs[:n_in], refs[n_in:n_in + n_out], refs[n_in + n_out:n_in + n_out + n_sem]
        ex["start"](in_refs, land_refs, *sems)
        token = refs[-1]
        token[...] = jnp.zeros_like(token)

    hbm = pl.BlockSpec(memory_space=pltpu.HBM)
    sem = pl.BlockSpec(memory_space=pltpu.SEMAPHORE)
    ins = [pltpu.with_memory_space_constraint(a, pltpu.HBM) for a in ex["ins"]]
    landing = [pltpu.with_memory_space_constraint(lax.empty(o.shape, o.dtype), pltpu.HBM) for o in ex["outs"]]
    res = pl.pallas_call(
        body, name=name,
        out_shape=list(ex["sems"]) + [pltpu.HBM(a.shape, a.dtype) for a in ex["ins"]]
        + [pltpu.HBM(o.shape, o.dtype) for o in ex["outs"]] + [jax.ShapeDtypeStruct((8, LANES), F32)],
        in_specs=[hbm] * (n_in + n_out),
        out_specs=[sem] * n_sem + [hbm] * (n_in + n_out) + [pl.BlockSpec(memory_space=pltpu.VMEM)],
        input_output_aliases={k: n_sem + k for k in range(n_in + n_out)},
        compiler_params=pltpu.CompilerParams(has_side_effects=pltpu.SideEffectType.DATAFLOW_SIDE_EFFECTING),
    )(*ins, *landing)
    return res[:-1], res[-1]


def _exchange_wait(ex, handles, after, name):
    n_in, n_out, n_sem = len(ex["ins"]), len(ex["outs"]), len(ex["sems"])
    sems, thru = handles[:n_sem], handles[n_sem:]

    def body(*refs):
        in_refs, land_refs = refs[:n_in], refs[n_in:n_in + n_out]
        sem_refs = refs[n_in + n_out:n_in + n_out + n_sem]
        ex["finish"](in_refs, land_refs, *sem_refs)

    hbm = pl.BlockSpec(memory_space=pltpu.HBM)
    sem = pl.BlockSpec(memory_space=pltpu.SEMAPHORE)
    res = pl.pallas_call(
        body, name=name,
        out_shape=[pltpu.HBM(a.shape, a.dtype) for a in thru],
        in_specs=[hbm] * (n_in + n_out) + [sem] * n_sem + [pl.BlockSpec(memory_space=pl.ANY)],
        out_specs=[hbm] * (n_in + n_out),
        input_output_aliases={k: k for k in range(n_in + n_out)},
        compiler_params=pltpu.CompilerParams(has_side_effects=pltpu.SideEffectType.DATAFLOW_SIDE_EFFECTING),
    )(*thru, *sems, after)
    return res[:n_in], res[n_in:]


def _start_finish(build):
    def start(*refs):
        for cp in build(*refs):
            cp.start()

    def finish(*refs):
        for cp in build(*refs):
            cp.wait()

    return dict(start=start, finish=finish)


def _pair_exchange(grads):
    n = len(grads)

    def build(srcs, outs, send_sems, recv_sems):
        x, y, c = lax.axis_index("x"), lax.axis_index("y"), lax.axis_index("c")
        copies = []
        for t in range(n):
            rows = grads[t].shape[1] // 2
            copies.append(pltpu.make_async_remote_copy(
                src_ref=srcs[t].at[:, pl.ds((1 - c) * rows, rows)], dst_ref=outs[t],
                send_sem=send_sems.at[t], recv_sem=recv_sems.at[t], device_id=(x, y, 1 - c), device_id_type=MESH))
        return copies

    return dict(ins=list(grads), **_start_finish(build),
                outs=[jax.ShapeDtypeStruct((g.shape[0], g.shape[1] // 2, g.shape[2]), g.dtype) for g in grads],
                sems=[pltpu.SemaphoreType.DMA((n,)), pltpu.SemaphoreType.DMA((n,))])


def _pair_add(core, own, got):
    nchip, rows2, width = own.shape
    rows = rows2 // 2
    tr = min(512, rows)
    nb = rows // tr

    def body(core_ref, own_ref, got_ref, out_ref):
        out_ref[...] = (own_ref[...] + got_ref[...]).astype(BF16)

    grid_spec = pltpu.PrefetchScalarGridSpec(
        num_scalar_prefetch=1, grid=(nchip, nb),
        in_specs=[pl.BlockSpec((None, tr, width), lambda k, i, core_ref: (k, core_ref[0] * nb + i, 0)),
                  pl.BlockSpec((None, tr, width), lambda k, i, core_ref: (k, i, 0))],
        out_specs=pl.BlockSpec((None, tr, width), lambda k, i, core_ref: (k, i, 0)))
    return pl.pallas_call(
        body, name=f"pair_add_{width}", grid_spec=grid_spec,
        out_shape=jax.ShapeDtypeStruct((nchip, rows, width), BF16),
    )(core, own, got)


def _chip_exchange(parts):
    n = len(parts)

    def build(srcs, outs, send_sems, recv_sems, local_sems):
        x, y, c = lax.axis_index("x"), lax.axis_index("y"), lax.axis_index("c")
        my_chip = 2 * x + y
        chips = [(1 - x, y), (x, 1 - y), (1 - x, 1 - y)]
        copies = [pltpu.make_async_copy(srcs[t].at[my_chip], outs[t].at[my_chip], local_sems.at[t]) for t in range(n)]
        for j, (cx, cy) in enumerate(chips):
            for t in range(n):
                k = n * j + t
                copies.append(pltpu.make_async_remote_copy(
                    src_ref=srcs[t].at[2 * cx + cy], dst_ref=outs[t].at[my_chip], send_sem=send_sems.at[k],
                    recv_sem=recv_sems.at[k], device_id=(cx, cy, c), device_id_type=MESH))
        return copies

    return dict(ins=list(parts), **_start_finish(build), outs=[jax.ShapeDtypeStruct(p.shape, p.dtype) for p in parts],
                sems=[pltpu.SemaphoreType.DMA((3 * n,)), pltpu.SemaphoreType.DMA((3 * n,)),
                      pltpu.SemaphoreType.DMA((n,))])


def _slot_sum(slots, name, core=None):
    ns, rows, width = slots.shape
    tr = min(512, rows)

    def body(*refs):
        in_ref, out_ref = refs[-2:]
        acc = in_ref[0].astype(F32)
        for s in range(1, ns):
            acc = acc + in_ref[s].astype(F32)
        out_ref[...] = acc

    if core is None:
        return pl.pallas_call(
            body, name=name, grid=(rows // tr,),
            in_specs=[pl.BlockSpec((ns, tr, width), lambda i: (0, i, 0))],
            out_specs=pl.BlockSpec((tr, width), lambda i: (i, 0)),
            out_shape=jax.ShapeDtypeStruct((rows, width), F32),
        )(slots)
    grid_spec = pltpu.PrefetchScalarGridSpec(
        num_scalar_prefetch=1, grid=(rows // tr,),
        in_specs=[pl.BlockSpec((ns, tr, width), lambda i, core_ref: (0, i, 0))],
        out_specs=pl.BlockSpec((None, tr, width), lambda i, core_ref: (core_ref[0], i, 0)))
    return pl.pallas_call(
        body, name=name, grid_spec=grid_spec, out_shape=jax.ShapeDtypeStruct((2, rows, width), F32),
    )(core, slots)


def _pair_gather(bufs, small):
    n = len(bufs)

    def body(*refs):
        small_ref, outs, small_out = refs[n], refs[n + 1:2 * n + 1], refs[2 * n + 1]
        send_sems, recv_sems, local_sem = refs[2 * n + 2:]
        x, y, c = lax.axis_index("x"), lax.axis_index("y"), lax.axis_index("c")
        me = 4 * x + 2 * y + c
        chips = [(1 - x, y), (x, 1 - y), (1 - x, 1 - y)]
        mine = pltpu.make_async_copy(small_ref, small_out.at[me], local_sem)
        mine.start()
        copies = [pltpu.make_async_remote_copy(
            src_ref=outs[t].at[c], dst_ref=outs[t].at[c], send_sem=send_sems.at[t], recv_sem=recv_sems.at[t],
            device_id=(x, y, 1 - c), device_id_type=MESH) for t in range(n)]
        peers = [(x, y, 1 - c)] + [(cx, cy, cc) for (cx, cy) in chips for cc in (c, 1 - c)]
        for j, peer in enumerate(peers):
            copies.append(pltpu.make_async_remote_copy(
                src_ref=small_ref, dst_ref=small_out.at[me], send_sem=send_sems.at[n + j],
                recv_sem=recv_sems.at[n + j], device_id=peer, device_id_type=MESH))
        for cp in copies:
            cp.start()
        for cp in copies:
            cp.wait()
        mine.wait()

    any_spec = pl.BlockSpec(memory_space=pl.ANY)
    res = pl.pallas_call(
        body, name="pair_gather",
        out_shape=[jax.ShapeDtypeStruct(b.shape, b.dtype) for b in bufs]
        + [jax.ShapeDtypeStruct((8,) + small.shape, small.dtype)],
        in_specs=[any_spec] * (n + 1), out_specs=[any_spec] * (n + 1),
        input_output_aliases={t: t for t in range(n)},
        scratch_shapes=[pltpu.SemaphoreType.DMA((n + 7,)), pltpu.SemaphoreType.DMA((n + 7,)),
                        pltpu.SemaphoreType.DMA],
    )(*bufs, small)
    return [r.reshape(2 * b.shape[1], b.shape[2]) for r, b in zip(res[:n], bufs)], res[n]


def _adamw(w, g, m, v, name):
    rows, width = w.shape
    tr = min(256, rows)
    c1 = 1.0 / (1.0 - ADAM_B1 ** ADAM_STEP)
    c2 = 1.0 / (1.0 - ADAM_B2 ** ADAM_STEP)

    def body(w_ref, g_ref, m_ref, v_ref, d_ref, nm_ref, nv_ref):
        gv = g_ref[...]
        nm = ADAM_B1 * m_ref[...] + (1.0 - ADAM_B1) * gv
        nv = ADAM_B2 * v_ref[...] + (1.0 - ADAM_B2) * (gv * gv)
        nm_ref[...] = nm
        nv_ref[...] = nv
        d_ref[...] = -ADAM_LR * ((nm * c1) / (jnp.sqrt(nv * c2) + ADAM_EPS) + ADAM_WD * w_ref[...])

    spec = pl.BlockSpec((tr, width), lambda i: (i, 0))
    return pl.pallas_call(
        body, name=name, grid=(rows // tr,), in_specs=[spec] * 4, out_specs=[spec] * 3,
        out_shape=[jax.ShapeDtypeStruct(w.shape, F32)] * 3,
    )(w, g, m, v)


def _local_step(x, mem, target, pre_norm, sink_a, mem_norm, post_norm, w_in_g, w_out, w_mkv, gathers=None):
    first_gather, late_gather = gathers if gathers else (None, None)
    u, ut, hosted = _pre_norm(x, pre_norm, first_gather)
    if gathers:
        w_in_g = hosted[0].reshape(N_CHIPS, D_MODEL, SHARD_IN)
    pr = _pre_proj(u, w_in_g, late_gather)
    pr["ut"] = ut
    if gathers:
        w_out, w_mkv = (g.reshape(D_MODEL, g.shape[-1]) for g in pr["hosted"])
    mk, mv = _mem_kv(mem, mem_norm, w_mkv)
    sink = sink_a.reshape(-1)
    qa, ka, va = pr["qa"][None], pr["ka"][None], pr["va"][None]
    oa, lse_a = _band_fwd(qa, ka, va, sink, max_dist=A_WINDOW - 1, name="swa_fwd")
    ob_list, lseb_list = [], []
    for k, (win, dil) in enumerate(B_CONFIGS):
        o_i, l_i = _band_fwd(pr["qb"][k], pr["kb"][k], pr["vb"][k], None, max_dist=win // dil, name=f"dil{dil}_fwd")
        ob_list.append(o_i)
        lseb_list.append(l_i)
    oc, lse_c = _mem_attn_fwd(pr["qc"], mk, mv)
    sink_row = jnp.pad(sink, (0, LANES - sink.shape[0])).reshape(1, LANES)
    po = _post(x, target, post_norm, w_out, sink_row, oa[0], lse_a[0], pr["ga"], ob_list, lseb_list, pr["gb"], oc,
               pr["gc"])
    dqc, dmk, dmv = _mem_attn_bwd(pr["qc"], mk, mv, po["doc"], lse_c, po["dl_c"])
    dqa, dka, dva = _band_bwd(qa, ka, va, po["doa"][None], lse_a, po["dl_a"][None], max_dist=A_WINDOW - 1,
                              name="swa_bwd")
    res = dict(qb=[], kb=[], vb=[])
    for k, (win, dil) in enumerate(B_CONFIGS):
        dq_i, dk_i, dv_i = _band_bwd(pr["qb"][k], pr["kb"][k], pr["vb"][k], po["dob"][k], po["lse_b"][k],
                                     po["dl_b"][k], max_dist=win // dil, name=f"dil{dil}_bwd")
        res["qb"].append(dq_i)
        res["kb"].append(dk_i)
        res["vb"].append(dv_i)
    nat = dict(qa=dqa[0], ka=dka[0], va=dva[0], ga=po["dga"], gb=po["dgb"], qc=dqc, gc=po["dgc"])
    dproj, gw_in = _grad_w_in(pr["ut"], nat, res)
    gw_mkv, gmem = _mem_kv_bwd(mem, mem_norm, w_mkv, dmk, dmv)
    gsink = -po["gsink"][0, :sink.shape[0]]
    return dict(loss=po["loss"], g=po["g"], dproj=dproj, gw_in=gw_in, gw_out=po["gw_out"], gw_mkv=gw_mkv,
                gpost=po["gpost"], gmem=gmem, gsink=gsink, w_in_g=w_in_g)


def kernel(x, mem, pre_norm, w_in, sink_a, mem_norm, w_mem_kv, w_out, post_norm, loss_target, m_pre_norm, m_w_in, m_sink_a, m_mem_norm, m_w_mem_kv, m_w_out, m_post_norm, v_pre_norm, v_w_in, v_sink_a, v_mem_norm, v_w_mem_kv, v_w_out, v_post_norm):
    gathers = (_gather_exchange([w_in[0].astype(BF16)]),
               _gather_exchange([w_out[0].astype(BF16), w_mem_kv[0].astype(BF16)]))
    loc = _local_step(x[0], mem[0], loss_target[0], pre_norm, sink_a, mem_norm, post_norm, None, None, None, gathers)
    big = [loc["gw_in"], loc["gw_out"].reshape(N_CHIPS, D_MODEL // N_CHIPS, D_MODEL),
           loc["gw_mkv"].reshape(N_CHIPS, D_MODEL // N_CHIPS, 2 * C_W)]
    core = lax.axis_index("c").astype(jnp.int32).reshape(1)
    w_in_full = loc["w_in_g"]
    step_in = (x[0], loc["g"], pre_norm, w_in_full, loc["dproj"])
    pair_ex = _pair_exchange(big)
    pair_handles, token = _exchange_start(pair_ex, "pair_exchange_start")
    gx_a, gpre_a = _input_grad(*step_in, None, (0, 3), token, "input_grad_a")
    big, got = _exchange_wait(pair_ex, pair_handles, gpre_a, "pair_exchange_wait")
    parts = [_pair_add(core, own, g) for own, g in zip(big, got)]
    chip_ex = _chip_exchange(parts)
    chip_handles, token = _exchange_start(chip_ex, "chip_exchange_start")
    grad_x, gpre_b = _input_grad(*step_in, gx_a, (3, 13), token, "input_grad_b")
    _, slots = _exchange_wait(chip_ex, chip_handles, gpre_b, "chip_exchange_wait")
    halves = [_slot_sum(s, name=f"chip_sum_{s.shape[2]}", core=core) for s in slots]
    widen = lambda a: jnp.pad(a.reshape(1, -1), ((0, 0), (0, D_MODEL - a.size)))
    small = jnp.concatenate([gpre_a, loc["gpost"], loc["gmem"], widen(loc["gsink"]), widen(loc["loss"]), gpre_b,
                             jnp.zeros((2, D_MODEL), F32)], axis=0)
    (g_in, g_out, g_mkv), small_slots = _pair_gather(halves, small)
    small_sum = _slot_sum(small_slots, name="device_sum")
    g_pre, g_post, g_mem = small_sum[0:1] + small_sum[5:6], small_sum[1:2], small_sum[2:3]
    g_sink = small_sum[3:4, :sink_a.shape[1]]
    loss = small_sum[4, 0]

    d_in, nm_in, nv_in = _adamw(w_in[0], g_in, m_w_in[0], v_w_in[0], "adamw_in")
    d_out, nm_out, nv_out = _adamw(w_out[0], g_out, m_w_out[0], v_w_out[0], "adamw_out")
    d_mkv, nm_mkv, nv_mkv = _adamw(w_mem_kv[0], g_mkv, m_w_mem_kv[0], v_w_mem_kv[0], "adamw_mkv")
    pad6 = lambda a: jnp.pad(a, ((0, 0), (0, D_MODEL - a.shape[1])))
    stack = lambda a, b, c_, d_: jnp.concatenate([a, b, c_, pad6(d_), jnp.zeros((4, D_MODEL), F32)], axis=0)
    d_s, nm_s, nv_s = _adamw(stack(pre_norm, post_norm, mem_norm, sink_a),
                             jnp.concatenate([g_pre, small_sum[1:]], axis=0),
                             stack(m_pre_norm, m_post_norm, m_mem_norm, m_sink_a),
                             stack(v_pre_norm, v_post_norm, v_mem_norm, v_sink_a), "adamw_small")
    ns_ = sink_a.shape[1]
    unpack = lambda a: (a[0:1], a[3:4, :ns_], a[2:3], a[1:2])
    d_pre, d_sink, d_mem, d_post = unpack(d_s)
    nm_pre, nm_sink, nm_mem, nm_post = unpack(nm_s)
    nv_pre, nv_sink, nv_mem, nv_post = unpack(nv_s)
    lead = lambda a: a[None]
    return (loss, lead(grad_x),
            g_pre, lead(g_in), g_sink, g_mem, lead(g_mkv), lead(g_out), g_post,
            d_pre, lead(d_in), d_sink, d_mem, lead(d_mkv), lead(d_out), d_post,
            nm_pre, lead(nm_in), nm_sink, nm_mem, lead(nm_mkv), lead(nm_out), nm_post,
            nv_pre, lead(nv_in), nv_sink, nv_mem, lead(nv_mkv), lead(nv_out), nv_post)
```

```python
import numpy as np
import jax
import jax.numpy as jnp
from jax import lax
from jax.experimental import pallas as pl
from jax.experimental.pallas import tpu as pltpu

F32 = jnp.float32
BF16 = jnp.bfloat16

D_MODEL = 1024
HEAD_DIM = 64
LANES = 128
BLOCK = 128
A_W, A_KV_W, B_W, C_W = 384, 128, 384, 256
N_MEM = 256
D_IN = 3072
N_CHIPS = 4
SHARD_IN = D_IN // N_CHIPS
B_CONFIGS = ((128, 1), (512, 4), (2048, 16))
B_DILS = tuple(d for _, d in B_CONFIGS)
A_WINDOW = 128
RMS_EPS = 1e-6
ROPE_THETA = 500000.0
SCALE = HEAD_DIM ** -0.5
NEG = -1e30
ADAM_LR, ADAM_B1, ADAM_B2, ADAM_EPS, ADAM_WD, ADAM_STEP = 0.001, 0.9, 0.999, 1e-08, 0.01, 10

NT = (((1,), (1,)), ((), ()))
TN = (((0,), (0,)), ((), ()))
MESH = pl.DeviceIdType.MESH

_PROJ_LAYOUT = (
    [("qa", 128 * i, True, True) for i in range(3)] + [("ka", 0, True, False), ("va", 0, False, False)]
    + [("ga", 128 * i, False, False) for i in range(3)]
    + [("qb", 128 * i, True, True) for i in range(3)] + [("kb", 128 * i, True, False) for i in range(3)]
    + [("vb", 128 * i, False, False) for i in range(3)] + [("gb", 128 * i, False, False) for i in range(3)]
    + [("qc", 128 * i, False, True) for i in range(2)] + [("gc", 128 * i, False, False) for i in range(2)]
)
_PROJ_WIDTH = dict(qa=A_W, ka=A_KV_W, va=A_KV_W, ga=A_W, qb=B_W, kb=B_W, vb=B_W, gb=B_W, qc=C_W, gc=C_W)
_NATURAL = ("qa", "ka", "va", "ga", "gb", "qc", "gc")
_DILATED = ("qb", "kb", "vb")


def _dot(a, b):
    return jnp.dot(a, b, preferred_element_type=F32)


def _dot_nt(a, b):
    return lax.dot_general(a, b, NT, preferred_element_type=F32)


def _dot_tn(a, b):
    return lax.dot_general(a, b, TN, preferred_element_type=F32)


def _half_masks(rows):
    lane = lax.broadcasted_iota(jnp.int32, (rows, LANES), 1)
    return lane < HEAD_DIM, lane >= HEAD_DIM


def _rope(t, c, sm, sp):
    return t * c + pltpu.roll(t, LANES - 8, 1) * sm + pltpu.roll(t, 8, 1) * sp


def _rope_tables(seq, tm):
    dim = jnp.arange(LANES) % HEAD_DIM
    inv_freq = ROPE_THETA ** (-jnp.arange(0, 16, 2, dtype=F32) / 16)
    freq = jnp.where(dim < 16, inv_freq[dim % 8], 0.0)[None, :]
    local = jnp.arange(tm, dtype=F32)[:, None] * freq
    base = (jnp.arange(seq // tm, dtype=F32) * tm)[:, None] * freq
    both = lambda a: jnp.concatenate([jnp.cos(a), jnp.sin(a)], axis=1)
    return both(local), jnp.repeat(both(base), 8, axis=0)


def _rope_coeffs(local_ref, base_ref):
    cl, sl = local_ref[:, :LANES], local_ref[:, LANES:]
    cb, sb = base_ref[0:1, :LANES], base_ref[0:1, LANES:]
    cos = cb * cl - sb * sl
    sin = sb * cl + cb * sl
    dim = lax.broadcasted_iota(jnp.int32, (1, LANES), 1) % HEAD_DIM
    return cos, jnp.where(dim < 8, -sin, 0.0), jnp.where((dim >= 8) & (dim < 16), sin, 0.0)


def _split3(x):
    a = x.astype(BF16)
    r = x - a.astype(F32)
    b = r.astype(BF16)
    c = (r - b.astype(F32)).astype(BF16)
    return a, b, c


def _rows_to_lanes(x):
    row = lax.broadcasted_iota(jnp.int32, (8, LANES), 0)
    lane = lax.broadcasted_iota(jnp.int32, (8, LANES), 1)
    eye = (row == lane).astype(BF16)
    a, b, c = _split3(x)
    return _dot_nt(eye, a) + _dot_nt(eye, b) + _dot_nt(eye, c)


def _head_sum_matrix(width):
    k = lax.broadcasted_iota(jnp.int32, (width, LANES), 0)
    h = lax.broadcasted_iota(jnp.int32, (width, LANES), 1)
    return (k // HEAD_DIM == h).astype(BF16)


def _head_expand_matrix(width):
    h = lax.broadcasted_iota(jnp.int32, (LANES, width), 0)
    k = lax.broadcasted_iota(jnp.int32, (LANES, width), 1)
    return (k // HEAD_DIM == h).astype(BF16)


def _dot_split(x, mat, terms):
    parts = _split3(x)[:terms]
    out = _dot(parts[0], mat)
    for p in parts[1:]:
        out = out + _dot(p, mat)
    return out


def _per_head(cols, fill=0.0):
    rows = cols[0].shape[0]
    lane = lax.broadcasted_iota(jnp.int32, (rows, LANES), 1)
    out = jnp.full((rows, LANES), fill, F32)
    for h, col in enumerate(cols):
        out = jnp.where(lane == h, col, out)
    return out


def _lane_blocks(width):
    return [slice(p * LANES, (p + 1) * LANES) for p in range(width // LANES)]


def _stage(rows, width):
    return pltpu.VMEM((width // LANES, rows, LANES), F32)


def _stage_write(buf, value):
    for p, lanes in enumerate(_lane_blocks(value.shape[1])):
        buf[p] = value[:, lanes]


def _stage_read(buf):
    return jnp.concatenate([buf[p] for p in range(buf.shape[0])], axis=1) if buf.shape[0] > 1 else buf[0]


def _to_residues(buf, out_ref, dil):
    rows = buf.shape[1] // dil
    for r in range(dil):
        for p in range(buf.shape[0]):
            plane = buf.at[p]
            out_ref[r, :, p * LANES:(p + 1) * LANES] = plane[pl.ds(r, rows, stride=dil), :].astype(out_ref.dtype)


def _from_residues(in_ref, buf, dil):
    rows = buf.shape[1] // dil
    for r in range(dil):
        for p in range(buf.shape[0]):
            plane = buf.at[p]
            plane[pl.ds(r, rows, stride=dil), :] = in_ref[r, :, p * LANES:(p + 1) * LANES].astype(F32)


def _residue_spec(dil, tm, width):
    return pl.BlockSpec((dil, tm // dil, width), lambda i: (0, i, 0))


def _gather_exchange(shards_2d):
    shards = tuple(s.reshape(2, s.shape[0] // 2, s.shape[1]) for s in shards_2d)
    n = len(shards)

    def copies(in_refs, out_refs, send_sems, recv_sems):
        srcs, outs = in_refs[:n], out_refs
        x, y, c = lax.axis_index("x"), lax.axis_index("y"), lax.axis_index("c")
        my_chip = 2 * x + y
        sibling = (x, y, 1 - c)
        chips = [(1 - x, y), (x, 1 - y), (1 - x, 1 - y)]

        def copy(k, src, dst, to):
            return pltpu.make_async_remote_copy(src_ref=src, dst_ref=dst, send_sem=send_sems.at[k],
                                                recv_sem=recv_sems.at[k], device_id=to, device_id_type=MESH)

        first, arrive, passed, sibling_arrive = [], [], [], []
        for j, (cx, cy) in enumerate(chips):
            chip = 2 * cx + cy
            for t in range(n):
                k = n * j + t
                first.append(copy(k, srcs[t].at[c], outs[t].at[my_chip, c], (cx, cy, c)))
                arrive.append(copy(k, srcs[t].at[c], outs[t].at[chip, c], (cx, cy, c)))
                passed.append(copy(n * 3 + k, outs[t].at[chip, c], outs[t].at[chip, c], sibling))
                sibling_arrive.append(copy(n * 3 + k, outs[t].at[chip, 1 - c], outs[t].at[chip, 1 - c], sibling))
        return first, arrive, passed, sibling_arrive

    def start(*refs):
        for cp in copies(*refs)[0]:
            cp.start()

    def finish(*refs):
        first, arrive, passed, sibling_arrive = copies(*refs)
        for got, fwd in zip(arrive, passed):
            got.wait_recv()
            fwd.start()
        for cp in sibling_arrive:
            cp.wait_recv()
        for cp in first + passed:
            cp.wait_send()

    my_chip = 2 * lax.axis_index("x") + lax.axis_index("y")
    landing = [lax.dynamic_update_slice(jnp.zeros((N_CHIPS,) + s.shape, s.dtype), s[None], (my_chip, 0, 0, 0))
               for s in shards]
    return dict(ins=list(shards) + landing, start=start, finish=finish, aliases={n + t: t for t in range(n)},
                outs=[jax.ShapeDtypeStruct((N_CHIPS,) + s.shape, s.dtype) for s in shards],
                sems=[pltpu.SemaphoreType.DMA((6 * n,)), pltpu.SemaphoreType.DMA((6 * n,))])


def _run_exchange(ex, name):
    n_in, n_out = len(ex["ins"]), len(ex["outs"])

    def body(*refs):
        in_refs, out_refs, sems = refs[:n_in], refs[n_in:n_in + n_out], refs[n_in + n_out:]
        ex["start"](in_refs, out_refs, *sems)
        ex["finish"](in_refs, out_refs, *sems)

    any_spec = pl.BlockSpec(memory_space=pl.ANY)
    return pl.pallas_call(
        body, name=name, out_shape=ex["outs"], in_specs=[any_spec] * n_in, out_specs=[any_spec] * n_out,
        input_output_aliases=ex.get("aliases", {}), scratch_shapes=ex["sems"],
    )(*ex["ins"])


def _mem_kv(mem, mem_norm, w_mkv):
    def body(mem_ref, g_ref, w_ref, mk_ref, mv_ref):
        m = mem_ref[...]
        r = lax.rsqrt(jnp.mean(m * m, axis=-1, keepdims=True) + RMS_EPS)
        mn = (m * r * g_ref[...]).astype(BF16)
        kv = _dot(mn, w_ref[...])
        mk_ref[...] = kv[:, :C_W].astype(BF16)
        mv_ref[...] = kv[:, C_W:].astype(BF16)

    return pl.pallas_call(
        body, name="mem_kv",
        out_shape=[jax.ShapeDtypeStruct((N_MEM, C_W), BF16)] * 2,
    )(mem, mem_norm, w_mkv)


def _mem_kv_bwd(mem, mem_norm, w_mkv, dmk, dmv):
    def body(mem_ref, g_ref, w_ref, dmk_ref, dmv_ref, gw_ref, gn_ref):
        m = mem_ref[...]
        r = lax.rsqrt(jnp.mean(m * m, axis=-1, keepdims=True) + RMS_EPS)
        mhat = m * r
        mn = (mhat * g_ref[...]).astype(BF16)
        dkv = jnp.concatenate([dmk_ref[...], dmv_ref[...]], axis=1).astype(BF16)
        gw_ref[...] = _dot_tn(mn, dkv)
        dmn = _dot_nt(dkv, w_ref[...])
        gn_ref[...] = jnp.sum(dmn * mhat, axis=0, keepdims=True)

    return pl.pallas_call(
        body, name="mem_kv_bwd",
        out_shape=[jax.ShapeDtypeStruct((D_MODEL, 2 * C_W), F32), jax.ShapeDtypeStruct((1, D_MODEL), F32)],
    )(mem, mem_norm, w_mkv, dmk, dmv)


def _pre_norm(x, pre_norm, host=None):
    seq = x.shape[0]
    tm = min(512, seq)
    n_host_in = len(host["ins"]) if host else 0
    n_host_out = len(host["outs"]) if host else 0

    def body(x_ref, g_ref, *refs):
        host_in, (u_ref, ut_ref), refs = refs[:n_host_in], refs[n_host_in:n_host_in + 2], refs[n_host_in + 2:]
        host_out, sems = refs[:n_host_out], refs[n_host_out:]
        if host:
            @pl.when(pl.program_id(0) == 0)
            def _():
                host["start"](host_in, host_out, *sems)

        xv = x_ref[...]
        r = lax.rsqrt(jnp.mean(xv * xv, axis=-1, keepdims=True) + RMS_EPS)
        u = xv * r * g_ref[...]
        u_ref[...] = u.astype(BF16)
        ut_ref[...] = u.T.astype(BF16)
        if host:
            @pl.when(pl.program_id(0) == seq // tm - 1)
            def _():
                host["finish"](host_in, host_out, *sems)

    any_spec = pl.BlockSpec(memory_space=pl.ANY)
    ins = [x, pre_norm]
    in_specs = [pl.BlockSpec((tm, D_MODEL), lambda i: (i, 0)), pl.BlockSpec(pre_norm.shape, lambda i: (0, 0))]
    out_shape = [jax.ShapeDtypeStruct((seq, D_MODEL), BF16), jax.ShapeDtypeStruct((D_MODEL, seq), BF16)]
    out_specs = [pl.BlockSpec((tm, D_MODEL), lambda i: (i, 0)), pl.BlockSpec((D_MODEL, tm), lambda i: (0, i))]
    aliases, scratch = {}, []
    if host:
        aliases = {len(ins) + k: 2 + v for k, v in host.get("aliases", {}).items()}
        ins += list(host["ins"])
        in_specs += [any_spec] * n_host_in
        out_shape += list(host["outs"])
        out_specs += [any_spec] * n_host_out
        scratch = list(host["sems"])
    res = pl.pallas_call(
        body, name="pre_norm", grid=(seq // tm,), in_specs=in_specs, out_specs=out_specs, out_shape=out_shape,
        input_output_aliases=aliases, scratch_shapes=scratch,
        compiler_params=pltpu.CompilerParams(dimension_semantics=("arbitrary",)),
    )(*ins)
    return res[0], res[1], res[2:]


def _pre_proj(u, w_in_g, host=None):
    seq = u.shape[0]
    tm = min(512, seq)
    n_nat, n_dil = len(_NATURAL), len(_DILATED) * len(B_DILS)
    rope = _rope_tables(seq, tm)

    n_host_in = len(host["ins"]) if host else 0
    n_host_out = len(host["outs"]) if host else 0
    n_own_out = n_nat + n_dil

    def body(u_ref, w_ref, rl_ref, rb_ref, *refs):
        host_in, refs = refs[:n_host_in], refs[n_host_in:]
        nat = dict(zip(_NATURAL, refs[:n_nat]))
        res = {n: refs[n_nat + len(B_DILS) * k:n_nat + len(B_DILS) * (k + 1)] for k, n in enumerate(_DILATED)}
        host_out = refs[n_own_out:n_own_out + n_host_out]
        bufs = dict(zip(_DILATED, refs[n_own_out + n_host_out:]))
        sems = refs[n_own_out + n_host_out + len(_DILATED):]
        if host:
            @pl.when(pl.program_id(0) == 0)
            def _():
                host["start"](host_in, host_out, *sems)

        ub = u_ref[...]
        c, sm, sp = _rope_coeffs(rl_ref, rb_ref)
        for j in range(N_CHIPS):
            pj = _dot(ub, w_ref[j])
            for b in range(SHARD_IN // LANES):
                name, off, roped, scaled = _PROJ_LAYOUT[(SHARD_IN // LANES) * j + b]
                piece = pj[:, LANES * b:LANES * (b + 1)]
                if roped:
                    piece = _rope(piece, c, sm, sp)
                if scaled:
                    piece = piece * SCALE
                if name in bufs:
                    bufs[name][off // LANES] = piece
                else:
                    nat[name][:, off:off + LANES] = piece.astype(BF16)
        for name in _DILATED:
            for ref, dil in zip(res[name], B_DILS):
                _to_residues(bufs[name], ref, dil)
        if host:
            @pl.when(pl.program_id(0) == seq // tm - 1)
            def _():
                host["finish"](host_in, host_out, *sems)

    row = lambda w: pl.BlockSpec((tm, w), lambda i: (i, 0))
    full = lambda a: pl.BlockSpec(a.shape, lambda i: (0,) * a.ndim)
    any_spec = pl.BlockSpec(memory_space=pl.ANY)
    out_shape = [jax.ShapeDtypeStruct((seq, _PROJ_WIDTH[n]), BF16) for n in _NATURAL]
    out_specs = [row(_PROJ_WIDTH[n]) for n in _NATURAL]
    for n in _DILATED:
        for dil in B_DILS:
            out_shape.append(jax.ShapeDtypeStruct((dil, seq // dil, B_W), BF16))
            out_specs.append(_residue_spec(dil, tm, B_W))
    ins = [u, w_in_g, *rope]
    in_specs = [row(D_MODEL), full(w_in_g), full(rope[0]), pl.BlockSpec((8, 2 * LANES), lambda i: (i, 0))]
    scratch = [_stage(tm, B_W)] * len(_DILATED)
    aliases = {}
    if host:
        aliases = {len(ins) + k: n_own_out + v for k, v in host.get("aliases", {}).items()}
        ins += list(host["ins"])
        in_specs += [any_spec] * n_host_in
        out_shape += list(host["outs"])
        out_specs += [any_spec] * n_host_out
        scratch += list(host["sems"])
    res = pl.pallas_call(
        body, name="pre_proj", grid=(seq // tm,), in_specs=in_specs, out_specs=out_specs, out_shape=out_shape,
        input_output_aliases=aliases, scratch_shapes=scratch,
        compiler_params=pltpu.CompilerParams(dimension_semantics=("arbitrary",)),
    )(*ins)
    out = dict(zip(_NATURAL, res[:n_nat]))
    for k, n in enumerate(_DILATED):
        out[n] = res[n_nat + len(B_DILS) * k:n_nat + len(B_DILS) * (k + 1)]
    out["hosted"] = res[n_own_out:]
    return out


def _band_bias(max_dist, transposed):
    i = np.arange(BLOCK)[:, None]
    j = np.arange(BLOCK)[None, :]
    if transposed:
        same = i <= j
        other = (j + BLOCK - i) <= max_dist
        vis = np.concatenate([same, other], axis=1)
    else:
        prev = (i + BLOCK - j) <= max_dist
        same = j <= i
        vis = np.concatenate([prev, same], axis=1)
    return jnp.asarray(np.where(vis, 0.0, NEG).astype(np.float32))


def _kv_place(h, gqa):
    return (0, h // 3) if gqa else (h // 2, h % 2)


def _band_fwd(q, k, v, sink, *, max_dist, name):
    dil, length, wq = q.shape
    wk = k.shape[2]
    gqa = wk != wq
    tq = min(512, length)
    ns, nt = tq // BLOCK, length // tq
    npair = wq // LANES
    bias = _band_bias(max_dist, transposed=False)
    has_sink = sink is not None

    def body(*refs):
        if has_sink:
            sink_ref, refs = refs[0], refs[1:]
        q_ref, k_ref, kp_ref, v_ref, vp_ref, bias_ref, o_ref, lse_ref, kbuf, vbuf = refs[:10]
        i = pl.program_id(1)
        kbuf[0:BLOCK] = kp_ref[...]
        kbuf[BLOCK:] = k_ref[...]
        vbuf[0:BLOCK] = vp_ref[...]
        vbuf[BLOCK:] = v_ref[...]
        if gqa:
            kroll, vroll = refs[10:12]
            kroll[...] = pltpu.roll(kbuf[...], HEAD_DIM, 1)
            vroll[...] = pltpu.roll(vbuf[...], HEAD_DIM, 1)
        half = _half_masks(BLOCK)
        col_prev = (lax.broadcasted_iota(jnp.int32, (1, 2 * BLOCK), 1) < BLOCK).astype(F32)

        def sub(a, carry):
            r0 = pl.multiple_of(a * BLOCK, BLOCK)
            pen = jnp.where((i == 0) & (a == 0), NEG, 0.0)
            b = bias_ref[...] + pen * col_prev
            scores = []
            for p in range(npair):
                qp = q_ref[pl.ds(r0, BLOCK), p * LANES:(p + 1) * LANES]
                for e in range(2):
                    pk, ek = _kv_place(2 * p + e, gqa)
                    kw = (kbuf if ek == e else kroll)[pl.ds(r0, 2 * BLOCK), pk * LANES:(pk + 1) * LANES]
                    scores.append(_dot_nt(jnp.where(half[e], qp, jnp.zeros_like(qp)), kw))
            m_cols, l_cols, probs = [], [], []
            for h, s in enumerate(scores):
                s = s + b
                m = jnp.max(s, axis=1, keepdims=True)
                if has_sink:
                    m = jnp.maximum(m, sink_ref[h])
                pe = jnp.exp(s - m)
                l = jnp.sum(pe, axis=1, keepdims=True)
                if has_sink:
                    l = l + jnp.exp(sink_ref[h] - m)
                probs.append(pe.astype(BF16))
                m_cols.append(m)
                l_cols.append(l)
            for p in range(npair):
                o_h = []
                for e in range(2):
                    h = 2 * p + e
                    pk, ek = _kv_place(h, gqa)
                    vw = (vbuf if ek == e else vroll)[pl.ds(r0, 2 * BLOCK), pk * LANES:(pk + 1) * LANES]
                    o_h.append(_dot(probs[h], vw) * (1.0 / l_cols[h]))
                o_ref[pl.ds(r0, BLOCK), p * LANES:(p + 1) * LANES] = jnp.where(half[0], o_h[0], o_h[1]).astype(BF16)
            lse_ref[pl.ds(r0, BLOCK), :] = _per_head(m_cols) + jnp.log(_per_head(l_cols, 1.0))
            return carry

        lax.fori_loop(0, ns, sub, 0, unroll=True)

    main = lambda w: pl.BlockSpec((None, tq, w), lambda r, i: (r, i, 0))
    prev = lambda w: pl.BlockSpec((None, BLOCK, w), lambda r, i: (r, jnp.maximum(i * ns - 1, 0), 0))
    in_specs = [main(wq), main(wk), prev(wk), main(wk), prev(wk), pl.BlockSpec(bias.shape, lambda r, i: (0, 0))]
    args = [q, k, k, v, v, bias]
    if has_sink:
        in_specs = [pl.BlockSpec(memory_space=pltpu.SMEM)] + in_specs
        args = [sink] + args
    scratch = [pltpu.VMEM((tq + BLOCK, wk), BF16)] * (4 if gqa else 2)
    return pl.pallas_call(
        body, name=name, grid=(dil, nt), in_specs=in_specs,
        out_specs=[main(wq), main(LANES)],
        out_shape=[jax.ShapeDtypeStruct((dil, length, wq), BF16), jax.ShapeDtypeStruct((dil, length, LANES), F32)],
        scratch_shapes=scratch,
    )(*args)


def _band_bwd(q, k, v, do, lse, delta, *, max_dist, name):
    dil, length, wq = q.shape
    wk = k.shape[2]
    gqa = wk != wq
    tq = min(512, length)
    ns, nt = tq // BLOCK, length // tq
    npair = wq // LANES
    nblocks = length // BLOCK
    bias = _band_bias(max_dist, transposed=True)

    def body(q_ref, qn_ref, do_ref, don_ref, lse_ref, lsen_ref, dl_ref, dln_ref, k_ref, v_ref, bias_ref,
             dq_ref, dk_ref, dv_ref, stat_l, stat_d, dqt, kt, *rolled):
        i = pl.program_id(1)
        for pk in range(wk // LANES):
            kt[pk] = k_ref[:, pk * LANES:(pk + 1) * LANES].astype(F32).T.astype(BF16)
        if gqa:
            kroll, vroll, ktroll = rolled
            kroll[...] = pltpu.roll(k_ref[...], HEAD_DIM, 1)
            vroll[...] = pltpu.roll(v_ref[...], HEAD_DIM, 1)
            ktroll[0] = kroll[...].astype(F32).T.astype(BF16)
        for a in range(ns):
            rows = slice(a * BLOCK, (a + 1) * BLOCK)
            stat_l[a] = _rows_to_lanes(lse_ref[rows, :])
            stat_d[a] = _rows_to_lanes(dl_ref[rows, :])
        stat_l[ns] = _rows_to_lanes(lsen_ref[...])
        stat_d[ns] = _rows_to_lanes(dln_ref[...])

        @pl.when(i == 0)
        def _():
            dqt[:, :, 0:BLOCK] = jnp.zeros((npair, LANES, BLOCK), F32)

        @pl.when(i > 0)
        def _():
            dqt[:, :, 0:BLOCK] = dqt[:, :, tq:tq + BLOCK]

        dqt[:, :, BLOCK:] = jnp.zeros((npair, LANES, tq), F32)
        half2 = _half_masks(2 * BLOCK)
        row = lax.broadcasted_iota(jnp.int32, (LANES, BLOCK), 0)
        row_half = (row < HEAD_DIM, row >= HEAD_DIM)
        col_next = (lax.broadcasted_iota(jnp.int32, (1, 2 * BLOCK), 1) >= BLOCK).astype(F32)

        for b in range(ns):
            rows = slice(b * BLOCK, (b + 1) * BLOCK)
            window = slice(b * BLOCK, (b + 2) * BLOCK)
            bt = bias_ref[...]
            if b == ns - 1:
                bt = bt + jnp.where(i == nt - 1, NEG, 0.0) * col_next
            acc = {}
            items = []
            nxt_rows = slice((b + 1) * BLOCK, (b + 2) * BLOCK)
            for p in range(npair):
                lanes = slice(p * LANES, (p + 1) * LANES)
                q_next = q_ref[nxt_rows, lanes] if b + 1 < ns else qn_ref[:, lanes]
                do_next = do_ref[nxt_rows, lanes] if b + 1 < ns else don_ref[:, lanes]
                qw = jnp.concatenate([q_ref[rows, lanes], q_next], axis=0)
                dow = jnp.concatenate([do_ref[rows, lanes], do_next], axis=0)
                for e in range(2):
                    h = 2 * p + e
                    pk, ek = _kv_place(h, gqa)
                    klanes = slice(pk * LANES, (pk + 1) * LANES)
                    kb = (k_ref if ek == e else kroll)[rows, klanes]
                    vb = (v_ref if ek == e else vroll)[rows, klanes]
                    qm = jnp.where(half2[e], qw, jnp.zeros_like(qw))
                    dom = jnp.where(half2[e], dow, jnp.zeros_like(dow))
                    items.append(dict(p=p, e=e, h=h, pk=pk, ek=ek, qm=qm, dom=dom,
                                      st=_dot_nt(kb, qm), dpt=_dot_nt(vb, dom)))
            for it in items:
                h = it["h"]
                lrow = jnp.concatenate([stat_l[b, h:h + 1, :], stat_l[b + 1, h:h + 1, :]], axis=1)
                drow = jnp.concatenate([stat_d[b, h:h + 1, :], stat_d[b + 1, h:h + 1, :]], axis=1)
                pt = jnp.exp(it["st"] + bt - lrow)
                it["ptb"] = pt.astype(BF16)
                it["dsb"] = (pt * (it["dpt"] - drow)).astype(BF16)
            for p in range(npair):
                pair = items[2 * p:2 * p + 2]
                lanes = slice(p * LANES, (p + 1) * LANES)
                kparts = []
                for it in pair:
                    kbt = (kt if it["ek"] == it["e"] else ktroll)[it["pk"], :, rows]
                    kparts.append(jnp.where(row_half[it["e"]], kbt, jnp.zeros_like(kbt)))
                ds_keys = jnp.concatenate([it["dsb"] for it in pair], axis=0)
                dqt[p, :, window] += _dot(jnp.concatenate(kparts, axis=1), ds_keys)
                if not gqa:
                    q_both = jnp.concatenate([it["qm"] for it in pair], axis=0)
                    do_both = jnp.concatenate([it["dom"] for it in pair], axis=0)
                    dk_ref[rows, lanes] = _dot(jnp.concatenate([it["dsb"] for it in pair], axis=1), q_both).astype(BF16)
                    dv_ref[rows, lanes] = _dot(jnp.concatenate([it["ptb"] for it in pair], axis=1), do_both).astype(BF16)
                else:
                    for it in pair:
                        dv_c = _dot(it["ptb"], it["dom"])
                        dk_c = _dot(it["dsb"], it["qm"])
                        key = (it["pk"], it["ek"] == it["e"])
                        if key in acc:
                            acc[key] = (acc[key][0] + dk_c, acc[key][1] + dv_c)
                        else:
                            acc[key] = (dk_c, dv_c)
            if gqa:
                dk_al, dv_al = acc[(0, True)]
                dk_mis, dv_mis = acc[(0, False)]
                dk_ref[rows, :] = (dk_al + pltpu.roll(dk_mis, HEAD_DIM, 1)).astype(BF16)
                dv_ref[rows, :] = (dv_al + pltpu.roll(dv_mis, HEAD_DIM, 1)).astype(BF16)

        for p in range(npair):
            dq_ref[:, p * LANES:(p + 1) * LANES] = dqt[p, :, 0:tq].T.astype(BF16)

    main = lambda w: pl.BlockSpec((None, tq, w), lambda r, i: (r, i, 0))
    nxt = lambda w: pl.BlockSpec((None, BLOCK, w), lambda r, i: (r, jnp.minimum((i + 1) * ns, nblocks - 1), 0))
    scratch = [pltpu.VMEM((ns + 1, 8, LANES), F32), pltpu.VMEM((ns + 1, 8, LANES), F32),
               pltpu.VMEM((npair, LANES, tq + BLOCK), F32), pltpu.VMEM((wk // LANES, LANES, tq), BF16)]
    if gqa:
        scratch = scratch + [pltpu.VMEM((tq, wk), BF16)] * 2 + [pltpu.VMEM((1, LANES, tq), BF16)]
    return pl.pallas_call(
        body, name=name, grid=(dil, nt),
        in_specs=[main(wq), nxt(wq), main(wq), nxt(wq), main(LANES), nxt(LANES), main(LANES), nxt(LANES),
                  main(wk), main(wk), pl.BlockSpec(bias.shape, lambda r, i: (0, 0))],
        out_specs=[main(wq), main(wk), main(wk)],
        out_shape=[jax.ShapeDtypeStruct((dil, length, wq), BF16), jax.ShapeDtypeStruct((dil, length, wk), BF16),
                   jax.ShapeDtypeStruct((dil, length, wk), BF16)],
        scratch_shapes=scratch,
        compiler_params=pltpu.CompilerParams(dimension_semantics=("arbitrary", "arbitrary")),
    )(q, q, do, do, lse, lse, delta, delta, k, v, bias)


def _mem_attn_fwd(q, mk, mv):
    seq = q.shape[0]
    tq = min(512, seq)
    ns = tq // BLOCK

    def body(q_ref, mk_ref, mv_ref, o_ref, lse_ref):
        half = _half_masks(BLOCK)

        def sub(a, carry):
            r0 = pl.multiple_of(a * BLOCK, BLOCK)
            scores = []
            for p in range(C_W // LANES):
                lanes = slice(p * LANES, (p + 1) * LANES)
                qp = q_ref[pl.ds(r0, BLOCK), lanes]
                for e in range(2):
                    scores.append(_dot_nt(jnp.where(half[e], qp, jnp.zeros_like(qp)), mk_ref[:, lanes]))
            m_cols, l_cols, probs = [], [], []
            for s in scores:
                m = jnp.max(s, axis=1, keepdims=True)
                pe = jnp.exp(s - m)
                probs.append(pe.astype(BF16))
                m_cols.append(m)
                l_cols.append(jnp.sum(pe, axis=1, keepdims=True))
            for p in range(C_W // LANES):
                lanes = slice(p * LANES, (p + 1) * LANES)
                o_h = [_dot(probs[2 * p + e], mv_ref[:, lanes]) * (1.0 / l_cols[2 * p + e]) for e in range(2)]
                o_ref[pl.ds(r0, BLOCK), lanes] = jnp.where(half[0], o_h[0], o_h[1]).astype(BF16)
            lse_ref[pl.ds(r0, BLOCK), :] = _per_head(m_cols) + jnp.log(_per_head(l_cols, 1.0))
            return carry

        lax.fori_loop(0, ns, sub, 0, unroll=True)

    row = lambda w: pl.BlockSpec((tq, w), lambda i: (i, 0))
    full = pl.BlockSpec((N_MEM, C_W), lambda i: (0, 0))
    return pl.pallas_call(
        body, name="mem_attn_fwd", grid=(seq // tq,), in_specs=[row(C_W), full, full],
        out_specs=[row(C_W), row(LANES)],
        out_shape=[jax.ShapeDtypeStruct((seq, C_W), BF16), jax.ShapeDtypeStruct((seq, LANES), F32)],
    )(q, mk, mv)


def _mem_attn_bwd(q, mk, mv, do, lse, delta):
    seq = q.shape[0]
    tq = min(512, seq)
    ns = tq // BLOCK
    npair = C_W // LANES

    def body(q_ref, mk_ref, mv_ref, do_ref, lse_ref, dl_ref, dq_ref, dmk_ref, dmv_ref, stat_l, stat_d, mkt, dqt):
        @pl.when(pl.program_id(0) == 0)
        def _():
            dmk_ref[...] = jnp.zeros_like(dmk_ref)
            dmv_ref[...] = jnp.zeros_like(dmv_ref)
            for p in range(npair):
                mkt[p] = mk_ref[:, p * LANES:(p + 1) * LANES].astype(F32).T.astype(BF16)

        for a in range(ns):
            rows = slice(a * BLOCK, (a + 1) * BLOCK)
            stat_l[a] = _rows_to_lanes(lse_ref[rows, :])
            stat_d[a] = _rows_to_lanes(dl_ref[rows, :])
        half = _half_masks(BLOCK)
        row = lax.broadcasted_iota(jnp.int32, (LANES, N_MEM), 0)
        row_half = (row < HEAD_DIM, row >= HEAD_DIM)

        for a in range(ns):
            rows = slice(a * BLOCK, (a + 1) * BLOCK)
            items = []
            for p in range(npair):
                lanes = slice(p * LANES, (p + 1) * LANES)
                qp = q_ref[rows, lanes]
                dop = do_ref[rows, lanes]
                for e in range(2):
                    qm = jnp.where(half[e], qp, jnp.zeros_like(qp))
                    dom = jnp.where(half[e], dop, jnp.zeros_like(dop))
                    items.append(dict(p=p, e=e, qm=qm, dom=dom, st=_dot_nt(mk_ref[:, lanes], qm),
                                      dpt=_dot_nt(mv_ref[:, lanes], dom)))
            for it in items:
                h = 2 * it["p"] + it["e"]
                pt = jnp.exp(it["st"] - stat_l[a, h:h + 1, :])
                it["ptb"] = pt.astype(BF16)
                it["dsb"] = (pt * (it["dpt"] - stat_d[a, h:h + 1, :])).astype(BF16)
            for p in range(npair):
                lanes = slice(p * LANES, (p + 1) * LANES)
                pair = [it for it in items if it["p"] == p]
                join = lambda name, axis: jnp.concatenate([it[name] for it in pair], axis=axis)
                dmv_ref[:, lanes] += _dot(join("ptb", 1), join("dom", 0))
                dmk_ref[:, lanes] += _dot(join("dsb", 1), join("qm", 0))
                kbt = mkt[p]
                k_both = jnp.concatenate([jnp.where(row_half[e], kbt, jnp.zeros_like(kbt)) for e in range(2)], axis=1)
                dqt[p, :, rows] = _dot(k_both, join("dsb", 0))
        for p in range(npair):
            dq_ref[:, p * LANES:(p + 1) * LANES] = dqt[p].T.astype(BF16)

    row = lambda w: pl.BlockSpec((tq, w), lambda i: (i, 0))
    full = pl.BlockSpec((N_MEM, C_W), lambda i: (0, 0))
    return pl.pallas_call(
        body, name="mem_attn_bwd", grid=(seq // tq,),
        in_specs=[row(C_W), full, full, row(C_W), row(LANES), row(LANES)], out_specs=[row(C_W), full, full],
        out_shape=[jax.ShapeDtypeStruct((seq, C_W), BF16), jax.ShapeDtypeStruct((N_MEM, C_W), F32),
                   jax.ShapeDtypeStruct((N_MEM, C_W), F32)],
        scratch_shapes=[pltpu.VMEM((ns, 8, LANES), F32)] * 2
        + [pltpu.VMEM((npair, LANES, N_MEM), BF16), pltpu.VMEM((npair, LANES, tq), F32)],
        compiler_params=pltpu.CompilerParams(dimension_semantics=("arbitrary",)),
    )(q, mk, mv, do, lse, delta)


def _silu_and_grad(g):
    s = 1.0 / (1.0 + jnp.exp(-g))
    return g * s, s * (1.0 + g * (1.0 - s))


def _post(x, target, post_norm, w_out, sink_row, oa, lse_a, ga, ob_list, lseb_list, gb, oc, gc):
    seq = x.shape[0]
    tm = min(512, seq)
    inv_d = 1.0 / D_MODEL
    nd = len(B_DILS)

    def body(*refs):
        (x_ref, t_ref, gp_ref, w_ref, sink_ref, oa_ref, lsea_ref, ga_ref), refs = refs[:8], refs[8:]
        ob_refs, lb_refs, (gb_ref, oc_ref, gc_ref), refs = refs[:nd], refs[nd:2 * nd], refs[2 * nd:2 * nd + 3], refs[2 * nd + 3:]
        (g_ref, doa_ref, dla_ref, dga_ref), refs = refs[:4], refs[4:]
        dob_refs, lsec_refs, dlb_refs, refs = refs[:nd], refs[nd:2 * nd], refs[2 * nd:3 * nd], refs[3 * nd:]
        (dgb_ref, doc_ref, dlc_ref, dgc_ref, gw_ref, gpost_ref, gsink_ref, loss_ref), refs = refs[:8], refs[8:]
        ycat, obufs, lbufs, st_do, st_l, st_d = refs[0], refs[1:nd], refs[nd:2 * nd - 1], refs[2 * nd - 1], refs[2 * nd], refs[2 * nd + 1]

        @pl.when(pl.program_id(0) == 0)
        def _():
            gw_ref[...] = jnp.zeros_like(gw_ref)
            gpost_ref[...] = jnp.zeros_like(gpost_ref)
            gsink_ref[...] = jnp.zeros_like(gsink_ref)
            loss_ref[...] = jnp.zeros_like(loss_ref)

        o_i, l_i = [ob_refs[0][0].astype(F32)], [lb_refs[0][0]]
        for k in range(1, nd):
            _from_residues(ob_refs[k], obufs[k - 1], B_DILS[k])
            _from_residues(lb_refs[k], lbufs[k - 1], B_DILS[k])
            o_i.append(_stage_read(obufs[k - 1]))
            l_i.append(_stage_read(lbufs[k - 1]))
        mx = l_i[0]
        for l in l_i[1:]:
            mx = jnp.maximum(mx, l)
        w_i = [jnp.exp(l - mx) for l in l_i]
        z = w_i[0]
        for w in w_i[1:]:
            z = z + w
        _stage_write(st_l, mx + jnp.log(z))
        expand = _head_expand_matrix(B_W)
        inv_z = 1.0 / z
        ob = None
        for w, o in zip(w_i, o_i):
            term = _dot_split(w * inv_z, expand, 2) * o
            ob = term if ob is None else ob + term
        oa, oc = oa_ref[...].astype(F32), oc_ref[...].astype(F32)
        sa, dsa = _silu_and_grad(ga_ref[...].astype(F32))
        sb, dsb = _silu_and_grad(gb_ref[...].astype(F32))
        sc, dsc = _silu_and_grad(gc_ref[...].astype(F32))
        ycat[:, 0:A_W] = (oa * sa).astype(BF16)
        ycat[:, A_W:A_W + B_W] = (ob * sb).astype(BF16)
        ycat[:, A_W + B_W:] = (oc * sc).astype(BF16)
        y2 = _dot(ycat[...], w_ref[...])
        r = lax.rsqrt(jnp.mean(y2 * y2, axis=-1, keepdims=True) + RMS_EPS)
        zhat = y2 * r
        gp = gp_ref[...]
        err = x_ref[...] + zhat * gp - t_ref[...]
        loss_ref[...] += jnp.sum(err * err) * (0.5 * inv_d)
        g = err * inv_d
        g_ref[...] = g
        gpost_ref[...] += jnp.sum(g * zhat, axis=0, keepdims=True)
        a = g * gp
        dy2 = (r * (a - zhat * jnp.mean(a * zhat, axis=-1, keepdims=True))).astype(BF16)
        gw_ref[...] += _dot_tn(ycat[...], dy2)
        dycat = _dot_nt(dy2, w_ref[...])
        dya, dyb, dyc = dycat[:, 0:A_W], dycat[:, A_W:A_W + B_W], dycat[:, A_W + B_W:]
        doa, dob, doc = dya * sa, dyb * sb, dyc * sc
        doa_ref[...] = doa.astype(BF16)
        doc_ref[...] = doc.astype(BF16)
        dga_ref[...] = (dya * oa * dsa).astype(BF16)
        dgb_ref[...] = (dyb * ob * dsb).astype(BF16)
        dgc_ref[...] = (dyc * oc * dsc).astype(BF16)
        dl_a = _dot_split(doa * oa, _head_sum_matrix(A_W), 2)
        dla_ref[...] = dl_a
        dlc_ref[...] = _dot_split(doc * oc, _head_sum_matrix(C_W), 2)
        gsink_ref[...] += jnp.sum(jnp.exp(sink_ref[...] - lsea_ref[...]) * dl_a, axis=0, keepdims=True)
        _stage_write(st_do, dob)
        _stage_write(st_d, _dot_split(dob * ob, _head_sum_matrix(B_W), 2))
        for k, dil in enumerate(B_DILS):
            _to_residues(st_do, dob_refs[k], dil)
            _to_residues(st_l, lsec_refs[k], dil)
            _to_residues(st_d, dlb_refs[k], dil)

    row = lambda w: pl.BlockSpec((tm, w), lambda i: (i, 0))
    full = lambda shape: pl.BlockSpec(shape, lambda i: (0,) * len(shape))
    res_specs = lambda w: [_residue_spec(d, tm, w) for d in B_DILS]
    res_shapes = lambda w, dt: [jax.ShapeDtypeStruct((d, seq // d, w), dt) for d in B_DILS]
    ins = [x, target, post_norm, w_out, sink_row, oa, lse_a, ga, *ob_list, *lseb_list, gb, oc, gc]
    in_specs = ([row(D_MODEL), row(D_MODEL), full((1, D_MODEL)), full((D_MODEL, D_MODEL)), full((1, LANES)),
                 row(A_W), row(LANES), row(A_W)] + res_specs(B_W) + res_specs(LANES) + [row(B_W), row(C_W), row(C_W)])
    out_shape = ([jax.ShapeDtypeStruct((seq, D_MODEL), F32), jax.ShapeDtypeStruct((seq, A_W), BF16),
                  jax.ShapeDtypeStruct((seq, LANES), F32), jax.ShapeDtypeStruct((seq, A_W), BF16)]
                 + res_shapes(B_W, BF16) + res_shapes(LANES, F32) + res_shapes(LANES, F32)
                 + [jax.ShapeDtypeStruct((seq, B_W), BF16), jax.ShapeDtypeStruct((seq, C_W), BF16),
                    jax.ShapeDtypeStruct((seq, LANES), F32), jax.ShapeDtypeStruct((seq, C_W), BF16),
                    jax.ShapeDtypeStruct((D_MODEL, D_MODEL), F32), jax.ShapeDtypeStruct((1, D_MODEL), F32),
                    jax.ShapeDtypeStruct((1, LANES), F32), jax.ShapeDtypeStruct((1, LANES), F32)])
    out_specs = ([row(D_MODEL), row(A_W), row(LANES), row(A_W)] + res_specs(B_W) + res_specs(LANES) + res_specs(LANES)
                 + [row(B_W), row(C_W), row(LANES), row(C_W),
                    full((D_MODEL, D_MODEL)), full((1, D_MODEL)), full((1, LANES)), full((1, LANES))])
    scratch = ([pltpu.VMEM((tm, D_MODEL), BF16)] + [_stage(tm, B_W)] * (nd - 1) + [_stage(tm, LANES)] * (nd - 1)
               + [_stage(tm, B_W), _stage(tm, LANES), _stage(tm, LANES)])
    res = pl.pallas_call(
        body, name="post", grid=(seq // tm,), in_specs=in_specs, out_specs=out_specs, out_shape=out_shape,
        scratch_shapes=scratch,
        compiler_params=pltpu.CompilerParams(dimension_semantics=("arbitrary",)),
    )(*ins)
    out = dict(g=res[0], doa=res[1], dl_a=res[2], dga=res[3], dob=res[4:4 + nd], lse_b=res[4 + nd:4 + 2 * nd],
               dl_b=res[4 + 2 * nd:4 + 3 * nd])
    rest = res[4 + 3 * nd:]
    out.update(dgb=rest[0], doc=rest[1], dl_c=rest[2], dgc=rest[3], gw_out=rest[4], gpost=rest[5], gsink=rest[6],
               loss=rest[7])
    return out


def _grad_w_in(ut, nat, res):
    seq = ut.shape[1]
    tm = min(512, seq)
    nd = len(B_DILS)
    nat_list = [nat[n] for n in _NATURAL]
    res_list = [a for n in _DILATED for a in res[n]]
    rope = _rope_tables(seq, tm)

    def body(rl_ref, rb_ref, ut_ref, *refs):
        nat_refs = dict(zip(_NATURAL, refs[:len(_NATURAL)]))
        refs = refs[len(_NATURAL):]
        res_refs = {n: refs[nd * k:nd * (k + 1)] for k, n in enumerate(_DILATED)}
        refs = refs[nd * len(_DILATED):]
        dproj_ref, gw_ref = refs[:2]
        bufs = {n: refs[2 + (nd - 1) * k:2 + (nd - 1) * (k + 1)] for k, n in enumerate(_DILATED)}

        @pl.when(pl.program_id(0) == 0)
        def _():
            gw_ref[...] = jnp.zeros_like(gw_ref)

        for n in _DILATED:
            for k in range(1, nd):
                _from_residues(res_refs[n][k], bufs[n][k - 1], B_DILS[k])
        c, sm, sp = _rope_coeffs(rl_ref, rb_ref)
        sm, sp = -sm, -sp
        for blk, (name, off, roped, scaled) in enumerate(_PROJ_LAYOUT):
            lanes = slice(off, off + LANES)
            if name in nat_refs:
                piece = nat_refs[name][:, lanes].astype(F32)
            else:
                piece = res_refs[name][0][0, :, lanes].astype(F32)
                for buf in bufs[name]:
                    piece = piece + buf[off // LANES]
            if roped:
                piece = _rope(piece, c, sm, sp)
            if scaled:
                piece = piece * SCALE
            dproj_ref[:, blk * LANES:(blk + 1) * LANES] = piece.astype(BF16)
        for j in range(N_CHIPS):
            gw_ref[j] += _dot(ut_ref[...], dproj_ref[:, j * SHARD_IN:(j + 1) * SHARD_IN])

    row = lambda w: pl.BlockSpec((tm, w), lambda i: (i, 0))
    in_specs = ([pl.BlockSpec(rope[0].shape, lambda i: (0, 0)), pl.BlockSpec((8, 2 * LANES), lambda i: (i, 0)),
                 pl.BlockSpec((D_MODEL, tm), lambda i: (0, i))]
                + [row(a.shape[1]) for a in nat_list]
                + [_residue_spec(d, tm, B_W) for _ in _DILATED for d in B_DILS])
    return pl.pallas_call(
        body, name="grad_w_in", grid=(seq // tm,), in_specs=in_specs,
        out_specs=[row(D_IN), pl.BlockSpec((N_CHIPS, D_MODEL, SHARD_IN), lambda i: (0, 0, 0))],
        out_shape=[jax.ShapeDtypeStruct((seq, D_IN), BF16), jax.ShapeDtypeStruct((N_CHIPS, D_MODEL, SHARD_IN), F32)],
        scratch_shapes=[_stage(tm, B_W)] * ((nd - 1) * len(_DILATED)),
        compiler_params=pltpu.CompilerParams(dimension_semantics=("arbitrary",)),
    )(*rope, ut, *nat_list, *res_list)


def _input_grad(x, g, pre_norm, w_in_g, dproj, gx_prev, span, after, name):
    seq = x.shape[0]
    tm = seq // 16
    first_block, steps = span

    def body(*refs):
        x_ref, g_ref, gp_ref, w_ref, dp_ref = refs[:5]
        gx_ref, gpre_ref = refs[-2:]

        @pl.when(pl.program_id(0) == 0)
        def _():
            gpre_ref[...] = jnp.zeros_like(gpre_ref)

        du = None
        for j in range(N_CHIPS):
            term = _dot_nt(dp_ref[:, j * SHARD_IN:(j + 1) * SHARD_IN], w_ref[j])
            du = term if du is None else du + term
        xv = x_ref[...]
        r = lax.rsqrt(jnp.mean(xv * xv, axis=-1, keepdims=True) + RMS_EPS)
        xhat = xv * r
        gpre_ref[...] += jnp.sum(du * xhat, axis=0, keepdims=True)
        a = du * gp_ref[...]
        gx_ref[...] = g_ref[...] + r * (a - xhat * jnp.mean(a * xhat, axis=-1, keepdims=True))

    row = lambda w: pl.BlockSpec((tm, w), lambda i: (first_block + i, 0))
    full = lambda a: pl.BlockSpec(a.shape, lambda i: (0,) * a.ndim)
    any_spec = pl.BlockSpec(memory_space=pl.ANY)
    ins = [x, g, pre_norm, w_in_g, dproj]
    in_specs = [row(D_MODEL), row(D_MODEL), full(pre_norm), full(w_in_g), row(D_IN)]
    aliases = {}
    if gx_prev is not None:
        aliases[len(ins)] = 0
        ins.append(gx_prev)
        in_specs.append(any_spec)
    if after is not None:
        ins.append(after)
        in_specs.append(any_spec)
    return pl.pallas_call(
        body, name=name, grid=(steps,), in_specs=in_specs,
        out_specs=[row(D_MODEL), pl.BlockSpec((1, D_MODEL), lambda i: (0, 0))],
        out_shape=[jax.ShapeDtypeStruct((seq, D_MODEL), F32), jax.ShapeDtypeStruct((1, D_MODEL), F32)],
        input_output_aliases=aliases,
        compiler_params=pltpu.CompilerParams(dimension_semantics=("arbitrary",)),
    )(*ins)


def _exchange_start(ex, name):
    n_in, n_out, n_sem = len(ex["ins"]), len(ex["outs"]), len(ex["sems"])

    def body(*refs):
        in_refs, land_refs, sems = refs[:n_in], refs[n_in:n_in + n_out], refs[n_in + n_out:n_in + n_out + n_sem]
        ex["start"](in_refs, land_refs, *sems)
        token = refs[-1]
        token[...] = jnp.zeros_like(token)

    hbm = pl.BlockSpec(memory_space=pltpu.HBM)
    sem = pl.BlockSpec(memory_space=pltpu.SEMAPHORE)
    ins = [pltpu.with_memory_space_constraint(a, pltpu.HBM) for a in ex["ins"]]
    landing = [pltpu.with_memory_space_constraint(lax.empty(o.shape, o.dtype), pltpu.HBM) for o in ex["outs"]]
    res = pl.pallas_call(
        body, name=name,
        out_shape=list(ex["sems"]) + [pltpu.HBM(a.shape, a.dtype) for a in ex["ins"]]
        + [pltpu.HBM(o.shape, o.dtype) for o in ex["outs"]] + [jax.ShapeDtypeStruct((8, LANES), F32)],
        in_specs=[hbm] * (n_in + n_out),
        out_specs=[sem] * n_sem + [hbm] * (n_in + n_out) + [pl.BlockSpec(memory_space=pltpu.VMEM)],
        input_output_aliases={k: n_sem + k for k in range(n_in + n_out)},
        compiler_params=pltpu.CompilerParams(has_side_effects=pltpu.SideEffectType.DATAFLOW_SIDE_EFFECTING),
    )(*ins, *landing)
    return res[:-1], res[-1]


def _exchange_wait(ex, handles, after, name):
    n_in, n_out, n_sem = len(ex["ins"]), len(ex["outs"]), len(ex["sems"])
    sems, thru = handles[:n_sem], handles[n_sem:]

    def body(*refs):
        in_refs, land_refs = refs[:n_in], refs[n_in:n_in + n_out]
        sem_refs = refs[n_in + n_out:n_in + n_out + n_sem]
        ex["finish"](in_refs, land_refs, *sem_refs)

    hbm = pl.BlockSpec(memory_space=pltpu.HBM)
    sem = pl.BlockSpec(memory_space=pltpu.SEMAPHORE)
    res = pl.pallas_call(
        body, name=name,
        out_shape=[pltpu.HBM(a.shape, a.dtype) for a in thru],
        in_specs=[hbm] * (n_in + n_out) + [sem] * n_sem + [pl.BlockSpec(memory_space=pl.ANY)],
        out_specs=[hbm] * (n_in + n_out),
        input_output_aliases={k: k for k in range(n_in + n_out)},
        compiler_params=pltpu.CompilerParams(has_side_effects=pltpu.SideEffectType.DATAFLOW_SIDE_EFFECTING),
    )(*thru, *sems, after)
    return res[:n_in], res[n_in:]


def _start_finish(build):
    def start(*refs):
        for cp in build(*refs):
            cp.start()

    def finish(*refs):
        for cp in build(*refs):
            cp.wait()

    return dict(start=start, finish=finish)


def _pair_exchange(grads):
    n = len(grads)

    def build(srcs, outs, send_sems, recv_sems):
        x, y, c = lax.axis_index("x"), lax.axis_index("y"), lax.axis_index("c")
        copies = []
        for t in range(n):
            rows = grads[t].shape[1] // 2
            copies.append(pltpu.make_async_remote_copy(
                src_ref=srcs[t].at[:, pl.ds((1 - c) * rows, rows)], dst_ref=outs[t],
                send_sem=send_sems.at[t], recv_sem=recv_sems.at[t], device_id=(x, y, 1 - c), device_id_type=MESH))
        return copies

    return dict(ins=list(grads), **_start_finish(build),
                outs=[jax.ShapeDtypeStruct((g.shape[0], g.shape[1] // 2, g.shape[2]), g.dtype) for g in grads],
                sems=[pltpu.SemaphoreType.DMA((n,)), pltpu.SemaphoreType.DMA((n,))])


def _pair_add(core, own, got):
    nchip, rows2, width = own.shape
    rows = rows2 // 2
    tr = min(512, rows)
    nb = rows // tr

    def body(core_ref, own_ref, got_ref, out_ref):
        out_ref[...] = (own_ref[...] + got_ref[...]).astype(BF16)

    grid_spec = pltpu.PrefetchScalarGridSpec(
        num_scalar_prefetch=1, grid=(nchip, nb),
        in_specs=[pl.BlockSpec((None, tr, width), lambda k, i, core_ref: (k, core_ref[0] * nb + i, 0)),
                  pl.BlockSpec((None, tr, width), lambda k, i, core_ref: (k, i, 0))],
        out_specs=pl.BlockSpec((None, tr, width), lambda k, i, core_ref: (k, i, 0)))
    return pl.pallas_call(
        body, name=f"pair_add_{width}", grid_spec=grid_spec,
        out_shape=jax.ShapeDtypeStruct((nchip, rows, width), BF16),
    )(core, own, got)


def _chip_exchange(parts):
    n = len(parts)

    def build(srcs, outs, send_sems, recv_sems, local_sems):
        x, y, c = lax.axis_index("x"), lax.axis_index("y"), lax.axis_index("c")
        my_chip = 2 * x + y
        chips = [(1 - x, y), (x, 1 - y), (1 - x, 1 - y)]
        copies = [pltpu.make_async_copy(srcs[t].at[my_chip], outs[t].at[my_chip], local_sems.at[t]) for t in range(n)]
        for j, (cx, cy) in enumerate(chips):
            for t in range(n):
                k = n * j + t
                copies.append(pltpu.make_async_remote_copy(
                    src_ref=srcs[t].at[2 * cx + cy], dst_ref=outs[t].at[my_chip], send_sem=send_sems.at[k],
                    recv_sem=recv_sems.at[k], device_id=(cx, cy, c), device_id_type=MESH))
        return copies

    return dict(ins=list(parts), **_start_finish(build), outs=[jax.ShapeDtypeStruct(p.shape, p.dtype) for p in parts],
                sems=[pltpu.SemaphoreType.DMA((3 * n,)), pltpu.SemaphoreType.DMA((3 * n,)),
                      pltpu.SemaphoreType.DMA((n,))])


def _slot_sum(slots, name, core=None):
    ns, rows, width = slots.shape
    tr = min(512, rows)

    def body(*refs):
        in_ref, out_ref = refs[-2:]
        acc = in_ref[0].astype(F32)
        for s in range(1, ns):
            acc = acc + in_ref[s].astype(F32)
        out_ref[...] = acc

    if core is None:
        return pl.pallas_call(
            body, name=name, grid=(rows // tr,),
            in_specs=[pl.BlockSpec((ns, tr, width), lambda i: (0, i, 0))],
            out_specs=pl.BlockSpec((tr, width), lambda i: (i, 0)),
            out_shape=jax.ShapeDtypeStruct((rows, width), F32),
        )(slots)
    grid_spec = pltpu.PrefetchScalarGridSpec(
        num_scalar_prefetch=1, grid=(rows // tr,),
        in_specs=[pl.BlockSpec((ns, tr, width), lambda i, core_ref: (0, i, 0))],
        out_specs=pl.BlockSpec((None, tr, width), lambda i, core_ref: (core_ref[0], i, 0)))
    return pl.pallas_call(
        body, name=name, grid_spec=grid_spec, out_shape=jax.ShapeDtypeStruct((2, rows, width), F32),
    )(core, slots)


def _pair_gather(bufs, small):
    n = len(bufs)

    def body(*refs):
        small_ref, outs, small_out = refs[n], refs[n + 1:2 * n + 1], refs[2 * n + 1]
        send_sems, recv_sems, local_sem = refs[2 * n + 2:]
        x, y, c = lax.axis_index("x"), lax.axis_index("y"), lax.axis_index("c")
        me = 4 * x + 2 * y + c
        chips = [(1 - x, y), (x, 1 - y), (1 - x, 1 - y)]
        mine = pltpu.make_async_copy(small_ref, small_out.at[me], local_sem)
        mine.start()
        copies = [pltpu.make_async_remote_copy(
            src_ref=outs[t].at[c], dst_ref=outs[t].at[c], send_sem=send_sems.at[t], recv_sem=recv_sems.at[t],
            device_id=(x, y, 1 - c), device_id_type=MESH) for t in range(n)]
        peers = [(x, y, 1 - c)] + [(cx, cy, cc) for (cx, cy) in chips for cc in (c, 1 - c)]
        for j, peer in enumerate(peers):
            copies.append(pltpu.make_async_remote_copy(
                src_ref=small_ref, dst_ref=small_out.at[me], send_sem=send_sems.at[n + j],
                recv_sem=recv_sems.at[n + j], device_id=peer, device_id_type=MESH))
        for cp in copies:
            cp.start()
        for cp in copies:
            cp.wait()
        mine.wait()

    any_spec = pl.BlockSpec(memory_space=pl.ANY)
    res = pl.pallas_call(
        body, name="pair_gather",
        out_shape=[jax.ShapeDtypeStruct(b.shape, b.dtype) for b in bufs]
        + [jax.ShapeDtypeStruct((8,) + small.shape, small.dtype)],
        in_specs=[any_spec] * (n + 1), out_specs=[any_spec] * (n + 1),
        input_output_aliases={t: t for t in range(n)},
        scratch_shapes=[pltpu.SemaphoreType.DMA((n + 7,)), pltpu.SemaphoreType.DMA((n + 7,)),
                        pltpu.SemaphoreType.DMA],
    )(*bufs, small)
    return [r.reshape(2 * b.shape[1], b.shape[2]) for r, b in zip(res[:n], bufs)], res[n]


def _adamw(w, g, m, v, name):
    rows, width = w.shape
    tr = min(256, rows)
    c1 = 1.0 / (1.0 - ADAM_B1 ** ADAM_STEP)
    c2 = 1.0 / (1.0 - ADAM_B2 ** ADAM_STEP)

    def body(w_ref, g_ref, m_ref, v_ref, d_ref, nm_ref, nv_ref):
        gv = g_ref[...]
        nm = ADAM_B1 * m_ref[...] + (1.0 - ADAM_B1) * gv
        nv = ADAM_B2 * v_ref[...] + (1.0 - ADAM_B2) * (gv * gv)
        nm_ref[...] = nm
        nv_ref[...] = nv
        d_ref[...] = -ADAM_LR * ((nm * c1) / (jnp.sqrt(nv * c2) + ADAM_EPS) + ADAM_WD * w_ref[...])

    spec = pl.BlockSpec((tr, width), lambda i: (i, 0))
    return pl.pallas_call(
        body, name=name, grid=(rows // tr,), in_specs=[spec] * 4, out_specs=[spec] * 3,
        out_shape=[jax.ShapeDtypeStruct(w.shape, F32)] * 3,
    )(w, g, m, v)


def _local_step(x, mem, target, pre_norm, sink_a, mem_norm, post_norm, w_in_g, w_out, w_mkv, gathers=None):
    first_gather, late_gather = gathers if gathers else (None, None)
    u, ut, hosted = _pre_norm(x, pre_norm, first_gather)
    if gathers:
        w_in_g = hosted[0].reshape(N_CHIPS, D_MODEL, SHARD_IN)
    pr = _pre_proj(u, w_in_g, late_gather)
    pr["ut"] = ut
    if gathers:
        w_out, w_mkv = (g.reshape(D_MODEL, g.shape[-1]) for g in pr["hosted"])
    mk, mv = _mem_kv(mem, mem_norm, w_mkv)
    sink = sink_a.reshape(-1)
    qa, ka, va = pr["qa"][None], pr["ka"][None], pr["va"][None]
    oa, lse_a = _band_fwd(qa, ka, va, sink, max_dist=A_WINDOW - 1, name="swa_fwd")
    ob_list, lseb_list = [], []
    for k, (win, dil) in enumerate(B_CONFIGS):
        o_i, l_i = _band_fwd(pr["qb"][k], pr["kb"][k], pr["vb"][k], None, max_dist=win // dil, name=f"dil{dil}_fwd")
        ob_list.append(o_i)
        lseb_list.append(l_i)
    oc, lse_c = _mem_attn_fwd(pr["qc"], mk, mv)
    sink_row = jnp.pad(sink, (0, LANES - sink.shape[0])).reshape(1, LANES)
    po = _post(x, target, post_norm, w_out, sink_row, oa[0], lse_a[0], pr["ga"], ob_list, lseb_list, pr["gb"], oc,
               pr["gc"])
    dqc, dmk, dmv = _mem_attn_bwd(pr["qc"], mk, mv, po["doc"], lse_c, po["dl_c"])
    dqa, dka, dva = _band_bwd(qa, ka, va, po["doa"][None], lse_a, po["dl_a"][None], max_dist=A_WINDOW - 1,
                              name="swa_bwd")
    res = dict(qb=[], kb=[], vb=[])
    for k, (win, dil) in enumerate(B_CONFIGS):
        dq_i, dk_i, dv_i = _band_bwd(pr["qb"][k], pr["kb"][k], pr["vb"][k], po["dob"][k], po["lse_b"][k],
                                     po["dl_b"][k], max_dist=win // dil, name=f"dil{dil}_bwd")
        res["qb"].append(dq_i)
        res["kb"].append(dk_i)
        res["vb"].append(dv_i)
    nat = dict(qa=dqa[0], ka=dka[0], va=dva[0], ga=po["dga"], gb=po["dgb"], qc=dqc, gc=po["dgc"])
    dproj, gw_in = _grad_w_in(pr["ut"], nat, res)
    gw_mkv, gmem = _mem_kv_bwd(mem, mem_norm, w_mkv, dmk, dmv)
    gsink = -po["gsink"][0, :sink.shape[0]]
    return dict(loss=po["loss"], g=po["g"], dproj=dproj, gw_in=gw_in, gw_out=po["gw_out"], gw_mkv=gw_mkv,
                gpost=po["gpost"], gmem=gmem, gsink=gsink, w_in_g=w_in_g)


def kernel(x, mem, pre_norm, w_in, sink_a, mem_norm, w_mem_kv, w_out, post_norm, loss_target, m_pre_norm, m_w_in, m_sink_a, m_mem_norm, m_w_mem_kv, m_w_out, m_post_norm, v_pre_norm, v_w_in, v_sink_a, v_mem_norm, v_w_mem_kv, v_w_out, v_post_norm):
    gathers = (_gather_exchange([w_in[0].astype(BF16)]),
               _gather_exchange([w_out[0].astype(BF16), w_mem_kv[0].astype(BF16)]))
    loc = _local_step(x[0], mem[0], loss_target[0], pre_norm, sink_a, mem_norm, post_norm, None, None, None, gathers)
    big = [loc["gw_in"], loc["gw_out"].reshape(N_CHIPS, D_MODEL // N_CHIPS, D_MODEL),
           loc["gw_mkv"].reshape(N_CHIPS, D_MODEL // N_CHIPS, 2 * C_W)]
    core = lax.axis_index("c").astype(jnp.int32).reshape(1)
    w_in_full = loc["w_in_g"]
    step_in = (x[0], loc["g"], pre_norm, w_in_full, loc["dproj"])
    pair_ex = _pair_exchange(big)
    pair_handles, token = _exchange_start(pair_ex, "pair_exchange_start")
    gx_a, gpre_a = _input_grad(*step_in, None, (0, 3), token, "input_grad_a")
    big, got = _exchange_wait(pair_ex, pair_handles, gpre_a, "pair_exchange_wait")
    parts = [_pair_add(core, own, g) for own, g in zip(big, got)]
    chip_ex = _chip_exchange(parts)
    chip_handles, token = _exchange_start(chip_ex, "chip_exchange_start")
    grad_x, gpre_b = _input_grad(*step_in, gx_a, (3, 13), token, "input_grad_b")
    _, slots = _exchange_wait(chip_ex, chip_handles, gpre_b, "chip_exchange_wait")
    halves = [_slot_sum(s, name=f"chip_sum_{s.shape[2]}", core=core) for s in slots]
    widen = lambda a: jnp.pad(a.reshape(1, -1), ((0, 0), (0, D_MODEL - a.size)))
    small = jnp.concatenate([gpre_a, loc["gpost"], loc["gmem"], widen(loc["gsink"]), widen(loc["loss"]), gpre_b,
                             jnp.zeros((2, D_MODEL), F32)], axis=0)
    (g_in, g_out, g_mkv), small_slots = _pair_gather(halves, small)
    small_sum = _slot_sum(small_slots, name="device_sum")
    g_pre, g_post, g_mem = small_sum[0:1] + small_sum[5:6], small_sum[1:2], small_sum[2:3]
    g_sink = small_sum[3:4, :sink_a.shape[1]]
    loss = small_sum[4, 0]

    d_in, nm_in, nv_in = _adamw(w_in[0], g_in, m_w_in[0], v_w_in[0], "adamw_in")
    d_out, nm_out, nv_out = _adamw(w_out[0], g_out, m_w_out[0], v_w_out[0], "adamw_out")
    d_mkv, nm_mkv, nv_mkv = _adamw(w_mem_kv[0], g_mkv, m_w_mem_kv[0], v_w_mem_kv[0], "adamw_mkv")
    pad6 = lambda a: jnp.pad(a, ((0, 0), (0, D_MODEL - a.shape[1])))
    stack = lambda a, b, c_, d_: jnp.concatenate([a, b, c_, pad6(d_), jnp.zeros((4, D_MODEL), F32)], axis=0)
    d_s, nm_s, nv_s = _adamw(stack(pre_norm, post_norm, mem_norm, sink_a),
                             jnp.concatenate([g_pre, small_sum[1:]], axis=0),
                             stack(m_pre_norm, m_post_norm, m_mem_norm, m_sink_a),
                             stack(v_pre_norm, v_post_norm, v_mem_norm, v_sink_a), "adamw_small")
    ns_ = sink_a.shape[1]
    unpack = lambda a: (a[0:1], a[3:4, :ns_], a[2:3], a[1:2])
    d_pre, d_sink, d_mem, d_post = unpack(d_s)
    nm_pre, nm_sink, nm_mem, nm_post = unpack(nm_s)
    nv_pre, nv_sink, nv_mem, nv_post = unpack(nv_s)
    lead = lambda a: a[None]
    return (loss, lead(grad_x),
            g_pre, lead(g_in), g_sink, g_mem, lead(g_mkv), lead(g_out), g_post,
            d_pre, lead(d_in), d_sink, d_mem, lead(d_mkv), lead(d_out), d_post,
            nm_pre, lead(nm_in), nm_sink, nm_mem, lead(nm_mkv), lead(nm_out), nm_post,
            nv_pre, lead(nv_in), nv_sink, nv_mem, lead(nv_mkv), lead(nv_out), nv_post)
```

```python
import numpy as np
import jax
import jax.numpy as jnp
from jax import lax
from jax.experimental import pallas as pl
from jax.experimental.pallas import tpu as pltpu

F32 = jnp.float32
BF16 = jnp.bfloat16

D_MODEL = 1024
HEAD_DIM = 64
LANES = 128
BLOCK = 128
A_W, A_KV_W, B_W, C_W = 384, 128, 384, 256
N_MEM = 256
D_IN = 3072
N_CHIPS = 4
SHARD_IN = D_IN // N_CHIPS
B_CONFIGS = ((128, 1), (512, 4), (2048, 16))
B_DILS = tuple(d for _, d in B_CONFIGS)
A_WINDOW = 128
RMS_EPS = 1e-6
ROPE_THETA = 500000.0
SCALE = HEAD_DIM ** -0.5
NEG = -1e30
ADAM_LR, ADAM_B1, ADAM_B2, ADAM_EPS, ADAM_WD, ADAM_STEP = 0.001, 0.9, 0.999, 1e-08, 0.01, 10

NT = (((1,), (1,)), ((), ()))
TN = (((0,), (0,)), ((), ()))
MESH = pl.DeviceIdType.MESH

_PROJ_LAYOUT = (
    [("qa", 128 * i, True, True) for i in range(3)] + [("ka", 0, True, False), ("va", 0, False, False)]
    + [("ga", 128 * i, False, False) for i in range(3)]
    + [("qb", 128 * i, True, True) for i in range(3)] + [("kb", 128 * i, True, False) for i in range(3)]
    + [("vb", 128 * i, False, False) for i in range(3)] + [("gb", 128 * i, False, False) for i in range(3)]
    + [("qc", 128 * i, False, True) for i in range(2)] + [("gc", 128 * i, False, False) for i in range(2)]
)
_PROJ_WIDTH = dict(qa=A_W, ka=A_KV_W, va=A_KV_W, ga=A_W, qb=B_W, kb=B_W, vb=B_W, gb=B_W, qc=C_W, gc=C_W)
_NATURAL = ("qa", "ka", "va", "ga", "gb", "qc", "gc")
_DILATED = ("qb", "kb", "vb")


def _dot(a, b):
    return jnp.dot(a, b, preferred_element_type=F32)


def _dot_nt(a, b):
    return lax.dot_general(a, b, NT, preferred_element_type=F32)


def _dot_tn(a, b):
    return lax.dot_general(a, b, TN, preferred_element_type=F32)


def _half_masks(rows):
    lane = lax.broadcasted_iota(jnp.int32, (rows, LANES), 1)
    return lane < HEAD_DIM, lane >= HEAD_DIM


def _rope(t, c, sm, sp):
    return t * c + pltpu.roll(t, LANES - 8, 1) * sm + pltpu.roll(t, 8, 1) * sp


def _rope_tables(seq, tm):
    dim = jnp.arange(LANES) % HEAD_DIM
    inv_freq = ROPE_THETA ** (-jnp.arange(0, 16, 2, dtype=F32) / 16)
    freq = jnp.where(dim < 16, inv_freq[dim % 8], 0.0)[None, :]
    local = jnp.arange(tm, dtype=F32)[:, None] * freq
    base = (jnp.arange(seq // tm, dtype=F32) * tm)[:, None] * freq
    both = lambda a: jnp.concatenate([jnp.cos(a), jnp.sin(a)], axis=1)
    return both(local), jnp.repeat(both(base), 8, axis=0)


def _rope_coeffs(local_ref, base_ref):
    cl, sl = local_ref[:, :LANES], local_ref[:, LANES:]
    cb, sb = base_ref[0:1, :LANES], base_ref[0:1, LANES:]
    cos = cb * cl - sb * sl
    sin = sb * cl + cb * sl
    dim = lax.broadcasted_iota(jnp.int32, (1, LANES), 1) % HEAD_DIM
    return cos, jnp.where(dim < 8, -sin, 0.0), jnp.where((dim >= 8) & (dim < 16), sin, 0.0)


def _split3(x):
    a = x.astype(BF16)
    r = x - a.astype(F32)
    b = r.astype(BF16)
    c = (r - b.astype(F32)).astype(BF16)
    return a, b, c


def _rows_to_lanes(x):
    row = lax.broadcasted_iota(jnp.int32, (8, LANES), 0)
    lane = lax.broadcasted_iota(jnp.int32, (8, LANES), 1)
    eye = (row == lane).astype(BF16)
    a, b, c = _split3(x)
    return _dot_nt(eye, a) + _dot_nt(eye, b) + _dot_nt(eye, c)


def _head_sum_matrix(width):
    k = lax.broadcasted_iota(jnp.int32, (width, LANES), 0)
    h = lax.broadcasted_iota(jnp.int32, (width, LANES), 1)
    return (k // HEAD_DIM == h).astype(BF16)


def _head_expand_matrix(width):
    h = lax.broadcasted_iota(jnp.int32, (LANES, width), 0)
    k = lax.broadcasted_iota(jnp.int32, (LANES, width), 1)
    return (k // HEAD_DIM == h).astype(BF16)


def _dot_split(x, mat, terms):
    parts = _split3(x)[:terms]
    out = _dot(parts[0], mat)
    for p in parts[1:]:
        out = out + _dot(p, mat)
    return out


def _per_head(cols, fill=0.0):
    rows = cols[0].shape[0]
    lane = lax.broadcasted_iota(jnp.int32, (rows, LANES), 1)
    out = jnp.full((rows, LANES), fill, F32)
    for h, col in enumerate(cols):
        out = jnp.where(lane == h, col, out)
    return out


def _lane_blocks(width):
    return [slice(p * LANES, (p + 1) * LANES) for p in range(width // LANES)]


def _stage(rows, width):
    return pltpu.VMEM((width // LANES, rows, LANES), F32)


def _stage_write(buf, value):
    for p, lanes in enumerate(_lane_blocks(value.shape[1])):
        buf[p] = value[:, lanes]


def _stage_read(buf):
    return jnp.concatenate([buf[p] for p in range(buf.shape[0])], axis=1) if buf.shape[0] > 1 else buf[0]


def _to_residues(buf, out_ref, dil):
    rows = buf.shape[1] // dil
    for r in range(dil):
        for p in range(buf.shape[0]):
            plane = buf.at[p]
            out_ref[r, :, p * LANES:(p + 1) * LANES] = plane[pl.ds(r, rows, stride=dil), :].astype(out_ref.dtype)


def _from_residues(in_ref, buf, dil):
    rows = buf.shape[1] // dil
    for r in range(dil):
        for p in range(buf.shape[0]):
            plane = buf.at[p]
            plane[pl.ds(r, rows, stride=dil), :] = in_ref[r, :, p * LANES:(p + 1) * LANES].astype(F32)


def _residue_spec(dil, tm, width):
    return pl.BlockSpec((dil, tm // dil, width), lambda i: (0, i, 0))


def _gather_exchange(shards_2d):
    shards = tuple(s.reshape(2, s.shape[0] // 2, s.shape[1]) for s in shards_2d)
    n = len(shards)

    def copies(in_refs, out_refs, send_sems, recv_sems):
        srcs, outs = in_refs[:n], out_refs
        x, y, c = lax.axis_index("x"), lax.axis_index("y"), lax.axis_index("c")
        my_chip = 2 * x + y
        sibling = (x, y, 1 - c)
        chips = [(1 - x, y), (x, 1 - y), (1 - x, 1 - y)]

        def copy(k, src, dst, to):
            return pltpu.make_async_remote_copy(src_ref=src, dst_ref=dst, send_sem=send_sems.at[k],
                                                recv_sem=recv_sems.at[k], device_id=to, device_id_type=MESH)

        first, arrive, passed, sibling_arrive = [], [], [], []
        for j, (cx, cy) in enumerate(chips):
            chip = 2 * cx + cy
            for t in range(n):
                k = n * j + t
                first.append(copy(k, srcs[t].at[c], outs[t].at[my_chip, c], (cx, cy, c)))
                arrive.append(copy(k, srcs[t].at[c], outs[t].at[chip, c], (cx, cy, c)))
                passed.append(copy(n * 3 + k, outs[t].at[chip, c], outs[t].at[chip, c], sibling))
                sibling_arrive.append(copy(n * 3 + k, outs[t].at[chip, 1 - c], outs[t].at[chip, 1 - c], sibling))
        return first, arrive, passed, sibling_arrive

    def start(*refs):
        for cp in copies(*refs)[0]:
            cp.start()

    def finish(*refs):
        first, arrive, passed, sibling_arrive = copies(*refs)
        for got, fwd in zip(arrive, passed):
            got.wait_recv()
            fwd.start()
        for cp in sibling_arrive:
            cp.wait_recv()
        for cp in first + passed:
            cp.wait_send()

    my_chip = 2 * lax.axis_index("x") + lax.axis_index("y")
    landing = [lax.dynamic_update_slice(jnp.zeros((N_CHIPS,) + s.shape, s.dtype), s[None], (my_chip, 0, 0, 0))
               for s in shards]
    return dict(ins=list(shards) + landing, start=start, finish=finish, aliases={n + t: t for t in range(n)},
                outs=[jax.ShapeDtypeStruct((N_CHIPS,) + s.shape, s.dtype) for s in shards],
                sems=[pltpu.SemaphoreType.DMA((6 * n,)), pltpu.SemaphoreType.DMA((6 * n,))])


def _run_exchange(ex, name):
    n_in, n_out = len(ex["ins"]), len(ex["outs"])

    def body(*refs):
        in_refs, out_refs, sems = refs[:n_in], refs[n_in:n_in + n_out], refs[n_in + n_out:]
        ex["start"](in_refs, out_refs, *sems)
        ex["finish"](in_refs, out_refs, *sems)

    any_spec = pl.BlockSpec(memory_space=pl.ANY)
    return pl.pallas_call(
        body, name=name, out_shape=ex["outs"], in_specs=[any_spec] * n_in, out_specs=[any_spec] * n_out,
        input_output_aliases=ex.get("aliases", {}), scratch_shapes=ex["sems"],
    )(*ex["ins"])


def _mem_kv(mem, mem_norm, w_mkv):
    def body(mem_ref, g_ref, w_ref, mk_ref, mv_ref):
        m = mem_ref[...]
        r = lax.rsqrt(jnp.mean(m * m, axis=-1, keepdims=True) + RMS_EPS)
        mn = (m * r * g_ref[...]).astype(BF16)
        kv = _dot(mn, w_ref[...])
        mk_ref[...] = kv[:, :C_W].astype(BF16)
        mv_ref[...] = kv[:, C_W:].astype(BF16)

    return pl.pallas_call(
        body, name="mem_kv",
        out_shape=[jax.ShapeDtypeStruct((N_MEM, C_W), BF16)] * 2,
    )(mem, mem_norm, w_mkv)


def _mem_kv_bwd(mem, mem_norm, w_mkv, dmk, dmv):
    def body(mem_ref, g_ref, w_ref, dmk_ref, dmv_ref, gw_ref, gn_ref):
        m = mem_ref[...]
        r = lax.rsqrt(jnp.mean(m * m, axis=-1, keepdims=True) + RMS_EPS)
        mhat = m * r
        mn = (mhat * g_ref[...]).astype(BF16)
        dkv = jnp.concatenate([dmk_ref[...], dmv_ref[...]], axis=1).astype(BF16)
        gw_ref[...] = _dot_tn(mn, dkv)
        dmn = _dot_nt(dkv, w_ref[...])
        gn_ref[...] = jnp.sum(dmn * mhat, axis=0, keepdims=True)

    return pl.pallas_call(
        body, name="mem_kv_bwd",
        out_shape=[jax.ShapeDtypeStruct((D_MODEL, 2 * C_W), F32), jax.ShapeDtypeStruct((1, D_MODEL), F32)],
    )(mem, mem_norm, w_mkv, dmk, dmv)


def _pre_norm(x, pre_norm, host=None):
    seq = x.shape[0]
    tm = min(512, seq)
    n_host_in = len(host["ins"]) if host else 0
    n_host_out = len(host["outs"]) if host else 0

    def body(x_ref, g_ref, *refs):
        host_in, (u_ref, ut_ref), refs = refs[:n_host_in], refs[n_host_in:n_host_in + 2], refs[n_host_in + 2:]
        host_out, sems = refs[:n_host_out], refs[n_host_out:]
        if host:
            @pl.when(pl.program_id(0) == 0)
            def _():
                host["start"](host_in, host_out, *sems)

        xv = x_ref[...]
        r = lax.rsqrt(jnp.mean(xv * xv, axis=-1, keepdims=True) + RMS_EPS)
        u = xv * r * g_ref[...]
        u_ref[...] = u.astype(BF16)
        ut_ref[...] = u.T.astype(BF16)
        if host:
            @pl.when(pl.program_id(0) == seq // tm - 1)
            def _():
                host["finish"](host_in, host_out, *sems)

    any_spec = pl.BlockSpec(memory_space=pl.ANY)
    ins = [x, pre_norm]
    in_specs = [pl.BlockSpec((tm, D_MODEL), lambda i: (i, 0)), pl.BlockSpec(pre_norm.shape, lambda i: (0, 0))]
    out_shape = [jax.ShapeDtypeStruct((seq, D_MODEL), BF16), jax.ShapeDtypeStruct((D_MODEL, seq), BF16)]
    out_specs = [pl.BlockSpec((tm, D_MODEL), lambda i: (i, 0)), pl.BlockSpec((D_MODEL, tm), lambda i: (0, i))]
    aliases, scratch = {}, []
    if host:
        aliases = {len(ins) + k: 2 + v for k, v in host.get("aliases", {}).items()}
        ins += list(host["ins"])
        in_specs += [any_spec] * n_host_in
        out_shape += list(host["outs"])
        out_specs += [any_spec] * n_host_out
        scratch = list(host["sems"])
    res = pl.pallas_call(
        body, name="pre_norm", grid=(seq // tm,), in_specs=in_specs, out_specs=out_specs, out_shape=out_shape,
        input_output_aliases=aliases, scratch_shapes=scratch,
        compiler_params=pltpu.CompilerParams(dimension_semantics=("arbitrary",)),
    )(*ins)
    return res[0], res[1], res[2:]


def _pre_proj(u, w_in_g, host=None):
    seq = u.shape[0]
    tm = min(512, seq)
    n_nat, n_dil = len(_NATURAL), len(_DILATED) * len(B_DILS)
    rope = _rope_tables(seq, tm)

    n_host_in = len(host["ins"]) if host else 0
    n_host_out = len(host["outs"]) if host else 0
    n_own_out = n_nat + n_dil

    def body(u_ref, w_ref, rl_ref, rb_ref, *refs):
        host_in, refs = refs[:n_host_in], refs[n_host_in:]
        nat = dict(zip(_NATURAL, refs[:n_nat]))
        res = {n: refs[n_nat + len(B_DILS) * k:n_nat + len(B_DILS) * (k + 1)] for k, n in enumerate(_DILATED)}
        host_out = refs[n_own_out:n_own_out + n_host_out]
        bufs = dict(zip(_DILATED, refs[n_own_out + n_host_out:]))
        sems = refs[n_own_out + n_host_out + len(_DILATED):]
        if host:
            @pl.when(pl.program_id(0) == 0)
            def _():
                host["start"](host_in, host_out, *sems)

        ub = u_ref[...]
        c, sm, sp = _rope_coeffs(rl_ref, rb_ref)
        for j in range(N_CHIPS):
            pj = _dot(ub, w_ref[j])
            for b in range(SHARD_IN // LANES):
                name, off, roped, scaled = _PROJ_LAYOUT[(SHARD_IN // LANES) * j + b]
                piece = pj[:, LANES * b:LANES * (b + 1)]
                if roped:
                    piece = _rope(piece, c, sm, sp)
                if scaled:
                    piece = piece * SCALE
                if name in bufs:
                    bufs[name][off // LANES] = piece
                else:
                    nat[name][:, off:off + LANES] = piece.astype(BF16)
        for name in _DILATED:
            for ref, dil in zip(res[name], B_DILS):
                _to_residues(bufs[name], ref, dil)
        if host:
            @pl.when(pl.program_id(0) == seq // tm - 1)
            def _():
                host["finish"](host_in, host_out, *sems)

    row = lambda w: pl.BlockSpec((tm, w), lambda i: (i, 0))
    full = lambda a: pl.BlockSpec(a.shape, lambda i: (0,) * a.ndim)
    any_spec = pl.BlockSpec(memory_space=pl.ANY)
    out_shape = [jax.ShapeDtypeStruct((seq, _PROJ_WIDTH[n]), BF16) for n in _NATURAL]
    out_specs = [row(_PROJ_WIDTH[n]) for n in _NATURAL]
    for n in _DILATED:
        for dil in B_DILS:
            out_shape.append(jax.ShapeDtypeStruct((dil, seq // dil, B_W), BF16))
            out_specs.append(_residue_spec(dil, tm, B_W))
    ins = [u, w_in_g, *rope]
    in_specs = [row(D_MODEL), full(w_in_g), full(rope[0]), pl.BlockSpec((8, 2 * LANES), lambda i: (i, 0))]
    scratch = [_stage(tm, B_W)] * len(_DILATED)
    aliases = {}
    if host:
        aliases = {len(ins) + k: n_own_out + v for k, v in host.get("aliases", {}).items()}
        ins += list(host["ins"])
        in_specs += [any_spec] * n_host_in
        out_shape += list(host["outs"])
        out_specs += [any_spec] * n_host_out
        scratch += list(host["sems"])
    res = pl.pallas_call(
        body, name="pre_proj", grid=(seq // tm,), in_specs=in_specs, out_specs=out_specs, out_shape=out_shape,
        input_output_aliases=aliases, scratch_shapes=scratch,
        compiler_params=pltpu.CompilerParams(dimension_semantics=("arbitrary",)),
    )(*ins)
    out = dict(zip(_NATURAL, res[:n_nat]))
    for k, n in enumerate(_DILATED):
        out[n] = res[n_nat + len(B_DILS) * k:n_nat + len(B_DILS) * (k + 1)]
    out["hosted"] = res[n_own_out:]
    return out


def _band_bias(max_dist, transposed):
    i = np.arange(BLOCK)[:, None]
    j = np.arange(BLOCK)[None, :]
    if transposed:
        same = i <= j
        other = (j + BLOCK - i) <= max_dist
        vis = np.concatenate([same, other], axis=1)
    else:
        prev = (i + BLOCK - j) <= max_dist
        same = j <= i
        vis = np.concatenate([prev, same], axis=1)
    return jnp.asarray(np.where(vis, 0.0, NEG).astype(np.float32))


def _kv_place(h, gqa):
    return (0, h // 3) if gqa else (h // 2, h % 2)


def _band_fwd(q, k, v, sink, *, max_dist, name):
    dil, length, wq = q.shape
    wk = k.shape[2]
    gqa = wk != wq
    tq = min(1024, length)
    ns, nt = tq // BLOCK, length // tq
    npair = wq // LANES
    bias = _band_bias(max_dist, transposed=False)
    has_sink = sink is not None

    def body(*refs):
        if has_sink:
            sink_ref, refs = refs[0], refs[1:]
        q_ref, k_ref, kp_ref, v_ref, vp_ref, bias_ref, o_ref, lse_ref, kbuf, vbuf = refs[:10]
        i = pl.program_id(1)
        kbuf[0:BLOCK] = kp_ref[...]
        kbuf[BLOCK:] = k_ref[...]
        vbuf[0:BLOCK] = vp_ref[...]
        vbuf[BLOCK:] = v_ref[...]
        if gqa:
            kroll, vroll = refs[10:12]
            kroll[...] = pltpu.roll(kbuf[...], HEAD_DIM, 1)
            vroll[...] = pltpu.roll(vbuf[...], HEAD_DIM, 1)
        half = _half_masks(BLOCK)
        col_prev = (lax.broadcasted_iota(jnp.int32, (1, 2 * BLOCK), 1) < BLOCK).astype(F32)

        def sub(a, carry):
            r0 = pl.multiple_of(a * BLOCK, BLOCK)
            pen = jnp.where((i == 0) & (a == 0), NEG, 0.0)
            b = bias_ref[...] + pen * col_prev
            scores = []
            for p in range(npair):
                qp = q_ref[pl.ds(r0, BLOCK), p * LANES:(p + 1) * LANES]
                for e in range(2):
                    pk, ek = _kv_place(2 * p + e, gqa)
                    kw = (kbuf if ek == e else kroll)[pl.ds(r0, 2 * BLOCK), pk * LANES:(pk + 1) * LANES]
                    scores.append(_dot_nt(jnp.where(half[e], qp, jnp.zeros_like(qp)), kw))
            m_cols, l_cols, probs = [], [], []
            for h, s in enumerate(scores):
                s = s + b
                m = jnp.max(s, axis=1, keepdims=True)
                if has_sink:
                    m = jnp.maximum(m, sink_ref[h])
                pe = jnp.exp(s - m)
                l = jnp.sum(pe, axis=1, keepdims=True)
                if has_sink:
                    l = l + jnp.exp(sink_ref[h] - m)
                probs.append(pe.astype(BF16))
                m_cols.append(m)
                l_cols.append(l)
            for p in range(npair):
                o_h = []
                for e in range(2):
                    h = 2 * p + e
                    pk, ek = _kv_place(h, gqa)
                    vw = (vbuf if ek == e else vroll)[pl.ds(r0, 2 * BLOCK), pk * LANES:(pk + 1) * LANES]
                    o_h.append(_dot(probs[h], vw) * (1.0 / l_cols[h]))
                o_ref[pl.ds(r0, BLOCK), p * LANES:(p + 1) * LANES] = jnp.where(half[0], o_h[0], o_h[1]).astype(BF16)
            lse_ref[pl.ds(r0, BLOCK), :] = _per_head(m_cols) + jnp.log(_per_head(l_cols, 1.0))
            return carry

        lax.fori_loop(0, ns, sub, 0, unroll=True)

    main = lambda w: pl.BlockSpec((None, tq, w), lambda r, i: (r, i, 0))
    prev = lambda w: pl.BlockSpec((None, BLOCK, w), lambda r, i: (r, jnp.maximum(i * ns - 1, 0), 0))
    in_specs = [main(wq), main(wk), prev(wk), main(wk), prev(wk), pl.BlockSpec(bias.shape, lambda r, i: (0, 0))]
    args = [q, k, k, v, v, bias]
    if has_sink:
        in_specs = [pl.BlockSpec(memory_space=pltpu.SMEM)] + in_specs
        args = [sink] + args
    scratch = [pltpu.VMEM((tq + BLOCK, wk), BF16)] * (4 if gqa else 2)
    return pl.pallas_call(
        body, name=name, grid=(dil, nt), in_specs=in_specs,
        out_specs=[main(wq), main(LANES)],
        out_shape=[jax.ShapeDtypeStruct((dil, length, wq), BF16), jax.ShapeDtypeStruct((dil, length, LANES), F32)],
        scratch_shapes=scratch,
    )(*args)


def _band_bwd(q, k, v, do, lse, delta, *, max_dist, name):
    dil, length, wq = q.shape
    wk = k.shape[2]
    gqa = wk != wq
    tq = min(1024, length)
    ns, nt = tq // BLOCK, length // tq
    npair = wq // LANES
    nblocks = length // BLOCK
    bias = _band_bias(max_dist, transposed=True)

    def body(q_ref, qn_ref, do_ref, don_ref, lse_ref, lsen_ref, dl_ref, dln_ref, k_ref, v_ref, bias_ref,
             dq_ref, dk_ref, dv_ref, stat_l, stat_d, dqt, kt, *rolled):
        i = pl.program_id(1)
        for pk in range(wk // LANES):
            kt[pk] = k_ref[:, pk * LANES:(pk + 1) * LANES].astype(F32).T.astype(BF16)
        if gqa:
            kroll, vroll, ktroll = rolled
            kroll[...] = pltpu.roll(k_ref[...], HEAD_DIM, 1)
            vroll[...] = pltpu.roll(v_ref[...], HEAD_DIM, 1)
            ktroll[0] = kroll[...].astype(F32).T.astype(BF16)
        for a in range(ns):
            rows = slice(a * BLOCK, (a + 1) * BLOCK)
            stat_l[a] = _rows_to_lanes(lse_ref[rows, :])
            stat_d[a] = _rows_to_lanes(dl_ref[rows, :])
        stat_l[ns] = _rows_to_lanes(lsen_ref[...])
        stat_d[ns] = _rows_to_lanes(dln_ref[...])

        @pl.when(i == 0)
        def _():
            dqt[:, :, 0:BLOCK] = jnp.zeros((npair, LANES, BLOCK), F32)

        @pl.when(i > 0)
        def _():
            dqt[:, :, 0:BLOCK] = dqt[:, :, tq:tq + BLOCK]

        dqt[:, :, BLOCK:] = jnp.zeros((npair, LANES, tq), F32)
        half2 = _half_masks(2 * BLOCK)
        row = lax.broadcasted_iota(jnp.int32, (LANES, BLOCK), 0)
        row_half = (row < HEAD_DIM, row >= HEAD_DIM)
        col_next = (lax.broadcasted_iota(jnp.int32, (1, 2 * BLOCK), 1) >= BLOCK).astype(F32)

        for b in range(ns):
            rows = slice(b * BLOCK, (b + 1) * BLOCK)
            window = slice(b * BLOCK, (b + 2) * BLOCK)
            bt = bias_ref[...]
            if b == ns - 1:
                bt = bt + jnp.where(i == nt - 1, NEG, 0.0) * col_next
            acc = {}
            items = []
            nxt_rows = slice((b + 1) * BLOCK, (b + 2) * BLOCK)
            for p in range(npair):
                lanes = slice(p * LANES, (p + 1) * LANES)
                q_next = q_ref[nxt_rows, lanes] if b + 1 < ns else qn_ref[:, lanes]
                do_next = do_ref[nxt_rows, lanes] if b + 1 < ns else don_ref[:, lanes]
                qw = jnp.concatenate([q_ref[rows, lanes], q_next], axis=0)
                dow = jnp.concatenate([do_ref[rows, lanes], do_next], axis=0)
                for e in range(2):
                    h = 2 * p + e
                    pk, ek = _kv_place(h, gqa)
                    klanes = slice(pk * LANES, (pk + 1) * LANES)
                    kb = (k_ref if ek == e else kroll)[rows, klanes]
                    vb = (v_ref if ek == e else vroll)[rows, klanes]
                    qm = jnp.where(half2[e], qw, jnp.zeros_like(qw))
                    dom = jnp.where(half2[e], dow, jnp.zeros_like(dow))
                    items.append(dict(p=p, e=e, h=h, pk=pk, ek=ek, qm=qm, dom=dom,
                                      st=_dot_nt(kb, qm), dpt=_dot_nt(vb, dom)))
            for it in items:
                h = it["h"]
                lrow = jnp.concatenate([stat_l[b, h:h + 1, :], stat_l[b + 1, h:h + 1, :]], axis=1)
                drow = jnp.concatenate([stat_d[b, h:h + 1, :], stat_d[b + 1, h:h + 1, :]], axis=1)
                pt = jnp.exp(it["st"] + bt - lrow)
                it["ptb"] = pt.astype(BF16)
                it["dsb"] = (pt * (it["dpt"] - drow)).astype(BF16)
            for p in range(npair):
                pair = items[2 * p:2 * p + 2]
                lanes = slice(p * LANES, (p + 1) * LANES)
                kparts = []
                for it in pair:
                    kbt = (kt if it["ek"] == it["e"] else ktroll)[it["pk"], :, rows]
                    kparts.append(jnp.where(row_half[it["e"]], kbt, jnp.zeros_like(kbt)))
                ds_keys = jnp.concatenate([it["dsb"] for it in pair], axis=0)
                dqt[p, :, window] += _dot(jnp.concatenate(kparts, axis=1), ds_keys)
                if not gqa:
                    q_both = jnp.concatenate([it["qm"] for it in pair], axis=0)
                    do_both = jnp.concatenate([it["dom"] for it in pair], axis=0)
                    dk_ref[rows, lanes] = _dot(jnp.concatenate([it["dsb"] for it in pair], axis=1), q_both).astype(BF16)
                    dv_ref[rows, lanes] = _dot(jnp.concatenate([it["ptb"] for it in pair], axis=1), do_both).astype(BF16)
                else:
                    for it in pair:
                        dv_c = _dot(it["ptb"], it["dom"])
                        dk_c = _dot(it["dsb"], it["qm"])
                        key = (it["pk"], it["ek"] == it["e"])
                        if key in acc:
                            acc[key] = (acc[key][0] + dk_c, acc[key][1] + dv_c)
                        else:
                            acc[key] = (dk_c, dv_c)
            if gqa:
                dk_al, dv_al = acc[(0, True)]
                dk_mis, dv_mis = acc[(0, False)]
                dk_ref[rows, :] = (dk_al + pltpu.roll(dk_mis, HEAD_DIM, 1)).astype(BF16)
                dv_ref[rows, :] = (dv_al + pltpu.roll(dv_mis, HEAD_DIM, 1)).astype(BF16)

        for p in range(npair):
            dq_ref[:, p * LANES:(p + 1) * LANES] = dqt[p, :, 0:tq].T.astype(BF16)

    main = lambda w: pl.BlockSpec((None, tq, w), lambda r, i: (r, i, 0))
    nxt = lambda w: pl.BlockSpec((None, BLOCK, w), lambda r, i: (r, jnp.minimum((i + 1) * ns, nblocks - 1), 0))
    scratch = [pltpu.VMEM((ns + 1, 8, LANES), F32), pltpu.VMEM((ns + 1, 8, LANES), F32),
               pltpu.VMEM((npair, LANES, tq + BLOCK), F32), pltpu.VMEM((wk // LANES, LANES, tq), BF16)]
    if gqa:
        scratch = scratch + [pltpu.VMEM((tq, wk), BF16)] * 2 + [pltpu.VMEM((1, LANES, tq), BF16)]
    return pl.pallas_call(
        body, name=name, grid=(dil, nt),
        in_specs=[main(wq), nxt(wq), main(wq), nxt(wq), main(LANES), nxt(LANES), main(LANES), nxt(LANES),
                  main(wk), main(wk), pl.BlockSpec(bias.shape, lambda r, i: (0, 0))],
        out_specs=[main(wq), main(wk), main(wk)],
        out_shape=[jax.ShapeDtypeStruct((dil, length, wq), BF16), jax.ShapeDtypeStruct((dil, length, wk), BF16),
                   jax.ShapeDtypeStruct((dil, length, wk), BF16)],
        scratch_shapes=scratch,
        compiler_params=pltpu.CompilerParams(dimension_semantics=("arbitrary", "arbitrary")),
    )(q, q, do, do, lse, lse, delta, delta, k, v, bias)


def _mem_attn_fwd(q, mk, mv):
    seq = q.shape[0]
    tq = min(1024, seq)
    ns = tq // BLOCK

    def body(q_ref, mk_ref, mv_ref, o_ref, lse_ref):
        half = _half_masks(BLOCK)

        def sub(a, carry):
            r0 = pl.multiple_of(a * BLOCK, BLOCK)
            scores = []
            for p in range(C_W // LANES):
                lanes = slice(p * LANES, (p + 1) * LANES)
                qp = q_ref[pl.ds(r0, BLOCK), lanes]
                for e in range(2):
                    scores.append(_dot_nt(jnp.where(half[e], qp, jnp.zeros_like(qp)), mk_ref[:, lanes]))
            m_cols, l_cols, probs = [], [], []
            for s in scores:
                m = jnp.max(s, axis=1, keepdims=True)
                pe = jnp.exp(s - m)
                probs.append(pe.astype(BF16))
                m_cols.append(m)
                l_cols.append(jnp.sum(pe, axis=1, keepdims=True))
            for p in range(C_W // LANES):
                lanes = slice(p * LANES, (p + 1) * LANES)
                o_h = [_dot(probs[2 * p + e], mv_ref[:, lanes]) * (1.0 / l_cols[2 * p + e]) for e in range(2)]
                o_ref[pl.ds(r0, BLOCK), lanes] = jnp.where(half[0], o_h[0], o_h[1]).astype(BF16)
            lse_ref[pl.ds(r0, BLOCK), :] = _per_head(m_cols) + jnp.log(_per_head(l_cols, 1.0))
            return carry

        lax.fori_loop(0, ns, sub, 0, unroll=True)

    row = lambda w: pl.BlockSpec((tq, w), lambda i: (i, 0))
    full = pl.BlockSpec((N_MEM, C_W), lambda i: (0, 0))
    return pl.pallas_call(
        body, name="mem_attn_fwd", grid=(seq // tq,), in_specs=[row(C_W), full, full],
        out_specs=[row(C_W), row(LANES)],
        out_shape=[jax.ShapeDtypeStruct((seq, C_W), BF16), jax.ShapeDtypeStruct((seq, LANES), F32)],
    )(q, mk, mv)


def _mem_attn_bwd(q, mk, mv, do, lse, delta):
    seq = q.shape[0]
    tq = min(1024, seq)
    ns = tq // BLOCK
    npair = C_W // LANES

    def body(q_ref, mk_ref, mv_ref, do_ref, lse_ref, dl_ref, dq_ref, dmk_ref, dmv_ref, stat_l, stat_d, mkt, dqt):
        @pl.when(pl.program_id(0) == 0)
        def _():
            dmk_ref[...] = jnp.zeros_like(dmk_ref)
            dmv_ref[...] = jnp.zeros_like(dmv_ref)
            for p in range(npair):
                mkt[p] = mk_ref[:, p * LANES:(p + 1) * LANES].astype(F32).T.astype(BF16)

        for a in range(ns):
            rows = slice(a * BLOCK, (a + 1) * BLOCK)
            stat_l[a] = _rows_to_lanes(lse_ref[rows, :])
            stat_d[a] = _rows_to_lanes(dl_ref[rows, :])
        half = _half_masks(BLOCK)
        row = lax.broadcasted_iota(jnp.int32, (LANES, N_MEM), 0)
        row_half = (row < HEAD_DIM, row >= HEAD_DIM)

        for a in range(ns):
            rows = slice(a * BLOCK, (a + 1) * BLOCK)
            items = []
            for p in range(npair):
                lanes = slice(p * LANES, (p + 1) * LANES)
                qp = q_ref[rows, lanes]
                dop = do_ref[rows, lanes]
                for e in range(2):
                    qm = jnp.where(half[e], qp, jnp.zeros_like(qp))
                    dom = jnp.where(half[e], dop, jnp.zeros_like(dop))
                    items.append(dict(p=p, e=e, qm=qm, dom=dom, st=_dot_nt(mk_ref[:, lanes], qm),
                                      dpt=_dot_nt(mv_ref[:, lanes], dom)))
            for it in items:
                h = 2 * it["p"] + it["e"]
                pt = jnp.exp(it["st"] - stat_l[a, h:h + 1, :])
                it["ptb"] = pt.astype(BF16)
                it["dsb"] = (pt * (it["dpt"] - stat_d[a, h:h + 1, :])).astype(BF16)
            for p in range(npair):
                lanes = slice(p * LANES, (p + 1) * LANES)
                pair = [it for it in items if it["p"] == p]
                join = lambda name, axis: jnp.concatenate([it[name] for it in pair], axis=axis)
                dmv_ref[:, lanes] += _dot(join("ptb", 1), join("dom", 0))
                dmk_ref[:, lanes] += _dot(join("dsb", 1), join("qm", 0))
                kbt = mkt[p]
                k_both = jnp.concatenate([jnp.where(row_half[e], kbt, jnp.zeros_like(kbt)) for e in range(2)], axis=1)
                dqt[p, :, rows] = _dot(k_both, join("dsb", 0))
        for p in range(npair):
            dq_ref[:, p * LANES:(p + 1) * LANES] = dqt[p].T.astype(BF16)

    row = lambda w: pl.BlockSpec((tq, w), lambda i: (i, 0))
    full = pl.BlockSpec((N_MEM, C_W), lambda i: (0, 0))
    return pl.pallas_call(
        body, name="mem_attn_bwd", grid=(seq // tq,),
        in_specs=[row(C_W), full, full, row(C_W), row(LANES), row(LANES)], out_specs=[row(C_W), full, full],
        out_shape=[jax.ShapeDtypeStruct((seq, C_W), BF16), jax.ShapeDtypeStruct((N_MEM, C_W), F32),
                   jax.ShapeDtypeStruct((N_MEM, C_W), F32)],
        scratch_shapes=[pltpu.VMEM((ns, 8, LANES), F32)] * 2
        + [pltpu.VMEM((npair, LANES, N_MEM), BF16), pltpu.VMEM((npair, LANES, tq), F32)],
        compiler_params=pltpu.CompilerParams(dimension_semantics=("arbitrary",)),
    )(q, mk, mv, do, lse, delta)


def _silu_and_grad(g):
    s = 1.0 / (1.0 + jnp.exp(-g))
    return g * s, s * (1.0 + g * (1.0 - s))


def _post(x, target, post_norm, w_out, sink_row, oa, lse_a, ga, ob_list, lseb_list, gb, oc, gc):
    seq = x.shape[0]
    tm = min(512, seq)
    inv_d = 1.0 / D_MODEL
    nd = len(B_DILS)

    def body(*refs):
        (x_ref, t_ref, gp_ref, w_ref, sink_ref, oa_ref, lsea_ref, ga_ref), refs = refs[:8], refs[8:]
        ob_refs, lb_refs, (gb_ref, oc_ref, gc_ref), refs = refs[:nd], refs[nd:2 * nd], refs[2 * nd:2 * nd + 3], refs[2 * nd + 3:]
        (g_ref, doa_ref, dla_ref, dga_ref), refs = refs[:4], refs[4:]
        dob_refs, lsec_refs, dlb_refs, refs = refs[:nd], refs[nd:2 * nd], refs[2 * nd:3 * nd], refs[3 * nd:]
        (dgb_ref, doc_ref, dlc_ref, dgc_ref, gw_ref, gpost_ref, gsink_ref, loss_ref), refs = refs[:8], refs[8:]
        ycat, obufs, lbufs, st_do, st_l, st_d = refs[0], refs[1:nd], refs[nd:2 * nd - 1], refs[2 * nd - 1], refs[2 * nd], refs[2 * nd + 1]

        @pl.when(pl.program_id(0) == 0)
        def _():
            gw_ref[...] = jnp.zeros_like(gw_ref)
            gpost_ref[...] = jnp.zeros_like(gpost_ref)
            gsink_ref[...] = jnp.zeros_like(gsink_ref)
            loss_ref[...] = jnp.zeros_like(loss_ref)

        o_i, l_i = [ob_refs[0][0].astype(F32)], [lb_refs[0][0]]
        for k in range(1, nd):
            _from_residues(ob_refs[k], obufs[k - 1], B_DILS[k])
            _from_residues(lb_refs[k], lbufs[k - 1], B_DILS[k])
            o_i.append(_stage_read(obufs[k - 1]))
            l_i.append(_stage_read(lbufs[k - 1]))
        mx = l_i[0]
        for l in l_i[1:]:
            mx = jnp.maximum(mx, l)
        w_i = [jnp.exp(l - mx) for l in l_i]
        z = w_i[0]
        for w in w_i[1:]:
            z = z + w
        _stage_write(st_l, mx + jnp.log(z))
        expand = _head_expand_matrix(B_W)
        inv_z = 1.0 / z
        ob = None
        for w, o in zip(w_i, o_i):
            term = _dot_split(w * inv_z, expand, 2) * o
            ob = term if ob is None else ob + term
        oa, oc = oa_ref[...].astype(F32), oc_ref[...].astype(F32)
        sa, dsa = _silu_and_grad(ga_ref[...].astype(F32))
        sb, dsb = _silu_and_grad(gb_ref[...].astype(F32))
        sc, dsc = _silu_and_grad(gc_ref[...].astype(F32))
        ycat[:, 0:A_W] = (oa * sa).astype(BF16)
        ycat[:, A_W:A_W + B_W] = (ob * sb).astype(BF16)
        ycat[:, A_W + B_W:] = (oc * sc).astype(BF16)
        y2 = _dot(ycat[...], w_ref[...])
        r = lax.rsqrt(jnp.mean(y2 * y2, axis=-1, keepdims=True) + RMS_EPS)
        zhat = y2 * r
        gp = gp_ref[...]
        err = x_ref[...] + zhat * gp - t_ref[...]
        loss_ref[...] += jnp.sum(err * err) * (0.5 * inv_d)
        g = err * inv_d
        g_ref[...] = g
        gpost_ref[...] += jnp.sum(g * zhat, axis=0, keepdims=True)
        a = g * gp
        dy2 = (r * (a - zhat * jnp.mean(a * zhat, axis=-1, keepdims=True))).astype(BF16)
        gw_ref[...] += _dot_tn(ycat[...], dy2)
        dycat = _dot_nt(dy2, w_ref[...])
        dya, dyb, dyc = dycat[:, 0:A_W], dycat[:, A_W:A_W + B_W], dycat[:, A_W + B_W:]
        doa, dob, doc = dya * sa, dyb * sb, dyc * sc
        doa_ref[...] = doa.astype(BF16)
        doc_ref[...] = doc.astype(BF16)
        dga_ref[...] = (dya * oa * dsa).astype(BF16)
        dgb_ref[...] = (dyb * ob * dsb).astype(BF16)
        dgc_ref[...] = (dyc * oc * dsc).astype(BF16)
        dl_a = _dot_split(doa * oa, _head_sum_matrix(A_W), 2)
        dla_ref[...] = dl_a
        dlc_ref[...] = _dot_split(doc * oc, _head_sum_matrix(C_W), 2)
        gsink_ref[...] += jnp.sum(jnp.exp(sink_ref[...] - lsea_ref[...]) * dl_a, axis=0, keepdims=True)
        _stage_write(st_do, dob)
        _stage_write(st_d, _dot_split(dob * ob, _head_sum_matrix(B_W), 2))
        for k, dil in enumerate(B_DILS):
            _to_residues(st_do, dob_refs[k], dil)
            _to_residues(st_l, lsec_refs[k], dil)
            _to_residues(st_d, dlb_refs[k], dil)

    row = lambda w: pl.BlockSpec((tm, w), lambda i: (i, 0))
    full = lambda shape: pl.BlockSpec(shape, lambda i: (0,) * len(shape))
    res_specs = lambda w: [_residue_spec(d, tm, w) for d in B_DILS]
    res_shapes = lambda w, dt: [jax.ShapeDtypeStruct((d, seq // d, w), dt) for d in B_DILS]
    ins = [x, target, post_norm, w_out, sink_row, oa, lse_a, ga, *ob_list, *lseb_list, gb, oc, gc]
    in_specs = ([row(D_MODEL), row(D_MODEL), full((1, D_MODEL)), full((D_MODEL, D_MODEL)), full((1, LANES)),
                 row(A_W), row(LANES), row(A_W)] + res_specs(B_W) + res_specs(LANES) + [row(B_W), row(C_W), row(C_W)])
    out_shape = ([jax.ShapeDtypeStruct((seq, D_MODEL), F32), jax.ShapeDtypeStruct((seq, A_W), BF16),
                  jax.ShapeDtypeStruct((seq, LANES), F32), jax.ShapeDtypeStruct((seq, A_W), BF16)]
                 + res_shapes(B_W, BF16) + res_shapes(LANES, F32) + res_shapes(LANES, F32)
                 + [jax.ShapeDtypeStruct((seq, B_W), BF16), jax.ShapeDtypeStruct((seq, C_W), BF16),
                    jax.ShapeDtypeStruct((seq, LANES), F32), jax.ShapeDtypeStruct((seq, C_W), BF16),
                    jax.ShapeDtypeStruct((D_MODEL, D_MODEL), F32), jax.ShapeDtypeStruct((1, D_MODEL), F32),
                    jax.ShapeDtypeStruct((1, LANES), F32), jax.ShapeDtypeStruct((1, LANES), F32)])
    out_specs = ([row(D_MODEL), row(A_W), row(LANES), row(A_W)] + res_specs(B_W) + res_specs(LANES) + res_specs(LANES)
                 + [row(B_W), row(C_W), row(LANES), row(C_W),
                    full((D_MODEL, D_MODEL)), full((1, D_MODEL)), full((1, LANES)), full((1, LANES))])
    scratch = ([pltpu.VMEM((tm, D_MODEL), BF16)] + [_stage(tm, B_W)] * (nd - 1) + [_stage(tm, LANES)] * (nd - 1)
               + [_stage(tm, B_W), _stage(tm, LANES), _stage(tm, LANES)])
    res = pl.pallas_call(
        body, name="post", grid=(seq // tm,), in_specs=in_specs, out_specs=out_specs, out_shape=out_shape,
        scratch_shapes=scratch,
        compiler_params=pltpu.CompilerParams(dimension_semantics=("arbitrary",)),
    )(*ins)
    out = dict(g=res[0], doa=res[1], dl_a=res[2], dga=res[3], dob=res[4:4 + nd], lse_b=res[4 + nd:4 + 2 * nd],
               dl_b=res[4 + 2 * nd:4 + 3 * nd])
    rest = res[4 + 3 * nd:]
    out.update(dgb=rest[0], doc=rest[1], dl_c=rest[2], dgc=rest[3], gw_out=rest[4], gpost=rest[5], gsink=rest[6],
               loss=rest[7])
    return out


def _grad_w_in(ut, nat, res):
    seq = ut.shape[1]
    tm = min(512, seq)
    nd = len(B_DILS)
    nat_list = [nat[n] for n in _NATURAL]
    res_list = [a for n in _DILATED for a in res[n]]
    rope = _rope_tables(seq, tm)

    def body(rl_ref, rb_ref, ut_ref, *refs):
        nat_refs = dict(zip(_NATURAL, refs[:len(_NATURAL)]))
        refs = refs[len(_NATURAL):]
        res_refs = {n: refs[nd * k:nd * (k + 1)] for k, n in enumerate(_DILATED)}
        refs = refs[nd * len(_DILATED):]
        dproj_ref, gw_ref = refs[:2]
        bufs = {n: refs[2 + (nd - 1) * k:2 + (nd - 1) * (k + 1)] for k, n in enumerate(_DILATED)}

        @pl.when(pl.program_id(0) == 0)
        def _():
            gw_ref[...] = jnp.zeros_like(gw_ref)

        for n in _DILATED:
            for k in range(1, nd):
                _from_residues(res_refs[n][k], bufs[n][k - 1], B_DILS[k])
        c, sm, sp = _rope_coeffs(rl_ref, rb_ref)
        sm, sp = -sm, -sp
        for blk, (name, off, roped, scaled) in enumerate(_PROJ_LAYOUT):
            lanes = slice(off, off + LANES)
            if name in nat_refs:
                piece = nat_refs[name][:, lanes].astype(F32)
            else:
                piece = res_refs[name][0][0, :, lanes].astype(F32)
                for buf in bufs[name]:
                    piece = piece + buf[off // LANES]
            if roped:
                piece = _rope(piece, c, sm, sp)
            if scaled:
                piece = piece * SCALE
            dproj_ref[:, blk * LANES:(blk + 1) * LANES] = piece.astype(BF16)
        for j in range(N_CHIPS):
            gw_ref[j] += _dot(ut_ref[...], dproj_ref[:, j * SHARD_IN:(j + 1) * SHARD_IN])

    row = lambda w: pl.BlockSpec((tm, w), lambda i: (i, 0))
    in_specs = ([pl.BlockSpec(rope[0].shape, lambda i: (0, 0)), pl.BlockSpec((8, 2 * LANES), lambda i: (i, 0)),
                 pl.BlockSpec((D_MODEL, tm), lambda i: (0, i))]
                + [row(a.shape[1]) for a in nat_list]
                + [_residue_spec(d, tm, B_W) for _ in _DILATED for d in B_DILS])
    return pl.pallas_call(
        body, name="grad_w_in", grid=(seq // tm,), in_specs=in_specs,
        out_specs=[row(D_IN), pl.BlockSpec((N_CHIPS, D_MODEL, SHARD_IN), lambda i: (0, 0, 0))],
        out_shape=[jax.ShapeDtypeStruct((seq, D_IN), BF16), jax.ShapeDtypeStruct((N_CHIPS, D_MODEL, SHARD_IN), F32)],
        scratch_shapes=[_stage(tm, B_W)] * ((nd - 1) * len(_DILATED)),
        compiler_params=pltpu.CompilerParams(dimension_semantics=("arbitrary",)),
    )(*rope, ut, *nat_list, *res_list)


def _input_grad(x, g, pre_norm, w_in_g, dproj, gx_prev, span, after, name):
    seq = x.shape[0]
    tm = seq // 16
    first_block, steps = span

    def body(*refs):
        x_ref, g_ref, gp_ref, w_ref, dp_ref = refs[:5]
        gx_ref, gpre_ref = refs[-2:]

        @pl.when(pl.program_id(0) == 0)
        def _():
            gpre_ref[...] = jnp.zeros_like(gpre_ref)

        du = None
        for j in range(N_CHIPS):
            term = _dot_nt(dp_ref[:, j * SHARD_IN:(j + 1) * SHARD_IN], w_ref[j])
            du = term if du is None else du + term
        xv = x_ref[...]
        r = lax.rsqrt(jnp.mean(xv * xv, axis=-1, keepdims=True) + RMS_EPS)
        xhat = xv * r
        gpre_ref[...] += jnp.sum(du * xhat, axis=0, keepdims=True)
        a = du * gp_ref[...]
        gx_ref[...] = g_ref[...] + r * (a - xhat * jnp.mean(a * xhat, axis=-1, keepdims=True))

    row = lambda w: pl.BlockSpec((tm, w), lambda i: (first_block + i, 0))
    full = lambda a: pl.BlockSpec(a.shape, lambda i: (0,) * a.ndim)
    any_spec = pl.BlockSpec(memory_space=pl.ANY)
    ins = [x, g, pre_norm, w_in_g, dproj]
    in_specs = [row(D_MODEL), row(D_MODEL), full(pre_norm), full(w_in_g), row(D_IN)]
    aliases = {}
    if gx_prev is not None:
        aliases[len(ins)] = 0
        ins.append(gx_prev)
        in_specs.append(any_spec)
    if after is not None:
        ins.append(after)
        in_specs.append(any_spec)
    return pl.pallas_call(
        body, name=name, grid=(steps,), in_specs=in_specs,
        out_specs=[row(D_MODEL), pl.BlockSpec((1, D_MODEL), lambda i: (0, 0))],
        out_shape=[jax.ShapeDtypeStruct((seq, D_MODEL), F32), jax.ShapeDtypeStruct((1, D_MODEL), F32)],
        input_output_aliases=aliases,
        compiler_params=pltpu.CompilerParams(dimension_semantics=("arbitrary",)),
    )(*ins)


def _exchange_start(ex, name):
    n_in, n_out, n_sem = len(ex["ins"]), len(ex["outs"]), len(ex["sems"])

    def body(*refs):
        in_refs, land_refs, sems = refs[:n_in], refs[n_in:n_in + n_out], refs[n_in + n_out:n_in + n_out + n_sem]
        ex["start"](in_refs, land_refs, *sems)
        token = refs[-1]
        token[...] = jnp.zeros_like(token)

    hbm = pl.BlockSpec(memory_space=pltpu.HBM)
    sem = pl.BlockSpec(memory_space=pltpu.SEMAPHORE)
    ins = [pltpu.with_memory_space_constraint(a, pltpu.HBM) for a in ex["ins"]]
    landing = [pltpu.with_memory_space_constraint(lax.empty(o.shape, o.dtype), pltpu.HBM) for o in ex["outs"]]
    res = pl.pallas_call(
        body, name=name,
        out_shape=list(ex["sems"]) + [pltpu.HBM(a.shape, a.dtype) for a in ex["ins"]]
        + [pltpu.HBM(o.shape, o.dtype) for o in ex["outs"]] + [jax.ShapeDtypeStruct((8, LANES), F32)],
        in_specs=[hbm] * (n_in + n_out),
        out_specs=[sem] * n_sem + [hbm] * (n_in + n_out) + [pl.BlockSpec(memory_space=pltpu.VMEM)],
        input_output_aliases={k: n_sem + k for k in range(n_in + n_out)},
        compiler_params=pltpu.CompilerParams(has_side_effects=pltpu.SideEffectType.DATAFLOW_SIDE_EFFECTING),
    )(*ins, *landing)
    return res[:-1], res[-1]


def _exchange_wait(ex, handles, after, name):
    n_in, n_out, n_sem = len(ex["ins"]), len(ex["outs"]), len(ex["sems"])
    sems, thru = handles[:n_sem], handles[n_sem:]

    def body(*refs):
        in_refs, land_refs = refs[:n_in], refs[n_in:n_in + n_out]
        sem_refs = refs[n_in + n_out:n_in + n_out + n_sem]
        ex["finish"](in_refs, land_refs, *sem_refs)

    hbm = pl.BlockSpec(memory_space=pltpu.HBM)
    sem = pl.BlockSpec(memory_space=pltpu.SEMAPHORE)
    res = pl.pallas_call(
        body, name=name,
        out_shape=[pltpu.HBM(a.shape, a.dtype) for a in thru],
        in_specs=[hbm] * (n_in + n_out) + [sem] * n_sem + [pl.BlockSpec(memory_space=pl.ANY)],
        out_specs=[hbm] * (n_in + n_out),
        input_output_aliases={k: k for k in range(n_in + n_out)},
        compiler_params=pltpu.CompilerParams(has_side_effects=pltpu.SideEffectType.DATAFLOW_SIDE_EFFECTING),
    )(*thru, *sems, after)
    return res[:n_in], res[n_in:]


def _start_finish(build):
    def start(*refs):
        for cp in build(*refs):
            cp.start()

    def finish(*refs):
        for cp in build(*refs):
            cp.wait()

    return dict(start=start, finish=finish)


def _pair_exchange(grads):
    n = len(grads)

    def build(srcs, outs, send_sems, recv_sems):
        x, y, c = lax.axis_index("x"), lax.axis_index("y"), lax.axis_index("c")
        copies = []
        for t in range(n):
            rows = grads[t].shape[1] // 2
            copies.append(pltpu.make_async_remote_copy(
                src_ref=srcs[t].at[:, pl.ds((1 - c) * rows, rows)], dst_ref=outs[t],
                send_sem=send_sems.at[t], recv_sem=recv_sems.at[t], device_id=(x, y, 1 - c), device_id_type=MESH))
        return copies

    return dict(ins=list(grads), **_start_finish(build),
                outs=[jax.ShapeDtypeStruct((g.shape[0], g.shape[1] // 2, g.shape[2]), g.dtype) for g in grads],
                sems=[pltpu.SemaphoreType.DMA((n,)), pltpu.SemaphoreType.DMA((n,))])


def _pair_add(core, own, got):
    nchip, rows2, width = own.shape
    rows = rows2 // 2
    tr = min(512, rows)
    nb = rows // tr

    def body(core_ref, own_ref, got_ref, out_ref):
        out_ref[...] = (own_ref[...] + got_ref[...]).astype(BF16)

    grid_spec = pltpu.PrefetchScalarGridSpec(
        num_scalar_prefetch=1, grid=(nchip, nb),
        in_specs=[pl.BlockSpec((None, tr, width), lambda k, i, core_ref: (k, core_ref[0] * nb + i, 0)),
                  pl.BlockSpec((None, tr, width), lambda k, i, core_ref: (k, i, 0))],
        out_specs=pl.BlockSpec((None, tr, width), lambda k, i, core_ref: (k, i, 0)))
    return pl.pallas_call(
        body, name=f"pair_add_{width}", grid_spec=grid_spec,
        out_shape=jax.ShapeDtypeStruct((nchip, rows, width), BF16),
    )(core, own, got)


def _chip_exchange(parts):
    n = len(parts)

    def build(srcs, outs, send_sems, recv_sems, local_sems):
        x, y, c = lax.axis_index("x"), lax.axis_index("y"), lax.axis_index("c")
        my_chip = 2 * x + y
        chips = [(1 - x, y), (x, 1 - y), (1 - x, 1 - y)]
        copies = [pltpu.make_async_copy(srcs[t].at[my_chip], outs[t].at[my_chip], local_sems.at[t]) for t in range(n)]
        for j, (cx, cy) in enumerate(chips):
            for t in range(n):
                k = n * j + t
                copies.append(pltpu.make_async_remote_copy(
                    src_ref=srcs[t].at[2 * cx + cy], dst_ref=outs[t].at[my_chip], send_sem=send_sems.at[k],
                    recv_sem=recv_sems.at[k], device_id=(cx, cy, c), device_id_type=MESH))
        return copies

    return dict(ins=list(parts), **_start_finish(build), outs=[jax.ShapeDtypeStruct(p.shape, p.dtype) for p in parts],
                sems=[pltpu.SemaphoreType.DMA((3 * n,)), pltpu.SemaphoreType.DMA((3 * n,)),
                      pltpu.SemaphoreType.DMA((n,))])


def _slot_sum(slots, name, core=None):
    ns, rows, width = slots.shape
    tr = min(512, rows)

    def body(*refs):
        in_ref, out_ref = refs[-2:]
        acc = in_ref[0].astype(F32)
        for s in range(1, ns):
            acc = acc + in_ref[s].astype(F32)
        out_ref[...] = acc

    if core is None:
        return pl.pallas_call(
            body, name=name, grid=(rows // tr,),
            in_specs=[pl.BlockSpec((ns, tr, width), lambda i: (0, i, 0))],
            out_specs=pl.BlockSpec((tr, width), lambda i: (i, 0)),
            out_shape=jax.ShapeDtypeStruct((rows, width), F32),
        )(slots)
    grid_spec = pltpu.PrefetchScalarGridSpec(
        num_scalar_prefetch=1, grid=(rows // tr,),
        in_specs=[pl.BlockSpec((ns, tr, width), lambda i, core_ref: (0, i, 0))],
        out_specs=pl.BlockSpec((None, tr, width), lambda i, core_ref: (core_ref[0], i, 0)))
    return pl.pallas_call(
        body, name=name, grid_spec=grid_spec, out_shape=jax.ShapeDtypeStruct((2, rows, width), F32),
    )(core, slots)


def _pair_gather(bufs, small):
    n = len(bufs)

    def body(*refs):
        small_ref, outs, small_out = refs[n], refs[n + 1:2 * n + 1], refs[2 * n + 1]
        send_sems, recv_sems, local_sem = refs[2 * n + 2:]
        x, y, c = lax.axis_index("x"), lax.axis_index("y"), lax.axis_index("c")
        me = 4 * x + 2 * y + c
        chips = [(1 - x, y), (x, 1 - y), (1 - x, 1 - y)]
        mine = pltpu.make_async_copy(small_ref, small_out.at[me], local_sem)
        mine.start()
        copies = [pltpu.make_async_remote_copy(
            src_ref=outs[t].at[c], dst_ref=outs[t].at[c], send_sem=send_sems.at[t], recv_sem=recv_sems.at[t],
            device_id=(x, y, 1 - c), device_id_type=MESH) for t in range(n)]
        peers = [(x, y, 1 - c)] + [(cx, cy, cc) for (cx, cy) in chips for cc in (c, 1 - c)]
        for j, peer in enumerate(peers):
            copies.append(pltpu.make_async_remote_copy(
                src_ref=small_ref, dst_ref=small_out.at[me], send_sem=send_sems.at[n + j],
                recv_sem=recv_sems.at[n + j], device_id=peer, device_id_type=MESH))
        for cp in copies:
            cp.start()
        for cp in copies:
            cp.wait()
        mine.wait()

    any_spec = pl.BlockSpec(memory_space=pl.ANY)
    res = pl.pallas_call(
        body, name="pair_gather",
        out_shape=[jax.ShapeDtypeStruct(b.shape, b.dtype) for b in bufs]
        + [jax.ShapeDtypeStruct((8,) + small.shape, small.dtype)],
        in_specs=[any_spec] * (n + 1), out_specs=[any_spec] * (n + 1),
        input_output_aliases={t: t for t in range(n)},
        scratch_shapes=[pltpu.SemaphoreType.DMA((n + 7,)), pltpu.SemaphoreType.DMA((n + 7,)),
                        pltpu.SemaphoreType.DMA],
    )(*bufs, small)
    return [r.reshape(2 * b.shape[1], b.shape[2]) for r, b in zip(res[:n], bufs)], res[n]


def _adamw(w, g, m, v, name):
    rows, width = w.shape
    tr = min(256, rows)
    c1 = 1.0 / (1.0 - ADAM_B1 ** ADAM_STEP)
    c2 = 1.0 / (1.0 - ADAM_B2 ** ADAM_STEP)

    def body(w_ref, g_ref, m_ref, v_ref, d_ref, nm_ref, nv_ref):
        gv = g_ref[...]
        nm = ADAM_B1 * m_ref[...] + (1.0 - ADAM_B1) * gv
        nv = ADAM_B2 * v_ref[...] + (1.0 - ADAM_B2) * (gv * gv)
        nm_ref[...] = nm
        nv_ref[...] = nv
        d_ref[...] = -ADAM_LR * ((nm * c1) / (jnp.sqrt(nv * c2) + ADAM_EPS) + ADAM_WD * w_ref[...])

    spec = pl.BlockSpec((tr, width), lambda i: (i, 0))
    return pl.pallas_call(
        body, name=name, grid=(rows // tr,), in_specs=[spec] * 4, out_specs=[spec] * 3,
        out_shape=[jax.ShapeDtypeStruct(w.shape, F32)] * 3,
    )(w, g, m, v)


def _local_step(x, mem, target, pre_norm, sink_a, mem_norm, post_norm, w_in_g, w_out, w_mkv, gathers=None):
    first_gather, late_gather = gathers if gathers else (None, None)
    u, ut, hosted = _pre_norm(x, pre_norm, first_gather)
    if gathers:
        w_in_g = hosted[0].reshape(N_CHIPS, D_MODEL, SHARD_IN)
    pr = _pre_proj(u, w_in_g, late_gather)
    pr["ut"] = ut
    if gathers:
        w_out, w_mkv = (g.reshape(D_MODEL, g.shape[-1]) for g in pr["hosted"])
    mk, mv = _mem_kv(mem, mem_norm, w_mkv)
    sink = sink_a.reshape(-1)
    qa, ka, va = pr["qa"][None], pr["ka"][None], pr["va"][None]
    oa, lse_a = _band_fwd(qa, ka, va, sink, max_dist=A_WINDOW - 1, name="swa_fwd")
    ob_list, lseb_list = [], []
    for k, (win, dil) in enumerate(B_CONFIGS):
        o_i, l_i = _band_fwd(pr["qb"][k], pr["kb"][k], pr["vb"][k], None, max_dist=win // dil, name=f"dil{dil}_fwd")
        ob_list.append(o_i)
        lseb_list.append(l_i)
    oc, lse_c = _mem_attn_fwd(pr["qc"], mk, mv)
    sink_row = jnp.pad(sink, (0, LANES - sink.shape[0])).reshape(1, LANES)
    po = _post(x, target, post_norm, w_out, sink_row, oa[0], lse_a[0], pr["ga"], ob_list, lseb_list, pr["gb"], oc,
               pr["gc"])
    dqc, dmk, dmv = _mem_attn_bwd(pr["qc"], mk, mv, po["doc"], lse_c, po["dl_c"])
    dqa, dka, dva = _band_bwd(qa, ka, va, po["doa"][None], lse_a, po["dl_a"][None], max_dist=A_WINDOW - 1,
                              name="swa_bwd")
    res = dict(qb=[], kb=[], vb=[])
    for k, (win, dil) in enumerate(B_CONFIGS):
        dq_i, dk_i, dv_i = _band_bwd(pr["qb"][k], pr["kb"][k], pr["vb"][k], po["dob"][k], po["lse_b"][k],
                                     po["dl_b"][k], max_dist=win // dil, name=f"dil{dil}_bwd")
        res["qb"].append(dq_i)
        res["kb"].append(dk_i)
        res["vb"].append(dv_i)
    nat = dict(qa=dqa[0], ka=dka[0], va=dva[0], ga=po["dga"], gb=po["dgb"], qc=dqc, gc=po["dgc"])
    dproj, gw_in = _grad_w_in(pr["ut"], nat, res)
    gw_mkv, gmem = _mem_kv_bwd(mem, mem_norm, w_mkv, dmk, dmv)
    gsink = -po["gsink"][0, :sink.shape[0]]
    return dict(loss=po["loss"], g=po["g"], dproj=dproj, gw_in=gw_in, gw_out=po["gw_out"], gw_mkv=gw_mkv,
                gpost=po["gpost"], gmem=gmem, gsink=gsink, w_in_g=w_in_g)


def kernel(x, mem, pre_norm, w_in, sink_a, mem_norm, w_mem_kv, w_out, post_norm, loss_target, m_pre_norm, m_w_in, m_sink_a, m_mem_norm, m_w_mem_kv, m_w_out, m_post_norm, v_pre_norm, v_w_in, v_sink_a, v_mem_norm, v_w_mem_kv, v_w_out, v_post_norm):
    gathers = (_gather_exchange([w_in[0].astype(BF16)]),
               _gather_exchange([w_out[0].astype(BF16), w_mem_kv[0].astype(BF16)]))
    loc = _local_step(x[0], mem[0], loss_target[0], pre_norm, sink_a, mem_norm, post_norm, None, None, None, gathers)
    big = [loc["gw_in"], loc["gw_out"].reshape(N_CHIPS, D_MODEL // N_CHIPS, D_MODEL),
           loc["gw_mkv"].reshape(N_CHIPS, D_MODEL // N_CHIPS, 2 * C_W)]
    core = lax.axis_index("c").astype(jnp.int32).reshape(1)
    w_in_full = loc["w_in_g"]
    step_in = (x[0], loc["g"], pre_norm, w_in_full, loc["dproj"])
    pair_ex = _pair_exchange(big)
    pair_handles, token = _exchange_start(pair_ex, "pair_exchange_start")
    gx_a, gpre_a = _input_grad(*step_in, None, (0, 3), token, "input_grad_a")
    big, got = _exchange_wait(pair_ex, pair_handles, gpre_a, "pair_exchange_wait")
    parts = [_pair_add(core, own, g) for own, g in zip(big, got)]
    chip_ex = _chip_exchange(parts)
    chip_handles, token = _exchange_start(chip_ex, "chip_exchange_start")
    grad_x, gpre_b = _input_grad(*step_in, gx_a, (3, 13), token, "input_grad_b")
    _, slots = _exchange_wait(chip_ex, chip_handles, gpre_b, "chip_exchange_wait")
    halves = [_slot_sum(s, name=f"chip_sum_{s.shape[2]}", core=core) for s in slots]
    widen = lambda a: jnp.pad(a.reshape(1, -1), ((0, 0), (0, D_MODEL - a.size)))
    small = jnp.concatenate([gpre_a, loc["gpost"], loc["gmem"], widen(loc["gsink"]), widen(loc["loss"]), gpre_b,
                             jnp.zeros((2, D_MODEL), F32)], axis=0)
    (g_in, g_out, g_mkv), small_slots = _pair_gather(halves, small)
    small_sum = _slot_sum(small_slots, name="device_sum")
    g_pre, g_post, g_mem = small_sum[0:1] + small_sum[5:6], small_sum[1:2], small_sum[2:3]
    g_sink = small_sum[3:4, :sink_a.shape[1]]
    loss = small_sum[4, 0]

    d_in, nm_in, nv_in = _adamw(w_in[0], g_in, m_w_in[0], v_w_in[0], "adamw_in")
    d_out, nm_out, nv_out = _adamw(w_out[0], g_out, m_w_out[0], v_w_out[0], "adamw_out")
    d_mkv, nm_mkv, nv_mkv = _adamw(w_mem_kv[0], g_mkv, m_w_mem_kv[0], v_w_mem_kv[0], "adamw_mkv")
    pad6 = lambda a: jnp.pad(a, ((0, 0), (0, D_MODEL - a.shape[1])))
    stack = lambda a, b, c_, d_: jnp.concatenate([a, b, c_, pad6(d_), jnp.zeros((4, D_MODEL), F32)], axis=0)
    d_s, nm_s, nv_s = _adamw(stack(pre_norm, post_norm, mem_norm, sink_a),
                             jnp.concatenate([g_pre, small_sum[1:]], axis=0),
                             stack(m_pre_norm, m_post_norm, m_mem_norm, m_sink_a),
                             stack(v_pre_norm, v_post_norm, v_mem_norm, v_sink_a), "adamw_small")
    ns_ = sink_a.shape[1]
    unpack = lambda a: (a[0:1], a[3:4, :ns_], a[2:3], a[1:2])
    d_pre, d_sink, d_mem, d_post = unpack(d_s)
    nm_pre, nm_sink, nm_mem, nm_post = unpack(nm_s)
    nv_pre, nv_sink, nv_mem, nv_post = unpack(nv_s)
    lead = lambda a: a[None]
    return (loss, lead(grad_x),
            g_pre, lead(g_in), g_sink, g_mem, lead(g_mkv), lead(g_out), g_post,
            d_pre, lead(d_in), d_sink, d_mem, lead(d_mkv), lead(d_out), d_post,
            nm_pre, lead(nm_in), nm_sink, nm_mem, lead(nm_mkv), lead(nm_out), nm_post,
            nv_pre, lead(nv_in), nv_sink, nv_mem, lead(nv_mkv), lead(nv_out), nv_post)
```

```python
import numpy as np
import jax
import jax.numpy as jnp
from jax import lax
from jax.experimental import pallas as pl
from jax.experimental.pallas import tpu as pltpu

F32 = jnp.float32
BF16 = jnp.bfloat16

D_MODEL = 1024
HEAD_DIM = 64
LANES = 128
BLOCK = 128
A_W, A_KV_W, B_W, C_W = 384, 128, 384, 256
N_MEM = 256
D_IN = 3072
N_CHIPS = 4
SHARD_IN = D_IN // N_CHIPS
B_CONFIGS = ((128, 1), (512, 4), (2048, 16))
B_DILS = tuple(d for _, d in B_CONFIGS)
A_WINDOW = 128
RMS_EPS = 1e-6
ROPE_THETA = 500000.0
SCALE = HEAD_DIM ** -0.5
NEG = -1e30
ADAM_LR, ADAM_B1, ADAM_B2, ADAM_EPS, ADAM_WD, ADAM_STEP = 0.001, 0.9, 0.999, 1e-08, 0.01, 10

NT = (((1,), (1,)), ((), ()))
TN = (((0,), (0,)), ((), ()))
MESH = pl.DeviceIdType.MESH

_PROJ_LAYOUT = (
    [("qa", 128 * i, True, True) for i in range(3)] + [("ka", 0, True, False), ("va", 0, False, False)]
    + [("ga", 128 * i, False, False) for i in range(3)]
    + [("qb", 128 * i, True, True) for i in range(3)] + [("kb", 128 * i, True, False) for i in range(3)]
    + [("vb", 128 * i, False, False) for i in range(3)] + [("gb", 128 * i, False, False) for i in range(3)]
    + [("qc", 128 * i, False, True) for i in range(2)] + [("gc", 128 * i, False, False) for i in range(2)]
)
_PROJ_WIDTH = dict(qa=A_W, ka=A_KV_W, va=A_KV_W, ga=A_W, qb=B_W, kb=B_W, vb=B_W, gb=B_W, qc=C_W, gc=C_W)
_NATURAL = ("qa", "ka", "va", "ga", "gb", "qc", "gc")
_DILATED = ("qb", "kb", "vb")


def _dot(a, b):
    return jnp.dot(a, b, preferred_element_type=F32)


def _dot_nt(a, b):
    return lax.dot_general(a, b, NT, preferred_element_type=F32)


def _dot_tn(a, b):
    return lax.dot_general(a, b, TN, preferred_element_type=F32)


def _half_masks(rows):
    lane = lax.broadcasted_iota(jnp.int32, (rows, LANES), 1)
    return lane < HEAD_DIM, lane >= HEAD_DIM


def _rope(t, c, sm, sp):
    return t * c + pltpu.roll(t, LANES - 8, 1) * sm + pltpu.roll(t, 8, 1) * sp


def _rope_tables(seq, tm):
    dim = jnp.arange(LANES) % HEAD_DIM
    inv_freq = ROPE_THETA ** (-jnp.arange(0, 16, 2, dtype=F32) / 16)
    freq = jnp.where(dim < 16, inv_freq[dim % 8], 0.0)[None, :]
    local = jnp.arange(tm, dtype=F32)[:, None] * freq
    base = (jnp.arange(seq // tm, dtype=F32) * tm)[:, None] * freq
    both = lambda a: jnp.concatenate([jnp.cos(a), jnp.sin(a)], axis=1)
    return both(local), jnp.repeat(both(base), 8, axis=0)


def _rope_coeffs(local_ref, base_ref):
    cl, sl = local_ref[:, :LANES], local_ref[:, LANES:]
    cb, sb = base_ref[0:1, :LANES], base_ref[0:1, LANES:]
    cos = cb * cl - sb * sl
    sin = sb * cl + cb * sl
    dim = lax.broadcasted_iota(jnp.int32, (1, LANES), 1) % HEAD_DIM
    return cos, jnp.where(dim < 8, -sin, 0.0), jnp.where((dim >= 8) & (dim < 16), sin, 0.0)


def _split3(x):
    a = x.astype(BF16)
    r = x - a.astype(F32)
    b = r.astype(BF16)
    c = (r - b.astype(F32)).astype(BF16)
    return a, b, c


def _rows_to_lanes(x):
    row = lax.broadcasted_iota(jnp.int32, (8, LANES), 0)
    lane = lax.broadcasted_iota(jnp.int32, (8, LANES), 1)
    eye = (row == lane).astype(BF16)
    a, b, c = _split3(x)
    return _dot_nt(eye, a) + _dot_nt(eye, b) + _dot_nt(eye, c)


def _head_sum_matrix(width):
    k = lax.broadcasted_iota(jnp.int32, (width, LANES), 0)
    h = lax.broadcasted_iota(jnp.int32, (width, LANES), 1)
    return (k // HEAD_DIM == h).astype(BF16)


def _head_expand_matrix(width):
    h = lax.broadcasted_iota(jnp.int32, (LANES, width), 0)
    k = lax.broadcasted_iota(jnp.int32, (LANES, width), 1)
    return (k // HEAD_DIM == h).astype(BF16)


def _dot_split(x, mat, terms):
    parts = _split3(x)[:terms]
    out = _dot(parts[0], mat)
    for p in parts[1:]:
        out = out + _dot(p, mat)
    return out


def _per_head(cols, fill=0.0):
    rows = cols[0].shape[0]
    lane = lax.broadcasted_iota(jnp.int32, (rows, LANES), 1)
    out = jnp.full((rows, LANES), fill, F32)
    for h, col in enumerate(cols):
        out = jnp.where(lane == h, col, out)
    return out


def _lane_blocks(width):
    return [slice(p * LANES, (p + 1) * LANES) for p in range(width // LANES)]


def _stage(rows, width):
    return pltpu.VMEM((width // LANES, rows, LANES), F32)


def _stage_write(buf, value):
    for p, lanes in enumerate(_lane_blocks(value.shape[1])):
        buf[p] = value[:, lanes]


def _stage_read(buf):
    return jnp.concatenate([buf[p] for p in range(buf.shape[0])], axis=1) if buf.shape[0] > 1 else buf[0]


def _to_residues(buf, out_ref, dil):
    rows = buf.shape[1] // dil
    for r in range(dil):
        for p in range(buf.shape[0]):
            plane = buf.at[p]
            out_ref[r, :, p * LANES:(p + 1) * LANES] = plane[pl.ds(r, rows, stride=dil), :].astype(out_ref.dtype)


def _from_residues(in_ref, buf, dil):
    rows = buf.shape[1] // dil
    for r in range(dil):
        for p in range(buf.shape[0]):
            plane = buf.at[p]
            plane[pl.ds(r, rows, stride=dil), :] = in_ref[r, :, p * LANES:(p + 1) * LANES].astype(F32)


def _residue_spec(dil, tm, width):
    return pl.BlockSpec((dil, tm // dil, width), lambda i: (0, i, 0))


def _gather_exchange(shards_2d):
    shards = tuple(s.reshape(2, s.shape[0] // 2, s.shape[1]) for s in shards_2d)
    n = len(shards)

    def copies(in_refs, out_refs, send_sems, recv_sems):
        srcs, outs = in_refs[:n], out_refs
        x, y, c = lax.axis_index("x"), lax.axis_index("y"), lax.axis_index("c")
        my_chip = 2 * x + y
        sibling = (x, y, 1 - c)
        chips = [(1 - x, y), (x, 1 - y), (1 - x, 1 - y)]

        def copy(k, src, dst, to):
            return pltpu.make_async_remote_copy(src_ref=src, dst_ref=dst, send_sem=send_sems.at[k],
                                                recv_sem=recv_sems.at[k], device_id=to, device_id_type=MESH)

        first, arrive, passed, sibling_arrive = [], [], [], []
        for j, (cx, cy) in enumerate(chips):
            chip = 2 * cx + cy
            for t in range(n):
                k = n * j + t
                first.append(copy(k, srcs[t].at[c], outs[t].at[my_chip, c], (cx, cy, c)))
                arrive.append(copy(k, srcs[t].at[c], outs[t].at[chip, c], (cx, cy, c)))
                passed.append(copy(n * 3 + k, outs[t].at[chip, c], outs[t].at[chip, c], sibling))
                sibling_arrive.append(copy(n * 3 + k, outs[t].at[chip, 1 - c], outs[t].at[chip, 1 - c], sibling))
        return first, arrive, passed, sibling_arrive

    def start(*refs):
        for cp in copies(*refs)[0]:
            cp.start()

    def finish(*refs):
        first, arrive, passed, sibling_arrive = copies(*refs)
        for got, fwd in zip(arrive, passed):
            got.wait_recv()
            fwd.start()
        for cp in sibling_arrive:
            cp.wait_recv()
        for cp in first + passed:
            cp.wait_send()

    my_chip = 2 * lax.axis_index("x") + lax.axis_index("y")
    landing = [lax.dynamic_update_slice(jnp.zeros((N_CHIPS,) + s.shape, s.dtype), s[None], (my_chip, 0, 0, 0))
               for s in shards]
    return dict(ins=list(shards) + landing, start=start, finish=finish, aliases={n + t: t for t in range(n)},
                outs=[jax.ShapeDtypeStruct((N_CHIPS,) + s.shape, s.dtype) for s in shards],
                sems=[pltpu.SemaphoreType.DMA((6 * n,)), pltpu.SemaphoreType.DMA((6 * n,))])


def _run_exchange(ex, name):
    n_in, n_out = len(ex["ins"]), len(ex["outs"])

    def body(*refs):
        in_refs, out_refs, sems = refs[:n_in], refs[n_in:n_in + n_out], refs[n_in + n_out:]
        ex["start"](in_refs, out_refs, *sems)
        ex["finish"](in_refs, out_refs, *sems)

    any_spec = pl.BlockSpec(memory_space=pl.ANY)
    return pl.pallas_call(
        body, name=name, out_shape=ex["outs"], in_specs=[any_spec] * n_in, out_specs=[any_spec] * n_out,
        input_output_aliases=ex.get("aliases", {}), scratch_shapes=ex["sems"],
    )(*ex["ins"])


def _mem_kv(mem, mem_norm, w_mkv):
    def body(mem_ref, g_ref, w_ref, mk_ref, mv_ref):
        m = mem_ref[...]
        r = lax.rsqrt(jnp.mean(m * m, axis=-1, keepdims=True) + RMS_EPS)
        mn = (m * r * g_ref[...]).astype(BF16)
        kv = _dot(mn, w_ref[...])
        mk_ref[...] = kv[:, :C_W].astype(BF16)
        mv_ref[...] = kv[:, C_W:].astype(BF16)

    return pl.pallas_call(
        body, name="mem_kv",
        out_shape=[jax.ShapeDtypeStruct((N_MEM, C_W), BF16)] * 2,
    )(mem, mem_norm, w_mkv)


def _mem_kv_bwd(mem, mem_norm, w_mkv, dmk, dmv):
    def body(mem_ref, g_ref, w_ref, dmk_ref, dmv_ref, gw_ref, gn_ref):
        m = mem_ref[...]
        r = lax.rsqrt(jnp.mean(m * m, axis=-1, keepdims=True) + RMS_EPS)
        mhat = m * r
        mn = (mhat * g_ref[...]).astype(BF16)
        dkv = jnp.concatenate([dmk_ref[...], dmv_ref[...]], axis=1).astype(BF16)
        gw_ref[...] = _dot_tn(mn, dkv)
        dmn = _dot_nt(dkv, w_ref[...])
        gn_ref[...] = jnp.sum(dmn * mhat, axis=0, keepdims=True)

    return pl.pallas_call(
        body, name="mem_kv_bwd",
        out_shape=[jax.ShapeDtypeStruct((D_MODEL, 2 * C_W), F32), jax.ShapeDtypeStruct((1, D_MODEL), F32)],
    )(mem, mem_norm, w_mkv, dmk, dmv)


def _pre_norm(x, pre_norm, host=None):
    seq = x.shape[0]
    tm = min(512, seq)
    n_host_in = len(host["ins"]) if host else 0
    n_host_out = len(host["outs"]) if host else 0

    def body(x_ref, g_ref, *refs):
        host_in, (u_ref, ut_ref), refs = refs[:n_host_in], refs[n_host_in:n_host_in + 2], refs[n_host_in + 2:]
        host_out, sems = refs[:n_host_out], refs[n_host_out:]
        if host:
            @pl.when(pl.program_id(0) == 0)
            def _():
                host["start"](host_in, host_out, *sems)

        xv = x_ref[...]
        r = lax.rsqrt(jnp.mean(xv * xv, axis=-1, keepdims=True) + RMS_EPS)
        u = xv * r * g_ref[...]
        u_ref[...] = u.astype(BF16)
        ut_ref[...] = u.T.astype(BF16)
        if host:
            @pl.when(pl.program_id(0) == seq // tm - 1)
            def _():
                host["finish"](host_in, host_out, *sems)

    any_spec = pl.BlockSpec(memory_space=pl.ANY)
    ins = [x, pre_norm]
    in_specs = [pl.BlockSpec((tm, D_MODEL), lambda i: (i, 0)), pl.BlockSpec(pre_norm.shape, lambda i: (0, 0))]
    out_shape = [jax.ShapeDtypeStruct((seq, D_MODEL), BF16), jax.ShapeDtypeStruct((D_MODEL, seq), BF16)]
    out_specs = [pl.BlockSpec((tm, D_MODEL), lambda i: (i, 0)), pl.BlockSpec((D_MODEL, tm), lambda i: (0, i))]
    aliases, scratch = {}, []
    if host:
        aliases = {len(ins) + k: 2 + v for k, v in host.get("aliases", {}).items()}
        ins += list(host["ins"])
        in_specs += [any_spec] * n_host_in
        out_shape += list(host["outs"])
        out_specs += [any_spec] * n_host_out
        scratch = list(host["sems"])
    res = pl.pallas_call(
        body, name="pre_norm", grid=(seq // tm,), in_specs=in_specs, out_specs=out_specs, out_shape=out_shape,
        input_output_aliases=aliases, scratch_shapes=scratch,
        compiler_params=pltpu.CompilerParams(dimension_semantics=("arbitrary",)),
    )(*ins)
    return res[0], res[1], res[2:]


def _pre_proj(u, w_in_g, host=None):
    seq = u.shape[0]
    tm = min(512, seq)
    n_nat, n_dil = len(_NATURAL), len(_DILATED) * len(B_DILS)
    rope = _rope_tables(seq, tm)

    n_host_in = len(host["ins"]) if host else 0
    n_host_out = len(host["outs"]) if host else 0
    n_own_out = n_nat + n_dil

    def body(u_ref, w_ref, rl_ref, rb_ref, *refs):
        host_in, refs = refs[:n_host_in], refs[n_host_in:]
        nat = dict(zip(_NATURAL, refs[:n_nat]))
        res = {n: refs[n_nat + len(B_DILS) * k:n_nat + len(B_DILS) * (k + 1)] for k, n in enumerate(_DILATED)}
        host_out = refs[n_own_out:n_own_out + n_host_out]
        bufs = dict(zip(_DILATED, refs[n_own_out + n_host_out:]))
        sems = refs[n_own_out + n_host_out + len(_DILATED):]
        if host:
            @pl.when(pl.program_id(0) == 0)
            def _():
                host["start"](host_in, host_out, *sems)

        ub = u_ref[...]
        c, sm, sp = _rope_coeffs(rl_ref, rb_ref)
        for j in range(N_CHIPS):
            pj = _dot(ub, w_ref[j])
            for b in range(SHARD_IN // LANES):
                name, off, roped, scaled = _PROJ_LAYOUT[(SHARD_IN // LANES) * j + b]
                piece = pj[:, LANES * b:LANES * (b + 1)]
                if roped:
                    piece = _rope(piece, c, sm, sp)
                if scaled:
                    piece = piece * SCALE
                if name in bufs:
                    bufs[name][off // LANES] = piece
                else:
                    nat[name][:, off:off + LANES] = piece.astype(BF16)
        for name in _DILATED:
            for ref, dil in zip(res[name], B_DILS):
                _to_residues(bufs[name], ref, dil)
        if host:
            @pl.when(pl.program_id(0) == seq // tm - 1)
            def _():
                host["finish"](host_in, host_out, *sems)

    row = lambda w: pl.BlockSpec((tm, w), lambda i: (i, 0))
    full = lambda a: pl.BlockSpec(a.shape, lambda i: (0,) * a.ndim)
    any_spec = pl.BlockSpec(memory_space=pl.ANY)
    out_shape = [jax.ShapeDtypeStruct((seq, _PROJ_WIDTH[n]), BF16) for n in _NATURAL]
    out_specs = [row(_PROJ_WIDTH[n]) for n in _NATURAL]
    for n in _DILATED:
        for dil in B_DILS:
            out_shape.append(jax.ShapeDtypeStruct((dil, seq // dil, B_W), BF16))
            out_specs.append(_residue_spec(dil, tm, B_W))
    ins = [u, w_in_g, *rope]
    in_specs = [row(D_MODEL), full(w_in_g), full(rope[0]), pl.BlockSpec((8, 2 * LANES), lambda i: (i, 0))]
    scratch = [_stage(tm, B_W)] * len(_DILATED)
    aliases = {}
    if host:
        aliases = {len(ins) + k: n_own_out + v for k, v in host.get("aliases", {}).items()}
        ins += list(host["ins"])
        in_specs += [any_spec] * n_host_in
        out_shape += list(host["outs"])
        out_specs += [any_spec] * n_host_out
        scratch += list(host["sems"])
    res = pl.pallas_call(
        body, name="pre_proj", grid=(seq // tm,), in_specs=in_specs, out_specs=out_specs, out_shape=out_shape,
        input_output_aliases=aliases, scratch_shapes=scratch,
        compiler_params=pltpu.CompilerParams(dimension_semantics=("arbitrary",)),
    )(*ins)
    out = dict(zip(_NATURAL, res[:n_nat]))
    for k, n in enumerate(_DILATED):
        out[n] = res[n_nat + len(B_DILS) * k:n_nat + len(B_DILS) * (k + 1)]
    out["hosted"] = res[n_own_out:]
    return out


def _band_bias(max_dist, transposed):
    i = np.arange(BLOCK)[:, None]
    j = np.arange(BLOCK)[None, :]
    if transposed:
        same = i <= j
        other = (j + BLOCK - i) <= max_dist
        vis = np.concatenate([same, other], axis=1)
    else:
        prev = (i + BLOCK - j) <= max_dist
        same = j <= i
        vis = np.concatenate([prev, same], axis=1)
    return jnp.asarray(np.where(vis, 0.0, NEG).astype(np.float32))


def _kv_place(h, gqa):
    return (0, h // 3) if gqa else (h // 2, h % 2)


def _band_fwd(q, k, v, sink, *, max_dist, name):
    dil, length, wq = q.shape
    wk = k.shape[2]
    gqa = wk != wq
    tq = min(2048, length)
    ns, nt = tq // BLOCK, length // tq
    npair = wq // LANES
    bias = _band_bias(max_dist, transposed=False)
    has_sink = sink is not None

    def body(*refs):
        if has_sink:
            sink_ref, refs = refs[0], refs[1:]
        q_ref, k_ref, kp_ref, v_ref, vp_ref, bias_ref, o_ref, lse_ref, kbuf, vbuf = refs[:10]
        i = pl.program_id(1)
        kbuf[0:BLOCK] = kp_ref[...]
        kbuf[BLOCK:] = k_ref[...]
        vbuf[0:BLOCK] = vp_ref[...]
        vbuf[BLOCK:] = v_ref[...]
        if gqa:
            kroll, vroll = refs[10:12]
            kroll[...] = pltpu.roll(kbuf[...], HEAD_DIM, 1)
            vroll[...] = pltpu.roll(vbuf[...], HEAD_DIM, 1)
        half = _half_masks(BLOCK)
        col_prev = (lax.broadcasted_iota(jnp.int32, (1, 2 * BLOCK), 1) < BLOCK).astype(F32)

        def sub(a, carry):
            r0 = pl.multiple_of(a * BLOCK, BLOCK)
            pen = jnp.where((i == 0) & (a == 0), NEG, 0.0)
            b = bias_ref[...] + pen * col_prev
            scores = []
            for p in range(npair):
                qp = q_ref[pl.ds(r0, BLOCK), p * LANES:(p + 1) * LANES]
                for e in range(2):
                    pk, ek = _kv_place(2 * p + e, gqa)
                    kw = (kbuf if ek == e else kroll)[pl.ds(r0, 2 * BLOCK), pk * LANES:(pk + 1) * LANES]
                    scores.append(_dot_nt(jnp.where(half[e], qp, jnp.zeros_like(qp)), kw))
            m_cols, l_cols, probs = [], [], []
            for h, s in enumerate(scores):
                s = s + b
                m = jnp.max(s, axis=1, keepdims=True)
                if has_sink:
                    m = jnp.maximum(m, sink_ref[h])
                pe = jnp.exp(s - m)
                l = jnp.sum(pe, axis=1, keepdims=True)
                if has_sink:
                    l = l + jnp.exp(sink_ref[h] - m)
                probs.append(pe.astype(BF16))
                m_cols.append(m)
                l_cols.append(l)
            for p in range(npair):
                o_h = []
                for e in range(2):
                    h = 2 * p + e
                    pk, ek = _kv_place(h, gqa)
                    vw = (vbuf if ek == e else vroll)[pl.ds(r0, 2 * BLOCK), pk * LANES:(pk + 1) * LANES]
                    o_h.append(_dot(probs[h], vw) * (1.0 / l_cols[h]))
                o_ref[pl.ds(r0, BLOCK), p * LANES:(p + 1) * LANES] = jnp.where(half[0], o_h[0], o_h[1]).astype(BF16)
            lse_ref[pl.ds(r0, BLOCK), :] = _per_head(m_cols) + jnp.log(_per_head(l_cols, 1.0))
            return carry

        lax.fori_loop(0, ns, sub, 0, unroll=True)

    main = lambda w: pl.BlockSpec((None, tq, w), lambda r, i: (r, i, 0))
    prev = lambda w: pl.BlockSpec((None, BLOCK, w), lambda r, i: (r, jnp.maximum(i * ns - 1, 0), 0))
    in_specs = [main(wq), main(wk), prev(wk), main(wk), prev(wk), pl.BlockSpec(bias.shape, lambda r, i: (0, 0))]
    args = [q, k, k, v, v, bias]
    if has_sink:
        in_specs = [pl.BlockSpec(memory_space=pltpu.SMEM)] + in_specs
        args = [sink] + args
    scratch = [pltpu.VMEM((tq + BLOCK, wk), BF16)] * (4 if gqa else 2)
    return pl.pallas_call(
        body, name=name, grid=(dil, nt), in_specs=in_specs,
        out_specs=[main(wq), main(LANES)],
        out_shape=[jax.ShapeDtypeStruct((dil, length, wq), BF16), jax.ShapeDtypeStruct((dil, length, LANES), F32)],
        scratch_shapes=scratch,
    )(*args)


def _band_bwd(q, k, v, do, lse, delta, *, max_dist, name):
    dil, length, wq = q.shape
    wk = k.shape[2]
    gqa = wk != wq
    tq = min(2048, length)
    ns, nt = tq // BLOCK, length // tq
    npair = wq // LANES
    nblocks = length // BLOCK
    bias = _band_bias(max_dist, transposed=True)

    def body(q_ref, qn_ref, do_ref, don_ref, lse_ref, lsen_ref, dl_ref, dln_ref, k_ref, v_ref, bias_ref,
             dq_ref, dk_ref, dv_ref, stat_l, stat_d, dqt, kt, *rolled):
        i = pl.program_id(1)
        for pk in range(wk // LANES):
            kt[pk] = k_ref[:, pk * LANES:(pk + 1) * LANES].astype(F32).T.astype(BF16)
        if gqa:
            kroll, vroll, ktroll = rolled
            kroll[...] = pltpu.roll(k_ref[...], HEAD_DIM, 1)
            vroll[...] = pltpu.roll(v_ref[...], HEAD_DIM, 1)
            ktroll[0] = kroll[...].astype(F32).T.astype(BF16)
        for a in range(ns):
            rows = slice(a * BLOCK, (a + 1) * BLOCK)
            stat_l[a] = _rows_to_lanes(lse_ref[rows, :])
            stat_d[a] = _rows_to_lanes(dl_ref[rows, :])
        stat_l[ns] = _rows_to_lanes(lsen_ref[...])
        stat_d[ns] = _rows_to_lanes(dln_ref[...])

        @pl.when(i == 0)
        def _():
            dqt[:, :, 0:BLOCK] = jnp.zeros((npair, LANES, BLOCK), F32)

        @pl.when(i > 0)
        def _():
            dqt[:, :, 0:BLOCK] = dqt[:, :, tq:tq + BLOCK]

        dqt[:, :, BLOCK:] = jnp.zeros((npair, LANES, tq), F32)
        half2 = _half_masks(2 * BLOCK)
        row = lax.broadcasted_iota(jnp.int32, (LANES, BLOCK), 0)
        row_half = (row < HEAD_DIM, row >= HEAD_DIM)
        col_next = (lax.broadcasted_iota(jnp.int32, (1, 2 * BLOCK), 1) >= BLOCK).astype(F32)

        for b in range(ns):
            rows = slice(b * BLOCK, (b + 1) * BLOCK)
            window = slice(b * BLOCK, (b + 2) * BLOCK)
            bt = bias_ref[...]
            if b == ns - 1:
                bt = bt + jnp.where(i == nt - 1, NEG, 0.0) * col_next
            acc = {}
            items = []
            nxt_rows = slice((b + 1) * BLOCK, (b + 2) * BLOCK)
            for p in range(npair):
                lanes = slice(p * LANES, (p + 1) * LANES)
                q_next = q_ref[nxt_rows, lanes] if b + 1 < ns else qn_ref[:, lanes]
                do_next = do_ref[nxt_rows, lanes] if b + 1 < ns else don_ref[:, lanes]
                qw = jnp.concatenate([q_ref[rows, lanes], q_next], axis=0)
                dow = jnp.concatenate([do_ref[rows, lanes], do_next], axis=0)
                for e in range(2):
                    h = 2 * p + e
                    pk, ek = _kv_place(h, gqa)
                    klanes = slice(pk * LANES, (pk + 1) * LANES)
                    kb = (k_ref if ek == e else kroll)[rows, klanes]
                    vb = (v_ref if ek == e else vroll)[rows, klanes]
                    qm = jnp.where(half2[e], qw, jnp.zeros_like(qw))
                    dom = jnp.where(half2[e], dow, jnp.zeros_like(dow))
                    items.append(dict(p=p, e=e, h=h, pk=pk, ek=ek, qm=qm, dom=dom,
                                      st=_dot_nt(kb, qm), dpt=_dot_nt(vb, dom)))
            for it in items:
                h = it["h"]
                lrow = jnp.concatenate([stat_l[b, h:h + 1, :], stat_l[b + 1, h:h + 1, :]], axis=1)
                drow = jnp.concatenate([stat_d[b, h:h + 1, :], stat_d[b + 1, h:h + 1, :]], axis=1)
                pt = jnp.exp(it["st"] + bt - lrow)
                it["ptb"] = pt.astype(BF16)
                it["dsb"] = (pt * (it["dpt"] - drow)).astype(BF16)
            for p in range(npair):
                pair = items[2 * p:2 * p + 2]
                lanes = slice(p * LANES, (p + 1) * LANES)
                kparts = []
                for it in pair:
                    kbt = (kt if it["ek"] == it["e"] else ktroll)[it["pk"], :, rows]
                    kparts.append(jnp.where(row_half[it["e"]], kbt, jnp.zeros_like(kbt)))
                ds_keys = jnp.concatenate([it["dsb"] for it in pair], axis=0)
                dqt[p, :, window] += _dot(jnp.concatenate(kparts, axis=1), ds_keys)
                if not gqa:
                    q_both = jnp.concatenate([it["qm"] for it in pair], axis=0)
                    do_both = jnp.concatenate([it["dom"] for it in pair], axis=0)
                    dk_ref[rows, lanes] = _dot(jnp.concatenate([it["dsb"] for it in pair], axis=1), q_both).astype(BF16)
                    dv_ref[rows, lanes] = _dot(jnp.concatenate([it["ptb"] for it in pair], axis=1), do_both).astype(BF16)
                else:
                    for it in pair:
                        dv_c = _dot(it["ptb"], it["dom"])
                        dk_c = _dot(it["dsb"], it["qm"])
                        key = (it["pk"], it["ek"] == it["e"])
                        if key in acc:
                            acc[key] = (acc[key][0] + dk_c, acc[key][1] + dv_c)
                        else:
                            acc[key] = (dk_c, dv_c)
            if gqa:
                dk_al, dv_al = acc[(0, True)]
                dk_mis, dv_mis = acc[(0, False)]
                dk_ref[rows, :] = (dk_al + pltpu.roll(dk_mis, HEAD_DIM, 1)).astype(BF16)
                dv_ref[rows, :] = (dv_al + pltpu.roll(dv_mis, HEAD_DIM, 1)).astype(BF16)

        for p in range(npair):
            dq_ref[:, p * LANES:(p + 1) * LANES] = dqt[p, :, 0:tq].T.astype(BF16)

    main = lambda w: pl.BlockSpec((None, tq, w), lambda r, i: (r, i, 0))
    nxt = lambda w: pl.BlockSpec((None, BLOCK, w), lambda r, i: (r, jnp.minimum((i + 1) * ns, nblocks - 1), 0))
    scratch = [pltpu.VMEM((ns + 1, 8, LANES), F32), pltpu.VMEM((ns + 1, 8, LANES), F32),
               pltpu.VMEM((npair, LANES, tq + BLOCK), F32), pltpu.VMEM((wk // LANES, LANES, tq), BF16)]
    if gqa:
        scratch = scratch + [pltpu.VMEM((tq, wk), BF16)] * 2 + [pltpu.VMEM((1, LANES, tq), BF16)]
    return pl.pallas_call(
        body, name=name, grid=(dil, nt),
        in_specs=[main(wq), nxt(wq), main(wq), nxt(wq), main(LANES), nxt(LANES), main(LANES), nxt(LANES),
                  main(wk), main(wk), pl.BlockSpec(bias.shape, lambda r, i: (0, 0))],
        out_specs=[main(wq), main(wk), main(wk)],
        out_shape=[jax.ShapeDtypeStruct((dil, length, wq), BF16), jax.ShapeDtypeStruct((dil, length, wk), BF16),
                   jax.ShapeDtypeStruct((dil, length, wk), BF16)],
        scratch_shapes=scratch,
        compiler_params=pltpu.CompilerParams(dimension_semantics=("arbitrary", "arbitrary")),
    )(q, q, do, do, lse, lse, delta, delta, k, v, bias)


def _mem_attn_fwd(q, mk, mv):
    seq = q.shape[0]
    tq = min(1024, seq)
    ns = tq // BLOCK

    def body(q_ref, mk_ref, mv_ref, o_ref, lse_ref):
        half = _half_masks(BLOCK)

        def sub(a, carry):
            r0 = pl.multiple_of(a * BLOCK, BLOCK)
            scores = []
            for p in range(C_W // LANES):
                lanes = slice(p * LANES, (p + 1) * LANES)
                qp = q_ref[pl.ds(r0, BLOCK), lanes]
                for e in range(2):
                    scores.append(_dot_nt(jnp.where(half[e], qp, jnp.zeros_like(qp)), mk_ref[:, lanes]))
            m_cols, l_cols, probs = [], [], []
            for s in scores:
                m = jnp.max(s, axis=1, keepdims=True)
                pe = jnp.exp(s - m)
                probs.append(pe.astype(BF16))
                m_cols.append(m)
                l_cols.append(jnp.sum(pe, axis=1, keepdims=True))
            for p in range(C_W // LANES):
                lanes = slice(p * LANES, (p + 1) * LANES)
                o_h = [_dot(probs[2 * p + e], mv_ref[:, lanes]) * (1.0 / l_cols[2 * p + e]) for e in range(2)]
                o_ref[pl.ds(r0, BLOCK), lanes] = jnp.where(half[0], o_h[0], o_h[1]).astype(BF16)
            lse_ref[pl.ds(r0, BLOCK), :] = _per_head(m_cols) + jnp.log(_per_head(l_cols, 1.0))
            return carry

        lax.fori_loop(0, ns, sub, 0, unroll=True)

    row = lambda w: pl.BlockSpec((tq, w), lambda i: (i, 0))
    full = pl.BlockSpec((N_MEM, C_W), lambda i: (0, 0))
    return pl.pallas_call(
        body, name="mem_attn_fwd", grid=(seq // tq,), in_specs=[row(C_W), full, full],
        out_specs=[row(C_W), row(LANES)],
        out_shape=[jax.ShapeDtypeStruct((seq, C_W), BF16), jax.ShapeDtypeStruct((seq, LANES), F32)],
    )(q, mk, mv)


def _mem_attn_bwd(q, mk, mv, do, lse, delta):
    seq = q.shape[0]
    tq = min(1024, seq)
    ns = tq // BLOCK
    npair = C_W // LANES

    def body(q_ref, mk_ref, mv_ref, do_ref, lse_ref, dl_ref, dq_ref, dmk_ref, dmv_ref, stat_l, stat_d, mkt, dqt):
        @pl.when(pl.program_id(0) == 0)
        def _():
            dmk_ref[...] = jnp.zeros_like(dmk_ref)
            dmv_ref[...] = jnp.zeros_like(dmv_ref)
            for p in range(npair):
                mkt[p] = mk_ref[:, p * LANES:(p + 1) * LANES].astype(F32).T.astype(BF16)

        for a in range(ns):
            rows = slice(a * BLOCK, (a + 1) * BLOCK)
            stat_l[a] = _rows_to_lanes(lse_ref[rows, :])
            stat_d[a] = _rows_to_lanes(dl_ref[rows, :])
        half = _half_masks(BLOCK)
        row = lax.broadcasted_iota(jnp.int32, (LANES, N_MEM), 0)
        row_half = (row < HEAD_DIM, row >= HEAD_DIM)

        for a in range(ns):
            rows = slice(a * BLOCK, (a + 1) * BLOCK)
            items = []
            for p in range(npair):
                lanes = slice(p * LANES, (p + 1) * LANES)
                qp = q_ref[rows, lanes]
                dop = do_ref[rows, lanes]
                for e in range(2):
                    qm = jnp.where(half[e], qp, jnp.zeros_like(qp))
                    dom = jnp.where(half[e], dop, jnp.zeros_like(dop))
                    items.append(dict(p=p, e=e, qm=qm, dom=dom, st=_dot_nt(mk_ref[:, lanes], qm),
                                      dpt=_dot_nt(mv_ref[:, lanes], dom)))
            for it in items:
                h = 2 * it["p"] + it["e"]
                pt = jnp.exp(it["st"] - stat_l[a, h:h + 1, :])
                it["ptb"] = pt.astype(BF16)
                it["dsb"] = (pt * (it["dpt"] - stat_d[a, h:h + 1, :])).astype(BF16)
            for p in range(npair):
                lanes = slice(p * LANES, (p + 1) * LANES)
                pair = [it for it in items if it["p"] == p]
                join = lambda name, axis: jnp.concatenate([it[name] for it in pair], axis=axis)
                dmv_ref[:, lanes] += _dot(join("ptb", 1), join("dom", 0))
                dmk_ref[:, lanes] += _dot(join("dsb", 1), join("qm", 0))
                kbt = mkt[p]
                k_both = jnp.concatenate([jnp.where(row_half[e], kbt, jnp.zeros_like(kbt)) for e in range(2)], axis=1)
                dqt[p, :, rows] = _dot(k_both, join("dsb", 0))
        for p in range(npair):
            dq_ref[:, p * LANES:(p + 1) * LANES] = dqt[p].T.astype(BF16)

    row = lambda w: pl.BlockSpec((tq, w), lambda i: (i, 0))
    full = pl.BlockSpec((N_MEM, C_W), lambda i: (0, 0))
    return pl.pallas_call(
        body, name="mem_attn_bwd", grid=(seq // tq,),
        in_specs=[row(C_W), full, full, row(C_W), row(LANES), row(LANES)], out_specs=[row(C_W), full, full],
        out_shape=[jax.ShapeDtypeStruct((seq, C_W), BF16), jax.ShapeDtypeStruct((N_MEM, C_W), F32),
                   jax.ShapeDtypeStruct((N_MEM, C_W), F32)],
        scratch_shapes=[pltpu.VMEM((ns, 8, LANES), F32)] * 2
        + [pltpu.VMEM((npair, LANES, N_MEM), BF16), pltpu.VMEM((npair, LANES, tq), F32)],
        compiler_params=pltpu.CompilerParams(dimension_semantics=("arbitrary",)),
    )(q, mk, mv, do, lse, delta)


def _silu_and_grad(g):
    s = 1.0 / (1.0 + jnp.exp(-g))
    return g * s, s * (1.0 + g * (1.0 - s))


def _post(x, target, post_norm, w_out, sink_row, oa, lse_a, ga, ob_list, lseb_list, gb, oc, gc):
    seq = x.shape[0]
    tm = min(512, seq)
    inv_d = 1.0 / D_MODEL
    nd = len(B_DILS)

    def body(*refs):
        (x_ref, t_ref, gp_ref, w_ref, sink_ref, oa_ref, lsea_ref, ga_ref), refs = refs[:8], refs[8:]
        ob_refs, lb_refs, (gb_ref, oc_ref, gc_ref), refs = refs[:nd], refs[nd:2 * nd], refs[2 * nd:2 * nd + 3], refs[2 * nd + 3:]
        (g_ref, doa_ref, dla_ref, dga_ref), refs = refs[:4], refs[4:]
        dob_refs, lsec_refs, dlb_refs, refs = refs[:nd], refs[nd:2 * nd], refs[2 * nd:3 * nd], refs[3 * nd:]
        (dgb_ref, doc_ref, dlc_ref, dgc_ref, gw_ref, gpost_ref, gsink_ref, loss_ref), refs = refs[:8], refs[8:]
        ycat, obufs, lbufs, st_do, st_l, st_d = refs[0], refs[1:nd], refs[nd:2 * nd - 1], refs[2 * nd - 1], refs[2 * nd], refs[2 * nd + 1]

        @pl.when(pl.program_id(0) == 0)
        def _():
            gw_ref[...] = jnp.zeros_like(gw_ref)
            gpost_ref[...] = jnp.zeros_like(gpost_ref)
            gsink_ref[...] = jnp.zeros_like(gsink_ref)
            loss_ref[...] = jnp.zeros_like(loss_ref)

        o_i, l_i = [ob_refs[0][0].astype(F32)], [lb_refs[0][0]]
        for k in range(1, nd):
            _from_residues(ob_refs[k], obufs[k - 1], B_DILS[k])
            _from_residues(lb_refs[k], lbufs[k - 1], B_DILS[k])
            o_i.append(_stage_read(obufs[k - 1]))
            l_i.append(_stage_read(lbufs[k - 1]))
        mx = l_i[0]
        for l in l_i[1:]:
            mx = jnp.maximum(mx, l)
        w_i = [jnp.exp(l - mx) for l in l_i]
        z = w_i[0]
        for w in w_i[1:]:
            z = z + w
        _stage_write(st_l, mx + jnp.log(z))
        expand = _head_expand_matrix(B_W)
        inv_z = 1.0 / z
        ob = None
        for w, o in zip(w_i, o_i):
            term = _dot_split(w * inv_z, expand, 2) * o
            ob = term if ob is None else ob + term
        oa, oc = oa_ref[...].astype(F32), oc_ref[...].astype(F32)
        sa, dsa = _silu_and_grad(ga_ref[...].astype(F32))
        sb, dsb = _silu_and_grad(gb_ref[...].astype(F32))
        sc, dsc = _silu_and_grad(gc_ref[...].astype(F32))
        ycat[:, 0:A_W] = (oa * sa).astype(BF16)
        ycat[:, A_W:A_W + B_W] = (ob * sb).astype(BF16)
        ycat[:, A_W + B_W:] = (oc * sc).astype(BF16)
        y2 = _dot(ycat[...], w_ref[...])
        r = lax.rsqrt(jnp.mean(y2 * y2, axis=-1, keepdims=True) + RMS_EPS)
        zhat = y2 * r
        gp = gp_ref[...]
        err = x_ref[...] + zhat * gp - t_ref[...]
        loss_ref[...] += jnp.sum(err * err) * (0.5 * inv_d)
        g = err * inv_d
        g_ref[...] = g
        gpost_ref[...] += jnp.sum(g * zhat, axis=0, keepdims=True)
        a = g * gp
        dy2 = (r * (a - zhat * jnp.mean(a * zhat, axis=-1, keepdims=True))).astype(BF16)
        gw_ref[...] += _dot_tn(ycat[...], dy2)
        dycat = _dot_nt(dy2, w_ref[...])
        dya, dyb, dyc = dycat[:, 0:A_W], dycat[:, A_W:A_W + B_W], dycat[:, A_W + B_W:]
        doa, dob, doc = dya * sa, dyb * sb, dyc * sc
        doa_ref[...] = doa.astype(BF16)
        doc_ref[...] = doc.astype(BF16)
        dga_ref[...] = (dya * oa * dsa).astype(BF16)
        dgb_ref[...] = (dyb * ob * dsb).astype(BF16)
        dgc_ref[...] = (dyc * oc * dsc).astype(BF16)
        dl_a = _dot_split(doa * oa, _head_sum_matrix(A_W), 2)
        dla_ref[...] = dl_a
        dlc_ref[...] = _dot_split(doc * oc, _head_sum_matrix(C_W), 2)
        gsink_ref[...] += jnp.sum(jnp.exp(sink_ref[...] - lsea_ref[...]) * dl_a, axis=0, keepdims=True)
        _stage_write(st_do, dob)
        _stage_write(st_d, _dot_split(dob * ob, _head_sum_matrix(B_W), 2))
        for k, dil in enumerate(B_DILS):
            _to_residues(st_do, dob_refs[k], dil)
            _to_residues(st_l, lsec_refs[k], dil)
            _to_residues(st_d, dlb_refs[k], dil)

    row = lambda w: pl.BlockSpec((tm, w), lambda i: (i, 0))
    full = lambda shape: pl.BlockSpec(shape, lambda i: (0,) * len(shape))
    res_specs = lambda w: [_residue_spec(d, tm, w) for d in B_DILS]
    res_shapes = lambda w, dt: [jax.ShapeDtypeStruct((d, seq // d, w), dt) for d in B_DILS]
    ins = [x, target, post_norm, w_out, sink_row, oa, lse_a, ga, *ob_list, *lseb_list, gb, oc, gc]
    in_specs = ([row(D_MODEL), row(D_MODEL), full((1, D_MODEL)), full((D_MODEL, D_MODEL)), full((1, LANES)),
                 row(A_W), row(LANES), row(A_W)] + res_specs(B_W) + res_specs(LANES) + [row(B_W), row(C_W), row(C_W)])
    out_shape = ([jax.ShapeDtypeStruct((seq, D_MODEL), F32), jax.ShapeDtypeStruct((seq, A_W), BF16),
                  jax.ShapeDtypeStruct((seq, LANES), F32), jax.ShapeDtypeStruct((seq, A_W), BF16)]
                 + res_shapes(B_W, BF16) + res_shapes(LANES, F32) + res_shapes(LANES, F32)
                 + [jax.ShapeDtypeStruct((seq, B_W), BF16), jax.ShapeDtypeStruct((seq, C_W), BF16),
                    jax.ShapeDtypeStruct((seq, LANES), F32), jax.ShapeDtypeStruct((seq, C_W), BF16),
                    jax.ShapeDtypeStruct((D_MODEL, D_MODEL), F32), jax.ShapeDtypeStruct((1, D_MODEL), F32),
                    jax.ShapeDtypeStruct((1, LANES), F32), jax.ShapeDtypeStruct((1, LANES), F32)])
    out_specs = ([row(D_MODEL), row(A_W), row(LANES), row(A_W)] + res_specs(B_W) + res_specs(LANES) + res_specs(LANES)
                 + [row(B_W), row(C_W), row(LANES), row(C_W),
                    full((D_MODEL, D_MODEL)), full((1, D_MODEL)), full((1, LANES)), full((1, LANES))])
    scratch = ([pltpu.VMEM((tm, D_MODEL), BF16)] + [_stage(tm, B_W)] * (nd - 1) + [_stage(tm, LANES)] * (nd - 1)
               + [_stage(tm, B_W), _stage(tm, LANES), _stage(tm, LANES)])
    res = pl.pallas_call(
        body, name="post", grid=(seq // tm,), in_specs=in_specs, out_specs=out_specs, out_shape=out_shape,
        scratch_shapes=scratch,
        compiler_params=pltpu.CompilerParams(dimension_semantics=("arbitrary",)),
    )(*ins)
    out = dict(g=res[0], doa=res[1], dl_a=res[2], dga=res[3], dob=res[4:4 + nd], lse_b=res[4 + nd:4 + 2 * nd],
               dl_b=res[4 + 2 * nd:4 + 3 * nd])
    rest = res[4 + 3 * nd:]
    out.update(dgb=rest[0], doc=rest[1], dl_c=rest[2], dgc=rest[3], gw_out=rest[4], gpost=rest[5], gsink=rest[6],
               loss=rest[7])
    return out


def _grad_w_in(ut, nat, res):
    seq = ut.shape[1]
    tm = min(512, seq)
    nd = len(B_DILS)
    nat_list = [nat[n] for n in _NATURAL]
    res_list = [a for n in _DILATED for a in res[n]]
    rope = _rope_tables(seq, tm)

    def body(rl_ref, rb_ref, ut_ref, *refs):
        nat_refs = dict(zip(_NATURAL, refs[:len(_NATURAL)]))
        refs = refs[len(_NATURAL):]
        res_refs = {n: refs[nd * k:nd * (k + 1)] for k, n in enumerate(_DILATED)}
        refs = refs[nd * len(_DILATED):]
        dproj_ref, gw_ref = refs[:2]
        bufs = {n: refs[2 + (nd - 1) * k:2 + (nd - 1) * (k + 1)] for k, n in enumerate(_DILATED)}

        @pl.when(pl.program_id(0) == 0)
        def _():
            gw_ref[...] = jnp.zeros_like(gw_ref)

        for n in _DILATED:
            for k in range(1, nd):
                _from_residues(res_refs[n][k], bufs[n][k - 1], B_DILS[k])
        c, sm, sp = _rope_coeffs(rl_ref, rb_ref)
        sm, sp = -sm, -sp
        for blk, (name, off, roped, scaled) in enumerate(_PROJ_LAYOUT):
            lanes = slice(off, off + LANES)
            if name in nat_refs:
                piece = nat_refs[name][:, lanes].astype(F32)
            else:
                piece = res_refs[name][0][0, :, lanes].astype(F32)
                for buf in bufs[name]:
                    piece = piece + buf[off // LANES]
            if roped:
                piece = _rope(piece, c, sm, sp)
            if scaled:
                piece = piece * SCALE
            dproj_ref[:, blk * LANES:(blk + 1) * LANES] = piece.astype(BF16)
        for j in range(N_CHIPS):
            gw_ref[j] += _dot(ut_ref[...], dproj_ref[:, j * SHARD_IN:(j + 1) * SHARD_IN])

    row = lambda w: pl.BlockSpec((tm, w), lambda i: (i, 0))
    in_specs = ([pl.BlockSpec(rope[0].shape, lambda i: (0, 0)), pl.BlockSpec((8, 2 * LANES), lambda i: (i, 0)),
                 pl.BlockSpec((D_MODEL, tm), lambda i: (0, i))]
                + [row(a.shape[1]) for a in nat_list]
                + [_residue_spec(d, tm, B_W) for _ in _DILATED for d in B_DILS])
    return pl.pallas_call(
        body, name="grad_w_in", grid=(seq // tm,), in_specs=in_specs,
        out_specs=[row(D_IN), pl.BlockSpec((N_CHIPS, D_MODEL, SHARD_IN), lambda i: (0, 0, 0))],
        out_shape=[jax.ShapeDtypeStruct((seq, D_IN), BF16), jax.ShapeDtypeStruct((N_CHIPS, D_MODEL, SHARD_IN), F32)],
        scratch_shapes=[_stage(tm, B_W)] * ((nd - 1) * len(_DILATED)),
        compiler_params=pltpu.CompilerParams(dimension_semantics=("arbitrary",)),
    )(*rope, ut, *nat_list, *res_list)


def _input_grad(x, g, pre_norm, w_in_g, dproj, gx_prev, span, after, name):
    seq = x.shape[0]
    tm = seq // 16
    first_block, steps = span

    def body(*refs):
        x_ref, g_ref, gp_ref, w_ref, dp_ref = refs[:5]
        gx_ref, gpre_ref = refs[-2:]

        @pl.when(pl.program_id(0) == 0)
        def _():
            gpre_ref[...] = jnp.zeros_like(gpre_ref)

        du = None
        for j in range(N_CHIPS):
            term = _dot_nt(dp_ref[:, j * SHARD_IN:(j + 1) * SHARD_IN], w_ref[j])
            du = term if du is None else du + term
        xv = x_ref[...]
        r = lax.rsqrt(jnp.mean(xv * xv, axis=-1, keepdims=True) + RMS_EPS)
        xhat = xv * r
        gpre_ref[...] += jnp.sum(du * xhat, axis=0, keepdims=True)
        a = du * gp_ref[...]
        gx_ref[...] = g_ref[...] + r * (a - xhat * jnp.mean(a * xhat, axis=-1, keepdims=True))

    row = lambda w: pl.BlockSpec((tm, w), lambda i: (first_block + i, 0))
    full = lambda a: pl.BlockSpec(a.shape, lambda i: (0,) * a.ndim)
    any_spec = pl.BlockSpec(memory_space=pl.ANY)
    ins = [x, g, pre_norm, w_in_g, dproj]
    in_specs = [row(D_MODEL), row(D_MODEL), full(pre_norm), full(w_in_g), row(D_IN)]
    aliases = {}
    if gx_prev is not None:
        aliases[len(ins)] = 0
        ins.append(gx_prev)
        in_specs.append(any_spec)
    if after is not None:
        ins.append(after)
        in_specs.append(any_spec)
    return pl.pallas_call(
        body, name=name, grid=(steps,), in_specs=in_specs,
        out_specs=[row(D_MODEL), pl.BlockSpec((1, D_MODEL), lambda i: (0, 0))],
        out_shape=[jax.ShapeDtypeStruct((seq, D_MODEL), F32), jax.ShapeDtypeStruct((1, D_MODEL), F32)],
        input_output_aliases=aliases,
        compiler_params=pltpu.CompilerParams(dimension_semantics=("arbitrary",)),
    )(*ins)


def _exchange_start(ex, name):
    n_in, n_out, n_sem = len(ex["ins"]), len(ex["outs"]), len(ex["sems"])

    def body(*refs):
        in_refs, land_refs, sems = refs[:n_in], refs[n_in:n_in + n_out], refs[n_in + n_out:n_in + n_out + n_sem]
        ex["start"](in_refs, land_refs, *sems)
        token = refs[-1]
        token[...] = jnp.zeros_like(token)

    hbm = pl.BlockSpec(memory_space=pltpu.HBM)
    sem = pl.BlockSpec(memory_space=pltpu.SEMAPHORE)
    ins = [pltpu.with_memory_space_constraint(a, pltpu.HBM) for a in ex["ins"]]
    landing = [pltpu.with_memory_space_constraint(lax.empty(o.shape, o.dtype), pltpu.HBM) for o in ex["outs"]]
    res = pl.pallas_call(
        body, name=name,
        out_shape=list(ex["sems"]) + [pltpu.HBM(a.shape, a.dtype) for a in ex["ins"]]
        + [pltpu.HBM(o.shape, o.dtype) for o in ex["outs"]] + [jax.ShapeDtypeStruct((8, LANES), F32)],
        in_specs=[hbm] * (n_in + n_out),
        out_specs=[sem] * n_sem + [hbm] * (n_in + n_out) + [pl.BlockSpec(memory_space=pltpu.VMEM)],
        input_output_aliases={k: n_sem + k for k in range(n_in + n_out)},
        compiler_params=pltpu.CompilerParams(has_side_effects=pltpu.SideEffectType.DATAFLOW_SIDE_EFFECTING),
    )(*ins, *landing)
    return res[:-1], res[-1]


def _exchange_wait(ex, handles, after, name):
    n_in, n_out, n_sem = len(ex["ins"]), len(ex["outs"]), len(ex["sems"])
    sems, thru = handles[:n_sem], handles[n_sem:]

    def body(*refs):
        in_refs, land_refs = refs[:n_in], refs[n_in:n_in + n_out]
        sem_refs = refs[n_in + n_out:n_in + n_out + n_sem]
        ex["finish"](in_refs, land_refs, *sem_refs)

    hbm = pl.BlockSpec(memory_space=pltpu.HBM)
    sem = pl.BlockSpec(memory_space=pltpu.SEMAPHORE)
    res = pl.pallas_call(
        body, name=name,
        out_shape=[pltpu.HBM(a.shape, a.dtype) for a in thru],
        in_specs=[hbm] * (n_in + n_out) + [sem] * n_sem + [pl.BlockSpec(memory_space=pl.ANY)],
        out_specs=[hbm] * (n_in + n_out),
        input_output_aliases={k: k for k in range(n_in + n_out)},
        compiler_params=pltpu.CompilerParams(has_side_effects=pltpu.SideEffectType.DATAFLOW_SIDE_EFFECTING),
    )(*thru, *sems, after)
    return res[:n_in], res[n_in:]


def _start_finish(build):
    def start(*refs):
        for cp in build(*refs):
            cp.start()

    def finish(*refs):
        for cp in build(*refs):
            cp.wait()

    return dict(start=start, finish=finish)


def _pair_exchange(grads):
    n = len(grads)

    def build(srcs, outs, send_sems, recv_sems):
        x, y, c = lax.axis_index("x"), lax.axis_index("y"), lax.axis_index("c")
        copies = []
        for t in range(n):
            rows = grads[t].shape[1] // 2
            copies.append(pltpu.make_async_remote_copy(
                src_ref=srcs[t].at[:, pl.ds((1 - c) * rows, rows)], dst_ref=outs[t],
                send_sem=send_sems.at[t], recv_sem=recv_sems.at[t], device_id=(x, y, 1 - c), device_id_type=MESH))
        return copies

    return dict(ins=list(grads), **_start_finish(build),
                outs=[jax.ShapeDtypeStruct((g.shape[0], g.shape[1] // 2, g.shape[2]), g.dtype) for g in grads],
                sems=[pltpu.SemaphoreType.DMA((n,)), pltpu.SemaphoreType.DMA((n,))])


def _pair_add(core, own, got):
    nchip, rows2, width = own.shape
    rows = rows2 // 2
    tr = min(512, rows)
    nb = rows // tr

    def body(core_ref, own_ref, got_ref, out_ref):
        out_ref[...] = (own_ref[...] + got_ref[...]).astype(BF16)

    grid_spec = pltpu.PrefetchScalarGridSpec(
        num_scalar_prefetch=1, grid=(nchip, nb),
        in_specs=[pl.BlockSpec((None, tr, width), lambda k, i, core_ref: (k, core_ref[0] * nb + i, 0)),
                  pl.BlockSpec((None, tr, width), lambda k, i, core_ref: (k, i, 0))],
        out_specs=pl.BlockSpec((None, tr, width), lambda k, i, core_ref: (k, i, 0)))
    return pl.pallas_call(
        body, name=f"pair_add_{width}", grid_spec=grid_spec,
        out_shape=jax.ShapeDtypeStruct((nchip, rows, width), BF16),
    )(core, own, got)


def _chip_exchange(parts):
    n = len(parts)

    def build(srcs, outs, send_sems, recv_sems, local_sems):
        x, y, c = lax.axis_index("x"), lax.axis_index("y"), lax.axis_index("c")
        my_chip = 2 * x + y
        chips = [(1 - x, y), (x, 1 - y), (1 - x, 1 - y)]
        copies = [pltpu.make_async_copy(srcs[t].at[my_chip], outs[t].at[my_chip], local_sems.at[t]) for t in range(n)]
        for j, (cx, cy) in enumerate(chips):
            for t in range(n):
                k = n * j + t
                copies.append(pltpu.make_async_remote_copy(
                    src_ref=srcs[t].at[2 * cx + cy], dst_ref=outs[t].at[my_chip], send_sem=send_sems.at[k],
                    recv_sem=recv_sems.at[k], device_id=(cx, cy, c), device_id_type=MESH))
        return copies

    return dict(ins=list(parts), **_start_finish(build), outs=[jax.ShapeDtypeStruct(p.shape, p.dtype) for p in parts],
                sems=[pltpu.SemaphoreType.DMA((3 * n,)), pltpu.SemaphoreType.DMA((3 * n,)),
                      pltpu.SemaphoreType.DMA((n,))])


def _slot_sum(slots, name, core=None):
    ns, rows, width = slots.shape
    tr = min(512, rows)

    def body(*refs):
        in_ref, out_ref = refs[-2:]
        acc = in_ref[0].astype(F32)
        for s in range(1, ns):
            acc = acc + in_ref[s].astype(F32)
        out_ref[...] = acc

    if core is None:
        return pl.pallas_call(
            body, name=name, grid=(rows // tr,),
            in_specs=[pl.BlockSpec((ns, tr, width), lambda i: (0, i, 0))],
            out_specs=pl.BlockSpec((tr, width), lambda i: (i, 0)),
            out_shape=jax.ShapeDtypeStruct((rows, width), F32),
        )(slots)
    grid_spec = pltpu.PrefetchScalarGridSpec(
        num_scalar_prefetch=1, grid=(rows // tr,),
        in_specs=[pl.BlockSpec((ns, tr, width), lambda i, core_ref: (0, i, 0))],
        out_specs=pl.BlockSpec((None, tr, width), lambda i, core_ref: (core_ref[0], i, 0)))
    return pl.pallas_call(
        body, name=name, grid_spec=grid_spec, out_shape=jax.ShapeDtypeStruct((2, rows, width), F32),
    )(core, slots)


def _pair_gather(bufs, small):
    n = len(bufs)

    def body(*refs):
        small_ref, outs, small_out = refs[n], refs[n + 1:2 * n + 1], refs[2 * n + 1]
        send_sems, recv_sems, local_sem = refs[2 * n + 2:]
        x, y, c = lax.axis_index("x"), lax.axis_index("y"), lax.axis_index("c")
        me = 4 * x + 2 * y + c
        chips = [(1 - x, y), (x, 1 - y), (1 - x, 1 - y)]
        mine = pltpu.make_async_copy(small_ref, small_out.at[me], local_sem)
        mine.start()
        copies = [pltpu.make_async_remote_copy(
            src_ref=outs[t].at[c], dst_ref=outs[t].at[c], send_sem=send_sems.at[t], recv_sem=recv_sems.at[t],
            device_id=(x, y, 1 - c), device_id_type=MESH) for t in range(n)]
        peers = [(x, y, 1 - c)] + [(cx, cy, cc) for (cx, cy) in chips for cc in (c, 1 - c)]
        for j, peer in enumerate(peers):
            copies.append(pltpu.make_async_remote_copy(
                src_ref=small_ref, dst_ref=small_out.at[me], send_sem=send_sems.at[n + j],
                recv_sem=recv_sems.at[n + j], device_id=peer, device_id_type=MESH))
        for cp in copies:
            cp.start()
        for cp in copies:
            cp.wait()
        mine.wait()

    any_spec = pl.BlockSpec(memory_space=pl.ANY)
    res = pl.pallas_call(
        body, name="pair_gather",
        out_shape=[jax.ShapeDtypeStruct(b.shape, b.dtype) for b in bufs]
        + [jax.ShapeDtypeStruct((8,) + small.shape, small.dtype)],
        in_specs=[any_spec] * (n + 1), out_specs=[any_spec] * (n + 1),
        input_output_aliases={t: t for t in range(n)},
        scratch_shapes=[pltpu.SemaphoreType.DMA((n + 7,)), pltpu.SemaphoreType.DMA((n + 7,)),
                        pltpu.SemaphoreType.DMA],
    )(*bufs, small)
    return [r.reshape(2 * b.shape[1], b.shape[2]) for r, b in zip(res[:n], bufs)], res[n]


def _adamw(w, g, m, v, name):
    rows, width = w.shape
    tr = min(256, rows)
    c1 = 1.0 / (1.0 - ADAM_B1 ** ADAM_STEP)
    c2 = 1.0 / (1.0 - ADAM_B2 ** ADAM_STEP)

    def body(w_ref, g_ref, m_ref, v_ref, d_ref, nm_ref, nv_ref):
        gv = g_ref[...]
        nm = ADAM_B1 * m_ref[...] + (1.0 - ADAM_B1) * gv
        nv = ADAM_B2 * v_ref[...] + (1.0 - ADAM_B2) * (gv * gv)
        nm_ref[...] = nm
        nv_ref[...] = nv
        d_ref[...] = -ADAM_LR * ((nm * c1) / (jnp.sqrt(nv * c2) + ADAM_EPS) + ADAM_WD * w_ref[...])

    spec = pl.BlockSpec((tr, width), lambda i: (i, 0))
    return pl.pallas_call(
        body, name=name, grid=(rows // tr,), in_specs=[spec] * 4, out_specs=[spec] * 3,
        out_shape=[jax.ShapeDtypeStruct(w.shape, F32)] * 3,
    )(w, g, m, v)


def _local_step(x, mem, target, pre_norm, sink_a, mem_norm, post_norm, w_in_g, w_out, w_mkv, gathers=None):
    first_gather, late_gather = gathers if gathers else (None, None)
    u, ut, hosted = _pre_norm(x, pre_norm, first_gather)
    if gathers:
        w_in_g = hosted[0].reshape(N_CHIPS, D_MODEL, SHARD_IN)
    pr = _pre_proj(u, w_in_g, late_gather)
    pr["ut"] = ut
    if gathers:
        w_out, w_mkv = (g.reshape(D_MODEL, g.shape[-1]) for g in pr["hosted"])
    mk, mv = _mem_kv(mem, mem_norm, w_mkv)
    sink = sink_a.reshape(-1)
    qa, ka, va = pr["qa"][None], pr["ka"][None], pr["va"][None]
    oa, lse_a = _band_fwd(qa, ka, va, sink, max_dist=A_WINDOW - 1, name="swa_fwd")
    ob_list, lseb_list = [], []
    for k, (win, dil) in enumerate(B_CONFIGS):
        o_i, l_i = _band_fwd(pr["qb"][k], pr["kb"][k], pr["vb"][k], None, max_dist=win // dil, name=f"dil{dil}_fwd")
        ob_list.append(o_i)
        lseb_list.append(l_i)
    oc, lse_c = _mem_attn_fwd(pr["qc"], mk, mv)
    sink_row = jnp.pad(sink, (0, LANES - sink.shape[0])).reshape(1, LANES)
    po = _post(x, target, post_norm, w_out, sink_row, oa[0], lse_a[0], pr["ga"], ob_list, lseb_list, pr["gb"], oc,
               pr["gc"])
    dqc, dmk, dmv = _mem_attn_bwd(pr["qc"], mk, mv, po["doc"], lse_c, po["dl_c"])
    dqa, dka, dva = _band_bwd(qa, ka, va, po["doa"][None], lse_a, po["dl_a"][None], max_dist=A_WINDOW - 1,
                              name="swa_bwd")
    res = dict(qb=[], kb=[], vb=[])
    for k, (win, dil) in enumerate(B_CONFIGS):
        dq_i, dk_i, dv_i = _band_bwd(pr["qb"][k], pr["kb"][k], pr["vb"][k], po["dob"][k], po["lse_b"][k],
                                     po["dl_b"][k], max_dist=win // dil, name=f"dil{dil}_bwd")
        res["qb"].append(dq_i)
        res["kb"].append(dk_i)
        res["vb"].append(dv_i)
    nat = dict(qa=dqa[0], ka=dka[0], va=dva[0], ga=po["dga"], gb=po["dgb"], qc=dqc, gc=po["dgc"])
    dproj, gw_in = _grad_w_in(pr["ut"], nat, res)
    gw_mkv, gmem = _mem_kv_bwd(mem, mem_norm, w_mkv, dmk, dmv)
    gsink = -po["gsink"][0, :sink.shape[0]]
    return dict(loss=po["loss"], g=po["g"], dproj=dproj, gw_in=gw_in, gw_out=po["gw_out"], gw_mkv=gw_mkv,
                gpost=po["gpost"], gmem=gmem, gsink=gsink, w_in_g=w_in_g)


def kernel(x, mem, pre_norm, w_in, sink_a, mem_norm, w_mem_kv, w_out, post_norm, loss_target, m_pre_norm, m_w_in, m_sink_a, m_mem_norm, m_w_mem_kv, m_w_out, m_post_norm, v_pre_norm, v_w_in, v_sink_a, v_mem_norm, v_w_mem_kv, v_w_out, v_post_norm):
    gathers = (_gather_exchange([w_in[0].astype(BF16)]),
               _gather_exchange([w_out[0].astype(BF16), w_mem_kv[0].astype(BF16)]))
    loc = _local_step(x[0], mem[0], loss_target[0], pre_norm, sink_a, mem_norm, post_norm, None, None, None, gathers)
    big = [loc["gw_in"], loc["gw_out"].reshape(N_CHIPS, D_MODEL // N_CHIPS, D_MODEL),
           loc["gw_mkv"].reshape(N_CHIPS, D_MODEL // N_CHIPS, 2 * C_W)]
    core = lax.axis_index("c").astype(jnp.int32).reshape(1)
    w_in_full = loc["w_in_g"]
    step_in = (x[0], loc["g"], pre_norm, w_in_full, loc["dproj"])
    pair_ex = _pair_exchange(big)
    pair_handles, token = _exchange_start(pair_ex, "pair_exchange_start")
    gx_a, gpre_a = _input_grad(*step_in, None, (0, 3), token, "input_grad_a")
    big, got = _exchange_wait(pair_ex, pair_handles, gpre_a, "pair_exchange_wait")
    parts = [_pair_add(core, own, g) for own, g in zip(big, got)]
    chip_ex = _chip_exchange(parts)
    chip_handles, token = _exchange_start(chip_ex, "chip_exchange_start")
    grad_x, gpre_b = _input_grad(*step_in, gx_a, (3, 13), token, "input_grad_b")
    _, slots = _exchange_wait(chip_ex, chip_handles, gpre_b, "chip_exchange_wait")
    halves = [_slot_sum(s, name=f"chip_sum_{s.shape[2]}", core=core) for s in slots]
    widen = lambda a: jnp.pad(a.reshape(1, -1), ((0, 0), (0, D_MODEL - a.size)))
    small = jnp.concatenate([gpre_a, loc["gpost"], loc["gmem"], widen(loc["gsink"]), widen(loc["loss"]), gpre_b,
                             jnp.zeros((2, D_MODEL), F32)], axis=0)
    (g_in, g_out, g_mkv), small_slots = _pair_gather(halves, small)
    small_sum = _slot_sum(small_slots, name="device_sum")
    g_pre, g_post, g_mem = small_sum[0:1] + small_sum[5:6], small_sum[1:2], small_sum[2:3]
    g_sink = small_sum[3:4, :sink_a.shape[1]]
    loss = small_sum[4, 0]

    d_in, nm_in, nv_in = _adamw(w_in[0], g_in, m_w_in[0], v_w_in[0], "adamw_in")
    d_out, nm_out, nv_out = _adamw(w_out[0], g_out, m_w_out[0], v_w_out[0], "adamw_out")
    d_mkv, nm_mkv, nv_mkv = _adamw(w_mem_kv[0], g_mkv, m_w_mem_kv[0], v_w_mem_kv[0], "adamw_mkv")
    pad6 = lambda a: jnp.pad(a, ((0, 0), (0, D_MODEL - a.shape[1])))
    stack = lambda a, b, c_, d_: jnp.concatenate([a, b, c_, pad6(d_), jnp.zeros((4, D_MODEL), F32)], axis=0)
    d_s, nm_s, nv_s = _adamw(stack(pre_norm, post_norm, mem_norm, sink_a),
                             jnp.concatenate([g_pre, small_sum[1:]], axis=0),
                             stack(m_pre_norm, m_post_norm, m_mem_norm, m_sink_a),
                             stack(v_pre_norm, v_post_norm, v_mem_norm, v_sink_a), "adamw_small")
    ns_ = sink_a.shape[1]
    unpack = lambda a: (a[0:1], a[3:4, :ns_], a[2:3], a[1:2])
    d_pre, d_sink, d_mem, d_post = unpack(d_s)
    nm_pre, nm_sink, nm_mem, nm_post = unpack(nm_s)
    nv_pre, nv_sink, nv_mem, nv_post = unpack(nv_s)
    lead = lambda a: a[None]
    return (loss, lead(grad_x),
            g_pre, lead(g_in), g_sink, g_mem, lead(g_mkv), lead(g_out), g_post,
            d_pre, lead(d_in), d_sink, d_mem, lead(d_mkv), lead(d_out), d_post,
            nm_pre, lead(nm_in), nm_sink, nm_mem, lead(nm_mkv), lead(nm_out), nm_post,
            nv_pre, lead(nv_in), nv_sink, nv_mem, lead(nv_mkv), lead(nv_out), nv_post)
```

```python
import numpy as np
import jax
import jax.numpy as jnp
from jax import lax
from jax.experimental import pallas as pl
from jax.experimental.pallas import tpu as pltpu

F32 = jnp.float32
BF16 = jnp.bfloat16

D_MODEL = 1024
HEAD_DIM = 64
LANES = 128
BLOCK = 128
ROW_TILE = 512
ATTN_TILE = 1024
INPUT_GRAD_TILES = 16
A_W, A_KV_W, B_W, C_W = 384, 128, 384, 256
N_MEM = 256
D_IN = 3072
N_CHIPS = 4
SHARD_IN = D_IN // N_CHIPS
B_CONFIGS = ((128, 1), (512, 4), (2048, 16))
B_DILS = tuple(d for _, d in B_CONFIGS)
A_WINDOW = 128
RMS_EPS = 1e-6
ROPE_THETA = 500000.0
SCALE = HEAD_DIM ** -0.5
NEG = -1e30
ADAM_LR, ADAM_B1, ADAM_B2, ADAM_EPS, ADAM_WD, ADAM_STEP = 0.001, 0.9, 0.999, 1e-08, 0.01, 10

NT = (((1,), (1,)), ((), ()))
TN = (((0,), (0,)), ((), ()))
MESH = pl.DeviceIdType.MESH

_PROJ_LAYOUT = (
    [("qa", 128 * i, True, True) for i in range(3)] + [("ka", 0, True, False), ("va", 0, False, False)]
    + [("ga", 128 * i, False, False) for i in range(3)]
    + [("qb", 128 * i, True, True) for i in range(3)] + [("kb", 128 * i, True, False) for i in range(3)]
    + [("vb", 128 * i, False, False) for i in range(3)] + [("gb", 128 * i, False, False) for i in range(3)]
    + [("qc", 128 * i, False, True) for i in range(2)] + [("gc", 128 * i, False, False) for i in range(2)]
)
_PROJ_WIDTH = dict(qa=A_W, ka=A_KV_W, va=A_KV_W, ga=A_W, qb=B_W, kb=B_W, vb=B_W, gb=B_W, qc=C_W, gc=C_W)
_NATURAL = ("qa", "ka", "va", "ga", "gb", "qc", "gc")
_DILATED = ("qb", "kb", "vb")


def _dot(a, b):
    return jnp.dot(a, b, preferred_element_type=F32)


def _dot_nt(a, b):
    return lax.dot_general(a, b, NT, preferred_element_type=F32)


def _dot_tn(a, b):
    return lax.dot_general(a, b, TN, preferred_element_type=F32)


def _half_masks(rows):
    lane = lax.broadcasted_iota(jnp.int32, (rows, LANES), 1)
    return lane < HEAD_DIM, lane >= HEAD_DIM


def _rope(t, c, sm, sp):
    return t * c + pltpu.roll(t, LANES - 8, 1) * sm + pltpu.roll(t, 8, 1) * sp


def _rope_tables(seq, tm):
    dim = jnp.arange(LANES) % HEAD_DIM
    inv_freq = ROPE_THETA ** (-jnp.arange(0, 16, 2, dtype=F32) / 16)
    freq = jnp.where(dim < 16, inv_freq[dim % 8], 0.0)[None, :]
    local = jnp.arange(tm, dtype=F32)[:, None] * freq
    base = (jnp.arange(seq // tm, dtype=F32) * tm)[:, None] * freq
    both = lambda a: jnp.concatenate([jnp.cos(a), jnp.sin(a)], axis=1)
    return both(local), jnp.repeat(both(base), 8, axis=0)


def _rope_coeffs(local_ref, base_ref):
    cl, sl = local_ref[:, :LANES], local_ref[:, LANES:]
    cb, sb = base_ref[0:1, :LANES], base_ref[0:1, LANES:]
    cos = cb * cl - sb * sl
    sin = sb * cl + cb * sl
    dim = lax.broadcasted_iota(jnp.int32, (1, LANES), 1) % HEAD_DIM
    return cos, jnp.where(dim < 8, -sin, 0.0), jnp.where((dim >= 8) & (dim < 16), sin, 0.0)


def _split3(x):
    a = x.astype(BF16)
    r = x - a.astype(F32)
    b = r.astype(BF16)
    c = (r - b.astype(F32)).astype(BF16)
    return a, b, c


def _rows_to_lanes(x):
    row = lax.broadcasted_iota(jnp.int32, (8, LANES), 0)
    lane = lax.broadcasted_iota(jnp.int32, (8, LANES), 1)
    eye = (row == lane).astype(BF16)
    a, b, c = _split3(x)
    return _dot_nt(eye, a) + _dot_nt(eye, b) + _dot_nt(eye, c)


def _head_sum_matrix(width):
    k = lax.broadcasted_iota(jnp.int32, (width, LANES), 0)
    h = lax.broadcasted_iota(jnp.int32, (width, LANES), 1)
    return (k // HEAD_DIM == h).astype(BF16)


def _head_expand_matrix(width):
    h = lax.broadcasted_iota(jnp.int32, (LANES, width), 0)
    k = lax.broadcasted_iota(jnp.int32, (LANES, width), 1)
    return (k // HEAD_DIM == h).astype(BF16)


def _dot_split(x, mat, terms):
    parts = _split3(x)[:terms]
    out = _dot(parts[0], mat)
    for p in parts[1:]:
        out = out + _dot(p, mat)
    return out


def _per_head(cols, fill=0.0):
    rows = cols[0].shape[0]
    lane = lax.broadcasted_iota(jnp.int32, (rows, LANES), 1)
    out = jnp.full((rows, LANES), fill, F32)
    for h, col in enumerate(cols):
        out = jnp.where(lane == h, col, out)
    return out


def _lane_blocks(width):
    return [slice(p * LANES, (p + 1) * LANES) for p in range(width // LANES)]


def _stage(rows, width):
    return pltpu.VMEM((width // LANES, rows, LANES), F32)


def _stage_write(buf, value):
    for p, lanes in enumerate(_lane_blocks(value.shape[1])):
        buf[p] = value[:, lanes]


def _stage_read(buf):
    return jnp.concatenate([buf[p] for p in range(buf.shape[0])], axis=1) if buf.shape[0] > 1 else buf[0]


def _to_residues(buf, out_ref, dil):
    rows = buf.shape[1] // dil
    for r in range(dil):
        for p in range(buf.shape[0]):
            plane = buf.at[p]
            out_ref[r, :, p * LANES:(p + 1) * LANES] = plane[pl.ds(r, rows, stride=dil), :].astype(out_ref.dtype)


def _from_residues(in_ref, buf, dil):
    rows = buf.shape[1] // dil
    for r in range(dil):
        for p in range(buf.shape[0]):
            plane = buf.at[p]
            plane[pl.ds(r, rows, stride=dil), :] = in_ref[r, :, p * LANES:(p + 1) * LANES].astype(F32)


def _residue_spec(dil, tm, width):
    return pl.BlockSpec((dil, tm // dil, width), lambda i: (0, i, 0))


def _gather_exchange(shards_2d):
    shards = tuple(s.reshape(2, s.shape[0] // 2, s.shape[1]) for s in shards_2d)
    n = len(shards)

    def copies(in_refs, out_refs, send_sems, recv_sems):
        srcs, outs = in_refs[:n], out_refs
        x, y, c = lax.axis_index("x"), lax.axis_index("y"), lax.axis_index("c")
        my_chip = 2 * x + y
        sibling = (x, y, 1 - c)
        chips = [(1 - x, y), (x, 1 - y), (1 - x, 1 - y)]

        def copy(k, src, dst, to):
            return pltpu.make_async_remote_copy(src_ref=src, dst_ref=dst, send_sem=send_sems.at[k],
                                                recv_sem=recv_sems.at[k], device_id=to, device_id_type=MESH)

        first, arrive, passed, sibling_arrive = [], [], [], []
        for j, (cx, cy) in enumerate(chips):
            chip = 2 * cx + cy
            for t in range(n):
                k = n * j + t
                first.append(copy(k, srcs[t].at[c], outs[t].at[my_chip, c], (cx, cy, c)))
                arrive.append(copy(k, srcs[t].at[c], outs[t].at[chip, c], (cx, cy, c)))
                passed.append(copy(n * 3 + k, outs[t].at[chip, c], outs[t].at[chip, c], sibling))
                sibling_arrive.append(copy(n * 3 + k, outs[t].at[chip, 1 - c], outs[t].at[chip, 1 - c], sibling))
        return first, arrive, passed, sibling_arrive

    def start(*refs):
        for cp in copies(*refs)[0]:
            cp.start()

    def finish(*refs):
        first, arrive, passed, sibling_arrive = copies(*refs)
        for got, fwd in zip(arrive, passed):
            got.wait_recv()
            fwd.start()
        for cp in sibling_arrive:
            cp.wait_recv()
        for cp in first + passed:
            cp.wait_send()

    my_chip = 2 * lax.axis_index("x") + lax.axis_index("y")
    landing = [lax.dynamic_update_slice(jnp.zeros((N_CHIPS,) + s.shape, s.dtype), s[None], (my_chip, 0, 0, 0))
               for s in shards]
    return dict(ins=list(shards) + landing, start=start, finish=finish, aliases={n + t: t for t in range(n)},
                outs=[jax.ShapeDtypeStruct((N_CHIPS,) + s.shape, s.dtype) for s in shards],
                sems=[pltpu.SemaphoreType.DMA((6 * n,)), pltpu.SemaphoreType.DMA((6 * n,))])


def _mem_kv(mem, mem_norm, w_mkv):
    def body(mem_ref, g_ref, w_ref, mk_ref, mv_ref):
        m = mem_ref[...]
        r = lax.rsqrt(jnp.mean(m * m, axis=-1, keepdims=True) + RMS_EPS)
        mn = (m * r * g_ref[...]).astype(BF16)
        kv = _dot(mn, w_ref[...])
        mk_ref[...] = kv[:, :C_W].astype(BF16)
        mv_ref[...] = kv[:, C_W:].astype(BF16)

    return pl.pallas_call(
        body, name="mem_kv",
        out_shape=[jax.ShapeDtypeStruct((N_MEM, C_W), BF16)] * 2,
    )(mem, mem_norm, w_mkv)


def _mem_kv_bwd(mem, mem_norm, w_mkv, dmk, dmv):
    def body(mem_ref, g_ref, w_ref, dmk_ref, dmv_ref, gw_ref, gn_ref):
        m = mem_ref[...]
        r = lax.rsqrt(jnp.mean(m * m, axis=-1, keepdims=True) + RMS_EPS)
        mhat = m * r
        mn = (mhat * g_ref[...]).astype(BF16)
        dkv = jnp.concatenate([dmk_ref[...], dmv_ref[...]], axis=1).astype(BF16)
        gw_ref[...] = _dot_tn(mn, dkv)
        dmn = _dot_nt(dkv, w_ref[...])
        gn_ref[...] = jnp.sum(dmn * mhat, axis=0, keepdims=True)

    return pl.pallas_call(
        body, name="mem_kv_bwd",
        out_shape=[jax.ShapeDtypeStruct((D_MODEL, 2 * C_W), F32), jax.ShapeDtypeStruct((1, D_MODEL), F32)],
    )(mem, mem_norm, w_mkv, dmk, dmv)


def _pre_norm(x, pre_norm, host=None):
    seq = x.shape[0]
    tm = min(ROW_TILE, seq)
    n_host_in = len(host["ins"]) if host else 0
    n_host_out = len(host["outs"]) if host else 0

    def body(x_ref, g_ref, *refs):
        host_in, (u_ref, ut_ref), refs = refs[:n_host_in], refs[n_host_in:n_host_in + 2], refs[n_host_in + 2:]
        host_out, sems = refs[:n_host_out], refs[n_host_out:]
        if host:
            @pl.when(pl.program_id(0) == 0)
            def _():
                host["start"](host_in, host_out, *sems)

        xv = x_ref[...]
        r = lax.rsqrt(jnp.mean(xv * xv, axis=-1, keepdims=True) + RMS_EPS)
        u = xv * r * g_ref[...]
        u_ref[...] = u.astype(BF16)
        ut_ref[...] = u.T.astype(BF16)
        if host:
            @pl.when(pl.program_id(0) == seq // tm - 1)
            def _():
                host["finish"](host_in, host_out, *sems)

    any_spec = pl.BlockSpec(memory_space=pl.ANY)
    ins = [x, pre_norm]
    in_specs = [pl.BlockSpec((tm, D_MODEL), lambda i: (i, 0)), pl.BlockSpec(pre_norm.shape, lambda i: (0, 0))]
    out_shape = [jax.ShapeDtypeStruct((seq, D_MODEL), BF16), jax.ShapeDtypeStruct((D_MODEL, seq), BF16)]
    out_specs = [pl.BlockSpec((tm, D_MODEL), lambda i: (i, 0)), pl.BlockSpec((D_MODEL, tm), lambda i: (0, i))]
    aliases, scratch = {}, []
    if host:
        aliases = {len(ins) + k: 2 + v for k, v in host.get("aliases", {}).items()}
        ins += list(host["ins"])
        in_specs += [any_spec] * n_host_in
        out_shape += list(host["outs"])
        out_specs += [any_spec] * n_host_out
        scratch = list(host["sems"])
    res = pl.pallas_call(
        body, name="pre_norm", grid=(seq // tm,), in_specs=in_specs, out_specs=out_specs, out_shape=out_shape,
        input_output_aliases=aliases, scratch_shapes=scratch,
        compiler_params=pltpu.CompilerParams(dimension_semantics=("arbitrary",)),
    )(*ins)
    return res[0], res[1], res[2:]


def _pre_proj(u, w_in_g, host=None):
    seq = u.shape[0]
    tm = min(ROW_TILE, seq)
    n_nat, n_dil = len(_NATURAL), len(_DILATED) * len(B_DILS)
    rope = _rope_tables(seq, tm)

    n_host_in = len(host["ins"]) if host else 0
    n_host_out = len(host["outs"]) if host else 0
    n_own_out = n_nat + n_dil

    def body(u_ref, w_ref, rl_ref, rb_ref, *refs):
        host_in, refs = refs[:n_host_in], refs[n_host_in:]
        nat = dict(zip(_NATURAL, refs[:n_nat]))
        res = {n: refs[n_nat + len(B_DILS) * k:n_nat + len(B_DILS) * (k + 1)] for k, n in enumerate(_DILATED)}
        host_out = refs[n_own_out:n_own_out + n_host_out]
        bufs = dict(zip(_DILATED, refs[n_own_out + n_host_out:]))
        sems = refs[n_own_out + n_host_out + len(_DILATED):]
        if host:
            @pl.when(pl.program_id(0) == 0)
            def _():
                host["start"](host_in, host_out, *sems)

        ub = u_ref[...]
        c, sm, sp = _rope_coeffs(rl_ref, rb_ref)
        for j in range(N_CHIPS):
            pj = _dot(ub, w_ref[j])
            for b in range(SHARD_IN // LANES):
                name, off, roped, scaled = _PROJ_LAYOUT[(SHARD_IN // LANES) * j + b]
                piece = pj[:, LANES * b:LANES * (b + 1)]
                if roped:
                    piece = _rope(piece, c, sm, sp)
                if scaled:
                    piece = piece * SCALE
                if name in bufs:
                    bufs[name][off // LANES] = piece
                else:
                    nat[name][:, off:off + LANES] = piece.astype(BF16)
        for name in _DILATED:
            for ref, dil in zip(res[name], B_DILS):
                _to_residues(bufs[name], ref, dil)
        if host:
            @pl.when(pl.program_id(0) == seq // tm - 1)
            def _():
                host["finish"](host_in, host_out, *sems)

    row = lambda w: pl.BlockSpec((tm, w), lambda i: (i, 0))
    full = lambda a: pl.BlockSpec(a.shape, lambda i: (0,) * a.ndim)
    any_spec = pl.BlockSpec(memory_space=pl.ANY)
    out_shape = [jax.ShapeDtypeStruct((seq, _PROJ_WIDTH[n]), BF16) for n in _NATURAL]
    out_specs = [row(_PROJ_WIDTH[n]) for n in _NATURAL]
    for n in _DILATED:
        for dil in B_DILS:
            out_shape.append(jax.ShapeDtypeStruct((dil, seq // dil, B_W), BF16))
            out_specs.append(_residue_spec(dil, tm, B_W))
    ins = [u, w_in_g, *rope]
    in_specs = [row(D_MODEL), full(w_in_g), full(rope[0]), pl.BlockSpec((8, 2 * LANES), lambda i: (i, 0))]
    scratch = [_stage(tm, B_W)] * len(_DILATED)
    aliases = {}
    if host:
        aliases = {len(ins) + k: n_own_out + v for k, v in host.get("aliases", {}).items()}
        ins += list(host["ins"])
        in_specs += [any_spec] * n_host_in
        out_shape += list(host["outs"])
        out_specs += [any_spec] * n_host_out
        scratch += list(host["sems"])
    res = pl.pallas_call(
        body, name="pre_proj", grid=(seq // tm,), in_specs=in_specs, out_specs=out_specs, out_shape=out_shape,
        input_output_aliases=aliases, scratch_shapes=scratch,
        compiler_params=pltpu.CompilerParams(dimension_semantics=("arbitrary",)),
    )(*ins)
    out = dict(zip(_NATURAL, res[:n_nat]))
    for k, n in enumerate(_DILATED):
        out[n] = res[n_nat + len(B_DILS) * k:n_nat + len(B_DILS) * (k + 1)]
    out["hosted"] = res[n_own_out:]
    return out


def _band_bias(max_dist, transposed):
    i = np.arange(BLOCK)[:, None]
    j = np.arange(BLOCK)[None, :]
    if transposed:
        same = i <= j
        other = (j + BLOCK - i) <= max_dist
        vis = np.concatenate([same, other], axis=1)
    else:
        prev = (i + BLOCK - j) <= max_dist
        same = j <= i
        vis = np.concatenate([prev, same], axis=1)
    return jnp.asarray(np.where(vis, 0.0, NEG).astype(np.float32))


def _kv_place(h, gqa):
    return (0, h // 3) if gqa else (h // 2, h % 2)


def _band_fwd(q, k, v, sink, *, max_dist, name):
    dil, length, wq = q.shape
    wk = k.shape[2]
    gqa = wk != wq
    tq = min(ATTN_TILE, length)
    ns, nt = tq // BLOCK, length // tq
    npair = wq // LANES
    bias = _band_bias(max_dist, transposed=False)
    has_sink = sink is not None

    def body(*refs):
        if has_sink:
            sink_ref, refs = refs[0], refs[1:]
        q_ref, k_ref, kp_ref, v_ref, vp_ref, bias_ref, o_ref, lse_ref, kbuf, vbuf = refs[:10]
        i = pl.program_id(1)
        kbuf[0:BLOCK] = kp_ref[...]
        kbuf[BLOCK:] = k_ref[...]
        vbuf[0:BLOCK] = vp_ref[...]
        vbuf[BLOCK:] = v_ref[...]
        if gqa:
            kroll, vroll = refs[10:12]
            kroll[...] = pltpu.roll(kbuf[...], HEAD_DIM, 1)
            vroll[...] = pltpu.roll(vbuf[...], HEAD_DIM, 1)
        half = _half_masks(BLOCK)
        col_prev = (lax.broadcasted_iota(jnp.int32, (1, 2 * BLOCK), 1) < BLOCK).astype(F32)

        def sub(a, carry):
            r0 = pl.multiple_of(a * BLOCK, BLOCK)
            pen = jnp.where((i == 0) & (a == 0), NEG, 0.0)
            b = bias_ref[...] + pen * col_prev
            scores = []
            for p in range(npair):
                qp = q_ref[pl.ds(r0, BLOCK), p * LANES:(p + 1) * LANES]
                for e in range(2):
                    pk, ek = _kv_place(2 * p + e, gqa)
                    kw = (kbuf if ek == e else kroll)[pl.ds(r0, 2 * BLOCK), pk * LANES:(pk + 1) * LANES]
                    scores.append(_dot_nt(jnp.where(half[e], qp, jnp.zeros_like(qp)), kw))
            m_cols, l_cols, probs = [], [], []
            for h, s in enumerate(scores):
                s = s + b
                m = jnp.max(s, axis=1, keepdims=True)
                if has_sink:
                    m = jnp.maximum(m, sink_ref[h])
                pe = jnp.exp(s - m)
                l = jnp.sum(pe, axis=1, keepdims=True)
                if has_sink:
                    l = l + jnp.exp(sink_ref[h] - m)
                probs.append(pe.astype(BF16))
                m_cols.append(m)
                l_cols.append(l)
            for p in range(npair):
                o_h = []
                for e in range(2):
                    h = 2 * p + e
                    pk, ek = _kv_place(h, gqa)
                    vw = (vbuf if ek == e else vroll)[pl.ds(r0, 2 * BLOCK), pk * LANES:(pk + 1) * LANES]
                    o_h.append(_dot(probs[h], vw) * (1.0 / l_cols[h]))
                o_ref[pl.ds(r0, BLOCK), p * LANES:(p + 1) * LANES] = jnp.where(half[0], o_h[0], o_h[1]).astype(BF16)
            lse_ref[pl.ds(r0, BLOCK), :] = _per_head(m_cols) + jnp.log(_per_head(l_cols, 1.0))
            return carry

        lax.fori_loop(0, ns, sub, 0, unroll=True)

    main = lambda w: pl.BlockSpec((None, tq, w), lambda r, i: (r, i, 0))
    prev = lambda w: pl.BlockSpec((None, BLOCK, w), lambda r, i: (r, jnp.maximum(i * ns - 1, 0), 0))
    in_specs = [main(wq), main(wk), prev(wk), main(wk), prev(wk), pl.BlockSpec(bias.shape, lambda r, i: (0, 0))]
    args = [q, k, k, v, v, bias]
    if has_sink:
        in_specs = [pl.BlockSpec(memory_space=pltpu.SMEM)] + in_specs
        args = [sink] + args
    scratch = [pltpu.VMEM((tq + BLOCK, wk), BF16)] * (4 if gqa else 2)
    return pl.pallas_call(
        body, name=name, grid=(dil, nt), in_specs=in_specs,
        out_specs=[main(wq), main(LANES)],
        out_shape=[jax.ShapeDtypeStruct((dil, length, wq), BF16), jax.ShapeDtypeStruct((dil, length, LANES), F32)],
        scratch_shapes=scratch,
    )(*args)


def _band_bwd(q, k, v, do, lse, delta, *, max_dist, name):
    dil, length, wq = q.shape
    wk = k.shape[2]
    gqa = wk != wq
    tq = min(ATTN_TILE, length)
    ns, nt = tq // BLOCK, length // tq
    npair = wq // LANES
    nblocks = length // BLOCK
    bias = _band_bias(max_dist, transposed=True)

    def body(q_ref, qn_ref, do_ref, don_ref, lse_ref, lsen_ref, dl_ref, dln_ref, k_ref, v_ref, bias_ref,
             dq_ref, dk_ref, dv_ref, stat_l, stat_d, dqt, kt, *rolled):
        i = pl.program_id(1)
        for pk in range(wk // LANES):
            kt[pk] = k_ref[:, pk * LANES:(pk + 1) * LANES].astype(F32).T.astype(BF16)
        if gqa:
            kroll, vroll, ktroll = rolled
            kroll[...] = pltpu.roll(k_ref[...], HEAD_DIM, 1)
            vroll[...] = pltpu.roll(v_ref[...], HEAD_DIM, 1)
            ktroll[0] = kroll[...].astype(F32).T.astype(BF16)
        for a in range(ns):
            rows = slice(a * BLOCK, (a + 1) * BLOCK)
            stat_l[a] = _rows_to_lanes(lse_ref[rows, :])
            stat_d[a] = _rows_to_lanes(dl_ref[rows, :])
        stat_l[ns] = _rows_to_lanes(lsen_ref[...])
        stat_d[ns] = _rows_to_lanes(dln_ref[...])

        @pl.when(i == 0)
        def _():
            dqt[:, :, 0:BLOCK] = jnp.zeros((npair, LANES, BLOCK), F32)

        @pl.when(i > 0)
        def _():
            dqt[:, :, 0:BLOCK] = dqt[:, :, tq:tq + BLOCK]

        dqt[:, :, BLOCK:] = jnp.zeros((npair, LANES, tq), F32)
        half2 = _half_masks(2 * BLOCK)
        row = lax.broadcasted_iota(jnp.int32, (LANES, BLOCK), 0)
        row_half = (row < HEAD_DIM, row >= HEAD_DIM)
        col_next = (lax.broadcasted_iota(jnp.int32, (1, 2 * BLOCK), 1) >= BLOCK).astype(F32)

        for b in range(ns):
            rows = slice(b * BLOCK, (b + 1) * BLOCK)
            window = slice(b * BLOCK, (b + 2) * BLOCK)
            bt = bias_ref[...]
            if b == ns - 1:
                bt = bt + jnp.where(i == nt - 1, NEG, 0.0) * col_next
            acc = {}
            items = []
            nxt_rows = slice((b + 1) * BLOCK, (b + 2) * BLOCK)
            for p in range(npair):
                lanes = slice(p * LANES, (p + 1) * LANES)
                q_next = q_ref[nxt_rows, lanes] if b + 1 < ns else qn_ref[:, lanes]
                do_next = do_ref[nxt_rows, lanes] if b + 1 < ns else don_ref[:, lanes]
                qw = jnp.concatenate([q_ref[rows, lanes], q_next], axis=0)
                dow = jnp.concatenate([do_ref[rows, lanes], do_next], axis=0)
                for e in range(2):
                    h = 2 * p + e
                    pk, ek = _kv_place(h, gqa)
                    klanes = slice(pk * LANES, (pk + 1) * LANES)
                    kb = (k_ref if ek == e else kroll)[rows, klanes]
                    vb = (v_ref if ek == e else vroll)[rows, klanes]
                    qm = jnp.where(half2[e], qw, jnp.zeros_like(qw))
                    dom = jnp.where(half2[e], dow, jnp.zeros_like(dow))
                    items.append(dict(p=p, e=e, h=h, pk=pk, ek=ek, qm=qm, dom=dom,
                                      st=_dot_nt(kb, qm), dpt=_dot_nt(vb, dom)))
            for it in items:
                h = it["h"]
                lrow = jnp.concatenate([stat_l[b, h:h + 1, :], stat_l[b + 1, h:h + 1, :]], axis=1)
                drow = jnp.concatenate([stat_d[b, h:h + 1, :], stat_d[b + 1, h:h + 1, :]], axis=1)
                pt = jnp.exp(it["st"] + bt - lrow)
                it["ptb"] = pt.astype(BF16)
                it["dsb"] = (pt * (it["dpt"] - drow)).astype(BF16)
            for p in range(npair):
                pair = items[2 * p:2 * p + 2]
                lanes = slice(p * LANES, (p + 1) * LANES)
                kparts = []
                for it in pair:
                    kbt = (kt if it["ek"] == it["e"] else ktroll)[it["pk"], :, rows]
                    kparts.append(jnp.where(row_half[it["e"]], kbt, jnp.zeros_like(kbt)))
                ds_keys = jnp.concatenate([it["dsb"] for it in pair], axis=0)
                dqt[p, :, window] += _dot(jnp.concatenate(kparts, axis=1), ds_keys)
                if not gqa:
                    q_both = jnp.concatenate([it["qm"] for it in pair], axis=0)
                    do_both = jnp.concatenate([it["dom"] for it in pair], axis=0)
                    dk_ref[rows, lanes] = _dot(jnp.concatenate([it["dsb"] for it in pair], axis=1), q_both).astype(BF16)
                    dv_ref[rows, lanes] = _dot(jnp.concatenate([it["ptb"] for it in pair], axis=1), do_both).astype(BF16)
                else:
                    for it in pair:
                        dv_c = _dot(it["ptb"], it["dom"])
                        dk_c = _dot(it["dsb"], it["qm"])
                        key = (it["pk"], it["ek"] == it["e"])
                        if key in acc:
                            acc[key] = (acc[key][0] + dk_c, acc[key][1] + dv_c)
                        else:
                            acc[key] = (dk_c, dv_c)
            if gqa:
                dk_al, dv_al = acc[(0, True)]
                dk_mis, dv_mis = acc[(0, False)]
                dk_ref[rows, :] = (dk_al + pltpu.roll(dk_mis, HEAD_DIM, 1)).astype(BF16)
                dv_ref[rows, :] = (dv_al + pltpu.roll(dv_mis, HEAD_DIM, 1)).astype(BF16)

        for p in range(npair):
            dq_ref[:, p * LANES:(p + 1) * LANES] = dqt[p, :, 0:tq].T.astype(BF16)

    main = lambda w: pl.BlockSpec((None, tq, w), lambda r, i: (r, i, 0))
    nxt = lambda w: pl.BlockSpec((None, BLOCK, w), lambda r, i: (r, jnp.minimum((i + 1) * ns, nblocks - 1), 0))
    scratch = [pltpu.VMEM((ns + 1, 8, LANES), F32), pltpu.VMEM((ns + 1, 8, LANES), F32),
               pltpu.VMEM((npair, LANES, tq + BLOCK), F32), pltpu.VMEM((wk // LANES, LANES, tq), BF16)]
    if gqa:
        scratch = scratch + [pltpu.VMEM((tq, wk), BF16)] * 2 + [pltpu.VMEM((1, LANES, tq), BF16)]
    return pl.pallas_call(
        body, name=name, grid=(dil, nt),
        in_specs=[main(wq), nxt(wq), main(wq), nxt(wq), main(LANES), nxt(LANES), main(LANES), nxt(LANES),
                  main(wk), main(wk), pl.BlockSpec(bias.shape, lambda r, i: (0, 0))],
        out_specs=[main(wq), main(wk), main(wk)],
        out_shape=[jax.ShapeDtypeStruct((dil, length, wq), BF16), jax.ShapeDtypeStruct((dil, length, wk), BF16),
                   jax.ShapeDtypeStruct((dil, length, wk), BF16)],
        scratch_shapes=scratch,
        compiler_params=pltpu.CompilerParams(dimension_semantics=("arbitrary", "arbitrary")),
    )(q, q, do, do, lse, lse, delta, delta, k, v, bias)


def _mem_attn_fwd(q, mk, mv):
    seq = q.shape[0]
    tq = min(ATTN_TILE, seq)
    ns = tq // BLOCK

    def body(q_ref, mk_ref, mv_ref, o_ref, lse_ref):
        half = _half_masks(BLOCK)

        def sub(a, carry):
            r0 = pl.multiple_of(a * BLOCK, BLOCK)
            scores = []
            for p in range(C_W // LANES):
                lanes = slice(p * LANES, (p + 1) * LANES)
                qp = q_ref[pl.ds(r0, BLOCK), lanes]
                for e in range(2):
                    scores.append(_dot_nt(jnp.where(half[e], qp, jnp.zeros_like(qp)), mk_ref[:, lanes]))
            m_cols, l_cols, probs = [], [], []
            for s in scores:
                m = jnp.max(s, axis=1, keepdims=True)
                pe = jnp.exp(s - m)
                probs.append(pe.astype(BF16))
                m_cols.append(m)
                l_cols.append(jnp.sum(pe, axis=1, keepdims=True))
            for p in range(C_W // LANES):
                lanes = slice(p * LANES, (p + 1) * LANES)
                o_h = [_dot(probs[2 * p + e], mv_ref[:, lanes]) * (1.0 / l_cols[2 * p + e]) for e in range(2)]
                o_ref[pl.ds(r0, BLOCK), lanes] = jnp.where(half[0], o_h[0], o_h[1]).astype(BF16)
            lse_ref[pl.ds(r0, BLOCK), :] = _per_head(m_cols) + jnp.log(_per_head(l_cols, 1.0))
            return carry

        lax.fori_loop(0, ns, sub, 0, unroll=True)

    row = lambda w: pl.BlockSpec((tq, w), lambda i: (i, 0))
    full = pl.BlockSpec((N_MEM, C_W), lambda i: (0, 0))
    return pl.pallas_call(
        body, name="mem_attn_fwd", grid=(seq // tq,), in_specs=[row(C_W), full, full],
        out_specs=[row(C_W), row(LANES)],
        out_shape=[jax.ShapeDtypeStruct((seq, C_W), BF16), jax.ShapeDtypeStruct((seq, LANES), F32)],
    )(q, mk, mv)


def _mem_attn_bwd(q, mk, mv, do, lse, delta):
    seq = q.shape[0]
    tq = min(ATTN_TILE, seq)
    ns = tq // BLOCK
    npair = C_W // LANES

    def body(q_ref, mk_ref, mv_ref, do_ref, lse_ref, dl_ref, dq_ref, dmk_ref, dmv_ref, stat_l, stat_d, mkt, dqt):
        @pl.when(pl.program_id(0) == 0)
        def _():
            dmk_ref[...] = jnp.zeros_like(dmk_ref)
            dmv_ref[...] = jnp.zeros_like(dmv_ref)
            for p in range(npair):
                mkt[p] = mk_ref[:, p * LANES:(p + 1) * LANES].astype(F32).T.astype(BF16)

        for a in range(ns):
            rows = slice(a * BLOCK, (a + 1) * BLOCK)
            stat_l[a] = _rows_to_lanes(lse_ref[rows, :])
            stat_d[a] = _rows_to_lanes(dl_ref[rows, :])
        half = _half_masks(BLOCK)
        row = lax.broadcasted_iota(jnp.int32, (LANES, N_MEM), 0)
        row_half = (row < HEAD_DIM, row >= HEAD_DIM)

        for a in range(ns):
            rows = slice(a * BLOCK, (a + 1) * BLOCK)
            items = []
            for p in range(npair):
                lanes = slice(p * LANES, (p + 1) * LANES)
                qp = q_ref[rows, lanes]
                dop = do_ref[rows, lanes]
                for e in range(2):
                    qm = jnp.where(half[e], qp, jnp.zeros_like(qp))
                    dom = jnp.where(half[e], dop, jnp.zeros_like(dop))
                    items.append(dict(p=p, e=e, qm=qm, dom=dom, st=_dot_nt(mk_ref[:, lanes], qm),
                                      dpt=_dot_nt(mv_ref[:, lanes], dom)))
            for it in items:
                h = 2 * it["p"] + it["e"]
                pt = jnp.exp(it["st"] - stat_l[a, h:h + 1, :])
                it["ptb"] = pt.astype(BF16)
                it["dsb"] = (pt * (it["dpt"] - stat_d[a, h:h + 1, :])).astype(BF16)
            for p in range(npair):
                lanes = slice(p * LANES, (p + 1) * LANES)
                pair = [it for it in items if it["p"] == p]
                join = lambda name, axis: jnp.concatenate([it[name] for it in pair], axis=axis)
                dmv_ref[:, lanes] += _dot(join("ptb", 1), join("dom", 0))
                dmk_ref[:, lanes] += _dot(join("dsb", 1), join("qm", 0))
                kbt = mkt[p]
                k_both = jnp.concatenate([jnp.where(row_half[e], kbt, jnp.zeros_like(kbt)) for e in range(2)], axis=1)
                dqt[p, :, rows] = _dot(k_both, join("dsb", 0))
        for p in range(npair):
            dq_ref[:, p * LANES:(p + 1) * LANES] = dqt[p].T.astype(BF16)

    row = lambda w: pl.BlockSpec((tq, w), lambda i: (i, 0))
    full = pl.BlockSpec((N_MEM, C_W), lambda i: (0, 0))
    return pl.pallas_call(
        body, name="mem_attn_bwd", grid=(seq // tq,),
        in_specs=[row(C_W), full, full, row(C_W), row(LANES), row(LANES)], out_specs=[row(C_W), full, full],
        out_shape=[jax.ShapeDtypeStruct((seq, C_W), BF16), jax.ShapeDtypeStruct((N_MEM, C_W), F32),
                   jax.ShapeDtypeStruct((N_MEM, C_W), F32)],
        scratch_shapes=[pltpu.VMEM((ns, 8, LANES), F32)] * 2
        + [pltpu.VMEM((npair, LANES, N_MEM), BF16), pltpu.VMEM((npair, LANES, tq), F32)],
        compiler_params=pltpu.CompilerParams(dimension_semantics=("arbitrary",)),
    )(q, mk, mv, do, lse, delta)


def _silu_and_grad(g):
    s = 1.0 / (1.0 + jnp.exp(-g))
    return g * s, s * (1.0 + g * (1.0 - s))


def _post(x, target, post_norm, w_out, sink_row, oa, lse_a, ga, ob_list, lseb_list, gb, oc, gc):
    seq = x.shape[0]
    tm = min(ROW_TILE, seq)
    inv_d = 1.0 / D_MODEL
    nd = len(B_DILS)

    def body(*refs):
        (x_ref, t_ref, gp_ref, w_ref, sink_ref, oa_ref, lsea_ref, ga_ref), refs = refs[:8], refs[8:]
        ob_refs, lb_refs, (gb_ref, oc_ref, gc_ref), refs = refs[:nd], refs[nd:2 * nd], refs[2 * nd:2 * nd + 3], refs[2 * nd + 3:]
        (g_ref, doa_ref, dla_ref, dga_ref), refs = refs[:4], refs[4:]
        dob_refs, lsec_refs, dlb_refs, refs = refs[:nd], refs[nd:2 * nd], refs[2 * nd:3 * nd], refs[3 * nd:]
        (dgb_ref, doc_ref, dlc_ref, dgc_ref, gw_ref, gpost_ref, gsink_ref, loss_ref), refs = refs[:8], refs[8:]
        ycat, obufs, lbufs, st_do, st_l, st_d = refs[0], refs[1:nd], refs[nd:2 * nd - 1], refs[2 * nd - 1], refs[2 * nd], refs[2 * nd + 1]

        @pl.when(pl.program_id(0) == 0)
        def _():
            gw_ref[...] = jnp.zeros_like(gw_ref)
            gpost_ref[...] = jnp.zeros_like(gpost_ref)
            gsink_ref[...] = jnp.zeros_like(gsink_ref)
            loss_ref[...] = jnp.zeros_like(loss_ref)

        o_i, l_i = [ob_refs[0][0].astype(F32)], [lb_refs[0][0]]
        for k in range(1, nd):
            _from_residues(ob_refs[k], obufs[k - 1], B_DILS[k])
            _from_residues(lb_refs[k], lbufs[k - 1], B_DILS[k])
            o_i.append(_stage_read(obufs[k - 1]))
            l_i.append(_stage_read(lbufs[k - 1]))
        mx = l_i[0]
        for l in l_i[1:]:
            mx = jnp.maximum(mx, l)
        w_i = [jnp.exp(l - mx) for l in l_i]
        z = w_i[0]
        for w in w_i[1:]:
            z = z + w
        _stage_write(st_l, mx + jnp.log(z))
        expand = _head_expand_matrix(B_W)
        inv_z = 1.0 / z
        ob = None
        for w, o in zip(w_i, o_i):
            term = _dot_split(w * inv_z, expand, 2) * o
            ob = term if ob is None else ob + term
        oa, oc = oa_ref[...].astype(F32), oc_ref[...].astype(F32)
        sa, dsa = _silu_and_grad(ga_ref[...].astype(F32))
        sb, dsb = _silu_and_grad(gb_ref[...].astype(F32))
        sc, dsc = _silu_and_grad(gc_ref[...].astype(F32))
        ycat[:, 0:A_W] = (oa * sa).astype(BF16)
        ycat[:, A_W:A_W + B_W] = (ob * sb).astype(BF16)
        ycat[:, A_W + B_W:] = (oc * sc).astype(BF16)
        y2 = _dot(ycat[...], w_ref[...])
        r = lax.rsqrt(jnp.mean(y2 * y2, axis=-1, keepdims=True) + RMS_EPS)
        zhat = y2 * r
        gp = gp_ref[...]
        err = x_ref[...] + zhat * gp - t_ref[...]
        loss_ref[...] += jnp.sum(err * err) * (0.5 * inv_d)
        g = err * inv_d
        g_ref[...] = g
        gpost_ref[...] += jnp.sum(g * zhat, axis=0, keepdims=True)
        a = g * gp
        dy2 = (r * (a - zhat * jnp.mean(a * zhat, axis=-1, keepdims=True))).astype(BF16)
        gw_ref[...] += _dot_tn(ycat[...], dy2)
        dycat = _dot_nt(dy2, w_ref[...])
        dya, dyb, dyc = dycat[:, 0:A_W], dycat[:, A_W:A_W + B_W], dycat[:, A_W + B_W:]
        doa, dob, doc = dya * sa, dyb * sb, dyc * sc
        doa_ref[...] = doa.astype(BF16)
        doc_ref[...] = doc.astype(BF16)
        dga_ref[...] = (dya * oa * dsa).astype(BF16)
        dgb_ref[...] = (dyb * ob * dsb).astype(BF16)
        dgc_ref[...] = (dyc * oc * dsc).astype(BF16)
        dl_a = _dot_split(doa * oa, _head_sum_matrix(A_W), 2)
        dla_ref[...] = dl_a
        dlc_ref[...] = _dot_split(doc * oc, _head_sum_matrix(C_W), 2)
        gsink_ref[...] += jnp.sum(jnp.exp(sink_ref[...] - lsea_ref[...]) * dl_a, axis=0, keepdims=True)
        _stage_write(st_do, dob)
        _stage_write(st_d, _dot_split(dob * ob, _head_sum_matrix(B_W), 2))
        for k, dil in enumerate(B_DILS):
            _to_residues(st_do, dob_refs[k], dil)
            _to_residues(st_l, lsec_refs[k], dil)
            _to_residues(st_d, dlb_refs[k], dil)

    row = lambda w: pl.BlockSpec((tm, w), lambda i: (i, 0))
    full = lambda shape: pl.BlockSpec(shape, lambda i: (0,) * len(shape))
    res_specs = lambda w: [_residue_spec(d, tm, w) for d in B_DILS]
    res_shapes = lambda w, dt: [jax.ShapeDtypeStruct((d, seq // d, w), dt) for d in B_DILS]
    ins = [x, target, post_norm, w_out, sink_row, oa, lse_a, ga, *ob_list, *lseb_list, gb, oc, gc]
    in_specs = ([row(D_MODEL), row(D_MODEL), full((1, D_MODEL)), full((D_MODEL, D_MODEL)), full((1, LANES)),
                 row(A_W), row(LANES), row(A_W)] + res_specs(B_W) + res_specs(LANES) + [row(B_W), row(C_W), row(C_W)])
    out_shape = ([jax.ShapeDtypeStruct((seq, D_MODEL), F32), jax.ShapeDtypeStruct((seq, A_W), BF16),
                  jax.ShapeDtypeStruct((seq, LANES), F32), jax.ShapeDtypeStruct((seq, A_W), BF16)]
                 + res_shapes(B_W, BF16) + res_shapes(LANES, F32) + res_shapes(LANES, F32)
                 + [jax.ShapeDtypeStruct((seq, B_W), BF16), jax.ShapeDtypeStruct((seq, C_W), BF16),
                    jax.ShapeDtypeStruct((seq, LANES), F32), jax.ShapeDtypeStruct((seq, C_W), BF16),
                    jax.ShapeDtypeStruct((D_MODEL, D_MODEL), F32), jax.ShapeDtypeStruct((1, D_MODEL), F32),
                    jax.ShapeDtypeStruct((1, LANES), F32), jax.ShapeDtypeStruct((1, LANES), F32)])
    out_specs = ([row(D_MODEL), row(A_W), row(LANES), row(A_W)] + res_specs(B_W) + res_specs(LANES) + res_specs(LANES)
                 + [row(B_W), row(C_W), row(LANES), row(C_W),
                    full((D_MODEL, D_MODEL)), full((1, D_MODEL)), full((1, LANES)), full((1, LANES))])
    scratch = ([pltpu.VMEM((tm, D_MODEL), BF16)] + [_stage(tm, B_W)] * (nd - 1) + [_stage(tm, LANES)] * (nd - 1)
               + [_stage(tm, B_W), _stage(tm, LANES), _stage(tm, LANES)])
    res = pl.pallas_call(
        body, name="post", grid=(seq // tm,), in_specs=in_specs, out_specs=out_specs, out_shape=out_shape,
        scratch_shapes=scratch,
        compiler_params=pltpu.CompilerParams(dimension_semantics=("arbitrary",)),
    )(*ins)
    out = dict(g=res[0], doa=res[1], dl_a=res[2], dga=res[3], dob=res[4:4 + nd], lse_b=res[4 + nd:4 + 2 * nd],
               dl_b=res[4 + 2 * nd:4 + 3 * nd])
    rest = res[4 + 3 * nd:]
    out.update(dgb=rest[0], doc=rest[1], dl_c=rest[2], dgc=rest[3], gw_out=rest[4], gpost=rest[5], gsink=rest[6],
               loss=rest[7])
    return out


def _grad_w_in(ut, nat, res):
    seq = ut.shape[1]
    tm = min(ROW_TILE, seq)
    nd = len(B_DILS)
    nat_list = [nat[n] for n in _NATURAL]
    res_list = [a for n in _DILATED for a in res[n]]
    rope = _rope_tables(seq, tm)

    def body(rl_ref, rb_ref, ut_ref, *refs):
        nat_refs = dict(zip(_NATURAL, refs[:len(_NATURAL)]))
        refs = refs[len(_NATURAL):]
        res_refs = {n: refs[nd * k:nd * (k + 1)] for k, n in enumerate(_DILATED)}
        refs = refs[nd * len(_DILATED):]
        dproj_ref, gw_ref = refs[:2]
        bufs = {n: refs[2 + (nd - 1) * k:2 + (nd - 1) * (k + 1)] for k, n in enumerate(_DILATED)}

        @pl.when(pl.program_id(0) == 0)
        def _():
            gw_ref[...] = jnp.zeros_like(gw_ref)

        for n in _DILATED:
            for k in range(1, nd):
                _from_residues(res_refs[n][k], bufs[n][k - 1], B_DILS[k])
        c, sm, sp = _rope_coeffs(rl_ref, rb_ref)
        sm, sp = -sm, -sp
        for blk, (name, off, roped, scaled) in enumerate(_PROJ_LAYOUT):
            lanes = slice(off, off + LANES)
            if name in nat_refs:
                piece = nat_refs[name][:, lanes].astype(F32)
            else:
                piece = res_refs[name][0][0, :, lanes].astype(F32)
                for buf in bufs[name]:
                    piece = piece + buf[off // LANES]
            if roped:
                piece = _rope(piece, c, sm, sp)
            if scaled:
                piece = piece * SCALE
            dproj_ref[:, blk * LANES:(blk + 1) * LANES] = piece.astype(BF16)
        for j in range(N_CHIPS):
            gw_ref[j] += _dot(ut_ref[...], dproj_ref[:, j * SHARD_IN:(j + 1) * SHARD_IN])

    row = lambda w: pl.BlockSpec((tm, w), lambda i: (i, 0))
    in_specs = ([pl.BlockSpec(rope[0].shape, lambda i: (0, 0)), pl.BlockSpec((8, 2 * LANES), lambda i: (i, 0)),
                 pl.BlockSpec((D_MODEL, tm), lambda i: (0, i))]
                + [row(a.shape[1]) for a in nat_list]
                + [_residue_spec(d, tm, B_W) for _ in _DILATED for d in B_DILS])
    return pl.pallas_call(
        body, name="grad_w_in", grid=(seq // tm,), in_specs=in_specs,
        out_specs=[row(D_IN), pl.BlockSpec((N_CHIPS, D_MODEL, SHARD_IN), lambda i: (0, 0, 0))],
        out_shape=[jax.ShapeDtypeStruct((seq, D_IN), BF16), jax.ShapeDtypeStruct((N_CHIPS, D_MODEL, SHARD_IN), F32)],
        scratch_shapes=[_stage(tm, B_W)] * ((nd - 1) * len(_DILATED)),
        compiler_params=pltpu.CompilerParams(dimension_semantics=("arbitrary",)),
    )(*rope, ut, *nat_list, *res_list)


def _input_grad(x, g, pre_norm, w_in_g, dproj, gx_prev, span, after, name):
    seq = x.shape[0]
    tm = seq // INPUT_GRAD_TILES
    first_block, steps = span

    def body(*refs):
        x_ref, g_ref, gp_ref, w_ref, dp_ref = refs[:5]
        gx_ref, gpre_ref = refs[-2:]

        @pl.when(pl.program_id(0) == 0)
        def _():
            gpre_ref[...] = jnp.zeros_like(gpre_ref)

        du = None
        for j in range(N_CHIPS):
            term = _dot_nt(dp_ref[:, j * SHARD_IN:(j + 1) * SHARD_IN], w_ref[j])
            du = term if du is None else du + term
        xv = x_ref[...]
        r = lax.rsqrt(jnp.mean(xv * xv, axis=-1, keepdims=True) + RMS_EPS)
        xhat = xv * r
        gpre_ref[...] += jnp.sum(du * xhat, axis=0, keepdims=True)
        a = du * gp_ref[...]
        gx_ref[...] = g_ref[...] + r * (a - xhat * jnp.mean(a * xhat, axis=-1, keepdims=True))

    row = lambda w: pl.BlockSpec((tm, w), lambda i: (first_block + i, 0))
    full = lambda a: pl.BlockSpec(a.shape, lambda i: (0,) * a.ndim)
    any_spec = pl.BlockSpec(memory_space=pl.ANY)
    ins = [x, g, pre_norm, w_in_g, dproj]
    in_specs = [row(D_MODEL), row(D_MODEL), full(pre_norm), full(w_in_g), row(D_IN)]
    aliases = {}
    if gx_prev is not None:
        aliases[len(ins)] = 0
        ins.append(gx_prev)
        in_specs.append(any_spec)
    if after is not None:
        ins.append(after)
        in_specs.append(any_spec)
    return pl.pallas_call(
        body, name=name, grid=(steps,), in_specs=in_specs,
        out_specs=[row(D_MODEL), pl.BlockSpec((1, D_MODEL), lambda i: (0, 0))],
        out_shape=[jax.ShapeDtypeStruct((seq, D_MODEL), F32), jax.ShapeDtypeStruct((1, D_MODEL), F32)],
        input_output_aliases=aliases,
        compiler_params=pltpu.CompilerParams(dimension_semantics=("arbitrary",)),
    )(*ins)


def _exchange_start(ex, name):
    n_in, n_out, n_sem = len(ex["ins"]), len(ex["outs"]), len(ex["sems"])

    def body(*refs):
        in_refs, land_refs, sems = refs[:n_in], refs[n_in:n_in + n_out], refs[n_in + n_out:n_in + n_out + n_sem]
        ex["start"](in_refs, land_refs, *sems)
        token = refs[-1]
        token[...] = jnp.zeros_like(token)

    hbm = pl.BlockSpec(memory_space=pltpu.HBM)
    sem = pl.BlockSpec(memory_space=pltpu.SEMAPHORE)
    ins = [pltpu.with_memory_space_constraint(a, pltpu.HBM) for a in ex["ins"]]
    landing = [pltpu.with_memory_space_constraint(lax.empty(o.shape, o.dtype), pltpu.HBM) for o in ex["outs"]]
    res = pl.pallas_call(
        body, name=name,
        out_shape=list(ex["sems"]) + [pltpu.HBM(a.shape, a.dtype) for a in ex["ins"]]
        + [pltpu.HBM(o.shape, o.dtype) for o in ex["outs"]] + [jax.ShapeDtypeStruct((8, LANES), F32)],
        in_specs=[hbm] * (n_in + n_out),
        out_specs=[sem] * n_sem + [hbm] * (n_in + n_out) + [pl.BlockSpec(memory_space=pltpu.VMEM)],
        input_output_aliases={k: n_sem + k for k in range(n_in + n_out)},
        compiler_params=pltpu.CompilerParams(has_side_effects=pltpu.SideEffectType.DATAFLOW_SIDE_EFFECTING),
    )(*ins, *landing)
    return res[:-1], res[-1]


def _exchange_wait(ex, handles, after, name):
    n_in, n_out, n_sem = len(ex["ins"]), len(ex["outs"]), len(ex["sems"])
    sems, thru = handles[:n_sem], handles[n_sem:]

    def body(*refs):
        in_refs, land_refs = refs[:n_in], refs[n_in:n_in + n_out]
        sem_refs = refs[n_in + n_out:n_in + n_out + n_sem]
        ex["finish"](in_refs, land_refs, *sem_refs)

    hbm = pl.BlockSpec(memory_space=pltpu.HBM)
    sem = pl.BlockSpec(memory_space=pltpu.SEMAPHORE)
    res = pl.pallas_call(
        body, name=name,
        out_shape=[pltpu.HBM(a.shape, a.dtype) for a in thru],
        in_specs=[hbm] * (n_in + n_out) + [sem] * n_sem + [pl.BlockSpec(memory_space=pl.ANY)],
        out_specs=[hbm] * (n_in + n_out),
        input_output_aliases={k: k for k in range(n_in + n_out)},
        compiler_params=pltpu.CompilerParams(has_side_effects=pltpu.SideEffectType.DATAFLOW_SIDE_EFFECTING),
    )(*thru, *sems, after)
    return res[:n_in], res[n_in:]


def _start_finish(build):
    def start(*refs):
        for cp in build(*refs):
            cp.start()

    def finish(*refs):
        for cp in build(*refs):
            cp.wait()

    return dict(start=start, finish=finish)


def _pair_exchange(grads):
    n = len(grads)

    def build(srcs, outs, send_sems, recv_sems):
        x, y, c = lax.axis_index("x"), lax.axis_index("y"), lax.axis_index("c")
        copies = []
        for t in range(n):
            rows = grads[t].shape[1] // 2
            copies.append(pltpu.make_async_remote_copy(
                src_ref=srcs[t].at[:, pl.ds((1 - c) * rows, rows)], dst_ref=outs[t],
                send_sem=send_sems.at[t], recv_sem=recv_sems.at[t], device_id=(x, y, 1 - c), device_id_type=MESH))
        return copies

    return dict(ins=list(grads), **_start_finish(build),
                outs=[jax.ShapeDtypeStruct((g.shape[0], g.shape[1] // 2, g.shape[2]), g.dtype) for g in grads],
                sems=[pltpu.SemaphoreType.DMA((n,)), pltpu.SemaphoreType.DMA((n,))])


def _pair_add(core, own, got):
    nchip, rows2, width = own.shape
    rows = rows2 // 2
    tr = min(ROW_TILE, rows)
    nb = rows // tr

    def body(core_ref, own_ref, got_ref, out_ref):
        out_ref[...] = (own_ref[...] + got_ref[...]).astype(BF16)

    grid_spec = pltpu.PrefetchScalarGridSpec(
        num_scalar_prefetch=1, grid=(nchip, nb),
        in_specs=[pl.BlockSpec((None, tr, width), lambda k, i, core_ref: (k, core_ref[0] * nb + i, 0)),
                  pl.BlockSpec((None, tr, width), lambda k, i, core_ref: (k, i, 0))],
        out_specs=pl.BlockSpec((None, tr, width), lambda k, i, core_ref: (k, i, 0)))
    return pl.pallas_call(
        body, name=f"pair_add_{width}", grid_spec=grid_spec,
        out_shape=jax.ShapeDtypeStruct((nchip, rows, width), BF16),
    )(core, own, got)


def _chip_exchange(parts):
    n = len(parts)

    def build(srcs, outs, send_sems, recv_sems, local_sems):
        x, y, c = lax.axis_index("x"), lax.axis_index("y"), lax.axis_index("c")
        my_chip = 2 * x + y
        chips = [(1 - x, y), (x, 1 - y), (1 - x, 1 - y)]
        copies = [pltpu.make_async_copy(srcs[t].at[my_chip], outs[t].at[my_chip], local_sems.at[t]) for t in range(n)]
        for j, (cx, cy) in enumerate(chips):
            for t in range(n):
                k = n * j + t
                copies.append(pltpu.make_async_remote_copy(
                    src_ref=srcs[t].at[2 * cx + cy], dst_ref=outs[t].at[my_chip], send_sem=send_sems.at[k],
                    recv_sem=recv_sems.at[k], device_id=(cx, cy, c), device_id_type=MESH))
        return copies

    return dict(ins=list(parts), **_start_finish(build), outs=[jax.ShapeDtypeStruct(p.shape, p.dtype) for p in parts],
                sems=[pltpu.SemaphoreType.DMA((3 * n,)), pltpu.SemaphoreType.DMA((3 * n,)),
                      pltpu.SemaphoreType.DMA((n,))])


def _slot_sum(slots, name, core=None):
    ns, rows, width = slots.shape
    tr = min(ROW_TILE, rows)

    def body(*refs):
        in_ref, out_ref = refs[-2:]
        acc = in_ref[0].astype(F32)
        for s in range(1, ns):
            acc = acc + in_ref[s].astype(F32)
        out_ref[...] = acc

    if core is None:
        return pl.pallas_call(
            body, name=name, grid=(rows // tr,),
            in_specs=[pl.BlockSpec((ns, tr, width), lambda i: (0, i, 0))],
            out_specs=pl.BlockSpec((tr, width), lambda i: (i, 0)),
            out_shape=jax.ShapeDtypeStruct((rows, width), F32),
        )(slots)
    grid_spec = pltpu.PrefetchScalarGridSpec(
        num_scalar_prefetch=1, grid=(rows // tr,),
        in_specs=[pl.BlockSpec((ns, tr, width), lambda i, core_ref: (0, i, 0))],
        out_specs=pl.BlockSpec((None, tr, width), lambda i, core_ref: (core_ref[0], i, 0)))
    return pl.pallas_call(
        body, name=name, grid_spec=grid_spec, out_shape=jax.ShapeDtypeStruct((2, rows, width), F32),
    )(core, slots)


def _pair_gather(bufs, small):
    n = len(bufs)

    def body(*refs):
        small_ref, outs, small_out = refs[n], refs[n + 1:2 * n + 1], refs[2 * n + 1]
        send_sems, recv_sems, local_sem = refs[2 * n + 2:]
        x, y, c = lax.axis_index("x"), lax.axis_index("y"), lax.axis_index("c")
        me = 4 * x + 2 * y + c
        chips = [(1 - x, y), (x, 1 - y), (1 - x, 1 - y)]
        mine = pltpu.make_async_copy(small_ref, small_out.at[me], local_sem)
        mine.start()
        copies = [pltpu.make_async_remote_copy(
            src_ref=outs[t].at[c], dst_ref=outs[t].at[c], send_sem=send_sems.at[t], recv_sem=recv_sems.at[t],
            device_id=(x, y, 1 - c), device_id_type=MESH) for t in range(n)]
        peers = [(x, y, 1 - c)] + [(cx, cy, cc) for (cx, cy) in chips for cc in (c, 1 - c)]
        for j, peer in enumerate(peers):
            copies.append(pltpu.make_async_remote_copy(
                src_ref=small_ref, dst_ref=small_out.at[me], send_sem=send_sems.at[n + j],
                recv_sem=recv_sems.at[n + j], device_id=peer, device_id_type=MESH))
        for cp in copies:
            cp.start()
        for cp in copies:
            cp.wait()
        mine.wait()

    any_spec = pl.BlockSpec(memory_space=pl.ANY)
    res = pl.pallas_call(
        body, name="pair_gather",
        out_shape=[jax.ShapeDtypeStruct(b.shape, b.dtype) for b in bufs]
        + [jax.ShapeDtypeStruct((8,) + small.shape, small.dtype)],
        in_specs=[any_spec] * (n + 1), out_specs=[any_spec] * (n + 1),
        input_output_aliases={t: t for t in range(n)},
        scratch_shapes=[pltpu.SemaphoreType.DMA((n + 7,)), pltpu.SemaphoreType.DMA((n + 7,)),
                        pltpu.SemaphoreType.DMA],
    )(*bufs, small)
    return [r.reshape(2 * b.shape[1], b.shape[2]) for r, b in zip(res[:n], bufs)], res[n]


def _adamw(w, g, m, v, name):
    rows, width = w.shape
    tr = min(ROW_TILE // 2, rows)
    c1 = 1.0 / (1.0 - ADAM_B1 ** ADAM_STEP)
    c2 = 1.0 / (1.0 - ADAM_B2 ** ADAM_STEP)

    def body(w_ref, g_ref, m_ref, v_ref, d_ref, nm_ref, nv_ref):
        gv = g_ref[...]
        nm = ADAM_B1 * m_ref[...] + (1.0 - ADAM_B1) * gv
        nv = ADAM_B2 * v_ref[...] + (1.0 - ADAM_B2) * (gv * gv)
        nm_ref[...] = nm
        nv_ref[...] = nv
        d_ref[...] = -ADAM_LR * ((nm * c1) / (jnp.sqrt(nv * c2) + ADAM_EPS) + ADAM_WD * w_ref[...])

    spec = pl.BlockSpec((tr, width), lambda i: (i, 0))
    return pl.pallas_call(
        body, name=name, grid=(rows // tr,), in_specs=[spec] * 4, out_specs=[spec] * 3,
        out_shape=[jax.ShapeDtypeStruct(w.shape, F32)] * 3,
    )(w, g, m, v)


def _local_step(x, mem, target, pre_norm, sink_a, mem_norm, post_norm, w_in_g, w_out, w_mkv, gathers=None):
    first_gather, late_gather = gathers if gathers else (None, None)
    u, ut, hosted = _pre_norm(x, pre_norm, first_gather)
    if gathers:
        w_in_g = hosted[0].reshape(N_CHIPS, D_MODEL, SHARD_IN)
    pr = _pre_proj(u, w_in_g, late_gather)
    pr["ut"] = ut
    if gathers:
        w_out, w_mkv = (g.reshape(D_MODEL, g.shape[-1]) for g in pr["hosted"])
    mk, mv = _mem_kv(mem, mem_norm, w_mkv)
    sink = sink_a.reshape(-1)
    qa, ka, va = pr["qa"][None], pr["ka"][None], pr["va"][None]
    oa, lse_a = _band_fwd(qa, ka, va, sink, max_dist=A_WINDOW - 1, name="swa_fwd")
    ob_list, lseb_list = [], []
    for k, (win, dil) in enumerate(B_CONFIGS):
        o_i, l_i = _band_fwd(pr["qb"][k], pr["kb"][k], pr["vb"][k], None, max_dist=win // dil, name=f"dil{dil}_fwd")
        ob_list.append(o_i)
        lseb_list.append(l_i)
    oc, lse_c = _mem_attn_fwd(pr["qc"], mk, mv)
    sink_row = jnp.pad(sink, (0, LANES - sink.shape[0])).reshape(1, LANES)
    po = _post(x, target, post_norm, w_out, sink_row, oa[0], lse_a[0], pr["ga"], ob_list, lseb_list, pr["gb"], oc,
               pr["gc"])
    dqc, dmk, dmv = _mem_attn_bwd(pr["qc"], mk, mv, po["doc"], lse_c, po["dl_c"])
    dqa, dka, dva = _band_bwd(qa, ka, va, po["doa"][None], lse_a, po["dl_a"][None], max_dist=A_WINDOW - 1,
                              name="swa_bwd")
    res = dict(qb=[], kb=[], vb=[])
    for k, (win, dil) in enumerate(B_CONFIGS):
        dq_i, dk_i, dv_i = _band_bwd(pr["qb"][k], pr["kb"][k], pr["vb"][k], po["dob"][k], po["lse_b"][k],
                                     po["dl_b"][k], max_dist=win // dil, name=f"dil{dil}_bwd")
        res["qb"].append(dq_i)
        res["kb"].append(dk_i)
        res["vb"].append(dv_i)
    nat = dict(qa=dqa[0], ka=dka[0], va=dva[0], ga=po["dga"], gb=po["dgb"], qc=dqc, gc=po["dgc"])
    dproj, gw_in = _grad_w_in(pr["ut"], nat, res)
    gw_mkv, gmem = _mem_kv_bwd(mem, mem_norm, w_mkv, dmk, dmv)
    gsink = -po["gsink"][0, :sink.shape[0]]
    return dict(loss=po["loss"], g=po["g"], dproj=dproj, gw_in=gw_in, gw_out=po["gw_out"], gw_mkv=gw_mkv,
                gpost=po["gpost"], gmem=gmem, gsink=gsink, w_in_g=w_in_g)


def kernel(x, mem, pre_norm, w_in, sink_a, mem_norm, w_mem_kv, w_out, post_norm, loss_target, m_pre_norm, m_w_in, m_sink_a, m_mem_norm, m_w_mem_kv, m_w_out, m_post_norm, v_pre_norm, v_w_in, v_sink_a, v_mem_norm, v_w_mem_kv, v_w_out, v_post_norm):
    gathers = (_gather_exchange([w_in[0].astype(BF16)]),
               _gather_exchange([w_out[0].astype(BF16), w_mem_kv[0].astype(BF16)]))
    loc = _local_step(x[0], mem[0], loss_target[0], pre_norm, sink_a, mem_norm, post_norm, None, None, None, gathers)
    big = [loc["gw_in"], loc["gw_out"].reshape(N_CHIPS, D_MODEL // N_CHIPS, D_MODEL),
           loc["gw_mkv"].reshape(N_CHIPS, D_MODEL // N_CHIPS, 2 * C_W)]
    core = lax.axis_index("c").astype(jnp.int32).reshape(1)
    w_in_full = loc["w_in_g"]
    step_in = (x[0], loc["g"], pre_norm, w_in_full, loc["dproj"])
    pair_ex = _pair_exchange(big)
    pair_handles, token = _exchange_start(pair_ex, "pair_exchange_start")
    gx_a, gpre_a = _input_grad(*step_in, None, (0, 2), token, "input_grad_a")
    big, got = _exchange_wait(pair_ex, pair_handles, gpre_a, "pair_exchange_wait")
    parts = [_pair_add(core, own, g) for own, g in zip(big, got)]
    chip_ex = _chip_exchange(parts)
    chip_handles, token = _exchange_start(chip_ex, "chip_exchange_start")
    grad_x, gpre_b = _input_grad(*step_in, gx_a, (2, 14), token, "input_grad_b")
    _, slots = _exchange_wait(chip_ex, chip_handles, gpre_b, "chip_exchange_wait")
    halves = [_slot_sum(s, name=f"chip_sum_{s.shape[2]}", core=core) for s in slots]
    widen = lambda a: jnp.pad(a.reshape(1, -1), ((0, 0), (0, D_MODEL - a.size)))
    small = jnp.concatenate([gpre_a, loc["gpost"], loc["gmem"], widen(loc["gsink"]), widen(loc["loss"]), gpre_b,
                             jnp.zeros((2, D_MODEL), F32)], axis=0)
    (g_in, g_out, g_mkv), small_slots = _pair_gather(halves, small)
    small_sum = _slot_sum(small_slots, name="device_sum")
    g_pre, g_post, g_mem = small_sum[0:1] + small_sum[5:6], small_sum[1:2], small_sum[2:3]
    g_sink = small_sum[3:4, :sink_a.shape[1]]
    loss = small_sum[4, 0]

    d_in, nm_in, nv_in = _adamw(w_in[0], g_in, m_w_in[0], v_w_in[0], "adamw_in")
    d_out, nm_out, nv_out = _adamw(w_out[0], g_out, m_w_out[0], v_w_out[0], "adamw_out")
    d_mkv, nm_mkv, nv_mkv = _adamw(w_mem_kv[0], g_mkv, m_w_mem_kv[0], v_w_mem_kv[0], "adamw_mkv")
    pad6 = lambda a: jnp.pad(a, ((0, 0), (0, D_MODEL - a.shape[1])))
    stack = lambda a, b, c_, d_: jnp.concatenate([a, b, c_, pad6(d_), jnp.zeros((4, D_MODEL), F32)], axis=0)
    d_s, nm_s, nv_s = _adamw(stack(pre_norm, post_norm, mem_norm, sink_a),
                             jnp.concatenate([g_pre, small_sum[1:]], axis=0),
                             stack(m_pre_norm, m_post_norm, m_mem_norm, m_sink_a),
                             stack(v_pre_norm, v_post_norm, v_mem_norm, v_sink_a), "adamw_small")
    ns_ = sink_a.shape[1]
    unpack = lambda a: (a[0:1], a[3:4, :ns_], a[2:3], a[1:2])
    d_pre, d_sink, d_mem, d_post = unpack(d_s)
    nm_pre, nm_sink, nm_mem, nm_post = unpack(nm_s)
    nv_pre, nv_sink, nv_mem, nv_post = unpack(nv_s)
    lead = lambda a: a[None]
    return (loss, lead(grad_x),
            g_pre, lead(g_in), g_sink, g_mem, lead(g_mkv), lead(g_out), g_post,
            d_pre, lead(d_in), d_sink, d_mem, lead(d_mkv), lead(d_out), d_post,
            nm_pre, lead(nm_in), nm_sink, nm_mem, lead(nm_mkv), lead(nm_out), nm_post,
            nv_pre, lead(nv_in), nv_sink, nv_mem, lead(nv_mkv), lead(nv_out), nv_post)
```

```python
import numpy as np
import jax
import jax.numpy as jnp
from jax import lax
from jax.experimental import pallas as pl
from jax.experimental.pallas import tpu as pltpu

F32 = jnp.float32
BF16 = jnp.bfloat16

D_MODEL = 1024
HEAD_DIM = 64
LANES = 128
BLOCK = 128
ROW_TILE = 512
ATTN_TILE = 1024
INPUT_GRAD_TILES = 16
A_W, A_KV_W, B_W, C_W = 384, 128, 384, 256
N_MEM = 256
D_IN = 3072
N_CHIPS = 4
SHARD_IN = D_IN // N_CHIPS
B_CONFIGS = ((128, 1), (512, 4), (2048, 16))
B_DILS = tuple(d for _, d in B_CONFIGS)
A_WINDOW = 128
RMS_EPS = 1e-6
ROPE_THETA = 500000.0
SCALE = HEAD_DIM ** -0.5
NEG = -1e30
ADAM_LR, ADAM_B1, ADAM_B2, ADAM_EPS, ADAM_WD, ADAM_STEP = 0.001, 0.9, 0.999, 1e-08, 0.01, 10

NT = (((1,), (1,)), ((), ()))
TN = (((0,), (0,)), ((), ()))
MESH = pl.DeviceIdType.MESH

_PROJ_LAYOUT = (
    [("qa", 128 * i, True, True) for i in range(3)] + [("ka", 0, True, False), ("va", 0, False, False)]
    + [("ga", 128 * i, False, False) for i in range(3)]
    + [("qb", 128 * i, True, True) for i in range(3)] + [("kb", 128 * i, True, False) for i in range(3)]
    + [("vb", 128 * i, False, False) for i in range(3)] + [("gb", 128 * i, False, False) for i in range(3)]
    + [("qc", 128 * i, False, True) for i in range(2)] + [("gc", 128 * i, False, False) for i in range(2)]
)
_PROJ_WIDTH = dict(qa=A_W, ka=A_KV_W, va=A_KV_W, ga=A_W, qb=B_W, kb=B_W, vb=B_W, gb=B_W, qc=C_W, gc=C_W)
_NATURAL = ("qa", "ka", "va", "ga", "gb", "qc", "gc")
_DILATED = ("qb", "kb", "vb")


def _dot(a, b):
    return jnp.dot(a, b, preferred_element_type=F32)


def _dot_nt(a, b):
    return lax.dot_general(a, b, NT, preferred_element_type=F32)


def _dot_tn(a, b):
    return lax.dot_general(a, b, TN, preferred_element_type=F32)


def _half_masks(rows):
    lane = lax.broadcasted_iota(jnp.int32, (rows, LANES), 1)
    return lane < HEAD_DIM, lane >= HEAD_DIM


def _rope(t, c, sm, sp):
    return t * c + pltpu.roll(t, LANES - 8, 1) * sm + pltpu.roll(t, 8, 1) * sp


def _rope_tables(seq, tm):
    dim = jnp.arange(LANES) % HEAD_DIM
    inv_freq = ROPE_THETA ** (-jnp.arange(0, 16, 2, dtype=F32) / 16)
    freq = jnp.where(dim < 16, inv_freq[dim % 8], 0.0)[None, :]
    local = jnp.arange(tm, dtype=F32)[:, None] * freq
    base = (jnp.arange(seq // tm, dtype=F32) * tm)[:, None] * freq
    both = lambda a: jnp.concatenate([jnp.cos(a), jnp.sin(a)], axis=1)
    return both(local), jnp.repeat(both(base), 8, axis=0)


def _rope_coeffs(local_ref, base_ref):
    cl, sl = local_ref[:, :LANES], local_ref[:, LANES:]
    cb, sb = base_ref[0:1, :LANES], base_ref[0:1, LANES:]
    cos = cb * cl - sb * sl
    sin = sb * cl + cb * sl
    dim = lax.broadcasted_iota(jnp.int32, (1, LANES), 1) % HEAD_DIM
    return cos, jnp.where(dim < 8, -sin, 0.0), jnp.where((dim >= 8) & (dim < 16), sin, 0.0)


def _split3(x):
    a = x.astype(BF16)
    r = x - a.astype(F32)
    b = r.astype(BF16)
    c = (r - b.astype(F32)).astype(BF16)
    return a, b, c


def _rows_to_lanes(x):
    row = lax.broadcasted_iota(jnp.int32, (8, LANES), 0)
    lane = lax.broadcasted_iota(jnp.int32, (8, LANES), 1)
    eye = (row == lane).astype(BF16)
    a, b, c = _split3(x)
    return _dot_nt(eye, a) + _dot_nt(eye, b) + _dot_nt(eye, c)


def _head_sum_matrix(width):
    k = lax.broadcasted_iota(jnp.int32, (width, LANES), 0)
    h = lax.broadcasted_iota(jnp.int32, (width, LANES), 1)
    return (k // HEAD_DIM == h).astype(BF16)


def _head_expand_matrix(width):
    h = lax.broadcasted_iota(jnp.int32, (LANES, width), 0)
    k = lax.broadcasted_iota(jnp.int32, (LANES, width), 1)
    return (k // HEAD_DIM == h).astype(BF16)


def _dot_split(x, mat, terms):
    parts = _split3(x)[:terms]
    out = _dot(parts[0], mat)
    for p in parts[1:]:
        out = out + _dot(p, mat)
    return out


def _per_head(cols, fill=0.0):
    rows = cols[0].shape[0]
    lane = lax.broadcasted_iota(jnp.int32, (rows, LANES), 1)
    out = jnp.full((rows, LANES), fill, F32)
    for h, col in enumerate(cols):
        out = jnp.where(lane == h, col, out)
    return out


def _lane_blocks(width):
    return [slice(p * LANES, (p + 1) * LANES) for p in range(width // LANES)]


def _stage(rows, width):
    return pltpu.VMEM((width // LANES, rows, LANES), F32)


def _stage_write(buf, value):
    for p, lanes in enumerate(_lane_blocks(value.shape[1])):
        buf[p] = value[:, lanes]


def _stage_read(buf):
    return jnp.concatenate([buf[p] for p in range(buf.shape[0])], axis=1) if buf.shape[0] > 1 else buf[0]


def _to_residues(buf, out_ref, dil):
    rows = buf.shape[1] // dil
    for r in range(dil):
        for p in range(buf.shape[0]):
            plane = buf.at[p]
            out_ref[r, :, p * LANES:(p + 1) * LANES] = plane[pl.ds(r, rows, stride=dil), :].astype(out_ref.dtype)


def _from_residues(in_ref, buf, dil):
    rows = buf.shape[1] // dil
    for r in range(dil):
        for p in range(buf.shape[0]):
            plane = buf.at[p]
            plane[pl.ds(r, rows, stride=dil), :] = in_ref[r, :, p * LANES:(p + 1) * LANES].astype(F32)


def _residue_spec(dil, tm, width):
    return pl.BlockSpec((dil, tm // dil, width), lambda i: (0, i, 0))


def _gather_exchange(shards_2d):
    shards = tuple(s.reshape(2, s.shape[0] // 2, s.shape[1]) for s in shards_2d)
    n = len(shards)

    def copies(in_refs, out_refs, send_sems, recv_sems):
        srcs, outs = in_refs[:n], out_refs
        x, y, c = lax.axis_index("x"), lax.axis_index("y"), lax.axis_index("c")
        my_chip = 2 * x + y
        sibling = (x, y, 1 - c)
        chips = [(1 - x, y), (x, 1 - y), (1 - x, 1 - y)]

        def copy(k, src, dst, to):
            return pltpu.make_async_remote_copy(src_ref=src, dst_ref=dst, send_sem=send_sems.at[k],
                                                recv_sem=recv_sems.at[k], device_id=to, device_id_type=MESH)

        first, arrive, passed, sibling_arrive = [], [], [], []
        for j, (cx, cy) in enumerate(chips):
            chip = 2 * cx + cy
            for t in range(n):
                k = n * j + t
                first.append(copy(k, srcs[t].at[c], outs[t].at[my_chip, c], (cx, cy, c)))
                arrive.append(copy(k, srcs[t].at[c], outs[t].at[chip, c], (cx, cy, c)))
                passed.append(copy(n * 3 + k, outs[t].at[chip, c], outs[t].at[chip, c], sibling))
                sibling_arrive.append(copy(n * 3 + k, outs[t].at[chip, 1 - c], outs[t].at[chip, 1 - c], sibling))
        return first, arrive, passed, sibling_arrive

    def start(*refs):
        for cp in copies(*refs)[0]:
            cp.start()

    def finish(*refs):
        first, arrive, passed, sibling_arrive = copies(*refs)
        for got, fwd in zip(arrive, passed):
            got.wait_recv()
            fwd.start()
        for cp in sibling_arrive:
            cp.wait_recv()
        for cp in first + passed:
            cp.wait_send()

    my_chip = 2 * lax.axis_index("x") + lax.axis_index("y")
    landing = [lax.dynamic_update_slice(jnp.zeros((N_CHIPS,) + s.shape, s.dtype), s[None], (my_chip, 0, 0, 0))
               for s in shards]
    return dict(ins=list(shards) + landing, start=start, finish=finish, aliases={n + t: t for t in range(n)},
                outs=[jax.ShapeDtypeStruct((N_CHIPS,) + s.shape, s.dtype) for s in shards],
                sems=[pltpu.SemaphoreType.DMA((6 * n,)), pltpu.SemaphoreType.DMA((6 * n,))])


def _mem_kv(mem, mem_norm, w_mkv):
    def body(mem_ref, g_ref, w_ref, mk_ref, mv_ref):
        m = mem_ref[...]
        r = lax.rsqrt(jnp.mean(m * m, axis=-1, keepdims=True) + RMS_EPS)
        mn = (m * r * g_ref[...]).astype(BF16)
        kv = _dot(mn, w_ref[...])
        mk_ref[...] = kv[:, :C_W].astype(BF16)
        mv_ref[...] = kv[:, C_W:].astype(BF16)

    return pl.pallas_call(
        body, name="mem_kv",
        out_shape=[jax.ShapeDtypeStruct((N_MEM, C_W), BF16)] * 2,
    )(mem, mem_norm, w_mkv)


def _mem_kv_bwd(mem, mem_norm, w_mkv, dmk, dmv):
    def body(mem_ref, g_ref, w_ref, dmk_ref, dmv_ref, gw_ref, gn_ref):
        m = mem_ref[...]
        r = lax.rsqrt(jnp.mean(m * m, axis=-1, keepdims=True) + RMS_EPS)
        mhat = m * r
        mn = (mhat * g_ref[...]).astype(BF16)
        dkv = jnp.concatenate([dmk_ref[...], dmv_ref[...]], axis=1).astype(BF16)
        gw_ref[...] = _dot_tn(mn, dkv)
        dmn = _dot_nt(dkv, w_ref[...])
        gn_ref[...] = jnp.sum(dmn * mhat, axis=0, keepdims=True)

    return pl.pallas_call(
        body, name="mem_kv_bwd",
        out_shape=[jax.ShapeDtypeStruct((D_MODEL, 2 * C_W), F32), jax.ShapeDtypeStruct((1, D_MODEL), F32)],
    )(mem, mem_norm, w_mkv, dmk, dmv)


def _pre_norm(x, pre_norm, host=None):
    seq = x.shape[0]
    tm = min(ROW_TILE, seq)
    n_host_in = len(host["ins"]) if host else 0
    n_host_out = len(host["outs"]) if host else 0

    def body(x_ref, g_ref, *refs):
        host_in, (u_ref, ut_ref), refs = refs[:n_host_in], refs[n_host_in:n_host_in + 2], refs[n_host_in + 2:]
        host_out, sems = refs[:n_host_out], refs[n_host_out:]
        if host:
            @pl.when(pl.program_id(0) == 0)
            def _():
                host["start"](host_in, host_out, *sems)

        xv = x_ref[...]
        r = lax.rsqrt(jnp.mean(xv * xv, axis=-1, keepdims=True) + RMS_EPS)
        u = xv * r * g_ref[...]
        u_ref[...] = u.astype(BF16)
        ut_ref[...] = u.T.astype(BF16)
        if host:
            @pl.when(pl.program_id(0) == seq // tm - 1)
            def _():
                host["finish"](host_in, host_out, *sems)

    any_spec = pl.BlockSpec(memory_space=pl.ANY)
    ins = [x, pre_norm]
    in_specs = [pl.BlockSpec((tm, D_MODEL), lambda i: (i, 0)), pl.BlockSpec(pre_norm.shape, lambda i: (0, 0))]
    out_shape = [jax.ShapeDtypeStruct((seq, D_MODEL), BF16), jax.ShapeDtypeStruct((D_MODEL, seq), BF16)]
    out_specs = [pl.BlockSpec((tm, D_MODEL), lambda i: (i, 0)), pl.BlockSpec((D_MODEL, tm), lambda i: (0, i))]
    aliases, scratch = {}, []
    if host:
        aliases = {len(ins) + k: 2 + v for k, v in host.get("aliases", {}).items()}
        ins += list(host["ins"])
        in_specs += [any_spec] * n_host_in
        out_shape += list(host["outs"])
        out_specs += [any_spec] * n_host_out
        scratch = list(host["sems"])
    res = pl.pallas_call(
        body, name="pre_norm", grid=(seq // tm,), in_specs=in_specs, out_specs=out_specs, out_shape=out_shape,
        input_output_aliases=aliases, scratch_shapes=scratch,
        compiler_params=pltpu.CompilerParams(dimension_semantics=("arbitrary",)),
    )(*ins)
    return res[0], res[1], res[2:]


def _pre_proj(u, w_in_g, host=None):
    seq = u.shape[0]
    tm = min(ROW_TILE, seq)
    n_nat, n_dil = len(_NATURAL), len(_DILATED) * len(B_DILS)
    rope = _rope_tables(seq, tm)

    n_host_in = len(host["ins"]) if host else 0
    n_host_out = len(host["outs"]) if host else 0
    n_own_out = n_nat + n_dil

    def body(u_ref, w_ref, rl_ref, rb_ref, *refs):
        host_in, refs = refs[:n_host_in], refs[n_host_in:]
        nat = dict(zip(_NATURAL, refs[:n_nat]))
        res = {n: refs[n_nat + len(B_DILS) * k:n_nat + len(B_DILS) * (k + 1)] for k, n in enumerate(_DILATED)}
        host_out = refs[n_own_out:n_own_out + n_host_out]
        bufs = dict(zip(_DILATED, refs[n_own_out + n_host_out:]))
        sems = refs[n_own_out + n_host_out + len(_DILATED):]
        if host:
            @pl.when(pl.program_id(0) == 0)
            def _():
                host["start"](host_in, host_out, *sems)

        ub = u_ref[...]
        c, sm, sp = _rope_coeffs(rl_ref, rb_ref)
        for j in range(N_CHIPS):
            pj = _dot(ub, w_ref[j])
            for b in range(SHARD_IN // LANES):
                name, off, roped, scaled = _PROJ_LAYOUT[(SHARD_IN // LANES) * j + b]
                piece = pj[:, LANES * b:LANES * (b + 1)]
                if roped:
                    piece = _rope(piece, c, sm, sp)
                if scaled:
                    piece = piece * SCALE
                if name in bufs:
                    bufs[name][off // LANES] = piece
                else:
                    nat[name][:, off:off + LANES] = piece.astype(BF16)
        for name in _DILATED:
            for ref, dil in zip(res[name], B_DILS):
                _to_residues(bufs[name], ref, dil)
        if host:
            @pl.when(pl.program_id(0) == seq // tm - 1)
            def _():
                host["finish"](host_in, host_out, *sems)

    row = lambda w: pl.BlockSpec((tm, w), lambda i: (i, 0))
    full = lambda a: pl.BlockSpec(a.shape, lambda i: (0,) * a.ndim)
    any_spec = pl.BlockSpec(memory_space=pl.ANY)
    out_shape = [jax.ShapeDtypeStruct((seq, _PROJ_WIDTH[n]), BF16) for n in _NATURAL]
    out_specs = [row(_PROJ_WIDTH[n]) for n in _NATURAL]
    for n in _DILATED:
        for dil in B_DILS:
            out_shape.append(jax.ShapeDtypeStruct((dil, seq // dil, B_W), BF16))
            out_specs.append(_residue_spec(dil, tm, B_W))
    ins = [u, w_in_g, *rope]
    in_specs = [row(D_MODEL), full(w_in_g), full(rope[0]), pl.BlockSpec((8, 2 * LANES), lambda i: (i, 0))]
    scratch = [_stage(tm, B_W)] * len(_DILATED)
    aliases = {}
    if host:
        aliases = {len(ins) + k: n_own_out + v for k, v in host.get("aliases", {}).items()}
        ins += list(host["ins"])
        in_specs += [any_spec] * n_host_in
        out_shape += list(host["outs"])
        out_specs += [any_spec] * n_host_out
        scratch += list(host["sems"])
    res = pl.pallas_call(
        body, name="pre_proj", grid=(seq // tm,), in_specs=in_specs, out_specs=out_specs, out_shape=out_shape,
        input_output_aliases=aliases, scratch_shapes=scratch,
        compiler_params=pltpu.CompilerParams(dimension_semantics=("arbitrary",)),
    )(*ins)
    out = dict(zip(_NATURAL, res[:n_nat]))
    for k, n in enumerate(_DILATED):
        out[n] = res[n_nat + len(B_DILS) * k:n_nat + len(B_DILS) * (k + 1)]
    out["hosted"] = res[n_own_out:]
    return out


def _band_bias(max_dist, transposed):
    i = np.arange(BLOCK)[:, None]
    j = np.arange(BLOCK)[None, :]
    if transposed:
        same = i <= j
        other = (j + BLOCK - i) <= max_dist
        vis = np.concatenate([same, other], axis=1)
    else:
        prev = (i + BLOCK - j) <= max_dist
        same = j <= i
        vis = np.concatenate([prev, same], axis=1)
    return jnp.asarray(np.where(vis, 0.0, NEG).astype(np.float32))


def _kv_place(h, gqa):
    return (0, h // 3) if gqa else (h // 2, h % 2)


def _band_fwd(q, k, v, sink, *, max_dist, name):
    dil, length, wq = q.shape
    wk = k.shape[2]
    gqa = wk != wq
    tq = min(ATTN_TILE, length)
    ns, nt = tq // BLOCK, length // tq
    npair = wq // LANES
    bias = _band_bias(max_dist, transposed=False)
    has_sink = sink is not None

    def body(*refs):
        if has_sink:
            sink_ref, refs = refs[0], refs[1:]
        q_ref, k_ref, kp_ref, v_ref, vp_ref, bias_ref, o_ref, lse_ref, kbuf, vbuf = refs[:10]
        i = pl.program_id(1)
        kbuf[0:BLOCK] = kp_ref[...]
        kbuf[BLOCK:] = k_ref[...]
        vbuf[0:BLOCK] = vp_ref[...]
        vbuf[BLOCK:] = v_ref[...]
        if gqa:
            kroll, vroll = refs[10:12]
            kroll[...] = pltpu.roll(kbuf[...], HEAD_DIM, 1)
            vroll[...] = pltpu.roll(vbuf[...], HEAD_DIM, 1)
        half = _half_masks(BLOCK)
        col_prev = (lax.broadcasted_iota(jnp.int32, (1, 2 * BLOCK), 1) < BLOCK).astype(F32)

        def score_matmuls(a):
            scores = []
            for p in range(npair):
                qp = q_ref[a * BLOCK:(a + 1) * BLOCK, p * LANES:(p + 1) * LANES]
                for e in range(2):
                    pk, ek = _kv_place(2 * p + e, gqa)
                    kw = (kbuf if ek == e else kroll)[a * BLOCK:(a + 2) * BLOCK, pk * LANES:(pk + 1) * LANES]
                    scores.append(_dot_nt(jnp.where(half[e], qp, jnp.zeros_like(qp)), kw))
            return scores

        pending = score_matmuls(0)
        for a in range(ns):
            r0 = a * BLOCK
            b = bias_ref[...]
            if a == 0:
                b = b + jnp.where(i == 0, NEG, 0.0) * col_prev
            scores = pending
            m_cols, l_cols, probs = [], [], []
            for h, s in enumerate(scores):
                s = s + b
                m = jnp.max(s, axis=1, keepdims=True)
                if has_sink:
                    m = jnp.maximum(m, sink_ref[h])
                pe = jnp.exp(s - m)
                l = jnp.sum(pe, axis=1, keepdims=True)
                if has_sink:
                    l = l + jnp.exp(sink_ref[h] - m)
                probs.append(pe.astype(BF16))
                m_cols.append(m)
                l_cols.append(l)
            pending = score_matmuls(a + 1) if a + 1 < ns else None
            for p in range(npair):
                o_h = []
                for e in range(2):
                    h = 2 * p + e
                    pk, ek = _kv_place(h, gqa)
                    vw = (vbuf if ek == e else vroll)[r0:r0 + 2 * BLOCK, pk * LANES:(pk + 1) * LANES]
                    o_h.append(_dot(probs[h], vw) * (1.0 / l_cols[h]))
                o_ref[r0:r0 + BLOCK, p * LANES:(p + 1) * LANES] = jnp.where(half[0], o_h[0], o_h[1]).astype(BF16)
            lse_ref[r0:r0 + BLOCK, :] = _per_head(m_cols) + jnp.log(_per_head(l_cols, 1.0))

    main = lambda w: pl.BlockSpec((None, tq, w), lambda r, i: (r, i, 0))
    prev = lambda w: pl.BlockSpec((None, BLOCK, w), lambda r, i: (r, jnp.maximum(i * ns - 1, 0), 0))
    in_specs = [main(wq), main(wk), prev(wk), main(wk), prev(wk), pl.BlockSpec(bias.shape, lambda r, i: (0, 0))]
    args = [q, k, k, v, v, bias]
    if has_sink:
        in_specs = [pl.BlockSpec(memory_space=pltpu.SMEM)] + in_specs
        args = [sink] + args
    scratch = [pltpu.VMEM((tq + BLOCK, wk), BF16)] * (4 if gqa else 2)
    return pl.pallas_call(
        body, name=name, grid=(dil, nt), in_specs=in_specs,
        out_specs=[main(wq), main(LANES)],
        out_shape=[jax.ShapeDtypeStruct((dil, length, wq), BF16), jax.ShapeDtypeStruct((dil, length, LANES), F32)],
        scratch_shapes=scratch,
    )(*args)


def _band_bwd(q, k, v, do, lse, delta, *, max_dist, name):
    dil, length, wq = q.shape
    wk = k.shape[2]
    gqa = wk != wq
    tq = min(ATTN_TILE, length)
    ns, nt = tq // BLOCK, length // tq
    npair = wq // LANES
    nblocks = length // BLOCK
    bias = _band_bias(max_dist, transposed=True)

    def body(q_ref, qn_ref, do_ref, don_ref, lse_ref, lsen_ref, dl_ref, dln_ref, k_ref, v_ref, bias_ref,
             dq_ref, dk_ref, dv_ref, stat_l, stat_d, dqt, kt, *rolled):
        i = pl.program_id(1)
        for pk in range(wk // LANES):
            kt[pk] = k_ref[:, pk * LANES:(pk + 1) * LANES].astype(F32).T.astype(BF16)
        if gqa:
            kroll, vroll, ktroll = rolled
            kroll[...] = pltpu.roll(k_ref[...], HEAD_DIM, 1)
            vroll[...] = pltpu.roll(v_ref[...], HEAD_DIM, 1)
            ktroll[0] = kroll[...].astype(F32).T.astype(BF16)
        for a in range(ns):
            rows = slice(a * BLOCK, (a + 1) * BLOCK)
            stat_l[a] = _rows_to_lanes(lse_ref[rows, :])
            stat_d[a] = _rows_to_lanes(dl_ref[rows, :])
        stat_l[ns] = _rows_to_lanes(lsen_ref[...])
        stat_d[ns] = _rows_to_lanes(dln_ref[...])

        @pl.when(i == 0)
        def _():
            dqt[:, :, 0:BLOCK] = jnp.zeros((npair, LANES, BLOCK), F32)

        @pl.when(i > 0)
        def _():
            dqt[:, :, 0:BLOCK] = dqt[:, :, tq:tq + BLOCK]

        dqt[:, :, BLOCK:] = jnp.zeros((npair, LANES, tq), F32)
        half2 = _half_masks(2 * BLOCK)
        row = lax.broadcasted_iota(jnp.int32, (LANES, BLOCK), 0)
        row_half = (row < HEAD_DIM, row >= HEAD_DIM)
        col_next = (lax.broadcasted_iota(jnp.int32, (1, 2 * BLOCK), 1) >= BLOCK).astype(F32)

        def scores(b):
            rows = slice(b * BLOCK, (b + 1) * BLOCK)
            nxt_rows = slice((b + 1) * BLOCK, (b + 2) * BLOCK)
            items = []
            for p in range(npair):
                lanes = slice(p * LANES, (p + 1) * LANES)
                q_next = q_ref[nxt_rows, lanes] if b + 1 < ns else qn_ref[:, lanes]
                do_next = do_ref[nxt_rows, lanes] if b + 1 < ns else don_ref[:, lanes]
                qw = jnp.concatenate([q_ref[rows, lanes], q_next], axis=0)
                dow = jnp.concatenate([do_ref[rows, lanes], do_next], axis=0)
                for e in range(2):
                    h = 2 * p + e
                    pk, ek = _kv_place(h, gqa)
                    klanes = slice(pk * LANES, (pk + 1) * LANES)
                    kb = (k_ref if ek == e else kroll)[rows, klanes]
                    vb = (v_ref if ek == e else vroll)[rows, klanes]
                    qm = jnp.where(half2[e], qw, jnp.zeros_like(qw))
                    dom = jnp.where(half2[e], dow, jnp.zeros_like(dow))
                    items.append(dict(p=p, e=e, h=h, pk=pk, ek=ek, qm=qm, dom=dom,
                                      st=_dot_nt(kb, qm), dpt=_dot_nt(vb, dom)))
            return items

        def probs(b, items):
            bt = bias_ref[...]
            if b == ns - 1:
                bt = bt + jnp.where(i == nt - 1, NEG, 0.0) * col_next
            for it in items:
                h = it["h"]
                lrow = jnp.concatenate([stat_l[b, h:h + 1, :], stat_l[b + 1, h:h + 1, :]], axis=1)
                drow = jnp.concatenate([stat_d[b, h:h + 1, :], stat_d[b + 1, h:h + 1, :]], axis=1)
                pt = jnp.exp(it["st"] + bt - lrow)
                it["ptb"] = pt.astype(BF16)
                it["dsb"] = (pt * (it["dpt"] - drow)).astype(BF16)

        pending = scores(0)
        for b in range(ns):
            rows = slice(b * BLOCK, (b + 1) * BLOCK)
            window = slice(b * BLOCK, (b + 2) * BLOCK)
            acc = {}
            items = pending
            probs(b, items)
            pending = scores(b + 1) if b + 1 < ns else None
            for p in range(npair):
                pair = items[2 * p:2 * p + 2]
                lanes = slice(p * LANES, (p + 1) * LANES)
                kparts = []
                for it in pair:
                    kbt = (kt if it["ek"] == it["e"] else ktroll)[it["pk"], :, rows]
                    kparts.append(jnp.where(row_half[it["e"]], kbt, jnp.zeros_like(kbt)))
                ds_keys = jnp.concatenate([it["dsb"] for it in pair], axis=0)
                dqt[p, :, window] += _dot(jnp.concatenate(kparts, axis=1), ds_keys)
                if not gqa:
                    q_both = jnp.concatenate([it["qm"] for it in pair], axis=0)
                    do_both = jnp.concatenate([it["dom"] for it in pair], axis=0)
                    dk_ref[rows, lanes] = _dot(jnp.concatenate([it["dsb"] for it in pair], axis=1), q_both).astype(BF16)
                    dv_ref[rows, lanes] = _dot(jnp.concatenate([it["ptb"] for it in pair], axis=1), do_both).astype(BF16)
                else:
                    for it in pair:
                        dv_c = _dot(it["ptb"], it["dom"])
                        dk_c = _dot(it["dsb"], it["qm"])
                        key = (it["pk"], it["ek"] == it["e"])
                        if key in acc:
                            acc[key] = (acc[key][0] + dk_c, acc[key][1] + dv_c)
                        else:
                            acc[key] = (dk_c, dv_c)
            if gqa:
                dk_al, dv_al = acc[(0, True)]
                dk_mis, dv_mis = acc[(0, False)]
                dk_ref[rows, :] = (dk_al + pltpu.roll(dk_mis, HEAD_DIM, 1)).astype(BF16)
                dv_ref[rows, :] = (dv_al + pltpu.roll(dv_mis, HEAD_DIM, 1)).astype(BF16)

        for p in range(npair):
            dq_ref[:, p * LANES:(p + 1) * LANES] = dqt[p, :, 0:tq].T.astype(BF16)

    main = lambda w: pl.BlockSpec((None, tq, w), lambda r, i: (r, i, 0))
    nxt = lambda w: pl.BlockSpec((None, BLOCK, w), lambda r, i: (r, jnp.minimum((i + 1) * ns, nblocks - 1), 0))
    scratch = [pltpu.VMEM((ns + 1, 8, LANES), F32), pltpu.VMEM((ns + 1, 8, LANES), F32),
               pltpu.VMEM((npair, LANES, tq + BLOCK), F32), pltpu.VMEM((wk // LANES, LANES, tq), BF16)]
    if gqa:
        scratch = scratch + [pltpu.VMEM((tq, wk), BF16)] * 2 + [pltpu.VMEM((1, LANES, tq), BF16)]
    return pl.pallas_call(
        body, name=name, grid=(dil, nt),
        in_specs=[main(wq), nxt(wq), main(wq), nxt(wq), main(LANES), nxt(LANES), main(LANES), nxt(LANES),
                  main(wk), main(wk), pl.BlockSpec(bias.shape, lambda r, i: (0, 0))],
        out_specs=[main(wq), main(wk), main(wk)],
        out_shape=[jax.ShapeDtypeStruct((dil, length, wq), BF16), jax.ShapeDtypeStruct((dil, length, wk), BF16),
                   jax.ShapeDtypeStruct((dil, length, wk), BF16)],
        scratch_shapes=scratch,
        compiler_params=pltpu.CompilerParams(dimension_semantics=("arbitrary", "arbitrary")),
    )(q, q, do, do, lse, lse, delta, delta, k, v, bias)


def _mem_attn_fwd(q, mk, mv):
    seq = q.shape[0]
    tq = min(ATTN_TILE, seq)
    ns = tq // BLOCK

    def body(q_ref, mk_ref, mv_ref, o_ref, lse_ref):
        half = _half_masks(BLOCK)

        def sub(a, carry):
            r0 = pl.multiple_of(a * BLOCK, BLOCK)
            scores = []
            for p in range(C_W // LANES):
                lanes = slice(p * LANES, (p + 1) * LANES)
                qp = q_ref[pl.ds(r0, BLOCK), lanes]
                for e in range(2):
                    scores.append(_dot_nt(jnp.where(half[e], qp, jnp.zeros_like(qp)), mk_ref[:, lanes]))
            m_cols, l_cols, probs = [], [], []
            for s in scores:
                m = jnp.max(s, axis=1, keepdims=True)
                pe = jnp.exp(s - m)
                probs.append(pe.astype(BF16))
                m_cols.append(m)
                l_cols.append(jnp.sum(pe, axis=1, keepdims=True))
            for p in range(C_W // LANES):
                lanes = slice(p * LANES, (p + 1) * LANES)
                o_h = [_dot(probs[2 * p + e], mv_ref[:, lanes]) * (1.0 / l_cols[2 * p + e]) for e in range(2)]
                o_ref[pl.ds(r0, BLOCK), lanes] = jnp.where(half[0], o_h[0], o_h[1]).astype(BF16)
            lse_ref[pl.ds(r0, BLOCK), :] = _per_head(m_cols) + jnp.log(_per_head(l_cols, 1.0))
            return carry

        lax.fori_loop(0, ns, sub, 0, unroll=True)

    row = lambda w: pl.BlockSpec((tq, w), lambda i: (i, 0))
    full = pl.BlockSpec((N_MEM, C_W), lambda i: (0, 0))
    return pl.pallas_call(
        body, name="mem_attn_fwd", grid=(seq // tq,), in_specs=[row(C_W), full, full],
        out_specs=[row(C_W), row(LANES)],
        out_shape=[jax.ShapeDtypeStruct((seq, C_W), BF16), jax.ShapeDtypeStruct((seq, LANES), F32)],
    )(q, mk, mv)


def _mem_attn_bwd(q, mk, mv, do, lse, delta):
    seq = q.shape[0]
    tq = min(ATTN_TILE, seq)
    ns = tq // BLOCK
    npair = C_W // LANES

    def body(q_ref, mk_ref, mv_ref, do_ref, lse_ref, dl_ref, dq_ref, dmk_ref, dmv_ref, stat_l, stat_d, mkt, dqt):
        @pl.when(pl.program_id(0) == 0)
        def _():
            dmk_ref[...] = jnp.zeros_like(dmk_ref)
            dmv_ref[...] = jnp.zeros_like(dmv_ref)
            for p in range(npair):
                mkt[p] = mk_ref[:, p * LANES:(p + 1) * LANES].astype(F32).T.astype(BF16)

        for a in range(ns):
            rows = slice(a * BLOCK, (a + 1) * BLOCK)
            stat_l[a] = _rows_to_lanes(lse_ref[rows, :])
            stat_d[a] = _rows_to_lanes(dl_ref[rows, :])
        half = _half_masks(BLOCK)
        row = lax.broadcasted_iota(jnp.int32, (LANES, N_MEM), 0)
        row_half = (row < HEAD_DIM, row >= HEAD_DIM)

        for a in range(ns):
            rows = slice(a * BLOCK, (a + 1) * BLOCK)
            items = []
            for p in range(npair):
                lanes = slice(p * LANES, (p + 1) * LANES)
                qp = q_ref[rows, lanes]
                dop = do_ref[rows, lanes]
                for e in range(2):
                    qm = jnp.where(half[e], qp, jnp.zeros_like(qp))
                    dom = jnp.where(half[e], dop, jnp.zeros_like(dop))
                    items.append(dict(p=p, e=e, qm=qm, dom=dom, st=_dot_nt(mk_ref[:, lanes], qm),
                                      dpt=_dot_nt(mv_ref[:, lanes], dom)))
            for it in items:
                h = 2 * it["p"] + it["e"]
                pt = jnp.exp(it["st"] - stat_l[a, h:h + 1, :])
                it["ptb"] = pt.astype(BF16)
                it["dsb"] = (pt * (it["dpt"] - stat_d[a, h:h + 1, :])).astype(BF16)
            for p in range(npair):
                lanes = slice(p * LANES, (p + 1) * LANES)
                pair = [it for it in items if it["p"] == p]
                join = lambda name, axis: jnp.concatenate([it[name] for it in pair], axis=axis)
                dmv_ref[:, lanes] += _dot(join("ptb", 1), join("dom", 0))
                dmk_ref[:, lanes] += _dot(join("dsb", 1), join("qm", 0))
                kbt = mkt[p]
                k_both = jnp.concatenate([jnp.where(row_half[e], kbt, jnp.zeros_like(kbt)) for e in range(2)], axis=1)
                dqt[p, :, rows] = _dot(k_both, join("dsb", 0))
        for p in range(npair):
            dq_ref[:, p * LANES:(p + 1) * LANES] = dqt[p].T.astype(BF16)

    row = lambda w: pl.BlockSpec((tq, w), lambda i: (i, 0))
    full = pl.BlockSpec((N_MEM, C_W), lambda i: (0, 0))
    return pl.pallas_call(
        body, name="mem_attn_bwd", grid=(seq // tq,),
        in_specs=[row(C_W), full, full, row(C_W), row(LANES), row(LANES)], out_specs=[row(C_W), full, full],
        out_shape=[jax.ShapeDtypeStruct((seq, C_W), BF16), jax.ShapeDtypeStruct((N_MEM, C_W), F32),
                   jax.ShapeDtypeStruct((N_MEM, C_W), F32)],
        scratch_shapes=[pltpu.VMEM((ns, 8, LANES), F32)] * 2
        + [pltpu.VMEM((npair, LANES, N_MEM), BF16), pltpu.VMEM((npair, LANES, tq), F32)],
        compiler_params=pltpu.CompilerParams(dimension_semantics=("arbitrary",)),
    )(q, mk, mv, do, lse, delta)


def _silu_and_grad(g):
    s = 1.0 / (1.0 + jnp.exp(-g))
    return g * s, s * (1.0 + g * (1.0 - s))


def _post(x, target, post_norm, w_out, sink_row, oa, lse_a, ga, ob_list, lseb_list, gb, oc, gc):
    seq = x.shape[0]
    tm = min(ROW_TILE, seq)
    inv_d = 1.0 / D_MODEL
    nd = len(B_DILS)

    def body(*refs):
        (x_ref, t_ref, gp_ref, w_ref, sink_ref, oa_ref, lsea_ref, ga_ref), refs = refs[:8], refs[8:]
        ob_refs, lb_refs, (gb_ref, oc_ref, gc_ref), refs = refs[:nd], refs[nd:2 * nd], refs[2 * nd:2 * nd + 3], refs[2 * nd + 3:]
        (g_ref, doa_ref, dla_ref, dga_ref), refs = refs[:4], refs[4:]
        dob_refs, lsec_refs, dlb_refs, refs = refs[:nd], refs[nd:2 * nd], refs[2 * nd:3 * nd], refs[3 * nd:]
        (dgb_ref, doc_ref, dlc_ref, dgc_ref, gw_ref, gpost_ref, gsink_ref, loss_ref), refs = refs[:8], refs[8:]
        ycat, obufs, lbufs, st_do, st_l, st_d = refs[0], refs[1:nd], refs[nd:2 * nd - 1], refs[2 * nd - 1], refs[2 * nd], refs[2 * nd + 1]

        @pl.when(pl.program_id(0) == 0)
        def _():
            gw_ref[...] = jnp.zeros_like(gw_ref)
            gpost_ref[...] = jnp.zeros_like(gpost_ref)
            gsink_ref[...] = jnp.zeros_like(gsink_ref)
            loss_ref[...] = jnp.zeros_like(loss_ref)

        o_i, l_i = [ob_refs[0][0].astype(F32)], [lb_refs[0][0]]
        for k in range(1, nd):
            _from_residues(ob_refs[k], obufs[k - 1], B_DILS[k])
            _from_residues(lb_refs[k], lbufs[k - 1], B_DILS[k])
            o_i.append(_stage_read(obufs[k - 1]))
            l_i.append(_stage_read(lbufs[k - 1]))
        mx = l_i[0]
        for l in l_i[1:]:
            mx = jnp.maximum(mx, l)
        w_i = [jnp.exp(l - mx) for l in l_i]
        z = w_i[0]
        for w in w_i[1:]:
            z = z + w
        _stage_write(st_l, mx + jnp.log(z))
        expand = _head_expand_matrix(B_W)
        inv_z = 1.0 / z
        ob = None
        for w, o in zip(w_i, o_i):
            term = _dot_split(w * inv_z, expand, 2) * o
            ob = term if ob is None else ob + term
        oa, oc = oa_ref[...].astype(F32), oc_ref[...].astype(F32)
        sa, dsa = _silu_and_grad(ga_ref[...].astype(F32))
        sb, dsb = _silu_and_grad(gb_ref[...].astype(F32))
        sc, dsc = _silu_and_grad(gc_ref[...].astype(F32))
        ycat[:, 0:A_W] = (oa * sa).astype(BF16)
        ycat[:, A_W:A_W + B_W] = (ob * sb).astype(BF16)
        ycat[:, A_W + B_W:] = (oc * sc).astype(BF16)
        y2 = _dot(ycat[...], w_ref[...])
        r = lax.rsqrt(jnp.mean(y2 * y2, axis=-1, keepdims=True) + RMS_EPS)
        zhat = y2 * r
        gp = gp_ref[...]
        err = x_ref[...] + zhat * gp - t_ref[...]
        loss_ref[...] += jnp.sum(err * err) * (0.5 * inv_d)
        g = err * inv_d
        g_ref[...] = g
        gpost_ref[...] += jnp.sum(g * zhat, axis=0, keepdims=True)
        a = g * gp
        dy2 = (r * (a - zhat * jnp.mean(a * zhat, axis=-1, keepdims=True))).astype(BF16)
        gw_ref[...] += _dot_tn(ycat[...], dy2)
        dycat = _dot_nt(dy2, w_ref[...])
        dya, dyb, dyc = dycat[:, 0:A_W], dycat[:, A_W:A_W + B_W], dycat[:, A_W + B_W:]
        doa, dob, doc = dya * sa, dyb * sb, dyc * sc
        doa_ref[...] = doa.astype(BF16)
        doc_ref[...] = doc.astype(BF16)
        dga_ref[...] = (dya * oa * dsa).astype(BF16)
        dgb_ref[...] = (dyb * ob * dsb).astype(BF16)
        dgc_ref[...] = (dyc * oc * dsc).astype(BF16)
        dl_a = _dot_split(doa * oa, _head_sum_matrix(A_W), 2)
        dla_ref[...] = dl_a
        dlc_ref[...] = _dot_split(doc * oc, _head_sum_matrix(C_W), 2)
        gsink_ref[...] += jnp.sum(jnp.exp(sink_ref[...] - lsea_ref[...]) * dl_a, axis=0, keepdims=True)
        _stage_write(st_do, dob)
        _stage_write(st_d, _dot_split(dob * ob, _head_sum_matrix(B_W), 2))
        for k, dil in enumerate(B_DILS):
            _to_residues(st_do, dob_refs[k], dil)
            _to_residues(st_l, lsec_refs[k], dil)
            _to_residues(st_d, dlb_refs[k], dil)

    row = lambda w: pl.BlockSpec((tm, w), lambda i: (i, 0))
    full = lambda shape: pl.BlockSpec(shape, lambda i: (0,) * len(shape))
    res_specs = lambda w: [_residue_spec(d, tm, w) for d in B_DILS]
    res_shapes = lambda w, dt: [jax.ShapeDtypeStruct((d, seq // d, w), dt) for d in B_DILS]
    ins = [x, target, post_norm, w_out, sink_row, oa, lse_a, ga, *ob_list, *lseb_list, gb, oc, gc]
    in_specs = ([row(D_MODEL), row(D_MODEL), full((1, D_MODEL)), full((D_MODEL, D_MODEL)), full((1, LANES)),
                 row(A_W), row(LANES), row(A_W)] + res_specs(B_W) + res_specs(LANES) + [row(B_W), row(C_W), row(C_W)])
    out_shape = ([jax.ShapeDtypeStruct((seq, D_MODEL), F32), jax.ShapeDtypeStruct((seq, A_W), BF16),
                  jax.ShapeDtypeStruct((seq, LANES), F32), jax.ShapeDtypeStruct((seq, A_W), BF16)]
                 + res_shapes(B_W, BF16) + res_shapes(LANES, F32) + res_shapes(LANES, F32)
                 + [jax.ShapeDtypeStruct((seq, B_W), BF16), jax.ShapeDtypeStruct((seq, C_W), BF16),
                    jax.ShapeDtypeStruct((seq, LANES), F32), jax.ShapeDtypeStruct((seq, C_W), BF16),
                    jax.ShapeDtypeStruct((D_MODEL, D_MODEL), F32), jax.ShapeDtypeStruct((1, D_MODEL), F32),
                    jax.ShapeDtypeStruct((1, LANES), F32), jax.ShapeDtypeStruct((1, LANES), F32)])
    out_specs = ([row(D_MODEL), row(A_W), row(LANES), row(A_W)] + res_specs(B_W) + res_specs(LANES) + res_specs(LANES)
                 + [row(B_W), row(C_W), row(LANES), row(C_W),
                    full((D_MODEL, D_MODEL)), full((1, D_MODEL)), full((1, LANES)), full((1, LANES))])
    scratch = ([pltpu.VMEM((tm, D_MODEL), BF16)] + [_stage(tm, B_W)] * (nd - 1) + [_stage(tm, LANES)] * (nd - 1)
               + [_stage(tm, B_W), _stage(tm, LANES), _stage(tm, LANES)])
    res = pl.pallas_call(
        body, name="post", grid=(seq // tm,), in_specs=in_specs, out_specs=out_specs, out_shape=out_shape,
        scratch_shapes=scratch,
        compiler_params=pltpu.CompilerParams(dimension_semantics=("arbitrary",)),
    )(*ins)
    out = dict(g=res[0], doa=res[1], dl_a=res[2], dga=res[3], dob=res[4:4 + nd], lse_b=res[4 + nd:4 + 2 * nd],
               dl_b=res[4 + 2 * nd:4 + 3 * nd])
    rest = res[4 + 3 * nd:]
    out.update(dgb=rest[0], doc=rest[1], dl_c=rest[2], dgc=rest[3], gw_out=rest[4], gpost=rest[5], gsink=rest[6],
               loss=rest[7])
    return out


def _grad_w_in(ut, nat, res):
    seq = ut.shape[1]
    tm = min(ROW_TILE, seq)
    nd = len(B_DILS)
    nat_list = [nat[n] for n in _NATURAL]
    res_list = [a for n in _DILATED for a in res[n]]
    rope = _rope_tables(seq, tm)

    def body(rl_ref, rb_ref, ut_ref, *refs):
        nat_refs = dict(zip(_NATURAL, refs[:len(_NATURAL)]))
        refs = refs[len(_NATURAL):]
        res_refs = {n: refs[nd * k:nd * (k + 1)] for k, n in enumerate(_DILATED)}
        refs = refs[nd * len(_DILATED):]
        dproj_ref, gw_ref = refs[:2]
        bufs = {n: refs[2 + (nd - 1) * k:2 + (nd - 1) * (k + 1)] for k, n in enumerate(_DILATED)}

        @pl.when(pl.program_id(0) == 0)
        def _():
            gw_ref[...] = jnp.zeros_like(gw_ref)

        for n in _DILATED:
            for k in range(1, nd):
                _from_residues(res_refs[n][k], bufs[n][k - 1], B_DILS[k])
        c, sm, sp = _rope_coeffs(rl_ref, rb_ref)
        sm, sp = -sm, -sp
        for blk, (name, off, roped, scaled) in enumerate(_PROJ_LAYOUT):
            lanes = slice(off, off + LANES)
            if name in nat_refs:
                piece = nat_refs[name][:, lanes].astype(F32)
            else:
                piece = res_refs[name][0][0, :, lanes].astype(F32)
                for buf in bufs[name]:
                    piece = piece + buf[off // LANES]
            if roped:
                piece = _rope(piece, c, sm, sp)
            if scaled:
                piece = piece * SCALE
            dproj_ref[:, blk * LANES:(blk + 1) * LANES] = piece.astype(BF16)
        for j in range(N_CHIPS):
            gw_ref[j] += _dot(ut_ref[...], dproj_ref[:, j * SHARD_IN:(j + 1) * SHARD_IN])

    row = lambda w: pl.BlockSpec((tm, w), lambda i: (i, 0))
    in_specs = ([pl.BlockSpec(rope[0].shape, lambda i: (0, 0)), pl.BlockSpec((8, 2 * LANES), lambda i: (i, 0)),
                 pl.BlockSpec((D_MODEL, tm), lambda i: (0, i))]
                + [row(a.shape[1]) for a in nat_list]
                + [_residue_spec(d, tm, B_W) for _ in _DILATED for d in B_DILS])
    return pl.pallas_call(
        body, name="grad_w_in", grid=(seq // tm,), in_specs=in_specs,
        out_specs=[row(D_IN), pl.BlockSpec((N_CHIPS, D_MODEL, SHARD_IN), lambda i: (0, 0, 0))],
        out_shape=[jax.ShapeDtypeStruct((seq, D_IN), BF16), jax.ShapeDtypeStruct((N_CHIPS, D_MODEL, SHARD_IN), F32)],
        scratch_shapes=[_stage(tm, B_W)] * ((nd - 1) * len(_DILATED)),
        compiler_params=pltpu.CompilerParams(dimension_semantics=("arbitrary",)),
    )(*rope, ut, *nat_list, *res_list)


def _input_grad(x, g, pre_norm, w_in_g, dproj, gx_prev, span, after, name):
    seq = x.shape[0]
    tm = seq // INPUT_GRAD_TILES
    first_block, steps = span

    def body(*refs):
        x_ref, g_ref, gp_ref, w_ref, dp_ref = refs[:5]
        gx_ref, gpre_ref = refs[-2:]

        @pl.when(pl.program_id(0) == 0)
        def _():
            gpre_ref[...] = jnp.zeros_like(gpre_ref)

        du = None
        for j in range(N_CHIPS):
            term = _dot_nt(dp_ref[:, j * SHARD_IN:(j + 1) * SHARD_IN], w_ref[j])
            du = term if du is None else du + term
        xv = x_ref[...]
        r = lax.rsqrt(jnp.mean(xv * xv, axis=-1, keepdims=True) + RMS_EPS)
        xhat = xv * r
        gpre_ref[...] += jnp.sum(du * xhat, axis=0, keepdims=True)
        a = du * gp_ref[...]
        gx_ref[...] = g_ref[...] + r * (a - xhat * jnp.mean(a * xhat, axis=-1, keepdims=True))

    row = lambda w: pl.BlockSpec((tm, w), lambda i: (first_block + i, 0))
    full = lambda a: pl.BlockSpec(a.shape, lambda i: (0,) * a.ndim)
    any_spec = pl.BlockSpec(memory_space=pl.ANY)
    ins = [x, g, pre_norm, w_in_g, dproj]
    in_specs = [row(D_MODEL), row(D_MODEL), full(pre_norm), full(w_in_g), row(D_IN)]
    aliases = {}
    if gx_prev is not None:
        aliases[len(ins)] = 0
        ins.append(gx_prev)
        in_specs.append(any_spec)
    if after is not None:
        ins.append(after)
        in_specs.append(any_spec)
    return pl.pallas_call(
        body, name=name, grid=(steps,), in_specs=in_specs,
        out_specs=[row(D_MODEL), pl.BlockSpec((1, D_MODEL), lambda i: (0, 0))],
        out_shape=[jax.ShapeDtypeStruct((seq, D_MODEL), F32), jax.ShapeDtypeStruct((1, D_MODEL), F32)],
        input_output_aliases=aliases,
        compiler_params=pltpu.CompilerParams(dimension_semantics=("arbitrary",)),
    )(*ins)


def _exchange_start(ex, name):
    n_in, n_out, n_sem = len(ex["ins"]), len(ex["outs"]), len(ex["sems"])

    def body(*refs):
        in_refs, land_refs, sems = refs[:n_in], refs[n_in:n_in + n_out], refs[n_in + n_out:n_in + n_out + n_sem]
        ex["start"](in_refs, land_refs, *sems)
        token = refs[-1]
        token[...] = jnp.zeros_like(token)

    hbm = pl.BlockSpec(memory_space=pltpu.HBM)
    sem = pl.BlockSpec(memory_space=pltpu.SEMAPHORE)
    ins = [pltpu.with_memory_space_constraint(a, pltpu.HBM) for a in ex["ins"]]
    landing = [pltpu.with_memory_space_constraint(lax.empty(o.shape, o.dtype), pltpu.HBM) for o in ex["outs"]]
    res = pl.pallas_call(
        body, name=name,
        out_shape=list(ex["sems"]) + [pltpu.HBM(a.shape, a.dtype) for a in ex["ins"]]
        + [pltpu.HBM(o.shape, o.dtype) for o in ex["outs"]] + [jax.ShapeDtypeStruct((8, LANES), F32)],
        in_specs=[hbm] * (n_in + n_out),
        out_specs=[sem] * n_sem + [hbm] * (n_in + n_out) + [pl.BlockSpec(memory_space=pltpu.VMEM)],
        input_output_aliases={k: n_sem + k for k in range(n_in + n_out)},
        compiler_params=pltpu.CompilerParams(has_side_effects=pltpu.SideEffectType.DATAFLOW_SIDE_EFFECTING),
    )(*ins, *landing)
    return res[:-1], res[-1]


def _exchange_wait(ex, handles, after, name):
    n_in, n_out, n_sem = len(ex["ins"]), len(ex["outs"]), len(ex["sems"])
    sems, thru = handles[:n_sem], handles[n_sem:]

    def body(*refs):
        in_refs, land_refs = refs[:n_in], refs[n_in:n_in + n_out]
        sem_refs = refs[n_in + n_out:n_in + n_out + n_sem]
        ex["finish"](in_refs, land_refs, *sem_refs)

    hbm = pl.BlockSpec(memory_space=pltpu.HBM)
    sem = pl.BlockSpec(memory_space=pltpu.SEMAPHORE)
    res = pl.pallas_call(
        body, name=name,
        out_shape=[pltpu.HBM(a.shape, a.dtype) for a in thru],
        in_specs=[hbm] * (n_in + n_out) + [sem] * n_sem + [pl.BlockSpec(memory_space=pl.ANY)],
        out_specs=[hbm] * (n_in + n_out),
        input_output_aliases={k: k for k in range(n_in + n_out)},
        compiler_params=pltpu.CompilerParams(has_side_effects=pltpu.SideEffectType.DATAFLOW_SIDE_EFFECTING),
    )(*thru, *sems, after)
    return res[:n_in], res[n_in:]


def _start_finish(build):
    def start(*refs):
        for cp in build(*refs):
            cp.start()

    def finish(*refs):
        for cp in build(*refs):
            cp.wait()

    return dict(start=start, finish=finish)


def _pair_exchange(grads):
    n = len(grads)

    def build(srcs, outs, send_sems, recv_sems):
        x, y, c = lax.axis_index("x"), lax.axis_index("y"), lax.axis_index("c")
        copies = []
        for t in range(n):
            rows = grads[t].shape[1] // 2
            copies.append(pltpu.make_async_remote_copy(
                src_ref=srcs[t].at[:, pl.ds((1 - c) * rows, rows)], dst_ref=outs[t],
                send_sem=send_sems.at[t], recv_sem=recv_sems.at[t], device_id=(x, y, 1 - c), device_id_type=MESH))
        return copies

    return dict(ins=list(grads), **_start_finish(build),
                outs=[jax.ShapeDtypeStruct((g.shape[0], g.shape[1] // 2, g.shape[2]), g.dtype) for g in grads],
                sems=[pltpu.SemaphoreType.DMA((n,)), pltpu.SemaphoreType.DMA((n,))])


def _pair_add(core, own, got):
    nchip, rows2, width = own.shape
    rows = rows2 // 2
    tr = min(ROW_TILE, rows)
    nb = rows // tr

    def body(core_ref, own_ref, got_ref, out_ref):
        out_ref[...] = (own_ref[...] + got_ref[...]).astype(BF16)

    grid_spec = pltpu.PrefetchScalarGridSpec(
        num_scalar_prefetch=1, grid=(nchip, nb),
        in_specs=[pl.BlockSpec((None, tr, width), lambda k, i, core_ref: (k, core_ref[0] * nb + i, 0)),
                  pl.BlockSpec((None, tr, width), lambda k, i, core_ref: (k, i, 0))],
        out_specs=pl.BlockSpec((None, tr, width), lambda k, i, core_ref: (k, i, 0)))
    return pl.pallas_call(
        body, name=f"pair_add_{width}", grid_spec=grid_spec,
        out_shape=jax.ShapeDtypeStruct((nchip, rows, width), BF16),
    )(core, own, got)


def _chip_exchange(parts):
    n = len(parts)

    def build(srcs, outs, send_sems, recv_sems, local_sems):
        x, y, c = lax.axis_index("x"), lax.axis_index("y"), lax.axis_index("c")
        my_chip = 2 * x + y
        chips = [(1 - x, y), (x, 1 - y), (1 - x, 1 - y)]
        copies = [pltpu.make_async_copy(srcs[t].at[my_chip], outs[t].at[my_chip], local_sems.at[t]) for t in range(n)]
        for j, (cx, cy) in enumerate(chips):
            for t in range(n):
                k = n * j + t
                copies.append(pltpu.make_async_remote_copy(
                    src_ref=srcs[t].at[2 * cx + cy], dst_ref=outs[t].at[my_chip], send_sem=send_sems.at[k],
                    recv_sem=recv_sems.at[k], device_id=(cx, cy, c), device_id_type=MESH))
        return copies

    return dict(ins=list(parts), **_start_finish(build), outs=[jax.ShapeDtypeStruct(p.shape, p.dtype) for p in parts],
                sems=[pltpu.SemaphoreType.DMA((3 * n,)), pltpu.SemaphoreType.DMA((3 * n,)),
                      pltpu.SemaphoreType.DMA((n,))])


def _slot_sum(slots, name, core=None):
    ns, rows, width = slots.shape
    tr = min(ROW_TILE, rows)

    def body(*refs):
        in_ref, out_ref = refs[-2:]
        acc = in_ref[0].astype(F32)
        for s in range(1, ns):
            acc = acc + in_ref[s].astype(F32)
        out_ref[...] = acc

    if core is None:
        return pl.pallas_call(
            body, name=name, grid=(rows // tr,),
            in_specs=[pl.BlockSpec((ns, tr, width), lambda i: (0, i, 0))],
            out_specs=pl.BlockSpec((tr, width), lambda i: (i, 0)),
            out_shape=jax.ShapeDtypeStruct((rows, width), F32),
        )(slots)
    grid_spec = pltpu.PrefetchScalarGridSpec(
        num_scalar_prefetch=1, grid=(rows // tr,),
        in_specs=[pl.BlockSpec((ns, tr, width), lambda i, core_ref: (0, i, 0))],
        out_specs=pl.BlockSpec((None, tr, width), lambda i, core_ref: (core_ref[0], i, 0)))
    return pl.pallas_call(
        body, name=name, grid_spec=grid_spec, out_shape=jax.ShapeDtypeStruct((2, rows, width), F32),
    )(core, slots)


def _pair_gather(bufs, small):
    n = len(bufs)

    def body(*refs):
        small_ref, outs, small_out = refs[n], refs[n + 1:2 * n + 1], refs[2 * n + 1]
        send_sems, recv_sems, local_sem = refs[2 * n + 2:]
        x, y, c = lax.axis_index("x"), lax.axis_index("y"), lax.axis_index("c")
        me = 4 * x + 2 * y + c
        chips = [(1 - x, y), (x, 1 - y), (1 - x, 1 - y)]
        mine = pltpu.make_async_copy(small_ref, small_out.at[me], local_sem)
        mine.start()
        copies = [pltpu.make_async_remote_copy(
            src_ref=outs[t].at[c], dst_ref=outs[t].at[c], send_sem=send_sems.at[t], recv_sem=recv_sems.at[t],
            device_id=(x, y, 1 - c), device_id_type=MESH) for t in range(n)]
        peers = [(x, y, 1 - c)] + [(cx, cy, cc) for (cx, cy) in chips for cc in (c, 1 - c)]
        for j, peer in enumerate(peers):
            copies.append(pltpu.make_async_remote_copy(
                src_ref=small_ref, dst_ref=small_out.at[me], send_sem=send_sems.at[n + j],
                recv_sem=recv_sems.at[n + j], device_id=peer, device_id_type=MESH))
        for cp in copies:
            cp.start()
        for cp in copies:
            cp.wait()
        mine.wait()

    any_spec = pl.BlockSpec(memory_space=pl.ANY)
    res = pl.pallas_call(
        body, name="pair_gather",
        out_shape=[jax.ShapeDtypeStruct(b.shape, b.dtype) for b in bufs]
        + [jax.ShapeDtypeStruct((8,) + small.shape, small.dtype)],
        in_specs=[any_spec] * (n + 1), out_specs=[any_spec] * (n + 1),
        input_output_aliases={t: t for t in range(n)},
        scratch_shapes=[pltpu.SemaphoreType.DMA((n + 7,)), pltpu.SemaphoreType.DMA((n + 7,)),
                        pltpu.SemaphoreType.DMA],
    )(*bufs, small)
    return [r.reshape(2 * b.shape[1], b.shape[2]) for r, b in zip(res[:n], bufs)], res[n]


def _adamw(w, g, m, v, name):
    rows, width = w.shape
    tr = min(ROW_TILE // 2, rows)
    c1 = 1.0 / (1.0 - ADAM_B1 ** ADAM_STEP)
    c2 = 1.0 / (1.0 - ADAM_B2 ** ADAM_STEP)

    def body(w_ref, g_ref, m_ref, v_ref, d_ref, nm_ref, nv_ref):
        gv = g_ref[...]
        nm = ADAM_B1 * m_ref[...] + (1.0 - ADAM_B1) * gv
        nv = ADAM_B2 * v_ref[...] + (1.0 - ADAM_B2) * (gv * gv)
        nm_ref[...] = nm
        nv_ref[...] = nv
        d_ref[...] = -ADAM_LR * ((nm * c1) / (jnp.sqrt(nv * c2) + ADAM_EPS) + ADAM_WD * w_ref[...])

    spec = pl.BlockSpec((tr, width), lambda i: (i, 0))
    return pl.pallas_call(
        body, name=name, grid=(rows // tr,), in_specs=[spec] * 4, out_specs=[spec] * 3,
        out_shape=[jax.ShapeDtypeStruct(w.shape, F32)] * 3,
    )(w, g, m, v)


def _local_step(x, mem, target, pre_norm, sink_a, mem_norm, post_norm, w_in_g, w_out, w_mkv, gathers=None):
    first_gather, late_gather = gathers if gathers else (None, None)
    u, ut, hosted = _pre_norm(x, pre_norm, first_gather)
    if gathers:
        w_in_g = hosted[0].reshape(N_CHIPS, D_MODEL, SHARD_IN)
    pr = _pre_proj(u, w_in_g, late_gather)
    pr["ut"] = ut
    if gathers:
        w_out, w_mkv = (g.reshape(D_MODEL, g.shape[-1]) for g in pr["hosted"])
    mk, mv = _mem_kv(mem, mem_norm, w_mkv)
    sink = sink_a.reshape(-1)
    qa, ka, va = pr["qa"][None], pr["ka"][None], pr["va"][None]
    oa, lse_a = _band_fwd(qa, ka, va, sink, max_dist=A_WINDOW - 1, name="swa_fwd")
    ob_list, lseb_list = [], []
    for k, (win, dil) in enumerate(B_CONFIGS):
        o_i, l_i = _band_fwd(pr["qb"][k], pr["kb"][k], pr["vb"][k], None, max_dist=win // dil, name=f"dil{dil}_fwd")
        ob_list.append(o_i)
        lseb_list.append(l_i)
    oc, lse_c = _mem_attn_fwd(pr["qc"], mk, mv)
    sink_row = jnp.pad(sink, (0, LANES - sink.shape[0])).reshape(1, LANES)
    po = _post(x, target, post_norm, w_out, sink_row, oa[0], lse_a[0], pr["ga"], ob_list, lseb_list, pr["gb"], oc,
               pr["gc"])
    dqc, dmk, dmv = _mem_attn_bwd(pr["qc"], mk, mv, po["doc"], lse_c, po["dl_c"])
    dqa, dka, dva = _band_bwd(qa, ka, va, po["doa"][None], lse_a, po["dl_a"][None], max_dist=A_WINDOW - 1,
                              name="swa_bwd")
    res = dict(qb=[], kb=[], vb=[])
    for k, (win, dil) in enumerate(B_CONFIGS):
        dq_i, dk_i, dv_i = _band_bwd(pr["qb"][k], pr["kb"][k], pr["vb"][k], po["dob"][k], po["lse_b"][k],
                                     po["dl_b"][k], max_dist=win // dil, name=f"dil{dil}_bwd")
        res["qb"].append(dq_i)
        res["kb"].append(dk_i)
        res["vb"].append(dv_i)
    nat = dict(qa=dqa[0], ka=dka[0], va=dva[0], ga=po["dga"], gb=po["dgb"], qc=dqc, gc=po["dgc"])
    dproj, gw_in = _grad_w_in(pr["ut"], nat, res)
    gw_mkv, gmem = _mem_kv_bwd(mem, mem_norm, w_mkv, dmk, dmv)
    gsink = -po["gsink"][0, :sink.shape[0]]
    return dict(loss=po["loss"], g=po["g"], dproj=dproj, gw_in=gw_in, gw_out=po["gw_out"], gw_mkv=gw_mkv,
                gpost=po["gpost"], gmem=gmem, gsink=gsink, w_in_g=w_in_g)


def kernel(x, mem, pre_norm, w_in, sink_a, mem_norm, w_mem_kv, w_out, post_norm, loss_target, m_pre_norm, m_w_in, m_sink_a, m_mem_norm, m_w_mem_kv, m_w_out, m_post_norm, v_pre_norm, v_w_in, v_sink_a, v_mem_norm, v_w_mem_kv, v_w_out, v_post_norm):
    gathers = (_gather_exchange([w_in[0].astype(BF16)]),
               _gather_exchange([w_out[0].astype(BF16), w_mem_kv[0].astype(BF16)]))
    loc = _local_step(x[0], mem[0], loss_target[0], pre_norm, sink_a, mem_norm, post_norm, None, None, None, gathers)
    big = [loc["gw_in"], loc["gw_out"].reshape(N_CHIPS, D_MODEL // N_CHIPS, D_MODEL),
           loc["gw_mkv"].reshape(N_CHIPS, D_MODEL // N_CHIPS, 2 * C_W)]
    core = lax.axis_index("c").astype(jnp.int32).reshape(1)
    w_in_full = loc["w_in_g"]
    step_in = (x[0], loc["g"], pre_norm, w_in_full, loc["dproj"])
    pair_ex = _pair_exchange(big)
    pair_handles, token = _exchange_start(pair_ex, "pair_exchange_start")
    gx_a, gpre_a = _input_grad(*step_in, None, (0, 2), token, "input_grad_a")
    big, got = _exchange_wait(pair_ex, pair_handles, gpre_a, "pair_exchange_wait")
    parts = [_pair_add(core, own, g) for own, g in zip(big, got)]
    chip_ex = _chip_exchange(parts)
    chip_handles, token = _exchange_start(chip_ex, "chip_exchange_start")
    grad_x, gpre_b = _input_grad(*step_in, gx_a, (2, 14), token, "input_grad_b")
    _, slots = _exchange_wait(chip_ex, chip_handles, gpre_b, "chip_exchange_wait")
    halves = [_slot_sum(s, name=f"chip_sum_{s.shape[2]}", core=core) for s in slots]
    widen = lambda a: jnp.pad(a.reshape(1, -1), ((0, 0), (0, D_MODEL - a.size)))
    small = jnp.concatenate([gpre_a, loc["gpost"], loc["gmem"], widen(loc["gsink"]), widen(loc["loss"]), gpre_b,
                             jnp.zeros((2, D_MODEL), F32)], axis=0)
    (g_in, g_out, g_mkv), small_slots = _pair_gather(halves, small)
    small_sum = _slot_sum(small_slots, name="device_sum")
    g_pre, g_post, g_mem = small_sum[0:1] + small_sum[5:6], small_sum[1:2], small_sum[2:3]
    g_sink = small_sum[3:4, :sink_a.shape[1]]
    loss = small_sum[4, 0]

    d_in, nm_in, nv_in = _adamw(w_in[0], g_in, m_w_in[0], v_w_in[0], "adamw_in")
    d_out, nm_out, nv_out = _adamw(w_out[0], g_out, m_w_out[0], v_w_out[0], "adamw_out")
    d_mkv, nm_mkv, nv_mkv = _adamw(w_mem_kv[0], g_mkv, m_w_mem_kv[0], v_w_mem_kv[0], "adamw_mkv")
    pad6 = lambda a: jnp.pad(a, ((0, 0), (0, D_MODEL - a.shape[1])))
    stack = lambda a, b, c_, d_: jnp.concatenate([a, b, c_, pad6(d_), jnp.zeros((4, D_MODEL), F32)], axis=0)
    d_s, nm_s, nv_s = _adamw(stack(pre_norm, post_norm, mem_norm, sink_a),
                             jnp.concatenate([g_pre, small_sum[1:]], axis=0),
                             stack(m_pre_norm, m_post_norm, m_mem_norm, m_sink_a),
                             stack(v_pre_norm, v_post_norm, v_mem_norm, v_sink_a), "adamw_small")
    ns_ = sink_a.shape[1]
    unpack = lambda a: (a[0:1], a[3:4, :ns_], a[2:3], a[1:2])
    d_pre, d_sink, d_mem, d_post = unpack(d_s)
    nm_pre, nm_sink, nm_mem, nm_post = unpack(nm_s)
    nv_pre, nv_sink, nv_mem, nv_post = unpack(nv_s)
    lead = lambda a: a[None]
    return (loss, lead(grad_x),
            g_pre, lead(g_in), g_sink, g_mem, lead(g_mkv), lead(g_out), g_post,
            d_pre, lead(d_in), d_sink, d_mem, lead(d_mkv), lead(d_out), d_post,
            nm_pre, lead(nm_in), nm_sink, nm_mem, lead(nm_mkv), lead(nm_out), nm_post,
            nv_pre, lead(nv_in), nv_sink, nv_mem, lead(nv_mkv), lead(nv_out), nv_post)
```

```python
import numpy as np
import jax
import jax.numpy as jnp
from jax import lax
from jax.experimental import pallas as pl
from jax.experimental.pallas import tpu as pltpu

F32 = jnp.float32
BF16 = jnp.bfloat16

D_MODEL = 1024
HEAD_DIM = 64
LANES = 128
BLOCK = 128
ROW_TILE = 512
ATTN_TILE = 1024
INPUT_GRAD_TILES = 16
A_W, A_KV_W, B_W, C_W = 384, 128, 384, 256
N_MEM = 256
D_IN = 3072
N_CHIPS = 4
SHARD_IN = D_IN // N_CHIPS
B_CONFIGS = ((128, 1), (512, 4), (2048, 16))
B_DILS = tuple(d for _, d in B_CONFIGS)
A_WINDOW = 128
RMS_EPS = 1e-6
ROPE_THETA = 500000.0
SCALE = HEAD_DIM ** -0.5
NEG = -1e30
ADAM_LR, ADAM_B1, ADAM_B2, ADAM_EPS, ADAM_WD, ADAM_STEP = 0.001, 0.9, 0.999, 1e-08, 0.01, 10

NT = (((1,), (1,)), ((), ()))
TN = (((0,), (0,)), ((), ()))
MESH = pl.DeviceIdType.MESH

_PROJ_LAYOUT = (
    [("qa", 128 * i, True, True) for i in range(3)] + [("ka", 0, True, False), ("va", 0, False, False)]
    + [("ga", 128 * i, False, False) for i in range(3)]
    + [("qb", 128 * i, True, True) for i in range(3)] + [("kb", 128 * i, True, False) for i in range(3)]
    + [("vb", 128 * i, False, False) for i in range(3)] + [("gb", 128 * i, False, False) for i in range(3)]
    + [("qc", 128 * i, False, True) for i in range(2)] + [("gc", 128 * i, False, False) for i in range(2)]
)
_PROJ_WIDTH = dict(qa=A_W, ka=A_KV_W, va=A_KV_W, ga=A_W, qb=B_W, kb=B_W, vb=B_W, gb=B_W, qc=C_W, gc=C_W)
_NATURAL = ("qa", "ka", "va", "ga", "gb", "qc", "gc")
_DILATED = ("qb", "kb", "vb")


def _dot(a, b):
    return jnp.dot(a, b, preferred_element_type=F32)


def _dot_nt(a, b):
    return lax.dot_general(a, b, NT, preferred_element_type=F32)


def _dot_tn(a, b):
    return lax.dot_general(a, b, TN, preferred_element_type=F32)


def _half_masks(rows):
    lane = lax.broadcasted_iota(jnp.int32, (rows, LANES), 1)
    return lane < HEAD_DIM, lane >= HEAD_DIM


def _rope(t, c, sm, sp):
    return t * c + pltpu.roll(t, LANES - 8, 1) * sm + pltpu.roll(t, 8, 1) * sp


def _rope_tables(seq, tm):
    dim = jnp.arange(LANES) % HEAD_DIM
    inv_freq = ROPE_THETA ** (-jnp.arange(0, 16, 2, dtype=F32) / 16)
    freq = jnp.where(dim < 16, inv_freq[dim % 8], 0.0)[None, :]
    local = jnp.arange(tm, dtype=F32)[:, None] * freq
    base = (jnp.arange(seq // tm, dtype=F32) * tm)[:, None] * freq
    both = lambda a: jnp.concatenate([jnp.cos(a), jnp.sin(a)], axis=1)
    return both(local), jnp.repeat(both(base), 8, axis=0)


def _rope_coeffs(local_ref, base_ref):
    cl, sl = local_ref[:, :LANES], local_ref[:, LANES:]
    cb, sb = base_ref[0:1, :LANES], base_ref[0:1, LANES:]
    cos = cb * cl - sb * sl
    sin = sb * cl + cb * sl
    dim = lax.broadcasted_iota(jnp.int32, (1, LANES), 1) % HEAD_DIM
    return cos, jnp.where(dim < 8, -sin, 0.0), jnp.where((dim >= 8) & (dim < 16), sin, 0.0)


def _split3(x):
    a = x.astype(BF16)
    r = x - a.astype(F32)
    b = r.astype(BF16)
    c = (r - b.astype(F32)).astype(BF16)
    return a, b, c


def _rows_to_lanes(x):
    row = lax.broadcasted_iota(jnp.int32, (8, LANES), 0)
    lane = lax.broadcasted_iota(jnp.int32, (8, LANES), 1)
    eye = (row == lane).astype(BF16)
    a, b, c = _split3(x)
    return _dot_nt(eye, a) + _dot_nt(eye, b) + _dot_nt(eye, c)


def _head_sum_matrix(width):
    k = lax.broadcasted_iota(jnp.int32, (width, LANES), 0)
    h = lax.broadcasted_iota(jnp.int32, (width, LANES), 1)
    return (k // HEAD_DIM == h).astype(BF16)


def _head_expand_matrix(width):
    h = lax.broadcasted_iota(jnp.int32, (LANES, width), 0)
    k = lax.broadcasted_iota(jnp.int32, (LANES, width), 1)
    return (k // HEAD_DIM == h).astype(BF16)


def _dot_split(x, mat, terms):
    parts = _split3(x)[:terms]
    out = _dot(parts[0], mat)
    for p in parts[1:]:
        out = out + _dot(p, mat)
    return out


def _per_head(cols, fill=0.0):
    rows = cols[0].shape[0]
    lane = lax.broadcasted_iota(jnp.int32, (rows, LANES), 1)
    out = jnp.full((rows, LANES), fill, F32)
    for h, col in enumerate(cols):
        out = jnp.where(lane == h, col, out)
    return out


def _lane_blocks(width):
    return [slice(p * LANES, (p + 1) * LANES) for p in range(width // LANES)]


def _stage(rows, width):
    return pltpu.VMEM((width // LANES, rows, LANES), F32)


def _stage_write(buf, value):
    for p, lanes in enumerate(_lane_blocks(value.shape[1])):
        buf[p] = value[:, lanes]


def _stage_read(buf):
    return jnp.concatenate([buf[p] for p in range(buf.shape[0])], axis=1) if buf.shape[0] > 1 else buf[0]


def _to_residues(buf, out_ref, dil):
    rows = buf.shape[1] // dil
    for r in range(dil):
        for p in range(buf.shape[0]):
            plane = buf.at[p]
            out_ref[r, :, p * LANES:(p + 1) * LANES] = plane[pl.ds(r, rows, stride=dil), :].astype(out_ref.dtype)


def _from_residues(in_ref, buf, dil):
    rows = buf.shape[1] // dil
    for r in range(dil):
        for p in range(buf.shape[0]):
            plane = buf.at[p]
            plane[pl.ds(r, rows, stride=dil), :] = in_ref[r, :, p * LANES:(p + 1) * LANES].astype(F32)


def _residue_spec(dil, tm, width):
    return pl.BlockSpec((dil, tm // dil, width), lambda i: (0, i, 0))


def _gather_exchange(shards_2d):
    shards = tuple(s.reshape(2, s.shape[0] // 2, s.shape[1]) for s in shards_2d)
    n = len(shards)

    def copies(in_refs, out_refs, send_sems, recv_sems):
        srcs, outs = in_refs[:n], out_refs
        x, y, c = lax.axis_index("x"), lax.axis_index("y"), lax.axis_index("c")
        my_chip = 2 * x + y
        sibling = (x, y, 1 - c)
        chips = [(1 - x, y), (x, 1 - y), (1 - x, 1 - y)]

        def copy(k, src, dst, to):
            return pltpu.make_async_remote_copy(src_ref=src, dst_ref=dst, send_sem=send_sems.at[k],
                                                recv_sem=recv_sems.at[k], device_id=to, device_id_type=MESH)

        first, arrive, passed, sibling_arrive = [], [], [], []
        for j, (cx, cy) in enumerate(chips):
            chip = 2 * cx + cy
            for t in range(n):
                k = n * j + t
                first.append(copy(k, srcs[t].at[c], outs[t].at[my_chip, c], (cx, cy, c)))
                arrive.append(copy(k, srcs[t].at[c], outs[t].at[chip, c], (cx, cy, c)))
                passed.append(copy(n * 3 + k, outs[t].at[chip, c], outs[t].at[chip, c], sibling))
                sibling_arrive.append(copy(n * 3 + k, outs[t].at[chip, 1 - c], outs[t].at[chip, 1 - c], sibling))
        return first, arrive, passed, sibling_arrive

    def start(*refs):
        for cp in copies(*refs)[0]:
            cp.start()

    def finish(*refs):
        first, arrive, passed, sibling_arrive = copies(*refs)
        for got, fwd in zip(arrive, passed):
            got.wait_recv()
            fwd.start()
        for cp in sibling_arrive:
            cp.wait_recv()
        for cp in first + passed:
            cp.wait_send()

    my_chip = 2 * lax.axis_index("x") + lax.axis_index("y")
    landing = [lax.dynamic_update_slice(jnp.zeros((N_CHIPS,) + s.shape, s.dtype), s[None], (my_chip, 0, 0, 0))
               for s in shards]
    return dict(ins=list(shards) + landing, start=start, finish=finish, aliases={n + t: t for t in range(n)},
                outs=[jax.ShapeDtypeStruct((N_CHIPS,) + s.shape, s.dtype) for s in shards],
                sems=[pltpu.SemaphoreType.DMA((6 * n,)), pltpu.SemaphoreType.DMA((6 * n,))])


def _mem_kv(mem, mem_norm, w_mkv):
    def body(mem_ref, g_ref, w_ref, mk_ref, mv_ref):
        m = mem_ref[...]
        r = lax.rsqrt(jnp.mean(m * m, axis=-1, keepdims=True) + RMS_EPS)
        mn = (m * r * g_ref[...]).astype(BF16)
        kv = _dot(mn, w_ref[...])
        mk_ref[...] = kv[:, :C_W].astype(BF16)
        mv_ref[...] = kv[:, C_W:].astype(BF16)

    return pl.pallas_call(
        body, name="mem_kv",
        out_shape=[jax.ShapeDtypeStruct((N_MEM, C_W), BF16)] * 2,
    )(mem, mem_norm, w_mkv)


def _mem_kv_bwd(mem, mem_norm, w_mkv, dmk, dmv):
    def body(mem_ref, g_ref, w_ref, dmk_ref, dmv_ref, gw_ref, gn_ref):
        m = mem_ref[...]
        r = lax.rsqrt(jnp.mean(m * m, axis=-1, keepdims=True) + RMS_EPS)
        mhat = m * r
        mn = (mhat * g_ref[...]).astype(BF16)
        dkv = jnp.concatenate([dmk_ref[...], dmv_ref[...]], axis=1).astype(BF16)
        gw_ref[...] = _dot_tn(mn, dkv)
        dmn = _dot_nt(dkv, w_ref[...])
        gn_ref[...] = jnp.sum(dmn * mhat, axis=0, keepdims=True)

    return pl.pallas_call(
        body, name="mem_kv_bwd",
        out_shape=[jax.ShapeDtypeStruct((D_MODEL, 2 * C_W), F32), jax.ShapeDtypeStruct((1, D_MODEL), F32)],
    )(mem, mem_norm, w_mkv, dmk, dmv)


def _pre_norm(x, pre_norm, host=None):
    seq = x.shape[0]
    tm = min(ROW_TILE, seq)
    n_host_in = len(host["ins"]) if host else 0
    n_host_out = len(host["outs"]) if host else 0

    def body(x_ref, g_ref, *refs):
        host_in, (u_ref, ut_ref), refs = refs[:n_host_in], refs[n_host_in:n_host_in + 2], refs[n_host_in + 2:]
        host_out, sems = refs[:n_host_out], refs[n_host_out:]
        if host:
            @pl.when(pl.program_id(0) == 0)
            def _():
                host["start"](host_in, host_out, *sems)

        xv = x_ref[...]
        r = lax.rsqrt(jnp.mean(xv * xv, axis=-1, keepdims=True) + RMS_EPS)
        u = xv * r * g_ref[...]
        u_ref[...] = u.astype(BF16)
        ut_ref[...] = u.T.astype(BF16)
        if host:
            @pl.when(pl.program_id(0) == seq // tm - 1)
            def _():
                host["finish"](host_in, host_out, *sems)

    any_spec = pl.BlockSpec(memory_space=pl.ANY)
    ins = [x, pre_norm]
    in_specs = [pl.BlockSpec((tm, D_MODEL), lambda i: (i, 0)), pl.BlockSpec(pre_norm.shape, lambda i: (0, 0))]
    out_shape = [jax.ShapeDtypeStruct((seq, D_MODEL), BF16), jax.ShapeDtypeStruct((D_MODEL, seq), BF16)]
    out_specs = [pl.BlockSpec((tm, D_MODEL), lambda i: (i, 0)), pl.BlockSpec((D_MODEL, tm), lambda i: (0, i))]
    aliases, scratch = {}, []
    if host:
        aliases = {len(ins) + k: 2 + v for k, v in host.get("aliases", {}).items()}
        ins += list(host["ins"])
        in_specs += [any_spec] * n_host_in
        out_shape += list(host["outs"])
        out_specs += [any_spec] * n_host_out
        scratch = list(host["sems"])
    res = pl.pallas_call(
        body, name="pre_norm", grid=(seq // tm,), in_specs=in_specs, out_specs=out_specs, out_shape=out_shape,
        input_output_aliases=aliases, scratch_shapes=scratch,
        compiler_params=pltpu.CompilerParams(dimension_semantics=("arbitrary",)),
    )(*ins)
    return res[0], res[1], res[2:]


def _pre_proj(u, w_in_g, host=None):
    seq = u.shape[0]
    tm = min(ROW_TILE, seq)
    n_nat, n_dil = len(_NATURAL), len(_DILATED) * len(B_DILS)
    rope = _rope_tables(seq, tm)

    n_host_in = len(host["ins"]) if host else 0
    n_host_out = len(host["outs"]) if host else 0
    n_own_out = n_nat + n_dil

    def body(u_ref, w_ref, rl_ref, rb_ref, *refs):
        host_in, refs = refs[:n_host_in], refs[n_host_in:]
        nat = dict(zip(_NATURAL, refs[:n_nat]))
        res = {n: refs[n_nat + len(B_DILS) * k:n_nat + len(B_DILS) * (k + 1)] for k, n in enumerate(_DILATED)}
        host_out = refs[n_own_out:n_own_out + n_host_out]
        bufs = dict(zip(_DILATED, refs[n_own_out + n_host_out:]))
        sems = refs[n_own_out + n_host_out + len(_DILATED):]
        if host:
            @pl.when(pl.program_id(0) == 0)
            def _():
                host["start"](host_in, host_out, *sems)

        ub = u_ref[...]
        c, sm, sp = _rope_coeffs(rl_ref, rb_ref)
        for j in range(N_CHIPS):
            pj = _dot(ub, w_ref[j])
            for b in range(SHARD_IN // LANES):
                name, off, roped, scaled = _PROJ_LAYOUT[(SHARD_IN // LANES) * j + b]
                piece = pj[:, LANES * b:LANES * (b + 1)]
                if roped:
                    piece = _rope(piece, c, sm, sp)
                if scaled:
                    piece = piece * SCALE
                if name in bufs:
                    bufs[name][off // LANES] = piece
                else:
                    nat[name][:, off:off + LANES] = piece.astype(BF16)
        for name in _DILATED:
            for ref, dil in zip(res[name], B_DILS):
                _to_residues(bufs[name], ref, dil)
        if host:
            @pl.when(pl.program_id(0) == seq // tm - 1)
            def _():
                host["finish"](host_in, host_out, *sems)

    row = lambda w: pl.BlockSpec((tm, w), lambda i: (i, 0))
    full = lambda a: pl.BlockSpec(a.shape, lambda i: (0,) * a.ndim)
    any_spec = pl.BlockSpec(memory_space=pl.ANY)
    out_shape = [jax.ShapeDtypeStruct((seq, _PROJ_WIDTH[n]), BF16) for n in _NATURAL]
    out_specs = [row(_PROJ_WIDTH[n]) for n in _NATURAL]
    for n in _DILATED:
        for dil in B_DILS:
            out_shape.append(jax.ShapeDtypeStruct((dil, seq // dil, B_W), BF16))
            out_specs.append(_residue_spec(dil, tm, B_W))
    ins = [u, w_in_g, *rope]
    in_specs = [row(D_MODEL), full(w_in_g), full(rope[0]), pl.BlockSpec((8, 2 * LANES), lambda i: (i, 0))]
    scratch = [_stage(tm, B_W)] * len(_DILATED)
    aliases = {}
    if host:
        aliases = {len(ins) + k: n_own_out + v for k, v in host.get("aliases", {}).items()}
        ins += list(host["ins"])
        in_specs += [any_spec] * n_host_in
        out_shape += list(host["outs"])
        out_specs += [any_spec] * n_host_out
        scratch += list(host["sems"])
    res = pl.pallas_call(
        body, name="pre_proj", grid=(seq // tm,), in_specs=in_specs, out_specs=out_specs, out_shape=out_shape,
        input_output_aliases=aliases, scratch_shapes=scratch,
        compiler_params=pltpu.CompilerParams(dimension_semantics=("arbitrary",)),
    )(*ins)
    out = dict(zip(_NATURAL, res[:n_nat]))
    for k, n in enumerate(_DILATED):
        out[n] = res[n_nat + len(B_DILS) * k:n_nat + len(B_DILS) * (k + 1)]
    out["hosted"] = res[n_own_out:]
    return out


def _band_bias(max_dist, transposed):
    i = np.arange(BLOCK)[:, None]
    j = np.arange(BLOCK)[None, :]
    if transposed:
        same = i <= j
        other = (j + BLOCK - i) <= max_dist
        vis = np.concatenate([same, other], axis=1)
    else:
        prev = (i + BLOCK - j) <= max_dist
        same = j <= i
        vis = np.concatenate([prev, same], axis=1)
    return jnp.asarray(np.where(vis, 0.0, NEG).astype(np.float32))


def _kv_place(h, gqa):
    return (0, h // 3) if gqa else (h // 2, h % 2)


def _band_fwd(q, k, v, sink, *, max_dist, name):
    dil, length, wq = q.shape
    wk = k.shape[2]
    gqa = wk != wq
    tq = min(ATTN_TILE, length)
    ns, nt = tq // BLOCK, length // tq
    npair = wq // LANES
    bias = _band_bias(max_dist, transposed=False)
    has_sink = sink is not None

    def body(*refs):
        if has_sink:
            sink_ref, refs = refs[0], refs[1:]
        q_ref, k_ref, kp_ref, v_ref, vp_ref, bias_ref, o_ref, lse_ref, kbuf, vbuf = refs[:10]
        i = pl.program_id(1)
        kbuf[0:BLOCK] = kp_ref[...]
        kbuf[BLOCK:] = k_ref[...]
        vbuf[0:BLOCK] = vp_ref[...]
        vbuf[BLOCK:] = v_ref[...]
        if gqa:
            kroll, vroll = refs[10:12]
            kroll[...] = pltpu.roll(kbuf[...], HEAD_DIM, 1)
            vroll[...] = pltpu.roll(vbuf[...], HEAD_DIM, 1)
        half = _half_masks(BLOCK)
        col_prev = (lax.broadcasted_iota(jnp.int32, (1, 2 * BLOCK), 1) < BLOCK).astype(F32)

        def score_matmuls(a):
            scores = []
            for p in range(npair):
                qp = q_ref[a * BLOCK:(a + 1) * BLOCK, p * LANES:(p + 1) * LANES]
                for e in range(2):
                    pk, ek = _kv_place(2 * p + e, gqa)
                    kw = (kbuf if ek == e else kroll)[a * BLOCK:(a + 2) * BLOCK, pk * LANES:(pk + 1) * LANES]
                    scores.append(_dot_nt(jnp.where(half[e], qp, jnp.zeros_like(qp)), kw))
            return scores

        pending = score_matmuls(0)
        for a in range(ns):
            r0 = a * BLOCK
            b = bias_ref[...]
            if a == 0:
                b = b + jnp.where(i == 0, NEG, 0.0) * col_prev
            scores = pending
            m_cols, l_cols, probs = [], [], []
            for h, s in enumerate(scores):
                s = s + b
                m = jnp.max(s, axis=1, keepdims=True)
                if has_sink:
                    m = jnp.maximum(m, sink_ref[h])
                pe = jnp.exp(s - m)
                l = jnp.sum(pe, axis=1, keepdims=True)
                if has_sink:
                    l = l + jnp.exp(sink_ref[h] - m)
                probs.append(pe.astype(BF16))
                m_cols.append(m)
                l_cols.append(l)
            pending = score_matmuls(a + 1) if a + 1 < ns else None
            for p in range(npair):
                o_h = []
                for e in range(2):
                    h = 2 * p + e
                    pk, ek = _kv_place(h, gqa)
                    vw = (vbuf if ek == e else vroll)[r0:r0 + 2 * BLOCK, pk * LANES:(pk + 1) * LANES]
                    o_h.append(_dot(probs[h], vw) * (1.0 / l_cols[h]))
                o_ref[r0:r0 + BLOCK, p * LANES:(p + 1) * LANES] = jnp.where(half[0], o_h[0], o_h[1]).astype(BF16)
            lse_ref[r0:r0 + BLOCK, :] = _per_head(m_cols) + jnp.log(_per_head(l_cols, 1.0))

    main = lambda w: pl.BlockSpec((None, tq, w), lambda r, i: (r, i, 0))
    prev = lambda w: pl.BlockSpec((None, BLOCK, w), lambda r, i: (r, jnp.maximum(i * ns - 1, 0), 0))
    in_specs = [main(wq), main(wk), prev(wk), main(wk), prev(wk), pl.BlockSpec(bias.shape, lambda r, i: (0, 0))]
    args = [q, k, k, v, v, bias]
    if has_sink:
        in_specs = [pl.BlockSpec(memory_space=pltpu.SMEM)] + in_specs
        args = [sink] + args
    scratch = [pltpu.VMEM((tq + BLOCK, wk), BF16)] * (4 if gqa else 2)
    return pl.pallas_call(
        body, name=name, grid=(dil, nt), in_specs=in_specs,
        out_specs=[main(wq), main(LANES)],
        out_shape=[jax.ShapeDtypeStruct((dil, length, wq), BF16), jax.ShapeDtypeStruct((dil, length, LANES), F32)],
        scratch_shapes=scratch,
    )(*args)


def _band_bwd(q, k, v, do, lse, delta, *, max_dist, name):
    dil, length, wq = q.shape
    wk = k.shape[2]
    gqa = wk != wq
    tq = min(ATTN_TILE, length)
    ns, nt = tq // BLOCK, length // tq
    npair = wq // LANES
    nblocks = length // BLOCK
    bias = _band_bias(max_dist, transposed=True)

    def body(q_ref, qn_ref, do_ref, don_ref, lse_ref, lsen_ref, dl_ref, dln_ref, k_ref, v_ref, bias_ref,
             dq_ref, dk_ref, dv_ref, stat_l, stat_d, dqt, kt, *rolled):
        i = pl.program_id(1)
        for pk in range(wk // LANES):
            kt[pk] = k_ref[:, pk * LANES:(pk + 1) * LANES].astype(F32).T.astype(BF16)
        if gqa:
            kroll, vroll, ktroll = rolled
            kroll[...] = pltpu.roll(k_ref[...], HEAD_DIM, 1)
            vroll[...] = pltpu.roll(v_ref[...], HEAD_DIM, 1)
            ktroll[0] = kroll[...].astype(F32).T.astype(BF16)
        for a in range(ns):
            rows = slice(a * BLOCK, (a + 1) * BLOCK)
            stat_l[a] = _rows_to_lanes(lse_ref[rows, :])
            stat_d[a] = _rows_to_lanes(dl_ref[rows, :])
        stat_l[ns] = _rows_to_lanes(lsen_ref[...])
        stat_d[ns] = _rows_to_lanes(dln_ref[...])

        @pl.when(i == 0)
        def _():
            dqt[:, :, 0:BLOCK] = jnp.zeros((npair, LANES, BLOCK), F32)

        @pl.when(i > 0)
        def _():
            dqt[:, :, 0:BLOCK] = dqt[:, :, tq:tq + BLOCK]

        dqt[:, :, BLOCK:] = jnp.zeros((npair, LANES, tq), F32)
        half2 = _half_masks(2 * BLOCK)
        row = lax.broadcasted_iota(jnp.int32, (LANES, BLOCK), 0)
        row_half = (row < HEAD_DIM, row >= HEAD_DIM)
        col_next = (lax.broadcasted_iota(jnp.int32, (1, 2 * BLOCK), 1) >= BLOCK).astype(F32)

        def scores(b):
            rows = slice(b * BLOCK, (b + 1) * BLOCK)
            nxt_rows = slice((b + 1) * BLOCK, (b + 2) * BLOCK)
            items = []
            for p in range(npair):
                lanes = slice(p * LANES, (p + 1) * LANES)
                q_next = q_ref[nxt_rows, lanes] if b + 1 < ns else qn_ref[:, lanes]
                do_next = do_ref[nxt_rows, lanes] if b + 1 < ns else don_ref[:, lanes]
                qw = jnp.concatenate([q_ref[rows, lanes], q_next], axis=0)
                dow = jnp.concatenate([do_ref[rows, lanes], do_next], axis=0)
                for e in range(2):
                    h = 2 * p + e
                    pk, ek = _kv_place(h, gqa)
                    klanes = slice(pk * LANES, (pk + 1) * LANES)
                    kb = (k_ref if ek == e else kroll)[rows, klanes]
                    vb = (v_ref if ek == e else vroll)[rows, klanes]
                    qm = jnp.where(half2[e], qw, jnp.zeros_like(qw))
                    dom = jnp.where(half2[e], dow, jnp.zeros_like(dow))
                    items.append(dict(p=p, e=e, h=h, pk=pk, ek=ek, qm=qm, dom=dom,
                                      st=_dot_nt(kb, qm), dpt=_dot_nt(vb, dom)))
            return items

        def probs(b, items):
            bt = bias_ref[...]
            if b == ns - 1:
                bt = bt + jnp.where(i == nt - 1, NEG, 0.0) * col_next
            for it in items:
                h = it["h"]
                lrow = jnp.concatenate([stat_l[b, h:h + 1, :], stat_l[b + 1, h:h + 1, :]], axis=1)
                drow = jnp.concatenate([stat_d[b, h:h + 1, :], stat_d[b + 1, h:h + 1, :]], axis=1)
                pt = jnp.exp(it["st"] + bt - lrow)
                it["ptb"] = pt.astype(BF16)
                it["dsb"] = (pt * (it["dpt"] - drow)).astype(BF16)

        pending = scores(0)
        for b in range(ns):
            rows = slice(b * BLOCK, (b + 1) * BLOCK)
            window = slice(b * BLOCK, (b + 2) * BLOCK)
            acc = {}
            items = pending
            probs(b, items)
            pending = scores(b + 1) if b + 1 < ns else None
            for p in range(npair):
                pair = items[2 * p:2 * p + 2]
                lanes = slice(p * LANES, (p + 1) * LANES)
                kparts = []
                for it in pair:
                    kbt = (kt if it["ek"] == it["e"] else ktroll)[it["pk"], :, rows]
                    kparts.append(jnp.where(row_half[it["e"]], kbt, jnp.zeros_like(kbt)))
                ds_keys = jnp.concatenate([it["dsb"] for it in pair], axis=0)
                dqt[p, :, window] += _dot(jnp.concatenate(kparts, axis=1), ds_keys)
                if not gqa:
                    q_both = jnp.concatenate([it["qm"] for it in pair], axis=0)
                    do_both = jnp.concatenate([it["dom"] for it in pair], axis=0)
                    dk_ref[rows, lanes] = _dot(jnp.concatenate([it["dsb"] for it in pair], axis=1), q_both).astype(BF16)
                    dv_ref[rows, lanes] = _dot(jnp.concatenate([it["ptb"] for it in pair], axis=1), do_both).astype(BF16)
                else:
                    for it in pair:
                        dv_c = _dot(it["ptb"], it["dom"])
                        dk_c = _dot(it["dsb"], it["qm"])
                        key = (it["pk"], it["ek"] == it["e"])
                        if key in acc:
                            acc[key] = (acc[key][0] + dk_c, acc[key][1] + dv_c)
                        else:
                            acc[key] = (dk_c, dv_c)
            if gqa:
                dk_al, dv_al = acc[(0, True)]
                dk_mis, dv_mis = acc[(0, False)]
                dk_ref[rows, :] = (dk_al + pltpu.roll(dk_mis, HEAD_DIM, 1)).astype(BF16)
                dv_ref[rows, :] = (dv_al + pltpu.roll(dv_mis, HEAD_DIM, 1)).astype(BF16)

        for p in range(npair):
            dq_ref[:, p * LANES:(p + 1) * LANES] = dqt[p, :, 0:tq].T.astype(BF16)

    main = lambda w: pl.BlockSpec((None, tq, w), lambda r, i: (r, i, 0))
    nxt = lambda w: pl.BlockSpec((None, BLOCK, w), lambda r, i: (r, jnp.minimum((i + 1) * ns, nblocks - 1), 0))
    scratch = [pltpu.VMEM((ns + 1, 8, LANES), F32), pltpu.VMEM((ns + 1, 8, LANES), F32),
               pltpu.VMEM((npair, LANES, tq + BLOCK), F32), pltpu.VMEM((wk // LANES, LANES, tq), BF16)]
    if gqa:
        scratch = scratch + [pltpu.VMEM((tq, wk), BF16)] * 2 + [pltpu.VMEM((1, LANES, tq), BF16)]
    return pl.pallas_call(
        body, name=name, grid=(dil, nt),
        in_specs=[main(wq), nxt(wq), main(wq), nxt(wq), main(LANES), nxt(LANES), main(LANES), nxt(LANES),
                  main(wk), main(wk), pl.BlockSpec(bias.shape, lambda r, i: (0, 0))],
        out_specs=[main(wq), main(wk), main(wk)],
        out_shape=[jax.ShapeDtypeStruct((dil, length, wq), BF16), jax.ShapeDtypeStruct((dil, length, wk), BF16),
                   jax.ShapeDtypeStruct((dil, length, wk), BF16)],
        scratch_shapes=scratch,
        compiler_params=pltpu.CompilerParams(dimension_semantics=("arbitrary", "arbitrary")),
    )(q, q, do, do, lse, lse, delta, delta, k, v, bias)


def _mem_attn_fwd(q, mk, mv):
    seq = q.shape[0]
    tq = min(ATTN_TILE, seq)
    sub_rows = min(4 * BLOCK, tq)
    ns = tq // sub_rows

    def body(q_ref, mk_ref, mv_ref, o_ref, lse_ref):
        half = _half_masks(sub_rows)

        def sub(a, carry):
            r0 = pl.multiple_of(a * sub_rows, sub_rows)
            scores = []
            for p in range(C_W // LANES):
                lanes = slice(p * LANES, (p + 1) * LANES)
                qp = q_ref[pl.ds(r0, sub_rows), lanes]
                for e in range(2):
                    scores.append(_dot_nt(jnp.where(half[e], qp, jnp.zeros_like(qp)), mk_ref[:, lanes]))
            m_cols, l_cols, probs = [], [], []
            for s in scores:
                m = jnp.max(s, axis=1, keepdims=True)
                pe = jnp.exp(s - m)
                probs.append(pe.astype(BF16))
                m_cols.append(m)
                l_cols.append(jnp.sum(pe, axis=1, keepdims=True))
            for p in range(C_W // LANES):
                lanes = slice(p * LANES, (p + 1) * LANES)
                o_h = [_dot(probs[2 * p + e], mv_ref[:, lanes]) * (1.0 / l_cols[2 * p + e]) for e in range(2)]
                o_ref[pl.ds(r0, sub_rows), lanes] = jnp.where(half[0], o_h[0], o_h[1]).astype(BF16)
            lse_ref[pl.ds(r0, sub_rows), :] = _per_head(m_cols) + jnp.log(_per_head(l_cols, 1.0))
            return carry

        lax.fori_loop(0, ns, sub, 0, unroll=True)

    row = lambda w: pl.BlockSpec((tq, w), lambda i: (i, 0))
    full = pl.BlockSpec((N_MEM, C_W), lambda i: (0, 0))
    return pl.pallas_call(
        body, name="mem_attn_fwd", grid=(seq // tq,), in_specs=[row(C_W), full, full],
        out_specs=[row(C_W), row(LANES)],
        out_shape=[jax.ShapeDtypeStruct((seq, C_W), BF16), jax.ShapeDtypeStruct((seq, LANES), F32)],
    )(q, mk, mv)


def _mem_attn_bwd(q, mk, mv, do, lse, delta):
    seq = q.shape[0]
    tq = min(ATTN_TILE, seq)
    ns = tq // BLOCK
    npair = C_W // LANES

    def body(q_ref, mk_ref, mv_ref, do_ref, lse_ref, dl_ref, dq_ref, dmk_ref, dmv_ref, stat_l, stat_d, mkt, dqt):
        @pl.when(pl.program_id(0) == 0)
        def _():
            dmk_ref[...] = jnp.zeros_like(dmk_ref)
            dmv_ref[...] = jnp.zeros_like(dmv_ref)
            for p in range(npair):
                mkt[p] = mk_ref[:, p * LANES:(p + 1) * LANES].astype(F32).T.astype(BF16)

        for a in range(ns):
            rows = slice(a * BLOCK, (a + 1) * BLOCK)
            stat_l[a] = _rows_to_lanes(lse_ref[rows, :])
            stat_d[a] = _rows_to_lanes(dl_ref[rows, :])
        span = min(2, ns)
        half = _half_masks(span * BLOCK)
        row = lax.broadcasted_iota(jnp.int32, (LANES, N_MEM), 0)
        row_half = (row < HEAD_DIM, row >= HEAD_DIM)

        for a in range(0, ns, span):
            rows = slice(a * BLOCK, (a + span) * BLOCK)
            items = []
            for p in range(npair):
                lanes = slice(p * LANES, (p + 1) * LANES)
                qp = q_ref[rows, lanes]
                dop = do_ref[rows, lanes]
                for e in range(2):
                    qm = jnp.where(half[e], qp, jnp.zeros_like(qp))
                    dom = jnp.where(half[e], dop, jnp.zeros_like(dop))
                    items.append(dict(p=p, e=e, qm=qm, dom=dom, st=_dot_nt(mk_ref[:, lanes], qm),
                                      dpt=_dot_nt(mv_ref[:, lanes], dom)))
            for it in items:
                h = 2 * it["p"] + it["e"]
                lrow = jnp.concatenate([stat_l[a + k, h:h + 1, :] for k in range(span)], axis=1)
                drow = jnp.concatenate([stat_d[a + k, h:h + 1, :] for k in range(span)], axis=1)
                pt = jnp.exp(it["st"] - lrow)
                it["ptb"] = pt.astype(BF16)
                it["dsb"] = (pt * (it["dpt"] - drow)).astype(BF16)
            for p in range(npair):
                lanes = slice(p * LANES, (p + 1) * LANES)
                pair = [it for it in items if it["p"] == p]
                join = lambda name, axis: jnp.concatenate([it[name] for it in pair], axis=axis)
                dmv_ref[:, lanes] += _dot(join("ptb", 1), join("dom", 0))
                dmk_ref[:, lanes] += _dot(join("dsb", 1), join("qm", 0))
                kbt = mkt[p]
                k_both = jnp.concatenate([jnp.where(row_half[e], kbt, jnp.zeros_like(kbt)) for e in range(2)], axis=1)
                dqt[p, :, rows] = _dot(k_both, join("dsb", 0))
        for p in range(npair):
            dq_ref[:, p * LANES:(p + 1) * LANES] = dqt[p].T.astype(BF16)

    row = lambda w: pl.BlockSpec((tq, w), lambda i: (i, 0))
    full = pl.BlockSpec((N_MEM, C_W), lambda i: (0, 0))
    return pl.pallas_call(
        body, name="mem_attn_bwd", grid=(seq // tq,),
        in_specs=[row(C_W), full, full, row(C_W), row(LANES), row(LANES)], out_specs=[row(C_W), full, full],
        out_shape=[jax.ShapeDtypeStruct((seq, C_W), BF16), jax.ShapeDtypeStruct((N_MEM, C_W), F32),
                   jax.ShapeDtypeStruct((N_MEM, C_W), F32)],
        scratch_shapes=[pltpu.VMEM((ns, 8, LANES), F32)] * 2
        + [pltpu.VMEM((npair, LANES, N_MEM), BF16), pltpu.VMEM((npair, LANES, tq), F32)],
        compiler_params=pltpu.CompilerParams(dimension_semantics=("arbitrary",)),
    )(q, mk, mv, do, lse, delta)


def _silu_and_grad(g):
    s = 1.0 / (1.0 + jnp.exp(-g))
    return g * s, s * (1.0 + g * (1.0 - s))


def _post(x, target, post_norm, w_out, sink_row, oa, lse_a, ga, ob_list, lseb_list, gb, oc, gc):
    seq = x.shape[0]
    tm = min(ROW_TILE, seq)
    inv_d = 1.0 / D_MODEL
    nd = len(B_DILS)

    def body(*refs):
        (x_ref, t_ref, gp_ref, w_ref, sink_ref, oa_ref, lsea_ref, ga_ref), refs = refs[:8], refs[8:]
        ob_refs, lb_refs, (gb_ref, oc_ref, gc_ref), refs = refs[:nd], refs[nd:2 * nd], refs[2 * nd:2 * nd + 3], refs[2 * nd + 3:]
        (g_ref, doa_ref, dla_ref, dga_ref), refs = refs[:4], refs[4:]
        dob_refs, lsec_refs, dlb_refs, refs = refs[:nd], refs[nd:2 * nd], refs[2 * nd:3 * nd], refs[3 * nd:]
        (dgb_ref, doc_ref, dlc_ref, dgc_ref, gw_ref, gpost_ref, gsink_ref, loss_ref), refs = refs[:8], refs[8:]
        ycat, obufs, lbufs, st_do, st_l, st_d = refs[0], refs[1:nd], refs[nd:2 * nd - 1], refs[2 * nd - 1], refs[2 * nd], refs[2 * nd + 1]

        @pl.when(pl.program_id(0) == 0)
        def _():
            gw_ref[...] = jnp.zeros_like(gw_ref)
            gpost_ref[...] = jnp.zeros_like(gpost_ref)
            gsink_ref[...] = jnp.zeros_like(gsink_ref)
            loss_ref[...] = jnp.zeros_like(loss_ref)

        o_i, l_i = [ob_refs[0][0].astype(F32)], [lb_refs[0][0]]
        for k in range(1, nd):
            _from_residues(ob_refs[k], obufs[k - 1], B_DILS[k])
            _from_residues(lb_refs[k], lbufs[k - 1], B_DILS[k])
            o_i.append(_stage_read(obufs[k - 1]))
            l_i.append(_stage_read(lbufs[k - 1]))
        mx = l_i[0]
        for l in l_i[1:]:
            mx = jnp.maximum(mx, l)
        w_i = [jnp.exp(l - mx) for l in l_i]
        z = w_i[0]
        for w in w_i[1:]:
            z = z + w
        _stage_write(st_l, mx + jnp.log(z))
        expand = _head_expand_matrix(B_W)
        inv_z = 1.0 / z
        ob = None
        for w, o in zip(w_i, o_i):
            term = _dot_split(w * inv_z, expand, 2) * o
            ob = term if ob is None else ob + term
        oa, oc = oa_ref[...].astype(F32), oc_ref[...].astype(F32)
        sa, dsa = _silu_and_grad(ga_ref[...].astype(F32))
        sb, dsb = _silu_and_grad(gb_ref[...].astype(F32))
        sc, dsc = _silu_and_grad(gc_ref[...].astype(F32))
        ycat[:, 0:A_W] = (oa * sa).astype(BF16)
        ycat[:, A_W:A_W + B_W] = (ob * sb).astype(BF16)
        ycat[:, A_W + B_W:] = (oc * sc).astype(BF16)
        y2 = _dot(ycat[...], w_ref[...])
        r = lax.rsqrt(jnp.mean(y2 * y2, axis=-1, keepdims=True) + RMS_EPS)
        zhat = y2 * r
        gp = gp_ref[...]
        err = x_ref[...] + zhat * gp - t_ref[...]
        loss_ref[...] += jnp.sum(err * err) * (0.5 * inv_d)
        g = err * inv_d
        g_ref[...] = g
        gpost_ref[...] += jnp.sum(g * zhat, axis=0, keepdims=True)
        a = g * gp
        dy2 = (r * (a - zhat * jnp.mean(a * zhat, axis=-1, keepdims=True))).astype(BF16)
        gw_ref[...] += _dot_tn(ycat[...], dy2)
        dycat = _dot_nt(dy2, w_ref[...])
        dya, dyb, dyc = dycat[:, 0:A_W], dycat[:, A_W:A_W + B_W], dycat[:, A_W + B_W:]
        doa, dob, doc = dya * sa, dyb * sb, dyc * sc
        doa_ref[...] = doa.astype(BF16)
        doc_ref[...] = doc.astype(BF16)
        dga_ref[...] = (dya * oa * dsa).astype(BF16)
        dgb_ref[...] = (dyb * ob * dsb).astype(BF16)
        dgc_ref[...] = (dyc * oc * dsc).astype(BF16)
        dl_a = _dot_split(doa * oa, _head_sum_matrix(A_W), 2)
        dla_ref[...] = dl_a
        dlc_ref[...] = _dot_split(doc * oc, _head_sum_matrix(C_W), 2)
        gsink_ref[...] += jnp.sum(jnp.exp(sink_ref[...] - lsea_ref[...]) * dl_a, axis=0, keepdims=True)
        _stage_write(st_do, dob)
        _stage_write(st_d, _dot_split(dob * ob, _head_sum_matrix(B_W), 2))
        for k, dil in enumerate(B_DILS):
            _to_residues(st_do, dob_refs[k], dil)
            _to_residues(st_l, lsec_refs[k], dil)
            _to_residues(st_d, dlb_refs[k], dil)

    row = lambda w: pl.BlockSpec((tm, w), lambda i: (i, 0))
    full = lambda shape: pl.BlockSpec(shape, lambda i: (0,) * len(shape))
    res_specs = lambda w: [_residue_spec(d, tm, w) for d in B_DILS]
    res_shapes = lambda w, dt: [jax.ShapeDtypeStruct((d, seq // d, w), dt) for d in B_DILS]
    ins = [x, target, post_norm, w_out, sink_row, oa, lse_a, ga, *ob_list, *lseb_list, gb, oc, gc]
    in_specs = ([row(D_MODEL), row(D_MODEL), full((1, D_MODEL)), full((D_MODEL, D_MODEL)), full((1, LANES)),
                 row(A_W), row(LANES), row(A_W)] + res_specs(B_W) + res_specs(LANES) + [row(B_W), row(C_W), row(C_W)])
    out_shape = ([jax.ShapeDtypeStruct((seq, D_MODEL), F32), jax.ShapeDtypeStruct((seq, A_W), BF16),
                  jax.ShapeDtypeStruct((seq, LANES), F32), jax.ShapeDtypeStruct((seq, A_W), BF16)]
                 + res_shapes(B_W, BF16) + res_shapes(LANES, F32) + res_shapes(LANES, F32)
                 + [jax.ShapeDtypeStruct((seq, B_W), BF16), jax.ShapeDtypeStruct((seq, C_W), BF16),
                    jax.ShapeDtypeStruct((seq, LANES), F32), jax.ShapeDtypeStruct((seq, C_W), BF16),
                    jax.ShapeDtypeStruct((D_MODEL, D_MODEL), F32), jax.ShapeDtypeStruct((1, D_MODEL), F32),
                    jax.ShapeDtypeStruct((1, LANES), F32), jax.ShapeDtypeStruct((1, LANES), F32)])
    out_specs = ([row(D_MODEL), row(A_W), row(LANES), row(A_W)] + res_specs(B_W) + res_specs(LANES) + res_specs(LANES)
                 + [row(B_W), row(C_W), row(LANES), row(C_W),
                    full((D_MODEL, D_MODEL)), full((1, D_MODEL)), full((1, LANES)), full((1, LANES))])
    scratch = ([pltpu.VMEM((tm, D_MODEL), BF16)] + [_stage(tm, B_W)] * (nd - 1) + [_stage(tm, LANES)] * (nd - 1)
               + [_stage(tm, B_W), _stage(tm, LANES), _stage(tm, LANES)])
    res = pl.pallas_call(
        body, name="post", grid=(seq // tm,), in_specs=in_specs, out_specs=out_specs, out_shape=out_shape,
        scratch_shapes=scratch,
        compiler_params=pltpu.CompilerParams(dimension_semantics=("arbitrary",)),
    )(*ins)
    out = dict(g=res[0], doa=res[1], dl_a=res[2], dga=res[3], dob=res[4:4 + nd], lse_b=res[4 + nd:4 + 2 * nd],
               dl_b=res[4 + 2 * nd:4 + 3 * nd])
    rest = res[4 + 3 * nd:]
    out.update(dgb=rest[0], doc=rest[1], dl_c=rest[2], dgc=rest[3], gw_out=rest[4], gpost=rest[5], gsink=rest[6],
               loss=rest[7])
    return out


def _grad_w_in(ut, nat, res):
    seq = ut.shape[1]
    tm = min(ROW_TILE, seq)
    nd = len(B_DILS)
    nat_list = [nat[n] for n in _NATURAL]
    res_list = [a for n in _DILATED for a in res[n]]
    rope = _rope_tables(seq, tm)

    def body(rl_ref, rb_ref, ut_ref, *refs):
        nat_refs = dict(zip(_NATURAL, refs[:len(_NATURAL)]))
        refs = refs[len(_NATURAL):]
        res_refs = {n: refs[nd * k:nd * (k + 1)] for k, n in enumerate(_DILATED)}
        refs = refs[nd * len(_DILATED):]
        dproj_ref, gw_ref = refs[:2]
        bufs = {n: refs[2 + (nd - 1) * k:2 + (nd - 1) * (k + 1)] for k, n in enumerate(_DILATED)}

        @pl.when(pl.program_id(0) == 0)
        def _():
            gw_ref[...] = jnp.zeros_like(gw_ref)

        for n in _DILATED:
            for k in range(1, nd):
                _from_residues(res_refs[n][k], bufs[n][k - 1], B_DILS[k])
        c, sm, sp = _rope_coeffs(rl_ref, rb_ref)
        sm, sp = -sm, -sp
        for blk, (name, off, roped, scaled) in enumerate(_PROJ_LAYOUT):
            lanes = slice(off, off + LANES)
            if name in nat_refs:
                piece = nat_refs[name][:, lanes].astype(F32)
            else:
                piece = res_refs[name][0][0, :, lanes].astype(F32)
                for buf in bufs[name]:
                    piece = piece + buf[off // LANES]
            if roped:
                piece = _rope(piece, c, sm, sp)
            if scaled:
                piece = piece * SCALE
            dproj_ref[:, blk * LANES:(blk + 1) * LANES] = piece.astype(BF16)
        for j in range(N_CHIPS):
            gw_ref[j] += _dot(ut_ref[...], dproj_ref[:, j * SHARD_IN:(j + 1) * SHARD_IN])

    row = lambda w: pl.BlockSpec((tm, w), lambda i: (i, 0))
    in_specs = ([pl.BlockSpec(rope[0].shape, lambda i: (0, 0)), pl.BlockSpec((8, 2 * LANES), lambda i: (i, 0)),
                 pl.BlockSpec((D_MODEL, tm), lambda i: (0, i))]
                + [row(a.shape[1]) for a in nat_list]
                + [_residue_spec(d, tm, B_W) for _ in _DILATED for d in B_DILS])
    return pl.pallas_call(
        body, name="grad_w_in", grid=(seq // tm,), in_specs=in_specs,
        out_specs=[row(D_IN), pl.BlockSpec((N_CHIPS, D_MODEL, SHARD_IN), lambda i: (0, 0, 0))],
        out_shape=[jax.ShapeDtypeStruct((seq, D_IN), BF16), jax.ShapeDtypeStruct((N_CHIPS, D_MODEL, SHARD_IN), F32)],
        scratch_shapes=[_stage(tm, B_W)] * ((nd - 1) * len(_DILATED)),
        compiler_params=pltpu.CompilerParams(dimension_semantics=("arbitrary",)),
    )(*rope, ut, *nat_list, *res_list)


def _input_grad(x, g, pre_norm, w_in_g, dproj, gx_prev, span, after, name):
    seq = x.shape[0]
    tm = seq // INPUT_GRAD_TILES
    first_block, steps = span

    def body(*refs):
        x_ref, g_ref, gp_ref, w_ref, dp_ref = refs[:5]
        gx_ref, gpre_ref = refs[-2:]

        @pl.when(pl.program_id(0) == 0)
        def _():
            gpre_ref[...] = jnp.zeros_like(gpre_ref)

        du = None
        for j in range(N_CHIPS):
            term = _dot_nt(dp_ref[:, j * SHARD_IN:(j + 1) * SHARD_IN], w_ref[j])
            du = term if du is None else du + term
        xv = x_ref[...]
        r = lax.rsqrt(jnp.mean(xv * xv, axis=-1, keepdims=True) + RMS_EPS)
        xhat = xv * r
        gpre_ref[...] += jnp.sum(du * xhat, axis=0, keepdims=True)
        a = du * gp_ref[...]
        gx_ref[...] = g_ref[...] + r * (a - xhat * jnp.mean(a * xhat, axis=-1, keepdims=True))

    row = lambda w: pl.BlockSpec((tm, w), lambda i: (first_block + i, 0))
    full = lambda a: pl.BlockSpec(a.shape, lambda i: (0,) * a.ndim)
    any_spec = pl.BlockSpec(memory_space=pl.ANY)
    ins = [x, g, pre_norm, w_in_g, dproj]
    in_specs = [row(D_MODEL), row(D_MODEL), full(pre_norm), full(w_in_g), row(D_IN)]
    aliases = {}
    if gx_prev is not None:
        aliases[len(ins)] = 0
        ins.append(gx_prev)
        in_specs.append(any_spec)
    if after is not None:
        ins.append(after)
        in_specs.append(any_spec)
    return pl.pallas_call(
        body, name=name, grid=(steps,), in_specs=in_specs,
        out_specs=[row(D_MODEL), pl.BlockSpec((1, D_MODEL), lambda i: (0, 0))],
        out_shape=[jax.ShapeDtypeStruct((seq, D_MODEL), F32), jax.ShapeDtypeStruct((1, D_MODEL), F32)],
        input_output_aliases=aliases,
        compiler_params=pltpu.CompilerParams(dimension_semantics=("arbitrary",)),
    )(*ins)


def _exchange_start(ex, name):
    n_in, n_out, n_sem = len(ex["ins"]), len(ex["outs"]), len(ex["sems"])

    def body(*refs):
        in_refs, land_refs, sems = refs[:n_in], refs[n_in:n_in + n_out], refs[n_in + n_out:n_in + n_out + n_sem]
        ex["start"](in_refs, land_refs, *sems)
        token = refs[-1]
        token[...] = jnp.zeros_like(token)

    hbm = pl.BlockSpec(memory_space=pltpu.HBM)
    sem = pl.BlockSpec(memory_space=pltpu.SEMAPHORE)
    ins = [pltpu.with_memory_space_constraint(a, pltpu.HBM) for a in ex["ins"]]
    landing = [pltpu.with_memory_space_constraint(lax.empty(o.shape, o.dtype), pltpu.HBM) for o in ex["outs"]]
    res = pl.pallas_call(
        body, name=name,
        out_shape=list(ex["sems"]) + [pltpu.HBM(a.shape, a.dtype) for a in ex["ins"]]
        + [pltpu.HBM(o.shape, o.dtype) for o in ex["outs"]] + [jax.ShapeDtypeStruct((8, LANES), F32)],
        in_specs=[hbm] * (n_in + n_out),
        out_specs=[sem] * n_sem + [hbm] * (n_in + n_out) + [pl.BlockSpec(memory_space=pltpu.VMEM)],
        input_output_aliases={k: n_sem + k for k in range(n_in + n_out)},
        compiler_params=pltpu.CompilerParams(has_side_effects=pltpu.SideEffectType.DATAFLOW_SIDE_EFFECTING),
    )(*ins, *landing)
    return res[:-1], res[-1]


def _exchange_wait(ex, handles, after, name):
    n_in, n_out, n_sem = len(ex["ins"]), len(ex["outs"]), len(ex["sems"])
    sems, thru = handles[:n_sem], handles[n_sem:]

    def body(*refs):
        in_refs, land_refs = refs[:n_in], refs[n_in:n_in + n_out]
        sem_refs = refs[n_in + n_out:n_in + n_out + n_sem]
        ex["finish"](in_refs, land_refs, *sem_refs)

    hbm = pl.BlockSpec(memory_space=pltpu.HBM)
    sem = pl.BlockSpec(memory_space=pltpu.SEMAPHORE)
    res = pl.pallas_call(
        body, name=name,
        out_shape=[pltpu.HBM(a.shape, a.dtype) for a in thru],
        in_specs=[hbm] * (n_in + n_out) + [sem] * n_sem + [pl.BlockSpec(memory_space=pl.ANY)],
        out_specs=[hbm] * (n_in + n_out),
        input_output_aliases={k: k for k in range(n_in + n_out)},
        compiler_params=pltpu.CompilerParams(has_side_effects=pltpu.SideEffectType.DATAFLOW_SIDE_EFFECTING),
    )(*thru, *sems, after)
    return res[:n_in], res[n_in:]


def _start_finish(build):
    def start(*refs):
        for cp in build(*refs):
            cp.start()

    def finish(*refs):
        for cp in build(*refs):
            cp.wait()

    return dict(start=start, finish=finish)


def _pair_exchange(grads):
    n = len(grads)

    def build(srcs, outs, send_sems, recv_sems):
        x, y, c = lax.axis_index("x"), lax.axis_index("y"), lax.axis_index("c")
        copies = []
        for t in range(n):
            rows = grads[t].shape[1] // 2
            copies.append(pltpu.make_async_remote_copy(
                src_ref=srcs[t].at[:, pl.ds((1 - c) * rows, rows)], dst_ref=outs[t],
                send_sem=send_sems.at[t], recv_sem=recv_sems.at[t], device_id=(x, y, 1 - c), device_id_type=MESH))
        return copies

    return dict(ins=list(grads), **_start_finish(build),
                outs=[jax.ShapeDtypeStruct((g.shape[0], g.shape[1] // 2, g.shape[2]), g.dtype) for g in grads],
                sems=[pltpu.SemaphoreType.DMA((n,)), pltpu.SemaphoreType.DMA((n,))])


def _pair_add(core, own, got):
    nchip, rows2, width = own.shape
    rows = rows2 // 2
    tr = min(ROW_TILE, rows)
    nb = rows // tr

    def body(core_ref, own_ref, got_ref, out_ref):
        out_ref[...] = (own_ref[...] + got_ref[...]).astype(BF16)

    grid_spec = pltpu.PrefetchScalarGridSpec(
        num_scalar_prefetch=1, grid=(nchip, nb),
        in_specs=[pl.BlockSpec((None, tr, width), lambda k, i, core_ref: (k, core_ref[0] * nb + i, 0)),
                  pl.BlockSpec((None, tr, width), lambda k, i, core_ref: (k, i, 0))],
        out_specs=pl.BlockSpec((None, tr, width), lambda k, i, core_ref: (k, i, 0)))
    return pl.pallas_call(
        body, name=f"pair_add_{width}", grid_spec=grid_spec,
        out_shape=jax.ShapeDtypeStruct((nchip, rows, width), BF16),
    )(core, own, got)


def _chip_exchange(parts):
    n = len(parts)

    def build(srcs, outs, send_sems, recv_sems, local_sems):
        x, y, c = lax.axis_index("x"), lax.axis_index("y"), lax.axis_index("c")
        my_chip = 2 * x + y
        chips = [(1 - x, y), (x, 1 - y), (1 - x, 1 - y)]
        copies = [pltpu.make_async_copy(srcs[t].at[my_chip], outs[t].at[my_chip], local_sems.at[t]) for t in range(n)]
        for j, (cx, cy) in enumerate(chips):
            for t in range(n):
                k = n * j + t
                copies.append(pltpu.make_async_remote_copy(
                    src_ref=srcs[t].at[2 * cx + cy], dst_ref=outs[t].at[my_chip], send_sem=send_sems.at[k],
                    recv_sem=recv_sems.at[k], device_id=(cx, cy, c), device_id_type=MESH))
        return copies

    return dict(ins=list(parts), **_start_finish(build), outs=[jax.ShapeDtypeStruct(p.shape, p.dtype) for p in parts],
                sems=[pltpu.SemaphoreType.DMA((3 * n,)), pltpu.SemaphoreType.DMA((3 * n,)),
                      pltpu.SemaphoreType.DMA((n,))])


def _slot_sum(slots, name, core=None):
    ns, rows, width = slots.shape
    tr = min(ROW_TILE, rows)

    def body(*refs):
        in_ref, out_ref = refs[-2:]
        acc = in_ref[0].astype(F32)
        for s in range(1, ns):
            acc = acc + in_ref[s].astype(F32)
        out_ref[...] = acc

    if core is None:
        return pl.pallas_call(
            body, name=name, grid=(rows // tr,),
            in_specs=[pl.BlockSpec((ns, tr, width), lambda i: (0, i, 0))],
            out_specs=pl.BlockSpec((tr, width), lambda i: (i, 0)),
            out_shape=jax.ShapeDtypeStruct((rows, width), F32),
        )(slots)
    grid_spec = pltpu.PrefetchScalarGridSpec(
        num_scalar_prefetch=1, grid=(rows // tr,),
        in_specs=[pl.BlockSpec((ns, tr, width), lambda i, core_ref: (0, i, 0))],
        out_specs=pl.BlockSpec((None, tr, width), lambda i, core_ref: (core_ref[0], i, 0)))
    return pl.pallas_call(
        body, name=name, grid_spec=grid_spec, out_shape=jax.ShapeDtypeStruct((2, rows, width), F32),
    )(core, slots)


def _pair_gather(bufs, small):
    n = len(bufs)

    def body(*refs):
        small_ref, outs, small_out = refs[n], refs[n + 1:2 * n + 1], refs[2 * n + 1]
        send_sems, recv_sems, local_sem = refs[2 * n + 2:]
        x, y, c = lax.axis_index("x"), lax.axis_index("y"), lax.axis_index("c")
        me = 4 * x + 2 * y + c
        chips = [(1 - x, y), (x, 1 - y), (1 - x, 1 - y)]
        mine = pltpu.make_async_copy(small_ref, small_out.at[me], local_sem)
        mine.start()
        copies = [pltpu.make_async_remote_copy(
            src_ref=outs[t].at[c], dst_ref=outs[t].at[c], send_sem=send_sems.at[t], recv_sem=recv_sems.at[t],
            device_id=(x, y, 1 - c), device_id_type=MESH) for t in range(n)]
        peers = [(x, y, 1 - c)] + [(cx, cy, cc) for (cx, cy) in chips for cc in (c, 1 - c)]
        for j, peer in enumerate(peers):
            copies.append(pltpu.make_async_remote_copy(
                src_ref=small_ref, dst_ref=small_out.at[me], send_sem=send_sems.at[n + j],
                recv_sem=recv_sems.at[n + j], device_id=peer, device_id_type=MESH))
        for cp in copies:
            cp.start()
        for cp in copies:
            cp.wait()
        mine.wait()

    any_spec = pl.BlockSpec(memory_space=pl.ANY)
    res = pl.pallas_call(
        body, name="pair_gather",
        out_shape=[jax.ShapeDtypeStruct(b.shape, b.dtype) for b in bufs]
        + [jax.ShapeDtypeStruct((8,) + small.shape, small.dtype)],
        in_specs=[any_spec] * (n + 1), out_specs=[any_spec] * (n + 1),
        input_output_aliases={t: t for t in range(n)},
        scratch_shapes=[pltpu.SemaphoreType.DMA((n + 7,)), pltpu.SemaphoreType.DMA((n + 7,)),
                        pltpu.SemaphoreType.DMA],
    )(*bufs, small)
    return [r.reshape(2 * b.shape[1], b.shape[2]) for r, b in zip(res[:n], bufs)], res[n]


def _adamw(w, g, m, v, name):
    rows, width = w.shape
    tr = min(ROW_TILE // 2, rows)
    c1 = 1.0 / (1.0 - ADAM_B1 ** ADAM_STEP)
    c2 = 1.0 / (1.0 - ADAM_B2 ** ADAM_STEP)

    def body(w_ref, g_ref, m_ref, v_ref, d_ref, nm_ref, nv_ref):
        gv = g_ref[...]
        nm = ADAM_B1 * m_ref[...] + (1.0 - ADAM_B1) * gv
        nv = ADAM_B2 * v_ref[...] + (1.0 - ADAM_B2) * (gv * gv)
        nm_ref[...] = nm
        nv_ref[...] = nv
        d_ref[...] = -ADAM_LR * ((nm * c1) / (jnp.sqrt(nv * c2) + ADAM_EPS) + ADAM_WD * w_ref[...])

    spec = pl.BlockSpec((tr, width), lambda i: (i, 0))
    return pl.pallas_call(
        body, name=name, grid=(rows // tr,), in_specs=[spec] * 4, out_specs=[spec] * 3,
        out_shape=[jax.ShapeDtypeStruct(w.shape, F32)] * 3,
    )(w, g, m, v)


def _local_step(x, mem, target, pre_norm, sink_a, mem_norm, post_norm, w_in_g, w_out, w_mkv, gathers=None):
    first_gather, late_gather = gathers if gathers else (None, None)
    u, ut, hosted = _pre_norm(x, pre_norm, first_gather)
    if gathers:
        w_in_g = hosted[0].reshape(N_CHIPS, D_MODEL, SHARD_IN)
    pr = _pre_proj(u, w_in_g, late_gather)
    pr["ut"] = ut
    if gathers:
        w_out, w_mkv = (g.reshape(D_MODEL, g.shape[-1]) for g in pr["hosted"])
    mk, mv = _mem_kv(mem, mem_norm, w_mkv)
    sink = sink_a.reshape(-1)
    qa, ka, va = pr["qa"][None], pr["ka"][None], pr["va"][None]
    oa, lse_a = _band_fwd(qa, ka, va, sink, max_dist=A_WINDOW - 1, name="swa_fwd")
    ob_list, lseb_list = [], []
    for k, (win, dil) in enumerate(B_CONFIGS):
        o_i, l_i = _band_fwd(pr["qb"][k], pr["kb"][k], pr["vb"][k], None, max_dist=win // dil, name=f"dil{dil}_fwd")
        ob_list.append(o_i)
        lseb_list.append(l_i)
    oc, lse_c = _mem_attn_fwd(pr["qc"], mk, mv)
    sink_row = jnp.pad(sink, (0, LANES - sink.shape[0])).reshape(1, LANES)
    po = _post(x, target, post_norm, w_out, sink_row, oa[0], lse_a[0], pr["ga"], ob_list, lseb_list, pr["gb"], oc,
               pr["gc"])
    dqc, dmk, dmv = _mem_attn_bwd(pr["qc"], mk, mv, po["doc"], lse_c, po["dl_c"])
    dqa, dka, dva = _band_bwd(qa, ka, va, po["doa"][None], lse_a, po["dl_a"][None], max_dist=A_WINDOW - 1,
                              name="swa_bwd")
    res = dict(qb=[], kb=[], vb=[])
    for k, (win, dil) in enumerate(B_CONFIGS):
        dq_i, dk_i, dv_i = _band_bwd(pr["qb"][k], pr["kb"][k], pr["vb"][k], po["dob"][k], po["lse_b"][k],
                                     po["dl_b"][k], max_dist=win // dil, name=f"dil{dil}_bwd")
        res["qb"].append(dq_i)
        res["kb"].append(dk_i)
        res["vb"].append(dv_i)
    nat = dict(qa=dqa[0], ka=dka[0], va=dva[0], ga=po["dga"], gb=po["dgb"], qc=dqc, gc=po["dgc"])
    dproj, gw_in = _grad_w_in(pr["ut"], nat, res)
    gw_mkv, gmem = _mem_kv_bwd(mem, mem_norm, w_mkv, dmk, dmv)
    gsink = -po["gsink"][0, :sink.shape[0]]
    return dict(loss=po["loss"], g=po["g"], dproj=dproj, gw_in=gw_in, gw_out=po["gw_out"], gw_mkv=gw_mkv,
                gpost=po["gpost"], gmem=gmem, gsink=gsink, w_in_g=w_in_g)


def kernel(x, mem, pre_norm, w_in, sink_a, mem_norm, w_mem_kv, w_out, post_norm, loss_target, m_pre_norm, m_w_in, m_sink_a, m_mem_norm, m_w_mem_kv, m_w_out, m_post_norm, v_pre_norm, v_w_in, v_sink_a, v_mem_norm, v_w_mem_kv, v_w_out, v_post_norm):
    gathers = (_gather_exchange([w_in[0].astype(BF16)]),
               _gather_exchange([w_out[0].astype(BF16), w_mem_kv[0].astype(BF16)]))
    loc = _local_step(x[0], mem[0], loss_target[0], pre_norm, sink_a, mem_norm, post_norm, None, None, None, gathers)
    big = [loc["gw_in"], loc["gw_out"].reshape(N_CHIPS, D_MODEL // N_CHIPS, D_MODEL),
           loc["gw_mkv"].reshape(N_CHIPS, D_MODEL // N_CHIPS, 2 * C_W)]
    core = lax.axis_index("c").astype(jnp.int32).reshape(1)
    w_in_full = loc["w_in_g"]
    step_in = (x[0], loc["g"], pre_norm, w_in_full, loc["dproj"])
    pair_ex = _pair_exchange(big)
    pair_handles, token = _exchange_start(pair_ex, "pair_exchange_start")
    gx_a, gpre_a = _input_grad(*step_in, None, (0, 2), token, "input_grad_a")
    big, got = _exchange_wait(pair_ex, pair_handles, gpre_a, "pair_exchange_wait")
    parts = [_pair_add(core, own, g) for own, g in zip(big, got)]
    chip_ex = _chip_exchange(parts)
    chip_handles, token = _exchange_start(chip_ex, "chip_exchange_start")
    grad_x, gpre_b = _input_grad(*step_in, gx_a, (2, 14), token, "input_grad_b")
    _, slots = _exchange_wait(chip_ex, chip_handles, gpre_b, "chip_exchange_wait")
    halves = [_slot_sum(s, name=f"chip_sum_{s.shape[2]}", core=core) for s in slots]
    widen = lambda a: jnp.pad(a.reshape(1, -1), ((0, 0), (0, D_MODEL - a.size)))
    small = jnp.concatenate([gpre_a, loc["gpost"], loc["gmem"], widen(loc["gsink"]), widen(loc["loss"]), gpre_b,
                             jnp.zeros((2, D_MODEL), F32)], axis=0)
    (g_in, g_out, g_mkv), small_slots = _pair_gather(halves, small)
    small_sum = _slot_sum(small_slots, name="device_sum")
    g_pre, g_post, g_mem = small_sum[0:1] + small_sum[5:6], small_sum[1:2], small_sum[2:3]
    g_sink = small_sum[3:4, :sink_a.shape[1]]
    loss = small_sum[4, 0]

    d_in, nm_in, nv_in = _adamw(w_in[0], g_in, m_w_in[0], v_w_in[0], "adamw_in")
    d_out, nm_out, nv_out = _adamw(w_out[0], g_out, m_w_out[0], v_w_out[0], "adamw_out")
    d_mkv, nm_mkv, nv_mkv = _adamw(w_mem_kv[0], g_mkv, m_w_mem_kv[0], v_w_mem_kv[0], "adamw_mkv")
    pad6 = lambda a: jnp.pad(a, ((0, 0), (0, D_MODEL - a.shape[1])))
    stack = lambda a, b, c_, d_: jnp.concatenate([a, b, c_, pad6(d_), jnp.zeros((4, D_MODEL), F32)], axis=0)
    d_s, nm_s, nv_s = _adamw(stack(pre_norm, post_norm, mem_norm, sink_a),
                             jnp.concatenate([g_pre, small_sum[1:]], axis=0),
                             stack(m_pre_norm, m_post_norm, m_mem_norm, m_sink_a),
                             stack(v_pre_norm, v_post_norm, v_mem_norm, v_sink_a), "adamw_small")
    ns_ = sink_a.shape[1]
    unpack = lambda a: (a[0:1], a[3:4, :ns_], a[2:3], a[1:2])
    d_pre, d_sink, d_mem, d_post = unpack(d_s)
    nm_pre, nm_sink, nm_mem, nm_post = unpack(nm_s)
    nv_pre, nv_sink, nv_mem, nv_post = unpack(nv_s)
    lead = lambda a: a[None]
    return (loss, lead(grad_x),
            g_pre, lead(g_in), g_sink, g_mem, lead(g_mkv), lead(g_out), g_post,
            d_pre, lead(d_in), d_sink, d_mem, lead(d_mkv), lead(d_out), d_post,
            nm_pre, lead(nm_in), nm_sink, nm_mem, lead(nm_mkv), lead(nm_out), nm_post,
            nv_pre, lead(nv_in), nv_sink, nv_mem, lead(nv_mkv), lead(nv_out), nv_post)
```

```python
import numpy as np
import jax
import jax.numpy as jnp
from jax import lax
from jax.experimental import pallas as pl
from jax.experimental.pallas import tpu as pltpu

F32 = jnp.float32
BF16 = jnp.bfloat16

D_MODEL = 1024
HEAD_DIM = 64
LANES = 128
BLOCK = 128
ROW_TILE = 512
ATTN_TILE = 1024
INPUT_GRAD_TILES = 16
A_W, A_KV_W, B_W, C_W = 384, 128, 384, 256
N_MEM = 256
D_IN = 3072
N_CHIPS = 4
SHARD_IN = D_IN // N_CHIPS
B_CONFIGS = ((128, 1), (512, 4), (2048, 16))
B_DILS = tuple(d for _, d in B_CONFIGS)
A_WINDOW = 128
RMS_EPS = 1e-6
ROPE_THETA = 500000.0
SCALE = HEAD_DIM ** -0.5
NEG = -1e30
ADAM_LR, ADAM_B1, ADAM_B2, ADAM_EPS, ADAM_WD, ADAM_STEP = 0.001, 0.9, 0.999, 1e-08, 0.01, 10

NT = (((1,), (1,)), ((), ()))
TN = (((0,), (0,)), ((), ()))
MESH = pl.DeviceIdType.MESH

_PROJ_LAYOUT = (
    [("qa", 128 * i, True, True) for i in range(3)] + [("ka", 0, True, False), ("va", 0, False, False)]
    + [("ga", 128 * i, False, False) for i in range(3)]
    + [("qb", 128 * i, True, True) for i in range(3)] + [("kb", 128 * i, True, False) for i in range(3)]
    + [("vb", 128 * i, False, False) for i in range(3)] + [("gb", 128 * i, False, False) for i in range(3)]
    + [("qc", 128 * i, False, True) for i in range(2)] + [("gc", 128 * i, False, False) for i in range(2)]
)
_PROJ_WIDTH = dict(qa=A_W, ka=A_KV_W, va=A_KV_W, ga=A_W, qb=B_W, kb=B_W, vb=B_W, gb=B_W, qc=C_W, gc=C_W)
_NATURAL = ("qa", "ka", "va", "ga", "gb", "qc", "gc")
_DILATED = ("qb", "kb", "vb")


def _dot(a, b):
    return jnp.dot(a, b, preferred_element_type=F32)


def _dot_nt(a, b):
    return lax.dot_general(a, b, NT, preferred_element_type=F32)


def _dot_tn(a, b):
    return lax.dot_general(a, b, TN, preferred_element_type=F32)


def _half_masks(rows):
    lane = lax.broadcasted_iota(jnp.int32, (rows, LANES), 1)
    return lane < HEAD_DIM, lane >= HEAD_DIM


def _rope(t, c, sm, sp):
    return t * c + pltpu.roll(t, LANES - 8, 1) * sm + pltpu.roll(t, 8, 1) * sp


def _rope_tables(seq, tm):
    dim = jnp.arange(LANES) % HEAD_DIM
    inv_freq = ROPE_THETA ** (-jnp.arange(0, 16, 2, dtype=F32) / 16)
    freq = jnp.where(dim < 16, inv_freq[dim % 8], 0.0)[None, :]
    local = jnp.arange(tm, dtype=F32)[:, None] * freq
    base = (jnp.arange(seq // tm, dtype=F32) * tm)[:, None] * freq
    both = lambda a: jnp.concatenate([jnp.cos(a), jnp.sin(a)], axis=1)
    return both(local), jnp.repeat(both(base), 8, axis=0)


def _rope_coeffs(local_ref, base_ref):
    cl, sl = local_ref[:, :LANES], local_ref[:, LANES:]
    cb, sb = base_ref[0:1, :LANES], base_ref[0:1, LANES:]
    cos = cb * cl - sb * sl
    sin = sb * cl + cb * sl
    dim = lax.broadcasted_iota(jnp.int32, (1, LANES), 1) % HEAD_DIM
    return cos, jnp.where(dim < 8, -sin, 0.0), jnp.where((dim >= 8) & (dim < 16), sin, 0.0)


def _split3(x):
    a = x.astype(BF16)
    r = x - a.astype(F32)
    b = r.astype(BF16)
    c = (r - b.astype(F32)).astype(BF16)
    return a, b, c


def _rows_to_lanes(x):
    row = lax.broadcasted_iota(jnp.int32, (8, LANES), 0)
    lane = lax.broadcasted_iota(jnp.int32, (8, LANES), 1)
    eye = (row == lane).astype(BF16)
    a, b, c = _split3(x)
    return _dot_nt(eye, a) + _dot_nt(eye, b) + _dot_nt(eye, c)


def _head_sum_matrix(width):
    k = lax.broadcasted_iota(jnp.int32, (width, LANES), 0)
    h = lax.broadcasted_iota(jnp.int32, (width, LANES), 1)
    return (k // HEAD_DIM == h).astype(BF16)


def _head_expand_matrix(width):
    h = lax.broadcasted_iota(jnp.int32, (LANES, width), 0)
    k = lax.broadcasted_iota(jnp.int32, (LANES, width), 1)
    return (k // HEAD_DIM == h).astype(BF16)


def _dot_split(x, mat, terms):
    parts = _split3(x)[:terms]
    out = _dot(parts[0], mat)
    for p in parts[1:]:
        out = out + _dot(p, mat)
    return out


def _per_head(cols, fill=0.0):
    rows = cols[0].shape[0]
    lane = lax.broadcasted_iota(jnp.int32, (rows, LANES), 1)
    out = jnp.full((rows, LANES), fill, F32)
    for h, col in enumerate(cols):
        out = jnp.where(lane == h, col, out)
    return out


def _lane_blocks(width):
    return [slice(p * LANES, (p + 1) * LANES) for p in range(width // LANES)]


def _stage(rows, width):
    return pltpu.VMEM((width // LANES, rows, LANES), F32)


def _stage_write(buf, value):
    for p, lanes in enumerate(_lane_blocks(value.shape[1])):
        buf[p] = value[:, lanes]


def _stage_read(buf):
    return jnp.concatenate([buf[p] for p in range(buf.shape[0])], axis=1) if buf.shape[0] > 1 else buf[0]


def _to_residues(buf, out_ref, dil):
    rows = buf.shape[1] // dil
    for r in range(dil):
        for p in range(buf.shape[0]):
            plane = buf.at[p]
            out_ref[r, :, p * LANES:(p + 1) * LANES] = plane[pl.ds(r, rows, stride=dil), :].astype(out_ref.dtype)


def _from_residues(in_ref, buf, dil):
    rows = buf.shape[1] // dil
    for r in range(dil):
        for p in range(buf.shape[0]):
            plane = buf.at[p]
            plane[pl.ds(r, rows, stride=dil), :] = in_ref[r, :, p * LANES:(p + 1) * LANES].astype(F32)


def _residue_spec(dil, tm, width):
    return pl.BlockSpec((dil, tm // dil, width), lambda i: (0, i, 0))


def _gather_exchange(shards_2d):
    shards = tuple(s.reshape(2, s.shape[0] // 2, s.shape[1]) for s in shards_2d)
    n = len(shards)

    def copies(in_refs, out_refs, send_sems, recv_sems):
        srcs, outs = in_refs[:n], out_refs
        x, y, c = lax.axis_index("x"), lax.axis_index("y"), lax.axis_index("c")
        my_chip = 2 * x + y
        sibling = (x, y, 1 - c)
        chips = [(1 - x, y), (x, 1 - y), (1 - x, 1 - y)]

        def copy(k, src, dst, to):
            return pltpu.make_async_remote_copy(src_ref=src, dst_ref=dst, send_sem=send_sems.at[k],
                                                recv_sem=recv_sems.at[k], device_id=to, device_id_type=MESH)

        first, arrive, passed, sibling_arrive = [], [], [], []
        for j, (cx, cy) in enumerate(chips):
            chip = 2 * cx + cy
            for t in range(n):
                k = n * j + t
                first.append(copy(k, srcs[t].at[c], outs[t].at[my_chip, c], (cx, cy, c)))
                arrive.append(copy(k, srcs[t].at[c], outs[t].at[chip, c], (cx, cy, c)))
                passed.append(copy(n * 3 + k, outs[t].at[chip, c], outs[t].at[chip, c], sibling))
                sibling_arrive.append(copy(n * 3 + k, outs[t].at[chip, 1 - c], outs[t].at[chip, 1 - c], sibling))
        return first, arrive, passed, sibling_arrive

    def start(*refs):
        for cp in copies(*refs)[0]:
            cp.start()

    def finish(*refs):
        first, arrive, passed, sibling_arrive = copies(*refs)
        for got, fwd in zip(arrive, passed):
            got.wait_recv()
            fwd.start()
        for cp in sibling_arrive:
            cp.wait_recv()
        for cp in first + passed:
            cp.wait_send()

    my_chip = 2 * lax.axis_index("x") + lax.axis_index("y")
    landing = [lax.dynamic_update_slice(jnp.zeros((N_CHIPS,) + s.shape, s.dtype), s[None], (my_chip, 0, 0, 0))
               for s in shards]
    return dict(ins=list(shards) + landing, start=start, finish=finish, aliases={n + t: t for t in range(n)},
                outs=[jax.ShapeDtypeStruct((N_CHIPS,) + s.shape, s.dtype) for s in shards],
                sems=[pltpu.SemaphoreType.DMA((6 * n,)), pltpu.SemaphoreType.DMA((6 * n,))])


def _mem_kv(mem, mem_norm, w_mkv):
    def body(mem_ref, g_ref, w_ref, mk_ref, mv_ref):
        m = mem_ref[...]
        r = lax.rsqrt(jnp.mean(m * m, axis=-1, keepdims=True) + RMS_EPS)
        mn = (m * r * g_ref[...]).astype(BF16)
        kv = _dot(mn, w_ref[...])
        mk_ref[...] = kv[:, :C_W].astype(BF16)
        mv_ref[...] = kv[:, C_W:].astype(BF16)

    return pl.pallas_call(
        body, name="mem_kv",
        out_shape=[jax.ShapeDtypeStruct((N_MEM, C_W), BF16)] * 2,
    )(mem, mem_norm, w_mkv)


def _mem_kv_bwd(mem, mem_norm, w_mkv, dmk, dmv):
    def body(mem_ref, g_ref, w_ref, dmk_ref, dmv_ref, gw_ref, gn_ref):
        m = mem_ref[...]
        r = lax.rsqrt(jnp.mean(m * m, axis=-1, keepdims=True) + RMS_EPS)
        mhat = m * r
        mn = (mhat * g_ref[...]).astype(BF16)
        dkv = jnp.concatenate([dmk_ref[...], dmv_ref[...]], axis=1).astype(BF16)
        gw_ref[...] = _dot_tn(mn, dkv).astype(BF16)
        dmn = _dot_nt(dkv, w_ref[...])
        gn_ref[...] = jnp.sum(dmn * mhat, axis=0, keepdims=True)

    return pl.pallas_call(
        body, name="mem_kv_bwd",
        out_shape=[jax.ShapeDtypeStruct((D_MODEL, 2 * C_W), BF16), jax.ShapeDtypeStruct((1, D_MODEL), F32)],
    )(mem, mem_norm, w_mkv, dmk, dmv)


def _pre_norm(x, pre_norm, host=None):
    seq = x.shape[0]
    tm = min(ROW_TILE, seq)
    n_host_in = len(host["ins"]) if host else 0
    n_host_out = len(host["outs"]) if host else 0

    def body(x_ref, g_ref, *refs):
        host_in, (u_ref, ut_ref), refs = refs[:n_host_in], refs[n_host_in:n_host_in + 2], refs[n_host_in + 2:]
        host_out, sems = refs[:n_host_out], refs[n_host_out:]
        if host:
            @pl.when(pl.program_id(0) == 0)
            def _():
                host["start"](host_in, host_out, *sems)

        xv = x_ref[...]
        r = lax.rsqrt(jnp.mean(xv * xv, axis=-1, keepdims=True) + RMS_EPS)
        u = xv * r * g_ref[...]
        u_ref[...] = u.astype(BF16)
        ut_ref[...] = u.T.astype(BF16)
        if host:
            @pl.when(pl.program_id(0) == seq // tm - 1)
            def _():
                host["finish"](host_in, host_out, *sems)

    any_spec = pl.BlockSpec(memory_space=pl.ANY)
    ins = [x, pre_norm]
    in_specs = [pl.BlockSpec((tm, D_MODEL), lambda i: (i, 0)), pl.BlockSpec(pre_norm.shape, lambda i: (0, 0))]
    out_shape = [jax.ShapeDtypeStruct((seq, D_MODEL), BF16), jax.ShapeDtypeStruct((D_MODEL, seq), BF16)]
    out_specs = [pl.BlockSpec((tm, D_MODEL), lambda i: (i, 0)), pl.BlockSpec((D_MODEL, tm), lambda i: (0, i))]
    aliases, scratch = {}, []
    if host:
        aliases = {len(ins) + k: 2 + v for k, v in host.get("aliases", {}).items()}
        ins += list(host["ins"])
        in_specs += [any_spec] * n_host_in
        out_shape += list(host["outs"])
        out_specs += [any_spec] * n_host_out
        scratch = list(host["sems"])
    res = pl.pallas_call(
        body, name="pre_norm", grid=(seq // tm,), in_specs=in_specs, out_specs=out_specs, out_shape=out_shape,
        input_output_aliases=aliases, scratch_shapes=scratch,
        compiler_params=pltpu.CompilerParams(dimension_semantics=("arbitrary",)),
    )(*ins)
    return res[0], res[1], res[2:]


def _pre_proj(u, w_in_g, host=None):
    seq = u.shape[0]
    tm = min(ROW_TILE, seq)
    n_nat, n_dil = len(_NATURAL), len(_DILATED) * len(B_DILS)
    rope = _rope_tables(seq, tm)

    n_host_in = len(host["ins"]) if host else 0
    n_host_out = len(host["outs"]) if host else 0
    n_own_out = n_nat + n_dil

    def body(u_ref, w_ref, rl_ref, rb_ref, *refs):
        host_in, refs = refs[:n_host_in], refs[n_host_in:]
        nat = dict(zip(_NATURAL, refs[:n_nat]))
        res = {n: refs[n_nat + len(B_DILS) * k:n_nat + len(B_DILS) * (k + 1)] for k, n in enumerate(_DILATED)}
        host_out = refs[n_own_out:n_own_out + n_host_out]
        bufs = dict(zip(_DILATED, refs[n_own_out + n_host_out:]))
        sems = refs[n_own_out + n_host_out + len(_DILATED):]
        if host:
            @pl.when(pl.program_id(0) == 0)
            def _():
                host["start"](host_in, host_out, *sems)

        ub = u_ref[...]
        c, sm, sp = _rope_coeffs(rl_ref, rb_ref)
        for j in range(N_CHIPS):
            pj = _dot(ub, w_ref[j])
            for b in range(SHARD_IN // LANES):
                name, off, roped, scaled = _PROJ_LAYOUT[(SHARD_IN // LANES) * j + b]
                piece = pj[:, LANES * b:LANES * (b + 1)]
                if roped:
                    piece = _rope(piece, c, sm, sp)
                if scaled:
                    piece = piece * SCALE
                if name in bufs:
                    bufs[name][off // LANES] = piece
                else:
                    nat[name][:, off:off + LANES] = piece.astype(BF16)
        for name in _DILATED:
            for ref, dil in zip(res[name], B_DILS):
                _to_residues(bufs[name], ref, dil)
        if host:
            @pl.when(pl.program_id(0) == seq // tm - 1)
            def _():
                host["finish"](host_in, host_out, *sems)

    row = lambda w: pl.BlockSpec((tm, w), lambda i: (i, 0))
    full = lambda a: pl.BlockSpec(a.shape, lambda i: (0,) * a.ndim)
    any_spec = pl.BlockSpec(memory_space=pl.ANY)
    out_shape = [jax.ShapeDtypeStruct((seq, _PROJ_WIDTH[n]), BF16) for n in _NATURAL]
    out_specs = [row(_PROJ_WIDTH[n]) for n in _NATURAL]
    for n in _DILATED:
        for dil in B_DILS:
            out_shape.append(jax.ShapeDtypeStruct((dil, seq // dil, B_W), BF16))
            out_specs.append(_residue_spec(dil, tm, B_W))
    ins = [u, w_in_g, *rope]
    in_specs = [row(D_MODEL), full(w_in_g), full(rope[0]), pl.BlockSpec((8, 2 * LANES), lambda i: (i, 0))]
    scratch = [_stage(tm, B_W)] * len(_DILATED)
    aliases = {}
    if host:
        aliases = {len(ins) + k: n_own_out + v for k, v in host.get("aliases", {}).items()}
        ins += list(host["ins"])
        in_specs += [any_spec] * n_host_in
        out_shape += list(host["outs"])
        out_specs += [any_spec] * n_host_out
        scratch += list(host["sems"])
    res = pl.pallas_call(
        body, name="pre_proj", grid=(seq // tm,), in_specs=in_specs, out_specs=out_specs, out_shape=out_shape,
        input_output_aliases=aliases, scratch_shapes=scratch,
        compiler_params=pltpu.CompilerParams(dimension_semantics=("arbitrary",)),
    )(*ins)
    out = dict(zip(_NATURAL, res[:n_nat]))
    for k, n in enumerate(_DILATED):
        out[n] = res[n_nat + len(B_DILS) * k:n_nat + len(B_DILS) * (k + 1)]
    out["hosted"] = res[n_own_out:]
    return out


def _band_bias(max_dist, transposed):
    i = np.arange(BLOCK)[:, None]
    j = np.arange(BLOCK)[None, :]
    if transposed:
        same = i <= j
        other = (j + BLOCK - i) <= max_dist
        vis = np.concatenate([same, other], axis=1)
    else:
        prev = (i + BLOCK - j) <= max_dist
        same = j <= i
        vis = np.concatenate([prev, same], axis=1)
    return jnp.asarray(np.where(vis, 0.0, NEG).astype(np.float32))


def _kv_place(h, gqa):
    return (0, h // 3) if gqa else (h // 2, h % 2)


def _band_fwd(q, k, v, sink, *, max_dist, name):
    dil, length, wq = q.shape
    wk = k.shape[2]
    gqa = wk != wq
    tq = min(ATTN_TILE, length)
    ns, nt = tq // BLOCK, length // tq
    npair = wq // LANES
    bias = _band_bias(max_dist, transposed=False)
    has_sink = sink is not None

    def body(*refs):
        if has_sink:
            sink_ref, refs = refs[0], refs[1:]
        q_ref, k_ref, kp_ref, v_ref, vp_ref, bias_ref, o_ref, lse_ref, kbuf, vbuf = refs[:10]
        i = pl.program_id(1)
        kbuf[0:BLOCK] = kp_ref[...]
        kbuf[BLOCK:] = k_ref[...]
        vbuf[0:BLOCK] = vp_ref[...]
        vbuf[BLOCK:] = v_ref[...]
        if gqa:
            kroll, vroll = refs[10:12]
            kroll[...] = pltpu.roll(kbuf[...], HEAD_DIM, 1)
            vroll[...] = pltpu.roll(vbuf[...], HEAD_DIM, 1)
        half = _half_masks(BLOCK)
        col_prev = (lax.broadcasted_iota(jnp.int32, (1, 2 * BLOCK), 1) < BLOCK).astype(F32)

        def score_matmuls(a):
            scores = []
            for p in range(npair):
                qp = q_ref[a * BLOCK:(a + 1) * BLOCK, p * LANES:(p + 1) * LANES]
                for e in range(2):
                    pk, ek = _kv_place(2 * p + e, gqa)
                    kw = (kbuf if ek == e else kroll)[a * BLOCK:(a + 2) * BLOCK, pk * LANES:(pk + 1) * LANES]
                    scores.append(_dot_nt(jnp.where(half[e], qp, jnp.zeros_like(qp)), kw))
            return scores

        pending = score_matmuls(0)
        for a in range(ns):
            r0 = a * BLOCK
            b = bias_ref[...]
            if a == 0:
                b = b + jnp.where(i == 0, NEG, 0.0) * col_prev
            scores = pending
            m_cols, l_cols, probs = [], [], []
            for h, s in enumerate(scores):
                s = s + b
                m = jnp.max(s, axis=1, keepdims=True)
                if has_sink:
                    m = jnp.maximum(m, sink_ref[h])
                pe = jnp.exp(s - m)
                l = jnp.sum(pe, axis=1, keepdims=True)
                if has_sink:
                    l = l + jnp.exp(sink_ref[h] - m)
                probs.append(pe.astype(BF16))
                m_cols.append(m)
                l_cols.append(l)
            pending = score_matmuls(a + 1) if a + 1 < ns else None
            for p in range(npair):
                o_h = []
                for e in range(2):
                    h = 2 * p + e
                    pk, ek = _kv_place(h, gqa)
                    vw = (vbuf if ek == e else vroll)[r0:r0 + 2 * BLOCK, pk * LANES:(pk + 1) * LANES]
                    o_h.append(_dot(probs[h], vw) * (1.0 / l_cols[h]))
                o_ref[r0:r0 + BLOCK, p * LANES:(p + 1) * LANES] = jnp.where(half[0], o_h[0], o_h[1]).astype(BF16)
            lse_ref[r0:r0 + BLOCK, :] = _per_head(m_cols) + jnp.log(_per_head(l_cols, 1.0))

    main = lambda w: pl.BlockSpec((None, tq, w), lambda r, i: (r, i, 0))
    prev = lambda w: pl.BlockSpec((None, BLOCK, w), lambda r, i: (r, jnp.maximum(i * ns - 1, 0), 0))
    in_specs = [main(wq), main(wk), prev(wk), main(wk), prev(wk), pl.BlockSpec(bias.shape, lambda r, i: (0, 0))]
    args = [q, k, k, v, v, bias]
    if has_sink:
        in_specs = [pl.BlockSpec(memory_space=pltpu.SMEM)] + in_specs
        args = [sink] + args
    scratch = [pltpu.VMEM((tq + BLOCK, wk), BF16)] * (4 if gqa else 2)
    return pl.pallas_call(
        body, name=name, grid=(dil, nt), in_specs=in_specs,
        out_specs=[main(wq), main(LANES)],
        out_shape=[jax.ShapeDtypeStruct((dil, length, wq), BF16), jax.ShapeDtypeStruct((dil, length, LANES), F32)],
        scratch_shapes=scratch,
    )(*args)


def _band_bwd(q, k, v, do, lse, delta, *, max_dist, name):
    dil, length, wq = q.shape
    wk = k.shape[2]
    gqa = wk != wq
    tq = min(ATTN_TILE, length)
    ns, nt = tq // BLOCK, length // tq
    npair = wq // LANES
    nblocks = length // BLOCK
    bias = _band_bias(max_dist, transposed=True)

    def body(q_ref, qn_ref, do_ref, don_ref, lse_ref, lsen_ref, dl_ref, dln_ref, k_ref, v_ref, bias_ref,
             dq_ref, dk_ref, dv_ref, stat_l, stat_d, dqt, kt, *rolled):
        i = pl.program_id(1)
        for pk in range(wk // LANES):
            kt[pk] = k_ref[:, pk * LANES:(pk + 1) * LANES].astype(F32).T.astype(BF16)
        if gqa:
            kroll, vroll, ktroll = rolled
            kroll[...] = pltpu.roll(k_ref[...], HEAD_DIM, 1)
            vroll[...] = pltpu.roll(v_ref[...], HEAD_DIM, 1)
            ktroll[0] = kroll[...].astype(F32).T.astype(BF16)
        for a in range(ns):
            rows = slice(a * BLOCK, (a + 1) * BLOCK)
            stat_l[a] = _rows_to_lanes(lse_ref[rows, :])
            stat_d[a] = _rows_to_lanes(dl_ref[rows, :])
        stat_l[ns] = _rows_to_lanes(lsen_ref[...])
        stat_d[ns] = _rows_to_lanes(dln_ref[...])

        @pl.when(i == 0)
        def _():
            dqt[:, :, 0:BLOCK] = jnp.zeros((npair, LANES, BLOCK), F32)

        @pl.when(i > 0)
        def _():
            dqt[:, :, 0:BLOCK] = dqt[:, :, tq:tq + BLOCK]

        dqt[:, :, BLOCK:] = jnp.zeros((npair, LANES, tq), F32)
        half2 = _half_masks(2 * BLOCK)
        row = lax.broadcasted_iota(jnp.int32, (LANES, BLOCK), 0)
        row_half = (row < HEAD_DIM, row >= HEAD_DIM)
        col_next = (lax.broadcasted_iota(jnp.int32, (1, 2 * BLOCK), 1) >= BLOCK).astype(F32)

        def scores(b):
            rows = slice(b * BLOCK, (b + 1) * BLOCK)
            nxt_rows = slice((b + 1) * BLOCK, (b + 2) * BLOCK)
            items = []
            for p in range(npair):
                lanes = slice(p * LANES, (p + 1) * LANES)
                q_next = q_ref[nxt_rows, lanes] if b + 1 < ns else qn_ref[:, lanes]
                do_next = do_ref[nxt_rows, lanes] if b + 1 < ns else don_ref[:, lanes]
                qw = jnp.concatenate([q_ref[rows, lanes], q_next], axis=0)
                dow = jnp.concatenate([do_ref[rows, lanes], do_next], axis=0)
                for e in range(2):
                    h = 2 * p + e
                    pk, ek = _kv_place(h, gqa)
                    klanes = slice(pk * LANES, (pk + 1) * LANES)
                    kb = (k_ref if ek == e else kroll)[rows, klanes]
                    vb = (v_ref if ek == e else vroll)[rows, klanes]
                    qm = jnp.where(half2[e], qw, jnp.zeros_like(qw))
                    dom = jnp.where(half2[e], dow, jnp.zeros_like(dow))
                    items.append(dict(p=p, e=e, h=h, pk=pk, ek=ek, qm=qm, dom=dom,
                                      st=_dot_nt(kb, qm), dpt=_dot_nt(vb, dom)))
            return items

        def probs(b, items):
            bt = bias_ref[...]
            if b == ns - 1:
                bt = bt + jnp.where(i == nt - 1, NEG, 0.0) * col_next
            for it in items:
                h = it["h"]
                lrow = jnp.concatenate([stat_l[b, h:h + 1, :], stat_l[b + 1, h:h + 1, :]], axis=1)
                drow = jnp.concatenate([stat_d[b, h:h + 1, :], stat_d[b + 1, h:h + 1, :]], axis=1)
                pt = jnp.exp(it["st"] + bt - lrow)
                it["ptb"] = pt.astype(BF16)
                it["dsb"] = (pt * (it["dpt"] - drow)).astype(BF16)

        pending = scores(0)
        for b in range(ns):
            rows = slice(b * BLOCK, (b + 1) * BLOCK)
            window = slice(b * BLOCK, (b + 2) * BLOCK)
            acc = {}
            items = pending
            probs(b, items)
            pending = scores(b + 1) if b + 1 < ns else None
            for p in range(npair):
                pair = items[2 * p:2 * p + 2]
                lanes = slice(p * LANES, (p + 1) * LANES)
                kparts = []
                for it in pair:
                    kbt = (kt if it["ek"] == it["e"] else ktroll)[it["pk"], :, rows]
                    kparts.append(jnp.where(row_half[it["e"]], kbt, jnp.zeros_like(kbt)))
                ds_keys = jnp.concatenate([it["dsb"] for it in pair], axis=0)
                dqt[p, :, window] += _dot(jnp.concatenate(kparts, axis=1), ds_keys)
                if not gqa:
                    q_both = jnp.concatenate([it["qm"] for it in pair], axis=0)
                    do_both = jnp.concatenate([it["dom"] for it in pair], axis=0)
                    dk_ref[rows, lanes] = _dot(jnp.concatenate([it["dsb"] for it in pair], axis=1), q_both).astype(BF16)
                    dv_ref[rows, lanes] = _dot(jnp.concatenate([it["ptb"] for it in pair], axis=1), do_both).astype(BF16)
                else:
                    for it in pair:
                        dv_c = _dot(it["ptb"], it["dom"])
                        dk_c = _dot(it["dsb"], it["qm"])
                        key = (it["pk"], it["ek"] == it["e"])
                        if key in acc:
                            acc[key] = (acc[key][0] + dk_c, acc[key][1] + dv_c)
                        else:
                            acc[key] = (dk_c, dv_c)
            if gqa:
                dk_al, dv_al = acc[(0, True)]
                dk_mis, dv_mis = acc[(0, False)]
                dk_ref[rows, :] = (dk_al + pltpu.roll(dk_mis, HEAD_DIM, 1)).astype(BF16)
                dv_ref[rows, :] = (dv_al + pltpu.roll(dv_mis, HEAD_DIM, 1)).astype(BF16)

        for p in range(npair):
            dq_ref[:, p * LANES:(p + 1) * LANES] = dqt[p, :, 0:tq].T.astype(BF16)

    main = lambda w: pl.BlockSpec((None, tq, w), lambda r, i: (r, i, 0))
    nxt = lambda w: pl.BlockSpec((None, BLOCK, w), lambda r, i: (r, jnp.minimum((i + 1) * ns, nblocks - 1), 0))
    scratch = [pltpu.VMEM((ns + 1, 8, LANES), F32), pltpu.VMEM((ns + 1, 8, LANES), F32),
               pltpu.VMEM((npair, LANES, tq + BLOCK), F32), pltpu.VMEM((wk // LANES, LANES, tq), BF16)]
    if gqa:
        scratch = scratch + [pltpu.VMEM((tq, wk), BF16)] * 2 + [pltpu.VMEM((1, LANES, tq), BF16)]
    return pl.pallas_call(
        body, name=name, grid=(dil, nt),
        in_specs=[main(wq), nxt(wq), main(wq), nxt(wq), main(LANES), nxt(LANES), main(LANES), nxt(LANES),
                  main(wk), main(wk), pl.BlockSpec(bias.shape, lambda r, i: (0, 0))],
        out_specs=[main(wq), main(wk), main(wk)],
        out_shape=[jax.ShapeDtypeStruct((dil, length, wq), BF16), jax.ShapeDtypeStruct((dil, length, wk), BF16),
                   jax.ShapeDtypeStruct((dil, length, wk), BF16)],
        scratch_shapes=scratch,
        compiler_params=pltpu.CompilerParams(dimension_semantics=("arbitrary", "arbitrary")),
    )(q, q, do, do, lse, lse, delta, delta, k, v, bias)


def _mem_attn_fwd(q, mk, mv):
    seq = q.shape[0]
    tq = min(ATTN_TILE, seq)
    sub_rows = min(4 * BLOCK, tq)
    ns = tq // sub_rows

    def body(q_ref, mk_ref, mv_ref, o_ref, lse_ref):
        half = _half_masks(sub_rows)

        def sub(a, carry):
            r0 = pl.multiple_of(a * sub_rows, sub_rows)
            scores = []
            for p in range(C_W // LANES):
                lanes = slice(p * LANES, (p + 1) * LANES)
                qp = q_ref[pl.ds(r0, sub_rows), lanes]
                for e in range(2):
                    scores.append(_dot_nt(jnp.where(half[e], qp, jnp.zeros_like(qp)), mk_ref[:, lanes]))
            m_cols, l_cols, probs = [], [], []
            for s in scores:
                m = jnp.max(s, axis=1, keepdims=True)
                pe = jnp.exp(s - m)
                probs.append(pe.astype(BF16))
                m_cols.append(m)
                l_cols.append(jnp.sum(pe, axis=1, keepdims=True))
            for p in range(C_W // LANES):
                lanes = slice(p * LANES, (p + 1) * LANES)
                o_h = [_dot(probs[2 * p + e], mv_ref[:, lanes]) * (1.0 / l_cols[2 * p + e]) for e in range(2)]
                o_ref[pl.ds(r0, sub_rows), lanes] = jnp.where(half[0], o_h[0], o_h[1]).astype(BF16)
            lse_ref[pl.ds(r0, sub_rows), :] = _per_head(m_cols) + jnp.log(_per_head(l_cols, 1.0))
            return carry

        lax.fori_loop(0, ns, sub, 0, unroll=True)

    row = lambda w: pl.BlockSpec((tq, w), lambda i: (i, 0))
    full = pl.BlockSpec((N_MEM, C_W), lambda i: (0, 0))
    return pl.pallas_call(
        body, name="mem_attn_fwd", grid=(seq // tq,), in_specs=[row(C_W), full, full],
        out_specs=[row(C_W), row(LANES)],
        out_shape=[jax.ShapeDtypeStruct((seq, C_W), BF16), jax.ShapeDtypeStruct((seq, LANES), F32)],
    )(q, mk, mv)


def _mem_attn_bwd(q, mk, mv, do, lse, delta):
    seq = q.shape[0]
    tq = min(ATTN_TILE, seq)
    ns = tq // BLOCK
    npair = C_W // LANES

    def body(q_ref, mk_ref, mv_ref, do_ref, lse_ref, dl_ref, dq_ref, dmk_ref, dmv_ref, stat_l, stat_d, mkt, dqt):
        @pl.when(pl.program_id(0) == 0)
        def _():
            dmk_ref[...] = jnp.zeros_like(dmk_ref)
            dmv_ref[...] = jnp.zeros_like(dmv_ref)
            for p in range(npair):
                mkt[p] = mk_ref[:, p * LANES:(p + 1) * LANES].astype(F32).T.astype(BF16)

        for a in range(ns):
            rows = slice(a * BLOCK, (a + 1) * BLOCK)
            stat_l[a] = _rows_to_lanes(lse_ref[rows, :])
            stat_d[a] = _rows_to_lanes(dl_ref[rows, :])
        span = min(2, ns)
        half = _half_masks(span * BLOCK)
        row = lax.broadcasted_iota(jnp.int32, (LANES, N_MEM), 0)
        row_half = (row < HEAD_DIM, row >= HEAD_DIM)

        for a in range(0, ns, span):
            rows = slice(a * BLOCK, (a + span) * BLOCK)
            items = []
            for p in range(npair):
                lanes = slice(p * LANES, (p + 1) * LANES)
                qp = q_ref[rows, lanes]
                dop = do_ref[rows, lanes]
                for e in range(2):
                    qm = jnp.where(half[e], qp, jnp.zeros_like(qp))
                    dom = jnp.where(half[e], dop, jnp.zeros_like(dop))
                    items.append(dict(p=p, e=e, qm=qm, dom=dom, st=_dot_nt(mk_ref[:, lanes], qm),
                                      dpt=_dot_nt(mv_ref[:, lanes], dom)))
            for it in items:
                h = 2 * it["p"] + it["e"]
                lrow = jnp.concatenate([stat_l[a + k, h:h + 1, :] for k in range(span)], axis=1)
                drow = jnp.concatenate([stat_d[a + k, h:h + 1, :] for k in range(span)], axis=1)
                pt = jnp.exp(it["st"] - lrow)
                it["ptb"] = pt.astype(BF16)
                it["dsb"] = (pt * (it["dpt"] - drow)).astype(BF16)
            for p in range(npair):
                lanes = slice(p * LANES, (p + 1) * LANES)
                pair = [it for it in items if it["p"] == p]
                join = lambda name, axis: jnp.concatenate([it[name] for it in pair], axis=axis)
                dmv_ref[:, lanes] += _dot(join("ptb", 1), join("dom", 0))
                dmk_ref[:, lanes] += _dot(join("dsb", 1), join("qm", 0))
                kbt = mkt[p]
                k_both = jnp.concatenate([jnp.where(row_half[e], kbt, jnp.zeros_like(kbt)) for e in range(2)], axis=1)
                dqt[p, :, rows] = _dot(k_both, join("dsb", 0))
        for p in range(npair):
            dq_ref[:, p * LANES:(p + 1) * LANES] = dqt[p].T.astype(BF16)

    row = lambda w: pl.BlockSpec((tq, w), lambda i: (i, 0))
    full = pl.BlockSpec((N_MEM, C_W), lambda i: (0, 0))
    return pl.pallas_call(
        body, name="mem_attn_bwd", grid=(seq // tq,),
        in_specs=[row(C_W), full, full, row(C_W), row(LANES), row(LANES)], out_specs=[row(C_W), full, full],
        out_shape=[jax.ShapeDtypeStruct((seq, C_W), BF16), jax.ShapeDtypeStruct((N_MEM, C_W), F32),
                   jax.ShapeDtypeStruct((N_MEM, C_W), F32)],
        scratch_shapes=[pltpu.VMEM((ns, 8, LANES), F32)] * 2
        + [pltpu.VMEM((npair, LANES, N_MEM), BF16), pltpu.VMEM((npair, LANES, tq), F32)],
        compiler_params=pltpu.CompilerParams(dimension_semantics=("arbitrary",)),
    )(q, mk, mv, do, lse, delta)


def _silu_and_grad(g):
    s = 1.0 / (1.0 + jnp.exp(-g))
    return g * s, s * (1.0 + g * (1.0 - s))


def _post(x, target, post_norm, w_out, sink_row, oa, lse_a, ga, ob_list, lseb_list, gb, oc, gc):
    seq = x.shape[0]
    tm = min(ROW_TILE, seq)
    inv_d = 1.0 / D_MODEL
    nd = len(B_DILS)

    def body(*refs):
        (x_ref, t_ref, gp_ref, w_ref, sink_ref, oa_ref, lsea_ref, ga_ref), refs = refs[:8], refs[8:]
        ob_refs, lb_refs, (gb_ref, oc_ref, gc_ref), refs = refs[:nd], refs[nd:2 * nd], refs[2 * nd:2 * nd + 3], refs[2 * nd + 3:]
        (g_ref, doa_ref, dla_ref, dga_ref), refs = refs[:4], refs[4:]
        dob_refs, lsec_refs, dlb_refs, refs = refs[:nd], refs[nd:2 * nd], refs[2 * nd:3 * nd], refs[3 * nd:]
        (dgb_ref, doc_ref, dlc_ref, dgc_ref, gw_ref, gpost_ref, gsink_ref, loss_ref), refs = refs[:8], refs[8:]
        ycat, obufs, lbufs, st_do, st_l, st_d = refs[0], refs[1:nd], refs[nd:2 * nd - 1], refs[2 * nd - 1], refs[2 * nd], refs[2 * nd + 1]
        gw_acc = refs[2 * nd + 2]

        @pl.when(pl.program_id(0) == 0)
        def _():
            gw_acc[...] = jnp.zeros_like(gw_acc)
            gpost_ref[...] = jnp.zeros_like(gpost_ref)
            gsink_ref[...] = jnp.zeros_like(gsink_ref)
            loss_ref[...] = jnp.zeros_like(loss_ref)

        o_i, l_i = [ob_refs[0][0].astype(F32)], [lb_refs[0][0]]
        for k in range(1, nd):
            _from_residues(ob_refs[k], obufs[k - 1], B_DILS[k])
            _from_residues(lb_refs[k], lbufs[k - 1], B_DILS[k])
            o_i.append(_stage_read(obufs[k - 1]))
            l_i.append(_stage_read(lbufs[k - 1]))
        mx = l_i[0]
        for l in l_i[1:]:
            mx = jnp.maximum(mx, l)
        w_i = [jnp.exp(l - mx) for l in l_i]
        z = w_i[0]
        for w in w_i[1:]:
            z = z + w
        _stage_write(st_l, mx + jnp.log(z))
        expand = _head_expand_matrix(B_W)
        inv_z = 1.0 / z
        ob = None
        for w, o in zip(w_i, o_i):
            term = _dot_split(w * inv_z, expand, 2) * o
            ob = term if ob is None else ob + term
        oa, oc = oa_ref[...].astype(F32), oc_ref[...].astype(F32)
        sa, dsa = _silu_and_grad(ga_ref[...].astype(F32))
        sb, dsb = _silu_and_grad(gb_ref[...].astype(F32))
        sc, dsc = _silu_and_grad(gc_ref[...].astype(F32))
        ycat[:, 0:A_W] = (oa * sa).astype(BF16)
        ycat[:, A_W:A_W + B_W] = (ob * sb).astype(BF16)
        ycat[:, A_W + B_W:] = (oc * sc).astype(BF16)
        y2 = _dot(ycat[...], w_ref[...])
        r = lax.rsqrt(jnp.mean(y2 * y2, axis=-1, keepdims=True) + RMS_EPS)
        zhat = y2 * r
        gp = gp_ref[...]
        err = x_ref[...] + zhat * gp - t_ref[...]
        loss_ref[...] += jnp.sum(err * err) * (0.5 * inv_d)
        g = err * inv_d
        g_ref[...] = g
        gpost_ref[...] += jnp.sum(g * zhat, axis=0, keepdims=True)
        a = g * gp
        dy2 = (r * (a - zhat * jnp.mean(a * zhat, axis=-1, keepdims=True))).astype(BF16)
        gw_acc[...] += _dot_tn(ycat[...], dy2)

        @pl.when(pl.program_id(0) == seq // tm - 1)
        def _():
            gw_ref[...] = gw_acc[...].astype(BF16)
        dycat = _dot_nt(dy2, w_ref[...])
        dya, dyb, dyc = dycat[:, 0:A_W], dycat[:, A_W:A_W + B_W], dycat[:, A_W + B_W:]
        doa, dob, doc = dya * sa, dyb * sb, dyc * sc
        doa_ref[...] = doa.astype(BF16)
        doc_ref[...] = doc.astype(BF16)
        dga_ref[...] = (dya * oa * dsa).astype(BF16)
        dgb_ref[...] = (dyb * ob * dsb).astype(BF16)
        dgc_ref[...] = (dyc * oc * dsc).astype(BF16)
        dl_a = _dot_split(doa * oa, _head_sum_matrix(A_W), 2)
        dla_ref[...] = dl_a
        dlc_ref[...] = _dot_split(doc * oc, _head_sum_matrix(C_W), 2)
        gsink_ref[...] += jnp.sum(jnp.exp(sink_ref[...] - lsea_ref[...]) * dl_a, axis=0, keepdims=True)
        _stage_write(st_do, dob)
        _stage_write(st_d, _dot_split(dob * ob, _head_sum_matrix(B_W), 2))
        for k, dil in enumerate(B_DILS):
            _to_residues(st_do, dob_refs[k], dil)
            _to_residues(st_l, lsec_refs[k], dil)
            _to_residues(st_d, dlb_refs[k], dil)

    row = lambda w: pl.BlockSpec((tm, w), lambda i: (i, 0))
    full = lambda shape: pl.BlockSpec(shape, lambda i: (0,) * len(shape))
    res_specs = lambda w: [_residue_spec(d, tm, w) for d in B_DILS]
    res_shapes = lambda w, dt: [jax.ShapeDtypeStruct((d, seq // d, w), dt) for d in B_DILS]
    ins = [x, target, post_norm, w_out, sink_row, oa, lse_a, ga, *ob_list, *lseb_list, gb, oc, gc]
    in_specs = ([row(D_MODEL), row(D_MODEL), full((1, D_MODEL)), full((D_MODEL, D_MODEL)), full((1, LANES)),
                 row(A_W), row(LANES), row(A_W)] + res_specs(B_W) + res_specs(LANES) + [row(B_W), row(C_W), row(C_W)])
    out_shape = ([jax.ShapeDtypeStruct((seq, D_MODEL), F32), jax.ShapeDtypeStruct((seq, A_W), BF16),
                  jax.ShapeDtypeStruct((seq, LANES), F32), jax.ShapeDtypeStruct((seq, A_W), BF16)]
                 + res_shapes(B_W, BF16) + res_shapes(LANES, F32) + res_shapes(LANES, F32)
                 + [jax.ShapeDtypeStruct((seq, B_W), BF16), jax.ShapeDtypeStruct((seq, C_W), BF16),
                    jax.ShapeDtypeStruct((seq, LANES), F32), jax.ShapeDtypeStruct((seq, C_W), BF16),
                    jax.ShapeDtypeStruct((D_MODEL, D_MODEL), BF16), jax.ShapeDtypeStruct((1, D_MODEL), F32),
                    jax.ShapeDtypeStruct((1, LANES), F32), jax.ShapeDtypeStruct((1, LANES), F32)])
    out_specs = ([row(D_MODEL), row(A_W), row(LANES), row(A_W)] + res_specs(B_W) + res_specs(LANES) + res_specs(LANES)
                 + [row(B_W), row(C_W), row(LANES), row(C_W),
                    full((D_MODEL, D_MODEL)), full((1, D_MODEL)), full((1, LANES)), full((1, LANES))])
    scratch = ([pltpu.VMEM((tm, D_MODEL), BF16)] + [_stage(tm, B_W)] * (nd - 1) + [_stage(tm, LANES)] * (nd - 1)
               + [_stage(tm, B_W), _stage(tm, LANES), _stage(tm, LANES), pltpu.VMEM((D_MODEL, D_MODEL), F32)])
    res = pl.pallas_call(
        body, name="post", grid=(seq // tm,), in_specs=in_specs, out_specs=out_specs, out_shape=out_shape,
        scratch_shapes=scratch,
        compiler_params=pltpu.CompilerParams(dimension_semantics=("arbitrary",)),
    )(*ins)
    out = dict(g=res[0], doa=res[1], dl_a=res[2], dga=res[3], dob=res[4:4 + nd], lse_b=res[4 + nd:4 + 2 * nd],
               dl_b=res[4 + 2 * nd:4 + 3 * nd])
    rest = res[4 + 3 * nd:]
    out.update(dgb=rest[0], doc=rest[1], dl_c=rest[2], dgc=rest[3], gw_out=rest[4], gpost=rest[5], gsink=rest[6],
               loss=rest[7])
    return out


def _grad_w_in(ut, nat, res):
    seq = ut.shape[1]
    tm = min(ROW_TILE, seq)
    nd = len(B_DILS)
    nat_list = [nat[n] for n in _NATURAL]
    res_list = [a for n in _DILATED for a in res[n]]
    rope = _rope_tables(seq, tm)

    def body(rl_ref, rb_ref, ut_ref, *refs):
        nat_refs = dict(zip(_NATURAL, refs[:len(_NATURAL)]))
        refs = refs[len(_NATURAL):]
        res_refs = {n: refs[nd * k:nd * (k + 1)] for k, n in enumerate(_DILATED)}
        refs = refs[nd * len(_DILATED):]
        dproj_ref, gw_ref, gw_acc = refs[:3]
        bufs = {n: refs[3 + (nd - 1) * k:3 + (nd - 1) * (k + 1)] for k, n in enumerate(_DILATED)}

        @pl.when(pl.program_id(0) == 0)
        def _():
            gw_acc[...] = jnp.zeros_like(gw_acc)

        for n in _DILATED:
            for k in range(1, nd):
                _from_residues(res_refs[n][k], bufs[n][k - 1], B_DILS[k])
        c, sm, sp = _rope_coeffs(rl_ref, rb_ref)
        sm, sp = -sm, -sp
        for blk, (name, off, roped, scaled) in enumerate(_PROJ_LAYOUT):
            lanes = slice(off, off + LANES)
            if name in nat_refs:
                piece = nat_refs[name][:, lanes].astype(F32)
            else:
                piece = res_refs[name][0][0, :, lanes].astype(F32)
                for buf in bufs[name]:
                    piece = piece + buf[off // LANES]
            if roped:
                piece = _rope(piece, c, sm, sp)
            if scaled:
                piece = piece * SCALE
            dproj_ref[:, blk * LANES:(blk + 1) * LANES] = piece.astype(BF16)
        for j in range(N_CHIPS):
            gw_acc[j] += _dot(ut_ref[...], dproj_ref[:, j * SHARD_IN:(j + 1) * SHARD_IN])

        @pl.when(pl.program_id(0) == seq // tm - 1)
        def _():
            gw_ref[...] = gw_acc[...].astype(BF16)

    row = lambda w: pl.BlockSpec((tm, w), lambda i: (i, 0))
    in_specs = ([pl.BlockSpec(rope[0].shape, lambda i: (0, 0)), pl.BlockSpec((8, 2 * LANES), lambda i: (i, 0)),
                 pl.BlockSpec((D_MODEL, tm), lambda i: (0, i))]
                + [row(a.shape[1]) for a in nat_list]
                + [_residue_spec(d, tm, B_W) for _ in _DILATED for d in B_DILS])
    return pl.pallas_call(
        body, name="grad_w_in", grid=(seq // tm,), in_specs=in_specs,
        out_specs=[row(D_IN), pl.BlockSpec((N_CHIPS, D_MODEL, SHARD_IN), lambda i: (0, 0, 0))],
        out_shape=[jax.ShapeDtypeStruct((seq, D_IN), BF16), jax.ShapeDtypeStruct((N_CHIPS, D_MODEL, SHARD_IN), BF16)],
        scratch_shapes=[pltpu.VMEM((N_CHIPS, D_MODEL, SHARD_IN), F32)] + [_stage(tm, B_W)] * ((nd - 1) * len(_DILATED)),
        compiler_params=pltpu.CompilerParams(dimension_semantics=("arbitrary",)),
    )(*rope, ut, *nat_list, *res_list)


def _input_grad(x, g, pre_norm, w_in_g, dproj, gx_prev, span, after, name):
    seq = x.shape[0]
    tm = seq // INPUT_GRAD_TILES
    first_block, steps = span

    def body(*refs):
        x_ref, g_ref, gp_ref, w_ref, dp_ref = refs[:5]
        gx_ref, gpre_ref = refs[-2:]

        @pl.when(pl.program_id(0) == 0)
        def _():
            gpre_ref[...] = jnp.zeros_like(gpre_ref)

        du = None
        for j in range(N_CHIPS):
            term = _dot_nt(dp_ref[:, j * SHARD_IN:(j + 1) * SHARD_IN], w_ref[j])
            du = term if du is None else du + term
        xv = x_ref[...]
        r = lax.rsqrt(jnp.mean(xv * xv, axis=-1, keepdims=True) + RMS_EPS)
        xhat = xv * r
        gpre_ref[...] += jnp.sum(du * xhat, axis=0, keepdims=True)
        a = du * gp_ref[...]
        gx_ref[...] = g_ref[...] + r * (a - xhat * jnp.mean(a * xhat, axis=-1, keepdims=True))

    row = lambda w: pl.BlockSpec((tm, w), lambda i: (first_block + i, 0))
    full = lambda a: pl.BlockSpec(a.shape, lambda i: (0,) * a.ndim)
    any_spec = pl.BlockSpec(memory_space=pl.ANY)
    ins = [x, g, pre_norm, w_in_g, dproj]
    in_specs = [row(D_MODEL), row(D_MODEL), full(pre_norm), full(w_in_g), row(D_IN)]
    aliases = {}
    if gx_prev is not None:
        aliases[len(ins)] = 0
        ins.append(gx_prev)
        in_specs.append(any_spec)
    if after is not None:
        ins.append(after)
        in_specs.append(any_spec)
    return pl.pallas_call(
        body, name=name, grid=(steps,), in_specs=in_specs,
        out_specs=[row(D_MODEL), pl.BlockSpec((1, D_MODEL), lambda i: (0, 0))],
        out_shape=[jax.ShapeDtypeStruct((seq, D_MODEL), F32), jax.ShapeDtypeStruct((1, D_MODEL), F32)],
        input_output_aliases=aliases,
        compiler_params=pltpu.CompilerParams(dimension_semantics=("arbitrary",)),
    )(*ins)


def _exchange_start(ex, name):
    n_in, n_out, n_sem = len(ex["ins"]), len(ex["outs"]), len(ex["sems"])

    def body(*refs):
        in_refs, land_refs, sems = refs[:n_in], refs[n_in:n_in + n_out], refs[n_in + n_out:n_in + n_out + n_sem]
        ex["start"](in_refs, land_refs, *sems)
        token = refs[-1]
        token[...] = jnp.zeros_like(token)

    hbm = pl.BlockSpec(memory_space=pltpu.HBM)
    sem = pl.BlockSpec(memory_space=pltpu.SEMAPHORE)
    ins = [pltpu.with_memory_space_constraint(a, pltpu.HBM) for a in ex["ins"]]
    landing = [pltpu.with_memory_space_constraint(lax.empty(o.shape, o.dtype), pltpu.HBM) for o in ex["outs"]]
    res = pl.pallas_call(
        body, name=name,
        out_shape=list(ex["sems"]) + [pltpu.HBM(a.shape, a.dtype) for a in ex["ins"]]
        + [pltpu.HBM(o.shape, o.dtype) for o in ex["outs"]] + [jax.ShapeDtypeStruct((8, LANES), F32)],
        in_specs=[hbm] * (n_in + n_out),
        out_specs=[sem] * n_sem + [hbm] * (n_in + n_out) + [pl.BlockSpec(memory_space=pltpu.VMEM)],
        input_output_aliases={k: n_sem + k for k in range(n_in + n_out)},
        compiler_params=pltpu.CompilerParams(has_side_effects=pltpu.SideEffectType.DATAFLOW_SIDE_EFFECTING),
    )(*ins, *landing)
    return res[:-1], res[-1]


def _exchange_wait(ex, handles, after, name):
    n_in, n_out, n_sem = len(ex["ins"]), len(ex["outs"]), len(ex["sems"])
    sems, thru = handles[:n_sem], handles[n_sem:]

    def body(*refs):
        in_refs, land_refs = refs[:n_in], refs[n_in:n_in + n_out]
        sem_refs = refs[n_in + n_out:n_in + n_out + n_sem]
        ex["finish"](in_refs, land_refs, *sem_refs)

    hbm = pl.BlockSpec(memory_space=pltpu.HBM)
    sem = pl.BlockSpec(memory_space=pltpu.SEMAPHORE)
    res = pl.pallas_call(
        body, name=name,
        out_shape=[pltpu.HBM(a.shape, a.dtype) for a in thru],
        in_specs=[hbm] * (n_in + n_out) + [sem] * n_sem + [pl.BlockSpec(memory_space=pl.ANY)],
        out_specs=[hbm] * (n_in + n_out),
        input_output_aliases={k: k for k in range(n_in + n_out)},
        compiler_params=pltpu.CompilerParams(has_side_effects=pltpu.SideEffectType.DATAFLOW_SIDE_EFFECTING),
    )(*thru, *sems, after)
    return res[:n_in], res[n_in:]


def _start_finish(build):
    def start(*refs):
        for cp in build(*refs):
            cp.start()

    def finish(*refs):
        for cp in build(*refs):
            cp.wait()

    return dict(start=start, finish=finish)


def _pair_exchange(grads):
    n = len(grads)

    def build(srcs, outs, send_sems, recv_sems):
        x, y, c = lax.axis_index("x"), lax.axis_index("y"), lax.axis_index("c")
        copies = []
        for t in range(n):
            rows = grads[t].shape[1] // 2
            copies.append(pltpu.make_async_remote_copy(
                src_ref=srcs[t].at[:, pl.ds((1 - c) * rows, rows)], dst_ref=outs[t],
                send_sem=send_sems.at[t], recv_sem=recv_sems.at[t], device_id=(x, y, 1 - c), device_id_type=MESH))
        return copies

    return dict(ins=list(grads), **_start_finish(build),
                outs=[jax.ShapeDtypeStruct((g.shape[0], g.shape[1] // 2, g.shape[2]), g.dtype) for g in grads],
                sems=[pltpu.SemaphoreType.DMA((n,)), pltpu.SemaphoreType.DMA((n,))])


def _pair_add(core, own, got):
    nchip, rows2, width = own.shape
    rows = rows2 // 2
    tr = min(ROW_TILE, rows)
    nb = rows // tr

    def body(core_ref, own_ref, got_ref, out_ref):
        out_ref[...] = (own_ref[...].astype(F32) + got_ref[...].astype(F32)).astype(BF16)

    grid_spec = pltpu.PrefetchScalarGridSpec(
        num_scalar_prefetch=1, grid=(nchip, nb),
        in_specs=[pl.BlockSpec((None, tr, width), lambda k, i, core_ref: (k, core_ref[0] * nb + i, 0)),
                  pl.BlockSpec((None, tr, width), lambda k, i, core_ref: (k, i, 0))],
        out_specs=pl.BlockSpec((None, tr, width), lambda k, i, core_ref: (k, i, 0)))
    return pl.pallas_call(
        body, name=f"pair_add_{width}", grid_spec=grid_spec,
        out_shape=jax.ShapeDtypeStruct((nchip, rows, width), BF16),
    )(core, own, got)


def _chip_exchange(parts):
    n = len(parts)

    def build(srcs, outs, send_sems, recv_sems, local_sems):
        x, y, c = lax.axis_index("x"), lax.axis_index("y"), lax.axis_index("c")
        my_chip = 2 * x + y
        chips = [(1 - x, y), (x, 1 - y), (1 - x, 1 - y)]
        copies = [pltpu.make_async_copy(srcs[t].at[my_chip], outs[t].at[my_chip], local_sems.at[t]) for t in range(n)]
        for j, (cx, cy) in enumerate(chips):
            for t in range(n):
                k = n * j + t
                copies.append(pltpu.make_async_remote_copy(
                    src_ref=srcs[t].at[2 * cx + cy], dst_ref=outs[t].at[my_chip], send_sem=send_sems.at[k],
                    recv_sem=recv_sems.at[k], device_id=(cx, cy, c), device_id_type=MESH))
        return copies

    return dict(ins=list(parts), **_start_finish(build), outs=[jax.ShapeDtypeStruct(p.shape, p.dtype) for p in parts],
                sems=[pltpu.SemaphoreType.DMA((3 * n,)), pltpu.SemaphoreType.DMA((3 * n,)),
                      pltpu.SemaphoreType.DMA((n,))])


def _slot_sum(slots, name, core=None):
    ns, rows, width = slots.shape
    tr = min(ROW_TILE, rows)

    def body(*refs):
        in_ref, out_ref = refs[-2:]
        acc = in_ref[0].astype(F32)
        for s in range(1, ns):
            acc = acc + in_ref[s].astype(F32)
        out_ref[...] = acc

    if core is None:
        return pl.pallas_call(
            body, name=name, grid=(rows // tr,),
            in_specs=[pl.BlockSpec((ns, tr, width), lambda i: (0, i, 0))],
            out_specs=pl.BlockSpec((tr, width), lambda i: (i, 0)),
            out_shape=jax.ShapeDtypeStruct((rows, width), F32),
        )(slots)
    grid_spec = pltpu.PrefetchScalarGridSpec(
        num_scalar_prefetch=1, grid=(rows // tr,),
        in_specs=[pl.BlockSpec((ns, tr, width), lambda i, core_ref: (0, i, 0))],
        out_specs=pl.BlockSpec((None, tr, width), lambda i, core_ref: (core_ref[0], i, 0)))
    return pl.pallas_call(
        body, name=name, grid_spec=grid_spec, out_shape=jax.ShapeDtypeStruct((2, rows, width), F32),
    )(core, slots)


def _pair_gather(bufs, small):
    n = len(bufs)

    def body(*refs):
        small_ref, outs, small_out = refs[n], refs[n + 1:2 * n + 1], refs[2 * n + 1]
        send_sems, recv_sems, local_sem = refs[2 * n + 2:]
        x, y, c = lax.axis_index("x"), lax.axis_index("y"), lax.axis_index("c")
        me = 4 * x + 2 * y + c
        chips = [(1 - x, y), (x, 1 - y), (1 - x, 1 - y)]
        mine = pltpu.make_async_copy(small_ref, small_out.at[me], local_sem)
        mine.start()
        copies = [pltpu.make_async_remote_copy(
            src_ref=outs[t].at[c], dst_ref=outs[t].at[c], send_sem=send_sems.at[t], recv_sem=recv_sems.at[t],
            device_id=(x, y, 1 - c), device_id_type=MESH) for t in range(n)]
        peers = [(x, y, 1 - c)] + [(cx, cy, cc) for (cx, cy) in chips for cc in (c, 1 - c)]
        for j, peer in enumerate(peers):
            copies.append(pltpu.make_async_remote_copy(
                src_ref=small_ref, dst_ref=small_out.at[me], send_sem=send_sems.at[n + j],
                recv_sem=recv_sems.at[n + j], device_id=peer, device_id_type=MESH))
        for cp in copies:
            cp.start()
        for cp in copies:
            cp.wait()
        mine.wait()

    any_spec = pl.BlockSpec(memory_space=pl.ANY)
    res = pl.pallas_call(
        body, name="pair_gather",
        out_shape=[jax.ShapeDtypeStruct(b.shape, b.dtype) for b in bufs]
        + [jax.ShapeDtypeStruct((8,) + small.shape, small.dtype)],
        in_specs=[any_spec] * (n + 1), out_specs=[any_spec] * (n + 1),
        input_output_aliases={t: t for t in range(n)},
        scratch_shapes=[pltpu.SemaphoreType.DMA((n + 7,)), pltpu.SemaphoreType.DMA((n + 7,)),
                        pltpu.SemaphoreType.DMA],
    )(*bufs, small)
    return [r.reshape(2 * b.shape[1], b.shape[2]) for r, b in zip(res[:n], bufs)], res[n]


def _adamw(w, g, m, v, name):
    rows, width = w.shape
    tr = min(ROW_TILE // 2, rows)
    c1 = 1.0 / (1.0 - ADAM_B1 ** ADAM_STEP)
    c2 = 1.0 / (1.0 - ADAM_B2 ** ADAM_STEP)

    def body(w_ref, g_ref, m_ref, v_ref, d_ref, nm_ref, nv_ref):
        gv = g_ref[...]
        nm = ADAM_B1 * m_ref[...] + (1.0 - ADAM_B1) * gv
        nv = ADAM_B2 * v_ref[...] + (1.0 - ADAM_B2) * (gv * gv)
        nm_ref[...] = nm
        nv_ref[...] = nv
        d_ref[...] = -ADAM_LR * ((nm * c1) / (jnp.sqrt(nv * c2) + ADAM_EPS) + ADAM_WD * w_ref[...])

    spec = pl.BlockSpec((tr, width), lambda i: (i, 0))
    return pl.pallas_call(
        body, name=name, grid=(rows // tr,), in_specs=[spec] * 4, out_specs=[spec] * 3,
        out_shape=[jax.ShapeDtypeStruct(w.shape, F32)] * 3,
    )(w, g, m, v)


def _local_step(x, mem, target, pre_norm, sink_a, mem_norm, post_norm, w_in_g, w_out, w_mkv, gathers=None):
    first_gather, late_gather = gathers if gathers else (None, None)
    u, ut, hosted = _pre_norm(x, pre_norm, first_gather)
    if gathers:
        w_in_g = hosted[0].reshape(N_CHIPS, D_MODEL, SHARD_IN)
    pr = _pre_proj(u, w_in_g, late_gather)
    pr["ut"] = ut
    if gathers:
        w_out, w_mkv = (g.reshape(D_MODEL, g.shape[-1]) for g in pr["hosted"])
    mk, mv = _mem_kv(mem, mem_norm, w_mkv)
    sink = sink_a.reshape(-1)
    qa, ka, va = pr["qa"][None], pr["ka"][None], pr["va"][None]
    oa, lse_a = _band_fwd(qa, ka, va, sink, max_dist=A_WINDOW - 1, name="swa_fwd")
    ob_list, lseb_list = [], []
    for k, (win, dil) in enumerate(B_CONFIGS):
        o_i, l_i = _band_fwd(pr["qb"][k], pr["kb"][k], pr["vb"][k], None, max_dist=win // dil, name=f"dil{dil}_fwd")
        ob_list.append(o_i)
        lseb_list.append(l_i)
    oc, lse_c = _mem_attn_fwd(pr["qc"], mk, mv)
    sink_row = jnp.pad(sink, (0, LANES - sink.shape[0])).reshape(1, LANES)
    po = _post(x, target, post_norm, w_out, sink_row, oa[0], lse_a[0], pr["ga"], ob_list, lseb_list, pr["gb"], oc,
               pr["gc"])
    dqc, dmk, dmv = _mem_attn_bwd(pr["qc"], mk, mv, po["doc"], lse_c, po["dl_c"])
    dqa, dka, dva = _band_bwd(qa, ka, va, po["doa"][None], lse_a, po["dl_a"][None], max_dist=A_WINDOW - 1,
                              name="swa_bwd")
    res = dict(qb=[], kb=[], vb=[])
    for k, (win, dil) in enumerate(B_CONFIGS):
        dq_i, dk_i, dv_i = _band_bwd(pr["qb"][k], pr["kb"][k], pr["vb"][k], po["dob"][k], po["lse_b"][k],
                                     po["dl_b"][k], max_dist=win // dil, name=f"dil{dil}_bwd")
        res["qb"].append(dq_i)
        res["kb"].append(dk_i)
        res["vb"].append(dv_i)
    nat = dict(qa=dqa[0], ka=dka[0], va=dva[0], ga=po["dga"], gb=po["dgb"], qc=dqc, gc=po["dgc"])
    dproj, gw_in = _grad_w_in(pr["ut"], nat, res)
    gw_mkv, gmem = _mem_kv_bwd(mem, mem_norm, w_mkv, dmk, dmv)
    gsink = -po["gsink"][0, :sink.shape[0]]
    return dict(loss=po["loss"], g=po["g"], dproj=dproj, gw_in=gw_in, gw_out=po["gw_out"], gw_mkv=gw_mkv,
                gpost=po["gpost"], gmem=gmem, gsink=gsink, w_in_g=w_in_g)


def kernel(x, mem, pre_norm, w_in, sink_a, mem_norm, w_mem_kv, w_out, post_norm, loss_target, m_pre_norm, m_w_in, m_sink_a, m_mem_norm, m_w_mem_kv, m_w_out, m_post_norm, v_pre_norm, v_w_in, v_sink_a, v_mem_norm, v_w_mem_kv, v_w_out, v_post_norm):
    gathers = (_gather_exchange([w_in[0].astype(BF16)]),
               _gather_exchange([w_out[0].astype(BF16), w_mem_kv[0].astype(BF16)]))
    loc = _local_step(x[0], mem[0], loss_target[0], pre_norm, sink_a, mem_norm, post_norm, None, None, None, gathers)
    big = [loc["gw_in"], loc["gw_out"].reshape(N_CHIPS, D_MODEL // N_CHIPS, D_MODEL),
           loc["gw_mkv"].reshape(N_CHIPS, D_MODEL // N_CHIPS, 2 * C_W)]
    core = lax.axis_index("c").astype(jnp.int32).reshape(1)
    w_in_full = loc["w_in_g"]
    step_in = (x[0], loc["g"], pre_norm, w_in_full, loc["dproj"])
    pair_ex = _pair_exchange(big)
    pair_handles, token = _exchange_start(pair_ex, "pair_exchange_start")
    gx_a, gpre_a = _input_grad(*step_in, None, (0, 1), token, "input_grad_a")
    big, got = _exchange_wait(pair_ex, pair_handles, gpre_a, "pair_exchange_wait")
    parts = [_pair_add(core, own, g) for own, g in zip(big, got)]
    chip_ex = _chip_exchange(parts)
    chip_handles, token = _exchange_start(chip_ex, "chip_exchange_start")
    grad_x, gpre_b = _input_grad(*step_in, gx_a, (1, 15), token, "input_grad_b")
    _, slots = _exchange_wait(chip_ex, chip_handles, gpre_b, "chip_exchange_wait")
    halves = [_slot_sum(s, name=f"chip_sum_{s.shape[2]}", core=core) for s in slots]
    widen = lambda a: jnp.pad(a.reshape(1, -1), ((0, 0), (0, D_MODEL - a.size)))
    small = jnp.concatenate([gpre_a, loc["gpost"], loc["gmem"], widen(loc["gsink"]), widen(loc["loss"]), gpre_b,
                             jnp.zeros((2, D_MODEL), F32)], axis=0)
    (g_in, g_out, g_mkv), small_slots = _pair_gather(halves, small)
    small_sum = _slot_sum(small_slots, name="device_sum")
    g_pre, g_post, g_mem = small_sum[0:1] + small_sum[5:6], small_sum[1:2], small_sum[2:3]
    g_sink = small_sum[3:4, :sink_a.shape[1]]
    loss = small_sum[4, 0]

    d_in, nm_in, nv_in = _adamw(w_in[0], g_in, m_w_in[0], v_w_in[0], "adamw_in")
    d_out, nm_out, nv_out = _adamw(w_out[0], g_out, m_w_out[0], v_w_out[0], "adamw_out")
    d_mkv, nm_mkv, nv_mkv = _adamw(w_mem_kv[0], g_mkv, m_w_mem_kv[0], v_w_mem_kv[0], "adamw_mkv")
    pad6 = lambda a: jnp.pad(a, ((0, 0), (0, D_MODEL - a.shape[1])))
    stack = lambda a, b, c_, d_: jnp.concatenate([a, b, c_, pad6(d_), jnp.zeros((4, D_MODEL), F32)], axis=0)
    d_s, nm_s, nv_s = _adamw(stack(pre_norm, post_norm, mem_norm, sink_a),
                             jnp.concatenate([g_pre, small_sum[1:]], axis=0),
                             stack(m_pre_norm, m_post_norm, m_mem_norm, m_sink_a),
                             stack(v_pre_norm, v_post_norm, v_mem_norm, v_sink_a), "adamw_small")
    ns_ = sink_a.shape[1]
    unpack = lambda a: (a[0:1], a[3:4, :ns_], a[2:3], a[1:2])
    d_pre, d_sink, d_mem, d_post = unpack(d_s)
    nm_pre, nm_sink, nm_mem, nm_post = unpack(nm_s)
    nv_pre, nv_sink, nv_mem, nv_post = unpack(nv_s)
    lead = lambda a: a[None]
    return (loss, lead(grad_x),
            g_pre, lead(g_in), g_sink, g_mem, lead(g_mkv), lead(g_out), g_post,
            d_pre, lead(d_in), d_sink, d_mem, lead(d_mkv), lead(d_out), d_post,
            nm_pre, lead(nm_in), nm_sink, nm_mem, lead(nm_mkv), lead(nm_out), nm_post,
            nv_pre, lead(nv_in), nv_sink, nv_mem, lead(nv_mkv), lead(nv_out), nv_post)
```

```python
import numpy as np
import jax
import jax.numpy as jnp
from jax import lax
from jax.experimental import pallas as pl
from jax.experimental.pallas import tpu as pltpu

F32 = jnp.float32
BF16 = jnp.bfloat16

D_MODEL = 1024
HEAD_DIM = 64
LANES = 128
BLOCK = 128
ROW_TILE = 512
ATTN_TILE = 1024
INPUT_GRAD_TILES = 16
A_W, A_KV_W, B_W, C_W = 384, 128, 384, 256
N_MEM = 256
D_IN = 3072
N_CHIPS = 4
SHARD_IN = D_IN // N_CHIPS
B_CONFIGS = ((128, 1), (512, 4), (2048, 16))
B_DILS = tuple(d for _, d in B_CONFIGS)
A_WINDOW = 128
RMS_EPS = 1e-6
ROPE_THETA = 500000.0
SCALE = HEAD_DIM ** -0.5
NEG = -1e30
ADAM_LR, ADAM_B1, ADAM_B2, ADAM_EPS, ADAM_WD, ADAM_STEP = 0.001, 0.9, 0.999, 1e-08, 0.01, 10

NT = (((1,), (1,)), ((), ()))
TN = (((0,), (0,)), ((), ()))
MESH = pl.DeviceIdType.MESH

_PROJ_LAYOUT = (
    [("qa", 128 * i, True, True) for i in range(3)] + [("ka", 0, True, False), ("va", 0, False, False)]
    + [("ga", 128 * i, False, False) for i in range(3)]
    + [("qb", 128 * i, True, True) for i in range(3)] + [("kb", 128 * i, True, False) for i in range(3)]
    + [("vb", 128 * i, False, False) for i in range(3)] + [("gb", 128 * i, False, False) for i in range(3)]
    + [("qc", 128 * i, False, True) for i in range(2)] + [("gc", 128 * i, False, False) for i in range(2)]
)
_PROJ_WIDTH = dict(qa=A_W, ka=A_KV_W, va=A_KV_W, ga=A_W, qb=B_W, kb=B_W, vb=B_W, gb=B_W, qc=C_W, gc=C_W)
_NATURAL = ("qa", "ka", "va", "ga", "gb", "qc", "gc")
_DILATED = ("qb", "kb", "vb")


def _dot(a, b):
    return jnp.dot(a, b, preferred_element_type=F32)


def _dot_nt(a, b):
    return lax.dot_general(a, b, NT, preferred_element_type=F32)


def _dot_tn(a, b):
    return lax.dot_general(a, b, TN, preferred_element_type=F32)


def _half_masks(rows):
    lane = lax.broadcasted_iota(jnp.int32, (rows, LANES), 1)
    return lane < HEAD_DIM, lane >= HEAD_DIM


def _rope(t, c, sm, sp):
    return t * c + pltpu.roll(t, LANES - 8, 1) * sm + pltpu.roll(t, 8, 1) * sp


def _rope_tables(seq, tm):
    dim = jnp.arange(LANES) % HEAD_DIM
    inv_freq = ROPE_THETA ** (-jnp.arange(0, 16, 2, dtype=F32) / 16)
    freq = jnp.where(dim < 16, inv_freq[dim % 8], 0.0)[None, :]
    local = jnp.arange(tm, dtype=F32)[:, None] * freq
    base = (jnp.arange(seq // tm, dtype=F32) * tm)[:, None] * freq
    both = lambda a: jnp.concatenate([jnp.cos(a), jnp.sin(a)], axis=1)
    return both(local), jnp.repeat(both(base), 8, axis=0)


def _rope_coeffs(local_ref, base_ref):
    cl, sl = local_ref[:, :LANES], local_ref[:, LANES:]
    cb, sb = base_ref[0:1, :LANES], base_ref[0:1, LANES:]
    cos = cb * cl - sb * sl
    sin = sb * cl + cb * sl
    dim = lax.broadcasted_iota(jnp.int32, (1, LANES), 1) % HEAD_DIM
    return cos, jnp.where(dim < 8, -sin, 0.0), jnp.where((dim >= 8) & (dim < 16), sin, 0.0)


def _split3(x):
    a = x.astype(BF16)
    r = x - a.astype(F32)
    b = r.astype(BF16)
    c = (r - b.astype(F32)).astype(BF16)
    return a, b, c


def _rows_to_lanes(x):
    row = lax.broadcasted_iota(jnp.int32, (8, LANES), 0)
    lane = lax.broadcasted_iota(jnp.int32, (8, LANES), 1)
    eye = (row == lane).astype(BF16)
    a, b, c = _split3(x)
    return _dot_nt(eye, a) + _dot_nt(eye, b) + _dot_nt(eye, c)


def _head_sum_matrix(width):
    k = lax.broadcasted_iota(jnp.int32, (width, LANES), 0)
    h = lax.broadcasted_iota(jnp.int32, (width, LANES), 1)
    return (k // HEAD_DIM == h).astype(BF16)


def _head_expand_matrix(width):
    h = lax.broadcasted_iota(jnp.int32, (LANES, width), 0)
    k = lax.broadcasted_iota(jnp.int32, (LANES, width), 1)
    return (k // HEAD_DIM == h).astype(BF16)


def _dot_split(x, mat, terms):
    parts = _split3(x)[:terms]
    out = _dot(parts[0], mat)
    for p in parts[1:]:
        out = out + _dot(p, mat)
    return out


def _per_head(cols, fill=0.0):
    rows = cols[0].shape[0]
    lane = lax.broadcasted_iota(jnp.int32, (rows, LANES), 1)
    out = jnp.full((rows, LANES), fill, F32)
    for h, col in enumerate(cols):
        out = jnp.where(lane == h, col, out)
    return out


def _lane_blocks(width):
    return [slice(p * LANES, (p + 1) * LANES) for p in range(width // LANES)]


def _stage(rows, width):
    return pltpu.VMEM((width // LANES, rows, LANES), F32)


def _stage_write(buf, value):
    for p, lanes in enumerate(_lane_blocks(value.shape[1])):
        buf[p] = value[:, lanes]


def _stage_read(buf):
    return jnp.concatenate([buf[p] for p in range(buf.shape[0])], axis=1) if buf.shape[0] > 1 else buf[0]


def _to_residues(buf, out_ref, dil):
    rows = buf.shape[1] // dil
    for r in range(dil):
        for p in range(buf.shape[0]):
            plane = buf.at[p]
            out_ref[r, :, p * LANES:(p + 1) * LANES] = plane[pl.ds(r, rows, stride=dil), :].astype(out_ref.dtype)


def _from_residues(in_ref, buf, dil):
    rows = buf.shape[1] // dil
    for r in range(dil):
        for p in range(buf.shape[0]):
            plane = buf.at[p]
            plane[pl.ds(r, rows, stride=dil), :] = in_ref[r, :, p * LANES:(p + 1) * LANES].astype(F32)


def _residue_spec(dil, tm, width):
    return pl.BlockSpec((dil, tm // dil, width), lambda i: (0, i, 0))


def _gather_exchange(shards_2d):
    shards = tuple(s.reshape(2, s.shape[0] // 2, s.shape[1]) for s in shards_2d)
    n = len(shards)

    def copies(in_refs, out_refs, send_sems, recv_sems):
        srcs, outs = in_refs[:n], out_refs
        x, y, c = lax.axis_index("x"), lax.axis_index("y"), lax.axis_index("c")
        my_chip = 2 * x + y
        sibling = (x, y, 1 - c)
        chips = [(1 - x, y), (x, 1 - y), (1 - x, 1 - y)]

        def copy(k, src, dst, to):
            return pltpu.make_async_remote_copy(src_ref=src, dst_ref=dst, send_sem=send_sems.at[k],
                                                recv_sem=recv_sems.at[k], device_id=to, device_id_type=MESH)

        first, arrive, passed, sibling_arrive = [], [], [], []
        for j, (cx, cy) in enumerate(chips):
            chip = 2 * cx + cy
            for t in range(n):
                k = n * j + t
                first.append(copy(k, srcs[t].at[c], outs[t].at[my_chip, c], (cx, cy, c)))
                arrive.append(copy(k, srcs[t].at[c], outs[t].at[chip, c], (cx, cy, c)))
                passed.append(copy(n * 3 + k, outs[t].at[chip, c], outs[t].at[chip, c], sibling))
                sibling_arrive.append(copy(n * 3 + k, outs[t].at[chip, 1 - c], outs[t].at[chip, 1 - c], sibling))
        return first, arrive, passed, sibling_arrive

    def start(*refs):
        for cp in copies(*refs)[0]:
            cp.start()

    def finish(*refs):
        first, arrive, passed, sibling_arrive = copies(*refs)
        for got, fwd in zip(arrive, passed):
            got.wait_recv()
            fwd.start()
        for cp in sibling_arrive:
            cp.wait_recv()
        for cp in first + passed:
            cp.wait_send()

    my_chip = 2 * lax.axis_index("x") + lax.axis_index("y")
    landing = [lax.dynamic_update_slice(jnp.zeros((N_CHIPS,) + s.shape, s.dtype), s[None], (my_chip, 0, 0, 0))
               for s in shards]
    return dict(ins=list(shards) + landing, start=start, finish=finish, aliases={n + t: t for t in range(n)},
                outs=[jax.ShapeDtypeStruct((N_CHIPS,) + s.shape, s.dtype) for s in shards],
                sems=[pltpu.SemaphoreType.DMA((6 * n,)), pltpu.SemaphoreType.DMA((6 * n,))])


def _mem_kv(mem, mem_norm, w_mkv):
    def body(mem_ref, g_ref, w_ref, mk_ref, mv_ref):
        m = mem_ref[...]
        r = lax.rsqrt(jnp.mean(m * m, axis=-1, keepdims=True) + RMS_EPS)
        mn = (m * r * g_ref[...]).astype(BF16)
        kv = _dot(mn, w_ref[...])
        mk_ref[...] = kv[:, :C_W].astype(BF16)
        mv_ref[...] = kv[:, C_W:].astype(BF16)

    return pl.pallas_call(
        body, name="mem_kv",
        out_shape=[jax.ShapeDtypeStruct((N_MEM, C_W), BF16)] * 2,
    )(mem, mem_norm, w_mkv)


def _mem_kv_bwd(mem, mem_norm, w_mkv, dmk, dmv):
    def body(mem_ref, g_ref, w_ref, dmk_ref, dmv_ref, gw_ref, gn_ref):
        m = mem_ref[...]
        r = lax.rsqrt(jnp.mean(m * m, axis=-1, keepdims=True) + RMS_EPS)
        mhat = m * r
        mn = (mhat * g_ref[...]).astype(BF16)
        dkv = jnp.concatenate([dmk_ref[...], dmv_ref[...]], axis=1).astype(BF16)
        gw_ref[...] = _dot_tn(mn, dkv)
        dmn = _dot_nt(dkv, w_ref[...])
        gn_ref[...] = jnp.sum(dmn * mhat, axis=0, keepdims=True)

    return pl.pallas_call(
        body, name="mem_kv_bwd",
        out_shape=[jax.ShapeDtypeStruct((D_MODEL, 2 * C_W), F32), jax.ShapeDtypeStruct((1, D_MODEL), F32)],
    )(mem, mem_norm, w_mkv, dmk, dmv)


def _pre_norm(x, pre_norm, w_own=None, host=None):
    seq = x.shape[0]
    tm = min(ROW_TILE, seq)
    n_own_in = 2 if w_own is None else 3
    n_own_out = n_own_in
    n_host_in = len(host["ins"]) if host else 0
    n_host_out = len(host["outs"]) if host else 0

    def body(x_ref, g_ref, *refs):
        w_ref = None if w_own is None else refs[0]
        refs = refs[n_own_in - 2:]
        host_in, own_out, refs = refs[:n_host_in], refs[n_host_in:n_host_in + n_own_out], refs[n_host_in + n_own_out:]
        host_out, sems = refs[:n_host_out], refs[n_host_out:]
        if host:
            @pl.when(pl.program_id(0) == 0)
            def _():
                host["start"](host_in, host_out, *sems)

        xv = x_ref[...]
        r = lax.rsqrt(jnp.mean(xv * xv, axis=-1, keepdims=True) + RMS_EPS)
        u = xv * r * g_ref[...]
        ub = u.astype(BF16)
        own_out[0][...] = ub
        own_out[1][...] = u.T.astype(BF16)
        if w_own is not None:
            own_out[2][...] = _dot(ub, w_ref[...])
        if host:
            @pl.when(pl.program_id(0) == seq // tm - 1)
            def _():
                host["finish"](host_in, host_out, *sems)

    any_spec = pl.BlockSpec(memory_space=pl.ANY)
    ins = [x, pre_norm]
    in_specs = [pl.BlockSpec((tm, D_MODEL), lambda i: (i, 0)), pl.BlockSpec(pre_norm.shape, lambda i: (0, 0))]
    out_shape = [jax.ShapeDtypeStruct((seq, D_MODEL), BF16), jax.ShapeDtypeStruct((D_MODEL, seq), BF16)]
    out_specs = [pl.BlockSpec((tm, D_MODEL), lambda i: (i, 0)), pl.BlockSpec((D_MODEL, tm), lambda i: (0, i))]
    if w_own is not None:
        ins.append(w_own)
        in_specs.append(pl.BlockSpec(w_own.shape, lambda i: (0, 0)))
        out_shape.append(jax.ShapeDtypeStruct((seq, w_own.shape[1]), F32))
        out_specs.append(pl.BlockSpec((tm, w_own.shape[1]), lambda i: (i, 0)))
    aliases, scratch = {}, []
    if host:
        aliases = {len(ins) + k: n_own_out + v for k, v in host.get("aliases", {}).items()}
        ins += list(host["ins"])
        in_specs += [any_spec] * n_host_in
        out_shape += list(host["outs"])
        out_specs += [any_spec] * n_host_out
        scratch = list(host["sems"])
    res = pl.pallas_call(
        body, name="pre_norm", grid=(seq // tm,), in_specs=in_specs, out_specs=out_specs, out_shape=out_shape,
        input_output_aliases=aliases, scratch_shapes=scratch,
        compiler_params=pltpu.CompilerParams(dimension_semantics=("arbitrary",)),
    )(*ins)
    return res[0], res[1], (None if w_own is None else res[2]), res[n_own_out:]


def _pre_proj(u, w_in_g, own=None, host=None):
    seq = u.shape[0]
    tm = min(ROW_TILE, seq)
    n_nat, n_dil = len(_NATURAL), len(_DILATED) * len(B_DILS)
    rope = _rope_tables(seq, tm)

    n_host_in = len(host["ins"]) if host else 0
    n_host_out = len(host["outs"]) if host else 0
    n_own_out = n_nat + n_dil

    def body(u_ref, w_ref, rl_ref, rb_ref, *refs):
        if own:
            (chip_ref, pown_ref), refs = refs[:2], refs[2:]
        host_in, refs = refs[:n_host_in], refs[n_host_in:]
        nat = dict(zip(_NATURAL, refs[:n_nat]))
        res = {n: refs[n_nat + len(B_DILS) * k:n_nat + len(B_DILS) * (k + 1)] for k, n in enumerate(_DILATED)}
        host_out = refs[n_own_out:n_own_out + n_host_out]
        bufs = dict(zip(_DILATED, refs[n_own_out + n_host_out:]))
        sems = refs[n_own_out + n_host_out + len(_DILATED):]
        if host:
            @pl.when(pl.program_id(0) == 0)
            def _():
                host["start"](host_in, host_out, *sems)

        def project(own_chip):
            ub = u_ref[...]
            c, sm, sp = _rope_coeffs(rl_ref, rb_ref)
            for j in range(N_CHIPS):
                pj = pown_ref[...] if j == own_chip else _dot(ub, w_ref[j])
                for b in range(SHARD_IN // LANES):
                    name, off, roped, scaled = _PROJ_LAYOUT[(SHARD_IN // LANES) * j + b]
                    piece = pj[:, LANES * b:LANES * (b + 1)]
                    if roped:
                        piece = _rope(piece, c, sm, sp)
                    if scaled:
                        piece = piece * SCALE
                    if name in bufs:
                        bufs[name][off // LANES] = piece
                    else:
                        nat[name][:, off:off + LANES] = piece.astype(BF16)
            for name in _DILATED:
                for ref, dil in zip(res[name], B_DILS):
                    _to_residues(bufs[name], ref, dil)

        if own:
            for chip in range(N_CHIPS):
                pl.when(chip_ref[0] == chip)(lambda chip=chip: project(chip))
        else:
            project(None)
        if host:
            @pl.when(pl.program_id(0) == seq // tm - 1)
            def _():
                host["finish"](host_in, host_out, *sems)

    row = lambda w: pl.BlockSpec((tm, w), lambda i: (i, 0))
    full = lambda a: pl.BlockSpec(a.shape, lambda i: (0,) * a.ndim)
    any_spec = pl.BlockSpec(memory_space=pl.ANY)
    out_shape = [jax.ShapeDtypeStruct((seq, _PROJ_WIDTH[n]), BF16) for n in _NATURAL]
    out_specs = [row(_PROJ_WIDTH[n]) for n in _NATURAL]
    for n in _DILATED:
        for dil in B_DILS:
            out_shape.append(jax.ShapeDtypeStruct((dil, seq // dil, B_W), BF16))
            out_specs.append(_residue_spec(dil, tm, B_W))
    ins = [u, w_in_g, *rope]
    in_specs = [row(D_MODEL), full(w_in_g), full(rope[0]), pl.BlockSpec((8, 2 * LANES), lambda i: (i, 0))]
    if own:
        ins += list(own)
        in_specs += [pl.BlockSpec(memory_space=pltpu.SMEM), row(SHARD_IN)]
    scratch = [_stage(tm, B_W)] * len(_DILATED)
    aliases = {}
    if host:
        aliases = {len(ins) + k: n_own_out + v for k, v in host.get("aliases", {}).items()}
        ins += list(host["ins"])
        in_specs += [any_spec] * n_host_in
        out_shape += list(host["outs"])
        out_specs += [any_spec] * n_host_out
        scratch += list(host["sems"])
    res = pl.pallas_call(
        body, name="pre_proj", grid=(seq // tm,), in_specs=in_specs, out_specs=out_specs, out_shape=out_shape,
        input_output_aliases=aliases, scratch_shapes=scratch,
        compiler_params=pltpu.CompilerParams(dimension_semantics=("arbitrary",)),
    )(*ins)
    out = dict(zip(_NATURAL, res[:n_nat]))
    for k, n in enumerate(_DILATED):
        out[n] = res[n_nat + len(B_DILS) * k:n_nat + len(B_DILS) * (k + 1)]
    out["hosted"] = res[n_own_out:]
    return out


def _band_bias(max_dist, transposed):
    i = np.arange(BLOCK)[:, None]
    j = np.arange(BLOCK)[None, :]
    if transposed:
        same = i <= j
        other = (j + BLOCK - i) <= max_dist
        vis = np.concatenate([same, other], axis=1)
    else:
        prev = (i + BLOCK - j) <= max_dist
        same = j <= i
        vis = np.concatenate([prev, same], axis=1)
    return jnp.asarray(np.where(vis, 0.0, NEG).astype(np.float32))


def _kv_place(h, gqa):
    return (0, h // 3) if gqa else (h // 2, h % 2)


def _band_fwd(q, k, v, sink, *, max_dist, name):
    dil, length, wq = q.shape
    wk = k.shape[2]
    gqa = wk != wq
    tq = min(ATTN_TILE, length)
    ns, nt = tq // BLOCK, length // tq
    npair = wq // LANES
    bias = _band_bias(max_dist, transposed=False)
    has_sink = sink is not None

    def body(*refs):
        if has_sink:
            sink_ref, refs = refs[0], refs[1:]
        q_ref, k_ref, kp_ref, v_ref, vp_ref, bias_ref, o_ref, lse_ref, kbuf, vbuf = refs[:10]
        i = pl.program_id(1)
        kbuf[0:BLOCK] = kp_ref[...]
        kbuf[BLOCK:] = k_ref[...]
        vbuf[0:BLOCK] = vp_ref[...]
        vbuf[BLOCK:] = v_ref[...]
        if gqa:
            kroll, vroll = refs[10:12]
            kroll[...] = pltpu.roll(kbuf[...], HEAD_DIM, 1)
            vroll[...] = pltpu.roll(vbuf[...], HEAD_DIM, 1)
        half = _half_masks(BLOCK)
        col_prev = (lax.broadcasted_iota(jnp.int32, (1, 2 * BLOCK), 1) < BLOCK).astype(F32)

        def score_matmuls(a):
            scores = []
            for p in range(npair):
                qp = q_ref[a * BLOCK:(a + 1) * BLOCK, p * LANES:(p + 1) * LANES]
                for e in range(2):
                    pk, ek = _kv_place(2 * p + e, gqa)
                    kw = (kbuf if ek == e else kroll)[a * BLOCK:(a + 2) * BLOCK, pk * LANES:(pk + 1) * LANES]
                    scores.append(_dot_nt(jnp.where(half[e], qp, jnp.zeros_like(qp)), kw))
            return scores

        pending = score_matmuls(0)
        for a in range(ns):
            r0 = a * BLOCK
            b = bias_ref[...]
            if a == 0:
                b = b + jnp.where(i == 0, NEG, 0.0) * col_prev
            scores = pending
            m_cols, l_cols, probs = [], [], []
            for h, s in enumerate(scores):
                s = s + b
                m = jnp.max(s, axis=1, keepdims=True)
                if has_sink:
                    m = jnp.maximum(m, sink_ref[h])
                pe = jnp.exp(s - m)
                l = jnp.sum(pe, axis=1, keepdims=True)
                if has_sink:
                    l = l + jnp.exp(sink_ref[h] - m)
                probs.append(pe.astype(BF16))
                m_cols.append(m)
                l_cols.append(l)
            pending = score_matmuls(a + 1) if a + 1 < ns else None
            for p in range(npair):
                o_h = []
                for e in range(2):
                    h = 2 * p + e
                    pk, ek = _kv_place(h, gqa)
                    vw = (vbuf if ek == e else vroll)[r0:r0 + 2 * BLOCK, pk * LANES:(pk + 1) * LANES]
                    o_h.append(_dot(probs[h], vw) * (1.0 / l_cols[h]))
                o_ref[r0:r0 + BLOCK, p * LANES:(p + 1) * LANES] = jnp.where(half[0], o_h[0], o_h[1]).astype(BF16)
            lse_ref[r0:r0 + BLOCK, :] = _per_head(m_cols) + jnp.log(_per_head(l_cols, 1.0))

    main = lambda w: pl.BlockSpec((None, tq, w), lambda r, i: (r, i, 0))
    prev = lambda w: pl.BlockSpec((None, BLOCK, w), lambda r, i: (r, jnp.maximum(i * ns - 1, 0), 0))
    in_specs = [main(wq), main(wk), prev(wk), main(wk), prev(wk), pl.BlockSpec(bias.shape, lambda r, i: (0, 0))]
    args = [q, k, k, v, v, bias]
    if has_sink:
        in_specs = [pl.BlockSpec(memory_space=pltpu.SMEM)] + in_specs
        args = [sink] + args
    scratch = [pltpu.VMEM((tq + BLOCK, wk), BF16)] * (4 if gqa else 2)
    return pl.pallas_call(
        body, name=name, grid=(dil, nt), in_specs=in_specs,
        out_specs=[main(wq), main(LANES)],
        out_shape=[jax.ShapeDtypeStruct((dil, length, wq), BF16), jax.ShapeDtypeStruct((dil, length, LANES), F32)],
        scratch_shapes=scratch,
    )(*args)


def _band_bwd(q, k, v, do, lse, delta, *, max_dist, name):
    dil, length, wq = q.shape
    wk = k.shape[2]
    gqa = wk != wq
    tq = min(ATTN_TILE, length)
    ns, nt = tq // BLOCK, length // tq
    npair = wq // LANES
    nblocks = length // BLOCK
    bias = _band_bias(max_dist, transposed=True)

    def body(q_ref, qn_ref, do_ref, don_ref, lse_ref, lsen_ref, dl_ref, dln_ref, k_ref, v_ref, bias_ref,
             dq_ref, dk_ref, dv_ref, stat_l, stat_d, dqt, kt, *rolled):
        i = pl.program_id(1)
        for pk in range(wk // LANES):
            kt[pk] = k_ref[:, pk * LANES:(pk + 1) * LANES].astype(F32).T.astype(BF16)
        if gqa:
            kroll, vroll, ktroll = rolled
            kroll[...] = pltpu.roll(k_ref[...], HEAD_DIM, 1)
            vroll[...] = pltpu.roll(v_ref[...], HEAD_DIM, 1)
            ktroll[0] = kroll[...].astype(F32).T.astype(BF16)
        for a in range(ns):
            rows = slice(a * BLOCK, (a + 1) * BLOCK)
            stat_l[a] = _rows_to_lanes(lse_ref[rows, :])
            stat_d[a] = _rows_to_lanes(dl_ref[rows, :])
        stat_l[ns] = _rows_to_lanes(lsen_ref[...])
        stat_d[ns] = _rows_to_lanes(dln_ref[...])

        @pl.when(i == 0)
        def _():
            dqt[:, :, 0:BLOCK] = jnp.zeros((npair, LANES, BLOCK), F32)

        @pl.when(i > 0)
        def _():
            dqt[:, :, 0:BLOCK] = dqt[:, :, tq:tq + BLOCK]

        dqt[:, :, BLOCK:] = jnp.zeros((npair, LANES, tq), F32)
        half2 = _half_masks(2 * BLOCK)
        row = lax.broadcasted_iota(jnp.int32, (LANES, BLOCK), 0)
        row_half = (row < HEAD_DIM, row >= HEAD_DIM)
        col_next = (lax.broadcasted_iota(jnp.int32, (1, 2 * BLOCK), 1) >= BLOCK).astype(F32)

        def scores(b):
            rows = slice(b * BLOCK, (b + 1) * BLOCK)
            nxt_rows = slice((b + 1) * BLOCK, (b + 2) * BLOCK)
            items = []
            for p in range(npair):
                lanes = slice(p * LANES, (p + 1) * LANES)
                q_next = q_ref[nxt_rows, lanes] if b + 1 < ns else qn_ref[:, lanes]
                do_next = do_ref[nxt_rows, lanes] if b + 1 < ns else don_ref[:, lanes]
                qw = jnp.concatenate([q_ref[rows, lanes], q_next], axis=0)
                dow = jnp.concatenate([do_ref[rows, lanes], do_next], axis=0)
                for e in range(2):
                    h = 2 * p + e
                    pk, ek = _kv_place(h, gqa)
                    klanes = slice(pk * LANES, (pk + 1) * LANES)
                    kb = (k_ref if ek == e else kroll)[rows, klanes]
                    vb = (v_ref if ek == e else vroll)[rows, klanes]
                    qm = jnp.where(half2[e], qw, jnp.zeros_like(qw))
                    dom = jnp.where(half2[e], dow, jnp.zeros_like(dow))
                    items.append(dict(p=p, e=e, h=h, pk=pk, ek=ek, qm=qm, dom=dom,
                                      st=_dot_nt(kb, qm), dpt=_dot_nt(vb, dom)))
            return items

        def probs(b, items):
            bt = bias_ref[...]
            if b == ns - 1:
                bt = bt + jnp.where(i == nt - 1, NEG, 0.0) * col_next
            for it in items:
                h = it["h"]
                lrow = jnp.concatenate([stat_l[b, h:h + 1, :], stat_l[b + 1, h:h + 1, :]], axis=1)
                drow = jnp.concatenate([stat_d[b, h:h + 1, :], stat_d[b + 1, h:h + 1, :]], axis=1)
                pt = jnp.exp(it["st"] + bt - lrow)
                it["ptb"] = pt.astype(BF16)
                it["dsb"] = (pt * (it["dpt"] - drow)).astype(BF16)

        pending = scores(0)
        for b in range(ns):
            rows = slice(b * BLOCK, (b + 1) * BLOCK)
            window = slice(b * BLOCK, (b + 2) * BLOCK)
            acc = {}
            items = pending
            probs(b, items)
            pending = scores(b + 1) if b + 1 < ns else None
            for p in range(npair):
                pair = items[2 * p:2 * p + 2]
                lanes = slice(p * LANES, (p + 1) * LANES)
                kparts = []
                for it in pair:
                    kbt = (kt if it["ek"] == it["e"] else ktroll)[it["pk"], :, rows]
                    kparts.append(jnp.where(row_half[it["e"]], kbt, jnp.zeros_like(kbt)))
                ds_keys = jnp.concatenate([it["dsb"] for it in pair], axis=0)
                dqt[p, :, window] += _dot(jnp.concatenate(kparts, axis=1), ds_keys)
                if not gqa:
                    q_both = jnp.concatenate([it["qm"] for it in pair], axis=0)
                    do_both = jnp.concatenate([it["dom"] for it in pair], axis=0)
                    dk_ref[rows, lanes] = _dot(jnp.concatenate([it["dsb"] for it in pair], axis=1), q_both).astype(BF16)
                    dv_ref[rows, lanes] = _dot(jnp.concatenate([it["ptb"] for it in pair], axis=1), do_both).astype(BF16)
                else:
                    for it in pair:
                        dv_c = _dot(it["ptb"], it["dom"])
                        dk_c = _dot(it["dsb"], it["qm"])
                        key = (it["pk"], it["ek"] == it["e"])
                        if key in acc:
                            acc[key] = (acc[key][0] + dk_c, acc[key][1] + dv_c)
                        else:
                            acc[key] = (dk_c, dv_c)
            if gqa:
                dk_al, dv_al = acc[(0, True)]
                dk_mis, dv_mis = acc[(0, False)]
                dk_ref[rows, :] = (dk_al + pltpu.roll(dk_mis, HEAD_DIM, 1)).astype(BF16)
                dv_ref[rows, :] = (dv_al + pltpu.roll(dv_mis, HEAD_DIM, 1)).astype(BF16)

        for p in range(npair):
            dq_ref[:, p * LANES:(p + 1) * LANES] = dqt[p, :, 0:tq].T.astype(BF16)

    main = lambda w: pl.BlockSpec((None, tq, w), lambda r, i: (r, i, 0))
    nxt = lambda w: pl.BlockSpec((None, BLOCK, w), lambda r, i: (r, jnp.minimum((i + 1) * ns, nblocks - 1), 0))
    scratch = [pltpu.VMEM((ns + 1, 8, LANES), F32), pltpu.VMEM((ns + 1, 8, LANES), F32),
               pltpu.VMEM((npair, LANES, tq + BLOCK), F32), pltpu.VMEM((wk // LANES, LANES, tq), BF16)]
    if gqa:
        scratch = scratch + [pltpu.VMEM((tq, wk), BF16)] * 2 + [pltpu.VMEM((1, LANES, tq), BF16)]
    return pl.pallas_call(
        body, name=name, grid=(dil, nt),
        in_specs=[main(wq), nxt(wq), main(wq), nxt(wq), main(LANES), nxt(LANES), main(LANES), nxt(LANES),
                  main(wk), main(wk), pl.BlockSpec(bias.shape, lambda r, i: (0, 0))],
        out_specs=[main(wq), main(wk), main(wk)],
        out_shape=[jax.ShapeDtypeStruct((dil, length, wq), BF16), jax.ShapeDtypeStruct((dil, length, wk), BF16),
                   jax.ShapeDtypeStruct((dil, length, wk), BF16)],
        scratch_shapes=scratch,
        compiler_params=pltpu.CompilerParams(dimension_semantics=("arbitrary", "arbitrary")),
    )(q, q, do, do, lse, lse, delta, delta, k, v, bias)


def _mem_attn_fwd(q, mk, mv):
    seq = q.shape[0]
    tq = min(ATTN_TILE, seq)
    sub_rows = min(4 * BLOCK, tq)
    ns = tq // sub_rows

    def body(q_ref, mk_ref, mv_ref, o_ref, lse_ref):
        half = _half_masks(sub_rows)

        def sub(a, carry):
            r0 = pl.multiple_of(a * sub_rows, sub_rows)
            scores = []
            for p in range(C_W // LANES):
                lanes = slice(p * LANES, (p + 1) * LANES)
                qp = q_ref[pl.ds(r0, sub_rows), lanes]
                for e in range(2):
                    scores.append(_dot_nt(jnp.where(half[e], qp, jnp.zeros_like(qp)), mk_ref[:, lanes]))
            m_cols, l_cols, probs = [], [], []
            for s in scores:
                m = jnp.max(s, axis=1, keepdims=True)
                pe = jnp.exp(s - m)
                probs.append(pe.astype(BF16))
                m_cols.append(m)
                l_cols.append(jnp.sum(pe, axis=1, keepdims=True))
            for p in range(C_W // LANES):
                lanes = slice(p * LANES, (p + 1) * LANES)
                o_h = [_dot(probs[2 * p + e], mv_ref[:, lanes]) * (1.0 / l_cols[2 * p + e]) for e in range(2)]
                o_ref[pl.ds(r0, sub_rows), lanes] = jnp.where(half[0], o_h[0], o_h[1]).astype(BF16)
            lse_ref[pl.ds(r0, sub_rows), :] = _per_head(m_cols) + jnp.log(_per_head(l_cols, 1.0))
            return carry

        lax.fori_loop(0, ns, sub, 0, unroll=True)

    row = lambda w: pl.BlockSpec((tq, w), lambda i: (i, 0))
    full = pl.BlockSpec((N_MEM, C_W), lambda i: (0, 0))
    return pl.pallas_call(
        body, name="mem_attn_fwd", grid=(seq // tq,), in_specs=[row(C_W), full, full],
        out_specs=[row(C_W), row(LANES)],
        out_shape=[jax.ShapeDtypeStruct((seq, C_W), BF16), jax.ShapeDtypeStruct((seq, LANES), F32)],
    )(q, mk, mv)


def _mem_attn_bwd(q, mk, mv, do, lse, delta):
    seq = q.shape[0]
    tq = min(ATTN_TILE, seq)
    ns = tq // BLOCK
    npair = C_W // LANES

    def body(q_ref, mk_ref, mv_ref, do_ref, lse_ref, dl_ref, dq_ref, dmk_ref, dmv_ref, stat_l, stat_d, mkt, dqt):
        @pl.when(pl.program_id(0) == 0)
        def _():
            dmk_ref[...] = jnp.zeros_like(dmk_ref)
            dmv_ref[...] = jnp.zeros_like(dmv_ref)
            for p in range(npair):
                mkt[p] = mk_ref[:, p * LANES:(p + 1) * LANES].astype(F32).T.astype(BF16)

        for a in range(ns):
            rows = slice(a * BLOCK, (a + 1) * BLOCK)
            stat_l[a] = _rows_to_lanes(lse_ref[rows, :])
            stat_d[a] = _rows_to_lanes(dl_ref[rows, :])
        span = min(2, ns)
        half = _half_masks(span * BLOCK)
        row = lax.broadcasted_iota(jnp.int32, (LANES, N_MEM), 0)
        row_half = (row < HEAD_DIM, row >= HEAD_DIM)

        for a in range(0, ns, span):
            rows = slice(a * BLOCK, (a + span) * BLOCK)
            items = []
            for p in range(npair):
                lanes = slice(p * LANES, (p + 1) * LANES)
                qp = q_ref[rows, lanes]
                dop = do_ref[rows, lanes]
                for e in range(2):
                    qm = jnp.where(half[e], qp, jnp.zeros_like(qp))
                    dom = jnp.where(half[e], dop, jnp.zeros_like(dop))
                    items.append(dict(p=p, e=e, qm=qm, dom=dom, st=_dot_nt(mk_ref[:, lanes], qm),
                                      dpt=_dot_nt(mv_ref[:, lanes], dom)))
            for it in items:
                h = 2 * it["p"] + it["e"]
                lrow = jnp.concatenate([stat_l[a + k, h:h + 1, :] for k in range(span)], axis=1)
                drow = jnp.concatenate([stat_d[a + k, h:h + 1, :] for k in range(span)], axis=1)
                pt = jnp.exp(it["st"] - lrow)
                it["ptb"] = pt.astype(BF16)
                it["dsb"] = (pt * (it["dpt"] - drow)).astype(BF16)
            for p in range(npair):
                lanes = slice(p * LANES, (p + 1) * LANES)
                pair = [it for it in items if it["p"] == p]
                join = lambda name, axis: jnp.concatenate([it[name] for it in pair], axis=axis)
                dmv_ref[:, lanes] += _dot(join("ptb", 1), join("dom", 0))
                dmk_ref[:, lanes] += _dot(join("dsb", 1), join("qm", 0))
                kbt = mkt[p]
                k_both = jnp.concatenate([jnp.where(row_half[e], kbt, jnp.zeros_like(kbt)) for e in range(2)], axis=1)
                dqt[p, :, rows] = _dot(k_both, join("dsb", 0))
        for p in range(npair):
            dq_ref[:, p * LANES:(p + 1) * LANES] = dqt[p].T.astype(BF16)

    row = lambda w: pl.BlockSpec((tq, w), lambda i: (i, 0))
    full = pl.BlockSpec((N_MEM, C_W), lambda i: (0, 0))
    return pl.pallas_call(
        body, name="mem_attn_bwd", grid=(seq // tq,),
        in_specs=[row(C_W), full, full, row(C_W), row(LANES), row(LANES)], out_specs=[row(C_W), full, full],
        out_shape=[jax.ShapeDtypeStruct((seq, C_W), BF16), jax.ShapeDtypeStruct((N_MEM, C_W), F32),
                   jax.ShapeDtypeStruct((N_MEM, C_W), F32)],
        scratch_shapes=[pltpu.VMEM((ns, 8, LANES), F32)] * 2
        + [pltpu.VMEM((npair, LANES, N_MEM), BF16), pltpu.VMEM((npair, LANES, tq), F32)],
        compiler_params=pltpu.CompilerParams(dimension_semantics=("arbitrary",)),
    )(q, mk, mv, do, lse, delta)


def _silu_and_grad(g):
    s = 1.0 / (1.0 + jnp.exp(-g))
    return g * s, s * (1.0 + g * (1.0 - s))


def _post(x, target, post_norm, w_out, sink_row, oa, lse_a, ga, ob_list, lseb_list, gb, oc, gc):
    seq = x.shape[0]
    tm = min(ROW_TILE, seq)
    inv_d = 1.0 / D_MODEL
    nd = len(B_DILS)

    def body(*refs):
        (x_ref, t_ref, gp_ref, w_ref, sink_ref, oa_ref, lsea_ref, ga_ref), refs = refs[:8], refs[8:]
        ob_refs, lb_refs, (gb_ref, oc_ref, gc_ref), refs = refs[:nd], refs[nd:2 * nd], refs[2 * nd:2 * nd + 3], refs[2 * nd + 3:]
        (g_ref, doa_ref, dla_ref, dga_ref), refs = refs[:4], refs[4:]
        dob_refs, lsec_refs, dlb_refs, refs = refs[:nd], refs[nd:2 * nd], refs[2 * nd:3 * nd], refs[3 * nd:]
        (dgb_ref, doc_ref, dlc_ref, dgc_ref, gw_ref, gpost_ref, gsink_ref, loss_ref), refs = refs[:8], refs[8:]
        ycat, obufs, lbufs, st_do, st_l, st_d = refs[0], refs[1:nd], refs[nd:2 * nd - 1], refs[2 * nd - 1], refs[2 * nd], refs[2 * nd + 1]

        @pl.when(pl.program_id(0) == 0)
        def _():
            gw_ref[...] = jnp.zeros_like(gw_ref)
            gpost_ref[...] = jnp.zeros_like(gpost_ref)
            gsink_ref[...] = jnp.zeros_like(gsink_ref)
            loss_ref[...] = jnp.zeros_like(loss_ref)

        o_i, l_i = [ob_refs[0][0].astype(F32)], [lb_refs[0][0]]
        for k in range(1, nd):
            _from_residues(ob_refs[k], obufs[k - 1], B_DILS[k])
            _from_residues(lb_refs[k], lbufs[k - 1], B_DILS[k])
            o_i.append(_stage_read(obufs[k - 1]))
            l_i.append(_stage_read(lbufs[k - 1]))
        mx = l_i[0]
        for l in l_i[1:]:
            mx = jnp.maximum(mx, l)
        w_i = [jnp.exp(l - mx) for l in l_i]
        z = w_i[0]
        for w in w_i[1:]:
            z = z + w
        _stage_write(st_l, mx + jnp.log(z))
        expand = _head_expand_matrix(B_W)
        inv_z = 1.0 / z
        ob = None
        for w, o in zip(w_i, o_i):
            term = _dot_split(w * inv_z, expand, 2) * o
            ob = term if ob is None else ob + term
        oa, oc = oa_ref[...].astype(F32), oc_ref[...].astype(F32)
        sa, dsa = _silu_and_grad(ga_ref[...].astype(F32))
        sb, dsb = _silu_and_grad(gb_ref[...].astype(F32))
        sc, dsc = _silu_and_grad(gc_ref[...].astype(F32))
        ycat[:, 0:A_W] = (oa * sa).astype(BF16)
        ycat[:, A_W:A_W + B_W] = (ob * sb).astype(BF16)
        ycat[:, A_W + B_W:] = (oc * sc).astype(BF16)
        y2 = _dot(ycat[...], w_ref[...])
        r = lax.rsqrt(jnp.mean(y2 * y2, axis=-1, keepdims=True) + RMS_EPS)
        zhat = y2 * r
        gp = gp_ref[...]
        err = x_ref[...] + zhat * gp - t_ref[...]
        loss_ref[...] += jnp.sum(err * err) * (0.5 * inv_d)
        g = err * inv_d
        g_ref[...] = g
        gpost_ref[...] += jnp.sum(g * zhat, axis=0, keepdims=True)
        a = g * gp
        dy2 = (r * (a - zhat * jnp.mean(a * zhat, axis=-1, keepdims=True))).astype(BF16)
        gw_ref[...] += _dot_tn(ycat[...], dy2)
        dycat = _dot_nt(dy2, w_ref[...])
        dya, dyb, dyc = dycat[:, 0:A_W], dycat[:, A_W:A_W + B_W], dycat[:, A_W + B_W:]
        doa, dob, doc = dya * sa, dyb * sb, dyc * sc
        doa_ref[...] = doa.astype(BF16)
        doc_ref[...] = doc.astype(BF16)
        dga_ref[...] = (dya * oa * dsa).astype(BF16)
        dgb_ref[...] = (dyb * ob * dsb).astype(BF16)
        dgc_ref[...] = (dyc * oc * dsc).astype(BF16)
        dl_a = _dot_split(doa * oa, _head_sum_matrix(A_W), 2)
        dla_ref[...] = dl_a
        dlc_ref[...] = _dot_split(doc * oc, _head_sum_matrix(C_W), 2)
        gsink_ref[...] += jnp.sum(jnp.exp(sink_ref[...] - lsea_ref[...]) * dl_a, axis=0, keepdims=True)
        _stage_write(st_do, dob)
        _stage_write(st_d, _dot_split(dob * ob, _head_sum_matrix(B_W), 2))
        for k, dil in enumerate(B_DILS):
            _to_residues(st_do, dob_refs[k], dil)
            _to_residues(st_l, lsec_refs[k], dil)
            _to_residues(st_d, dlb_refs[k], dil)

    row = lambda w: pl.BlockSpec((tm, w), lambda i: (i, 0))
    full = lambda shape: pl.BlockSpec(shape, lambda i: (0,) * len(shape))
    res_specs = lambda w: [_residue_spec(d, tm, w) for d in B_DILS]
    res_shapes = lambda w, dt: [jax.ShapeDtypeStruct((d, seq // d, w), dt) for d in B_DILS]
    ins = [x, target, post_norm, w_out, sink_row, oa, lse_a, ga, *ob_list, *lseb_list, gb, oc, gc]
    in_specs = ([row(D_MODEL), row(D_MODEL), full((1, D_MODEL)), full((D_MODEL, D_MODEL)), full((1, LANES)),
                 row(A_W), row(LANES), row(A_W)] + res_specs(B_W) + res_specs(LANES) + [row(B_W), row(C_W), row(C_W)])
    out_shape = ([jax.ShapeDtypeStruct((seq, D_MODEL), F32), jax.ShapeDtypeStruct((seq, A_W), BF16),
                  jax.ShapeDtypeStruct((seq, LANES), F32), jax.ShapeDtypeStruct((seq, A_W), BF16)]
                 + res_shapes(B_W, BF16) + res_shapes(LANES, F32) + res_shapes(LANES, F32)
                 + [jax.ShapeDtypeStruct((seq, B_W), BF16), jax.ShapeDtypeStruct((seq, C_W), BF16),
                    jax.ShapeDtypeStruct((seq, LANES), F32), jax.ShapeDtypeStruct((seq, C_W), BF16),
                    jax.ShapeDtypeStruct((D_MODEL, D_MODEL), F32), jax.ShapeDtypeStruct((1, D_MODEL), F32),
                    jax.ShapeDtypeStruct((1, LANES), F32), jax.ShapeDtypeStruct((1, LANES), F32)])
    out_specs = ([row(D_MODEL), row(A_W), row(LANES), row(A_W)] + res_specs(B_W) + res_specs(LANES) + res_specs(LANES)
                 + [row(B_W), row(C_W), row(LANES), row(C_W),
                    full((D_MODEL, D_MODEL)), full((1, D_MODEL)), full((1, LANES)), full((1, LANES))])
    scratch = ([pltpu.VMEM((tm, D_MODEL), BF16)] + [_stage(tm, B_W)] * (nd - 1) + [_stage(tm, LANES)] * (nd - 1)
               + [_stage(tm, B_W), _stage(tm, LANES), _stage(tm, LANES)])
    res = pl.pallas_call(
        body, name="post", grid=(seq // tm,), in_specs=in_specs, out_specs=out_specs, out_shape=out_shape,
        scratch_shapes=scratch,
        compiler_params=pltpu.CompilerParams(dimension_semantics=("arbitrary",)),
    )(*ins)
    out = dict(g=res[0], doa=res[1], dl_a=res[2], dga=res[3], dob=res[4:4 + nd], lse_b=res[4 + nd:4 + 2 * nd],
               dl_b=res[4 + 2 * nd:4 + 3 * nd])
    rest = res[4 + 3 * nd:]
    out.update(dgb=rest[0], doc=rest[1], dl_c=rest[2], dgc=rest[3], gw_out=rest[4], gpost=rest[5], gsink=rest[6],
               loss=rest[7])
    return out


def _grad_w_in(ut, nat, res):
    seq = ut.shape[1]
    tm = min(ROW_TILE, seq)
    nd = len(B_DILS)
    nat_list = [nat[n] for n in _NATURAL]
    res_list = [a for n in _DILATED for a in res[n]]
    rope = _rope_tables(seq, tm)

    def body(rl_ref, rb_ref, ut_ref, *refs):
        nat_refs = dict(zip(_NATURAL, refs[:len(_NATURAL)]))
        refs = refs[len(_NATURAL):]
        res_refs = {n: refs[nd * k:nd * (k + 1)] for k, n in enumerate(_DILATED)}
        refs = refs[nd * len(_DILATED):]
        dproj_ref, gw_ref = refs[:2]
        bufs = {n: refs[2 + (nd - 1) * k:2 + (nd - 1) * (k + 1)] for k, n in enumerate(_DILATED)}

        @pl.when(pl.program_id(0) == 0)
        def _():
            gw_ref[...] = jnp.zeros_like(gw_ref)

        for n in _DILATED:
            for k in range(1, nd):
                _from_residues(res_refs[n][k], bufs[n][k - 1], B_DILS[k])
        c, sm, sp = _rope_coeffs(rl_ref, rb_ref)
        sm, sp = -sm, -sp
        for blk, (name, off, roped, scaled) in enumerate(_PROJ_LAYOUT):
            lanes = slice(off, off + LANES)
            if name in nat_refs:
                piece = nat_refs[name][:, lanes].astype(F32)
            else:
                piece = res_refs[name][0][0, :, lanes].astype(F32)
                for buf in bufs[name]:
                    piece = piece + buf[off // LANES]
            if roped:
                piece = _rope(piece, c, sm, sp)
            if scaled:
                piece = piece * SCALE
            dproj_ref[:, blk * LANES:(blk + 1) * LANES] = piece.astype(BF16)
        for j in range(N_CHIPS):
            gw_ref[j] += _dot(ut_ref[...], dproj_ref[:, j * SHARD_IN:(j + 1) * SHARD_IN])

    row = lambda w: pl.BlockSpec((tm, w), lambda i: (i, 0))
    in_specs = ([pl.BlockSpec(rope[0].shape, lambda i: (0, 0)), pl.BlockSpec((8, 2 * LANES), lambda i: (i, 0)),
                 pl.BlockSpec((D_MODEL, tm), lambda i: (0, i))]
                + [row(a.shape[1]) for a in nat_list]
                + [_residue_spec(d, tm, B_W) for _ in _DILATED for d in B_DILS])
    return pl.pallas_call(
        body, name="grad_w_in", grid=(seq // tm,), in_specs=in_specs,
        out_specs=[row(D_IN), pl.BlockSpec((N_CHIPS, D_MODEL, SHARD_IN), lambda i: (0, 0, 0))],
        out_shape=[jax.ShapeDtypeStruct((seq, D_IN), BF16), jax.ShapeDtypeStruct((N_CHIPS, D_MODEL, SHARD_IN), F32)],
        scratch_shapes=[_stage(tm, B_W)] * ((nd - 1) * len(_DILATED)),
        compiler_params=pltpu.CompilerParams(dimension_semantics=("arbitrary",)),
    )(*rope, ut, *nat_list, *res_list)


def _input_grad(x, g, pre_norm, w_in_g, dproj, gx_prev, span, after, name):
    seq = x.shape[0]
    tm = seq // INPUT_GRAD_TILES
    first_block, steps = span

    def body(*refs):
        x_ref, g_ref, gp_ref, w_ref, dp_ref = refs[:5]
        gx_ref, gpre_ref = refs[-2:]

        @pl.when(pl.program_id(0) == 0)
        def _():
            gpre_ref[...] = jnp.zeros_like(gpre_ref)

        du = None
        for j in range(N_CHIPS):
            term = _dot_nt(dp_ref[:, j * SHARD_IN:(j + 1) * SHARD_IN], w_ref[j])
            du = term if du is None else du + term
        xv = x_ref[...]
        r = lax.rsqrt(jnp.mean(xv * xv, axis=-1, keepdims=True) + RMS_EPS)
        xhat = xv * r
        gpre_ref[...] += jnp.sum(du * xhat, axis=0, keepdims=True)
        a = du * gp_ref[...]
        gx_ref[...] = g_ref[...] + r * (a - xhat * jnp.mean(a * xhat, axis=-1, keepdims=True))

    row = lambda w: pl.BlockSpec((tm, w), lambda i: (first_block + i, 0))
    full = lambda a: pl.BlockSpec(a.shape, lambda i: (0,) * a.ndim)
    any_spec = pl.BlockSpec(memory_space=pl.ANY)
    ins = [x, g, pre_norm, w_in_g, dproj]
    in_specs = [row(D_MODEL), row(D_MODEL), full(pre_norm), full(w_in_g), row(D_IN)]
    aliases = {}
    if gx_prev is not None:
        aliases[len(ins)] = 0
        ins.append(gx_prev)
        in_specs.append(any_spec)
    if after is not None:
        ins.append(after)
        in_specs.append(any_spec)
    return pl.pallas_call(
        body, name=name, grid=(steps,), in_specs=in_specs,
        out_specs=[row(D_MODEL), pl.BlockSpec((1, D_MODEL), lambda i: (0, 0))],
        out_shape=[jax.ShapeDtypeStruct((seq, D_MODEL), F32), jax.ShapeDtypeStruct((1, D_MODEL), F32)],
        input_output_aliases=aliases,
        compiler_params=pltpu.CompilerParams(dimension_semantics=("arbitrary",)),
    )(*ins)


def _exchange_start(ex, name):
    n_in, n_out, n_sem = len(ex["ins"]), len(ex["outs"]), len(ex["sems"])

    def body(*refs):
        in_refs, land_refs, sems = refs[:n_in], refs[n_in:n_in + n_out], refs[n_in + n_out:n_in + n_out + n_sem]
        ex["start"](in_refs, land_refs, *sems)
        token = refs[-1]
        token[...] = jnp.zeros_like(token)

    hbm = pl.BlockSpec(memory_space=pltpu.HBM)
    sem = pl.BlockSpec(memory_space=pltpu.SEMAPHORE)
    ins = [pltpu.with_memory_space_constraint(a, pltpu.HBM) for a in ex["ins"]]
    landing = [pltpu.with_memory_space_constraint(lax.empty(o.shape, o.dtype), pltpu.HBM) for o in ex["outs"]]
    res = pl.pallas_call(
        body, name=name,
        out_shape=list(ex["sems"]) + [pltpu.HBM(a.shape, a.dtype) for a in ex["ins"]]
        + [pltpu.HBM(o.shape, o.dtype) for o in ex["outs"]] + [jax.ShapeDtypeStruct((8, LANES), F32)],
        in_specs=[hbm] * (n_in + n_out),
        out_specs=[sem] * n_sem + [hbm] * (n_in + n_out) + [pl.BlockSpec(memory_space=pltpu.VMEM)],
        input_output_aliases={k: n_sem + k for k in range(n_in + n_out)},
        compiler_params=pltpu.CompilerParams(has_side_effects=pltpu.SideEffectType.DATAFLOW_SIDE_EFFECTING),
    )(*ins, *landing)
    return res[:-1], res[-1]


def _exchange_wait(ex, handles, after, name):
    n_in, n_out, n_sem = len(ex["ins"]), len(ex["outs"]), len(ex["sems"])
    sems, thru = handles[:n_sem], handles[n_sem:]

    def body(*refs):
        in_refs, land_refs = refs[:n_in], refs[n_in:n_in + n_out]
        sem_refs = refs[n_in + n_out:n_in + n_out + n_sem]
        ex["finish"](in_refs, land_refs, *sem_refs)

    hbm = pl.BlockSpec(memory_space=pltpu.HBM)
    sem = pl.BlockSpec(memory_space=pltpu.SEMAPHORE)
    res = pl.pallas_call(
        body, name=name,
        out_shape=[pltpu.HBM(a.shape, a.dtype) for a in thru],
        in_specs=[hbm] * (n_in + n_out) + [sem] * n_sem + [pl.BlockSpec(memory_space=pl.ANY)],
        out_specs=[hbm] * (n_in + n_out),
        input_output_aliases={k: k for k in range(n_in + n_out)},
        compiler_params=pltpu.CompilerParams(has_side_effects=pltpu.SideEffectType.DATAFLOW_SIDE_EFFECTING),
    )(*thru, *sems, after)
    return res[:n_in], res[n_in:]


def _start_finish(build):
    def start(*refs):
        for cp in build(*refs):
            cp.start()

    def finish(*refs):
        for cp in build(*refs):
            cp.wait()

    return dict(start=start, finish=finish)


def _pair_exchange(grads):
    n = len(grads)

    def build(srcs, outs, send_sems, recv_sems):
        x, y, c = lax.axis_index("x"), lax.axis_index("y"), lax.axis_index("c")
        copies = []
        for t in range(n):
            rows = grads[t].shape[1] // 2
            copies.append(pltpu.make_async_remote_copy(
                src_ref=srcs[t].at[:, pl.ds((1 - c) * rows, rows)], dst_ref=outs[t],
                send_sem=send_sems.at[t], recv_sem=recv_sems.at[t], device_id=(x, y, 1 - c), device_id_type=MESH))
        return copies

    return dict(ins=list(grads), **_start_finish(build),
                outs=[jax.ShapeDtypeStruct((g.shape[0], g.shape[1] // 2, g.shape[2]), g.dtype) for g in grads],
                sems=[pltpu.SemaphoreType.DMA((n,)), pltpu.SemaphoreType.DMA((n,))])


def _pair_add(core, own, got):
    nchip, rows2, width = own.shape
    rows = rows2 // 2
    tr = min(ROW_TILE, rows)
    nb = rows // tr

    def body(core_ref, own_ref, got_ref, out_ref):
        out_ref[...] = (own_ref[...] + got_ref[...]).astype(BF16)

    grid_spec = pltpu.PrefetchScalarGridSpec(
        num_scalar_prefetch=1, grid=(nchip, nb),
        in_specs=[pl.BlockSpec((None, tr, width), lambda k, i, core_ref: (k, core_ref[0] * nb + i, 0)),
                  pl.BlockSpec((None, tr, width), lambda k, i, core_ref: (k, i, 0))],
        out_specs=pl.BlockSpec((None, tr, width), lambda k, i, core_ref: (k, i, 0)))
    return pl.pallas_call(
        body, name=f"pair_add_{width}", grid_spec=grid_spec,
        out_shape=jax.ShapeDtypeStruct((nchip, rows, width), BF16),
    )(core, own, got)


def _chip_exchange(parts):
    n = len(parts)

    def build(srcs, outs, send_sems, recv_sems, local_sems):
        x, y, c = lax.axis_index("x"), lax.axis_index("y"), lax.axis_index("c")
        my_chip = 2 * x + y
        chips = [(1 - x, y), (x, 1 - y), (1 - x, 1 - y)]
        copies = [pltpu.make_async_copy(srcs[t].at[my_chip], outs[t].at[my_chip], local_sems.at[t]) for t in range(n)]
        for j, (cx, cy) in enumerate(chips):
            for t in range(n):
                k = n * j + t
                copies.append(pltpu.make_async_remote_copy(
                    src_ref=srcs[t].at[2 * cx + cy], dst_ref=outs[t].at[my_chip], send_sem=send_sems.at[k],
                    recv_sem=recv_sems.at[k], device_id=(cx, cy, c), device_id_type=MESH))
        return copies

    return dict(ins=list(parts), **_start_finish(build), outs=[jax.ShapeDtypeStruct(p.shape, p.dtype) for p in parts],
                sems=[pltpu.SemaphoreType.DMA((3 * n,)), pltpu.SemaphoreType.DMA((3 * n,)),
                      pltpu.SemaphoreType.DMA((n,))])


def _slot_sum(slots, name, core=None):
    ns, rows, width = slots.shape
    tr = min(ROW_TILE, rows)

    def body(*refs):
        in_ref, out_ref = refs[-2:]
        acc = in_ref[0].astype(F32)
        for s in range(1, ns):
            acc = acc + in_ref[s].astype(F32)
        out_ref[...] = acc

    if core is None:
        return pl.pallas_call(
            body, name=name, grid=(rows // tr,),
            in_specs=[pl.BlockSpec((ns, tr, width), lambda i: (0, i, 0))],
            out_specs=pl.BlockSpec((tr, width), lambda i: (i, 0)),
            out_shape=jax.ShapeDtypeStruct((rows, width), F32),
        )(slots)
    grid_spec = pltpu.PrefetchScalarGridSpec(
        num_scalar_prefetch=1, grid=(rows // tr,),
        in_specs=[pl.BlockSpec((ns, tr, width), lambda i, core_ref: (0, i, 0))],
        out_specs=pl.BlockSpec((None, tr, width), lambda i, core_ref: (core_ref[0], i, 0)))
    return pl.pallas_call(
        body, name=name, grid_spec=grid_spec, out_shape=jax.ShapeDtypeStruct((2, rows, width), F32),
    )(core, slots)


def _pair_gather(bufs, small):
    n = len(bufs)

    def body(*refs):
        small_ref, outs, small_out = refs[n], refs[n + 1:2 * n + 1], refs[2 * n + 1]
        send_sems, recv_sems, local_sem = refs[2 * n + 2:]
        x, y, c = lax.axis_index("x"), lax.axis_index("y"), lax.axis_index("c")
        me = 4 * x + 2 * y + c
        chips = [(1 - x, y), (x, 1 - y), (1 - x, 1 - y)]
        mine = pltpu.make_async_copy(small_ref, small_out.at[me], local_sem)
        mine.start()
        copies = [pltpu.make_async_remote_copy(
            src_ref=outs[t].at[c], dst_ref=outs[t].at[c], send_sem=send_sems.at[t], recv_sem=recv_sems.at[t],
            device_id=(x, y, 1 - c), device_id_type=MESH) for t in range(n)]
        peers = [(x, y, 1 - c)] + [(cx, cy, cc) for (cx, cy) in chips for cc in (c, 1 - c)]
        for j, peer in enumerate(peers):
            copies.append(pltpu.make_async_remote_copy(
                src_ref=small_ref, dst_ref=small_out.at[me], send_sem=send_sems.at[n + j],
                recv_sem=recv_sems.at[n + j], device_id=peer, device_id_type=MESH))
        for cp in copies:
            cp.start()
        for cp in copies:
            cp.wait()
        mine.wait()

    any_spec = pl.BlockSpec(memory_space=pl.ANY)
    res = pl.pallas_call(
        body, name="pair_gather",
        out_shape=[jax.ShapeDtypeStruct(b.shape, b.dtype) for b in bufs]
        + [jax.ShapeDtypeStruct((8,) + small.shape, small.dtype)],
        in_specs=[any_spec] * (n + 1), out_specs=[any_spec] * (n + 1),
        input_output_aliases={t: t for t in range(n)},
        scratch_shapes=[pltpu.SemaphoreType.DMA((n + 7,)), pltpu.SemaphoreType.DMA((n + 7,)),
                        pltpu.SemaphoreType.DMA],
    )(*bufs, small)
    return [r.reshape(2 * b.shape[1], b.shape[2]) for r, b in zip(res[:n], bufs)], res[n]


def _adamw(w, g, m, v, name):
    rows, width = w.shape
    tr = min(ROW_TILE // 2, rows)
    c1 = 1.0 / (1.0 - ADAM_B1 ** ADAM_STEP)
    c2 = 1.0 / (1.0 - ADAM_B2 ** ADAM_STEP)

    def body(w_ref, g_ref, m_ref, v_ref, d_ref, nm_ref, nv_ref):
        gv = g_ref[...]
        nm = ADAM_B1 * m_ref[...] + (1.0 - ADAM_B1) * gv
        nv = ADAM_B2 * v_ref[...] + (1.0 - ADAM_B2) * (gv * gv)
        nm_ref[...] = nm
        nv_ref[...] = nv
        d_ref[...] = -ADAM_LR * ((nm * c1) / (jnp.sqrt(nv * c2) + ADAM_EPS) + ADAM_WD * w_ref[...])

    spec = pl.BlockSpec((tr, width), lambda i: (i, 0))
    return pl.pallas_call(
        body, name=name, grid=(rows // tr,), in_specs=[spec] * 4, out_specs=[spec] * 3,
        out_shape=[jax.ShapeDtypeStruct(w.shape, F32)] * 3,
    )(w, g, m, v)


def _local_step(x, mem, target, pre_norm, sink_a, mem_norm, post_norm, w_in_g, w_out, w_mkv, gathers=None,
                own=None):
    first_gather, late_gather = gathers if gathers else (None, None)
    u, ut, p_own, hosted = _pre_norm(x, pre_norm, own[1] if own else None, first_gather)
    if gathers:
        w_in_g = hosted[0].reshape(N_CHIPS, D_MODEL, SHARD_IN)
    pr = _pre_proj(u, w_in_g, (own[0], p_own) if own else None, late_gather)
    pr["ut"] = ut
    if gathers:
        w_out, w_mkv = (g.reshape(D_MODEL, g.shape[-1]) for g in pr["hosted"])
    mk, mv = _mem_kv(mem, mem_norm, w_mkv)
    sink = sink_a.reshape(-1)
    qa, ka, va = pr["qa"][None], pr["ka"][None], pr["va"][None]
    oa, lse_a = _band_fwd(qa, ka, va, sink, max_dist=A_WINDOW - 1, name="swa_fwd")
    ob_list, lseb_list = [], []
    for k, (win, dil) in enumerate(B_CONFIGS):
        o_i, l_i = _band_fwd(pr["qb"][k], pr["kb"][k], pr["vb"][k], None, max_dist=win // dil, name=f"dil{dil}_fwd")
        ob_list.append(o_i)
        lseb_list.append(l_i)
    oc, lse_c = _mem_attn_fwd(pr["qc"], mk, mv)
    sink_row = jnp.pad(sink, (0, LANES - sink.shape[0])).reshape(1, LANES)
    po = _post(x, target, post_norm, w_out, sink_row, oa[0], lse_a[0], pr["ga"], ob_list, lseb_list, pr["gb"], oc,
               pr["gc"])
    dqc, dmk, dmv = _mem_attn_bwd(pr["qc"], mk, mv, po["doc"], lse_c, po["dl_c"])
    dqa, dka, dva = _band_bwd(qa, ka, va, po["doa"][None], lse_a, po["dl_a"][None], max_dist=A_WINDOW - 1,
                              name="swa_bwd")
    res = dict(qb=[], kb=[], vb=[])
    for k, (win, dil) in enumerate(B_CONFIGS):
        dq_i, dk_i, dv_i = _band_bwd(pr["qb"][k], pr["kb"][k], pr["vb"][k], po["dob"][k], po["lse_b"][k],
                                     po["dl_b"][k], max_dist=win // dil, name=f"dil{dil}_bwd")
        res["qb"].append(dq_i)
        res["kb"].append(dk_i)
        res["vb"].append(dv_i)
    nat = dict(qa=dqa[0], ka=dka[0], va=dva[0], ga=po["dga"], gb=po["dgb"], qc=dqc, gc=po["dgc"])
    dproj, gw_in = _grad_w_in(pr["ut"], nat, res)
    gw_mkv, gmem = _mem_kv_bwd(mem, mem_norm, w_mkv, dmk, dmv)
    gsink = -po["gsink"][0, :sink.shape[0]]
    return dict(loss=po["loss"], g=po["g"], dproj=dproj, gw_in=gw_in, gw_out=po["gw_out"], gw_mkv=gw_mkv,
                gpost=po["gpost"], gmem=gmem, gsink=gsink, w_in_g=w_in_g)


def kernel(x, mem, pre_norm, w_in, sink_a, mem_norm, w_mem_kv, w_out, post_norm, loss_target, m_pre_norm, m_w_in, m_sink_a, m_mem_norm, m_w_mem_kv, m_w_out, m_post_norm, v_pre_norm, v_w_in, v_sink_a, v_mem_norm, v_w_mem_kv, v_w_out, v_post_norm):
    w_own = w_in[0].astype(BF16)
    gathers = (_gather_exchange([w_own]), _gather_exchange([w_out[0].astype(BF16), w_mem_kv[0].astype(BF16)]))
    chip = (2 * lax.axis_index("x") + lax.axis_index("y")).astype(jnp.int32).reshape(1)
    loc = _local_step(x[0], mem[0], loss_target[0], pre_norm, sink_a, mem_norm, post_norm, None, None, None, gathers,
                      (chip, w_own))
    big = [loc["gw_in"], loc["gw_out"].reshape(N_CHIPS, D_MODEL // N_CHIPS, D_MODEL),
           loc["gw_mkv"].reshape(N_CHIPS, D_MODEL // N_CHIPS, 2 * C_W)]
    core = lax.axis_index("c").astype(jnp.int32).reshape(1)
    w_in_full = loc["w_in_g"]
    step_in = (x[0], loc["g"], pre_norm, w_in_full, loc["dproj"])
    pair_ex = _pair_exchange(big)
    pair_handles, token = _exchange_start(pair_ex, "pair_exchange_start")
    gx_a, gpre_a = _input_grad(*step_in, None, (0, 2), token, "input_grad_a")
    big, got = _exchange_wait(pair_ex, pair_handles, gpre_a, "pair_exchange_wait")
    parts = [_pair_add(core, own, g) for own, g in zip(big, got)]
    chip_ex = _chip_exchange(parts)
    chip_handles, token = _exchange_start(chip_ex, "chip_exchange_start")
    grad_x, gpre_b = _input_grad(*step_in, gx_a, (2, 14), token, "input_grad_b")
    _, slots = _exchange_wait(chip_ex, chip_handles, gpre_b, "chip_exchange_wait")
    halves = [_slot_sum(s, name=f"chip_sum_{s.shape[2]}", core=core) for s in slots]
    widen = lambda a: jnp.pad(a.reshape(1, -1), ((0, 0), (0, D_MODEL - a.size)))
    small = jnp.concatenate([gpre_a, loc["gpost"], loc["gmem"], widen(loc["gsink"]), widen(loc["loss"]), gpre_b,
                             jnp.zeros((2, D_MODEL), F32)], axis=0)
    (g_in, g_out, g_mkv), small_slots = _pair_gather(halves, small)
    small_sum = _slot_sum(small_slots, name="device_sum")
    g_pre, g_post, g_mem = small_sum[0:1] + small_sum[5:6], small_sum[1:2], small_sum[2:3]
    g_sink = small_sum[3:4, :sink_a.shape[1]]
    loss = small_sum[4, 0]

    d_in, nm_in, nv_in = _adamw(w_in[0], g_in, m_w_in[0], v_w_in[0], "adamw_in")
    d_out, nm_out, nv_out = _adamw(w_out[0], g_out, m_w_out[0], v_w_out[0], "adamw_out")
    d_mkv, nm_mkv, nv_mkv = _adamw(w_mem_kv[0], g_mkv, m_w_mem_kv[0], v_w_mem_kv[0], "adamw_mkv")
    pad6 = lambda a: jnp.pad(a, ((0, 0), (0, D_MODEL - a.shape[1])))
    stack = lambda a, b, c_, d_: jnp.concatenate([a, b, c_, pad6(d_), jnp.zeros((4, D_MODEL), F32)], axis=0)
    d_s, nm_s, nv_s = _adamw(stack(pre_norm, post_norm, mem_norm, sink_a),
                             jnp.concatenate([g_pre, small_sum[1:]], axis=0),
                             stack(m_pre_norm, m_post_norm, m_mem_norm, m_sink_a),
                             stack(v_pre_norm, v_post_norm, v_mem_norm, v_sink_a), "adamw_small")
    ns_ = sink_a.shape[1]
    unpack = lambda a: (a[0:1], a[3:4, :ns_], a[2:3], a[1:2])
    d_pre, d_sink, d_mem, d_post = unpack(d_s)
    nm_pre, nm_sink, nm_mem, nm_post = unpack(nm_s)
    nv_pre, nv_sink, nv_mem, nv_post = unpack(nv_s)
    lead = lambda a: a[None]
    return (loss, lead(grad_x),
            g_pre, lead(g_in), g_sink, g_mem, lead(g_mkv), lead(g_out), g_post,
            d_pre, lead(d_in), d_sink, d_mem, lead(d_mkv), lead(d_out), d_post,
            nm_pre, lead(nm_in), nm_sink, nm_mem, lead(nm_mkv), lead(nm_out), nm_post,
            nv_pre, lead(nv_in), nv_sink, nv_mem, lead(nv_mkv), lead(nv_out), nv_post)
```

```python
import numpy as np
import jax
import jax.numpy as jnp
from jax import lax
from jax.experimental import pallas as pl
from jax.experimental.pallas import tpu as pltpu

F32 = jnp.float32
BF16 = jnp.bfloat16

D_MODEL = 1024
HEAD_DIM = 64
LANES = 128
BLOCK = 128
ROW_TILE = 512
ATTN_TILE = 1024
INPUT_GRAD_TILES = 16
A_W, A_KV_W, B_W, C_W = 384, 128, 384, 256
N_MEM = 256
D_IN = 3072
N_CHIPS = 4
SHARD_IN = D_IN // N_CHIPS
B_CONFIGS = ((128, 1), (512, 4), (2048, 16))
B_DILS = tuple(d for _, d in B_CONFIGS)
A_WINDOW = 128
RMS_EPS = 1e-6
ROPE_THETA = 500000.0
SCALE = HEAD_DIM ** -0.5
NEG = -1e30
ADAM_LR, ADAM_B1, ADAM_B2, ADAM_EPS, ADAM_WD, ADAM_STEP = 0.001, 0.9, 0.999, 1e-08, 0.01, 10

NT = (((1,), (1,)), ((), ()))
TN = (((0,), (0,)), ((), ()))
MESH = pl.DeviceIdType.MESH

_PROJ_LAYOUT = (
    [("qa", 128 * i, True, True) for i in range(3)] + [("ka", 0, True, False), ("va", 0, False, False)]
    + [("ga", 128 * i, False, False) for i in range(3)]
    + [("qb", 128 * i, True, True) for i in range(3)] + [("kb", 128 * i, True, False) for i in range(3)]
    + [("vb", 128 * i, False, False) for i in range(3)] + [("gb", 128 * i, False, False) for i in range(3)]
    + [("qc", 128 * i, False, True) for i in range(2)] + [("gc", 128 * i, False, False) for i in range(2)]
)
_PROJ_WIDTH = dict(qa=A_W, ka=A_KV_W, va=A_KV_W, ga=A_W, qb=B_W, kb=B_W, vb=B_W, gb=B_W, qc=C_W, gc=C_W)
_NATURAL = ("qa", "ka", "va", "ga", "gb", "qc", "gc")
_DILATED = ("qb", "kb", "vb")


def _dot(a, b):
    return jnp.dot(a, b, preferred_element_type=F32)


def _dot_nt(a, b):
    return lax.dot_general(a, b, NT, preferred_element_type=F32)


def _dot_tn(a, b):
    return lax.dot_general(a, b, TN, preferred_element_type=F32)


def _half_masks(rows):
    lane = lax.broadcasted_iota(jnp.int32, (rows, LANES), 1)
    return lane < HEAD_DIM, lane >= HEAD_DIM


def _rope(t, c, sm, sp):
    return t * c + pltpu.roll(t, LANES - 8, 1) * sm + pltpu.roll(t, 8, 1) * sp


def _rope_tables(seq, tm):
    dim = jnp.arange(LANES) % HEAD_DIM
    inv_freq = ROPE_THETA ** (-jnp.arange(0, 16, 2, dtype=F32) / 16)
    freq = jnp.where(dim < 16, inv_freq[dim % 8], 0.0)[None, :]
    local = jnp.arange(tm, dtype=F32)[:, None] * freq
    base = (jnp.arange(seq // tm, dtype=F32) * tm)[:, None] * freq
    both = lambda a: jnp.concatenate([jnp.cos(a), jnp.sin(a)], axis=1)
    return both(local), jnp.repeat(both(base), 8, axis=0)


def _rope_coeffs(local_ref, base_ref):
    cl, sl = local_ref[:, :LANES], local_ref[:, LANES:]
    cb, sb = base_ref[0:1, :LANES], base_ref[0:1, LANES:]
    cos = cb * cl - sb * sl
    sin = sb * cl + cb * sl
    dim = lax.broadcasted_iota(jnp.int32, (1, LANES), 1) % HEAD_DIM
    return cos, jnp.where(dim < 8, -sin, 0.0), jnp.where((dim >= 8) & (dim < 16), sin, 0.0)


def _split3(x):
    a = x.astype(BF16)
    r = x - a.astype(F32)
    b = r.astype(BF16)
    c = (r - b.astype(F32)).astype(BF16)
    return a, b, c


def _rows_to_lanes(x):
    row = lax.broadcasted_iota(jnp.int32, (8, LANES), 0)
    lane = lax.broadcasted_iota(jnp.int32, (8, LANES), 1)
    eye = (row == lane).astype(BF16)
    a, b, c = _split3(x)
    return _dot_nt(eye, a) + _dot_nt(eye, b) + _dot_nt(eye, c)


def _head_sum_matrix(width):
    k = lax.broadcasted_iota(jnp.int32, (width, LANES), 0)
    h = lax.broadcasted_iota(jnp.int32, (width, LANES), 1)
    return (k // HEAD_DIM == h).astype(BF16)


def _head_expand_matrix(width):
    h = lax.broadcasted_iota(jnp.int32, (LANES, width), 0)
    k = lax.broadcasted_iota(jnp.int32, (LANES, width), 1)
    return (k // HEAD_DIM == h).astype(BF16)


def _dot_split(x, mat, terms):
    parts = _split3(x)[:terms]
    out = _dot(parts[0], mat)
    for p in parts[1:]:
        out = out + _dot(p, mat)
    return out


def _per_head(cols, fill=0.0):
    rows = cols[0].shape[0]
    lane = lax.broadcasted_iota(jnp.int32, (rows, LANES), 1)
    out = jnp.full((rows, LANES), fill, F32)
    for h, col in enumerate(cols):
        out = jnp.where(lane == h, col, out)
    return out


def _lane_blocks(width):
    return [slice(p * LANES, (p + 1) * LANES) for p in range(width // LANES)]


def _stage(rows, width):
    return pltpu.VMEM((width // LANES, rows, LANES), F32)


def _stage_write(buf, value):
    for p, lanes in enumerate(_lane_blocks(value.shape[1])):
        buf[p] = value[:, lanes]


def _stage_read(buf):
    return jnp.concatenate([buf[p] for p in range(buf.shape[0])], axis=1) if buf.shape[0] > 1 else buf[0]


def _to_residues(buf, out_ref, dil):
    rows = buf.shape[1] // dil
    for r in range(dil):
        for p in range(buf.shape[0]):
            plane = buf.at[p]
            out_ref[r, :, p * LANES:(p + 1) * LANES] = plane[pl.ds(r, rows, stride=dil), :].astype(out_ref.dtype)


def _from_residues(in_ref, buf, dil):
    rows = buf.shape[1] // dil
    for r in range(dil):
        for p in range(buf.shape[0]):
            plane = buf.at[p]
            plane[pl.ds(r, rows, stride=dil), :] = in_ref[r, :, p * LANES:(p + 1) * LANES].astype(F32)


def _residue_spec(dil, tm, width):
    return pl.BlockSpec((dil, tm // dil, width), lambda i: (0, i, 0))


def _gather_exchange(shards_2d, hosted=True):
    shards = tuple(s.reshape(2, s.shape[0] // 2, s.shape[1]) for s in shards_2d)
    n = len(shards)

    def copies(srcs, outs, send_sems, recv_sems, swap_base):
        x, y, c = lax.axis_index("x"), lax.axis_index("y"), lax.axis_index("c")
        my_chip = 2 * x + y
        sibling = (x, y, 1 - c)
        chips = [(1 - x, y), (x, 1 - y), (1 - x, 1 - y)]

        def copy(k, src, dst, to):
            return pltpu.make_async_remote_copy(src_ref=src, dst_ref=dst, send_sem=send_sems.at[k],
                                                recv_sem=recv_sems.at[k], device_id=to, device_id_type=MESH)

        first, arrive, passed, sibling_arrive = [], [], [], []
        for j, (cx, cy) in enumerate(chips):
            chip = 2 * cx + cy
            for t in range(n):
                k = n * j + t
                if srcs is not None:
                    first.append(copy(k, srcs[t].at[c], outs[t].at[my_chip, c], (cx, cy, c)))
                    arrive.append(copy(k, srcs[t].at[c], outs[t].at[chip, c], (cx, cy, c)))
                if swap_base is not None:
                    passed.append(copy(swap_base + k, outs[t].at[chip, c], outs[t].at[chip, c], sibling))
                    sibling_arrive.append(copy(swap_base + k, outs[t].at[chip, 1 - c], outs[t].at[chip, 1 - c], sibling))
        return first, arrive, passed, sibling_arrive

    swap_base = 3 * n if hosted else None

    def start(in_refs, out_refs, *sems):
        for cp in copies(in_refs[:n], out_refs, *sems, swap_base)[0]:
            cp.start()

    def finish(in_refs, out_refs, *sems):
        first, arrive, passed, sibling_arrive = copies(in_refs[:n], out_refs, *sems, swap_base)
        for k, got in enumerate(arrive):
            got.wait_recv()
            if passed:
                passed[k].start()
        for cp in sibling_arrive:
            cp.wait_recv()
        for cp in first + passed:
            cp.wait_send()

    def swap(out_refs, *sems):
        _, _, passed, sibling_arrive = copies(None, out_refs, *sems, 0)
        for cp in passed:
            cp.start()
        for cp in sibling_arrive:
            cp.wait_recv()
        for cp in passed:
            cp.wait_send()

    n_sem = 6 * n if hosted else 3 * n
    ex = dict(ins=list(shards), start=start, finish=finish, swap=swap,
              outs=[jax.ShapeDtypeStruct((N_CHIPS,) + s.shape, s.dtype) for s in shards],
              sems=[pltpu.SemaphoreType.DMA((n_sem,)), pltpu.SemaphoreType.DMA((n_sem,))])
    if hosted:
        my_chip = 2 * lax.axis_index("x") + lax.axis_index("y")
        ex["ins"] += [lax.dynamic_update_slice(jnp.zeros((N_CHIPS,) + s.shape, s.dtype), s[None], (my_chip, 0, 0, 0))
                      for s in shards]
        ex["aliases"] = {n + t: t for t in range(n)}
    return ex


def _sibling_swap(ex, landed, name):
    n = len(landed)

    def body(*refs):
        ex["swap"](refs[n:2 * n], *refs[2 * n:])

    any_spec = pl.BlockSpec(memory_space=pl.ANY)
    return pl.pallas_call(
        body, name=name, in_specs=[any_spec] * n, out_specs=[any_spec] * n,
        out_shape=[jax.ShapeDtypeStruct(a.shape, a.dtype) for a in landed],
        input_output_aliases={t: t for t in range(n)}, scratch_shapes=list(ex["sems"]),
    )(*landed)


def _mem_kv(mem, mem_norm, w_mkv):
    def body(mem_ref, g_ref, w_ref, mk_ref, mv_ref):
        m = mem_ref[...]
        r = lax.rsqrt(jnp.mean(m * m, axis=-1, keepdims=True) + RMS_EPS)
        mn = (m * r * g_ref[...]).astype(BF16)
        kv = _dot(mn, w_ref[...])
        mk_ref[...] = kv[:, :C_W].astype(BF16)
        mv_ref[...] = kv[:, C_W:].astype(BF16)

    return pl.pallas_call(
        body, name="mem_kv",
        out_shape=[jax.ShapeDtypeStruct((N_MEM, C_W), BF16)] * 2,
    )(mem, mem_norm, w_mkv)


def _mem_kv_bwd(mem, mem_norm, w_mkv, dmk, dmv):
    def body(mem_ref, g_ref, w_ref, dmk_ref, dmv_ref, gw_ref, gn_ref):
        m = mem_ref[...]
        r = lax.rsqrt(jnp.mean(m * m, axis=-1, keepdims=True) + RMS_EPS)
        mhat = m * r
        mn = (mhat * g_ref[...]).astype(BF16)
        dkv = jnp.concatenate([dmk_ref[...], dmv_ref[...]], axis=1).astype(BF16)
        gw_ref[...] = _dot_tn(mn, dkv)
        dmn = _dot_nt(dkv, w_ref[...])
        gn_ref[...] = jnp.sum(dmn * mhat, axis=0, keepdims=True)

    return pl.pallas_call(
        body, name="mem_kv_bwd",
        out_shape=[jax.ShapeDtypeStruct((D_MODEL, 2 * C_W), F32), jax.ShapeDtypeStruct((1, D_MODEL), F32)],
    )(mem, mem_norm, w_mkv, dmk, dmv)


def _pre_norm(x, pre_norm, w_own=None, after=None):
    seq = x.shape[0]
    tm = min(ROW_TILE, seq)

    def body(x_ref, g_ref, *refs):
        xv = x_ref[...]
        r = lax.rsqrt(jnp.mean(xv * xv, axis=-1, keepdims=True) + RMS_EPS)
        u = xv * r * g_ref[...]
        ub = u.astype(BF16)
        if w_own is None:
            u_ref, ut_ref = refs[-2:]
        else:
            w_ref, (u_ref, ut_ref, p_ref) = refs[0], refs[-3:]
            p_ref[...] = _dot(ub, w_ref[...])
        u_ref[...] = ub
        ut_ref[...] = u.T.astype(BF16)

    ins = [x, pre_norm]
    in_specs = [pl.BlockSpec((tm, D_MODEL), lambda i: (i, 0)), pl.BlockSpec(pre_norm.shape, lambda i: (0, 0))]
    out_shape = [jax.ShapeDtypeStruct((seq, D_MODEL), BF16), jax.ShapeDtypeStruct((D_MODEL, seq), BF16)]
    out_specs = [pl.BlockSpec((tm, D_MODEL), lambda i: (i, 0)), pl.BlockSpec((D_MODEL, tm), lambda i: (0, i))]
    if w_own is not None:
        ins.append(w_own)
        in_specs.append(pl.BlockSpec(w_own.shape, lambda i: (0, 0)))
        out_shape.append(jax.ShapeDtypeStruct((seq, w_own.shape[1]), F32))
        out_specs.append(pl.BlockSpec((tm, w_own.shape[1]), lambda i: (i, 0)))
    if after is not None:
        ins.append(after)
        in_specs.append(pl.BlockSpec(memory_space=pl.ANY))
    res = pl.pallas_call(
        body, name="pre_norm", grid=(seq // tm,), in_specs=in_specs, out_specs=out_specs, out_shape=out_shape,
        compiler_params=pltpu.CompilerParams(dimension_semantics=("arbitrary",)),
    )(*ins)
    return res[0], res[1], (None if w_own is None else res[2])


def _pre_proj(u, w_in_g, own=None, host=None):
    seq = u.shape[0]
    tm = min(ROW_TILE, seq)
    n_nat, n_dil = len(_NATURAL), len(_DILATED) * len(B_DILS)
    rope = _rope_tables(seq, tm)

    n_host_in = len(host["ins"]) if host else 0
    n_host_out = len(host["outs"]) if host else 0
    n_own_out = n_nat + n_dil

    def body(u_ref, w_ref, rl_ref, rb_ref, *refs):
        if own:
            (chip_ref, pown_ref), refs = refs[:2], refs[2:]
        host_in, refs = refs[:n_host_in], refs[n_host_in:]
        nat = dict(zip(_NATURAL, refs[:n_nat]))
        res = {n: refs[n_nat + len(B_DILS) * k:n_nat + len(B_DILS) * (k + 1)] for k, n in enumerate(_DILATED)}
        host_out = refs[n_own_out:n_own_out + n_host_out]
        bufs = dict(zip(_DILATED, refs[n_own_out + n_host_out:]))
        sems = refs[n_own_out + n_host_out + len(_DILATED):]
        if host:
            @pl.when(pl.program_id(0) == 0)
            def _():
                host["start"](host_in, host_out, *sems)

        def project(own_chip):
            ub = u_ref[...]
            c, sm, sp = _rope_coeffs(rl_ref, rb_ref)
            for j in range(N_CHIPS):
                pj = pown_ref[...] if j == own_chip else _dot(ub, w_ref[j])
                for b in range(SHARD_IN // LANES):
                    name, off, roped, scaled = _PROJ_LAYOUT[(SHARD_IN // LANES) * j + b]
                    piece = pj[:, LANES * b:LANES * (b + 1)]
                    if roped:
                        piece = _rope(piece, c, sm, sp)
                    if scaled:
                        piece = piece * SCALE
                    if name in bufs:
                        bufs[name][off // LANES] = piece
                    else:
                        nat[name][:, off:off + LANES] = piece.astype(BF16)
            for name in _DILATED:
                for ref, dil in zip(res[name], B_DILS):
                    _to_residues(bufs[name], ref, dil)

        if own:
            for chip in range(N_CHIPS):
                pl.when(chip_ref[0] == chip)(lambda chip=chip: project(chip))
        else:
            project(None)
        if host:
            @pl.when(pl.program_id(0) == seq // tm - 1)
            def _():
                host["finish"](host_in, host_out, *sems)

    row = lambda w: pl.BlockSpec((tm, w), lambda i: (i, 0))
    full = lambda a: pl.BlockSpec(a.shape, lambda i: (0,) * a.ndim)
    any_spec = pl.BlockSpec(memory_space=pl.ANY)
    out_shape = [jax.ShapeDtypeStruct((seq, _PROJ_WIDTH[n]), BF16) for n in _NATURAL]
    out_specs = [row(_PROJ_WIDTH[n]) for n in _NATURAL]
    for n in _DILATED:
        for dil in B_DILS:
            out_shape.append(jax.ShapeDtypeStruct((dil, seq // dil, B_W), BF16))
            out_specs.append(_residue_spec(dil, tm, B_W))
    ins = [u, w_in_g, *rope]
    in_specs = [row(D_MODEL), full(w_in_g), full(rope[0]), pl.BlockSpec((8, 2 * LANES), lambda i: (i, 0))]
    if own:
        ins += list(own)
        in_specs += [pl.BlockSpec(memory_space=pltpu.SMEM), row(SHARD_IN)]
    scratch = [_stage(tm, B_W)] * len(_DILATED)
    aliases = {}
    if host:
        aliases = {len(ins) + k: n_own_out + v for k, v in host.get("aliases", {}).items()}
        ins += list(host["ins"])
        in_specs += [any_spec] * n_host_in
        out_shape += list(host["outs"])
        out_specs += [any_spec] * n_host_out
        scratch += list(host["sems"])
    res = pl.pallas_call(
        body, name="pre_proj", grid=(seq // tm,), in_specs=in_specs, out_specs=out_specs, out_shape=out_shape,
        input_output_aliases=aliases, scratch_shapes=scratch,
        compiler_params=pltpu.CompilerParams(dimension_semantics=("arbitrary",)),
    )(*ins)
    out = dict(zip(_NATURAL, res[:n_nat]))
    for k, n in enumerate(_DILATED):
        out[n] = res[n_nat + len(B_DILS) * k:n_nat + len(B_DILS) * (k + 1)]
    out["hosted"] = res[n_own_out:]
    return out


def _band_bias(max_dist, transposed):
    i = np.arange(BLOCK)[:, None]
    j = np.arange(BLOCK)[None, :]
    if transposed:
        same = i <= j
        other = (j + BLOCK - i) <= max_dist
        vis = np.concatenate([same, other], axis=1)
    else:
        prev = (i + BLOCK - j) <= max_dist
        same = j <= i
        vis = np.concatenate([prev, same], axis=1)
    return jnp.asarray(np.where(vis, 0.0, NEG).astype(np.float32))


def _kv_place(h, gqa):
    return (0, h // 3) if gqa else (h // 2, h % 2)


def _band_fwd(q, k, v, sink, *, max_dist, name):
    dil, length, wq = q.shape
    wk = k.shape[2]
    gqa = wk != wq
    tq = min(ATTN_TILE, length)
    ns, nt = tq // BLOCK, length // tq
    npair = wq // LANES
    bias = _band_bias(max_dist, transposed=False)
    has_sink = sink is not None

    def body(*refs):
        if has_sink:
            sink_ref, refs = refs[0], refs[1:]
        q_ref, k_ref, kp_ref, v_ref, vp_ref, bias_ref, o_ref, lse_ref, kbuf, vbuf = refs[:10]
        i = pl.program_id(1)
        kbuf[0:BLOCK] = kp_ref[...]
        kbuf[BLOCK:] = k_ref[...]
        vbuf[0:BLOCK] = vp_ref[...]
        vbuf[BLOCK:] = v_ref[...]
        if gqa:
            kroll, vroll = refs[10:12]
            kroll[...] = pltpu.roll(kbuf[...], HEAD_DIM, 1)
            vroll[...] = pltpu.roll(vbuf[...], HEAD_DIM, 1)
        half = _half_masks(BLOCK)
        col_prev = (lax.broadcasted_iota(jnp.int32, (1, 2 * BLOCK), 1) < BLOCK).astype(F32)

        def score_matmuls(a):
            scores = []
            for p in range(npair):
                qp = q_ref[a * BLOCK:(a + 1) * BLOCK, p * LANES:(p + 1) * LANES]
                for e in range(2):
                    pk, ek = _kv_place(2 * p + e, gqa)
                    kw = (kbuf if ek == e else kroll)[a * BLOCK:(a + 2) * BLOCK, pk * LANES:(pk + 1) * LANES]
                    scores.append(_dot_nt(jnp.where(half[e], qp, jnp.zeros_like(qp)), kw))
            return scores

        pending = score_matmuls(0)
        for a in range(ns):
            r0 = a * BLOCK
            b = bias_ref[...]
            if a == 0:
                b = b + jnp.where(i == 0, NEG, 0.0) * col_prev
            scores = pending
            m_cols, l_cols, probs = [], [], []
            for h, s in enumerate(scores):
                s = s + b
                m = jnp.max(s, axis=1, keepdims=True)
                if has_sink:
                    m = jnp.maximum(m, sink_ref[h])
                pe = jnp.exp(s - m)
                l = jnp.sum(pe, axis=1, keepdims=True)
                if has_sink:
                    l = l + jnp.exp(sink_ref[h] - m)
                probs.append(pe.astype(BF16))
                m_cols.append(m)
                l_cols.append(l)
            pending = score_matmuls(a + 1) if a + 1 < ns else None
            for p in range(npair):
                o_h = []
                for e in range(2):
                    h = 2 * p + e
                    pk, ek = _kv_place(h, gqa)
                    vw = (vbuf if ek == e else vroll)[r0:r0 + 2 * BLOCK, pk * LANES:(pk + 1) * LANES]
                    o_h.append(_dot(probs[h], vw) * (1.0 / l_cols[h]))
                o_ref[r0:r0 + BLOCK, p * LANES:(p + 1) * LANES] = jnp.where(half[0], o_h[0], o_h[1]).astype(BF16)
            lse_ref[r0:r0 + BLOCK, :] = _per_head(m_cols) + jnp.log(_per_head(l_cols, 1.0))

    main = lambda w: pl.BlockSpec((None, tq, w), lambda r, i: (r, i, 0))
    prev = lambda w: pl.BlockSpec((None, BLOCK, w), lambda r, i: (r, jnp.maximum(i * ns - 1, 0), 0))
    in_specs = [main(wq), main(wk), prev(wk), main(wk), prev(wk), pl.BlockSpec(bias.shape, lambda r, i: (0, 0))]
    args = [q, k, k, v, v, bias]
    if has_sink:
        in_specs = [pl.BlockSpec(memory_space=pltpu.SMEM)] + in_specs
        args = [sink] + args
    scratch = [pltpu.VMEM((tq + BLOCK, wk), BF16)] * (4 if gqa else 2)
    return pl.pallas_call(
        body, name=name, grid=(dil, nt), in_specs=in_specs,
        out_specs=[main(wq), main(LANES)],
        out_shape=[jax.ShapeDtypeStruct((dil, length, wq), BF16), jax.ShapeDtypeStruct((dil, length, LANES), F32)],
        scratch_shapes=scratch,
    )(*args)


def _band_bwd(q, k, v, do, lse, delta, *, max_dist, name):
    dil, length, wq = q.shape
    wk = k.shape[2]
    gqa = wk != wq
    tq = min(ATTN_TILE, length)
    ns, nt = tq // BLOCK, length // tq
    npair = wq // LANES
    nblocks = length // BLOCK
    bias = _band_bias(max_dist, transposed=True)

    def body(q_ref, qn_ref, do_ref, don_ref, lse_ref, lsen_ref, dl_ref, dln_ref, k_ref, v_ref, bias_ref,
             dq_ref, dk_ref, dv_ref, stat_l, stat_d, dqt, kt, *rolled):
        i = pl.program_id(1)
        for pk in range(wk // LANES):
            kt[pk] = k_ref[:, pk * LANES:(pk + 1) * LANES].astype(F32).T.astype(BF16)
        if gqa:
            kroll, vroll, ktroll = rolled
            kroll[...] = pltpu.roll(k_ref[...], HEAD_DIM, 1)
            vroll[...] = pltpu.roll(v_ref[...], HEAD_DIM, 1)
            ktroll[0] = kroll[...].astype(F32).T.astype(BF16)
        for a in range(ns):
            rows = slice(a * BLOCK, (a + 1) * BLOCK)
            stat_l[a] = _rows_to_lanes(lse_ref[rows, :])
            stat_d[a] = _rows_to_lanes(dl_ref[rows, :])
        stat_l[ns] = _rows_to_lanes(lsen_ref[...])
        stat_d[ns] = _rows_to_lanes(dln_ref[...])

        @pl.when(i == 0)
        def _():
            dqt[:, :, 0:BLOCK] = jnp.zeros((npair, LANES, BLOCK), F32)

        @pl.when(i > 0)
        def _():
            dqt[:, :, 0:BLOCK] = dqt[:, :, tq:tq + BLOCK]

        dqt[:, :, BLOCK:] = jnp.zeros((npair, LANES, tq), F32)
        half2 = _half_masks(2 * BLOCK)
        row = lax.broadcasted_iota(jnp.int32, (LANES, BLOCK), 0)
        row_half = (row < HEAD_DIM, row >= HEAD_DIM)
        col_next = (lax.broadcasted_iota(jnp.int32, (1, 2 * BLOCK), 1) >= BLOCK).astype(F32)

        def scores(b):
            rows = slice(b * BLOCK, (b + 1) * BLOCK)
            nxt_rows = slice((b + 1) * BLOCK, (b + 2) * BLOCK)
            items = []
            for p in range(npair):
                lanes = slice(p * LANES, (p + 1) * LANES)
                q_next = q_ref[nxt_rows, lanes] if b + 1 < ns else qn_ref[:, lanes]
                do_next = do_ref[nxt_rows, lanes] if b + 1 < ns else don_ref[:, lanes]
                qw = jnp.concatenate([q_ref[rows, lanes], q_next], axis=0)
                dow = jnp.concatenate([do_ref[rows, lanes], do_next], axis=0)
                for e in range(2):
                    h = 2 * p + e
                    pk, ek = _kv_place(h, gqa)
                    klanes = slice(pk * LANES, (pk + 1) * LANES)
                    kb = (k_ref if ek == e else kroll)[rows, klanes]
                    vb = (v_ref if ek == e else vroll)[rows, klanes]
                    qm = jnp.where(half2[e], qw, jnp.zeros_like(qw))
                    dom = jnp.where(half2[e], dow, jnp.zeros_like(dow))
                    items.append(dict(p=p, e=e, h=h, pk=pk, ek=ek, qm=qm, dom=dom,
                                      st=_dot_nt(kb, qm), dpt=_dot_nt(vb, dom)))
            return items

        def probs(b, items):
            bt = bias_ref[...]
            if b == ns - 1:
                bt = bt + jnp.where(i == nt - 1, NEG, 0.0) * col_next
            for it in items:
                h = it["h"]
                lrow = jnp.concatenate([stat_l[b, h:h + 1, :], stat_l[b + 1, h:h + 1, :]], axis=1)
                drow = jnp.concatenate([stat_d[b, h:h + 1, :], stat_d[b + 1, h:h + 1, :]], axis=1)
                pt = jnp.exp(it["st"] + bt - lrow)
                it["ptb"] = pt.astype(BF16)
                it["dsb"] = (pt * (it["dpt"] - drow)).astype(BF16)

        pending = scores(0)
        for b in range(ns):
            rows = slice(b * BLOCK, (b + 1) * BLOCK)
            window = slice(b * BLOCK, (b + 2) * BLOCK)
            acc = {}
            items = pending
            probs(b, items)
            pending = scores(b + 1) if b + 1 < ns else None
            for p in range(npair):
                pair = items[2 * p:2 * p + 2]
                lanes = slice(p * LANES, (p + 1) * LANES)
                kparts = []
                for it in pair:
                    kbt = (kt if it["ek"] == it["e"] else ktroll)[it["pk"], :, rows]
                    kparts.append(jnp.where(row_half[it["e"]], kbt, jnp.zeros_like(kbt)))
                ds_keys = jnp.concatenate([it["dsb"] for it in pair], axis=0)
                dqt[p, :, window] += _dot(jnp.concatenate(kparts, axis=1), ds_keys)
                if not gqa:
                    q_both = jnp.concatenate([it["qm"] for it in pair], axis=0)
                    do_both = jnp.concatenate([it["dom"] for it in pair], axis=0)
                    dk_ref[rows, lanes] = _dot(jnp.concatenate([it["dsb"] for it in pair], axis=1), q_both).astype(BF16)
                    dv_ref[rows, lanes] = _dot(jnp.concatenate([it["ptb"] for it in pair], axis=1), do_both).astype(BF16)
                else:
                    for it in pair:
                        dv_c = _dot(it["ptb"], it["dom"])
                        dk_c = _dot(it["dsb"], it["qm"])
                        key = (it["pk"], it["ek"] == it["e"])
                        if key in acc:
                            acc[key] = (acc[key][0] + dk_c, acc[key][1] + dv_c)
                        else:
                            acc[key] = (dk_c, dv_c)
            if gqa:
                dk_al, dv_al = acc[(0, True)]
                dk_mis, dv_mis = acc[(0, False)]
                dk_ref[rows, :] = (dk_al + pltpu.roll(dk_mis, HEAD_DIM, 1)).astype(BF16)
                dv_ref[rows, :] = (dv_al + pltpu.roll(dv_mis, HEAD_DIM, 1)).astype(BF16)

        for p in range(npair):
            dq_ref[:, p * LANES:(p + 1) * LANES] = dqt[p, :, 0:tq].T.astype(BF16)

    main = lambda w: pl.BlockSpec((None, tq, w), lambda r, i: (r, i, 0))
    nxt = lambda w: pl.BlockSpec((None, BLOCK, w), lambda r, i: (r, jnp.minimum((i + 1) * ns, nblocks - 1), 0))
    scratch = [pltpu.VMEM((ns + 1, 8, LANES), F32), pltpu.VMEM((ns + 1, 8, LANES), F32),
               pltpu.VMEM((npair, LANES, tq + BLOCK), F32), pltpu.VMEM((wk // LANES, LANES, tq), BF16)]
    if gqa:
        scratch = scratch + [pltpu.VMEM((tq, wk), BF16)] * 2 + [pltpu.VMEM((1, LANES, tq), BF16)]
    return pl.pallas_call(
        body, name=name, grid=(dil, nt),
        in_specs=[main(wq), nxt(wq), main(wq), nxt(wq), main(LANES), nxt(LANES), main(LANES), nxt(LANES),
                  main(wk), main(wk), pl.BlockSpec(bias.shape, lambda r, i: (0, 0))],
        out_specs=[main(wq), main(wk), main(wk)],
        out_shape=[jax.ShapeDtypeStruct((dil, length, wq), BF16), jax.ShapeDtypeStruct((dil, length, wk), BF16),
                   jax.ShapeDtypeStruct((dil, length, wk), BF16)],
        scratch_shapes=scratch,
        compiler_params=pltpu.CompilerParams(dimension_semantics=("arbitrary", "arbitrary")),
    )(q, q, do, do, lse, lse, delta, delta, k, v, bias)


def _mem_attn_fwd(q, mk, mv):
    seq = q.shape[0]
    tq = min(ATTN_TILE, seq)
    sub_rows = min(4 * BLOCK, tq)
    ns = tq // sub_rows

    def body(q_ref, mk_ref, mv_ref, o_ref, lse_ref):
        half = _half_masks(sub_rows)

        def sub(a, carry):
            r0 = pl.multiple_of(a * sub_rows, sub_rows)
            scores = []
            for p in range(C_W // LANES):
                lanes = slice(p * LANES, (p + 1) * LANES)
                qp = q_ref[pl.ds(r0, sub_rows), lanes]
                for e in range(2):
                    scores.append(_dot_nt(jnp.where(half[e], qp, jnp.zeros_like(qp)), mk_ref[:, lanes]))
            m_cols, l_cols, probs = [], [], []
            for s in scores:
                m = jnp.max(s, axis=1, keepdims=True)
                pe = jnp.exp(s - m)
                probs.append(pe.astype(BF16))
                m_cols.append(m)
                l_cols.append(jnp.sum(pe, axis=1, keepdims=True))
            for p in range(C_W // LANES):
                lanes = slice(p * LANES, (p + 1) * LANES)
                o_h = [_dot(probs[2 * p + e], mv_ref[:, lanes]) * (1.0 / l_cols[2 * p + e]) for e in range(2)]
                o_ref[pl.ds(r0, sub_rows), lanes] = jnp.where(half[0], o_h[0], o_h[1]).astype(BF16)
            lse_ref[pl.ds(r0, sub_rows), :] = _per_head(m_cols) + jnp.log(_per_head(l_cols, 1.0))
            return carry

        lax.fori_loop(0, ns, sub, 0, unroll=True)

    row = lambda w: pl.BlockSpec((tq, w), lambda i: (i, 0))
    full = pl.BlockSpec((N_MEM, C_W), lambda i: (0, 0))
    return pl.pallas_call(
        body, name="mem_attn_fwd", grid=(seq // tq,), in_specs=[row(C_W), full, full],
        out_specs=[row(C_W), row(LANES)],
        out_shape=[jax.ShapeDtypeStruct((seq, C_W), BF16), jax.ShapeDtypeStruct((seq, LANES), F32)],
    )(q, mk, mv)


def _mem_attn_bwd(q, mk, mv, do, lse, delta):
    seq = q.shape[0]
    tq = min(ATTN_TILE, seq)
    ns = tq // BLOCK
    npair = C_W // LANES

    def body(q_ref, mk_ref, mv_ref, do_ref, lse_ref, dl_ref, dq_ref, dmk_ref, dmv_ref, stat_l, stat_d, mkt, dqt):
        @pl.when(pl.program_id(0) == 0)
        def _():
            dmk_ref[...] = jnp.zeros_like(dmk_ref)
            dmv_ref[...] = jnp.zeros_like(dmv_ref)
            for p in range(npair):
                mkt[p] = mk_ref[:, p * LANES:(p + 1) * LANES].astype(F32).T.astype(BF16)

        for a in range(ns):
            rows = slice(a * BLOCK, (a + 1) * BLOCK)
            stat_l[a] = _rows_to_lanes(lse_ref[rows, :])
            stat_d[a] = _rows_to_lanes(dl_ref[rows, :])
        span = min(2, ns)
        half = _half_masks(span * BLOCK)
        row = lax.broadcasted_iota(jnp.int32, (LANES, N_MEM), 0)
        row_half = (row < HEAD_DIM, row >= HEAD_DIM)

        for a in range(0, ns, span):
            rows = slice(a * BLOCK, (a + span) * BLOCK)
            items = []
            for p in range(npair):
                lanes = slice(p * LANES, (p + 1) * LANES)
                qp = q_ref[rows, lanes]
                dop = do_ref[rows, lanes]
                for e in range(2):
                    qm = jnp.where(half[e], qp, jnp.zeros_like(qp))
                    dom = jnp.where(half[e], dop, jnp.zeros_like(dop))
                    items.append(dict(p=p, e=e, qm=qm, dom=dom, st=_dot_nt(mk_ref[:, lanes], qm),
                                      dpt=_dot_nt(mv_ref[:, lanes], dom)))
            for it in items:
                h = 2 * it["p"] + it["e"]
                lrow = jnp.concatenate([stat_l[a + k, h:h + 1, :] for k in range(span)], axis=1)
                drow = jnp.concatenate([stat_d[a + k, h:h + 1, :] for k in range(span)], axis=1)
                pt = jnp.exp(it["st"] - lrow)
                it["ptb"] = pt.astype(BF16)
                it["dsb"] = (pt * (it["dpt"] - drow)).astype(BF16)
            for p in range(npair):
                lanes = slice(p * LANES, (p + 1) * LANES)
                pair = [it for it in items if it["p"] == p]
                join = lambda name, axis: jnp.concatenate([it[name] for it in pair], axis=axis)
                dmv_ref[:, lanes] += _dot(join("ptb", 1), join("dom", 0))
                dmk_ref[:, lanes] += _dot(join("dsb", 1), join("qm", 0))
                kbt = mkt[p]
                k_both = jnp.concatenate([jnp.where(row_half[e], kbt, jnp.zeros_like(kbt)) for e in range(2)], axis=1)
                dqt[p, :, rows] = _dot(k_both, join("dsb", 0))
        for p in range(npair):
            dq_ref[:, p * LANES:(p + 1) * LANES] = dqt[p].T.astype(BF16)

    row = lambda w: pl.BlockSpec((tq, w), lambda i: (i, 0))
    full = pl.BlockSpec((N_MEM, C_W), lambda i: (0, 0))
    return pl.pallas_call(
        body, name="mem_attn_bwd", grid=(seq // tq,),
        in_specs=[row(C_W), full, full, row(C_W), row(LANES), row(LANES)], out_specs=[row(C_W), full, full],
        out_shape=[jax.ShapeDtypeStruct((seq, C_W), BF16), jax.ShapeDtypeStruct((N_MEM, C_W), F32),
                   jax.ShapeDtypeStruct((N_MEM, C_W), F32)],
        scratch_shapes=[pltpu.VMEM((ns, 8, LANES), F32)] * 2
        + [pltpu.VMEM((npair, LANES, N_MEM), BF16), pltpu.VMEM((npair, LANES, tq), F32)],
        compiler_params=pltpu.CompilerParams(dimension_semantics=("arbitrary",)),
    )(q, mk, mv, do, lse, delta)


def _silu_and_grad(g):
    s = 1.0 / (1.0 + jnp.exp(-g))
    return g * s, s * (1.0 + g * (1.0 - s))


def _post(x, target, post_norm, w_out, sink_row, oa, lse_a, ga, ob_list, lseb_list, gb, oc, gc):
    seq = x.shape[0]
    tm = min(ROW_TILE, seq)
    inv_d = 1.0 / D_MODEL
    nd = len(B_DILS)

    def body(*refs):
        (x_ref, t_ref, gp_ref, w_ref, sink_ref, oa_ref, lsea_ref, ga_ref), refs = refs[:8], refs[8:]
        ob_refs, lb_refs, (gb_ref, oc_ref, gc_ref), refs = refs[:nd], refs[nd:2 * nd], refs[2 * nd:2 * nd + 3], refs[2 * nd + 3:]
        (g_ref, doa_ref, dla_ref, dga_ref), refs = refs[:4], refs[4:]
        dob_refs, lsec_refs, dlb_refs, refs = refs[:nd], refs[nd:2 * nd], refs[2 * nd:3 * nd], refs[3 * nd:]
        (dgb_ref, doc_ref, dlc_ref, dgc_ref, gw_ref, gpost_ref, gsink_ref, loss_ref), refs = refs[:8], refs[8:]
        ycat, obufs, lbufs, st_do, st_l, st_d = refs[0], refs[1:nd], refs[nd:2 * nd - 1], refs[2 * nd - 1], refs[2 * nd], refs[2 * nd + 1]

        @pl.when(pl.program_id(0) == 0)
        def _():
            gw_ref[...] = jnp.zeros_like(gw_ref)
            gpost_ref[...] = jnp.zeros_like(gpost_ref)
            gsink_ref[...] = jnp.zeros_like(gsink_ref)
            loss_ref[...] = jnp.zeros_like(loss_ref)

        o_i, l_i = [ob_refs[0][0].astype(F32)], [lb_refs[0][0]]
        for k in range(1, nd):
            _from_residues(ob_refs[k], obufs[k - 1], B_DILS[k])
            _from_residues(lb_refs[k], lbufs[k - 1], B_DILS[k])
            o_i.append(_stage_read(obufs[k - 1]))
            l_i.append(_stage_read(lbufs[k - 1]))
        mx = l_i[0]
        for l in l_i[1:]:
            mx = jnp.maximum(mx, l)
        w_i = [jnp.exp(l - mx) for l in l_i]
        z = w_i[0]
        for w in w_i[1:]:
            z = z + w
        _stage_write(st_l, mx + jnp.log(z))
        expand = _head_expand_matrix(B_W)
        inv_z = 1.0 / z
        ob = None
        for w, o in zip(w_i, o_i):
            term = _dot_split(w * inv_z, expand, 2) * o
            ob = term if ob is None else ob + term
        oa, oc = oa_ref[...].astype(F32), oc_ref[...].astype(F32)
        sa, dsa = _silu_and_grad(ga_ref[...].astype(F32))
        sb, dsb = _silu_and_grad(gb_ref[...].astype(F32))
        sc, dsc = _silu_and_grad(gc_ref[...].astype(F32))
        ycat[:, 0:A_W] = (oa * sa).astype(BF16)
        ycat[:, A_W:A_W + B_W] = (ob * sb).astype(BF16)
        ycat[:, A_W + B_W:] = (oc * sc).astype(BF16)
        y2 = _dot(ycat[...], w_ref[...])
        r = lax.rsqrt(jnp.mean(y2 * y2, axis=-1, keepdims=True) + RMS_EPS)
        zhat = y2 * r
        gp = gp_ref[...]
        err = x_ref[...] + zhat * gp - t_ref[...]
        loss_ref[...] += jnp.sum(err * err) * (0.5 * inv_d)
        g = err * inv_d
        g_ref[...] = g
        gpost_ref[...] += jnp.sum(g * zhat, axis=0, keepdims=True)
        a = g * gp
        dy2 = (r * (a - zhat * jnp.mean(a * zhat, axis=-1, keepdims=True))).astype(BF16)
        gw_ref[...] += _dot_tn(ycat[...], dy2)
        dycat = _dot_nt(dy2, w_ref[...])
        dya, dyb, dyc = dycat[:, 0:A_W], dycat[:, A_W:A_W + B_W], dycat[:, A_W + B_W:]
        doa, dob, doc = dya * sa, dyb * sb, dyc * sc
        doa_ref[...] = doa.astype(BF16)
        doc_ref[...] = doc.astype(BF16)
        dga_ref[...] = (dya * oa * dsa).astype(BF16)
        dgb_ref[...] = (dyb * ob * dsb).astype(BF16)
        dgc_ref[...] = (dyc * oc * dsc).astype(BF16)
        dl_a = _dot_split(doa * oa, _head_sum_matrix(A_W), 2)
        dla_ref[...] = dl_a
        dlc_ref[...] = _dot_split(doc * oc, _head_sum_matrix(C_W), 2)
        gsink_ref[...] += jnp.sum(jnp.exp(sink_ref[...] - lsea_ref[...]) * dl_a, axis=0, keepdims=True)
        _stage_write(st_do, dob)
        _stage_write(st_d, _dot_split(dob * ob, _head_sum_matrix(B_W), 2))
        for k, dil in enumerate(B_DILS):
            _to_residues(st_do, dob_refs[k], dil)
            _to_residues(st_l, lsec_refs[k], dil)
            _to_residues(st_d, dlb_refs[k], dil)

    row = lambda w: pl.BlockSpec((tm, w), lambda i: (i, 0))
    full = lambda shape: pl.BlockSpec(shape, lambda i: (0,) * len(shape))
    res_specs = lambda w: [_residue_spec(d, tm, w) for d in B_DILS]
    res_shapes = lambda w, dt: [jax.ShapeDtypeStruct((d, seq // d, w), dt) for d in B_DILS]
    ins = [x, target, post_norm, w_out, sink_row, oa, lse_a, ga, *ob_list, *lseb_list, gb, oc, gc]
    in_specs = ([row(D_MODEL), row(D_MODEL), full((1, D_MODEL)), full((D_MODEL, D_MODEL)), full((1, LANES)),
                 row(A_W), row(LANES), row(A_W)] + res_specs(B_W) + res_specs(LANES) + [row(B_W), row(C_W), row(C_W)])
    out_shape = ([jax.ShapeDtypeStruct((seq, D_MODEL), F32), jax.ShapeDtypeStruct((seq, A_W), BF16),
                  jax.ShapeDtypeStruct((seq, LANES), F32), jax.ShapeDtypeStruct((seq, A_W), BF16)]
                 + res_shapes(B_W, BF16) + res_shapes(LANES, F32) + res_shapes(LANES, F32)
                 + [jax.ShapeDtypeStruct((seq, B_W), BF16), jax.ShapeDtypeStruct((seq, C_W), BF16),
                    jax.ShapeDtypeStruct((seq, LANES), F32), jax.ShapeDtypeStruct((seq, C_W), BF16),
                    jax.ShapeDtypeStruct((D_MODEL, D_MODEL), F32), jax.ShapeDtypeStruct((1, D_MODEL), F32),
                    jax.ShapeDtypeStruct((1, LANES), F32), jax.ShapeDtypeStruct((1, LANES), F32)])
    out_specs = ([row(D_MODEL), row(A_W), row(LANES), row(A_W)] + res_specs(B_W) + res_specs(LANES) + res_specs(LANES)
                 + [row(B_W), row(C_W), row(LANES), row(C_W),
                    full((D_MODEL, D_MODEL)), full((1, D_MODEL)), full((1, LANES)), full((1, LANES))])
    scratch = ([pltpu.VMEM((tm, D_MODEL), BF16)] + [_stage(tm, B_W)] * (nd - 1) + [_stage(tm, LANES)] * (nd - 1)
               + [_stage(tm, B_W), _stage(tm, LANES), _stage(tm, LANES)])
    res = pl.pallas_call(
        body, name="post", grid=(seq // tm,), in_specs=in_specs, out_specs=out_specs, out_shape=out_shape,
        scratch_shapes=scratch,
        compiler_params=pltpu.CompilerParams(dimension_semantics=("arbitrary",)),
    )(*ins)
    out = dict(g=res[0], doa=res[1], dl_a=res[2], dga=res[3], dob=res[4:4 + nd], lse_b=res[4 + nd:4 + 2 * nd],
               dl_b=res[4 + 2 * nd:4 + 3 * nd])
    rest = res[4 + 3 * nd:]
    out.update(dgb=rest[0], doc=rest[1], dl_c=rest[2], dgc=rest[3], gw_out=rest[4], gpost=rest[5], gsink=rest[6],
               loss=rest[7])
    return out


def _grad_w_in(ut, nat, res):
    seq = ut.shape[1]
    tm = min(ROW_TILE, seq)
    nd = len(B_DILS)
    nat_list = [nat[n] for n in _NATURAL]
    res_list = [a for n in _DILATED for a in res[n]]
    rope = _rope_tables(seq, tm)

    def body(rl_ref, rb_ref, ut_ref, *refs):
        nat_refs = dict(zip(_NATURAL, refs[:len(_NATURAL)]))
        refs = refs[len(_NATURAL):]
        res_refs = {n: refs[nd * k:nd * (k + 1)] for k, n in enumerate(_DILATED)}
        refs = refs[nd * len(_DILATED):]
        dproj_ref, gw_ref = refs[:2]
        bufs = {n: refs[2 + (nd - 1) * k:2 + (nd - 1) * (k + 1)] for k, n in enumerate(_DILATED)}

        @pl.when(pl.program_id(0) == 0)
        def _():
            gw_ref[...] = jnp.zeros_like(gw_ref)

        for n in _DILATED:
            for k in range(1, nd):
                _from_residues(res_refs[n][k], bufs[n][k - 1], B_DILS[k])
        c, sm, sp = _rope_coeffs(rl_ref, rb_ref)
        sm, sp = -sm, -sp
        for blk, (name, off, roped, scaled) in enumerate(_PROJ_LAYOUT):
            lanes = slice(off, off + LANES)
            if name in nat_refs:
                piece = nat_refs[name][:, lanes].astype(F32)
            else:
                piece = res_refs[name][0][0, :, lanes].astype(F32)
                for buf in bufs[name]:
                    piece = piece + buf[off // LANES]
            if roped:
                piece = _rope(piece, c, sm, sp)
            if scaled:
                piece = piece * SCALE
            dproj_ref[:, blk * LANES:(blk + 1) * LANES] = piece.astype(BF16)
        for j in range(N_CHIPS):
            gw_ref[j] += _dot(ut_ref[...], dproj_ref[:, j * SHARD_IN:(j + 1) * SHARD_IN])

    row = lambda w: pl.BlockSpec((tm, w), lambda i: (i, 0))
    in_specs = ([pl.BlockSpec(rope[0].shape, lambda i: (0, 0)), pl.BlockSpec((8, 2 * LANES), lambda i: (i, 0)),
                 pl.BlockSpec((D_MODEL, tm), lambda i: (0, i))]
                + [row(a.shape[1]) for a in nat_list]
                + [_residue_spec(d, tm, B_W) for _ in _DILATED for d in B_DILS])
    return pl.pallas_call(
        body, name="grad_w_in", grid=(seq // tm,), in_specs=in_specs,
        out_specs=[row(D_IN), pl.BlockSpec((N_CHIPS, D_MODEL, SHARD_IN), lambda i: (0, 0, 0))],
        out_shape=[jax.ShapeDtypeStruct((seq, D_IN), BF16), jax.ShapeDtypeStruct((N_CHIPS, D_MODEL, SHARD_IN), F32)],
        scratch_shapes=[_stage(tm, B_W)] * ((nd - 1) * len(_DILATED)),
        compiler_params=pltpu.CompilerParams(dimension_semantics=("arbitrary",)),
    )(*rope, ut, *nat_list, *res_list)


def _input_grad(x, g, pre_norm, w_in_g, dproj, gx_prev, span, after, name):
    seq = x.shape[0]
    tm = seq // INPUT_GRAD_TILES
    first_block, steps = span

    def body(*refs):
        x_ref, g_ref, gp_ref, w_ref, dp_ref = refs[:5]
        gx_ref, gpre_ref = refs[-2:]

        @pl.when(pl.program_id(0) == 0)
        def _():
            gpre_ref[...] = jnp.zeros_like(gpre_ref)

        du = None
        for j in range(N_CHIPS):
            term = _dot_nt(dp_ref[:, j * SHARD_IN:(j + 1) * SHARD_IN], w_ref[j])
            du = term if du is None else du + term
        xv = x_ref[...]
        r = lax.rsqrt(jnp.mean(xv * xv, axis=-1, keepdims=True) + RMS_EPS)
        xhat = xv * r
        gpre_ref[...] += jnp.sum(du * xhat, axis=0, keepdims=True)
        a = du * gp_ref[...]
        gx_ref[...] = g_ref[...] + r * (a - xhat * jnp.mean(a * xhat, axis=-1, keepdims=True))

    row = lambda w: pl.BlockSpec((tm, w), lambda i: (first_block + i, 0))
    full = lambda a: pl.BlockSpec(a.shape, lambda i: (0,) * a.ndim)
    any_spec = pl.BlockSpec(memory_space=pl.ANY)
    ins = [x, g, pre_norm, w_in_g, dproj]
    in_specs = [row(D_MODEL), row(D_MODEL), full(pre_norm), full(w_in_g), row(D_IN)]
    aliases = {}
    if gx_prev is not None:
        aliases[len(ins)] = 0
        ins.append(gx_prev)
        in_specs.append(any_spec)
    if after is not None:
        ins.append(after)
        in_specs.append(any_spec)
    return pl.pallas_call(
        body, name=name, grid=(steps,), in_specs=in_specs,
        out_specs=[row(D_MODEL), pl.BlockSpec((1, D_MODEL), lambda i: (0, 0))],
        out_shape=[jax.ShapeDtypeStruct((seq, D_MODEL), F32), jax.ShapeDtypeStruct((1, D_MODEL), F32)],
        input_output_aliases=aliases,
        compiler_params=pltpu.CompilerParams(dimension_semantics=("arbitrary",)),
    )(*ins)


def _exchange_start(ex, name):
    n_in, n_out, n_sem = len(ex["ins"]), len(ex["outs"]), len(ex["sems"])

    def body(*refs):
        in_refs, land_refs, sems = refs[:n_in], refs[n_in:n_in + n_out], refs[n_in + n_out:n_in + n_out + n_sem]
        ex["start"](in_refs, land_refs, *sems)
        token = refs[-1]
        token[...] = jnp.zeros_like(token)

    hbm = pl.BlockSpec(memory_space=pltpu.HBM)
    sem = pl.BlockSpec(memory_space=pltpu.SEMAPHORE)
    ins = [pltpu.with_memory_space_constraint(a, pltpu.HBM) for a in ex["ins"]]
    landing = [pltpu.with_memory_space_constraint(lax.empty(o.shape, o.dtype), pltpu.HBM) for o in ex["outs"]]
    res = pl.pallas_call(
        body, name=name,
        out_shape=list(ex["sems"]) + [pltpu.HBM(a.shape, a.dtype) for a in ex["ins"]]
        + [pltpu.HBM(o.shape, o.dtype) for o in ex["outs"]] + [jax.ShapeDtypeStruct((8, LANES), F32)],
        in_specs=[hbm] * (n_in + n_out),
        out_specs=[sem] * n_sem + [hbm] * (n_in + n_out) + [pl.BlockSpec(memory_space=pltpu.VMEM)],
        input_output_aliases={k: n_sem + k for k in range(n_in + n_out)},
        compiler_params=pltpu.CompilerParams(has_side_effects=pltpu.SideEffectType.DATAFLOW_SIDE_EFFECTING),
    )(*ins, *landing)
    return res[:-1], res[-1]


def _exchange_wait(ex, handles, after, name):
    n_in, n_out, n_sem = len(ex["ins"]), len(ex["outs"]), len(ex["sems"])
    sems, thru = handles[:n_sem], handles[n_sem:]

    def body(*refs):
        in_refs, land_refs = refs[:n_in], refs[n_in:n_in + n_out]
        sem_refs = refs[n_in + n_out:n_in + n_out + n_sem]
        ex["finish"](in_refs, land_refs, *sem_refs)

    hbm = pl.BlockSpec(memory_space=pltpu.HBM)
    sem = pl.BlockSpec(memory_space=pltpu.SEMAPHORE)
    res = pl.pallas_call(
        body, name=name,
        out_shape=[pltpu.HBM(a.shape, a.dtype) for a in thru],
        in_specs=[hbm] * (n_in + n_out) + [sem] * n_sem + [pl.BlockSpec(memory_space=pl.ANY)],
        out_specs=[hbm] * (n_in + n_out),
        input_output_aliases={k: k for k in range(n_in + n_out)},
        compiler_params=pltpu.CompilerParams(has_side_effects=pltpu.SideEffectType.DATAFLOW_SIDE_EFFECTING),
    )(*thru, *sems, after)
    return res[:n_in], res[n_in:]


def _start_finish(build):
    def start(*refs):
        for cp in build(*refs):
            cp.start()

    def finish(*refs):
        for cp in build(*refs):
            cp.wait()

    return dict(start=start, finish=finish)


def _pair_exchange(grads):
    n = len(grads)

    def build(srcs, outs, send_sems, recv_sems):
        x, y, c = lax.axis_index("x"), lax.axis_index("y"), lax.axis_index("c")
        copies = []
        for t in range(n):
            rows = grads[t].shape[1] // 2
            copies.append(pltpu.make_async_remote_copy(
                src_ref=srcs[t].at[:, pl.ds((1 - c) * rows, rows)], dst_ref=outs[t],
                send_sem=send_sems.at[t], recv_sem=recv_sems.at[t], device_id=(x, y, 1 - c), device_id_type=MESH))
        return copies

    return dict(ins=list(grads), **_start_finish(build),
                outs=[jax.ShapeDtypeStruct((g.shape[0], g.shape[1] // 2, g.shape[2]), g.dtype) for g in grads],
                sems=[pltpu.SemaphoreType.DMA((n,)), pltpu.SemaphoreType.DMA((n,))])


def _pair_add(core, own, got):
    nchip, rows2, width = own.shape
    rows = rows2 // 2
    tr = min(ROW_TILE, rows)
    nb = rows // tr

    def body(core_ref, own_ref, got_ref, out_ref):
        out_ref[...] = (own_ref[...] + got_ref[...]).astype(BF16)

    grid_spec = pltpu.PrefetchScalarGridSpec(
        num_scalar_prefetch=1, grid=(nchip, nb),
        in_specs=[pl.BlockSpec((None, tr, width), lambda k, i, core_ref: (k, core_ref[0] * nb + i, 0)),
                  pl.BlockSpec((None, tr, width), lambda k, i, core_ref: (k, i, 0))],
        out_specs=pl.BlockSpec((None, tr, width), lambda k, i, core_ref: (k, i, 0)))
    return pl.pallas_call(
        body, name=f"pair_add_{width}", grid_spec=grid_spec,
        out_shape=jax.ShapeDtypeStruct((nchip, rows, width), BF16),
    )(core, own, got)


def _chip_exchange(parts):
    n = len(parts)

    def build(srcs, outs, send_sems, recv_sems, local_sems):
        x, y, c = lax.axis_index("x"), lax.axis_index("y"), lax.axis_index("c")
        my_chip = 2 * x + y
        chips = [(1 - x, y), (x, 1 - y), (1 - x, 1 - y)]
        copies = [pltpu.make_async_copy(srcs[t].at[my_chip], outs[t].at[my_chip], local_sems.at[t]) for t in range(n)]
        for j, (cx, cy) in enumerate(chips):
            for t in range(n):
                k = n * j + t
                copies.append(pltpu.make_async_remote_copy(
                    src_ref=srcs[t].at[2 * cx + cy], dst_ref=outs[t].at[my_chip], send_sem=send_sems.at[k],
                    recv_sem=recv_sems.at[k], device_id=(cx, cy, c), device_id_type=MESH))
        return copies

    return dict(ins=list(parts), **_start_finish(build), outs=[jax.ShapeDtypeStruct(p.shape, p.dtype) for p in parts],
                sems=[pltpu.SemaphoreType.DMA((3 * n,)), pltpu.SemaphoreType.DMA((3 * n,)),
                      pltpu.SemaphoreType.DMA((n,))])


def _slot_sum(slots, name, core=None):
    ns, rows, width = slots.shape
    tr = min(ROW_TILE, rows)

    def body(*refs):
        in_ref, out_ref = refs[-2:]
        acc = in_ref[0].astype(F32)
        for s in range(1, ns):
            acc = acc + in_ref[s].astype(F32)
        out_ref[...] = acc

    if core is None:
        return pl.pallas_call(
            body, name=name, grid=(rows // tr,),
            in_specs=[pl.BlockSpec((ns, tr, width), lambda i: (0, i, 0))],
            out_specs=pl.BlockSpec((tr, width), lambda i: (i, 0)),
            out_shape=jax.ShapeDtypeStruct((rows, width), F32),
        )(slots)
    grid_spec = pltpu.PrefetchScalarGridSpec(
        num_scalar_prefetch=1, grid=(rows // tr,),
        in_specs=[pl.BlockSpec((ns, tr, width), lambda i, core_ref: (0, i, 0))],
        out_specs=pl.BlockSpec((None, tr, width), lambda i, core_ref: (core_ref[0], i, 0)))
    return pl.pallas_call(
        body, name=name, grid_spec=grid_spec, out_shape=jax.ShapeDtypeStruct((2, rows, width), F32),
    )(core, slots)


def _pair_gather(bufs, small):
    n = len(bufs)

    def body(*refs):
        small_ref, outs, small_out = refs[n], refs[n + 1:2 * n + 1], refs[2 * n + 1]
        send_sems, recv_sems, local_sem = refs[2 * n + 2:]
        x, y, c = lax.axis_index("x"), lax.axis_index("y"), lax.axis_index("c")
        me = 4 * x + 2 * y + c
        chips = [(1 - x, y), (x, 1 - y), (1 - x, 1 - y)]
        mine = pltpu.make_async_copy(small_ref, small_out.at[me], local_sem)
        mine.start()
        copies = [pltpu.make_async_remote_copy(
            src_ref=outs[t].at[c], dst_ref=outs[t].at[c], send_sem=send_sems.at[t], recv_sem=recv_sems.at[t],
            device_id=(x, y, 1 - c), device_id_type=MESH) for t in range(n)]
        peers = [(x, y, 1 - c)] + [(cx, cy, cc) for (cx, cy) in chips for cc in (c, 1 - c)]
        for j, peer in enumerate(peers):
            copies.append(pltpu.make_async_remote_copy(
                src_ref=small_ref, dst_ref=small_out.at[me], send_sem=send_sems.at[n + j],
                recv_sem=recv_sems.at[n + j], device_id=peer, device_id_type=MESH))
        for cp in copies:
            cp.start()
        for cp in copies:
            cp.wait()
        mine.wait()

    any_spec = pl.BlockSpec(memory_space=pl.ANY)
    res = pl.pallas_call(
        body, name="pair_gather",
        out_shape=[jax.ShapeDtypeStruct(b.shape, b.dtype) for b in bufs]
        + [jax.ShapeDtypeStruct((8,) + small.shape, small.dtype)],
        in_specs=[any_spec] * (n + 1), out_specs=[any_spec] * (n + 1),
        input_output_aliases={t: t for t in range(n)},
        scratch_shapes=[pltpu.SemaphoreType.DMA((n + 7,)), pltpu.SemaphoreType.DMA((n + 7,)),
                        pltpu.SemaphoreType.DMA],
    )(*bufs, small)
    return [r.reshape(2 * b.shape[1], b.shape[2]) for r, b in zip(res[:n], bufs)], res[n]


def _adamw(w, g, m, v, name):
    rows, width = w.shape
    tr = min(ROW_TILE // 2, rows)
    c1 = 1.0 / (1.0 - ADAM_B1 ** ADAM_STEP)
    c2 = 1.0 / (1.0 - ADAM_B2 ** ADAM_STEP)

    def body(w_ref, g_ref, m_ref, v_ref, d_ref, nm_ref, nv_ref):
        gv = g_ref[...]
        nm = ADAM_B1 * m_ref[...] + (1.0 - ADAM_B1) * gv
        nv = ADAM_B2 * v_ref[...] + (1.0 - ADAM_B2) * (gv * gv)
        nm_ref[...] = nm
        nv_ref[...] = nv
        d_ref[...] = -ADAM_LR * ((nm * c1) / (jnp.sqrt(nv * c2) + ADAM_EPS) + ADAM_WD * w_ref[...])

    spec = pl.BlockSpec((tr, width), lambda i: (i, 0))
    return pl.pallas_call(
        body, name=name, grid=(rows // tr,), in_specs=[spec] * 4, out_specs=[spec] * 3,
        out_shape=[jax.ShapeDtypeStruct(w.shape, F32)] * 3,
    )(w, g, m, v)


def _local_step(x, mem, target, pre_norm, sink_a, mem_norm, post_norm, w_in_g, w_out, w_mkv, gathers=None,
                own=None):
    first_gather, late_gather = gathers if gathers else (None, None)
    chip, w_own = own if own else (None, None)
    if gathers:
        handles, token = _exchange_start(first_gather, "w_in_gather_start")
        w_own = handles[len(first_gather["sems"])].reshape(D_MODEL, SHARD_IN)
        u, ut, p_own = _pre_norm(x, pre_norm, w_own, token)
        (w_own_halves,), landed = _exchange_wait(first_gather, handles, ut, "w_in_gather_wait")
        landed, = _sibling_swap(first_gather, landed, "w_in_sibling_swap")
        w_in_g = lax.dynamic_update_slice(landed, w_own_halves[None], (chip[0], 0, 0, 0))
        w_in_g = w_in_g.reshape(N_CHIPS, D_MODEL, SHARD_IN)
    else:
        u, ut, p_own = _pre_norm(x, pre_norm, w_own)
    pr = _pre_proj(u, w_in_g, (chip, p_own) if own else None, late_gather)
    pr["ut"] = ut
    if gathers:
        w_out, w_mkv = (g.reshape(D_MODEL, g.shape[-1]) for g in pr["hosted"])
    mk, mv = _mem_kv(mem, mem_norm, w_mkv)
    sink = sink_a.reshape(-1)
    qa, ka, va = pr["qa"][None], pr["ka"][None], pr["va"][None]
    oa, lse_a = _band_fwd(qa, ka, va, sink, max_dist=A_WINDOW - 1, name="swa_fwd")
    ob_list, lseb_list = [], []
    for k, (win, dil) in enumerate(B_CONFIGS):
        o_i, l_i = _band_fwd(pr["qb"][k], pr["kb"][k], pr["vb"][k], None, max_dist=win // dil, name=f"dil{dil}_fwd")
        ob_list.append(o_i)
        lseb_list.append(l_i)
    oc, lse_c = _mem_attn_fwd(pr["qc"], mk, mv)
    sink_row = jnp.pad(sink, (0, LANES - sink.shape[0])).reshape(1, LANES)
    po = _post(x, target, post_norm, w_out, sink_row, oa[0], lse_a[0], pr["ga"], ob_list, lseb_list, pr["gb"], oc,
               pr["gc"])
    dqc, dmk, dmv = _mem_attn_bwd(pr["qc"], mk, mv, po["doc"], lse_c, po["dl_c"])
    dqa, dka, dva = _band_bwd(qa, ka, va, po["doa"][None], lse_a, po["dl_a"][None], max_dist=A_WINDOW - 1,
                              name="swa_bwd")
    res = dict(qb=[], kb=[], vb=[])
    for k, (win, dil) in enumerate(B_CONFIGS):
        dq_i, dk_i, dv_i = _band_bwd(pr["qb"][k], pr["kb"][k], pr["vb"][k], po["dob"][k], po["lse_b"][k],
                                     po["dl_b"][k], max_dist=win // dil, name=f"dil{dil}_bwd")
        res["qb"].append(dq_i)
        res["kb"].append(dk_i)
        res["vb"].append(dv_i)
    nat = dict(qa=dqa[0], ka=dka[0], va=dva[0], ga=po["dga"], gb=po["dgb"], qc=dqc, gc=po["dgc"])
    dproj, gw_in = _grad_w_in(pr["ut"], nat, res)
    gw_mkv, gmem = _mem_kv_bwd(mem, mem_norm, w_mkv, dmk, dmv)
    gsink = -po["gsink"][0, :sink.shape[0]]
    return dict(loss=po["loss"], g=po["g"], dproj=dproj, gw_in=gw_in, gw_out=po["gw_out"], gw_mkv=gw_mkv,
                gpost=po["gpost"], gmem=gmem, gsink=gsink, w_in_g=w_in_g)


def kernel(x, mem, pre_norm, w_in, sink_a, mem_norm, w_mem_kv, w_out, post_norm, loss_target, m_pre_norm, m_w_in, m_sink_a, m_mem_norm, m_w_mem_kv, m_w_out, m_post_norm, v_pre_norm, v_w_in, v_sink_a, v_mem_norm, v_w_mem_kv, v_w_out, v_post_norm):
    w_own = w_in[0].astype(BF16)
    gathers = (_gather_exchange([w_own], hosted=False),
               _gather_exchange([w_out[0].astype(BF16), w_mem_kv[0].astype(BF16)]))
    chip = (2 * lax.axis_index("x") + lax.axis_index("y")).astype(jnp.int32).reshape(1)
    loc = _local_step(x[0], mem[0], loss_target[0], pre_norm, sink_a, mem_norm, post_norm, None, None, None, gathers,
                      (chip, w_own))
    big = [loc["gw_in"], loc["gw_out"].reshape(N_CHIPS, D_MODEL // N_CHIPS, D_MODEL),
           loc["gw_mkv"].reshape(N_CHIPS, D_MODEL // N_CHIPS, 2 * C_W)]
    core = lax.axis_index("c").astype(jnp.int32).reshape(1)
    w_in_full = loc["w_in_g"]
    step_in = (x[0], loc["g"], pre_norm, w_in_full, loc["dproj"])
    pair_ex = _pair_exchange(big)
    pair_handles, token = _exchange_start(pair_ex, "pair_exchange_start")
    gx_a, gpre_a = _input_grad(*step_in, None, (0, 2), token, "input_grad_a")
    big, got = _exchange_wait(pair_ex, pair_handles, gpre_a, "pair_exchange_wait")
    parts = [_pair_add(core, own, g) for own, g in zip(big, got)]
    chip_ex = _chip_exchange(parts)
    chip_handles, token = _exchange_start(chip_ex, "chip_exchange_start")
    grad_x, gpre_b = _input_grad(*step_in, gx_a, (2, 14), token, "input_grad_b")
    _, slots = _exchange_wait(chip_ex, chip_handles, gpre_b, "chip_exchange_wait")
    halves = [_slot_sum(s, name=f"chip_sum_{s.shape[2]}", core=core) for s in slots]
    widen = lambda a: jnp.pad(a.reshape(1, -1), ((0, 0), (0, D_MODEL - a.size)))
    small = jnp.concatenate([gpre_a, loc["gpost"], loc["gmem"], widen(loc["gsink"]), widen(loc["loss"]), gpre_b,
                             jnp.zeros((2, D_MODEL), F32)], axis=0)
    (g_in, g_out, g_mkv), small_slots = _pair_gather(halves, small)
    small_sum = _slot_sum(small_slots, name="device_sum")
    g_pre, g_post, g_mem = small_sum[0:1] + small_sum[5:6], small_sum[1:2], small_sum[2:3]
    g_sink = small_sum[3:4, :sink_a.shape[1]]
    loss = small_sum[4, 0]

    d_in, nm_in, nv_in = _adamw(w_in[0], g_in, m_w_in[0], v_w_in[0], "adamw_in")
    d_out, nm_out, nv_out = _adamw(w_out[0], g_out, m_w_out[0], v_w_out[0], "adamw_out")
    d_mkv, nm_mkv, nv_mkv = _adamw(w_mem_kv[0], g_mkv, m_w_mem_kv[0], v_w_mem_kv[0], "adamw_mkv")
    pad6 = lambda a: jnp.pad(a, ((0, 0), (0, D_MODEL - a.shape[1])))
    stack = lambda a, b, c_, d_: jnp.concatenate([a, b, c_, pad6(d_), jnp.zeros((4, D_MODEL), F32)], axis=0)
    d_s, nm_s, nv_s = _adamw(stack(pre_norm, post_norm, mem_norm, sink_a),
                             jnp.concatenate([g_pre, small_sum[1:]], axis=0),
                             stack(m_pre_norm, m_post_norm, m_mem_norm, m_sink_a),
                             stack(v_pre_norm, v_post_norm, v_mem_norm, v_sink_a), "adamw_small")
    ns_ = sink_a.shape[1]
    unpack = lambda a: (a[0:1], a[3:4, :ns_], a[2:3], a[1:2])
    d_pre, d_sink, d_mem, d_post = unpack(d_s)
    nm_pre, nm_sink, nm_mem, nm_post = unpack(nm_s)
    nv_pre, nv_sink, nv_mem, nv_post = unpack(nv_s)
    lead = lambda a: a[None]
    return (loss, lead(grad_x),
            g_pre, lead(g_in), g_sink, g_mem, lead(g_mkv), lead(g_out), g_post,
            d_pre, lead(d_in), d_sink, d_mem, lead(d_mkv), lead(d_out), d_post,
            nm_pre, lead(nm_in), nm_sink, nm_mem, lead(nm_mkv), lead(nm_out), nm_post,
            nv_pre, lead(nv_in), nv_sink, nv_mem, lead(nv_mkv), lead(nv_out), nv_post)
```

```python
import numpy as np
import jax
import jax.numpy as jnp
from jax import lax
from jax.experimental import pallas as pl
from jax.experimental.pallas import tpu as pltpu

F32 = jnp.float32
BF16 = jnp.bfloat16

D_MODEL = 1024
HEAD_DIM = 64
LANES = 128
BLOCK = 128
ROW_TILE = 512
ATTN_TILE = 1024
INPUT_GRAD_TILES = 16
A_W, A_KV_W, B_W, C_W = 384, 128, 384, 256
N_MEM = 256
D_IN = 3072
N_CHIPS = 4
SHARD_IN = D_IN // N_CHIPS
B_CONFIGS = ((128, 1), (512, 4), (2048, 16))
B_DILS = tuple(d for _, d in B_CONFIGS)
A_WINDOW = 128
RMS_EPS = 1e-6
ROPE_THETA = 500000.0
SCALE = HEAD_DIM ** -0.5
NEG = -1e30
ADAM_LR, ADAM_B1, ADAM_B2, ADAM_EPS, ADAM_WD, ADAM_STEP = 0.001, 0.9, 0.999, 1e-08, 0.01, 10

NT = (((1,), (1,)), ((), ()))
TN = (((0,), (0,)), ((), ()))
MESH = pl.DeviceIdType.MESH

_PROJ_LAYOUT = (
    [("qa", 128 * i, True, True) for i in range(3)] + [("ka", 0, True, False), ("va", 0, False, False)]
    + [("ga", 128 * i, False, False) for i in range(3)]
    + [("qb", 128 * i, True, True) for i in range(3)] + [("kb", 128 * i, True, False) for i in range(3)]
    + [("vb", 128 * i, False, False) for i in range(3)] + [("gb", 128 * i, False, False) for i in range(3)]
    + [("qc", 128 * i, False, True) for i in range(2)] + [("gc", 128 * i, False, False) for i in range(2)]
)
_PROJ_WIDTH = dict(qa=A_W, ka=A_KV_W, va=A_KV_W, ga=A_W, qb=B_W, kb=B_W, vb=B_W, gb=B_W, qc=C_W, gc=C_W)
_NATURAL = ("qa", "ka", "va", "ga", "gb", "qc", "gc")
_DILATED = ("qb", "kb", "vb")


def _dot(a, b):
    return jnp.dot(a, b, preferred_element_type=F32)


def _dot_nt(a, b):
    return lax.dot_general(a, b, NT, preferred_element_type=F32)


def _dot_tn(a, b):
    return lax.dot_general(a, b, TN, preferred_element_type=F32)


def _half_masks(rows):
    lane = lax.broadcasted_iota(jnp.int32, (rows, LANES), 1)
    return lane < HEAD_DIM, lane >= HEAD_DIM


def _rope(t, c, sm, sp):
    return t * c + pltpu.roll(t, LANES - 8, 1) * sm + pltpu.roll(t, 8, 1) * sp


def _rope_tables(seq, tm):
    dim = np.arange(LANES) % HEAD_DIM
    inv_freq = (np.float32(ROPE_THETA) ** (-np.arange(0, 16, 2, dtype=np.float32) / np.float32(16))).astype(np.float64)
    freq = np.where(dim < 16, inv_freq[dim % 8], 0.0)[None, :]
    local = np.arange(tm, dtype=np.float64)[:, None] * freq
    base = (np.arange(seq // tm, dtype=np.float64) * tm)[:, None] * freq
    both = lambda a: np.concatenate([np.cos(a), np.sin(a)], axis=1).astype(np.float32)
    return jnp.asarray(both(local)), jnp.asarray(np.repeat(both(base), 8, axis=0))


def _rope_coeffs(local_ref, base_ref):
    cl, sl = local_ref[:, :LANES], local_ref[:, LANES:]
    cb, sb = base_ref[0:1, :LANES], base_ref[0:1, LANES:]
    cos = cb * cl - sb * sl
    sin = sb * cl + cb * sl
    dim = lax.broadcasted_iota(jnp.int32, (1, LANES), 1) % HEAD_DIM
    return cos, jnp.where(dim < 8, -sin, 0.0), jnp.where((dim >= 8) & (dim < 16), sin, 0.0)


def _split3(x):
    a = x.astype(BF16)
    r = x - a.astype(F32)
    b = r.astype(BF16)
    c = (r - b.astype(F32)).astype(BF16)
    return a, b, c


def _rows_to_lanes(x):
    row = lax.broadcasted_iota(jnp.int32, (8, LANES), 0)
    lane = lax.broadcasted_iota(jnp.int32, (8, LANES), 1)
    eye = (row == lane).astype(BF16)
    a, b, c = _split3(x)
    return _dot_nt(eye, a) + _dot_nt(eye, b) + _dot_nt(eye, c)


def _head_sum_matrix(width):
    k = lax.broadcasted_iota(jnp.int32, (width, LANES), 0)
    h = lax.broadcasted_iota(jnp.int32, (width, LANES), 1)
    return (k // HEAD_DIM == h).astype(BF16)


def _head_expand_matrix(width):
    h = lax.broadcasted_iota(jnp.int32, (LANES, width), 0)
    k = lax.broadcasted_iota(jnp.int32, (LANES, width), 1)
    return (k // HEAD_DIM == h).astype(BF16)


def _dot_split(x, mat, terms):
    parts = _split3(x)[:terms]
    out = _dot(parts[0], mat)
    for p in parts[1:]:
        out = out + _dot(p, mat)
    return out


def _per_head(cols, fill=0.0):
    rows = cols[0].shape[0]
    lane = lax.broadcasted_iota(jnp.int32, (rows, LANES), 1)
    out = jnp.full((rows, LANES), fill, F32)
    for h, col in enumerate(cols):
        out = jnp.where(lane == h, col, out)
    return out


def _lane_blocks(width):
    return [slice(p * LANES, (p + 1) * LANES) for p in range(width // LANES)]


def _stage(rows, width):
    return pltpu.VMEM((width // LANES, rows, LANES), F32)


def _stage_write(buf, value):
    for p, lanes in enumerate(_lane_blocks(value.shape[1])):
        buf[p] = value[:, lanes]


def _stage_read(buf):
    return jnp.concatenate([buf[p] for p in range(buf.shape[0])], axis=1) if buf.shape[0] > 1 else buf[0]


def _to_residues(buf, out_ref, dil):
    rows = buf.shape[1] // dil
    for r in range(dil):
        for p in range(buf.shape[0]):
            plane = buf.at[p]
            out_ref[r, :, p * LANES:(p + 1) * LANES] = plane[pl.ds(r, rows, stride=dil), :].astype(out_ref.dtype)


def _from_residues(in_ref, buf, dil):
    rows = buf.shape[1] // dil
    for r in range(dil):
        for p in range(buf.shape[0]):
            plane = buf.at[p]
            plane[pl.ds(r, rows, stride=dil), :] = in_ref[r, :, p * LANES:(p + 1) * LANES].astype(F32)


def _residue_spec(dil, tm, width):
    return pl.BlockSpec((dil, tm // dil, width), lambda i: (0, i, 0))


def _gather_exchange(shards_2d):
    shards = tuple(s.reshape(2, s.shape[0] // 2, s.shape[1]) for s in shards_2d)
    n = len(shards)

    def copies(in_refs, out_refs, send_sems, recv_sems):
        srcs, outs = in_refs[:n], out_refs
        x, y, c = lax.axis_index("x"), lax.axis_index("y"), lax.axis_index("c")
        my_chip = 2 * x + y
        sibling = (x, y, 1 - c)
        chips = [(1 - x, y), (x, 1 - y), (1 - x, 1 - y)]

        def copy(k, src, dst, to):
            return pltpu.make_async_remote_copy(src_ref=src, dst_ref=dst, send_sem=send_sems.at[k],
                                                recv_sem=recv_sems.at[k], device_id=to, device_id_type=MESH)

        first, arrive, passed, sibling_arrive = [], [], [], []
        for j, (cx, cy) in enumerate(chips):
            chip = 2 * cx + cy
            for t in range(n):
                k = n * j + t
                first.append(copy(k, srcs[t].at[c], outs[t].at[my_chip, c], (cx, cy, c)))
                arrive.append(copy(k, srcs[t].at[c], outs[t].at[chip, c], (cx, cy, c)))
                passed.append(copy(n * 3 + k, outs[t].at[chip, c], outs[t].at[chip, c], sibling))
                sibling_arrive.append(copy(n * 3 + k, outs[t].at[chip, 1 - c], outs[t].at[chip, 1 - c], sibling))
        return first, arrive, passed, sibling_arrive

    def start(*refs):
        for cp in copies(*refs)[0]:
            cp.start()

    def finish(*refs):
        first, arrive, passed, sibling_arrive = copies(*refs)
        for got, fwd in zip(arrive, passed):
            got.wait_recv()
            fwd.start()
        for cp in sibling_arrive:
            cp.wait_recv()
        for cp in first + passed:
            cp.wait_send()

    my_chip = 2 * lax.axis_index("x") + lax.axis_index("y")
    landing = [lax.dynamic_update_slice(jnp.zeros((N_CHIPS,) + s.shape, s.dtype), s[None], (my_chip, 0, 0, 0))
               for s in shards]
    return dict(ins=list(shards) + landing, start=start, finish=finish, aliases={n + t: t for t in range(n)},
                outs=[jax.ShapeDtypeStruct((N_CHIPS,) + s.shape, s.dtype) for s in shards],
                sems=[pltpu.SemaphoreType.DMA((6 * n,)), pltpu.SemaphoreType.DMA((6 * n,))])


def _mem_kv(mem, mem_norm, w_mkv):
    def body(mem_ref, g_ref, w_ref, mk_ref, mv_ref):
        m = mem_ref[...]
        r = lax.rsqrt(jnp.mean(m * m, axis=-1, keepdims=True) + RMS_EPS)
        mn = (m * r * g_ref[...]).astype(BF16)
        kv = _dot(mn, w_ref[...])
        mk_ref[...] = kv[:, :C_W].astype(BF16)
        mv_ref[...] = kv[:, C_W:].astype(BF16)

    return pl.pallas_call(
        body, name="mem_kv",
        out_shape=[jax.ShapeDtypeStruct((N_MEM, C_W), BF16)] * 2,
    )(mem, mem_norm, w_mkv)


def _mem_kv_bwd(mem, mem_norm, w_mkv, dmk, dmv):
    def body(mem_ref, g_ref, w_ref, dmk_ref, dmv_ref, gw_ref, gn_ref):
        m = mem_ref[...]
        r = lax.rsqrt(jnp.mean(m * m, axis=-1, keepdims=True) + RMS_EPS)
        mhat = m * r
        mn = (mhat * g_ref[...]).astype(BF16)
        dkv = jnp.concatenate([dmk_ref[...], dmv_ref[...]], axis=1).astype(BF16)
        gw_ref[...] = _dot_tn(mn, dkv)
        dmn = _dot_nt(dkv, w_ref[...])
        gn_ref[...] = jnp.sum(dmn * mhat, axis=0, keepdims=True)

    return pl.pallas_call(
        body, name="mem_kv_bwd",
        out_shape=[jax.ShapeDtypeStruct((D_MODEL, 2 * C_W), F32), jax.ShapeDtypeStruct((1, D_MODEL), F32)],
    )(mem, mem_norm, w_mkv, dmk, dmv)


def _pre_norm(x, pre_norm, w_own=None, host=None):
    seq = x.shape[0]
    tm = min(ROW_TILE, seq)
    n_own_in = 2 if w_own is None else 3
    n_own_out = n_own_in
    n_host_in = len(host["ins"]) if host else 0
    n_host_out = len(host["outs"]) if host else 0

    def body(x_ref, g_ref, *refs):
        w_ref = None if w_own is None else refs[0]
        refs = refs[n_own_in - 2:]
        host_in, own_out, refs = refs[:n_host_in], refs[n_host_in:n_host_in + n_own_out], refs[n_host_in + n_own_out:]
        host_out, sems = refs[:n_host_out], refs[n_host_out:]
        if host:
            @pl.when(pl.program_id(0) == 0)
            def _():
                host["start"](host_in, host_out, *sems)

        xv = x_ref[...]
        r = lax.rsqrt(jnp.mean(xv * xv, axis=-1, keepdims=True) + RMS_EPS)
        u = xv * r * g_ref[...]
        ub = u.astype(BF16)
        own_out[0][...] = ub
        own_out[1][...] = u.T.astype(BF16)
        if w_own is not None:
            own_out[2][...] = _dot(ub, w_ref[...])
        if host:
            @pl.when(pl.program_id(0) == seq // tm - 1)
            def _():
                host["finish"](host_in, host_out, *sems)

    any_spec = pl.BlockSpec(memory_space=pl.ANY)
    ins = [x, pre_norm]
    in_specs = [pl.BlockSpec((tm, D_MODEL), lambda i: (i, 0)), pl.BlockSpec(pre_norm.shape, lambda i: (0, 0))]
    out_shape = [jax.ShapeDtypeStruct((seq, D_MODEL), BF16), jax.ShapeDtypeStruct((D_MODEL, seq), BF16)]
    out_specs = [pl.BlockSpec((tm, D_MODEL), lambda i: (i, 0)), pl.BlockSpec((D_MODEL, tm), lambda i: (0, i))]
    if w_own is not None:
        ins.append(w_own)
        in_specs.append(pl.BlockSpec(w_own.shape, lambda i: (0, 0)))
        out_shape.append(jax.ShapeDtypeStruct((seq, w_own.shape[1]), F32))
        out_specs.append(pl.BlockSpec((tm, w_own.shape[1]), lambda i: (i, 0)))
    aliases, scratch = {}, []
    if host:
        aliases = {len(ins) + k: n_own_out + v for k, v in host.get("aliases", {}).items()}
        ins += list(host["ins"])
        in_specs += [any_spec] * n_host_in
        out_shape += list(host["outs"])
        out_specs += [any_spec] * n_host_out
        scratch = list(host["sems"])
    res = pl.pallas_call(
        body, name="pre_norm", grid=(seq // tm,), in_specs=in_specs, out_specs=out_specs, out_shape=out_shape,
        input_output_aliases=aliases, scratch_shapes=scratch,
        compiler_params=pltpu.CompilerParams(dimension_semantics=("arbitrary",)),
    )(*ins)
    return res[0], res[1], (None if w_own is None else res[2]), res[n_own_out:]


def _pre_proj(u, w_in_g, own=None, host=None):
    seq = u.shape[0]
    tm = min(ROW_TILE, seq)
    n_nat, n_dil = len(_NATURAL), len(_DILATED) * len(B_DILS)
    rope = _rope_tables(seq, tm)

    n_host_in = len(host["ins"]) if host else 0
    n_host_out = len(host["outs"]) if host else 0
    n_own_out = n_nat + n_dil

    def body(u_ref, w_ref, rl_ref, rb_ref, *refs):
        if own:
            (chip_ref, pown_ref), refs = refs[:2], refs[2:]
        host_in, refs = refs[:n_host_in], refs[n_host_in:]
        nat = dict(zip(_NATURAL, refs[:n_nat]))
        res = {n: refs[n_nat + len(B_DILS) * k:n_nat + len(B_DILS) * (k + 1)] for k, n in enumerate(_DILATED)}
        host_out = refs[n_own_out:n_own_out + n_host_out]
        bufs = dict(zip(_DILATED, refs[n_own_out + n_host_out:]))
        sems = refs[n_own_out + n_host_out + len(_DILATED):]
        if host:
            @pl.when(pl.program_id(0) == 0)
            def _():
                host["start"](host_in, host_out, *sems)

        def project(own_chip):
            ub = u_ref[...]
            c, sm, sp = _rope_coeffs(rl_ref, rb_ref)
            for j in range(N_CHIPS):
                pj = pown_ref[...] if j == own_chip else _dot(ub, w_ref[j])
                for b in range(SHARD_IN // LANES):
                    name, off, roped, scaled = _PROJ_LAYOUT[(SHARD_IN // LANES) * j + b]
                    piece = pj[:, LANES * b:LANES * (b + 1)]
                    if roped:
                        piece = _rope(piece, c, sm, sp)
                    if scaled:
                        piece = piece * SCALE
                    if name in bufs:
                        bufs[name][off // LANES] = piece
                    else:
                        nat[name][:, off:off + LANES] = piece.astype(BF16)
            for name in _DILATED:
                for ref, dil in zip(res[name], B_DILS):
                    _to_residues(bufs[name], ref, dil)

        if own:
            for chip in range(N_CHIPS):
                pl.when(chip_ref[0] == chip)(lambda chip=chip: project(chip))
        else:
            project(None)
        if host:
            @pl.when(pl.program_id(0) == seq // tm - 1)
            def _():
                host["finish"](host_in, host_out, *sems)

    row = lambda w: pl.BlockSpec((tm, w), lambda i: (i, 0))
    full = lambda a: pl.BlockSpec(a.shape, lambda i: (0,) * a.ndim)
    any_spec = pl.BlockSpec(memory_space=pl.ANY)
    out_shape = [jax.ShapeDtypeStruct((seq, _PROJ_WIDTH[n]), BF16) for n in _NATURAL]
    out_specs = [row(_PROJ_WIDTH[n]) for n in _NATURAL]
    for n in _DILATED:
        for dil in B_DILS:
            out_shape.append(jax.ShapeDtypeStruct((dil, seq // dil, B_W), BF16))
            out_specs.append(_residue_spec(dil, tm, B_W))
    ins = [u, w_in_g, *rope]
    in_specs = [row(D_MODEL), full(w_in_g), full(rope[0]), pl.BlockSpec((8, 2 * LANES), lambda i: (i, 0))]
    if own:
        ins += list(own)
        in_specs += [pl.BlockSpec(memory_space=pltpu.SMEM), row(SHARD_IN)]
    scratch = [_stage(tm, B_W)] * len(_DILATED)
    aliases = {}
    if host:
        aliases = {len(ins) + k: n_own_out + v for k, v in host.get("aliases", {}).items()}
        ins += list(host["ins"])
        in_specs += [any_spec] * n_host_in
        out_shape += list(host["outs"])
        out_specs += [any_spec] * n_host_out
        scratch += list(host["sems"])
    res = pl.pallas_call(
        body, name="pre_proj", grid=(seq // tm,), in_specs=in_specs, out_specs=out_specs, out_shape=out_shape,
        input_output_aliases=aliases, scratch_shapes=scratch,
        compiler_params=pltpu.CompilerParams(dimension_semantics=("arbitrary",)),
    )(*ins)
    out = dict(zip(_NATURAL, res[:n_nat]))
    for k, n in enumerate(_DILATED):
        out[n] = res[n_nat + len(B_DILS) * k:n_nat + len(B_DILS) * (k + 1)]
    out["hosted"] = res[n_own_out:]
    return out


def _band_bias(max_dist, transposed):
    i = np.arange(BLOCK)[:, None]
    j = np.arange(BLOCK)[None, :]
    if transposed:
        same = i <= j
        other = (j + BLOCK - i) <= max_dist
        vis = np.concatenate([same, other], axis=1)
    else:
        prev = (i + BLOCK - j) <= max_dist
        same = j <= i
        vis = np.concatenate([prev, same], axis=1)
    return jnp.asarray(np.where(vis, 0.0, NEG).astype(np.float32))


def _kv_place(h, gqa):
    return (0, h // 3) if gqa else (h // 2, h % 2)


def _band_fwd(q, k, v, sink, *, max_dist, name):
    dil, length, wq = q.shape
    wk = k.shape[2]
    gqa = wk != wq
    tq = min(ATTN_TILE, length)
    ns, nt = tq // BLOCK, length // tq
    npair = wq // LANES
    bias = _band_bias(max_dist, transposed=False)
    has_sink = sink is not None

    def body(*refs):
        if has_sink:
            sink_ref, refs = refs[0], refs[1:]
        q_ref, k_ref, kp_ref, v_ref, vp_ref, bias_ref, o_ref, lse_ref, kbuf, vbuf = refs[:10]
        i = pl.program_id(1)
        kbuf[0:BLOCK] = kp_ref[...]
        kbuf[BLOCK:] = k_ref[...]
        vbuf[0:BLOCK] = vp_ref[...]
        vbuf[BLOCK:] = v_ref[...]
        if gqa:
            kroll, vroll = refs[10:12]
            kroll[...] = pltpu.roll(kbuf[...], HEAD_DIM, 1)
            vroll[...] = pltpu.roll(vbuf[...], HEAD_DIM, 1)
        half = _half_masks(BLOCK)
        col_prev = (lax.broadcasted_iota(jnp.int32, (1, 2 * BLOCK), 1) < BLOCK).astype(F32)

        def score_matmuls(a):
            scores = []
            for p in range(npair):
                qp = q_ref[a * BLOCK:(a + 1) * BLOCK, p * LANES:(p + 1) * LANES]
                for e in range(2):
                    pk, ek = _kv_place(2 * p + e, gqa)
                    kw = (kbuf if ek == e else kroll)[a * BLOCK:(a + 2) * BLOCK, pk * LANES:(pk + 1) * LANES]
                    scores.append(_dot_nt(jnp.where(half[e], qp, jnp.zeros_like(qp)), kw))
            return scores

        pending = score_matmuls(0)
        for a in range(ns):
            r0 = a * BLOCK
            b = bias_ref[...]
            if a == 0:
                b = b + jnp.where(i == 0, NEG, 0.0) * col_prev
            scores = pending
            m_cols, l_cols, probs = [], [], []
            for h, s in enumerate(scores):
                s = s + b
                m = jnp.max(s, axis=1, keepdims=True)
                if has_sink:
                    m = jnp.maximum(m, sink_ref[h])
                pe = jnp.exp(s - m)
                l = jnp.sum(pe, axis=1, keepdims=True)
                if has_sink:
                    l = l + jnp.exp(sink_ref[h] - m)
                probs.append(pe.astype(BF16))
                m_cols.append(m)
                l_cols.append(l)
            pending = score_matmuls(a + 1) if a + 1 < ns else None
            for p in range(npair):
                o_h = []
                for e in range(2):
                    h = 2 * p + e
                    pk, ek = _kv_place(h, gqa)
                    vw = (vbuf if ek == e else vroll)[r0:r0 + 2 * BLOCK, pk * LANES:(pk + 1) * LANES]
                    o_h.append(_dot(probs[h], vw) * (1.0 / l_cols[h]))
                o_ref[r0:r0 + BLOCK, p * LANES:(p + 1) * LANES] = jnp.where(half[0], o_h[0], o_h[1]).astype(BF16)
            lse_ref[r0:r0 + BLOCK, :] = _per_head(m_cols) + jnp.log(_per_head(l_cols, 1.0))

    main = lambda w: pl.BlockSpec((None, tq, w), lambda r, i: (r, i, 0))
    prev = lambda w: pl.BlockSpec((None, BLOCK, w), lambda r, i: (r, jnp.maximum(i * ns - 1, 0), 0))
    in_specs = [main(wq), main(wk), prev(wk), main(wk), prev(wk), pl.BlockSpec(bias.shape, lambda r, i: (0, 0))]
    args = [q, k, k, v, v, bias]
    if has_sink:
        in_specs = [pl.BlockSpec(memory_space=pltpu.SMEM)] + in_specs
        args = [sink] + args
    scratch = [pltpu.VMEM((tq + BLOCK, wk), BF16)] * (4 if gqa else 2)
    return pl.pallas_call(
        body, name=name, grid=(dil, nt), in_specs=in_specs,
        out_specs=[main(wq), main(LANES)],
        out_shape=[jax.ShapeDtypeStruct((dil, length, wq), BF16), jax.ShapeDtypeStruct((dil, length, LANES), F32)],
        scratch_shapes=scratch,
    )(*args)


def _band_bwd(q, k, v, do, lse, delta, *, max_dist, name):
    dil, length, wq = q.shape
    wk = k.shape[2]
    gqa = wk != wq
    tq = min(ATTN_TILE, length)
    ns, nt = tq // BLOCK, length // tq
    npair = wq // LANES
    nblocks = length // BLOCK
    bias = _band_bias(max_dist, transposed=True)

    def body(q_ref, qn_ref, do_ref, don_ref, lse_ref, lsen_ref, dl_ref, dln_ref, k_ref, v_ref, bias_ref,
             dq_ref, dk_ref, dv_ref, stat_l, stat_d, dqt, kt, *rolled):
        i = pl.program_id(1)
        for pk in range(wk // LANES):
            kt[pk] = k_ref[:, pk * LANES:(pk + 1) * LANES].astype(F32).T.astype(BF16)
        if gqa:
            kroll, vroll, ktroll = rolled
            kroll[...] = pltpu.roll(k_ref[...], HEAD_DIM, 1)
            vroll[...] = pltpu.roll(v_ref[...], HEAD_DIM, 1)
            ktroll[0] = kroll[...].astype(F32).T.astype(BF16)
        for a in range(ns):
            rows = slice(a * BLOCK, (a + 1) * BLOCK)
            stat_l[a] = _rows_to_lanes(lse_ref[rows, :])
            stat_d[a] = _rows_to_lanes(dl_ref[rows, :])
        stat_l[ns] = _rows_to_lanes(lsen_ref[...])
        stat_d[ns] = _rows_to_lanes(dln_ref[...])

        @pl.when(i == 0)
        def _():
            dqt[:, :, 0:BLOCK] = jnp.zeros((npair, LANES, BLOCK), F32)

        @pl.when(i > 0)
        def _():
            dqt[:, :, 0:BLOCK] = dqt[:, :, tq:tq + BLOCK]

        dqt[:, :, BLOCK:] = jnp.zeros((npair, LANES, tq), F32)
        half2 = _half_masks(2 * BLOCK)
        row = lax.broadcasted_iota(jnp.int32, (LANES, BLOCK), 0)
        row_half = (row < HEAD_DIM, row >= HEAD_DIM)
        col_next = (lax.broadcasted_iota(jnp.int32, (1, 2 * BLOCK), 1) >= BLOCK).astype(F32)

        def scores(b):
            rows = slice(b * BLOCK, (b + 1) * BLOCK)
            nxt_rows = slice((b + 1) * BLOCK, (b + 2) * BLOCK)
            items = []
            for p in range(npair):
                lanes = slice(p * LANES, (p + 1) * LANES)
                q_next = q_ref[nxt_rows, lanes] if b + 1 < ns else qn_ref[:, lanes]
                do_next = do_ref[nxt_rows, lanes] if b + 1 < ns else don_ref[:, lanes]
                qw = jnp.concatenate([q_ref[rows, lanes], q_next], axis=0)
                dow = jnp.concatenate([do_ref[rows, lanes], do_next], axis=0)
                for e in range(2):
                    h = 2 * p + e
                    pk, ek = _kv_place(h, gqa)
                    klanes = slice(pk * LANES, (pk + 1) * LANES)
                    kb = (k_ref if ek == e else kroll)[rows, klanes]
                    vb = (v_ref if ek == e else vroll)[rows, klanes]
                    qm = jnp.where(half2[e], qw, jnp.zeros_like(qw))
                    dom = jnp.where(half2[e], dow, jnp.zeros_like(dow))
                    items.append(dict(p=p, e=e, h=h, pk=pk, ek=ek, qm=qm, dom=dom,
                                      st=_dot_nt(kb, qm), dpt=_dot_nt(vb, dom)))
            return items

        def probs(b, items):
            bt = bias_ref[...]
            if b == ns - 1:
                bt = bt + jnp.where(i == nt - 1, NEG, 0.0) * col_next
            for it in items:
                h = it["h"]
                lrow = jnp.concatenate([stat_l[b, h:h + 1, :], stat_l[b + 1, h:h + 1, :]], axis=1)
                drow = jnp.concatenate([stat_d[b, h:h + 1, :], stat_d[b + 1, h:h + 1, :]], axis=1)
                pt = jnp.exp(it["st"] + bt - lrow)
                it["ptb"] = pt.astype(BF16)
                it["dsb"] = (pt * (it["dpt"] - drow)).astype(BF16)

        pending = scores(0)
        for b in range(ns):
            rows = slice(b * BLOCK, (b + 1) * BLOCK)
            window = slice(b * BLOCK, (b + 2) * BLOCK)
            acc = {}
            items = pending
            probs(b, items)
            pending = scores(b + 1) if b + 1 < ns else None
            for p in range(npair):
                pair = items[2 * p:2 * p + 2]
                lanes = slice(p * LANES, (p + 1) * LANES)
                kparts = []
                for it in pair:
                    kbt = (kt if it["ek"] == it["e"] else ktroll)[it["pk"], :, rows]
                    kparts.append(jnp.where(row_half[it["e"]], kbt, jnp.zeros_like(kbt)))
                ds_keys = jnp.concatenate([it["dsb"] for it in pair], axis=0)
                dqt[p, :, window] += _dot(jnp.concatenate(kparts, axis=1), ds_keys)
                if not gqa:
                    q_both = jnp.concatenate([it["qm"] for it in pair], axis=0)
                    do_both = jnp.concatenate([it["dom"] for it in pair], axis=0)
                    dk_ref[rows, lanes] = _dot(jnp.concatenate([it["dsb"] for it in pair], axis=1), q_both).astype(BF16)
                    dv_ref[rows, lanes] = _dot(jnp.concatenate([it["ptb"] for it in pair], axis=1), do_both).astype(BF16)
                else:
                    for it in pair:
                        dv_c = _dot(it["ptb"], it["dom"])
                        dk_c = _dot(it["dsb"], it["qm"])
                        key = (it["pk"], it["ek"] == it["e"])
                        if key in acc:
                            acc[key] = (acc[key][0] + dk_c, acc[key][1] + dv_c)
                        else:
                            acc[key] = (dk_c, dv_c)
            if gqa:
                dk_al, dv_al = acc[(0, True)]
                dk_mis, dv_mis = acc[(0, False)]
                dk_ref[rows, :] = (dk_al + pltpu.roll(dk_mis, HEAD_DIM, 1)).astype(BF16)
                dv_ref[rows, :] = (dv_al + pltpu.roll(dv_mis, HEAD_DIM, 1)).astype(BF16)

        for p in range(npair):
            dq_ref[:, p * LANES:(p + 1) * LANES] = dqt[p, :, 0:tq].T.astype(BF16)

    main = lambda w: pl.BlockSpec((None, tq, w), lambda r, i: (r, i, 0))
    nxt = lambda w: pl.BlockSpec((None, BLOCK, w), lambda r, i: (r, jnp.minimum((i + 1) * ns, nblocks - 1), 0))
    scratch = [pltpu.VMEM((ns + 1, 8, LANES), F32), pltpu.VMEM((ns + 1, 8, LANES), F32),
               pltpu.VMEM((npair, LANES, tq + BLOCK), F32), pltpu.VMEM((wk // LANES, LANES, tq), BF16)]
    if gqa:
        scratch = scratch + [pltpu.VMEM((tq, wk), BF16)] * 2 + [pltpu.VMEM((1, LANES, tq), BF16)]
    return pl.pallas_call(
        body, name=name, grid=(dil, nt),
        in_specs=[main(wq), nxt(wq), main(wq), nxt(wq), main(LANES), nxt(LANES), main(LANES), nxt(LANES),
                  main(wk), main(wk), pl.BlockSpec(bias.shape, lambda r, i: (0, 0))],
        out_specs=[main(wq), main(wk), main(wk)],
        out_shape=[jax.ShapeDtypeStruct((dil, length, wq), BF16), jax.ShapeDtypeStruct((dil, length, wk), BF16),
                   jax.ShapeDtypeStruct((dil, length, wk), BF16)],
        scratch_shapes=scratch,
        compiler_params=pltpu.CompilerParams(dimension_semantics=("arbitrary", "arbitrary")),
    )(q, q, do, do, lse, lse, delta, delta, k, v, bias)


def _mem_attn_fwd(q, mk, mv):
    seq = q.shape[0]
    tq = min(ATTN_TILE, seq)
    sub_rows = min(4 * BLOCK, tq)
    ns = tq // sub_rows

    def body(q_ref, mk_ref, mv_ref, o_ref, lse_ref):
        half = _half_masks(sub_rows)

        def sub(a, carry):
            r0 = pl.multiple_of(a * sub_rows, sub_rows)
            scores = []
            for p in range(C_W // LANES):
                lanes = slice(p * LANES, (p + 1) * LANES)
                qp = q_ref[pl.ds(r0, sub_rows), lanes]
                for e in range(2):
                    scores.append(_dot_nt(jnp.where(half[e], qp, jnp.zeros_like(qp)), mk_ref[:, lanes]))
            m_cols, l_cols, probs = [], [], []
            for s in scores:
                m = jnp.max(s, axis=1, keepdims=True)
                pe = jnp.exp(s - m)
                probs.append(pe.astype(BF16))
                m_cols.append(m)
                l_cols.append(jnp.sum(pe, axis=1, keepdims=True))
            for p in range(C_W // LANES):
                lanes = slice(p * LANES, (p + 1) * LANES)
                o_h = [_dot(probs[2 * p + e], mv_ref[:, lanes]) * (1.0 / l_cols[2 * p + e]) for e in range(2)]
                o_ref[pl.ds(r0, sub_rows), lanes] = jnp.where(half[0], o_h[0], o_h[1]).astype(BF16)
            lse_ref[pl.ds(r0, sub_rows), :] = _per_head(m_cols) + jnp.log(_per_head(l_cols, 1.0))
            return carry

        lax.fori_loop(0, ns, sub, 0, unroll=True)

    row = lambda w: pl.BlockSpec((tq, w), lambda i: (i, 0))
    full = pl.BlockSpec((N_MEM, C_W), lambda i: (0, 0))
    return pl.pallas_call(
        body, name="mem_attn_fwd", grid=(seq // tq,), in_specs=[row(C_W), full, full],
        out_specs=[row(C_W), row(LANES)],
        out_shape=[jax.ShapeDtypeStruct((seq, C_W), BF16), jax.ShapeDtypeStruct((seq, LANES), F32)],
    )(q, mk, mv)


def _mem_attn_bwd(q, mk, mv, do, lse, delta):
    seq = q.shape[0]
    tq = min(ATTN_TILE, seq)
    ns = tq // BLOCK
    npair = C_W // LANES

    def body(q_ref, mk_ref, mv_ref, do_ref, lse_ref, dl_ref, dq_ref, dmk_ref, dmv_ref, stat_l, stat_d, mkt, dqt):
        @pl.when(pl.program_id(0) == 0)
        def _():
            dmk_ref[...] = jnp.zeros_like(dmk_ref)
            dmv_ref[...] = jnp.zeros_like(dmv_ref)
            for p in range(npair):
                mkt[p] = mk_ref[:, p * LANES:(p + 1) * LANES].astype(F32).T.astype(BF16)

        for a in range(ns):
            rows = slice(a * BLOCK, (a + 1) * BLOCK)
            stat_l[a] = _rows_to_lanes(lse_ref[rows, :])
            stat_d[a] = _rows_to_lanes(dl_ref[rows, :])
        span = min(2, ns)
        half = _half_masks(span * BLOCK)
        row = lax.broadcasted_iota(jnp.int32, (LANES, N_MEM), 0)
        row_half = (row < HEAD_DIM, row >= HEAD_DIM)

        for a in range(0, ns, span):
            rows = slice(a * BLOCK, (a + span) * BLOCK)
            items = []
            for p in range(npair):
                lanes = slice(p * LANES, (p + 1) * LANES)
                qp = q_ref[rows, lanes]
                dop = do_ref[rows, lanes]
                for e in range(2):
                    qm = jnp.where(half[e], qp, jnp.zeros_like(qp))
                    dom = jnp.where(half[e], dop, jnp.zeros_like(dop))
                    items.append(dict(p=p, e=e, qm=qm, dom=dom, st=_dot_nt(mk_ref[:, lanes], qm),
                                      dpt=_dot_nt(mv_ref[:, lanes], dom)))
            for it in items:
                h = 2 * it["p"] + it["e"]
                lrow = jnp.concatenate([stat_l[a + k, h:h + 1, :] for k in range(span)], axis=1)
                drow = jnp.concatenate([stat_d[a + k, h:h + 1, :] for k in range(span)], axis=1)
                pt = jnp.exp(it["st"] - lrow)
                it["ptb"] = pt.astype(BF16)
                it["dsb"] = (pt * (it["dpt"] - drow)).astype(BF16)
            for p in range(npair):
                lanes = slice(p * LANES, (p + 1) * LANES)
                pair = [it for it in items if it["p"] == p]
                join = lambda name, axis: jnp.concatenate([it[name] for it in pair], axis=axis)
                dmv_ref[:, lanes] += _dot(join("ptb", 1), join("dom", 0))
                dmk_ref[:, lanes] += _dot(join("dsb", 1), join("qm", 0))
                kbt = mkt[p]
                k_both = jnp.concatenate([jnp.where(row_half[e], kbt, jnp.zeros_like(kbt)) for e in range(2)], axis=1)
                dqt[p, :, rows] = _dot(k_both, join("dsb", 0))
        for p in range(npair):
            dq_ref[:, p * LANES:(p + 1) * LANES] = dqt[p].T.astype(BF16)

    row = lambda w: pl.BlockSpec((tq, w), lambda i: (i, 0))
    full = pl.BlockSpec((N_MEM, C_W), lambda i: (0, 0))
    return pl.pallas_call(
        body, name="mem_attn_bwd", grid=(seq // tq,),
        in_specs=[row(C_W), full, full, row(C_W), row(LANES), row(LANES)], out_specs=[row(C_W), full, full],
        out_shape=[jax.ShapeDtypeStruct((seq, C_W), BF16), jax.ShapeDtypeStruct((N_MEM, C_W), F32),
                   jax.ShapeDtypeStruct((N_MEM, C_W), F32)],
        scratch_shapes=[pltpu.VMEM((ns, 8, LANES), F32)] * 2
        + [pltpu.VMEM((npair, LANES, N_MEM), BF16), pltpu.VMEM((npair, LANES, tq), F32)],
        compiler_params=pltpu.CompilerParams(dimension_semantics=("arbitrary",)),
    )(q, mk, mv, do, lse, delta)


def _silu_and_grad(g):
    s = 1.0 / (1.0 + jnp.exp(-g))
    return g * s, s * (1.0 + g * (1.0 - s))


def _post(x, target, post_norm, w_out, sink_row, oa, lse_a, ga, ob_list, lseb_list, gb, oc, gc):
    seq = x.shape[0]
    tm = min(ROW_TILE, seq)
    inv_d = 1.0 / D_MODEL
    nd = len(B_DILS)

    def body(*refs):
        (x_ref, t_ref, gp_ref, w_ref, sink_ref, oa_ref, lsea_ref, ga_ref), refs = refs[:8], refs[8:]
        ob_refs, lb_refs, (gb_ref, oc_ref, gc_ref), refs = refs[:nd], refs[nd:2 * nd], refs[2 * nd:2 * nd + 3], refs[2 * nd + 3:]
        (g_ref, doa_ref, dla_ref, dga_ref), refs = refs[:4], refs[4:]
        dob_refs, lsec_refs, dlb_refs, refs = refs[:nd], refs[nd:2 * nd], refs[2 * nd:3 * nd], refs[3 * nd:]
        (dgb_ref, doc_ref, dlc_ref, dgc_ref, gw_ref, gpost_ref, gsink_ref, loss_ref), refs = refs[:8], refs[8:]
        ycat, obufs, lbufs, st_do, st_l, st_d = refs[0], refs[1:nd], refs[nd:2 * nd - 1], refs[2 * nd - 1], refs[2 * nd], refs[2 * nd + 1]

        @pl.when(pl.program_id(0) == 0)
        def _():
            gw_ref[...] = jnp.zeros_like(gw_ref)
            gpost_ref[...] = jnp.zeros_like(gpost_ref)
            gsink_ref[...] = jnp.zeros_like(gsink_ref)
            loss_ref[...] = jnp.zeros_like(loss_ref)

        o_i, l_i = [ob_refs[0][0].astype(F32)], [lb_refs[0][0]]
        for k in range(1, nd):
            _from_residues(ob_refs[k], obufs[k - 1], B_DILS[k])
            _from_residues(lb_refs[k], lbufs[k - 1], B_DILS[k])
            o_i.append(_stage_read(obufs[k - 1]))
            l_i.append(_stage_read(lbufs[k - 1]))
        mx = l_i[0]
        for l in l_i[1:]:
            mx = jnp.maximum(mx, l)
        w_i = [jnp.exp(l - mx) for l in l_i]
        z = w_i[0]
        for w in w_i[1:]:
            z = z + w
        _stage_write(st_l, mx + jnp.log(z))
        expand = _head_expand_matrix(B_W)
        inv_z = 1.0 / z
        ob = None
        for w, o in zip(w_i, o_i):
            term = _dot_split(w * inv_z, expand, 2) * o
            ob = term if ob is None else ob + term
        oa, oc = oa_ref[...].astype(F32), oc_ref[...].astype(F32)
        sa, dsa = _silu_and_grad(ga_ref[...].astype(F32))
        sb, dsb = _silu_and_grad(gb_ref[...].astype(F32))
        sc, dsc = _silu_and_grad(gc_ref[...].astype(F32))
        ycat[:, 0:A_W] = (oa * sa).astype(BF16)
        ycat[:, A_W:A_W + B_W] = (ob * sb).astype(BF16)
        ycat[:, A_W + B_W:] = (oc * sc).astype(BF16)
        y2 = _dot(ycat[...], w_ref[...])
        r = lax.rsqrt(jnp.mean(y2 * y2, axis=-1, keepdims=True) + RMS_EPS)
        zhat = y2 * r
        gp = gp_ref[...]
        err = x_ref[...] + zhat * gp - t_ref[...]
        loss_ref[...] += jnp.sum(err * err) * (0.5 * inv_d)
        g = err * inv_d
        g_ref[...] = g
        gpost_ref[...] += jnp.sum(g * zhat, axis=0, keepdims=True)
        a = g * gp
        dy2 = (r * (a - zhat * jnp.mean(a * zhat, axis=-1, keepdims=True))).astype(BF16)
        gw_ref[...] += _dot_tn(ycat[...], dy2)
        dycat = _dot_nt(dy2, w_ref[...])
        dya, dyb, dyc = dycat[:, 0:A_W], dycat[:, A_W:A_W + B_W], dycat[:, A_W + B_W:]
        doa, dob, doc = dya * sa, dyb * sb, dyc * sc
        doa_ref[...] = doa.astype(BF16)
        doc_ref[...] = doc.astype(BF16)
        dga_ref[...] = (dya * oa * dsa).astype(BF16)
        dgb_ref[...] = (dyb * ob * dsb).astype(BF16)
        dgc_ref[...] = (dyc * oc * dsc).astype(BF16)
        dl_a = _dot_split(doa * oa, _head_sum_matrix(A_W), 2)
        dla_ref[...] = dl_a
        dlc_ref[...] = _dot_split(doc * oc, _head_sum_matrix(C_W), 2)
        gsink_ref[...] += jnp.sum(jnp.exp(sink_ref[...] - lsea_ref[...]) * dl_a, axis=0, keepdims=True)
        _stage_write(st_do, dob)
        _stage_write(st_d, _dot_split(dob * ob, _head_sum_matrix(B_W), 2))
        for k, dil in enumerate(B_DILS):
            _to_residues(st_do, dob_refs[k], dil)
            _to_residues(st_l, lsec_refs[k], dil)
            _to_residues(st_d, dlb_refs[k], dil)

    row = lambda w: pl.BlockSpec((tm, w), lambda i: (i, 0))
    full = lambda shape: pl.BlockSpec(shape, lambda i: (0,) * len(shape))
    res_specs = lambda w: [_residue_spec(d, tm, w) for d in B_DILS]
    res_shapes = lambda w, dt: [jax.ShapeDtypeStruct((d, seq // d, w), dt) for d in B_DILS]
    ins = [x, target, post_norm, w_out, sink_row, oa, lse_a, ga, *ob_list, *lseb_list, gb, oc, gc]
    in_specs = ([row(D_MODEL), row(D_MODEL), full((1, D_MODEL)), full((D_MODEL, D_MODEL)), full((1, LANES)),
                 row(A_W), row(LANES), row(A_W)] + res_specs(B_W) + res_specs(LANES) + [row(B_W), row(C_W), row(C_W)])
    out_shape = ([jax.ShapeDtypeStruct((seq, D_MODEL), F32), jax.ShapeDtypeStruct((seq, A_W), BF16),
                  jax.ShapeDtypeStruct((seq, LANES), F32), jax.ShapeDtypeStruct((seq, A_W), BF16)]
                 + res_shapes(B_W, BF16) + res_shapes(LANES, F32) + res_shapes(LANES, F32)
                 + [jax.ShapeDtypeStruct((seq, B_W), BF16), jax.ShapeDtypeStruct((seq, C_W), BF16),
                    jax.ShapeDtypeStruct((seq, LANES), F32), jax.ShapeDtypeStruct((seq, C_W), BF16),
                    jax.ShapeDtypeStruct((D_MODEL, D_MODEL), F32), jax.ShapeDtypeStruct((1, D_MODEL), F32),
                    jax.ShapeDtypeStruct((1, LANES), F32), jax.ShapeDtypeStruct((1, LANES), F32)])
    out_specs = ([row(D_MODEL), row(A_W), row(LANES), row(A_W)] + res_specs(B_W) + res_specs(LANES) + res_specs(LANES)
                 + [row(B_W), row(C_W), row(LANES), row(C_W),
                    full((D_MODEL, D_MODEL)), full((1, D_MODEL)), full((1, LANES)), full((1, LANES))])
    scratch = ([pltpu.VMEM((tm, D_MODEL), BF16)] + [_stage(tm, B_W)] * (nd - 1) + [_stage(tm, LANES)] * (nd - 1)
               + [_stage(tm, B_W), _stage(tm, LANES), _stage(tm, LANES)])
    res = pl.pallas_call(
        body, name="post", grid=(seq // tm,), in_specs=in_specs, out_specs=out_specs, out_shape=out_shape,
        scratch_shapes=scratch,
        compiler_params=pltpu.CompilerParams(dimension_semantics=("arbitrary",)),
    )(*ins)
    out = dict(g=res[0], doa=res[1], dl_a=res[2], dga=res[3], dob=res[4:4 + nd], lse_b=res[4 + nd:4 + 2 * nd],
               dl_b=res[4 + 2 * nd:4 + 3 * nd])
    rest = res[4 + 3 * nd:]
    out.update(dgb=rest[0], doc=rest[1], dl_c=rest[2], dgc=rest[3], gw_out=rest[4], gpost=rest[5], gsink=rest[6],
               loss=rest[7])
    return out


def _grad_w_in(ut, nat, res):
    seq = ut.shape[1]
    tm = min(ROW_TILE, seq)
    nd = len(B_DILS)
    nat_list = [nat[n] for n in _NATURAL]
    res_list = [a for n in _DILATED for a in res[n]]
    rope = _rope_tables(seq, tm)

    def body(rl_ref, rb_ref, ut_ref, *refs):
        nat_refs = dict(zip(_NATURAL, refs[:len(_NATURAL)]))
        refs = refs[len(_NATURAL):]
        res_refs = {n: refs[nd * k:nd * (k + 1)] for k, n in enumerate(_DILATED)}
        refs = refs[nd * len(_DILATED):]
        dproj_ref, gw_ref = refs[:2]
        bufs = {n: refs[2 + (nd - 1) * k:2 + (nd - 1) * (k + 1)] for k, n in enumerate(_DILATED)}

        @pl.when(pl.program_id(0) == 0)
        def _():
            gw_ref[...] = jnp.zeros_like(gw_ref)

        for n in _DILATED:
            for k in range(1, nd):
                _from_residues(res_refs[n][k], bufs[n][k - 1], B_DILS[k])
        c, sm, sp = _rope_coeffs(rl_ref, rb_ref)
        sm, sp = -sm, -sp
        for blk, (name, off, roped, scaled) in enumerate(_PROJ_LAYOUT):
            lanes = slice(off, off + LANES)
            if name in nat_refs:
                piece = nat_refs[name][:, lanes].astype(F32)
            else:
                piece = res_refs[name][0][0, :, lanes].astype(F32)
                for buf in bufs[name]:
                    piece = piece + buf[off // LANES]
            if roped:
                piece = _rope(piece, c, sm, sp)
            if scaled:
                piece = piece * SCALE
            dproj_ref[:, blk * LANES:(blk + 1) * LANES] = piece.astype(BF16)
        for j in range(N_CHIPS):
            gw_ref[j] += _dot(ut_ref[...], dproj_ref[:, j * SHARD_IN:(j + 1) * SHARD_IN])

    row = lambda w: pl.BlockSpec((tm, w), lambda i: (i, 0))
    in_specs = ([pl.BlockSpec(rope[0].shape, lambda i: (0, 0)), pl.BlockSpec((8, 2 * LANES), lambda i: (i, 0)),
                 pl.BlockSpec((D_MODEL, tm), lambda i: (0, i))]
                + [row(a.shape[1]) for a in nat_list]
                + [_residue_spec(d, tm, B_W) for _ in _DILATED for d in B_DILS])
    return pl.pallas_call(
        body, name="grad_w_in", grid=(seq // tm,), in_specs=in_specs,
        out_specs=[row(D_IN), pl.BlockSpec((N_CHIPS, D_MODEL, SHARD_IN), lambda i: (0, 0, 0))],
        out_shape=[jax.ShapeDtypeStruct((seq, D_IN), BF16), jax.ShapeDtypeStruct((N_CHIPS, D_MODEL, SHARD_IN), F32)],
        scratch_shapes=[_stage(tm, B_W)] * ((nd - 1) * len(_DILATED)),
        compiler_params=pltpu.CompilerParams(dimension_semantics=("arbitrary",)),
    )(*rope, ut, *nat_list, *res_list)


def _input_grad(x, g, pre_norm, w_in_g, dproj, gx_prev, span, after, name):
    seq = x.shape[0]
    tm = seq // INPUT_GRAD_TILES
    first_block, steps = span

    def body(*refs):
        x_ref, g_ref, gp_ref, w_ref, dp_ref = refs[:5]
        gx_ref, gpre_ref = refs[-2:]

        @pl.when(pl.program_id(0) == 0)
        def _():
            gpre_ref[...] = jnp.zeros_like(gpre_ref)

        du = None
        for j in range(N_CHIPS):
            term = _dot_nt(dp_ref[:, j * SHARD_IN:(j + 1) * SHARD_IN], w_ref[j])
            du = term if du is None else du + term
        xv = x_ref[...]
        r = lax.rsqrt(jnp.mean(xv * xv, axis=-1, keepdims=True) + RMS_EPS)
        xhat = xv * r
        gpre_ref[...] += jnp.sum(du * xhat, axis=0, keepdims=True)
        a = du * gp_ref[...]
        gx_ref[...] = g_ref[...] + r * (a - xhat * jnp.mean(a * xhat, axis=-1, keepdims=True))

    row = lambda w: pl.BlockSpec((tm, w), lambda i: (first_block + i, 0))
    full = lambda a: pl.BlockSpec(a.shape, lambda i: (0,) * a.ndim)
    any_spec = pl.BlockSpec(memory_space=pl.ANY)
    ins = [x, g, pre_norm, w_in_g, dproj]
    in_specs = [row(D_MODEL), row(D_MODEL), full(pre_norm), full(w_in_g), row(D_IN)]
    aliases = {}
    if gx_prev is not None:
        aliases[len(ins)] = 0
        ins.append(gx_prev)
        in_specs.append(any_spec)
    if after is not None:
        ins.append(after)
        in_specs.append(any_spec)
    return pl.pallas_call(
        body, name=name, grid=(steps,), in_specs=in_specs,
        out_specs=[row(D_MODEL), pl.BlockSpec((1, D_MODEL), lambda i: (0, 0))],
        out_shape=[jax.ShapeDtypeStruct((seq, D_MODEL), F32), jax.ShapeDtypeStruct((1, D_MODEL), F32)],
        input_output_aliases=aliases,
        compiler_params=pltpu.CompilerParams(dimension_semantics=("arbitrary",)),
    )(*ins)


def _exchange_start(ex, name):
    n_in, n_out, n_sem = len(ex["ins"]), len(ex["outs"]), len(ex["sems"])

    def body(*refs):
        in_refs, land_refs, sems = refs[:n_in], refs[n_in:n_in + n_out], refs[n_in + n_out:n_in + n_out + n_sem]
        ex["start"](in_refs, land_refs, *sems)
        token = refs[-1]
        token[...] = jnp.zeros_like(token)

    hbm = pl.BlockSpec(memory_space=pltpu.HBM)
    sem = pl.BlockSpec(memory_space=pltpu.SEMAPHORE)
    ins = [pltpu.with_memory_space_constraint(a, pltpu.HBM) for a in ex["ins"]]
    landing = [pltpu.with_memory_space_constraint(lax.empty(o.shape, o.dtype), pltpu.HBM) for o in ex["outs"]]
    res = pl.pallas_call(
        body, name=name,
        out_shape=list(ex["sems"]) + [pltpu.HBM(a.shape, a.dtype) for a in ex["ins"]]
        + [pltpu.HBM(o.shape, o.dtype) for o in ex["outs"]] + [jax.ShapeDtypeStruct((8, LANES), F32)],
        in_specs=[hbm] * (n_in + n_out),
        out_specs=[sem] * n_sem + [hbm] * (n_in + n_out) + [pl.BlockSpec(memory_space=pltpu.VMEM)],
        input_output_aliases={k: n_sem + k for k in range(n_in + n_out)},
        compiler_params=pltpu.CompilerParams(has_side_effects=pltpu.SideEffectType.DATAFLOW_SIDE_EFFECTING),
    )(*ins, *landing)
    return res[:-1], res[-1]


def _exchange_wait(ex, handles, after, name):
    n_in, n_out, n_sem = len(ex["ins"]), len(ex["outs"]), len(ex["sems"])
    sems, thru = handles[:n_sem], handles[n_sem:]

    def body(*refs):
        in_refs, land_refs = refs[:n_in], refs[n_in:n_in + n_out]
        sem_refs = refs[n_in + n_out:n_in + n_out + n_sem]
        ex["finish"](in_refs, land_refs, *sem_refs)

    hbm = pl.BlockSpec(memory_space=pltpu.HBM)
    sem = pl.BlockSpec(memory_space=pltpu.SEMAPHORE)
    res = pl.pallas_call(
        body, name=name,
        out_shape=[pltpu.HBM(a.shape, a.dtype) for a in thru],
        in_specs=[hbm] * (n_in + n_out) + [sem] * n_sem + [pl.BlockSpec(memory_space=pl.ANY)],
        out_specs=[hbm] * (n_in + n_out),
        input_output_aliases={k: k for k in range(n_in + n_out)},
        compiler_params=pltpu.CompilerParams(has_side_effects=pltpu.SideEffectType.DATAFLOW_SIDE_EFFECTING),
    )(*thru, *sems, after)
    return res[:n_in], res[n_in:]


def _start_finish(build):
    def start(*refs):
        for cp in build(*refs):
            cp.start()

    def finish(*refs):
        for cp in build(*refs):
            cp.wait()

    return dict(start=start, finish=finish)


def _pair_exchange(grads):
    n = len(grads)

    def build(srcs, outs, send_sems, recv_sems):
        x, y, c = lax.axis_index("x"), lax.axis_index("y"), lax.axis_index("c")
        copies = []
        for t in range(n):
            rows = grads[t].shape[1] // 2
            copies.append(pltpu.make_async_remote_copy(
                src_ref=srcs[t].at[:, pl.ds((1 - c) * rows, rows)], dst_ref=outs[t],
                send_sem=send_sems.at[t], recv_sem=recv_sems.at[t], device_id=(x, y, 1 - c), device_id_type=MESH))
        return copies

    return dict(ins=list(grads), **_start_finish(build),
                outs=[jax.ShapeDtypeStruct((g.shape[0], g.shape[1] // 2, g.shape[2]), g.dtype) for g in grads],
                sems=[pltpu.SemaphoreType.DMA((n,)), pltpu.SemaphoreType.DMA((n,))])


def _pair_add(core, own, got):
    nchip, rows2, width = own.shape
    rows = rows2 // 2
    tr = min(ROW_TILE, rows)
    nb = rows // tr

    def body(core_ref, own_ref, got_ref, out_ref):
        out_ref[...] = (own_ref[...] + got_ref[...]).astype(BF16)

    grid_spec = pltpu.PrefetchScalarGridSpec(
        num_scalar_prefetch=1, grid=(nchip, nb),
        in_specs=[pl.BlockSpec((None, tr, width), lambda k, i, core_ref: (k, core_ref[0] * nb + i, 0)),
                  pl.BlockSpec((None, tr, width), lambda k, i, core_ref: (k, i, 0))],
        out_specs=pl.BlockSpec((None, tr, width), lambda k, i, core_ref: (k, i, 0)))
    return pl.pallas_call(
        body, name=f"pair_add_{width}", grid_spec=grid_spec,
        out_shape=jax.ShapeDtypeStruct((nchip, rows, width), BF16),
    )(core, own, got)


def _chip_exchange(parts):
    n = len(parts)

    def build(srcs, outs, send_sems, recv_sems, local_sems):
        x, y, c = lax.axis_index("x"), lax.axis_index("y"), lax.axis_index("c")
        my_chip = 2 * x + y
        chips = [(1 - x, y), (x, 1 - y), (1 - x, 1 - y)]
        copies = [pltpu.make_async_copy(srcs[t].at[my_chip], outs[t].at[my_chip], local_sems.at[t]) for t in range(n)]
        for j, (cx, cy) in enumerate(chips):
            for t in range(n):
                k = n * j + t
                copies.append(pltpu.make_async_remote_copy(
                    src_ref=srcs[t].at[2 * cx + cy], dst_ref=outs[t].at[my_chip], send_sem=send_sems.at[k],
                    recv_sem=recv_sems.at[k], device_id=(cx, cy, c), device_id_type=MESH))
        return copies

    return dict(ins=list(parts), **_start_finish(build), outs=[jax.ShapeDtypeStruct(p.shape, p.dtype) for p in parts],
                sems=[pltpu.SemaphoreType.DMA((3 * n,)), pltpu.SemaphoreType.DMA((3 * n,)),
                      pltpu.SemaphoreType.DMA((n,))])


def _slot_sum(slots, name, core):
    ns, rows, width = slots.shape
    tr = min(ROW_TILE, rows)

    def body(core_ref, in_ref, out_ref):
        acc = in_ref[0].astype(F32)
        for s in range(1, ns):
            acc = acc + in_ref[s].astype(F32)
        out_ref[...] = acc

    grid_spec = pltpu.PrefetchScalarGridSpec(
        num_scalar_prefetch=1, grid=(rows // tr,),
        in_specs=[pl.BlockSpec((ns, tr, width), lambda i, core_ref: (0, i, 0))],
        out_specs=pl.BlockSpec((None, tr, width), lambda i, core_ref: (core_ref[0], i, 0)))
    return pl.pallas_call(
        body, name=name, grid_spec=grid_spec, out_shape=jax.ShapeDtypeStruct((2, rows, width), F32),
    )(core, slots)


def _pair_gather(bufs, small):
    n = len(bufs)

    def body(*refs):
        small_ref, outs, small_out = refs[n], refs[n + 1:2 * n + 1], refs[2 * n + 1]
        send_sems, recv_sems, local_sem = refs[2 * n + 2:]
        x, y, c = lax.axis_index("x"), lax.axis_index("y"), lax.axis_index("c")
        me = 4 * x + 2 * y + c
        chips = [(1 - x, y), (x, 1 - y), (1 - x, 1 - y)]
        mine = pltpu.make_async_copy(small_ref, small_out.at[me], local_sem)
        mine.start()
        copies = [pltpu.make_async_remote_copy(
            src_ref=outs[t].at[c], dst_ref=outs[t].at[c], send_sem=send_sems.at[t], recv_sem=recv_sems.at[t],
            device_id=(x, y, 1 - c), device_id_type=MESH) for t in range(n)]
        peers = [(x, y, 1 - c)] + [(cx, cy, cc) for (cx, cy) in chips for cc in (c, 1 - c)]
        for j, peer in enumerate(peers):
            copies.append(pltpu.make_async_remote_copy(
                src_ref=small_ref, dst_ref=small_out.at[me], send_sem=send_sems.at[n + j],
                recv_sem=recv_sems.at[n + j], device_id=peer, device_id_type=MESH))
        for cp in copies:
            cp.start()
        for cp in copies:
            cp.wait()
        mine.wait()

    any_spec = pl.BlockSpec(memory_space=pl.ANY)
    res = pl.pallas_call(
        body, name="pair_gather",
        out_shape=[jax.ShapeDtypeStruct(b.shape, b.dtype) for b in bufs]
        + [jax.ShapeDtypeStruct((8,) + small.shape, small.dtype)],
        in_specs=[any_spec] * (n + 1), out_specs=[any_spec] * (n + 1),
        input_output_aliases={t: t for t in range(n)},
        scratch_shapes=[pltpu.SemaphoreType.DMA((n + 7,)), pltpu.SemaphoreType.DMA((n + 7,)),
                        pltpu.SemaphoreType.DMA],
    )(*bufs, small)
    return [r.reshape(2 * b.shape[1], b.shape[2]) for r, b in zip(res[:n], bufs)], res[n]


def _adamw(w, g, m, v, name):
    rows, width = w.shape
    tr = min(ROW_TILE // 2, rows)

    def body(w_ref, g_ref, m_ref, v_ref, d_ref, nm_ref, nv_ref):
        d_ref[...], nm_ref[...], nv_ref[...] = _adamw_math(w_ref[...], g_ref[...], m_ref[...], v_ref[...])

    spec = pl.BlockSpec((tr, width), lambda i: (i, 0))
    return pl.pallas_call(
        body, name=name, grid=(rows // tr,), in_specs=[spec] * 4, out_specs=[spec] * 3,
        out_shape=[jax.ShapeDtypeStruct(w.shape, F32)] * 3,
    )(w, g, m, v)


def _adamw_math(w, g, m, v):
    c1 = 1.0 / (1.0 - ADAM_B1 ** ADAM_STEP)
    c2 = 1.0 / (1.0 - ADAM_B2 ** ADAM_STEP)
    nm = ADAM_B1 * m + (1.0 - ADAM_B1) * g
    nv = ADAM_B2 * v + (1.0 - ADAM_B2) * (g * g)
    return -ADAM_LR * ((nm * c1) / (jnp.sqrt(nv * c2) + ADAM_EPS) + ADAM_WD * w), nm, nv


def _small_update(slots, params, ms, vs):
    n = len(params)

    def body(slots_ref, *refs):
        w_refs, m_refs, v_refs, loss_ref = refs[:n], refs[n:2 * n], refs[2 * n:3 * n], refs[3 * n]
        g_refs, d_refs, nm_refs, nv_refs = (refs[3 * n + 1 + k * n:3 * n + 1 + (k + 1) * n] for k in range(4))
        acc = slots_ref[0]
        for s in range(1, slots.shape[0]):
            acc = acc + slots_ref[s]
        loss_ref[...] = acc[4:5, 0:1]
        grads = (acc[0:1] + acc[5:6], acc[3:4], acc[2:3], acc[1:2])
        for k in range(n):
            g = grads[k][:, :w_refs[k].shape[1]]
            g_refs[k][...] = g
            d_refs[k][...], nm_refs[k][...], nv_refs[k][...] = _adamw_math(w_refs[k][...], g, m_refs[k][...],
                                                                           v_refs[k][...])

    return pl.pallas_call(
        body, name="small_update",
        out_shape=[jax.ShapeDtypeStruct((1, 1), F32)] + [jax.ShapeDtypeStruct(p.shape, F32) for p in params] * 4,
    )(slots, *params, *ms, *vs)


def _local_step(x, mem, target, pre_norm, sink_a, mem_norm, post_norm, w_in_g, w_out, w_mkv, gathers=None,
                own=None):
    first_gather, late_gather = gathers if gathers else (None, None)
    u, ut, p_own, hosted = _pre_norm(x, pre_norm, own[1] if own else None, first_gather)
    if gathers:
        w_in_g = hosted[0].reshape(N_CHIPS, D_MODEL, SHARD_IN)
    pr = _pre_proj(u, w_in_g, (own[0], p_own) if own else None, late_gather)
    pr["ut"] = ut
    if gathers:
        w_out, w_mkv = (g.reshape(D_MODEL, g.shape[-1]) for g in pr["hosted"])
    mk, mv = _mem_kv(mem, mem_norm, w_mkv)
    sink = sink_a.reshape(-1)
    qa, ka, va = pr["qa"][None], pr["ka"][None], pr["va"][None]
    oa, lse_a = _band_fwd(qa, ka, va, sink, max_dist=A_WINDOW - 1, name="swa_fwd")
    ob_list, lseb_list = [], []
    for k, (win, dil) in enumerate(B_CONFIGS):
        o_i, l_i = _band_fwd(pr["qb"][k], pr["kb"][k], pr["vb"][k], None, max_dist=win // dil, name=f"dil{dil}_fwd")
        ob_list.append(o_i)
        lseb_list.append(l_i)
    oc, lse_c = _mem_attn_fwd(pr["qc"], mk, mv)
    sink_row = jnp.pad(sink, (0, LANES - sink.shape[0])).reshape(1, LANES)
    po = _post(x, target, post_norm, w_out, sink_row, oa[0], lse_a[0], pr["ga"], ob_list, lseb_list, pr["gb"], oc,
               pr["gc"])
    dqc, dmk, dmv = _mem_attn_bwd(pr["qc"], mk, mv, po["doc"], lse_c, po["dl_c"])
    dqa, dka, dva = _band_bwd(qa, ka, va, po["doa"][None], lse_a, po["dl_a"][None], max_dist=A_WINDOW - 1,
                              name="swa_bwd")
    res = dict(qb=[], kb=[], vb=[])
    for k, (win, dil) in enumerate(B_CONFIGS):
        dq_i, dk_i, dv_i = _band_bwd(pr["qb"][k], pr["kb"][k], pr["vb"][k], po["dob"][k], po["lse_b"][k],
                                     po["dl_b"][k], max_dist=win // dil, name=f"dil{dil}_bwd")
        res["qb"].append(dq_i)
        res["kb"].append(dk_i)
        res["vb"].append(dv_i)
    nat = dict(qa=dqa[0], ka=dka[0], va=dva[0], ga=po["dga"], gb=po["dgb"], qc=dqc, gc=po["dgc"])
    dproj, gw_in = _grad_w_in(pr["ut"], nat, res)
    gw_mkv, gmem = _mem_kv_bwd(mem, mem_norm, w_mkv, dmk, dmv)
    gsink = -po["gsink"][0, :sink.shape[0]]
    return dict(loss=po["loss"], g=po["g"], dproj=dproj, gw_in=gw_in, gw_out=po["gw_out"], gw_mkv=gw_mkv,
                gpost=po["gpost"], gmem=gmem, gsink=gsink, w_in_g=w_in_g)


def kernel(x, mem, pre_norm, w_in, sink_a, mem_norm, w_mem_kv, w_out, post_norm, loss_target, m_pre_norm, m_w_in, m_sink_a, m_mem_norm, m_w_mem_kv, m_w_out, m_post_norm, v_pre_norm, v_w_in, v_sink_a, v_mem_norm, v_w_mem_kv, v_w_out, v_post_norm):
    w_own = w_in[0].astype(BF16)
    gathers = (_gather_exchange([w_own]), _gather_exchange([w_out[0].astype(BF16), w_mem_kv[0].astype(BF16)]))
    chip = (2 * lax.axis_index("x") + lax.axis_index("y")).astype(jnp.int32).reshape(1)
    loc = _local_step(x[0], mem[0], loss_target[0], pre_norm, sink_a, mem_norm, post_norm, None, None, None, gathers,
                      (chip, w_own))
    big = [loc["gw_in"], loc["gw_out"].reshape(N_CHIPS, D_MODEL // N_CHIPS, D_MODEL),
           loc["gw_mkv"].reshape(N_CHIPS, D_MODEL // N_CHIPS, 2 * C_W)]
    core = lax.axis_index("c").astype(jnp.int32).reshape(1)
    w_in_full = loc["w_in_g"]
    step_in = (x[0], loc["g"], pre_norm, w_in_full, loc["dproj"])
    pair_ex = _pair_exchange(big)
    pair_handles, token = _exchange_start(pair_ex, "pair_exchange_start")
    gx_a, gpre_a = _input_grad(*step_in, None, (0, 2), token, "input_grad_a")
    big, got = _exchange_wait(pair_ex, pair_handles, gpre_a, "pair_exchange_wait")
    parts = [_pair_add(core, own, g) for own, g in zip(big, got)]
    chip_ex = _chip_exchange(parts)
    chip_handles, token = _exchange_start(chip_ex, "chip_exchange_start")
    grad_x, gpre_b = _input_grad(*step_in, gx_a, (2, 14), token, "input_grad_b")
    _, slots = _exchange_wait(chip_ex, chip_handles, gpre_b, "chip_exchange_wait")
    halves = [_slot_sum(s, name=f"chip_sum_{s.shape[2]}", core=core) for s in slots]
    widen = lambda a: jnp.pad(a.reshape(1, -1), ((0, 0), (0, D_MODEL - a.size)))
    small = jnp.concatenate([gpre_a, loc["gpost"], loc["gmem"], widen(loc["gsink"]), widen(loc["loss"]), gpre_b,
                             jnp.zeros((2, D_MODEL), F32)], axis=0)
    (g_in, g_out, g_mkv), small_slots = _pair_gather(halves, small)
    (loss, g_pre, g_sink, g_mem, g_post, d_pre, d_sink, d_mem, d_post, nm_pre, nm_sink, nm_mem, nm_post,
     nv_pre, nv_sink, nv_mem, nv_post) = _small_update(
        small_slots, (pre_norm, sink_a, mem_norm, post_norm), (m_pre_norm, m_sink_a, m_mem_norm, m_post_norm),
        (v_pre_norm, v_sink_a, v_mem_norm, v_post_norm))

    d_in, nm_in, nv_in = _adamw(w_in[0], g_in, m_w_in[0], v_w_in[0], "adamw_in")
    d_out, nm_out, nv_out = _adamw(w_out[0], g_out, m_w_out[0], v_w_out[0], "adamw_out")
    d_mkv, nm_mkv, nv_mkv = _adamw(w_mem_kv[0], g_mkv, m_w_mem_kv[0], v_w_mem_kv[0], "adamw_mkv")
    lead = lambda a: a[None]
    return (loss.reshape(()), lead(grad_x),
            g_pre, lead(g_in), g_sink, g_mem, lead(g_mkv), lead(g_out), g_post,
            d_pre, lead(d_in), d_sink, d_mem, lead(d_mkv), lead(d_out), d_post,
            nm_pre, lead(nm_in), nm_sink, nm_mem, lead(nm_mkv), lead(nm_out), nm_post,
            nv_pre, lead(nv_in), nv_sink, nv_mem, lead(nv_mkv), lead(nv_out), nv_post)
```

```python
import numpy as np
import jax
import jax.numpy as jnp
from jax import lax
from jax.experimental import pallas as pl
from jax.experimental.pallas import tpu as pltpu

F32 = jnp.float32
BF16 = jnp.bfloat16

D_MODEL = 1024
HEAD_DIM = 64
LANES = 128
BLOCK = 128
ROW_TILE = 512
ATTN_TILE = 1024
INPUT_GRAD_TILES = 16
A_W, A_KV_W, B_W, C_W = 384, 128, 384, 256
N_MEM = 256
D_IN = 3072
N_CHIPS = 4
SHARD_IN = D_IN // N_CHIPS
B_CONFIGS = ((128, 1), (512, 4), (2048, 16))
B_DILS = tuple(d for _, d in B_CONFIGS)
A_WINDOW = 128
RMS_EPS = 1e-6
ROPE_THETA = 500000.0
SCALE = HEAD_DIM ** -0.5
NEG = -1e30
ADAM_LR, ADAM_B1, ADAM_B2, ADAM_EPS, ADAM_WD, ADAM_STEP = 0.001, 0.9, 0.999, 1e-08, 0.01, 10

NT = (((1,), (1,)), ((), ()))
TN = (((0,), (0,)), ((), ()))
MESH = pl.DeviceIdType.MESH

_PROJ_LAYOUT = (
    [("qa", 128 * i, True, True) for i in range(3)] + [("ka", 0, True, False), ("va", 0, False, False)]
    + [("ga", 128 * i, False, False) for i in range(3)]
    + [("qb", 128 * i, True, True) for i in range(3)] + [("kb", 128 * i, True, False) for i in range(3)]
    + [("vb", 128 * i, False, False) for i in range(3)] + [("gb", 128 * i, False, False) for i in range(3)]
    + [("qc", 128 * i, False, True) for i in range(2)] + [("gc", 128 * i, False, False) for i in range(2)]
)
_PROJ_WIDTH = dict(qa=A_W, ka=A_KV_W, va=A_KV_W, ga=A_W, qb=B_W, kb=B_W, vb=B_W, gb=B_W, qc=C_W, gc=C_W)
_NATURAL = ("qa", "ka", "va", "ga", "gb", "qc", "gc")
_DILATED = ("qb", "kb", "vb")


def _dot(a, b):
    return jnp.dot(a, b, preferred_element_type=F32)


def _dot_nt(a, b):
    return lax.dot_general(a, b, NT, preferred_element_type=F32)


def _dot_tn(a, b):
    return lax.dot_general(a, b, TN, preferred_element_type=F32)


def _half_masks(rows):
    lane = lax.broadcasted_iota(jnp.int32, (rows, LANES), 1)
    return lane < HEAD_DIM, lane >= HEAD_DIM


def _rope(t, c, sm, sp):
    return t * c + pltpu.roll(t, LANES - 8, 1) * sm + pltpu.roll(t, 8, 1) * sp


def _rope_tables(seq, tm):
    dim = np.arange(LANES) % HEAD_DIM
    inv_freq = (np.float32(ROPE_THETA) ** (-np.arange(0, 16, 2, dtype=np.float32) / np.float32(16))).astype(np.float64)
    freq = np.where(dim < 16, inv_freq[dim % 8], 0.0)[None, :]
    local = np.arange(tm, dtype=np.float64)[:, None] * freq
    base = (np.arange(seq // tm, dtype=np.float64) * tm)[:, None] * freq
    both = lambda a: np.concatenate([np.cos(a), np.sin(a)], axis=1).astype(np.float32)
    return jnp.asarray(both(local)), jnp.asarray(np.repeat(both(base), 8, axis=0))


def _rope_coeffs(local_ref, base_ref):
    cl, sl = local_ref[:, :LANES], local_ref[:, LANES:]
    cb, sb = base_ref[0:1, :LANES], base_ref[0:1, LANES:]
    cos = cb * cl - sb * sl
    sin = sb * cl + cb * sl
    dim = lax.broadcasted_iota(jnp.int32, (1, LANES), 1) % HEAD_DIM
    return cos, jnp.where(dim < 8, -sin, 0.0), jnp.where((dim >= 8) & (dim < 16), sin, 0.0)


def _split3(x):
    a = x.astype(BF16)
    r = x - a.astype(F32)
    b = r.astype(BF16)
    c = (r - b.astype(F32)).astype(BF16)
    return a, b, c


def _rows_to_lanes(x):
    row = lax.broadcasted_iota(jnp.int32, (8, LANES), 0)
    lane = lax.broadcasted_iota(jnp.int32, (8, LANES), 1)
    eye = (row == lane).astype(BF16)
    a, b, c = _split3(x)
    return _dot_nt(eye, a) + _dot_nt(eye, b) + _dot_nt(eye, c)


def _head_sum_matrix(width):
    k = lax.broadcasted_iota(jnp.int32, (width, LANES), 0)
    h = lax.broadcasted_iota(jnp.int32, (width, LANES), 1)
    return (k // HEAD_DIM == h).astype(BF16)


def _head_expand_matrix(width):
    h = lax.broadcasted_iota(jnp.int32, (LANES, width), 0)
    k = lax.broadcasted_iota(jnp.int32, (LANES, width), 1)
    return (k // HEAD_DIM == h).astype(BF16)


def _dot_split(x, mat, terms):
    parts = _split3(x)[:terms]
    out = _dot(parts[0], mat)
    for p in parts[1:]:
        out = out + _dot(p, mat)
    return out


def _per_head(cols, fill=0.0):
    rows = cols[0].shape[0]
    lane = lax.broadcasted_iota(jnp.int32, (rows, LANES), 1)
    out = jnp.full((rows, LANES), fill, F32)
    for h, col in enumerate(cols):
        out = jnp.where(lane == h, col, out)
    return out


def _lane_blocks(width):
    return [slice(p * LANES, (p + 1) * LANES) for p in range(width // LANES)]


def _stage(rows, width):
    return pltpu.VMEM((width // LANES, rows, LANES), F32)


def _stage_write(buf, value):
    for p, lanes in enumerate(_lane_blocks(value.shape[1])):
        buf[p] = value[:, lanes]


def _stage_read(buf):
    return jnp.concatenate([buf[p] for p in range(buf.shape[0])], axis=1) if buf.shape[0] > 1 else buf[0]


def _to_residues(buf, out_ref, dil):
    rows = buf.shape[1] // dil
    for r in range(dil):
        for p in range(buf.shape[0]):
            plane = buf.at[p]
            out_ref[r, :, p * LANES:(p + 1) * LANES] = plane[pl.ds(r, rows, stride=dil), :].astype(out_ref.dtype)


def _from_residues(in_ref, buf, dil):
    rows = buf.shape[1] // dil
    for r in range(dil):
        for p in range(buf.shape[0]):
            plane = buf.at[p]
            plane[pl.ds(r, rows, stride=dil), :] = in_ref[r, :, p * LANES:(p + 1) * LANES].astype(F32)


def _residue_spec(dil, tm, width):
    return pl.BlockSpec((dil, tm // dil, width), lambda i: (0, i, 0))


def _gather_exchange(shards_2d, prefilled=True):
    shards = tuple(jax.ShapeDtypeStruct((2, s.shape[0] // 2, s.shape[1]), BF16) for s in shards_2d)
    n = len(shards)

    def copies(in_refs, out_refs, send_sems, recv_sems, *local_sems):
        srcs, outs = in_refs[:n], out_refs
        x, y, c = lax.axis_index("x"), lax.axis_index("y"), lax.axis_index("c")
        my_chip = 2 * x + y
        sibling = (x, y, 1 - c)
        chips = [(1 - x, y), (x, 1 - y), (1 - x, 1 - y)]

        def copy(k, src, dst, to):
            return pltpu.make_async_remote_copy(src_ref=src, dst_ref=dst, send_sem=send_sems.at[k],
                                                recv_sem=recv_sems.at[k], device_id=to, device_id_type=MESH)

        first, arrive, passed, sibling_arrive = [], [], [], []
        for j, (cx, cy) in enumerate(chips):
            chip = 2 * cx + cy
            for t in range(n):
                k = n * j + t
                first.append(copy(k, srcs[t].at[c], outs[t].at[my_chip, c], (cx, cy, c)))
                arrive.append(copy(k, srcs[t].at[c], outs[t].at[chip, c], (cx, cy, c)))
                passed.append(copy(n * 3 + k, outs[t].at[chip, c], outs[t].at[chip, c], sibling))
                sibling_arrive.append(copy(n * 3 + k, outs[t].at[chip, 1 - c], outs[t].at[chip, 1 - c], sibling))
        own = [pltpu.make_async_copy(srcs[t], outs[t].at[my_chip], local_sems[0].at[t]) for t in range(n)
               ] if local_sems else []
        return first, arrive, passed, sibling_arrive, own

    def start(*refs):
        first, _, _, _, own = copies(*refs)
        for cp in first + own:
            cp.start()

    def forward(refs, senders):
        _, arrive, passed, _, _ = copies(*refs)
        for j in senders:
            for k in range(n * j, n * (j + 1)):
                arrive[k].wait_recv()
                passed[k].start()

    def mid(*refs):
        forward(refs, (0, 1))

    def finish(*refs):
        forward(refs, (2,))
        first, _, passed, sibling_arrive, own = copies(*refs)
        for cp in sibling_arrive:
            cp.wait_recv()
        for cp in first + passed:
            cp.wait_send()
        for cp in own:
            cp.wait()

    ex = dict(ins=[], start=start, mid=mid, finish=finish, prefilled=prefilled,
              outs=[jax.ShapeDtypeStruct((N_CHIPS,) + s.shape, s.dtype) for s in shards],
              sems=[pltpu.SemaphoreType.DMA((6 * n,)), pltpu.SemaphoreType.DMA((6 * n,))])
    if prefilled:
        my_chip = 2 * lax.axis_index("x") + lax.axis_index("y")
        halves = [a.reshape(s.shape) for a, s in zip(shards_2d, shards)]
        landing = [lax.dynamic_update_slice(jnp.zeros((N_CHIPS,) + s.shape, s.dtype), a[None], (my_chip, 0, 0, 0))
                   for a, s in zip(halves, shards)]
        ex.update(ins=halves + landing, aliases={n + t: t for t in range(n)})
    else:
        ex["sems"].append(pltpu.SemaphoreType.DMA((n,)))
    return ex


def _mem_kv(mem, mem_norm, w_mkv):
    def body(mem_ref, g_ref, w_ref, mk_ref, mv_ref):
        m = mem_ref[...]
        r = lax.rsqrt(jnp.mean(m * m, axis=-1, keepdims=True) + RMS_EPS)
        mn = (m * r * g_ref[...]).astype(BF16)
        kv = _dot(mn, w_ref[...])
        mk_ref[...] = kv[:, :C_W].astype(BF16)
        mv_ref[...] = kv[:, C_W:].astype(BF16)

    return pl.pallas_call(
        body, name="mem_kv",
        out_shape=[jax.ShapeDtypeStruct((N_MEM, C_W), BF16)] * 2,
    )(mem, mem_norm, w_mkv)


def _mem_kv_bwd(mem, mem_norm, w_mkv, dmk, dmv):
    def body(mem_ref, g_ref, w_ref, dmk_ref, dmv_ref, gw_ref, gn_ref):
        m = mem_ref[...]
        r = lax.rsqrt(jnp.mean(m * m, axis=-1, keepdims=True) + RMS_EPS)
        mhat = m * r
        mn = (mhat * g_ref[...]).astype(BF16)
        dkv = jnp.concatenate([dmk_ref[...], dmv_ref[...]], axis=1).astype(BF16)
        gw_ref[...] = _dot_tn(mn, dkv)
        dmn = _dot_nt(dkv, w_ref[...])
        gn_ref[...] = jnp.sum(dmn * mhat, axis=0, keepdims=True)

    return pl.pallas_call(
        body, name="mem_kv_bwd",
        out_shape=[jax.ShapeDtypeStruct((D_MODEL, 2 * C_W), F32), jax.ShapeDtypeStruct((1, D_MODEL), F32)],
    )(mem, mem_norm, w_mkv, dmk, dmv)


def _host_phases(host, in_refs, out_refs, sems, steps, before):
    if not host:
        return
    step = pl.program_id(0)
    phases = [("start", 0)] if before else [("mid", max(steps - 3, 0)), ("finish", steps - 1)]
    for phase, at in phases:
        pl.when(step == at)(lambda phase=phase: host[phase](in_refs, out_refs, *sems))


def _pre_norm(x, pre_norm, w_own=None, host=None):
    seq = x.shape[0]
    tm = min(ROW_TILE, seq)
    n_own_in = 2 if w_own is None else 3
    n_own_out = n_own_in
    n_host_in = len(host["ins"]) if host else 0
    n_host_out = len(host["outs"]) if host else 0
    half = D_MODEL // 2

    def body(x_ref, g_ref, *refs):
        w_ref = None if w_own is None else refs[0]
        refs = refs[n_own_in - 2:]
        host_in, own_out, refs = refs[:n_host_in], refs[n_host_in:n_host_in + n_own_out], refs[n_host_in + n_own_out:]
        host_out, refs = refs[:n_host_out], refs[n_host_out:]
        if w_own is not None:
            wb, sems = refs[0], refs[1:]

            @pl.when(pl.program_id(0) == 0)
            def _():
                for h in range(2):
                    wb[h] = w_ref[h * half:(h + 1) * half, :].astype(BF16)
            if host and not host["prefilled"]:
                host_in = [wb]
        else:
            sems = refs
        _host_phases(host, host_in, host_out, sems, seq // tm, before=True)

        xv = x_ref[...]
        r = lax.rsqrt(jnp.mean(xv * xv, axis=-1, keepdims=True) + RMS_EPS)
        u = xv * r * g_ref[...]
        ub = u.astype(BF16)
        own_out[0][...] = ub
        own_out[1][...] = u.T.astype(BF16)
        if w_own is not None:
            own_out[2][...] = _dot(ub[:, :half], wb[0]) + _dot(ub[:, half:], wb[1])
        _host_phases(host, host_in, host_out, sems, seq // tm, before=False)

    any_spec = pl.BlockSpec(memory_space=pl.ANY)
    ins = [x, pre_norm]
    in_specs = [pl.BlockSpec((tm, D_MODEL), lambda i: (i, 0)), pl.BlockSpec(pre_norm.shape, lambda i: (0, 0))]
    out_shape = [jax.ShapeDtypeStruct((seq, D_MODEL), BF16), jax.ShapeDtypeStruct((D_MODEL, seq), BF16)]
    out_specs = [pl.BlockSpec((tm, D_MODEL), lambda i: (i, 0)), pl.BlockSpec((D_MODEL, tm), lambda i: (0, i))]
    aliases, scratch = {}, []
    if w_own is not None:
        ins.append(w_own)
        in_specs.append(pl.BlockSpec(w_own.shape, lambda i: (0, 0)))
        out_shape.append(jax.ShapeDtypeStruct((seq, w_own.shape[1]), F32))
        out_specs.append(pl.BlockSpec((tm, w_own.shape[1]), lambda i: (i, 0)))
        scratch.append(pltpu.VMEM((2, half, w_own.shape[1]), BF16))
    if host:
        aliases = {len(ins) + k: n_own_out + v for k, v in host.get("aliases", {}).items()}
        ins += list(host["ins"])
        in_specs += [any_spec] * n_host_in
        out_shape += list(host["outs"])
        out_specs += [any_spec] * n_host_out
        scratch += list(host["sems"])
    res = pl.pallas_call(
        body, name="pre_norm", grid=(seq // tm,), in_specs=in_specs, out_specs=out_specs, out_shape=out_shape,
        input_output_aliases=aliases, scratch_shapes=scratch,
        compiler_params=pltpu.CompilerParams(dimension_semantics=("arbitrary",)),
    )(*ins)
    return res[0], res[1], (None if w_own is None else res[2]), res[n_own_out:]


def _pre_proj(u, w_in_g, own=None, host=None):
    seq = u.shape[0]
    tm = min(ROW_TILE, seq)
    n_nat, n_dil = len(_NATURAL), len(_DILATED) * len(B_DILS)
    rope = _rope_tables(seq, tm)

    n_host_in = len(host["ins"]) if host else 0
    n_host_out = len(host["outs"]) if host else 0
    n_own_out = n_nat + n_dil

    def body(u_ref, w_ref, rl_ref, rb_ref, *refs):
        if own:
            (chip_ref, pown_ref), refs = refs[:2], refs[2:]
        host_in, refs = refs[:n_host_in], refs[n_host_in:]
        nat = dict(zip(_NATURAL, refs[:n_nat]))
        res = {n: refs[n_nat + len(B_DILS) * k:n_nat + len(B_DILS) * (k + 1)] for k, n in enumerate(_DILATED)}
        host_out = refs[n_own_out:n_own_out + n_host_out]
        bufs = dict(zip(_DILATED, refs[n_own_out + n_host_out:]))
        sems = refs[n_own_out + n_host_out + len(_DILATED):]
        _host_phases(host, host_in, host_out, sems, seq // tm, before=True)

        def project(own_chip):
            ub = u_ref[...]
            c, sm, sp = _rope_coeffs(rl_ref, rb_ref)
            for j in range(N_CHIPS):
                pj = pown_ref[...] if j == own_chip else _dot(ub, w_ref[j])
                for b in range(SHARD_IN // LANES):
                    name, off, roped, scaled = _PROJ_LAYOUT[(SHARD_IN // LANES) * j + b]
                    piece = pj[:, LANES * b:LANES * (b + 1)]
                    if roped:
                        piece = _rope(piece, c, sm, sp)
                    if scaled:
                        piece = piece * SCALE
                    if name in bufs:
                        bufs[name][off // LANES] = piece
                    else:
                        nat[name][:, off:off + LANES] = piece.astype(BF16)
            for name in _DILATED:
                for ref, dil in zip(res[name], B_DILS):
                    _to_residues(bufs[name], ref, dil)

        if own:
            for chip in range(N_CHIPS):
                pl.when(chip_ref[0] == chip)(lambda chip=chip: project(chip))
        else:
            project(None)
        _host_phases(host, host_in, host_out, sems, seq // tm, before=False)

    row = lambda w: pl.BlockSpec((tm, w), lambda i: (i, 0))
    full = lambda a: pl.BlockSpec(a.shape, lambda i: (0,) * a.ndim)
    any_spec = pl.BlockSpec(memory_space=pl.ANY)
    out_shape = [jax.ShapeDtypeStruct((seq, _PROJ_WIDTH[n]), BF16) for n in _NATURAL]
    out_specs = [row(_PROJ_WIDTH[n]) for n in _NATURAL]
    for n in _DILATED:
        for dil in B_DILS:
            out_shape.append(jax.ShapeDtypeStruct((dil, seq // dil, B_W), BF16))
            out_specs.append(_residue_spec(dil, tm, B_W))
    ins = [u, w_in_g, *rope]
    in_specs = [row(D_MODEL), full(w_in_g), full(rope[0]), pl.BlockSpec((8, 2 * LANES), lambda i: (i, 0))]
    if own:
        ins += list(own)
        in_specs += [pl.BlockSpec(memory_space=pltpu.SMEM), row(SHARD_IN)]
    scratch = [_stage(tm, B_W)] * len(_DILATED)
    aliases = {}
    if host:
        aliases = {len(ins) + k: n_own_out + v for k, v in host.get("aliases", {}).items()}
        ins += list(host["ins"])
        in_specs += [any_spec] * n_host_in
        out_shape += list(host["outs"])
        out_specs += [any_spec] * n_host_out
        scratch += list(host["sems"])
    res = pl.pallas_call(
        body, name="pre_proj", grid=(seq // tm,), in_specs=in_specs, out_specs=out_specs, out_shape=out_shape,
        input_output_aliases=aliases, scratch_shapes=scratch,
        compiler_params=pltpu.CompilerParams(dimension_semantics=("arbitrary",)),
    )(*ins)
    out = dict(zip(_NATURAL, res[:n_nat]))
    for k, n in enumerate(_DILATED):
        out[n] = res[n_nat + len(B_DILS) * k:n_nat + len(B_DILS) * (k + 1)]
    out["hosted"] = res[n_own_out:]
    return out


def _band_bias(max_dist, transposed):
    i = np.arange(BLOCK)[:, None]
    j = np.arange(BLOCK)[None, :]
    if transposed:
        same = i <= j
        other = (j + BLOCK - i) <= max_dist
        vis = np.concatenate([same, other], axis=1)
    else:
        prev = (i + BLOCK - j) <= max_dist
        same = j <= i
        vis = np.concatenate([prev, same], axis=1)
    return jnp.asarray(np.where(vis, 0.0, NEG).astype(np.float32))


def _kv_place(h, gqa):
    return (0, h // 3) if gqa else (h // 2, h % 2)


def _band_fwd(q, k, v, sink, *, max_dist, name):
    dil, length, wq = q.shape
    wk = k.shape[2]
    gqa = wk != wq
    tq = min(ATTN_TILE, length)
    ns, nt = tq // BLOCK, length // tq
    npair = wq // LANES
    bias = _band_bias(max_dist, transposed=False)
    has_sink = sink is not None

    def body(*refs):
        if has_sink:
            sink_ref, refs = refs[0], refs[1:]
        q_ref, k_ref, kp_ref, v_ref, vp_ref, bias_ref, o_ref, lse_ref, kbuf, vbuf = refs[:10]
        i = pl.program_id(1)
        kbuf[0:BLOCK] = kp_ref[...]
        kbuf[BLOCK:] = k_ref[...]
        vbuf[0:BLOCK] = vp_ref[...]
        vbuf[BLOCK:] = v_ref[...]
        if gqa:
            kroll, vroll = refs[10:12]
            kroll[...] = pltpu.roll(kbuf[...], HEAD_DIM, 1)
            vroll[...] = pltpu.roll(vbuf[...], HEAD_DIM, 1)
        half = _half_masks(BLOCK)
        col_prev = (lax.broadcasted_iota(jnp.int32, (1, 2 * BLOCK), 1) < BLOCK).astype(F32)

        def score_matmuls(a):
            scores = []
            for p in range(npair):
                qp = q_ref[a * BLOCK:(a + 1) * BLOCK, p * LANES:(p + 1) * LANES]
                for e in range(2):
                    pk, ek = _kv_place(2 * p + e, gqa)
                    kw = (kbuf if ek == e else kroll)[a * BLOCK:(a + 2) * BLOCK, pk * LANES:(pk + 1) * LANES]
                    scores.append(_dot_nt(jnp.where(half[e], qp, jnp.zeros_like(qp)), kw))
            return scores

        pending = score_matmuls(0)
        for a in range(ns):
            r0 = a * BLOCK
            b = bias_ref[...]
            if a == 0:
                b = b + jnp.where(i == 0, NEG, 0.0) * col_prev
            scores = pending
            m_cols, l_cols, probs = [], [], []
            for h, s in enumerate(scores):
                s = s + b
                m = jnp.max(s, axis=1, keepdims=True)
                if has_sink:
                    m = jnp.maximum(m, sink_ref[h])
                pe = jnp.exp(s - m)
                l = jnp.sum(pe, axis=1, keepdims=True)
                if has_sink:
                    l = l + jnp.exp(sink_ref[h] - m)
                probs.append(pe.astype(BF16))
                m_cols.append(m)
                l_cols.append(l)
            pending = score_matmuls(a + 1) if a + 1 < ns else None
            for p in range(npair):
                o_h = []
                for e in range(2):
                    h = 2 * p + e
                    pk, ek = _kv_place(h, gqa)
                    vw = (vbuf if ek == e else vroll)[r0:r0 + 2 * BLOCK, pk * LANES:(pk + 1) * LANES]
                    o_h.append(_dot(probs[h], vw) * (1.0 / l_cols[h]))
                o_ref[r0:r0 + BLOCK, p * LANES:(p + 1) * LANES] = jnp.where(half[0], o_h[0], o_h[1]).astype(BF16)
            lse_ref[r0:r0 + BLOCK, :] = _per_head(m_cols) + jnp.log(_per_head(l_cols, 1.0))

    main = lambda w: pl.BlockSpec((None, tq, w), lambda r, i: (r, i, 0))
    prev = lambda w: pl.BlockSpec((None, BLOCK, w), lambda r, i: (r, jnp.maximum(i * ns - 1, 0), 0))
    in_specs = [main(wq), main(wk), prev(wk), main(wk), prev(wk), pl.BlockSpec(bias.shape, lambda r, i: (0, 0))]
    args = [q, k, k, v, v, bias]
    if has_sink:
        in_specs = [pl.BlockSpec(memory_space=pltpu.SMEM)] + in_specs
        args = [sink] + args
    scratch = [pltpu.VMEM((tq + BLOCK, wk), BF16)] * (4 if gqa else 2)
    return pl.pallas_call(
        body, name=name, grid=(dil, nt), in_specs=in_specs,
        out_specs=[main(wq), main(LANES)],
        out_shape=[jax.ShapeDtypeStruct((dil, length, wq), BF16), jax.ShapeDtypeStruct((dil, length, LANES), F32)],
        scratch_shapes=scratch,
    )(*args)


def _band_bwd(q, k, v, do, lse, delta, *, max_dist, name):
    dil, length, wq = q.shape
    wk = k.shape[2]
    gqa = wk != wq
    tq = min(ATTN_TILE, length)
    ns, nt = tq // BLOCK, length // tq
    npair = wq // LANES
    nblocks = length // BLOCK
    bias = _band_bias(max_dist, transposed=True)

    def body(q_ref, qn_ref, do_ref, don_ref, lse_ref, lsen_ref, dl_ref, dln_ref, k_ref, v_ref, bias_ref,
             dq_ref, dk_ref, dv_ref, stat_l, stat_d, dqt, kt, *rolled):
        i = pl.program_id(1)
        for pk in range(wk // LANES):
            kt[pk] = k_ref[:, pk * LANES:(pk + 1) * LANES].astype(F32).T.astype(BF16)
        if gqa:
            kroll, vroll, ktroll = rolled
            kroll[...] = pltpu.roll(k_ref[...], HEAD_DIM, 1)
            vroll[...] = pltpu.roll(v_ref[...], HEAD_DIM, 1)
            ktroll[0] = kroll[...].astype(F32).T.astype(BF16)
        for a in range(ns):
            rows = slice(a * BLOCK, (a + 1) * BLOCK)
            stat_l[a] = _rows_to_lanes(lse_ref[rows, :])
            stat_d[a] = _rows_to_lanes(dl_ref[rows, :])
        stat_l[ns] = _rows_to_lanes(lsen_ref[...])
        stat_d[ns] = _rows_to_lanes(dln_ref[...])

        @pl.when(i == 0)
        def _():
            dqt[:, :, 0:BLOCK] = jnp.zeros((npair, LANES, BLOCK), F32)

        @pl.when(i > 0)
        def _():
            dqt[:, :, 0:BLOCK] = dqt[:, :, tq:tq + BLOCK]

        dqt[:, :, BLOCK:] = jnp.zeros((npair, LANES, tq), F32)
        half2 = _half_masks(2 * BLOCK)
        row = lax.broadcasted_iota(jnp.int32, (LANES, BLOCK), 0)
        row_half = (row < HEAD_DIM, row >= HEAD_DIM)
        col_next = (lax.broadcasted_iota(jnp.int32, (1, 2 * BLOCK), 1) >= BLOCK).astype(F32)

        def scores(b):
            rows = slice(b * BLOCK, (b + 1) * BLOCK)
            nxt_rows = slice((b + 1) * BLOCK, (b + 2) * BLOCK)
            items = []
            for p in range(npair):
                lanes = slice(p * LANES, (p + 1) * LANES)
                q_next = q_ref[nxt_rows, lanes] if b + 1 < ns else qn_ref[:, lanes]
                do_next = do_ref[nxt_rows, lanes] if b + 1 < ns else don_ref[:, lanes]
                qw = jnp.concatenate([q_ref[rows, lanes], q_next], axis=0)
                dow = jnp.concatenate([do_ref[rows, lanes], do_next], axis=0)
                for e in range(2):
                    h = 2 * p + e
                    pk, ek = _kv_place(h, gqa)
                    klanes = slice(pk * LANES, (pk + 1) * LANES)
                    kb = (k_ref if ek == e else kroll)[rows, klanes]
                    vb = (v_ref if ek == e else vroll)[rows, klanes]
                    qm = jnp.where(half2[e], qw, jnp.zeros_like(qw))
                    dom = jnp.where(half2[e], dow, jnp.zeros_like(dow))
                    items.append(dict(p=p, e=e, h=h, pk=pk, ek=ek, qm=qm, dom=dom,
                                      st=_dot_nt(kb, qm), dpt=_dot_nt(vb, dom)))
            return items

        def probs(b, items):
            bt = bias_ref[...]
            if b == ns - 1:
                bt = bt + jnp.where(i == nt - 1, NEG, 0.0) * col_next
            for it in items:
                h = it["h"]
                lrow = jnp.concatenate([stat_l[b, h:h + 1, :], stat_l[b + 1, h:h + 1, :]], axis=1)
                drow = jnp.concatenate([stat_d[b, h:h + 1, :], stat_d[b + 1, h:h + 1, :]], axis=1)
                pt = jnp.exp(it["st"] + bt - lrow)
                it["ptb"] = pt.astype(BF16)
                it["dsb"] = (pt * (it["dpt"] - drow)).astype(BF16)

        pending = scores(0)
        for b in range(ns):
            rows = slice(b * BLOCK, (b + 1) * BLOCK)
            window = slice(b * BLOCK, (b + 2) * BLOCK)
            acc = {}
            items = pending
            probs(b, items)
            pending = scores(b + 1) if b + 1 < ns else None
            for p in range(npair):
                pair = items[2 * p:2 * p + 2]
                lanes = slice(p * LANES, (p + 1) * LANES)
                kparts = []
                for it in pair:
                    kbt = (kt if it["ek"] == it["e"] else ktroll)[it["pk"], :, rows]
                    kparts.append(jnp.where(row_half[it["e"]], kbt, jnp.zeros_like(kbt)))
                ds_keys = jnp.concatenate([it["dsb"] for it in pair], axis=0)
                dqt[p, :, window] += _dot(jnp.concatenate(kparts, axis=1), ds_keys)
                if not gqa:
                    q_both = jnp.concatenate([it["qm"] for it in pair], axis=0)
                    do_both = jnp.concatenate([it["dom"] for it in pair], axis=0)
                    dk_ref[rows, lanes] = _dot(jnp.concatenate([it["dsb"] for it in pair], axis=1), q_both).astype(BF16)
                    dv_ref[rows, lanes] = _dot(jnp.concatenate([it["ptb"] for it in pair], axis=1), do_both).astype(BF16)
                else:
                    for it in pair:
                        dv_c = _dot(it["ptb"], it["dom"])
                        dk_c = _dot(it["dsb"], it["qm"])
                        key = (it["pk"], it["ek"] == it["e"])
                        if key in acc:
                            acc[key] = (acc[key][0] + dk_c, acc[key][1] + dv_c)
                        else:
                            acc[key] = (dk_c, dv_c)
            if gqa:
                dk_al, dv_al = acc[(0, True)]
                dk_mis, dv_mis = acc[(0, False)]
                dk_ref[rows, :] = (dk_al + pltpu.roll(dk_mis, HEAD_DIM, 1)).astype(BF16)
                dv_ref[rows, :] = (dv_al + pltpu.roll(dv_mis, HEAD_DIM, 1)).astype(BF16)

        for p in range(npair):
            dq_ref[:, p * LANES:(p + 1) * LANES] = dqt[p, :, 0:tq].T.astype(BF16)

    main = lambda w: pl.BlockSpec((None, tq, w), lambda r, i: (r, i, 0))
    nxt = lambda w: pl.BlockSpec((None, BLOCK, w), lambda r, i: (r, jnp.minimum((i + 1) * ns, nblocks - 1), 0))
    scratch = [pltpu.VMEM((ns + 1, 8, LANES), F32), pltpu.VMEM((ns + 1, 8, LANES), F32),
               pltpu.VMEM((npair, LANES, tq + BLOCK), F32), pltpu.VMEM((wk // LANES, LANES, tq), BF16)]
    if gqa:
        scratch = scratch + [pltpu.VMEM((tq, wk), BF16)] * 2 + [pltpu.VMEM((1, LANES, tq), BF16)]
    return pl.pallas_call(
        body, name=name, grid=(dil, nt),
        in_specs=[main(wq), nxt(wq), main(wq), nxt(wq), main(LANES), nxt(LANES), main(LANES), nxt(LANES),
                  main(wk), main(wk), pl.BlockSpec(bias.shape, lambda r, i: (0, 0))],
        out_specs=[main(wq), main(wk), main(wk)],
        out_shape=[jax.ShapeDtypeStruct((dil, length, wq), BF16), jax.ShapeDtypeStruct((dil, length, wk), BF16),
                   jax.ShapeDtypeStruct((dil, length, wk), BF16)],
        scratch_shapes=scratch,
        compiler_params=pltpu.CompilerParams(dimension_semantics=("arbitrary", "arbitrary")),
    )(q, q, do, do, lse, lse, delta, delta, k, v, bias)


def _mem_attn_fwd(q, mk, mv):
    seq = q.shape[0]
    tq = min(ATTN_TILE, seq)
    sub_rows = min(4 * BLOCK, tq)
    ns = tq // sub_rows

    def body(q_ref, mk_ref, mv_ref, o_ref, lse_ref):
        half = _half_masks(sub_rows)

        def sub(a, carry):
            r0 = pl.multiple_of(a * sub_rows, sub_rows)
            scores = []
            for p in range(C_W // LANES):
                lanes = slice(p * LANES, (p + 1) * LANES)
                qp = q_ref[pl.ds(r0, sub_rows), lanes]
                for e in range(2):
                    scores.append(_dot_nt(jnp.where(half[e], qp, jnp.zeros_like(qp)), mk_ref[:, lanes]))
            m_cols, l_cols, probs = [], [], []
            for s in scores:
                m = jnp.max(s, axis=1, keepdims=True)
                pe = jnp.exp(s - m)
                probs.append(pe.astype(BF16))
                m_cols.append(m)
                l_cols.append(jnp.sum(pe, axis=1, keepdims=True))
            for p in range(C_W // LANES):
                lanes = slice(p * LANES, (p + 1) * LANES)
                o_h = [_dot(probs[2 * p + e], mv_ref[:, lanes]) * (1.0 / l_cols[2 * p + e]) for e in range(2)]
                o_ref[pl.ds(r0, sub_rows), lanes] = jnp.where(half[0], o_h[0], o_h[1]).astype(BF16)
            lse_ref[pl.ds(r0, sub_rows), :] = _per_head(m_cols) + jnp.log(_per_head(l_cols, 1.0))
            return carry

        lax.fori_loop(0, ns, sub, 0, unroll=True)

    row = lambda w: pl.BlockSpec((tq, w), lambda i: (i, 0))
    full = pl.BlockSpec((N_MEM, C_W), lambda i: (0, 0))
    return pl.pallas_call(
        body, name="mem_attn_fwd", grid=(seq // tq,), in_specs=[row(C_W), full, full],
        out_specs=[row(C_W), row(LANES)],
        out_shape=[jax.ShapeDtypeStruct((seq, C_W), BF16), jax.ShapeDtypeStruct((seq, LANES), F32)],
    )(q, mk, mv)


def _mem_attn_bwd(q, mk, mv, do, lse, delta):
    seq = q.shape[0]
    tq = min(ATTN_TILE, seq)
    ns = tq // BLOCK
    npair = C_W // LANES

    def body(q_ref, mk_ref, mv_ref, do_ref, lse_ref, dl_ref, dq_ref, dmk_ref, dmv_ref, stat_l, stat_d, mkt, dqt):
        @pl.when(pl.program_id(0) == 0)
        def _():
            dmk_ref[...] = jnp.zeros_like(dmk_ref)
            dmv_ref[...] = jnp.zeros_like(dmv_ref)
            for p in range(npair):
                mkt[p] = mk_ref[:, p * LANES:(p + 1) * LANES].astype(F32).T.astype(BF16)

        for a in range(ns):
            rows = slice(a * BLOCK, (a + 1) * BLOCK)
            stat_l[a] = _rows_to_lanes(lse_ref[rows, :])
            stat_d[a] = _rows_to_lanes(dl_ref[rows, :])
        span = min(2, ns)
        half = _half_masks(span * BLOCK)
        row = lax.broadcasted_iota(jnp.int32, (LANES, N_MEM), 0)
        row_half = (row < HEAD_DIM, row >= HEAD_DIM)

        for a in range(0, ns, span):
            rows = slice(a * BLOCK, (a + span) * BLOCK)
            items = []
            for p in range(npair):
                lanes = slice(p * LANES, (p + 1) * LANES)
                qp = q_ref[rows, lanes]
                dop = do_ref[rows, lanes]
                for e in range(2):
                    qm = jnp.where(half[e], qp, jnp.zeros_like(qp))
                    dom = jnp.where(half[e], dop, jnp.zeros_like(dop))
                    items.append(dict(p=p, e=e, qm=qm, dom=dom, st=_dot_nt(mk_ref[:, lanes], qm),
                                      dpt=_dot_nt(mv_ref[:, lanes], dom)))
            for it in items:
                h = 2 * it["p"] + it["e"]
                lrow = jnp.concatenate([stat_l[a + k, h:h + 1, :] for k in range(span)], axis=1)
                drow = jnp.concatenate([stat_d[a + k, h:h + 1, :] for k in range(span)], axis=1)
                pt = jnp.exp(it["st"] - lrow)
                it["ptb"] = pt.astype(BF16)
                it["dsb"] = (pt * (it["dpt"] - drow)).astype(BF16)
            for p in range(npair):
                lanes = slice(p * LANES, (p + 1) * LANES)
                pair = [it for it in items if it["p"] == p]
                join = lambda name, axis: jnp.concatenate([it[name] for it in pair], axis=axis)
                dmv_ref[:, lanes] += _dot(join("ptb", 1), join("dom", 0))
                dmk_ref[:, lanes] += _dot(join("dsb", 1), join("qm", 0))
                kbt = mkt[p]
                k_both = jnp.concatenate([jnp.where(row_half[e], kbt, jnp.zeros_like(kbt)) for e in range(2)], axis=1)
                dqt[p, :, rows] = _dot(k_both, join("dsb", 0))
        for p in range(npair):
            dq_ref[:, p * LANES:(p + 1) * LANES] = dqt[p].T.astype(BF16)

    row = lambda w: pl.BlockSpec((tq, w), lambda i: (i, 0))
    full = pl.BlockSpec((N_MEM, C_W), lambda i: (0, 0))
    return pl.pallas_call(
        body, name="mem_attn_bwd", grid=(seq // tq,),
        in_specs=[row(C_W), full, full, row(C_W), row(LANES), row(LANES)], out_specs=[row(C_W), full, full],
        out_shape=[jax.ShapeDtypeStruct((seq, C_W), BF16), jax.ShapeDtypeStruct((N_MEM, C_W), F32),
                   jax.ShapeDtypeStruct((N_MEM, C_W), F32)],
        scratch_shapes=[pltpu.VMEM((ns, 8, LANES), F32)] * 2
        + [pltpu.VMEM((npair, LANES, N_MEM), BF16), pltpu.VMEM((npair, LANES, tq), F32)],
        compiler_params=pltpu.CompilerParams(dimension_semantics=("arbitrary",)),
    )(q, mk, mv, do, lse, delta)


def _silu_and_grad(g):
    s = 1.0 / (1.0 + jnp.exp(-g))
    return g * s, s * (1.0 + g * (1.0 - s))


def _post(x, target, post_norm, w_out, sink_row, oa, lse_a, ga, ob_list, lseb_list, gb, oc, gc):
    seq = x.shape[0]
    tm = min(ROW_TILE, seq)
    inv_d = 1.0 / D_MODEL
    nd = len(B_DILS)

    def body(*refs):
        (x_ref, t_ref, gp_ref, w_ref, sink_ref, oa_ref, lsea_ref, ga_ref), refs = refs[:8], refs[8:]
        ob_refs, lb_refs, (gb_ref, oc_ref, gc_ref), refs = refs[:nd], refs[nd:2 * nd], refs[2 * nd:2 * nd + 3], refs[2 * nd + 3:]
        (g_ref, doa_ref, dla_ref, dga_ref), refs = refs[:4], refs[4:]
        dob_refs, lsec_refs, dlb_refs, refs = refs[:nd], refs[nd:2 * nd], refs[2 * nd:3 * nd], refs[3 * nd:]
        (dgb_ref, doc_ref, dlc_ref, dgc_ref, gw_ref, gpost_ref, gsink_ref, loss_ref), refs = refs[:8], refs[8:]
        ycat, obufs, lbufs, st_do, st_l, st_d = refs[0], refs[1:nd], refs[nd:2 * nd - 1], refs[2 * nd - 1], refs[2 * nd], refs[2 * nd + 1]

        @pl.when(pl.program_id(0) == 0)
        def _():
            gw_ref[...] = jnp.zeros_like(gw_ref)
            gpost_ref[...] = jnp.zeros_like(gpost_ref)
            gsink_ref[...] = jnp.zeros_like(gsink_ref)
            loss_ref[...] = jnp.zeros_like(loss_ref)

        o_i, l_i = [ob_refs[0][0].astype(F32)], [lb_refs[0][0]]
        for k in range(1, nd):
            _from_residues(ob_refs[k], obufs[k - 1], B_DILS[k])
            _from_residues(lb_refs[k], lbufs[k - 1], B_DILS[k])
            o_i.append(_stage_read(obufs[k - 1]))
            l_i.append(_stage_read(lbufs[k - 1]))
        mx = l_i[0]
        for l in l_i[1:]:
            mx = jnp.maximum(mx, l)
        w_i = [jnp.exp(l - mx) for l in l_i]
        z = w_i[0]
        for w in w_i[1:]:
            z = z + w
        _stage_write(st_l, mx + jnp.log(z))
        expand = _head_expand_matrix(B_W)
        inv_z = 1.0 / z
        ob = None
        for w, o in zip(w_i, o_i):
            term = _dot_split(w * inv_z, expand, 2) * o
            ob = term if ob is None else ob + term
        oa, oc = oa_ref[...].astype(F32), oc_ref[...].astype(F32)
        sa, dsa = _silu_and_grad(ga_ref[...].astype(F32))
        sb, dsb = _silu_and_grad(gb_ref[...].astype(F32))
        sc, dsc = _silu_and_grad(gc_ref[...].astype(F32))
        ycat[:, 0:A_W] = (oa * sa).astype(BF16)
        ycat[:, A_W:A_W + B_W] = (ob * sb).astype(BF16)
        ycat[:, A_W + B_W:] = (oc * sc).astype(BF16)
        y2 = _dot(ycat[...], w_ref[...])
        r = lax.rsqrt(jnp.mean(y2 * y2, axis=-1, keepdims=True) + RMS_EPS)
        zhat = y2 * r
        gp = gp_ref[...]
        err = x_ref[...] + zhat * gp - t_ref[...]
        loss_ref[...] += jnp.sum(err * err) * (0.5 * inv_d)
        g = err * inv_d
        g_ref[...] = g
        gpost_ref[...] += jnp.sum(g * zhat, axis=0, keepdims=True)
        a = g * gp
        dy2 = (r * (a - zhat * jnp.mean(a * zhat, axis=-1, keepdims=True))).astype(BF16)
        gw_ref[...] += _dot_tn(ycat[...], dy2)
        dycat = _dot_nt(dy2, w_ref[...])
        dya, dyb, dyc = dycat[:, 0:A_W], dycat[:, A_W:A_W + B_W], dycat[:, A_W + B_W:]
        doa, dob, doc = dya * sa, dyb * sb, dyc * sc
        doa_ref[...] = doa.astype(BF16)
        doc_ref[...] = doc.astype(BF16)
        dga_ref[...] = (dya * oa * dsa).astype(BF16)
        dgb_ref[...] = (dyb * ob * dsb).astype(BF16)
        dgc_ref[...] = (dyc * oc * dsc).astype(BF16)
        dl_a = _dot_split(doa * oa, _head_sum_matrix(A_W), 2)
        dla_ref[...] = dl_a
        dlc_ref[...] = _dot_split(doc * oc, _head_sum_matrix(C_W), 2)
        gsink_ref[...] += jnp.sum(jnp.exp(sink_ref[...] - lsea_ref[...]) * dl_a, axis=0, keepdims=True)
        _stage_write(st_do, dob)
        _stage_write(st_d, _dot_split(dob * ob, _head_sum_matrix(B_W), 2))
        for k, dil in enumerate(B_DILS):
            _to_residues(st_do, dob_refs[k], dil)
            _to_residues(st_l, lsec_refs[k], dil)
            _to_residues(st_d, dlb_refs[k], dil)

    row = lambda w: pl.BlockSpec((tm, w), lambda i: (i, 0))
    full = lambda shape: pl.BlockSpec(shape, lambda i: (0,) * len(shape))
    res_specs = lambda w: [_residue_spec(d, tm, w) for d in B_DILS]
    res_shapes = lambda w, dt: [jax.ShapeDtypeStruct((d, seq // d, w), dt) for d in B_DILS]
    ins = [x, target, post_norm, w_out, sink_row, oa, lse_a, ga, *ob_list, *lseb_list, gb, oc, gc]
    in_specs = ([row(D_MODEL), row(D_MODEL), full((1, D_MODEL)), full((D_MODEL, D_MODEL)), full((1, LANES)),
                 row(A_W), row(LANES), row(A_W)] + res_specs(B_W) + res_specs(LANES) + [row(B_W), row(C_W), row(C_W)])
    out_shape = ([jax.ShapeDtypeStruct((seq, D_MODEL), F32), jax.ShapeDtypeStruct((seq, A_W), BF16),
                  jax.ShapeDtypeStruct((seq, LANES), F32), jax.ShapeDtypeStruct((seq, A_W), BF16)]
                 + res_shapes(B_W, BF16) + res_shapes(LANES, F32) + res_shapes(LANES, F32)
                 + [jax.ShapeDtypeStruct((seq, B_W), BF16), jax.ShapeDtypeStruct((seq, C_W), BF16),
                    jax.ShapeDtypeStruct((seq, LANES), F32), jax.ShapeDtypeStruct((seq, C_W), BF16),
                    jax.ShapeDtypeStruct((D_MODEL, D_MODEL), F32), jax.ShapeDtypeStruct((1, D_MODEL), F32),
                    jax.ShapeDtypeStruct((1, LANES), F32), jax.ShapeDtypeStruct((1, LANES), F32)])
    out_specs = ([row(D_MODEL), row(A_W), row(LANES), row(A_W)] + res_specs(B_W) + res_specs(LANES) + res_specs(LANES)
                 + [row(B_W), row(C_W), row(LANES), row(C_W),
                    full((D_MODEL, D_MODEL)), full((1, D_MODEL)), full((1, LANES)), full((1, LANES))])
    scratch = ([pltpu.VMEM((tm, D_MODEL), BF16)] + [_stage(tm, B_W)] * (nd - 1) + [_stage(tm, LANES)] * (nd - 1)
               + [_stage(tm, B_W), _stage(tm, LANES), _stage(tm, LANES)])
    res = pl.pallas_call(
        body, name="post", grid=(seq // tm,), in_specs=in_specs, out_specs=out_specs, out_shape=out_shape,
        scratch_shapes=scratch,
        compiler_params=pltpu.CompilerParams(dimension_semantics=("arbitrary",)),
    )(*ins)
    out = dict(g=res[0], doa=res[1], dl_a=res[2], dga=res[3], dob=res[4:4 + nd], lse_b=res[4 + nd:4 + 2 * nd],
               dl_b=res[4 + 2 * nd:4 + 3 * nd])
    rest = res[4 + 3 * nd:]
    out.update(dgb=rest[0], doc=rest[1], dl_c=rest[2], dgc=rest[3], gw_out=rest[4], gpost=rest[5], gsink=rest[6],
               loss=rest[7])
    return out


def _grad_w_in(ut, nat, res):
    seq = ut.shape[1]
    tm = min(ROW_TILE, seq)
    nd = len(B_DILS)
    nat_list = [nat[n] for n in _NATURAL]
    res_list = [a for n in _DILATED for a in res[n]]
    rope = _rope_tables(seq, tm)

    def body(rl_ref, rb_ref, ut_ref, *refs):
        nat_refs = dict(zip(_NATURAL, refs[:len(_NATURAL)]))
        refs = refs[len(_NATURAL):]
        res_refs = {n: refs[nd * k:nd * (k + 1)] for k, n in enumerate(_DILATED)}
        refs = refs[nd * len(_DILATED):]
        dproj_ref, gw_ref = refs[:2]
        bufs = {n: refs[2 + (nd - 1) * k:2 + (nd - 1) * (k + 1)] for k, n in enumerate(_DILATED)}

        @pl.when(pl.program_id(0) == 0)
        def _():
            gw_ref[...] = jnp.zeros_like(gw_ref)

        for n in _DILATED:
            for k in range(1, nd):
                _from_residues(res_refs[n][k], bufs[n][k - 1], B_DILS[k])
        c, sm, sp = _rope_coeffs(rl_ref, rb_ref)
        sm, sp = -sm, -sp
        for blk, (name, off, roped, scaled) in enumerate(_PROJ_LAYOUT):
            lanes = slice(off, off + LANES)
            if name in nat_refs:
                piece = nat_refs[name][:, lanes].astype(F32)
            else:
                piece = res_refs[name][0][0, :, lanes].astype(F32)
                for buf in bufs[name]:
                    piece = piece + buf[off // LANES]
            if roped:
                piece = _rope(piece, c, sm, sp)
            if scaled:
                piece = piece * SCALE
            dproj_ref[:, blk * LANES:(blk + 1) * LANES] = piece.astype(BF16)
        for j in range(N_CHIPS):
            gw_ref[j] += _dot(ut_ref[...], dproj_ref[:, j * SHARD_IN:(j + 1) * SHARD_IN])

    row = lambda w: pl.BlockSpec((tm, w), lambda i: (i, 0))
    in_specs = ([pl.BlockSpec(rope[0].shape, lambda i: (0, 0)), pl.BlockSpec((8, 2 * LANES), lambda i: (i, 0)),
                 pl.BlockSpec((D_MODEL, tm), lambda i: (0, i))]
                + [row(a.shape[1]) for a in nat_list]
                + [_residue_spec(d, tm, B_W) for _ in _DILATED for d in B_DILS])
    return pl.pallas_call(
        body, name="grad_w_in", grid=(seq // tm,), in_specs=in_specs,
        out_specs=[row(D_IN), pl.BlockSpec((N_CHIPS, D_MODEL, SHARD_IN), lambda i: (0, 0, 0))],
        out_shape=[jax.ShapeDtypeStruct((seq, D_IN), BF16), jax.ShapeDtypeStruct((N_CHIPS, D_MODEL, SHARD_IN), F32)],
        scratch_shapes=[_stage(tm, B_W)] * ((nd - 1) * len(_DILATED)),
        compiler_params=pltpu.CompilerParams(dimension_semantics=("arbitrary",)),
    )(*rope, ut, *nat_list, *res_list)


def _input_grad(x, g, pre_norm, w_in_g, dproj, gx_prev, span, after, name):
    seq = x.shape[0]
    tm = seq // INPUT_GRAD_TILES
    first_block, steps = span

    def body(*refs):
        x_ref, g_ref, gp_ref, w_ref, dp_ref = refs[:5]
        gx_ref, gpre_ref = refs[-2:]

        @pl.when(pl.program_id(0) == 0)
        def _():
            gpre_ref[...] = jnp.zeros_like(gpre_ref)

        du = None
        for j in range(N_CHIPS):
            term = _dot_nt(dp_ref[:, j * SHARD_IN:(j + 1) * SHARD_IN], w_ref[j])
            du = term if du is None else du + term
        xv = x_ref[...]
        r = lax.rsqrt(jnp.mean(xv * xv, axis=-1, keepdims=True) + RMS_EPS)
        xhat = xv * r
        gpre_ref[...] += jnp.sum(du * xhat, axis=0, keepdims=True)
        a = du * gp_ref[...]
        gx_ref[...] = g_ref[...] + r * (a - xhat * jnp.mean(a * xhat, axis=-1, keepdims=True))

    row = lambda w: pl.BlockSpec((tm, w), lambda i: (first_block + i, 0))
    full = lambda a: pl.BlockSpec(a.shape, lambda i: (0,) * a.ndim)
    any_spec = pl.BlockSpec(memory_space=pl.ANY)
    ins = [x, g, pre_norm, w_in_g, dproj]
    in_specs = [row(D_MODEL), row(D_MODEL), full(pre_norm), full(w_in_g), row(D_IN)]
    aliases = {}
    if gx_prev is not None:
        aliases[len(ins)] = 0
        ins.append(gx_prev)
        in_specs.append(any_spec)
    if after is not None:
        ins.append(after)
        in_specs.append(any_spec)
    return pl.pallas_call(
        body, name=name, grid=(steps,), in_specs=in_specs,
        out_specs=[row(D_MODEL), pl.BlockSpec((1, D_MODEL), lambda i: (0, 0))],
        out_shape=[jax.ShapeDtypeStruct((seq, D_MODEL), F32), jax.ShapeDtypeStruct((1, D_MODEL), F32)],
        input_output_aliases=aliases,
        compiler_params=pltpu.CompilerParams(dimension_semantics=("arbitrary",)),
    )(*ins)


def _exchange_start(ex, name):
    n_in, n_out, n_sem = len(ex["ins"]), len(ex["outs"]), len(ex["sems"])

    def body(*refs):
        in_refs, land_refs, sems = refs[:n_in], refs[n_in:n_in + n_out], refs[n_in + n_out:n_in + n_out + n_sem]
        ex["start"](in_refs, land_refs, *sems)
        token = refs[-1]
        token[...] = jnp.zeros_like(token)

    hbm = pl.BlockSpec(memory_space=pltpu.HBM)
    sem = pl.BlockSpec(memory_space=pltpu.SEMAPHORE)
    ins = [pltpu.with_memory_space_constraint(a, pltpu.HBM) for a in ex["ins"]]
    landing = [pltpu.with_memory_space_constraint(lax.empty(o.shape, o.dtype), pltpu.HBM) for o in ex["outs"]]
    res = pl.pallas_call(
        body, name=name,
        out_shape=list(ex["sems"]) + [pltpu.HBM(a.shape, a.dtype) for a in ex["ins"]]
        + [pltpu.HBM(o.shape, o.dtype) for o in ex["outs"]] + [jax.ShapeDtypeStruct((8, LANES), F32)],
        in_specs=[hbm] * (n_in + n_out),
        out_specs=[sem] * n_sem + [hbm] * (n_in + n_out) + [pl.BlockSpec(memory_space=pltpu.VMEM)],
        input_output_aliases={k: n_sem + k for k in range(n_in + n_out)},
        compiler_params=pltpu.CompilerParams(has_side_effects=pltpu.SideEffectType.DATAFLOW_SIDE_EFFECTING),
    )(*ins, *landing)
    return res[:-1], res[-1]


def _exchange_wait(ex, handles, after, name):
    n_in, n_out, n_sem = len(ex["ins"]), len(ex["outs"]), len(ex["sems"])
    sems, thru = handles[:n_sem], handles[n_sem:]

    def body(*refs):
        in_refs, land_refs = refs[:n_in], refs[n_in:n_in + n_out]
        sem_refs = refs[n_in + n_out:n_in + n_out + n_sem]
        ex["finish"](in_refs, land_refs, *sem_refs)

    hbm = pl.BlockSpec(memory_space=pltpu.HBM)
    sem = pl.BlockSpec(memory_space=pltpu.SEMAPHORE)
    res = pl.pallas_call(
        body, name=name,
        out_shape=[pltpu.HBM(a.shape, a.dtype) for a in thru],
        in_specs=[hbm] * (n_in + n_out) + [sem] * n_sem + [pl.BlockSpec(memory_space=pl.ANY)],
        out_specs=[hbm] * (n_in + n_out),
        input_output_aliases={k: k for k in range(n_in + n_out)},
        compiler_params=pltpu.CompilerParams(has_side_effects=pltpu.SideEffectType.DATAFLOW_SIDE_EFFECTING),
    )(*thru, *sems, after)
    return res[:n_in], res[n_in:]


def _start_finish(build):
    def start(*refs):
        for cp in build(*refs):
            cp.start()

    def finish(*refs):
        for cp in build(*refs):
            cp.wait()

    return dict(start=start, finish=finish)


def _pair_exchange(grads):
    n = len(grads)

    def build(srcs, outs, send_sems, recv_sems):
        x, y, c = lax.axis_index("x"), lax.axis_index("y"), lax.axis_index("c")
        copies = []
        for t in range(n):
            rows = grads[t].shape[1] // 2
            copies.append(pltpu.make_async_remote_copy(
                src_ref=srcs[t].at[:, pl.ds((1 - c) * rows, rows)], dst_ref=outs[t],
                send_sem=send_sems.at[t], recv_sem=recv_sems.at[t], device_id=(x, y, 1 - c), device_id_type=MESH))
        return copies

    return dict(ins=list(grads), **_start_finish(build),
                outs=[jax.ShapeDtypeStruct((g.shape[0], g.shape[1] // 2, g.shape[2]), g.dtype) for g in grads],
                sems=[pltpu.SemaphoreType.DMA((n,)), pltpu.SemaphoreType.DMA((n,))])


def _pair_add(core, own, got):
    nchip, rows2, width = own.shape
    rows = rows2 // 2
    tr = min(ROW_TILE, rows)
    nb = rows // tr

    def body(core_ref, own_ref, got_ref, out_ref):
        out_ref[...] = (own_ref[...] + got_ref[...]).astype(BF16)

    grid_spec = pltpu.PrefetchScalarGridSpec(
        num_scalar_prefetch=1, grid=(nchip, nb),
        in_specs=[pl.BlockSpec((None, tr, width), lambda k, i, core_ref: (k, core_ref[0] * nb + i, 0)),
                  pl.BlockSpec((None, tr, width), lambda k, i, core_ref: (k, i, 0))],
        out_specs=pl.BlockSpec((None, tr, width), lambda k, i, core_ref: (k, i, 0)))
    return pl.pallas_call(
        body, name=f"pair_add_{width}", grid_spec=grid_spec,
        out_shape=jax.ShapeDtypeStruct((nchip, rows, width), BF16),
    )(core, own, got)


def _chip_exchange(parts):
    n = len(parts)

    def build(srcs, outs, send_sems, recv_sems, local_sems):
        x, y, c = lax.axis_index("x"), lax.axis_index("y"), lax.axis_index("c")
        my_chip = 2 * x + y
        chips = [(1 - x, y), (x, 1 - y), (1 - x, 1 - y)]
        copies = [pltpu.make_async_copy(srcs[t].at[my_chip], outs[t].at[my_chip], local_sems.at[t]) for t in range(n)]
        for j, (cx, cy) in enumerate(chips):
            for t in range(n):
                k = n * j + t
                copies.append(pltpu.make_async_remote_copy(
                    src_ref=srcs[t].at[2 * cx + cy], dst_ref=outs[t].at[my_chip], send_sem=send_sems.at[k],
                    recv_sem=recv_sems.at[k], device_id=(cx, cy, c), device_id_type=MESH))
        return copies

    return dict(ins=list(parts), **_start_finish(build), outs=[jax.ShapeDtypeStruct(p.shape, p.dtype) for p in parts],
                sems=[pltpu.SemaphoreType.DMA((3 * n,)), pltpu.SemaphoreType.DMA((3 * n,)),
                      pltpu.SemaphoreType.DMA((n,))])


def _slot_sum(slots, name, core):
    ns, rows, width = slots.shape
    tr = min(ROW_TILE, rows)

    def body(core_ref, in_ref, out_ref):
        acc = in_ref[0].astype(F32)
        for s in range(1, ns):
            acc = acc + in_ref[s].astype(F32)
        out_ref[...] = acc

    grid_spec = pltpu.PrefetchScalarGridSpec(
        num_scalar_prefetch=1, grid=(rows // tr,),
        in_specs=[pl.BlockSpec((ns, tr, width), lambda i, core_ref: (0, i, 0))],
        out_specs=pl.BlockSpec((None, tr, width), lambda i, core_ref: (core_ref[0], i, 0)))
    return pl.pallas_call(
        body, name=name, grid_spec=grid_spec, out_shape=jax.ShapeDtypeStruct((2, rows, width), F32),
    )(core, slots)


def _pair_gather(bufs, small):
    n = len(bufs)

    def body(*refs):
        small_ref, outs, small_out = refs[n], refs[n + 1:2 * n + 1], refs[2 * n + 1]
        send_sems, recv_sems, local_sem = refs[2 * n + 2:]
        x, y, c = lax.axis_index("x"), lax.axis_index("y"), lax.axis_index("c")
        me = 4 * x + 2 * y + c
        chips = [(1 - x, y), (x, 1 - y), (1 - x, 1 - y)]
        mine = pltpu.make_async_copy(small_ref, small_out.at[me], local_sem)
        mine.start()
        copies = [pltpu.make_async_remote_copy(
            src_ref=outs[t].at[c], dst_ref=outs[t].at[c], send_sem=send_sems.at[t], recv_sem=recv_sems.at[t],
            device_id=(x, y, 1 - c), device_id_type=MESH) for t in range(n)]
        peers = [(x, y, 1 - c)] + [(cx, cy, cc) for (cx, cy) in chips for cc in (c, 1 - c)]
        for j, peer in enumerate(peers):
            copies.append(pltpu.make_async_remote_copy(
                src_ref=small_ref, dst_ref=small_out.at[me], send_sem=send_sems.at[n + j],
                recv_sem=recv_sems.at[n + j], device_id=peer, device_id_type=MESH))
        for cp in copies:
            cp.start()
        for cp in copies:
            cp.wait()
        mine.wait()

    any_spec = pl.BlockSpec(memory_space=pl.ANY)
    res = pl.pallas_call(
        body, name="pair_gather",
        out_shape=[jax.ShapeDtypeStruct(b.shape, b.dtype) for b in bufs]
        + [jax.ShapeDtypeStruct((8,) + small.shape, small.dtype)],
        in_specs=[any_spec] * (n + 1), out_specs=[any_spec] * (n + 1),
        input_output_aliases={t: t for t in range(n)},
        scratch_shapes=[pltpu.SemaphoreType.DMA((n + 7,)), pltpu.SemaphoreType.DMA((n + 7,)),
                        pltpu.SemaphoreType.DMA],
    )(*bufs, small)
    return [r.reshape(2 * b.shape[1], b.shape[2]) for r, b in zip(res[:n], bufs)], res[n]


def _adamw(w, g, m, v, name):
    rows, width = w.shape
    tr = min(ROW_TILE // 2, rows)

    def body(w_ref, g_ref, m_ref, v_ref, d_ref, nm_ref, nv_ref):
        d_ref[...], nm_ref[...], nv_ref[...] = _adamw_math(w_ref[...], g_ref[...], m_ref[...], v_ref[...])

    spec = pl.BlockSpec((tr, width), lambda i: (i, 0))
    return pl.pallas_call(
        body, name=name, grid=(rows // tr,), in_specs=[spec] * 4, out_specs=[spec] * 3,
        out_shape=[jax.ShapeDtypeStruct(w.shape, F32)] * 3,
    )(w, g, m, v)


def _adamw_math(w, g, m, v):
    c1 = 1.0 / (1.0 - ADAM_B1 ** ADAM_STEP)
    c2 = 1.0 / (1.0 - ADAM_B2 ** ADAM_STEP)
    nm = ADAM_B1 * m + (1.0 - ADAM_B1) * g
    nv = ADAM_B2 * v + (1.0 - ADAM_B2) * (g * g)
    return -ADAM_LR * ((nm * c1) / (jnp.sqrt(nv * c2) + ADAM_EPS) + ADAM_WD * w), nm, nv


def _small_update(slots, params, ms, vs):
    n = len(params)

    def body(slots_ref, *refs):
        w_refs, m_refs, v_refs, loss_ref = refs[:n], refs[n:2 * n], refs[2 * n:3 * n], refs[3 * n]
        g_refs, d_refs, nm_refs, nv_refs = (refs[3 * n + 1 + k * n:3 * n + 1 + (k + 1) * n] for k in range(4))
        acc = slots_ref[0]
        for s in range(1, slots.shape[0]):
            acc = acc + slots_ref[s]
        loss_ref[...] = acc[4:5, 0:1]
        grads = (acc[0:1] + acc[5:6], acc[3:4], acc[2:3], acc[1:2])
        for k in range(n):
            g = grads[k][:, :w_refs[k].shape[1]]
            g_refs[k][...] = g
            d_refs[k][...], nm_refs[k][...], nv_refs[k][...] = _adamw_math(w_refs[k][...], g, m_refs[k][...],
                                                                           v_refs[k][...])

    return pl.pallas_call(
        body, name="small_update",
        out_shape=[jax.ShapeDtypeStruct((1, 1), F32)] + [jax.ShapeDtypeStruct(p.shape, F32) for p in params] * 4,
    )(slots, *params, *ms, *vs)


def _local_step(x, mem, target, pre_norm, sink_a, mem_norm, post_norm, w_in_g, w_out, w_mkv, gathers=None,
                own=None):
    first_gather, late_gather = gathers if gathers else (None, None)
    u, ut, p_own, hosted = _pre_norm(x, pre_norm, own[1] if own else None, first_gather)
    if gathers:
        w_in_g = hosted[0].reshape(N_CHIPS, D_MODEL, SHARD_IN)
    pr = _pre_proj(u, w_in_g, (own[0], p_own) if own else None, late_gather)
    pr["ut"] = ut
    if gathers:
        w_out, w_mkv = (g.reshape(D_MODEL, g.shape[-1]) for g in pr["hosted"])
    mk, mv = _mem_kv(mem, mem_norm, w_mkv)
    sink = sink_a.reshape(-1)
    qa, ka, va = pr["qa"][None], pr["ka"][None], pr["va"][None]
    oa, lse_a = _band_fwd(qa, ka, va, sink, max_dist=A_WINDOW - 1, name="swa_fwd")
    ob_list, lseb_list = [], []
    for k, (win, dil) in enumerate(B_CONFIGS):
        o_i, l_i = _band_fwd(pr["qb"][k], pr["kb"][k], pr["vb"][k], None, max_dist=win // dil, name=f"dil{dil}_fwd")
        ob_list.append(o_i)
        lseb_list.append(l_i)
    oc, lse_c = _mem_attn_fwd(pr["qc"], mk, mv)
    sink_row = jnp.pad(sink, (0, LANES - sink.shape[0])).reshape(1, LANES)
    po = _post(x, target, post_norm, w_out, sink_row, oa[0], lse_a[0], pr["ga"], ob_list, lseb_list, pr["gb"], oc,
               pr["gc"])
    dqc, dmk, dmv = _mem_attn_bwd(pr["qc"], mk, mv, po["doc"], lse_c, po["dl_c"])
    dqa, dka, dva = _band_bwd(qa, ka, va, po["doa"][None], lse_a, po["dl_a"][None], max_dist=A_WINDOW - 1,
                              name="swa_bwd")
    res = dict(qb=[], kb=[], vb=[])
    for k, (win, dil) in enumerate(B_CONFIGS):
        dq_i, dk_i, dv_i = _band_bwd(pr["qb"][k], pr["kb"][k], pr["vb"][k], po["dob"][k], po["lse_b"][k],
                                     po["dl_b"][k], max_dist=win // dil, name=f"dil{dil}_bwd")
        res["qb"].append(dq_i)
        res["kb"].append(dk_i)
        res["vb"].append(dv_i)
    nat = dict(qa=dqa[0], ka=dka[0], va=dva[0], ga=po["dga"], gb=po["dgb"], qc=dqc, gc=po["dgc"])
    dproj, gw_in = _grad_w_in(pr["ut"], nat, res)
    gw_mkv, gmem = _mem_kv_bwd(mem, mem_norm, w_mkv, dmk, dmv)
    gsink = -po["gsink"][0, :sink.shape[0]]
    return dict(loss=po["loss"], g=po["g"], dproj=dproj, gw_in=gw_in, gw_out=po["gw_out"], gw_mkv=gw_mkv,
                gpost=po["gpost"], gmem=gmem, gsink=gsink, w_in_g=w_in_g)


def kernel(x, mem, pre_norm, w_in, sink_a, mem_norm, w_mem_kv, w_out, post_norm, loss_target, m_pre_norm, m_w_in, m_sink_a, m_mem_norm, m_w_mem_kv, m_w_out, m_post_norm, v_pre_norm, v_w_in, v_sink_a, v_mem_norm, v_w_mem_kv, v_w_out, v_post_norm):
    w_own = w_in[0]
    gathers = (_gather_exchange([w_own], prefilled=False),
               _gather_exchange([w_out[0].astype(BF16), w_mem_kv[0].astype(BF16)]))
    chip = (2 * lax.axis_index("x") + lax.axis_index("y")).astype(jnp.int32).reshape(1)
    loc = _local_step(x[0], mem[0], loss_target[0], pre_norm, sink_a, mem_norm, post_norm, None, None, None, gathers,
                      (chip, w_own))
    big = [loc["gw_in"], loc["gw_out"].reshape(N_CHIPS, D_MODEL // N_CHIPS, D_MODEL),
           loc["gw_mkv"].reshape(N_CHIPS, D_MODEL // N_CHIPS, 2 * C_W)]
    core = lax.axis_index("c").astype(jnp.int32).reshape(1)
    w_in_full = loc["w_in_g"]
    step_in = (x[0], loc["g"], pre_norm, w_in_full, loc["dproj"])
    pair_ex = _pair_exchange(big)
    pair_handles, token = _exchange_start(pair_ex, "pair_exchange_start")
    gx_a, gpre_a = _input_grad(*step_in, None, (0, 2), token, "input_grad_a")
    big, got = _exchange_wait(pair_ex, pair_handles, gpre_a, "pair_exchange_wait")
    parts = [_pair_add(core, own, g) for own, g in zip(big, got)]
    chip_ex = _chip_exchange(parts)
    chip_handles, token = _exchange_start(chip_ex, "chip_exchange_start")
    grad_x, gpre_b = _input_grad(*step_in, gx_a, (2, 14), token, "input_grad_b")
    _, slots = _exchange_wait(chip_ex, chip_handles, gpre_b, "chip_exchange_wait")
    halves = [_slot_sum(s, name=f"chip_sum_{s.shape[2]}", core=core) for s in slots]
    widen = lambda a: jnp.pad(a.reshape(1, -1), ((0, 0), (0, D_MODEL - a.size)))
    small = jnp.concatenate([gpre_a, loc["gpost"], loc["gmem"], widen(loc["gsink"]), widen(loc["loss"]), gpre_b,
                             jnp.zeros((2, D_MODEL), F32)], axis=0)
    (g_in, g_out, g_mkv), small_slots = _pair_gather(halves, small)
    (loss, g_pre, g_sink, g_mem, g_post, d_pre, d_sink, d_mem, d_post, nm_pre, nm_sink, nm_mem, nm_post,
     nv_pre, nv_sink, nv_mem, nv_post) = _small_update(
        small_slots, (pre_norm, sink_a, mem_norm, post_norm), (m_pre_norm, m_sink_a, m_mem_norm, m_post_norm),
        (v_pre_norm, v_sink_a, v_mem_norm, v_post_norm))

    d_in, nm_in, nv_in = _adamw(w_in[0], g_in, m_w_in[0], v_w_in[0], "adamw_in")
    d_out, nm_out, nv_out = _adamw(w_out[0], g_out, m_w_out[0], v_w_out[0], "adamw_out")
    d_mkv, nm_mkv, nv_mkv = _adamw(w_mem_kv[0], g_mkv, m_w_mem_kv[0], v_w_mem_kv[0], "adamw_mkv")
    lead = lambda a: a[None]
    return (loss.reshape(()), lead(grad_x),
            g_pre, lead(g_in), g_sink, g_mem, lead(g_mkv), lead(g_out), g_post,
            d_pre, lead(d_in), d_sink, d_mem, lead(d_mkv), lead(d_out), d_post,
            nm_pre, lead(nm_in), nm_sink, nm_mem, lead(nm_mkv), lead(nm_out), nm_post,
            nv_pre, lead(nv_in), nv_sink, nv_mem, lead(nv_mkv), lead(nv_out), nv_post)
```

```python
import numpy as np
import jax
import jax.numpy as jnp
from jax import lax
from jax.experimental import pallas as pl
from jax.experimental.pallas import tpu as pltpu

F32 = jnp.float32
BF16 = jnp.bfloat16

D_MODEL = 1024
HEAD_DIM = 64
LANES = 128
BLOCK = 128
ROW_TILE = 512
ATTN_TILE = 1024
INPUT_GRAD_TILES = 16
TAIL_STEPS = 4
A_W, A_KV_W, B_W, C_W = 384, 128, 384, 256
N_MEM = 256
D_IN = 3072
N_CHIPS = 4
SHARD_IN = D_IN // N_CHIPS
B_CONFIGS = ((128, 1), (512, 4), (2048, 16))
B_DILS = tuple(d for _, d in B_CONFIGS)
A_WINDOW = 128
RMS_EPS = 1e-6
ROPE_THETA = 500000.0
SCALE = HEAD_DIM ** -0.5
NEG = -1e30
ADAM_LR, ADAM_B1, ADAM_B2, ADAM_EPS, ADAM_WD, ADAM_STEP = 0.001, 0.9, 0.999, 1e-08, 0.01, 10

NT = (((1,), (1,)), ((), ()))
TN = (((0,), (0,)), ((), ()))
MESH = pl.DeviceIdType.MESH

_PROJ_LAYOUT = (
    [("qa", 128 * i, True, True) for i in range(3)] + [("ka", 0, True, False), ("va", 0, False, False)]
    + [("ga", 128 * i, False, False) for i in range(3)]
    + [("qb", 128 * i, True, True) for i in range(3)] + [("kb", 128 * i, True, False) for i in range(3)]
    + [("vb", 128 * i, False, False) for i in range(3)] + [("gb", 128 * i, False, False) for i in range(3)]
    + [("qc", 128 * i, False, True) for i in range(2)] + [("gc", 128 * i, False, False) for i in range(2)]
)
_PROJ_WIDTH = dict(qa=A_W, ka=A_KV_W, va=A_KV_W, ga=A_W, qb=B_W, kb=B_W, vb=B_W, gb=B_W, qc=C_W, gc=C_W)
_NATURAL = ("qa", "ka", "va", "ga", "gb", "qc", "gc")
_DILATED = ("qb", "kb", "vb")


def _dot(a, b):
    return jnp.dot(a, b, preferred_element_type=F32)


def _dot_nt(a, b):
    return lax.dot_general(a, b, NT, preferred_element_type=F32)


def _dot_tn(a, b):
    return lax.dot_general(a, b, TN, preferred_element_type=F32)


def _half_masks(rows):
    lane = lax.broadcasted_iota(jnp.int32, (rows, LANES), 1)
    return lane < HEAD_DIM, lane >= HEAD_DIM


def _rope(t, c, sm, sp):
    return t * c + pltpu.roll(t, LANES - 8, 1) * sm + pltpu.roll(t, 8, 1) * sp


def _rope_tables(seq, tm):
    dim = np.arange(LANES) % HEAD_DIM
    inv_freq = (np.float32(ROPE_THETA) ** (-np.arange(0, 16, 2, dtype=np.float32) / np.float32(16))).astype(np.float64)
    freq = np.where(dim < 16, inv_freq[dim % 8], 0.0)[None, :]
    local = np.arange(tm, dtype=np.float64)[:, None] * freq
    base = (np.arange(seq // tm, dtype=np.float64) * tm)[:, None] * freq
    both = lambda a: np.concatenate([np.cos(a), np.sin(a)], axis=1).astype(np.float32)
    return jnp.asarray(both(local)), jnp.asarray(np.repeat(both(base), 8, axis=0))


def _rope_coeffs(local_ref, base_ref):
    cl, sl = local_ref[:, :LANES], local_ref[:, LANES:]
    cb, sb = base_ref[0:1, :LANES], base_ref[0:1, LANES:]
    cos = cb * cl - sb * sl
    sin = sb * cl + cb * sl
    dim = lax.broadcasted_iota(jnp.int32, (1, LANES), 1) % HEAD_DIM
    return cos, jnp.where(dim < 8, -sin, 0.0), jnp.where((dim >= 8) & (dim < 16), sin, 0.0)


def _split3(x):
    a = x.astype(BF16)
    r = x - a.astype(F32)
    b = r.astype(BF16)
    c = (r - b.astype(F32)).astype(BF16)
    return a, b, c


def _rows_to_lanes(x):
    row = lax.broadcasted_iota(jnp.int32, (8, LANES), 0)
    lane = lax.broadcasted_iota(jnp.int32, (8, LANES), 1)
    eye = (row == lane).astype(BF16)
    a, b, c = _split3(x)
    return _dot_nt(eye, a) + _dot_nt(eye, b) + _dot_nt(eye, c)


def _head_sum_matrix(width):
    k = lax.broadcasted_iota(jnp.int32, (width, LANES), 0)
    h = lax.broadcasted_iota(jnp.int32, (width, LANES), 1)
    return (k // HEAD_DIM == h).astype(BF16)


def _head_expand_matrix(width):
    h = lax.broadcasted_iota(jnp.int32, (LANES, width), 0)
    k = lax.broadcasted_iota(jnp.int32, (LANES, width), 1)
    return (k // HEAD_DIM == h).astype(BF16)


def _dot_split(x, mat, terms):
    parts = _split3(x)[:terms]
    out = _dot(parts[0], mat)
    for p in parts[1:]:
        out = out + _dot(p, mat)
    return out


def _per_head(cols, fill=0.0):
    rows = cols[0].shape[0]
    lane = lax.broadcasted_iota(jnp.int32, (rows, LANES), 1)
    out = jnp.full((rows, LANES), fill, F32)
    for h, col in enumerate(cols):
        out = jnp.where(lane == h, col, out)
    return out


def _lane_blocks(width):
    return [slice(p * LANES, (p + 1) * LANES) for p in range(width // LANES)]


def _stage(rows, width):
    return pltpu.VMEM((width // LANES, rows, LANES), F32)


def _stage_write(buf, value):
    for p, lanes in enumerate(_lane_blocks(value.shape[1])):
        buf[p] = value[:, lanes]


def _stage_read(buf):
    return jnp.concatenate([buf[p] for p in range(buf.shape[0])], axis=1) if buf.shape[0] > 1 else buf[0]


def _to_residues(buf, out_ref, dil):
    rows = buf.shape[1] // dil
    for r in range(dil):
        for p in range(buf.shape[0]):
            plane = buf.at[p]
            out_ref[r, :, p * LANES:(p + 1) * LANES] = plane[pl.ds(r, rows, stride=dil), :].astype(out_ref.dtype)


def _from_residues(in_ref, buf, dil):
    rows = buf.shape[1] // dil
    for r in range(dil):
        for p in range(buf.shape[0]):
            plane = buf.at[p]
            plane[pl.ds(r, rows, stride=dil), :] = in_ref[r, :, p * LANES:(p + 1) * LANES].astype(F32)


def _residue_spec(dil, tm, width):
    return pl.BlockSpec((dil, tm // dil, width), lambda i: (0, i, 0))


def _gather_exchange(shards_2d, prefilled=True):
    shards = tuple(jax.ShapeDtypeStruct((2, s.shape[0] // 2, s.shape[1]), BF16) for s in shards_2d)
    n = len(shards)

    def copies(in_refs, out_refs, send_sems, recv_sems, *local_sems):
        srcs, outs = in_refs[:n], out_refs
        x, y, c = lax.axis_index("x"), lax.axis_index("y"), lax.axis_index("c")
        my_chip = 2 * x + y
        sibling = (x, y, 1 - c)
        chips = [(1 - x, y), (x, 1 - y), (1 - x, 1 - y)]

        def copy(k, src, dst, to):
            return pltpu.make_async_remote_copy(src_ref=src, dst_ref=dst, send_sem=send_sems.at[k],
                                                recv_sem=recv_sems.at[k], device_id=to, device_id_type=MESH)

        first, arrive, passed, sibling_arrive = [], [], [], []
        for j, (cx, cy) in enumerate(chips):
            chip = 2 * cx + cy
            for t in range(n):
                k = n * j + t
                first.append(copy(k, srcs[t].at[c], outs[t].at[my_chip, c], (cx, cy, c)))
                arrive.append(copy(k, srcs[t].at[c], outs[t].at[chip, c], (cx, cy, c)))
                passed.append(copy(n * 3 + k, outs[t].at[chip, c], outs[t].at[chip, c], sibling))
                sibling_arrive.append(copy(n * 3 + k, outs[t].at[chip, 1 - c], outs[t].at[chip, 1 - c], sibling))
        own = [pltpu.make_async_copy(srcs[t], outs[t].at[my_chip], local_sems[0].at[t]) for t in range(n)
               ] if local_sems else []
        return first, arrive, passed, sibling_arrive, own

    def start(*refs):
        first, _, _, _, own = copies(*refs)
        for cp in first + own:
            cp.start()

    def forward(refs, senders):
        _, arrive, passed, _, _ = copies(*refs)
        for j in senders:
            for k in range(n * j, n * (j + 1)):
                arrive[k].wait_recv()
                passed[k].start()

    def mid(*refs):
        forward(refs, (0, 1))

    def finish(*refs):
        forward(refs, (2,))
        first, _, passed, sibling_arrive, own = copies(*refs)
        for cp in sibling_arrive:
            cp.wait_recv()
        for cp in first + passed:
            cp.wait_send()
        for cp in own:
            cp.wait()

    ex = dict(ins=[], start=start, mid=mid, finish=finish, prefilled=prefilled,
              outs=[jax.ShapeDtypeStruct((N_CHIPS,) + s.shape, s.dtype) for s in shards],
              sems=[pltpu.SemaphoreType.DMA((6 * n,)), pltpu.SemaphoreType.DMA((6 * n,))])
    if prefilled:
        my_chip = 2 * lax.axis_index("x") + lax.axis_index("y")
        halves = [a.reshape(s.shape) for a, s in zip(shards_2d, shards)]
        landing = [lax.dynamic_update_slice(jnp.zeros((N_CHIPS,) + s.shape, s.dtype), a[None], (my_chip, 0, 0, 0))
                   for a, s in zip(halves, shards)]
        ex.update(ins=halves + landing, aliases={n + t: t for t in range(n)})
    else:
        ex["sems"].append(pltpu.SemaphoreType.DMA((n,)))
    return ex


def _mem_kv(mem, mem_norm, w_mkv):
    def body(mem_ref, g_ref, w_ref, mk_ref, mv_ref):
        m = mem_ref[...]
        r = lax.rsqrt(jnp.mean(m * m, axis=-1, keepdims=True) + RMS_EPS)
        mn = (m * r * g_ref[...]).astype(BF16)
        kv = _dot(mn, w_ref[...])
        mk_ref[...] = kv[:, :C_W].astype(BF16)
        mv_ref[...] = kv[:, C_W:].astype(BF16)

    return pl.pallas_call(
        body, name="mem_kv",
        out_shape=[jax.ShapeDtypeStruct((N_MEM, C_W), BF16)] * 2,
    )(mem, mem_norm, w_mkv)


def _mem_kv_bwd(mem, mem_norm, w_mkv, dmk, dmv):
    def body(mem_ref, g_ref, w_ref, dmk_ref, dmv_ref, gw_ref, gn_ref):
        m = mem_ref[...]
        r = lax.rsqrt(jnp.mean(m * m, axis=-1, keepdims=True) + RMS_EPS)
        mhat = m * r
        mn = (mhat * g_ref[...]).astype(BF16)
        dkv = jnp.concatenate([dmk_ref[...], dmv_ref[...]], axis=1).astype(BF16)
        gw_ref[...] = _dot_tn(mn, dkv)
        dmn = _dot_nt(dkv, w_ref[...])
        gn_ref[...] = jnp.sum(dmn * mhat, axis=0, keepdims=True)

    return pl.pallas_call(
        body, name="mem_kv_bwd",
        out_shape=[jax.ShapeDtypeStruct((D_MODEL, 2 * C_W), F32), jax.ShapeDtypeStruct((1, D_MODEL), F32)],
    )(mem, mem_norm, w_mkv, dmk, dmv)


def _host_phases(host, in_refs, out_refs, sems, steps, before):
    if not host:
        return
    step = pl.program_id(0)
    phases = [("start", 0)] if before else [("mid", max(steps - 3, 0)), ("finish", steps - 1)]
    for phase, at in phases:
        pl.when(step == at)(lambda phase=phase: host[phase](in_refs, out_refs, *sems))


def _pre_norm(x, pre_norm, w_own=None, host=None):
    seq = x.shape[0]
    tm = min(ROW_TILE, seq)
    n_own_in = 2 if w_own is None else 3
    n_own_out = n_own_in
    n_host_in = len(host["ins"]) if host else 0
    n_host_out = len(host["outs"]) if host else 0
    half = D_MODEL // 2

    def body(x_ref, g_ref, *refs):
        w_ref = None if w_own is None else refs[0]
        refs = refs[n_own_in - 2:]
        host_in, own_out, refs = refs[:n_host_in], refs[n_host_in:n_host_in + n_own_out], refs[n_host_in + n_own_out:]
        host_out, refs = refs[:n_host_out], refs[n_host_out:]
        if w_own is not None:
            wb, sems = refs[0], refs[1:]

            @pl.when(pl.program_id(0) == 0)
            def _():
                for h in range(2):
                    wb[h] = w_ref[h * half:(h + 1) * half, :].astype(BF16)
            if host and not host["prefilled"]:
                host_in = [wb]
        else:
            sems = refs
        _host_phases(host, host_in, host_out, sems, seq // tm, before=True)

        xv = x_ref[...]
        r = lax.rsqrt(jnp.mean(xv * xv, axis=-1, keepdims=True) + RMS_EPS)
        u = xv * r * g_ref[...]
        ub = u.astype(BF16)
        own_out[0][...] = ub
        own_out[1][...] = u.T.astype(BF16)
        if w_own is not None:
            own_out[2][...] = _dot(ub[:, :half], wb[0]) + _dot(ub[:, half:], wb[1])
        _host_phases(host, host_in, host_out, sems, seq // tm, before=False)

    any_spec = pl.BlockSpec(memory_space=pl.ANY)
    ins = [x, pre_norm]
    in_specs = [pl.BlockSpec((tm, D_MODEL), lambda i: (i, 0)), pl.BlockSpec(pre_norm.shape, lambda i: (0, 0))]
    out_shape = [jax.ShapeDtypeStruct((seq, D_MODEL), BF16), jax.ShapeDtypeStruct((D_MODEL, seq), BF16)]
    out_specs = [pl.BlockSpec((tm, D_MODEL), lambda i: (i, 0)), pl.BlockSpec((D_MODEL, tm), lambda i: (0, i))]
    aliases, scratch = {}, []
    if w_own is not None:
        ins.append(w_own)
        in_specs.append(pl.BlockSpec(w_own.shape, lambda i: (0, 0)))
        out_shape.append(jax.ShapeDtypeStruct((seq, w_own.shape[1]), F32))
        out_specs.append(pl.BlockSpec((tm, w_own.shape[1]), lambda i: (i, 0)))
        scratch.append(pltpu.VMEM((2, half, w_own.shape[1]), BF16))
    if host:
        aliases = {len(ins) + k: n_own_out + v for k, v in host.get("aliases", {}).items()}
        ins += list(host["ins"])
        in_specs += [any_spec] * n_host_in
        out_shape += list(host["outs"])
        out_specs += [any_spec] * n_host_out
        scratch += list(host["sems"])
    res = pl.pallas_call(
        body, name="pre_norm", grid=(seq // tm,), in_specs=in_specs, out_specs=out_specs, out_shape=out_shape,
        input_output_aliases=aliases, scratch_shapes=scratch,
        compiler_params=pltpu.CompilerParams(dimension_semantics=("arbitrary",)),
    )(*ins)
    return res[0], res[1], (None if w_own is None else res[2]), res[n_own_out:]


def _pre_proj(u, w_in_g, own=None, host=None):
    seq = u.shape[0]
    tm = min(ROW_TILE, seq)
    n_nat, n_dil = len(_NATURAL), len(_DILATED) * len(B_DILS)
    rope = _rope_tables(seq, tm)

    n_host_in = len(host["ins"]) if host else 0
    n_host_out = len(host["outs"]) if host else 0
    n_own_out = n_nat + n_dil

    def body(u_ref, w_ref, rl_ref, rb_ref, *refs):
        if own:
            (chip_ref, pown_ref), refs = refs[:2], refs[2:]
        host_in, refs = refs[:n_host_in], refs[n_host_in:]
        nat = dict(zip(_NATURAL, refs[:n_nat]))
        res = {n: refs[n_nat + len(B_DILS) * k:n_nat + len(B_DILS) * (k + 1)] for k, n in enumerate(_DILATED)}
        host_out = refs[n_own_out:n_own_out + n_host_out]
        bufs = dict(zip(_DILATED, refs[n_own_out + n_host_out:]))
        sems = refs[n_own_out + n_host_out + len(_DILATED):]
        _host_phases(host, host_in, host_out, sems, seq // tm, before=True)

        def project(own_chip):
            ub = u_ref[...]
            c, sm, sp = _rope_coeffs(rl_ref, rb_ref)
            for j in range(N_CHIPS):
                pj = pown_ref[...] if j == own_chip else _dot(ub, w_ref[j])
                for b in range(SHARD_IN // LANES):
                    name, off, roped, scaled = _PROJ_LAYOUT[(SHARD_IN // LANES) * j + b]
                    piece = pj[:, LANES * b:LANES * (b + 1)]
                    if roped:
                        piece = _rope(piece, c, sm, sp)
                    if scaled:
                        piece = piece * SCALE
                    if name in bufs:
                        bufs[name][off // LANES] = piece
                    else:
                        nat[name][:, off:off + LANES] = piece.astype(BF16)
            for name in _DILATED:
                for ref, dil in zip(res[name], B_DILS):
                    _to_residues(bufs[name], ref, dil)

        if own:
            for chip in range(N_CHIPS):
                pl.when(chip_ref[0] == chip)(lambda chip=chip: project(chip))
        else:
            project(None)
        _host_phases(host, host_in, host_out, sems, seq // tm, before=False)

    row = lambda w: pl.BlockSpec((tm, w), lambda i: (i, 0))
    full = lambda a: pl.BlockSpec(a.shape, lambda i: (0,) * a.ndim)
    any_spec = pl.BlockSpec(memory_space=pl.ANY)
    out_shape = [jax.ShapeDtypeStruct((seq, _PROJ_WIDTH[n]), BF16) for n in _NATURAL]
    out_specs = [row(_PROJ_WIDTH[n]) for n in _NATURAL]
    for n in _DILATED:
        for dil in B_DILS:
            out_shape.append(jax.ShapeDtypeStruct((dil, seq // dil, B_W), BF16))
            out_specs.append(_residue_spec(dil, tm, B_W))
    ins = [u, w_in_g, *rope]
    in_specs = [row(D_MODEL), full(w_in_g), full(rope[0]), pl.BlockSpec((8, 2 * LANES), lambda i: (i, 0))]
    if own:
        ins += list(own)
        in_specs += [pl.BlockSpec(memory_space=pltpu.SMEM), row(SHARD_IN)]
    scratch = [_stage(tm, B_W)] * len(_DILATED)
    aliases = {}
    if host:
        aliases = {len(ins) + k: n_own_out + v for k, v in host.get("aliases", {}).items()}
        ins += list(host["ins"])
        in_specs += [any_spec] * n_host_in
        out_shape += list(host["outs"])
        out_specs += [any_spec] * n_host_out
        scratch += list(host["sems"])
    res = pl.pallas_call(
        body, name="pre_proj", grid=(seq // tm,), in_specs=in_specs, out_specs=out_specs, out_shape=out_shape,
        input_output_aliases=aliases, scratch_shapes=scratch,
        compiler_params=pltpu.CompilerParams(dimension_semantics=("arbitrary",)),
    )(*ins)
    out = dict(zip(_NATURAL, res[:n_nat]))
    for k, n in enumerate(_DILATED):
        out[n] = res[n_nat + len(B_DILS) * k:n_nat + len(B_DILS) * (k + 1)]
    out["hosted"] = res[n_own_out:]
    return out


def _band_bias(max_dist, transposed):
    i = np.arange(BLOCK)[:, None]
    j = np.arange(BLOCK)[None, :]
    if transposed:
        same = i <= j
        other = (j + BLOCK - i) <= max_dist
        vis = np.concatenate([same, other], axis=1)
    else:
        prev = (i + BLOCK - j) <= max_dist
        same = j <= i
        vis = np.concatenate([prev, same], axis=1)
    return jnp.asarray(np.where(vis, 0.0, NEG).astype(np.float32))


def _kv_place(h, gqa):
    return (0, h // 3) if gqa else (h // 2, h % 2)


def _band_fwd(q, k, v, sink, *, max_dist, name):
    dil, length, wq = q.shape
    wk = k.shape[2]
    gqa = wk != wq
    tq = min(ATTN_TILE, length)
    ns, nt = tq // BLOCK, length // tq
    npair = wq // LANES
    bias = _band_bias(max_dist, transposed=False)
    has_sink = sink is not None

    def body(*refs):
        if has_sink:
            sink_ref, refs = refs[0], refs[1:]
        q_ref, k_ref, kp_ref, v_ref, vp_ref, bias_ref, o_ref, lse_ref, kbuf, vbuf = refs[:10]
        i = pl.program_id(1)
        kbuf[0:BLOCK] = kp_ref[...]
        kbuf[BLOCK:] = k_ref[...]
        vbuf[0:BLOCK] = vp_ref[...]
        vbuf[BLOCK:] = v_ref[...]
        if gqa:
            kroll, vroll = refs[10:12]
            kroll[...] = pltpu.roll(kbuf[...], HEAD_DIM, 1)
            vroll[...] = pltpu.roll(vbuf[...], HEAD_DIM, 1)
        half = _half_masks(BLOCK)
        col_prev = (lax.broadcasted_iota(jnp.int32, (1, 2 * BLOCK), 1) < BLOCK).astype(F32)

        def score_matmuls(a):
            scores = []
            for p in range(npair):
                qp = q_ref[a * BLOCK:(a + 1) * BLOCK, p * LANES:(p + 1) * LANES]
                for e in range(2):
                    pk, ek = _kv_place(2 * p + e, gqa)
                    kw = (kbuf if ek == e else kroll)[a * BLOCK:(a + 2) * BLOCK, pk * LANES:(pk + 1) * LANES]
                    scores.append(_dot_nt(jnp.where(half[e], qp, jnp.zeros_like(qp)), kw))
            return scores

        pending = score_matmuls(0)
        for a in range(ns):
            r0 = a * BLOCK
            b = bias_ref[...]
            if a == 0:
                b = b + jnp.where(i == 0, NEG, 0.0) * col_prev
            scores = pending
            m_cols, l_cols, probs = [], [], []
            for h, s in enumerate(scores):
                s = s + b
                m = jnp.max(s, axis=1, keepdims=True)
                if has_sink:
                    m = jnp.maximum(m, sink_ref[h])
                pe = jnp.exp(s - m)
                l = jnp.sum(pe, axis=1, keepdims=True)
                if has_sink:
                    l = l + jnp.exp(sink_ref[h] - m)
                probs.append(pe.astype(BF16))
                m_cols.append(m)
                l_cols.append(l)
            pending = score_matmuls(a + 1) if a + 1 < ns else None
            for p in range(npair):
                o_h = []
                for e in range(2):
                    h = 2 * p + e
                    pk, ek = _kv_place(h, gqa)
                    vw = (vbuf if ek == e else vroll)[r0:r0 + 2 * BLOCK, pk * LANES:(pk + 1) * LANES]
                    o_h.append(_dot(probs[h], vw) * (1.0 / l_cols[h]))
                o_ref[r0:r0 + BLOCK, p * LANES:(p + 1) * LANES] = jnp.where(half[0], o_h[0], o_h[1]).astype(BF16)
            lse_ref[r0:r0 + BLOCK, :] = _per_head(m_cols) + jnp.log(_per_head(l_cols, 1.0))

    main = lambda w: pl.BlockSpec((None, tq, w), lambda r, i: (r, i, 0))
    prev = lambda w: pl.BlockSpec((None, BLOCK, w), lambda r, i: (r, jnp.maximum(i * ns - 1, 0), 0))
    in_specs = [main(wq), main(wk), prev(wk), main(wk), prev(wk), pl.BlockSpec(bias.shape, lambda r, i: (0, 0))]
    args = [q, k, k, v, v, bias]
    if has_sink:
        in_specs = [pl.BlockSpec(memory_space=pltpu.SMEM)] + in_specs
        args = [sink] + args
    scratch = [pltpu.VMEM((tq + BLOCK, wk), BF16)] * (4 if gqa else 2)
    return pl.pallas_call(
        body, name=name, grid=(dil, nt), in_specs=in_specs,
        out_specs=[main(wq), main(LANES)],
        out_shape=[jax.ShapeDtypeStruct((dil, length, wq), BF16), jax.ShapeDtypeStruct((dil, length, LANES), F32)],
        scratch_shapes=scratch,
    )(*args)


def _band_bwd(q, k, v, do, lse, delta, *, max_dist, name):
    dil, length, wq = q.shape
    wk = k.shape[2]
    gqa = wk != wq
    tq = min(ATTN_TILE, length)
    ns, nt = tq // BLOCK, length // tq
    npair = wq // LANES
    nblocks = length // BLOCK
    bias = _band_bias(max_dist, transposed=True)

    def body(q_ref, qn_ref, do_ref, don_ref, lse_ref, lsen_ref, dl_ref, dln_ref, k_ref, v_ref, bias_ref,
             dq_ref, dk_ref, dv_ref, stat_l, stat_d, dqt, kt, *rolled):
        i = pl.program_id(1)
        for pk in range(wk // LANES):
            kt[pk] = k_ref[:, pk * LANES:(pk + 1) * LANES].astype(F32).T.astype(BF16)
        if gqa:
            kroll, vroll, ktroll = rolled
            kroll[...] = pltpu.roll(k_ref[...], HEAD_DIM, 1)
            vroll[...] = pltpu.roll(v_ref[...], HEAD_DIM, 1)
            ktroll[0] = kroll[...].astype(F32).T.astype(BF16)
        for a in range(ns):
            rows = slice(a * BLOCK, (a + 1) * BLOCK)
            stat_l[a] = _rows_to_lanes(lse_ref[rows, :])
            stat_d[a] = _rows_to_lanes(dl_ref[rows, :])
        stat_l[ns] = _rows_to_lanes(lsen_ref[...])
        stat_d[ns] = _rows_to_lanes(dln_ref[...])

        @pl.when(i == 0)
        def _():
            dqt[:, :, 0:BLOCK] = jnp.zeros((npair, LANES, BLOCK), F32)

        @pl.when(i > 0)
        def _():
            dqt[:, :, 0:BLOCK] = dqt[:, :, tq:tq + BLOCK]

        dqt[:, :, BLOCK:] = jnp.zeros((npair, LANES, tq), F32)
        half2 = _half_masks(2 * BLOCK)
        row = lax.broadcasted_iota(jnp.int32, (LANES, BLOCK), 0)
        row_half = (row < HEAD_DIM, row >= HEAD_DIM)
        col_next = (lax.broadcasted_iota(jnp.int32, (1, 2 * BLOCK), 1) >= BLOCK).astype(F32)

        def scores(b):
            rows = slice(b * BLOCK, (b + 1) * BLOCK)
            nxt_rows = slice((b + 1) * BLOCK, (b + 2) * BLOCK)
            items = []
            for p in range(npair):
                lanes = slice(p * LANES, (p + 1) * LANES)
                q_next = q_ref[nxt_rows, lanes] if b + 1 < ns else qn_ref[:, lanes]
                do_next = do_ref[nxt_rows, lanes] if b + 1 < ns else don_ref[:, lanes]
                qw = jnp.concatenate([q_ref[rows, lanes], q_next], axis=0)
                dow = jnp.concatenate([do_ref[rows, lanes], do_next], axis=0)
                for e in range(2):
                    h = 2 * p + e
                    pk, ek = _kv_place(h, gqa)
                    klanes = slice(pk * LANES, (pk + 1) * LANES)
                    kb = (k_ref if ek == e else kroll)[rows, klanes]
                    vb = (v_ref if ek == e else vroll)[rows, klanes]
                    qm = jnp.where(half2[e], qw, jnp.zeros_like(qw))
                    dom = jnp.where(half2[e], dow, jnp.zeros_like(dow))
                    items.append(dict(p=p, e=e, h=h, pk=pk, ek=ek, qm=qm, dom=dom,
                                      st=_dot_nt(kb, qm), dpt=_dot_nt(vb, dom)))
            return items

        def probs(b, items):
            bt = bias_ref[...]
            if b == ns - 1:
                bt = bt + jnp.where(i == nt - 1, NEG, 0.0) * col_next
            for it in items:
                h = it["h"]
                lrow = jnp.concatenate([stat_l[b, h:h + 1, :], stat_l[b + 1, h:h + 1, :]], axis=1)
                drow = jnp.concatenate([stat_d[b, h:h + 1, :], stat_d[b + 1, h:h + 1, :]], axis=1)
                pt = jnp.exp(it["st"] + bt - lrow)
                it["ptb"] = pt.astype(BF16)
                it["dsb"] = (pt * (it["dpt"] - drow)).astype(BF16)

        pending = scores(0)
        for b in range(ns):
            rows = slice(b * BLOCK, (b + 1) * BLOCK)
            window = slice(b * BLOCK, (b + 2) * BLOCK)
            acc = {}
            items = pending
            probs(b, items)
            pending = scores(b + 1) if b + 1 < ns else None
            for p in range(npair):
                pair = items[2 * p:2 * p + 2]
                lanes = slice(p * LANES, (p + 1) * LANES)
                kparts = []
                for it in pair:
                    kbt = (kt if it["ek"] == it["e"] else ktroll)[it["pk"], :, rows]
                    kparts.append(jnp.where(row_half[it["e"]], kbt, jnp.zeros_like(kbt)))
                ds_keys = jnp.concatenate([it["dsb"] for it in pair], axis=0)
                dqt[p, :, window] += _dot(jnp.concatenate(kparts, axis=1), ds_keys)
                if not gqa:
                    q_both = jnp.concatenate([it["qm"] for it in pair], axis=0)
                    do_both = jnp.concatenate([it["dom"] for it in pair], axis=0)
                    dk_ref[rows, lanes] = _dot(jnp.concatenate([it["dsb"] for it in pair], axis=1), q_both).astype(BF16)
                    dv_ref[rows, lanes] = _dot(jnp.concatenate([it["ptb"] for it in pair], axis=1), do_both).astype(BF16)
                else:
                    for it in pair:
                        dv_c = _dot(it["ptb"], it["dom"])
                        dk_c = _dot(it["dsb"], it["qm"])
                        key = (it["pk"], it["ek"] == it["e"])
                        if key in acc:
                            acc[key] = (acc[key][0] + dk_c, acc[key][1] + dv_c)
                        else:
                            acc[key] = (dk_c, dv_c)
            if gqa:
                dk_al, dv_al = acc[(0, True)]
                dk_mis, dv_mis = acc[(0, False)]
                dk_ref[rows, :] = (dk_al + pltpu.roll(dk_mis, HEAD_DIM, 1)).astype(BF16)
                dv_ref[rows, :] = (dv_al + pltpu.roll(dv_mis, HEAD_DIM, 1)).astype(BF16)

        for p in range(npair):
            dq_ref[:, p * LANES:(p + 1) * LANES] = dqt[p, :, 0:tq].T.astype(BF16)

    main = lambda w: pl.BlockSpec((None, tq, w), lambda r, i: (r, i, 0))
    nxt = lambda w: pl.BlockSpec((None, BLOCK, w), lambda r, i: (r, jnp.minimum((i + 1) * ns, nblocks - 1), 0))
    scratch = [pltpu.VMEM((ns + 1, 8, LANES), F32), pltpu.VMEM((ns + 1, 8, LANES), F32),
               pltpu.VMEM((npair, LANES, tq + BLOCK), F32), pltpu.VMEM((wk // LANES, LANES, tq), BF16)]
    if gqa:
        scratch = scratch + [pltpu.VMEM((tq, wk), BF16)] * 2 + [pltpu.VMEM((1, LANES, tq), BF16)]
    return pl.pallas_call(
        body, name=name, grid=(dil, nt),
        in_specs=[main(wq), nxt(wq), main(wq), nxt(wq), main(LANES), nxt(LANES), main(LANES), nxt(LANES),
                  main(wk), main(wk), pl.BlockSpec(bias.shape, lambda r, i: (0, 0))],
        out_specs=[main(wq), main(wk), main(wk)],
        out_shape=[jax.ShapeDtypeStruct((dil, length, wq), BF16), jax.ShapeDtypeStruct((dil, length, wk), BF16),
                   jax.ShapeDtypeStruct((dil, length, wk), BF16)],
        scratch_shapes=scratch,
        compiler_params=pltpu.CompilerParams(dimension_semantics=("arbitrary", "arbitrary")),
    )(q, q, do, do, lse, lse, delta, delta, k, v, bias)


def _mem_attn_fwd(q, mk, mv):
    seq = q.shape[0]
    tq = min(ATTN_TILE, seq)
    sub_rows = min(4 * BLOCK, tq)
    ns = tq // sub_rows

    def body(q_ref, mk_ref, mv_ref, o_ref, lse_ref):
        half = _half_masks(sub_rows)

        def sub(a, carry):
            r0 = pl.multiple_of(a * sub_rows, sub_rows)
            scores = []
            for p in range(C_W // LANES):
                lanes = slice(p * LANES, (p + 1) * LANES)
                qp = q_ref[pl.ds(r0, sub_rows), lanes]
                for e in range(2):
                    scores.append(_dot_nt(jnp.where(half[e], qp, jnp.zeros_like(qp)), mk_ref[:, lanes]))
            m_cols, l_cols, probs = [], [], []
            for s in scores:
                m = jnp.max(s, axis=1, keepdims=True)
                pe = jnp.exp(s - m)
                probs.append(pe.astype(BF16))
                m_cols.append(m)
                l_cols.append(jnp.sum(pe, axis=1, keepdims=True))
            for p in range(C_W // LANES):
                lanes = slice(p * LANES, (p + 1) * LANES)
                o_h = [_dot(probs[2 * p + e], mv_ref[:, lanes]) * (1.0 / l_cols[2 * p + e]) for e in range(2)]
                o_ref[pl.ds(r0, sub_rows), lanes] = jnp.where(half[0], o_h[0], o_h[1]).astype(BF16)
            lse_ref[pl.ds(r0, sub_rows), :] = _per_head(m_cols) + jnp.log(_per_head(l_cols, 1.0))
            return carry

        lax.fori_loop(0, ns, sub, 0, unroll=True)

    row = lambda w: pl.BlockSpec((tq, w), lambda i: (i, 0))
    full = pl.BlockSpec((N_MEM, C_W), lambda i: (0, 0))
    return pl.pallas_call(
        body, name="mem_attn_fwd", grid=(seq // tq,), in_specs=[row(C_W), full, full],
        out_specs=[row(C_W), row(LANES)],
        out_shape=[jax.ShapeDtypeStruct((seq, C_W), BF16), jax.ShapeDtypeStruct((seq, LANES), F32)],
    )(q, mk, mv)


def _mem_attn_bwd(q, mk, mv, do, lse, delta):
    seq = q.shape[0]
    tq = min(ATTN_TILE, seq)
    ns = tq // BLOCK
    npair = C_W // LANES

    def body(q_ref, mk_ref, mv_ref, do_ref, lse_ref, dl_ref, dq_ref, dmk_ref, dmv_ref, stat_l, stat_d, mkt, dqt):
        @pl.when(pl.program_id(0) == 0)
        def _():
            dmk_ref[...] = jnp.zeros_like(dmk_ref)
            dmv_ref[...] = jnp.zeros_like(dmv_ref)
            for p in range(npair):
                mkt[p] = mk_ref[:, p * LANES:(p + 1) * LANES].astype(F32).T.astype(BF16)

        for a in range(ns):
            rows = slice(a * BLOCK, (a + 1) * BLOCK)
            stat_l[a] = _rows_to_lanes(lse_ref[rows, :])
            stat_d[a] = _rows_to_lanes(dl_ref[rows, :])
        span = min(2, ns)
        half = _half_masks(span * BLOCK)
        row = lax.broadcasted_iota(jnp.int32, (LANES, N_MEM), 0)
        row_half = (row < HEAD_DIM, row >= HEAD_DIM)

        for a in range(0, ns, span):
            rows = slice(a * BLOCK, (a + span) * BLOCK)
            items = []
            for p in range(npair):
                lanes = slice(p * LANES, (p + 1) * LANES)
                qp = q_ref[rows, lanes]
                dop = do_ref[rows, lanes]
                for e in range(2):
                    qm = jnp.where(half[e], qp, jnp.zeros_like(qp))
                    dom = jnp.where(half[e], dop, jnp.zeros_like(dop))
                    items.append(dict(p=p, e=e, qm=qm, dom=dom, st=_dot_nt(mk_ref[:, lanes], qm),
                                      dpt=_dot_nt(mv_ref[:, lanes], dom)))
            for it in items:
                h = 2 * it["p"] + it["e"]
                lrow = jnp.concatenate([stat_l[a + k, h:h + 1, :] for k in range(span)], axis=1)
                drow = jnp.concatenate([stat_d[a + k, h:h + 1, :] for k in range(span)], axis=1)
                pt = jnp.exp(it["st"] - lrow)
                it["ptb"] = pt.astype(BF16)
                it["dsb"] = (pt * (it["dpt"] - drow)).astype(BF16)
            for p in range(npair):
                lanes = slice(p * LANES, (p + 1) * LANES)
                pair = [it for it in items if it["p"] == p]
                join = lambda name, axis: jnp.concatenate([it[name] for it in pair], axis=axis)
                dmv_ref[:, lanes] += _dot(join("ptb", 1), join("dom", 0))
                dmk_ref[:, lanes] += _dot(join("dsb", 1), join("qm", 0))
                kbt = mkt[p]
                k_both = jnp.concatenate([jnp.where(row_half[e], kbt, jnp.zeros_like(kbt)) for e in range(2)], axis=1)
                dqt[p, :, rows] = _dot(k_both, join("dsb", 0))
        for p in range(npair):
            dq_ref[:, p * LANES:(p + 1) * LANES] = dqt[p].T.astype(BF16)

    row = lambda w: pl.BlockSpec((tq, w), lambda i: (i, 0))
    full = pl.BlockSpec((N_MEM, C_W), lambda i: (0, 0))
    return pl.pallas_call(
        body, name="mem_attn_bwd", grid=(seq // tq,),
        in_specs=[row(C_W), full, full, row(C_W), row(LANES), row(LANES)], out_specs=[row(C_W), full, full],
        out_shape=[jax.ShapeDtypeStruct((seq, C_W), BF16), jax.ShapeDtypeStruct((N_MEM, C_W), F32),
                   jax.ShapeDtypeStruct((N_MEM, C_W), F32)],
        scratch_shapes=[pltpu.VMEM((ns, 8, LANES), F32)] * 2
        + [pltpu.VMEM((npair, LANES, N_MEM), BF16), pltpu.VMEM((npair, LANES, tq), F32)],
        compiler_params=pltpu.CompilerParams(dimension_semantics=("arbitrary",)),
    )(q, mk, mv, do, lse, delta)


def _silu_and_grad(g):
    s = 1.0 / (1.0 + jnp.exp(-g))
    return g * s, s * (1.0 + g * (1.0 - s))


def _post(x, target, post_norm, w_out, sink_row, oa, lse_a, ga, ob_list, lseb_list, gb, oc, gc):
    seq = x.shape[0]
    tm = min(ROW_TILE, seq)
    inv_d = 1.0 / D_MODEL
    nd = len(B_DILS)

    def body(*refs):
        (x_ref, t_ref, gp_ref, w_ref, sink_ref, oa_ref, lsea_ref, ga_ref), refs = refs[:8], refs[8:]
        ob_refs, lb_refs, (gb_ref, oc_ref, gc_ref), refs = refs[:nd], refs[nd:2 * nd], refs[2 * nd:2 * nd + 3], refs[2 * nd + 3:]
        (g_ref, doa_ref, dla_ref, dga_ref), refs = refs[:4], refs[4:]
        dob_refs, lsec_refs, dlb_refs, refs = refs[:nd], refs[nd:2 * nd], refs[2 * nd:3 * nd], refs[3 * nd:]
        (dgb_ref, doc_ref, dlc_ref, dgc_ref, gw_ref, gpost_ref, gsink_ref, loss_ref), refs = refs[:8], refs[8:]
        ycat, obufs, lbufs, st_do, st_l, st_d = refs[0], refs[1:nd], refs[nd:2 * nd - 1], refs[2 * nd - 1], refs[2 * nd], refs[2 * nd + 1]

        @pl.when(pl.program_id(0) == 0)
        def _():
            gw_ref[...] = jnp.zeros_like(gw_ref)
            gpost_ref[...] = jnp.zeros_like(gpost_ref)
            gsink_ref[...] = jnp.zeros_like(gsink_ref)
            loss_ref[...] = jnp.zeros_like(loss_ref)

        o_i, l_i = [ob_refs[0][0].astype(F32)], [lb_refs[0][0]]
        for k in range(1, nd):
            _from_residues(ob_refs[k], obufs[k - 1], B_DILS[k])
            _from_residues(lb_refs[k], lbufs[k - 1], B_DILS[k])
            o_i.append(_stage_read(obufs[k - 1]))
            l_i.append(_stage_read(lbufs[k - 1]))
        mx = l_i[0]
        for l in l_i[1:]:
            mx = jnp.maximum(mx, l)
        w_i = [jnp.exp(l - mx) for l in l_i]
        z = w_i[0]
        for w in w_i[1:]:
            z = z + w
        _stage_write(st_l, mx + jnp.log(z))
        expand = _head_expand_matrix(B_W)
        inv_z = 1.0 / z
        ob = None
        for w, o in zip(w_i, o_i):
            term = _dot_split(w * inv_z, expand, 2) * o
            ob = term if ob is None else ob + term
        oa, oc = oa_ref[...].astype(F32), oc_ref[...].astype(F32)
        sa, dsa = _silu_and_grad(ga_ref[...].astype(F32))
        sb, dsb = _silu_and_grad(gb_ref[...].astype(F32))
        sc, dsc = _silu_and_grad(gc_ref[...].astype(F32))
        ycat[:, 0:A_W] = (oa * sa).astype(BF16)
        ycat[:, A_W:A_W + B_W] = (ob * sb).astype(BF16)
        ycat[:, A_W + B_W:] = (oc * sc).astype(BF16)
        y2 = _dot(ycat[...], w_ref[...])
        r = lax.rsqrt(jnp.mean(y2 * y2, axis=-1, keepdims=True) + RMS_EPS)
        zhat = y2 * r
        gp = gp_ref[...]
        err = x_ref[...] + zhat * gp - t_ref[...]
        loss_ref[...] += jnp.sum(err * err) * (0.5 * inv_d)
        g = err * inv_d
        g_ref[...] = g
        gpost_ref[...] += jnp.sum(g * zhat, axis=0, keepdims=True)
        a = g * gp
        dy2 = (r * (a - zhat * jnp.mean(a * zhat, axis=-1, keepdims=True))).astype(BF16)
        gw_ref[...] += _dot_tn(ycat[...], dy2)
        dycat = _dot_nt(dy2, w_ref[...])
        dya, dyb, dyc = dycat[:, 0:A_W], dycat[:, A_W:A_W + B_W], dycat[:, A_W + B_W:]
        doa, dob, doc = dya * sa, dyb * sb, dyc * sc
        doa_ref[...] = doa.astype(BF16)
        doc_ref[...] = doc.astype(BF16)
        dga_ref[...] = (dya * oa * dsa).astype(BF16)
        dgb_ref[...] = (dyb * ob * dsb).astype(BF16)
        dgc_ref[...] = (dyc * oc * dsc).astype(BF16)
        dl_a = _dot_split(doa * oa, _head_sum_matrix(A_W), 2)
        dla_ref[...] = dl_a
        dlc_ref[...] = _dot_split(doc * oc, _head_sum_matrix(C_W), 2)
        gsink_ref[...] += jnp.sum(jnp.exp(sink_ref[...] - lsea_ref[...]) * dl_a, axis=0, keepdims=True)
        _stage_write(st_do, dob)
        _stage_write(st_d, _dot_split(dob * ob, _head_sum_matrix(B_W), 2))
        for k, dil in enumerate(B_DILS):
            _to_residues(st_do, dob_refs[k], dil)
            _to_residues(st_l, lsec_refs[k], dil)
            _to_residues(st_d, dlb_refs[k], dil)

    row = lambda w: pl.BlockSpec((tm, w), lambda i: (i, 0))
    full = lambda shape: pl.BlockSpec(shape, lambda i: (0,) * len(shape))
    res_specs = lambda w: [_residue_spec(d, tm, w) for d in B_DILS]
    res_shapes = lambda w, dt: [jax.ShapeDtypeStruct((d, seq // d, w), dt) for d in B_DILS]
    ins = [x, target, post_norm, w_out, sink_row, oa, lse_a, ga, *ob_list, *lseb_list, gb, oc, gc]
    in_specs = ([row(D_MODEL), row(D_MODEL), full((1, D_MODEL)), full((D_MODEL, D_MODEL)), full((1, LANES)),
                 row(A_W), row(LANES), row(A_W)] + res_specs(B_W) + res_specs(LANES) + [row(B_W), row(C_W), row(C_W)])
    out_shape = ([jax.ShapeDtypeStruct((seq, D_MODEL), F32), jax.ShapeDtypeStruct((seq, A_W), BF16),
                  jax.ShapeDtypeStruct((seq, LANES), F32), jax.ShapeDtypeStruct((seq, A_W), BF16)]
                 + res_shapes(B_W, BF16) + res_shapes(LANES, F32) + res_shapes(LANES, F32)
                 + [jax.ShapeDtypeStruct((seq, B_W), BF16), jax.ShapeDtypeStruct((seq, C_W), BF16),
                    jax.ShapeDtypeStruct((seq, LANES), F32), jax.ShapeDtypeStruct((seq, C_W), BF16),
                    jax.ShapeDtypeStruct((D_MODEL, D_MODEL), F32), jax.ShapeDtypeStruct((1, D_MODEL), F32),
                    jax.ShapeDtypeStruct((1, LANES), F32), jax.ShapeDtypeStruct((1, LANES), F32)])
    out_specs = ([row(D_MODEL), row(A_W), row(LANES), row(A_W)] + res_specs(B_W) + res_specs(LANES) + res_specs(LANES)
                 + [row(B_W), row(C_W), row(LANES), row(C_W),
                    full((D_MODEL, D_MODEL)), full((1, D_MODEL)), full((1, LANES)), full((1, LANES))])
    scratch = ([pltpu.VMEM((tm, D_MODEL), BF16)] + [_stage(tm, B_W)] * (nd - 1) + [_stage(tm, LANES)] * (nd - 1)
               + [_stage(tm, B_W), _stage(tm, LANES), _stage(tm, LANES)])
    res = pl.pallas_call(
        body, name="post", grid=(seq // tm,), in_specs=in_specs, out_specs=out_specs, out_shape=out_shape,
        scratch_shapes=scratch,
        compiler_params=pltpu.CompilerParams(dimension_semantics=("arbitrary",)),
    )(*ins)
    out = dict(g=res[0], doa=res[1], dl_a=res[2], dga=res[3], dob=res[4:4 + nd], lse_b=res[4 + nd:4 + 2 * nd],
               dl_b=res[4 + 2 * nd:4 + 3 * nd])
    rest = res[4 + 3 * nd:]
    out.update(dgb=rest[0], doc=rest[1], dl_c=rest[2], dgc=rest[3], gw_out=rest[4], gpost=rest[5], gsink=rest[6],
               loss=rest[7])
    return out


def _grad_w_in(ut, nat, res):
    seq = ut.shape[1]
    tm = min(ROW_TILE, seq)
    nd = len(B_DILS)
    nat_list = [nat[n] for n in _NATURAL]
    res_list = [a for n in _DILATED for a in res[n]]
    rope = _rope_tables(seq, tm)

    def body(rl_ref, rb_ref, ut_ref, *refs):
        nat_refs = dict(zip(_NATURAL, refs[:len(_NATURAL)]))
        refs = refs[len(_NATURAL):]
        res_refs = {n: refs[nd * k:nd * (k + 1)] for k, n in enumerate(_DILATED)}
        refs = refs[nd * len(_DILATED):]
        dproj_ref, gw_ref = refs[:2]
        bufs = {n: refs[2 + (nd - 1) * k:2 + (nd - 1) * (k + 1)] for k, n in enumerate(_DILATED)}

        @pl.when(pl.program_id(0) == 0)
        def _():
            gw_ref[...] = jnp.zeros_like(gw_ref)

        for n in _DILATED:
            for k in range(1, nd):
                _from_residues(res_refs[n][k], bufs[n][k - 1], B_DILS[k])
        c, sm, sp = _rope_coeffs(rl_ref, rb_ref)
        sm, sp = -sm, -sp
        for blk, (name, off, roped, scaled) in enumerate(_PROJ_LAYOUT):
            lanes = slice(off, off + LANES)
            if name in nat_refs:
                piece = nat_refs[name][:, lanes].astype(F32)
            else:
                piece = res_refs[name][0][0, :, lanes].astype(F32)
                for buf in bufs[name]:
                    piece = piece + buf[off // LANES]
            if roped:
                piece = _rope(piece, c, sm, sp)
            if scaled:
                piece = piece * SCALE
            dproj_ref[:, blk * LANES:(blk + 1) * LANES] = piece.astype(BF16)
        for j in range(N_CHIPS):
            gw_ref[j] += _dot(ut_ref[...], dproj_ref[:, j * SHARD_IN:(j + 1) * SHARD_IN])

    row = lambda w: pl.BlockSpec((tm, w), lambda i: (i, 0))
    in_specs = ([pl.BlockSpec(rope[0].shape, lambda i: (0, 0)), pl.BlockSpec((8, 2 * LANES), lambda i: (i, 0)),
                 pl.BlockSpec((D_MODEL, tm), lambda i: (0, i))]
                + [row(a.shape[1]) for a in nat_list]
                + [_residue_spec(d, tm, B_W) for _ in _DILATED for d in B_DILS])
    return pl.pallas_call(
        body, name="grad_w_in", grid=(seq // tm,), in_specs=in_specs,
        out_specs=[row(D_IN), pl.BlockSpec((N_CHIPS, D_MODEL, SHARD_IN), lambda i: (0, 0, 0))],
        out_shape=[jax.ShapeDtypeStruct((seq, D_IN), BF16), jax.ShapeDtypeStruct((N_CHIPS, D_MODEL, SHARD_IN), F32)],
        scratch_shapes=[_stage(tm, B_W)] * ((nd - 1) * len(_DILATED)),
        compiler_params=pltpu.CompilerParams(dimension_semantics=("arbitrary",)),
    )(*rope, ut, *nat_list, *res_list)


def _input_grad(x, g, pre_norm, w_in_g, dproj, gx_prev, span, after, name):
    seq = x.shape[0]
    tm = seq // INPUT_GRAD_TILES
    first_block, steps = span

    def body(*refs):
        x_ref, g_ref, gp_ref, w_ref, dp_ref = refs[:5]
        gx_ref, gpre_ref = refs[-2:]

        @pl.when(pl.program_id(0) == 0)
        def _():
            gpre_ref[...] = jnp.zeros_like(gpre_ref)

        du = None
        for j in range(N_CHIPS):
            term = _dot_nt(dp_ref[:, j * SHARD_IN:(j + 1) * SHARD_IN], w_ref[j])
            du = term if du is None else du + term
        xv = x_ref[...]
        r = lax.rsqrt(jnp.mean(xv * xv, axis=-1, keepdims=True) + RMS_EPS)
        xhat = xv * r
        gpre_ref[...] += jnp.sum(du * xhat, axis=0, keepdims=True)
        a = du * gp_ref[...]
        gx_ref[...] = g_ref[...] + r * (a - xhat * jnp.mean(a * xhat, axis=-1, keepdims=True))

    row = lambda w: pl.BlockSpec((tm, w), lambda i: (first_block + i, 0))
    full = lambda a: pl.BlockSpec(a.shape, lambda i: (0,) * a.ndim)
    any_spec = pl.BlockSpec(memory_space=pl.ANY)
    ins = [x, g, pre_norm, w_in_g, dproj]
    in_specs = [row(D_MODEL), row(D_MODEL), full(pre_norm), full(w_in_g), row(D_IN)]
    aliases = {}
    if gx_prev is not None:
        aliases[len(ins)] = 0
        ins.append(gx_prev)
        in_specs.append(any_spec)
    if after is not None:
        ins.append(after)
        in_specs.append(any_spec)
    return pl.pallas_call(
        body, name=name, grid=(steps,), in_specs=in_specs,
        out_specs=[row(D_MODEL), pl.BlockSpec((1, D_MODEL), lambda i: (0, 0))],
        out_shape=[jax.ShapeDtypeStruct((seq, D_MODEL), F32), jax.ShapeDtypeStruct((1, D_MODEL), F32)],
        input_output_aliases=aliases,
        compiler_params=pltpu.CompilerParams(dimension_semantics=("arbitrary",)),
    )(*ins)


def _exchange_start(ex, name):
    n_in, n_out, n_sem = len(ex["ins"]), len(ex["outs"]), len(ex["sems"])

    def body(*refs):
        in_refs, land_refs, sems = refs[:n_in], refs[n_in:n_in + n_out], refs[n_in + n_out:n_in + n_out + n_sem]
        ex["start"](in_refs, land_refs, *sems)
        token = refs[-1]
        token[...] = jnp.zeros_like(token)

    hbm = pl.BlockSpec(memory_space=pltpu.HBM)
    sem = pl.BlockSpec(memory_space=pltpu.SEMAPHORE)
    ins = [pltpu.with_memory_space_constraint(a, pltpu.HBM) for a in ex["ins"]]
    landing = [pltpu.with_memory_space_constraint(lax.empty(o.shape, o.dtype), pltpu.HBM) for o in ex["outs"]]
    res = pl.pallas_call(
        body, name=name,
        out_shape=list(ex["sems"]) + [pltpu.HBM(a.shape, a.dtype) for a in ex["ins"]]
        + [pltpu.HBM(o.shape, o.dtype) for o in ex["outs"]] + [jax.ShapeDtypeStruct((8, LANES), F32)],
        in_specs=[hbm] * (n_in + n_out),
        out_specs=[sem] * n_sem + [hbm] * (n_in + n_out) + [pl.BlockSpec(memory_space=pltpu.VMEM)],
        input_output_aliases={k: n_sem + k for k in range(n_in + n_out)},
        compiler_params=pltpu.CompilerParams(has_side_effects=pltpu.SideEffectType.DATAFLOW_SIDE_EFFECTING),
    )(*ins, *landing)
    return res[:-1], res[-1]


def _exchange_wait(ex, handles, after, name):
    n_in, n_out, n_sem = len(ex["ins"]), len(ex["outs"]), len(ex["sems"])
    sems, thru = handles[:n_sem], handles[n_sem:]

    def body(*refs):
        in_refs, land_refs = refs[:n_in], refs[n_in:n_in + n_out]
        sem_refs = refs[n_in + n_out:n_in + n_out + n_sem]
        ex["finish"](in_refs, land_refs, *sem_refs)

    hbm = pl.BlockSpec(memory_space=pltpu.HBM)
    sem = pl.BlockSpec(memory_space=pltpu.SEMAPHORE)
    res = pl.pallas_call(
        body, name=name,
        out_shape=[pltpu.HBM(a.shape, a.dtype) for a in thru],
        in_specs=[hbm] * (n_in + n_out) + [sem] * n_sem + [pl.BlockSpec(memory_space=pl.ANY)],
        out_specs=[hbm] * (n_in + n_out),
        input_output_aliases={k: k for k in range(n_in + n_out)},
        compiler_params=pltpu.CompilerParams(has_side_effects=pltpu.SideEffectType.DATAFLOW_SIDE_EFFECTING),
    )(*thru, *sems, after)
    return res[:n_in], res[n_in:]


def _start_finish(build):
    def start(*refs):
        for cp in build(*refs):
            cp.start()

    def finish(*refs):
        for cp in build(*refs):
            cp.wait()

    return dict(start=start, finish=finish)


def _pair_exchange(grads):
    n = len(grads)

    def build(srcs, outs, send_sems, recv_sems):
        x, y, c = lax.axis_index("x"), lax.axis_index("y"), lax.axis_index("c")
        copies = []
        for t in range(n):
            rows = grads[t].shape[1] // 2
            copies.append(pltpu.make_async_remote_copy(
                src_ref=srcs[t].at[:, pl.ds((1 - c) * rows, rows)], dst_ref=outs[t],
                send_sem=send_sems.at[t], recv_sem=recv_sems.at[t], device_id=(x, y, 1 - c), device_id_type=MESH))
        return copies

    return dict(ins=list(grads), **_start_finish(build),
                outs=[jax.ShapeDtypeStruct((g.shape[0], g.shape[1] // 2, g.shape[2]), g.dtype) for g in grads],
                sems=[pltpu.SemaphoreType.DMA((n,)), pltpu.SemaphoreType.DMA((n,))])


def _pair_add(core, owns, gots):
    n = len(owns)

    def body(core_ref, *refs):
        for t in range(n):
            refs[2 * n + t][...] = (refs[t][...] + refs[n + t][...]).astype(BF16)

    halves = [(None,) + g.shape[1:] for g in gots]
    grid_spec = pltpu.PrefetchScalarGridSpec(
        num_scalar_prefetch=1, grid=(N_CHIPS,),
        in_specs=[pl.BlockSpec(h, lambda k, core_ref: (k, core_ref[0], 0)) for h in halves]
        + [pl.BlockSpec(h, lambda k, core_ref: (k, 0, 0)) for h in halves],
        out_specs=[pl.BlockSpec(h, lambda k, core_ref: (k, 0, 0)) for h in halves])
    return pl.pallas_call(
        body, name="pair_add", grid_spec=grid_spec,
        out_shape=[jax.ShapeDtypeStruct(g.shape, BF16) for g in gots],
    )(core, *owns, *gots)


def _chip_exchange(parts):
    n = len(parts)

    def build(srcs, outs, send_sems, recv_sems, local_sems):
        x, y, c = lax.axis_index("x"), lax.axis_index("y"), lax.axis_index("c")
        my_chip = 2 * x + y
        chips = [(1 - x, y), (x, 1 - y), (1 - x, 1 - y)]
        copies = [pltpu.make_async_copy(srcs[t].at[my_chip], outs[t].at[my_chip], local_sems.at[t]) for t in range(n)]
        for j, (cx, cy) in enumerate(chips):
            for t in range(n):
                k = n * j + t
                copies.append(pltpu.make_async_remote_copy(
                    src_ref=srcs[t].at[2 * cx + cy], dst_ref=outs[t].at[my_chip], send_sem=send_sems.at[k],
                    recv_sem=recv_sems.at[k], device_id=(cx, cy, c), device_id_type=MESH))
        return copies

    return dict(ins=list(parts), **_start_finish(build), outs=[jax.ShapeDtypeStruct(p.shape, p.dtype) for p in parts],
                sems=[pltpu.SemaphoreType.DMA((3 * n,)), pltpu.SemaphoreType.DMA((3 * n,)),
                      pltpu.SemaphoreType.DMA((n,))])


def _chip_sum(core, slots):
    n = len(slots)

    def body(core_ref, *refs):
        for t in range(n):
            acc = refs[t][0].astype(F32)
            for s in range(1, N_CHIPS):
                acc = acc + refs[t][s].astype(F32)
            refs[n + t][...] = acc

    blocks = [(s.shape[1] // TAIL_STEPS, s.shape[2]) for s in slots]
    grid_spec = pltpu.PrefetchScalarGridSpec(
        num_scalar_prefetch=1, grid=(TAIL_STEPS,),
        in_specs=[pl.BlockSpec((N_CHIPS,) + b, lambda i, core_ref: (0, i, 0)) for b in blocks],
        out_specs=[pl.BlockSpec((None,) + b, lambda i, core_ref: (core_ref[0], i, 0)) for b in blocks])
    return pl.pallas_call(
        body, name="chip_sum", grid_spec=grid_spec,
        out_shape=[jax.ShapeDtypeStruct((2,) + s.shape[1:], F32) for s in slots],
    )(core, *slots)


def _pair_gather(bufs, small):
    n = len(bufs)

    def body(*refs):
        small_ref, outs, small_out = refs[n], refs[n + 1:2 * n + 1], refs[2 * n + 1]
        send_sems, recv_sems, local_sem = refs[2 * n + 2:]
        x, y, c = lax.axis_index("x"), lax.axis_index("y"), lax.axis_index("c")
        me = 4 * x + 2 * y + c
        chips = [(1 - x, y), (x, 1 - y), (1 - x, 1 - y)]
        mine = pltpu.make_async_copy(small_ref, small_out.at[me], local_sem)
        mine.start()
        copies = [pltpu.make_async_remote_copy(
            src_ref=outs[t].at[c], dst_ref=outs[t].at[c], send_sem=send_sems.at[t], recv_sem=recv_sems.at[t],
            device_id=(x, y, 1 - c), device_id_type=MESH) for t in range(n)]
        peers = [(x, y, 1 - c)] + [(cx, cy, cc) for (cx, cy) in chips for cc in (c, 1 - c)]
        for j, peer in enumerate(peers):
            copies.append(pltpu.make_async_remote_copy(
                src_ref=small_ref, dst_ref=small_out.at[me], send_sem=send_sems.at[n + j],
                recv_sem=recv_sems.at[n + j], device_id=peer, device_id_type=MESH))
        for cp in copies:
            cp.start()
        for cp in copies:
            cp.wait()
        mine.wait()

    any_spec = pl.BlockSpec(memory_space=pl.ANY)
    res = pl.pallas_call(
        body, name="pair_gather",
        out_shape=[jax.ShapeDtypeStruct(b.shape, b.dtype) for b in bufs]
        + [jax.ShapeDtypeStruct((8,) + small.shape, small.dtype)],
        in_specs=[any_spec] * (n + 1), out_specs=[any_spec] * (n + 1),
        input_output_aliases={t: t for t in range(n)},
        scratch_shapes=[pltpu.SemaphoreType.DMA((n + 7,)), pltpu.SemaphoreType.DMA((n + 7,)),
                        pltpu.SemaphoreType.DMA],
    )(*bufs, small)
    return [r.reshape(2 * b.shape[1], b.shape[2]) for r, b in zip(res[:n], bufs)], res[n]


def _adamw(ws, gs, ms, vs):
    n = len(ws)

    def body(*refs):
        for t in range(n):
            w_ref, g_ref, m_ref, v_ref = refs[t:4 * n:n]
            gout_ref, d_ref, nm_ref, nv_ref = refs[4 * n + t::n]
            g = g_ref[...]
            gout_ref[...] = g
            d_ref[...], nm_ref[...], nv_ref[...] = _adamw_math(w_ref[...], g, m_ref[...], v_ref[...])

    specs = [pl.BlockSpec((w.shape[0] // TAIL_STEPS, w.shape[1]), lambda i: (i, 0)) for w in ws]
    res = pl.pallas_call(
        body, name="adamw", grid=(TAIL_STEPS,), in_specs=specs * 4, out_specs=specs * 4,
        out_shape=[jax.ShapeDtypeStruct(w.shape, F32) for w in ws] * 4,
    )(*ws, *gs, *ms, *vs)
    return [res[t::n] for t in range(n)]


def _adamw_math(w, g, m, v):
    c1 = 1.0 / (1.0 - ADAM_B1 ** ADAM_STEP)
    c2 = 1.0 / (1.0 - ADAM_B2 ** ADAM_STEP)
    nm = ADAM_B1 * m + (1.0 - ADAM_B1) * g
    nv = ADAM_B2 * v + (1.0 - ADAM_B2) * (g * g)
    return -ADAM_LR * ((nm * c1) / (jnp.sqrt(nv * c2) + ADAM_EPS) + ADAM_WD * w), nm, nv


def _small_update(slots, params, ms, vs):
    n = len(params)

    def body(slots_ref, *refs):
        w_refs, m_refs, v_refs, loss_ref = refs[:n], refs[n:2 * n], refs[2 * n:3 * n], refs[3 * n]
        g_refs, d_refs, nm_refs, nv_refs = (refs[3 * n + 1 + k * n:3 * n + 1 + (k + 1) * n] for k in range(4))
        acc = slots_ref[0]
        for s in range(1, slots.shape[0]):
            acc = acc + slots_ref[s]
        loss_ref[...] = acc[4:5, 0:1]
        grads = (acc[0:1] + acc[5:6], acc[3:4], acc[2:3], acc[1:2])
        for k in range(n):
            g = grads[k][:, :w_refs[k].shape[1]]
            g_refs[k][...] = g
            d_refs[k][...], nm_refs[k][...], nv_refs[k][...] = _adamw_math(w_refs[k][...], g, m_refs[k][...],
                                                                           v_refs[k][...])

    return pl.pallas_call(
        body, name="small_update",
        out_shape=[jax.ShapeDtypeStruct((1, 1), F32)] + [jax.ShapeDtypeStruct(p.shape, F32) for p in params] * 4,
    )(slots, *params, *ms, *vs)


def _local_step(x, mem, target, pre_norm, sink_a, mem_norm, post_norm, w_in_g, w_out, w_mkv, gathers=None,
                own=None):
    first_gather, late_gather = gathers if gathers else (None, None)
    u, ut, p_own, hosted = _pre_norm(x, pre_norm, own[1] if own else None, first_gather)
    if gathers:
        w_in_g = hosted[0].reshape(N_CHIPS, D_MODEL, SHARD_IN)
    pr = _pre_proj(u, w_in_g, (own[0], p_own) if own else None, late_gather)
    pr["ut"] = ut
    if gathers:
        w_out, w_mkv = (g.reshape(D_MODEL, g.shape[-1]) for g in pr["hosted"])
    mk, mv = _mem_kv(mem, mem_norm, w_mkv)
    sink = sink_a.reshape(-1)
    qa, ka, va = pr["qa"][None], pr["ka"][None], pr["va"][None]
    oa, lse_a = _band_fwd(qa, ka, va, sink, max_dist=A_WINDOW - 1, name="swa_fwd")
    ob_list, lseb_list = [], []
    for k, (win, dil) in enumerate(B_CONFIGS):
        o_i, l_i = _band_fwd(pr["qb"][k], pr["kb"][k], pr["vb"][k], None, max_dist=win // dil, name=f"dil{dil}_fwd")
        ob_list.append(o_i)
        lseb_list.append(l_i)
    oc, lse_c = _mem_attn_fwd(pr["qc"], mk, mv)
    sink_row = jnp.pad(sink, (0, LANES - sink.shape[0])).reshape(1, LANES)
    po = _post(x, target, post_norm, w_out, sink_row, oa[0], lse_a[0], pr["ga"], ob_list, lseb_list, pr["gb"], oc,
               pr["gc"])
    dqc, dmk, dmv = _mem_attn_bwd(pr["qc"], mk, mv, po["doc"], lse_c, po["dl_c"])
    dqa, dka, dva = _band_bwd(qa, ka, va, po["doa"][None], lse_a, po["dl_a"][None], max_dist=A_WINDOW - 1,
                              name="swa_bwd")
    res = dict(qb=[], kb=[], vb=[])
    for k, (win, dil) in enumerate(B_CONFIGS):
        dq_i, dk_i, dv_i = _band_bwd(pr["qb"][k], pr["kb"][k], pr["vb"][k], po["dob"][k], po["lse_b"][k],
                                     po["dl_b"][k], max_dist=win // dil, name=f"dil{dil}_bwd")
        res["qb"].append(dq_i)
        res["kb"].append(dk_i)
        res["vb"].append(dv_i)
    nat = dict(qa=dqa[0], ka=dka[0], va=dva[0], ga=po["dga"], gb=po["dgb"], qc=dqc, gc=po["dgc"])
    dproj, gw_in = _grad_w_in(pr["ut"], nat, res)
    gw_mkv, gmem = _mem_kv_bwd(mem, mem_norm, w_mkv, dmk, dmv)
    gsink = -po["gsink"][0, :sink.shape[0]]
    return dict(loss=po["loss"], g=po["g"], dproj=dproj, gw_in=gw_in, gw_out=po["gw_out"], gw_mkv=gw_mkv,
                gpost=po["gpost"], gmem=gmem, gsink=gsink, w_in_g=w_in_g)


def kernel(x, mem, pre_norm, w_in, sink_a, mem_norm, w_mem_kv, w_out, post_norm, loss_target, m_pre_norm, m_w_in, m_sink_a, m_mem_norm, m_w_mem_kv, m_w_out, m_post_norm, v_pre_norm, v_w_in, v_sink_a, v_mem_norm, v_w_mem_kv, v_w_out, v_post_norm):
    w_own = w_in[0]
    gathers = (_gather_exchange([w_own], prefilled=False),
               _gather_exchange([w_out[0].astype(BF16), w_mem_kv[0].astype(BF16)]))
    chip = (2 * lax.axis_index("x") + lax.axis_index("y")).astype(jnp.int32).reshape(1)
    loc = _local_step(x[0], mem[0], loss_target[0], pre_norm, sink_a, mem_norm, post_norm, None, None, None, gathers,
                      (chip, w_own))
    big = [loc["gw_in"], loc["gw_out"].reshape(N_CHIPS, D_MODEL // N_CHIPS, D_MODEL),
           loc["gw_mkv"].reshape(N_CHIPS, D_MODEL // N_CHIPS, 2 * C_W)]
    core = lax.axis_index("c").astype(jnp.int32).reshape(1)
    w_in_full = loc["w_in_g"]
    step_in = (x[0], loc["g"], pre_norm, w_in_full, loc["dproj"])
    pair_ex = _pair_exchange(big)
    pair_handles, token = _exchange_start(pair_ex, "pair_exchange_start")
    gx_a, gpre_a = _input_grad(*step_in, None, (0, 2), token, "input_grad_a")
    big, got = _exchange_wait(pair_ex, pair_handles, gpre_a, "pair_exchange_wait")
    parts = _pair_add(core, big, got)
    chip_ex = _chip_exchange(parts)
    chip_handles, token = _exchange_start(chip_ex, "chip_exchange_start")
    grad_x, gpre_b = _input_grad(*step_in, gx_a, (2, 14), token, "input_grad_b")
    _, slots = _exchange_wait(chip_ex, chip_handles, gpre_b, "chip_exchange_wait")
    halves = _chip_sum(core, slots)
    widen = lambda a: jnp.pad(a.reshape(1, -1), ((0, 0), (0, D_MODEL - a.size)))
    small = jnp.concatenate([gpre_a, loc["gpost"], loc["gmem"], widen(loc["gsink"]), widen(loc["loss"]), gpre_b,
                             jnp.zeros((2, D_MODEL), F32)], axis=0)
    (g_in, g_out, g_mkv), small_slots = _pair_gather(halves, small)
    (loss, g_pre, g_sink, g_mem, g_post, d_pre, d_sink, d_mem, d_post, nm_pre, nm_sink, nm_mem, nm_post,
     nv_pre, nv_sink, nv_mem, nv_post) = _small_update(
        small_slots, (pre_norm, sink_a, mem_norm, post_norm), (m_pre_norm, m_sink_a, m_mem_norm, m_post_norm),
        (v_pre_norm, v_sink_a, v_mem_norm, v_post_norm))

    (g_in, d_in, nm_in, nv_in), (g_out, d_out, nm_out, nv_out), (g_mkv, d_mkv, nm_mkv, nv_mkv) = _adamw(
        (w_in[0], w_out[0], w_mem_kv[0]), (g_in, g_out, g_mkv), (m_w_in[0], m_w_out[0], m_w_mem_kv[0]),
        (v_w_in[0], v_w_out[0], v_w_mem_kv[0]))
    lead = lambda a: a[None]
    return (loss.reshape(()), lead(grad_x),
            g_pre, lead(g_in), g_sink, g_mem, lead(g_mkv), lead(g_out), g_post,
            d_pre, lead(d_in), d_sink, d_mem, lead(d_mkv), lead(d_out), d_post,
            nm_pre, lead(nm_in), nm_sink, nm_mem, lead(nm_mkv), lead(nm_out), nm_post,
            nv_pre, lead(nv_in), nv_sink, nv_mem, lead(nv_mkv), lead(nv_out), nv_post)
```

```python
import numpy as np
import jax
import jax.numpy as jnp
from jax import lax
from jax.experimental import pallas as pl
from jax.experimental.pallas import tpu as pltpu

F32 = jnp.float32
BF16 = jnp.bfloat16

D_MODEL = 1024
HEAD_DIM = 64
LANES = 128
BLOCK = 128
ROW_TILE = 512
ATTN_TILE = 1024
INPUT_GRAD_TILES = 16
TAIL_STEPS = 4
A_W, A_KV_W, B_W, C_W = 384, 128, 384, 256
N_MEM = 256
D_IN = 3072
N_CHIPS = 4
SHARD_IN = D_IN // N_CHIPS
B_CONFIGS = ((128, 1), (512, 4), (2048, 16))
B_DILS = tuple(d for _, d in B_CONFIGS)
A_WINDOW = 128
RMS_EPS = 1e-6
ROPE_THETA = 500000.0
SCALE = HEAD_DIM ** -0.5
NEG = -1e30
ADAM_LR, ADAM_B1, ADAM_B2, ADAM_EPS, ADAM_WD, ADAM_STEP = 0.001, 0.9, 0.999, 1e-08, 0.01, 10

NT = (((1,), (1,)), ((), ()))
TN = (((0,), (0,)), ((), ()))
MESH = pl.DeviceIdType.MESH

_PROJ_LAYOUT = (
    [("qa", 128 * i, True, True) for i in range(3)] + [("ka", 0, True, False), ("va", 0, False, False)]
    + [("ga", 128 * i, False, False) for i in range(3)]
    + [("qb", 128 * i, True, True) for i in range(3)] + [("kb", 128 * i, True, False) for i in range(3)]
    + [("vb", 128 * i, False, False) for i in range(3)] + [("gb", 128 * i, False, False) for i in range(3)]
    + [("qc", 128 * i, False, True) for i in range(2)] + [("gc", 128 * i, False, False) for i in range(2)]
)
_PROJ_WIDTH = dict(qa=A_W, ka=A_KV_W, va=A_KV_W, ga=A_W, qb=B_W, kb=B_W, vb=B_W, gb=B_W, qc=C_W, gc=C_W)
_NATURAL = ("qa", "ka", "va", "ga", "gb", "qc", "gc")
_DILATED = ("qb", "kb", "vb")


def _dot(a, b):
    return jnp.dot(a, b, preferred_element_type=F32)


def _dot_nt(a, b):
    return lax.dot_general(a, b, NT, preferred_element_type=F32)


def _dot_tn(a, b):
    return lax.dot_general(a, b, TN, preferred_element_type=F32)


def _half_masks(rows):
    lane = lax.broadcasted_iota(jnp.int32, (rows, LANES), 1)
    return lane < HEAD_DIM, lane >= HEAD_DIM


def _rope(t, c, sm, sp):
    return t * c + pltpu.roll(t, LANES - 8, 1) * sm + pltpu.roll(t, 8, 1) * sp


def _rope_tables(seq, tm):
    dim = np.arange(LANES) % HEAD_DIM
    inv_freq = (np.float32(ROPE_THETA) ** (-np.arange(0, 16, 2, dtype=np.float32) / np.float32(16))).astype(np.float64)
    freq = np.where(dim < 16, inv_freq[dim % 8], 0.0)[None, :]
    local = np.arange(tm, dtype=np.float64)[:, None] * freq
    base = (np.arange(seq // tm, dtype=np.float64) * tm)[:, None] * freq
    both = lambda a: np.concatenate([np.cos(a), np.sin(a)], axis=1).astype(np.float32)
    return jnp.asarray(both(local)), jnp.asarray(np.repeat(both(base), 8, axis=0))


def _rope_coeffs(local_ref, base_ref):
    cl, sl = local_ref[:, :LANES], local_ref[:, LANES:]
    cb, sb = base_ref[0:1, :LANES], base_ref[0:1, LANES:]
    cos = cb * cl - sb * sl
    sin = sb * cl + cb * sl
    dim = lax.broadcasted_iota(jnp.int32, (1, LANES), 1) % HEAD_DIM
    return cos, jnp.where(dim < 8, -sin, 0.0), jnp.where((dim >= 8) & (dim < 16), sin, 0.0)


def _split3(x):
    a = x.astype(BF16)
    r = x - a.astype(F32)
    b = r.astype(BF16)
    c = (r - b.astype(F32)).astype(BF16)
    return a, b, c


def _rows_to_lanes(x):
    row = lax.broadcasted_iota(jnp.int32, (8, LANES), 0)
    lane = lax.broadcasted_iota(jnp.int32, (8, LANES), 1)
    eye = (row == lane).astype(BF16)
    a, b, c = _split3(x)
    return _dot_nt(eye, a) + _dot_nt(eye, b) + _dot_nt(eye, c)


def _head_sum_matrix(width):
    k = lax.broadcasted_iota(jnp.int32, (width, LANES), 0)
    h = lax.broadcasted_iota(jnp.int32, (width, LANES), 1)
    return (k // HEAD_DIM == h).astype(BF16)


def _head_expand_matrix(width):
    h = lax.broadcasted_iota(jnp.int32, (LANES, width), 0)
    k = lax.broadcasted_iota(jnp.int32, (LANES, width), 1)
    return (k // HEAD_DIM == h).astype(BF16)


def _dot_split(x, mat, terms):
    parts = _split3(x)[:terms]
    out = _dot(parts[0], mat)
    for p in parts[1:]:
        out = out + _dot(p, mat)
    return out


def _per_head(cols, fill=0.0):
    rows = cols[0].shape[0]
    lane = lax.broadcasted_iota(jnp.int32, (rows, LANES), 1)
    out = jnp.full((rows, LANES), fill, F32)
    for h, col in enumerate(cols):
        out = jnp.where(lane == h, col, out)
    return out


def _lane_blocks(width):
    return [slice(p * LANES, (p + 1) * LANES) for p in range(width // LANES)]


def _stage(rows, width):
    return pltpu.VMEM((width // LANES, rows, LANES), F32)


def _stage_write(buf, value):
    for p, lanes in enumerate(_lane_blocks(value.shape[1])):
        buf[p] = value[:, lanes]


def _stage_read(buf):
    return jnp.concatenate([buf[p] for p in range(buf.shape[0])], axis=1) if buf.shape[0] > 1 else buf[0]


def _to_residues(buf, out_ref, dil):
    rows = buf.shape[1] // dil
    for r in range(dil):
        for p in range(buf.shape[0]):
            plane = buf.at[p]
            out_ref[r, :, p * LANES:(p + 1) * LANES] = plane[pl.ds(r, rows, stride=dil), :].astype(out_ref.dtype)


def _from_residues(in_ref, buf, dil):
    rows = buf.shape[1] // dil
    for r in range(dil):
        for p in range(buf.shape[0]):
            plane = buf.at[p]
            plane[pl.ds(r, rows, stride=dil), :] = in_ref[r, :, p * LANES:(p + 1) * LANES].astype(F32)


def _residue_spec(dil, tm, width):
    return pl.BlockSpec((dil, tm // dil, width), lambda i: (0, i, 0))


def _gather_exchange(shards_2d, prefilled=True):
    shards = tuple(jax.ShapeDtypeStruct((2, s.shape[0] // 2, s.shape[1]), BF16) for s in shards_2d)
    n = len(shards)

    def copies(in_refs, out_refs, send_sems, recv_sems, *local_sems):
        srcs, outs = in_refs[:n], out_refs
        x, y, c = lax.axis_index("x"), lax.axis_index("y"), lax.axis_index("c")
        my_chip = 2 * x + y
        sibling = (x, y, 1 - c)
        chips = [(1 - x, y), (x, 1 - y), (1 - x, 1 - y)]

        def copy(k, src, dst, to):
            return pltpu.make_async_remote_copy(src_ref=src, dst_ref=dst, send_sem=send_sems.at[k],
                                                recv_sem=recv_sems.at[k], device_id=to, device_id_type=MESH)

        first, arrive, passed, sibling_arrive = [], [], [], []
        for j, (cx, cy) in enumerate(chips):
            chip = 2 * cx + cy
            for t in range(n):
                k = n * j + t
                first.append(copy(k, srcs[t].at[c], outs[t].at[my_chip, c], (cx, cy, c)))
                arrive.append(copy(k, srcs[t].at[c], outs[t].at[chip, c], (cx, cy, c)))
                passed.append(copy(n * 3 + k, outs[t].at[chip, c], outs[t].at[chip, c], sibling))
                sibling_arrive.append(copy(n * 3 + k, outs[t].at[chip, 1 - c], outs[t].at[chip, 1 - c], sibling))
        own = [pltpu.make_async_copy(srcs[t], outs[t].at[my_chip], local_sems[0].at[t]) for t in range(n)
               ] if local_sems else []
        return first, arrive, passed, sibling_arrive, own

    def start(*refs):
        first, _, _, _, own = copies(*refs)
        for cp in first + own:
            cp.start()

    def forward(refs, senders):
        _, arrive, passed, _, _ = copies(*refs)
        for j in senders:
            for k in range(n * j, n * (j + 1)):
                arrive[k].wait_recv()
                passed[k].start()

    def mid(*refs):
        forward(refs, (0, 1))

    def finish(*refs):
        forward(refs, (2,))
        first, _, passed, sibling_arrive, own = copies(*refs)
        for cp in sibling_arrive:
            cp.wait_recv()
        for cp in first + passed:
            cp.wait_send()
        for cp in own:
            cp.wait()

    ex = dict(ins=[], start=start, mid=mid, finish=finish, prefilled=prefilled,
              outs=[jax.ShapeDtypeStruct((N_CHIPS,) + s.shape, s.dtype) for s in shards],
              sems=[pltpu.SemaphoreType.DMA((6 * n,)), pltpu.SemaphoreType.DMA((6 * n,))])
    if prefilled:
        my_chip = 2 * lax.axis_index("x") + lax.axis_index("y")
        halves = [a.reshape(s.shape) for a, s in zip(shards_2d, shards)]
        landing = [lax.dynamic_update_slice(jnp.zeros((N_CHIPS,) + s.shape, s.dtype), a[None], (my_chip, 0, 0, 0))
                   for a, s in zip(halves, shards)]
        ex.update(ins=halves + landing, aliases={n + t: t for t in range(n)})
    else:
        ex["sems"].append(pltpu.SemaphoreType.DMA((n,)))
    return ex


def _mem_kv(mem, mem_norm, w_mkv):
    def body(mem_ref, g_ref, w_ref, mk_ref, mv_ref):
        m = mem_ref[...]
        r = lax.rsqrt(jnp.mean(m * m, axis=-1, keepdims=True) + RMS_EPS)
        mn = (m * r * g_ref[...]).astype(BF16)
        kv = _dot(mn, w_ref[...])
        mk_ref[...] = kv[:, :C_W].astype(BF16)
        mv_ref[...] = kv[:, C_W:].astype(BF16)

    return pl.pallas_call(
        body, name="mem_kv",
        out_shape=[jax.ShapeDtypeStruct((N_MEM, C_W), BF16)] * 2,
    )(mem, mem_norm, w_mkv)


def _mem_kv_bwd(mem, mem_norm, w_mkv, dmk, dmv):
    def body(mem_ref, g_ref, w_ref, dmk_ref, dmv_ref, gw_ref, gn_ref):
        m = mem_ref[...]
        r = lax.rsqrt(jnp.mean(m * m, axis=-1, keepdims=True) + RMS_EPS)
        mhat = m * r
        mn = (mhat * g_ref[...]).astype(BF16)
        dkv = jnp.concatenate([dmk_ref[...], dmv_ref[...]], axis=1).astype(BF16)
        gw_ref[...] = _dot_tn(mn, dkv)
        dmn = _dot_nt(dkv, w_ref[...])
        gn_ref[...] = jnp.sum(dmn * mhat, axis=0, keepdims=True)

    return pl.pallas_call(
        body, name="mem_kv_bwd",
        out_shape=[jax.ShapeDtypeStruct((D_MODEL, 2 * C_W), F32), jax.ShapeDtypeStruct((1, D_MODEL), F32)],
    )(mem, mem_norm, w_mkv, dmk, dmv)


def _host_phases(host, in_refs, out_refs, sems, steps, before):
    if not host:
        return
    step = pl.program_id(0)
    phases = [("start", 0)] if before else [("mid", max(steps - 3, 0)), ("finish", steps - 1)]
    for phase, at in phases:
        pl.when(step == at)(lambda phase=phase: host[phase](in_refs, out_refs, *sems))


def _pre_norm(x, pre_norm, w_own=None, host=None):
    seq = x.shape[0]
    tm = min(ROW_TILE, seq)
    n_own_in = 2 if w_own is None else 3
    n_own_out = n_own_in
    n_host_in = len(host["ins"]) if host else 0
    n_host_out = len(host["outs"]) if host else 0
    half = D_MODEL // 2

    def body(x_ref, g_ref, *refs):
        w_ref = None if w_own is None else refs[0]
        refs = refs[n_own_in - 2:]
        host_in, own_out, refs = refs[:n_host_in], refs[n_host_in:n_host_in + n_own_out], refs[n_host_in + n_own_out:]
        host_out, refs = refs[:n_host_out], refs[n_host_out:]
        if w_own is not None:
            wb, sems = refs[0], refs[1:]

            @pl.when(pl.program_id(0) == 0)
            def _():
                for h in range(2):
                    wb[h] = w_ref[h * half:(h + 1) * half, :].astype(BF16)
            if host and not host["prefilled"]:
                host_in = [wb]
        else:
            sems = refs
        _host_phases(host, host_in, host_out, sems, seq // tm, before=True)

        xv = x_ref[...]
        r = lax.rsqrt(jnp.mean(xv * xv, axis=-1, keepdims=True) + RMS_EPS)
        u = xv * r * g_ref[...]
        ub = u.astype(BF16)
        own_out[0][...] = ub
        own_out[1][...] = u.T.astype(BF16)
        if w_own is not None:
            own_out[2][...] = _dot(ub[:, :half], wb[0]) + _dot(ub[:, half:], wb[1])
        _host_phases(host, host_in, host_out, sems, seq // tm, before=False)

    any_spec = pl.BlockSpec(memory_space=pl.ANY)
    ins = [x, pre_norm]
    in_specs = [pl.BlockSpec((tm, D_MODEL), lambda i: (i, 0)), pl.BlockSpec(pre_norm.shape, lambda i: (0, 0))]
    out_shape = [jax.ShapeDtypeStruct((seq, D_MODEL), BF16), jax.ShapeDtypeStruct((seq // tm, D_MODEL, tm), BF16)]
    out_specs = [pl.BlockSpec((tm, D_MODEL), lambda i: (i, 0)), pl.BlockSpec((None, D_MODEL, tm), lambda i: (i, 0, 0))]
    aliases, scratch = {}, []
    if w_own is not None:
        ins.append(w_own)
        in_specs.append(pl.BlockSpec(w_own.shape, lambda i: (0, 0)))
        out_shape.append(jax.ShapeDtypeStruct((seq, w_own.shape[1]), F32))
        out_specs.append(pl.BlockSpec((tm, w_own.shape[1]), lambda i: (i, 0)))
        scratch.append(pltpu.VMEM((2, half, w_own.shape[1]), BF16))
    if host:
        aliases = {len(ins) + k: n_own_out + v for k, v in host.get("aliases", {}).items()}
        ins += list(host["ins"])
        in_specs += [any_spec] * n_host_in
        out_shape += list(host["outs"])
        out_specs += [any_spec] * n_host_out
        scratch += list(host["sems"])
    res = pl.pallas_call(
        body, name="pre_norm", grid=(seq // tm,), in_specs=in_specs, out_specs=out_specs, out_shape=out_shape,
        input_output_aliases=aliases, scratch_shapes=scratch,
        compiler_params=pltpu.CompilerParams(dimension_semantics=("arbitrary",)),
    )(*ins)
    return res[0], res[1], (None if w_own is None else res[2]), res[n_own_out:]


def _pre_proj(u, w_in_g, own=None, host=None):
    seq = u.shape[0]
    tm = min(ROW_TILE, seq)
    n_nat, n_dil = len(_NATURAL), len(_DILATED) * len(B_DILS)
    rope = _rope_tables(seq, tm)

    n_host_in = len(host["ins"]) if host else 0
    n_host_out = len(host["outs"]) if host else 0
    n_own_out = n_nat + n_dil

    def body(u_ref, w_ref, rl_ref, rb_ref, *refs):
        if own:
            (chip_ref, pown_ref), refs = refs[:2], refs[2:]
        host_in, refs = refs[:n_host_in], refs[n_host_in:]
        nat = dict(zip(_NATURAL, refs[:n_nat]))
        res = {n: refs[n_nat + len(B_DILS) * k:n_nat + len(B_DILS) * (k + 1)] for k, n in enumerate(_DILATED)}
        host_out = refs[n_own_out:n_own_out + n_host_out]
        bufs = dict(zip(_DILATED, refs[n_own_out + n_host_out:]))
        sems = refs[n_own_out + n_host_out + len(_DILATED):]
        _host_phases(host, host_in, host_out, sems, seq // tm, before=True)

        def project(own_chip):
            ub = u_ref[...]
            c, sm, sp = _rope_coeffs(rl_ref, rb_ref)
            for j in range(N_CHIPS):
                pj = pown_ref[...] if j == own_chip else _dot(ub, w_ref[j])
                for b in range(SHARD_IN // LANES):
                    name, off, roped, scaled = _PROJ_LAYOUT[(SHARD_IN // LANES) * j + b]
                    piece = pj[:, LANES * b:LANES * (b + 1)]
                    if roped:
                        piece = _rope(piece, c, sm, sp)
                    if scaled:
                        piece = piece * SCALE
                    if name in bufs:
                        bufs[name][off // LANES] = piece
                    else:
                        nat[name][:, off:off + LANES] = piece.astype(BF16)
            for name in _DILATED:
                for ref, dil in zip(res[name], B_DILS):
                    _to_residues(bufs[name], ref, dil)

        if own:
            for chip in range(N_CHIPS):
                pl.when(chip_ref[0] == chip)(lambda chip=chip: project(chip))
        else:
            project(None)
        _host_phases(host, host_in, host_out, sems, seq // tm, before=False)

    row = lambda w: pl.BlockSpec((tm, w), lambda i: (i, 0))
    full = lambda a: pl.BlockSpec(a.shape, lambda i: (0,) * a.ndim)
    any_spec = pl.BlockSpec(memory_space=pl.ANY)
    out_shape = [jax.ShapeDtypeStruct((seq, _PROJ_WIDTH[n]), BF16) for n in _NATURAL]
    out_specs = [row(_PROJ_WIDTH[n]) for n in _NATURAL]
    for n in _DILATED:
        for dil in B_DILS:
            out_shape.append(jax.ShapeDtypeStruct((dil, seq // dil, B_W), BF16))
            out_specs.append(_residue_spec(dil, tm, B_W))
    ins = [u, w_in_g, *rope]
    in_specs = [row(D_MODEL), full(w_in_g), full(rope[0]), pl.BlockSpec((8, 2 * LANES), lambda i: (i, 0))]
    if own:
        ins += list(own)
        in_specs += [pl.BlockSpec(memory_space=pltpu.SMEM), row(SHARD_IN)]
    scratch = [_stage(tm, B_W)] * len(_DILATED)
    aliases = {}
    if host:
        aliases = {len(ins) + k: n_own_out + v for k, v in host.get("aliases", {}).items()}
        ins += list(host["ins"])
        in_specs += [any_spec] * n_host_in
        out_shape += list(host["outs"])
        out_specs += [any_spec] * n_host_out
        scratch += list(host["sems"])
    res = pl.pallas_call(
        body, name="pre_proj", grid=(seq // tm,), in_specs=in_specs, out_specs=out_specs, out_shape=out_shape,
        input_output_aliases=aliases, scratch_shapes=scratch,
        compiler_params=pltpu.CompilerParams(dimension_semantics=("arbitrary",)),
    )(*ins)
    out = dict(zip(_NATURAL, res[:n_nat]))
    for k, n in enumerate(_DILATED):
        out[n] = res[n_nat + len(B_DILS) * k:n_nat + len(B_DILS) * (k + 1)]
    out["hosted"] = res[n_own_out:]
    return out


def _band_bias(max_dist, transposed):
    i = np.arange(BLOCK)[:, None]
    j = np.arange(BLOCK)[None, :]
    if transposed:
        same = i <= j
        other = (j + BLOCK - i) <= max_dist
        vis = np.concatenate([same, other], axis=1)
    else:
        prev = (i + BLOCK - j) <= max_dist
        same = j <= i
        vis = np.concatenate([prev, same], axis=1)
    return jnp.asarray(np.where(vis, 0.0, NEG).astype(np.float32))


def _kv_place(h, gqa):
    return (0, h // 3) if gqa else (h // 2, h % 2)


def _band_fwd(q, k, v, sink, *, max_dist, name):
    dil, length, wq = q.shape
    wk = k.shape[2]
    gqa = wk != wq
    tq = min(ATTN_TILE, length)
    ns, nt = tq // BLOCK, length // tq
    npair = wq // LANES
    bias = _band_bias(max_dist, transposed=False)
    has_sink = sink is not None

    def body(*refs):
        if has_sink:
            sink_ref, refs = refs[0], refs[1:]
        q_ref, k_ref, kp_ref, v_ref, vp_ref, bias_ref, o_ref, lse_ref, kbuf, vbuf = refs[:10]
        i = pl.program_id(1)
        kbuf[0:BLOCK] = kp_ref[...]
        kbuf[BLOCK:] = k_ref[...]
        vbuf[0:BLOCK] = vp_ref[...]
        vbuf[BLOCK:] = v_ref[...]
        if gqa:
            kroll, vroll = refs[10:12]
            kroll[...] = pltpu.roll(kbuf[...], HEAD_DIM, 1)
            vroll[...] = pltpu.roll(vbuf[...], HEAD_DIM, 1)
        half = _half_masks(BLOCK)
        col_prev = (lax.broadcasted_iota(jnp.int32, (1, 2 * BLOCK), 1) < BLOCK).astype(F32)

        def score_matmuls(a):
            scores = []
            for p in range(npair):
                qp = q_ref[a * BLOCK:(a + 1) * BLOCK, p * LANES:(p + 1) * LANES]
                for e in range(2):
                    pk, ek = _kv_place(2 * p + e, gqa)
                    kw = (kbuf if ek == e else kroll)[a * BLOCK:(a + 2) * BLOCK, pk * LANES:(pk + 1) * LANES]
                    scores.append(_dot_nt(jnp.where(half[e], qp, jnp.zeros_like(qp)), kw))
            return scores

        pending = score_matmuls(0)
        for a in range(ns):
            r0 = a * BLOCK
            b = bias_ref[...]
            if a == 0:
                b = b + jnp.where(i == 0, NEG, 0.0) * col_prev
            scores = pending
            m_cols, l_cols, probs = [], [], []
            for h, s in enumerate(scores):
                s = s + b
                m = jnp.max(s, axis=1, keepdims=True)
                if has_sink:
                    m = jnp.maximum(m, sink_ref[h])
                pe = jnp.exp(s - m)
                l = jnp.sum(pe, axis=1, keepdims=True)
                if has_sink:
                    l = l + jnp.exp(sink_ref[h] - m)
                probs.append(pe.astype(BF16))
                m_cols.append(m)
                l_cols.append(l)
            pending = score_matmuls(a + 1) if a + 1 < ns else None
            for p in range(npair):
                o_h = []
                for e in range(2):
                    h = 2 * p + e
                    pk, ek = _kv_place(h, gqa)
                    vw = (vbuf if ek == e else vroll)[r0:r0 + 2 * BLOCK, pk * LANES:(pk + 1) * LANES]
                    o_h.append(_dot(probs[h], vw) * (1.0 / l_cols[h]))
                o_ref[r0:r0 + BLOCK, p * LANES:(p + 1) * LANES] = jnp.where(half[0], o_h[0], o_h[1]).astype(BF16)
            lse_ref[r0:r0 + BLOCK, :] = _per_head(m_cols) + jnp.log(_per_head(l_cols, 1.0))

    main = lambda w: pl.BlockSpec((None, tq, w), lambda r, i: (r, i, 0))
    prev = lambda w: pl.BlockSpec((None, BLOCK, w), lambda r, i: (r, jnp.maximum(i * ns - 1, 0), 0))
    in_specs = [main(wq), main(wk), prev(wk), main(wk), prev(wk), pl.BlockSpec(bias.shape, lambda r, i: (0, 0))]
    args = [q, k, k, v, v, bias]
    if has_sink:
        in_specs = [pl.BlockSpec(memory_space=pltpu.SMEM)] + in_specs
        args = [sink] + args
    scratch = [pltpu.VMEM((tq + BLOCK, wk), BF16)] * (4 if gqa else 2)
    return pl.pallas_call(
        body, name=name, grid=(dil, nt), in_specs=in_specs,
        out_specs=[main(wq), main(LANES)],
        out_shape=[jax.ShapeDtypeStruct((dil, length, wq), BF16), jax.ShapeDtypeStruct((dil, length, LANES), F32)],
        scratch_shapes=scratch,
    )(*args)


def _band_bwd(q, k, v, do, lse, delta, *, max_dist, name):
    dil, length, wq = q.shape
    wk = k.shape[2]
    gqa = wk != wq
    tq = min(ATTN_TILE, length)
    ns, nt = tq // BLOCK, length // tq
    npair = wq // LANES
    nblocks = length // BLOCK
    bias = _band_bias(max_dist, transposed=True)

    def body(q_ref, qn_ref, do_ref, don_ref, lse_ref, lsen_ref, dl_ref, dln_ref, k_ref, v_ref, bias_ref,
             dq_ref, dk_ref, dv_ref, stat_l, stat_d, dqt, kt, *rolled):
        i = pl.program_id(1)
        for pk in range(wk // LANES):
            kt[pk] = k_ref[:, pk * LANES:(pk + 1) * LANES].astype(F32).T.astype(BF16)
        if gqa:
            kroll, vroll, ktroll = rolled
            kroll[...] = pltpu.roll(k_ref[...], HEAD_DIM, 1)
            vroll[...] = pltpu.roll(v_ref[...], HEAD_DIM, 1)
            ktroll[0] = kroll[...].astype(F32).T.astype(BF16)
        for a in range(ns):
            rows = slice(a * BLOCK, (a + 1) * BLOCK)
            stat_l[a] = _rows_to_lanes(lse_ref[rows, :])
            stat_d[a] = _rows_to_lanes(dl_ref[rows, :])
        stat_l[ns] = _rows_to_lanes(lsen_ref[...])
        stat_d[ns] = _rows_to_lanes(dln_ref[...])

        @pl.when(i == 0)
        def _():
            dqt[:, :, 0:BLOCK] = jnp.zeros((npair, LANES, BLOCK), F32)

        @pl.when(i > 0)
        def _():
            dqt[:, :, 0:BLOCK] = dqt[:, :, tq:tq + BLOCK]

        dqt[:, :, BLOCK:] = jnp.zeros((npair, LANES, tq), F32)
        half2 = _half_masks(2 * BLOCK)
        row = lax.broadcasted_iota(jnp.int32, (LANES, BLOCK), 0)
        row_half = (row < HEAD_DIM, row >= HEAD_DIM)
        col_next = (lax.broadcasted_iota(jnp.int32, (1, 2 * BLOCK), 1) >= BLOCK).astype(F32)

        def scores(b):
            rows = slice(b * BLOCK, (b + 1) * BLOCK)
            nxt_rows = slice((b + 1) * BLOCK, (b + 2) * BLOCK)
            items = []
            for p in range(npair):
                lanes = slice(p * LANES, (p + 1) * LANES)
                q_next = q_ref[nxt_rows, lanes] if b + 1 < ns else qn_ref[:, lanes]
                do_next = do_ref[nxt_rows, lanes] if b + 1 < ns else don_ref[:, lanes]
                qw = jnp.concatenate([q_ref[rows, lanes], q_next], axis=0)
                dow = jnp.concatenate([do_ref[rows, lanes], do_next], axis=0)
                for e in range(2):
                    h = 2 * p + e
                    pk, ek = _kv_place(h, gqa)
                    klanes = slice(pk * LANES, (pk + 1) * LANES)
                    kb = (k_ref if ek == e else kroll)[rows, klanes]
                    vb = (v_ref if ek == e else vroll)[rows, klanes]
                    qm = jnp.where(half2[e], qw, jnp.zeros_like(qw))
                    dom = jnp.where(half2[e], dow, jnp.zeros_like(dow))
                    items.append(dict(p=p, e=e, h=h, pk=pk, ek=ek, qm=qm, dom=dom,
                                      st=_dot_nt(kb, qm), dpt=_dot_nt(vb, dom)))
            return items

        def probs(b, items):
            bt = bias_ref[...]
            if b == ns - 1:
                bt = bt + jnp.where(i == nt - 1, NEG, 0.0) * col_next
            for it in items:
                h = it["h"]
                lrow = jnp.concatenate([stat_l[b, h:h + 1, :], stat_l[b + 1, h:h + 1, :]], axis=1)
                drow = jnp.concatenate([stat_d[b, h:h + 1, :], stat_d[b + 1, h:h + 1, :]], axis=1)
                pt = jnp.exp(it["st"] + bt - lrow)
                it["ptb"] = pt.astype(BF16)
                it["dsb"] = (pt * (it["dpt"] - drow)).astype(BF16)

        pending = scores(0)
        for b in range(ns):
            rows = slice(b * BLOCK, (b + 1) * BLOCK)
            window = slice(b * BLOCK, (b + 2) * BLOCK)
            acc = {}
            items = pending
            probs(b, items)
            pending = scores(b + 1) if b + 1 < ns else None
            for p in range(npair):
                pair = items[2 * p:2 * p + 2]
                lanes = slice(p * LANES, (p + 1) * LANES)
                kparts = []
                for it in pair:
                    kbt = (kt if it["ek"] == it["e"] else ktroll)[it["pk"], :, rows]
                    kparts.append(jnp.where(row_half[it["e"]], kbt, jnp.zeros_like(kbt)))
                ds_keys = jnp.concatenate([it["dsb"] for it in pair], axis=0)
                dqt[p, :, window] += _dot(jnp.concatenate(kparts, axis=1), ds_keys)
                if not gqa:
                    q_both = jnp.concatenate([it["qm"] for it in pair], axis=0)
                    do_both = jnp.concatenate([it["dom"] for it in pair], axis=0)
                    dk_ref[rows, lanes] = _dot(jnp.concatenate([it["dsb"] for it in pair], axis=1), q_both).astype(BF16)
                    dv_ref[rows, lanes] = _dot(jnp.concatenate([it["ptb"] for it in pair], axis=1), do_both).astype(BF16)
                else:
                    for it in pair:
                        dv_c = _dot(it["ptb"], it["dom"])
                        dk_c = _dot(it["dsb"], it["qm"])
                        key = (it["pk"], it["ek"] == it["e"])
                        if key in acc:
                            acc[key] = (acc[key][0] + dk_c, acc[key][1] + dv_c)
                        else:
                            acc[key] = (dk_c, dv_c)
            if gqa:
                dk_al, dv_al = acc[(0, True)]
                dk_mis, dv_mis = acc[(0, False)]
                dk_ref[rows, :] = (dk_al + pltpu.roll(dk_mis, HEAD_DIM, 1)).astype(BF16)
                dv_ref[rows, :] = (dv_al + pltpu.roll(dv_mis, HEAD_DIM, 1)).astype(BF16)

        for p in range(npair):
            dq_ref[:, p * LANES:(p + 1) * LANES] = dqt[p, :, 0:tq].T.astype(BF16)

    main = lambda w: pl.BlockSpec((None, tq, w), lambda r, i: (r, i, 0))
    nxt = lambda w: pl.BlockSpec((None, BLOCK, w), lambda r, i: (r, jnp.minimum((i + 1) * ns, nblocks - 1), 0))
    scratch = [pltpu.VMEM((ns + 1, 8, LANES), F32), pltpu.VMEM((ns + 1, 8, LANES), F32),
               pltpu.VMEM((npair, LANES, tq + BLOCK), F32), pltpu.VMEM((wk // LANES, LANES, tq), BF16)]
    if gqa:
        scratch = scratch + [pltpu.VMEM((tq, wk), BF16)] * 2 + [pltpu.VMEM((1, LANES, tq), BF16)]
    return pl.pallas_call(
        body, name=name, grid=(dil, nt),
        in_specs=[main(wq), nxt(wq), main(wq), nxt(wq), main(LANES), nxt(LANES), main(LANES), nxt(LANES),
                  main(wk), main(wk), pl.BlockSpec(bias.shape, lambda r, i: (0, 0))],
        out_specs=[main(wq), main(wk), main(wk)],
        out_shape=[jax.ShapeDtypeStruct((dil, length, wq), BF16), jax.ShapeDtypeStruct((dil, length, wk), BF16),
                   jax.ShapeDtypeStruct((dil, length, wk), BF16)],
        scratch_shapes=scratch,
        compiler_params=pltpu.CompilerParams(dimension_semantics=("arbitrary", "arbitrary")),
    )(q, q, do, do, lse, lse, delta, delta, k, v, bias)


def _mem_attn_fwd(q, mk, mv):
    seq = q.shape[0]
    tq = min(ATTN_TILE, seq)
    sub_rows = min(4 * BLOCK, tq)
    ns = tq // sub_rows

    def body(q_ref, mk_ref, mv_ref, o_ref, lse_ref):
        half = _half_masks(sub_rows)

        def sub(a, carry):
            r0 = pl.multiple_of(a * sub_rows, sub_rows)
            scores = []
            for p in range(C_W // LANES):
                lanes = slice(p * LANES, (p + 1) * LANES)
                qp = q_ref[pl.ds(r0, sub_rows), lanes]
                for e in range(2):
                    scores.append(_dot_nt(jnp.where(half[e], qp, jnp.zeros_like(qp)), mk_ref[:, lanes]))
            m_cols, l_cols, probs = [], [], []
            for s in scores:
                m = jnp.max(s, axis=1, keepdims=True)
                pe = jnp.exp(s - m)
                probs.append(pe.astype(BF16))
                m_cols.append(m)
                l_cols.append(jnp.sum(pe, axis=1, keepdims=True))
            for p in range(C_W // LANES):
                lanes = slice(p * LANES, (p + 1) * LANES)
                o_h = [_dot(probs[2 * p + e], mv_ref[:, lanes]) * (1.0 / l_cols[2 * p + e]) for e in range(2)]
                o_ref[pl.ds(r0, sub_rows), lanes] = jnp.where(half[0], o_h[0], o_h[1]).astype(BF16)
            lse_ref[pl.ds(r0, sub_rows), :] = _per_head(m_cols) + jnp.log(_per_head(l_cols, 1.0))
            return carry

        lax.fori_loop(0, ns, sub, 0, unroll=True)

    row = lambda w: pl.BlockSpec((tq, w), lambda i: (i, 0))
    full = pl.BlockSpec((N_MEM, C_W), lambda i: (0, 0))
    return pl.pallas_call(
        body, name="mem_attn_fwd", grid=(seq // tq,), in_specs=[row(C_W), full, full],
        out_specs=[row(C_W), row(LANES)],
        out_shape=[jax.ShapeDtypeStruct((seq, C_W), BF16), jax.ShapeDtypeStruct((seq, LANES), F32)],
    )(q, mk, mv)


def _mem_attn_bwd(q, mk, mv, do, lse, delta):
    seq = q.shape[0]
    tq = min(ATTN_TILE, seq)
    ns = tq // BLOCK
    npair = C_W // LANES

    def body(q_ref, mk_ref, mv_ref, do_ref, lse_ref, dl_ref, dq_ref, dmk_ref, dmv_ref, stat_l, stat_d, mkt, dqt):
        @pl.when(pl.program_id(0) == 0)
        def _():
            dmk_ref[...] = jnp.zeros_like(dmk_ref)
            dmv_ref[...] = jnp.zeros_like(dmv_ref)
            for p in range(npair):
                mkt[p] = mk_ref[:, p * LANES:(p + 1) * LANES].astype(F32).T.astype(BF16)

        for a in range(ns):
            rows = slice(a * BLOCK, (a + 1) * BLOCK)
            stat_l[a] = _rows_to_lanes(lse_ref[rows, :])
            stat_d[a] = _rows_to_lanes(dl_ref[rows, :])
        span = min(2, ns)
        half = _half_masks(span * BLOCK)
        row = lax.broadcasted_iota(jnp.int32, (LANES, N_MEM), 0)
        row_half = (row < HEAD_DIM, row >= HEAD_DIM)

        for a in range(0, ns, span):
            rows = slice(a * BLOCK, (a + span) * BLOCK)
            items = []
            for p in range(npair):
                lanes = slice(p * LANES, (p + 1) * LANES)
                qp = q_ref[rows, lanes]
                dop = do_ref[rows, lanes]
                for e in range(2):
                    qm = jnp.where(half[e], qp, jnp.zeros_like(qp))
                    dom = jnp.where(half[e], dop, jnp.zeros_like(dop))
                    items.append(dict(p=p, e=e, qm=qm, dom=dom, st=_dot_nt(mk_ref[:, lanes], qm),
                                      dpt=_dot_nt(mv_ref[:, lanes], dom)))
            for it in items:
                h = 2 * it["p"] + it["e"]
                lrow = jnp.concatenate([stat_l[a + k, h:h + 1, :] for k in range(span)], axis=1)
                drow = jnp.concatenate([stat_d[a + k, h:h + 1, :] for k in range(span)], axis=1)
                pt = jnp.exp(it["st"] - lrow)
                it["ptb"] = pt.astype(BF16)
                it["dsb"] = (pt * (it["dpt"] - drow)).astype(BF16)
            for p in range(npair):
                lanes = slice(p * LANES, (p + 1) * LANES)
                pair = [it for it in items if it["p"] == p]
                join = lambda name, axis: jnp.concatenate([it[name] for it in pair], axis=axis)
                dmv_ref[:, lanes] += _dot(join("ptb", 1), join("dom", 0))
                dmk_ref[:, lanes] += _dot(join("dsb", 1), join("qm", 0))
                kbt = mkt[p]
                k_both = jnp.concatenate([jnp.where(row_half[e], kbt, jnp.zeros_like(kbt)) for e in range(2)], axis=1)
                dqt[p, :, rows] = _dot(k_both, join("dsb", 0))
        for p in range(npair):
            dq_ref[:, p * LANES:(p + 1) * LANES] = dqt[p].T.astype(BF16)

    row = lambda w: pl.BlockSpec((tq, w), lambda i: (i, 0))
    full = pl.BlockSpec((N_MEM, C_W), lambda i: (0, 0))
    return pl.pallas_call(
        body, name="mem_attn_bwd", grid=(seq // tq,),
        in_specs=[row(C_W), full, full, row(C_W), row(LANES), row(LANES)], out_specs=[row(C_W), full, full],
        out_shape=[jax.ShapeDtypeStruct((seq, C_W), BF16), jax.ShapeDtypeStruct((N_MEM, C_W), F32),
                   jax.ShapeDtypeStruct((N_MEM, C_W), F32)],
        scratch_shapes=[pltpu.VMEM((ns, 8, LANES), F32)] * 2
        + [pltpu.VMEM((npair, LANES, N_MEM), BF16), pltpu.VMEM((npair, LANES, tq), F32)],
        compiler_params=pltpu.CompilerParams(dimension_semantics=("arbitrary",)),
    )(q, mk, mv, do, lse, delta)


def _silu_and_grad(g):
    s = 1.0 / (1.0 + jnp.exp(-g))
    return g * s, s * (1.0 + g * (1.0 - s))


def _post(x, target, post_norm, w_out, sink_row, oa, lse_a, ga, ob_list, lseb_list, gb, oc, gc):
    seq = x.shape[0]
    tm = min(ROW_TILE, seq)
    inv_d = 1.0 / D_MODEL
    nd = len(B_DILS)

    def body(*refs):
        (x_ref, t_ref, gp_ref, w_ref, sink_ref, oa_ref, lsea_ref, ga_ref), refs = refs[:8], refs[8:]
        ob_refs, lb_refs, (gb_ref, oc_ref, gc_ref), refs = refs[:nd], refs[nd:2 * nd], refs[2 * nd:2 * nd + 3], refs[2 * nd + 3:]
        (g_ref, doa_ref, dla_ref, dga_ref), refs = refs[:4], refs[4:]
        dob_refs, lsec_refs, dlb_refs, refs = refs[:nd], refs[nd:2 * nd], refs[2 * nd:3 * nd], refs[3 * nd:]
        (dgb_ref, doc_ref, dlc_ref, dgc_ref, gw_ref, gpost_ref, gsink_ref, loss_ref), refs = refs[:8], refs[8:]
        ycat, obufs, lbufs, st_do, st_l, st_d = refs[0], refs[1:nd], refs[nd:2 * nd - 1], refs[2 * nd - 1], refs[2 * nd], refs[2 * nd + 1]

        @pl.when(pl.program_id(0) == 0)
        def _():
            gw_ref[...] = jnp.zeros_like(gw_ref)
            gpost_ref[...] = jnp.zeros_like(gpost_ref)
            gsink_ref[...] = jnp.zeros_like(gsink_ref)
            loss_ref[...] = jnp.zeros_like(loss_ref)

        o_i, l_i = [ob_refs[0][0].astype(F32)], [lb_refs[0][0]]
        for k in range(1, nd):
            _from_residues(ob_refs[k], obufs[k - 1], B_DILS[k])
            _from_residues(lb_refs[k], lbufs[k - 1], B_DILS[k])
            o_i.append(_stage_read(obufs[k - 1]))
            l_i.append(_stage_read(lbufs[k - 1]))
        mx = l_i[0]
        for l in l_i[1:]:
            mx = jnp.maximum(mx, l)
        w_i = [jnp.exp(l - mx) for l in l_i]
        z = w_i[0]
        for w in w_i[1:]:
            z = z + w
        _stage_write(st_l, mx + jnp.log(z))
        expand = _head_expand_matrix(B_W)
        inv_z = 1.0 / z
        ob = None
        for w, o in zip(w_i, o_i):
            term = _dot_split(w * inv_z, expand, 2) * o
            ob = term if ob is None else ob + term
        oa, oc = oa_ref[...].astype(F32), oc_ref[...].astype(F32)
        sa, dsa = _silu_and_grad(ga_ref[...].astype(F32))
        sb, dsb = _silu_and_grad(gb_ref[...].astype(F32))
        sc, dsc = _silu_and_grad(gc_ref[...].astype(F32))
        ycat[:, 0:A_W] = (oa * sa).astype(BF16)
        ycat[:, A_W:A_W + B_W] = (ob * sb).astype(BF16)
        ycat[:, A_W + B_W:] = (oc * sc).astype(BF16)
        y2 = _dot(ycat[...], w_ref[...])
        r = lax.rsqrt(jnp.mean(y2 * y2, axis=-1, keepdims=True) + RMS_EPS)
        zhat = y2 * r
        gp = gp_ref[...]
        err = x_ref[...] + zhat * gp - t_ref[...]
        loss_ref[...] += jnp.sum(err * err) * (0.5 * inv_d)
        g = err * inv_d
        g_ref[...] = g
        gpost_ref[...] += jnp.sum(g * zhat, axis=0, keepdims=True)
        a = g * gp
        dy2 = (r * (a - zhat * jnp.mean(a * zhat, axis=-1, keepdims=True))).astype(BF16)
        gw_ref[...] += _dot_tn(ycat[...], dy2)
        dycat = _dot_nt(dy2, w_ref[...])
        dya, dyb, dyc = dycat[:, 0:A_W], dycat[:, A_W:A_W + B_W], dycat[:, A_W + B_W:]
        doa, dob, doc = dya * sa, dyb * sb, dyc * sc
        doa_ref[...] = doa.astype(BF16)
        doc_ref[...] = doc.astype(BF16)
        dga_ref[...] = (dya * oa * dsa).astype(BF16)
        dgb_ref[...] = (dyb * ob * dsb).astype(BF16)
        dgc_ref[...] = (dyc * oc * dsc).astype(BF16)
        dl_a = _dot_split(doa * oa, _head_sum_matrix(A_W), 2)
        dla_ref[...] = dl_a
        dlc_ref[...] = _dot_split(doc * oc, _head_sum_matrix(C_W), 2)
        gsink_ref[...] += jnp.sum(jnp.exp(sink_ref[...] - lsea_ref[...]) * dl_a, axis=0, keepdims=True)
        _stage_write(st_do, dob)
        _stage_write(st_d, _dot_split(dob * ob, _head_sum_matrix(B_W), 2))
        for k, dil in enumerate(B_DILS):
            _to_residues(st_do, dob_refs[k], dil)
            _to_residues(st_l, lsec_refs[k], dil)
            _to_residues(st_d, dlb_refs[k], dil)

    row = lambda w: pl.BlockSpec((tm, w), lambda i: (i, 0))
    full = lambda shape: pl.BlockSpec(shape, lambda i: (0,) * len(shape))
    res_specs = lambda w: [_residue_spec(d, tm, w) for d in B_DILS]
    res_shapes = lambda w, dt: [jax.ShapeDtypeStruct((d, seq // d, w), dt) for d in B_DILS]
    ins = [x, target, post_norm, w_out, sink_row, oa, lse_a, ga, *ob_list, *lseb_list, gb, oc, gc]
    in_specs = ([row(D_MODEL), row(D_MODEL), full((1, D_MODEL)), full((D_MODEL, D_MODEL)), full((1, LANES)),
                 row(A_W), row(LANES), row(A_W)] + res_specs(B_W) + res_specs(LANES) + [row(B_W), row(C_W), row(C_W)])
    out_shape = ([jax.ShapeDtypeStruct((seq, D_MODEL), F32), jax.ShapeDtypeStruct((seq, A_W), BF16),
                  jax.ShapeDtypeStruct((seq, LANES), F32), jax.ShapeDtypeStruct((seq, A_W), BF16)]
                 + res_shapes(B_W, BF16) + res_shapes(LANES, F32) + res_shapes(LANES, F32)
                 + [jax.ShapeDtypeStruct((seq, B_W), BF16), jax.ShapeDtypeStruct((seq, C_W), BF16),
                    jax.ShapeDtypeStruct((seq, LANES), F32), jax.ShapeDtypeStruct((seq, C_W), BF16),
                    jax.ShapeDtypeStruct((D_MODEL, D_MODEL), F32), jax.ShapeDtypeStruct((1, D_MODEL), F32),
                    jax.ShapeDtypeStruct((1, LANES), F32), jax.ShapeDtypeStruct((1, LANES), F32)])
    out_specs = ([row(D_MODEL), row(A_W), row(LANES), row(A_W)] + res_specs(B_W) + res_specs(LANES) + res_specs(LANES)
                 + [row(B_W), row(C_W), row(LANES), row(C_W),
                    full((D_MODEL, D_MODEL)), full((1, D_MODEL)), full((1, LANES)), full((1, LANES))])
    scratch = ([pltpu.VMEM((tm, D_MODEL), BF16)] + [_stage(tm, B_W)] * (nd - 1) + [_stage(tm, LANES)] * (nd - 1)
               + [_stage(tm, B_W), _stage(tm, LANES), _stage(tm, LANES)])
    res = pl.pallas_call(
        body, name="post", grid=(seq // tm,), in_specs=in_specs, out_specs=out_specs, out_shape=out_shape,
        scratch_shapes=scratch,
        compiler_params=pltpu.CompilerParams(dimension_semantics=("arbitrary",)),
    )(*ins)
    out = dict(g=res[0], doa=res[1], dl_a=res[2], dga=res[3], dob=res[4:4 + nd], lse_b=res[4 + nd:4 + 2 * nd],
               dl_b=res[4 + 2 * nd:4 + 3 * nd])
    rest = res[4 + 3 * nd:]
    out.update(dgb=rest[0], doc=rest[1], dl_c=rest[2], dgc=rest[3], gw_out=rest[4], gpost=rest[5], gsink=rest[6],
               loss=rest[7])
    return out


def _grad_w_in(ut, nat, res):
    tm = ut.shape[2]
    seq = ut.shape[0] * tm
    nd = len(B_DILS)
    nat_list = [nat[n] for n in _NATURAL]
    res_list = [a for n in _DILATED for a in res[n]]
    rope = _rope_tables(seq, tm)

    def body(rl_ref, rb_ref, ut_ref, *refs):
        nat_refs = dict(zip(_NATURAL, refs[:len(_NATURAL)]))
        refs = refs[len(_NATURAL):]
        res_refs = {n: refs[nd * k:nd * (k + 1)] for k, n in enumerate(_DILATED)}
        refs = refs[nd * len(_DILATED):]
        dproj_ref, gw_ref = refs[:2]
        bufs = {n: refs[2 + (nd - 1) * k:2 + (nd - 1) * (k + 1)] for k, n in enumerate(_DILATED)}

        @pl.when(pl.program_id(0) == 0)
        def _():
            gw_ref[...] = jnp.zeros_like(gw_ref)

        for n in _DILATED:
            for k in range(1, nd):
                _from_residues(res_refs[n][k], bufs[n][k - 1], B_DILS[k])
        c, sm, sp = _rope_coeffs(rl_ref, rb_ref)
        sm, sp = -sm, -sp
        for blk, (name, off, roped, scaled) in enumerate(_PROJ_LAYOUT):
            lanes = slice(off, off + LANES)
            if name in nat_refs:
                piece = nat_refs[name][:, lanes].astype(F32)
            else:
                piece = res_refs[name][0][0, :, lanes].astype(F32)
                for buf in bufs[name]:
                    piece = piece + buf[off // LANES]
            if roped:
                piece = _rope(piece, c, sm, sp)
            if scaled:
                piece = piece * SCALE
            dproj_ref[:, blk * LANES:(blk + 1) * LANES] = piece.astype(BF16)
        for j in range(N_CHIPS):
            gw_ref[j] += _dot(ut_ref[...], dproj_ref[:, j * SHARD_IN:(j + 1) * SHARD_IN])

    row = lambda w: pl.BlockSpec((tm, w), lambda i: (i, 0))
    in_specs = ([pl.BlockSpec(rope[0].shape, lambda i: (0, 0)), pl.BlockSpec((8, 2 * LANES), lambda i: (i, 0)),
                 pl.BlockSpec((None, D_MODEL, tm), lambda i: (i, 0, 0))]
                + [row(a.shape[1]) for a in nat_list]
                + [_residue_spec(d, tm, B_W) for _ in _DILATED for d in B_DILS])
    return pl.pallas_call(
        body, name="grad_w_in", grid=(seq // tm,), in_specs=in_specs,
        out_specs=[row(D_IN), pl.BlockSpec((N_CHIPS, D_MODEL, SHARD_IN), lambda i: (0, 0, 0))],
        out_shape=[jax.ShapeDtypeStruct((seq, D_IN), BF16), jax.ShapeDtypeStruct((N_CHIPS, D_MODEL, SHARD_IN), F32)],
        scratch_shapes=[_stage(tm, B_W)] * ((nd - 1) * len(_DILATED)),
        compiler_params=pltpu.CompilerParams(dimension_semantics=("arbitrary",)),
    )(*rope, ut, *nat_list, *res_list)


def _input_grad(x, g, pre_norm, w_in_g, dproj, gx_prev, span, after, name):
    seq = x.shape[0]
    tm = seq // INPUT_GRAD_TILES
    first_block, steps = span

    def body(*refs):
        x_ref, g_ref, gp_ref, w_ref, dp_ref = refs[:5]
        gx_ref, gpre_ref = refs[-2:]

        @pl.when(pl.program_id(0) == 0)
        def _():
            gpre_ref[...] = jnp.zeros_like(gpre_ref)

        du = None
        for j in range(N_CHIPS):
            term = _dot_nt(dp_ref[:, j * SHARD_IN:(j + 1) * SHARD_IN], w_ref[j])
            du = term if du is None else du + term
        xv = x_ref[...]
        r = lax.rsqrt(jnp.mean(xv * xv, axis=-1, keepdims=True) + RMS_EPS)
        xhat = xv * r
        gpre_ref[...] += jnp.sum(du * xhat, axis=0, keepdims=True)
        a = du * gp_ref[...]
        gx_ref[...] = g_ref[...] + r * (a - xhat * jnp.mean(a * xhat, axis=-1, keepdims=True))

    row = lambda w: pl.BlockSpec((tm, w), lambda i: (first_block + i, 0))
    full = lambda a: pl.BlockSpec(a.shape, lambda i: (0,) * a.ndim)
    any_spec = pl.BlockSpec(memory_space=pl.ANY)
    ins = [x, g, pre_norm, w_in_g, dproj]
    in_specs = [row(D_MODEL), row(D_MODEL), full(pre_norm), full(w_in_g), row(D_IN)]
    aliases = {}
    if gx_prev is not None:
        aliases[len(ins)] = 0
        ins.append(gx_prev)
        in_specs.append(any_spec)
    if after is not None:
        ins.append(after)
        in_specs.append(any_spec)
    return pl.pallas_call(
        body, name=name, grid=(steps,), in_specs=in_specs,
        out_specs=[row(D_MODEL), pl.BlockSpec((1, D_MODEL), lambda i: (0, 0))],
        out_shape=[jax.ShapeDtypeStruct((seq, D_MODEL), F32), jax.ShapeDtypeStruct((1, D_MODEL), F32)],
        input_output_aliases=aliases,
        compiler_params=pltpu.CompilerParams(dimension_semantics=("arbitrary",)),
    )(*ins)


def _exchange_start(ex, name):
    n_in, n_out, n_sem = len(ex["ins"]), len(ex["outs"]), len(ex["sems"])

    def body(*refs):
        in_refs, land_refs, sems = refs[:n_in], refs[n_in:n_in + n_out], refs[n_in + n_out:n_in + n_out + n_sem]
        ex["start"](in_refs, land_refs, *sems)
        token = refs[-1]
        token[...] = jnp.zeros_like(token)

    hbm = pl.BlockSpec(memory_space=pltpu.HBM)
    sem = pl.BlockSpec(memory_space=pltpu.SEMAPHORE)
    ins = [pltpu.with_memory_space_constraint(a, pltpu.HBM) for a in ex["ins"]]
    landing = [pltpu.with_memory_space_constraint(lax.empty(o.shape, o.dtype), pltpu.HBM) for o in ex["outs"]]
    res = pl.pallas_call(
        body, name=name,
        out_shape=list(ex["sems"]) + [pltpu.HBM(a.shape, a.dtype) for a in ex["ins"]]
        + [pltpu.HBM(o.shape, o.dtype) for o in ex["outs"]] + [jax.ShapeDtypeStruct((8, LANES), F32)],
        in_specs=[hbm] * (n_in + n_out),
        out_specs=[sem] * n_sem + [hbm] * (n_in + n_out) + [pl.BlockSpec(memory_space=pltpu.VMEM)],
        input_output_aliases={k: n_sem + k for k in range(n_in + n_out)},
        compiler_params=pltpu.CompilerParams(has_side_effects=pltpu.SideEffectType.DATAFLOW_SIDE_EFFECTING),
    )(*ins, *landing)
    return res[:-1], res[-1]


def _exchange_wait(ex, handles, after, name):
    n_in, n_out, n_sem = len(ex["ins"]), len(ex["outs"]), len(ex["sems"])
    sems, thru = handles[:n_sem], handles[n_sem:]

    def body(*refs):
        in_refs, land_refs = refs[:n_in], refs[n_in:n_in + n_out]
        sem_refs = refs[n_in + n_out:n_in + n_out + n_sem]
        ex["finish"](in_refs, land_refs, *sem_refs)

    hbm = pl.BlockSpec(memory_space=pltpu.HBM)
    sem = pl.BlockSpec(memory_space=pltpu.SEMAPHORE)
    res = pl.pallas_call(
        body, name=name,
        out_shape=[pltpu.HBM(a.shape, a.dtype) for a in thru],
        in_specs=[hbm] * (n_in + n_out) + [sem] * n_sem + [pl.BlockSpec(memory_space=pl.ANY)],
        out_specs=[hbm] * (n_in + n_out),
        input_output_aliases={k: k for k in range(n_in + n_out)},
        compiler_params=pltpu.CompilerParams(has_side_effects=pltpu.SideEffectType.DATAFLOW_SIDE_EFFECTING),
    )(*thru, *sems, after)
    return res[:n_in], res[n_in:]


def _start_finish(build):
    def start(*refs):
        for cp in build(*refs):
            cp.start()

    def finish(*refs):
        for cp in build(*refs):
            cp.wait()

    return dict(start=start, finish=finish)


def _pair_exchange(grads):
    n = len(grads)

    def build(srcs, outs, send_sems, recv_sems):
        x, y, c = lax.axis_index("x"), lax.axis_index("y"), lax.axis_index("c")
        copies = []
        for t in range(n):
            rows = grads[t].shape[1] // 2
            copies.append(pltpu.make_async_remote_copy(
                src_ref=srcs[t].at[:, pl.ds((1 - c) * rows, rows)], dst_ref=outs[t],
                send_sem=send_sems.at[t], recv_sem=recv_sems.at[t], device_id=(x, y, 1 - c), device_id_type=MESH))
        return copies

    return dict(ins=list(grads), **_start_finish(build),
                outs=[jax.ShapeDtypeStruct((g.shape[0], g.shape[1] // 2, g.shape[2]), g.dtype) for g in grads],
                sems=[pltpu.SemaphoreType.DMA((n,)), pltpu.SemaphoreType.DMA((n,))])


def _pair_add(core, owns, gots):
    n = len(owns)

    def body(core_ref, *refs):
        for t in range(n):
            refs[2 * n + t][...] = (refs[t][...] + refs[n + t][...]).astype(BF16)

    halves = [(None,) + g.shape[1:] for g in gots]
    grid_spec = pltpu.PrefetchScalarGridSpec(
        num_scalar_prefetch=1, grid=(N_CHIPS,),
        in_specs=[pl.BlockSpec(h, lambda k, core_ref: (k, core_ref[0], 0)) for h in halves]
        + [pl.BlockSpec(h, lambda k, core_ref: (k, 0, 0)) for h in halves],
        out_specs=[pl.BlockSpec(h, lambda k, core_ref: (k, 0, 0)) for h in halves])
    return pl.pallas_call(
        body, name="pair_add", grid_spec=grid_spec,
        out_shape=[jax.ShapeDtypeStruct(g.shape, BF16) for g in gots],
    )(core, *owns, *gots)


def _chip_exchange(parts):
    n = len(parts)

    def build(srcs, outs, send_sems, recv_sems, local_sems):
        x, y, c = lax.axis_index("x"), lax.axis_index("y"), lax.axis_index("c")
        my_chip = 2 * x + y
        chips = [(1 - x, y), (x, 1 - y), (1 - x, 1 - y)]
        copies = [pltpu.make_async_copy(srcs[t].at[my_chip], outs[t].at[my_chip], local_sems.at[t]) for t in range(n)]
        for j, (cx, cy) in enumerate(chips):
            for t in range(n):
                k = n * j + t
                copies.append(pltpu.make_async_remote_copy(
                    src_ref=srcs[t].at[2 * cx + cy], dst_ref=outs[t].at[my_chip], send_sem=send_sems.at[k],
                    recv_sem=recv_sems.at[k], device_id=(cx, cy, c), device_id_type=MESH))
        return copies

    return dict(ins=list(parts), **_start_finish(build), outs=[jax.ShapeDtypeStruct(p.shape, p.dtype) for p in parts],
                sems=[pltpu.SemaphoreType.DMA((3 * n,)), pltpu.SemaphoreType.DMA((3 * n,)),
                      pltpu.SemaphoreType.DMA((n,))])


def _chip_sum(core, slots):
    n = len(slots)

    def body(core_ref, *refs):
        for t in range(n):
            acc = refs[t][0].astype(F32)
            for s in range(1, N_CHIPS):
                acc = acc + refs[t][s].astype(F32)
            refs[n + t][...] = acc

    blocks = [(s.shape[1] // TAIL_STEPS, s.shape[2]) for s in slots]
    grid_spec = pltpu.PrefetchScalarGridSpec(
        num_scalar_prefetch=1, grid=(TAIL_STEPS,),
        in_specs=[pl.BlockSpec((N_CHIPS,) + b, lambda i, core_ref: (0, i, 0)) for b in blocks],
        out_specs=[pl.BlockSpec((None,) + b, lambda i, core_ref: (core_ref[0], i, 0)) for b in blocks])
    return pl.pallas_call(
        body, name="chip_sum", grid_spec=grid_spec,
        out_shape=[jax.ShapeDtypeStruct((2,) + s.shape[1:], F32) for s in slots],
    )(core, *slots)


def _pair_gather(bufs, small):
    n = len(bufs)

    def body(*refs):
        small_ref, outs, small_out = refs[n], refs[n + 1:2 * n + 1], refs[2 * n + 1]
        send_sems, recv_sems, local_sem = refs[2 * n + 2:]
        x, y, c = lax.axis_index("x"), lax.axis_index("y"), lax.axis_index("c")
        me = 4 * x + 2 * y + c
        chips = [(1 - x, y), (x, 1 - y), (1 - x, 1 - y)]
        mine = pltpu.make_async_copy(small_ref, small_out.at[me], local_sem)
        mine.start()
        copies = [pltpu.make_async_remote_copy(
            src_ref=outs[t].at[c], dst_ref=outs[t].at[c], send_sem=send_sems.at[t], recv_sem=recv_sems.at[t],
            device_id=(x, y, 1 - c), device_id_type=MESH) for t in range(n)]
        peers = [(x, y, 1 - c)] + [(cx, cy, cc) for (cx, cy) in chips for cc in (c, 1 - c)]
        for j, peer in enumerate(peers):
            copies.append(pltpu.make_async_remote_copy(
                src_ref=small_ref, dst_ref=small_out.at[me], send_sem=send_sems.at[n + j],
                recv_sem=recv_sems.at[n + j], device_id=peer, device_id_type=MESH))
        for cp in copies:
            cp.start()
        for cp in copies:
            cp.wait()
        mine.wait()

    any_spec = pl.BlockSpec(memory_space=pl.ANY)
    res = pl.pallas_call(
        body, name="pair_gather",
        out_shape=[jax.ShapeDtypeStruct(b.shape, b.dtype) for b in bufs]
        + [jax.ShapeDtypeStruct((8,) + small.shape, small.dtype)],
        in_specs=[any_spec] * (n + 1), out_specs=[any_spec] * (n + 1),
        input_output_aliases={t: t for t in range(n)},
        scratch_shapes=[pltpu.SemaphoreType.DMA((n + 7,)), pltpu.SemaphoreType.DMA((n + 7,)),
                        pltpu.SemaphoreType.DMA],
    )(*bufs, small)
    return [r.reshape(2 * b.shape[1], b.shape[2]) for r, b in zip(res[:n], bufs)], res[n]


def _adamw(ws, gs, ms, vs):
    n = len(ws)

    def body(*refs):
        for t in range(n):
            w_ref, g_ref, m_ref, v_ref = refs[t:4 * n:n]
            gout_ref, d_ref, nm_ref, nv_ref = refs[4 * n + t::n]
            g = g_ref[...]
            gout_ref[...] = g
            d_ref[...], nm_ref[...], nv_ref[...] = _adamw_math(w_ref[...], g, m_ref[...], v_ref[...])

    specs = [pl.BlockSpec((w.shape[0] // TAIL_STEPS, w.shape[1]), lambda i: (i, 0)) for w in ws]
    res = pl.pallas_call(
        body, name="adamw", grid=(TAIL_STEPS,), in_specs=specs * 4, out_specs=specs * 4,
        out_shape=[jax.ShapeDtypeStruct(w.shape, F32) for w in ws] * 4,
    )(*ws, *gs, *ms, *vs)
    return [res[t::n] for t in range(n)]


def _adamw_math(w, g, m, v):
    c1 = 1.0 / (1.0 - ADAM_B1 ** ADAM_STEP)
    c2 = 1.0 / (1.0 - ADAM_B2 ** ADAM_STEP)
    nm = ADAM_B1 * m + (1.0 - ADAM_B1) * g
    nv = ADAM_B2 * v + (1.0 - ADAM_B2) * (g * g)
    return -ADAM_LR * ((nm * c1) / (jnp.sqrt(nv * c2) + ADAM_EPS) + ADAM_WD * w), nm, nv


def _small_update(slots, params, ms, vs):
    n = len(params)

    def body(slots_ref, *refs):
        w_refs, m_refs, v_refs, loss_ref = refs[:n], refs[n:2 * n], refs[2 * n:3 * n], refs[3 * n]
        g_refs, d_refs, nm_refs, nv_refs = (refs[3 * n + 1 + k * n:3 * n + 1 + (k + 1) * n] for k in range(4))
        acc = slots_ref[0]
        for s in range(1, slots.shape[0]):
            acc = acc + slots_ref[s]
        loss_ref[...] = acc[4:5, 0:1]
        grads = (acc[0:1] + acc[5:6], acc[3:4], acc[2:3], acc[1:2])
        for k in range(n):
            g = grads[k][:, :w_refs[k].shape[1]]
            g_refs[k][...] = g
            d_refs[k][...], nm_refs[k][...], nv_refs[k][...] = _adamw_math(w_refs[k][...], g, m_refs[k][...],
                                                                           v_refs[k][...])

    return pl.pallas_call(
        body, name="small_update",
        out_shape=[jax.ShapeDtypeStruct((1, 1), F32)] + [jax.ShapeDtypeStruct(p.shape, F32) for p in params] * 4,
    )(slots, *params, *ms, *vs)


def _local_step(x, mem, target, pre_norm, sink_a, mem_norm, post_norm, w_in_g, w_out, w_mkv, gathers=None,
                own=None):
    first_gather, late_gather = gathers if gathers else (None, None)
    u, ut, p_own, hosted = _pre_norm(x, pre_norm, own[1] if own else None, first_gather)
    if gathers:
        w_in_g = hosted[0].reshape(N_CHIPS, D_MODEL, SHARD_IN)
    pr = _pre_proj(u, w_in_g, (own[0], p_own) if own else None, late_gather)
    pr["ut"] = ut
    if gathers:
        w_out, w_mkv = (g.reshape(D_MODEL, g.shape[-1]) for g in pr["hosted"])
    mk, mv = _mem_kv(mem, mem_norm, w_mkv)
    sink = sink_a.reshape(-1)
    qa, ka, va = pr["qa"][None], pr["ka"][None], pr["va"][None]
    oa, lse_a = _band_fwd(qa, ka, va, sink, max_dist=A_WINDOW - 1, name="swa_fwd")
    ob_list, lseb_list = [], []
    for k, (win, dil) in enumerate(B_CONFIGS):
        o_i, l_i = _band_fwd(pr["qb"][k], pr["kb"][k], pr["vb"][k], None, max_dist=win // dil, name=f"dil{dil}_fwd")
        ob_list.append(o_i)
        lseb_list.append(l_i)
    oc, lse_c = _mem_attn_fwd(pr["qc"], mk, mv)
    sink_row = jnp.pad(sink, (0, LANES - sink.shape[0])).reshape(1, LANES)
    po = _post(x, target, post_norm, w_out, sink_row, oa[0], lse_a[0], pr["ga"], ob_list, lseb_list, pr["gb"], oc,
               pr["gc"])
    dqc, dmk, dmv = _mem_attn_bwd(pr["qc"], mk, mv, po["doc"], lse_c, po["dl_c"])
    dqa, dka, dva = _band_bwd(qa, ka, va, po["doa"][None], lse_a, po["dl_a"][None], max_dist=A_WINDOW - 1,
                              name="swa_bwd")
    res = dict(qb=[], kb=[], vb=[])
    for k, (win, dil) in enumerate(B_CONFIGS):
        dq_i, dk_i, dv_i = _band_bwd(pr["qb"][k], pr["kb"][k], pr["vb"][k], po["dob"][k], po["lse_b"][k],
                                     po["dl_b"][k], max_dist=win // dil, name=f"dil{dil}_bwd")
        res["qb"].append(dq_i)
        res["kb"].append(dk_i)
        res["vb"].append(dv_i)
    nat = dict(qa=dqa[0], ka=dka[0], va=dva[0], ga=po["dga"], gb=po["dgb"], qc=dqc, gc=po["dgc"])
    dproj, gw_in = _grad_w_in(pr["ut"], nat, res)
    gw_mkv, gmem = _mem_kv_bwd(mem, mem_norm, w_mkv, dmk, dmv)
    gsink = -po["gsink"][0, :sink.shape[0]]
    return dict(loss=po["loss"], g=po["g"], dproj=dproj, gw_in=gw_in, gw_out=po["gw_out"], gw_mkv=gw_mkv,
                gpost=po["gpost"], gmem=gmem, gsink=gsink, w_in_g=w_in_g)


def kernel(x, mem, pre_norm, w_in, sink_a, mem_norm, w_mem_kv, w_out, post_norm, loss_target, m_pre_norm, m_w_in, m_sink_a, m_mem_norm, m_w_mem_kv, m_w_out, m_post_norm, v_pre_norm, v_w_in, v_sink_a, v_mem_norm, v_w_mem_kv, v_w_out, v_post_norm):
    w_own = w_in[0]
    gathers = (_gather_exchange([w_own], prefilled=False),
               _gather_exchange([w_out[0].astype(BF16), w_mem_kv[0].astype(BF16)]))
    chip = (2 * lax.axis_index("x") + lax.axis_index("y")).astype(jnp.int32).reshape(1)
    loc = _local_step(x[0], mem[0], loss_target[0], pre_norm, sink_a, mem_norm, post_norm, None, None, None, gathers,
                      (chip, w_own))
    big = [loc["gw_in"], loc["gw_out"].reshape(N_CHIPS, D_MODEL // N_CHIPS, D_MODEL),
           loc["gw_mkv"].reshape(N_CHIPS, D_MODEL // N_CHIPS, 2 * C_W)]
    core = lax.axis_index("c").astype(jnp.int32).reshape(1)
    w_in_full = loc["w_in_g"]
    step_in = (x[0], loc["g"], pre_norm, w_in_full, loc["dproj"])
    pair_ex = _pair_exchange(big)
    pair_handles, token = _exchange_start(pair_ex, "pair_exchange_start")
    gx_a, gpre_a = _input_grad(*step_in, None, (0, 2), token, "input_grad_a")
    big, got = _exchange_wait(pair_ex, pair_handles, gpre_a, "pair_exchange_wait")
    parts = _pair_add(core, big, got)
    chip_ex = _chip_exchange(parts)
    chip_handles, token = _exchange_start(chip_ex, "chip_exchange_start")
    grad_x, gpre_b = _input_grad(*step_in, gx_a, (2, 14), token, "input_grad_b")
    _, slots = _exchange_wait(chip_ex, chip_handles, gpre_b, "chip_exchange_wait")
    halves = _chip_sum(core, slots)
    widen = lambda a: jnp.pad(a.reshape(1, -1), ((0, 0), (0, D_MODEL - a.size)))
    small = jnp.concatenate([gpre_a, loc["gpost"], loc["gmem"], widen(loc["gsink"]), widen(loc["loss"]), gpre_b,
                             jnp.zeros((2, D_MODEL), F32)], axis=0)
    (g_in, g_out, g_mkv), small_slots = _pair_gather(halves, small)
    (loss, g_pre, g_sink, g_mem, g_post, d_pre, d_sink, d_mem, d_post, nm_pre, nm_sink, nm_mem, nm_post,
     nv_pre, nv_sink, nv_mem, nv_post) = _small_update(
        small_slots, (pre_norm, sink_a, mem_norm, post_norm), (m_pre_norm, m_sink_a, m_mem_norm, m_post_norm),
        (v_pre_norm, v_sink_a, v_mem_norm, v_post_norm))

    (g_in, d_in, nm_in, nv_in), (g_out, d_out, nm_out, nv_out), (g_mkv, d_mkv, nm_mkv, nv_mkv) = _adamw(
        (w_in[0], w_out[0], w_mem_kv[0]), (g_in, g_out, g_mkv), (m_w_in[0], m_w_out[0], m_w_mem_kv[0]),
        (v_w_in[0], v_w_out[0], v_w_mem_kv[0]))
    lead = lambda a: a[None]
    return (loss.reshape(()), lead(grad_x),
            g_pre, lead(g_in), g_sink, g_mem, lead(g_mkv), lead(g_out), g_post,
            d_pre, lead(d_in), d_sink, d_mem, lead(d_mkv), lead(d_out), d_post,
            nm_pre, lead(nm_in), nm_sink, nm_mem, lead(nm_mkv), lead(nm_out), nm_post,
            nv_pre, lead(nv_in), nv_sink, nv_mem, lead(nv_mkv), lead(nv_out), nv_post)
```

```python
import numpy as np
import jax
import jax.numpy as jnp
from jax import lax
from jax.experimental import pallas as pl
from jax.experimental.pallas import tpu as pltpu

F32 = jnp.float32
BF16 = jnp.bfloat16

D_MODEL = 1024
HEAD_DIM = 64
LANES = 128
BLOCK = 128
ROW_TILE = 512
ATTN_TILE = 1024
INPUT_GRAD_TILES = 16
TAIL_STEPS = 4
A_W, A_KV_W, B_W, C_W = 384, 128, 384, 256
N_MEM = 256
D_IN = 3072
N_CHIPS = 4
SHARD_IN = D_IN // N_CHIPS
B_CONFIGS = ((128, 1), (512, 4), (2048, 16))
B_DILS = tuple(d for _, d in B_CONFIGS)
A_WINDOW = 128
RMS_EPS = 1e-6
ROPE_THETA = 500000.0
SCALE = HEAD_DIM ** -0.5
NEG = -1e30
ADAM_LR, ADAM_B1, ADAM_B2, ADAM_EPS, ADAM_WD, ADAM_STEP = 0.001, 0.9, 0.999, 1e-08, 0.01, 10

NT = (((1,), (1,)), ((), ()))
TN = (((0,), (0,)), ((), ()))
MESH = pl.DeviceIdType.MESH

_PROJ_LAYOUT = (
    [("qa", 128 * i, True, True) for i in range(3)] + [("ka", 0, True, False), ("va", 0, False, False)]
    + [("ga", 128 * i, False, False) for i in range(3)]
    + [("qb", 128 * i, True, True) for i in range(3)] + [("kb", 128 * i, True, False) for i in range(3)]
    + [("vb", 128 * i, False, False) for i in range(3)] + [("gb", 128 * i, False, False) for i in range(3)]
    + [("qc", 128 * i, False, True) for i in range(2)] + [("gc", 128 * i, False, False) for i in range(2)]
)
_PROJ_WIDTH = dict(qa=A_W, ka=A_KV_W, va=A_KV_W, ga=A_W, qb=B_W, kb=B_W, vb=B_W, gb=B_W, qc=C_W, gc=C_W)
_NATURAL = ("qa", "ka", "va", "ga", "gb", "qc", "gc")
_DILATED = ("qb", "kb", "vb")


def _dot(a, b):
    return jnp.dot(a, b, preferred_element_type=F32)


def _dot_nt(a, b):
    return lax.dot_general(a, b, NT, preferred_element_type=F32)


def _dot_tn(a, b):
    return lax.dot_general(a, b, TN, preferred_element_type=F32)


def _half_masks(rows):
    lane = lax.broadcasted_iota(jnp.int32, (rows, LANES), 1)
    return lane < HEAD_DIM, lane >= HEAD_DIM


def _rope(t, c, sm, sp):
    return t * c + pltpu.roll(t, LANES - 8, 1) * sm + pltpu.roll(t, 8, 1) * sp


def _rope_tables(seq, tm):
    dim = np.arange(LANES) % HEAD_DIM
    inv_freq = (np.float32(ROPE_THETA) ** (-np.arange(0, 16, 2, dtype=np.float32) / np.float32(16))).astype(np.float64)
    freq = np.where(dim < 16, inv_freq[dim % 8], 0.0)[None, :]
    local = np.arange(tm, dtype=np.float64)[:, None] * freq
    base = (np.arange(seq // tm, dtype=np.float64) * tm)[:, None] * freq
    both = lambda a: np.concatenate([np.cos(a), np.sin(a)], axis=1).astype(np.float32)
    return jnp.asarray(both(local)), jnp.asarray(np.repeat(both(base), 8, axis=0))


def _rope_coeffs(local_ref, base_ref):
    cl, sl = local_ref[:, :LANES], local_ref[:, LANES:]
    cb, sb = base_ref[0:1, :LANES], base_ref[0:1, LANES:]
    cos = cb * cl - sb * sl
    sin = sb * cl + cb * sl
    dim = lax.broadcasted_iota(jnp.int32, (1, LANES), 1) % HEAD_DIM
    return cos, jnp.where(dim < 8, -sin, 0.0), jnp.where((dim >= 8) & (dim < 16), sin, 0.0)


def _split3(x):
    a = x.astype(BF16)
    r = x - a.astype(F32)
    b = r.astype(BF16)
    c = (r - b.astype(F32)).astype(BF16)
    return a, b, c


def _rows_to_lanes(x):
    row = lax.broadcasted_iota(jnp.int32, (8, LANES), 0)
    lane = lax.broadcasted_iota(jnp.int32, (8, LANES), 1)
    eye = (row == lane).astype(BF16)
    a, b, c = _split3(x)
    return _dot_nt(eye, a) + _dot_nt(eye, b) + _dot_nt(eye, c)


def _head_sum_matrix(width):
    k = lax.broadcasted_iota(jnp.int32, (width, LANES), 0)
    h = lax.broadcasted_iota(jnp.int32, (width, LANES), 1)
    return (k // HEAD_DIM == h).astype(BF16)


def _head_expand_matrix(width):
    h = lax.broadcasted_iota(jnp.int32, (LANES, width), 0)
    k = lax.broadcasted_iota(jnp.int32, (LANES, width), 1)
    return (k // HEAD_DIM == h).astype(BF16)


def _dot_split(x, mat, terms):
    parts = _split3(x)[:terms]
    out = _dot(parts[0], mat)
    for p in parts[1:]:
        out = out + _dot(p, mat)
    return out


def _per_head(cols, fill=0.0):
    rows = cols[0].shape[0]
    lane = lax.broadcasted_iota(jnp.int32, (rows, LANES), 1)
    out = jnp.full((rows, LANES), fill, F32)
    for h, col in enumerate(cols):
        out = jnp.where(lane == h, col, out)
    return out


def _lane_blocks(width):
    return [slice(p * LANES, (p + 1) * LANES) for p in range(width // LANES)]


def _stage(rows, width):
    return pltpu.VMEM((width // LANES, rows, LANES), F32)


def _stage_write(buf, value):
    for p, lanes in enumerate(_lane_blocks(value.shape[1])):
        buf[p] = value[:, lanes]


def _stage_read(buf):
    return jnp.concatenate([buf[p] for p in range(buf.shape[0])], axis=1) if buf.shape[0] > 1 else buf[0]


def _to_residues(buf, out_ref, dil):
    rows = buf.shape[1] // dil
    for r in range(dil):
        for p in range(buf.shape[0]):
            plane = buf.at[p]
            out_ref[r, :, p * LANES:(p + 1) * LANES] = plane[pl.ds(r, rows, stride=dil), :].astype(out_ref.dtype)


def _from_residues(in_ref, buf, dil):
    rows = buf.shape[1] // dil
    for r in range(dil):
        for p in range(buf.shape[0]):
            plane = buf.at[p]
            plane[pl.ds(r, rows, stride=dil), :] = in_ref[r, :, p * LANES:(p + 1) * LANES].astype(F32)


def _residue_spec(dil, tm, width):
    return pl.BlockSpec((dil, tm // dil, width), lambda i: (0, i, 0))


def _gather_exchange(shards_2d, prefilled=True):
    shards = tuple(jax.ShapeDtypeStruct((2, s.shape[0] // 2, s.shape[1]), BF16) for s in shards_2d)
    n = len(shards)

    def copies(in_refs, out_refs, send_sems, recv_sems, *local_sems):
        srcs, outs = in_refs[:n], out_refs
        x, y, c = lax.axis_index("x"), lax.axis_index("y"), lax.axis_index("c")
        my_chip = 2 * x + y
        sibling = (x, y, 1 - c)
        chips = [(1 - x, y), (x, 1 - y), (1 - x, 1 - y)]

        def copy(k, src, dst, to):
            return pltpu.make_async_remote_copy(src_ref=src, dst_ref=dst, send_sem=send_sems.at[k],
                                                recv_sem=recv_sems.at[k], device_id=to, device_id_type=MESH)

        first, arrive, passed, sibling_arrive = [], [], [], []
        for j, (cx, cy) in enumerate(chips):
            chip = 2 * cx + cy
            for t in range(n):
                k = n * j + t
                first.append(copy(k, srcs[t].at[c], outs[t].at[my_chip, c], (cx, cy, c)))
                arrive.append(copy(k, srcs[t].at[c], outs[t].at[chip, c], (cx, cy, c)))
                passed.append(copy(n * 3 + k, outs[t].at[chip, c], outs[t].at[chip, c], sibling))
                sibling_arrive.append(copy(n * 3 + k, outs[t].at[chip, 1 - c], outs[t].at[chip, 1 - c], sibling))
        own = [pltpu.make_async_copy(srcs[t], outs[t].at[my_chip], local_sems[0].at[t]) for t in range(n)
               ] if local_sems else []
        return first, arrive, passed, sibling_arrive, own

    def start(*refs):
        first, _, _, _, own = copies(*refs)
        for cp in first + own:
            cp.start()

    def forward(refs, senders):
        _, arrive, passed, _, _ = copies(*refs)
        for j in senders:
            for k in range(n * j, n * (j + 1)):
                arrive[k].wait_recv()
                passed[k].start()

    def mid(*refs):
        forward(refs, (0, 1))

    def finish(*refs):
        forward(refs, (2,))
        first, _, passed, sibling_arrive, own = copies(*refs)
        for cp in sibling_arrive:
            cp.wait_recv()
        for cp in first + passed:
            cp.wait_send()
        for cp in own:
            cp.wait()

    ex = dict(ins=[], start=start, mid=mid, finish=finish, prefilled=prefilled,
              outs=[jax.ShapeDtypeStruct((N_CHIPS,) + s.shape, s.dtype) for s in shards],
              sems=[pltpu.SemaphoreType.DMA((6 * n,)), pltpu.SemaphoreType.DMA((6 * n,))])
    if prefilled:
        my_chip = 2 * lax.axis_index("x") + lax.axis_index("y")
        halves = [a.reshape(s.shape) for a, s in zip(shards_2d, shards)]
        landing = [lax.dynamic_update_slice(jnp.zeros((N_CHIPS,) + s.shape, s.dtype), a[None], (my_chip, 0, 0, 0))
                   for a, s in zip(halves, shards)]
        ex.update(ins=halves + landing, aliases={n + t: t for t in range(n)})
    else:
        ex["sems"].append(pltpu.SemaphoreType.DMA((n,)))
    return ex


def _mem_kv(mem, mem_norm, w_mkv):
    def body(mem_ref, g_ref, w_ref, mk_ref, mv_ref):
        m = mem_ref[...]
        r = lax.rsqrt(jnp.mean(m * m, axis=-1, keepdims=True) + RMS_EPS)
        mn = (m * r * g_ref[...]).astype(BF16)
        kv = _dot(mn, w_ref[...])
        mk_ref[...] = kv[:, :C_W].astype(BF16)
        mv_ref[...] = kv[:, C_W:].astype(BF16)

    return pl.pallas_call(
        body, name="mem_kv",
        out_shape=[jax.ShapeDtypeStruct((N_MEM, C_W), BF16)] * 2,
    )(mem, mem_norm, w_mkv)


def _mem_kv_bwd(mem, mem_norm, w_mkv, dmk, dmv):
    def body(mem_ref, g_ref, w_ref, dmk_ref, dmv_ref, gw_ref, gn_ref):
        m = mem_ref[...]
        r = lax.rsqrt(jnp.mean(m * m, axis=-1, keepdims=True) + RMS_EPS)
        mhat = m * r
        mn = (mhat * g_ref[...]).astype(BF16)
        dkv = jnp.concatenate([dmk_ref[...], dmv_ref[...]], axis=1).astype(BF16)
        gw_ref[...] = _dot_tn(mn, dkv)
        dmn = _dot_nt(dkv, w_ref[...])
        gn_ref[...] = jnp.sum(dmn * mhat, axis=0, keepdims=True)

    return pl.pallas_call(
        body, name="mem_kv_bwd",
        out_shape=[jax.ShapeDtypeStruct((D_MODEL, 2 * C_W), F32), jax.ShapeDtypeStruct((1, D_MODEL), F32)],
    )(mem, mem_norm, w_mkv, dmk, dmv)


def _host_phases(host, in_refs, out_refs, sems, steps, before):
    if not host:
        return
    step = pl.program_id(0)
    phases = [("start", 0)] if before else [("mid", max(steps - 3, 0)), ("finish", steps - 1)]
    for phase, at in phases:
        pl.when(step == at)(lambda phase=phase: host[phase](in_refs, out_refs, *sems))


def _pre_norm(x, pre_norm, w_own=None, host=None):
    seq = x.shape[0]
    tm = min(ROW_TILE, seq)
    n_own_in = 2 if w_own is None else 3
    n_own_out = n_own_in
    n_host_in = len(host["ins"]) if host else 0
    n_host_out = len(host["outs"]) if host else 0
    half = D_MODEL // 2

    def body(x_ref, g_ref, *refs):
        w_ref = None if w_own is None else refs[0]
        refs = refs[n_own_in - 2:]
        host_in, own_out, refs = refs[:n_host_in], refs[n_host_in:n_host_in + n_own_out], refs[n_host_in + n_own_out:]
        host_out, refs = refs[:n_host_out], refs[n_host_out:]
        if w_own is not None:
            wb, sems = refs[0], refs[1:]

            @pl.when(pl.program_id(0) == 0)
            def _():
                for h in range(2):
                    wb[h] = w_ref[h * half:(h + 1) * half, :].astype(BF16)
            if host and not host["prefilled"]:
                host_in = [wb]
        else:
            sems = refs
        _host_phases(host, host_in, host_out, sems, seq // tm, before=True)

        xv = x_ref[...]
        r = lax.rsqrt(jnp.mean(xv * xv, axis=-1, keepdims=True) + RMS_EPS)
        u = xv * r * g_ref[...]
        ub = u.astype(BF16)
        own_out[0][...] = ub
        own_out[1][...] = u.T.astype(BF16)
        if w_own is not None:
            own_out[2][...] = _dot(ub[:, :half], wb[0]) + _dot(ub[:, half:], wb[1])
        _host_phases(host, host_in, host_out, sems, seq // tm, before=False)

    any_spec = pl.BlockSpec(memory_space=pl.ANY)
    ins = [x, pre_norm]
    in_specs = [pl.BlockSpec((tm, D_MODEL), lambda i: (i, 0)), pl.BlockSpec(pre_norm.shape, lambda i: (0, 0))]
    out_shape = [jax.ShapeDtypeStruct((seq, D_MODEL), BF16), jax.ShapeDtypeStruct((seq // tm, D_MODEL, tm), BF16)]
    out_specs = [pl.BlockSpec((tm, D_MODEL), lambda i: (i, 0)), pl.BlockSpec((None, D_MODEL, tm), lambda i: (i, 0, 0))]
    aliases, scratch = {}, []
    if w_own is not None:
        ins.append(w_own)
        in_specs.append(pl.BlockSpec(w_own.shape, lambda i: (0, 0)))
        out_shape.append(jax.ShapeDtypeStruct((seq, w_own.shape[1]), F32))
        out_specs.append(pl.BlockSpec((tm, w_own.shape[1]), lambda i: (i, 0)))
        scratch.append(pltpu.VMEM((2, half, w_own.shape[1]), BF16))
    if host:
        aliases = {len(ins) + k: n_own_out + v for k, v in host.get("aliases", {}).items()}
        ins += list(host["ins"])
        in_specs += [any_spec] * n_host_in
        out_shape += list(host["outs"])
        out_specs += [any_spec] * n_host_out
        scratch += list(host["sems"])
    res = pl.pallas_call(
        body, name="pre_norm", grid=(seq // tm,), in_specs=in_specs, out_specs=out_specs, out_shape=out_shape,
        input_output_aliases=aliases, scratch_shapes=scratch,
        compiler_params=pltpu.CompilerParams(dimension_semantics=("arbitrary",)),
    )(*ins)
    return res[0], res[1], (None if w_own is None else res[2]), res[n_own_out:]


def _pre_proj(u, w_in_g, own=None, host=None):
    seq = u.shape[0]
    tm = min(ROW_TILE, seq)
    n_nat, n_dil = len(_NATURAL), len(_DILATED) * len(B_DILS)
    rope = _rope_tables(seq, tm)

    n_host_in = len(host["ins"]) if host else 0
    n_host_out = len(host["outs"]) if host else 0
    n_own_out = n_nat + n_dil

    def body(u_ref, w_ref, rl_ref, rb_ref, *refs):
        if own:
            (chip_ref, pown_ref), refs = refs[:2], refs[2:]
        host_in, refs = refs[:n_host_in], refs[n_host_in:]
        nat = dict(zip(_NATURAL, refs[:n_nat]))
        res = {n: refs[n_nat + len(B_DILS) * k:n_nat + len(B_DILS) * (k + 1)] for k, n in enumerate(_DILATED)}
        host_out = refs[n_own_out:n_own_out + n_host_out]
        bufs = dict(zip(_DILATED, refs[n_own_out + n_host_out:]))
        sems = refs[n_own_out + n_host_out + len(_DILATED):]
        _host_phases(host, host_in, host_out, sems, seq // tm, before=True)

        def project(own_chip):
            ub = u_ref[...]
            c, sm, sp = _rope_coeffs(rl_ref, rb_ref)
            for j in range(N_CHIPS):
                pj = pown_ref[...] if j == own_chip else _dot(ub, w_ref[j])
                for b in range(SHARD_IN // LANES):
                    name, off, roped, scaled = _PROJ_LAYOUT[(SHARD_IN // LANES) * j + b]
                    piece = pj[:, LANES * b:LANES * (b + 1)]
                    if roped:
                        piece = _rope(piece, c, sm, sp)
                    if scaled:
                        piece = piece * SCALE
                    if name in bufs:
                        bufs[name][off // LANES] = piece
                    else:
                        nat[name][:, off:off + LANES] = piece.astype(BF16)
            for name in _DILATED:
                for ref, dil in zip(res[name], B_DILS):
                    _to_residues(bufs[name], ref, dil)

        if own:
            for chip in range(N_CHIPS):
                pl.when(chip_ref[0] == chip)(lambda chip=chip: project(chip))
        else:
            project(None)
        _host_phases(host, host_in, host_out, sems, seq // tm, before=False)

    row = lambda w: pl.BlockSpec((tm, w), lambda i: (i, 0))
    full = lambda a: pl.BlockSpec(a.shape, lambda i: (0,) * a.ndim)
    any_spec = pl.BlockSpec(memory_space=pl.ANY)
    out_shape = [jax.ShapeDtypeStruct((seq, _PROJ_WIDTH[n]), BF16) for n in _NATURAL]
    out_specs = [row(_PROJ_WIDTH[n]) for n in _NATURAL]
    for n in _DILATED:
        for dil in B_DILS:
            out_shape.append(jax.ShapeDtypeStruct((dil, seq // dil, B_W), BF16))
            out_specs.append(_residue_spec(dil, tm, B_W))
    ins = [u, w_in_g, *rope]
    in_specs = [row(D_MODEL), full(w_in_g), full(rope[0]), pl.BlockSpec((8, 2 * LANES), lambda i: (i, 0))]
    if own:
        ins += list(own)
        in_specs += [pl.BlockSpec(memory_space=pltpu.SMEM), row(SHARD_IN)]
    scratch = [_stage(tm, B_W)] * len(_DILATED)
    aliases = {}
    if host:
        aliases = {len(ins) + k: n_own_out + v for k, v in host.get("aliases", {}).items()}
        ins += list(host["ins"])
        in_specs += [any_spec] * n_host_in
        out_shape += list(host["outs"])
        out_specs += [any_spec] * n_host_out
        scratch += list(host["sems"])
    res = pl.pallas_call(
        body, name="pre_proj", grid=(seq // tm,), in_specs=in_specs, out_specs=out_specs, out_shape=out_shape,
        input_output_aliases=aliases, scratch_shapes=scratch,
        compiler_params=pltpu.CompilerParams(dimension_semantics=("arbitrary",)),
    )(*ins)
    out = dict(zip(_NATURAL, res[:n_nat]))
    for k, n in enumerate(_DILATED):
        out[n] = res[n_nat + len(B_DILS) * k:n_nat + len(B_DILS) * (k + 1)]
    out["hosted"] = res[n_own_out:]
    return out


def _band_bias(max_dist, transposed):
    i = np.arange(BLOCK)[:, None]
    j = np.arange(BLOCK)[None, :]
    if transposed:
        same = i <= j
        other = (j + BLOCK - i) <= max_dist
        vis = np.concatenate([same, other], axis=1)
    else:
        prev = (i + BLOCK - j) <= max_dist
        same = j <= i
        vis = np.concatenate([prev, same], axis=1)
    return jnp.asarray(np.where(vis, 0.0, NEG).astype(np.float32))


def _kv_place(h, gqa):
    return (0, h // 3) if gqa else (h // 2, h % 2)


def _band_fwd(q, k, v, sink, *, max_dist, name):
    dil, length, wq = q.shape
    wk = k.shape[2]
    gqa = wk != wq
    tq = min(ATTN_TILE, length)
    ns, nt = tq // BLOCK, length // tq
    npair = wq // LANES
    bias = _band_bias(max_dist, transposed=False)
    has_sink = sink is not None

    def body(*refs):
        if has_sink:
            sink_ref, refs = refs[0], refs[1:]
        q_ref, k_ref, kp_ref, v_ref, vp_ref, bias_ref, o_ref, lse_ref, kbuf, vbuf = refs[:10]
        i = pl.program_id(1)
        kbuf[0:BLOCK] = kp_ref[...]
        kbuf[BLOCK:] = k_ref[...]
        vbuf[0:BLOCK] = vp_ref[...]
        vbuf[BLOCK:] = v_ref[...]
        if gqa:
            kroll, vroll = refs[10:12]
            kroll[...] = pltpu.roll(kbuf[...], HEAD_DIM, 1)
            vroll[...] = pltpu.roll(vbuf[...], HEAD_DIM, 1)
        half = _half_masks(BLOCK)
        col_prev = (lax.broadcasted_iota(jnp.int32, (1, 2 * BLOCK), 1) < BLOCK).astype(F32)

        def score_matmuls(a):
            scores = []
            for p in range(npair):
                qp = q_ref[a * BLOCK:(a + 1) * BLOCK, p * LANES:(p + 1) * LANES]
                for e in range(2):
                    pk, ek = _kv_place(2 * p + e, gqa)
                    kw = (kbuf if ek == e else kroll)[a * BLOCK:(a + 2) * BLOCK, pk * LANES:(pk + 1) * LANES]
                    scores.append(_dot_nt(jnp.where(half[e], qp, jnp.zeros_like(qp)), kw))
            return scores

        pending = score_matmuls(0)
        for a in range(ns):
            r0 = a * BLOCK
            b = bias_ref[...]
            if a == 0:
                b = b + jnp.where(i == 0, NEG, 0.0) * col_prev
            scores = pending
            m_cols, l_cols, probs = [], [], []
            for h, s in enumerate(scores):
                s = s + b
                m = jnp.max(s, axis=1, keepdims=True)
                if has_sink:
                    m = jnp.maximum(m, sink_ref[h])
                pe = jnp.exp(s - m)
                l = jnp.sum(pe, axis=1, keepdims=True)
                if has_sink:
                    l = l + jnp.exp(sink_ref[h] - m)
                probs.append(pe.astype(BF16))
                m_cols.append(m)
                l_cols.append(l)
            pending = score_matmuls(a + 1) if a + 1 < ns else None
            for p in range(npair):
                o_h = []
                for e in range(2):
                    h = 2 * p + e
                    pk, ek = _kv_place(h, gqa)
                    vw = (vbuf if ek == e else vroll)[r0:r0 + 2 * BLOCK, pk * LANES:(pk + 1) * LANES]
                    o_h.append(_dot(probs[h], vw) * (1.0 / l_cols[h]))
                o_ref[r0:r0 + BLOCK, p * LANES:(p + 1) * LANES] = jnp.where(half[0], o_h[0], o_h[1]).astype(BF16)
            lse_ref[r0:r0 + BLOCK, :] = _per_head(m_cols) + jnp.log(_per_head(l_cols, 1.0))

    main = lambda w: pl.BlockSpec((None, tq, w), lambda r, i: (r, i, 0))
    prev = lambda w: pl.BlockSpec((None, BLOCK, w), lambda r, i: (r, jnp.maximum(i * ns - 1, 0), 0))
    in_specs = [main(wq), main(wk), prev(wk), main(wk), prev(wk), pl.BlockSpec(bias.shape, lambda r, i: (0, 0))]
    args = [q, k, k, v, v, bias]
    if has_sink:
        in_specs = [pl.BlockSpec(memory_space=pltpu.SMEM)] + in_specs
        args = [sink] + args
    scratch = [pltpu.VMEM((tq + BLOCK, wk), BF16)] * (4 if gqa else 2)
    return pl.pallas_call(
        body, name=name, grid=(dil, nt), in_specs=in_specs,
        out_specs=[main(wq), main(LANES)],
        out_shape=[jax.ShapeDtypeStruct((dil, length, wq), BF16), jax.ShapeDtypeStruct((dil, length, LANES), F32)],
        scratch_shapes=scratch,
    )(*args)


def _band_bwd(q, k, v, do, lse, delta, *, max_dist, name):
    dil, length, wq = q.shape
    wk = k.shape[2]
    gqa = wk != wq
    tq = min(ATTN_TILE, length)
    ns, nt = tq // BLOCK, length // tq
    npair = wq // LANES
    nblocks = length // BLOCK
    bias = _band_bias(max_dist, transposed=True)

    def body(q_ref, qn_ref, do_ref, don_ref, lse_ref, lsen_ref, dl_ref, dln_ref, k_ref, v_ref, bias_ref,
             dq_ref, dk_ref, dv_ref, stat_l, stat_d, dqt, kt, *rolled):
        i = pl.program_id(1)
        for pk in range(wk // LANES):
            kt[pk] = k_ref[:, pk * LANES:(pk + 1) * LANES].astype(F32).T.astype(BF16)
        if gqa:
            kroll, vroll, ktroll = rolled
            kroll[...] = pltpu.roll(k_ref[...], HEAD_DIM, 1)
            vroll[...] = pltpu.roll(v_ref[...], HEAD_DIM, 1)
            ktroll[0] = kroll[...].astype(F32).T.astype(BF16)
        for a in range(ns):
            rows = slice(a * BLOCK, (a + 1) * BLOCK)
            stat_l[a] = _rows_to_lanes(lse_ref[rows, :])
            stat_d[a] = _rows_to_lanes(dl_ref[rows, :])
        stat_l[ns] = _rows_to_lanes(lsen_ref[...])
        stat_d[ns] = _rows_to_lanes(dln_ref[...])

        @pl.when(i == 0)
        def _():
            dqt[:, :, 0:BLOCK] = jnp.zeros((npair, LANES, BLOCK), F32)

        @pl.when(i > 0)
        def _():
            dqt[:, :, 0:BLOCK] = dqt[:, :, tq:tq + BLOCK]

        dqt[:, :, BLOCK:] = jnp.zeros((npair, LANES, tq), F32)
        half2 = _half_masks(2 * BLOCK)
        row = lax.broadcasted_iota(jnp.int32, (LANES, BLOCK), 0)
        row_half = (row < HEAD_DIM, row >= HEAD_DIM)
        col_next = (lax.broadcasted_iota(jnp.int32, (1, 2 * BLOCK), 1) >= BLOCK).astype(F32)

        def scores(b):
            rows = slice(b * BLOCK, (b + 1) * BLOCK)
            nxt_rows = slice((b + 1) * BLOCK, (b + 2) * BLOCK)
            items = []
            for p in range(npair):
                lanes = slice(p * LANES, (p + 1) * LANES)
                q_next = q_ref[nxt_rows, lanes] if b + 1 < ns else qn_ref[:, lanes]
                do_next = do_ref[nxt_rows, lanes] if b + 1 < ns else don_ref[:, lanes]
                qw = jnp.concatenate([q_ref[rows, lanes], q_next], axis=0)
                dow = jnp.concatenate([do_ref[rows, lanes], do_next], axis=0)
                for e in range(2):
                    h = 2 * p + e
                    pk, ek = _kv_place(h, gqa)
                    klanes = slice(pk * LANES, (pk + 1) * LANES)
                    kb = (k_ref if ek == e else kroll)[rows, klanes]
                    vb = (v_ref if ek == e else vroll)[rows, klanes]
                    qm = jnp.where(half2[e], qw, jnp.zeros_like(qw))
                    dom = jnp.where(half2[e], dow, jnp.zeros_like(dow))
                    items.append(dict(p=p, e=e, h=h, pk=pk, ek=ek, qm=qm, dom=dom,
                                      st=_dot_nt(kb, qm), dpt=_dot_nt(vb, dom)))
            return items

        def probs(b, items):
            bt = bias_ref[...]
            if b == ns - 1:
                bt = bt + jnp.where(i == nt - 1, NEG, 0.0) * col_next
            for it in items:
                h = it["h"]
                lrow = jnp.concatenate([stat_l[b, h:h + 1, :], stat_l[b + 1, h:h + 1, :]], axis=1)
                drow = jnp.concatenate([stat_d[b, h:h + 1, :], stat_d[b + 1, h:h + 1, :]], axis=1)
                pt = jnp.exp(it["st"] + bt - lrow)
                it["ptb"] = pt.astype(BF16)
                it["dsb"] = (pt * (it["dpt"] - drow)).astype(BF16)

        pending = scores(0)
        for b in range(ns):
            rows = slice(b * BLOCK, (b + 1) * BLOCK)
            window = slice(b * BLOCK, (b + 2) * BLOCK)
            acc = {}
            items = pending
            probs(b, items)
            pending = scores(b + 1) if b + 1 < ns else None
            for p in range(npair):
                pair = items[2 * p:2 * p + 2]
                lanes = slice(p * LANES, (p + 1) * LANES)
                kparts = []
                for it in pair:
                    kbt = (kt if it["ek"] == it["e"] else ktroll)[it["pk"], :, rows]
                    kparts.append(jnp.where(row_half[it["e"]], kbt, jnp.zeros_like(kbt)))
                ds_keys = jnp.concatenate([it["dsb"] for it in pair], axis=0)
                dqt[p, :, window] += _dot(jnp.concatenate(kparts, axis=1), ds_keys)
                if not gqa:
                    q_both = jnp.concatenate([it["qm"] for it in pair], axis=0)
                    do_both = jnp.concatenate([it["dom"] for it in pair], axis=0)
                    dk_ref[rows, lanes] = _dot(jnp.concatenate([it["dsb"] for it in pair], axis=1), q_both).astype(BF16)
                    dv_ref[rows, lanes] = _dot(jnp.concatenate([it["ptb"] for it in pair], axis=1), do_both).astype(BF16)
                else:
                    for it in pair:
                        dv_c = _dot(it["ptb"], it["dom"])
                        dk_c = _dot(it["dsb"], it["qm"])
                        key = (it["pk"], it["ek"] == it["e"])
                        if key in acc:
                            acc[key] = (acc[key][0] + dk_c, acc[key][1] + dv_c)
                        else:
                            acc[key] = (dk_c, dv_c)
            if gqa:
                dk_al, dv_al = acc[(0, True)]
                dk_mis, dv_mis = acc[(0, False)]
                dk_ref[rows, :] = (dk_al + pltpu.roll(dk_mis, HEAD_DIM, 1)).astype(BF16)
                dv_ref[rows, :] = (dv_al + pltpu.roll(dv_mis, HEAD_DIM, 1)).astype(BF16)

        for p in range(npair):
            dq_ref[:, p * LANES:(p + 1) * LANES] = dqt[p, :, 0:tq].T.astype(BF16)

    main = lambda w: pl.BlockSpec((None, tq, w), lambda r, i: (r, i, 0))
    nxt = lambda w: pl.BlockSpec((None, BLOCK, w), lambda r, i: (r, jnp.minimum((i + 1) * ns, nblocks - 1), 0))
    scratch = [pltpu.VMEM((ns + 1, 8, LANES), F32), pltpu.VMEM((ns + 1, 8, LANES), F32),
               pltpu.VMEM((npair, LANES, tq + BLOCK), F32), pltpu.VMEM((wk // LANES, LANES, tq), BF16)]
    if gqa:
        scratch = scratch + [pltpu.VMEM((tq, wk), BF16)] * 2 + [pltpu.VMEM((1, LANES, tq), BF16)]
    return pl.pallas_call(
        body, name=name, grid=(dil, nt),
        in_specs=[main(wq), nxt(wq), main(wq), nxt(wq), main(LANES), nxt(LANES), main(LANES), nxt(LANES),
                  main(wk), main(wk), pl.BlockSpec(bias.shape, lambda r, i: (0, 0))],
        out_specs=[main(wq), main(wk), main(wk)],
        out_shape=[jax.ShapeDtypeStruct((dil, length, wq), BF16), jax.ShapeDtypeStruct((dil, length, wk), BF16),
                   jax.ShapeDtypeStruct((dil, length, wk), BF16)],
        scratch_shapes=scratch,
        compiler_params=pltpu.CompilerParams(dimension_semantics=("arbitrary", "arbitrary")),
    )(q, q, do, do, lse, lse, delta, delta, k, v, bias)


def _mem_attn_fwd(q, mk, mv):
    seq = q.shape[0]
    tq = min(ATTN_TILE, seq)
    sub_rows = min(4 * BLOCK, tq)
    ns = tq // sub_rows

    def body(q_ref, mk_ref, mv_ref, o_ref, lse_ref):
        half = _half_masks(sub_rows)

        def sub(a, carry):
            r0 = pl.multiple_of(a * sub_rows, sub_rows)
            scores = []
            for p in range(C_W // LANES):
                lanes = slice(p * LANES, (p + 1) * LANES)
                qp = q_ref[pl.ds(r0, sub_rows), lanes]
                for e in range(2):
                    scores.append(_dot_nt(jnp.where(half[e], qp, jnp.zeros_like(qp)), mk_ref[:, lanes]))
            m_cols, l_cols, probs = [], [], []
            for s in scores:
                m = jnp.max(s, axis=1, keepdims=True)
                pe = jnp.exp(s - m)
                probs.append(pe.astype(BF16))
                m_cols.append(m)
                l_cols.append(jnp.sum(pe, axis=1, keepdims=True))
            for p in range(C_W // LANES):
                lanes = slice(p * LANES, (p + 1) * LANES)
                o_h = [_dot(probs[2 * p + e], mv_ref[:, lanes]) * (1.0 / l_cols[2 * p + e]) for e in range(2)]
                o_ref[pl.ds(r0, sub_rows), lanes] = jnp.where(half[0], o_h[0], o_h[1]).astype(BF16)
            lse_ref[pl.ds(r0, sub_rows), :] = _per_head(m_cols) + jnp.log(_per_head(l_cols, 1.0))
            return carry

        lax.fori_loop(0, ns, sub, 0, unroll=True)

    row = lambda w: pl.BlockSpec((tq, w), lambda i: (i, 0))
    full = pl.BlockSpec((N_MEM, C_W), lambda i: (0, 0))
    return pl.pallas_call(
        body, name="mem_attn_fwd", grid=(seq // tq,), in_specs=[row(C_W), full, full],
        out_specs=[row(C_W), row(LANES)],
        out_shape=[jax.ShapeDtypeStruct((seq, C_W), BF16), jax.ShapeDtypeStruct((seq, LANES), F32)],
    )(q, mk, mv)


def _mem_attn_bwd(q, mk, mv, do, lse, delta):
    seq = q.shape[0]
    tq = min(ATTN_TILE, seq)
    ns = tq // BLOCK
    npair = C_W // LANES

    def body(q_ref, mk_ref, mv_ref, do_ref, lse_ref, dl_ref, dq_ref, dmk_ref, dmv_ref, stat_l, stat_d, mkt, dqt):
        @pl.when(pl.program_id(0) == 0)
        def _():
            dmk_ref[...] = jnp.zeros_like(dmk_ref)
            dmv_ref[...] = jnp.zeros_like(dmv_ref)
            for p in range(npair):
                mkt[p] = mk_ref[:, p * LANES:(p + 1) * LANES].astype(F32).T.astype(BF16)

        for a in range(ns):
            rows = slice(a * BLOCK, (a + 1) * BLOCK)
            stat_l[a] = _rows_to_lanes(lse_ref[rows, :])
            stat_d[a] = _rows_to_lanes(dl_ref[rows, :])
        span = min(2, ns)
        half = _half_masks(span * BLOCK)
        row = lax.broadcasted_iota(jnp.int32, (LANES, N_MEM), 0)
        row_half = (row < HEAD_DIM, row >= HEAD_DIM)

        for a in range(0, ns, span):
            rows = slice(a * BLOCK, (a + span) * BLOCK)
            items = []
            for p in range(npair):
                lanes = slice(p * LANES, (p + 1) * LANES)
                qp = q_ref[rows, lanes]
                dop = do_ref[rows, lanes]
                for e in range(2):
                    qm = jnp.where(half[e], qp, jnp.zeros_like(qp))
                    dom = jnp.where(half[e], dop, jnp.zeros_like(dop))
                    items.append(dict(p=p, e=e, qm=qm, dom=dom, st=_dot_nt(mk_ref[:, lanes], qm),
                                      dpt=_dot_nt(mv_ref[:, lanes], dom)))
            for it in items:
                h = 2 * it["p"] + it["e"]
                lrow = jnp.concatenate([stat_l[a + k, h:h + 1, :] for k in range(span)], axis=1)
                drow = jnp.concatenate([stat_d[a + k, h:h + 1, :] for k in range(span)], axis=1)
                pt = jnp.exp(it["st"] - lrow)
                it["ptb"] = pt.astype(BF16)
                it["dsb"] = (pt * (it["dpt"] - drow)).astype(BF16)
            for p in range(npair):
                lanes = slice(p * LANES, (p + 1) * LANES)
                pair = [it for it in items if it["p"] == p]
                join = lambda name, axis: jnp.concatenate([it[name] for it in pair], axis=axis)
                dmv_ref[:, lanes] += _dot(join("ptb", 1), join("dom", 0))
                dmk_ref[:, lanes] += _dot(join("dsb", 1), join("qm", 0))
                kbt = mkt[p]
                k_both = jnp.concatenate([jnp.where(row_half[e], kbt, jnp.zeros_like(kbt)) for e in range(2)], axis=1)
                dqt[p, :, rows] = _dot(k_both, join("dsb", 0))
        for p in range(npair):
            dq_ref[:, p * LANES:(p + 1) * LANES] = dqt[p].T.astype(BF16)

    row = lambda w: pl.BlockSpec((tq, w), lambda i: (i, 0))
    full = pl.BlockSpec((N_MEM, C_W), lambda i: (0, 0))
    return pl.pallas_call(
        body, name="mem_attn_bwd", grid=(seq // tq,),
        in_specs=[row(C_W), full, full, row(C_W), row(LANES), row(LANES)], out_specs=[row(C_W), full, full],
        out_shape=[jax.ShapeDtypeStruct((seq, C_W), BF16), jax.ShapeDtypeStruct((N_MEM, C_W), F32),
                   jax.ShapeDtypeStruct((N_MEM, C_W), F32)],
        scratch_shapes=[pltpu.VMEM((ns, 8, LANES), F32)] * 2
        + [pltpu.VMEM((npair, LANES, N_MEM), BF16), pltpu.VMEM((npair, LANES, tq), F32)],
        compiler_params=pltpu.CompilerParams(dimension_semantics=("arbitrary",)),
    )(q, mk, mv, do, lse, delta)


def _silu_and_grad(g):
    s = 1.0 / (1.0 + jnp.exp(-g))
    return g * s, s * (1.0 + g * (1.0 - s))


def _post(x, target, post_norm, w_out, sink_row, oa, lse_a, ga, ob_list, lseb_list, gb, oc, gc):
    seq = x.shape[0]
    tm = min(ROW_TILE, seq)
    inv_d = 1.0 / D_MODEL
    nd = len(B_DILS)

    def body(*refs):
        (x_ref, t_ref, gp_ref, w_ref, sink_ref, oa_ref, lsea_ref, ga_ref), refs = refs[:8], refs[8:]
        ob_refs, lb_refs, (gb_ref, oc_ref, gc_ref), refs = refs[:nd], refs[nd:2 * nd], refs[2 * nd:2 * nd + 3], refs[2 * nd + 3:]
        (g_ref, doa_ref, dla_ref, dga_ref), refs = refs[:4], refs[4:]
        dob_refs, lsec_refs, dlb_refs, refs = refs[:nd], refs[nd:2 * nd], refs[2 * nd:3 * nd], refs[3 * nd:]
        (dgb_ref, doc_ref, dlc_ref, dgc_ref, gw_ref, gpost_ref, gsink_ref, loss_ref), refs = refs[:8], refs[8:]
        ycat, obufs, lbufs, st_do, st_l, st_d = refs[0], refs[1:nd], refs[nd:2 * nd - 1], refs[2 * nd - 1], refs[2 * nd], refs[2 * nd + 1]

        @pl.when(pl.program_id(0) == 0)
        def _():
            gw_ref[...] = jnp.zeros_like(gw_ref)
            gpost_ref[...] = jnp.zeros_like(gpost_ref)
            gsink_ref[...] = jnp.zeros_like(gsink_ref)
            loss_ref[...] = jnp.zeros_like(loss_ref)

        o_i, l_i = [ob_refs[0][0].astype(F32)], [lb_refs[0][0]]
        for k in range(1, nd):
            _from_residues(ob_refs[k], obufs[k - 1], B_DILS[k])
            _from_residues(lb_refs[k], lbufs[k - 1], B_DILS[k])
            o_i.append(_stage_read(obufs[k - 1]))
            l_i.append(_stage_read(lbufs[k - 1]))
        mx = l_i[0]
        for l in l_i[1:]:
            mx = jnp.maximum(mx, l)
        w_i = [jnp.exp(l - mx) for l in l_i]
        z = w_i[0]
        for w in w_i[1:]:
            z = z + w
        _stage_write(st_l, mx + jnp.log(z))
        expand = _head_expand_matrix(B_W)
        inv_z = 1.0 / z
        ob = None
        for w, o in zip(w_i, o_i):
            term = _dot_split(w * inv_z, expand, 2) * o
            ob = term if ob is None else ob + term
        oa, oc = oa_ref[...].astype(F32), oc_ref[...].astype(F32)
        sa, dsa = _silu_and_grad(ga_ref[...].astype(F32))
        sb, dsb = _silu_and_grad(gb_ref[...].astype(F32))
        sc, dsc = _silu_and_grad(gc_ref[...].astype(F32))
        ycat[:, 0:A_W] = (oa * sa).astype(BF16)
        ycat[:, A_W:A_W + B_W] = (ob * sb).astype(BF16)
        ycat[:, A_W + B_W:] = (oc * sc).astype(BF16)
        y2 = _dot(ycat[...], w_ref[...])
        r = lax.rsqrt(jnp.mean(y2 * y2, axis=-1, keepdims=True) + RMS_EPS)
        zhat = y2 * r
        gp = gp_ref[...]
        err = x_ref[...] + zhat * gp - t_ref[...]
        loss_ref[...] += jnp.sum(err * err) * (0.5 * inv_d)
        g = err * inv_d
        g_ref[...] = g
        gpost_ref[...] += jnp.sum(g * zhat, axis=0, keepdims=True)
        a = g * gp
        dy2 = (r * (a - zhat * jnp.mean(a * zhat, axis=-1, keepdims=True))).astype(BF16)
        gw_ref[...] += _dot_tn(ycat[...], dy2)
        dycat = _dot_nt(dy2, w_ref[...])
        dya, dyb, dyc = dycat[:, 0:A_W], dycat[:, A_W:A_W + B_W], dycat[:, A_W + B_W:]
        doa, dob, doc = dya * sa, dyb * sb, dyc * sc
        doa_ref[...] = doa.astype(BF16)
        doc_ref[...] = doc.astype(BF16)
        dga_ref[...] = (dya * oa * dsa).astype(BF16)
        dgb_ref[...] = (dyb * ob * dsb).astype(BF16)
        dgc_ref[...] = (dyc * oc * dsc).astype(BF16)
        dl_a = _dot_split(doa * oa, _head_sum_matrix(A_W), 2)
        dla_ref[...] = dl_a
        dlc_ref[...] = _dot_split(doc * oc, _head_sum_matrix(C_W), 2)
        gsink_ref[...] += jnp.sum(jnp.exp(sink_ref[...] - lsea_ref[...]) * dl_a, axis=0, keepdims=True)
        _stage_write(st_do, dob)
        _stage_write(st_d, _dot_split(dob * ob, _head_sum_matrix(B_W), 2))
        for k, dil in enumerate(B_DILS):
            _to_residues(st_do, dob_refs[k], dil)
            _to_residues(st_l, lsec_refs[k], dil)
            _to_residues(st_d, dlb_refs[k], dil)

    row = lambda w: pl.BlockSpec((tm, w), lambda i: (i, 0))
    full = lambda shape: pl.BlockSpec(shape, lambda i: (0,) * len(shape))
    res_specs = lambda w: [_residue_spec(d, tm, w) for d in B_DILS]
    res_shapes = lambda w, dt: [jax.ShapeDtypeStruct((d, seq // d, w), dt) for d in B_DILS]
    ins = [x, target, post_norm, w_out, sink_row, oa, lse_a, ga, *ob_list, *lseb_list, gb, oc, gc]
    in_specs = ([row(D_MODEL), row(D_MODEL), full((1, D_MODEL)), full((D_MODEL, D_MODEL)), full((1, LANES)),
                 row(A_W), row(LANES), row(A_W)] + res_specs(B_W) + res_specs(LANES) + [row(B_W), row(C_W), row(C_W)])
    out_shape = ([jax.ShapeDtypeStruct((seq, D_MODEL), F32), jax.ShapeDtypeStruct((seq, A_W), BF16),
                  jax.ShapeDtypeStruct((seq, LANES), F32), jax.ShapeDtypeStruct((seq, A_W), BF16)]
                 + res_shapes(B_W, BF16) + res_shapes(LANES, F32) + res_shapes(LANES, F32)
                 + [jax.ShapeDtypeStruct((seq, B_W), BF16), jax.ShapeDtypeStruct((seq, C_W), BF16),
                    jax.ShapeDtypeStruct((seq, LANES), F32), jax.ShapeDtypeStruct((seq, C_W), BF16),
                    jax.ShapeDtypeStruct((D_MODEL, D_MODEL), F32), jax.ShapeDtypeStruct((1, D_MODEL), F32),
                    jax.ShapeDtypeStruct((1, LANES), F32), jax.ShapeDtypeStruct((1, LANES), F32)])
    out_specs = ([row(D_MODEL), row(A_W), row(LANES), row(A_W)] + res_specs(B_W) + res_specs(LANES) + res_specs(LANES)
                 + [row(B_W), row(C_W), row(LANES), row(C_W),
                    full((D_MODEL, D_MODEL)), full((1, D_MODEL)), full((1, LANES)), full((1, LANES))])
    scratch = ([pltpu.VMEM((tm, D_MODEL), BF16)] + [_stage(tm, B_W)] * (nd - 1) + [_stage(tm, LANES)] * (nd - 1)
               + [_stage(tm, B_W), _stage(tm, LANES), _stage(tm, LANES)])
    res = pl.pallas_call(
        body, name="post", grid=(seq // tm,), in_specs=in_specs, out_specs=out_specs, out_shape=out_shape,
        scratch_shapes=scratch,
        compiler_params=pltpu.CompilerParams(dimension_semantics=("arbitrary",)),
    )(*ins)
    out = dict(g=res[0], doa=res[1], dl_a=res[2], dga=res[3], dob=res[4:4 + nd], lse_b=res[4 + nd:4 + 2 * nd],
               dl_b=res[4 + 2 * nd:4 + 3 * nd])
    rest = res[4 + 3 * nd:]
    out.update(dgb=rest[0], doc=rest[1], dl_c=rest[2], dgc=rest[3], gw_out=rest[4], gpost=rest[5], gsink=rest[6],
               loss=rest[7])
    return out


def _grad_w_in(ut, nat, res):
    tm = ut.shape[2]
    seq = ut.shape[0] * tm
    nd = len(B_DILS)
    nat_list = [nat[n] for n in _NATURAL]
    res_list = [a for n in _DILATED for a in res[n]]
    rope = _rope_tables(seq, tm)

    def body(rl_ref, rb_ref, ut_ref, *refs):
        nat_refs = dict(zip(_NATURAL, refs[:len(_NATURAL)]))
        refs = refs[len(_NATURAL):]
        res_refs = {n: refs[nd * k:nd * (k + 1)] for k, n in enumerate(_DILATED)}
        refs = refs[nd * len(_DILATED):]
        dproj_ref, gw_ref = refs[:2]
        bufs = {n: refs[2 + (nd - 1) * k:2 + (nd - 1) * (k + 1)] for k, n in enumerate(_DILATED)}

        @pl.when(pl.program_id(0) == 0)
        def _():
            gw_ref[...] = jnp.zeros_like(gw_ref)

        for n in _DILATED:
            for k in range(1, nd):
                _from_residues(res_refs[n][k], bufs[n][k - 1], B_DILS[k])
        c, sm, sp = _rope_coeffs(rl_ref, rb_ref)
        sm, sp = -sm, -sp
        for blk, (name, off, roped, scaled) in enumerate(_PROJ_LAYOUT):
            lanes = slice(off, off + LANES)
            if name in nat_refs:
                piece = nat_refs[name][:, lanes].astype(F32)
            else:
                piece = res_refs[name][0][0, :, lanes].astype(F32)
                for buf in bufs[name]:
                    piece = piece + buf[off // LANES]
            if roped:
                piece = _rope(piece, c, sm, sp)
            if scaled:
                piece = piece * SCALE
            dproj_ref[:, blk * LANES:(blk + 1) * LANES] = piece.astype(BF16)
        for j in range(N_CHIPS):
            gw_ref[j] += _dot(ut_ref[...], dproj_ref[:, j * SHARD_IN:(j + 1) * SHARD_IN])

    row = lambda w: pl.BlockSpec((tm, w), lambda i: (i, 0))
    in_specs = ([pl.BlockSpec(rope[0].shape, lambda i: (0, 0)), pl.BlockSpec((8, 2 * LANES), lambda i: (i, 0)),
                 pl.BlockSpec((None, D_MODEL, tm), lambda i: (i, 0, 0))]
                + [row(a.shape[1]) for a in nat_list]
                + [_residue_spec(d, tm, B_W) for _ in _DILATED for d in B_DILS])
    return pl.pallas_call(
        body, name="grad_w_in", grid=(seq // tm,), in_specs=in_specs,
        out_specs=[row(D_IN), pl.BlockSpec((N_CHIPS, D_MODEL, SHARD_IN), lambda i: (0, 0, 0))],
        out_shape=[jax.ShapeDtypeStruct((seq, D_IN), BF16), jax.ShapeDtypeStruct((N_CHIPS, D_MODEL, SHARD_IN), F32)],
        scratch_shapes=[_stage(tm, B_W)] * ((nd - 1) * len(_DILATED)),
        compiler_params=pltpu.CompilerParams(dimension_semantics=("arbitrary",)),
    )(*rope, ut, *nat_list, *res_list)


def _input_grad(x, g, pre_norm, w_in_g, dproj, gx_prev, span, after, name):
    seq = x.shape[0]
    tm = seq // INPUT_GRAD_TILES
    first_block, steps = span

    def body(*refs):
        x_ref, g_ref, gp_ref, w_ref, dp_ref = refs[:5]
        gx_ref, gpre_ref = refs[-2:]

        @pl.when(pl.program_id(0) == 0)
        def _():
            gpre_ref[...] = jnp.zeros_like(gpre_ref)

        du = None
        for j in range(N_CHIPS):
            term = _dot_nt(dp_ref[:, j * SHARD_IN:(j + 1) * SHARD_IN], w_ref[j])
            du = term if du is None else du + term
        xv = x_ref[...]
        r = lax.rsqrt(jnp.mean(xv * xv, axis=-1, keepdims=True) + RMS_EPS)
        xhat = xv * r
        gpre_ref[...] += jnp.sum(du * xhat, axis=0, keepdims=True)
        a = du * gp_ref[...]
        gx_ref[...] = g_ref[...] + r * (a - xhat * jnp.mean(a * xhat, axis=-1, keepdims=True))

    row = lambda w: pl.BlockSpec((tm, w), lambda i: (first_block + i, 0))
    full = lambda a: pl.BlockSpec(a.shape, lambda i: (0,) * a.ndim)
    any_spec = pl.BlockSpec(memory_space=pl.ANY)
    ins = [x, g, pre_norm, w_in_g, dproj]
    in_specs = [row(D_MODEL), row(D_MODEL), full(pre_norm), full(w_in_g), row(D_IN)]
    aliases = {}
    if gx_prev is not None:
        aliases[len(ins)] = 0
        ins.append(gx_prev)
        in_specs.append(any_spec)
    if after is not None:
        ins.append(after)
        in_specs.append(any_spec)
    return pl.pallas_call(
        body, name=name, grid=(steps,), in_specs=in_specs,
        out_specs=[row(D_MODEL), pl.BlockSpec((1, D_MODEL), lambda i: (0, 0))],
        out_shape=[jax.ShapeDtypeStruct((seq, D_MODEL), F32), jax.ShapeDtypeStruct((1, D_MODEL), F32)],
        input_output_aliases=aliases,
        compiler_params=pltpu.CompilerParams(dimension_semantics=("arbitrary",)),
    )(*ins)


def _exchange_start(ex, name):
    n_in, n_out, n_sem = len(ex["ins"]), len(ex["outs"]), len(ex["sems"])

    def body(*refs):
        in_refs, land_refs, sems = refs[:n_in], refs[n_in:n_in + n_out], refs[n_in + n_out:n_in + n_out + n_sem]
        ex["start"](in_refs, land_refs, *sems)
        token = refs[-1]
        token[...] = jnp.zeros_like(token)

    hbm = pl.BlockSpec(memory_space=pltpu.HBM)
    sem = pl.BlockSpec(memory_space=pltpu.SEMAPHORE)
    ins = [pltpu.with_memory_space_constraint(a, pltpu.HBM) for a in ex["ins"]]
    landing = [pltpu.with_memory_space_constraint(lax.empty(o.shape, o.dtype), pltpu.HBM) for o in ex["outs"]]
    res = pl.pallas_call(
        body, name=name,
        out_shape=list(ex["sems"]) + [pltpu.HBM(a.shape, a.dtype) for a in ex["ins"]]
        + [pltpu.HBM(o.shape, o.dtype) for o in ex["outs"]] + [jax.ShapeDtypeStruct((8, LANES), F32)],
        in_specs=[hbm] * (n_in + n_out),
        out_specs=[sem] * n_sem + [hbm] * (n_in + n_out) + [pl.BlockSpec(memory_space=pltpu.VMEM)],
        input_output_aliases={k: n_sem + k for k in range(n_in + n_out)},
        compiler_params=pltpu.CompilerParams(has_side_effects=pltpu.SideEffectType.DATAFLOW_SIDE_EFFECTING),
    )(*ins, *landing)
    return res[:-1], res[-1]


def _exchange_wait(ex, handles, after, name):
    n_in, n_out, n_sem = len(ex["ins"]), len(ex["outs"]), len(ex["sems"])
    sems, thru = handles[:n_sem], handles[n_sem:]

    def body(*refs):
        in_refs, land_refs = refs[:n_in], refs[n_in:n_in + n_out]
        sem_refs = refs[n_in + n_out:n_in + n_out + n_sem]
        ex["finish"](in_refs, land_refs, *sem_refs)

    hbm = pl.BlockSpec(memory_space=pltpu.HBM)
    sem = pl.BlockSpec(memory_space=pltpu.SEMAPHORE)
    res = pl.pallas_call(
        body, name=name,
        out_shape=[pltpu.HBM(a.shape, a.dtype) for a in thru],
        in_specs=[hbm] * (n_in + n_out) + [sem] * n_sem + [pl.BlockSpec(memory_space=pl.ANY)],
        out_specs=[hbm] * (n_in + n_out),
        input_output_aliases={k: k for k in range(n_in + n_out)},
        compiler_params=pltpu.CompilerParams(has_side_effects=pltpu.SideEffectType.DATAFLOW_SIDE_EFFECTING),
    )(*thru, *sems, after)
    return res[:n_in], res[n_in:]


def _start_finish(build):
    def start(*refs):
        for cp in build(*refs):
            cp.start()

    def finish(*refs):
        for cp in build(*refs):
            cp.wait()

    return dict(start=start, finish=finish)


def _pair_exchange(grads):
    n = len(grads)

    def build(srcs, outs, send_sems, recv_sems):
        x, y, c = lax.axis_index("x"), lax.axis_index("y"), lax.axis_index("c")
        copies = []
        for t in range(n):
            rows = grads[t].shape[1] // 2
            copies.append(pltpu.make_async_remote_copy(
                src_ref=srcs[t].at[:, pl.ds((1 - c) * rows, rows)], dst_ref=outs[t],
                send_sem=send_sems.at[t], recv_sem=recv_sems.at[t], device_id=(x, y, 1 - c), device_id_type=MESH))
        return copies

    return dict(ins=list(grads), **_start_finish(build),
                outs=[jax.ShapeDtypeStruct((g.shape[0], g.shape[1] // 2, g.shape[2]), g.dtype) for g in grads],
                sems=[pltpu.SemaphoreType.DMA((n,)), pltpu.SemaphoreType.DMA((n,))])


def _pair_add(core, owns, gots):
    n = len(owns)

    def body(core_ref, *refs):
        for t in range(n):
            refs[2 * n + t][...] = (refs[t][...] + refs[n + t][...]).astype(BF16)

    halves = [(None,) + g.shape[1:] for g in gots]
    grid_spec = pltpu.PrefetchScalarGridSpec(
        num_scalar_prefetch=1, grid=(N_CHIPS,),
        in_specs=[pl.BlockSpec(h, lambda k, core_ref: (k, core_ref[0], 0)) for h in halves]
        + [pl.BlockSpec(h, lambda k, core_ref: (k, 0, 0)) for h in halves],
        out_specs=[pl.BlockSpec(h, lambda k, core_ref: (k, 0, 0)) for h in halves])
    return pl.pallas_call(
        body, name="pair_add", grid_spec=grid_spec,
        out_shape=[jax.ShapeDtypeStruct(g.shape, BF16) for g in gots],
    )(core, *owns, *gots)


def _chip_exchange(parts):
    n = len(parts)

    def build(srcs, outs, send_sems, recv_sems, local_sems):
        x, y, c = lax.axis_index("x"), lax.axis_index("y"), lax.axis_index("c")
        my_chip = 2 * x + y
        chips = [(1 - x, y), (x, 1 - y), (1 - x, 1 - y)]
        copies = [pltpu.make_async_copy(srcs[t].at[my_chip], outs[t].at[my_chip], local_sems.at[t]) for t in range(n)]
        for j, (cx, cy) in enumerate(chips):
            for t in range(n):
                k = n * j + t
                copies.append(pltpu.make_async_remote_copy(
                    src_ref=srcs[t].at[2 * cx + cy], dst_ref=outs[t].at[my_chip], send_sem=send_sems.at[k],
                    recv_sem=recv_sems.at[k], device_id=(cx, cy, c), device_id_type=MESH))
        return copies

    return dict(ins=list(parts), **_start_finish(build), outs=[jax.ShapeDtypeStruct(p.shape, p.dtype) for p in parts],
                sems=[pltpu.SemaphoreType.DMA((3 * n,)), pltpu.SemaphoreType.DMA((3 * n,)),
                      pltpu.SemaphoreType.DMA((n,))])


def _chip_sum(slots):
    n = len(slots)

    def body(*refs):
        for t in range(n):
            acc = refs[t][0].astype(F32)
            for s in range(1, N_CHIPS):
                acc = acc + refs[t][s].astype(F32)
            refs[n + t][...] = acc

    blocks = [(s.shape[1] // TAIL_STEPS, s.shape[2]) for s in slots]
    return pl.pallas_call(
        body, name="chip_sum", grid=(TAIL_STEPS,),
        in_specs=[pl.BlockSpec((N_CHIPS,) + b, lambda i: (0, i, 0)) for b in blocks],
        out_specs=[pl.BlockSpec(b, lambda i: (i, 0)) for b in blocks],
        out_shape=[jax.ShapeDtypeStruct(s.shape[1:], F32) for s in slots],
    )(*slots)


def _half_exchange(halves, small):
    n = len(halves)

    def build(srcs, outs, send_sems, recv_sems, local_sems):
        x, y, c = lax.axis_index("x"), lax.axis_index("y"), lax.axis_index("c")
        me = 4 * x + 2 * y + c
        chips = [(1 - x, y), (x, 1 - y), (1 - x, 1 - y)]
        copies = [pltpu.make_async_copy(srcs[n], outs[n].at[me], local_sems.at[0])]
        copies += [pltpu.make_async_remote_copy(
            src_ref=srcs[t], dst_ref=outs[t], send_sem=send_sems.at[t], recv_sem=recv_sems.at[t],
            device_id=(x, y, 1 - c), device_id_type=MESH) for t in range(n)]
        peers = [(x, y, 1 - c)] + [(cx, cy, cc) for (cx, cy) in chips for cc in (c, 1 - c)]
        for j, peer in enumerate(peers):
            copies.append(pltpu.make_async_remote_copy(
                src_ref=srcs[n], dst_ref=outs[n].at[me], send_sem=send_sems.at[n + j],
                recv_sem=recv_sems.at[n + j], device_id=peer, device_id_type=MESH))
        return copies

    return dict(ins=list(halves) + [small], **_start_finish(build),
                outs=[jax.ShapeDtypeStruct(h.shape, h.dtype) for h in halves]
                + [jax.ShapeDtypeStruct((8,) + small.shape, small.dtype)],
                sems=[pltpu.SemaphoreType.DMA((n + 7,)), pltpu.SemaphoreType.DMA((n + 7,)),
                      pltpu.SemaphoreType.DMA((1,))])


def _adamw(half, ws, gs, ms, vs, prev=None, after=None):
    n = len(ws)

    def body(half_ref, *refs):
        outs = refs[len(refs) - 4 * n:]
        for t in range(n):
            w_ref, g_ref, m_ref, v_ref = refs[t:4 * n:n]
            gout_ref, d_ref, nm_ref, nv_ref = outs[t::n]
            g = g_ref[...]
            gout_ref[...] = g
            d_ref[...], nm_ref[...], nv_ref[...] = _adamw_math(w_ref[...], g, m_ref[...], v_ref[...])

    blocks = [(w.shape[0] // (2 * TAIL_STEPS), w.shape[1]) for w in ws]
    whole = [pl.BlockSpec(b, lambda i, half_ref: (half_ref[0] * TAIL_STEPS + i, 0)) for b in blocks]
    halved = [pl.BlockSpec(b, lambda i, half_ref: (i, 0)) for b in blocks]
    any_spec = pl.BlockSpec(memory_space=pl.ANY)
    ins = [half, *ws, *gs, *ms, *vs]
    in_specs = whole + halved + whole * 2
    aliases = {}
    if prev is not None:
        flat = [prev[t][k] for k in range(4) for t in range(n)]
        aliases = {len(ins) + k: k for k in range(4 * n)}
        ins += flat
        in_specs += [any_spec] * (4 * n)
    if after is not None:
        ins.append(after)
        in_specs.append(any_spec)
    grid_spec = pltpu.PrefetchScalarGridSpec(num_scalar_prefetch=1, grid=(TAIL_STEPS,), in_specs=in_specs,
                                             out_specs=whole * 4)
    res = pl.pallas_call(
        body, name="adamw_own_half" if prev is None else "adamw_other_half", grid_spec=grid_spec,
        out_shape=[jax.ShapeDtypeStruct(w.shape, F32) for w in ws] * 4, input_output_aliases=aliases,
    )(*ins)
    return [res[t::n] for t in range(n)]


def _adamw_math(w, g, m, v):
    c1 = 1.0 / (1.0 - ADAM_B1 ** ADAM_STEP)
    c2 = 1.0 / (1.0 - ADAM_B2 ** ADAM_STEP)
    nm = ADAM_B1 * m + (1.0 - ADAM_B1) * g
    nv = ADAM_B2 * v + (1.0 - ADAM_B2) * (g * g)
    return -ADAM_LR * ((nm * c1) / (jnp.sqrt(nv * c2) + ADAM_EPS) + ADAM_WD * w), nm, nv


def _small_update(slots, params, ms, vs):
    n = len(params)

    def body(slots_ref, *refs):
        w_refs, m_refs, v_refs, loss_ref = refs[:n], refs[n:2 * n], refs[2 * n:3 * n], refs[3 * n]
        g_refs, d_refs, nm_refs, nv_refs = (refs[3 * n + 1 + k * n:3 * n + 1 + (k + 1) * n] for k in range(4))
        acc = slots_ref[0]
        for s in range(1, slots.shape[0]):
            acc = acc + slots_ref[s]
        loss_ref[...] = acc[4:5, 0:1]
        grads = (acc[0:1] + acc[5:6], acc[3:4], acc[2:3], acc[1:2])
        for k in range(n):
            g = grads[k][:, :w_refs[k].shape[1]]
            g_refs[k][...] = g
            d_refs[k][...], nm_refs[k][...], nv_refs[k][...] = _adamw_math(w_refs[k][...], g, m_refs[k][...],
                                                                           v_refs[k][...])

    return pl.pallas_call(
        body, name="small_update",
        out_shape=[jax.ShapeDtypeStruct((1, 1), F32)] + [jax.ShapeDtypeStruct(p.shape, F32) for p in params] * 4,
    )(slots, *params, *ms, *vs)


def _local_step(x, mem, target, pre_norm, sink_a, mem_norm, post_norm, w_in_g, w_out, w_mkv, gathers=None,
                own=None):
    first_gather, late_gather = gathers if gathers else (None, None)
    u, ut, p_own, hosted = _pre_norm(x, pre_norm, own[1] if own else None, first_gather)
    if gathers:
        w_in_g = hosted[0].reshape(N_CHIPS, D_MODEL, SHARD_IN)
    pr = _pre_proj(u, w_in_g, (own[0], p_own) if own else None, late_gather)
    pr["ut"] = ut
    if gathers:
        w_out, w_mkv = (g.reshape(D_MODEL, g.shape[-1]) for g in pr["hosted"])
    mk, mv = _mem_kv(mem, mem_norm, w_mkv)
    sink = sink_a.reshape(-1)
    qa, ka, va = pr["qa"][None], pr["ka"][None], pr["va"][None]
    oa, lse_a = _band_fwd(qa, ka, va, sink, max_dist=A_WINDOW - 1, name="swa_fwd")
    ob_list, lseb_list = [], []
    for k, (win, dil) in enumerate(B_CONFIGS):
        o_i, l_i = _band_fwd(pr["qb"][k], pr["kb"][k], pr["vb"][k], None, max_dist=win // dil, name=f"dil{dil}_fwd")
        ob_list.append(o_i)
        lseb_list.append(l_i)
    oc, lse_c = _mem_attn_fwd(pr["qc"], mk, mv)
    sink_row = jnp.pad(sink, (0, LANES - sink.shape[0])).reshape(1, LANES)
    po = _post(x, target, post_norm, w_out, sink_row, oa[0], lse_a[0], pr["ga"], ob_list, lseb_list, pr["gb"], oc,
               pr["gc"])
    dqc, dmk, dmv = _mem_attn_bwd(pr["qc"], mk, mv, po["doc"], lse_c, po["dl_c"])
    dqa, dka, dva = _band_bwd(qa, ka, va, po["doa"][None], lse_a, po["dl_a"][None], max_dist=A_WINDOW - 1,
                              name="swa_bwd")
    res = dict(qb=[], kb=[], vb=[])
    for k, (win, dil) in enumerate(B_CONFIGS):
        dq_i, dk_i, dv_i = _band_bwd(pr["qb"][k], pr["kb"][k], pr["vb"][k], po["dob"][k], po["lse_b"][k],
                                     po["dl_b"][k], max_dist=win // dil, name=f"dil{dil}_bwd")
        res["qb"].append(dq_i)
        res["kb"].append(dk_i)
        res["vb"].append(dv_i)
    nat = dict(qa=dqa[0], ka=dka[0], va=dva[0], ga=po["dga"], gb=po["dgb"], qc=dqc, gc=po["dgc"])
    dproj, gw_in = _grad_w_in(pr["ut"], nat, res)
    gw_mkv, gmem = _mem_kv_bwd(mem, mem_norm, w_mkv, dmk, dmv)
    gsink = -po["gsink"][0, :sink.shape[0]]
    return dict(loss=po["loss"], g=po["g"], dproj=dproj, gw_in=gw_in, gw_out=po["gw_out"], gw_mkv=gw_mkv,
                gpost=po["gpost"], gmem=gmem, gsink=gsink, w_in_g=w_in_g)


def kernel(x, mem, pre_norm, w_in, sink_a, mem_norm, w_mem_kv, w_out, post_norm, loss_target, m_pre_norm, m_w_in, m_sink_a, m_mem_norm, m_w_mem_kv, m_w_out, m_post_norm, v_pre_norm, v_w_in, v_sink_a, v_mem_norm, v_w_mem_kv, v_w_out, v_post_norm):
    w_own = w_in[0]
    gathers = (_gather_exchange([w_own], prefilled=False),
               _gather_exchange([w_out[0].astype(BF16), w_mem_kv[0].astype(BF16)]))
    chip = (2 * lax.axis_index("x") + lax.axis_index("y")).astype(jnp.int32).reshape(1)
    loc = _local_step(x[0], mem[0], loss_target[0], pre_norm, sink_a, mem_norm, post_norm, None, None, None, gathers,
                      (chip, w_own))
    big = [loc["gw_in"], loc["gw_out"].reshape(N_CHIPS, D_MODEL // N_CHIPS, D_MODEL),
           loc["gw_mkv"].reshape(N_CHIPS, D_MODEL // N_CHIPS, 2 * C_W)]
    core = lax.axis_index("c").astype(jnp.int32).reshape(1)
    w_in_full = loc["w_in_g"]
    step_in = (x[0], loc["g"], pre_norm, w_in_full, loc["dproj"])
    pair_ex = _pair_exchange(big)
    pair_handles, token = _exchange_start(pair_ex, "pair_exchange_start")
    gx_a, gpre_a = _input_grad(*step_in, None, (0, 2), token, "input_grad_a")
    big, got = _exchange_wait(pair_ex, pair_handles, gpre_a, "pair_exchange_wait")
    parts = _pair_add(core, big, got)
    chip_ex = _chip_exchange(parts)
    chip_handles, token = _exchange_start(chip_ex, "chip_exchange_start")
    grad_x, gpre_b = _input_grad(*step_in, gx_a, (2, 14), token, "input_grad_b")
    _, slots = _exchange_wait(chip_ex, chip_handles, gpre_b, "chip_exchange_wait")
    halves = _chip_sum(slots)
    widen = lambda a: jnp.pad(a.reshape(1, -1), ((0, 0), (0, D_MODEL - a.size)))
    small = jnp.concatenate([gpre_a, loc["gpost"], loc["gmem"], widen(loc["gsink"]), widen(loc["loss"]), gpre_b,
                             jnp.zeros((2, D_MODEL), F32)], axis=0)
    half_ex = _half_exchange(halves, small)
    half_handles, token = _exchange_start(half_ex, "half_exchange_start")
    n_sem = len(half_ex["sems"])
    weights = ((w_in[0], w_out[0], w_mem_kv[0]), (m_w_in[0], m_w_out[0], m_w_mem_kv[0]),
               (v_w_in[0], v_w_out[0], v_w_mem_kv[0]))
    own_done = _adamw(core, weights[0], half_handles[n_sem:n_sem + 3], weights[1], weights[2], None, token)
    _, landed = _exchange_wait(half_ex, half_handles, own_done[0][0], "half_exchange_wait")
    (loss, g_pre, g_sink, g_mem, g_post, d_pre, d_sink, d_mem, d_post, nm_pre, nm_sink, nm_mem, nm_post,
     nv_pre, nv_sink, nv_mem, nv_post) = _small_update(
        landed[3], (pre_norm, sink_a, mem_norm, post_norm), (m_pre_norm, m_sink_a, m_mem_norm, m_post_norm),
        (v_pre_norm, v_sink_a, v_mem_norm, v_post_norm))
    (g_in, d_in, nm_in, nv_in), (g_out, d_out, nm_out, nv_out), (g_mkv, d_mkv, nm_mkv, nv_mkv) = _adamw(
        1 - core, weights[0], landed[:3], weights[1], weights[2], own_done)
    lead = lambda a: a[None]
    return (loss.reshape(()), lead(grad_x),
            g_pre, lead(g_in), g_sink, g_mem, lead(g_mkv), lead(g_out), g_post,
            d_pre, lead(d_in), d_sink, d_mem, lead(d_mkv), lead(d_out), d_post,
            nm_pre, lead(nm_in), nm_sink, nm_mem, lead(nm_mkv), lead(nm_out), nm_post,
            nv_pre, lead(nv_in), nv_sink, nv_mem, lead(nv_mkv), lead(nv_out), nv_post)
```

```python
import numpy as np
import jax
import jax.numpy as jnp
from jax import lax
from jax.experimental import pallas as pl
from jax.experimental.pallas import tpu as pltpu

F32 = jnp.float32
BF16 = jnp.bfloat16

D_MODEL = 1024
HEAD_DIM = 64
LANES = 128
BLOCK = 128
ROW_TILE = 512
ATTN_TILE = 1024
INPUT_GRAD_TILES = 16
TAIL_STEPS = 4
A_W, A_KV_W, B_W, C_W = 384, 128, 384, 256
N_MEM = 256
D_IN = 3072
N_CHIPS = 4
SHARD_IN = D_IN // N_CHIPS
B_CONFIGS = ((128, 1), (512, 4), (2048, 16))
B_DILS = tuple(d for _, d in B_CONFIGS)
A_WINDOW = 128
RMS_EPS = 1e-6
ROPE_THETA = 500000.0
SCALE = HEAD_DIM ** -0.5
NEG = -1e30
ADAM_LR, ADAM_B1, ADAM_B2, ADAM_EPS, ADAM_WD, ADAM_STEP = 0.001, 0.9, 0.999, 1e-08, 0.01, 10

NT = (((1,), (1,)), ((), ()))
TN = (((0,), (0,)), ((), ()))
MESH = pl.DeviceIdType.MESH

_PROJ_LAYOUT = (
    [("qa", 128 * i, True, True) for i in range(3)] + [("ka", 0, True, False), ("va", 0, False, False)]
    + [("ga", 128 * i, False, False) for i in range(3)]
    + [("qb", 128 * i, True, True) for i in range(3)] + [("kb", 128 * i, True, False) for i in range(3)]
    + [("vb", 128 * i, False, False) for i in range(3)] + [("gb", 128 * i, False, False) for i in range(3)]
    + [("qc", 128 * i, False, True) for i in range(2)] + [("gc", 128 * i, False, False) for i in range(2)]
)
_PROJ_WIDTH = dict(qa=A_W, ka=A_KV_W, va=A_KV_W, ga=A_W, qb=B_W, kb=B_W, vb=B_W, gb=B_W, qc=C_W, gc=C_W)
_NATURAL = ("qa", "ka", "va", "ga", "gb", "qc", "gc")
_DILATED = ("qb", "kb", "vb")


def _dot(a, b):
    return jnp.dot(a, b, preferred_element_type=F32)


def _dot_nt(a, b):
    return lax.dot_general(a, b, NT, preferred_element_type=F32)


def _dot_tn(a, b):
    return lax.dot_general(a, b, TN, preferred_element_type=F32)


def _half_masks(rows):
    lane = lax.broadcasted_iota(jnp.int32, (rows, LANES), 1)
    return lane < HEAD_DIM, lane >= HEAD_DIM


def _rope(t, c, sm, sp):
    return t * c + pltpu.roll(t, LANES - 8, 1) * sm + pltpu.roll(t, 8, 1) * sp


def _rope_tables(seq, tm):
    dim = np.arange(LANES) % HEAD_DIM
    inv_freq = (np.float32(ROPE_THETA) ** (-np.arange(0, 16, 2, dtype=np.float32) / np.float32(16))).astype(np.float64)
    freq = np.where(dim < 16, inv_freq[dim % 8], 0.0)[None, :]
    local = np.arange(tm, dtype=np.float64)[:, None] * freq
    base = (np.arange(seq // tm, dtype=np.float64) * tm)[:, None] * freq
    both = lambda a: np.concatenate([np.cos(a), np.sin(a)], axis=1).astype(np.float32)
    return jnp.asarray(both(local)), jnp.asarray(np.repeat(both(base), 8, axis=0))


def _rope_coeffs(local_ref, base_ref):
    cl, sl = local_ref[:, :LANES], local_ref[:, LANES:]
    cb, sb = base_ref[0:1, :LANES], base_ref[0:1, LANES:]
    cos = cb * cl - sb * sl
    sin = sb * cl + cb * sl
    dim = lax.broadcasted_iota(jnp.int32, (1, LANES), 1) % HEAD_DIM
    return cos, jnp.where(dim < 8, -sin, 0.0), jnp.where((dim >= 8) & (dim < 16), sin, 0.0)


def _split3(x):
    a = x.astype(BF16)
    r = x - a.astype(F32)
    b = r.astype(BF16)
    c = (r - b.astype(F32)).astype(BF16)
    return a, b, c


def _rows_to_lanes(x):
    row = lax.broadcasted_iota(jnp.int32, (8, LANES), 0)
    lane = lax.broadcasted_iota(jnp.int32, (8, LANES), 1)
    eye = (row == lane).astype(BF16)
    a, b, c = _split3(x)
    return _dot_nt(eye, a) + _dot_nt(eye, b) + _dot_nt(eye, c)


def _head_sum_matrix(width):
    k = lax.broadcasted_iota(jnp.int32, (width, LANES), 0)
    h = lax.broadcasted_iota(jnp.int32, (width, LANES), 1)
    return (k // HEAD_DIM == h).astype(BF16)


def _head_expand_matrix(width):
    h = lax.broadcasted_iota(jnp.int32, (LANES, width), 0)
    k = lax.broadcasted_iota(jnp.int32, (LANES, width), 1)
    return (k // HEAD_DIM == h).astype(BF16)


def _dot_split(x, mat, terms):
    parts = _split3(x)[:terms]
    out = _dot(parts[0], mat)
    for p in parts[1:]:
        out = out + _dot(p, mat)
    return out


def _per_head(cols, fill=0.0):
    rows = cols[0].shape[0]
    lane = lax.broadcasted_iota(jnp.int32, (rows, LANES), 1)
    out = jnp.full((rows, LANES), fill, F32)
    for h, col in enumerate(cols):
        out = jnp.where(lane == h, col, out)
    return out


def _lane_blocks(width):
    return [slice(p * LANES, (p + 1) * LANES) for p in range(width // LANES)]


def _stage(rows, width):
    return pltpu.VMEM((width // LANES, rows, LANES), F32)


def _stage_write(buf, value):
    for p, lanes in enumerate(_lane_blocks(value.shape[1])):
        buf[p] = value[:, lanes]


def _stage_read(buf):
    return jnp.concatenate([buf[p] for p in range(buf.shape[0])], axis=1) if buf.shape[0] > 1 else buf[0]


def _to_residues(buf, out_ref, dil):
    rows = buf.shape[1] // dil
    for r in range(dil):
        for p in range(buf.shape[0]):
            plane = buf.at[p]
            out_ref[r, :, p * LANES:(p + 1) * LANES] = plane[pl.ds(r, rows, stride=dil), :].astype(out_ref.dtype)


def _from_residues(in_ref, buf, dil):
    rows = buf.shape[1] // dil
    for r in range(dil):
        for p in range(buf.shape[0]):
            plane = buf.at[p]
            plane[pl.ds(r, rows, stride=dil), :] = in_ref[r, :, p * LANES:(p + 1) * LANES].astype(F32)


def _residue_spec(dil, tm, width):
    return pl.BlockSpec((dil, tm // dil, width), lambda i: (0, i, 0))


def _gather_exchange(shards_2d, prefilled=True):
    shards = tuple(jax.ShapeDtypeStruct((2, s.shape[0] // 2, s.shape[1]), BF16) for s in shards_2d)
    n = len(shards)

    def copies(in_refs, out_refs, send_sems, recv_sems, *local_sems):
        srcs, outs = in_refs[:n], out_refs
        x, y, c = lax.axis_index("x"), lax.axis_index("y"), lax.axis_index("c")
        my_chip = 2 * x + y
        sibling = (x, y, 1 - c)
        chips = [(1 - x, y), (x, 1 - y), (1 - x, 1 - y)]

        def copy(k, src, dst, to):
            return pltpu.make_async_remote_copy(src_ref=src, dst_ref=dst, send_sem=send_sems.at[k],
                                                recv_sem=recv_sems.at[k], device_id=to, device_id_type=MESH)

        first, arrive, passed, sibling_arrive = [], [], [], []
        for j, (cx, cy) in enumerate(chips):
            chip = 2 * cx + cy
            for t in range(n):
                k = n * j + t
                first.append(copy(k, srcs[t].at[c], outs[t].at[my_chip, c], (cx, cy, c)))
                arrive.append(copy(k, srcs[t].at[c], outs[t].at[chip, c], (cx, cy, c)))
                passed.append(copy(n * 3 + k, outs[t].at[chip, c], outs[t].at[chip, c], sibling))
                sibling_arrive.append(copy(n * 3 + k, outs[t].at[chip, 1 - c], outs[t].at[chip, 1 - c], sibling))
        own = [pltpu.make_async_copy(srcs[t], outs[t].at[my_chip], local_sems[0].at[t]) for t in range(n)
               ] if local_sems else []
        return first, arrive, passed, sibling_arrive, own

    def start(*refs):
        first, _, _, _, own = copies(*refs)
        for cp in first + own:
            cp.start()

    def forward(refs, senders):
        _, arrive, passed, _, _ = copies(*refs)
        for j in senders:
            for k in range(n * j, n * (j + 1)):
                arrive[k].wait_recv()
                passed[k].start()

    def mid(*refs):
        forward(refs, (0, 1))

    def finish(*refs):
        forward(refs, (2,))
        first, _, passed, sibling_arrive, own = copies(*refs)
        for cp in sibling_arrive:
            cp.wait_recv()
        for cp in first + passed:
            cp.wait_send()
        for cp in own:
            cp.wait()

    ex = dict(ins=[], start=start, mid=mid, finish=finish, prefilled=prefilled,
              outs=[jax.ShapeDtypeStruct((N_CHIPS,) + s.shape, s.dtype) for s in shards],
              sems=[pltpu.SemaphoreType.DMA((6 * n,)), pltpu.SemaphoreType.DMA((6 * n,))])
    if prefilled:
        my_chip = 2 * lax.axis_index("x") + lax.axis_index("y")
        halves = [a.reshape(s.shape) for a, s in zip(shards_2d, shards)]
        landing = [lax.dynamic_update_slice(jnp.zeros((N_CHIPS,) + s.shape, s.dtype), a[None], (my_chip, 0, 0, 0))
                   for a, s in zip(halves, shards)]
        ex.update(ins=halves + landing, aliases={n + t: t for t in range(n)})
    else:
        ex["sems"].append(pltpu.SemaphoreType.DMA((n,)))
    return ex


def _mem_kv(mem, mem_norm, w_mkv):
    def body(mem_ref, g_ref, w_ref, mk_ref, mv_ref):
        m = mem_ref[...]
        r = lax.rsqrt(jnp.mean(m * m, axis=-1, keepdims=True) + RMS_EPS)
        mn = (m * r * g_ref[...]).astype(BF16)
        kv = _dot(mn, w_ref[...])
        mk_ref[...] = kv[:, :C_W].astype(BF16)
        mv_ref[...] = kv[:, C_W:].astype(BF16)

    return pl.pallas_call(
        body, name="mem_kv",
        out_shape=[jax.ShapeDtypeStruct((N_MEM, C_W), BF16)] * 2,
    )(mem, mem_norm, w_mkv)


def _mem_kv_bwd(mem, mem_norm, w_mkv, dmk, dmv):
    def body(mem_ref, g_ref, w_ref, dmk_ref, dmv_ref, gw_ref, gn_ref):
        m = mem_ref[...]
        r = lax.rsqrt(jnp.mean(m * m, axis=-1, keepdims=True) + RMS_EPS)
        mhat = m * r
        mn = (mhat * g_ref[...]).astype(BF16)
        dkv = jnp.concatenate([dmk_ref[...], dmv_ref[...]], axis=1).astype(BF16)
        gw_ref[...] = _dot_tn(mn, dkv)
        dmn = _dot_nt(dkv, w_ref[...])
        gn_ref[...] = jnp.sum(dmn * mhat, axis=0, keepdims=True)

    return pl.pallas_call(
        body, name="mem_kv_bwd",
        out_shape=[jax.ShapeDtypeStruct((D_MODEL, 2 * C_W), F32), jax.ShapeDtypeStruct((1, D_MODEL), F32)],
    )(mem, mem_norm, w_mkv, dmk, dmv)


def _host_phases(host, in_refs, out_refs, sems, steps, before):
    if not host:
        return
    step = pl.program_id(0)
    phases = [("start", 0)] if before else [("mid", max(steps - 3, 0)), ("finish", steps - 1)]
    for phase, at in phases:
        pl.when(step == at)(lambda phase=phase: host[phase](in_refs, out_refs, *sems))


def _pre_norm(x, pre_norm, w_own=None, host=None):
    seq = x.shape[0]
    tm = min(ROW_TILE, seq)
    n_own_in = 2 if w_own is None else 3
    n_own_out = n_own_in
    n_host_in = len(host["ins"]) if host else 0
    n_host_out = len(host["outs"]) if host else 0
    half = D_MODEL // 2

    def body(x_ref, g_ref, *refs):
        w_ref = None if w_own is None else refs[0]
        refs = refs[n_own_in - 2:]
        host_in, own_out, refs = refs[:n_host_in], refs[n_host_in:n_host_in + n_own_out], refs[n_host_in + n_own_out:]
        host_out, refs = refs[:n_host_out], refs[n_host_out:]
        if w_own is not None:
            wb, sems = refs[0], refs[1:]

            @pl.when(pl.program_id(0) == 0)
            def _():
                for h in range(2):
                    wb[h] = w_ref[h * half:(h + 1) * half, :].astype(BF16)
            if host and not host["prefilled"]:
                host_in = [wb]
        else:
            sems = refs
        _host_phases(host, host_in, host_out, sems, seq // tm, before=True)

        xv = x_ref[...]
        r = lax.rsqrt(jnp.mean(xv * xv, axis=-1, keepdims=True) + RMS_EPS)
        u = xv * r * g_ref[...]
        ub = u.astype(BF16)
        own_out[0][...] = ub
        own_out[1][...] = u.T.astype(BF16)
        if w_own is not None:
            own_out[2][...] = _dot(ub[:, :half], wb[0]) + _dot(ub[:, half:], wb[1])
        _host_phases(host, host_in, host_out, sems, seq // tm, before=False)

    any_spec = pl.BlockSpec(memory_space=pl.ANY)
    ins = [x, pre_norm]
    in_specs = [pl.BlockSpec((tm, D_MODEL), lambda i: (i, 0)), pl.BlockSpec(pre_norm.shape, lambda i: (0, 0))]
    out_shape = [jax.ShapeDtypeStruct((seq, D_MODEL), BF16), jax.ShapeDtypeStruct((seq // tm, D_MODEL, tm), BF16)]
    out_specs = [pl.BlockSpec((tm, D_MODEL), lambda i: (i, 0)), pl.BlockSpec((None, D_MODEL, tm), lambda i: (i, 0, 0))]
    aliases, scratch = {}, []
    if w_own is not None:
        ins.append(w_own)
        in_specs.append(pl.BlockSpec(w_own.shape, lambda i: (0, 0)))
        out_shape.append(jax.ShapeDtypeStruct((seq, w_own.shape[1]), F32))
        out_specs.append(pl.BlockSpec((tm, w_own.shape[1]), lambda i: (i, 0)))
        scratch.append(pltpu.VMEM((2, half, w_own.shape[1]), BF16))
    if host:
        aliases = {len(ins) + k: n_own_out + v for k, v in host.get("aliases", {}).items()}
        ins += list(host["ins"])
        in_specs += [any_spec] * n_host_in
        out_shape += list(host["outs"])
        out_specs += [any_spec] * n_host_out
        scratch += list(host["sems"])
    res = pl.pallas_call(
        body, name="pre_norm", grid=(seq // tm,), in_specs=in_specs, out_specs=out_specs, out_shape=out_shape,
        input_output_aliases=aliases, scratch_shapes=scratch,
        compiler_params=pltpu.CompilerParams(dimension_semantics=("arbitrary",)),
    )(*ins)
    return res[0], res[1], (None if w_own is None else res[2]), res[n_own_out:]


def _pre_proj(u, w_in_g, own=None, host=None):
    seq = u.shape[0]
    tm = min(ROW_TILE, seq)
    n_nat, n_dil = len(_NATURAL), len(_DILATED) * len(B_DILS)
    rope = _rope_tables(seq, tm)

    n_host_in = len(host["ins"]) if host else 0
    n_host_out = len(host["outs"]) if host else 0
    n_own_out = n_nat + n_dil

    def body(u_ref, w_ref, rl_ref, rb_ref, *refs):
        if own:
            (chip_ref, pown_ref), refs = refs[:2], refs[2:]
        host_in, refs = refs[:n_host_in], refs[n_host_in:]
        nat = dict(zip(_NATURAL, refs[:n_nat]))
        res = {n: refs[n_nat + len(B_DILS) * k:n_nat + len(B_DILS) * (k + 1)] for k, n in enumerate(_DILATED)}
        host_out = refs[n_own_out:n_own_out + n_host_out]
        bufs = dict(zip(_DILATED, refs[n_own_out + n_host_out:]))
        sems = refs[n_own_out + n_host_out + len(_DILATED):]
        _host_phases(host, host_in, host_out, sems, seq // tm, before=True)

        def project(own_chip):
            ub = u_ref[...]
            c, sm, sp = _rope_coeffs(rl_ref, rb_ref)
            for j in range(N_CHIPS):
                pj = pown_ref[...] if j == own_chip else _dot(ub, w_ref[j])
                for b in range(SHARD_IN // LANES):
                    name, off, roped, scaled = _PROJ_LAYOUT[(SHARD_IN // LANES) * j + b]
                    piece = pj[:, LANES * b:LANES * (b + 1)]
                    if roped:
                        piece = _rope(piece, c, sm, sp)
                    if scaled:
                        piece = piece * SCALE
                    if name in bufs:
                        bufs[name][off // LANES] = piece
                    else:
                        nat[name][:, off:off + LANES] = piece.astype(BF16)
            for name in _DILATED:
                for ref, dil in zip(res[name], B_DILS):
                    _to_residues(bufs[name], ref, dil)

        if own:
            for chip in range(N_CHIPS):
                pl.when(chip_ref[0] == chip)(lambda chip=chip: project(chip))
        else:
            project(None)
        _host_phases(host, host_in, host_out, sems, seq // tm, before=False)

    row = lambda w: pl.BlockSpec((tm, w), lambda i: (i, 0))
    full = lambda a: pl.BlockSpec(a.shape, lambda i: (0,) * a.ndim)
    any_spec = pl.BlockSpec(memory_space=pl.ANY)
    out_shape = [jax.ShapeDtypeStruct((seq, _PROJ_WIDTH[n]), BF16) for n in _NATURAL]
    out_specs = [row(_PROJ_WIDTH[n]) for n in _NATURAL]
    for n in _DILATED:
        for dil in B_DILS:
            out_shape.append(jax.ShapeDtypeStruct((dil, seq // dil, B_W), BF16))
            out_specs.append(_residue_spec(dil, tm, B_W))
    ins = [u, w_in_g, *rope]
    in_specs = [row(D_MODEL), full(w_in_g), full(rope[0]), pl.BlockSpec((8, 2 * LANES), lambda i: (i, 0))]
    if own:
        ins += list(own)
        in_specs += [pl.BlockSpec(memory_space=pltpu.SMEM), row(SHARD_IN)]
    scratch = [_stage(tm, B_W)] * len(_DILATED)
    aliases = {}
    if host:
        aliases = {len(ins) + k: n_own_out + v for k, v in host.get("aliases", {}).items()}
        ins += list(host["ins"])
        in_specs += [any_spec] * n_host_in
        out_shape += list(host["outs"])
        out_specs += [any_spec] * n_host_out
        scratch += list(host["sems"])
    res = pl.pallas_call(
        body, name="pre_proj", grid=(seq // tm,), in_specs=in_specs, out_specs=out_specs, out_shape=out_shape,
        input_output_aliases=aliases, scratch_shapes=scratch,
        compiler_params=pltpu.CompilerParams(dimension_semantics=("arbitrary",)),
    )(*ins)
    out = dict(zip(_NATURAL, res[:n_nat]))
    for k, n in enumerate(_DILATED):
        out[n] = res[n_nat + len(B_DILS) * k:n_nat + len(B_DILS) * (k + 1)]
    out["hosted"] = res[n_own_out:]
    return out


def _band_bias(max_dist, transposed):
    i = np.arange(BLOCK)[:, None]
    j = np.arange(BLOCK)[None, :]
    if transposed:
        same = i <= j
        other = (j + BLOCK - i) <= max_dist
        vis = np.concatenate([same, other], axis=1)
    else:
        prev = (i + BLOCK - j) <= max_dist
        same = j <= i
        vis = np.concatenate([prev, same], axis=1)
    return jnp.asarray(np.where(vis, 0.0, NEG).astype(np.float32))


def _kv_place(h, gqa):
    return (0, h // 3) if gqa else (h // 2, h % 2)


def _band_fwd(q, k, v, sink, *, max_dist, name):
    dil, length, wq = q.shape
    wk = k.shape[2]
    gqa = wk != wq
    tq = min(ATTN_TILE, length)
    ns, nt = tq // BLOCK, length // tq
    npair = wq // LANES
    bias = _band_bias(max_dist, transposed=False)
    has_sink = sink is not None

    def body(*refs):
        if has_sink:
            sink_ref, refs = refs[0], refs[1:]
        q_ref, k_ref, kp_ref, v_ref, vp_ref, bias_ref, o_ref, lse_ref, kbuf, vbuf = refs[:10]
        i = pl.program_id(1)
        kbuf[0:BLOCK] = kp_ref[...]
        kbuf[BLOCK:] = k_ref[...]
        vbuf[0:BLOCK] = vp_ref[...]
        vbuf[BLOCK:] = v_ref[...]
        if gqa:
            kroll, vroll = refs[10:12]
            kroll[...] = pltpu.roll(kbuf[...], HEAD_DIM, 1)
            vroll[...] = pltpu.roll(vbuf[...], HEAD_DIM, 1)
        half = _half_masks(BLOCK)
        col_prev = (lax.broadcasted_iota(jnp.int32, (1, 2 * BLOCK), 1) < BLOCK).astype(F32)

        def score_matmuls(a):
            scores = []
            for p in range(npair):
                qp = q_ref[a * BLOCK:(a + 1) * BLOCK, p * LANES:(p + 1) * LANES]
                for e in range(2):
                    pk, ek = _kv_place(2 * p + e, gqa)
                    kw = (kbuf if ek == e else kroll)[a * BLOCK:(a + 2) * BLOCK, pk * LANES:(pk + 1) * LANES]
                    scores.append(_dot_nt(jnp.where(half[e], qp, jnp.zeros_like(qp)), kw))
            return scores

        pending = score_matmuls(0)
        for a in range(ns):
            r0 = a * BLOCK
            b = bias_ref[...]
            if a == 0:
                b = b + jnp.where(i == 0, NEG, 0.0) * col_prev
            scores = pending
            m_cols, l_cols, probs = [], [], []
            for h, s in enumerate(scores):
                s = s + b
                m = jnp.max(s, axis=1, keepdims=True)
                if has_sink:
                    m = jnp.maximum(m, sink_ref[h])
                pe = jnp.exp(s - m)
                l = jnp.sum(pe, axis=1, keepdims=True)
                if has_sink:
                    l = l + jnp.exp(sink_ref[h] - m)
                probs.append(pe.astype(BF16))
                m_cols.append(m)
                l_cols.append(l)
            pending = score_matmuls(a + 1) if a + 1 < ns else None
            for p in range(npair):
                o_h = []
                for e in range(2):
                    h = 2 * p + e
                    pk, ek = _kv_place(h, gqa)
                    vw = (vbuf if ek == e else vroll)[r0:r0 + 2 * BLOCK, pk * LANES:(pk + 1) * LANES]
                    o_h.append(_dot(probs[h], vw) * (1.0 / l_cols[h]))
                o_ref[r0:r0 + BLOCK, p * LANES:(p + 1) * LANES] = jnp.where(half[0], o_h[0], o_h[1]).astype(BF16)
            lse_ref[r0:r0 + BLOCK, :] = _per_head(m_cols) + jnp.log(_per_head(l_cols, 1.0))

    main = lambda w: pl.BlockSpec((None, tq, w), lambda r, i: (r, i, 0))
    prev = lambda w: pl.BlockSpec((None, BLOCK, w), lambda r, i: (r, jnp.maximum(i * ns - 1, 0), 0))
    in_specs = [main(wq), main(wk), prev(wk), main(wk), prev(wk), pl.BlockSpec(bias.shape, lambda r, i: (0, 0))]
    args = [q, k, k, v, v, bias]
    if has_sink:
        in_specs = [pl.BlockSpec(memory_space=pltpu.SMEM)] + in_specs
        args = [sink] + args
    scratch = [pltpu.VMEM((tq + BLOCK, wk), BF16)] * (4 if gqa else 2)
    return pl.pallas_call(
        body, name=name, grid=(dil, nt), in_specs=in_specs,
        out_specs=[main(wq), main(LANES)],
        out_shape=[jax.ShapeDtypeStruct((dil, length, wq), BF16), jax.ShapeDtypeStruct((dil, length, LANES), F32)],
        scratch_shapes=scratch,
    )(*args)


def _band_bwd(q, k, v, do, lse, delta, *, max_dist, name):
    dil, length, wq = q.shape
    wk = k.shape[2]
    gqa = wk != wq
    tq = min(ATTN_TILE, length)
    ns, nt = tq // BLOCK, length // tq
    npair = wq // LANES
    nblocks = length // BLOCK
    bias = _band_bias(max_dist, transposed=True)

    def body(q_ref, qn_ref, do_ref, don_ref, lse_ref, lsen_ref, dl_ref, dln_ref, k_ref, v_ref, bias_ref,
             dq_ref, dk_ref, dv_ref, stat_l, stat_d, dqt, kt, *rolled):
        i = pl.program_id(1)
        for pk in range(wk // LANES):
            kt[pk] = k_ref[:, pk * LANES:(pk + 1) * LANES].astype(F32).T.astype(BF16)
        if gqa:
            kroll, vroll, ktroll = rolled
            kroll[...] = pltpu.roll(k_ref[...], HEAD_DIM, 1)
            vroll[...] = pltpu.roll(v_ref[...], HEAD_DIM, 1)
            ktroll[0] = kroll[...].astype(F32).T.astype(BF16)
        for a in range(ns):
            rows = slice(a * BLOCK, (a + 1) * BLOCK)
            stat_l[a] = _rows_to_lanes(lse_ref[rows, :])
            stat_d[a] = _rows_to_lanes(dl_ref[rows, :])
        stat_l[ns] = _rows_to_lanes(lsen_ref[...])
        stat_d[ns] = _rows_to_lanes(dln_ref[...])

        @pl.when(i == 0)
        def _():
            dqt[:, :, 0:BLOCK] = jnp.zeros((npair, LANES, BLOCK), F32)

        @pl.when(i > 0)
        def _():
            dqt[:, :, 0:BLOCK] = dqt[:, :, tq:tq + BLOCK]

        dqt[:, :, BLOCK:] = jnp.zeros((npair, LANES, tq), F32)
        half2 = _half_masks(2 * BLOCK)
        row = lax.broadcasted_iota(jnp.int32, (LANES, BLOCK), 0)
        row_half = (row < HEAD_DIM, row >= HEAD_DIM)
        col_next = (lax.broadcasted_iota(jnp.int32, (1, 2 * BLOCK), 1) >= BLOCK).astype(F32)

        def scores(b):
            rows = slice(b * BLOCK, (b + 1) * BLOCK)
            nxt_rows = slice((b + 1) * BLOCK, (b + 2) * BLOCK)
            items = []
            for p in range(npair):
                lanes = slice(p * LANES, (p + 1) * LANES)
                q_next = q_ref[nxt_rows, lanes] if b + 1 < ns else qn_ref[:, lanes]
                do_next = do_ref[nxt_rows, lanes] if b + 1 < ns else don_ref[:, lanes]
                qw = jnp.concatenate([q_ref[rows, lanes], q_next], axis=0)
                dow = jnp.concatenate([do_ref[rows, lanes], do_next], axis=0)
                for e in range(2):
                    h = 2 * p + e
                    pk, ek = _kv_place(h, gqa)
                    klanes = slice(pk * LANES, (pk + 1) * LANES)
                    kb = (k_ref if ek == e else kroll)[rows, klanes]
                    vb = (v_ref if ek == e else vroll)[rows, klanes]
                    qm = jnp.where(half2[e], qw, jnp.zeros_like(qw))
                    dom = jnp.where(half2[e], dow, jnp.zeros_like(dow))
                    items.append(dict(p=p, e=e, h=h, pk=pk, ek=ek, qm=qm, dom=dom,
                                      st=_dot_nt(kb, qm), dpt=_dot_nt(vb, dom)))
            return items

        def probs(b, items):
            bt = bias_ref[...]
            if b == ns - 1:
                bt = bt + jnp.where(i == nt - 1, NEG, 0.0) * col_next
            for it in items:
                h = it["h"]
                lrow = jnp.concatenate([stat_l[b, h:h + 1, :], stat_l[b + 1, h:h + 1, :]], axis=1)
                drow = jnp.concatenate([stat_d[b, h:h + 1, :], stat_d[b + 1, h:h + 1, :]], axis=1)
                pt = jnp.exp(it["st"] + bt - lrow)
                it["ptb"] = pt.astype(BF16)
                it["dsb"] = (pt * (it["dpt"] - drow)).astype(BF16)

        pending = scores(0)
        for b in range(ns):
            rows = slice(b * BLOCK, (b + 1) * BLOCK)
            window = slice(b * BLOCK, (b + 2) * BLOCK)
            acc = {}
            items = pending
            probs(b, items)
            pending = scores(b + 1) if b + 1 < ns else None
            for p in range(npair):
                pair = items[2 * p:2 * p + 2]
                lanes = slice(p * LANES, (p + 1) * LANES)
                kparts = []
                for it in pair:
                    kbt = (kt if it["ek"] == it["e"] else ktroll)[it["pk"], :, rows]
                    kparts.append(jnp.where(row_half[it["e"]], kbt, jnp.zeros_like(kbt)))
                ds_keys = jnp.concatenate([it["dsb"] for it in pair], axis=0)
                dqt[p, :, window] += _dot(jnp.concatenate(kparts, axis=1), ds_keys)
                if not gqa:
                    q_both = jnp.concatenate([it["qm"] for it in pair], axis=0)
                    do_both = jnp.concatenate([it["dom"] for it in pair], axis=0)
                    dk_ref[rows, lanes] = _dot(jnp.concatenate([it["dsb"] for it in pair], axis=1), q_both).astype(BF16)
                    dv_ref[rows, lanes] = _dot(jnp.concatenate([it["ptb"] for it in pair], axis=1), do_both).astype(BF16)
                else:
                    for it in pair:
                        dv_c = _dot(it["ptb"], it["dom"])
                        dk_c = _dot(it["dsb"], it["qm"])
                        key = (it["pk"], it["ek"] == it["e"])
                        if key in acc:
                            acc[key] = (acc[key][0] + dk_c, acc[key][1] + dv_c)
                        else:
                            acc[key] = (dk_c, dv_c)
            if gqa:
                dk_al, dv_al = acc[(0, True)]
                dk_mis, dv_mis = acc[(0, False)]
                dk_ref[rows, :] = (dk_al + pltpu.roll(dk_mis, HEAD_DIM, 1)).astype(BF16)
                dv_ref[rows, :] = (dv_al + pltpu.roll(dv_mis, HEAD_DIM, 1)).astype(BF16)

        for p in range(npair):
            dq_ref[:, p * LANES:(p + 1) * LANES] = dqt[p, :, 0:tq].T.astype(BF16)

    main = lambda w: pl.BlockSpec((None, tq, w), lambda r, i: (r, i, 0))
    nxt = lambda w: pl.BlockSpec((None, BLOCK, w), lambda r, i: (r, jnp.minimum((i + 1) * ns, nblocks - 1), 0))
    scratch = [pltpu.VMEM((ns + 1, 8, LANES), F32), pltpu.VMEM((ns + 1, 8, LANES), F32),
               pltpu.VMEM((npair, LANES, tq + BLOCK), F32), pltpu.VMEM((wk // LANES, LANES, tq), BF16)]
    if gqa:
        scratch = scratch + [pltpu.VMEM((tq, wk), BF16)] * 2 + [pltpu.VMEM((1, LANES, tq), BF16)]
    return pl.pallas_call(
        body, name=name, grid=(dil, nt),
        in_specs=[main(wq), nxt(wq), main(wq), nxt(wq), main(LANES), nxt(LANES), main(LANES), nxt(LANES),
                  main(wk), main(wk), pl.BlockSpec(bias.shape, lambda r, i: (0, 0))],
        out_specs=[main(wq), main(wk), main(wk)],
        out_shape=[jax.ShapeDtypeStruct((dil, length, wq), BF16), jax.ShapeDtypeStruct((dil, length, wk), BF16),
                   jax.ShapeDtypeStruct((dil, length, wk), BF16)],
        scratch_shapes=scratch,
        compiler_params=pltpu.CompilerParams(dimension_semantics=("arbitrary", "arbitrary")),
    )(q, q, do, do, lse, lse, delta, delta, k, v, bias)


def _mem_attn_fwd(q, mk, mv):
    seq = q.shape[0]
    tq = min(ATTN_TILE, seq)
    sub_rows = min(4 * BLOCK, tq)
    ns = tq // sub_rows

    def body(q_ref, mk_ref, mv_ref, o_ref, lse_ref):
        half = _half_masks(sub_rows)

        def sub(a, carry):
            r0 = pl.multiple_of(a * sub_rows, sub_rows)
            scores = []
            for p in range(C_W // LANES):
                lanes = slice(p * LANES, (p + 1) * LANES)
                qp = q_ref[pl.ds(r0, sub_rows), lanes]
                for e in range(2):
                    scores.append(_dot_nt(jnp.where(half[e], qp, jnp.zeros_like(qp)), mk_ref[:, lanes]))
            m_cols, l_cols, probs = [], [], []
            for s in scores:
                m = jnp.max(s, axis=1, keepdims=True)
                pe = jnp.exp(s - m)
                probs.append(pe.astype(BF16))
                m_cols.append(m)
                l_cols.append(jnp.sum(pe, axis=1, keepdims=True))
            for p in range(C_W // LANES):
                lanes = slice(p * LANES, (p + 1) * LANES)
                o_h = [_dot(probs[2 * p + e], mv_ref[:, lanes]) * (1.0 / l_cols[2 * p + e]) for e in range(2)]
                o_ref[pl.ds(r0, sub_rows), lanes] = jnp.where(half[0], o_h[0], o_h[1]).astype(BF16)
            lse_ref[pl.ds(r0, sub_rows), :] = _per_head(m_cols) + jnp.log(_per_head(l_cols, 1.0))
            return carry

        lax.fori_loop(0, ns, sub, 0, unroll=True)

    row = lambda w: pl.BlockSpec((tq, w), lambda i: (i, 0))
    full = pl.BlockSpec((N_MEM, C_W), lambda i: (0, 0))
    return pl.pallas_call(
        body, name="mem_attn_fwd", grid=(seq // tq,), in_specs=[row(C_W), full, full],
        out_specs=[row(C_W), row(LANES)],
        out_shape=[jax.ShapeDtypeStruct((seq, C_W), BF16), jax.ShapeDtypeStruct((seq, LANES), F32)],
    )(q, mk, mv)


def _mem_attn_bwd(q, mk, mv, do, lse, delta):
    seq = q.shape[0]
    tq = min(ATTN_TILE, seq)
    ns = tq // BLOCK
    npair = C_W // LANES

    def body(q_ref, mk_ref, mv_ref, do_ref, lse_ref, dl_ref, dq_ref, dmk_ref, dmv_ref, stat_l, stat_d, mkt, dqt):
        @pl.when(pl.program_id(0) == 0)
        def _():
            dmk_ref[...] = jnp.zeros_like(dmk_ref)
            dmv_ref[...] = jnp.zeros_like(dmv_ref)
            for p in range(npair):
                mkt[p] = mk_ref[:, p * LANES:(p + 1) * LANES].astype(F32).T.astype(BF16)

        for a in range(ns):
            rows = slice(a * BLOCK, (a + 1) * BLOCK)
            stat_l[a] = _rows_to_lanes(lse_ref[rows, :])
            stat_d[a] = _rows_to_lanes(dl_ref[rows, :])
        span = min(2, ns)
        half = _half_masks(span * BLOCK)
        row = lax.broadcasted_iota(jnp.int32, (LANES, N_MEM), 0)
        row_half = (row < HEAD_DIM, row >= HEAD_DIM)

        for a in range(0, ns, span):
            rows = slice(a * BLOCK, (a + span) * BLOCK)
            items = []
            for p in range(npair):
                lanes = slice(p * LANES, (p + 1) * LANES)
                qp = q_ref[rows, lanes]
                dop = do_ref[rows, lanes]
                for e in range(2):
                    qm = jnp.where(half[e], qp, jnp.zeros_like(qp))
                    dom = jnp.where(half[e], dop, jnp.zeros_like(dop))
                    items.append(dict(p=p, e=e, qm=qm, dom=dom, st=_dot_nt(mk_ref[:, lanes], qm),
                                      dpt=_dot_nt(mv_ref[:, lanes], dom)))
            for it in items:
                h = 2 * it["p"] + it["e"]
                lrow = jnp.concatenate([stat_l[a + k, h:h + 1, :] for k in range(span)], axis=1)
                drow = jnp.concatenate([stat_d[a + k, h:h + 1, :] for k in range(span)], axis=1)
                pt = jnp.exp(it["st"] - lrow)
                it["ptb"] = pt.astype(BF16)
                it["dsb"] = (pt * (it["dpt"] - drow)).astype(BF16)
            for p in range(npair):
                lanes = slice(p * LANES, (p + 1) * LANES)
                pair = [it for it in items if it["p"] == p]
                join = lambda name, axis: jnp.concatenate([it[name] for it in pair], axis=axis)
                dmv_ref[:, lanes] += _dot(join("ptb", 1), join("dom", 0))
                dmk_ref[:, lanes] += _dot(join("dsb", 1), join("qm", 0))
                kbt = mkt[p]
                k_both = jnp.concatenate([jnp.where(row_half[e], kbt, jnp.zeros_like(kbt)) for e in range(2)], axis=1)
                dqt[p, :, rows] = _dot(k_both, join("dsb", 0))
        for p in range(npair):
            dq_ref[:, p * LANES:(p + 1) * LANES] = dqt[p].T.astype(BF16)

    row = lambda w: pl.BlockSpec((tq, w), lambda i: (i, 0))
    full = pl.BlockSpec((N_MEM, C_W), lambda i: (0, 0))
    return pl.pallas_call(
        body, name="mem_attn_bwd", grid=(seq // tq,),
        in_specs=[row(C_W), full, full, row(C_W), row(LANES), row(LANES)], out_specs=[row(C_W), full, full],
        out_shape=[jax.ShapeDtypeStruct((seq, C_W), BF16), jax.ShapeDtypeStruct((N_MEM, C_W), F32),
                   jax.ShapeDtypeStruct((N_MEM, C_W), F32)],
        scratch_shapes=[pltpu.VMEM((ns, 8, LANES), F32)] * 2
        + [pltpu.VMEM((npair, LANES, N_MEM), BF16), pltpu.VMEM((npair, LANES, tq), F32)],
        compiler_params=pltpu.CompilerParams(dimension_semantics=("arbitrary",)),
    )(q, mk, mv, do, lse, delta)


def _silu_and_grad(g):
    s = 1.0 / (1.0 + jnp.exp(-g))
    return g * s, s * (1.0 + g * (1.0 - s))


def _post(x, target, post_norm, w_out, sink_row, oa, lse_a, ga, ob_list, lseb_list, gb, oc, gc):
    seq = x.shape[0]
    tm = min(ROW_TILE, seq)
    inv_d = 1.0 / D_MODEL
    nd = len(B_DILS)

    def body(*refs):
        (x_ref, t_ref, gp_ref, w_ref, sink_ref, oa_ref, lsea_ref, ga_ref), refs = refs[:8], refs[8:]
        ob_refs, lb_refs, (gb_ref, oc_ref, gc_ref), refs = refs[:nd], refs[nd:2 * nd], refs[2 * nd:2 * nd + 3], refs[2 * nd + 3:]
        (g_ref, doa_ref, dla_ref, dga_ref), refs = refs[:4], refs[4:]
        dob_refs, lsec_refs, dlb_refs, refs = refs[:nd], refs[nd:2 * nd], refs[2 * nd:3 * nd], refs[3 * nd:]
        (dgb_ref, doc_ref, dlc_ref, dgc_ref, gw_ref, gpost_ref, gsink_ref, loss_ref), refs = refs[:8], refs[8:]
        ycat, obufs, lbufs, st_do, st_l, st_d = refs[0], refs[1:nd], refs[nd:2 * nd - 1], refs[2 * nd - 1], refs[2 * nd], refs[2 * nd + 1]

        @pl.when(pl.program_id(0) == 0)
        def _():
            gw_ref[...] = jnp.zeros_like(gw_ref)
            gpost_ref[...] = jnp.zeros_like(gpost_ref)
            gsink_ref[...] = jnp.zeros_like(gsink_ref)
            loss_ref[...] = jnp.zeros_like(loss_ref)

        o_i, l_i = [ob_refs[0][0].astype(F32)], [lb_refs[0][0]]
        for k in range(1, nd):
            _from_residues(ob_refs[k], obufs[k - 1], B_DILS[k])
            _from_residues(lb_refs[k], lbufs[k - 1], B_DILS[k])
            o_i.append(_stage_read(obufs[k - 1]))
            l_i.append(_stage_read(lbufs[k - 1]))
        mx = l_i[0]
        for l in l_i[1:]:
            mx = jnp.maximum(mx, l)
        w_i = [jnp.exp(l - mx) for l in l_i]
        z = w_i[0]
        for w in w_i[1:]:
            z = z + w
        _stage_write(st_l, mx + jnp.log(z))
        expand = _head_expand_matrix(B_W)
        inv_z = 1.0 / z
        ob = None
        for w, o in zip(w_i, o_i):
            term = _dot_split(w * inv_z, expand, 2) * o
            ob = term if ob is None else ob + term
        oa, oc = oa_ref[...].astype(F32), oc_ref[...].astype(F32)
        sa, dsa = _silu_and_grad(ga_ref[...].astype(F32))
        sb, dsb = _silu_and_grad(gb_ref[...].astype(F32))
        sc, dsc = _silu_and_grad(gc_ref[...].astype(F32))
        ycat[:, 0:A_W] = (oa * sa).astype(BF16)
        ycat[:, A_W:A_W + B_W] = (ob * sb).astype(BF16)
        ycat[:, A_W + B_W:] = (oc * sc).astype(BF16)
        y2 = _dot(ycat[...], w_ref[...])
        r = lax.rsqrt(jnp.mean(y2 * y2, axis=-1, keepdims=True) + RMS_EPS)
        zhat = y2 * r
        gp = gp_ref[...]
        err = x_ref[...] + zhat * gp - t_ref[...]
        loss_ref[...] += jnp.sum(err * err) * (0.5 * inv_d)
        g = err * inv_d
        g_ref[...] = g
        gpost_ref[...] += jnp.sum(g * zhat, axis=0, keepdims=True)
        a = g * gp
        dy2 = (r * (a - zhat * jnp.mean(a * zhat, axis=-1, keepdims=True))).astype(BF16)
        gw_ref[...] += _dot_tn(ycat[...], dy2)
        dycat = _dot_nt(dy2, w_ref[...])
        dya, dyb, dyc = dycat[:, 0:A_W], dycat[:, A_W:A_W + B_W], dycat[:, A_W + B_W:]
        doa, dob, doc = dya * sa, dyb * sb, dyc * sc
        doa_ref[...] = doa.astype(BF16)
        doc_ref[...] = doc.astype(BF16)
        dga_ref[...] = (dya * oa * dsa).astype(BF16)
        dgb_ref[...] = (dyb * ob * dsb).astype(BF16)
        dgc_ref[...] = (dyc * oc * dsc).astype(BF16)
        dl_a = _dot_split(doa * oa, _head_sum_matrix(A_W), 2)
        dla_ref[...] = dl_a
        dlc_ref[...] = _dot_split(doc * oc, _head_sum_matrix(C_W), 2)
        gsink_ref[...] += jnp.sum(jnp.exp(sink_ref[...] - lsea_ref[...]) * dl_a, axis=0, keepdims=True)
        _stage_write(st_do, dob)
        _stage_write(st_d, _dot_split(dob * ob, _head_sum_matrix(B_W), 2))
        for k, dil in enumerate(B_DILS):
            _to_residues(st_do, dob_refs[k], dil)
            _to_residues(st_l, lsec_refs[k], dil)
            _to_residues(st_d, dlb_refs[k], dil)

    row = lambda w: pl.BlockSpec((tm, w), lambda i: (i, 0))
    full = lambda shape: pl.BlockSpec(shape, lambda i: (0,) * len(shape))
    res_specs = lambda w: [_residue_spec(d, tm, w) for d in B_DILS]
    res_shapes = lambda w, dt: [jax.ShapeDtypeStruct((d, seq // d, w), dt) for d in B_DILS]
    ins = [x, target, post_norm, w_out, sink_row, oa, lse_a, ga, *ob_list, *lseb_list, gb, oc, gc]
    in_specs = ([row(D_MODEL), row(D_MODEL), full((1, D_MODEL)), full((D_MODEL, D_MODEL)), full((1, LANES)),
                 row(A_W), row(LANES), row(A_W)] + res_specs(B_W) + res_specs(LANES) + [row(B_W), row(C_W), row(C_W)])
    out_shape = ([jax.ShapeDtypeStruct((seq, D_MODEL), F32), jax.ShapeDtypeStruct((seq, A_W), BF16),
                  jax.ShapeDtypeStruct((seq, LANES), F32), jax.ShapeDtypeStruct((seq, A_W), BF16)]
                 + res_shapes(B_W, BF16) + res_shapes(LANES, F32) + res_shapes(LANES, F32)
                 + [jax.ShapeDtypeStruct((seq, B_W), BF16), jax.ShapeDtypeStruct((seq, C_W), BF16),
                    jax.ShapeDtypeStruct((seq, LANES), F32), jax.ShapeDtypeStruct((seq, C_W), BF16),
                    jax.ShapeDtypeStruct((D_MODEL, D_MODEL), F32), jax.ShapeDtypeStruct((1, D_MODEL), F32),
                    jax.ShapeDtypeStruct((1, LANES), F32), jax.ShapeDtypeStruct((1, LANES), F32)])
    out_specs = ([row(D_MODEL), row(A_W), row(LANES), row(A_W)] + res_specs(B_W) + res_specs(LANES) + res_specs(LANES)
                 + [row(B_W), row(C_W), row(LANES), row(C_W),
                    full((D_MODEL, D_MODEL)), full((1, D_MODEL)), full((1, LANES)), full((1, LANES))])
    scratch = ([pltpu.VMEM((tm, D_MODEL), BF16)] + [_stage(tm, B_W)] * (nd - 1) + [_stage(tm, LANES)] * (nd - 1)
               + [_stage(tm, B_W), _stage(tm, LANES), _stage(tm, LANES)])
    res = pl.pallas_call(
        body, name="post", grid=(seq // tm,), in_specs=in_specs, out_specs=out_specs, out_shape=out_shape,
        scratch_shapes=scratch,
        compiler_params=pltpu.CompilerParams(dimension_semantics=("arbitrary",)),
    )(*ins)
    out = dict(g=res[0], doa=res[1], dl_a=res[2], dga=res[3], dob=res[4:4 + nd], lse_b=res[4 + nd:4 + 2 * nd],
               dl_b=res[4 + 2 * nd:4 + 3 * nd])
    rest = res[4 + 3 * nd:]
    out.update(dgb=rest[0], doc=rest[1], dl_c=rest[2], dgc=rest[3], gw_out=rest[4], gpost=rest[5], gsink=rest[6],
               loss=rest[7])
    return out


def _grad_w_in(ut, nat, res):
    tm = ut.shape[2]
    seq = ut.shape[0] * tm
    nd = len(B_DILS)
    nat_list = [nat[n] for n in _NATURAL]
    res_list = [a for n in _DILATED for a in res[n]]
    rope = _rope_tables(seq, tm)

    def body(rl_ref, rb_ref, ut_ref, *refs):
        nat_refs = dict(zip(_NATURAL, refs[:len(_NATURAL)]))
        refs = refs[len(_NATURAL):]
        res_refs = {n: refs[nd * k:nd * (k + 1)] for k, n in enumerate(_DILATED)}
        refs = refs[nd * len(_DILATED):]
        dproj_ref, gw_ref = refs[:2]
        bufs = {n: refs[2 + (nd - 1) * k:2 + (nd - 1) * (k + 1)] for k, n in enumerate(_DILATED)}

        @pl.when(pl.program_id(0) == 0)
        def _():
            gw_ref[...] = jnp.zeros_like(gw_ref)

        c, sm, sp = _rope_coeffs(rl_ref, rb_ref)
        sm, sp = -sm, -sp
        per_shard = SHARD_IN // LANES
        interleaved = False
        for j in (3, 0, 1, 2):
            for blk in range(per_shard * j, per_shard * (j + 1)):
                name, off, roped, scaled = _PROJ_LAYOUT[blk]
                lanes = slice(off, off + LANES)
                if name in nat_refs:
                    piece = nat_refs[name][:, lanes].astype(F32)
                else:
                    if not interleaved:
                        for n in _DILATED:
                            for k in range(1, nd):
                                _from_residues(res_refs[n][k], bufs[n][k - 1], B_DILS[k])
                        interleaved = True
                    piece = res_refs[name][0][0, :, lanes].astype(F32)
                    for buf in bufs[name]:
                        piece = piece + buf[off // LANES]
                if roped:
                    piece = _rope(piece, c, sm, sp)
                if scaled:
                    piece = piece * SCALE
                dproj_ref[:, blk * LANES:(blk + 1) * LANES] = piece.astype(BF16)
            gw_ref[j] += _dot(ut_ref[...], dproj_ref[:, j * SHARD_IN:(j + 1) * SHARD_IN])

    row = lambda w: pl.BlockSpec((tm, w), lambda i: (i, 0))
    in_specs = ([pl.BlockSpec(rope[0].shape, lambda i: (0, 0)), pl.BlockSpec((8, 2 * LANES), lambda i: (i, 0)),
                 pl.BlockSpec((None, D_MODEL, tm), lambda i: (i, 0, 0))]
                + [row(a.shape[1]) for a in nat_list]
                + [_residue_spec(d, tm, B_W) for _ in _DILATED for d in B_DILS])
    return pl.pallas_call(
        body, name="grad_w_in", grid=(seq // tm,), in_specs=in_specs,
        out_specs=[row(D_IN), pl.BlockSpec((N_CHIPS, D_MODEL, SHARD_IN), lambda i: (0, 0, 0))],
        out_shape=[jax.ShapeDtypeStruct((seq, D_IN), BF16), jax.ShapeDtypeStruct((N_CHIPS, D_MODEL, SHARD_IN), F32)],
        scratch_shapes=[_stage(tm, B_W)] * ((nd - 1) * len(_DILATED)),
        compiler_params=pltpu.CompilerParams(dimension_semantics=("arbitrary",)),
    )(*rope, ut, *nat_list, *res_list)


def _input_grad(x, g, pre_norm, w_in_g, dproj, gx_prev, span, after, name):
    seq = x.shape[0]
    tm = seq // INPUT_GRAD_TILES
    first_block, steps = span

    def body(*refs):
        x_ref, g_ref, gp_ref, w_ref, dp_ref = refs[:5]
        gx_ref, gpre_ref = refs[-2:]

        @pl.when(pl.program_id(0) == 0)
        def _():
            gpre_ref[...] = jnp.zeros_like(gpre_ref)

        du = None
        for j in range(N_CHIPS):
            term = _dot_nt(dp_ref[:, j * SHARD_IN:(j + 1) * SHARD_IN], w_ref[j])
            du = term if du is None else du + term
        xv = x_ref[...]
        r = lax.rsqrt(jnp.mean(xv * xv, axis=-1, keepdims=True) + RMS_EPS)
        xhat = xv * r
        gpre_ref[...] += jnp.sum(du * xhat, axis=0, keepdims=True)
        a = du * gp_ref[...]
        gx_ref[...] = g_ref[...] + r * (a - xhat * jnp.mean(a * xhat, axis=-1, keepdims=True))

    row = lambda w: pl.BlockSpec((tm, w), lambda i: (first_block + i, 0))
    full = lambda a: pl.BlockSpec(a.shape, lambda i: (0,) * a.ndim)
    any_spec = pl.BlockSpec(memory_space=pl.ANY)
    ins = [x, g, pre_norm, w_in_g, dproj]
    in_specs = [row(D_MODEL), row(D_MODEL), full(pre_norm), full(w_in_g), row(D_IN)]
    aliases = {}
    if gx_prev is not None:
        aliases[len(ins)] = 0
        ins.append(gx_prev)
        in_specs.append(any_spec)
    if after is not None:
        ins.append(after)
        in_specs.append(any_spec)
    return pl.pallas_call(
        body, name=name, grid=(steps,), in_specs=in_specs,
        out_specs=[row(D_MODEL), pl.BlockSpec((1, D_MODEL), lambda i: (0, 0))],
        out_shape=[jax.ShapeDtypeStruct((seq, D_MODEL), F32), jax.ShapeDtypeStruct((1, D_MODEL), F32)],
        input_output_aliases=aliases,
        compiler_params=pltpu.CompilerParams(dimension_semantics=("arbitrary",)),
    )(*ins)


def _exchange_start(ex, name):
    n_in, n_out, n_sem = len(ex["ins"]), len(ex["outs"]), len(ex["sems"])

    def body(*refs):
        in_refs, land_refs, sems = refs[:n_in], refs[n_in:n_in + n_out], refs[n_in + n_out:n_in + n_out + n_sem]
        ex["start"](in_refs, land_refs, *sems)
        token = refs[-1]
        token[...] = jnp.zeros_like(token)

    hbm = pl.BlockSpec(memory_space=pltpu.HBM)
    sem = pl.BlockSpec(memory_space=pltpu.SEMAPHORE)
    ins = [pltpu.with_memory_space_constraint(a, pltpu.HBM) for a in ex["ins"]]
    landing = [pltpu.with_memory_space_constraint(lax.empty(o.shape, o.dtype), pltpu.HBM) for o in ex["outs"]]
    res = pl.pallas_call(
        body, name=name,
        out_shape=list(ex["sems"]) + [pltpu.HBM(a.shape, a.dtype) for a in ex["ins"]]
        + [pltpu.HBM(o.shape, o.dtype) for o in ex["outs"]] + [jax.ShapeDtypeStruct((8, LANES), F32)],
        in_specs=[hbm] * (n_in + n_out),
        out_specs=[sem] * n_sem + [hbm] * (n_in + n_out) + [pl.BlockSpec(memory_space=pltpu.VMEM)],
        input_output_aliases={k: n_sem + k for k in range(n_in + n_out)},
        compiler_params=pltpu.CompilerParams(has_side_effects=pltpu.SideEffectType.DATAFLOW_SIDE_EFFECTING),
    )(*ins, *landing)
    return res[:-1], res[-1]


def _exchange_wait(ex, handles, after, name):
    n_in, n_out, n_sem = len(ex["ins"]), len(ex["outs"]), len(ex["sems"])
    sems, thru = handles[:n_sem], handles[n_sem:]

    def body(*refs):
        in_refs, land_refs = refs[:n_in], refs[n_in:n_in + n_out]
        sem_refs = refs[n_in + n_out:n_in + n_out + n_sem]
        ex["finish"](in_refs, land_refs, *sem_refs)

    hbm = pl.BlockSpec(memory_space=pltpu.HBM)
    sem = pl.BlockSpec(memory_space=pltpu.SEMAPHORE)
    res = pl.pallas_call(
        body, name=name,
        out_shape=[pltpu.HBM(a.shape, a.dtype) for a in thru],
        in_specs=[hbm] * (n_in + n_out) + [sem] * n_sem + [pl.BlockSpec(memory_space=pl.ANY)],
        out_specs=[hbm] * (n_in + n_out),
        input_output_aliases={k: k for k in range(n_in + n_out)},
        compiler_params=pltpu.CompilerParams(has_side_effects=pltpu.SideEffectType.DATAFLOW_SIDE_EFFECTING),
    )(*thru, *sems, after)
    return res[:n_in], res[n_in:]


def _start_finish(build):
    def start(*refs):
        for cp in build(*refs):
            cp.start()

    def finish(*refs):
        for cp in build(*refs):
            cp.wait()

    return dict(start=start, finish=finish)


def _pair_exchange(grads):
    n = len(grads)

    def build(srcs, outs, send_sems, recv_sems):
        x, y, c = lax.axis_index("x"), lax.axis_index("y"), lax.axis_index("c")
        copies = []
        for t in range(n):
            rows = grads[t].shape[1] // 2
            copies.append(pltpu.make_async_remote_copy(
                src_ref=srcs[t].at[:, pl.ds((1 - c) * rows, rows)], dst_ref=outs[t],
                send_sem=send_sems.at[t], recv_sem=recv_sems.at[t], device_id=(x, y, 1 - c), device_id_type=MESH))
        return copies

    return dict(ins=list(grads), **_start_finish(build),
                outs=[jax.ShapeDtypeStruct((g.shape[0], g.shape[1] // 2, g.shape[2]), g.dtype) for g in grads],
                sems=[pltpu.SemaphoreType.DMA((n,)), pltpu.SemaphoreType.DMA((n,))])


def _pair_add(core, owns, gots):
    n = len(owns)

    def body(core_ref, *refs):
        for t in range(n):
            refs[2 * n + t][...] = (refs[t][...] + refs[n + t][...]).astype(BF16)

    halves = [(None,) + g.shape[1:] for g in gots]
    grid_spec = pltpu.PrefetchScalarGridSpec(
        num_scalar_prefetch=1, grid=(N_CHIPS,),
        in_specs=[pl.BlockSpec(h, lambda k, core_ref: (k, core_ref[0], 0)) for h in halves]
        + [pl.BlockSpec(h, lambda k, core_ref: (k, 0, 0)) for h in halves],
        out_specs=[pl.BlockSpec(h, lambda k, core_ref: (k, 0, 0)) for h in halves])
    return pl.pallas_call(
        body, name="pair_add", grid_spec=grid_spec,
        out_shape=[jax.ShapeDtypeStruct(g.shape, BF16) for g in gots],
    )(core, *owns, *gots)


def _chip_exchange(parts):
    n = len(parts)

    def build(srcs, outs, send_sems, recv_sems, local_sems):
        x, y, c = lax.axis_index("x"), lax.axis_index("y"), lax.axis_index("c")
        my_chip = 2 * x + y
        chips = [(1 - x, y), (x, 1 - y), (1 - x, 1 - y)]
        copies = [pltpu.make_async_copy(srcs[t].at[my_chip], outs[t].at[my_chip], local_sems.at[t]) for t in range(n)]
        for j, (cx, cy) in enumerate(chips):
            for t in range(n):
                k = n * j + t
                copies.append(pltpu.make_async_remote_copy(
                    src_ref=srcs[t].at[2 * cx + cy], dst_ref=outs[t].at[my_chip], send_sem=send_sems.at[k],
                    recv_sem=recv_sems.at[k], device_id=(cx, cy, c), device_id_type=MESH))
        return copies

    return dict(ins=list(parts), **_start_finish(build), outs=[jax.ShapeDtypeStruct(p.shape, p.dtype) for p in parts],
                sems=[pltpu.SemaphoreType.DMA((3 * n,)), pltpu.SemaphoreType.DMA((3 * n,)),
                      pltpu.SemaphoreType.DMA((n,))])


def _chip_sum(core, slots):
    n = len(slots)

    def body(core_ref, *refs):
        for t in range(n):
            acc = refs[t][0].astype(F32)
            for s in range(1, N_CHIPS):
                acc = acc + refs[t][s].astype(F32)
            refs[n + t][...] = acc

    blocks = [(s.shape[1] // TAIL_STEPS, s.shape[2]) for s in slots]
    grid_spec = pltpu.PrefetchScalarGridSpec(
        num_scalar_prefetch=1, grid=(TAIL_STEPS,),
        in_specs=[pl.BlockSpec((N_CHIPS,) + b, lambda i, core_ref: (0, i, 0)) for b in blocks],
        out_specs=[pl.BlockSpec((None,) + b, lambda i, core_ref: (core_ref[0], i, 0)) for b in blocks])
    return pl.pallas_call(
        body, name="chip_sum", grid_spec=grid_spec,
        out_shape=[jax.ShapeDtypeStruct((2,) + s.shape[1:], F32) for s in slots],
    )(core, *slots)


def _pair_gather(bufs, small):
    n = len(bufs)

    def body(*refs):
        small_ref, outs, small_out = refs[n], refs[n + 1:2 * n + 1], refs[2 * n + 1]
        send_sems, recv_sems, local_sem = refs[2 * n + 2:]
        x, y, c = lax.axis_index("x"), lax.axis_index("y"), lax.axis_index("c")
        me = 4 * x + 2 * y + c
        chips = [(1 - x, y), (x, 1 - y), (1 - x, 1 - y)]
        mine = pltpu.make_async_copy(small_ref, small_out.at[me], local_sem)
        mine.start()
        copies = [pltpu.make_async_remote_copy(
            src_ref=outs[t].at[c], dst_ref=outs[t].at[c], send_sem=send_sems.at[t], recv_sem=recv_sems.at[t],
            device_id=(x, y, 1 - c), device_id_type=MESH) for t in range(n)]
        peers = [(x, y, 1 - c)] + [(cx, cy, cc) for (cx, cy) in chips for cc in (c, 1 - c)]
        for j, peer in enumerate(peers):
            copies.append(pltpu.make_async_remote_copy(
                src_ref=small_ref, dst_ref=small_out.at[me], send_sem=send_sems.at[n + j],
                recv_sem=recv_sems.at[n + j], device_id=peer, device_id_type=MESH))
        for cp in copies:
            cp.start()
        for cp in copies:
            cp.wait()
        mine.wait()

    any_spec = pl.BlockSpec(memory_space=pl.ANY)
    res = pl.pallas_call(
        body, name="pair_gather",
        out_shape=[jax.ShapeDtypeStruct(b.shape, b.dtype) for b in bufs]
        + [jax.ShapeDtypeStruct((8,) + small.shape, small.dtype)],
        in_specs=[any_spec] * (n + 1), out_specs=[any_spec] * (n + 1),
        input_output_aliases={t: t for t in range(n)},
        scratch_shapes=[pltpu.SemaphoreType.DMA((n + 7,)), pltpu.SemaphoreType.DMA((n + 7,)),
                        pltpu.SemaphoreType.DMA],
    )(*bufs, small)
    return [r.reshape(2 * b.shape[1], b.shape[2]) for r, b in zip(res[:n], bufs)], res[n]


def _adamw(ws, gs, ms, vs):
    n = len(ws)

    def body(*refs):
        for t in range(n):
            w_ref, g_ref, m_ref, v_ref = refs[t:4 * n:n]
            gout_ref, d_ref, nm_ref, nv_ref = refs[4 * n + t::n]
            g = g_ref[...]
            gout_ref[...] = g
            d_ref[...], nm_ref[...], nv_ref[...] = _adamw_math(w_ref[...], g, m_ref[...], v_ref[...])

    specs = [pl.BlockSpec((w.shape[0] // TAIL_STEPS, w.shape[1]), lambda i: (i, 0)) for w in ws]
    res = pl.pallas_call(
        body, name="adamw", grid=(TAIL_STEPS,), in_specs=specs * 4, out_specs=specs * 4,
        out_shape=[jax.ShapeDtypeStruct(w.shape, F32) for w in ws] * 4,
    )(*ws, *gs, *ms, *vs)
    return [res[t::n] for t in range(n)]


def _adamw_math(w, g, m, v):
    c1 = 1.0 / (1.0 - ADAM_B1 ** ADAM_STEP)
    c2 = 1.0 / (1.0 - ADAM_B2 ** ADAM_STEP)
    nm = ADAM_B1 * m + (1.0 - ADAM_B1) * g
    nv = ADAM_B2 * v + (1.0 - ADAM_B2) * (g * g)
    return -ADAM_LR * ((nm * c1) / (jnp.sqrt(nv * c2) + ADAM_EPS) + ADAM_WD * w), nm, nv


def _small_update(slots, params, ms, vs):
    n = len(params)

    def body(slots_ref, *refs):
        w_refs, m_refs, v_refs, loss_ref = refs[:n], refs[n:2 * n], refs[2 * n:3 * n], refs[3 * n]
        g_refs, d_refs, nm_refs, nv_refs = (refs[3 * n + 1 + k * n:3 * n + 1 + (k + 1) * n] for k in range(4))
        acc = slots_ref[0]
        for s in range(1, slots.shape[0]):
            acc = acc + slots_ref[s]
        loss_ref[...] = acc[4:5, 0:1]
        grads = (acc[0:1] + acc[5:6], acc[3:4], acc[2:3], acc[1:2])
        for k in range(n):
            g = grads[k][:, :w_refs[k].shape[1]]
            g_refs[k][...] = g
            d_refs[k][...], nm_refs[k][...], nv_refs[k][...] = _adamw_math(w_refs[k][...], g, m_refs[k][...],
                                                                           v_refs[k][...])

    return pl.pallas_call(
        body, name="small_update",
        out_shape=[jax.ShapeDtypeStruct((1, 1), F32)] + [jax.ShapeDtypeStruct(p.shape, F32) for p in params] * 4,
    )(slots, *params, *ms, *vs)


def _local_step(x, mem, target, pre_norm, sink_a, mem_norm, post_norm, w_in_g, w_out, w_mkv, gathers=None,
                own=None):
    first_gather, late_gather = gathers if gathers else (None, None)
    u, ut, p_own, hosted = _pre_norm(x, pre_norm, own[1] if own else None, first_gather)
    if gathers:
        w_in_g = hosted[0].reshape(N_CHIPS, D_MODEL, SHARD_IN)
    pr = _pre_proj(u, w_in_g, (own[0], p_own) if own else None, late_gather)
    pr["ut"] = ut
    if gathers:
        w_out, w_mkv = (g.reshape(D_MODEL, g.shape[-1]) for g in pr["hosted"])
    mk, mv = _mem_kv(mem, mem_norm, w_mkv)
    sink = sink_a.reshape(-1)
    qa, ka, va = pr["qa"][None], pr["ka"][None], pr["va"][None]
    oa, lse_a = _band_fwd(qa, ka, va, sink, max_dist=A_WINDOW - 1, name="swa_fwd")
    ob_list, lseb_list = [], []
    for k, (win, dil) in enumerate(B_CONFIGS):
        o_i, l_i = _band_fwd(pr["qb"][k], pr["kb"][k], pr["vb"][k], None, max_dist=win // dil, name=f"dil{dil}_fwd")
        ob_list.append(o_i)
        lseb_list.append(l_i)
    oc, lse_c = _mem_attn_fwd(pr["qc"], mk, mv)
    sink_row = jnp.pad(sink, (0, LANES - sink.shape[0])).reshape(1, LANES)
    po = _post(x, target, post_norm, w_out, sink_row, oa[0], lse_a[0], pr["ga"], ob_list, lseb_list, pr["gb"], oc,
               pr["gc"])
    dqc, dmk, dmv = _mem_attn_bwd(pr["qc"], mk, mv, po["doc"], lse_c, po["dl_c"])
    dqa, dka, dva = _band_bwd(qa, ka, va, po["doa"][None], lse_a, po["dl_a"][None], max_dist=A_WINDOW - 1,
                              name="swa_bwd")
    res = dict(qb=[], kb=[], vb=[])
    for k, (win, dil) in enumerate(B_CONFIGS):
        dq_i, dk_i, dv_i = _band_bwd(pr["qb"][k], pr["kb"][k], pr["vb"][k], po["dob"][k], po["lse_b"][k],
                                     po["dl_b"][k], max_dist=win // dil, name=f"dil{dil}_bwd")
        res["qb"].append(dq_i)
        res["kb"].append(dk_i)
        res["vb"].append(dv_i)
    nat = dict(qa=dqa[0], ka=dka[0], va=dva[0], ga=po["dga"], gb=po["dgb"], qc=dqc, gc=po["dgc"])
    dproj, gw_in = _grad_w_in(pr["ut"], nat, res)
    gw_mkv, gmem = _mem_kv_bwd(mem, mem_norm, w_mkv, dmk, dmv)
    gsink = -po["gsink"][0, :sink.shape[0]]
    return dict(loss=po["loss"], g=po["g"], dproj=dproj, gw_in=gw_in, gw_out=po["gw_out"], gw_mkv=gw_mkv,
                gpost=po["gpost"], gmem=gmem, gsink=gsink, w_in_g=w_in_g)


def kernel(x, mem, pre_norm, w_in, sink_a, mem_norm, w_mem_kv, w_out, post_norm, loss_target, m_pre_norm, m_w_in, m_sink_a, m_mem_norm, m_w_mem_kv, m_w_out, m_post_norm, v_pre_norm, v_w_in, v_sink_a, v_mem_norm, v_w_mem_kv, v_w_out, v_post_norm):
    w_own = w_in[0]
    gathers = (_gather_exchange([w_own], prefilled=False),
               _gather_exchange([w_out[0].astype(BF16), w_mem_kv[0].astype(BF16)]))
    chip = (2 * lax.axis_index("x") + lax.axis_index("y")).astype(jnp.int32).reshape(1)
    loc = _local_step(x[0], mem[0], loss_target[0], pre_norm, sink_a, mem_norm, post_norm, None, None, None, gathers,
                      (chip, w_own))
    big = [loc["gw_in"], loc["gw_out"].reshape(N_CHIPS, D_MODEL // N_CHIPS, D_MODEL),
           loc["gw_mkv"].reshape(N_CHIPS, D_MODEL // N_CHIPS, 2 * C_W)]
    core = lax.axis_index("c").astype(jnp.int32).reshape(1)
    w_in_full = loc["w_in_g"]
    step_in = (x[0], loc["g"], pre_norm, w_in_full, loc["dproj"])
    pair_ex = _pair_exchange(big)
    pair_handles, token = _exchange_start(pair_ex, "pair_exchange_start")
    gx_a, gpre_a = _input_grad(*step_in, None, (0, 2), token, "input_grad_a")
    big, got = _exchange_wait(pair_ex, pair_handles, gpre_a, "pair_exchange_wait")
    parts = _pair_add(core, big, got)
    chip_ex = _chip_exchange(parts)
    chip_handles, token = _exchange_start(chip_ex, "chip_exchange_start")
    grad_x, gpre_b = _input_grad(*step_in, gx_a, (2, 14), token, "input_grad_b")
    _, slots = _exchange_wait(chip_ex, chip_handles, gpre_b, "chip_exchange_wait")
    halves = _chip_sum(core, slots)
    widen = lambda a: jnp.pad(a.reshape(1, -1), ((0, 0), (0, D_MODEL - a.size)))
    small = jnp.concatenate([gpre_a, loc["gpost"], loc["gmem"], widen(loc["gsink"]), widen(loc["loss"]), gpre_b,
                             jnp.zeros((2, D_MODEL), F32)], axis=0)
    (g_in, g_out, g_mkv), small_slots = _pair_gather(halves, small)
    (loss, g_pre, g_sink, g_mem, g_post, d_pre, d_sink, d_mem, d_post, nm_pre, nm_sink, nm_mem, nm_post,
     nv_pre, nv_sink, nv_mem, nv_post) = _small_update(
        small_slots, (pre_norm, sink_a, mem_norm, post_norm), (m_pre_norm, m_sink_a, m_mem_norm, m_post_norm),
        (v_pre_norm, v_sink_a, v_mem_norm, v_post_norm))

    (g_in, d_in, nm_in, nv_in), (g_out, d_out, nm_out, nv_out), (g_mkv, d_mkv, nm_mkv, nv_mkv) = _adamw(
        (w_in[0], w_out[0], w_mem_kv[0]), (g_in, g_out, g_mkv), (m_w_in[0], m_w_out[0], m_w_mem_kv[0]),
        (v_w_in[0], v_w_out[0], v_w_mem_kv[0]))
    lead = lambda a: a[None]
    return (loss.reshape(()), lead(grad_x),
            g_pre, lead(g_in), g_sink, g_mem, lead(g_mkv), lead(g_out), g_post,
            d_pre, lead(d_in), d_sink, d_mem, lead(d_mkv), lead(d_out), d_post,
            nm_pre, lead(nm_in), nm_sink, nm_mem, lead(nm_mkv), lead(nm_out), nm_post,
            nv_pre, lead(nv_in), nv_sink, nv_mem, lead(nv_mkv), lead(nv_out), nv_post)
```

```python
import numpy as np
import jax
import jax.numpy as jnp
from jax import lax
from jax.experimental import pallas as pl
from jax.experimental.pallas import tpu as pltpu

F32 = jnp.float32
BF16 = jnp.bfloat16

D_MODEL = 1024
HEAD_DIM = 64
LANES = 128
BLOCK = 128
ROW_TILE = 512
ATTN_TILE = 1024
INPUT_GRAD_TILES = 16
TAIL_STEPS = 4
A_W, A_KV_W, B_W, C_W = 384, 128, 384, 256
N_MEM = 256
D_IN = 3072
N_CHIPS = 4
SHARD_IN = D_IN // N_CHIPS
B_CONFIGS = ((128, 1), (512, 4), (2048, 16))
B_DILS = tuple(d for _, d in B_CONFIGS)
A_WINDOW = 128
RMS_EPS = 1e-6
ROPE_THETA = 500000.0
SCALE = HEAD_DIM ** -0.5
NEG = -1e30
ADAM_LR, ADAM_B1, ADAM_B2, ADAM_EPS, ADAM_WD, ADAM_STEP = 0.001, 0.9, 0.999, 1e-08, 0.01, 10

NT = (((1,), (1,)), ((), ()))
TN = (((0,), (0,)), ((), ()))
MESH = pl.DeviceIdType.MESH

_PROJ_LAYOUT = (
    [("qa", 128 * i, True, True) for i in range(3)] + [("ka", 0, True, False), ("va", 0, False, False)]
    + [("ga", 128 * i, False, False) for i in range(3)]
    + [("qb", 128 * i, True, True) for i in range(3)] + [("kb", 128 * i, True, False) for i in range(3)]
    + [("vb", 128 * i, False, False) for i in range(3)] + [("gb", 128 * i, False, False) for i in range(3)]
    + [("qc", 128 * i, False, True) for i in range(2)] + [("gc", 128 * i, False, False) for i in range(2)]
)
_PROJ_WIDTH = dict(qa=A_W, ka=A_KV_W, va=A_KV_W, ga=A_W, qb=B_W, kb=B_W, vb=B_W, gb=B_W, qc=C_W, gc=C_W)
_NATURAL = ("qa", "ka", "va", "ga", "gb", "qc", "gc")
_DILATED = ("qb", "kb", "vb")


def _dot(a, b):
    return jnp.dot(a, b, preferred_element_type=F32)


def _dot_nt(a, b):
    return lax.dot_general(a, b, NT, preferred_element_type=F32)


def _dot_tn(a, b):
    return lax.dot_general(a, b, TN, preferred_element_type=F32)


def _half_masks(rows):
    lane = lax.broadcasted_iota(jnp.int32, (rows, LANES), 1)
    return lane < HEAD_DIM, lane >= HEAD_DIM


def _rope(t, c, sm, sp):
    return t * c + pltpu.roll(t, LANES - 8, 1) * sm + pltpu.roll(t, 8, 1) * sp


def _rope_tables(seq, tm):
    dim = np.arange(LANES) % HEAD_DIM
    inv_freq = (np.float32(ROPE_THETA) ** (-np.arange(0, 16, 2, dtype=np.float32) / np.float32(16))).astype(np.float64)
    freq = np.where(dim < 16, inv_freq[dim % 8], 0.0)[None, :]
    local = np.arange(tm, dtype=np.float64)[:, None] * freq
    base = (np.arange(seq // tm, dtype=np.float64) * tm)[:, None] * freq
    both = lambda a: np.concatenate([np.cos(a), np.sin(a)], axis=1).astype(np.float32)
    return jnp.asarray(both(local)), jnp.asarray(np.repeat(both(base), 8, axis=0))


def _rope_coeffs(local_ref, base_ref):
    cl, sl = local_ref[:, :LANES], local_ref[:, LANES:]
    cb, sb = base_ref[0:1, :LANES], base_ref[0:1, LANES:]
    cos = cb * cl - sb * sl
    sin = sb * cl + cb * sl
    dim = lax.broadcasted_iota(jnp.int32, (1, LANES), 1) % HEAD_DIM
    return cos, jnp.where(dim < 8, -sin, 0.0), jnp.where((dim >= 8) & (dim < 16), sin, 0.0)


def _split3(x):
    a = x.astype(BF16)
    r = x - a.astype(F32)
    b = r.astype(BF16)
    c = (r - b.astype(F32)).astype(BF16)
    return a, b, c


def _rows_to_lanes(x):
    row = lax.broadcasted_iota(jnp.int32, (8, LANES), 0)
    lane = lax.broadcasted_iota(jnp.int32, (8, LANES), 1)
    eye = (row == lane).astype(BF16)
    a, b, c = _split3(x)
    return _dot_nt(eye, a) + _dot_nt(eye, b) + _dot_nt(eye, c)


def _head_sum_matrix(width):
    k = lax.broadcasted_iota(jnp.int32, (width, LANES), 0)
    h = lax.broadcasted_iota(jnp.int32, (width, LANES), 1)
    return (k // HEAD_DIM == h).astype(BF16)


def _head_expand_matrix(width):
    h = lax.broadcasted_iota(jnp.int32, (LANES, width), 0)
    k = lax.broadcasted_iota(jnp.int32, (LANES, width), 1)
    return (k // HEAD_DIM == h).astype(BF16)


def _dot_split(x, mat, terms):
    parts = _split3(x)[:terms]
    out = _dot(parts[0], mat)
    for p in parts[1:]:
        out = out + _dot(p, mat)
    return out


def _per_head(cols, fill=0.0):
    rows = cols[0].shape[0]
    lane = lax.broadcasted_iota(jnp.int32, (rows, LANES), 1)
    out = jnp.full((rows, LANES), fill, F32)
    for h, col in enumerate(cols):
        out = jnp.where(lane == h, col, out)
    return out


def _lane_blocks(width):
    return [slice(p * LANES, (p + 1) * LANES) for p in range(width // LANES)]


def _stage(rows, width):
    return pltpu.VMEM((width // LANES, rows, LANES), F32)


def _stage_write(buf, value):
    for p, lanes in enumerate(_lane_blocks(value.shape[1])):
        buf[p] = value[:, lanes]


def _stage_read(buf):
    return jnp.concatenate([buf[p] for p in range(buf.shape[0])], axis=1) if buf.shape[0] > 1 else buf[0]


def _to_residues(buf, out_ref, dil):
    rows = buf.shape[1] // dil
    for r in range(dil):
        for p in range(buf.shape[0]):
            plane = buf.at[p]
            out_ref[r, :, p * LANES:(p + 1) * LANES] = plane[pl.ds(r, rows, stride=dil), :].astype(out_ref.dtype)


def _from_residues(in_ref, buf, dil):
    rows = buf.shape[1] // dil
    for r in range(dil):
        for p in range(buf.shape[0]):
            plane = buf.at[p]
            plane[pl.ds(r, rows, stride=dil), :] = in_ref[r, :, p * LANES:(p + 1) * LANES].astype(F32)


def _residue_spec(dil, tm, width):
    return pl.BlockSpec((dil, tm // dil, width), lambda i: (0, i, 0))


def _gather_exchange(shards_2d):
    shards = tuple(jax.ShapeDtypeStruct((2, s.shape[0] // 2, s.shape[1]), BF16) for s in shards_2d)
    n = len(shards)

    def copies(srcs, outs, send_sems, recv_sems, local_sems):
        x, y, c = lax.axis_index("x"), lax.axis_index("y"), lax.axis_index("c")
        my_chip = 2 * x + y
        sibling = (x, y, 1 - c)
        chips = [(1 - x, y), (x, 1 - y), (1 - x, 1 - y)]

        def copy(k, src, dst, to):
            return pltpu.make_async_remote_copy(src_ref=src, dst_ref=dst, send_sem=send_sems.at[k],
                                                recv_sem=recv_sems.at[k], device_id=to, device_id_type=MESH)

        first, arrive, passed, sibling_arrive = [], [], [], []
        for j, (cx, cy) in enumerate(chips):
            chip = 2 * cx + cy
            for t in range(n):
                k = n * j + t
                first.append(copy(k, srcs[t].at[c], outs[t].at[my_chip, c], (cx, cy, c)))
                arrive.append(copy(k, srcs[t].at[c], outs[t].at[chip, c], (cx, cy, c)))
                passed.append(copy(n * 3 + k, outs[t].at[chip, c], outs[t].at[chip, c], sibling))
                sibling_arrive.append(copy(n * 3 + k, outs[t].at[chip, 1 - c], outs[t].at[chip, 1 - c], sibling))
        own = [pltpu.make_async_copy(srcs[t], outs[t].at[my_chip], local_sems.at[t]) for t in range(n)]
        return first, arrive, passed, sibling_arrive, own

    def start(*refs):
        first, _, _, _, own = copies(*refs)
        for cp in first + own:
            cp.start()

    def forward(refs, senders):
        _, arrive, passed, _, _ = copies(*refs)
        for j in senders:
            for k in range(n * j, n * (j + 1)):
                arrive[k].wait_recv()
                passed[k].start()

    def mid(*refs):
        forward(refs, (0, 1))

    def finish(*refs):
        forward(refs, (2,))
        first, _, passed, sibling_arrive, own = copies(*refs)
        for cp in sibling_arrive:
            cp.wait_recv()
        for cp in first + passed:
            cp.wait_send()
        for cp in own:
            cp.wait()

    return dict(ins=[], start=start, mid=mid, finish=finish,
                outs=[jax.ShapeDtypeStruct((N_CHIPS,) + s.shape, s.dtype) for s in shards],
                sems=[pltpu.SemaphoreType.DMA((6 * n,)), pltpu.SemaphoreType.DMA((6 * n,)),
                      pltpu.SemaphoreType.DMA((n,))])


def _mem_kv(mem, mem_norm, w_mkv):
    def body(mem_ref, g_ref, w_ref, mk_ref, mv_ref):
        m = mem_ref[...]
        r = lax.rsqrt(jnp.mean(m * m, axis=-1, keepdims=True) + RMS_EPS)
        mn = (m * r * g_ref[...]).astype(BF16)
        kv = _dot(mn, w_ref[...])
        mk_ref[...] = kv[:, :C_W].astype(BF16)
        mv_ref[...] = kv[:, C_W:].astype(BF16)

    return pl.pallas_call(
        body, name="mem_kv",
        out_shape=[jax.ShapeDtypeStruct((N_MEM, C_W), BF16)] * 2,
    )(mem, mem_norm, w_mkv)


def _mem_kv_bwd(mem, mem_norm, w_mkv, dmk, dmv):
    def body(mem_ref, g_ref, w_ref, dmk_ref, dmv_ref, gw_ref, gn_ref):
        m = mem_ref[...]
        r = lax.rsqrt(jnp.mean(m * m, axis=-1, keepdims=True) + RMS_EPS)
        mhat = m * r
        mn = (mhat * g_ref[...]).astype(BF16)
        dkv = jnp.concatenate([dmk_ref[...], dmv_ref[...]], axis=1).astype(BF16)
        gw_ref[...] = _dot_tn(mn, dkv)
        dmn = _dot_nt(dkv, w_ref[...])
        gn_ref[...] = jnp.sum(dmn * mhat, axis=0, keepdims=True)

    return pl.pallas_call(
        body, name="mem_kv_bwd",
        out_shape=[jax.ShapeDtypeStruct((D_MODEL, 2 * C_W), F32), jax.ShapeDtypeStruct((1, D_MODEL), F32)],
    )(mem, mem_norm, w_mkv, dmk, dmv)


def _host_phases(host, in_refs, out_refs, sems, steps, before):
    if not host:
        return
    step = pl.program_id(0)
    phases = [("start", 0)] if before else [("mid", max(steps - 3, 0)), ("finish", steps - 1)]
    for phase, at in phases:
        pl.when(step == at)(lambda phase=phase: host[phase](in_refs, out_refs, *sems))


def _pre_norm(x, pre_norm, w_own=None, host=None):
    seq = x.shape[0]
    tm = min(ROW_TILE, seq)
    n_own_in = 2 if w_own is None else 3
    n_own_out = n_own_in
    n_host_in = len(host["ins"]) if host else 0
    n_host_out = len(host["outs"]) if host else 0
    half = D_MODEL // 2

    def body(x_ref, g_ref, *refs):
        w_ref = None if w_own is None else refs[0]
        refs = refs[n_own_in - 2:]
        host_in, own_out, refs = refs[:n_host_in], refs[n_host_in:n_host_in + n_own_out], refs[n_host_in + n_own_out:]
        host_out, refs = refs[:n_host_out], refs[n_host_out:]
        if w_own is not None:
            wb, sems = refs[0], refs[1:]

            @pl.when(pl.program_id(0) == 0)
            def _():
                for h in range(2):
                    wb[h] = w_ref[h * half:(h + 1) * half, :].astype(BF16)
            if host:
                host_in = [wb]
        else:
            sems = refs
        _host_phases(host, host_in, host_out, sems, seq // tm, before=True)

        xv = x_ref[...]
        r = lax.rsqrt(jnp.mean(xv * xv, axis=-1, keepdims=True) + RMS_EPS)
        u = xv * r * g_ref[...]
        ub = u.astype(BF16)
        own_out[0][...] = ub
        own_out[1][...] = u.T.astype(BF16)
        if w_own is not None:
            own_out[2][...] = _dot(ub[:, :half], wb[0]) + _dot(ub[:, half:], wb[1])
        _host_phases(host, host_in, host_out, sems, seq // tm, before=False)

    any_spec = pl.BlockSpec(memory_space=pl.ANY)
    ins = [x, pre_norm]
    in_specs = [pl.BlockSpec((tm, D_MODEL), lambda i: (i, 0)), pl.BlockSpec(pre_norm.shape, lambda i: (0, 0))]
    out_shape = [jax.ShapeDtypeStruct((seq, D_MODEL), BF16), jax.ShapeDtypeStruct((seq // tm, D_MODEL, tm), BF16)]
    out_specs = [pl.BlockSpec((tm, D_MODEL), lambda i: (i, 0)), pl.BlockSpec((None, D_MODEL, tm), lambda i: (i, 0, 0))]
    aliases, scratch = {}, []
    if w_own is not None:
        ins.append(w_own)
        in_specs.append(pl.BlockSpec(w_own.shape, lambda i: (0, 0)))
        out_shape.append(jax.ShapeDtypeStruct((seq, w_own.shape[1]), F32))
        out_specs.append(pl.BlockSpec((tm, w_own.shape[1]), lambda i: (i, 0)))
        scratch.append(pltpu.VMEM((2, half, w_own.shape[1]), BF16))
    if host:
        aliases = {len(ins) + k: n_own_out + v for k, v in host.get("aliases", {}).items()}
        ins += list(host["ins"])
        in_specs += [any_spec] * n_host_in
        out_shape += list(host["outs"])
        out_specs += [any_spec] * n_host_out
        scratch += list(host["sems"])
    res = pl.pallas_call(
        body, name="pre_norm", grid=(seq // tm,), in_specs=in_specs, out_specs=out_specs, out_shape=out_shape,
        input_output_aliases=aliases, scratch_shapes=scratch,
        compiler_params=pltpu.CompilerParams(dimension_semantics=("arbitrary",)),
    )(*ins)
    return res[0], res[1], (None if w_own is None else res[2]), res[n_own_out:]


def _pre_proj(u, w_in_g, own=None, host=None, host_shards=()):
    seq = u.shape[0]
    tm = min(ROW_TILE, seq)
    n_nat, n_dil = len(_NATURAL), len(_DILATED) * len(B_DILS)
    rope = _rope_tables(seq, tm)

    n_host_in = len(host["ins"]) if host else 0
    n_host_out = len(host["outs"]) if host else 0
    n_own_out = n_nat + n_dil
    n_shards = len(host_shards)

    def body(u_ref, w_ref, rl_ref, rb_ref, *refs):
        if own:
            (chip_ref, pown_ref), refs = refs[:2], refs[2:]
        shard_refs, refs = refs[:n_shards], refs[n_shards:]
        host_in, refs = refs[:n_host_in], refs[n_host_in:]
        nat = dict(zip(_NATURAL, refs[:n_nat]))
        res = {n: refs[n_nat + len(B_DILS) * k:n_nat + len(B_DILS) * (k + 1)] for k, n in enumerate(_DILATED)}
        host_out = refs[n_own_out:n_own_out + n_host_out]
        bufs = dict(zip(_DILATED, refs[n_own_out + n_host_out:]))
        refs = refs[n_own_out + n_host_out + len(_DILATED):]
        shard_bufs, sems = refs[:n_shards], refs[n_shards:]
        if host_shards:
            @pl.when(pl.program_id(0) == 0)
            def _():
                for src, dst in zip(shard_refs, shard_bufs):
                    rows = src.shape[0] // 2
                    for h in range(2):
                        dst[h] = src[h * rows:(h + 1) * rows, :].astype(BF16)
            host_in = shard_bufs
        _host_phases(host, host_in, host_out, sems, seq // tm, before=True)

        def project(own_chip):
            ub = u_ref[...]
            c, sm, sp = _rope_coeffs(rl_ref, rb_ref)
            for j in range(N_CHIPS):
                pj = pown_ref[...] if j == own_chip else _dot(ub, w_ref[j])
                for b in range(SHARD_IN // LANES):
                    name, off, roped, scaled = _PROJ_LAYOUT[(SHARD_IN // LANES) * j + b]
                    piece = pj[:, LANES * b:LANES * (b + 1)]
                    if roped:
                        piece = _rope(piece, c, sm, sp)
                    if scaled:
                        piece = piece * SCALE
                    if name in bufs:
                        bufs[name][off // LANES] = piece
                    else:
                        nat[name][:, off:off + LANES] = piece.astype(BF16)
            for name in _DILATED:
                for ref, dil in zip(res[name], B_DILS):
                    _to_residues(bufs[name], ref, dil)

        if own:
            for chip in range(N_CHIPS):
                pl.when(chip_ref[0] == chip)(lambda chip=chip: project(chip))
        else:
            project(None)
        _host_phases(host, host_in, host_out, sems, seq // tm, before=False)

    row = lambda w: pl.BlockSpec((tm, w), lambda i: (i, 0))
    full = lambda a: pl.BlockSpec(a.shape, lambda i: (0,) * a.ndim)
    any_spec = pl.BlockSpec(memory_space=pl.ANY)
    out_shape = [jax.ShapeDtypeStruct((seq, _PROJ_WIDTH[n]), BF16) for n in _NATURAL]
    out_specs = [row(_PROJ_WIDTH[n]) for n in _NATURAL]
    for n in _DILATED:
        for dil in B_DILS:
            out_shape.append(jax.ShapeDtypeStruct((dil, seq // dil, B_W), BF16))
            out_specs.append(_residue_spec(dil, tm, B_W))
    ins = [u, w_in_g, *rope]
    in_specs = [row(D_MODEL), full(w_in_g), full(rope[0]), pl.BlockSpec((8, 2 * LANES), lambda i: (i, 0))]
    if own:
        ins += list(own)
        in_specs += [pl.BlockSpec(memory_space=pltpu.SMEM), row(SHARD_IN)]
    ins += list(host_shards)
    in_specs += [full(s) for s in host_shards]
    scratch = [_stage(tm, B_W)] * len(_DILATED)
    scratch += [pltpu.VMEM((2, s.shape[0] // 2, s.shape[1]), BF16) for s in host_shards]
    aliases = {}
    if host:
        aliases = {len(ins) + k: n_own_out + v for k, v in host.get("aliases", {}).items()}
        ins += list(host["ins"])
        in_specs += [any_spec] * n_host_in
        out_shape += list(host["outs"])
        out_specs += [any_spec] * n_host_out
        scratch += list(host["sems"])
    res = pl.pallas_call(
        body, name="pre_proj", grid=(seq // tm,), in_specs=in_specs, out_specs=out_specs, out_shape=out_shape,
        input_output_aliases=aliases, scratch_shapes=scratch,
        compiler_params=pltpu.CompilerParams(dimension_semantics=("arbitrary",)),
    )(*ins)
    out = dict(zip(_NATURAL, res[:n_nat]))
    for k, n in enumerate(_DILATED):
        out[n] = res[n_nat + len(B_DILS) * k:n_nat + len(B_DILS) * (k + 1)]
    out["hosted"] = res[n_own_out:]
    return out


def _band_bias(max_dist, transposed):
    i = np.arange(BLOCK)[:, None]
    j = np.arange(BLOCK)[None, :]
    if transposed:
        same = i <= j
        other = (j + BLOCK - i) <= max_dist
        vis = np.concatenate([same, other], axis=1)
    else:
        prev = (i + BLOCK - j) <= max_dist
        same = j <= i
        vis = np.concatenate([prev, same], axis=1)
    return jnp.asarray(np.where(vis, 0.0, NEG).astype(np.float32))


def _kv_place(h, gqa):
    return (0, h // 3) if gqa else (h // 2, h % 2)


def _band_fwd(q, k, v, sink, *, max_dist, name):
    dil, length, wq = q.shape
    wk = k.shape[2]
    gqa = wk != wq
    tq = min(ATTN_TILE, length)
    ns, nt = tq // BLOCK, length // tq
    npair = wq // LANES
    bias = _band_bias(max_dist, transposed=False)
    has_sink = sink is not None

    def body(*refs):
        if has_sink:
            sink_ref, refs = refs[0], refs[1:]
        q_ref, k_ref, kp_ref, v_ref, vp_ref, bias_ref, o_ref, lse_ref, kbuf, vbuf = refs[:10]
        i = pl.program_id(1)
        kbuf[0:BLOCK] = kp_ref[...]
        kbuf[BLOCK:] = k_ref[...]
        vbuf[0:BLOCK] = vp_ref[...]
        vbuf[BLOCK:] = v_ref[...]
        if gqa:
            kroll, vroll = refs[10:12]
            kroll[...] = pltpu.roll(kbuf[...], HEAD_DIM, 1)
            vroll[...] = pltpu.roll(vbuf[...], HEAD_DIM, 1)
        half = _half_masks(BLOCK)
        col_prev = (lax.broadcasted_iota(jnp.int32, (1, 2 * BLOCK), 1) < BLOCK).astype(F32)

        def score_matmuls(a):
            scores = []
            for p in range(npair):
                qp = q_ref[a * BLOCK:(a + 1) * BLOCK, p * LANES:(p + 1) * LANES]
                for e in range(2):
                    pk, ek = _kv_place(2 * p + e, gqa)
                    kw = (kbuf if ek == e else kroll)[a * BLOCK:(a + 2) * BLOCK, pk * LANES:(pk + 1) * LANES]
                    scores.append(_dot_nt(jnp.where(half[e], qp, jnp.zeros_like(qp)), kw))
            return scores

        pending = score_matmuls(0)
        for a in range(ns):
            r0 = a * BLOCK
            b = bias_ref[...]
            if a == 0:
                b = b + jnp.where(i == 0, NEG, 0.0) * col_prev
            scores = pending
            m_cols, l_cols, probs = [], [], []
            for h, s in enumerate(scores):
                s = s + b
                m = jnp.max(s, axis=1, keepdims=True)
                if has_sink:
                    m = jnp.maximum(m, sink_ref[h])
                pe = jnp.exp(s - m)
                l = jnp.sum(pe, axis=1, keepdims=True)
                if has_sink:
                    l = l + jnp.exp(sink_ref[h] - m)
                probs.append(pe.astype(BF16))
                m_cols.append(m)
                l_cols.append(l)
            pending = score_matmuls(a + 1) if a + 1 < ns else None
            for p in range(npair):
                o_h = []
                for e in range(2):
                    h = 2 * p + e
                    pk, ek = _kv_place(h, gqa)
                    vw = (vbuf if ek == e else vroll)[r0:r0 + 2 * BLOCK, pk * LANES:(pk + 1) * LANES]
                    o_h.append(_dot(probs[h], vw) * (1.0 / l_cols[h]))
                o_ref[r0:r0 + BLOCK, p * LANES:(p + 1) * LANES] = jnp.where(half[0], o_h[0], o_h[1]).astype(BF16)
            lse_ref[r0:r0 + BLOCK, :] = _per_head(m_cols) + jnp.log(_per_head(l_cols, 1.0))

    main = lambda w: pl.BlockSpec((None, tq, w), lambda r, i: (r, i, 0))
    prev = lambda w: pl.BlockSpec((None, BLOCK, w), lambda r, i: (r, jnp.maximum(i * ns - 1, 0), 0))
    in_specs = [main(wq), main(wk), prev(wk), main(wk), prev(wk), pl.BlockSpec(bias.shape, lambda r, i: (0, 0))]
    args = [q, k, k, v, v, bias]
    if has_sink:
        in_specs = [pl.BlockSpec(memory_space=pltpu.SMEM)] + in_specs
        args = [sink] + args
    scratch = [pltpu.VMEM((tq + BLOCK, wk), BF16)] * (4 if gqa else 2)
    return pl.pallas_call(
        body, name=name, grid=(dil, nt), in_specs=in_specs,
        out_specs=[main(wq), main(LANES)],
        out_shape=[jax.ShapeDtypeStruct((dil, length, wq), BF16), jax.ShapeDtypeStruct((dil, length, LANES), F32)],
        scratch_shapes=scratch,
    )(*args)


def _band_bwd(q, k, v, do, lse, delta, *, max_dist, name):
    dil, length, wq = q.shape
    wk = k.shape[2]
    gqa = wk != wq
    tq = min(ATTN_TILE, length)
    ns, nt = tq // BLOCK, length // tq
    npair = wq // LANES
    nblocks = length // BLOCK
    bias = _band_bias(max_dist, transposed=True)

    def body(q_ref, qn_ref, do_ref, don_ref, lse_ref, lsen_ref, dl_ref, dln_ref, k_ref, v_ref, bias_ref,
             dq_ref, dk_ref, dv_ref, stat_l, stat_d, dqt, kt, *rolled):
        i = pl.program_id(1)
        for pk in range(wk // LANES):
            kt[pk] = k_ref[:, pk * LANES:(pk + 1) * LANES].astype(F32).T.astype(BF16)
        if gqa:
            kroll, vroll, ktroll = rolled
            kroll[...] = pltpu.roll(k_ref[...], HEAD_DIM, 1)
            vroll[...] = pltpu.roll(v_ref[...], HEAD_DIM, 1)
            ktroll[0] = kroll[...].astype(F32).T.astype(BF16)
        for a in range(ns):
            rows = slice(a * BLOCK, (a + 1) * BLOCK)
            stat_l[a] = _rows_to_lanes(lse_ref[rows, :])
            stat_d[a] = _rows_to_lanes(dl_ref[rows, :])
        stat_l[ns] = _rows_to_lanes(lsen_ref[...])
        stat_d[ns] = _rows_to_lanes(dln_ref[...])

        @pl.when(i == 0)
        def _():
            dqt[:, :, 0:BLOCK] = jnp.zeros((npair, LANES, BLOCK), F32)

        @pl.when(i > 0)
        def _():
            dqt[:, :, 0:BLOCK] = dqt[:, :, tq:tq + BLOCK]

        dqt[:, :, BLOCK:] = jnp.zeros((npair, LANES, tq), F32)
        half2 = _half_masks(2 * BLOCK)
        row = lax.broadcasted_iota(jnp.int32, (LANES, BLOCK), 0)
        row_half = (row < HEAD_DIM, row >= HEAD_DIM)
        col_next = (lax.broadcasted_iota(jnp.int32, (1, 2 * BLOCK), 1) >= BLOCK).astype(F32)

        def scores(b):
            rows = slice(b * BLOCK, (b + 1) * BLOCK)
            nxt_rows = slice((b + 1) * BLOCK, (b + 2) * BLOCK)
            items = []
            for p in range(npair):
                lanes = slice(p * LANES, (p + 1) * LANES)
                q_next = q_ref[nxt_rows, lanes] if b + 1 < ns else qn_ref[:, lanes]
                do_next = do_ref[nxt_rows, lanes] if b + 1 < ns else don_ref[:, lanes]
                qw = jnp.concatenate([q_ref[rows, lanes], q_next], axis=0)
                dow = jnp.concatenate([do_ref[rows, lanes], do_next], axis=0)
                for e in range(2):
                    h = 2 * p + e
                    pk, ek = _kv_place(h, gqa)
                    klanes = slice(pk * LANES, (pk + 1) * LANES)
                    kb = (k_ref if ek == e else kroll)[rows, klanes]
                    vb = (v_ref if ek == e else vroll)[rows, klanes]
                    qm = jnp.where(half2[e], qw, jnp.zeros_like(qw))
                    dom = jnp.where(half2[e], dow, jnp.zeros_like(dow))
                    items.append(dict(p=p, e=e, h=h, pk=pk, ek=ek, qm=qm, dom=dom,
                                      st=_dot_nt(kb, qm), dpt=_dot_nt(vb, dom)))
            return items

        def probs(b, items):
            bt = bias_ref[...]
            if b == ns - 1:
                bt = bt + jnp.where(i == nt - 1, NEG, 0.0) * col_next
            for it in items:
                h = it["h"]
                lrow = jnp.concatenate([stat_l[b, h:h + 1, :], stat_l[b + 1, h:h + 1, :]], axis=1)
                drow = jnp.concatenate([stat_d[b, h:h + 1, :], stat_d[b + 1, h:h + 1, :]], axis=1)
                pt = jnp.exp(it["st"] + bt - lrow)
                it["ptb"] = pt.astype(BF16)
                it["dsb"] = (pt * (it["dpt"] - drow)).astype(BF16)

        pending = scores(0)
        for b in range(ns):
            rows = slice(b * BLOCK, (b + 1) * BLOCK)
            window = slice(b * BLOCK, (b + 2) * BLOCK)
            acc = {}
            items = pending
            probs(b, items)
            pending = scores(b + 1) if b + 1 < ns else None
            for p in range(npair):
                pair = items[2 * p:2 * p + 2]
                lanes = slice(p * LANES, (p + 1) * LANES)
                kparts = []
                for it in pair:
                    kbt = (kt if it["ek"] == it["e"] else ktroll)[it["pk"], :, rows]
                    kparts.append(jnp.where(row_half[it["e"]], kbt, jnp.zeros_like(kbt)))
                ds_keys = jnp.concatenate([it["dsb"] for it in pair], axis=0)
                dqt[p, :, window] += _dot(jnp.concatenate(kparts, axis=1), ds_keys)
                if not gqa:
                    q_both = jnp.concatenate([it["qm"] for it in pair], axis=0)
                    do_both = jnp.concatenate([it["dom"] for it in pair], axis=0)
                    dk_ref[rows, lanes] = _dot(jnp.concatenate([it["dsb"] for it in pair], axis=1), q_both).astype(BF16)
                    dv_ref[rows, lanes] = _dot(jnp.concatenate([it["ptb"] for it in pair], axis=1), do_both).astype(BF16)
                else:
                    for it in pair:
                        dv_c = _dot(it["ptb"], it["dom"])
                        dk_c = _dot(it["dsb"], it["qm"])
                        key = (it["pk"], it["ek"] == it["e"])
                        if key in acc:
                            acc[key] = (acc[key][0] + dk_c, acc[key][1] + dv_c)
                        else:
                            acc[key] = (dk_c, dv_c)
            if gqa:
                dk_al, dv_al = acc[(0, True)]
                dk_mis, dv_mis = acc[(0, False)]
                dk_ref[rows, :] = (dk_al + pltpu.roll(dk_mis, HEAD_DIM, 1)).astype(BF16)
                dv_ref[rows, :] = (dv_al + pltpu.roll(dv_mis, HEAD_DIM, 1)).astype(BF16)

        for p in range(npair):
            dq_ref[:, p * LANES:(p + 1) * LANES] = dqt[p, :, 0:tq].T.astype(BF16)

    main = lambda w: pl.BlockSpec((None, tq, w), lambda r, i: (r, i, 0))
    nxt = lambda w: pl.BlockSpec((None, BLOCK, w), lambda r, i: (r, jnp.minimum((i + 1) * ns, nblocks - 1), 0))
    scratch = [pltpu.VMEM((ns + 1, 8, LANES), F32), pltpu.VMEM((ns + 1, 8, LANES), F32),
               pltpu.VMEM((npair, LANES, tq + BLOCK), F32), pltpu.VMEM((wk // LANES, LANES, tq), BF16)]
    if gqa:
        scratch = scratch + [pltpu.VMEM((tq, wk), BF16)] * 2 + [pltpu.VMEM((1, LANES, tq), BF16)]
    return pl.pallas_call(
        body, name=name, grid=(dil, nt),
        in_specs=[main(wq), nxt(wq), main(wq), nxt(wq), main(LANES), nxt(LANES), main(LANES), nxt(LANES),
                  main(wk), main(wk), pl.BlockSpec(bias.shape, lambda r, i: (0, 0))],
        out_specs=[main(wq), main(wk), main(wk)],
        out_shape=[jax.ShapeDtypeStruct((dil, length, wq), BF16), jax.ShapeDtypeStruct((dil, length, wk), BF16),
                   jax.ShapeDtypeStruct((dil, length, wk), BF16)],
        scratch_shapes=scratch,
        compiler_params=pltpu.CompilerParams(dimension_semantics=("arbitrary", "arbitrary")),
    )(q, q, do, do, lse, lse, delta, delta, k, v, bias)


def _mem_attn_fwd(q, mk, mv):
    seq = q.shape[0]
    tq = min(ATTN_TILE, seq)
    sub_rows = min(4 * BLOCK, tq)
    ns = tq // sub_rows

    def body(q_ref, mk_ref, mv_ref, o_ref, lse_ref):
        half = _half_masks(sub_rows)

        def sub(a, carry):
            r0 = pl.multiple_of(a * sub_rows, sub_rows)
            scores = []
            for p in range(C_W // LANES):
                lanes = slice(p * LANES, (p + 1) * LANES)
                qp = q_ref[pl.ds(r0, sub_rows), lanes]
                for e in range(2):
                    scores.append(_dot_nt(jnp.where(half[e], qp, jnp.zeros_like(qp)), mk_ref[:, lanes]))
            m_cols, l_cols, probs = [], [], []
            for s in scores:
                m = jnp.max(s, axis=1, keepdims=True)
                pe = jnp.exp(s - m)
                probs.append(pe.astype(BF16))
                m_cols.append(m)
                l_cols.append(jnp.sum(pe, axis=1, keepdims=True))
            for p in range(C_W // LANES):
                lanes = slice(p * LANES, (p + 1) * LANES)
                o_h = [_dot(probs[2 * p + e], mv_ref[:, lanes]) * (1.0 / l_cols[2 * p + e]) for e in range(2)]
                o_ref[pl.ds(r0, sub_rows), lanes] = jnp.where(half[0], o_h[0], o_h[1]).astype(BF16)
            lse_ref[pl.ds(r0, sub_rows), :] = _per_head(m_cols) + jnp.log(_per_head(l_cols, 1.0))
            return carry

        lax.fori_loop(0, ns, sub, 0, unroll=True)

    row = lambda w: pl.BlockSpec((tq, w), lambda i: (i, 0))
    full = pl.BlockSpec((N_MEM, C_W), lambda i: (0, 0))
    return pl.pallas_call(
        body, name="mem_attn_fwd", grid=(seq // tq,), in_specs=[row(C_W), full, full],
        out_specs=[row(C_W), row(LANES)],
        out_shape=[jax.ShapeDtypeStruct((seq, C_W), BF16), jax.ShapeDtypeStruct((seq, LANES), F32)],
    )(q, mk, mv)


def _mem_attn_bwd(q, mk, mv, do, lse, delta):
    seq = q.shape[0]
    tq = min(ATTN_TILE, seq)
    ns = tq // BLOCK
    npair = C_W // LANES

    def body(q_ref, mk_ref, mv_ref, do_ref, lse_ref, dl_ref, dq_ref, dmk_ref, dmv_ref, stat_l, stat_d, mkt, dqt):
        @pl.when(pl.program_id(0) == 0)
        def _():
            dmk_ref[...] = jnp.zeros_like(dmk_ref)
            dmv_ref[...] = jnp.zeros_like(dmv_ref)
            for p in range(npair):
                mkt[p] = mk_ref[:, p * LANES:(p + 1) * LANES].astype(F32).T.astype(BF16)

        for a in range(ns):
            rows = slice(a * BLOCK, (a + 1) * BLOCK)
            stat_l[a] = _rows_to_lanes(lse_ref[rows, :])
            stat_d[a] = _rows_to_lanes(dl_ref[rows, :])
        span = min(2, ns)
        half = _half_masks(span * BLOCK)
        row = lax.broadcasted_iota(jnp.int32, (LANES, N_MEM), 0)
        row_half = (row < HEAD_DIM, row >= HEAD_DIM)

        for a in range(0, ns, span):
            rows = slice(a * BLOCK, (a + span) * BLOCK)
            items = []
            for p in range(npair):
                lanes = slice(p * LANES, (p + 1) * LANES)
                qp = q_ref[rows, lanes]
                dop = do_ref[rows, lanes]
                for e in range(2):
                    qm = jnp.where(half[e], qp, jnp.zeros_like(qp))
                    dom = jnp.where(half[e], dop, jnp.zeros_like(dop))
                    items.append(dict(p=p, e=e, qm=qm, dom=dom, st=_dot_nt(mk_ref[:, lanes], qm),
                                      dpt=_dot_nt(mv_ref[:, lanes], dom)))
            for it in items:
                h = 2 * it["p"] + it["e"]
                lrow = jnp.concatenate([stat_l[a + k, h:h + 1, :] for k in range(span)], axis=1)
                drow = jnp.concatenate([stat_d[a + k, h:h + 1, :] for k in range(span)], axis=1)
                pt = jnp.exp(it["st"] - lrow)
                it["ptb"] = pt.astype(BF16)
                it["dsb"] = (pt * (it["dpt"] - drow)).astype(BF16)
            for p in range(npair):
                lanes = slice(p * LANES, (p + 1) * LANES)
                pair = [it for it in items if it["p"] == p]
                join = lambda name, axis: jnp.concatenate([it[name] for it in pair], axis=axis)
                dmv_ref[:, lanes] += _dot(join("ptb", 1), join("dom", 0))
                dmk_ref[:, lanes] += _dot(join("dsb", 1), join("qm", 0))
                kbt = mkt[p]
                k_both = jnp.concatenate([jnp.where(row_half[e], kbt, jnp.zeros_like(kbt)) for e in range(2)], axis=1)
                dqt[p, :, rows] = _dot(k_both, join("dsb", 0))
        for p in range(npair):
            dq_ref[:, p * LANES:(p + 1) * LANES] = dqt[p].T.astype(BF16)

    row = lambda w: pl.BlockSpec((tq, w), lambda i: (i, 0))
    full = pl.BlockSpec((N_MEM, C_W), lambda i: (0, 0))
    return pl.pallas_call(
        body, name="mem_attn_bwd", grid=(seq // tq,),
        in_specs=[row(C_W), full, full, row(C_W), row(LANES), row(LANES)], out_specs=[row(C_W), full, full],
        out_shape=[jax.ShapeDtypeStruct((seq, C_W), BF16), jax.ShapeDtypeStruct((N_MEM, C_W), F32),
                   jax.ShapeDtypeStruct((N_MEM, C_W), F32)],
        scratch_shapes=[pltpu.VMEM((ns, 8, LANES), F32)] * 2
        + [pltpu.VMEM((npair, LANES, N_MEM), BF16), pltpu.VMEM((npair, LANES, tq), F32)],
        compiler_params=pltpu.CompilerParams(dimension_semantics=("arbitrary",)),
    )(q, mk, mv, do, lse, delta)


def _silu_and_grad(g):
    s = 1.0 / (1.0 + jnp.exp(-g))
    return g * s, s * (1.0 + g * (1.0 - s))


def _post(x, target, post_norm, w_out, sink_row, oa, lse_a, ga, ob_list, lseb_list, gb, oc, gc):
    seq = x.shape[0]
    tm = min(ROW_TILE, seq)
    inv_d = 1.0 / D_MODEL
    nd = len(B_DILS)

    def body(*refs):
        (x_ref, t_ref, gp_ref, w_ref, sink_ref, oa_ref, lsea_ref, ga_ref), refs = refs[:8], refs[8:]
        ob_refs, lb_refs, (gb_ref, oc_ref, gc_ref), refs = refs[:nd], refs[nd:2 * nd], refs[2 * nd:2 * nd + 3], refs[2 * nd + 3:]
        (g_ref, doa_ref, dla_ref, dga_ref), refs = refs[:4], refs[4:]
        dob_refs, lsec_refs, dlb_refs, refs = refs[:nd], refs[nd:2 * nd], refs[2 * nd:3 * nd], refs[3 * nd:]
        (dgb_ref, doc_ref, dlc_ref, dgc_ref, gw_ref, gpost_ref, gsink_ref, loss_ref), refs = refs[:8], refs[8:]
        ycat, obufs, lbufs, st_do, st_l, st_d = refs[0], refs[1:nd], refs[nd:2 * nd - 1], refs[2 * nd - 1], refs[2 * nd], refs[2 * nd + 1]

        @pl.when(pl.program_id(0) == 0)
        def _():
            gw_ref[...] = jnp.zeros_like(gw_ref)
            gpost_ref[...] = jnp.zeros_like(gpost_ref)
            gsink_ref[...] = jnp.zeros_like(gsink_ref)
            loss_ref[...] = jnp.zeros_like(loss_ref)

        o_i, l_i = [ob_refs[0][0].astype(F32)], [lb_refs[0][0]]
        for k in range(1, nd):
            _from_residues(ob_refs[k], obufs[k - 1], B_DILS[k])
            _from_residues(lb_refs[k], lbufs[k - 1], B_DILS[k])
            o_i.append(_stage_read(obufs[k - 1]))
            l_i.append(_stage_read(lbufs[k - 1]))
        mx = l_i[0]
        for l in l_i[1:]:
            mx = jnp.maximum(mx, l)
        w_i = [jnp.exp(l - mx) for l in l_i]
        z = w_i[0]
        for w in w_i[1:]:
            z = z + w
        _stage_write(st_l, mx + jnp.log(z))
        expand = _head_expand_matrix(B_W)
        inv_z = 1.0 / z
        ob = None
        for w, o in zip(w_i, o_i):
            term = _dot_split(w * inv_z, expand, 2) * o
            ob = term if ob is None else ob + term
        oa, oc = oa_ref[...].astype(F32), oc_ref[...].astype(F32)
        sa, dsa = _silu_and_grad(ga_ref[...].astype(F32))
        sb, dsb = _silu_and_grad(gb_ref[...].astype(F32))
        sc, dsc = _silu_and_grad(gc_ref[...].astype(F32))
        ycat[:, 0:A_W] = (oa * sa).astype(BF16)
        ycat[:, A_W:A_W + B_W] = (ob * sb).astype(BF16)
        ycat[:, A_W + B_W:] = (oc * sc).astype(BF16)
        y2 = _dot(ycat[...], w_ref[...])
        r = lax.rsqrt(jnp.mean(y2 * y2, axis=-1, keepdims=True) + RMS_EPS)
        zhat = y2 * r
        gp = gp_ref[...]
        err = x_ref[...] + zhat * gp - t_ref[...]
        loss_ref[...] += jnp.sum(err * err) * (0.5 * inv_d)
        g = err * inv_d
        g_ref[...] = g
        gpost_ref[...] += jnp.sum(g * zhat, axis=0, keepdims=True)
        a = g * gp
        dy2 = (r * (a - zhat * jnp.mean(a * zhat, axis=-1, keepdims=True))).astype(BF16)
        gw_ref[...] += _dot_tn(ycat[...], dy2)
        dycat = _dot_nt(dy2, w_ref[...])
        dya, dyb, dyc = dycat[:, 0:A_W], dycat[:, A_W:A_W + B_W], dycat[:, A_W + B_W:]
        doa, dob, doc = dya * sa, dyb * sb, dyc * sc
        doa_ref[...] = doa.astype(BF16)
        doc_ref[...] = doc.astype(BF16)
        dga_ref[...] = (dya * oa * dsa).astype(BF16)
        dgb_ref[...] = (dyb * ob * dsb).astype(BF16)
        dgc_ref[...] = (dyc * oc * dsc).astype(BF16)
        dl_a = _dot_split(doa * oa, _head_sum_matrix(A_W), 2)
        dla_ref[...] = dl_a
        dlc_ref[...] = _dot_split(doc * oc, _head_sum_matrix(C_W), 2)
        gsink_ref[...] += jnp.sum(jnp.exp(sink_ref[...] - lsea_ref[...]) * dl_a, axis=0, keepdims=True)
        _stage_write(st_do, dob)
        _stage_write(st_d, _dot_split(dob * ob, _head_sum_matrix(B_W), 2))
        for k, dil in enumerate(B_DILS):
            _to_residues(st_do, dob_refs[k], dil)
            _to_residues(st_l, lsec_refs[k], dil)
            _to_residues(st_d, dlb_refs[k], dil)

    row = lambda w: pl.BlockSpec((tm, w), lambda i: (i, 0))
    full = lambda shape: pl.BlockSpec(shape, lambda i: (0,) * len(shape))
    res_specs = lambda w: [_residue_spec(d, tm, w) for d in B_DILS]
    res_shapes = lambda w, dt: [jax.ShapeDtypeStruct((d, seq // d, w), dt) for d in B_DILS]
    ins = [x, target, post_norm, w_out, sink_row, oa, lse_a, ga, *ob_list, *lseb_list, gb, oc, gc]
    in_specs = ([row(D_MODEL), row(D_MODEL), full((1, D_MODEL)), full((D_MODEL, D_MODEL)), full((1, LANES)),
                 row(A_W), row(LANES), row(A_W)] + res_specs(B_W) + res_specs(LANES) + [row(B_W), row(C_W), row(C_W)])
    out_shape = ([jax.ShapeDtypeStruct((seq, D_MODEL), F32), jax.ShapeDtypeStruct((seq, A_W), BF16),
                  jax.ShapeDtypeStruct((seq, LANES), F32), jax.ShapeDtypeStruct((seq, A_W), BF16)]
                 + res_shapes(B_W, BF16) + res_shapes(LANES, F32) + res_shapes(LANES, F32)
                 + [jax.ShapeDtypeStruct((seq, B_W), BF16), jax.ShapeDtypeStruct((seq, C_W), BF16),
                    jax.ShapeDtypeStruct((seq, LANES), F32), jax.ShapeDtypeStruct((seq, C_W), BF16),
                    jax.ShapeDtypeStruct((D_MODEL, D_MODEL), F32), jax.ShapeDtypeStruct((1, D_MODEL), F32),
                    jax.ShapeDtypeStruct((1, LANES), F32), jax.ShapeDtypeStruct((1, LANES), F32)])
    out_specs = ([row(D_MODEL), row(A_W), row(LANES), row(A_W)] + res_specs(B_W) + res_specs(LANES) + res_specs(LANES)
                 + [row(B_W), row(C_W), row(LANES), row(C_W),
                    full((D_MODEL, D_MODEL)), full((1, D_MODEL)), full((1, LANES)), full((1, LANES))])
    scratch = ([pltpu.VMEM((tm, D_MODEL), BF16)] + [_stage(tm, B_W)] * (nd - 1) + [_stage(tm, LANES)] * (nd - 1)
               + [_stage(tm, B_W), _stage(tm, LANES), _stage(tm, LANES)])
    res = pl.pallas_call(
        body, name="post", grid=(seq // tm,), in_specs=in_specs, out_specs=out_specs, out_shape=out_shape,
        scratch_shapes=scratch,
        compiler_params=pltpu.CompilerParams(dimension_semantics=("arbitrary",)),
    )(*ins)
    out = dict(g=res[0], doa=res[1], dl_a=res[2], dga=res[3], dob=res[4:4 + nd], lse_b=res[4 + nd:4 + 2 * nd],
               dl_b=res[4 + 2 * nd:4 + 3 * nd])
    rest = res[4 + 3 * nd:]
    out.update(dgb=rest[0], doc=rest[1], dl_c=rest[2], dgc=rest[3], gw_out=rest[4], gpost=rest[5], gsink=rest[6],
               loss=rest[7])
    return out


def _grad_w_in(ut, nat, res):
    tm = ut.shape[2]
    seq = ut.shape[0] * tm
    nd = len(B_DILS)
    nat_list = [nat[n] for n in _NATURAL]
    res_list = [a for n in _DILATED for a in res[n]]
    rope = _rope_tables(seq, tm)

    def body(rl_ref, rb_ref, ut_ref, *refs):
        nat_refs = dict(zip(_NATURAL, refs[:len(_NATURAL)]))
        refs = refs[len(_NATURAL):]
        res_refs = {n: refs[nd * k:nd * (k + 1)] for k, n in enumerate(_DILATED)}
        refs = refs[nd * len(_DILATED):]
        dproj_ref, gw_ref = refs[:2]
        bufs = {n: refs[2 + (nd - 1) * k:2 + (nd - 1) * (k + 1)] for k, n in enumerate(_DILATED)}

        @pl.when(pl.program_id(0) == 0)
        def _():
            gw_ref[...] = jnp.zeros_like(gw_ref)

        c, sm, sp = _rope_coeffs(rl_ref, rb_ref)
        sm, sp = -sm, -sp
        per_shard = SHARD_IN // LANES
        interleaved = False
        for j in (3, 0, 1, 2):
            for blk in range(per_shard * j, per_shard * (j + 1)):
                name, off, roped, scaled = _PROJ_LAYOUT[blk]
                lanes = slice(off, off + LANES)
                if name in nat_refs:
                    piece = nat_refs[name][:, lanes].astype(F32)
                else:
                    if not interleaved:
                        for n in _DILATED:
                            for k in range(1, nd):
                                _from_residues(res_refs[n][k], bufs[n][k - 1], B_DILS[k])
                        interleaved = True
                    piece = res_refs[name][0][0, :, lanes].astype(F32)
                    for buf in bufs[name]:
                        piece = piece + buf[off // LANES]
                if roped:
                    piece = _rope(piece, c, sm, sp)
                if scaled:
                    piece = piece * SCALE
                dproj_ref[:, blk * LANES:(blk + 1) * LANES] = piece.astype(BF16)
            gw_ref[j] += _dot(ut_ref[...], dproj_ref[:, j * SHARD_IN:(j + 1) * SHARD_IN])

    row = lambda w: pl.BlockSpec((tm, w), lambda i: (i, 0))
    in_specs = ([pl.BlockSpec(rope[0].shape, lambda i: (0, 0)), pl.BlockSpec((8, 2 * LANES), lambda i: (i, 0)),
                 pl.BlockSpec((None, D_MODEL, tm), lambda i: (i, 0, 0))]
                + [row(a.shape[1]) for a in nat_list]
                + [_residue_spec(d, tm, B_W) for _ in _DILATED for d in B_DILS])
    return pl.pallas_call(
        body, name="grad_w_in", grid=(seq // tm,), in_specs=in_specs,
        out_specs=[row(D_IN), pl.BlockSpec((N_CHIPS, D_MODEL, SHARD_IN), lambda i: (0, 0, 0))],
        out_shape=[jax.ShapeDtypeStruct((seq, D_IN), BF16), jax.ShapeDtypeStruct((N_CHIPS, D_MODEL, SHARD_IN), F32)],
        scratch_shapes=[_stage(tm, B_W)] * ((nd - 1) * len(_DILATED)),
        compiler_params=pltpu.CompilerParams(dimension_semantics=("arbitrary",)),
    )(*rope, ut, *nat_list, *res_list)


def _input_grad(x, g, pre_norm, w_in_g, dproj, gx_prev, span, after, name):
    seq = x.shape[0]
    tm = seq // INPUT_GRAD_TILES
    first_block, steps = span

    def body(*refs):
        x_ref, g_ref, gp_ref, w_ref, dp_ref = refs[:5]
        gx_ref, gpre_ref = refs[-2:]

        @pl.when(pl.program_id(0) == 0)
        def _():
            gpre_ref[...] = jnp.zeros_like(gpre_ref)

        du = None
        for j in range(N_CHIPS):
            term = _dot_nt(dp_ref[:, j * SHARD_IN:(j + 1) * SHARD_IN], w_ref[j])
            du = term if du is None else du + term
        xv = x_ref[...]
        r = lax.rsqrt(jnp.mean(xv * xv, axis=-1, keepdims=True) + RMS_EPS)
        xhat = xv * r
        gpre_ref[...] += jnp.sum(du * xhat, axis=0, keepdims=True)
        a = du * gp_ref[...]
        gx_ref[...] = g_ref[...] + r * (a - xhat * jnp.mean(a * xhat, axis=-1, keepdims=True))

    row = lambda w: pl.BlockSpec((tm, w), lambda i: (first_block + i, 0))
    full = lambda a: pl.BlockSpec(a.shape, lambda i: (0,) * a.ndim)
    any_spec = pl.BlockSpec(memory_space=pl.ANY)
    ins = [x, g, pre_norm, w_in_g, dproj]
    in_specs = [row(D_MODEL), row(D_MODEL), full(pre_norm), full(w_in_g), row(D_IN)]
    aliases = {}
    if gx_prev is not None:
        aliases[len(ins)] = 0
        ins.append(gx_prev)
        in_specs.append(any_spec)
    if after is not None:
        ins.append(after)
        in_specs.append(any_spec)
    return pl.pallas_call(
        body, name=name, grid=(steps,), in_specs=in_specs,
        out_specs=[row(D_MODEL), pl.BlockSpec((1, D_MODEL), lambda i: (0, 0))],
        out_shape=[jax.ShapeDtypeStruct((seq, D_MODEL), F32), jax.ShapeDtypeStruct((1, D_MODEL), F32)],
        input_output_aliases=aliases,
        compiler_params=pltpu.CompilerParams(dimension_semantics=("arbitrary",)),
    )(*ins)


def _exchange_start(ex, name):
    n_in, n_out, n_sem = len(ex["ins"]), len(ex["outs"]), len(ex["sems"])

    def body(*refs):
        in_refs, land_refs, sems = refs[:n_in], refs[n_in:n_in + n_out], refs[n_in + n_out:n_in + n_out + n_sem]
        ex["start"](in_refs, land_refs, *sems)
        token = refs[-1]
        token[...] = jnp.zeros_like(token)

    hbm = pl.BlockSpec(memory_space=pltpu.HBM)
    sem = pl.BlockSpec(memory_space=pltpu.SEMAPHORE)
    ins = [pltpu.with_memory_space_constraint(a, pltpu.HBM) for a in ex["ins"]]
    landing = [pltpu.with_memory_space_constraint(lax.empty(o.shape, o.dtype), pltpu.HBM) for o in ex["outs"]]
    res = pl.pallas_call(
        body, name=name,
        out_shape=list(ex["sems"]) + [pltpu.HBM(a.shape, a.dtype) for a in ex["ins"]]
        + [pltpu.HBM(o.shape, o.dtype) for o in ex["outs"]] + [jax.ShapeDtypeStruct((8, LANES), F32)],
        in_specs=[hbm] * (n_in + n_out),
        out_specs=[sem] * n_sem + [hbm] * (n_in + n_out) + [pl.BlockSpec(memory_space=pltpu.VMEM)],
        input_output_aliases={k: n_sem + k for k in range(n_in + n_out)},
        compiler_params=pltpu.CompilerParams(has_side_effects=pltpu.SideEffectType.DATAFLOW_SIDE_EFFECTING),
    )(*ins, *landing)
    return res[:-1], res[-1]


def _exchange_wait(ex, handles, after, name):
    n_in, n_out, n_sem = len(ex["ins"]), len(ex["outs"]), len(ex["sems"])
    sems, thru = handles[:n_sem], handles[n_sem:]

    def body(*refs):
        in_refs, land_refs = refs[:n_in], refs[n_in:n_in + n_out]
        sem_refs = refs[n_in + n_out:n_in + n_out + n_sem]
        ex["finish"](in_refs, land_refs, *sem_refs)

    hbm = pl.BlockSpec(memory_space=pltpu.HBM)
    sem = pl.BlockSpec(memory_space=pltpu.SEMAPHORE)
    res = pl.pallas_call(
        body, name=name,
        out_shape=[pltpu.HBM(a.shape, a.dtype) for a in thru],
        in_specs=[hbm] * (n_in + n_out) + [sem] * n_sem + [pl.BlockSpec(memory_space=pl.ANY)],
        out_specs=[hbm] * (n_in + n_out),
        input_output_aliases={k: k for k in range(n_in + n_out)},
        compiler_params=pltpu.CompilerParams(has_side_effects=pltpu.SideEffectType.DATAFLOW_SIDE_EFFECTING),
    )(*thru, *sems, after)
    return res[:n_in], res[n_in:]


def _start_finish(build):
    def start(*refs):
        for cp in build(*refs):
            cp.start()

    def finish(*refs):
        for cp in build(*refs):
            cp.wait()

    return dict(start=start, finish=finish)


def _pair_exchange(grads):
    n = len(grads)

    def build(srcs, outs, send_sems, recv_sems):
        x, y, c = lax.axis_index("x"), lax.axis_index("y"), lax.axis_index("c")
        copies = []
        for t in range(n):
            rows = grads[t].shape[1] // 2
            copies.append(pltpu.make_async_remote_copy(
                src_ref=srcs[t].at[:, pl.ds((1 - c) * rows, rows)], dst_ref=outs[t],
                send_sem=send_sems.at[t], recv_sem=recv_sems.at[t], device_id=(x, y, 1 - c), device_id_type=MESH))
        return copies

    return dict(ins=list(grads), **_start_finish(build),
                outs=[jax.ShapeDtypeStruct((g.shape[0], g.shape[1] // 2, g.shape[2]), g.dtype) for g in grads],
                sems=[pltpu.SemaphoreType.DMA((n,)), pltpu.SemaphoreType.DMA((n,))])


def _pair_add(core, owns, gots):
    n = len(owns)

    def body(core_ref, *refs):
        for t in range(n):
            refs[2 * n + t][...] = (refs[t][...] + refs[n + t][...]).astype(BF16)

    halves = [(None,) + g.shape[1:] for g in gots]
    grid_spec = pltpu.PrefetchScalarGridSpec(
        num_scalar_prefetch=1, grid=(N_CHIPS,),
        in_specs=[pl.BlockSpec(h, lambda k, core_ref: (k, core_ref[0], 0)) for h in halves]
        + [pl.BlockSpec(h, lambda k, core_ref: (k, 0, 0)) for h in halves],
        out_specs=[pl.BlockSpec(h, lambda k, core_ref: (k, 0, 0)) for h in halves])
    return pl.pallas_call(
        body, name="pair_add", grid_spec=grid_spec,
        out_shape=[jax.ShapeDtypeStruct(g.shape, BF16) for g in gots],
    )(core, *owns, *gots)


def _chip_exchange(parts):
    n = len(parts)

    def build(srcs, outs, send_sems, recv_sems, local_sems):
        x, y, c = lax.axis_index("x"), lax.axis_index("y"), lax.axis_index("c")
        my_chip = 2 * x + y
        chips = [(1 - x, y), (x, 1 - y), (1 - x, 1 - y)]
        copies = [pltpu.make_async_copy(srcs[t].at[my_chip], outs[t].at[my_chip], local_sems.at[t]) for t in range(n)]
        for j, (cx, cy) in enumerate(chips):
            for t in range(n):
                k = n * j + t
                copies.append(pltpu.make_async_remote_copy(
                    src_ref=srcs[t].at[2 * cx + cy], dst_ref=outs[t].at[my_chip], send_sem=send_sems.at[k],
                    recv_sem=recv_sems.at[k], device_id=(cx, cy, c), device_id_type=MESH))
        return copies

    return dict(ins=list(parts), **_start_finish(build), outs=[jax.ShapeDtypeStruct(p.shape, p.dtype) for p in parts],
                sems=[pltpu.SemaphoreType.DMA((3 * n,)), pltpu.SemaphoreType.DMA((3 * n,)),
                      pltpu.SemaphoreType.DMA((n,))])


def _chip_sum(core, slots):
    n = len(slots)

    def body(core_ref, *refs):
        for t in range(n):
            acc = refs[t][0].astype(F32)
            for s in range(1, N_CHIPS):
                acc = acc + refs[t][s].astype(F32)
            refs[n + t][...] = acc

    blocks = [(s.shape[1] // TAIL_STEPS, s.shape[2]) for s in slots]
    grid_spec = pltpu.PrefetchScalarGridSpec(
        num_scalar_prefetch=1, grid=(TAIL_STEPS,),
        in_specs=[pl.BlockSpec((N_CHIPS,) + b, lambda i, core_ref: (0, i, 0)) for b in blocks],
        out_specs=[pl.BlockSpec((None,) + b, lambda i, core_ref: (core_ref[0], i, 0)) for b in blocks])
    return pl.pallas_call(
        body, name="chip_sum", grid_spec=grid_spec,
        out_shape=[jax.ShapeDtypeStruct((2,) + s.shape[1:], F32) for s in slots],
    )(core, *slots)


def _pair_gather(bufs, small):
    n = len(bufs)

    def body(*refs):
        small_ref, outs, small_out = refs[n], refs[n + 1:2 * n + 1], refs[2 * n + 1]
        send_sems, recv_sems, local_sem = refs[2 * n + 2:]
        x, y, c = lax.axis_index("x"), lax.axis_index("y"), lax.axis_index("c")
        me = 4 * x + 2 * y + c
        chips = [(1 - x, y), (x, 1 - y), (1 - x, 1 - y)]
        mine = pltpu.make_async_copy(small_ref, small_out.at[me], local_sem)
        mine.start()
        copies = [pltpu.make_async_remote_copy(
            src_ref=outs[t].at[c], dst_ref=outs[t].at[c], send_sem=send_sems.at[t], recv_sem=recv_sems.at[t],
            device_id=(x, y, 1 - c), device_id_type=MESH) for t in range(n)]
        peers = [(x, y, 1 - c)] + [(cx, cy, cc) for (cx, cy) in chips for cc in (c, 1 - c)]
        for j, peer in enumerate(peers):
            copies.append(pltpu.make_async_remote_copy(
                src_ref=small_ref, dst_ref=small_out.at[me], send_sem=send_sems.at[n + j],
                recv_sem=recv_sems.at[n + j], device_id=peer, device_id_type=MESH))
        for cp in copies:
            cp.start()
        for cp in copies:
            cp.wait()
        mine.wait()

    any_spec = pl.BlockSpec(memory_space=pl.ANY)
    res = pl.pallas_call(
        body, name="pair_gather",
        out_shape=[jax.ShapeDtypeStruct(b.shape, b.dtype) for b in bufs]
        + [jax.ShapeDtypeStruct((8,) + small.shape, small.dtype)],
        in_specs=[any_spec] * (n + 1), out_specs=[any_spec] * (n + 1),
        input_output_aliases={t: t for t in range(n)},
        scratch_shapes=[pltpu.SemaphoreType.DMA((n + 7,)), pltpu.SemaphoreType.DMA((n + 7,)),
                        pltpu.SemaphoreType.DMA],
    )(*bufs, small)
    return [r.reshape(2 * b.shape[1], b.shape[2]) for r, b in zip(res[:n], bufs)], res[n]


def _adamw(ws, gs, ms, vs):
    n = len(ws)

    def body(*refs):
        for t in range(n):
            w_ref, g_ref, m_ref, v_ref = refs[t:4 * n:n]
            gout_ref, d_ref, nm_ref, nv_ref = refs[4 * n + t::n]
            g = g_ref[...]
            gout_ref[...] = g
            d_ref[...], nm_ref[...], nv_ref[...] = _adamw_math(w_ref[...], g, m_ref[...], v_ref[...])

    specs = [pl.BlockSpec((w.shape[0] // TAIL_STEPS, w.shape[1]), lambda i: (i, 0)) for w in ws]
    res = pl.pallas_call(
        body, name="adamw", grid=(TAIL_STEPS,), in_specs=specs * 4, out_specs=specs * 4,
        out_shape=[jax.ShapeDtypeStruct(w.shape, F32) for w in ws] * 4,
    )(*ws, *gs, *ms, *vs)
    return [res[t::n] for t in range(n)]


def _adamw_math(w, g, m, v):
    c1 = 1.0 / (1.0 - ADAM_B1 ** ADAM_STEP)
    c2 = 1.0 / (1.0 - ADAM_B2 ** ADAM_STEP)
    nm = ADAM_B1 * m + (1.0 - ADAM_B1) * g
    nv = ADAM_B2 * v + (1.0 - ADAM_B2) * (g * g)
    return -ADAM_LR * ((nm * c1) / (jnp.sqrt(nv * c2) + ADAM_EPS) + ADAM_WD * w), nm, nv


def _small_update(slots, params, ms, vs):
    n = len(params)

    def body(slots_ref, *refs):
        w_refs, m_refs, v_refs, loss_ref = refs[:n], refs[n:2 * n], refs[2 * n:3 * n], refs[3 * n]
        g_refs, d_refs, nm_refs, nv_refs = (refs[3 * n + 1 + k * n:3 * n + 1 + (k + 1) * n] for k in range(4))
        acc = slots_ref[0]
        for s in range(1, slots.shape[0]):
            acc = acc + slots_ref[s]
        loss_ref[...] = acc[4:5, 0:1]
        grads = (acc[0:1] + acc[5:6], acc[3:4], acc[2:3], acc[1:2])
        for k in range(n):
            g = grads[k][:, :w_refs[k].shape[1]]
            g_refs[k][...] = g
            d_refs[k][...], nm_refs[k][...], nv_refs[k][...] = _adamw_math(w_refs[k][...], g, m_refs[k][...],
                                                                           v_refs[k][...])

    return pl.pallas_call(
        body, name="small_update",
        out_shape=[jax.ShapeDtypeStruct((1, 1), F32)] + [jax.ShapeDtypeStruct(p.shape, F32) for p in params] * 4,
    )(slots, *params, *ms, *vs)


def _local_step(x, mem, target, pre_norm, sink_a, mem_norm, post_norm, w_in_g, w_out, w_mkv, gathers=None,
                own=None):
    first_gather, late_gather, late_shards = gathers if gathers else (None, None, ())
    u, ut, p_own, hosted = _pre_norm(x, pre_norm, own[1] if own else None, first_gather)
    if gathers:
        w_in_g = hosted[0].reshape(N_CHIPS, D_MODEL, SHARD_IN)
    pr = _pre_proj(u, w_in_g, (own[0], p_own) if own else None, late_gather, late_shards)
    pr["ut"] = ut
    if gathers:
        w_out, w_mkv = (g.reshape(D_MODEL, g.shape[-1]) for g in pr["hosted"])
    mk, mv = _mem_kv(mem, mem_norm, w_mkv)
    sink = sink_a.reshape(-1)
    qa, ka, va = pr["qa"][None], pr["ka"][None], pr["va"][None]
    oa, lse_a = _band_fwd(qa, ka, va, sink, max_dist=A_WINDOW - 1, name="swa_fwd")
    ob_list, lseb_list = [], []
    for k, (win, dil) in enumerate(B_CONFIGS):
        o_i, l_i = _band_fwd(pr["qb"][k], pr["kb"][k], pr["vb"][k], None, max_dist=win // dil, name=f"dil{dil}_fwd")
        ob_list.append(o_i)
        lseb_list.append(l_i)
    oc, lse_c = _mem_attn_fwd(pr["qc"], mk, mv)
    sink_row = jnp.pad(sink, (0, LANES - sink.shape[0])).reshape(1, LANES)
    po = _post(x, target, post_norm, w_out, sink_row, oa[0], lse_a[0], pr["ga"], ob_list, lseb_list, pr["gb"], oc,
               pr["gc"])
    dqc, dmk, dmv = _mem_attn_bwd(pr["qc"], mk, mv, po["doc"], lse_c, po["dl_c"])
    dqa, dka, dva = _band_bwd(qa, ka, va, po["doa"][None], lse_a, po["dl_a"][None], max_dist=A_WINDOW - 1,
                              name="swa_bwd")
    res = dict(qb=[], kb=[], vb=[])
    for k, (win, dil) in enumerate(B_CONFIGS):
        dq_i, dk_i, dv_i = _band_bwd(pr["qb"][k], pr["kb"][k], pr["vb"][k], po["dob"][k], po["lse_b"][k],
                                     po["dl_b"][k], max_dist=win // dil, name=f"dil{dil}_bwd")
        res["qb"].append(dq_i)
        res["kb"].append(dk_i)
        res["vb"].append(dv_i)
    nat = dict(qa=dqa[0], ka=dka[0], va=dva[0], ga=po["dga"], gb=po["dgb"], qc=dqc, gc=po["dgc"])
    dproj, gw_in = _grad_w_in(pr["ut"], nat, res)
    gw_mkv, gmem = _mem_kv_bwd(mem, mem_norm, w_mkv, dmk, dmv)
    gsink = -po["gsink"][0, :sink.shape[0]]
    return dict(loss=po["loss"], g=po["g"], dproj=dproj, gw_in=gw_in, gw_out=po["gw_out"], gw_mkv=gw_mkv,
                gpost=po["gpost"], gmem=gmem, gsink=gsink, w_in_g=w_in_g)


def kernel(x, mem, pre_norm, w_in, sink_a, mem_norm, w_mem_kv, w_out, post_norm, loss_target, m_pre_norm, m_w_in, m_sink_a, m_mem_norm, m_w_mem_kv, m_w_out, m_post_norm, v_pre_norm, v_w_in, v_sink_a, v_mem_norm, v_w_mem_kv, v_w_out, v_post_norm):
    w_own = w_in[0]
    late_shards = (w_out[0], w_mem_kv[0])
    gathers = (_gather_exchange([w_own]), _gather_exchange(late_shards), late_shards)
    chip = (2 * lax.axis_index("x") + lax.axis_index("y")).astype(jnp.int32).reshape(1)
    loc = _local_step(x[0], mem[0], loss_target[0], pre_norm, sink_a, mem_norm, post_norm, None, None, None, gathers,
                      (chip, w_own))
    big = [loc["gw_in"], loc["gw_out"].reshape(N_CHIPS, D_MODEL // N_CHIPS, D_MODEL),
           loc["gw_mkv"].reshape(N_CHIPS, D_MODEL // N_CHIPS, 2 * C_W)]
    core = lax.axis_index("c").astype(jnp.int32).reshape(1)
    w_in_full = loc["w_in_g"]
    step_in = (x[0], loc["g"], pre_norm, w_in_full, loc["dproj"])
    pair_ex = _pair_exchange(big)
    pair_handles, token = _exchange_start(pair_ex, "pair_exchange_start")
    gx_a, gpre_a = _input_grad(*step_in, None, (0, 2), token, "input_grad_a")
    big, got = _exchange_wait(pair_ex, pair_handles, gpre_a, "pair_exchange_wait")
    parts = _pair_add(core, big, got)
    chip_ex = _chip_exchange(parts)
    chip_handles, token = _exchange_start(chip_ex, "chip_exchange_start")
    grad_x, gpre_b = _input_grad(*step_in, gx_a, (2, 14), token, "input_grad_b")
    _, slots = _exchange_wait(chip_ex, chip_handles, gpre_b, "chip_exchange_wait")
    halves = _chip_sum(core, slots)
    widen = lambda a: jnp.pad(a.reshape(1, -1), ((0, 0), (0, D_MODEL - a.size)))
    small = jnp.concatenate([gpre_a, loc["gpost"], loc["gmem"], widen(loc["gsink"]), widen(loc["loss"]), gpre_b,
                             jnp.zeros((2, D_MODEL), F32)], axis=0)
    (g_in, g_out, g_mkv), small_slots = _pair_gather(halves, small)
    (loss, g_pre, g_sink, g_mem, g_post, d_pre, d_sink, d_mem, d_post, nm_pre, nm_sink, nm_mem, nm_post,
     nv_pre, nv_sink, nv_mem, nv_post) = _small_update(
        small_slots, (pre_norm, sink_a, mem_norm, post_norm), (m_pre_norm, m_sink_a, m_mem_norm, m_post_norm),
        (v_pre_norm, v_sink_a, v_mem_norm, v_post_norm))

    (g_in, d_in, nm_in, nv_in), (g_out, d_out, nm_out, nv_out), (g_mkv, d_mkv, nm_mkv, nv_mkv) = _adamw(
        (w_in[0], w_out[0], w_mem_kv[0]), (g_in, g_out, g_mkv), (m_w_in[0], m_w_out[0], m_w_mem_kv[0]),
        (v_w_in[0], v_w_out[0], v_w_mem_kv[0]))
    lead = lambda a: a[None]
    return (loss.reshape(()), lead(grad_x),
            g_pre, lead(g_in), g_sink, g_mem, lead(g_mkv), lead(g_out), g_post,
            d_pre, lead(d_in), d_sink, d_mem, lead(d_mkv), lead(d_out), d_post,
            nm_pre, lead(nm_in), nm_sink, nm_mem, lead(nm_mkv), lead(nm_out), nm_post,
            nv_pre, lead(nv_in), nv_sink, nv_mem, lead(nv_mkv), lead(nv_out), nv_post)
```

```python
import numpy as np
import jax
import jax.numpy as jnp
from jax import lax
from jax.experimental import pallas as pl
from jax.experimental.pallas import tpu as pltpu

F32 = jnp.float32
BF16 = jnp.bfloat16

D_MODEL = 1024
HEAD_DIM = 64
LANES = 128
BLOCK = 128
ROW_TILE = 512
ATTN_TILE = 1024
INPUT_GRAD_TILES = 16
TAIL_STEPS = 4
A_W, A_KV_W, B_W, C_W = 384, 128, 384, 256
N_MEM = 256
D_IN = 3072
N_CHIPS = 4
SHARD_IN = D_IN // N_CHIPS
B_CONFIGS = ((128, 1), (512, 4), (2048, 16))
B_DILS = tuple(d for _, d in B_CONFIGS)
A_WINDOW = 128
RMS_EPS = 1e-6
ROPE_THETA = 500000.0
SCALE = HEAD_DIM ** -0.5
NEG = -1e30
ADAM_LR, ADAM_B1, ADAM_B2, ADAM_EPS, ADAM_WD, ADAM_STEP = 0.001, 0.9, 0.999, 1e-08, 0.01, 10

NT = (((1,), (1,)), ((), ()))
TN = (((0,), (0,)), ((), ()))
MESH = pl.DeviceIdType.MESH

_PROJ_LAYOUT = (
    [("qa", 128 * i, True, True) for i in range(3)] + [("ka", 0, True, False), ("va", 0, False, False)]
    + [("ga", 128 * i, False, False) for i in range(3)]
    + [("qb", 128 * i, True, True) for i in range(3)] + [("kb", 128 * i, True, False) for i in range(3)]
    + [("vb", 128 * i, False, False) for i in range(3)] + [("gb", 128 * i, False, False) for i in range(3)]
    + [("qc", 128 * i, False, True) for i in range(2)] + [("gc", 128 * i, False, False) for i in range(2)]
)
_PROJ_WIDTH = dict(qa=A_W, ka=A_KV_W, va=A_KV_W, ga=A_W, qb=B_W, kb=B_W, vb=B_W, gb=B_W, qc=C_W, gc=C_W)
_NATURAL = ("qa", "ka", "va", "ga", "gb", "qc", "gc")
_DILATED = ("qb", "kb", "vb")


def _dot(a, b):
    return jnp.dot(a, b, preferred_element_type=F32)


def _dot_nt(a, b):
    return lax.dot_general(a, b, NT, preferred_element_type=F32)


def _dot_tn(a, b):
    return lax.dot_general(a, b, TN, preferred_element_type=F32)


def _half_masks(rows):
    lane = lax.broadcasted_iota(jnp.int32, (rows, LANES), 1)
    return lane < HEAD_DIM, lane >= HEAD_DIM


def _rope(t, c, sm, sp):
    return t * c + pltpu.roll(t, LANES - 8, 1) * sm + pltpu.roll(t, 8, 1) * sp


def _rope_tables(seq, tm):
    dim = np.arange(LANES) % HEAD_DIM
    inv_freq = (np.float32(ROPE_THETA) ** (-np.arange(0, 16, 2, dtype=np.float32) / np.float32(16))).astype(np.float64)
    freq = np.where(dim < 16, inv_freq[dim % 8], 0.0)[None, :]
    local = np.arange(tm, dtype=np.float64)[:, None] * freq
    base = (np.arange(seq // tm, dtype=np.float64) * tm)[:, None] * freq
    both = lambda a: np.concatenate([np.cos(a), np.sin(a)], axis=1).astype(np.float32)
    return jnp.asarray(both(local)), jnp.asarray(np.repeat(both(base), 8, axis=0))


def _rope_coeffs(local_ref, base_ref):
    cl, sl = local_ref[:, :LANES], local_ref[:, LANES:]
    cb, sb = base_ref[0:1, :LANES], base_ref[0:1, LANES:]
    cos = cb * cl - sb * sl
    sin = sb * cl + cb * sl
    dim = lax.broadcasted_iota(jnp.int32, (1, LANES), 1) % HEAD_DIM
    return cos, jnp.where(dim < 8, -sin, 0.0), jnp.where((dim >= 8) & (dim < 16), sin, 0.0)


def _split3(x):
    a = x.astype(BF16)
    r = x - a.astype(F32)
    b = r.astype(BF16)
    c = (r - b.astype(F32)).astype(BF16)
    return a, b, c


def _rows_to_lanes(x):
    row = lax.broadcasted_iota(jnp.int32, (8, LANES), 0)
    lane = lax.broadcasted_iota(jnp.int32, (8, LANES), 1)
    eye = (row == lane).astype(BF16)
    a, b, c = _split3(x)
    return _dot_nt(eye, a) + _dot_nt(eye, b) + _dot_nt(eye, c)


def _head_sum_matrix(width):
    k = lax.broadcasted_iota(jnp.int32, (width, LANES), 0)
    h = lax.broadcasted_iota(jnp.int32, (width, LANES), 1)
    return (k // HEAD_DIM == h).astype(BF16)


def _head_expand_matrix(width):
    h = lax.broadcasted_iota(jnp.int32, (LANES, width), 0)
    k = lax.broadcasted_iota(jnp.int32, (LANES, width), 1)
    return (k // HEAD_DIM == h).astype(BF16)


def _dot_split(x, mat, terms):
    parts = _split3(x)[:terms]
    out = _dot(parts[0], mat)
    for p in parts[1:]:
        out = out + _dot(p, mat)
    return out


def _per_head(cols, fill=0.0):
    rows = cols[0].shape[0]
    lane = lax.broadcasted_iota(jnp.int32, (rows, LANES), 1)
    out = jnp.full((rows, LANES), fill, F32)
    for h, col in enumerate(cols):
        out = jnp.where(lane == h, col, out)
    return out


def _lane_blocks(width):
    return [slice(p * LANES, (p + 1) * LANES) for p in range(width // LANES)]


def _stage(rows, width):
    return pltpu.VMEM((width // LANES, rows, LANES), F32)


def _stage_write(buf, value):
    for p, lanes in enumerate(_lane_blocks(value.shape[1])):
        buf[p] = value[:, lanes]


def _stage_read(buf):
    return jnp.concatenate([buf[p] for p in range(buf.shape[0])], axis=1) if buf.shape[0] > 1 else buf[0]


def _to_residues(buf, out_ref, dil):
    rows = buf.shape[1] // dil
    for r in range(dil):
        for p in range(buf.shape[0]):
            plane = buf.at[p]
            out_ref[r, :, p * LANES:(p + 1) * LANES] = plane[pl.ds(r, rows, stride=dil), :].astype(out_ref.dtype)


def _from_residues(in_ref, buf, dil):
    rows = buf.shape[1] // dil
    for r in range(dil):
        for p in range(buf.shape[0]):
            plane = buf.at[p]
            plane[pl.ds(r, rows, stride=dil), :] = in_ref[r, :, p * LANES:(p + 1) * LANES].astype(F32)


def _residue_spec(dil, tm, width):
    return pl.BlockSpec((dil, tm // dil, width), lambda i: (0, i, 0))


def _gather_exchange(shards_2d):
    shards = tuple(jax.ShapeDtypeStruct((2, s.shape[0] // 2, s.shape[1]), BF16) for s in shards_2d)
    n = len(shards)

    def copies(srcs, outs, send_sems, recv_sems, local_sems):
        x, y, c = lax.axis_index("x"), lax.axis_index("y"), lax.axis_index("c")
        my_chip = 2 * x + y
        sibling = (x, y, 1 - c)
        chips = [(1 - x, y), (x, 1 - y), (1 - x, 1 - y)]

        def copy(k, src, dst, to):
            return pltpu.make_async_remote_copy(src_ref=src, dst_ref=dst, send_sem=send_sems.at[k],
                                                recv_sem=recv_sems.at[k], device_id=to, device_id_type=MESH)

        first, arrive, passed, sibling_arrive = [], [], [], []
        for j, (cx, cy) in enumerate(chips):
            chip = 2 * cx + cy
            for t in range(n):
                k = n * j + t
                first.append(copy(k, srcs[t].at[c], outs[t].at[my_chip, c], (cx, cy, c)))
                arrive.append(copy(k, srcs[t].at[c], outs[t].at[chip, c], (cx, cy, c)))
                passed.append(copy(n * 3 + k, outs[t].at[chip, c], outs[t].at[chip, c], sibling))
                sibling_arrive.append(copy(n * 3 + k, outs[t].at[chip, 1 - c], outs[t].at[chip, 1 - c], sibling))
        own = [pltpu.make_async_copy(srcs[t], outs[t].at[my_chip], local_sems.at[t]) for t in range(n)]
        return first, arrive, passed, sibling_arrive, own

    def start(*refs):
        first, _, _, _, own = copies(*refs)
        for cp in first + own:
            cp.start()

    def forward(refs, senders):
        _, arrive, passed, _, _ = copies(*refs)
        for j in senders:
            for k in range(n * j, n * (j + 1)):
                arrive[k].wait_recv()
                passed[k].start()

    def mid(*refs):
        forward(refs, (0, 1))

    def finish(*refs):
        forward(refs, (2,))
        first, _, passed, sibling_arrive, own = copies(*refs)
        for cp in sibling_arrive:
            cp.wait_recv()
        for cp in first + passed:
            cp.wait_send()
        for cp in own:
            cp.wait()

    return dict(ins=[], start=start, mid=mid, finish=finish,
                outs=[jax.ShapeDtypeStruct((N_CHIPS,) + s.shape, s.dtype) for s in shards],
                sems=[pltpu.SemaphoreType.DMA((6 * n,)), pltpu.SemaphoreType.DMA((6 * n,)),
                      pltpu.SemaphoreType.DMA((n,))])


def _mem_kv(mem, mem_norm, w_mkv):
    def body(mem_ref, g_ref, w_ref, mk_ref, mv_ref):
        m = mem_ref[...]
        r = lax.rsqrt(jnp.mean(m * m, axis=-1, keepdims=True) + RMS_EPS)
        mn = (m * r * g_ref[...]).astype(BF16)
        kv = _dot(mn, w_ref[...])
        mk_ref[...] = kv[:, :C_W].astype(BF16)
        mv_ref[...] = kv[:, C_W:].astype(BF16)

    return pl.pallas_call(
        body, name="mem_kv",
        out_shape=[jax.ShapeDtypeStruct((N_MEM, C_W), BF16)] * 2,
    )(mem, mem_norm, w_mkv)


def _mem_kv_bwd(mem, mem_norm, w_mkv, dmk, dmv):
    def body(mem_ref, g_ref, w_ref, dmk_ref, dmv_ref, gw_ref, gn_ref):
        m = mem_ref[...]
        r = lax.rsqrt(jnp.mean(m * m, axis=-1, keepdims=True) + RMS_EPS)
        mhat = m * r
        mn = (mhat * g_ref[...]).astype(BF16)
        dkv = jnp.concatenate([dmk_ref[...], dmv_ref[...]], axis=1).astype(BF16)
        gw_ref[...] = _dot_tn(mn, dkv)
        dmn = _dot_nt(dkv, w_ref[...])
        gn_ref[...] = jnp.sum(dmn * mhat, axis=0, keepdims=True)

    return pl.pallas_call(
        body, name="mem_kv_bwd",
        out_shape=[jax.ShapeDtypeStruct((D_MODEL, 2 * C_W), F32), jax.ShapeDtypeStruct((1, D_MODEL), F32)],
    )(mem, mem_norm, w_mkv, dmk, dmv)


def _host_phases(host, in_refs, out_refs, sems, steps, before):
    if not host:
        return
    step = pl.program_id(0)
    phases = [("start", 0)] if before else [("mid", max(steps - 3, 0)), ("finish", steps - 1)]
    for phase, at in phases:
        pl.when(step == at)(lambda phase=phase: host[phase](in_refs, out_refs, *sems))


def _pre_norm(x, pre_norm, w_own=None, host=None):
    seq = x.shape[0]
    tm = min(ROW_TILE, seq)
    n_own_in = 2 if w_own is None else 3
    n_own_out = n_own_in
    n_host_in = len(host["ins"]) if host else 0
    n_host_out = len(host["outs"]) if host else 0
    half = D_MODEL // 2

    def body(x_ref, g_ref, *refs):
        w_ref = None if w_own is None else refs[0]
        refs = refs[n_own_in - 2:]
        host_in, own_out, refs = refs[:n_host_in], refs[n_host_in:n_host_in + n_own_out], refs[n_host_in + n_own_out:]
        host_out, refs = refs[:n_host_out], refs[n_host_out:]
        if w_own is not None:
            wb, sems = refs[0], refs[1:]

            @pl.when(pl.program_id(0) == 0)
            def _():
                for h in range(2):
                    wb[h] = w_ref[h * half:(h + 1) * half, :].astype(BF16)
            if host:
                host_in = [wb]
        else:
            sems = refs
        _host_phases(host, host_in, host_out, sems, seq // tm, before=True)

        xv = x_ref[...]
        r = lax.rsqrt(jnp.mean(xv * xv, axis=-1, keepdims=True) + RMS_EPS)
        u = xv * r * g_ref[...]
        ub = u.astype(BF16)
        own_out[0][...] = ub
        own_out[1][...] = u.T.astype(BF16)
        if w_own is not None:
            own_out[2][...] = _dot(ub[:, :half], wb[0]) + _dot(ub[:, half:], wb[1])
        _host_phases(host, host_in, host_out, sems, seq // tm, before=False)

    any_spec = pl.BlockSpec(memory_space=pl.ANY)
    ins = [x, pre_norm]
    in_specs = [pl.BlockSpec((tm, D_MODEL), lambda i: (i, 0)), pl.BlockSpec(pre_norm.shape, lambda i: (0, 0))]
    out_shape = [jax.ShapeDtypeStruct((seq, D_MODEL), BF16), jax.ShapeDtypeStruct((seq // tm, D_MODEL, tm), BF16)]
    out_specs = [pl.BlockSpec((tm, D_MODEL), lambda i: (i, 0)), pl.BlockSpec((None, D_MODEL, tm), lambda i: (i, 0, 0))]
    aliases, scratch = {}, []
    if w_own is not None:
        ins.append(w_own)
        in_specs.append(pl.BlockSpec(w_own.shape, lambda i: (0, 0)))
        out_shape.append(jax.ShapeDtypeStruct((seq, w_own.shape[1]), F32))
        out_specs.append(pl.BlockSpec((tm, w_own.shape[1]), lambda i: (i, 0)))
        scratch.append(pltpu.VMEM((2, half, w_own.shape[1]), BF16))
    if host:
        aliases = {len(ins) + k: n_own_out + v for k, v in host.get("aliases", {}).items()}
        ins += list(host["ins"])
        in_specs += [any_spec] * n_host_in
        out_shape += list(host["outs"])
        out_specs += [any_spec] * n_host_out
        scratch += list(host["sems"])
    res = pl.pallas_call(
        body, name="pre_norm", grid=(seq // tm,), in_specs=in_specs, out_specs=out_specs, out_shape=out_shape,
        input_output_aliases=aliases, scratch_shapes=scratch,
        compiler_params=pltpu.CompilerParams(dimension_semantics=("arbitrary",)),
    )(*ins)
    return res[0], res[1], (None if w_own is None else res[2]), res[n_own_out:]


def _pre_proj(u, w_in_g, own=None, host=None, host_shards=()):
    seq = u.shape[0]
    tm = min(ROW_TILE, seq)
    n_nat, n_dil = len(_NATURAL), len(_DILATED) * len(B_DILS)
    rope = _rope_tables(seq, tm)

    n_host_in = len(host["ins"]) if host else 0
    n_host_out = len(host["outs"]) if host else 0
    n_own_out = n_nat + n_dil
    n_shards = len(host_shards)

    def body(u_ref, w_ref, rl_ref, rb_ref, *refs):
        if own:
            (chip_ref, pown_ref), refs = refs[:2], refs[2:]
        shard_refs, refs = refs[:n_shards], refs[n_shards:]
        host_in, refs = refs[:n_host_in], refs[n_host_in:]
        nat = dict(zip(_NATURAL, refs[:n_nat]))
        res = {n: refs[n_nat + len(B_DILS) * k:n_nat + len(B_DILS) * (k + 1)] for k, n in enumerate(_DILATED)}
        host_out = refs[n_own_out:n_own_out + n_host_out]
        bufs = dict(zip(_DILATED, refs[n_own_out + n_host_out:]))
        refs = refs[n_own_out + n_host_out + len(_DILATED):]
        shard_bufs, sems = refs[:n_shards], refs[n_shards:]
        if host_shards:
            @pl.when(pl.program_id(0) == 0)
            def _():
                for src, dst in zip(shard_refs, shard_bufs):
                    rows = src.shape[0] // 2
                    for h in range(2):
                        dst[h] = src[h * rows:(h + 1) * rows, :].astype(BF16)
            host_in = shard_bufs
        _host_phases(host, host_in, host_out, sems, seq // tm, before=True)

        def project(own_chip):
            ub = u_ref[...]
            c, sm, sp = _rope_coeffs(rl_ref, rb_ref)
            for j in range(N_CHIPS):
                pj = pown_ref[...] if j == own_chip else _dot(ub, w_ref[j])
                for b in range(SHARD_IN // LANES):
                    name, off, roped, scaled = _PROJ_LAYOUT[(SHARD_IN // LANES) * j + b]
                    piece = pj[:, LANES * b:LANES * (b + 1)]
                    if roped:
                        piece = _rope(piece, c, sm, sp)
                    if scaled:
                        piece = piece * SCALE
                    if name in bufs:
                        bufs[name][off // LANES] = piece
                    else:
                        nat[name][:, off:off + LANES] = piece.astype(BF16)
            for name in _DILATED:
                for ref, dil in zip(res[name], B_DILS):
                    _to_residues(bufs[name], ref, dil)

        if own:
            for chip in range(N_CHIPS):
                pl.when(chip_ref[0] == chip)(lambda chip=chip: project(chip))
        else:
            project(None)
        _host_phases(host, host_in, host_out, sems, seq // tm, before=False)

    row = lambda w: pl.BlockSpec((tm, w), lambda i: (i, 0))
    full = lambda a: pl.BlockSpec(a.shape, lambda i: (0,) * a.ndim)
    any_spec = pl.BlockSpec(memory_space=pl.ANY)
    out_shape = [jax.ShapeDtypeStruct((seq, _PROJ_WIDTH[n]), BF16) for n in _NATURAL]
    out_specs = [row(_PROJ_WIDTH[n]) for n in _NATURAL]
    for n in _DILATED:
        for dil in B_DILS:
            out_shape.append(jax.ShapeDtypeStruct((dil, seq // dil, B_W), BF16))
            out_specs.append(_residue_spec(dil, tm, B_W))
    ins = [u, w_in_g, *rope]
    in_specs = [row(D_MODEL), full(w_in_g), full(rope[0]), pl.BlockSpec((8, 2 * LANES), lambda i: (i, 0))]
    if own:
        ins += list(own)
        in_specs += [pl.BlockSpec(memory_space=pltpu.SMEM), row(SHARD_IN)]
    ins += list(host_shards)
    in_specs += [full(s) for s in host_shards]
    scratch = [_stage(tm, B_W)] * len(_DILATED)
    scratch += [pltpu.VMEM((2, s.shape[0] // 2, s.shape[1]), BF16) for s in host_shards]
    aliases = {}
    if host:
        aliases = {len(ins) + k: n_own_out + v for k, v in host.get("aliases", {}).items()}
        ins += list(host["ins"])
        in_specs += [any_spec] * n_host_in
        out_shape += list(host["outs"])
        out_specs += [any_spec] * n_host_out
        scratch += list(host["sems"])
    res = pl.pallas_call(
        body, name="pre_proj", grid=(seq // tm,), in_specs=in_specs, out_specs=out_specs, out_shape=out_shape,
        input_output_aliases=aliases, scratch_shapes=scratch,
        compiler_params=pltpu.CompilerParams(dimension_semantics=("arbitrary",)),
    )(*ins)
    out = dict(zip(_NATURAL, res[:n_nat]))
    for k, n in enumerate(_DILATED):
        out[n] = res[n_nat + len(B_DILS) * k:n_nat + len(B_DILS) * (k + 1)]
    out["hosted"] = res[n_own_out:]
    return out


def _band_bias(max_dist, transposed):
    i = np.arange(BLOCK)[:, None]
    j = np.arange(BLOCK)[None, :]
    if transposed:
        same = i <= j
        other = (j + BLOCK - i) <= max_dist
        vis = np.concatenate([same, other], axis=1)
    else:
        prev = (i + BLOCK - j) <= max_dist
        same = j <= i
        vis = np.concatenate([prev, same], axis=1)
    return jnp.asarray(np.where(vis, 0.0, NEG).astype(np.float32))


def _kv_place(h, gqa):
    return (0, h // 3) if gqa else (h // 2, h % 2)


def _band_fwd(q, k, v, sink, *, max_dist, name):
    dil, length, wq = q.shape
    wk = k.shape[2]
    gqa = wk != wq
    tq = min(ATTN_TILE, length)
    ns, nt = tq // BLOCK, length // tq
    npair = wq // LANES
    bias = _band_bias(max_dist, transposed=False)
    has_sink = sink is not None

    def body(*refs):
        if has_sink:
            sink_ref, refs = refs[0], refs[1:]
        q_ref, k_ref, kp_ref, v_ref, vp_ref, bias_ref, o_ref, lse_ref, kbuf, vbuf = refs[:10]
        i = pl.program_id(1)
        kbuf[0:BLOCK] = kp_ref[...]
        kbuf[BLOCK:] = k_ref[...]
        vbuf[0:BLOCK] = vp_ref[...]
        vbuf[BLOCK:] = v_ref[...]
        if gqa:
            kroll, vroll = refs[10:12]
            kroll[...] = pltpu.roll(kbuf[...], HEAD_DIM, 1)
            vroll[...] = pltpu.roll(vbuf[...], HEAD_DIM, 1)
        half = _half_masks(BLOCK)
        col_prev = (lax.broadcasted_iota(jnp.int32, (1, 2 * BLOCK), 1) < BLOCK).astype(F32)

        def score_matmuls(a):
            scores = []
            for p in range(npair):
                qp = q_ref[a * BLOCK:(a + 1) * BLOCK, p * LANES:(p + 1) * LANES]
                for e in range(2):
                    pk, ek = _kv_place(2 * p + e, gqa)
                    kw = (kbuf if ek == e else kroll)[a * BLOCK:(a + 2) * BLOCK, pk * LANES:(pk + 1) * LANES]
                    scores.append(_dot_nt(jnp.where(half[e], qp, jnp.zeros_like(qp)), kw))
            return scores

        pending = score_matmuls(0)
        for a in range(ns):
            r0 = a * BLOCK
            b = bias_ref[...]
            if a == 0:
                b = b + jnp.where(i == 0, NEG, 0.0) * col_prev
            scores = pending
            m_cols, l_cols, probs = [], [], []
            for h, s in enumerate(scores):
                s = s + b
                m = jnp.max(s, axis=1, keepdims=True)
                if has_sink:
                    m = jnp.maximum(m, sink_ref[h])
                pe = jnp.exp(s - m)
                l = jnp.sum(pe, axis=1, keepdims=True)
                if has_sink:
                    l = l + jnp.exp(sink_ref[h] - m)
                probs.append(pe.astype(BF16))
                m_cols.append(m)
                l_cols.append(l)
            pending = score_matmuls(a + 1) if a + 1 < ns else None
            for p in range(npair):
                o_h = []
                for e in range(2):
                    h = 2 * p + e
                    pk, ek = _kv_place(h, gqa)
                    vw = (vbuf if ek == e else vroll)[r0:r0 + 2 * BLOCK, pk * LANES:(pk + 1) * LANES]
                    o_h.append(_dot(probs[h], vw) * (1.0 / l_cols[h]))
                o_ref[r0:r0 + BLOCK, p * LANES:(p + 1) * LANES] = jnp.where(half[0], o_h[0], o_h[1]).astype(BF16)
            lse_ref[r0:r0 + BLOCK, :] = _per_head(m_cols) + jnp.log(_per_head(l_cols, 1.0))

    main = lambda w: pl.BlockSpec((None, tq, w), lambda r, i: (r, i, 0))
    prev = lambda w: pl.BlockSpec((None, BLOCK, w), lambda r, i: (r, jnp.maximum(i * ns - 1, 0), 0))
    in_specs = [main(wq), main(wk), prev(wk), main(wk), prev(wk), pl.BlockSpec(bias.shape, lambda r, i: (0, 0))]
    args = [q, k, k, v, v, bias]
    if has_sink:
        in_specs = [pl.BlockSpec(memory_space=pltpu.SMEM)] + in_specs
        args = [sink] + args
    scratch = [pltpu.VMEM((tq + BLOCK, wk), BF16)] * (4 if gqa else 2)
    return pl.pallas_call(
        body, name=name, grid=(dil, nt), in_specs=in_specs,
        out_specs=[main(wq), main(LANES)],
        out_shape=[jax.ShapeDtypeStruct((dil, length, wq), BF16), jax.ShapeDtypeStruct((dil, length, LANES), F32)],
        scratch_shapes=scratch,
    )(*args)


def _band_bwd(q, k, v, do, lse, delta, *, max_dist, name):
    dil, length, wq = q.shape
    wk = k.shape[2]
    gqa = wk != wq
    tq = min(ATTN_TILE, length)
    ns, nt = tq // BLOCK, length // tq
    npair = wq // LANES
    nblocks = length // BLOCK
    bias = _band_bias(max_dist, transposed=True)

    def body(q_ref, qn_ref, do_ref, don_ref, lse_ref, lsen_ref, dl_ref, dln_ref, k_ref, v_ref, bias_ref,
             dq_ref, dk_ref, dv_ref, stat_l, stat_d, dqt, kt, *rolled):
        i = pl.program_id(1)
        for pk in range(wk // LANES):
            kt[pk] = k_ref[:, pk * LANES:(pk + 1) * LANES].astype(F32).T.astype(BF16)
        if gqa:
            kroll, vroll, ktroll = rolled
            kroll[...] = pltpu.roll(k_ref[...], HEAD_DIM, 1)
            vroll[...] = pltpu.roll(v_ref[...], HEAD_DIM, 1)
            ktroll[0] = kroll[...].astype(F32).T.astype(BF16)
        for a in range(ns):
            rows = slice(a * BLOCK, (a + 1) * BLOCK)
            stat_l[a] = _rows_to_lanes(lse_ref[rows, :])
            stat_d[a] = _rows_to_lanes(dl_ref[rows, :])
        stat_l[ns] = _rows_to_lanes(lsen_ref[...])
        stat_d[ns] = _rows_to_lanes(dln_ref[...])

        @pl.when(i == 0)
        def _():
            dqt[:, :, 0:BLOCK] = jnp.zeros((npair, LANES, BLOCK), F32)

        @pl.when(i > 0)
        def _():
            dqt[:, :, 0:BLOCK] = dqt[:, :, tq:tq + BLOCK]

        dqt[:, :, BLOCK:] = jnp.zeros((npair, LANES, tq), F32)
        half2 = _half_masks(2 * BLOCK)
        row = lax.broadcasted_iota(jnp.int32, (LANES, BLOCK), 0)
        row_half = (row < HEAD_DIM, row >= HEAD_DIM)
        col_next = (lax.broadcasted_iota(jnp.int32, (1, 2 * BLOCK), 1) >= BLOCK).astype(F32)

        def scores(b):
            rows = slice(b * BLOCK, (b + 1) * BLOCK)
            nxt_rows = slice((b + 1) * BLOCK, (b + 2) * BLOCK)
            items = []
            for p in range(npair):
                lanes = slice(p * LANES, (p + 1) * LANES)
                q_next = q_ref[nxt_rows, lanes] if b + 1 < ns else qn_ref[:, lanes]
                do_next = do_ref[nxt_rows, lanes] if b + 1 < ns else don_ref[:, lanes]
                qw = jnp.concatenate([q_ref[rows, lanes], q_next], axis=0)
                dow = jnp.concatenate([do_ref[rows, lanes], do_next], axis=0)
                for e in range(2):
                    h = 2 * p + e
                    pk, ek = _kv_place(h, gqa)
                    klanes = slice(pk * LANES, (pk + 1) * LANES)
                    kb = (k_ref if ek == e else kroll)[rows, klanes]
                    vb = (v_ref if ek == e else vroll)[rows, klanes]
                    qm = jnp.where(half2[e], qw, jnp.zeros_like(qw))
                    dom = jnp.where(half2[e], dow, jnp.zeros_like(dow))
                    items.append(dict(p=p, e=e, h=h, pk=pk, ek=ek, qm=qm, dom=dom,
                                      st=_dot_nt(kb, qm), dpt=_dot_nt(vb, dom)))
            return items

        def probs(b, items):
            bt = bias_ref[...]
            if b == ns - 1:
                bt = bt + jnp.where(i == nt - 1, NEG, 0.0) * col_next
            for it in items:
                h = it["h"]
                lrow = jnp.concatenate([stat_l[b, h:h + 1, :], stat_l[b + 1, h:h + 1, :]], axis=1)
                drow = jnp.concatenate([stat_d[b, h:h + 1, :], stat_d[b + 1, h:h + 1, :]], axis=1)
                pt = jnp.exp(it["st"] + bt - lrow)
                it["ptb"] = pt.astype(BF16)
                it["dsb"] = (pt * (it["dpt"] - drow)).astype(BF16)

        pending = scores(0)
        for b in range(ns):
            rows = slice(b * BLOCK, (b + 1) * BLOCK)
            window = slice(b * BLOCK, (b + 2) * BLOCK)
            acc = {}
            items = pending
            probs(b, items)
            pending = scores(b + 1) if b + 1 < ns else None
            for p in range(npair):
                pair = items[2 * p:2 * p + 2]
                lanes = slice(p * LANES, (p + 1) * LANES)
                kparts = []
                for it in pair:
                    kbt = (kt if it["ek"] == it["e"] else ktroll)[it["pk"], :, rows]
                    kparts.append(jnp.where(row_half[it["e"]], kbt, jnp.zeros_like(kbt)))
                ds_keys = jnp.concatenate([it["dsb"] for it in pair], axis=0)
                dqt[p, :, window] += _dot(jnp.concatenate(kparts, axis=1), ds_keys)
                if not gqa:
                    q_both = jnp.concatenate([it["qm"] for it in pair], axis=0)
                    do_both = jnp.concatenate([it["dom"] for it in pair], axis=0)
                    dk_ref[rows, lanes] = _dot(jnp.concatenate([it["dsb"] for it in pair], axis=1), q_both).astype(BF16)
                    dv_ref[rows, lanes] = _dot(jnp.concatenate([it["ptb"] for it in pair], axis=1), do_both).astype(BF16)
                else:
                    for it in pair:
                        dv_c = _dot(it["ptb"], it["dom"])
                        dk_c = _dot(it["dsb"], it["qm"])
                        key = (it["pk"], it["ek"] == it["e"])
                        if key in acc:
                            acc[key] = (acc[key][0] + dk_c, acc[key][1] + dv_c)
                        else:
                            acc[key] = (dk_c, dv_c)
            if gqa:
                dk_al, dv_al = acc[(0, True)]
                dk_mis, dv_mis = acc[(0, False)]
                dk_ref[rows, :] = (dk_al + pltpu.roll(dk_mis, HEAD_DIM, 1)).astype(BF16)
                dv_ref[rows, :] = (dv_al + pltpu.roll(dv_mis, HEAD_DIM, 1)).astype(BF16)

        for p in range(npair):
            dq_ref[:, p * LANES:(p + 1) * LANES] = dqt[p, :, 0:tq].T.astype(BF16)

    main = lambda w: pl.BlockSpec((None, tq, w), lambda r, i: (r, i, 0))
    nxt = lambda w: pl.BlockSpec((None, BLOCK, w), lambda r, i: (r, jnp.minimum((i + 1) * ns, nblocks - 1), 0))
    scratch = [pltpu.VMEM((ns + 1, 8, LANES), F32), pltpu.VMEM((ns + 1, 8, LANES), F32),
               pltpu.VMEM((npair, LANES, tq + BLOCK), F32), pltpu.VMEM((wk // LANES, LANES, tq), BF16)]
    if gqa:
        scratch = scratch + [pltpu.VMEM((tq, wk), BF16)] * 2 + [pltpu.VMEM((1, LANES, tq), BF16)]
    return pl.pallas_call(
        body, name=name, grid=(dil, nt),
        in_specs=[main(wq), nxt(wq), main(wq), nxt(wq), main(LANES), nxt(LANES), main(LANES), nxt(LANES),
                  main(wk), main(wk), pl.BlockSpec(bias.shape, lambda r, i: (0, 0))],
        out_specs=[main(wq), main(wk), main(wk)],
        out_shape=[jax.ShapeDtypeStruct((dil, length, wq), BF16), jax.ShapeDtypeStruct((dil, length, wk), BF16),
                   jax.ShapeDtypeStruct((dil, length, wk), BF16)],
        scratch_shapes=scratch,
        compiler_params=pltpu.CompilerParams(dimension_semantics=("arbitrary", "arbitrary")),
    )(q, q, do, do, lse, lse, delta, delta, k, v, bias)


def _mem_attn_fwd(q, mk, mv):
    seq = q.shape[0]
    tq = min(ATTN_TILE, seq)
    sub_rows = min(4 * BLOCK, tq)
    ns = tq // sub_rows

    def body(q_ref, mk_ref, mv_ref, o_ref, lse_ref):
        half = _half_masks(sub_rows)

        def sub(a, carry):
            r0 = pl.multiple_of(a * sub_rows, sub_rows)
            scores = []
            for p in range(C_W // LANES):
                lanes = slice(p * LANES, (p + 1) * LANES)
                qp = q_ref[pl.ds(r0, sub_rows), lanes]
                for e in range(2):
                    scores.append(_dot_nt(jnp.where(half[e], qp, jnp.zeros_like(qp)), mk_ref[:, lanes]))
            m_cols, l_cols, probs = [], [], []
            for s in scores:
                m = jnp.max(s, axis=1, keepdims=True)
                pe = jnp.exp(s - m)
                probs.append(pe.astype(BF16))
                m_cols.append(m)
                l_cols.append(jnp.sum(pe, axis=1, keepdims=True))
            for p in range(C_W // LANES):
                lanes = slice(p * LANES, (p + 1) * LANES)
                o_h = [_dot(probs[2 * p + e], mv_ref[:, lanes]) * (1.0 / l_cols[2 * p + e]) for e in range(2)]
                o_ref[pl.ds(r0, sub_rows), lanes] = jnp.where(half[0], o_h[0], o_h[1]).astype(BF16)
            lse_ref[pl.ds(r0, sub_rows), :] = _per_head(m_cols) + jnp.log(_per_head(l_cols, 1.0))
            return carry

        lax.fori_loop(0, ns, sub, 0, unroll=True)

    row = lambda w: pl.BlockSpec((tq, w), lambda i: (i, 0))
    full = pl.BlockSpec((N_MEM, C_W), lambda i: (0, 0))
    return pl.pallas_call(
        body, name="mem_attn_fwd", grid=(seq // tq,), in_specs=[row(C_W), full, full],
        out_specs=[row(C_W), row(LANES)],
        out_shape=[jax.ShapeDtypeStruct((seq, C_W), BF16), jax.ShapeDtypeStruct((seq, LANES), F32)],
    )(q, mk, mv)


def _mem_attn_bwd(q, mk, mv, do, lse, delta):
    seq = q.shape[0]
    tq = min(ATTN_TILE, seq)
    ns = tq // BLOCK
    npair = C_W // LANES

    def body(q_ref, mk_ref, mv_ref, do_ref, lse_ref, dl_ref, dq_ref, dmk_ref, dmv_ref, stat_l, stat_d, mkt, dqt):
        @pl.when(pl.program_id(0) == 0)
        def _():
            dmk_ref[...] = jnp.zeros_like(dmk_ref)
            dmv_ref[...] = jnp.zeros_like(dmv_ref)
            for p in range(npair):
                mkt[p] = mk_ref[:, p * LANES:(p + 1) * LANES].astype(F32).T.astype(BF16)

        for a in range(ns):
            rows = slice(a * BLOCK, (a + 1) * BLOCK)
            stat_l[a] = _rows_to_lanes(lse_ref[rows, :])
            stat_d[a] = _rows_to_lanes(dl_ref[rows, :])
        span = min(2, ns)
        half = _half_masks(span * BLOCK)
        row = lax.broadcasted_iota(jnp.int32, (LANES, N_MEM), 0)
        row_half = (row < HEAD_DIM, row >= HEAD_DIM)

        for a in range(0, ns, span):
            rows = slice(a * BLOCK, (a + span) * BLOCK)
            items = []
            for p in range(npair):
                lanes = slice(p * LANES, (p + 1) * LANES)
                qp = q_ref[rows, lanes]
                dop = do_ref[rows, lanes]
                for e in range(2):
                    qm = jnp.where(half[e], qp, jnp.zeros_like(qp))
                    dom = jnp.where(half[e], dop, jnp.zeros_like(dop))
                    items.append(dict(p=p, e=e, qm=qm, dom=dom, st=_dot_nt(mk_ref[:, lanes], qm),
                                      dpt=_dot_nt(mv_ref[:, lanes], dom)))
            for it in items:
                h = 2 * it["p"] + it["e"]
                lrow = jnp.concatenate([stat_l[a + k, h:h + 1, :] for k in range(span)], axis=1)
                drow = jnp.concatenate([stat_d[a + k, h:h + 1, :] for k in range(span)], axis=1)
                pt = jnp.exp(it["st"] - lrow)
                it["ptb"] = pt.astype(BF16)
                it["dsb"] = (pt * (it["dpt"] - drow)).astype(BF16)
            for p in range(npair):
                lanes = slice(p * LANES, (p + 1) * LANES)
                pair = [it for it in items if it["p"] == p]
                join = lambda name, axis: jnp.concatenate([it[name] for it in pair], axis=axis)
                dmv_ref[:, lanes] += _dot(join("ptb", 1), join("dom", 0))
                dmk_ref[:, lanes] += _dot(join("dsb", 1), join("qm", 0))
                kbt = mkt[p]
                k_both = jnp.concatenate([jnp.where(row_half[e], kbt, jnp.zeros_like(kbt)) for e in range(2)], axis=1)
                dqt[p, :, rows] = _dot(k_both, join("dsb", 0))
        for p in range(npair):
            dq_ref[:, p * LANES:(p + 1) * LANES] = dqt[p].T.astype(BF16)

    row = lambda w: pl.BlockSpec((tq, w), lambda i: (i, 0))
    full = pl.BlockSpec((N_MEM, C_W), lambda i: (0, 0))
    return pl.pallas_call(
        body, name="mem_attn_bwd", grid=(seq // tq,),
        in_specs=[row(C_W), full, full, row(C_W), row(LANES), row(LANES)], out_specs=[row(C_W), full, full],
        out_shape=[jax.ShapeDtypeStruct((seq, C_W), BF16), jax.ShapeDtypeStruct((N_MEM, C_W), F32),
                   jax.ShapeDtypeStruct((N_MEM, C_W), F32)],
        scratch_shapes=[pltpu.VMEM((ns, 8, LANES), F32)] * 2
        + [pltpu.VMEM((npair, LANES, N_MEM), BF16), pltpu.VMEM((npair, LANES, tq), F32)],
        compiler_params=pltpu.CompilerParams(dimension_semantics=("arbitrary",)),
    )(q, mk, mv, do, lse, delta)


def _silu_and_grad(g):
    s = 1.0 / (1.0 + jnp.exp(-g))
    return g * s, s * (1.0 + g * (1.0 - s))


def _post(x, target, post_norm, w_out, sink_row, oa, lse_a, ga, ob_list, lseb_list, gb, oc, gc):
    seq = x.shape[0]
    tm = min(ROW_TILE, seq)
    inv_d = 1.0 / D_MODEL
    nd = len(B_DILS)

    def body(*refs):
        (x_ref, t_ref, gp_ref, w_ref, sink_ref, oa_ref, lsea_ref, ga_ref), refs = refs[:8], refs[8:]
        ob_refs, lb_refs, (gb_ref, oc_ref, gc_ref), refs = refs[:nd], refs[nd:2 * nd], refs[2 * nd:2 * nd + 3], refs[2 * nd + 3:]
        (g_ref, doa_ref, dla_ref, dga_ref), refs = refs[:4], refs[4:]
        dob_refs, lsec_refs, dlb_refs, refs = refs[:nd], refs[nd:2 * nd], refs[2 * nd:3 * nd], refs[3 * nd:]
        (dgb_ref, doc_ref, dlc_ref, dgc_ref, gw_ref, gpost_ref, gsink_ref, loss_ref), refs = refs[:8], refs[8:]
        ycat, obufs, lbufs, st_do, st_l, st_d = refs[0], refs[1:nd], refs[nd:2 * nd - 1], refs[2 * nd - 1], refs[2 * nd], refs[2 * nd + 1]

        @pl.when(pl.program_id(0) == 0)
        def _():
            gw_ref[...] = jnp.zeros_like(gw_ref)
            gpost_ref[...] = jnp.zeros_like(gpost_ref)
            gsink_ref[...] = jnp.zeros_like(gsink_ref)
            loss_ref[...] = jnp.zeros_like(loss_ref)

        o_i, l_i = [ob_refs[0][0].astype(F32)], [lb_refs[0][0]]
        for k in range(1, nd):
            _from_residues(ob_refs[k], obufs[k - 1], B_DILS[k])
            _from_residues(lb_refs[k], lbufs[k - 1], B_DILS[k])
            o_i.append(_stage_read(obufs[k - 1]))
            l_i.append(_stage_read(lbufs[k - 1]))
        mx = l_i[0]
        for l in l_i[1:]:
            mx = jnp.maximum(mx, l)
        w_i = [jnp.exp(l - mx) for l in l_i]
        z = w_i[0]
        for w in w_i[1:]:
            z = z + w
        _stage_write(st_l, mx + jnp.log(z))
        expand = _head_expand_matrix(B_W)
        inv_z = 1.0 / z
        ob = None
        for w, o in zip(w_i, o_i):
            term = _dot_split(w * inv_z, expand, 2) * o
            ob = term if ob is None else ob + term
        oa, oc = oa_ref[...].astype(F32), oc_ref[...].astype(F32)
        sa, dsa = _silu_and_grad(ga_ref[...].astype(F32))
        sb, dsb = _silu_and_grad(gb_ref[...].astype(F32))
        sc, dsc = _silu_and_grad(gc_ref[...].astype(F32))
        ycat[:, 0:A_W] = (oa * sa).astype(BF16)
        ycat[:, A_W:A_W + B_W] = (ob * sb).astype(BF16)
        ycat[:, A_W + B_W:] = (oc * sc).astype(BF16)
        y2 = _dot(ycat[...], w_ref[...])
        r = lax.rsqrt(jnp.mean(y2 * y2, axis=-1, keepdims=True) + RMS_EPS)
        zhat = y2 * r
        gp = gp_ref[...]
        err = x_ref[...] + zhat * gp - t_ref[...]
        loss_ref[...] += jnp.sum(err * err) * (0.5 * inv_d)
        g = err * inv_d
        g_ref[...] = g
        gpost_ref[...] += jnp.sum(g * zhat, axis=0, keepdims=True)
        a = g * gp
        dy2 = (r * (a - zhat * jnp.mean(a * zhat, axis=-1, keepdims=True))).astype(BF16)
        dycat = _dot_nt(dy2, w_ref[...])
        gw_ref[...] += _dot_tn(ycat[...], dy2)
        dya, dyb, dyc = dycat[:, 0:A_W], dycat[:, A_W:A_W + B_W], dycat[:, A_W + B_W:]
        doa, dob, doc = dya * sa, dyb * sb, dyc * sc
        doa_ref[...] = doa.astype(BF16)
        doc_ref[...] = doc.astype(BF16)
        dga_ref[...] = (dya * oa * dsa).astype(BF16)
        dgb_ref[...] = (dyb * ob * dsb).astype(BF16)
        dgc_ref[...] = (dyc * oc * dsc).astype(BF16)
        dl_a = _dot_split(doa * oa, _head_sum_matrix(A_W), 2)
        dla_ref[...] = dl_a
        dlc_ref[...] = _dot_split(doc * oc, _head_sum_matrix(C_W), 2)
        gsink_ref[...] += jnp.sum(jnp.exp(sink_ref[...] - lsea_ref[...]) * dl_a, axis=0, keepdims=True)
        _stage_write(st_do, dob)
        _stage_write(st_d, _dot_split(dob * ob, _head_sum_matrix(B_W), 2))
        for k, dil in enumerate(B_DILS):
            _to_residues(st_do, dob_refs[k], dil)
            _to_residues(st_l, lsec_refs[k], dil)
            _to_residues(st_d, dlb_refs[k], dil)

    row = lambda w: pl.BlockSpec((tm, w), lambda i: (i, 0))
    full = lambda shape: pl.BlockSpec(shape, lambda i: (0,) * len(shape))
    res_specs = lambda w: [_residue_spec(d, tm, w) for d in B_DILS]
    res_shapes = lambda w, dt: [jax.ShapeDtypeStruct((d, seq // d, w), dt) for d in B_DILS]
    ins = [x, target, post_norm, w_out, sink_row, oa, lse_a, ga, *ob_list, *lseb_list, gb, oc, gc]
    in_specs = ([row(D_MODEL), row(D_MODEL), full((1, D_MODEL)), full((D_MODEL, D_MODEL)), full((1, LANES)),
                 row(A_W), row(LANES), row(A_W)] + res_specs(B_W) + res_specs(LANES) + [row(B_W), row(C_W), row(C_W)])
    out_shape = ([jax.ShapeDtypeStruct((seq, D_MODEL), F32), jax.ShapeDtypeStruct((seq, A_W), BF16),
                  jax.ShapeDtypeStruct((seq, LANES), F32), jax.ShapeDtypeStruct((seq, A_W), BF16)]
                 + res_shapes(B_W, BF16) + res_shapes(LANES, F32) + res_shapes(LANES, F32)
                 + [jax.ShapeDtypeStruct((seq, B_W), BF16), jax.ShapeDtypeStruct((seq, C_W), BF16),
                    jax.ShapeDtypeStruct((seq, LANES), F32), jax.ShapeDtypeStruct((seq, C_W), BF16),
                    jax.ShapeDtypeStruct((D_MODEL, D_MODEL), F32), jax.ShapeDtypeStruct((1, D_MODEL), F32),
                    jax.ShapeDtypeStruct((1, LANES), F32), jax.ShapeDtypeStruct((1, LANES), F32)])
    out_specs = ([row(D_MODEL), row(A_W), row(LANES), row(A_W)] + res_specs(B_W) + res_specs(LANES) + res_specs(LANES)
                 + [row(B_W), row(C_W), row(LANES), row(C_W),
                    full((D_MODEL, D_MODEL)), full((1, D_MODEL)), full((1, LANES)), full((1, LANES))])
    scratch = ([pltpu.VMEM((tm, D_MODEL), BF16)] + [_stage(tm, B_W)] * (nd - 1) + [_stage(tm, LANES)] * (nd - 1)
               + [_stage(tm, B_W), _stage(tm, LANES), _stage(tm, LANES)])
    res = pl.pallas_call(
        body, name="post", grid=(seq // tm,), in_specs=in_specs, out_specs=out_specs, out_shape=out_shape,
        scratch_shapes=scratch,
        compiler_params=pltpu.CompilerParams(dimension_semantics=("arbitrary",)),
    )(*ins)
    out = dict(g=res[0], doa=res[1], dl_a=res[2], dga=res[3], dob=res[4:4 + nd], lse_b=res[4 + nd:4 + 2 * nd],
               dl_b=res[4 + 2 * nd:4 + 3 * nd])
    rest = res[4 + 3 * nd:]
    out.update(dgb=rest[0], doc=rest[1], dl_c=rest[2], dgc=rest[3], gw_out=rest[4], gpost=rest[5], gsink=rest[6],
               loss=rest[7])
    return out


def _grad_w_in(ut, nat, res):
    tm = ut.shape[2]
    seq = ut.shape[0] * tm
    nd = len(B_DILS)
    nat_list = [nat[n] for n in _NATURAL]
    res_list = [a for n in _DILATED for a in res[n]]
    rope = _rope_tables(seq, tm)

    def body(rl_ref, rb_ref, ut_ref, *refs):
        nat_refs = dict(zip(_NATURAL, refs[:len(_NATURAL)]))
        refs = refs[len(_NATURAL):]
        res_refs = {n: refs[nd * k:nd * (k + 1)] for k, n in enumerate(_DILATED)}
        refs = refs[nd * len(_DILATED):]
        dproj_ref, gw_ref = refs[:2]
        bufs = {n: refs[2 + (nd - 1) * k:2 + (nd - 1) * (k + 1)] for k, n in enumerate(_DILATED)}

        @pl.when(pl.program_id(0) == 0)
        def _():
            gw_ref[...] = jnp.zeros_like(gw_ref)

        c, sm, sp = _rope_coeffs(rl_ref, rb_ref)
        sm, sp = -sm, -sp
        per_shard = SHARD_IN // LANES
        interleaved = False
        for j in (3, 0, 1, 2):
            for blk in range(per_shard * j, per_shard * (j + 1)):
                name, off, roped, scaled = _PROJ_LAYOUT[blk]
                lanes = slice(off, off + LANES)
                if name in nat_refs:
                    piece = nat_refs[name][:, lanes].astype(F32)
                else:
                    if not interleaved:
                        for n in _DILATED:
                            for k in range(1, nd):
                                _from_residues(res_refs[n][k], bufs[n][k - 1], B_DILS[k])
                        interleaved = True
                    piece = res_refs[name][0][0, :, lanes].astype(F32)
                    for buf in bufs[name]:
                        piece = piece + buf[off // LANES]
                if roped:
                    piece = _rope(piece, c, sm, sp)
                if scaled:
                    piece = piece * SCALE
                dproj_ref[:, blk * LANES:(blk + 1) * LANES] = piece.astype(BF16)
            gw_ref[j] += _dot(ut_ref[...], dproj_ref[:, j * SHARD_IN:(j + 1) * SHARD_IN])

    row = lambda w: pl.BlockSpec((tm, w), lambda i: (i, 0))
    in_specs = ([pl.BlockSpec(rope[0].shape, lambda i: (0, 0)), pl.BlockSpec((8, 2 * LANES), lambda i: (i, 0)),
                 pl.BlockSpec((None, D_MODEL, tm), lambda i: (i, 0, 0))]
                + [row(a.shape[1]) for a in nat_list]
                + [_residue_spec(d, tm, B_W) for _ in _DILATED for d in B_DILS])
    return pl.pallas_call(
        body, name="grad_w_in", grid=(seq // tm,), in_specs=in_specs,
        out_specs=[row(D_IN), pl.BlockSpec((N_CHIPS, D_MODEL, SHARD_IN), lambda i: (0, 0, 0))],
        out_shape=[jax.ShapeDtypeStruct((seq, D_IN), BF16), jax.ShapeDtypeStruct((N_CHIPS, D_MODEL, SHARD_IN), F32)],
        scratch_shapes=[_stage(tm, B_W)] * ((nd - 1) * len(_DILATED)),
        compiler_params=pltpu.CompilerParams(dimension_semantics=("arbitrary",)),
    )(*rope, ut, *nat_list, *res_list)


def _input_grad(x, g, pre_norm, w_in_g, dproj, gx_prev, span, after, name):
    seq = x.shape[0]
    tm = seq // INPUT_GRAD_TILES
    first_block, steps = span

    def body(*refs):
        x_ref, g_ref, gp_ref, w_ref, dp_ref = refs[:5]
        gx_ref, gpre_ref = refs[-2:]

        @pl.when(pl.program_id(0) == 0)
        def _():
            gpre_ref[...] = jnp.zeros_like(gpre_ref)

        du = None
        for j in range(N_CHIPS):
            term = _dot_nt(dp_ref[:, j * SHARD_IN:(j + 1) * SHARD_IN], w_ref[j])
            du = term if du is None else du + term
        xv = x_ref[...]
        r = lax.rsqrt(jnp.mean(xv * xv, axis=-1, keepdims=True) + RMS_EPS)
        xhat = xv * r
        gpre_ref[...] += jnp.sum(du * xhat, axis=0, keepdims=True)
        a = du * gp_ref[...]
        gx_ref[...] = g_ref[...] + r * (a - xhat * jnp.mean(a * xhat, axis=-1, keepdims=True))

    row = lambda w: pl.BlockSpec((tm, w), lambda i: (first_block + i, 0))
    full = lambda a: pl.BlockSpec(a.shape, lambda i: (0,) * a.ndim)
    any_spec = pl.BlockSpec(memory_space=pl.ANY)
    ins = [x, g, pre_norm, w_in_g, dproj]
    in_specs = [row(D_MODEL), row(D_MODEL), full(pre_norm), full(w_in_g), row(D_IN)]
    aliases = {}
    if gx_prev is not None:
        aliases[len(ins)] = 0
        ins.append(gx_prev)
        in_specs.append(any_spec)
    if after is not None:
        ins.append(after)
        in_specs.append(any_spec)
    return pl.pallas_call(
        body, name=name, grid=(steps,), in_specs=in_specs,
        out_specs=[row(D_MODEL), pl.BlockSpec((1, D_MODEL), lambda i: (0, 0))],
        out_shape=[jax.ShapeDtypeStruct((seq, D_MODEL), F32), jax.ShapeDtypeStruct((1, D_MODEL), F32)],
        input_output_aliases=aliases,
        compiler_params=pltpu.CompilerParams(dimension_semantics=("arbitrary",)),
    )(*ins)


def _exchange_start(ex, name):
    n_in, n_out, n_sem = len(ex["ins"]), len(ex["outs"]), len(ex["sems"])

    def body(*refs):
        in_refs, land_refs, sems = refs[:n_in], refs[n_in:n_in + n_out], refs[n_in + n_out:n_in + n_out + n_sem]
        ex["start"](in_refs, land_refs, *sems)
        token = refs[-1]
        token[...] = jnp.zeros_like(token)

    hbm = pl.BlockSpec(memory_space=pltpu.HBM)
    sem = pl.BlockSpec(memory_space=pltpu.SEMAPHORE)
    ins = [pltpu.with_memory_space_constraint(a, pltpu.HBM) for a in ex["ins"]]
    landing = [pltpu.with_memory_space_constraint(lax.empty(o.shape, o.dtype), pltpu.HBM) for o in ex["outs"]]
    res = pl.pallas_call(
        body, name=name,
        out_shape=list(ex["sems"]) + [pltpu.HBM(a.shape, a.dtype) for a in ex["ins"]]
        + [pltpu.HBM(o.shape, o.dtype) for o in ex["outs"]] + [jax.ShapeDtypeStruct((8, LANES), F32)],
        in_specs=[hbm] * (n_in + n_out),
        out_specs=[sem] * n_sem + [hbm] * (n_in + n_out) + [pl.BlockSpec(memory_space=pltpu.VMEM)],
        input_output_aliases={k: n_sem + k for k in range(n_in + n_out)},
        compiler_params=pltpu.CompilerParams(has_side_effects=pltpu.SideEffectType.DATAFLOW_SIDE_EFFECTING),
    )(*ins, *landing)
    return res[:-1], res[-1]


def _exchange_wait(ex, handles, after, name):
    n_in, n_out, n_sem = len(ex["ins"]), len(ex["outs"]), len(ex["sems"])
    sems, thru = handles[:n_sem], handles[n_sem:]

    def body(*refs):
        in_refs, land_refs = refs[:n_in], refs[n_in:n_in + n_out]
        sem_refs = refs[n_in + n_out:n_in + n_out + n_sem]
        ex["finish"](in_refs, land_refs, *sem_refs)

    hbm = pl.BlockSpec(memory_space=pltpu.HBM)
    sem = pl.BlockSpec(memory_space=pltpu.SEMAPHORE)
    res = pl.pallas_call(
        body, name=name,
        out_shape=[pltpu.HBM(a.shape, a.dtype) for a in thru],
        in_specs=[hbm] * (n_in + n_out) + [sem] * n_sem + [pl.BlockSpec(memory_space=pl.ANY)],
        out_specs=[hbm] * (n_in + n_out),
        input_output_aliases={k: k for k in range(n_in + n_out)},
        compiler_params=pltpu.CompilerParams(has_side_effects=pltpu.SideEffectType.DATAFLOW_SIDE_EFFECTING),
    )(*thru, *sems, after)
    return res[:n_in], res[n_in:]


def _start_finish(build):
    def start(*refs):
        for cp in build(*refs):
            cp.start()

    def finish(*refs):
        for cp in build(*refs):
            cp.wait()

    return dict(start=start, finish=finish)


def _pair_exchange(grads):
    n = len(grads)

    def build(srcs, outs, send_sems, recv_sems):
        x, y, c = lax.axis_index("x"), lax.axis_index("y"), lax.axis_index("c")
        copies = []
        for t in range(n):
            rows = grads[t].shape[1] // 2
            copies.append(pltpu.make_async_remote_copy(
                src_ref=srcs[t].at[:, pl.ds((1 - c) * rows, rows)], dst_ref=outs[t],
                send_sem=send_sems.at[t], recv_sem=recv_sems.at[t], device_id=(x, y, 1 - c), device_id_type=MESH))
        return copies

    return dict(ins=list(grads), **_start_finish(build),
                outs=[jax.ShapeDtypeStruct((g.shape[0], g.shape[1] // 2, g.shape[2]), g.dtype) for g in grads],
                sems=[pltpu.SemaphoreType.DMA((n,)), pltpu.SemaphoreType.DMA((n,))])


def _pair_add(core, owns, gots):
    n = len(owns)

    def body(core_ref, *refs):
        for t in range(n):
            refs[2 * n + t][...] = (refs[t][...] + refs[n + t][...]).astype(BF16)

    halves = [(None,) + g.shape[1:] for g in gots]
    grid_spec = pltpu.PrefetchScalarGridSpec(
        num_scalar_prefetch=1, grid=(N_CHIPS,),
        in_specs=[pl.BlockSpec(h, lambda k, core_ref: (k, core_ref[0], 0)) for h in halves]
        + [pl.BlockSpec(h, lambda k, core_ref: (k, 0, 0)) for h in halves],
        out_specs=[pl.BlockSpec(h, lambda k, core_ref: (k, 0, 0)) for h in halves])
    return pl.pallas_call(
        body, name="pair_add", grid_spec=grid_spec,
        out_shape=[jax.ShapeDtypeStruct(g.shape, BF16) for g in gots],
    )(core, *owns, *gots)


def _chip_exchange(parts):
    n = len(parts)

    def build(srcs, outs, send_sems, recv_sems, local_sems):
        x, y, c = lax.axis_index("x"), lax.axis_index("y"), lax.axis_index("c")
        my_chip = 2 * x + y
        chips = [(1 - x, y), (x, 1 - y), (1 - x, 1 - y)]
        copies = [pltpu.make_async_copy(srcs[t].at[my_chip], outs[t].at[my_chip], local_sems.at[t]) for t in range(n)]
        for j, (cx, cy) in enumerate(chips):
            for t in range(n):
                k = n * j + t
                copies.append(pltpu.make_async_remote_copy(
                    src_ref=srcs[t].at[2 * cx + cy], dst_ref=outs[t].at[my_chip], send_sem=send_sems.at[k],
                    recv_sem=recv_sems.at[k], device_id=(cx, cy, c), device_id_type=MESH))
        return copies

    return dict(ins=list(parts), **_start_finish(build), outs=[jax.ShapeDtypeStruct(p.shape, p.dtype) for p in parts],
                sems=[pltpu.SemaphoreType.DMA((3 * n,)), pltpu.SemaphoreType.DMA((3 * n,)),
                      pltpu.SemaphoreType.DMA((n,))])


def _chip_sum(core, slots):
    n = len(slots)

    def body(core_ref, *refs):
        for t in range(n):
            acc = refs[t][0].astype(F32)
            for s in range(1, N_CHIPS):
                acc = acc + refs[t][s].astype(F32)
            refs[n + t][...] = acc

    blocks = [(s.shape[1] // TAIL_STEPS, s.shape[2]) for s in slots]
    grid_spec = pltpu.PrefetchScalarGridSpec(
        num_scalar_prefetch=1, grid=(TAIL_STEPS,),
        in_specs=[pl.BlockSpec((N_CHIPS,) + b, lambda i, core_ref: (0, i, 0)) for b in blocks],
        out_specs=[pl.BlockSpec((None,) + b, lambda i, core_ref: (core_ref[0], i, 0)) for b in blocks])
    return pl.pallas_call(
        body, name="chip_sum", grid_spec=grid_spec,
        out_shape=[jax.ShapeDtypeStruct((2,) + s.shape[1:], F32) for s in slots],
    )(core, *slots)


def _pair_gather(bufs, small):
    n = len(bufs)

    def body(*refs):
        small_ref, outs, small_out = refs[n], refs[n + 1:2 * n + 1], refs[2 * n + 1]
        send_sems, recv_sems, local_sem = refs[2 * n + 2:]
        x, y, c = lax.axis_index("x"), lax.axis_index("y"), lax.axis_index("c")
        me = 4 * x + 2 * y + c
        chips = [(1 - x, y), (x, 1 - y), (1 - x, 1 - y)]
        mine = pltpu.make_async_copy(small_ref, small_out.at[me], local_sem)
        mine.start()
        copies = [pltpu.make_async_remote_copy(
            src_ref=outs[t].at[c], dst_ref=outs[t].at[c], send_sem=send_sems.at[t], recv_sem=recv_sems.at[t],
            device_id=(x, y, 1 - c), device_id_type=MESH) for t in range(n)]
        peers = [(x, y, 1 - c)] + [(cx, cy, cc) for (cx, cy) in chips for cc in (c, 1 - c)]
        for j, peer in enumerate(peers):
            copies.append(pltpu.make_async_remote_copy(
                src_ref=small_ref, dst_ref=small_out.at[me], send_sem=send_sems.at[n + j],
                recv_sem=recv_sems.at[n + j], device_id=peer, device_id_type=MESH))
        for cp in copies:
            cp.start()
        for cp in copies:
            cp.wait()
        mine.wait()

    any_spec = pl.BlockSpec(memory_space=pl.ANY)
    res = pl.pallas_call(
        body, name="pair_gather",
        out_shape=[jax.ShapeDtypeStruct(b.shape, b.dtype) for b in bufs]
        + [jax.ShapeDtypeStruct((8,) + small.shape, small.dtype)],
        in_specs=[any_spec] * (n + 1), out_specs=[any_spec] * (n + 1),
        input_output_aliases={t: t for t in range(n)},
        scratch_shapes=[pltpu.SemaphoreType.DMA((n + 7,)), pltpu.SemaphoreType.DMA((n + 7,)),
                        pltpu.SemaphoreType.DMA],
    )(*bufs, small)
    return [r.reshape(2 * b.shape[1], b.shape[2]) for r, b in zip(res[:n], bufs)], res[n]


def _adamw(ws, gs, ms, vs):
    n = len(ws)

    def body(*refs):
        for t in range(n):
            w_ref, g_ref, m_ref, v_ref = refs[t:4 * n:n]
            gout_ref, d_ref, nm_ref, nv_ref = refs[4 * n + t::n]
            g = g_ref[...]
            gout_ref[...] = g
            d_ref[...], nm_ref[...], nv_ref[...] = _adamw_math(w_ref[...], g, m_ref[...], v_ref[...])

    specs = [pl.BlockSpec((w.shape[0] // TAIL_STEPS, w.shape[1]), lambda i: (i, 0)) for w in ws]
    res = pl.pallas_call(
        body, name="adamw", grid=(TAIL_STEPS,), in_specs=specs * 4, out_specs=specs * 4,
        out_shape=[jax.ShapeDtypeStruct(w.shape, F32) for w in ws] * 4,
    )(*ws, *gs, *ms, *vs)
    return [res[t::n] for t in range(n)]


def _adamw_math(w, g, m, v):
    c1 = 1.0 / (1.0 - ADAM_B1 ** ADAM_STEP)
    c2 = 1.0 / (1.0 - ADAM_B2 ** ADAM_STEP)
    nm = ADAM_B1 * m + (1.0 - ADAM_B1) * g
    nv = ADAM_B2 * v + (1.0 - ADAM_B2) * (g * g)
    return -ADAM_LR * ((nm * c1) / (jnp.sqrt(nv * c2) + ADAM_EPS) + ADAM_WD * w), nm, nv


def _small_update(slots, params, ms, vs):
    n = len(params)

    def body(slots_ref, *refs):
        w_refs, m_refs, v_refs, loss_ref = refs[:n], refs[n:2 * n], refs[2 * n:3 * n], refs[3 * n]
        g_refs, d_refs, nm_refs, nv_refs = (refs[3 * n + 1 + k * n:3 * n + 1 + (k + 1) * n] for k in range(4))
        acc = slots_ref[0]
        for s in range(1, slots.shape[0]):
            acc = acc + slots_ref[s]
        loss_ref[...] = acc[4:5, 0:1]
        grads = (acc[0:1] + acc[5:6], acc[3:4], acc[2:3], acc[1:2])
        for k in range(n):
            g = grads[k][:, :w_refs[k].shape[1]]
            g_refs[k][...] = g
            d_refs[k][...], nm_refs[k][...], nv_refs[k][...] = _adamw_math(w_refs[k][...], g, m_refs[k][...],
                                                                           v_refs[k][...])

    return pl.pallas_call(
        body, name="small_update",
        out_shape=[jax.ShapeDtypeStruct((1, 1), F32)] + [jax.ShapeDtypeStruct(p.shape, F32) for p in params] * 4,
    )(slots, *params, *ms, *vs)


def _local_step(x, mem, target, pre_norm, sink_a, mem_norm, post_norm, w_in_g, w_out, w_mkv, gathers=None,
                own=None):
    first_gather, late_gather, late_shards = gathers if gathers else (None, None, ())
    u, ut, p_own, hosted = _pre_norm(x, pre_norm, own[1] if own else None, first_gather)
    if gathers:
        w_in_g = hosted[0].reshape(N_CHIPS, D_MODEL, SHARD_IN)
    pr = _pre_proj(u, w_in_g, (own[0], p_own) if own else None, late_gather, late_shards)
    pr["ut"] = ut
    if gathers:
        w_out, w_mkv = (g.reshape(D_MODEL, g.shape[-1]) for g in pr["hosted"])
    mk, mv = _mem_kv(mem, mem_norm, w_mkv)
    sink = sink_a.reshape(-1)
    qa, ka, va = pr["qa"][None], pr["ka"][None], pr["va"][None]
    oa, lse_a = _band_fwd(qa, ka, va, sink, max_dist=A_WINDOW - 1, name="swa_fwd")
    ob_list, lseb_list = [], []
    for k, (win, dil) in enumerate(B_CONFIGS):
        o_i, l_i = _band_fwd(pr["qb"][k], pr["kb"][k], pr["vb"][k], None, max_dist=win // dil, name=f"dil{dil}_fwd")
        ob_list.append(o_i)
        lseb_list.append(l_i)
    oc, lse_c = _mem_attn_fwd(pr["qc"], mk, mv)
    sink_row = jnp.pad(sink, (0, LANES - sink.shape[0])).reshape(1, LANES)
    po = _post(x, target, post_norm, w_out, sink_row, oa[0], lse_a[0], pr["ga"], ob_list, lseb_list, pr["gb"], oc,
               pr["gc"])
    dqc, dmk, dmv = _mem_attn_bwd(pr["qc"], mk, mv, po["doc"], lse_c, po["dl_c"])
    dqa, dka, dva = _band_bwd(qa, ka, va, po["doa"][None], lse_a, po["dl_a"][None], max_dist=A_WINDOW - 1,
                              name="swa_bwd")
    res = dict(qb=[], kb=[], vb=[])
    for k, (win, dil) in enumerate(B_CONFIGS):
        dq_i, dk_i, dv_i = _band_bwd(pr["qb"][k], pr["kb"][k], pr["vb"][k], po["dob"][k], po["lse_b"][k],
                                     po["dl_b"][k], max_dist=win // dil, name=f"dil{dil}_bwd")
        res["qb"].append(dq_i)
        res["kb"].append(dk_i)
        res["vb"].append(dv_i)
    nat = dict(qa=dqa[0], ka=dka[0], va=dva[0], ga=po["dga"], gb=po["dgb"], qc=dqc, gc=po["dgc"])
    dproj, gw_in = _grad_w_in(pr["ut"], nat, res)
    gw_mkv, gmem = _mem_kv_bwd(mem, mem_norm, w_mkv, dmk, dmv)
    gsink = -po["gsink"][0, :sink.shape[0]]
    return dict(loss=po["loss"], g=po["g"], dproj=dproj, gw_in=gw_in, gw_out=po["gw_out"], gw_mkv=gw_mkv,
                gpost=po["gpost"], gmem=gmem, gsink=gsink, w_in_g=w_in_g)


def kernel(x, mem, pre_norm, w_in, sink_a, mem_norm, w_mem_kv, w_out, post_norm, loss_target, m_pre_norm, m_w_in, m_sink_a, m_mem_norm, m_w_mem_kv, m_w_out, m_post_norm, v_pre_norm, v_w_in, v_sink_a, v_mem_norm, v_w_mem_kv, v_w_out, v_post_norm):
    w_own = w_in[0]
    late_shards = (w_out[0], w_mem_kv[0])
    gathers = (_gather_exchange([w_own]), _gather_exchange(late_shards), late_shards)
    chip = (2 * lax.axis_index("x") + lax.axis_index("y")).astype(jnp.int32).reshape(1)
    loc = _local_step(x[0], mem[0], loss_target[0], pre_norm, sink_a, mem_norm, post_norm, None, None, None, gathers,
                      (chip, w_own))
    big = [loc["gw_in"], loc["gw_out"].reshape(N_CHIPS, D_MODEL // N_CHIPS, D_MODEL),
           loc["gw_mkv"].reshape(N_CHIPS, D_MODEL // N_CHIPS, 2 * C_W)]
    core = lax.axis_index("c").astype(jnp.int32).reshape(1)
    w_in_full = loc["w_in_g"]
    step_in = (x[0], loc["g"], pre_norm, w_in_full, loc["dproj"])
    pair_ex = _pair_exchange(big)
    pair_handles, token = _exchange_start(pair_ex, "pair_exchange_start")
    gx_a, gpre_a = _input_grad(*step_in, None, (0, 2), token, "input_grad_a")
    big, got = _exchange_wait(pair_ex, pair_handles, gpre_a, "pair_exchange_wait")
    parts = _pair_add(core, big, got)
    chip_ex = _chip_exchange(parts)
    chip_handles, token = _exchange_start(chip_ex, "chip_exchange_start")
    grad_x, gpre_b = _input_grad(*step_in, gx_a, (2, 14), token, "input_grad_b")
    _, slots = _exchange_wait(chip_ex, chip_handles, gpre_b, "chip_exchange_wait")
    halves = _chip_sum(core, slots)
    widen = lambda a: jnp.pad(a.reshape(1, -1), ((0, 0), (0, D_MODEL - a.size)))
    small = jnp.concatenate([gpre_a, loc["gpost"], loc["gmem"], widen(loc["gsink"]), widen(loc["loss"]), gpre_b,
                             jnp.zeros((2, D_MODEL), F32)], axis=0)
    (g_in, g_out, g_mkv), small_slots = _pair_gather(halves, small)
    (loss, g_pre, g_sink, g_mem, g_post, d_pre, d_sink, d_mem, d_post, nm_pre, nm_sink, nm_mem, nm_post,
     nv_pre, nv_sink, nv_mem, nv_post) = _small_update(
        small_slots, (pre_norm, sink_a, mem_norm, post_norm), (m_pre_norm, m_sink_a, m_mem_norm, m_post_norm),
        (v_pre_norm, v_sink_a, v_mem_norm, v_post_norm))

    (g_in, d_in, nm_in, nv_in), (g_out, d_out, nm_out, nv_out), (g_mkv, d_mkv, nm_mkv, nv_mkv) = _adamw(
        (w_in[0], w_out[0], w_mem_kv[0]), (g_in, g_out, g_mkv), (m_w_in[0], m_w_out[0], m_w_mem_kv[0]),
        (v_w_in[0], v_w_out[0], v_w_mem_kv[0]))
    lead = lambda a: a[None]
    return (loss.reshape(()), lead(grad_x),
            g_pre, lead(g_in), g_sink, g_mem, lead(g_mkv), lead(g_out), g_post,
            d_pre, lead(d_in), d_sink, d_mem, lead(d_mkv), lead(d_out), d_post,
            nm_pre, lead(nm_in), nm_sink, nm_mem, lead(nm_mkv), lead(nm_out), nm_post,
            nv_pre, lead(nv_in), nv_sink, nv_mem, lead(nv_mkv), lead(nv_out), nv_post)
```

```python
import numpy as np
import jax
import jax.numpy as jnp
from jax import lax
from jax.experimental import pallas as pl
from jax.experimental.pallas import tpu as pltpu

F32 = jnp.float32
BF16 = jnp.bfloat16

D_MODEL = 1024
HEAD_DIM = 64
LANES = 128
BLOCK = 128
ROW_TILE = 512
ATTN_TILE = 1024
INPUT_GRAD_TILES = 16
U_DEPTH = 3
TAIL_STEPS = 4
A_W, A_KV_W, B_W, C_W = 384, 128, 384, 256
N_MEM = 256
D_IN = 3072
N_CHIPS = 4
SHARD_IN = D_IN // N_CHIPS
B_CONFIGS = ((128, 1), (512, 4), (2048, 16))
B_DILS = tuple(d for _, d in B_CONFIGS)
A_WINDOW = 128
RMS_EPS = 1e-6
ROPE_THETA = 500000.0
SCALE = HEAD_DIM ** -0.5
NEG = -1e30
ADAM_LR, ADAM_B1, ADAM_B2, ADAM_EPS, ADAM_WD, ADAM_STEP = 0.001, 0.9, 0.999, 1e-08, 0.01, 10

NT = (((1,), (1,)), ((), ()))
TN = (((0,), (0,)), ((), ()))
MESH = pl.DeviceIdType.MESH

_PROJ_LAYOUT = (
    [("qa", 128 * i, True, True) for i in range(3)] + [("ka", 0, True, False), ("va", 0, False, False)]
    + [("ga", 128 * i, False, False) for i in range(3)]
    + [("qb", 128 * i, True, True) for i in range(3)] + [("kb", 128 * i, True, False) for i in range(3)]
    + [("vb", 128 * i, False, False) for i in range(3)] + [("gb", 128 * i, False, False) for i in range(3)]
    + [("qc", 128 * i, False, True) for i in range(2)] + [("gc", 128 * i, False, False) for i in range(2)]
)
_PROJ_WIDTH = dict(qa=A_W, ka=A_KV_W, va=A_KV_W, ga=A_W, qb=B_W, kb=B_W, vb=B_W, gb=B_W, qc=C_W, gc=C_W)
_NATURAL = ("qa", "ka", "va", "ga", "gb", "qc", "gc")
_DILATED = ("qb", "kb", "vb")


def _dot(a, b):
    return jnp.dot(a, b, preferred_element_type=F32)


def _dot_nt(a, b):
    return lax.dot_general(a, b, NT, preferred_element_type=F32)


def _dot_tn(a, b):
    return lax.dot_general(a, b, TN, preferred_element_type=F32)


def _half_masks(rows):
    lane = lax.broadcasted_iota(jnp.int32, (rows, LANES), 1)
    return lane < HEAD_DIM, lane >= HEAD_DIM


def _rope(t, c, sm, sp):
    return t * c + pltpu.roll(t, LANES - 8, 1) * sm + pltpu.roll(t, 8, 1) * sp


def _rope_tables(seq, tm):
    dim = np.arange(LANES) % HEAD_DIM
    inv_freq = (np.float32(ROPE_THETA) ** (-np.arange(0, 16, 2, dtype=np.float32) / np.float32(16))).astype(np.float64)
    freq = np.where(dim < 16, inv_freq[dim % 8], 0.0)[None, :]
    local = np.arange(tm, dtype=np.float64)[:, None] * freq
    base = (np.arange(seq // tm, dtype=np.float64) * tm)[:, None] * freq
    both = lambda a: np.concatenate([np.cos(a), np.sin(a)], axis=1).astype(np.float32)
    return jnp.asarray(both(local)), jnp.asarray(np.repeat(both(base), 8, axis=0))


def _rope_coeffs(local_ref, base_ref):
    cl, sl = local_ref[:, :LANES], local_ref[:, LANES:]
    cb, sb = base_ref[0:1, :LANES], base_ref[0:1, LANES:]
    cos = cb * cl - sb * sl
    sin = sb * cl + cb * sl
    dim = lax.broadcasted_iota(jnp.int32, (1, LANES), 1) % HEAD_DIM
    return cos, jnp.where(dim < 8, -sin, 0.0), jnp.where((dim >= 8) & (dim < 16), sin, 0.0)


def _split3(x):
    a = x.astype(BF16)
    r = x - a.astype(F32)
    b = r.astype(BF16)
    c = (r - b.astype(F32)).astype(BF16)
    return a, b, c


def _rows_to_lanes(x):
    row = lax.broadcasted_iota(jnp.int32, (8, LANES), 0)
    lane = lax.broadcasted_iota(jnp.int32, (8, LANES), 1)
    eye = (row == lane).astype(BF16)
    a, b, c = _split3(x)
    return _dot_nt(eye, a) + _dot_nt(eye, b) + _dot_nt(eye, c)


def _head_sum_matrix(width):
    k = lax.broadcasted_iota(jnp.int32, (width, LANES), 0)
    h = lax.broadcasted_iota(jnp.int32, (width, LANES), 1)
    return (k // HEAD_DIM == h).astype(BF16)


def _head_expand_matrix(width):
    h = lax.broadcasted_iota(jnp.int32, (LANES, width), 0)
    k = lax.broadcasted_iota(jnp.int32, (LANES, width), 1)
    return (k // HEAD_DIM == h).astype(BF16)


def _dot_split(x, mat, terms):
    parts = _split3(x)[:terms]
    out = _dot(parts[0], mat)
    for p in parts[1:]:
        out = out + _dot(p, mat)
    return out


def _per_head(cols, fill=0.0):
    rows = cols[0].shape[0]
    lane = lax.broadcasted_iota(jnp.int32, (rows, LANES), 1)
    out = jnp.full((rows, LANES), fill, F32)
    for h, col in enumerate(cols):
        out = jnp.where(lane == h, col, out)
    return out


def _lane_blocks(width):
    return [slice(p * LANES, (p + 1) * LANES) for p in range(width // LANES)]


def _stage(rows, width):
    return pltpu.VMEM((width // LANES, rows, LANES), F32)


def _stage_write(buf, value):
    for p, lanes in enumerate(_lane_blocks(value.shape[1])):
        buf[p] = value[:, lanes]


def _stage_read(buf):
    return jnp.concatenate([buf[p] for p in range(buf.shape[0])], axis=1) if buf.shape[0] > 1 else buf[0]


def _to_residues(buf, out_ref, dil):
    rows = buf.shape[1] // dil
    for r in range(dil):
        for p in range(buf.shape[0]):
            plane = buf.at[p]
            out_ref[r, :, p * LANES:(p + 1) * LANES] = plane[pl.ds(r, rows, stride=dil), :].astype(out_ref.dtype)


def _from_residues(in_ref, buf, dil):
    rows = buf.shape[1] // dil
    for r in range(dil):
        for p in range(buf.shape[0]):
            plane = buf.at[p]
            plane[pl.ds(r, rows, stride=dil), :] = in_ref[r, :, p * LANES:(p + 1) * LANES].astype(F32)


def _residue_spec(dil, tm, width):
    return pl.BlockSpec((dil, tm // dil, width), lambda i: (0, i, 0))


def _gather_exchange(shards_2d):
    shards = tuple(jax.ShapeDtypeStruct((2, s.shape[0] // 2, s.shape[1]), BF16) for s in shards_2d)
    n = len(shards)

    def copies(srcs, outs, send_sems, recv_sems, local_sems):
        x, y, c = lax.axis_index("x"), lax.axis_index("y"), lax.axis_index("c")
        my_chip = 2 * x + y
        sibling = (x, y, 1 - c)
        chips = [(1 - x, y), (x, 1 - y), (1 - x, 1 - y)]

        def copy(k, src, dst, to):
            return pltpu.make_async_remote_copy(src_ref=src, dst_ref=dst, send_sem=send_sems.at[k],
                                                recv_sem=recv_sems.at[k], device_id=to, device_id_type=MESH)

        first, arrive, passed, sibling_arrive = [], [], [], []
        for j, (cx, cy) in enumerate(chips):
            chip = 2 * cx + cy
            for t in range(n):
                k = n * j + t
                first.append(copy(k, srcs[t].at[c], outs[t].at[my_chip, c], (cx, cy, c)))
                arrive.append(copy(k, srcs[t].at[c], outs[t].at[chip, c], (cx, cy, c)))
                passed.append(copy(n * 3 + k, outs[t].at[chip, c], outs[t].at[chip, c], sibling))
                sibling_arrive.append(copy(n * 3 + k, outs[t].at[chip, 1 - c], outs[t].at[chip, 1 - c], sibling))
        own = [pltpu.make_async_copy(srcs[t], outs[t].at[my_chip], local_sems.at[t]) for t in range(n)]
        return first, arrive, passed, sibling_arrive, own

    def start(*refs):
        first, _, _, _, own = copies(*refs)
        for cp in first + own:
            cp.start()

    def forward(refs, senders):
        _, arrive, passed, _, _ = copies(*refs)
        for j in senders:
            for k in range(n * j, n * (j + 1)):
                arrive[k].wait_recv()
                passed[k].start()

    def mid(*refs):
        forward(refs, (0, 1))

    def finish(*refs):
        forward(refs, (2,))
        first, _, passed, sibling_arrive, own = copies(*refs)
        for cp in sibling_arrive:
            cp.wait_recv()
        for cp in first + passed:
            cp.wait_send()
        for cp in own:
            cp.wait()

    return dict(ins=[], start=start, mid=mid, finish=finish,
                outs=[jax.ShapeDtypeStruct((N_CHIPS,) + s.shape, s.dtype) for s in shards],
                sems=[pltpu.SemaphoreType.DMA((6 * n,)), pltpu.SemaphoreType.DMA((6 * n,)),
                      pltpu.SemaphoreType.DMA((n,))])


def _mem_kv(mem, mem_norm, w_mkv):
    def body(mem_ref, g_ref, w_ref, mk_ref, mv_ref):
        m = mem_ref[...]
        r = lax.rsqrt(jnp.mean(m * m, axis=-1, keepdims=True) + RMS_EPS)
        mn = (m * r * g_ref[...]).astype(BF16)
        kv = _dot(mn, w_ref[...])
        mk_ref[...] = kv[:, :C_W].astype(BF16)
        mv_ref[...] = kv[:, C_W:].astype(BF16)

    return pl.pallas_call(
        body, name="mem_kv",
        out_shape=[jax.ShapeDtypeStruct((N_MEM, C_W), BF16)] * 2,
    )(mem, mem_norm, w_mkv)


def _mem_kv_bwd(mem, mem_norm, w_mkv, dmk, dmv):
    def body(mem_ref, g_ref, w_ref, dmk_ref, dmv_ref, gw_ref, gn_ref):
        m = mem_ref[...]
        r = lax.rsqrt(jnp.mean(m * m, axis=-1, keepdims=True) + RMS_EPS)
        mhat = m * r
        mn = (mhat * g_ref[...]).astype(BF16)
        dkv = jnp.concatenate([dmk_ref[...], dmv_ref[...]], axis=1).astype(BF16)
        gw_ref[...] = _dot_tn(mn, dkv)
        dmn = _dot_nt(dkv, w_ref[...])
        gn_ref[...] = jnp.sum(dmn * mhat, axis=0, keepdims=True)

    return pl.pallas_call(
        body, name="mem_kv_bwd",
        out_shape=[jax.ShapeDtypeStruct((D_MODEL, 2 * C_W), F32), jax.ShapeDtypeStruct((1, D_MODEL), F32)],
    )(mem, mem_norm, w_mkv, dmk, dmv)


def _host_phases(host, in_refs, out_refs, sems, steps, before):
    if not host:
        return
    step = pl.program_id(0)
    phases = [("start", 0)] if before else [("mid", max(steps - 3, 0)), ("finish", steps - 1)]
    for phase, at in phases:
        pl.when(step == at)(lambda phase=phase: host[phase](in_refs, out_refs, *sems))


def _pre_norm(x, pre_norm, w_own=None, host=None):
    seq = x.shape[0]
    tm = min(ROW_TILE, seq)
    n_own_in = 2 if w_own is None else 3
    n_own_out = n_own_in
    n_host_in = len(host["ins"]) if host else 0
    n_host_out = len(host["outs"]) if host else 0
    half = D_MODEL // 2

    def body(x_ref, g_ref, *refs):
        w_ref = None if w_own is None else refs[0]
        refs = refs[n_own_in - 2:]
        host_in, own_out, refs = refs[:n_host_in], refs[n_host_in:n_host_in + n_own_out], refs[n_host_in + n_own_out:]
        host_out, refs = refs[:n_host_out], refs[n_host_out:]
        if w_own is not None:
            wb, sems = refs[0], refs[1:]

            @pl.when(pl.program_id(0) == 0)
            def _():
                for h in range(2):
                    wb[h] = w_ref[h * half:(h + 1) * half, :].astype(BF16)
            if host:
                host_in = [wb]
        else:
            sems = refs
        _host_phases(host, host_in, host_out, sems, seq // tm, before=True)

        xv = x_ref[...]
        r = lax.rsqrt(jnp.mean(xv * xv, axis=-1, keepdims=True) + RMS_EPS)
        u = xv * r * g_ref[...]
        ub = u.astype(BF16)
        own_out[0][...] = ub
        own_out[1][...] = u.T.astype(BF16)
        if w_own is not None:
            own_out[2][...] = _dot(ub[:, :half], wb[0]) + _dot(ub[:, half:], wb[1])
        _host_phases(host, host_in, host_out, sems, seq // tm, before=False)

    any_spec = pl.BlockSpec(memory_space=pl.ANY)
    ins = [x, pre_norm]
    in_specs = [pl.BlockSpec((tm, D_MODEL), lambda i: (i, 0)), pl.BlockSpec(pre_norm.shape, lambda i: (0, 0))]
    out_shape = [jax.ShapeDtypeStruct((seq, D_MODEL), BF16), jax.ShapeDtypeStruct((seq // tm, D_MODEL, tm), BF16)]
    out_specs = [pl.BlockSpec((tm, D_MODEL), lambda i: (i, 0)), pl.BlockSpec((None, D_MODEL, tm), lambda i: (i, 0, 0))]
    aliases, scratch = {}, []
    if w_own is not None:
        ins.append(w_own)
        in_specs.append(pl.BlockSpec(w_own.shape, lambda i: (0, 0)))
        out_shape.append(jax.ShapeDtypeStruct((seq, w_own.shape[1]), F32))
        out_specs.append(pl.BlockSpec((tm, w_own.shape[1]), lambda i: (i, 0)))
        scratch.append(pltpu.VMEM((2, half, w_own.shape[1]), BF16))
    if host:
        aliases = {len(ins) + k: n_own_out + v for k, v in host.get("aliases", {}).items()}
        ins += list(host["ins"])
        in_specs += [any_spec] * n_host_in
        out_shape += list(host["outs"])
        out_specs += [any_spec] * n_host_out
        scratch += list(host["sems"])
    res = pl.pallas_call(
        body, name="pre_norm", grid=(seq // tm,), in_specs=in_specs, out_specs=out_specs, out_shape=out_shape,
        input_output_aliases=aliases, scratch_shapes=scratch,
        compiler_params=pltpu.CompilerParams(dimension_semantics=("arbitrary",)),
    )(*ins)
    return res[0], res[1], (None if w_own is None else res[2]), res[n_own_out:]


def _pre_proj(u, w_in_g, own=None, host=None, host_shards=()):
    seq = u.shape[0]
    tm = min(ROW_TILE, seq)
    n_nat, n_dil = len(_NATURAL), len(_DILATED) * len(B_DILS)
    rope = _rope_tables(seq, tm)

    n_host_in = len(host["ins"]) if host else 0
    n_host_out = len(host["outs"]) if host else 0
    n_own_out = n_nat + n_dil
    n_shards = len(host_shards)

    def body(u_ref, w_ref, rl_ref, rb_ref, *refs):
        if own:
            (chip_ref, pown_ref), refs = refs[:2], refs[2:]
        shard_refs, refs = refs[:n_shards], refs[n_shards:]
        host_in, refs = refs[:n_host_in], refs[n_host_in:]
        nat = dict(zip(_NATURAL, refs[:n_nat]))
        res = {n: refs[n_nat + len(B_DILS) * k:n_nat + len(B_DILS) * (k + 1)] for k, n in enumerate(_DILATED)}
        host_out = refs[n_own_out:n_own_out + n_host_out]
        bufs = dict(zip(_DILATED, refs[n_own_out + n_host_out:]))
        (u_bufs, u_sems), refs = refs[n_own_out + n_host_out + len(_DILATED):][:2], refs[n_own_out + n_host_out + len(_DILATED) + 2:]
        shard_bufs, sems = refs[:n_shards], refs[n_shards:]

        step, steps = pl.program_id(0), seq // tm

        def fetch(tile, slot):
            return pltpu.make_async_copy(u_ref.at[pl.ds(tile * tm, tm)], u_bufs.at[slot], u_sems.at[slot])

        @pl.when(step == 0)
        def _():
            for k in range(min(U_DEPTH - 1, steps)):
                fetch(k, k).start()

        @pl.when(step + U_DEPTH - 1 < steps)
        def _():
            fetch(step + U_DEPTH - 1, (step + U_DEPTH - 1) % U_DEPTH).start()

        slot = step % U_DEPTH
        fetch(step, slot).wait()
        if host_shards:
            @pl.when(pl.program_id(0) == 0)
            def _():
                for src, dst in zip(shard_refs, shard_bufs):
                    rows = src.shape[0] // 2
                    for h in range(2):
                        dst[h] = src[h * rows:(h + 1) * rows, :].astype(BF16)
            host_in = shard_bufs
        _host_phases(host, host_in, host_out, sems, seq // tm, before=True)

        def project(own_chip):
            ub = u_bufs[slot]
            c, sm, sp = _rope_coeffs(rl_ref, rb_ref)
            for j in range(N_CHIPS):
                pj = pown_ref[...] if j == own_chip else _dot(ub, w_ref[j])
                for b in range(SHARD_IN // LANES):
                    name, off, roped, scaled = _PROJ_LAYOUT[(SHARD_IN // LANES) * j + b]
                    piece = pj[:, LANES * b:LANES * (b + 1)]
                    if roped:
                        piece = _rope(piece, c, sm, sp)
                    if scaled:
                        piece = piece * SCALE
                    if name in bufs:
                        bufs[name][off // LANES] = piece
                    else:
                        nat[name][:, off:off + LANES] = piece.astype(BF16)
            for name in _DILATED:
                for ref, dil in zip(res[name], B_DILS):
                    _to_residues(bufs[name], ref, dil)

        if own:
            for chip in range(N_CHIPS):
                pl.when(chip_ref[0] == chip)(lambda chip=chip: project(chip))
        else:
            project(None)
        _host_phases(host, host_in, host_out, sems, seq // tm, before=False)

    row = lambda w: pl.BlockSpec((tm, w), lambda i: (i, 0))
    full = lambda a: pl.BlockSpec(a.shape, lambda i: (0,) * a.ndim)
    any_spec = pl.BlockSpec(memory_space=pl.ANY)
    out_shape = [jax.ShapeDtypeStruct((seq, _PROJ_WIDTH[n]), BF16) for n in _NATURAL]
    out_specs = [row(_PROJ_WIDTH[n]) for n in _NATURAL]
    for n in _DILATED:
        for dil in B_DILS:
            out_shape.append(jax.ShapeDtypeStruct((dil, seq // dil, B_W), BF16))
            out_specs.append(_residue_spec(dil, tm, B_W))
    ins = [u, w_in_g, *rope]
    in_specs = [any_spec, full(w_in_g), full(rope[0]), pl.BlockSpec((8, 2 * LANES), lambda i: (i, 0))]
    if own:
        ins += list(own)
        in_specs += [pl.BlockSpec(memory_space=pltpu.SMEM), row(SHARD_IN)]
    ins += list(host_shards)
    in_specs += [full(s) for s in host_shards]
    scratch = [_stage(tm, B_W)] * len(_DILATED)
    scratch += [pltpu.VMEM((U_DEPTH, tm, D_MODEL), BF16), pltpu.SemaphoreType.DMA((U_DEPTH,))]
    scratch += [pltpu.VMEM((2, s.shape[0] // 2, s.shape[1]), BF16) for s in host_shards]
    aliases = {}
    if host:
        aliases = {len(ins) + k: n_own_out + v for k, v in host.get("aliases", {}).items()}
        ins += list(host["ins"])
        in_specs += [any_spec] * n_host_in
        out_shape += list(host["outs"])
        out_specs += [any_spec] * n_host_out
        scratch += list(host["sems"])
    res = pl.pallas_call(
        body, name="pre_proj", grid=(seq // tm,), in_specs=in_specs, out_specs=out_specs, out_shape=out_shape,
        input_output_aliases=aliases, scratch_shapes=scratch,
        compiler_params=pltpu.CompilerParams(dimension_semantics=("arbitrary",)),
    )(*ins)
    out = dict(zip(_NATURAL, res[:n_nat]))
    for k, n in enumerate(_DILATED):
        out[n] = res[n_nat + len(B_DILS) * k:n_nat + len(B_DILS) * (k + 1)]
    out["hosted"] = res[n_own_out:]
    return out


def _band_bias(max_dist, transposed):
    i = np.arange(BLOCK)[:, None]
    j = np.arange(BLOCK)[None, :]
    if transposed:
        same = i <= j
        other = (j + BLOCK - i) <= max_dist
        vis = np.concatenate([same, other], axis=1)
    else:
        prev = (i + BLOCK - j) <= max_dist
        same = j <= i
        vis = np.concatenate([prev, same], axis=1)
    return jnp.asarray(np.where(vis, 0.0, NEG).astype(np.float32))


def _kv_place(h, gqa):
    return (0, h // 3) if gqa else (h // 2, h % 2)


def _band_fwd(q, k, v, sink, *, max_dist, name):
    dil, length, wq = q.shape
    wk = k.shape[2]
    gqa = wk != wq
    tq = min(ATTN_TILE, length)
    ns, nt = tq // BLOCK, length // tq
    npair = wq // LANES
    bias = _band_bias(max_dist, transposed=False)
    has_sink = sink is not None

    def body(*refs):
        if has_sink:
            sink_ref, refs = refs[0], refs[1:]
        q_ref, k_ref, kp_ref, v_ref, vp_ref, bias_ref, o_ref, lse_ref, kbuf, vbuf = refs[:10]
        i = pl.program_id(1)
        kbuf[0:BLOCK] = kp_ref[...]
        kbuf[BLOCK:] = k_ref[...]
        vbuf[0:BLOCK] = vp_ref[...]
        vbuf[BLOCK:] = v_ref[...]
        if gqa:
            kroll, vroll = refs[10:12]
            kroll[...] = pltpu.roll(kbuf[...], HEAD_DIM, 1)
            vroll[...] = pltpu.roll(vbuf[...], HEAD_DIM, 1)
        half = _half_masks(BLOCK)
        col_prev = (lax.broadcasted_iota(jnp.int32, (1, 2 * BLOCK), 1) < BLOCK).astype(F32)

        def score_matmuls(a):
            scores = []
            for p in range(npair):
                qp = q_ref[a * BLOCK:(a + 1) * BLOCK, p * LANES:(p + 1) * LANES]
                for e in range(2):
                    pk, ek = _kv_place(2 * p + e, gqa)
                    kw = (kbuf if ek == e else kroll)[a * BLOCK:(a + 2) * BLOCK, pk * LANES:(pk + 1) * LANES]
                    scores.append(_dot_nt(jnp.where(half[e], qp, jnp.zeros_like(qp)), kw))
            return scores

        pending = score_matmuls(0)
        for a in range(ns):
            r0 = a * BLOCK
            b = bias_ref[...]
            if a == 0:
                b = b + jnp.where(i == 0, NEG, 0.0) * col_prev
            scores = pending
            m_cols, l_cols, probs = [], [], []
            for h, s in enumerate(scores):
                s = s + b
                m = jnp.max(s, axis=1, keepdims=True)
                if has_sink:
                    m = jnp.maximum(m, sink_ref[h])
                pe = jnp.exp(s - m)
                l = jnp.sum(pe, axis=1, keepdims=True)
                if has_sink:
                    l = l + jnp.exp(sink_ref[h] - m)
                probs.append(pe.astype(BF16))
                m_cols.append(m)
                l_cols.append(l)
            pending = score_matmuls(a + 1) if a + 1 < ns else None
            for p in range(npair):
                o_h = []
                for e in range(2):
                    h = 2 * p + e
                    pk, ek = _kv_place(h, gqa)
                    vw = (vbuf if ek == e else vroll)[r0:r0 + 2 * BLOCK, pk * LANES:(pk + 1) * LANES]
                    o_h.append(_dot(probs[h], vw) * (1.0 / l_cols[h]))
                o_ref[r0:r0 + BLOCK, p * LANES:(p + 1) * LANES] = jnp.where(half[0], o_h[0], o_h[1]).astype(BF16)
            lse_ref[r0:r0 + BLOCK, :] = _per_head(m_cols) + jnp.log(_per_head(l_cols, 1.0))

    main = lambda w: pl.BlockSpec((None, tq, w), lambda r, i: (r, i, 0))
    prev = lambda w: pl.BlockSpec((None, BLOCK, w), lambda r, i: (r, jnp.maximum(i * ns - 1, 0), 0))
    in_specs = [main(wq), main(wk), prev(wk), main(wk), prev(wk), pl.BlockSpec(bias.shape, lambda r, i: (0, 0))]
    args = [q, k, k, v, v, bias]
    if has_sink:
        in_specs = [pl.BlockSpec(memory_space=pltpu.SMEM)] + in_specs
        args = [sink] + args
    scratch = [pltpu.VMEM((tq + BLOCK, wk), BF16)] * (4 if gqa else 2)
    return pl.pallas_call(
        body, name=name, grid=(dil, nt), in_specs=in_specs,
        out_specs=[main(wq), main(LANES)],
        out_shape=[jax.ShapeDtypeStruct((dil, length, wq), BF16), jax.ShapeDtypeStruct((dil, length, LANES), F32)],
        scratch_shapes=scratch,
    )(*args)


def _band_bwd(q, k, v, do, lse, delta, *, max_dist, name):
    dil, length, wq = q.shape
    wk = k.shape[2]
    gqa = wk != wq
    tq = min(ATTN_TILE, length)
    ns, nt = tq // BLOCK, length // tq
    npair = wq // LANES
    nblocks = length // BLOCK
    bias = _band_bias(max_dist, transposed=True)

    def body(q_ref, qn_ref, do_ref, don_ref, lse_ref, lsen_ref, dl_ref, dln_ref, k_ref, v_ref, bias_ref,
             dq_ref, dk_ref, dv_ref, stat_l, stat_d, dqt, kt, *rolled):
        i = pl.program_id(1)
        for pk in range(wk // LANES):
            kt[pk] = k_ref[:, pk * LANES:(pk + 1) * LANES].astype(F32).T.astype(BF16)
        if gqa:
            kroll, vroll, ktroll = rolled
            kroll[...] = pltpu.roll(k_ref[...], HEAD_DIM, 1)
            vroll[...] = pltpu.roll(v_ref[...], HEAD_DIM, 1)
            ktroll[0] = kroll[...].astype(F32).T.astype(BF16)
        for a in range(ns):
            rows = slice(a * BLOCK, (a + 1) * BLOCK)
            stat_l[a] = _rows_to_lanes(lse_ref[rows, :])
            stat_d[a] = _rows_to_lanes(dl_ref[rows, :])
        stat_l[ns] = _rows_to_lanes(lsen_ref[...])
        stat_d[ns] = _rows_to_lanes(dln_ref[...])

        @pl.when(i == 0)
        def _():
            dqt[:, :, 0:BLOCK] = jnp.zeros((npair, LANES, BLOCK), F32)

        @pl.when(i > 0)
        def _():
            dqt[:, :, 0:BLOCK] = dqt[:, :, tq:tq + BLOCK]

        dqt[:, :, BLOCK:] = jnp.zeros((npair, LANES, tq), F32)
        half2 = _half_masks(2 * BLOCK)
        row = lax.broadcasted_iota(jnp.int32, (LANES, BLOCK), 0)
        row_half = (row < HEAD_DIM, row >= HEAD_DIM)
        col_next = (lax.broadcasted_iota(jnp.int32, (1, 2 * BLOCK), 1) >= BLOCK).astype(F32)

        def scores(b):
            rows = slice(b * BLOCK, (b + 1) * BLOCK)
            nxt_rows = slice((b + 1) * BLOCK, (b + 2) * BLOCK)
            items = []
            for p in range(npair):
                lanes = slice(p * LANES, (p + 1) * LANES)
                q_next = q_ref[nxt_rows, lanes] if b + 1 < ns else qn_ref[:, lanes]
                do_next = do_ref[nxt_rows, lanes] if b + 1 < ns else don_ref[:, lanes]
                qw = jnp.concatenate([q_ref[rows, lanes], q_next], axis=0)
                dow = jnp.concatenate([do_ref[rows, lanes], do_next], axis=0)
                for e in range(2):
                    h = 2 * p + e
                    pk, ek = _kv_place(h, gqa)
                    klanes = slice(pk * LANES, (pk + 1) * LANES)
                    kb = (k_ref if ek == e else kroll)[rows, klanes]
                    vb = (v_ref if ek == e else vroll)[rows, klanes]
                    qm = jnp.where(half2[e], qw, jnp.zeros_like(qw))
                    dom = jnp.where(half2[e], dow, jnp.zeros_like(dow))
                    items.append(dict(p=p, e=e, h=h, pk=pk, ek=ek, qm=qm, dom=dom,
                                      st=_dot_nt(kb, qm), dpt=_dot_nt(vb, dom)))
            return items

        def probs(b, items):
            bt = bias_ref[...]
            if b == ns - 1:
                bt = bt + jnp.where(i == nt - 1, NEG, 0.0) * col_next
            for it in items:
                h = it["h"]
                lrow = jnp.concatenate([stat_l[b, h:h + 1, :], stat_l[b + 1, h:h + 1, :]], axis=1)
                drow = jnp.concatenate([stat_d[b, h:h + 1, :], stat_d[b + 1, h:h + 1, :]], axis=1)
                pt = jnp.exp(it["st"] + bt - lrow)
                it["ptb"] = pt.astype(BF16)
                it["dsb"] = (pt * (it["dpt"] - drow)).astype(BF16)

        pending = scores(0)
        for b in range(ns):
            rows = slice(b * BLOCK, (b + 1) * BLOCK)
            window = slice(b * BLOCK, (b + 2) * BLOCK)
            acc = {}
            items = pending
            probs(b, items)
            pending = scores(b + 1) if b + 1 < ns else None
            for p in range(npair):
                pair = items[2 * p:2 * p + 2]
                lanes = slice(p * LANES, (p + 1) * LANES)
                kparts = []
                for it in pair:
                    kbt = (kt if it["ek"] == it["e"] else ktroll)[it["pk"], :, rows]
                    kparts.append(jnp.where(row_half[it["e"]], kbt, jnp.zeros_like(kbt)))
                ds_keys = jnp.concatenate([it["dsb"] for it in pair], axis=0)
                dqt[p, :, window] += _dot(jnp.concatenate(kparts, axis=1), ds_keys)
                if not gqa:
                    q_both = jnp.concatenate([it["qm"] for it in pair], axis=0)
                    do_both = jnp.concatenate([it["dom"] for it in pair], axis=0)
                    dk_ref[rows, lanes] = _dot(jnp.concatenate([it["dsb"] for it in pair], axis=1), q_both).astype(BF16)
                    dv_ref[rows, lanes] = _dot(jnp.concatenate([it["ptb"] for it in pair], axis=1), do_both).astype(BF16)
                else:
                    for it in pair:
                        dv_c = _dot(it["ptb"], it["dom"])
                        dk_c = _dot(it["dsb"], it["qm"])
                        key = (it["pk"], it["ek"] == it["e"])
                        if key in acc:
                            acc[key] = (acc[key][0] + dk_c, acc[key][1] + dv_c)
                        else:
                            acc[key] = (dk_c, dv_c)
            if gqa:
                dk_al, dv_al = acc[(0, True)]
                dk_mis, dv_mis = acc[(0, False)]
                dk_ref[rows, :] = (dk_al + pltpu.roll(dk_mis, HEAD_DIM, 1)).astype(BF16)
                dv_ref[rows, :] = (dv_al + pltpu.roll(dv_mis, HEAD_DIM, 1)).astype(BF16)

        for p in range(npair):
            dq_ref[:, p * LANES:(p + 1) * LANES] = dqt[p, :, 0:tq].T.astype(BF16)

    main = lambda w: pl.BlockSpec((None, tq, w), lambda r, i: (r, i, 0))
    nxt = lambda w: pl.BlockSpec((None, BLOCK, w), lambda r, i: (r, jnp.minimum((i + 1) * ns, nblocks - 1), 0))
    scratch = [pltpu.VMEM((ns + 1, 8, LANES), F32), pltpu.VMEM((ns + 1, 8, LANES), F32),
               pltpu.VMEM((npair, LANES, tq + BLOCK), F32), pltpu.VMEM((wk // LANES, LANES, tq), BF16)]
    if gqa:
        scratch = scratch + [pltpu.VMEM((tq, wk), BF16)] * 2 + [pltpu.VMEM((1, LANES, tq), BF16)]
    return pl.pallas_call(
        body, name=name, grid=(dil, nt),
        in_specs=[main(wq), nxt(wq), main(wq), nxt(wq), main(LANES), nxt(LANES), main(LANES), nxt(LANES),
                  main(wk), main(wk), pl.BlockSpec(bias.shape, lambda r, i: (0, 0))],
        out_specs=[main(wq), main(wk), main(wk)],
        out_shape=[jax.ShapeDtypeStruct((dil, length, wq), BF16), jax.ShapeDtypeStruct((dil, length, wk), BF16),
                   jax.ShapeDtypeStruct((dil, length, wk), BF16)],
        scratch_shapes=scratch,
        compiler_params=pltpu.CompilerParams(dimension_semantics=("arbitrary", "arbitrary")),
    )(q, q, do, do, lse, lse, delta, delta, k, v, bias)


def _mem_attn_fwd(q, mk, mv):
    seq = q.shape[0]
    tq = min(ATTN_TILE, seq)
    sub_rows = min(4 * BLOCK, tq)
    ns = tq // sub_rows

    def body(q_ref, mk_ref, mv_ref, o_ref, lse_ref):
        half = _half_masks(sub_rows)

        def sub(a, carry):
            r0 = pl.multiple_of(a * sub_rows, sub_rows)
            scores = []
            for p in range(C_W // LANES):
                lanes = slice(p * LANES, (p + 1) * LANES)
                qp = q_ref[pl.ds(r0, sub_rows), lanes]
                for e in range(2):
                    scores.append(_dot_nt(jnp.where(half[e], qp, jnp.zeros_like(qp)), mk_ref[:, lanes]))
            m_cols, l_cols, probs = [], [], []
            for s in scores:
                m = jnp.max(s, axis=1, keepdims=True)
                pe = jnp.exp(s - m)
                probs.append(pe.astype(BF16))
                m_cols.append(m)
                l_cols.append(jnp.sum(pe, axis=1, keepdims=True))
            for p in range(C_W // LANES):
                lanes = slice(p * LANES, (p + 1) * LANES)
                o_h = [_dot(probs[2 * p + e], mv_ref[:, lanes]) * (1.0 / l_cols[2 * p + e]) for e in range(2)]
                o_ref[pl.ds(r0, sub_rows), lanes] = jnp.where(half[0], o_h[0], o_h[1]).astype(BF16)
            lse_ref[pl.ds(r0, sub_rows), :] = _per_head(m_cols) + jnp.log(_per_head(l_cols, 1.0))
            return carry

        lax.fori_loop(0, ns, sub, 0, unroll=True)

    row = lambda w: pl.BlockSpec((tq, w), lambda i: (i, 0))
    full = pl.BlockSpec((N_MEM, C_W), lambda i: (0, 0))
    return pl.pallas_call(
        body, name="mem_attn_fwd", grid=(seq // tq,), in_specs=[row(C_W), full, full],
        out_specs=[row(C_W), row(LANES)],
        out_shape=[jax.ShapeDtypeStruct((seq, C_W), BF16), jax.ShapeDtypeStruct((seq, LANES), F32)],
    )(q, mk, mv)


def _mem_attn_bwd(q, mk, mv, do, lse, delta):
    seq = q.shape[0]
    tq = min(ATTN_TILE, seq)
    ns = tq // BLOCK
    npair = C_W // LANES

    def body(q_ref, mk_ref, mv_ref, do_ref, lse_ref, dl_ref, dq_ref, dmk_ref, dmv_ref, stat_l, stat_d, mkt, dqt):
        @pl.when(pl.program_id(0) == 0)
        def _():
            dmk_ref[...] = jnp.zeros_like(dmk_ref)
            dmv_ref[...] = jnp.zeros_like(dmv_ref)
            for p in range(npair):
                mkt[p] = mk_ref[:, p * LANES:(p + 1) * LANES].astype(F32).T.astype(BF16)

        for a in range(ns):
            rows = slice(a * BLOCK, (a + 1) * BLOCK)
            stat_l[a] = _rows_to_lanes(lse_ref[rows, :])
            stat_d[a] = _rows_to_lanes(dl_ref[rows, :])
        span = min(2, ns)
        half = _half_masks(span * BLOCK)
        row = lax.broadcasted_iota(jnp.int32, (LANES, N_MEM), 0)
        row_half = (row < HEAD_DIM, row >= HEAD_DIM)

        for a in range(0, ns, span):
            rows = slice(a * BLOCK, (a + span) * BLOCK)
            items = []
            for p in range(npair):
                lanes = slice(p * LANES, (p + 1) * LANES)
                qp = q_ref[rows, lanes]
                dop = do_ref[rows, lanes]
                for e in range(2):
                    qm = jnp.where(half[e], qp, jnp.zeros_like(qp))
                    dom = jnp.where(half[e], dop, jnp.zeros_like(dop))
                    items.append(dict(p=p, e=e, qm=qm, dom=dom, st=_dot_nt(mk_ref[:, lanes], qm),
                                      dpt=_dot_nt(mv_ref[:, lanes], dom)))
            for it in items:
                h = 2 * it["p"] + it["e"]
                lrow = jnp.concatenate([stat_l[a + k, h:h + 1, :] for k in range(span)], axis=1)
                drow = jnp.concatenate([stat_d[a + k, h:h + 1, :] for k in range(span)], axis=1)
                pt = jnp.exp(it["st"] - lrow)
                it["ptb"] = pt.astype(BF16)
                it["dsb"] = (pt * (it["dpt"] - drow)).astype(BF16)
            for p in range(npair):
                lanes = slice(p * LANES, (p + 1) * LANES)
                pair = [it for it in items if it["p"] == p]
                join = lambda name, axis: jnp.concatenate([it[name] for it in pair], axis=axis)
                dmv_ref[:, lanes] += _dot(join("ptb", 1), join("dom", 0))
                dmk_ref[:, lanes] += _dot(join("dsb", 1), join("qm", 0))
                kbt = mkt[p]
                k_both = jnp.concatenate([jnp.where(row_half[e], kbt, jnp.zeros_like(kbt)) for e in range(2)], axis=1)
                dqt[p, :, rows] = _dot(k_both, join("dsb", 0))
        for p in range(npair):
            dq_ref[:, p * LANES:(p + 1) * LANES] = dqt[p].T.astype(BF16)

    row = lambda w: pl.BlockSpec((tq, w), lambda i: (i, 0))
    full = pl.BlockSpec((N_MEM, C_W), lambda i: (0, 0))
    return pl.pallas_call(
        body, name="mem_attn_bwd", grid=(seq // tq,),
        in_specs=[row(C_W), full, full, row(C_W), row(LANES), row(LANES)], out_specs=[row(C_W), full, full],
        out_shape=[jax.ShapeDtypeStruct((seq, C_W), BF16), jax.ShapeDtypeStruct((N_MEM, C_W), F32),
                   jax.ShapeDtypeStruct((N_MEM, C_W), F32)],
        scratch_shapes=[pltpu.VMEM((ns, 8, LANES), F32)] * 2
        + [pltpu.VMEM((npair, LANES, N_MEM), BF16), pltpu.VMEM((npair, LANES, tq), F32)],
        compiler_params=pltpu.CompilerParams(dimension_semantics=("arbitrary",)),
    )(q, mk, mv, do, lse, delta)


def _silu_and_grad(g):
    s = 1.0 / (1.0 + jnp.exp(-g))
    return g * s, s * (1.0 + g * (1.0 - s))


def _post(x, target, post_norm, w_out, sink_row, oa, lse_a, ga, ob_list, lseb_list, gb, oc, gc):
    seq = x.shape[0]
    tm = min(ROW_TILE, seq)
    inv_d = 1.0 / D_MODEL
    nd = len(B_DILS)

    def body(*refs):
        (x_ref, t_ref, gp_ref, w_ref, sink_ref, oa_ref, lsea_ref, ga_ref), refs = refs[:8], refs[8:]
        ob_refs, lb_refs, (gb_ref, oc_ref, gc_ref), refs = refs[:nd], refs[nd:2 * nd], refs[2 * nd:2 * nd + 3], refs[2 * nd + 3:]
        (g_ref, doa_ref, dla_ref, dga_ref), refs = refs[:4], refs[4:]
        dob_refs, lsec_refs, dlb_refs, refs = refs[:nd], refs[nd:2 * nd], refs[2 * nd:3 * nd], refs[3 * nd:]
        (dgb_ref, doc_ref, dlc_ref, dgc_ref, gw_ref, gpost_ref, gsink_ref, loss_ref), refs = refs[:8], refs[8:]
        ycat, obufs, lbufs, st_do, st_l, st_d = refs[0], refs[1:nd], refs[nd:2 * nd - 1], refs[2 * nd - 1], refs[2 * nd], refs[2 * nd + 1]

        @pl.when(pl.program_id(0) == 0)
        def _():
            gw_ref[...] = jnp.zeros_like(gw_ref)
            gpost_ref[...] = jnp.zeros_like(gpost_ref)
            gsink_ref[...] = jnp.zeros_like(gsink_ref)
            loss_ref[...] = jnp.zeros_like(loss_ref)

        o_i, l_i = [ob_refs[0][0].astype(F32)], [lb_refs[0][0]]
        for k in range(1, nd):
            _from_residues(ob_refs[k], obufs[k - 1], B_DILS[k])
            _from_residues(lb_refs[k], lbufs[k - 1], B_DILS[k])
            o_i.append(_stage_read(obufs[k - 1]))
            l_i.append(_stage_read(lbufs[k - 1]))
        mx = l_i[0]
        for l in l_i[1:]:
            mx = jnp.maximum(mx, l)
        w_i = [jnp.exp(l - mx) for l in l_i]
        z = w_i[0]
        for w in w_i[1:]:
            z = z + w
        _stage_write(st_l, mx + jnp.log(z))
        expand = _head_expand_matrix(B_W)
        inv_z = 1.0 / z
        ob = None
        for w, o in zip(w_i, o_i):
            term = _dot_split(w * inv_z, expand, 2) * o
            ob = term if ob is None else ob + term
        oa, oc = oa_ref[...].astype(F32), oc_ref[...].astype(F32)
        sa, dsa = _silu_and_grad(ga_ref[...].astype(F32))
        sb, dsb = _silu_and_grad(gb_ref[...].astype(F32))
        sc, dsc = _silu_and_grad(gc_ref[...].astype(F32))
        ycat[:, 0:A_W] = (oa * sa).astype(BF16)
        ycat[:, A_W:A_W + B_W] = (ob * sb).astype(BF16)
        ycat[:, A_W + B_W:] = (oc * sc).astype(BF16)
        y2 = _dot(ycat[...], w_ref[...])
        r = lax.rsqrt(jnp.mean(y2 * y2, axis=-1, keepdims=True) + RMS_EPS)
        zhat = y2 * r
        gp = gp_ref[...]
        err = x_ref[...] + zhat * gp - t_ref[...]
        loss_ref[...] += jnp.sum(err * err) * (0.5 * inv_d)
        g = err * inv_d
        g_ref[...] = g
        gpost_ref[...] += jnp.sum(g * zhat, axis=0, keepdims=True)
        a = g * gp
        dy2 = (r * (a - zhat * jnp.mean(a * zhat, axis=-1, keepdims=True))).astype(BF16)
        dycat = _dot_nt(dy2, w_ref[...])
        gw_ref[...] += _dot_tn(ycat[...], dy2)
        dya, dyb, dyc = dycat[:, 0:A_W], dycat[:, A_W:A_W + B_W], dycat[:, A_W + B_W:]
        doa, dob, doc = dya * sa, dyb * sb, dyc * sc
        doa_ref[...] = doa.astype(BF16)
        doc_ref[...] = doc.astype(BF16)
        dga_ref[...] = (dya * oa * dsa).astype(BF16)
        dgb_ref[...] = (dyb * ob * dsb).astype(BF16)
        dgc_ref[...] = (dyc * oc * dsc).astype(BF16)
        dl_a = _dot_split(doa * oa, _head_sum_matrix(A_W), 2)
        dla_ref[...] = dl_a
        dlc_ref[...] = _dot_split(doc * oc, _head_sum_matrix(C_W), 2)
        gsink_ref[...] += jnp.sum(jnp.exp(sink_ref[...] - lsea_ref[...]) * dl_a, axis=0, keepdims=True)
        _stage_write(st_do, dob)
        _stage_write(st_d, _dot_split(dob * ob, _head_sum_matrix(B_W), 2))
        for k, dil in enumerate(B_DILS):
            _to_residues(st_do, dob_refs[k], dil)
            _to_residues(st_l, lsec_refs[k], dil)
            _to_residues(st_d, dlb_refs[k], dil)

    row = lambda w: pl.BlockSpec((tm, w), lambda i: (i, 0))
    full = lambda shape: pl.BlockSpec(shape, lambda i: (0,) * len(shape))
    res_specs = lambda w: [_residue_spec(d, tm, w) for d in B_DILS]
    res_shapes = lambda w, dt: [jax.ShapeDtypeStruct((d, seq // d, w), dt) for d in B_DILS]
    ins = [x, target, post_norm, w_out, sink_row, oa, lse_a, ga, *ob_list, *lseb_list, gb, oc, gc]
    in_specs = ([row(D_MODEL), row(D_MODEL), full((1, D_MODEL)), full((D_MODEL, D_MODEL)), full((1, LANES)),
                 row(A_W), row(LANES), row(A_W)] + res_specs(B_W) + res_specs(LANES) + [row(B_W), row(C_W), row(C_W)])
    out_shape = ([jax.ShapeDtypeStruct((seq, D_MODEL), F32), jax.ShapeDtypeStruct((seq, A_W), BF16),
                  jax.ShapeDtypeStruct((seq, LANES), F32), jax.ShapeDtypeStruct((seq, A_W), BF16)]
                 + res_shapes(B_W, BF16) + res_shapes(LANES, F32) + res_shapes(LANES, F32)
                 + [jax.ShapeDtypeStruct((seq, B_W), BF16), jax.ShapeDtypeStruct((seq, C_W), BF16),
                    jax.ShapeDtypeStruct((seq, LANES), F32), jax.ShapeDtypeStruct((seq, C_W), BF16),
                    jax.ShapeDtypeStruct((D_MODEL, D_MODEL), F32), jax.ShapeDtypeStruct((1, D_MODEL), F32),
                    jax.ShapeDtypeStruct((1, LANES), F32), jax.ShapeDtypeStruct((1, LANES), F32)])
    out_specs = ([row(D_MODEL), row(A_W), row(LANES), row(A_W)] + res_specs(B_W) + res_specs(LANES) + res_specs(LANES)
                 + [row(B_W), row(C_W), row(LANES), row(C_W),
                    full((D_MODEL, D_MODEL)), full((1, D_MODEL)), full((1, LANES)), full((1, LANES))])
    scratch = ([pltpu.VMEM((tm, D_MODEL), BF16)] + [_stage(tm, B_W)] * (nd - 1) + [_stage(tm, LANES)] * (nd - 1)
               + [_stage(tm, B_W), _stage(tm, LANES), _stage(tm, LANES)])
    res = pl.pallas_call(
        body, name="post", grid=(seq // tm,), in_specs=in_specs, out_specs=out_specs, out_shape=out_shape,
        scratch_shapes=scratch,
        compiler_params=pltpu.CompilerParams(dimension_semantics=("arbitrary",)),
    )(*ins)
    out = dict(g=res[0], doa=res[1], dl_a=res[2], dga=res[3], dob=res[4:4 + nd], lse_b=res[4 + nd:4 + 2 * nd],
               dl_b=res[4 + 2 * nd:4 + 3 * nd])
    rest = res[4 + 3 * nd:]
    out.update(dgb=rest[0], doc=rest[1], dl_c=rest[2], dgc=rest[3], gw_out=rest[4], gpost=rest[5], gsink=rest[6],
               loss=rest[7])
    return out


def _grad_w_in(ut, nat, res):
    tm = ut.shape[2]
    seq = ut.shape[0] * tm
    nd = len(B_DILS)
    nat_list = [nat[n] for n in _NATURAL]
    res_list = [a for n in _DILATED for a in res[n]]
    rope = _rope_tables(seq, tm)

    def body(rl_ref, rb_ref, ut_ref, *refs):
        nat_refs = dict(zip(_NATURAL, refs[:len(_NATURAL)]))
        refs = refs[len(_NATURAL):]
        res_refs = {n: refs[nd * k:nd * (k + 1)] for k, n in enumerate(_DILATED)}
        refs = refs[nd * len(_DILATED):]
        dproj_ref, gw_ref = refs[:2]
        bufs = {n: refs[2 + (nd - 1) * k:2 + (nd - 1) * (k + 1)] for k, n in enumerate(_DILATED)}

        @pl.when(pl.program_id(0) == 0)
        def _():
            gw_ref[...] = jnp.zeros_like(gw_ref)

        c, sm, sp = _rope_coeffs(rl_ref, rb_ref)
        sm, sp = -sm, -sp
        per_shard = SHARD_IN // LANES
        interleaved = False
        for j in (3, 0, 1, 2):
            for blk in range(per_shard * j, per_shard * (j + 1)):
                name, off, roped, scaled = _PROJ_LAYOUT[blk]
                lanes = slice(off, off + LANES)
                if name in nat_refs:
                    piece = nat_refs[name][:, lanes].astype(F32)
                else:
                    if not interleaved:
                        for n in _DILATED:
                            for k in range(1, nd):
                                _from_residues(res_refs[n][k], bufs[n][k - 1], B_DILS[k])
                        interleaved = True
                    piece = res_refs[name][0][0, :, lanes].astype(F32)
                    for buf in bufs[name]:
                        piece = piece + buf[off // LANES]
                if roped:
                    piece = _rope(piece, c, sm, sp)
                if scaled:
                    piece = piece * SCALE
                dproj_ref[:, blk * LANES:(blk + 1) * LANES] = piece.astype(BF16)
            gw_ref[j] += _dot(ut_ref[...], dproj_ref[:, j * SHARD_IN:(j + 1) * SHARD_IN])

    row = lambda w: pl.BlockSpec((tm, w), lambda i: (i, 0))
    in_specs = ([pl.BlockSpec(rope[0].shape, lambda i: (0, 0)), pl.BlockSpec((8, 2 * LANES), lambda i: (i, 0)),
                 pl.BlockSpec((None, D_MODEL, tm), lambda i: (i, 0, 0))]
                + [row(a.shape[1]) for a in nat_list]
                + [_residue_spec(d, tm, B_W) for _ in _DILATED for d in B_DILS])
    return pl.pallas_call(
        body, name="grad_w_in", grid=(seq // tm,), in_specs=in_specs,
        out_specs=[row(D_IN), pl.BlockSpec((N_CHIPS, D_MODEL, SHARD_IN), lambda i: (0, 0, 0))],
        out_shape=[jax.ShapeDtypeStruct((seq, D_IN), BF16), jax.ShapeDtypeStruct((N_CHIPS, D_MODEL, SHARD_IN), F32)],
        scratch_shapes=[_stage(tm, B_W)] * ((nd - 1) * len(_DILATED)),
        compiler_params=pltpu.CompilerParams(dimension_semantics=("arbitrary",)),
    )(*rope, ut, *nat_list, *res_list)


def _input_grad(x, g, pre_norm, w_in_g, dproj, gx_prev, span, after, name):
    seq = x.shape[0]
    tm = seq // INPUT_GRAD_TILES
    first_block, steps = span

    def body(*refs):
        x_ref, g_ref, gp_ref, w_ref, dp_ref = refs[:5]
        gx_ref, gpre_ref = refs[-2:]

        @pl.when(pl.program_id(0) == 0)
        def _():
            gpre_ref[...] = jnp.zeros_like(gpre_ref)

        du = None
        for j in range(N_CHIPS):
            term = _dot_nt(dp_ref[:, j * SHARD_IN:(j + 1) * SHARD_IN], w_ref[j])
            du = term if du is None else du + term
        xv = x_ref[...]
        r = lax.rsqrt(jnp.mean(xv * xv, axis=-1, keepdims=True) + RMS_EPS)
        xhat = xv * r
        gpre_ref[...] += jnp.sum(du * xhat, axis=0, keepdims=True)
        a = du * gp_ref[...]
        gx_ref[...] = g_ref[...] + r * (a - xhat * jnp.mean(a * xhat, axis=-1, keepdims=True))

    row = lambda w: pl.BlockSpec((tm, w), lambda i: (first_block + i, 0))
    full = lambda a: pl.BlockSpec(a.shape, lambda i: (0,) * a.ndim)
    any_spec = pl.BlockSpec(memory_space=pl.ANY)
    ins = [x, g, pre_norm, w_in_g, dproj]
    in_specs = [row(D_MODEL), row(D_MODEL), full(pre_norm), full(w_in_g), row(D_IN)]
    aliases = {}
    if gx_prev is not None:
        aliases[len(ins)] = 0
        ins.append(gx_prev)
        in_specs.append(any_spec)
    if after is not None:
        ins.append(after)
        in_specs.append(any_spec)
    return pl.pallas_call(
        body, name=name, grid=(steps,), in_specs=in_specs,
        out_specs=[row(D_MODEL), pl.BlockSpec((1, D_MODEL), lambda i: (0, 0))],
        out_shape=[jax.ShapeDtypeStruct((seq, D_MODEL), F32), jax.ShapeDtypeStruct((1, D_MODEL), F32)],
        input_output_aliases=aliases,
        compiler_params=pltpu.CompilerParams(dimension_semantics=("arbitrary",)),
    )(*ins)


def _exchange_start(ex, name):
    n_in, n_out, n_sem = len(ex["ins"]), len(ex["outs"]), len(ex["sems"])

    def body(*refs):
        in_refs, land_refs, sems = refs[:n_in], refs[n_in:n_in + n_out], refs[n_in + n_out:n_in + n_out + n_sem]
        ex["start"](in_refs, land_refs, *sems)
        token = refs[-1]
        token[...] = jnp.zeros_like(token)

    hbm = pl.BlockSpec(memory_space=pltpu.HBM)
    sem = pl.BlockSpec(memory_space=pltpu.SEMAPHORE)
    ins = [pltpu.with_memory_space_constraint(a, pltpu.HBM) for a in ex["ins"]]
    landing = [pltpu.with_memory_space_constraint(lax.empty(o.shape, o.dtype), pltpu.HBM) for o in ex["outs"]]
    res = pl.pallas_call(
        body, name=name,
        out_shape=list(ex["sems"]) + [pltpu.HBM(a.shape, a.dtype) for a in ex["ins"]]
        + [pltpu.HBM(o.shape, o.dtype) for o in ex["outs"]] + [jax.ShapeDtypeStruct((8, LANES), F32)],
        in_specs=[hbm] * (n_in + n_out),
        out_specs=[sem] * n_sem + [hbm] * (n_in + n_out) + [pl.BlockSpec(memory_space=pltpu.VMEM)],
        input_output_aliases={k: n_sem + k for k in range(n_in + n_out)},
        compiler_params=pltpu.CompilerParams(has_side_effects=pltpu.SideEffectType.DATAFLOW_SIDE_EFFECTING),
    )(*ins, *landing)
    return res[:-1], res[-1]


def _exchange_wait(ex, handles, after, name):
    n_in, n_out, n_sem = len(ex["ins"]), len(ex["outs"]), len(ex["sems"])
    sems, thru = handles[:n_sem], handles[n_sem:]

    def body(*refs):
        in_refs, land_refs = refs[:n_in], refs[n_in:n_in + n_out]
        sem_refs = refs[n_in + n_out:n_in + n_out + n_sem]
        ex["finish"](in_refs, land_refs, *sem_refs)

    hbm = pl.BlockSpec(memory_space=pltpu.HBM)
    sem = pl.BlockSpec(memory_space=pltpu.SEMAPHORE)
    res = pl.pallas_call(
        body, name=name,
        out_shape=[pltpu.HBM(a.shape, a.dtype) for a in thru],
        in_specs=[hbm] * (n_in + n_out) + [sem] * n_sem + [pl.BlockSpec(memory_space=pl.ANY)],
        out_specs=[hbm] * (n_in + n_out),
        input_output_aliases={k: k for k in range(n_in + n_out)},
        compiler_params=pltpu.CompilerParams(has_side_effects=pltpu.SideEffectType.DATAFLOW_SIDE_EFFECTING),
    )(*thru, *sems, after)
    return res[:n_in], res[n_in:]


def _start_finish(build):
    def start(*refs):
        for cp in build(*refs):
            cp.start()

    def finish(*refs):
        for cp in build(*refs):
            cp.wait()

    return dict(start=start, finish=finish)


def _pair_exchange(grads):
    n = len(grads)

    def build(srcs, outs, send_sems, recv_sems):
        x, y, c = lax.axis_index("x"), lax.axis_index("y"), lax.axis_index("c")
        copies = []
        for t in range(n):
            rows = grads[t].shape[1] // 2
            copies.append(pltpu.make_async_remote_copy(
                src_ref=srcs[t].at[:, pl.ds((1 - c) * rows, rows)], dst_ref=outs[t],
                send_sem=send_sems.at[t], recv_sem=recv_sems.at[t], device_id=(x, y, 1 - c), device_id_type=MESH))
        return copies

    return dict(ins=list(grads), **_start_finish(build),
                outs=[jax.ShapeDtypeStruct((g.shape[0], g.shape[1] // 2, g.shape[2]), g.dtype) for g in grads],
                sems=[pltpu.SemaphoreType.DMA((n,)), pltpu.SemaphoreType.DMA((n,))])


def _pair_add(core, owns, gots):
    n = len(owns)

    def body(core_ref, *refs):
        for t in range(n):
            refs[2 * n + t][...] = (refs[t][...] + refs[n + t][...]).astype(BF16)

    halves = [(None,) + g.shape[1:] for g in gots]
    grid_spec = pltpu.PrefetchScalarGridSpec(
        num_scalar_prefetch=1, grid=(N_CHIPS,),
        in_specs=[pl.BlockSpec(h, lambda k, core_ref: (k, core_ref[0], 0)) for h in halves]
        + [pl.BlockSpec(h, lambda k, core_ref: (k, 0, 0)) for h in halves],
        out_specs=[pl.BlockSpec(h, lambda k, core_ref: (k, 0, 0)) for h in halves])
    return pl.pallas_call(
        body, name="pair_add", grid_spec=grid_spec,
        out_shape=[jax.ShapeDtypeStruct(g.shape, BF16) for g in gots],
    )(core, *owns, *gots)


def _chip_exchange(parts):
    n = len(parts)

    def build(srcs, outs, send_sems, recv_sems, local_sems):
        x, y, c = lax.axis_index("x"), lax.axis_index("y"), lax.axis_index("c")
        my_chip = 2 * x + y
        chips = [(1 - x, y), (x, 1 - y), (1 - x, 1 - y)]
        copies = [pltpu.make_async_copy(srcs[t].at[my_chip], outs[t].at[my_chip], local_sems.at[t]) for t in range(n)]
        for j, (cx, cy) in enumerate(chips):
            for t in range(n):
                k = n * j + t
                copies.append(pltpu.make_async_remote_copy(
                    src_ref=srcs[t].at[2 * cx + cy], dst_ref=outs[t].at[my_chip], send_sem=send_sems.at[k],
                    recv_sem=recv_sems.at[k], device_id=(cx, cy, c), device_id_type=MESH))
        return copies

    return dict(ins=list(parts), **_start_finish(build), outs=[jax.ShapeDtypeStruct(p.shape, p.dtype) for p in parts],
                sems=[pltpu.SemaphoreType.DMA((3 * n,)), pltpu.SemaphoreType.DMA((3 * n,)),
                      pltpu.SemaphoreType.DMA((n,))])


def _chip_sum(core, slots):
    n = len(slots)

    def body(core_ref, *refs):
        for t in range(n):
            acc = refs[t][0].astype(F32)
            for s in range(1, N_CHIPS):
                acc = acc + refs[t][s].astype(F32)
            refs[n + t][...] = acc

    blocks = [(s.shape[1] // TAIL_STEPS, s.shape[2]) for s in slots]
    grid_spec = pltpu.PrefetchScalarGridSpec(
        num_scalar_prefetch=1, grid=(TAIL_STEPS,),
        in_specs=[pl.BlockSpec((N_CHIPS,) + b, lambda i, core_ref: (0, i, 0)) for b in blocks],
        out_specs=[pl.BlockSpec((None,) + b, lambda i, core_ref: (core_ref[0], i, 0)) for b in blocks])
    return pl.pallas_call(
        body, name="chip_sum", grid_spec=grid_spec,
        out_shape=[jax.ShapeDtypeStruct((2,) + s.shape[1:], F32) for s in slots],
    )(core, *slots)


def _pair_gather(bufs, small):
    n = len(bufs)

    def body(*refs):
        small_ref, outs, small_out = refs[n], refs[n + 1:2 * n + 1], refs[2 * n + 1]
        send_sems, recv_sems, local_sem = refs[2 * n + 2:]
        x, y, c = lax.axis_index("x"), lax.axis_index("y"), lax.axis_index("c")
        me = 4 * x + 2 * y + c
        chips = [(1 - x, y), (x, 1 - y), (1 - x, 1 - y)]
        mine = pltpu.make_async_copy(small_ref, small_out.at[me], local_sem)
        mine.start()
        copies = [pltpu.make_async_remote_copy(
            src_ref=outs[t].at[c], dst_ref=outs[t].at[c], send_sem=send_sems.at[t], recv_sem=recv_sems.at[t],
            device_id=(x, y, 1 - c), device_id_type=MESH) for t in range(n)]
        peers = [(x, y, 1 - c)] + [(cx, cy, cc) for (cx, cy) in chips for cc in (c, 1 - c)]
        for j, peer in enumerate(peers):
            copies.append(pltpu.make_async_remote_copy(
                src_ref=small_ref, dst_ref=small_out.at[me], send_sem=send_sems.at[n + j],
                recv_sem=recv_sems.at[n + j], device_id=peer, device_id_type=MESH))
        for cp in copies:
            cp.start()
        for cp in copies:
            cp.wait()
        mine.wait()

    any_spec = pl.BlockSpec(memory_space=pl.ANY)
    res = pl.pallas_call(
        body, name="pair_gather",
        out_shape=[jax.ShapeDtypeStruct(b.shape, b.dtype) for b in bufs]
        + [jax.ShapeDtypeStruct((8,) + small.shape, small.dtype)],
        in_specs=[any_spec] * (n + 1), out_specs=[any_spec] * (n + 1),
        input_output_aliases={t: t for t in range(n)},
        scratch_shapes=[pltpu.SemaphoreType.DMA((n + 7,)), pltpu.SemaphoreType.DMA((n + 7,)),
                        pltpu.SemaphoreType.DMA],
    )(*bufs, small)
    return [r.reshape(2 * b.shape[1], b.shape[2]) for r, b in zip(res[:n], bufs)], res[n]


def _adamw(ws, gs, ms, vs):
    n = len(ws)

    def body(*refs):
        for t in range(n):
            w_ref, g_ref, m_ref, v_ref = refs[t:4 * n:n]
            gout_ref, d_ref, nm_ref, nv_ref = refs[4 * n + t::n]
            g = g_ref[...]
            gout_ref[...] = g
            d_ref[...], nm_ref[...], nv_ref[...] = _adamw_math(w_ref[...], g, m_ref[...], v_ref[...])

    specs = [pl.BlockSpec((w.shape[0] // TAIL_STEPS, w.shape[1]), lambda i: (i, 0)) for w in ws]
    res = pl.pallas_call(
        body, name="adamw", grid=(TAIL_STEPS,), in_specs=specs * 4, out_specs=specs * 4,
        out_shape=[jax.ShapeDtypeStruct(w.shape, F32) for w in ws] * 4,
    )(*ws, *gs, *ms, *vs)
    return [res[t::n] for t in range(n)]


def _adamw_math(w, g, m, v):
    c1 = 1.0 / (1.0 - ADAM_B1 ** ADAM_STEP)
    c2 = 1.0 / (1.0 - ADAM_B2 ** ADAM_STEP)
    nm = ADAM_B1 * m + (1.0 - ADAM_B1) * g
    nv = ADAM_B2 * v + (1.0 - ADAM_B2) * (g * g)
    return -ADAM_LR * ((nm * c1) / (jnp.sqrt(nv * c2) + ADAM_EPS) + ADAM_WD * w), nm, nv


def _small_update(slots, params, ms, vs):
    n = len(params)

    def body(slots_ref, *refs):
        w_refs, m_refs, v_refs, loss_ref = refs[:n], refs[n:2 * n], refs[2 * n:3 * n], refs[3 * n]
        g_refs, d_refs, nm_refs, nv_refs = (refs[3 * n + 1 + k * n:3 * n + 1 + (k + 1) * n] for k in range(4))
        acc = slots_ref[0]
        for s in range(1, slots.shape[0]):
            acc = acc + slots_ref[s]
        loss_ref[...] = acc[4:5, 0:1]
        grads = (acc[0:1] + acc[5:6], acc[3:4], acc[2:3], acc[1:2])
        for k in range(n):
            g = grads[k][:, :w_refs[k].shape[1]]
            g_refs[k][...] = g
            d_refs[k][...], nm_refs[k][...], nv_refs[k][...] = _adamw_math(w_refs[k][...], g, m_refs[k][...],
                                                                           v_refs[k][...])

    return pl.pallas_call(
        body, name="small_update",
        out_shape=[jax.ShapeDtypeStruct((1, 1), F32)] + [jax.ShapeDtypeStruct(p.shape, F32) for p in params] * 4,
    )(slots, *params, *ms, *vs)


def _local_step(x, mem, target, pre_norm, sink_a, mem_norm, post_norm, w_in_g, w_out, w_mkv, gathers=None,
                own=None):
    first_gather, late_gather, late_shards = gathers if gathers else (None, None, ())
    u, ut, p_own, hosted = _pre_norm(x, pre_norm, own[1] if own else None, first_gather)
    if gathers:
        w_in_g = hosted[0].reshape(N_CHIPS, D_MODEL, SHARD_IN)
    pr = _pre_proj(u, w_in_g, (own[0], p_own) if own else None, late_gather, late_shards)
    pr["ut"] = ut
    if gathers:
        w_out, w_mkv = (g.reshape(D_MODEL, g.shape[-1]) for g in pr["hosted"])
    mk, mv = _mem_kv(mem, mem_norm, w_mkv)
    sink = sink_a.reshape(-1)
    qa, ka, va = pr["qa"][None], pr["ka"][None], pr["va"][None]
    oa, lse_a = _band_fwd(qa, ka, va, sink, max_dist=A_WINDOW - 1, name="swa_fwd")
    ob_list, lseb_list = [], []
    for k, (win, dil) in enumerate(B_CONFIGS):
        o_i, l_i = _band_fwd(pr["qb"][k], pr["kb"][k], pr["vb"][k], None, max_dist=win // dil, name=f"dil{dil}_fwd")
        ob_list.append(o_i)
        lseb_list.append(l_i)
    oc, lse_c = _mem_attn_fwd(pr["qc"], mk, mv)
    sink_row = jnp.pad(sink, (0, LANES - sink.shape[0])).reshape(1, LANES)
    po = _post(x, target, post_norm, w_out, sink_row, oa[0], lse_a[0], pr["ga"], ob_list, lseb_list, pr["gb"], oc,
               pr["gc"])
    dqc, dmk, dmv = _mem_attn_bwd(pr["qc"], mk, mv, po["doc"], lse_c, po["dl_c"])
    dqa, dka, dva = _band_bwd(qa, ka, va, po["doa"][None], lse_a, po["dl_a"][None], max_dist=A_WINDOW - 1,
                              name="swa_bwd")
    res = dict(qb=[], kb=[], vb=[])
    for k, (win, dil) in enumerate(B_CONFIGS):
        dq_i, dk_i, dv_i = _band_bwd(pr["qb"][k], pr["kb"][k], pr["vb"][k], po["dob"][k], po["lse_b"][k],
                                     po["dl_b"][k], max_dist=win // dil, name=f"dil{dil}_bwd")
        res["qb"].append(dq_i)
        res["kb"].append(dk_i)
        res["vb"].append(dv_i)
    nat = dict(qa=dqa[0], ka=dka[0], va=dva[0], ga=po["dga"], gb=po["dgb"], qc=dqc, gc=po["dgc"])
    dproj, gw_in = _grad_w_in(pr["ut"], nat, res)
    gw_mkv, gmem = _mem_kv_bwd(mem, mem_norm, w_mkv, dmk, dmv)
    gsink = -po["gsink"][0, :sink.shape[0]]
    return dict(loss=po["loss"], g=po["g"], dproj=dproj, gw_in=gw_in, gw_out=po["gw_out"], gw_mkv=gw_mkv,
                gpost=po["gpost"], gmem=gmem, gsink=gsink, w_in_g=w_in_g)


def kernel(x, mem, pre_norm, w_in, sink_a, mem_norm, w_mem_kv, w_out, post_norm, loss_target, m_pre_norm, m_w_in, m_sink_a, m_mem_norm, m_w_mem_kv, m_w_out, m_post_norm, v_pre_norm, v_w_in, v_sink_a, v_mem_norm, v_w_mem_kv, v_w_out, v_post_norm):
    w_own = w_in[0]
    late_shards = (w_out[0], w_mem_kv[0])
    gathers = (_gather_exchange([w_own]), _gather_exchange(late_shards), late_shards)
    chip = (2 * lax.axis_index("x") + lax.axis_index("y")).astype(jnp.int32).reshape(1)
    loc = _local_step(x[0], mem[0], loss_target[0], pre_norm, sink_a, mem_norm, post_norm, None, None, None, gathers,
                      (chip, w_own))
    big = [loc["gw_in"], loc["gw_out"].reshape(N_CHIPS, D_MODEL // N_CHIPS, D_MODEL),
           loc["gw_mkv"].reshape(N_CHIPS, D_MODEL // N_CHIPS, 2 * C_W)]
    core = lax.axis_index("c").astype(jnp.int32).reshape(1)
    w_in_full = loc["w_in_g"]
    step_in = (x[0], loc["g"], pre_norm, w_in_full, loc["dproj"])
    pair_ex = _pair_exchange(big)
    pair_handles, token = _exchange_start(pair_ex, "pair_exchange_start")
    gx_a, gpre_a = _input_grad(*step_in, None, (0, 2), token, "input_grad_a")
    big, got = _exchange_wait(pair_ex, pair_handles, gpre_a, "pair_exchange_wait")
    parts = _pair_add(core, big, got)
    chip_ex = _chip_exchange(parts)
    chip_handles, token = _exchange_start(chip_ex, "chip_exchange_start")
    grad_x, gpre_b = _input_grad(*step_in, gx_a, (2, 14), token, "input_grad_b")
    _, slots = _exchange_wait(chip_ex, chip_handles, gpre_b, "chip_exchange_wait")
    halves = _chip_sum(core, slots)
    widen = lambda a: jnp.pad(a.reshape(1, -1), ((0, 0), (0, D_MODEL - a.size)))
    small = jnp.concatenate([gpre_a, loc["gpost"], loc["gmem"], widen(loc["gsink"]), widen(loc["loss"]), gpre_b,
                             jnp.zeros((2, D_MODEL), F32)], axis=0)
    (g_in, g_out, g_mkv), small_slots = _pair_gather(halves, small)
    (loss, g_pre, g_sink, g_mem, g_post, d_pre, d_sink, d_mem, d_post, nm_pre, nm_sink, nm_mem, nm_post,
     nv_pre, nv_sink, nv_mem, nv_post) = _small_update(
        small_slots, (pre_norm, sink_a, mem_norm, post_norm), (m_pre_norm, m_sink_a, m_mem_norm, m_post_norm),
        (v_pre_norm, v_sink_a, v_mem_norm, v_post_norm))

    (g_in, d_in, nm_in, nv_in), (g_out, d_out, nm_out, nv_out), (g_mkv, d_mkv, nm_mkv, nv_mkv) = _adamw(
        (w_in[0], w_out[0], w_mem_kv[0]), (g_in, g_out, g_mkv), (m_w_in[0], m_w_out[0], m_w_mem_kv[0]),
        (v_w_in[0], v_w_out[0], v_w_mem_kv[0]))
    lead = lambda a: a[None]
    return (loss.reshape(()), lead(grad_x),
            g_pre, lead(g_in), g_sink, g_mem, lead(g_mkv), lead(g_out), g_post,
            d_pre, lead(d_in), d_sink, d_mem, lead(d_mkv), lead(d_out), d_post,
            nm_pre, lead(nm_in), nm_sink, nm_mem, lead(nm_mkv), lead(nm_out), nm_post,
            nv_pre, lead(nv_in), nv_sink, nv_mem, lead(nv_mkv), lead(nv_out), nv_post)
```
